```python
import jax, jax.numpy as jnp
from jax import lax
import numpy as np

D_MODEL = 1024
BATCH = 16
SEQ = 2048
DEPTH = 1

LRU_WIDTH = D_MODEL
LRU_BLOCKS = 8
LRU_BLOCK_DIM = LRU_WIDTH // LRU_BLOCKS
CONV_WIDTH = 4
LRU_C = 8.0
HG_WIDTH = D_MODEL
HG_EXPAND = 128
HG_HEADS = HG_WIDTH // HG_EXPAND
HG_HEAD_V = HG_WIDTH // HG_HEADS
CHUNK = 64
HG_SCALE = HG_EXPAND ** -0.5
N_BRANCH = 2
EPS = 1e-6
IN_COLS = 2 * LRU_WIDTH + 4 * HG_WIDTH + N_BRANCH * D_MODEL
SPLITS = [LRU_WIDTH, 2 * LRU_WIDTH, 2 * LRU_WIDTH + HG_WIDTH, 2 * LRU_WIDTH + 2 * HG_WIDTH,
          2 * LRU_WIDTH + 3 * HG_WIDTH, 2 * LRU_WIDTH + 4 * HG_WIDTH]

kernel_name = "hybrid_hawk_hgrn2_gated_block"


def rms_norm(x, g):
    xf = x.astype(jnp.float32)
    y = xf * lax.rsqrt(jnp.mean(xf * xf, axis=-1, keepdims=True) + EPS)
    return y.astype(x.dtype) * g


def causal_depthwise_conv(x, w, b):
    s = x.shape[1]
    xp = jnp.pad(x, ((0, 0), (CONV_WIDTH - 1, 0), (0, 0)))
    return b + sum(xp[:, k:k + s] * w[k] for k in range(CONV_WIDTH))


def block_diag_linear(x, w, b):
    bsz, s, _ = x.shape
    xb = x.reshape(bsz, s, LRU_BLOCKS, LRU_BLOCK_DIM)
    y = jnp.einsum('bshi,hij->bshj', xb, w) + b
    return y.reshape(bsz, s, LRU_WIDTH)


def _linear_recurrence_combine(left, right):
    a_l, u_l = left
    a_r, u_r = right
    return a_l * a_r, a_r * u_l + u_r


def rg_lru(x, wx, bx, wa, ba, lam):
    xf = x.astype(jnp.float32)
    gate_i = jax.nn.sigmoid(block_diag_linear(xf, wx, bx))
    gate_r = jax.nn.sigmoid(block_diag_linear(xf, wa, ba))
    log_a = -LRU_C * gate_r * jax.nn.softplus(-lam.astype(jnp.float32))
    a = jnp.exp(log_a)
    mult = jnp.sqrt(-jnp.expm1(2.0 * log_a))
    u = mult * gate_i * xf
    _, h = lax.associative_scan(_linear_recurrence_combine, (a, u), axis=1)
    return h.astype(x.dtype)


def hgrn2_chunked(q, k, v, log_f):
    bsz, s, h, dk = q.shape
    dv = v.shape[-1]
    n = s // CHUNK

    def to_chunks(t):
        return t.reshape(bsz, n, CHUNK, h, t.shape[-1]).transpose(1, 0, 3, 2, 4)

    q, k, v, log_f = map(to_chunks, (q, k, v, log_f))
    b = jnp.cumsum(log_f, axis=3)
    b_mid = b[:, :, :, CHUNK // 2:CHUNK // 2 + 1]
    b_last = b[:, :, :, -1:]
    q_in = q * jnp.exp(b - b_mid) * HG_SCALE
    k_in = k * jnp.exp(b_mid - b)
    causal = jnp.tril(jnp.ones((CHUNK, CHUNK), dtype=bool))
    att = jnp.where(causal, jnp.einsum('nbhtd,nbhsd->nbhts', q_in, k_in), 0.0)
    o_intra = jnp.einsum('nbhts,nbhsv->nbhtv', att, v)
    q_inter = q * jnp.exp(b) * HG_SCALE
    k_state = k * jnp.exp(b_last - b)
    chunk_decay = jnp.exp(b_last[:, :, :, 0])

    def step(state, inp):
        qc, kc, vc, dc = inp
        o = jnp.einsum('bhtd,bhdv->bhtv', qc, state)
        state = state * dc[..., None] + jnp.einsum('bhsd,bhsv->bhdv', kc, vc)
        return state, o

    s0 = jnp.zeros((bsz, h, dk, dv), dtype=q.dtype)
    _, o_inter = lax.scan(step, s0, (q_inter, k_state, v, chunk_decay))
    o = o_intra + o_inter
    return o.transpose(1, 0, 3, 2, 4).reshape(bsz, s, h, dv)


def _fwd_setup_inputs(seed: int = 0) -> dict:
    key = jax.random.key(seed)
    ks = jax.random.split(key, 20)
    f32 = jnp.float32
    nrm = lambda k, shape, scale: jax.random.normal(k, shape, f32) * scale
    x = jax.random.normal(ks[0], (BATCH, SEQ, D_MODEL), f32)
    w_in = nrm(ks[1], (DEPTH, D_MODEL, IN_COLS), D_MODEL ** -0.5)
    b_merge = nrm(ks[2], (DEPTH, N_BRANCH * D_MODEL), 0.01)
    conv_w = nrm(ks[3], (DEPTH, CONV_WIDTH, LRU_WIDTH), CONV_WIDTH ** -0.5)
    conv_b = nrm(ks[4], (DEPTH, LRU_WIDTH), 0.01)
    rg_wx = nrm(ks[5], (DEPTH, LRU_BLOCKS, LRU_BLOCK_DIM, LRU_BLOCK_DIM), LRU_BLOCK_DIM ** -0.5)
    rg_bx = nrm(ks[6], (DEPTH, LRU_BLOCKS, LRU_BLOCK_DIM), 0.01)
    rg_wa = nrm(ks[7], (DEPTH, LRU_BLOCKS, LRU_BLOCK_DIM, LRU_BLOCK_DIM), LRU_BLOCK_DIM ** -0.5)
    rg_ba = nrm(ks[8], (DEPTH, LRU_BLOCKS, LRU_BLOCK_DIM), 0.01)
    u = jax.random.uniform(ks[9], (DEPTH, LRU_WIDTH), f32, minval=0.9, maxval=0.999)
    a0 = u ** (1.0 / LRU_C)
    rg_lambda = jnp.log(a0) - jnp.log1p(-a0)
    hg_lb_logits = nrm(ks[10], (DEPTH + 1, HG_WIDTH), 0.1)
    hg_norm_g = 1.0 + nrm(ks[11], (DEPTH, HG_HEAD_V), 0.05)
    proj_a = nrm(ks[12], (DEPTH, LRU_WIDTH, D_MODEL), LRU_WIDTH ** -0.5)
    proj_b = nrm(ks[13], (DEPTH, HG_WIDTH, D_MODEL), HG_WIDTH ** -0.5)
    w_out = nrm(ks[14], (DEPTH, D_MODEL, D_MODEL), D_MODEL ** -0.5)
    norm_g = 1.0 + nrm(ks[15], (DEPTH, D_MODEL), 0.05)
    final_norm_g = 1.0 + nrm(ks[16], (D_MODEL,), 0.05)
    return {"x": x, "w_in": w_in, "b_merge": b_merge, "conv_w": conv_w, "conv_b": conv_b,
            "rg_wx": rg_wx, "rg_bx": rg_bx, "rg_wa": rg_wa, "rg_ba": rg_ba, "rg_lambda": rg_lambda,
            "hg_lb_logits": hg_lb_logits, "hg_norm_g": hg_norm_g, "proj_a": proj_a, "proj_b": proj_b,
            "w_out": w_out, "norm_g": norm_g, "final_norm_g": final_norm_g}


def _fwd_reference(x, w_in, b_merge, conv_w, conv_b, rg_wx, rg_bx, rg_wa, rg_ba, rg_lambda,
              hg_lb_logits, hg_norm_g, proj_a, proj_b, w_out, norm_g, final_norm_g):
    bsz, s, _ = x.shape
    lb_all = jnp.cumsum(jax.nn.softmax(hg_lb_logits.astype(jnp.float32), axis=0), axis=0)
    for l in range(DEPTH):
        h = rms_norm(x, norm_g[l])
        z = h @ w_in[l]
        xa, ga, q, f_pre, i_in, gb, gm = jnp.split(z, SPLITS, axis=-1)

        xa = causal_depthwise_conv(xa, conv_w[l], conv_b[l])
        ya = rg_lru(xa, rg_wx[l], rg_bx[l], rg_wa[l], rg_ba[l], rg_lambda[l])
        ya = ya * jax.nn.silu(ga)
        out_a = ya @ proj_a[l]

        lb = lb_all[l]
        f = lb + (1.0 - lb) * jax.nn.sigmoid(f_pre.astype(jnp.float32))
        log_f = jnp.log(f)
        k = 1.0 - f
        qh = jax.nn.silu(q.astype(jnp.float32))
        heads = lambda t: t.reshape(bsz, s, HG_HEADS, t.shape[-1] // HG_HEADS)
        o = hgrn2_chunked(heads(qh), heads(k), heads(i_in.astype(jnp.float32)), heads(log_f))
        o = rms_norm(o, hg_norm_g[l]).reshape(bsz, s, HG_WIDTH).astype(x.dtype)
        yb = o * jax.nn.silu(gb)
        out_b = yb @ proj_b[l]

        gates = jax.nn.sigmoid(gm + b_merge[l])
        g_a, g_b = jnp.split(gates, N_BRANCH, axis=-1)
        mixed = g_a * out_a + g_b * out_b
        x = x + mixed @ w_out[l]
    return rms_norm(x, final_norm_g)


import jax as _jax
import jax.numpy as _jnp

TWIN_FORMAT = 'train_step'
FWD_PARAMS = ['x', 'w_in', 'b_merge', 'conv_w', 'conv_b', 'rg_wx', 'rg_bx', 'rg_wa', 'rg_ba', 'rg_lambda', 'hg_lb_logits', 'hg_norm_g', 'proj_a', 'proj_b', 'w_out', 'norm_g', 'final_norm_g']
TWIN_WEIGHTS = ['w_in', 'b_merge', 'conv_w', 'conv_b', 'rg_wx', 'rg_bx', 'rg_wa', 'rg_ba', 'rg_lambda', 'hg_lb_logits', 'hg_norm_g', 'proj_a', 'proj_b', 'w_out', 'norm_g', 'final_norm_g']
TWIN_DIFF_INPUT = 'x'
TWIN_INPUTS = ['x', 'w_in', 'b_merge', 'conv_w', 'conv_b', 'rg_wx', 'rg_bx', 'rg_wa', 'rg_ba', 'rg_lambda', 'hg_lb_logits', 'hg_norm_g', 'proj_a', 'proj_b', 'w_out', 'norm_g', 'final_norm_g', 'loss_target', 'm_w_in', 'm_b_merge', 'm_conv_w', 'm_conv_b', 'm_rg_wx', 'm_rg_bx', 'm_rg_wa', 'm_rg_ba', 'm_rg_lambda', 'm_hg_lb_logits', 'm_hg_norm_g', 'm_proj_a', 'm_proj_b', 'm_w_out', 'm_norm_g', 'm_final_norm_g', 'v_w_in', 'v_b_merge', 'v_conv_w', 'v_conv_b', 'v_rg_wx', 'v_rg_bx', 'v_rg_wa', 'v_rg_ba', 'v_rg_lambda', 'v_hg_lb_logits', 'v_hg_norm_g', 'v_proj_a', 'v_proj_b', 'v_w_out', 'v_norm_g', 'v_final_norm_g']
TWIN_OUTPUTS = ['loss', 'grad_x', 'grad_w_in', 'grad_b_merge', 'grad_conv_w', 'grad_conv_b', 'grad_rg_wx', 'grad_rg_bx', 'grad_rg_wa', 'grad_rg_ba', 'grad_rg_lambda', 'grad_hg_lb_logits', 'grad_hg_norm_g', 'grad_proj_a', 'grad_proj_b', 'grad_w_out', 'grad_norm_g', 'grad_final_norm_g', 'delta_w_in', 'delta_b_merge', 'delta_conv_w', 'delta_conv_b', 'delta_rg_wx', 'delta_rg_bx', 'delta_rg_wa', 'delta_rg_ba', 'delta_rg_lambda', 'delta_hg_lb_logits', 'delta_hg_norm_g', 'delta_proj_a', 'delta_proj_b', 'delta_w_out', 'delta_norm_g', 'delta_final_norm_g', 'new_m_w_in', 'new_m_b_merge', 'new_m_conv_w', 'new_m_conv_b', 'new_m_rg_wx', 'new_m_rg_bx', 'new_m_rg_wa', 'new_m_rg_ba', 'new_m_rg_lambda', 'new_m_hg_lb_logits', 'new_m_hg_norm_g', 'new_m_proj_a', 'new_m_proj_b', 'new_m_w_out', 'new_m_norm_g', 'new_m_final_norm_g', 'new_v_w_in', 'new_v_b_merge', 'new_v_conv_w', 'new_v_conv_b', 'new_v_rg_wx', 'new_v_rg_bx', 'new_v_rg_wa', 'new_v_rg_ba', 'new_v_rg_lambda', 'new_v_hg_lb_logits', 'new_v_hg_norm_g', 'new_v_proj_a', 'new_v_proj_b', 'new_v_w_out', 'new_v_norm_g', 'new_v_final_norm_g']
TWIN_LEAF_KINDS = {'loss': 'loss', 'grad_x': 'grad_x', 'grad_w_in': 'grad_w', 'grad_b_merge': 'grad_w', 'grad_conv_w': 'grad_w', 'grad_conv_b': 'grad_w', 'grad_rg_wx': 'grad_w', 'grad_rg_bx': 'grad_w', 'grad_rg_wa': 'grad_w', 'grad_rg_ba': 'grad_w', 'grad_rg_lambda': 'grad_w', 'grad_hg_lb_logits': 'grad_w', 'grad_hg_norm_g': 'grad_w', 'grad_proj_a': 'grad_w', 'grad_proj_b': 'grad_w', 'grad_w_out': 'grad_w', 'grad_norm_g': 'grad_w', 'grad_final_norm_g': 'grad_w', 'delta_w_in': 'delta_w', 'delta_b_merge': 'delta_w', 'delta_conv_w': 'delta_w', 'delta_conv_b': 'delta_w', 'delta_rg_wx': 'delta_w', 'delta_rg_bx': 'delta_w', 'delta_rg_wa': 'delta_w', 'delta_rg_ba': 'delta_w', 'delta_rg_lambda': 'delta_w', 'delta_hg_lb_logits': 'delta_w', 'delta_hg_norm_g': 'delta_w', 'delta_proj_a': 'delta_w', 'delta_proj_b': 'delta_w', 'delta_w_out': 'delta_w', 'delta_norm_g': 'delta_w', 'delta_final_norm_g': 'delta_w', 'new_m_w_in': 'new_m', 'new_m_b_merge': 'new_m', 'new_m_conv_w': 'new_m', 'new_m_conv_b': 'new_m', 'new_m_rg_wx': 'new_m', 'new_m_rg_bx': 'new_m', 'new_m_rg_wa': 'new_m', 'new_m_rg_ba': 'new_m', 'new_m_rg_lambda': 'new_m', 'new_m_hg_lb_logits': 'new_m', 'new_m_hg_norm_g': 'new_m', 'new_m_proj_a': 'new_m', 'new_m_proj_b': 'new_m', 'new_m_w_out': 'new_m', 'new_m_norm_g': 'new_m', 'new_m_final_norm_g': 'new_m', 'new_v_w_in': 'new_v', 'new_v_b_merge': 'new_v', 'new_v_conv_w': 'new_v', 'new_v_conv_b': 'new_v', 'new_v_rg_wx': 'new_v', 'new_v_rg_bx': 'new_v', 'new_v_rg_wa': 'new_v', 'new_v_rg_ba': 'new_v', 'new_v_rg_lambda': 'new_v', 'new_v_hg_lb_logits': 'new_v', 'new_v_hg_norm_g': 'new_v', 'new_v_proj_a': 'new_v', 'new_v_proj_b': 'new_v', 'new_v_w_out': 'new_v', 'new_v_norm_g': 'new_v', 'new_v_final_norm_g': 'new_v'}


def _forward(args):
    return _fwd_reference(*[args[k] for k in FWD_PARAMS])


def _output_shape():
    out = _jax.eval_shape(lambda: _forward(_fwd_setup_inputs(0)))
    return out.shape, out.dtype

N_MICROBATCH = 1
ADAM_LR = 0.001
ADAM_B1 = 0.9
ADAM_B2 = 0.999
ADAM_EPS = 1e-08
ADAM_WD = 0.01
ADAM_STEP = 10
PER_EXAMPLE_BATCH_AXIS = {'x': 0, 'loss_target': 0}
SHARED_INPUTS = []
_WEIGHT_DTYPES = {'w_in': _jnp.float32, 'b_merge': _jnp.float32, 'conv_w': _jnp.float32, 'conv_b': _jnp.float32, 'rg_wx': _jnp.float32, 'rg_bx': _jnp.float32, 'rg_wa': _jnp.float32, 'rg_ba': _jnp.float32, 'rg_lambda': _jnp.float32, 'hg_lb_logits': _jnp.float32, 'hg_norm_g': _jnp.float32, 'proj_a': _jnp.float32, 'proj_b': _jnp.float32, 'w_out': _jnp.float32, 'norm_g': _jnp.float32, 'final_norm_g': _jnp.float32}
MOMENT_SCALE = {'w_in': 3.374337e-02, 'b_merge': 1.919594e-02, 'conv_w': 3.729166e-02, 'conv_b': 3.977044e-01, 'rg_wx': 2.087048e-02, 'rg_bx': 1.283801e-02, 'rg_wa': 1.164579e-02, 'rg_ba': 9.467175e-03, 'rg_lambda': 1.795698e-02, 'hg_lb_logits': 5.093152e-03, 'hg_norm_g': 1.407552e-01, 'proj_a': 3.501109e-02, 'proj_b': 5.426529e-02, 'w_out': 6.444633e-02, 'norm_g': 9.114430e-02, 'final_norm_g': 3.206226e+01}


def _to_microbatches(a, axis):
    t = _jnp.moveaxis(a, axis, 0)
    t = t.reshape((N_MICROBATCH, t.shape[0] // N_MICROBATCH) + t.shape[1:])
    return _jnp.moveaxis(t, 1, axis + 1)


def setup_inputs(seed: int = 0) -> dict:
    inp = _fwd_setup_inputs(seed)
    key = _jax.random.fold_in(_jax.random.key(seed), 7919)
    shape, _ = _output_shape()
    out = dict(inp)
    out["loss_target"] = _jax.random.normal(_jax.random.fold_in(key, 0), shape, _jnp.float32)
    for i, name in enumerate(TWIN_WEIGHTS):
        w = inp[name].astype(_jnp.float32)
        if MOMENT_SCALE is None:
            s = _jnp.sqrt(_jnp.mean(_jnp.square(w)) + 1e-30)
        else:
            s = MOMENT_SCALE[name]
        km, kv = _jax.random.split(_jax.random.fold_in(key, i + 1))
        out[name] = w
        out["m_" + name] = s * _jax.random.normal(km, w.shape, _jnp.float32)
        out["v_" + name] = (s * s) * _jax.random.uniform(kv, w.shape, _jnp.float32, 0.5, 1.5)
    if N_MICROBATCH > 1:
        for name, axis in PER_EXAMPLE_BATCH_AXIS.items():
            out[name] = _to_microbatches(out[name], axis)
    return {'x': out['x'], 'w_in': out['w_in'], 'b_merge': out['b_merge'], 'conv_w': out['conv_w'], 'conv_b': out['conv_b'], 'rg_wx': out['rg_wx'], 'rg_bx': out['rg_bx'], 'rg_wa': out['rg_wa'], 'rg_ba': out['rg_ba'], 'rg_lambda': out['rg_lambda'], 'hg_lb_logits': out['hg_lb_logits'], 'hg_norm_g': out['hg_norm_g'], 'proj_a': out['proj_a'], 'proj_b': out['proj_b'], 'w_out': out['w_out'], 'norm_g': out['norm_g'], 'final_norm_g': out['final_norm_g'], 'loss_target': out['loss_target'], 'm_w_in': out['m_w_in'], 'm_b_merge': out['m_b_merge'], 'm_conv_w': out['m_conv_w'], 'm_conv_b': out['m_conv_b'], 'm_rg_wx': out['m_rg_wx'], 'm_rg_bx': out['m_rg_bx'], 'm_rg_wa': out['m_rg_wa'], 'm_rg_ba': out['m_rg_ba'], 'm_rg_lambda': out['m_rg_lambda'], 'm_hg_lb_logits': out['m_hg_lb_logits'], 'm_hg_norm_g': out['m_hg_norm_g'], 'm_proj_a': out['m_proj_a'], 'm_proj_b': out['m_proj_b'], 'm_w_out': out['m_w_out'], 'm_norm_g': out['m_norm_g'], 'm_final_norm_g': out['m_final_norm_g'], 'v_w_in': out['v_w_in'], 'v_b_merge': out['v_b_merge'], 'v_conv_w': out['v_conv_w'], 'v_conv_b': out['v_conv_b'], 'v_rg_wx': out['v_rg_wx'], 'v_rg_bx': out['v_rg_bx'], 'v_rg_wa': out['v_rg_wa'], 'v_rg_ba': out['v_rg_ba'], 'v_rg_lambda': out['v_rg_lambda'], 'v_hg_lb_logits': out['v_hg_lb_logits'], 'v_hg_norm_g': out['v_hg_norm_g'], 'v_proj_a': out['v_proj_a'], 'v_proj_b': out['v_proj_b'], 'v_w_out': out['v_w_out'], 'v_norm_g': out['v_norm_g'], 'v_final_norm_g': out['v_final_norm_g']}


def _loss(weights, diff, rest, loss_target):
    with _jax.named_scope("forward"):
        args = {**rest, TWIN_DIFF_INPUT: diff, **{k: w.astype(_WEIGHT_DTYPES[k]) for k, w in weights.items()}}
        y = _forward(args)
    with _jax.named_scope("loss_head"):
        err = _jnp.square(y.astype(_jnp.float32) - loss_target)
        return 0.5 * _jnp.sum(_jnp.mean(err, axis=-1)) if err.ndim else 0.5 * err


def _adamw(w, g, m, v):
    m = ADAM_B1 * m + (1.0 - ADAM_B1) * g
    v = ADAM_B2 * v + (1.0 - ADAM_B2) * _jnp.square(g)
    m_hat = m / (1.0 - ADAM_B1 ** ADAM_STEP)
    v_hat = v / (1.0 - ADAM_B2 ** ADAM_STEP)
    delta = -ADAM_LR * (m_hat / (_jnp.sqrt(v_hat) + ADAM_EPS) + ADAM_WD * w)
    return delta, m, v


def reference(x, w_in, b_merge, conv_w, conv_b, rg_wx, rg_bx, rg_wa, rg_ba, rg_lambda, hg_lb_logits, hg_norm_g, proj_a, proj_b, w_out, norm_g, final_norm_g, loss_target, m_w_in, m_b_merge, m_conv_w, m_conv_b, m_rg_wx, m_rg_bx, m_rg_wa, m_rg_ba, m_rg_lambda, m_hg_lb_logits, m_hg_norm_g, m_proj_a, m_proj_b, m_w_out, m_norm_g, m_final_norm_g, v_w_in, v_b_merge, v_conv_w, v_conv_b, v_rg_wx, v_rg_bx, v_rg_wa, v_rg_ba, v_rg_lambda, v_hg_lb_logits, v_hg_norm_g, v_proj_a, v_proj_b, v_w_out, v_norm_g, v_final_norm_g):
    given = dict(x=x, w_in=w_in, b_merge=b_merge, conv_w=conv_w, conv_b=conv_b, rg_wx=rg_wx, rg_bx=rg_bx, rg_wa=rg_wa, rg_ba=rg_ba, rg_lambda=rg_lambda, hg_lb_logits=hg_lb_logits, hg_norm_g=hg_norm_g, proj_a=proj_a, proj_b=proj_b, w_out=w_out, norm_g=norm_g, final_norm_g=final_norm_g, loss_target=loss_target, m_w_in=m_w_in, m_b_merge=m_b_merge, m_conv_w=m_conv_w, m_conv_b=m_conv_b, m_rg_wx=m_rg_wx, m_rg_bx=m_rg_bx, m_rg_wa=m_rg_wa, m_rg_ba=m_rg_ba, m_rg_lambda=m_rg_lambda, m_hg_lb_logits=m_hg_lb_logits, m_hg_norm_g=m_hg_norm_g, m_proj_a=m_proj_a, m_proj_b=m_proj_b, m_w_out=m_w_out, m_norm_g=m_norm_g, m_final_norm_g=m_final_norm_g, v_w_in=v_w_in, v_b_merge=v_b_merge, v_conv_w=v_conv_w, v_conv_b=v_conv_b, v_rg_wx=v_rg_wx, v_rg_bx=v_rg_bx, v_rg_wa=v_rg_wa, v_rg_ba=v_rg_ba, v_rg_lambda=v_rg_lambda, v_hg_lb_logits=v_hg_lb_logits, v_hg_norm_g=v_hg_norm_g, v_proj_a=v_proj_a, v_proj_b=v_proj_b, v_w_out=v_w_out, v_norm_g=v_norm_g, v_final_norm_g=v_final_norm_g)
    weights = {n: given[n] for n in TWIN_WEIGHTS}
    shared = {n: given[n] for n in SHARED_INPUTS}
    per_example = {n: given[n] for n in ['x']}
    grad_fn = _jax.value_and_grad(_loss, argnums=(0, 1))

    def one_microbatch(ex, loss_target):
        ex = dict(ex)
        diff = ex.pop(TWIN_DIFF_INPUT)
        return grad_fn(weights, diff, {**shared, **ex}, loss_target)

    if N_MICROBATCH == 1:
        loss, (grad_w, grad_x) = one_microbatch(per_example, given["loss_target"])
    else:
        def body(carry, xs):
            loss_sum, grad_sum = carry
            l_k, (gw_k, gx_k) = one_microbatch(xs[0], xs[1])
            with _jax.named_scope("update"):
                return (loss_sum + l_k, _jax.tree.map(_jnp.add, grad_sum, gw_k)), gx_k

        init = (_jnp.zeros((), _jnp.float32), _jax.tree.map(_jnp.zeros_like, weights))
        (loss, grad_w), grad_x = _jax.lax.scan(body, init, (per_example, given["loss_target"]))
    with _jax.named_scope("update"):
        delta_w, new_m, new_v = {}, {}, {}
        for n in TWIN_WEIGHTS:
            delta_w[n], new_m[n], new_v[n] = _adamw(weights[n], grad_w[n], given["m_" + n], given["v_" + n])
    return (loss, grad_x, *[grad_w[n] for n in TWIN_WEIGHTS], *[delta_w[n] for n in TWIN_WEIGHTS],
            *[new_m[n] for n in TWIN_WEIGHTS], *[new_v[n] for n in TWIN_WEIGHTS])
```

```python
import functools

import jax
import jax.numpy as jnp
from jax import lax
from jax.experimental import pallas as pl
from jax.experimental.pallas import tpu as pltpu

F32 = jnp.float32
_MXU_DTYPE = jnp.bfloat16

D_MODEL = 1024
LANES = 128
SUBLANES = 8
N_BLK = D_MODEL // LANES
N_GROUPS = 8
N_SHARDS = 4
CONV_WIDTH = 4
LRU_C = 8.0
CHUNK = 64
HG_SCALE = float(LANES) ** -0.5
EPS = 1e-6
ADAM_LR, ADAM_B1, ADAM_B2, ADAM_EPS, ADAM_WD, ADAM_STEP = 0.001, 0.9, 0.999, 1e-08, 0.01, 10
VMEM_LIMIT = 56 * 1024 * 1024
MESH = pl.DeviceIdType.MESH

_SLOT_TO_GROUP = (2, 3, 4, 5, 0, 1, 6, 7)


def _group_of_slot(s):
    return jnp.where(s < 4, s + 2, jnp.where(s < 6, s - 4, s))


def _mm(a, b):
    return lax.dot_general(a.astype(_MXU_DTYPE), b.astype(_MXU_DTYPE), (((1,), (0,)), ((), ())),
                           preferred_element_type=F32)


def _mm_nt(a, b):
    return lax.dot_general(a.astype(_MXU_DTYPE), b.astype(_MXU_DTYPE), (((1,), (1,)), ((), ())),
                           preferred_element_type=F32)


def _mm_tn(a, b):
    return lax.dot_general(a.astype(_MXU_DTYPE), b.astype(_MXU_DTYPE), (((0,), (0,)), ((), ())),
                           preferred_element_type=F32)


def _mm_exact(a, b):
    return lax.dot_general(a, b, (((1,), (0,)), ((), ())), precision=lax.Precision.HIGHEST,
                           preferred_element_type=F32)


def _sigmoid(x):
    return 1.0 / (1.0 + jnp.exp(-x))


def _log1p_pos(y):
    series = y * (1.0 - y * (0.5 - y * (1.0 / 3.0 - y * 0.25)))
    return jnp.where(y < 0.01, series, jnp.log(1.0 + y))


def _expm1_neg(y):
    series = y * (1.0 + y * 0.5 * (1.0 + y * (1.0 / 3.0) * (1.0 + y * 0.25 * (1.0 + y * 0.2))))
    return jnp.where(y > -0.02, series, jnp.exp(y) - 1.0)


def _softplus(x):
    return jnp.maximum(x, 0.0) + _log1p_pos(jnp.exp(-jnp.abs(x)))


def _shift_down(x, n):
    rows = lax.broadcasted_iota(jnp.int32, x.shape, 0)
    return jnp.where(rows >= n, pltpu.roll(x, n, 0), 0.0)


def _shift_up(x, n):
    size = x.shape[0]
    rows = lax.broadcasted_iota(jnp.int32, x.shape, 0)
    return jnp.where(rows < size - n, pltpu.roll(x, size - n, 0), 0.0)


def _params(dims, vmem=VMEM_LIMIT):
    return pltpu.CompilerParams(dimension_semantics=dims, vmem_limit_bytes=vmem)


def _inproj_fwd(x2d, norm_g, w_all):
    tokens, d = x2d.shape
    tm = min(512, tokens)

    def body(x_ref, g_ref, w_ref, z_ref, h_ref, h_scr):
        @pl.when(pl.program_id(1) == 0)
        def _():
            x = x_ref[...]
            r = lax.rsqrt(jnp.mean(x * x, axis=-1, keepdims=True) + EPS)
            h = ((x * r) * g_ref[...]).astype(_MXU_DTYPE)
            h_scr[...] = h
            h_ref[...] = h

        z_ref[...] = _mm(h_scr[...], w_ref[...])

    def w_index(i, s):
        g = _group_of_slot(s)
        return (g // 2, 0, g % 2)

    return pl.pallas_call(
        body, name="inproj_fwd",
        grid=(tokens // tm, N_GROUPS),
        in_specs=[pl.BlockSpec((tm, d), lambda i, s: (i, 0)),
                  pl.BlockSpec((1, d), lambda i, s: (0, 0)),
                  pl.BlockSpec((None, d, D_MODEL), w_index)],
        out_specs=[pl.BlockSpec((None, tm, D_MODEL), lambda i, s: (s, i, 0)),
                   pl.BlockSpec((tm, d), lambda i, s: (i, 0))],
        out_shape=[jax.ShapeDtypeStruct((N_GROUPS, tokens, D_MODEL), F32),
                   jax.ShapeDtypeStruct((tokens, d), _MXU_DTYPE)],
        scratch_shapes=[pltpu.VMEM((tm, d), _MXU_DTYPE)],
        compiler_params=_params(("parallel", "arbitrary")),
    )(x2d, norm_g, w_all)


def _lru_gates(xa, cw_ref, cb_ref, wx_ref, bx_ref, wa_ref, ba_ref, lam_ref):
    xc = (cb_ref[...] + cw_ref[3:4, :] * xa + cw_ref[2:3, :] * _shift_down(xa, 1)
          + cw_ref[1:2, :] * _shift_down(xa, 2) + cw_ref[0:1, :] * _shift_down(xa, 3))
    gi = _sigmoid(_mm(xc, wx_ref[...]) + bx_ref[...])
    gr = _sigmoid(_mm(xc, wa_ref[...]) + ba_ref[...])
    sp = _softplus(-lam_ref[...])
    log_a = (-LRU_C) * gr * sp
    a = jnp.exp(log_a)
    mult = jnp.sqrt(-_expm1_neg(2.0 * log_a))
    return xc, gi, gr, sp, a, mult


def _tile_rows():
    return lax.broadcasted_iota(jnp.int32, (SUBLANES, LANES), 0)


def _scan_forward(a_scr, u_scr, h_scr, seq):
    rows = _tile_rows()

    def tile(j, carry):
        sl = pl.ds(pl.multiple_of(j * SUBLANES, SUBLANES), SUBLANES)
        a = a_scr[sl, :]
        u = u_scr[sl, :]
        for d in (1, 2, 4):
            keep = rows >= d
            a_sh = jnp.where(keep, pltpu.roll(a, d, 0), 1.0)
            u_sh = jnp.where(keep, pltpu.roll(u, d, 0), 0.0)
            u = a * u_sh + u
            a = a * a_sh
        h = u + a * carry
        h_scr[sl, :] = h
        return jnp.broadcast_to(h[SUBLANES - 1:SUBLANES, :], (SUBLANES, LANES))

    lax.fori_loop(0, seq // SUBLANES, tile, jnp.zeros((SUBLANES, LANES), F32))


def _scan_backward(c_scr, d_scr, g_scr, seq):
    rows = _tile_rows()
    n_tiles = seq // SUBLANES

    def tile(jj, carry):
        j = n_tiles - 1 - jj
        sl = pl.ds(pl.multiple_of(j * SUBLANES, SUBLANES), SUBLANES)
        c = c_scr[sl, :]
        g = d_scr[sl, :]
        for d in (1, 2, 4):
            keep = rows < SUBLANES - d
            c_sh = jnp.where(keep, pltpu.roll(c, SUBLANES - d, 0), 1.0)
            g_sh = jnp.where(keep, pltpu.roll(g, SUBLANES - d, 0), 0.0)
            g = c * g_sh + g
            c = c * c_sh
        g = g + c * carry
        g_scr[sl, :] = g
        return jnp.broadcast_to(g[0:1, :], (SUBLANES, LANES))

    lax.fori_loop(0, n_tiles, tile, jnp.zeros((SUBLANES, LANES), F32))


def _lru_param_specs(cb_axis):
    def pick(*ids):
        return ids[cb_axis]

    vec = pl.BlockSpec((1, LANES), lambda *ids: (0, pick(*ids)))
    mat = pl.BlockSpec((None, LANES, LANES), lambda *ids: (pick(*ids), 0, 0))
    return [pl.BlockSpec((CONV_WIDTH, LANES), lambda *ids: (0, pick(*ids))), vec, mat, vec, mat, vec, vec]


def _branch_a_fwd(z, conv_w, conv_b, wx, bx, wa, ba, lam, batch, seq):
    tokens = batch * seq

    def body(z_ref, cw_ref, cb_ref, wx_ref, bx_ref, wa_ref, ba_ref, lam_ref, ya_ref, hl_ref, a_scr, u_scr):
        xa = z_ref[0]
        ga = z_ref[1]
        xc, gi, _, _, a, mult = _lru_gates(xa, cw_ref, cb_ref, wx_ref, bx_ref, wa_ref, ba_ref, lam_ref)
        a_scr[...] = a
        u_scr[...] = mult * gi * xc
        _scan_forward(a_scr, u_scr, hl_ref, seq)
        ya_ref[...] = (hl_ref[...] * (ga * _sigmoid(ga))).astype(_MXU_DTYPE)

    blk = pl.BlockSpec((seq, LANES), lambda b, c: (b, c))
    return pl.pallas_call(
        body, name="branch_a_fwd",
        grid=(batch, N_BLK),
        in_specs=[pl.BlockSpec((2, seq, LANES), lambda b, c: (2, b, c))] + _lru_param_specs(1),
        out_specs=[blk, blk],
        out_shape=[jax.ShapeDtypeStruct((tokens, D_MODEL), _MXU_DTYPE), jax.ShapeDtypeStruct((tokens, D_MODEL), F32)],
        scratch_shapes=[pltpu.VMEM((seq, LANES), F32), pltpu.VMEM((seq, LANES), F32)],
        compiler_params=_params(("parallel", "parallel")),
    )(z, conv_w, conv_b, wx, bx, wa, ba, lam)


def _branch_a_bwd(z, hl, dya, dz, conv_w, conv_b, wx, bx, wa, ba, lam, batch, seq):
    def body(z_ref, hl_ref, dya_ref, dz_in_ref, cw_ref, cb_ref, wx_ref, bx_ref, wa_ref, ba_ref, lam_ref,
             dz_ref, dcw_ref, dcb_ref, dwx_ref, dbx_ref, dwa_ref, dba_ref, dlam_ref, c_scr, d_scr, g_scr):
        del dz_in_ref
        xa = z_ref[0]
        ga = z_ref[1]
        hl = hl_ref[...]
        dya = dya_ref[...]
        xc, gi, gr, sp, a, mult = _lru_gates(xa, cw_ref, cb_ref, wx_ref, bx_ref, wa_ref, ba_ref, lam_ref)
        sga = _sigmoid(ga)
        dz_ref[1] = (dya * hl * (sga * (1.0 + ga * (1.0 - sga)))).astype(_MXU_DTYPE)
        c_scr[...] = _shift_up(a, 1)
        d_scr[...] = dya * (ga * sga)
        _scan_backward(c_scr, d_scr, g_scr, seq)
        g = g_scr[...]
        da = g * _shift_down(hl, 1)
        dmult = g * gi * xc
        dgi = g * mult * xc
        dxc = g * mult * gi
        dlog_a = da * a - dmult * (a * a) / mult
        dgr = dlog_a * (-LRU_C) * sp
        dsp = jnp.sum(dlog_a * gr, axis=0, keepdims=True) * (-LRU_C)
        dlam = -dsp * _sigmoid(-lam_ref[...])
        dpi = dgi * gi * (1.0 - gi)
        dpr = dgr * gr * (1.0 - gr)
        dxc = dxc + _mm_nt(dpi, wx_ref[...]) + _mm_nt(dpr, wa_ref[...])
        dwx = _mm_tn(xc, dpi)
        dwa = _mm_tn(xc, dpr)
        dbx = jnp.sum(dpi, axis=0, keepdims=True)
        dba = jnp.sum(dpr, axis=0, keepdims=True)
        dxa = (cw_ref[3:4, :] * dxc + cw_ref[2:3, :] * _shift_up(dxc, 1) + cw_ref[1:2, :] * _shift_up(dxc, 2)
               + cw_ref[0:1, :] * _shift_up(dxc, 3))
        dz_ref[0] = dxa.astype(_MXU_DTYPE)
        dcb = jnp.sum(dxc, axis=0, keepdims=True)
        dcw = [jnp.sum(dxc * _shift_down(xa, CONV_WIDTH - 1 - k), axis=0, keepdims=True) if k < CONV_WIDTH - 1
               else jnp.sum(dxc * xa, axis=0, keepdims=True) for k in range(CONV_WIDTH)]

        @pl.when(pl.program_id(1) == 0)
        def _():
            for k in range(CONV_WIDTH):
                dcw_ref[k:k + 1, :] = dcw[k]
            dcb_ref[...] = dcb
            dwx_ref[...] = dwx
            dbx_ref[...] = dbx
            dwa_ref[...] = dwa
            dba_ref[...] = dba
            dlam_ref[...] = dlam

        @pl.when(pl.program_id(1) != 0)
        def _():
            for k in range(CONV_WIDTH):
                dcw_ref[k:k + 1, :] += dcw[k]
            dcb_ref[...] += dcb
            dwx_ref[...] += dwx
            dbx_ref[...] += dbx
            dwa_ref[...] += dwa
            dba_ref[...] += dba
            dlam_ref[...] += dlam

    tokens = batch * seq
    blk = pl.BlockSpec((seq, LANES), lambda c, b: (b, c))
    vec = pl.BlockSpec((1, LANES), lambda c, b: (0, c))
    mat = pl.BlockSpec((None, LANES, LANES), lambda c, b: (c, 0, 0))
    vec_shape = jax.ShapeDtypeStruct((1, D_MODEL), F32)
    mat_shape = jax.ShapeDtypeStruct((N_BLK, LANES, LANES), F32)
    return pl.pallas_call(
        body, name="branch_a_bwd",
        grid=(N_BLK, batch),
        in_specs=[pl.BlockSpec((2, seq, LANES), lambda c, b: (2, b, c)), blk, blk,
                  pl.BlockSpec(memory_space=pl.ANY)] + _lru_param_specs(0),
        out_specs=[pl.BlockSpec((2, seq, LANES), lambda c, b: (2, b, c)),
                   pl.BlockSpec((CONV_WIDTH, LANES), lambda c, b: (0, c)), vec, mat, vec, mat, vec, vec],
        out_shape=[jax.ShapeDtypeStruct((N_GROUPS, tokens, D_MODEL), _MXU_DTYPE),
                   jax.ShapeDtypeStruct((CONV_WIDTH, D_MODEL), F32), vec_shape, mat_shape, vec_shape, mat_shape,
                   vec_shape, vec_shape],
        scratch_shapes=[pltpu.VMEM((seq, LANES), F32)] * 3,
        input_output_aliases={3: 0},
        compiler_params=_params(("parallel", "arbitrary")),
    )(z, hl, dya, dz, conv_w, conv_b, wx, bx, wa, ba, lam)


def _chunk_masks():
    r = lax.broadcasted_iota(jnp.int32, (CHUNK, CHUNK), 0)
    c = lax.broadcasted_iota(jnp.int32, (CHUNK, CHUNK), 1)
    return r >= c


def _hgrn_prepare(z_ref, lb_ref, f_scr, logf_scr, qh_scr):
    lb = _sigmoid(lb_ref[0:1, :] - lb_ref[1:2, :])
    q = z_ref[0]
    sg = _sigmoid(z_ref[1])
    f = lb + (1.0 - lb) * sg
    f_scr[...] = f
    logf_scr[...] = jnp.log(f)
    qh_scr[...] = q * _sigmoid(q)
    return lb


def _chunk_terms(c, z_ref, f_scr, logf_scr, qh_scr, b_scr, tril):
    rows = pl.ds(pl.multiple_of(c * CHUNK, CHUNK), CHUNK)
    b_scr[...] = _mm_exact(tril, logf_scr[rows, :])
    b = b_scr[...]
    b_mid = b_scr[CHUNK // 2:CHUNK // 2 + 1, :]
    b_last = b_scr[CHUNK - 1:CHUNK, :]
    qh = qh_scr[rows, :]
    k = 1.0 - f_scr[rows, :]
    v = z_ref[2, rows, :]
    e_q = jnp.exp(b - b_mid) * HG_SCALE
    e_k = jnp.exp(b_mid - b)
    e_qi = jnp.exp(b) * HG_SCALE
    e_ks = jnp.exp(b_last - b)
    decay = jnp.exp(b_last)
    return rows, qh, k, v, e_q, e_k, e_qi, e_ks, decay


def _branch_b_fwd(z, lb_logits, hg_g, batch, seq):
    tokens = batch * seq
    n_chunks = seq // CHUNK

    def body(z_ref, lb_ref, g_ref, yb_ref, st_ref, f_scr, logf_scr, qh_scr, b_scr):
        _hgrn_prepare(z_ref, lb_ref, f_scr, logf_scr, qh_scr)
        causal = _chunk_masks()
        tril = causal.astype(F32)
        gain = g_ref[...]

        def chunk(c, state_t):
            rows, qh, k, v, e_q, e_k, e_qi, e_ks, decay = _chunk_terms(c, z_ref, f_scr, logf_scr, qh_scr, b_scr, tril)
            st_ref[c] = state_t
            att = jnp.where(causal, _mm_nt(qh * e_q, k * e_k), 0.0)
            o = _mm(att, v) + _mm_nt(qh * e_qi, state_t)
            r = lax.rsqrt(jnp.mean(o * o, axis=-1, keepdims=True) + EPS)
            gb = z_ref[3, rows, :]
            yb_ref[rows, :] = (((o * r) * gain) * (gb * _sigmoid(gb))).astype(_MXU_DTYPE)
            return state_t * decay + _mm_tn(v, k * e_ks)

        lax.fori_loop(0, n_chunks, chunk, jnp.zeros((LANES, LANES), F32))

    seq_buf = pltpu.VMEM((seq, LANES), F32)
    return pl.pallas_call(
        body, name="branch_b_fwd",
        grid=(batch, N_BLK),
        in_specs=[pl.BlockSpec((4, seq, LANES), lambda b, h: (0, b, h)),
                  pl.BlockSpec((2, LANES), lambda b, h: (0, h)),
                  pl.BlockSpec((1, LANES), lambda b, h: (0, 0))],
        out_specs=[pl.BlockSpec((seq, LANES), lambda b, h: (b, h)),
                   pl.BlockSpec((None, n_chunks, LANES, LANES), lambda b, h: (b * N_BLK + h, 0, 0, 0))],
        out_shape=[jax.ShapeDtypeStruct((tokens, D_MODEL), _MXU_DTYPE),
                   jax.ShapeDtypeStruct((batch * N_BLK, n_chunks, LANES, LANES), F32)],
        scratch_shapes=[seq_buf, seq_buf, seq_buf, pltpu.VMEM((CHUNK, LANES), F32)],
        compiler_params=_params(("parallel", "parallel")),
    )(z, lb_logits, hg_g)


def _branch_b_bwd(z, states, dyb, dz, lb_logits, hg_g, batch, seq):
    n_chunks = seq // CHUNK

    def body(z_ref, st_ref, dyb_ref, dz_in_ref, lb_ref, g_ref, dz_ref, dlog_ref, dg_ref,
             f_scr, logf_scr, qh_scr, b_scr, dlb_scr, dst_scr):
        del dz_in_ref
        first = (pl.program_id(0) == 0) & (pl.program_id(1) == 0)
        lb = _hgrn_prepare(z_ref, lb_ref, f_scr, logf_scr, qh_scr)
        causal = _chunk_masks()
        tril = causal.astype(F32)
        triu = jnp.transpose(tril)
        gain = g_ref[...]

        @pl.when(first)
        def _():
            dg_ref[...] = jnp.zeros_like(dg_ref)

        @pl.when(pl.program_id(1) == 0)
        def _():
            dlb_scr[...] = jnp.zeros_like(dlb_scr)

        def chunk(cc, carry):
            c = n_chunks - 1 - cc
            d_state_t = dst_scr[...]
            rows, qh, k, v, e_q, e_k, e_qi, e_ks, decay = _chunk_terms(c, z_ref, f_scr, logf_scr, qh_scr, b_scr, tril)
            state_t = st_ref[c]
            q_in, k_in, q_int, k_st = qh * e_q, k * e_k, qh * e_qi, k * e_ks
            att = jnp.where(causal, _mm_nt(q_in, k_in), 0.0)
            o = _mm(att, v) + _mm_nt(q_int, state_t)
            r = lax.rsqrt(jnp.mean(o * o, axis=-1, keepdims=True) + EPS)
            o_n = o * r
            gb = z_ref[3, rows, :]
            sgb = _sigmoid(gb)
            dyb_c = dyb_ref[rows, :]
            d_ong = dyb_c * (gb * sgb)
            dz_ref[3, rows, :] = (dyb_c * (o_n * gain) * (sgb * (1.0 + gb * (1.0 - sgb)))).astype(_MXU_DTYPE)
            dg_ref[...] += jnp.sum(d_ong * o_n, axis=0, keepdims=True)
            d_on = d_ong * gain
            d_o = r * (d_on - o_n * jnp.mean(d_on * o_n, axis=-1, keepdims=True))
            d_att = jnp.where(causal, _mm_nt(d_o, v), 0.0)
            d_v = _mm_tn(att, d_o) + _mm_nt(k_st, d_state_t)
            dq_in = _mm(d_att, k_in)
            dk_in = _mm_tn(d_att, q_in)
            dq_int = _mm(d_o, state_t)
            dk_st = _mm(v, d_state_t)
            d_decay = jnp.sum(state_t * d_state_t, axis=0, keepdims=True)
            d_qh = dq_in * e_q + dq_int * e_qi
            d_k = dk_in * e_k + dk_st * e_ks
            kk = dk_st * k_st
            d_b = dq_in * q_in + dq_int * q_int - dk_in * k_in - kk
            d_b_last = jnp.sum(kk, axis=0, keepdims=True) + decay * d_decay
            d_logf = _mm_exact(triu, d_b) + d_b_last
            dz_ref[2, rows, :] = d_v.astype(_MXU_DTYPE)
            qh_scr[rows, :] = d_qh
            logf_scr[rows, :] = d_logf / f_scr[rows, :] - d_k
            dst_scr[...] = _mm_tn(d_o, q_int) + d_state_t * decay
            return carry

        dst_scr[...] = jnp.zeros_like(dst_scr)
        lax.fori_loop(0, n_chunks, chunk, 0)
        q = z_ref[0]
        sq = _sigmoid(q)
        dz_ref[0] = (qh_scr[...] * (sq * (1.0 + q * (1.0 - sq)))).astype(_MXU_DTYPE)
        sg = _sigmoid(z_ref[1])
        d_f = logf_scr[...]
        dz_ref[1] = (d_f * (1.0 - lb) * sg * (1.0 - sg)).astype(_MXU_DTYPE)
        dlb_scr[...] += jnp.sum(d_f * (1.0 - sg), axis=0, keepdims=True)
        d_l0 = dlb_scr[...] * lb * (1.0 - lb)
        dlog_ref[0:1, :] = d_l0
        dlog_ref[1:2, :] = -d_l0

    tokens = batch * seq
    seq_buf = pltpu.VMEM((seq, LANES), F32)
    return pl.pallas_call(
        body, name="branch_b_bwd",
        grid=(N_BLK, batch),
        in_specs=[pl.BlockSpec((4, seq, LANES), lambda h, b: (0, b, h)),
                  pl.BlockSpec((None, n_chunks, LANES, LANES), lambda h, b: (b * N_BLK + h, 0, 0, 0)),
                  pl.BlockSpec((seq, LANES), lambda h, b: (b, h)),
                  pl.BlockSpec(memory_space=pl.ANY),
                  pl.BlockSpec((2, LANES), lambda h, b: (0, h)),
                  pl.BlockSpec((1, LANES), lambda h, b: (0, 0))],
        out_specs=[pl.BlockSpec((4, seq, LANES), lambda h, b: (0, b, h)),
                   pl.BlockSpec((2, LANES), lambda h, b: (0, h)),
                   pl.BlockSpec((1, LANES), lambda h, b: (0, 0))],
        out_shape=[jax.ShapeDtypeStruct((N_GROUPS, tokens, D_MODEL), _MXU_DTYPE),
                   jax.ShapeDtypeStruct((2, D_MODEL), F32),
                   jax.ShapeDtypeStruct((1, LANES), F32)],
        scratch_shapes=[seq_buf, seq_buf, seq_buf, pltpu.VMEM((CHUNK, LANES), F32), pltpu.VMEM((1, LANES), F32),
                        pltpu.VMEM((LANES, LANES), F32)],
        input_output_aliases={3: 0},
        compiler_params=_params(("arbitrary", "arbitrary")),
    )(z, states, dyb, dz, lb_logits, hg_g)


def _merge_tail(ya, yb, z, x2d, tgt2d, b_merge, final_g, pa, pb, wo):
    tokens, d = x2d.shape
    tm = min(256, tokens)
    n_tiles = tokens // tm

    def body(ya_ref, yb_ref, z_ref, x_ref, t_ref, bm_ref, fg_ref, pa_hbm, pb_hbm, wo_hbm,
             dya_ref, dyb_ref, dx2_ref, dz_ref, loss_ref, dfg_ref, dbm_ref, dpa_hbm, dpb_hbm, dwo_hbm,
             pa_s, pb_s, wo_s, dpa_s, dpb_s, dwo_s):
        i = pl.program_id(0)

        @pl.when(i == 0)
        def _():
            pltpu.sync_copy(pa_hbm, pa_s)
            pltpu.sync_copy(pb_hbm, pb_s)
            pltpu.sync_copy(wo_hbm, wo_s)
            dpa_s[...] = jnp.zeros_like(dpa_s)
            dpb_s[...] = jnp.zeros_like(dpb_s)
            dwo_s[...] = jnp.zeros_like(dwo_s)
            loss_ref[...] = jnp.zeros_like(loss_ref)
            dfg_ref[...] = jnp.zeros_like(dfg_ref)
            dbm_ref[...] = jnp.zeros_like(dbm_ref)

        ya_t = ya_ref[...]
        yb_t = yb_ref[...]
        out_a = _mm(ya_t, pa_s[...])
        out_b = _mm(yb_t, pb_s[...])
        g_a = _sigmoid(z_ref[0] + bm_ref[:, :d])
        g_b = _sigmoid(z_ref[1] + bm_ref[:, d:])
        mixed = g_a * out_a + g_b * out_b
        x2 = x_ref[...] + _mm(mixed, wo_s[...])
        r = lax.rsqrt(jnp.mean(x2 * x2, axis=-1, keepdims=True) + EPS)
        xn = x2 * r
        fg = fg_ref[...]
        diff = xn * fg - t_ref[...]
        loss_ref[...] += jnp.sum(diff * diff) * (0.5 / d)
        dy = diff * (1.0 / d)
        dfg_ref[...] += jnp.sum(dy * xn, axis=0, keepdims=True)
        dxn = dy * fg
        dx2 = r * (dxn - xn * jnp.mean(dxn * xn, axis=-1, keepdims=True))
        dx2_ref[...] = dx2
        dmixed = _mm_nt(dx2, wo_s[...])
        dwo_s[...] += _mm_tn(mixed, dx2)
        dgm_a = dmixed * out_a * g_a * (1.0 - g_a)
        dgm_b = dmixed * out_b * g_b * (1.0 - g_b)
        dz_ref[0] = dgm_a.astype(_MXU_DTYPE)
        dz_ref[1] = dgm_b.astype(_MXU_DTYPE)
        dbm_ref[:, :d] += jnp.sum(dgm_a, axis=0, keepdims=True)
        dbm_ref[:, d:] += jnp.sum(dgm_b, axis=0, keepdims=True)
        dout_a = dmixed * g_a
        dout_b = dmixed * g_b
        dpa_s[...] += _mm_tn(ya_t, dout_a)
        dpb_s[...] += _mm_tn(yb_t, dout_b)
        dya_ref[...] = _mm_nt(dout_a, pa_s[...])
        dyb_ref[...] = _mm_nt(dout_b, pb_s[...])

        @pl.when(i == n_tiles - 1)
        def _():
            pltpu.sync_copy(dpa_s, dpa_hbm)
            pltpu.sync_copy(dpb_s, dpb_hbm)
            pltpu.sync_copy(dwo_s, dwo_hbm)

    tile = pl.BlockSpec((tm, d), lambda i: (i, 0))
    gm = pl.BlockSpec((2, tm, d), lambda i: (3, i, 0))
    row = lambda n: pl.BlockSpec((1, n), lambda i: (0, 0))
    hbm = pl.BlockSpec(memory_space=pl.ANY)
    act = jax.ShapeDtypeStruct((tokens, d), F32)
    mat = jax.ShapeDtypeStruct((d, d), F32)
    return pl.pallas_call(
        body, name="merge_tail",
        grid=(n_tiles,),
        in_specs=[tile, tile, gm, tile, tile, row(2 * d), row(d), hbm, hbm, hbm],
        out_specs=[tile, tile, tile, gm, row(LANES), row(d), row(2 * d), hbm, hbm, hbm],
        out_shape=[act, act, act, jax.ShapeDtypeStruct((N_GROUPS, tokens, d), _MXU_DTYPE),
                   jax.ShapeDtypeStruct((1, LANES), F32), jax.ShapeDtypeStruct((1, d), F32),
                   jax.ShapeDtypeStruct((1, 2 * d), F32), mat, mat, mat],
        scratch_shapes=[pltpu.VMEM((d, d), _MXU_DTYPE)] * 3 + [pltpu.VMEM((d, d), F32)] * 3,
        compiler_params=_params(("arbitrary",)),
    )(ya, yb, z, x2d, tgt2d, b_merge, final_g, pa, pb, wo)


def _inproj_dw(h, dz):
    tokens, d = h.shape
    tm = min(512, tokens)

    def body(h_ref, dz_ref, dw_ref):
        part = _mm_tn(h_ref[...], dz_ref[...])

        @pl.when(pl.program_id(1) == 0)
        def _():
            dw_ref[...] = part

        @pl.when(pl.program_id(1) != 0)
        def _():
            dw_ref[...] += part

    def out_index(s, i):
        g = _group_of_slot(s)
        return (g // 2, 0, g % 2)

    return pl.pallas_call(
        body, name="inproj_dw",
        grid=(N_GROUPS, tokens // tm),
        in_specs=[pl.BlockSpec((tm, d), lambda s, i: (i, 0)),
                  pl.BlockSpec((None, tm, D_MODEL), lambda s, i: (s, i, 0))],
        out_specs=pl.BlockSpec((None, d, D_MODEL), out_index),
        out_shape=jax.ShapeDtypeStruct((N_SHARDS, d, 2 * D_MODEL), F32),
        compiler_params=_params(("parallel", "arbitrary")),
    )(h, dz)


def _inproj_dx(dz, w_all, x2d, dx2, norm_g):
    tokens, d = x2d.shape
    tm = min(512, tokens)

    def body(dz_ref, w_ref, x_ref, dx2_ref, g_ref, dx_ref, dg_ref, acc):
        s = pl.program_id(1)
        part = _mm_nt(dz_ref[...], w_ref[...])

        @pl.when(s == 0)
        def _():
            acc[...] = part

        @pl.when(s != 0)
        def _():
            acc[...] += part

        @pl.when((pl.program_id(0) == 0) & (s == 0))
        def _():
            dg_ref[...] = jnp.zeros_like(dg_ref)

        @pl.when(s == N_GROUPS - 1)
        def _():
            x = x_ref[...]
            r = lax.rsqrt(jnp.mean(x * x, axis=-1, keepdims=True) + EPS)
            xn = x * r
            dh = acc[...]
            dg_ref[...] += jnp.sum(dh * xn, axis=0, keepdims=True)
            dxn = dh * g_ref[...]
            dx_ref[...] = r * (dxn - xn * jnp.mean(dxn * xn, axis=-1, keepdims=True)) + dx2_ref[...]

    def w_index(i, s):
        g = _group_of_slot(s)
        return (g // 2, 0, g % 2)

    tile = pl.BlockSpec((tm, d), lambda i, s: (i, 0))
    return pl.pallas_call(
        body, name="inproj_dx",
        grid=(tokens // tm, N_GROUPS),
        in_specs=[pl.BlockSpec((None, tm, D_MODEL), lambda i, s: (s, i, 0)),
                  pl.BlockSpec((None, d, D_MODEL), w_index), tile, tile,
                  pl.BlockSpec((1, d), lambda i, s: (0, 0))],
        out_specs=[tile, pl.BlockSpec((1, d), lambda i, s: (0, 0))],
        out_shape=[jax.ShapeDtypeStruct((tokens, d), F32), jax.ShapeDtypeStruct((1, d), F32)],
        scratch_shapes=[pltpu.VMEM((tm, d), F32)],
        compiler_params=_params(("arbitrary", "arbitrary")),
    )(dz, w_all, x2d, dx2, norm_g)


def _row_tile(rows, cols, itemsize=4, budget=2 * 1024 * 1024):
    tr = rows
    while tr * cols * itemsize > budget and tr % 16 == 0:
        tr //= 2
    return tr


def _cast_bf16(a, name):
    rows, cols = a.shape
    tr = _row_tile(rows, cols)

    def body(a_ref, o_ref):
        o_ref[...] = a_ref[...].astype(_MXU_DTYPE)

    spec = pl.BlockSpec((tr, cols), lambda i: (i, 0))
    return pl.pallas_call(body, name=name, grid=(rows // tr,), in_specs=[spec], out_specs=spec,
                          out_shape=jax.ShapeDtypeStruct(a.shape, _MXU_DTYPE),
                          compiler_params=_params(("parallel",)))(a)


def _sum_slots(stack, name):
    n, rows, cols = stack.shape
    tr = _row_tile(rows, cols * n)

    def body(s_ref, o_ref):
        total = s_ref[0]
        for k in range(1, n):
            total = total + s_ref[k]
        o_ref[...] = total

    return pl.pallas_call(body, name=name, grid=(rows // tr,),
                          in_specs=[pl.BlockSpec((n, tr, cols), lambda i: (0, i, 0))],
                          out_specs=pl.BlockSpec((tr, cols), lambda i: (i, 0)),
                          out_shape=jax.ShapeDtypeStruct((rows, cols), F32),
                          compiler_params=_params(("parallel",)))(stack)


def _add_half(full, landed, core, name):
    n, rows, cols = full.shape
    half = rows // 2
    tr = _row_tile(half, cols)
    nb = half // tr

    def body(core_ref, a_ref, b_ref, o_ref):
        del core_ref
        o_ref[...] = a_ref[...] + b_ref[...]

    grid_spec = pltpu.PrefetchScalarGridSpec(
        num_scalar_prefetch=1, grid=(n, nb),
        in_specs=[pl.BlockSpec((None, tr, cols), lambda j, i, core_ref: (j, core_ref[0] * nb + i, 0)),
                  pl.BlockSpec((None, tr, cols), lambda j, i, core_ref: (j, i, 0))],
        out_specs=pl.BlockSpec((None, tr, cols), lambda j, i, core_ref: (j, i, 0)))
    return pl.pallas_call(body, name=name, grid_spec=grid_spec,
                          out_shape=jax.ShapeDtypeStruct((n, half, cols), F32),
                          compiler_params=_params(("parallel", "parallel")))(core, full, landed)


def _adamw(w, g, m, v, name):
    rows, cols = w.shape
    tr = _row_tile(rows, cols, budget=1024 * 1024)
    c1 = 1.0 - ADAM_B1 ** ADAM_STEP
    c2 = 1.0 - ADAM_B2 ** ADAM_STEP

    def body(w_ref, g_ref, m_ref, v_ref, d_ref, nm_ref, nv_ref):
        grad = g_ref[...]
        nm = ADAM_B1 * m_ref[...] + (1.0 - ADAM_B1) * grad
        nv = ADAM_B2 * v_ref[...] + (1.0 - ADAM_B2) * (grad * grad)
        nm_ref[...] = nm
        nv_ref[...] = nv
        d_ref[...] = (-ADAM_LR) * ((nm / c1) / (jnp.sqrt(nv / c2) + ADAM_EPS) + ADAM_WD * w_ref[...])

    spec = pl.BlockSpec((tr, cols), lambda i: (i, 0))
    shape = jax.ShapeDtypeStruct((rows, cols), F32)
    return pl.pallas_call(body, name=name, grid=(rows // tr,), in_specs=[spec] * 4, out_specs=[spec] * 3,
                          out_shape=[shape] * 3, compiler_params=_params(("parallel",)))(w, g, m, v)


def _local_step(x, loss_target, w_all, pa, pb, wo, conv_w, b_merge, conv_b, rg_wx, rg_bx, rg_wa, rg_ba,
                rg_lambda, hg_lb_logits, hg_norm_g, norm_g, final_norm_g):
    batch, seq, d = x.shape
    x2d = x.reshape(batch * seq, d)
    tgt2d = loss_target.reshape(batch * seq, d)
    z, h = _inproj_fwd(x2d, norm_g, w_all)
    lru = (conv_w, conv_b, rg_wx, rg_bx, rg_wa, rg_ba, rg_lambda)
    ya, hl = _branch_a_fwd(z, *lru, batch, seq)
    yb, states = _branch_b_fwd(z, hg_lb_logits, hg_norm_g, batch, seq)
    dya, dyb, dx2, dz, loss, d_final_g, d_b_merge, d_pa, d_pb, d_wo = _merge_tail(
        ya, yb, z, x2d, tgt2d, b_merge, final_norm_g, pa, pb, wo)
    dz, d_lb_logits, d_hg_g = _branch_b_bwd(z, states, dyb, dz, hg_lb_logits, hg_norm_g, batch, seq)
    dz, d_conv_w, d_conv_b, d_wx, d_bx, d_wa, d_ba, d_lam = _branch_a_bwd(z, hl, dya, dz, *lru, batch, seq)
    d_w_in = _inproj_dw(h, dz)
    grad_x, d_norm_g = _inproj_dx(dz, w_all, x2d, dx2, norm_g)
    big = (d_w_in, d_pa, d_pb, d_wo)
    small = dict(b_merge=d_b_merge, conv_w=d_conv_w, conv_b=d_conv_b, rg_wx=d_wx, rg_bx=d_bx, rg_wa=d_wa,
                 rg_ba=d_ba, rg_lambda=d_lam, hg_lb_logits=d_lb_logits, hg_norm_g=d_hg_g, norm_g=d_norm_g,
                 final_norm_g=d_final_g)
    return loss[0, 0], grad_x.reshape(batch, seq, d), big, small


_SMALL_ORDER = ("b_merge", "conv_w", "conv_b", "rg_wx", "rg_bx", "rg_wa", "rg_ba", "rg_lambda", "hg_lb_logits",
                "hg_norm_g", "norm_g", "final_norm_g")
N_DEV = 8
PIECE_ROWS = 272


def _pack_small(tree):
    flat = jnp.concatenate([tree[k].reshape(-1) for k in _SMALL_ORDER])
    flat = jnp.pad(flat, (0, N_DEV * PIECE_ROWS * LANES - flat.shape[0]))
    return flat.reshape(N_DEV * PIECE_ROWS, LANES)


def _unpack_small(packed, like):
    flat = packed.reshape(-1)
    out, pos = {}, 0
    for k in _SMALL_ORDER:
        n = like[k].size
        out[k] = flat[pos:pos + n].reshape(like[k].shape)
        pos += n
    return out


def _mesh_position():
    x, y, c = lax.axis_index("x"), lax.axis_index("y"), lax.axis_index("c")
    other_chips = [(1 - x, y), (x, 1 - y), (1 - x, 1 - y)]
    return x, y, c, other_chips


def _other_devices(x, y, c):
    flips = [(fx, fy, fc) for fx in (0, 1) for fy in (0, 1) for fc in (0, 1) if (fx, fy, fc) != (0, 0, 0)]
    return [(jnp.where(fx, 1 - x, x), jnp.where(fy, 1 - y, y), jnp.where(fc, 1 - c, c)) for fx, fy, fc in flips]


def _remote(src, dst, send_sems, recv_sems, k, device):
    return pltpu.make_async_remote_copy(src_ref=src, dst_ref=dst, send_sem=send_sems.at[k], recv_sem=recv_sems.at[k],
                                        device_id=device, device_id_type=MESH)


def _gather_weights(shards, conv_w):
    n_big = len(shards)
    n_sem = 6 * n_big + 3

    def body(*refs):
        srcs, cw_src = refs[:n_big], refs[n_big]
        outs, cw_out = refs[n_big + 1:2 * n_big + 1], refs[2 * n_big + 1]
        send_sems, recv_sems, local_sems = refs[2 * n_big + 2:]
        x, y, c, chips = _mesh_position()
        me, sibling = 2 * x + y, (x, y, 1 - c)

        def half(ref, which):
            hs = ref.shape[0] // 2
            return pl.ds(which * hs, hs)

        local = [pltpu.make_async_copy(srcs[a], outs[a].at[me], local_sems.at[a]) for a in range(n_big)]
        local.append(pltpu.make_async_copy(cw_src, cw_out.at[me], local_sems.at[n_big]))
        for cp in local:
            cp.start()
        sends = []
        for a in range(n_big):
            for j, (cx, cy) in enumerate(chips):
                sends.append(_remote(srcs[a].at[half(srcs[a], c), :], outs[a].at[me, half(srcs[a], c), :],
                                     send_sems, recv_sems, 6 * a + j, (cx, cy, c)))
        for j, (cx, cy) in enumerate(chips):
            sends.append(_remote(cw_src, cw_out.at[me], send_sems, recv_sems, 6 * n_big + j, (cx, cy, c)))
        for cp in sends:
            cp.start()
        for j, (cx, cy) in enumerate(chips):
            theirs = 2 * cx + cy
            for a in range(n_big):
                landed = outs[a].at[theirs, half(srcs[a], c), :]
                _remote(landed, landed, send_sems, recv_sems, 6 * a + j, (cx, cy, c)).wait_recv()
                passed = _remote(landed, landed, send_sems, recv_sems, 6 * a + 3 + j, sibling)
                passed.start()
                sends.append(passed)
        for j, (cx, cy) in enumerate(chips):
            theirs = 2 * cx + cy
            _remote(cw_out.at[theirs], cw_out.at[theirs], send_sems, recv_sems, 6 * n_big + j, (cx, cy, c)).wait_recv()
            for a in range(n_big):
                landed = outs[a].at[theirs, half(srcs[a], 1 - c), :]
                _remote(landed, landed, send_sems, recv_sems, 6 * a + 3 + j, sibling).wait_recv()
        for cp in sends:
            cp.wait_send()
        for cp in local:
            cp.wait()

    hbm = pl.BlockSpec(memory_space=pl.ANY)
    out_shape = [jax.ShapeDtypeStruct((N_SHARDS,) + s.shape, s.dtype) for s in shards]
    out_shape.append(jax.ShapeDtypeStruct((N_SHARDS,) + conv_w.shape, conv_w.dtype))
    return pl.pallas_call(
        body, name="gather_weights",
        in_specs=[hbm] * (n_big + 1), out_specs=[hbm] * (n_big + 1), out_shape=out_shape,
        scratch_shapes=[pltpu.SemaphoreType.DMA((n_sem,)), pltpu.SemaphoreType.DMA((n_sem,)),
                        pltpu.SemaphoreType.DMA((n_big + 1,))],
    )(*shards, conv_w)


def _exchange_halves(bigs, small):
    n_big = len(bigs)
    n_sem = n_big + N_DEV - 1

    def body(*refs):
        srcs, small_src = refs[:n_big], refs[n_big]
        outs, small_out = refs[n_big + 1:2 * n_big + 1], refs[2 * n_big + 1]
        send_sems, recv_sems, local_sem = refs[2 * n_big + 2:]
        x, y, c, _ = _mesh_position()
        me, sibling = 4 * x + 2 * y + c, (x, y, 1 - c)
        mine = pltpu.make_async_copy(small_src.at[pl.ds(me * PIECE_ROWS, PIECE_ROWS), :], small_out.at[me], local_sem)
        mine.start()
        copies = []
        for a in range(n_big):
            hs = srcs[a].shape[1] // 2
            copies.append(_remote(srcs[a].at[:, pl.ds((1 - c) * hs, hs), :], outs[a], send_sems, recv_sems, a, sibling))
        for k, (px, py, pc) in enumerate(_other_devices(x, y, c)):
            piece = small_src.at[pl.ds((4 * px + 2 * py + pc) * PIECE_ROWS, PIECE_ROWS), :]
            copies.append(_remote(piece, small_out.at[me], send_sems, recv_sems, n_big + k, (px, py, pc)))
        for cp in copies:
            cp.start()
        for cp in copies:
            cp.wait()
        mine.wait()

    hbm = pl.BlockSpec(memory_space=pl.ANY)
    out_shape = [jax.ShapeDtypeStruct((g.shape[0], g.shape[1] // 2, g.shape[2]), F32) for g in bigs]
    out_shape.append(jax.ShapeDtypeStruct((N_DEV, PIECE_ROWS, LANES), F32))
    return pl.pallas_call(
        body, name="exchange_halves",
        in_specs=[hbm] * (n_big + 1), out_specs=[hbm] * (n_big + 1), out_shape=out_shape,
        scratch_shapes=[pltpu.SemaphoreType.DMA((n_sem,)), pltpu.SemaphoreType.DMA((n_sem,)), pltpu.SemaphoreType.DMA],
    )(*bigs, small)


def _scatter_shards(bigs, small_piece):
    n_big = len(bigs)
    n_sem = 3 * n_big + N_DEV - 1

    def body(*refs):
        srcs, small_src = refs[:n_big], refs[n_big]
        outs, small_out = refs[n_big + 1:2 * n_big + 1], refs[2 * n_big + 1]
        send_sems, recv_sems, local_sems = refs[2 * n_big + 2:]
        x, y, c, chips = _mesh_position()
        chip, me = 2 * x + y, 4 * x + 2 * y + c
        local = [pltpu.make_async_copy(srcs[a].at[chip], outs[a].at[chip], local_sems.at[a]) for a in range(n_big)]
        local.append(pltpu.make_async_copy(small_src, small_out.at[me], local_sems.at[n_big]))
        for cp in local:
            cp.start()
        copies = []
        for a in range(n_big):
            for j, (cx, cy) in enumerate(chips):
                copies.append(_remote(srcs[a].at[2 * cx + cy], outs[a].at[chip], send_sems, recv_sems, 3 * a + j, (cx, cy, c)))
        for k, peer in enumerate(_other_devices(x, y, c)):
            copies.append(_remote(small_src, small_out.at[me], send_sems, recv_sems, 3 * n_big + k, peer))
        for cp in copies:
            cp.start()
        for cp in copies:
            cp.wait()
        for cp in local:
            cp.wait()

    hbm = pl.BlockSpec(memory_space=pl.ANY)
    out_shape = [jax.ShapeDtypeStruct(g.shape, F32) for g in bigs]
    out_shape.append(jax.ShapeDtypeStruct((N_DEV, PIECE_ROWS, LANES), F32))
    return pl.pallas_call(
        body, name="scatter_shards",
        in_specs=[hbm] * (n_big + 1), out_specs=[hbm] * (n_big + 1), out_shape=out_shape,
        scratch_shapes=[pltpu.SemaphoreType.DMA((n_sem,)), pltpu.SemaphoreType.DMA((n_sem,)),
                        pltpu.SemaphoreType.DMA((n_big + 1,))],
    )(*bigs, small_piece)


def _join_halves(halves):
    n_big = len(halves)

    def body(*refs):
        srcs, outs = refs[:n_big], refs[n_big:2 * n_big]
        send_sems, recv_sems, local_sems = refs[2 * n_big:]
        x, y, c, _ = _mesh_position()
        sibling = (x, y, 1 - c)
        local, copies = [], []
        for a in range(n_big):
            hs = srcs[a].shape[0]
            local.append(pltpu.make_async_copy(srcs[a], outs[a].at[pl.ds(c * hs, hs), :], local_sems.at[a]))
            copies.append(_remote(srcs[a], outs[a].at[pl.ds(c * hs, hs), :], send_sems, recv_sems, a, sibling))
        for cp in local + copies:
            cp.start()
        for a in range(n_big):
            hs = srcs[a].shape[0]
            copies[a].wait_send()
            _remote(srcs[a], outs[a].at[pl.ds((1 - c) * hs, hs), :], send_sems, recv_sems, a, sibling).wait_recv()
        for cp in local:
            cp.wait()

    hbm = pl.BlockSpec(memory_space=pl.ANY)
    return pl.pallas_call(
        body, name="join_halves",
        in_specs=[hbm] * n_big, out_specs=[hbm] * n_big,
        out_shape=[jax.ShapeDtypeStruct((2 * h.shape[0], h.shape[1]), F32) for h in halves],
        scratch_shapes=[pltpu.SemaphoreType.DMA((n_big,)), pltpu.SemaphoreType.DMA((n_big,)),
                        pltpu.SemaphoreType.DMA((n_big,))],
    )(*halves)


def _reduce_gradients(bigs, small_tree):
    core = lax.axis_index("c").astype(jnp.int32).reshape(1)
    *landed, small_landed = _exchange_halves(bigs, _pack_small(small_tree))
    chip_sums = [_add_half(g, l, core, f"add_half_{a}") for a, (g, l) in enumerate(zip(bigs, landed))]
    small_piece = _sum_slots(small_landed, "sum_small")
    *by_chip, small_all = _scatter_shards(chip_sums, small_piece)
    halves = [_sum_slots(s, f"sum_chips_{a}") for a, s in enumerate(by_chip)]
    shards = _join_halves(halves)
    return shards, _unpack_small(small_all, small_tree)


def kernel(x, w_in, b_merge, conv_w, conv_b, rg_wx, rg_bx, rg_wa, rg_ba, rg_lambda, hg_lb_logits, hg_norm_g, proj_a, proj_b, w_out, norm_g, final_norm_g, loss_target, m_w_in, m_b_merge, m_conv_w, m_conv_b, m_rg_wx, m_rg_bx, m_rg_wa, m_rg_ba, m_rg_lambda, m_hg_lb_logits, m_hg_norm_g, m_proj_a, m_proj_b, m_w_out, m_norm_g, m_final_norm_g, v_w_in, v_b_merge, v_conv_w, v_conv_b, v_rg_wx, v_rg_bx, v_rg_wa, v_rg_ba, v_rg_lambda, v_hg_lb_logits, v_hg_norm_g, v_proj_a, v_proj_b, v_w_out, v_norm_g, v_final_norm_g):
    d = D_MODEL
    weights = dict(w_in=w_in, b_merge=b_merge, conv_w=conv_w, conv_b=conv_b, rg_wx=rg_wx, rg_bx=rg_bx, rg_wa=rg_wa,
                   rg_ba=rg_ba, rg_lambda=rg_lambda, hg_lb_logits=hg_lb_logits, hg_norm_g=hg_norm_g, proj_a=proj_a,
                   proj_b=proj_b, w_out=w_out, norm_g=norm_g, final_norm_g=final_norm_g)
    m = dict(w_in=m_w_in, b_merge=m_b_merge, conv_w=m_conv_w, conv_b=m_conv_b, rg_wx=m_rg_wx, rg_bx=m_rg_bx,
             rg_wa=m_rg_wa, rg_ba=m_rg_ba, rg_lambda=m_rg_lambda, hg_lb_logits=m_hg_lb_logits, hg_norm_g=m_hg_norm_g,
             proj_a=m_proj_a, proj_b=m_proj_b, w_out=m_w_out, norm_g=m_norm_g, final_norm_g=m_final_norm_g)
    v = dict(w_in=v_w_in, b_merge=v_b_merge, conv_w=v_conv_w, conv_b=v_conv_b, rg_wx=v_rg_wx, rg_bx=v_rg_bx,
             rg_wa=v_rg_wa, rg_ba=v_rg_ba, rg_lambda=v_rg_lambda, hg_lb_logits=v_hg_lb_logits, hg_norm_g=v_hg_norm_g,
             proj_a=v_proj_a, proj_b=v_proj_b, w_out=v_w_out, norm_g=v_norm_g, final_norm_g=v_final_norm_g)
    big_names = ("w_in", "proj_a", "proj_b", "w_out")

    shards = [_cast_bf16(weights[k][0], f"cast_{k}") for k in big_names]
    w_all, pa_all, pb_all, wo_all, cw_all = _gather_weights(shards, conv_w[0])
    conv_w_full = jnp.transpose(cw_all, (1, 0, 2)).reshape(CONV_WIDTH, d)

    loss_part, grad_x, big_grads, small_grads = _local_step(
        x, loss_target, w_all, pa_all.reshape(d, d), pb_all.reshape(d, d), wo_all.reshape(d, d), conv_w_full,
        b_merge, conv_b, rg_wx[0], rg_bx.reshape(1, d), rg_wa[0], rg_ba.reshape(1, d), rg_lambda, hg_lb_logits,
        hg_norm_g, norm_g, final_norm_g.reshape(1, d))
    loss = lax.psum(loss_part, ("x", "y", "c"))

    big_grads = [big_grads[0]] + [g.reshape(N_SHARDS, d // N_SHARDS, d) for g in big_grads[1:]]
    big_shards, small_red = _reduce_gradients(big_grads, small_grads)

    grads, delta, new_m, new_v = {}, {}, {}, {}
    for k, g in zip(big_names, big_shards):
        grads[k] = g.reshape(weights[k].shape)
        dl, nm, nv = _adamw(weights[k][0], g, m[k][0], v[k][0], f"adamw_{k}")
        delta[k], new_m[k], new_v[k] = (t.reshape(weights[k].shape) for t in (dl, nm, nv))
    chip = 2 * lax.axis_index("x") + lax.axis_index("y")
    cols = d // N_SHARDS
    g_conv = lax.dynamic_slice(small_red["conv_w"], (0, chip * cols), (CONV_WIDTH, cols))
    grads["conv_w"] = g_conv.reshape(conv_w.shape)
    dl, nm, nv = _adamw(conv_w[0], g_conv, m_conv_w[0], v_conv_w[0], "adamw_conv_w")
    delta["conv_w"], new_m["conv_w"], new_v["conv_w"] = (t.reshape(conv_w.shape) for t in (dl, nm, nv))
    rest = [k for k in _SMALL_ORDER if k != "conv_w"]
    like = {k: (weights[k] if k != "conv_w" else jnp.zeros((CONV_WIDTH, d), F32)) for k in _SMALL_ORDER}
    packs = [_pack_small({k: (t[k] if k != "conv_w" else like[k]) for k in _SMALL_ORDER}) for t in (weights, m, v)]
    g_pack = _pack_small({k: small_red[k].reshape(like[k].shape) for k in _SMALL_ORDER})
    outs = [_unpack_small(p, like) for p in _adamw(packs[0], g_pack, packs[1], packs[2], "adamw_small")]
    for k in rest:
        grads[k] = small_red[k].reshape(weights[k].shape)
        delta[k], new_m[k], new_v[k] = outs[0][k], outs[1][k], outs[2][k]

    order = ("w_in", "b_merge", "conv_w", "conv_b", "rg_wx", "rg_bx", "rg_wa", "rg_ba", "rg_lambda", "hg_lb_logits",
             "hg_norm_g", "proj_a", "proj_b", "w_out", "norm_g", "final_norm_g")
    return (loss, grad_x, *[grads[k] for k in order], *[delta[k] for k in order], *[new_m[k] for k in order],
            *[new_v[k] for k in order])
```

```python
import functools

import jax
import jax.numpy as jnp
from jax import lax
from jax.experimental import pallas as pl
from jax.experimental.pallas import tpu as pltpu

F32 = jnp.float32
_MXU_DTYPE = jnp.bfloat16

D_MODEL = 1024
LANES = 128
SUBLANES = 8
N_BLK = D_MODEL // LANES
N_GROUPS = 8
N_SHARDS = 4
CONV_WIDTH = 4
LRU_C = 8.0
CHUNK = 64
CHUNKS_IN_FLIGHT = 4
HG_SCALE = float(LANES) ** -0.5
EPS = 1e-6
ADAM_LR, ADAM_B1, ADAM_B2, ADAM_EPS, ADAM_WD, ADAM_STEP = 0.001, 0.9, 0.999, 1e-08, 0.01, 10
VMEM_LIMIT = 56 * 1024 * 1024
MESH = pl.DeviceIdType.MESH

_SLOT_TO_GROUP = (2, 3, 4, 5, 0, 1, 6, 7)


def _group_of_slot(s):
    return jnp.where(s < 4, s + 2, jnp.where(s < 6, s - 4, s))


def _mm(a, b):
    return lax.dot_general(a.astype(_MXU_DTYPE), b.astype(_MXU_DTYPE), (((1,), (0,)), ((), ())),
                           preferred_element_type=F32)


def _mm_nt(a, b):
    return lax.dot_general(a.astype(_MXU_DTYPE), b.astype(_MXU_DTYPE), (((1,), (1,)), ((), ())),
                           preferred_element_type=F32)


def _mm_tn(a, b):
    return lax.dot_general(a.astype(_MXU_DTYPE), b.astype(_MXU_DTYPE), (((0,), (0,)), ((), ())),
                           preferred_element_type=F32)


def _sigmoid(x):
    return 1.0 / (1.0 + jnp.exp(-x))


def _log1p_pos(y):
    series = y * (1.0 - y * (0.5 - y * (1.0 / 3.0 - y * 0.25)))
    return jnp.where(y < 0.01, series, jnp.log(1.0 + y))


def _expm1_neg(y):
    series = y * (1.0 + y * 0.5 * (1.0 + y * (1.0 / 3.0) * (1.0 + y * 0.25 * (1.0 + y * 0.2))))
    return jnp.where(y > -0.02, series, jnp.exp(y) - 1.0)


def _softplus(x):
    return jnp.maximum(x, 0.0) + _log1p_pos(jnp.exp(-jnp.abs(x)))


def _shift_down(x, n):
    rows = lax.broadcasted_iota(jnp.int32, x.shape, 0)
    return jnp.where(rows >= n, pltpu.roll(x, n, 0), 0.0)


def _shift_up(x, n):
    size = x.shape[0]
    rows = lax.broadcasted_iota(jnp.int32, x.shape, 0)
    return jnp.where(rows < size - n, pltpu.roll(x, size - n, 0), 0.0)


def _params(dims, vmem=VMEM_LIMIT):
    return pltpu.CompilerParams(dimension_semantics=dims, vmem_limit_bytes=vmem)


def _inproj_fwd(x2d, norm_g, w_all):
    tokens, d = x2d.shape
    tm = min(512, tokens)

    def body(x_ref, g_ref, w_ref, z_ref, h_ref, h_scr):
        @pl.when(pl.program_id(1) == 0)
        def _():
            x = x_ref[...]
            r = lax.rsqrt(jnp.mean(x * x, axis=-1, keepdims=True) + EPS)
            h = ((x * r) * g_ref[...]).astype(_MXU_DTYPE)
            h_scr[...] = h
            h_ref[...] = h

        z_ref[...] = _mm(h_scr[...], w_ref[...])

    def w_index(i, s):
        g = _group_of_slot(s)
        return (g // 2, 0, g % 2)

    return pl.pallas_call(
        body, name="inproj_fwd",
        grid=(tokens // tm, N_GROUPS),
        in_specs=[pl.BlockSpec((tm, d), lambda i, s: (i, 0)),
                  pl.BlockSpec((1, d), lambda i, s: (0, 0)),
                  pl.BlockSpec((None, d, D_MODEL), w_index)],
        out_specs=[pl.BlockSpec((None, tm, D_MODEL), lambda i, s: (s, i, 0)),
                   pl.BlockSpec((tm, d), lambda i, s: (i, 0))],
        out_shape=[jax.ShapeDtypeStruct((N_GROUPS, tokens, D_MODEL), F32),
                   jax.ShapeDtypeStruct((tokens, d), _MXU_DTYPE)],
        scratch_shapes=[pltpu.VMEM((tm, d), _MXU_DTYPE)],
        compiler_params=_params(("parallel", "arbitrary")),
    )(x2d, norm_g, w_all)


def _lru_gates(xa, cw_ref, cb_ref, wx_ref, bx_ref, wa_ref, ba_ref, lam_ref):
    xc = (cb_ref[...] + cw_ref[3:4, :] * xa + cw_ref[2:3, :] * _shift_down(xa, 1)
          + cw_ref[1:2, :] * _shift_down(xa, 2) + cw_ref[0:1, :] * _shift_down(xa, 3))
    gi = _sigmoid(_mm(xc, wx_ref[...]) + bx_ref[...])
    gr = _sigmoid(_mm(xc, wa_ref[...]) + ba_ref[...])
    sp = _softplus(-lam_ref[...])
    log_a = (-LRU_C) * gr * sp
    a = jnp.exp(log_a)
    mult = jnp.sqrt(-_expm1_neg(2.0 * log_a))
    return xc, gi, gr, sp, a, mult


def _tile_rows():
    return lax.broadcasted_iota(jnp.int32, (SUBLANES, LANES), 0)


def _scan_forward(a_scr, u_scr, h_scr, seq):
    rows = _tile_rows()

    def tile(j, carry):
        sl = pl.ds(pl.multiple_of(j * SUBLANES, SUBLANES), SUBLANES)
        a = a_scr[sl, :]
        u = u_scr[sl, :]
        for d in (1, 2, 4):
            keep = rows >= d
            a_sh = jnp.where(keep, pltpu.roll(a, d, 0), 1.0)
            u_sh = jnp.where(keep, pltpu.roll(u, d, 0), 0.0)
            u = a * u_sh + u
            a = a * a_sh
        h = u + a * carry
        h_scr[sl, :] = h
        return jnp.broadcast_to(h[SUBLANES - 1:SUBLANES, :], (SUBLANES, LANES))

    lax.fori_loop(0, seq // SUBLANES, tile, jnp.zeros((SUBLANES, LANES), F32))


def _scan_backward(c_scr, d_scr, g_scr, seq):
    rows = _tile_rows()
    n_tiles = seq // SUBLANES

    def tile(jj, carry):
        j = n_tiles - 1 - jj
        sl = pl.ds(pl.multiple_of(j * SUBLANES, SUBLANES), SUBLANES)
        c = c_scr[sl, :]
        g = d_scr[sl, :]
        for d in (1, 2, 4):
            keep = rows < SUBLANES - d
            c_sh = jnp.where(keep, pltpu.roll(c, SUBLANES - d, 0), 1.0)
            g_sh = jnp.where(keep, pltpu.roll(g, SUBLANES - d, 0), 0.0)
            g = c * g_sh + g
            c = c * c_sh
        g = g + c * carry
        g_scr[sl, :] = g
        return jnp.broadcast_to(g[0:1, :], (SUBLANES, LANES))

    lax.fori_loop(0, n_tiles, tile, jnp.zeros((SUBLANES, LANES), F32))


def _lru_param_specs(cb_axis):
    def pick(*ids):
        return ids[cb_axis]

    vec = pl.BlockSpec((1, LANES), lambda *ids: (0, pick(*ids)))
    mat = pl.BlockSpec((None, LANES, LANES), lambda *ids: (pick(*ids), 0, 0))
    return [pl.BlockSpec((CONV_WIDTH, LANES), lambda *ids: (0, pick(*ids))), vec, mat, vec, mat, vec, vec]


def _branch_a_fwd(z, conv_w, conv_b, wx, bx, wa, ba, lam, batch, seq):
    tokens = batch * seq

    def body(z_ref, cw_ref, cb_ref, wx_ref, bx_ref, wa_ref, ba_ref, lam_ref, ya_ref, hl_ref, a_scr, u_scr):
        xa = z_ref[0]
        ga = z_ref[1]
        xc, gi, _, _, a, mult = _lru_gates(xa, cw_ref, cb_ref, wx_ref, bx_ref, wa_ref, ba_ref, lam_ref)
        a_scr[...] = a
        u_scr[...] = mult * gi * xc
        _scan_forward(a_scr, u_scr, hl_ref, seq)
        ya_ref[...] = (hl_ref[...] * (ga * _sigmoid(ga))).astype(_MXU_DTYPE)

    blk = pl.BlockSpec((seq, LANES), lambda b, c: (b, c))
    return pl.pallas_call(
        body, name="branch_a_fwd",
        grid=(batch, N_BLK),
        in_specs=[pl.BlockSpec((2, seq, LANES), lambda b, c: (2, b, c))] + _lru_param_specs(1),
        out_specs=[blk, blk],
        out_shape=[jax.ShapeDtypeStruct((tokens, D_MODEL), _MXU_DTYPE), jax.ShapeDtypeStruct((tokens, D_MODEL), F32)],
        scratch_shapes=[pltpu.VMEM((seq, LANES), F32), pltpu.VMEM((seq, LANES), F32)],
        compiler_params=_params(("parallel", "parallel")),
    )(z, conv_w, conv_b, wx, bx, wa, ba, lam)


def _branch_a_bwd(z, hl, dya, dz, conv_w, conv_b, wx, bx, wa, ba, lam, batch, seq):
    def body(z_ref, hl_ref, dya_ref, dz_in_ref, cw_ref, cb_ref, wx_ref, bx_ref, wa_ref, ba_ref, lam_ref,
             dz_ref, dcw_ref, dcb_ref, dwx_ref, dbx_ref, dwa_ref, dba_ref, dlam_ref, c_scr, d_scr, g_scr):
        del dz_in_ref
        xa = z_ref[0]
        ga = z_ref[1]
        hl = hl_ref[...]
        dya = dya_ref[...]
        xc, gi, gr, sp, a, mult = _lru_gates(xa, cw_ref, cb_ref, wx_ref, bx_ref, wa_ref, ba_ref, lam_ref)
        sga = _sigmoid(ga)
        dz_ref[1] = (dya * hl * (sga * (1.0 + ga * (1.0 - sga)))).astype(_MXU_DTYPE)
        c_scr[...] = _shift_up(a, 1)
        d_scr[...] = dya * (ga * sga)
        _scan_backward(c_scr, d_scr, g_scr, seq)
        g = g_scr[...]
        da = g * _shift_down(hl, 1)
        dmult = g * gi * xc
        dgi = g * mult * xc
        dxc = g * mult * gi
        dlog_a = da * a - dmult * (a * a) / mult
        dgr = dlog_a * (-LRU_C) * sp
        dsp = jnp.sum(dlog_a * gr, axis=0, keepdims=True) * (-LRU_C)
        dlam = -dsp * _sigmoid(-lam_ref[...])
        dpi = dgi * gi * (1.0 - gi)
        dpr = dgr * gr * (1.0 - gr)
        dxc = dxc + _mm_nt(dpi, wx_ref[...]) + _mm_nt(dpr, wa_ref[...])
        dwx = _mm_tn(xc, dpi)
        dwa = _mm_tn(xc, dpr)
        dbx = jnp.sum(dpi, axis=0, keepdims=True)
        dba = jnp.sum(dpr, axis=0, keepdims=True)
        dxa = (cw_ref[3:4, :] * dxc + cw_ref[2:3, :] * _shift_up(dxc, 1) + cw_ref[1:2, :] * _shift_up(dxc, 2)
               + cw_ref[0:1, :] * _shift_up(dxc, 3))
        dz_ref[0] = dxa.astype(_MXU_DTYPE)
        dcb = jnp.sum(dxc, axis=0, keepdims=True)
        dcw = [jnp.sum(dxc * _shift_down(xa, CONV_WIDTH - 1 - k), axis=0, keepdims=True) if k < CONV_WIDTH - 1
               else jnp.sum(dxc * xa, axis=0, keepdims=True) for k in range(CONV_WIDTH)]

        @pl.when(pl.program_id(1) == 0)
        def _():
            for k in range(CONV_WIDTH):
                dcw_ref[k:k + 1, :] = dcw[k]
            dcb_ref[...] = dcb
            dwx_ref[...] = dwx
            dbx_ref[...] = dbx
            dwa_ref[...] = dwa
            dba_ref[...] = dba
            dlam_ref[...] = dlam

        @pl.when(pl.program_id(1) != 0)
        def _():
            for k in range(CONV_WIDTH):
                dcw_ref[k:k + 1, :] += dcw[k]
            dcb_ref[...] += dcb
            dwx_ref[...] += dwx
            dbx_ref[...] += dbx
            dwa_ref[...] += dwa
            dba_ref[...] += dba
            dlam_ref[...] += dlam

    tokens = batch * seq
    blk = pl.BlockSpec((seq, LANES), lambda c, b: (b, c))
    vec = pl.BlockSpec((1, LANES), lambda c, b: (0, c))
    mat = pl.BlockSpec((None, LANES, LANES), lambda c, b: (c, 0, 0))
    vec_shape = jax.ShapeDtypeStruct((1, D_MODEL), F32)
    mat_shape = jax.ShapeDtypeStruct((N_BLK, LANES, LANES), F32)
    return pl.pallas_call(
        body, name="branch_a_bwd",
        grid=(N_BLK, batch),
        in_specs=[pl.BlockSpec((2, seq, LANES), lambda c, b: (2, b, c)), blk, blk,
                  pl.BlockSpec(memory_space=pl.ANY)] + _lru_param_specs(0),
        out_specs=[pl.BlockSpec((2, seq, LANES), lambda c, b: (2, b, c)),
                   pl.BlockSpec((CONV_WIDTH, LANES), lambda c, b: (0, c)), vec, mat, vec, mat, vec, vec],
        out_shape=[jax.ShapeDtypeStruct((N_GROUPS, tokens, D_MODEL), _MXU_DTYPE),
                   jax.ShapeDtypeStruct((CONV_WIDTH, D_MODEL), F32), vec_shape, mat_shape, vec_shape, mat_shape,
                   vec_shape, vec_shape],
        scratch_shapes=[pltpu.VMEM((seq, LANES), F32)] * 3,
        input_output_aliases={3: 0},
        compiler_params=_params(("parallel", "arbitrary")),
    )(z, hl, dya, dz, conv_w, conv_b, wx, bx, wa, ba, lam)


def _chunk_masks(transposed=False):
    r = lax.broadcasted_iota(jnp.int32, (CHUNK, CHUNK), 0)
    c = lax.broadcasted_iota(jnp.int32, (CHUNK, CHUNK), 1)
    return r <= c if transposed else r >= c


def _row_blocks(seq, fn):
    block = min(256, seq)

    def trip(i, carry):
        fn(pl.ds(pl.multiple_of(i * block, block), block))
        return carry

    lax.fori_loop(0, seq // block, trip, 0)


def _hgrn_prepare(z_ref, lb_ref, f_scr, logf_scr, qh_scr, seq):
    lb = _sigmoid(lb_ref[0:1, :] - lb_ref[1:2, :])

    def block(rows):
        q = z_ref[0, rows, :]
        f = lb + (1.0 - lb) * _sigmoid(z_ref[1, rows, :])
        f_scr[rows, :] = f
        logf_scr[rows, :] = jnp.log(f)
        qh_scr[rows, :] = q * _sigmoid(q)

    _row_blocks(seq, block)
    return lb


def _cumsum_rows(x, reverse=False):
    shift = _shift_up if reverse else _shift_down
    d = 1
    while d < x.shape[0]:
        x = x + shift(x, d)
        d *= 2
    return x


def _lane_mean(x):
    return jnp.mean(x, axis=-1, keepdims=True)


def _token_contractions(lhs_scr, rhs_scr, out_ref, seq):
    rows_id = lax.broadcasted_iota(jnp.int32, (LANES, LANES), 0)

    def transposed(p):
        rows = pl.ds(pl.multiple_of(p * LANES, LANES), LANES)
        return jnp.transpose(lhs_scr[rows, :]).astype(_MXU_DTYPE), rhs_scr[rows, :]

    def contract(p, s):
        lhs_t, rhs = s
        return (_mm(lhs_t, jnp.where(rows_id < CHUNK, rhs, 0.0)), _mm(lhs_t, jnp.where(rows_id >= CHUNK, rhs, 0.0)))

    def store(p, out):
        out_ref[2 * p] = out[0]
        out_ref[2 * p + 1] = out[1]

    _independent_trips(seq // LANES, [transposed, contract], store)


def _chunk_rows(c):
    return pl.ds(pl.multiple_of(c * CHUNK, CHUNK), CHUNK)


def _chunk_terms(c, z_ref, f_scr, qh_scr, b_scr):
    rows = _chunk_rows(c)
    b = b_scr[rows, :]
    b_mid = b_scr[pl.ds(c * CHUNK + CHUNK // 2, 1), :]
    b_last = b_scr[pl.ds(c * CHUNK + CHUNK - 1, 1), :]
    qh = qh_scr[rows, :]
    k = 1.0 - f_scr[rows, :]
    v = z_ref[2, rows, :]
    e_q = jnp.exp(b - b_mid) * HG_SCALE
    e_k = jnp.exp(b_mid - b)
    e_qi = jnp.exp(b) * HG_SCALE
    e_ks = jnp.exp(b_last - b)
    decay = jnp.exp(b_last)
    return rows, qh, k, v, e_q, e_k, e_qi, e_ks, decay


def _independent_trips(n, stages, store, group=CHUNKS_IN_FLIGHT):
    stages = stages if isinstance(stages, (list, tuple)) else [stages]
    group = min(group, n)

    def trip(g, carry):
        ids = [g * group + i for i in range(group)]
        state = [stages[0](c) for c in ids]
        for stage in stages[1:]:
            state = [stage(c, s) for c, s in zip(ids, state)]
        for c, s in zip(ids, state):
            store(c, s)
        return carry

    lax.fori_loop(0, n // group, trip, 0)


def _branch_b_fwd(z, lb_logits, hg_g, batch, seq):
    tokens = batch * seq
    n_chunks = seq // CHUNK

    def body(z_ref, lb_ref, g_ref, yb_ref, st_ref, f_scr, logf_scr, qh_scr, b_scr, o_scr, qi_scr, ks_scr, dec_scr):
        _hgrn_prepare(z_ref, lb_ref, f_scr, logf_scr, qh_scr, seq)
        causal = _chunk_masks()
        gain = g_ref[...]

        def cumulate(c):
            return _cumsum_rows(logf_scr[_chunk_rows(c), :])

        def store_cumulated(c, b):
            b_scr[_chunk_rows(c), :] = b

        def scores(c):
            _, qh, k, v, e_q, e_k, e_qi, e_ks, decay = _chunk_terms(c, z_ref, f_scr, qh_scr, b_scr)
            return _mm_nt(qh * e_q, k * e_k), v, qh * e_qi, k * e_ks, decay

        def within_chunk(c, s):
            att, v, q_int, k_st, decay = s
            return _mm(jnp.where(causal, att, 0.0), v), q_int, k_st, decay

        def store_within_chunk(c, out):
            rows = _chunk_rows(c)
            o_scr[rows, :], qi_scr[rows, :], ks_scr[rows, :], dec_scr[pl.ds(c, 1), :] = out

        def carry_state(c, state_t):
            update = st_ref[c]
            st_ref[c] = state_t
            return state_t * dec_scr[pl.ds(c, 1), :] + update

        def finish(c):
            rows = _chunk_rows(c)
            o = o_scr[rows, :] + _mm_nt(qi_scr[rows, :], st_ref[c])
            r = lax.rsqrt(_lane_mean(o * o) + EPS)
            gb = z_ref[3, rows, :]
            return (((o * r) * gain) * (gb * _sigmoid(gb))).astype(_MXU_DTYPE)

        def store_finished(c, yb):
            yb_ref[_chunk_rows(c), :] = yb

        _independent_trips(n_chunks, cumulate, store_cumulated)
        _independent_trips(n_chunks, [scores, within_chunk], store_within_chunk)
        _token_contractions(z_ref.at[2], ks_scr, st_ref, seq)
        lax.fori_loop(0, n_chunks, carry_state, jnp.zeros((LANES, LANES), F32))
        _independent_trips(n_chunks, finish, store_finished, group=2 * CHUNKS_IN_FLIGHT)

    seq_buf = pltpu.VMEM((seq, LANES), F32)
    return pl.pallas_call(
        body, name="branch_b_fwd",
        grid=(batch, N_BLK),
        in_specs=[pl.BlockSpec((4, seq, LANES), lambda b, h: (0, b, h)),
                  pl.BlockSpec((2, LANES), lambda b, h: (0, h)),
                  pl.BlockSpec((1, LANES), lambda b, h: (0, 0))],
        out_specs=[pl.BlockSpec((seq, LANES), lambda b, h: (b, h)),
                   pl.BlockSpec((None, n_chunks, LANES, LANES), lambda b, h: (b * N_BLK + h, 0, 0, 0))],
        out_shape=[jax.ShapeDtypeStruct((tokens, D_MODEL), _MXU_DTYPE),
                   jax.ShapeDtypeStruct((batch * N_BLK, n_chunks, LANES, LANES), F32)],
        scratch_shapes=[seq_buf] * 7 + [pltpu.VMEM((n_chunks, LANES), F32)],
        compiler_params=_params(("parallel", "parallel")),
    )(z, lb_logits, hg_g)


def _branch_b_bwd(z, states, dyb, dz, lb_logits, hg_g, batch, seq):
    n_chunks = seq // CHUNK

    def body(z_ref, st_ref, dyb_ref, dz_in_ref, lb_ref, g_ref, dz_ref, dlog_ref, dg_ref,
             f_scr, logf_scr, qh_scr, b_scr, do_scr, qi_scr, dqh_scr, df_scr, dec_scr, dgp_scr, dlb_scr, dst_scr):
        del dz_in_ref
        first = (pl.program_id(0) == 0) & (pl.program_id(1) == 0)
        lb = _hgrn_prepare(z_ref, lb_ref, f_scr, logf_scr, qh_scr, seq)
        causal = _chunk_masks()
        anti_causal = _chunk_masks(transposed=True)
        gain = g_ref[...]

        @pl.when(first)
        def _():
            dg_ref[...] = jnp.zeros_like(dg_ref)

        @pl.when(pl.program_id(1) == 0)
        def _():
            dlb_scr[...] = jnp.zeros_like(dlb_scr)

        def cumulate(c):
            return _cumsum_rows(logf_scr[_chunk_rows(c), :])

        def store_cumulated(c, b):
            b_scr[_chunk_rows(c), :] = b

        def scores(c):
            _, qh, k, v, e_q, e_k, e_qi, e_ks, decay = _chunk_terms(c, z_ref, f_scr, qh_scr, b_scr)
            q_int = qh * e_qi
            return _mm_nt(qh * e_q, k * e_k), _mm_nt(q_int, st_ref[c]), v, q_int, decay

        def output_gradient(c, s):
            att, o_inter, v, q_int, decay = s
            rows = _chunk_rows(c)
            o = _mm(jnp.where(causal, att, 0.0), v) + o_inter
            r = lax.rsqrt(_lane_mean(o * o) + EPS)
            o_n = o * r
            gb = z_ref[3, rows, :]
            sgb = _sigmoid(gb)
            dyb_c = dyb_ref[rows, :]
            d_ong = dyb_c * (gb * sgb)
            d_gb = (dyb_c * (o_n * gain) * (sgb * (1.0 + gb * (1.0 - sgb)))).astype(_MXU_DTYPE)
            d_gain = jnp.sum(d_ong * o_n, axis=0, keepdims=True)
            d_on = d_ong * gain
            return d_gb, d_gain, r * (d_on - o_n * _lane_mean(d_on * o_n)), q_int, decay

        def store_output_gradient(c, out):
            rows = _chunk_rows(c)
            dz_ref[3, rows, :], dgp_scr[pl.ds(c, 1), :], do_scr[rows, :], qi_scr[rows, :], dec_scr[pl.ds(c, 1), :] = out

        def carry_state_gradient(cc, d_state_t):
            c = n_chunks - 1 - cc
            update = dst_scr[c]
            dst_scr[c] = d_state_t
            return d_state_t * dec_scr[pl.ds(c, 1), :] + update

        def score_gradients(c):
            rows, qh, k, v, e_q, e_k, e_qi, e_ks, decay = _chunk_terms(c, z_ref, f_scr, qh_scr, b_scr)
            state_t = st_ref[c]
            d_state_t = dst_scr[c]
            d_o = do_scr[rows, :]
            q_in, k_in, q_int, k_st = qh * e_q, k * e_k, qh * e_qi, k * e_ks
            first = (_mm_nt(k_in, q_in), _mm_nt(d_o, v), _mm_nt(v, d_o), _mm_nt(k_st, d_state_t), _mm(d_o, state_t),
                     _mm(v, d_state_t))
            d_decay = jnp.sum(state_t * d_state_t, axis=0, keepdims=True)
            return first, d_o, q_in, k_in, q_int, k_st, e_q, e_k, e_qi, e_ks, decay, d_decay

        def input_gradients(c, s):
            (att_t, d_att, d_att_t, dv_inter, dq_int, dk_st), d_o, q_in, k_in, q_int, k_st, e_q, e_k, e_qi, e_ks, decay, d_decay = s
            rows = _chunk_rows(c)
            d_v = _mm(jnp.where(anti_causal, att_t, 0.0), d_o) + dv_inter
            dq_in = _mm(jnp.where(causal, d_att, 0.0), k_in)
            dk_in = _mm(jnp.where(anti_causal, d_att_t, 0.0), q_in)
            d_k = dk_in * e_k + dk_st * e_ks
            kk = dk_st * k_st
            d_b = dq_in * q_in + dq_int * q_int - dk_in * k_in - kk
            d_b_last = jnp.sum(kk, axis=0, keepdims=True) + decay * d_decay
            d_logf = _cumsum_rows(d_b, reverse=True) + d_b_last
            return d_v.astype(_MXU_DTYPE), dq_in * e_q + dq_int * e_qi, d_logf / f_scr[rows, :] - d_k

        def store_input_gradients(c, out):
            rows = _chunk_rows(c)
            dz_ref[2, rows, :], dqh_scr[rows, :], df_scr[rows, :] = out

        def input_activations(rows):
            q = z_ref[0, rows, :]
            sq = _sigmoid(q)
            dz_ref[0, rows, :] = (dqh_scr[rows, :] * (sq * (1.0 + q * (1.0 - sq)))).astype(_MXU_DTYPE)
            sg = _sigmoid(z_ref[1, rows, :])
            d_f = df_scr[rows, :]
            dz_ref[1, rows, :] = (d_f * (1.0 - lb) * sg * (1.0 - sg)).astype(_MXU_DTYPE)
            dlb_scr[...] += jnp.sum(d_f * (1.0 - sg), axis=0, keepdims=True)

        _independent_trips(n_chunks, cumulate, store_cumulated)
        _independent_trips(n_chunks, [scores, output_gradient], store_output_gradient)
        _token_contractions(do_scr, qi_scr, dst_scr, seq)
        lax.fori_loop(0, n_chunks, carry_state_gradient, jnp.zeros((LANES, LANES), F32))
        _independent_trips(n_chunks, [score_gradients, input_gradients], store_input_gradients)
        dg_ref[...] += jnp.sum(dgp_scr[...], axis=0, keepdims=True)
        _row_blocks(seq, input_activations)
        d_l0 = dlb_scr[...] * lb * (1.0 - lb)
        dlog_ref[0:1, :] = d_l0
        dlog_ref[1:2, :] = -d_l0

    tokens = batch * seq
    seq_buf = pltpu.VMEM((seq, LANES), F32)
    chunk_rows = pltpu.VMEM((n_chunks, LANES), F32)
    return pl.pallas_call(
        body, name="branch_b_bwd",
        grid=(N_BLK, batch),
        in_specs=[pl.BlockSpec((4, seq, LANES), lambda h, b: (0, b, h)),
                  pl.BlockSpec((None, n_chunks, LANES, LANES), lambda h, b: (b * N_BLK + h, 0, 0, 0)),
                  pl.BlockSpec((seq, LANES), lambda h, b: (b, h)),
                  pl.BlockSpec(memory_space=pl.ANY),
                  pl.BlockSpec((2, LANES), lambda h, b: (0, h)),
                  pl.BlockSpec((1, LANES), lambda h, b: (0, 0))],
        out_specs=[pl.BlockSpec((4, seq, LANES), lambda h, b: (0, b, h)),
                   pl.BlockSpec((2, LANES), lambda h, b: (0, h)),
                   pl.BlockSpec((1, LANES), lambda h, b: (0, 0))],
        out_shape=[jax.ShapeDtypeStruct((N_GROUPS, tokens, D_MODEL), _MXU_DTYPE),
                   jax.ShapeDtypeStruct((2, D_MODEL), F32),
                   jax.ShapeDtypeStruct((1, LANES), F32)],
        scratch_shapes=[seq_buf] * 8 + [chunk_rows, chunk_rows, pltpu.VMEM((1, LANES), F32),
                                        pltpu.VMEM((n_chunks, LANES, LANES), F32)],
        input_output_aliases={3: 0},
        compiler_params=_params(("arbitrary", "arbitrary")),
    )(z, states, dyb, dz, lb_logits, hg_g)


def _merge_tail(ya, yb, z, x2d, tgt2d, b_merge, final_g, pa, pb, wo):
    tokens, d = x2d.shape
    tm = min(256, tokens)
    n_tiles = tokens // tm

    def body(ya_ref, yb_ref, z_ref, x_ref, t_ref, bm_ref, fg_ref, pa_hbm, pb_hbm, wo_hbm,
             dya_ref, dyb_ref, dx2_ref, dz_ref, loss_ref, dfg_ref, dbm_ref, dpa_hbm, dpb_hbm, dwo_hbm,
             pa_s, pb_s, wo_s, dpa_s, dpb_s, dwo_s):
        i = pl.program_id(0)

        @pl.when(i == 0)
        def _():
            pltpu.sync_copy(pa_hbm, pa_s)
            pltpu.sync_copy(pb_hbm, pb_s)
            pltpu.sync_copy(wo_hbm, wo_s)
            dpa_s[...] = jnp.zeros_like(dpa_s)
            dpb_s[...] = jnp.zeros_like(dpb_s)
            dwo_s[...] = jnp.zeros_like(dwo_s)
            loss_ref[...] = jnp.zeros_like(loss_ref)
            dfg_ref[...] = jnp.zeros_like(dfg_ref)
            dbm_ref[...] = jnp.zeros_like(dbm_ref)

        ya_t = ya_ref[...]
        yb_t = yb_ref[...]
        out_a = _mm(ya_t, pa_s[...])
        out_b = _mm(yb_t, pb_s[...])
        g_a = _sigmoid(z_ref[0] + bm_ref[:, :d])
        g_b = _sigmoid(z_ref[1] + bm_ref[:, d:])
        mixed = g_a * out_a + g_b * out_b
        x2 = x_ref[...] + _mm(mixed, wo_s[...])
        r = lax.rsqrt(jnp.mean(x2 * x2, axis=-1, keepdims=True) + EPS)
        xn = x2 * r
        fg = fg_ref[...]
        diff = xn * fg - t_ref[...]
        loss_ref[...] += jnp.sum(diff * diff) * (0.5 / d)
        dy = diff * (1.0 / d)
        dfg_ref[...] += jnp.sum(dy * xn, axis=0, keepdims=True)
        dxn = dy * fg
        dx2 = r * (dxn - xn * jnp.mean(dxn * xn, axis=-1, keepdims=True))
        dx2_ref[...] = dx2
        dmixed = _mm_nt(dx2, wo_s[...])
        dwo_s[...] += _mm_tn(mixed, dx2)
        dgm_a = dmixed * out_a * g_a * (1.0 - g_a)
        dgm_b = dmixed * out_b * g_b * (1.0 - g_b)
        dz_ref[0] = dgm_a.astype(_MXU_DTYPE)
        dz_ref[1] = dgm_b.astype(_MXU_DTYPE)
        dbm_ref[:, :d] += jnp.sum(dgm_a, axis=0, keepdims=True)
        dbm_ref[:, d:] += jnp.sum(dgm_b, axis=0, keepdims=True)
        dout_a = dmixed * g_a
        dout_b = dmixed * g_b
        dpa_s[...] += _mm_tn(ya_t, dout_a)
        dpb_s[...] += _mm_tn(yb_t, dout_b)
        dya_ref[...] = _mm_nt(dout_a, pa_s[...])
        dyb_ref[...] = _mm_nt(dout_b, pb_s[...])

        @pl.when(i == n_tiles - 1)
        def _():
            pltpu.sync_copy(dpa_s, dpa_hbm)
            pltpu.sync_copy(dpb_s, dpb_hbm)
            pltpu.sync_copy(dwo_s, dwo_hbm)

    tile = pl.BlockSpec((tm, d), lambda i: (i, 0))
    gm = pl.BlockSpec((2, tm, d), lambda i: (3, i, 0))
    row = lambda n: pl.BlockSpec((1, n), lambda i: (0, 0))
    hbm = pl.BlockSpec(memory_space=pl.ANY)
    act = jax.ShapeDtypeStruct((tokens, d), F32)
    mat = jax.ShapeDtypeStruct((d, d), F32)
    return pl.pallas_call(
        body, name="merge_tail",
        grid=(n_tiles,),
        in_specs=[tile, tile, gm, tile, tile, row(2 * d), row(d), hbm, hbm, hbm],
        out_specs=[tile, tile, tile, gm, row(LANES), row(d), row(2 * d), hbm, hbm, hbm],
        out_shape=[act, act, act, jax.ShapeDtypeStruct((N_GROUPS, tokens, d), _MXU_DTYPE),
                   jax.ShapeDtypeStruct((1, LANES), F32), jax.ShapeDtypeStruct((1, d), F32),
                   jax.ShapeDtypeStruct((1, 2 * d), F32), mat, mat, mat],
        scratch_shapes=[pltpu.VMEM((d, d), _MXU_DTYPE)] * 3 + [pltpu.VMEM((d, d), F32)] * 3,
        compiler_params=_params(("arbitrary",)),
    )(ya, yb, z, x2d, tgt2d, b_merge, final_g, pa, pb, wo)


def _inproj_dw(h, dz):
    tokens, d = h.shape
    tm = min(512, tokens)

    def body(h_ref, dz_ref, dw_ref):
        part = _mm_tn(h_ref[...], dz_ref[...])

        @pl.when(pl.program_id(1) == 0)
        def _():
            dw_ref[...] = part

        @pl.when(pl.program_id(1) != 0)
        def _():
            dw_ref[...] += part

    def out_index(s, i):
        g = _group_of_slot(s)
        return (g // 2, 0, g % 2)

    return pl.pallas_call(
        body, name="inproj_dw",
        grid=(N_GROUPS, tokens // tm),
        in_specs=[pl.BlockSpec((tm, d), lambda s, i: (i, 0)),
                  pl.BlockSpec((None, tm, D_MODEL), lambda s, i: (s, i, 0))],
        out_specs=pl.BlockSpec((None, d, D_MODEL), out_index),
        out_shape=jax.ShapeDtypeStruct((N_SHARDS, d, 2 * D_MODEL), F32),
        compiler_params=_params(("parallel", "arbitrary")),
    )(h, dz)


def _inproj_dx(dz, w_all, x2d, dx2, norm_g):
    tokens, d = x2d.shape
    tm = min(512, tokens)

    def body(dz_ref, w_ref, x_ref, dx2_ref, g_ref, dx_ref, dg_ref, acc):
        s = pl.program_id(1)
        part = _mm_nt(dz_ref[...], w_ref[...])

        @pl.when(s == 0)
        def _():
            acc[...] = part

        @pl.when(s != 0)
        def _():
            acc[...] += part

        @pl.when((pl.program_id(0) == 0) & (s == 0))
        def _():
            dg_ref[...] = jnp.zeros_like(dg_ref)

        @pl.when(s == N_GROUPS - 1)
        def _():
            x = x_ref[...]
            r = lax.rsqrt(jnp.mean(x * x, axis=-1, keepdims=True) + EPS)
            xn = x * r
            dh = acc[...]
            dg_ref[...] += jnp.sum(dh * xn, axis=0, keepdims=True)
            dxn = dh * g_ref[...]
            dx_ref[...] = r * (dxn - xn * jnp.mean(dxn * xn, axis=-1, keepdims=True)) + dx2_ref[...]

    def w_index(i, s):
        g = _group_of_slot(s)
        return (g // 2, 0, g % 2)

    tile = pl.BlockSpec((tm, d), lambda i, s: (i, 0))
    return pl.pallas_call(
        body, name="inproj_dx",
        grid=(tokens // tm, N_GROUPS),
        in_specs=[pl.BlockSpec((None, tm, D_MODEL), lambda i, s: (s, i, 0)),
                  pl.BlockSpec((None, d, D_MODEL), w_index), tile, tile,
                  pl.BlockSpec((1, d), lambda i, s: (0, 0))],
        out_specs=[tile, pl.BlockSpec((1, d), lambda i, s: (0, 0))],
        out_shape=[jax.ShapeDtypeStruct((tokens, d), F32), jax.ShapeDtypeStruct((1, d), F32)],
        scratch_shapes=[pltpu.VMEM((tm, d), F32)],
        compiler_params=_params(("arbitrary", "arbitrary")),
    )(dz, w_all, x2d, dx2, norm_g)


def _row_tile(rows, cols, itemsize=4, budget=2 * 1024 * 1024):
    tr = rows
    while tr * cols * itemsize > budget and tr % 16 == 0:
        tr //= 2
    return tr


def _cast_bf16(a, name):
    rows, cols = a.shape
    tr = _row_tile(rows, cols)

    def body(a_ref, o_ref):
        o_ref[...] = a_ref[...].astype(_MXU_DTYPE)

    spec = pl.BlockSpec((tr, cols), lambda i: (i, 0))
    return pl.pallas_call(body, name=name, grid=(rows // tr,), in_specs=[spec], out_specs=spec,
                          out_shape=jax.ShapeDtypeStruct(a.shape, _MXU_DTYPE),
                          compiler_params=_params(("parallel",)))(a)


def _sum_slots(stack, name):
    n, rows, cols = stack.shape
    tr = _row_tile(rows, cols * n)

    def body(s_ref, o_ref):
        total = s_ref[0]
        for k in range(1, n):
            total = total + s_ref[k]
        o_ref[...] = total

    return pl.pallas_call(body, name=name, grid=(rows // tr,),
                          in_specs=[pl.BlockSpec((n, tr, cols), lambda i: (0, i, 0))],
                          out_specs=pl.BlockSpec((tr, cols), lambda i: (i, 0)),
                          out_shape=jax.ShapeDtypeStruct((rows, cols), F32),
                          compiler_params=_params(("parallel",)))(stack)


def _add_half(full, landed, core, name):
    n, rows, cols = full.shape
    half = rows // 2
    tr = _row_tile(half, cols)
    nb = half // tr

    def body(core_ref, a_ref, b_ref, o_ref):
        del core_ref
        o_ref[...] = a_ref[...] + b_ref[...]

    grid_spec = pltpu.PrefetchScalarGridSpec(
        num_scalar_prefetch=1, grid=(n, nb),
        in_specs=[pl.BlockSpec((None, tr, cols), lambda j, i, core_ref: (j, core_ref[0] * nb + i, 0)),
                  pl.BlockSpec((None, tr, cols), lambda j, i, core_ref: (j, i, 0))],
        out_specs=pl.BlockSpec((None, tr, cols), lambda j, i, core_ref: (j, i, 0)))
    return pl.pallas_call(body, name=name, grid_spec=grid_spec,
                          out_shape=jax.ShapeDtypeStruct((n, half, cols), F32),
                          compiler_params=_params(("parallel", "parallel")))(core, full, landed)


def _adamw(w, g, m, v, name):
    rows, cols = w.shape
    tr = _row_tile(rows, cols, budget=1024 * 1024)
    c1 = 1.0 - ADAM_B1 ** ADAM_STEP
    c2 = 1.0 - ADAM_B2 ** ADAM_STEP

    def body(w_ref, g_ref, m_ref, v_ref, d_ref, nm_ref, nv_ref):
        grad = g_ref[...]
        nm = ADAM_B1 * m_ref[...] + (1.0 - ADAM_B1) * grad
        nv = ADAM_B2 * v_ref[...] + (1.0 - ADAM_B2) * (grad * grad)
        nm_ref[...] = nm
        nv_ref[...] = nv
        d_ref[...] = (-ADAM_LR) * ((nm / c1) / (jnp.sqrt(nv / c2) + ADAM_EPS) + ADAM_WD * w_ref[...])

    spec = pl.BlockSpec((tr, cols), lambda i: (i, 0))
    shape = jax.ShapeDtypeStruct((rows, cols), F32)
    return pl.pallas_call(body, name=name, grid=(rows // tr,), in_specs=[spec] * 4, out_specs=[spec] * 3,
                          out_shape=[shape] * 3, compiler_params=_params(("parallel",)))(w, g, m, v)


def _local_step(x, loss_target, w_all, pa, pb, wo, conv_w, b_merge, conv_b, rg_wx, rg_bx, rg_wa, rg_ba,
                rg_lambda, hg_lb_logits, hg_norm_g, norm_g, final_norm_g):
    batch, seq, d = x.shape
    x2d = x.reshape(batch * seq, d)
    tgt2d = loss_target.reshape(batch * seq, d)
    z, h = _inproj_fwd(x2d, norm_g, w_all)
    lru = (conv_w, conv_b, rg_wx, rg_bx, rg_wa, rg_ba, rg_lambda)
    ya, hl = _branch_a_fwd(z, *lru, batch, seq)
    yb, states = _branch_b_fwd(z, hg_lb_logits, hg_norm_g, batch, seq)
    dya, dyb, dx2, dz, loss, d_final_g, d_b_merge, d_pa, d_pb, d_wo = _merge_tail(
        ya, yb, z, x2d, tgt2d, b_merge, final_norm_g, pa, pb, wo)
    dz, d_lb_logits, d_hg_g = _branch_b_bwd(z, states, dyb, dz, hg_lb_logits, hg_norm_g, batch, seq)
    dz, d_conv_w, d_conv_b, d_wx, d_bx, d_wa, d_ba, d_lam = _branch_a_bwd(z, hl, dya, dz, *lru, batch, seq)
    d_w_in = _inproj_dw(h, dz)
    grad_x, d_norm_g = _inproj_dx(dz, w_all, x2d, dx2, norm_g)
    big = (d_w_in, d_pa, d_pb, d_wo)
    small = dict(b_merge=d_b_merge, conv_w=d_conv_w, conv_b=d_conv_b, rg_wx=d_wx, rg_bx=d_bx, rg_wa=d_wa,
                 rg_ba=d_ba, rg_lambda=d_lam, hg_lb_logits=d_lb_logits, hg_norm_g=d_hg_g, norm_g=d_norm_g,
                 final_norm_g=d_final_g)
    return loss[0, 0], grad_x.reshape(batch, seq, d), big, small


_SMALL_ORDER = ("b_merge", "conv_w", "conv_b", "rg_wx", "rg_bx", "rg_wa", "rg_ba", "rg_lambda", "hg_lb_logits",
                "hg_norm_g", "norm_g", "final_norm_g")
N_DEV = 8
PIECE_ROWS = 272


def _pack_small(tree):
    flat = jnp.concatenate([tree[k].reshape(-1) for k in _SMALL_ORDER])
    flat = jnp.pad(flat, (0, N_DEV * PIECE_ROWS * LANES - flat.shape[0]))
    return flat.reshape(N_DEV * PIECE_ROWS, LANES)


def _unpack_small(packed, like):
    flat = packed.reshape(-1)
    out, pos = {}, 0
    for k in _SMALL_ORDER:
        n = like[k].size
        out[k] = flat[pos:pos + n].reshape(like[k].shape)
        pos += n
    return out


def _mesh_position():
    x, y, c = lax.axis_index("x"), lax.axis_index("y"), lax.axis_index("c")
    other_chips = [(1 - x, y), (x, 1 - y), (1 - x, 1 - y)]
    return x, y, c, other_chips


def _other_devices(x, y, c):
    flips = [(fx, fy, fc) for fx in (0, 1) for fy in (0, 1) for fc in (0, 1) if (fx, fy, fc) != (0, 0, 0)]
    return [(jnp.where(fx, 1 - x, x), jnp.where(fy, 1 - y, y), jnp.where(fc, 1 - c, c)) for fx, fy, fc in flips]


def _remote(src, dst, send_sems, recv_sems, k, device):
    return pltpu.make_async_remote_copy(src_ref=src, dst_ref=dst, send_sem=send_sems.at[k], recv_sem=recv_sems.at[k],
                                        device_id=device, device_id_type=MESH)


def _gather_weights(shards, conv_w):
    n_big = len(shards)
    n_sem = 6 * n_big + 3

    def body(*refs):
        srcs, cw_src = refs[:n_big], refs[n_big]
        outs, cw_out = refs[n_big + 1:2 * n_big + 1], refs[2 * n_big + 1]
        send_sems, recv_sems, local_sems = refs[2 * n_big + 2:]
        x, y, c, chips = _mesh_position()
        me, sibling = 2 * x + y, (x, y, 1 - c)

        def half(ref, which):
            hs = ref.shape[0] // 2
            return pl.ds(which * hs, hs)

        local = [pltpu.make_async_copy(srcs[a], outs[a].at[me], local_sems.at[a]) for a in range(n_big)]
        local.append(pltpu.make_async_copy(cw_src, cw_out.at[me], local_sems.at[n_big]))
        for cp in local:
            cp.start()
        sends = []
        for a in range(n_big):
            for j, (cx, cy) in enumerate(chips):
                sends.append(_remote(srcs[a].at[half(srcs[a], c), :], outs[a].at[me, half(srcs[a], c), :],
                                     send_sems, recv_sems, 6 * a + j, (cx, cy, c)))
        for j, (cx, cy) in enumerate(chips):
            sends.append(_remote(cw_src, cw_out.at[me], send_sems, recv_sems, 6 * n_big + j, (cx, cy, c)))
        for cp in sends:
            cp.start()
        for j, (cx, cy) in enumerate(chips):
            theirs = 2 * cx + cy
            for a in range(n_big):
                landed = outs[a].at[theirs, half(srcs[a], c), :]
                _remote(landed, landed, send_sems, recv_sems, 6 * a + j, (cx, cy, c)).wait_recv()
                passed = _remote(landed, landed, send_sems, recv_sems, 6 * a + 3 + j, sibling)
                passed.start()
                sends.append(passed)
        for j, (cx, cy) in enumerate(chips):
            theirs = 2 * cx + cy
            _remote(cw_out.at[theirs], cw_out.at[theirs], send_sems, recv_sems, 6 * n_big + j, (cx, cy, c)).wait_recv()
            for a in range(n_big):
                landed = outs[a].at[theirs, half(srcs[a], 1 - c), :]
                _remote(landed, landed, send_sems, recv_sems, 6 * a + 3 + j, sibling).wait_recv()
        for cp in sends:
            cp.wait_send()
        for cp in local:
            cp.wait()

    hbm = pl.BlockSpec(memory_space=pl.ANY)
    out_shape = [jax.ShapeDtypeStruct((N_SHARDS,) + s.shape, s.dtype) for s in shards]
    out_shape.append(jax.ShapeDtypeStruct((N_SHARDS,) + conv_w.shape, conv_w.dtype))
    return pl.pallas_call(
        body, name="gather_weights",
        in_specs=[hbm] * (n_big + 1), out_specs=[hbm] * (n_big + 1), out_shape=out_shape,
        scratch_shapes=[pltpu.SemaphoreType.DMA((n_sem,)), pltpu.SemaphoreType.DMA((n_sem,)),
                        pltpu.SemaphoreType.DMA((n_big + 1,))],
    )(*shards, conv_w)


def _exchange_halves(bigs, small):
    n_big = len(bigs)
    n_sem = n_big + N_DEV - 1

    def body(*refs):
        srcs, small_src = refs[:n_big], refs[n_big]
        outs, small_out = refs[n_big + 1:2 * n_big + 1], refs[2 * n_big + 1]
        send_sems, recv_sems, local_sem = refs[2 * n_big + 2:]
        x, y, c, _ = _mesh_position()
        me, sibling = 4 * x + 2 * y + c, (x, y, 1 - c)
        mine = pltpu.make_async_copy(small_src.at[pl.ds(me * PIECE_ROWS, PIECE_ROWS), :], small_out.at[me], local_sem)
        mine.start()
        copies = []
        for a in range(n_big):
            hs = srcs[a].shape[1] // 2
            copies.append(_remote(srcs[a].at[:, pl.ds((1 - c) * hs, hs), :], outs[a], send_sems, recv_sems, a, sibling))
        for k, (px, py, pc) in enumerate(_other_devices(x, y, c)):
            piece = small_src.at[pl.ds((4 * px + 2 * py + pc) * PIECE_ROWS, PIECE_ROWS), :]
            copies.append(_remote(piece, small_out.at[me], send_sems, recv_sems, n_big + k, (px, py, pc)))
        for cp in copies:
            cp.start()
        for cp in copies:
            cp.wait()
        mine.wait()

    hbm = pl.BlockSpec(memory_space=pl.ANY)
    out_shape = [jax.ShapeDtypeStruct((g.shape[0], g.shape[1] // 2, g.shape[2]), F32) for g in bigs]
    out_shape.append(jax.ShapeDtypeStruct((N_DEV, PIECE_ROWS, LANES), F32))
    return pl.pallas_call(
        body, name="exchange_halves",
        in_specs=[hbm] * (n_big + 1), out_specs=[hbm] * (n_big + 1), out_shape=out_shape,
        scratch_shapes=[pltpu.SemaphoreType.DMA((n_sem,)), pltpu.SemaphoreType.DMA((n_sem,)), pltpu.SemaphoreType.DMA],
    )(*bigs, small)


def _scatter_shards(bigs, small_piece):
    n_big = len(bigs)
    n_sem = 3 * n_big + N_DEV - 1

    def body(*refs):
        srcs, small_src = refs[:n_big], refs[n_big]
        outs, small_out = refs[n_big + 1:2 * n_big + 1], refs[2 * n_big + 1]
        send_sems, recv_sems, local_sems = refs[2 * n_big + 2:]
        x, y, c, chips = _mesh_position()
        chip, me = 2 * x + y, 4 * x + 2 * y + c
        local = [pltpu.make_async_copy(srcs[a].at[chip], outs[a].at[chip], local_sems.at[a]) for a in range(n_big)]
        local.append(pltpu.make_async_copy(small_src, small_out.at[me], local_sems.at[n_big]))
        for cp in local:
            cp.start()
        copies = []
        for a in range(n_big):
            for j, (cx, cy) in enumerate(chips):
                copies.append(_remote(srcs[a].at[2 * cx + cy], outs[a].at[chip], send_sems, recv_sems, 3 * a + j, (cx, cy, c)))
        for k, peer in enumerate(_other_devices(x, y, c)):
            copies.append(_remote(small_src, small_out.at[me], send_sems, recv_sems, 3 * n_big + k, peer))
        for cp in copies:
            cp.start()
        for cp in copies:
            cp.wait()
        for cp in local:
            cp.wait()

    hbm = pl.BlockSpec(memory_space=pl.ANY)
    out_shape = [jax.ShapeDtypeStruct(g.shape, F32) for g in bigs]
    out_shape.append(jax.ShapeDtypeStruct((N_DEV, PIECE_ROWS, LANES), F32))
    return pl.pallas_call(
        body, name="scatter_shards",
        in_specs=[hbm] * (n_big + 1), out_specs=[hbm] * (n_big + 1), out_shape=out_shape,
        scratch_shapes=[pltpu.SemaphoreType.DMA((n_sem,)), pltpu.SemaphoreType.DMA((n_sem,)),
                        pltpu.SemaphoreType.DMA((n_big + 1,))],
    )(*bigs, small_piece)


def _join_halves(halves):
    n_big = len(halves)

    def body(*refs):
        srcs, outs = refs[:n_big], refs[n_big:2 * n_big]
        send_sems, recv_sems, local_sems = refs[2 * n_big:]
        x, y, c, _ = _mesh_position()
        sibling = (x, y, 1 - c)
        local, copies = [], []
        for a in range(n_big):
            hs = srcs[a].shape[0]
            local.append(pltpu.make_async_copy(srcs[a], outs[a].at[pl.ds(c * hs, hs), :], local_sems.at[a]))
            copies.append(_remote(srcs[a], outs[a].at[pl.ds(c * hs, hs), :], send_sems, recv_sems, a, sibling))
        for cp in local + copies:
            cp.start()
        for a in range(n_big):
            hs = srcs[a].shape[0]
            copies[a].wait_send()
            _remote(srcs[a], outs[a].at[pl.ds((1 - c) * hs, hs), :], send_sems, recv_sems, a, sibling).wait_recv()
        for cp in local:
            cp.wait()

    hbm = pl.BlockSpec(memory_space=pl.ANY)
    return pl.pallas_call(
        body, name="join_halves",
        in_specs=[hbm] * n_big, out_specs=[hbm] * n_big,
        out_shape=[jax.ShapeDtypeStruct((2 * h.shape[0], h.shape[1]), F32) for h in halves],
        scratch_shapes=[pltpu.SemaphoreType.DMA((n_big,)), pltpu.SemaphoreType.DMA((n_big,)),
                        pltpu.SemaphoreType.DMA((n_big,))],
    )(*halves)


def _reduce_gradients(bigs, small_tree):
    core = lax.axis_index("c").astype(jnp.int32).reshape(1)
    *landed, small_landed = _exchange_halves(bigs, _pack_small(small_tree))
    chip_sums = [_add_half(g, l, core, f"add_half_{a}") for a, (g, l) in enumerate(zip(bigs, landed))]
    small_piece = _sum_slots(small_landed, "sum_small")
    *by_chip, small_all = _scatter_shards(chip_sums, small_piece)
    halves = [_sum_slots(s, f"sum_chips_{a}") for a, s in enumerate(by_chip)]
    shards = _join_halves(halves)
    return shards, _unpack_small(small_all, small_tree)


def kernel(x, w_in, b_merge, conv_w, conv_b, rg_wx, rg_bx, rg_wa, rg_ba, rg_lambda, hg_lb_logits, hg_norm_g, proj_a, proj_b, w_out, norm_g, final_norm_g, loss_target, m_w_in, m_b_merge, m_conv_w, m_conv_b, m_rg_wx, m_rg_bx, m_rg_wa, m_rg_ba, m_rg_lambda, m_hg_lb_logits, m_hg_norm_g, m_proj_a, m_proj_b, m_w_out, m_norm_g, m_final_norm_g, v_w_in, v_b_merge, v_conv_w, v_conv_b, v_rg_wx, v_rg_bx, v_rg_wa, v_rg_ba, v_rg_lambda, v_hg_lb_logits, v_hg_norm_g, v_proj_a, v_proj_b, v_w_out, v_norm_g, v_final_norm_g):
    d = D_MODEL
    weights = dict(w_in=w_in, b_merge=b_merge, conv_w=conv_w, conv_b=conv_b, rg_wx=rg_wx, rg_bx=rg_bx, rg_wa=rg_wa,
                   rg_ba=rg_ba, rg_lambda=rg_lambda, hg_lb_logits=hg_lb_logits, hg_norm_g=hg_norm_g, proj_a=proj_a,
                   proj_b=proj_b, w_out=w_out, norm_g=norm_g, final_norm_g=final_norm_g)
    m = dict(w_in=m_w_in, b_merge=m_b_merge, conv_w=m_conv_w, conv_b=m_conv_b, rg_wx=m_rg_wx, rg_bx=m_rg_bx,
             rg_wa=m_rg_wa, rg_ba=m_rg_ba, rg_lambda=m_rg_lambda, hg_lb_logits=m_hg_lb_logits, hg_norm_g=m_hg_norm_g,
             proj_a=m_proj_a, proj_b=m_proj_b, w_out=m_w_out, norm_g=m_norm_g, final_norm_g=m_final_norm_g)
    v = dict(w_in=v_w_in, b_merge=v_b_merge, conv_w=v_conv_w, conv_b=v_conv_b, rg_wx=v_rg_wx, rg_bx=v_rg_bx,
             rg_wa=v_rg_wa, rg_ba=v_rg_ba, rg_lambda=v_rg_lambda, hg_lb_logits=v_hg_lb_logits, hg_norm_g=v_hg_norm_g,
             proj_a=v_proj_a, proj_b=v_proj_b, w_out=v_w_out, norm_g=v_norm_g, final_norm_g=v_final_norm_g)
    big_names = ("w_in", "proj_a", "proj_b", "w_out")

    shards = [_cast_bf16(weights[k][0], f"cast_{k}") for k in big_names]
    w_all, pa_all, pb_all, wo_all, cw_all = _gather_weights(shards, conv_w[0])
    conv_w_full = jnp.transpose(cw_all, (1, 0, 2)).reshape(CONV_WIDTH, d)

    loss_part, grad_x, big_grads, small_grads = _local_step(
        x, loss_target, w_all, pa_all.reshape(d, d), pb_all.reshape(d, d), wo_all.reshape(d, d), conv_w_full,
        b_merge, conv_b, rg_wx[0], rg_bx.reshape(1, d), rg_wa[0], rg_ba.reshape(1, d), rg_lambda, hg_lb_logits,
        hg_norm_g, norm_g, final_norm_g.reshape(1, d))
    loss = lax.psum(loss_part, ("x", "y", "c"))

    big_grads = [big_grads[0]] + [g.reshape(N_SHARDS, d // N_SHARDS, d) for g in big_grads[1:]]
    big_shards, small_red = _reduce_gradients(big_grads, small_grads)

    grads, delta, new_m, new_v = {}, {}, {}, {}
    for k, g in zip(big_names, big_shards):
        grads[k] = g.reshape(weights[k].shape)
        dl, nm, nv = _adamw(weights[k][0], g, m[k][0], v[k][0], f"adamw_{k}")
        delta[k], new_m[k], new_v[k] = (t.reshape(weights[k].shape) for t in (dl, nm, nv))
    chip = 2 * lax.axis_index("x") + lax.axis_index("y")
    cols = d // N_SHARDS
    g_conv = lax.dynamic_slice(small_red["conv_w"], (0, chip * cols), (CONV_WIDTH, cols))
    grads["conv_w"] = g_conv.reshape(conv_w.shape)
    dl, nm, nv = _adamw(conv_w[0], g_conv, m_conv_w[0], v_conv_w[0], "adamw_conv_w")
    delta["conv_w"], new_m["conv_w"], new_v["conv_w"] = (t.reshape(conv_w.shape) for t in (dl, nm, nv))
    rest = [k for k in _SMALL_ORDER if k != "conv_w"]
    like = {k: (weights[k] if k != "conv_w" else jnp.zeros((CONV_WIDTH, d), F32)) for k in _SMALL_ORDER}
    packs = [_pack_small({k: (t[k] if k != "conv_w" else like[k]) for k in _SMALL_ORDER}) for t in (weights, m, v)]
    g_pack = _pack_small({k: small_red[k].reshape(like[k].shape) for k in _SMALL_ORDER})
    outs = [_unpack_small(p, like) for p in _adamw(packs[0], g_pack, packs[1], packs[2], "adamw_small")]
    for k in rest:
        grads[k] = small_red[k].reshape(weights[k].shape)
        delta[k], new_m[k], new_v[k] = outs[0][k], outs[1][k], outs[2][k]

    order = ("w_in", "b_merge", "conv_w", "conv_b", "rg_wx", "rg_bx", "rg_wa", "rg_ba", "rg_lambda", "hg_lb_logits",
             "hg_norm_g", "proj_a", "proj_b", "w_out", "norm_g", "final_norm_g")
    return (loss, grad_x, *[grads[k] for k in order], *[delta[k] for k in order], *[new_m[k] for k in order],
            *[new_v[k] for k in order])
```

```python
import functools

import jax
import jax.numpy as jnp
from jax import lax
from jax.experimental import pallas as pl
from jax.experimental.pallas import tpu as pltpu

F32 = jnp.float32
_MXU_DTYPE = jnp.bfloat16

D_MODEL = 1024
LANES = 128
SUBLANES = 8
N_BLK = D_MODEL // LANES
N_GROUPS = 8
N_SHARDS = 4
CONV_WIDTH = 4
LRU_C = 8.0
CHUNK = 64
CHUNKS_IN_FLIGHT = 4
HG_SCALE = float(LANES) ** -0.5
EPS = 1e-6
ADAM_LR, ADAM_B1, ADAM_B2, ADAM_EPS, ADAM_WD, ADAM_STEP = 0.001, 0.9, 0.999, 1e-08, 0.01, 10
VMEM_LIMIT = 56 * 1024 * 1024
MESH = pl.DeviceIdType.MESH

_SLOT_TO_GROUP = (2, 3, 4, 5, 0, 1, 6, 7)


def _group_of_slot(s):
    return jnp.where(s < 4, s + 2, jnp.where(s < 6, s - 4, s))


def _mm(a, b):
    return lax.dot_general(a.astype(_MXU_DTYPE), b.astype(_MXU_DTYPE), (((1,), (0,)), ((), ())),
                           preferred_element_type=F32)


def _mm_nt(a, b):
    return lax.dot_general(a.astype(_MXU_DTYPE), b.astype(_MXU_DTYPE), (((1,), (1,)), ((), ())),
                           preferred_element_type=F32)


def _mm_tn(a, b):
    return lax.dot_general(a.astype(_MXU_DTYPE), b.astype(_MXU_DTYPE), (((0,), (0,)), ((), ())),
                           preferred_element_type=F32)


def _sigmoid(x):
    return 1.0 / (1.0 + jnp.exp(-x))


def _log1p_pos(y):
    series = y * (1.0 - y * (0.5 - y * (1.0 / 3.0 - y * 0.25)))
    return jnp.where(y < 0.01, series, jnp.log(1.0 + y))


def _expm1_neg(y):
    series = y * (1.0 + y * 0.5 * (1.0 + y * (1.0 / 3.0) * (1.0 + y * 0.25 * (1.0 + y * 0.2))))
    return jnp.where(y > -0.02, series, jnp.exp(y) - 1.0)


def _softplus(x):
    return jnp.maximum(x, 0.0) + _log1p_pos(jnp.exp(-jnp.abs(x)))


def _shift_down(x, n):
    rows = lax.broadcasted_iota(jnp.int32, x.shape, 0)
    return jnp.where(rows >= n, pltpu.roll(x, n, 0), 0.0)


def _shift_up(x, n):
    size = x.shape[0]
    rows = lax.broadcasted_iota(jnp.int32, x.shape, 0)
    return jnp.where(rows < size - n, pltpu.roll(x, size - n, 0), 0.0)


def _params(dims, vmem=VMEM_LIMIT):
    return pltpu.CompilerParams(dimension_semantics=dims, vmem_limit_bytes=vmem)


def _inproj_fwd(x2d, norm_g, w_all):
    tokens, d = x2d.shape
    tm = min(512, tokens)

    def body(x_ref, g_ref, w_ref, z_ref, h_ref, h_scr):
        @pl.when(pl.program_id(1) == 0)
        def _():
            x = x_ref[...]
            r = lax.rsqrt(jnp.mean(x * x, axis=-1, keepdims=True) + EPS)
            h = ((x * r) * g_ref[...]).astype(_MXU_DTYPE)
            h_scr[...] = h
            h_ref[...] = h

        z_ref[...] = _mm(h_scr[...], w_ref[...])

    def w_index(i, s):
        g = _group_of_slot(s)
        return (g // 2, 0, g % 2)

    return pl.pallas_call(
        body, name="inproj_fwd",
        grid=(tokens // tm, N_GROUPS),
        in_specs=[pl.BlockSpec((tm, d), lambda i, s: (i, 0)),
                  pl.BlockSpec((1, d), lambda i, s: (0, 0)),
                  pl.BlockSpec((None, d, D_MODEL), w_index)],
        out_specs=[pl.BlockSpec((None, tm, D_MODEL), lambda i, s: (s, i, 0)),
                   pl.BlockSpec((tm, d), lambda i, s: (i, 0))],
        out_shape=[jax.ShapeDtypeStruct((N_GROUPS, tokens, D_MODEL), F32),
                   jax.ShapeDtypeStruct((tokens, d), _MXU_DTYPE)],
        scratch_shapes=[pltpu.VMEM((tm, d), _MXU_DTYPE)],
        compiler_params=_params(("parallel", "arbitrary")),
    )(x2d, norm_g, w_all)


def _lru_gates(xa, cw_ref, cb_ref, wx_ref, bx_ref, wa_ref, ba_ref, lam_ref):
    xc = (cb_ref[...] + cw_ref[3:4, :] * xa + cw_ref[2:3, :] * _shift_down(xa, 1)
          + cw_ref[1:2, :] * _shift_down(xa, 2) + cw_ref[0:1, :] * _shift_down(xa, 3))
    gi = _sigmoid(_mm(xc, wx_ref[...]) + bx_ref[...])
    gr = _sigmoid(_mm(xc, wa_ref[...]) + ba_ref[...])
    sp = _softplus(-lam_ref[...])
    log_a = (-LRU_C) * gr * sp
    a = jnp.exp(log_a)
    mult = jnp.sqrt(-_expm1_neg(2.0 * log_a))
    return xc, gi, gr, sp, a, mult


def _tile_rows():
    return lax.broadcasted_iota(jnp.int32, (SUBLANES, LANES), 0)


def _scan_forward(a_scr, u_scr, h_scr, seq):
    rows = _tile_rows()

    def tile(j, carry):
        sl = pl.ds(pl.multiple_of(j * SUBLANES, SUBLANES), SUBLANES)
        a = a_scr[sl, :]
        u = u_scr[sl, :]
        for d in (1, 2, 4):
            keep = rows >= d
            a_sh = jnp.where(keep, pltpu.roll(a, d, 0), 1.0)
            u_sh = jnp.where(keep, pltpu.roll(u, d, 0), 0.0)
            u = a * u_sh + u
            a = a * a_sh
        h = u + a * carry
        h_scr[sl, :] = h
        return jnp.broadcast_to(h[SUBLANES - 1:SUBLANES, :], (SUBLANES, LANES))

    lax.fori_loop(0, seq // SUBLANES, tile, jnp.zeros((SUBLANES, LANES), F32))


def _scan_backward(c_scr, d_scr, g_scr, seq):
    rows = _tile_rows()
    n_tiles = seq // SUBLANES

    def tile(jj, carry):
        j = n_tiles - 1 - jj
        sl = pl.ds(pl.multiple_of(j * SUBLANES, SUBLANES), SUBLANES)
        c = c_scr[sl, :]
        g = d_scr[sl, :]
        for d in (1, 2, 4):
            keep = rows < SUBLANES - d
            c_sh = jnp.where(keep, pltpu.roll(c, SUBLANES - d, 0), 1.0)
            g_sh = jnp.where(keep, pltpu.roll(g, SUBLANES - d, 0), 0.0)
            g = c * g_sh + g
            c = c * c_sh
        g = g + c * carry
        g_scr[sl, :] = g
        return jnp.broadcast_to(g[0:1, :], (SUBLANES, LANES))

    lax.fori_loop(0, n_tiles, tile, jnp.zeros((SUBLANES, LANES), F32))


def _lru_param_specs(cb_axis):
    def pick(*ids):
        return ids[cb_axis]

    vec = pl.BlockSpec((1, LANES), lambda *ids: (0, pick(*ids)))
    mat = pl.BlockSpec((None, LANES, LANES), lambda *ids: (pick(*ids), 0, 0))
    return [pl.BlockSpec((CONV_WIDTH, LANES), lambda *ids: (0, pick(*ids))), vec, mat, vec, mat, vec, vec]


def _branch_a_fwd(z, conv_w, conv_b, wx, bx, wa, ba, lam, batch, seq):
    tokens = batch * seq

    def body(z_ref, cw_ref, cb_ref, wx_ref, bx_ref, wa_ref, ba_ref, lam_ref, ya_ref, hl_ref, a_scr, u_scr):
        xa = z_ref[0]
        ga = z_ref[1]
        xc, gi, _, _, a, mult = _lru_gates(xa, cw_ref, cb_ref, wx_ref, bx_ref, wa_ref, ba_ref, lam_ref)
        a_scr[...] = a
        u_scr[...] = mult * gi * xc
        _scan_forward(a_scr, u_scr, hl_ref, seq)
        ya_ref[...] = (hl_ref[...] * (ga * _sigmoid(ga))).astype(_MXU_DTYPE)

    blk = pl.BlockSpec((seq, LANES), lambda b, c: (b, c))
    return pl.pallas_call(
        body, name="branch_a_fwd",
        grid=(batch, N_BLK),
        in_specs=[pl.BlockSpec((2, seq, LANES), lambda b, c: (2, b, c))] + _lru_param_specs(1),
        out_specs=[blk, blk],
        out_shape=[jax.ShapeDtypeStruct((tokens, D_MODEL), _MXU_DTYPE), jax.ShapeDtypeStruct((tokens, D_MODEL), F32)],
        scratch_shapes=[pltpu.VMEM((seq, LANES), F32), pltpu.VMEM((seq, LANES), F32)],
        compiler_params=_params(("parallel", "parallel")),
    )(z, conv_w, conv_b, wx, bx, wa, ba, lam)


def _branch_a_bwd(z, hl, dya, dz, conv_w, conv_b, wx, bx, wa, ba, lam, batch, seq):
    def body(z_ref, hl_ref, dya_ref, dz_in_ref, cw_ref, cb_ref, wx_ref, bx_ref, wa_ref, ba_ref, lam_ref,
             dz_ref, dcw_ref, dcb_ref, dwx_ref, dbx_ref, dwa_ref, dba_ref, dlam_ref, c_scr, d_scr, g_scr):
        del dz_in_ref
        xa = z_ref[0]
        ga = z_ref[1]
        hl = hl_ref[...]
        dya = dya_ref[...]
        xc, gi, gr, sp, a, mult = _lru_gates(xa, cw_ref, cb_ref, wx_ref, bx_ref, wa_ref, ba_ref, lam_ref)
        sga = _sigmoid(ga)
        dz_ref[1] = (dya * hl * (sga * (1.0 + ga * (1.0 - sga)))).astype(_MXU_DTYPE)
        c_scr[...] = _shift_up(a, 1)
        d_scr[...] = dya * (ga * sga)
        _scan_backward(c_scr, d_scr, g_scr, seq)
        g = g_scr[...]
        da = g * _shift_down(hl, 1)
        dmult = g * gi * xc
        dgi = g * mult * xc
        dxc = g * mult * gi
        dlog_a = da * a - dmult * (a * a) / mult
        dgr = dlog_a * (-LRU_C) * sp
        dsp = jnp.sum(dlog_a * gr, axis=0, keepdims=True) * (-LRU_C)
        dlam = -dsp * _sigmoid(-lam_ref[...])
        dpi = dgi * gi * (1.0 - gi)
        dpr = dgr * gr * (1.0 - gr)
        dxc = dxc + _mm_nt(dpi, wx_ref[...]) + _mm_nt(dpr, wa_ref[...])
        dwx = _mm_tn(xc, dpi)
        dwa = _mm_tn(xc, dpr)
        dbx = jnp.sum(dpi, axis=0, keepdims=True)
        dba = jnp.sum(dpr, axis=0, keepdims=True)
        dxa = (cw_ref[3:4, :] * dxc + cw_ref[2:3, :] * _shift_up(dxc, 1) + cw_ref[1:2, :] * _shift_up(dxc, 2)
               + cw_ref[0:1, :] * _shift_up(dxc, 3))
        dz_ref[0] = dxa.astype(_MXU_DTYPE)
        dcb = jnp.sum(dxc, axis=0, keepdims=True)
        dcw = [jnp.sum(dxc * _shift_down(xa, CONV_WIDTH - 1 - k), axis=0, keepdims=True) if k < CONV_WIDTH - 1
               else jnp.sum(dxc * xa, axis=0, keepdims=True) for k in range(CONV_WIDTH)]

        @pl.when(pl.program_id(1) == 0)
        def _():
            for k in range(CONV_WIDTH):
                dcw_ref[k:k + 1, :] = dcw[k]
            dcb_ref[...] = dcb
            dwx_ref[...] = dwx
            dbx_ref[...] = dbx
            dwa_ref[...] = dwa
            dba_ref[...] = dba
            dlam_ref[...] = dlam

        @pl.when(pl.program_id(1) != 0)
        def _():
            for k in range(CONV_WIDTH):
                dcw_ref[k:k + 1, :] += dcw[k]
            dcb_ref[...] += dcb
            dwx_ref[...] += dwx
            dbx_ref[...] += dbx
            dwa_ref[...] += dwa
            dba_ref[...] += dba
            dlam_ref[...] += dlam

    tokens = batch * seq
    blk = pl.BlockSpec((seq, LANES), lambda c, b: (b, c))
    vec = pl.BlockSpec((1, LANES), lambda c, b: (0, c))
    mat = pl.BlockSpec((None, LANES, LANES), lambda c, b: (c, 0, 0))
    vec_shape = jax.ShapeDtypeStruct((1, D_MODEL), F32)
    mat_shape = jax.ShapeDtypeStruct((N_BLK, LANES, LANES), F32)
    return pl.pallas_call(
        body, name="branch_a_bwd",
        grid=(N_BLK, batch),
        in_specs=[pl.BlockSpec((2, seq, LANES), lambda c, b: (2, b, c)), blk, blk,
                  pl.BlockSpec(memory_space=pl.ANY)] + _lru_param_specs(0),
        out_specs=[pl.BlockSpec((2, seq, LANES), lambda c, b: (2, b, c)),
                   pl.BlockSpec((CONV_WIDTH, LANES), lambda c, b: (0, c)), vec, mat, vec, mat, vec, vec],
        out_shape=[jax.ShapeDtypeStruct((N_GROUPS, tokens, D_MODEL), _MXU_DTYPE),
                   jax.ShapeDtypeStruct((CONV_WIDTH, D_MODEL), F32), vec_shape, mat_shape, vec_shape, mat_shape,
                   vec_shape, vec_shape],
        scratch_shapes=[pltpu.VMEM((seq, LANES), F32)] * 3,
        input_output_aliases={3: 0},
        compiler_params=_params(("parallel", "arbitrary")),
    )(z, hl, dya, dz, conv_w, conv_b, wx, bx, wa, ba, lam)


def _chunk_masks(transposed=False):
    r = lax.broadcasted_iota(jnp.int32, (CHUNK, CHUNK), 0)
    c = lax.broadcasted_iota(jnp.int32, (CHUNK, CHUNK), 1)
    return r <= c if transposed else r >= c


def _row_blocks(seq, fn):
    block = min(256, seq)

    def trip(i, carry):
        fn(pl.ds(pl.multiple_of(i * block, block), block))
        return carry

    lax.fori_loop(0, seq // block, trip, 0)


def _hgrn_prepare(z_ref, lb_ref, f_scr, logf_scr, qh_scr, seq):
    lb = _sigmoid(lb_ref[0:1, :] - lb_ref[1:2, :])

    def block(rows):
        q = z_ref[0, rows, :]
        f = lb + (1.0 - lb) * _sigmoid(z_ref[1, rows, :])
        f_scr[rows, :] = f
        logf_scr[rows, :] = jnp.log(f)
        qh_scr[rows, :] = q * _sigmoid(q)

    _row_blocks(seq, block)
    return lb


def _cumsum_rows(x, reverse=False):
    shift = _shift_up if reverse else _shift_down
    d = 1
    while d < x.shape[0]:
        x = x + shift(x, d)
        d *= 2
    return x


def _lane_mean(x):
    return jnp.mean(x, axis=-1, keepdims=True)


def _token_contractions(lhs_scr, rhs_scr, out_ref, seq):
    rows_id = lax.broadcasted_iota(jnp.int32, (LANES, LANES), 0)

    def transposed(p):
        rows = pl.ds(pl.multiple_of(p * LANES, LANES), LANES)
        return jnp.transpose(lhs_scr[rows, :]).astype(_MXU_DTYPE), rhs_scr[rows, :]

    def contract(p, s):
        lhs_t, rhs = s
        return (_mm(lhs_t, jnp.where(rows_id < CHUNK, rhs, 0.0)), _mm(lhs_t, jnp.where(rows_id >= CHUNK, rhs, 0.0)))

    def store(p, out):
        out_ref[2 * p] = out[0]
        out_ref[2 * p + 1] = out[1]

    _independent_trips(seq // LANES, [transposed, contract], store)


def _chunk_rows(c):
    return pl.ds(pl.multiple_of(c * CHUNK, CHUNK), CHUNK)


def _chunk_terms(c, z_ref, f_scr, qh_scr, b_scr):
    rows = _chunk_rows(c)
    b = b_scr[rows, :]
    b_mid = b_scr[pl.ds(c * CHUNK + CHUNK // 2, 1), :]
    b_last = b_scr[pl.ds(c * CHUNK + CHUNK - 1, 1), :]
    qh = qh_scr[rows, :]
    k = 1.0 - f_scr[rows, :]
    v = z_ref[2, rows, :]
    e_q = jnp.exp(b - b_mid) * HG_SCALE
    e_k = jnp.exp(b_mid - b)
    e_qi = jnp.exp(b) * HG_SCALE
    e_ks = jnp.exp(b_last - b)
    decay = jnp.exp(b_last)
    return rows, qh, k, v, e_q, e_k, e_qi, e_ks, decay


def _independent_trips(n, stages, store, group=CHUNKS_IN_FLIGHT):
    stages = stages if isinstance(stages, (list, tuple)) else [stages]
    group = min(group, n)

    def trip(g, carry):
        ids = [g * group + i for i in range(group)]
        state = [stages[0](c) for c in ids]
        for stage in stages[1:]:
            state = [stage(c, s) for c, s in zip(ids, state)]
        for c, s in zip(ids, state):
            store(c, s)
        return carry

    lax.fori_loop(0, n // group, trip, 0)


def _branch_b_fwd(z, lb_logits, hg_g, batch, seq):
    tokens = batch * seq
    n_chunks = seq // CHUNK

    def body(z_ref, lb_ref, g_ref, yb_ref, st_ref, f_scr, logf_scr, qh_scr, b_scr, o_scr, qi_scr, ks_scr, dec_scr):
        _hgrn_prepare(z_ref, lb_ref, f_scr, logf_scr, qh_scr, seq)
        causal = _chunk_masks()
        gain = g_ref[...]

        def cumulate(c):
            return _cumsum_rows(logf_scr[_chunk_rows(c), :])

        def store_cumulated(c, b):
            b_scr[_chunk_rows(c), :] = b

        def scores(c):
            _, qh, k, v, e_q, e_k, e_qi, e_ks, decay = _chunk_terms(c, z_ref, f_scr, qh_scr, b_scr)
            return _mm_nt(qh * e_q, k * e_k), v, qh * e_qi, k * e_ks, decay

        def within_chunk(c, s):
            att, v, q_int, k_st, decay = s
            return _mm(jnp.where(causal, att, 0.0), v), q_int, k_st, decay

        def store_within_chunk(c, out):
            rows = _chunk_rows(c)
            o_scr[rows, :], qi_scr[rows, :], ks_scr[rows, :], dec_scr[pl.ds(c, 1), :] = out

        def carry_state(c, state_t):
            update = st_ref[c]
            st_ref[c] = state_t
            return state_t * dec_scr[pl.ds(c, 1), :] + update

        def finish(c):
            rows = _chunk_rows(c)
            o = o_scr[rows, :] + _mm_nt(qi_scr[rows, :], st_ref[c])
            r = lax.rsqrt(_lane_mean(o * o) + EPS)
            gb = z_ref[3, rows, :]
            return (((o * r) * gain) * (gb * _sigmoid(gb))).astype(_MXU_DTYPE)

        def store_finished(c, yb):
            yb_ref[_chunk_rows(c), :] = yb

        _independent_trips(n_chunks, cumulate, store_cumulated)
        _independent_trips(n_chunks, [scores, within_chunk], store_within_chunk)
        _token_contractions(z_ref.at[2], ks_scr, st_ref, seq)
        lax.fori_loop(0, n_chunks, carry_state, jnp.zeros((LANES, LANES), F32))
        _independent_trips(n_chunks, finish, store_finished, group=2 * CHUNKS_IN_FLIGHT)

    seq_buf = pltpu.VMEM((seq, LANES), F32)
    return pl.pallas_call(
        body, name="branch_b_fwd",
        grid=(batch, N_BLK),
        in_specs=[pl.BlockSpec((4, seq, LANES), lambda b, h: (0, b, h)),
                  pl.BlockSpec((2, LANES), lambda b, h: (0, h)),
                  pl.BlockSpec((1, LANES), lambda b, h: (0, 0))],
        out_specs=[pl.BlockSpec((seq, LANES), lambda b, h: (b, h)),
                   pl.BlockSpec((None, n_chunks, LANES, LANES), lambda b, h: (b * N_BLK + h, 0, 0, 0))],
        out_shape=[jax.ShapeDtypeStruct((tokens, D_MODEL), _MXU_DTYPE),
                   jax.ShapeDtypeStruct((batch * N_BLK, n_chunks, LANES, LANES), F32)],
        scratch_shapes=[seq_buf] * 7 + [pltpu.VMEM((n_chunks, LANES), F32)],
        compiler_params=_params(("parallel", "parallel")),
    )(z, lb_logits, hg_g)


def _branch_b_bwd(z, states, dyb, dz, lb_logits, hg_g, batch, seq):
    n_chunks = seq // CHUNK

    def body(z_ref, st_ref, dyb_ref, dz_in_ref, lb_ref, g_ref, dz_ref, dlog_ref, dg_ref,
             f_scr, logf_scr, qh_scr, b_scr, do_scr, qi_scr, dqh_scr, df_scr, dec_scr, dgp_scr, dlb_scr, dst_scr):
        del dz_in_ref
        first = (pl.program_id(0) == 0) & (pl.program_id(1) == 0)
        lb = _hgrn_prepare(z_ref, lb_ref, f_scr, logf_scr, qh_scr, seq)
        causal = _chunk_masks()
        anti_causal = _chunk_masks(transposed=True)
        gain = g_ref[...]

        @pl.when(first)
        def _():
            dg_ref[...] = jnp.zeros_like(dg_ref)

        @pl.when(pl.program_id(1) == 0)
        def _():
            dlb_scr[...] = jnp.zeros_like(dlb_scr)

        def cumulate(c):
            return _cumsum_rows(logf_scr[_chunk_rows(c), :])

        def store_cumulated(c, b):
            b_scr[_chunk_rows(c), :] = b

        def scores(c):
            _, qh, k, v, e_q, e_k, e_qi, e_ks, decay = _chunk_terms(c, z_ref, f_scr, qh_scr, b_scr)
            q_int = qh * e_qi
            return _mm_nt(qh * e_q, k * e_k), _mm_nt(q_int, st_ref[c]), v, q_int, decay

        def output_gradient(c, s):
            att, o_inter, v, q_int, decay = s
            rows = _chunk_rows(c)
            o = _mm(jnp.where(causal, att, 0.0), v) + o_inter
            r = lax.rsqrt(_lane_mean(o * o) + EPS)
            o_n = o * r
            gb = z_ref[3, rows, :]
            sgb = _sigmoid(gb)
            dyb_c = dyb_ref[rows, :]
            d_ong = dyb_c * (gb * sgb)
            d_gb = (dyb_c * (o_n * gain) * (sgb * (1.0 + gb * (1.0 - sgb)))).astype(_MXU_DTYPE)
            d_gain = jnp.sum(d_ong * o_n, axis=0, keepdims=True)
            d_on = d_ong * gain
            return d_gb, d_gain, r * (d_on - o_n * _lane_mean(d_on * o_n)), q_int, decay

        def store_output_gradient(c, out):
            rows = _chunk_rows(c)
            dz_ref[3, rows, :], dgp_scr[pl.ds(c, 1), :], do_scr[rows, :], qi_scr[rows, :], dec_scr[pl.ds(c, 1), :] = out

        def carry_state_gradient(cc, d_state_t):
            c = n_chunks - 1 - cc
            update = dst_scr[c]
            dst_scr[c] = d_state_t
            return d_state_t * dec_scr[pl.ds(c, 1), :] + update

        def score_gradients(c):
            rows, qh, k, v, e_q, e_k, e_qi, e_ks, decay = _chunk_terms(c, z_ref, f_scr, qh_scr, b_scr)
            state_t = st_ref[c]
            d_state_t = dst_scr[c]
            d_o = do_scr[rows, :]
            q_in, k_in, q_int, k_st = qh * e_q, k * e_k, qh * e_qi, k * e_ks
            first = (_mm_nt(k_in, q_in), _mm_nt(d_o, v), _mm_nt(v, d_o), _mm_nt(k_st, d_state_t), _mm(d_o, state_t),
                     _mm(v, d_state_t))
            d_decay = jnp.sum(state_t * d_state_t, axis=0, keepdims=True)
            return first, d_o, q_in, k_in, q_int, k_st, e_q, e_k, e_qi, e_ks, decay, d_decay

        def input_gradients(c, s):
            (att_t, d_att, d_att_t, dv_inter, dq_int, dk_st), d_o, q_in, k_in, q_int, k_st, e_q, e_k, e_qi, e_ks, decay, d_decay = s
            rows = _chunk_rows(c)
            d_v = _mm(jnp.where(anti_causal, att_t, 0.0), d_o) + dv_inter
            dq_in = _mm(jnp.where(causal, d_att, 0.0), k_in)
            dk_in = _mm(jnp.where(anti_causal, d_att_t, 0.0), q_in)
            d_k = dk_in * e_k + dk_st * e_ks
            kk = dk_st * k_st
            d_b = dq_in * q_in + dq_int * q_int - dk_in * k_in - kk
            d_b_last = jnp.sum(kk, axis=0, keepdims=True) + decay * d_decay
            d_logf = _cumsum_rows(d_b, reverse=True) + d_b_last
            return d_v.astype(_MXU_DTYPE), dq_in * e_q + dq_int * e_qi, d_logf / f_scr[rows, :] - d_k

        def store_input_gradients(c, out):
            rows = _chunk_rows(c)
            dz_ref[2, rows, :], dqh_scr[rows, :], df_scr[rows, :] = out

        def input_activations(rows):
            q = z_ref[0, rows, :]
            sq = _sigmoid(q)
            dz_ref[0, rows, :] = (dqh_scr[rows, :] * (sq * (1.0 + q * (1.0 - sq)))).astype(_MXU_DTYPE)
            sg = _sigmoid(z_ref[1, rows, :])
            d_f = df_scr[rows, :]
            dz_ref[1, rows, :] = (d_f * (1.0 - lb) * sg * (1.0 - sg)).astype(_MXU_DTYPE)
            dlb_scr[...] += jnp.sum(d_f * (1.0 - sg), axis=0, keepdims=True)

        _independent_trips(n_chunks, cumulate, store_cumulated)
        _independent_trips(n_chunks, [scores, output_gradient], store_output_gradient)
        _token_contractions(do_scr, qi_scr, dst_scr, seq)
        lax.fori_loop(0, n_chunks, carry_state_gradient, jnp.zeros((LANES, LANES), F32))
        _independent_trips(n_chunks, [score_gradients, input_gradients], store_input_gradients)
        dg_ref[...] += jnp.sum(dgp_scr[...], axis=0, keepdims=True)
        _row_blocks(seq, input_activations)
        d_l0 = dlb_scr[...] * lb * (1.0 - lb)
        dlog_ref[0:1, :] = d_l0
        dlog_ref[1:2, :] = -d_l0

    tokens = batch * seq
    seq_buf = pltpu.VMEM((seq, LANES), F32)
    chunk_rows = pltpu.VMEM((n_chunks, LANES), F32)
    return pl.pallas_call(
        body, name="branch_b_bwd",
        grid=(N_BLK, batch),
        in_specs=[pl.BlockSpec((4, seq, LANES), lambda h, b: (0, b, h)),
                  pl.BlockSpec((None, n_chunks, LANES, LANES), lambda h, b: (b * N_BLK + h, 0, 0, 0)),
                  pl.BlockSpec((seq, LANES), lambda h, b: (b, h)),
                  pl.BlockSpec(memory_space=pl.ANY),
                  pl.BlockSpec((2, LANES), lambda h, b: (0, h)),
                  pl.BlockSpec((1, LANES), lambda h, b: (0, 0))],
        out_specs=[pl.BlockSpec((4, seq, LANES), lambda h, b: (0, b, h)),
                   pl.BlockSpec((2, LANES), lambda h, b: (0, h)),
                   pl.BlockSpec((1, LANES), lambda h, b: (0, 0))],
        out_shape=[jax.ShapeDtypeStruct((N_GROUPS, tokens, D_MODEL), _MXU_DTYPE),
                   jax.ShapeDtypeStruct((2, D_MODEL), F32),
                   jax.ShapeDtypeStruct((1, LANES), F32)],
        scratch_shapes=[seq_buf] * 8 + [chunk_rows, chunk_rows, pltpu.VMEM((1, LANES), F32),
                                        pltpu.VMEM((n_chunks, LANES, LANES), F32)],
        input_output_aliases={3: 0},
        compiler_params=_params(("arbitrary", "arbitrary")),
    )(z, states, dyb, dz, lb_logits, hg_g)


def _merge_tail(ya, yb, z, x2d, tgt2d, b_merge, final_g, pa, pb, wo):
    tokens, d = x2d.shape
    tm = min(256, tokens)
    n_tiles = tokens // tm

    def body(ya_ref, yb_ref, z_ref, x_ref, t_ref, bm_ref, fg_ref, pa_hbm, pb_hbm, wo_hbm,
             dya_ref, dyb_ref, dx2_ref, dz_ref, loss_ref, dfg_ref, dbm_ref, dpa_hbm, dpb_hbm, dwo_hbm,
             pa_s, pb_s, wo_s, dpa_s, dpb_s, dwo_s):
        i = pl.program_id(0)

        @pl.when(i == 0)
        def _():
            pltpu.sync_copy(pa_hbm, pa_s)
            pltpu.sync_copy(pb_hbm, pb_s)
            pltpu.sync_copy(wo_hbm, wo_s)
            dpa_s[...] = jnp.zeros_like(dpa_s)
            dpb_s[...] = jnp.zeros_like(dpb_s)
            dwo_s[...] = jnp.zeros_like(dwo_s)
            loss_ref[...] = jnp.zeros_like(loss_ref)
            dfg_ref[...] = jnp.zeros_like(dfg_ref)
            dbm_ref[...] = jnp.zeros_like(dbm_ref)

        ya_t = ya_ref[...]
        yb_t = yb_ref[...]
        out_a = _mm(ya_t, pa_s[...])
        out_b = _mm(yb_t, pb_s[...])
        g_a = _sigmoid(z_ref[0] + bm_ref[:, :d])
        g_b = _sigmoid(z_ref[1] + bm_ref[:, d:])
        mixed = g_a * out_a + g_b * out_b
        x2 = x_ref[...] + _mm(mixed, wo_s[...])
        r = lax.rsqrt(jnp.mean(x2 * x2, axis=-1, keepdims=True) + EPS)
        xn = x2 * r
        fg = fg_ref[...]
        diff = xn * fg - t_ref[...]
        loss_ref[...] += jnp.sum(diff * diff) * (0.5 / d)
        dy = diff * (1.0 / d)
        dfg_ref[...] += jnp.sum(dy * xn, axis=0, keepdims=True)
        dxn = dy * fg
        dx2 = r * (dxn - xn * jnp.mean(dxn * xn, axis=-1, keepdims=True))
        dx2_ref[...] = dx2
        dmixed = _mm_nt(dx2, wo_s[...])
        dwo_s[...] += _mm_tn(mixed, dx2)
        dgm_a = dmixed * out_a * g_a * (1.0 - g_a)
        dgm_b = dmixed * out_b * g_b * (1.0 - g_b)
        dz_ref[0] = dgm_a.astype(_MXU_DTYPE)
        dz_ref[1] = dgm_b.astype(_MXU_DTYPE)
        dbm_ref[:, :d] += jnp.sum(dgm_a, axis=0, keepdims=True)
        dbm_ref[:, d:] += jnp.sum(dgm_b, axis=0, keepdims=True)
        dout_a = dmixed * g_a
        dout_b = dmixed * g_b
        dpa_s[...] += _mm_tn(ya_t, dout_a)
        dpb_s[...] += _mm_tn(yb_t, dout_b)
        dya_ref[...] = _mm_nt(dout_a, pa_s[...])
        dyb_ref[...] = _mm_nt(dout_b, pb_s[...])

        @pl.when(i == n_tiles - 1)
        def _():
            pltpu.sync_copy(dpa_s, dpa_hbm)
            pltpu.sync_copy(dpb_s, dpb_hbm)
            pltpu.sync_copy(dwo_s, dwo_hbm)

    tile = pl.BlockSpec((tm, d), lambda i: (i, 0))
    gm = pl.BlockSpec((2, tm, d), lambda i: (3, i, 0))
    row = lambda n: pl.BlockSpec((1, n), lambda i: (0, 0))
    hbm = pl.BlockSpec(memory_space=pl.ANY)
    act = jax.ShapeDtypeStruct((tokens, d), F32)
    mat = jax.ShapeDtypeStruct((d, d), F32)
    return pl.pallas_call(
        body, name="merge_tail",
        grid=(n_tiles,),
        in_specs=[tile, tile, gm, tile, tile, row(2 * d), row(d), hbm, hbm, hbm],
        out_specs=[tile, tile, tile, gm, row(LANES), row(d), row(2 * d), hbm, hbm, hbm],
        out_shape=[act, act, act, jax.ShapeDtypeStruct((N_GROUPS, tokens, d), _MXU_DTYPE),
                   jax.ShapeDtypeStruct((1, LANES), F32), jax.ShapeDtypeStruct((1, d), F32),
                   jax.ShapeDtypeStruct((1, 2 * d), F32), mat, mat, mat],
        scratch_shapes=[pltpu.VMEM((d, d), _MXU_DTYPE)] * 3 + [pltpu.VMEM((d, d), F32)] * 3,
        compiler_params=_params(("arbitrary",)),
    )(ya, yb, z, x2d, tgt2d, b_merge, final_g, pa, pb, wo)


def _inproj_dw(h, dz):
    tokens, d = h.shape
    tm = min(512, tokens)

    def body(h_ref, dz_ref, dw_ref):
        part = _mm_tn(h_ref[...], dz_ref[...])

        @pl.when(pl.program_id(1) == 0)
        def _():
            dw_ref[...] = part

        @pl.when(pl.program_id(1) != 0)
        def _():
            dw_ref[...] += part

    def out_index(s, i):
        g = _group_of_slot(s)
        return (g // 2, 0, g % 2)

    return pl.pallas_call(
        body, name="inproj_dw",
        grid=(N_GROUPS, tokens // tm),
        in_specs=[pl.BlockSpec((tm, d), lambda s, i: (i, 0)),
                  pl.BlockSpec((None, tm, D_MODEL), lambda s, i: (s, i, 0))],
        out_specs=pl.BlockSpec((None, d, D_MODEL), out_index),
        out_shape=jax.ShapeDtypeStruct((N_SHARDS, d, 2 * D_MODEL), F32),
        compiler_params=_params(("parallel", "arbitrary")),
    )(h, dz)


def _inproj_dx(dz, w_all, x2d, dx2, norm_g):
    tokens, d = x2d.shape
    tm = min(512, tokens)

    def body(dz_ref, w_ref, x_ref, dx2_ref, g_ref, dx_ref, dg_ref, acc):
        s = pl.program_id(1)
        part = _mm_nt(dz_ref[...], w_ref[...])

        @pl.when(s == 0)
        def _():
            acc[...] = part

        @pl.when(s != 0)
        def _():
            acc[...] += part

        @pl.when((pl.program_id(0) == 0) & (s == 0))
        def _():
            dg_ref[...] = jnp.zeros_like(dg_ref)

        @pl.when(s == N_GROUPS - 1)
        def _():
            x = x_ref[...]
            r = lax.rsqrt(jnp.mean(x * x, axis=-1, keepdims=True) + EPS)
            xn = x * r
            dh = acc[...]
            dg_ref[...] += jnp.sum(dh * xn, axis=0, keepdims=True)
            dxn = dh * g_ref[...]
            dx_ref[...] = r * (dxn - xn * jnp.mean(dxn * xn, axis=-1, keepdims=True)) + dx2_ref[...]

    def w_index(i, s):
        g = _group_of_slot(s)
        return (g // 2, 0, g % 2)

    tile = pl.BlockSpec((tm, d), lambda i, s: (i, 0))
    return pl.pallas_call(
        body, name="inproj_dx",
        grid=(tokens // tm, N_GROUPS),
        in_specs=[pl.BlockSpec((None, tm, D_MODEL), lambda i, s: (s, i, 0)),
                  pl.BlockSpec((None, d, D_MODEL), w_index), tile, tile,
                  pl.BlockSpec((1, d), lambda i, s: (0, 0))],
        out_specs=[tile, pl.BlockSpec((1, d), lambda i, s: (0, 0))],
        out_shape=[jax.ShapeDtypeStruct((tokens, d), F32), jax.ShapeDtypeStruct((1, d), F32)],
        scratch_shapes=[pltpu.VMEM((tm, d), F32)],
        compiler_params=_params(("arbitrary", "arbitrary")),
    )(dz, w_all, x2d, dx2, norm_g)


def _row_tile(rows, cols, itemsize=4, budget=2 * 1024 * 1024):
    tr = rows
    while tr * cols * itemsize > budget and tr % 16 == 0:
        tr //= 2
    return tr


def _cast_into_slot(a, chip, dtype, name):
    rows, cols = a.shape
    tr = _row_tile(rows, cols)

    def body(chip_ref, a_ref, o_ref):
        del chip_ref
        o_ref[...] = a_ref[...].astype(dtype)

    grid_spec = pltpu.PrefetchScalarGridSpec(
        num_scalar_prefetch=1, grid=(rows // tr,),
        in_specs=[pl.BlockSpec((tr, cols), lambda i, chip_ref: (i, 0))],
        out_specs=pl.BlockSpec((None, tr, cols), lambda i, chip_ref: (chip_ref[0], i, 0)))
    return pl.pallas_call(body, name=name, grid_spec=grid_spec,
                          out_shape=jax.ShapeDtypeStruct((N_SHARDS, rows, cols), dtype),
                          compiler_params=_params(("arbitrary",)))(chip, a)


def _sum_slots(stack, name):
    n, rows, cols = stack.shape
    tr = _row_tile(rows, cols * n)

    def body(s_ref, o_ref):
        total = s_ref[0].astype(F32)
        for k in range(1, n):
            total = total + s_ref[k].astype(F32)
        o_ref[...] = total

    return pl.pallas_call(body, name=name, grid=(rows // tr,),
                          in_specs=[pl.BlockSpec((n, tr, cols), lambda i: (0, i, 0))],
                          out_specs=pl.BlockSpec((tr, cols), lambda i: (i, 0)),
                          out_shape=jax.ShapeDtypeStruct((rows, cols), F32),
                          compiler_params=_params(("parallel",)))(stack)


def _add_half(full, landed, place, name):
    n, rows, cols = full.shape
    half = rows // 2
    tr = _row_tile(half, cols)
    nb = half // tr

    def body(place_ref, a_ref, b_ref, o_ref, own_ref):
        total = (a_ref[...] + b_ref[...]).astype(_MXU_DTYPE)
        o_ref[...] = total

        @pl.when(pl.program_id(1) == place_ref[1])
        def _():
            own_ref[...] = total

    grid_spec = pltpu.PrefetchScalarGridSpec(
        num_scalar_prefetch=1, grid=(nb, n),
        in_specs=[pl.BlockSpec((None, tr, cols), lambda i, j, place_ref: (j, place_ref[0] * nb + i, 0)),
                  pl.BlockSpec((None, tr, cols), lambda i, j, place_ref: (j, i, 0))],
        out_specs=[pl.BlockSpec((None, tr, cols), lambda i, j, place_ref: (j, i, 0)),
                   pl.BlockSpec((None, tr, cols), lambda i, j, place_ref: (place_ref[1], i, 0))])
    shape = jax.ShapeDtypeStruct((n, half, cols), _MXU_DTYPE)
    return pl.pallas_call(body, name=name, grid_spec=grid_spec, out_shape=[shape, shape],
                          compiler_params=_params(("parallel", "arbitrary")))(place, full, landed)


def _adamw_update(w, grad, m, v):
    c1 = 1.0 - ADAM_B1 ** ADAM_STEP
    c2 = 1.0 - ADAM_B2 ** ADAM_STEP
    nm = ADAM_B1 * m + (1.0 - ADAM_B1) * grad
    nv = ADAM_B2 * v + (1.0 - ADAM_B2) * (grad * grad)
    return (-ADAM_LR) * ((nm / c1) / (jnp.sqrt(nv / c2) + ADAM_EPS) + ADAM_WD * w), nm, nv


def _adamw(w, g, m, v, name):
    rows, cols = w.shape
    tr = _row_tile(rows, cols, budget=1024 * 1024)

    def body(w_ref, g_ref, m_ref, v_ref, d_ref, nm_ref, nv_ref):
        d_ref[...], nm_ref[...], nv_ref[...] = _adamw_update(w_ref[...], g_ref[...], m_ref[...], v_ref[...])

    spec = pl.BlockSpec((tr, cols), lambda i: (i, 0))
    shape = jax.ShapeDtypeStruct((rows, cols), F32)
    return pl.pallas_call(body, name=name, grid=(rows // tr,), in_specs=[spec] * 4, out_specs=[spec] * 3,
                          out_shape=[shape] * 3, compiler_params=_params(("parallel",)))(w, g, m, v)


def _adamw_halves(w, g_mine, g_sibling, m, v, core, name):
    rows, cols = w.shape
    half = rows // 2
    tr = _row_tile(half, cols, budget=1024 * 1024)
    nb = half // tr

    def body(core_ref, w_ref, gm_ref, gs_ref, m_ref, v_ref, g_ref, d_ref, nm_ref, nv_ref):
        mine = pl.program_id(0) // nb == core_ref[0]
        grad = jnp.where(mine, gm_ref[...], gs_ref[...])
        g_ref[...] = grad
        d_ref[...], nm_ref[...], nv_ref[...] = _adamw_update(w_ref[...], grad, m_ref[...], v_ref[...])

    spec = pl.BlockSpec((tr, cols), lambda i, core_ref: (i, 0))
    half_spec = pl.BlockSpec((tr, cols), lambda i, core_ref: (i % nb, 0))
    grid_spec = pltpu.PrefetchScalarGridSpec(num_scalar_prefetch=1, grid=(rows // tr,),
                                             in_specs=[spec, half_spec, half_spec, spec, spec], out_specs=[spec] * 4)
    shape = jax.ShapeDtypeStruct((rows, cols), F32)
    return pl.pallas_call(body, name=name, grid_spec=grid_spec, out_shape=[shape] * 4,
                          compiler_params=_params(("parallel",)))(core, w, g_mine, g_sibling, m, v)


def _local_step(x, loss_target, w_all, pa, pb, wo, conv_w, b_merge, conv_b, rg_wx, rg_bx, rg_wa, rg_ba,
                rg_lambda, hg_lb_logits, hg_norm_g, norm_g, final_norm_g):
    batch, seq, d = x.shape
    x2d = x.reshape(batch * seq, d)
    tgt2d = loss_target.reshape(batch * seq, d)
    z, h = _inproj_fwd(x2d, norm_g, w_all)
    lru = (conv_w, conv_b, rg_wx, rg_bx, rg_wa, rg_ba, rg_lambda)
    ya, hl = _branch_a_fwd(z, *lru, batch, seq)
    yb, states = _branch_b_fwd(z, hg_lb_logits, hg_norm_g, batch, seq)
    dya, dyb, dx2, dz, loss, d_final_g, d_b_merge, d_pa, d_pb, d_wo = _merge_tail(
        ya, yb, z, x2d, tgt2d, b_merge, final_norm_g, pa, pb, wo)
    dz, d_lb_logits, d_hg_g = _branch_b_bwd(z, states, dyb, dz, hg_lb_logits, hg_norm_g, batch, seq)
    dz, d_conv_w, d_conv_b, d_wx, d_bx, d_wa, d_ba, d_lam = _branch_a_bwd(z, hl, dya, dz, *lru, batch, seq)
    d_w_in = _inproj_dw(h, dz)
    grad_x, d_norm_g = _inproj_dx(dz, w_all, x2d, dx2, norm_g)
    big = (d_w_in, d_pa, d_pb, d_wo)
    small = dict(b_merge=d_b_merge, conv_w=d_conv_w, conv_b=d_conv_b, rg_wx=d_wx, rg_bx=d_bx, rg_wa=d_wa,
                 rg_ba=d_ba, rg_lambda=d_lam, hg_lb_logits=d_lb_logits, hg_norm_g=d_hg_g, norm_g=d_norm_g,
                 final_norm_g=d_final_g)
    return loss[0, 0], grad_x.reshape(batch, seq, d), big, small


_SMALL_ORDER = ("b_merge", "conv_w", "conv_b", "rg_wx", "rg_bx", "rg_wa", "rg_ba", "rg_lambda", "hg_lb_logits",
                "hg_norm_g", "norm_g", "final_norm_g")
N_DEV = 8
PIECE_ROWS = 272


def _pack_small(tree):
    flat = jnp.concatenate([tree[k].reshape(-1) for k in _SMALL_ORDER])
    flat = jnp.pad(flat, (0, N_DEV * PIECE_ROWS * LANES - flat.shape[0]))
    return flat.reshape(N_DEV * PIECE_ROWS, LANES)


def _unpack_small(packed, like):
    flat = packed.reshape(-1)
    out, pos = {}, 0
    for k in _SMALL_ORDER:
        n = like[k].size
        out[k] = flat[pos:pos + n].reshape(like[k].shape)
        pos += n
    return out


def _mesh_position():
    x, y, c = lax.axis_index("x"), lax.axis_index("y"), lax.axis_index("c")
    other_chips = [(1 - x, y), (x, 1 - y), (1 - x, 1 - y)]
    return x, y, c, other_chips


def _other_devices(x, y, c):
    flips = [(fx, fy, fc) for fx in (0, 1) for fy in (0, 1) for fc in (0, 1) if (fx, fy, fc) != (0, 0, 0)]
    return [(jnp.where(fx, 1 - x, x), jnp.where(fy, 1 - y, y), jnp.where(fc, 1 - c, c)) for fx, fy, fc in flips]


def _remote(src, dst, send_sems, recv_sems, k, device):
    return pltpu.make_async_remote_copy(src_ref=src, dst_ref=dst, send_sem=send_sems.at[k], recv_sem=recv_sems.at[k],
                                        device_id=device, device_id_type=MESH)


def _gather_weights(slotted, conv_slotted):
    n_big = len(slotted)
    n_sem = 6 * n_big + 3

    def body(*refs):
        bufs, cw = refs[n_big + 1:2 * n_big + 1], refs[2 * n_big + 1]
        send_sems, recv_sems = refs[2 * n_big + 2:]
        x, y, c, chips = _mesh_position()
        me, sibling = 2 * x + y, (x, y, 1 - c)

        def half(buf, slot, which):
            hs = buf.shape[1] // 2
            return buf.at[slot, pl.ds(which * hs, hs), :]

        sends = []
        for a in range(n_big):
            for j, (cx, cy) in enumerate(chips):
                mine = half(bufs[a], me, c)
                sends.append(_remote(mine, mine, send_sems, recv_sems, 6 * a + j, (cx, cy, c)))
        for j, (cx, cy) in enumerate(chips):
            sends.append(_remote(cw.at[me], cw.at[me], send_sems, recv_sems, 6 * n_big + j, (cx, cy, c)))
        for cp in sends:
            cp.start()
        for j, (cx, cy) in enumerate(chips):
            theirs = 2 * cx + cy
            for a in range(n_big):
                landed = half(bufs[a], theirs, c)
                _remote(landed, landed, send_sems, recv_sems, 6 * a + j, (cx, cy, c)).wait_recv()
                passed = _remote(landed, landed, send_sems, recv_sems, 6 * a + 3 + j, sibling)
                passed.start()
                sends.append(passed)
        for j, (cx, cy) in enumerate(chips):
            theirs = 2 * cx + cy
            _remote(cw.at[theirs], cw.at[theirs], send_sems, recv_sems, 6 * n_big + j, (cx, cy, c)).wait_recv()
            for a in range(n_big):
                landed = half(bufs[a], theirs, 1 - c)
                _remote(landed, landed, send_sems, recv_sems, 6 * a + 3 + j, sibling).wait_recv()
        for cp in sends:
            cp.wait_send()

    hbm = pl.BlockSpec(memory_space=pl.ANY)
    operands = list(slotted) + [conv_slotted]
    return pl.pallas_call(
        body, name="gather_weights",
        in_specs=[hbm] * (n_big + 1), out_specs=[hbm] * (n_big + 1),
        out_shape=[jax.ShapeDtypeStruct(a.shape, a.dtype) for a in operands],
        scratch_shapes=[pltpu.SemaphoreType.DMA((n_sem,)), pltpu.SemaphoreType.DMA((n_sem,))],
        input_output_aliases={i: i for i in range(n_big + 1)},
    )(*operands)


def _exchange_halves(bigs, small):
    n_big = len(bigs)
    n_sem = n_big + N_DEV - 1

    def body(*refs):
        srcs, small_src = refs[:n_big], refs[n_big]
        outs, small_out = refs[n_big + 1:2 * n_big + 1], refs[2 * n_big + 1]
        send_sems, recv_sems, local_sem = refs[2 * n_big + 2:]
        x, y, c, _ = _mesh_position()
        me, sibling = 4 * x + 2 * y + c, (x, y, 1 - c)
        mine = pltpu.make_async_copy(small_src.at[pl.ds(me * PIECE_ROWS, PIECE_ROWS), :], small_out.at[me], local_sem)
        mine.start()
        copies = []
        for a in range(n_big):
            hs = srcs[a].shape[1] // 2
            copies.append(_remote(srcs[a].at[:, pl.ds((1 - c) * hs, hs), :], outs[a], send_sems, recv_sems, a, sibling))
        for k, (px, py, pc) in enumerate(_other_devices(x, y, c)):
            piece = small_src.at[pl.ds((4 * px + 2 * py + pc) * PIECE_ROWS, PIECE_ROWS), :]
            copies.append(_remote(piece, small_out.at[me], send_sems, recv_sems, n_big + k, (px, py, pc)))
        for cp in copies:
            cp.start()
        for cp in copies:
            cp.wait()
        mine.wait()

    hbm = pl.BlockSpec(memory_space=pl.ANY)
    out_shape = [jax.ShapeDtypeStruct((g.shape[0], g.shape[1] // 2, g.shape[2]), F32) for g in bigs]
    out_shape.append(jax.ShapeDtypeStruct((N_DEV, PIECE_ROWS, LANES), F32))
    return pl.pallas_call(
        body, name="exchange_halves",
        in_specs=[hbm] * (n_big + 1), out_specs=[hbm] * (n_big + 1), out_shape=out_shape,
        scratch_shapes=[pltpu.SemaphoreType.DMA((n_sem,)), pltpu.SemaphoreType.DMA((n_sem,)), pltpu.SemaphoreType.DMA],
    )(*bigs, small)


def _scatter_shards(bigs, by_chip, small_piece):
    n_big = len(bigs)
    n_sem = 3 * n_big + N_DEV - 1

    def body(*refs):
        srcs, small_src = refs[:n_big], refs[2 * n_big]
        outs, small_out = refs[2 * n_big + 1:3 * n_big + 1], refs[3 * n_big + 1]
        send_sems, recv_sems, local_sem = refs[3 * n_big + 2:]
        x, y, c, chips = _mesh_position()
        chip, me = 2 * x + y, 4 * x + 2 * y + c
        mine = pltpu.make_async_copy(small_src, small_out.at[me], local_sem)
        mine.start()
        copies = []
        for a in range(n_big):
            for j, (cx, cy) in enumerate(chips):
                copies.append(_remote(srcs[a].at[2 * cx + cy], outs[a].at[chip], send_sems, recv_sems, 3 * a + j, (cx, cy, c)))
        for k, peer in enumerate(_other_devices(x, y, c)):
            copies.append(_remote(small_src, small_out.at[me], send_sems, recv_sems, 3 * n_big + k, peer))
        for cp in copies:
            cp.start()
        for cp in copies:
            cp.wait()
        mine.wait()

    hbm = pl.BlockSpec(memory_space=pl.ANY)
    out_shape = [jax.ShapeDtypeStruct(g.shape, g.dtype) for g in by_chip]
    out_shape.append(jax.ShapeDtypeStruct((N_DEV, PIECE_ROWS, LANES), F32))
    return pl.pallas_call(
        body, name="scatter_shards",
        in_specs=[hbm] * (2 * n_big + 1), out_specs=[hbm] * (n_big + 1), out_shape=out_shape,
        scratch_shapes=[pltpu.SemaphoreType.DMA((n_sem,)), pltpu.SemaphoreType.DMA((n_sem,)), pltpu.SemaphoreType.DMA],
        input_output_aliases={n_big + a: a for a in range(n_big)},
    )(*bigs, *by_chip, small_piece)


def _swap_halves(halves):
    n_big = len(halves)

    def body(*refs):
        srcs, outs = refs[:n_big], refs[n_big:2 * n_big]
        send_sems, recv_sems = refs[2 * n_big:]
        x, y, c, _ = _mesh_position()
        copies = [_remote(srcs[a], outs[a], send_sems, recv_sems, a, (x, y, 1 - c)) for a in range(n_big)]
        for cp in copies:
            cp.start()
        for cp in copies:
            cp.wait()

    hbm = pl.BlockSpec(memory_space=pl.ANY)
    return pl.pallas_call(
        body, name="swap_halves",
        in_specs=[hbm] * n_big, out_specs=[hbm] * n_big,
        out_shape=[jax.ShapeDtypeStruct(h.shape, F32) for h in halves],
        scratch_shapes=[pltpu.SemaphoreType.DMA((n_big,)), pltpu.SemaphoreType.DMA((n_big,))],
    )(*halves)


def _reduce_gradients(bigs, small_tree):
    core = lax.axis_index("c").astype(jnp.int32)
    chip = (2 * lax.axis_index("x") + lax.axis_index("y")).astype(jnp.int32)
    place = jnp.stack([core, chip])
    *landed, small_landed = _exchange_halves(bigs, _pack_small(small_tree))
    sums = [_add_half(g, l, place, f"add_half_{a}") for a, (g, l) in enumerate(zip(bigs, landed))]
    small_piece = _sum_slots(small_landed, "sum_small")
    *by_chip, small_all = _scatter_shards([s[0] for s in sums], [s[1] for s in sums], small_piece)
    mine = [_sum_slots(s, f"sum_chips_{a}") for a, s in enumerate(by_chip)]
    theirs = _swap_halves(mine)
    return mine, theirs, _unpack_small(small_all, small_tree)


def kernel(x, w_in, b_merge, conv_w, conv_b, rg_wx, rg_bx, rg_wa, rg_ba, rg_lambda, hg_lb_logits, hg_norm_g, proj_a, proj_b, w_out, norm_g, final_norm_g, loss_target, m_w_in, m_b_merge, m_conv_w, m_conv_b, m_rg_wx, m_rg_bx, m_rg_wa, m_rg_ba, m_rg_lambda, m_hg_lb_logits, m_hg_norm_g, m_proj_a, m_proj_b, m_w_out, m_norm_g, m_final_norm_g, v_w_in, v_b_merge, v_conv_w, v_conv_b, v_rg_wx, v_rg_bx, v_rg_wa, v_rg_ba, v_rg_lambda, v_hg_lb_logits, v_hg_norm_g, v_proj_a, v_proj_b, v_w_out, v_norm_g, v_final_norm_g):
    d = D_MODEL
    weights = dict(w_in=w_in, b_merge=b_merge, conv_w=conv_w, conv_b=conv_b, rg_wx=rg_wx, rg_bx=rg_bx, rg_wa=rg_wa,
                   rg_ba=rg_ba, rg_lambda=rg_lambda, hg_lb_logits=hg_lb_logits, hg_norm_g=hg_norm_g, proj_a=proj_a,
                   proj_b=proj_b, w_out=w_out, norm_g=norm_g, final_norm_g=final_norm_g)
    m = dict(w_in=m_w_in, b_merge=m_b_merge, conv_w=m_conv_w, conv_b=m_conv_b, rg_wx=m_rg_wx, rg_bx=m_rg_bx,
             rg_wa=m_rg_wa, rg_ba=m_rg_ba, rg_lambda=m_rg_lambda, hg_lb_logits=m_hg_lb_logits, hg_norm_g=m_hg_norm_g,
             proj_a=m_proj_a, proj_b=m_proj_b, w_out=m_w_out, norm_g=m_norm_g, final_norm_g=m_final_norm_g)
    v = dict(w_in=v_w_in, b_merge=v_b_merge, conv_w=v_conv_w, conv_b=v_conv_b, rg_wx=v_rg_wx, rg_bx=v_rg_bx,
             rg_wa=v_rg_wa, rg_ba=v_rg_ba, rg_lambda=v_rg_lambda, hg_lb_logits=v_hg_lb_logits, hg_norm_g=v_hg_norm_g,
             proj_a=v_proj_a, proj_b=v_proj_b, w_out=v_w_out, norm_g=v_norm_g, final_norm_g=v_final_norm_g)
    big_names = ("w_in", "proj_a", "proj_b", "w_out")

    core = lax.axis_index("c").astype(jnp.int32).reshape(1)
    chip = (2 * lax.axis_index("x") + lax.axis_index("y")).astype(jnp.int32)

    slotted = [_cast_into_slot(weights[k][0], chip.reshape(1), _MXU_DTYPE, f"cast_{k}") for k in big_names]
    conv_slotted = _cast_into_slot(conv_w[0], chip.reshape(1), F32, "slot_conv_w")
    w_all, pa_all, pb_all, wo_all, cw_all = _gather_weights(slotted, conv_slotted)
    conv_w_full = jnp.transpose(cw_all, (1, 0, 2)).reshape(CONV_WIDTH, d)

    loss_part, grad_x, big_grads, small_grads = _local_step(
        x, loss_target, w_all, pa_all.reshape(d, d), pb_all.reshape(d, d), wo_all.reshape(d, d), conv_w_full,
        b_merge, conv_b, rg_wx[0], rg_bx.reshape(1, d), rg_wa[0], rg_ba.reshape(1, d), rg_lambda, hg_lb_logits,
        hg_norm_g, norm_g, final_norm_g.reshape(1, d))
    loss = lax.psum(loss_part, ("x", "y", "c"))

    big_grads = [big_grads[0]] + [g.reshape(N_SHARDS, d // N_SHARDS, d) for g in big_grads[1:]]
    mine, theirs, small_red = _reduce_gradients(big_grads, small_grads)

    grads, delta, new_m, new_v = {}, {}, {}, {}
    for k, g_mine, g_theirs in zip(big_names, mine, theirs):
        out = _adamw_halves(weights[k][0], g_mine, g_theirs, m[k][0], v[k][0], core, f"adamw_{k}")
        grads[k], delta[k], new_m[k], new_v[k] = (t.reshape(weights[k].shape) for t in out)
    cols = d // N_SHARDS
    g_conv = lax.dynamic_slice(small_red["conv_w"], (0, chip * cols), (CONV_WIDTH, cols))
    grads["conv_w"] = g_conv.reshape(conv_w.shape)
    dl, nm, nv = _adamw(conv_w[0], g_conv, m_conv_w[0], v_conv_w[0], "adamw_conv_w")
    delta["conv_w"], new_m["conv_w"], new_v["conv_w"] = (t.reshape(conv_w.shape) for t in (dl, nm, nv))
    rest = [k for k in _SMALL_ORDER if k != "conv_w"]
    like = {k: (weights[k] if k != "conv_w" else jnp.zeros((CONV_WIDTH, d), F32)) for k in _SMALL_ORDER}
    packs = [_pack_small({k: (t[k] if k != "conv_w" else like[k]) for k in _SMALL_ORDER}) for t in (weights, m, v)]
    g_pack = _pack_small({k: small_red[k].reshape(like[k].shape) for k in _SMALL_ORDER})
    outs = [_unpack_small(p, like) for p in _adamw(packs[0], g_pack, packs[1], packs[2], "adamw_small")]
    for k in rest:
        grads[k] = small_red[k].reshape(weights[k].shape)
        delta[k], new_m[k], new_v[k] = outs[0][k], outs[1][k], outs[2][k]

    order = ("w_in", "b_merge", "conv_w", "conv_b", "rg_wx", "rg_bx", "rg_wa", "rg_ba", "rg_lambda", "hg_lb_logits",
             "hg_norm_g", "proj_a", "proj_b", "w_out", "norm_g", "final_norm_g")
    return (loss, grad_x, *[grads[k] for k in order], *[delta[k] for k in order], *[new_m[k] for k in order],
            *[new_v[k] for k in order])
```

```python
import functools

import jax
import jax.numpy as jnp
from jax import lax
from jax.experimental import pallas as pl
from jax.experimental.pallas import tpu as pltpu

F32 = jnp.float32
_MXU_DTYPE = jnp.bfloat16

D_MODEL = 1024
LANES = 128
SUBLANES = 8
N_BLK = D_MODEL // LANES
N_GROUPS = 8
N_SHARDS = 4
CONV_WIDTH = 4
LRU_C = 8.0
CHUNK = 64
CHUNKS_IN_FLIGHT = 4
HG_SCALE = float(LANES) ** -0.5
EPS = 1e-6
ADAM_LR, ADAM_B1, ADAM_B2, ADAM_EPS, ADAM_WD, ADAM_STEP = 0.001, 0.9, 0.999, 1e-08, 0.01, 10
VMEM_LIMIT = 56 * 1024 * 1024
MESH = pl.DeviceIdType.MESH

_SLOT_TO_GROUP = (2, 3, 4, 5, 0, 1, 6, 7)


def _group_of_slot(s):
    return jnp.where(s < 4, s + 2, jnp.where(s < 6, s - 4, s))


def _mm(a, b):
    return lax.dot_general(a.astype(_MXU_DTYPE), b.astype(_MXU_DTYPE), (((1,), (0,)), ((), ())),
                           preferred_element_type=F32)


def _mm_nt(a, b):
    return lax.dot_general(a.astype(_MXU_DTYPE), b.astype(_MXU_DTYPE), (((1,), (1,)), ((), ())),
                           preferred_element_type=F32)


def _mm_tn(a, b):
    return lax.dot_general(a.astype(_MXU_DTYPE), b.astype(_MXU_DTYPE), (((0,), (0,)), ((), ())),
                           preferred_element_type=F32)


def _sigmoid(x):
    return 1.0 / (1.0 + jnp.exp(-x))


def _log1p_pos(y):
    series = y * (1.0 - y * (0.5 - y * (1.0 / 3.0 - y * 0.25)))
    return jnp.where(y < 0.01, series, jnp.log(1.0 + y))


def _expm1_neg(y):
    series = y * (1.0 + y * 0.5 * (1.0 + y * (1.0 / 3.0) * (1.0 + y * 0.25 * (1.0 + y * 0.2))))
    return jnp.where(y > -0.02, series, jnp.exp(y) - 1.0)


def _softplus(x):
    return jnp.maximum(x, 0.0) + _log1p_pos(jnp.exp(-jnp.abs(x)))


def _shift_down(x, n):
    rows = lax.broadcasted_iota(jnp.int32, x.shape, 0)
    return jnp.where(rows >= n, pltpu.roll(x, n, 0), 0.0)


def _shift_up(x, n):
    size = x.shape[0]
    rows = lax.broadcasted_iota(jnp.int32, x.shape, 0)
    return jnp.where(rows < size - n, pltpu.roll(x, size - n, 0), 0.0)


def _params(dims, vmem=VMEM_LIMIT):
    return pltpu.CompilerParams(dimension_semantics=dims, vmem_limit_bytes=vmem)


def _load_w_in_by_slot(w_hbm, w_res):
    for slot, g in enumerate(_SLOT_TO_GROUP):
        pltpu.sync_copy(w_hbm.at[g // 2, :, pl.ds((g % 2) * D_MODEL, D_MODEL)], w_res.at[slot])


def _inproj_fwd(x2d, norm_g, w_all):
    tokens, d = x2d.shape
    tm = min(512, tokens)

    def body(x_ref, g_ref, w_hbm, z_ref, ht_ref, h_scr, w_res):
        @pl.when((pl.program_id(0) == 0) & (pl.program_id(1) == 0))
        def _():
            _load_w_in_by_slot(w_hbm, w_res)

        @pl.when(pl.program_id(1) == 0)
        def _():
            x = x_ref[...]
            r = lax.rsqrt(jnp.mean(x * x, axis=-1, keepdims=True) + EPS)
            h = (x * r) * g_ref[...]
            h_scr[...] = h.astype(_MXU_DTYPE)
            ht_ref[...] = jnp.transpose(h).astype(_MXU_DTYPE)

        z_ref[...] = _mm(h_scr[...], w_res[pl.program_id(1)])

    return pl.pallas_call(
        body, name="inproj_fwd",
        grid=(tokens // tm, N_GROUPS),
        in_specs=[pl.BlockSpec((tm, d), lambda i, s: (i, 0)),
                  pl.BlockSpec((1, d), lambda i, s: (0, 0)),
                  pl.BlockSpec(memory_space=pl.ANY)],
        out_specs=[pl.BlockSpec((None, tm, D_MODEL), lambda i, s: (s, i, 0)),
                   pl.BlockSpec((d, tm), lambda i, s: (0, i))],
        out_shape=[jax.ShapeDtypeStruct((N_GROUPS, tokens, D_MODEL), F32),
                   jax.ShapeDtypeStruct((d, tokens), _MXU_DTYPE)],
        scratch_shapes=[pltpu.VMEM((tm, d), _MXU_DTYPE), pltpu.VMEM((N_GROUPS, d, D_MODEL), _MXU_DTYPE)],
        compiler_params=_params(("arbitrary", "arbitrary")),
    )(x2d, norm_g, w_all)


def _lru_gates(xa, cw_ref, cb_ref, wx_ref, bx_ref, wa_ref, ba_ref, lam_ref):
    xc = (cb_ref[...] + cw_ref[3:4, :] * xa + cw_ref[2:3, :] * _shift_down(xa, 1)
          + cw_ref[1:2, :] * _shift_down(xa, 2) + cw_ref[0:1, :] * _shift_down(xa, 3))
    gi = _sigmoid(_mm(xc, wx_ref[...]) + bx_ref[...])
    gr = _sigmoid(_mm(xc, wa_ref[...]) + ba_ref[...])
    sp = _softplus(-lam_ref[...])
    log_a = (-LRU_C) * gr * sp
    a = jnp.exp(log_a)
    mult = jnp.sqrt(-_expm1_neg(2.0 * log_a))
    return xc, gi, gr, sp, a, mult


def _tile_rows():
    return lax.broadcasted_iota(jnp.int32, (SUBLANES, LANES), 0)


def _scan_forward(a_scr, u_scr, h_scr, seq):
    rows = _tile_rows()

    def tile(j, carry):
        sl = pl.ds(pl.multiple_of(j * SUBLANES, SUBLANES), SUBLANES)
        a = a_scr[sl, :]
        u = u_scr[sl, :]
        for d in (1, 2, 4):
            keep = rows >= d
            a_sh = jnp.where(keep, pltpu.roll(a, d, 0), 1.0)
            u_sh = jnp.where(keep, pltpu.roll(u, d, 0), 0.0)
            u = a * u_sh + u
            a = a * a_sh
        h = u + a * carry
        h_scr[sl, :] = h
        return jnp.broadcast_to(h[SUBLANES - 1:SUBLANES, :], (SUBLANES, LANES))

    lax.fori_loop(0, seq // SUBLANES, tile, jnp.zeros((SUBLANES, LANES), F32))


def _scan_backward(c_scr, d_scr, g_scr, seq):
    rows = _tile_rows()
    n_tiles = seq // SUBLANES

    def tile(jj, carry):
        j = n_tiles - 1 - jj
        sl = pl.ds(pl.multiple_of(j * SUBLANES, SUBLANES), SUBLANES)
        c = c_scr[sl, :]
        g = d_scr[sl, :]
        for d in (1, 2, 4):
            keep = rows < SUBLANES - d
            c_sh = jnp.where(keep, pltpu.roll(c, SUBLANES - d, 0), 1.0)
            g_sh = jnp.where(keep, pltpu.roll(g, SUBLANES - d, 0), 0.0)
            g = c * g_sh + g
            c = c * c_sh
        g = g + c * carry
        g_scr[sl, :] = g
        return jnp.broadcast_to(g[0:1, :], (SUBLANES, LANES))

    lax.fori_loop(0, n_tiles, tile, jnp.zeros((SUBLANES, LANES), F32))


def _lru_param_specs(cb_axis):
    def pick(*ids):
        return ids[cb_axis]

    vec = pl.BlockSpec((1, LANES), lambda *ids: (0, pick(*ids)))
    mat = pl.BlockSpec((None, LANES, LANES), lambda *ids: (pick(*ids), 0, 0))
    return [pl.BlockSpec((CONV_WIDTH, LANES), lambda *ids: (0, pick(*ids))), vec, mat, vec, mat, vec, vec]


def _branch_a_fwd(z, conv_w, conv_b, wx, bx, wa, ba, lam, batch, seq):
    tokens = batch * seq

    def body(z_ref, cw_ref, cb_ref, wx_ref, bx_ref, wa_ref, ba_ref, lam_ref, ya_ref, hl_ref, a_scr, u_scr):
        xa = z_ref[0]
        ga = z_ref[1]
        xc, gi, _, _, a, mult = _lru_gates(xa, cw_ref, cb_ref, wx_ref, bx_ref, wa_ref, ba_ref, lam_ref)
        a_scr[...] = a
        u_scr[...] = mult * gi * xc
        _scan_forward(a_scr, u_scr, hl_ref, seq)
        ya_ref[...] = (hl_ref[...] * (ga * _sigmoid(ga))).astype(_MXU_DTYPE)

    blk = pl.BlockSpec((seq, LANES), lambda b, c: (b, c))
    return pl.pallas_call(
        body, name="branch_a_fwd",
        grid=(batch, N_BLK),
        in_specs=[pl.BlockSpec((2, seq, LANES), lambda b, c: (2, b, c))] + _lru_param_specs(1),
        out_specs=[blk, blk],
        out_shape=[jax.ShapeDtypeStruct((tokens, D_MODEL), _MXU_DTYPE), jax.ShapeDtypeStruct((tokens, D_MODEL), F32)],
        scratch_shapes=[pltpu.VMEM((seq, LANES), F32), pltpu.VMEM((seq, LANES), F32)],
        compiler_params=_params(("parallel", "parallel")),
    )(z, conv_w, conv_b, wx, bx, wa, ba, lam)


def _branch_a_bwd(z, hl, dya, dz, conv_w, conv_b, wx, bx, wa, ba, lam, batch, seq):
    def body(z_ref, hl_ref, dya_ref, dz_in_ref, cw_ref, cb_ref, wx_ref, bx_ref, wa_ref, ba_ref, lam_ref,
             dz_ref, dcw_ref, dcb_ref, dwx_ref, dbx_ref, dwa_ref, dba_ref, dlam_ref, c_scr, d_scr, g_scr):
        del dz_in_ref
        xa = z_ref[0]
        ga = z_ref[1]
        hl = hl_ref[...]
        dya = dya_ref[...]
        xc, gi, gr, sp, a, mult = _lru_gates(xa, cw_ref, cb_ref, wx_ref, bx_ref, wa_ref, ba_ref, lam_ref)
        sga = _sigmoid(ga)
        dz_ref[1] = (dya * hl * (sga * (1.0 + ga * (1.0 - sga)))).astype(_MXU_DTYPE)
        c_scr[...] = _shift_up(a, 1)
        d_scr[...] = dya * (ga * sga)
        _scan_backward(c_scr, d_scr, g_scr, seq)
        g = g_scr[...]
        da = g * _shift_down(hl, 1)
        dmult = g * gi * xc
        dgi = g * mult * xc
        dxc = g * mult * gi
        dlog_a = da * a - dmult * (a * a) / mult
        dgr = dlog_a * (-LRU_C) * sp
        dsp = jnp.sum(dlog_a * gr, axis=0, keepdims=True) * (-LRU_C)
        dlam = -dsp * _sigmoid(-lam_ref[...])
        dpi = dgi * gi * (1.0 - gi)
        dpr = dgr * gr * (1.0 - gr)
        dxc = dxc + _mm_nt(dpi, wx_ref[...]) + _mm_nt(dpr, wa_ref[...])
        dwx = _mm_tn(xc, dpi)
        dwa = _mm_tn(xc, dpr)
        dbx = jnp.sum(dpi, axis=0, keepdims=True)
        dba = jnp.sum(dpr, axis=0, keepdims=True)
        dxa = (cw_ref[3:4, :] * dxc + cw_ref[2:3, :] * _shift_up(dxc, 1) + cw_ref[1:2, :] * _shift_up(dxc, 2)
               + cw_ref[0:1, :] * _shift_up(dxc, 3))
        dz_ref[0] = dxa.astype(_MXU_DTYPE)
        dcb = jnp.sum(dxc, axis=0, keepdims=True)
        dcw = [jnp.sum(dxc * _shift_down(xa, CONV_WIDTH - 1 - k), axis=0, keepdims=True) if k < CONV_WIDTH - 1
               else jnp.sum(dxc * xa, axis=0, keepdims=True) for k in range(CONV_WIDTH)]

        @pl.when(pl.program_id(1) == 0)
        def _():
            for k in range(CONV_WIDTH):
                dcw_ref[k:k + 1, :] = dcw[k]
            dcb_ref[...] = dcb
            dwx_ref[...] = dwx
            dbx_ref[...] = dbx
            dwa_ref[...] = dwa
            dba_ref[...] = dba
            dlam_ref[...] = dlam

        @pl.when(pl.program_id(1) != 0)
        def _():
            for k in range(CONV_WIDTH):
                dcw_ref[k:k + 1, :] += dcw[k]
            dcb_ref[...] += dcb
            dwx_ref[...] += dwx
            dbx_ref[...] += dbx
            dwa_ref[...] += dwa
            dba_ref[...] += dba
            dlam_ref[...] += dlam

    tokens = batch * seq
    blk = pl.BlockSpec((seq, LANES), lambda c, b: (b, c))
    vec = pl.BlockSpec((1, LANES), lambda c, b: (0, c))
    mat = pl.BlockSpec((None, LANES, LANES), lambda c, b: (c, 0, 0))
    vec_shape = jax.ShapeDtypeStruct((1, D_MODEL), F32)
    mat_shape = jax.ShapeDtypeStruct((N_BLK, LANES, LANES), F32)
    return pl.pallas_call(
        body, name="branch_a_bwd",
        grid=(N_BLK, batch),
        in_specs=[pl.BlockSpec((2, seq, LANES), lambda c, b: (2, b, c)), blk, blk,
                  pl.BlockSpec(memory_space=pl.ANY)] + _lru_param_specs(0),
        out_specs=[pl.BlockSpec((2, seq, LANES), lambda c, b: (2, b, c)),
                   pl.BlockSpec((CONV_WIDTH, LANES), lambda c, b: (0, c)), vec, mat, vec, mat, vec, vec],
        out_shape=[jax.ShapeDtypeStruct((N_GROUPS, tokens, D_MODEL), _MXU_DTYPE),
                   jax.ShapeDtypeStruct((CONV_WIDTH, D_MODEL), F32), vec_shape, mat_shape, vec_shape, mat_shape,
                   vec_shape, vec_shape],
        scratch_shapes=[pltpu.VMEM((seq, LANES), F32)] * 3,
        input_output_aliases={3: 0},
        compiler_params=_params(("parallel", "arbitrary")),
    )(z, hl, dya, dz, conv_w, conv_b, wx, bx, wa, ba, lam)


def _chunk_masks(transposed=False):
    r = lax.broadcasted_iota(jnp.int32, (CHUNK, CHUNK), 0)
    c = lax.broadcasted_iota(jnp.int32, (CHUNK, CHUNK), 1)
    return r <= c if transposed else r >= c


def _row_blocks(seq, fn):
    block = min(256, seq)

    def trip(i, carry):
        fn(pl.ds(pl.multiple_of(i * block, block), block))
        return carry

    lax.fori_loop(0, seq // block, trip, 0)


def _hgrn_prepare(z_ref, lb_ref, f_scr, logf_scr, qh_scr, seq):
    lb = _sigmoid(lb_ref[0:1, :] - lb_ref[1:2, :])

    def block(rows):
        q = z_ref[0, rows, :]
        f = lb + (1.0 - lb) * _sigmoid(z_ref[1, rows, :])
        f_scr[rows, :] = f
        logf_scr[rows, :] = jnp.log(f)
        qh_scr[rows, :] = q * _sigmoid(q)

    _row_blocks(seq, block)
    return lb


def _cumsum_rows(x, reverse=False):
    shift = _shift_up if reverse else _shift_down
    d = 1
    while d < x.shape[0]:
        x = x + shift(x, d)
        d *= 2
    return x


def _lane_mean(x):
    return jnp.mean(x, axis=-1, keepdims=True)


def _token_contractions(lhs_scr, rhs_scr, out_ref, seq):
    rows_id = lax.broadcasted_iota(jnp.int32, (LANES, LANES), 0)

    def transposed(p):
        rows = pl.ds(pl.multiple_of(p * LANES, LANES), LANES)
        return jnp.transpose(lhs_scr[rows, :]).astype(_MXU_DTYPE), rhs_scr[rows, :]

    def contract(p, s):
        lhs_t, rhs = s
        return (_mm(lhs_t, jnp.where(rows_id < CHUNK, rhs, 0.0)), _mm(lhs_t, jnp.where(rows_id >= CHUNK, rhs, 0.0)))

    def store(p, out):
        out_ref[2 * p] = out[0]
        out_ref[2 * p + 1] = out[1]

    _independent_trips(seq // LANES, [transposed, contract], store)


def _chunk_rows(c):
    return pl.ds(pl.multiple_of(c * CHUNK, CHUNK), CHUNK)


def _chunk_terms(c, z_ref, f_scr, qh_scr, b_scr):
    rows = _chunk_rows(c)
    b = b_scr[rows, :]
    b_mid = b_scr[pl.ds(c * CHUNK + CHUNK // 2, 1), :]
    b_last = b_scr[pl.ds(c * CHUNK + CHUNK - 1, 1), :]
    qh = qh_scr[rows, :]
    k = 1.0 - f_scr[rows, :]
    v = z_ref[2, rows, :]
    e_q = jnp.exp(b - b_mid) * HG_SCALE
    e_k = jnp.exp(b_mid - b)
    e_qi = jnp.exp(b) * HG_SCALE
    e_ks = jnp.exp(b_last - b)
    decay = jnp.exp(b_last)
    return rows, qh, k, v, e_q, e_k, e_qi, e_ks, decay


def _independent_trips(n, stages, store, group=CHUNKS_IN_FLIGHT):
    stages = stages if isinstance(stages, (list, tuple)) else [stages]
    group = min(group, n)

    def trip(g, carry):
        ids = [g * group + i for i in range(group)]
        state = [stages[0](c) for c in ids]
        for stage in stages[1:]:
            state = [stage(c, s) for c, s in zip(ids, state)]
        for c, s in zip(ids, state):
            store(c, s)
        return carry

    lax.fori_loop(0, n // group, trip, 0)


def _branch_b_fwd(z, lb_logits, hg_g, batch, seq):
    tokens = batch * seq
    n_chunks = seq // CHUNK

    def body(z_ref, lb_ref, g_ref, yb_ref, st_ref, f_scr, logf_scr, qh_scr, b_scr, o_scr, qi_scr, ks_scr, dec_scr):
        _hgrn_prepare(z_ref, lb_ref, f_scr, logf_scr, qh_scr, seq)
        causal = _chunk_masks()
        gain = g_ref[...]

        def cumulate(c):
            return _cumsum_rows(logf_scr[_chunk_rows(c), :])

        def store_cumulated(c, b):
            b_scr[_chunk_rows(c), :] = b

        def scores(c):
            _, qh, k, v, e_q, e_k, e_qi, e_ks, decay = _chunk_terms(c, z_ref, f_scr, qh_scr, b_scr)
            return _mm_nt(qh * e_q, k * e_k), v, qh * e_qi, k * e_ks, decay

        def within_chunk(c, s):
            att, v, q_int, k_st, decay = s
            return _mm(jnp.where(causal, att, 0.0), v), q_int, k_st, decay

        def store_within_chunk(c, out):
            rows = _chunk_rows(c)
            o_scr[rows, :], qi_scr[rows, :], ks_scr[rows, :], dec_scr[pl.ds(c, 1), :] = out

        def carry_state(c, state_t):
            update = st_ref[c]
            st_ref[c] = state_t
            return state_t * dec_scr[pl.ds(c, 1), :] + update

        def finish(c):
            rows = _chunk_rows(c)
            o = o_scr[rows, :] + _mm_nt(qi_scr[rows, :], st_ref[c])
            r = lax.rsqrt(_lane_mean(o * o) + EPS)
            gb = z_ref[3, rows, :]
            return (((o * r) * gain) * (gb * _sigmoid(gb))).astype(_MXU_DTYPE)

        def store_finished(c, yb):
            yb_ref[_chunk_rows(c), :] = yb

        _independent_trips(n_chunks, cumulate, store_cumulated)
        _independent_trips(n_chunks, [scores, within_chunk], store_within_chunk)
        _token_contractions(z_ref.at[2], ks_scr, st_ref, seq)
        lax.fori_loop(0, n_chunks, carry_state, jnp.zeros((LANES, LANES), F32))
        _independent_trips(n_chunks, finish, store_finished, group=2 * CHUNKS_IN_FLIGHT)

    seq_buf = pltpu.VMEM((seq, LANES), F32)
    return pl.pallas_call(
        body, name="branch_b_fwd",
        grid=(batch, N_BLK),
        in_specs=[pl.BlockSpec((4, seq, LANES), lambda b, h: (0, b, h)),
                  pl.BlockSpec((2, LANES), lambda b, h: (0, h)),
                  pl.BlockSpec((1, LANES), lambda b, h: (0, 0))],
        out_specs=[pl.BlockSpec((seq, LANES), lambda b, h: (b, h)),
                   pl.BlockSpec((None, n_chunks, LANES, LANES), lambda b, h: (b * N_BLK + h, 0, 0, 0))],
        out_shape=[jax.ShapeDtypeStruct((tokens, D_MODEL), _MXU_DTYPE),
                   jax.ShapeDtypeStruct((batch * N_BLK, n_chunks, LANES, LANES), F32)],
        scratch_shapes=[seq_buf] * 7 + [pltpu.VMEM((n_chunks, LANES), F32)],
        compiler_params=_params(("parallel", "parallel")),
    )(z, lb_logits, hg_g)


def _branch_b_bwd(z, states, dyb, dz, lb_logits, hg_g, batch, seq):
    n_chunks = seq // CHUNK

    def body(z_ref, st_ref, dyb_ref, dz_in_ref, lb_ref, g_ref, dz_ref, dlog_ref, dg_ref,
             f_scr, logf_scr, qh_scr, b_scr, do_scr, qi_scr, dqh_scr, df_scr, dec_scr, dgp_scr, dlb_scr, dst_scr):
        del dz_in_ref
        first = (pl.program_id(0) == 0) & (pl.program_id(1) == 0)
        lb = _hgrn_prepare(z_ref, lb_ref, f_scr, logf_scr, qh_scr, seq)
        causal = _chunk_masks()
        anti_causal = _chunk_masks(transposed=True)
        gain = g_ref[...]

        @pl.when(first)
        def _():
            dg_ref[...] = jnp.zeros_like(dg_ref)

        @pl.when(pl.program_id(1) == 0)
        def _():
            dlb_scr[...] = jnp.zeros_like(dlb_scr)

        def cumulate(c):
            return _cumsum_rows(logf_scr[_chunk_rows(c), :])

        def store_cumulated(c, b):
            b_scr[_chunk_rows(c), :] = b

        def scores(c):
            _, qh, k, v, e_q, e_k, e_qi, e_ks, decay = _chunk_terms(c, z_ref, f_scr, qh_scr, b_scr)
            q_int = qh * e_qi
            return _mm_nt(qh * e_q, k * e_k), _mm_nt(q_int, st_ref[c]), v, q_int, decay

        def output_gradient(c, s):
            att, o_inter, v, q_int, decay = s
            rows = _chunk_rows(c)
            o = _mm(jnp.where(causal, att, 0.0), v) + o_inter
            r = lax.rsqrt(_lane_mean(o * o) + EPS)
            o_n = o * r
            gb = z_ref[3, rows, :]
            sgb = _sigmoid(gb)
            dyb_c = dyb_ref[rows, :]
            d_ong = dyb_c * (gb * sgb)
            d_gb = (dyb_c * (o_n * gain) * (sgb * (1.0 + gb * (1.0 - sgb)))).astype(_MXU_DTYPE)
            d_gain = jnp.sum(d_ong * o_n, axis=0, keepdims=True)
            d_on = d_ong * gain
            return d_gb, d_gain, r * (d_on - o_n * _lane_mean(d_on * o_n)), q_int, decay

        def store_output_gradient(c, out):
            rows = _chunk_rows(c)
            dz_ref[3, rows, :], dgp_scr[pl.ds(c, 1), :], do_scr[rows, :], qi_scr[rows, :], dec_scr[pl.ds(c, 1), :] = out

        def carry_state_gradient(cc, d_state_t):
            c = n_chunks - 1 - cc
            update = dst_scr[c]
            dst_scr[c] = d_state_t
            return d_state_t * dec_scr[pl.ds(c, 1), :] + update

        def score_gradients(c):
            rows, qh, k, v, e_q, e_k, e_qi, e_ks, decay = _chunk_terms(c, z_ref, f_scr, qh_scr, b_scr)
            state_t = st_ref[c]
            d_state_t = dst_scr[c]
            d_o = do_scr[rows, :]
            q_in, k_in, q_int, k_st = qh * e_q, k * e_k, qh * e_qi, k * e_ks
            first = (_mm_nt(k_in, q_in), _mm_nt(d_o, v), _mm_nt(v, d_o), _mm_nt(k_st, d_state_t), _mm(d_o, state_t),
                     _mm(v, d_state_t))
            d_decay = jnp.sum(state_t * d_state_t, axis=0, keepdims=True)
            return first, d_o, q_in, k_in, q_int, k_st, e_q, e_k, e_qi, e_ks, decay, d_decay

        def input_gradients(c, s):
            (att_t, d_att, d_att_t, dv_inter, dq_int, dk_st), d_o, q_in, k_in, q_int, k_st, e_q, e_k, e_qi, e_ks, decay, d_decay = s
            rows = _chunk_rows(c)
            d_v = _mm(jnp.where(anti_causal, att_t, 0.0), d_o) + dv_inter
            dq_in = _mm(jnp.where(causal, d_att, 0.0), k_in)
            dk_in = _mm(jnp.where(anti_causal, d_att_t, 0.0), q_in)
            d_k = dk_in * e_k + dk_st * e_ks
            kk = dk_st * k_st
            d_b = dq_in * q_in + dq_int * q_int - dk_in * k_in - kk
            d_b_last = jnp.sum(kk, axis=0, keepdims=True) + decay * d_decay
            d_logf = _cumsum_rows(d_b, reverse=True) + d_b_last
            return d_v.astype(_MXU_DTYPE), dq_in * e_q + dq_int * e_qi, d_logf / f_scr[rows, :] - d_k

        def store_input_gradients(c, out):
            rows = _chunk_rows(c)
            dz_ref[2, rows, :], dqh_scr[rows, :], df_scr[rows, :] = out

        def input_activations(rows):
            q = z_ref[0, rows, :]
            sq = _sigmoid(q)
            dz_ref[0, rows, :] = (dqh_scr[rows, :] * (sq * (1.0 + q * (1.0 - sq)))).astype(_MXU_DTYPE)
            sg = _sigmoid(z_ref[1, rows, :])
            d_f = df_scr[rows, :]
            dz_ref[1, rows, :] = (d_f * (1.0 - lb) * sg * (1.0 - sg)).astype(_MXU_DTYPE)
            dlb_scr[...] += jnp.sum(d_f * (1.0 - sg), axis=0, keepdims=True)

        _independent_trips(n_chunks, cumulate, store_cumulated)
        _independent_trips(n_chunks, [scores, output_gradient], store_output_gradient)
        _token_contractions(do_scr, qi_scr, dst_scr, seq)
        lax.fori_loop(0, n_chunks, carry_state_gradient, jnp.zeros((LANES, LANES), F32))
        _independent_trips(n_chunks, [score_gradients, input_gradients], store_input_gradients)
        dg_ref[...] += jnp.sum(dgp_scr[...], axis=0, keepdims=True)
        _row_blocks(seq, input_activations)
        d_l0 = dlb_scr[...] * lb * (1.0 - lb)
        dlog_ref[0:1, :] = d_l0
        dlog_ref[1:2, :] = -d_l0

    tokens = batch * seq
    seq_buf = pltpu.VMEM((seq, LANES), F32)
    chunk_rows = pltpu.VMEM((n_chunks, LANES), F32)
    return pl.pallas_call(
        body, name="branch_b_bwd",
        grid=(N_BLK, batch),
        in_specs=[pl.BlockSpec((4, seq, LANES), lambda h, b: (0, b, h)),
                  pl.BlockSpec((None, n_chunks, LANES, LANES), lambda h, b: (b * N_BLK + h, 0, 0, 0)),
                  pl.BlockSpec((seq, LANES), lambda h, b: (b, h)),
                  pl.BlockSpec(memory_space=pl.ANY),
                  pl.BlockSpec((2, LANES), lambda h, b: (0, h)),
                  pl.BlockSpec((1, LANES), lambda h, b: (0, 0))],
        out_specs=[pl.BlockSpec((4, seq, LANES), lambda h, b: (0, b, h)),
                   pl.BlockSpec((2, LANES), lambda h, b: (0, h)),
                   pl.BlockSpec((1, LANES), lambda h, b: (0, 0))],
        out_shape=[jax.ShapeDtypeStruct((N_GROUPS, tokens, D_MODEL), _MXU_DTYPE),
                   jax.ShapeDtypeStruct((2, D_MODEL), F32),
                   jax.ShapeDtypeStruct((1, LANES), F32)],
        scratch_shapes=[seq_buf] * 8 + [chunk_rows, chunk_rows, pltpu.VMEM((1, LANES), F32),
                                        pltpu.VMEM((n_chunks, LANES, LANES), F32)],
        input_output_aliases={3: 0},
        compiler_params=_params(("arbitrary", "arbitrary")),
    )(z, states, dyb, dz, lb_logits, hg_g)


def _merge_tail(ya, yb, z, x2d, tgt2d, b_merge, final_g, pa, pb, wo):
    tokens, d = x2d.shape
    tm = min(256, tokens)
    n_tiles = tokens // tm

    def body(ya_ref, yb_ref, z_ref, x_ref, t_ref, bm_ref, fg_ref, pa_hbm, pb_hbm, wo_hbm,
             dya_ref, dyb_ref, dx2_ref, dz_ref, loss_ref, dfg_ref, dbm_ref, dpa_hbm, dpb_hbm, dwo_hbm,
             pa_s, pb_s, wo_s, dpa_s, dpb_s, dwo_s):
        i = pl.program_id(0)

        @pl.when(i == 0)
        def _():
            pltpu.sync_copy(pa_hbm, pa_s)
            pltpu.sync_copy(pb_hbm, pb_s)
            pltpu.sync_copy(wo_hbm, wo_s)
            dpa_s[...] = jnp.zeros_like(dpa_s)
            dpb_s[...] = jnp.zeros_like(dpb_s)
            dwo_s[...] = jnp.zeros_like(dwo_s)
            loss_ref[...] = jnp.zeros_like(loss_ref)
            dfg_ref[...] = jnp.zeros_like(dfg_ref)
            dbm_ref[...] = jnp.zeros_like(dbm_ref)

        ya_t = ya_ref[...]
        yb_t = yb_ref[...]
        out_a = _mm(ya_t, pa_s[...])
        out_b = _mm(yb_t, pb_s[...])
        g_a = _sigmoid(z_ref[0] + bm_ref[:, :d])
        g_b = _sigmoid(z_ref[1] + bm_ref[:, d:])
        mixed = g_a * out_a + g_b * out_b
        x2 = x_ref[...] + _mm(mixed, wo_s[...])
        r = lax.rsqrt(jnp.mean(x2 * x2, axis=-1, keepdims=True) + EPS)
        xn = x2 * r
        fg = fg_ref[...]
        diff = xn * fg - t_ref[...]
        loss_ref[...] += jnp.sum(diff * diff) * (0.5 / d)
        dy = diff * (1.0 / d)
        dfg_ref[...] += jnp.sum(dy * xn, axis=0, keepdims=True)
        dxn = dy * fg
        dx2 = r * (dxn - xn * jnp.mean(dxn * xn, axis=-1, keepdims=True))
        dx2_ref[...] = dx2
        dmixed = _mm_nt(dx2, wo_s[...])
        dwo_s[...] += _mm_tn(mixed, dx2)
        dgm_a = dmixed * out_a * g_a * (1.0 - g_a)
        dgm_b = dmixed * out_b * g_b * (1.0 - g_b)
        dz_ref[0] = dgm_a.astype(_MXU_DTYPE)
        dz_ref[1] = dgm_b.astype(_MXU_DTYPE)
        dbm_ref[:, :d] += jnp.sum(dgm_a, axis=0, keepdims=True)
        dbm_ref[:, d:] += jnp.sum(dgm_b, axis=0, keepdims=True)
        dout_a = dmixed * g_a
        dout_b = dmixed * g_b
        dpa_s[...] += _mm_tn(ya_t, dout_a)
        dpb_s[...] += _mm_tn(yb_t, dout_b)
        dya_ref[...] = _mm_nt(dout_a, pa_s[...])
        dyb_ref[...] = _mm_nt(dout_b, pb_s[...])

        @pl.when(i == n_tiles - 1)
        def _():
            pltpu.sync_copy(dpa_s, dpa_hbm)
            pltpu.sync_copy(dpb_s, dpb_hbm)
            pltpu.sync_copy(dwo_s, dwo_hbm)

    tile = pl.BlockSpec((tm, d), lambda i: (i, 0))
    gm = pl.BlockSpec((2, tm, d), lambda i: (3, i, 0))
    row = lambda n: pl.BlockSpec((1, n), lambda i: (0, 0))
    hbm = pl.BlockSpec(memory_space=pl.ANY)
    act = jax.ShapeDtypeStruct((tokens, d), F32)
    mat = jax.ShapeDtypeStruct((d, d), F32)
    return pl.pallas_call(
        body, name="merge_tail",
        grid=(n_tiles,),
        in_specs=[tile, tile, gm, tile, tile, row(2 * d), row(d), hbm, hbm, hbm],
        out_specs=[tile, tile, tile, gm, row(LANES), row(d), row(2 * d), hbm, hbm, hbm],
        out_shape=[act, act, act, jax.ShapeDtypeStruct((N_GROUPS, tokens, d), _MXU_DTYPE),
                   jax.ShapeDtypeStruct((1, LANES), F32), jax.ShapeDtypeStruct((1, d), F32),
                   jax.ShapeDtypeStruct((1, 2 * d), F32), mat, mat, mat],
        scratch_shapes=[pltpu.VMEM((d, d), _MXU_DTYPE)] * 3 + [pltpu.VMEM((d, d), F32)] * 3,
        compiler_params=_params(("arbitrary",)),
    )(ya, yb, z, x2d, tgt2d, b_merge, final_g, pa, pb, wo)


def _inproj_dw(h_t, dz):
    d, tokens = h_t.shape
    tm = min(512, tokens)

    def body(h_ref, dz_ref, dw_ref):
        part = _mm(h_ref[...], dz_ref[...])

        @pl.when(pl.program_id(1) == 0)
        def _():
            dw_ref[...] = part

        @pl.when(pl.program_id(1) != 0)
        def _():
            dw_ref[...] += part

    def out_index(s, i):
        g = _group_of_slot(s)
        return (g // 2, 0, g % 2)

    return pl.pallas_call(
        body, name="inproj_dw",
        grid=(N_GROUPS, tokens // tm),
        in_specs=[pl.BlockSpec((d, tm), lambda s, i: (0, i)),
                  pl.BlockSpec((None, tm, D_MODEL), lambda s, i: (s, i, 0))],
        out_specs=pl.BlockSpec((None, d, D_MODEL), out_index),
        out_shape=jax.ShapeDtypeStruct((N_SHARDS, d, 2 * D_MODEL), F32),
        compiler_params=_params(("parallel", "arbitrary")),
    )(h_t, dz)


def _inproj_dx(dz, w_all, x2d, dx2, norm_g):
    tokens, d = x2d.shape
    tm = min(512, tokens)

    def body(dz_ref, w_hbm, x_ref, dx2_ref, g_ref, dx_ref, dg_ref, acc, w_res):
        s = pl.program_id(1)

        @pl.when((pl.program_id(0) == 0) & (s == 0))
        def _():
            _load_w_in_by_slot(w_hbm, w_res)

        part = _mm_nt(dz_ref[...], w_res[s])

        @pl.when(s == 0)
        def _():
            acc[...] = part

        @pl.when(s != 0)
        def _():
            acc[...] += part

        @pl.when((pl.program_id(0) == 0) & (s == 0))
        def _():
            dg_ref[...] = jnp.zeros_like(dg_ref)

        @pl.when(s == N_GROUPS - 1)
        def _():
            x = x_ref[...]
            r = lax.rsqrt(jnp.mean(x * x, axis=-1, keepdims=True) + EPS)
            xn = x * r
            dh = acc[...]
            dg_ref[...] += jnp.sum(dh * xn, axis=0, keepdims=True)
            dxn = dh * g_ref[...]
            dx_ref[...] = r * (dxn - xn * jnp.mean(dxn * xn, axis=-1, keepdims=True)) + dx2_ref[...]

    tile = pl.BlockSpec((tm, d), lambda i, s: (i, 0))
    return pl.pallas_call(
        body, name="inproj_dx",
        grid=(tokens // tm, N_GROUPS),
        in_specs=[pl.BlockSpec((None, tm, D_MODEL), lambda i, s: (s, i, 0)),
                  pl.BlockSpec(memory_space=pl.ANY), tile, tile,
                  pl.BlockSpec((1, d), lambda i, s: (0, 0))],
        out_specs=[tile, pl.BlockSpec((1, d), lambda i, s: (0, 0))],
        out_shape=[jax.ShapeDtypeStruct((tokens, d), F32), jax.ShapeDtypeStruct((1, d), F32)],
        scratch_shapes=[pltpu.VMEM((tm, d), F32), pltpu.VMEM((N_GROUPS, d, D_MODEL), _MXU_DTYPE)],
        compiler_params=_params(("arbitrary", "arbitrary")),
    )(dz, w_all, x2d, dx2, norm_g)


def _row_tile(rows, cols, itemsize=4, budget=2 * 1024 * 1024):
    tr = rows
    while tr * cols * itemsize > budget and tr % 16 == 0:
        tr //= 2
    return tr


def _cast_into_slot(a, chip, dtype, name):
    rows, cols = a.shape
    tr = _row_tile(rows, cols)

    def body(chip_ref, a_ref, o_ref):
        del chip_ref
        o_ref[...] = a_ref[...].astype(dtype)

    grid_spec = pltpu.PrefetchScalarGridSpec(
        num_scalar_prefetch=1, grid=(rows // tr,),
        in_specs=[pl.BlockSpec((tr, cols), lambda i, chip_ref: (i, 0))],
        out_specs=pl.BlockSpec((None, tr, cols), lambda i, chip_ref: (chip_ref[0], i, 0)))
    return pl.pallas_call(body, name=name, grid_spec=grid_spec,
                          out_shape=jax.ShapeDtypeStruct((N_SHARDS, rows, cols), dtype),
                          compiler_params=_params(("arbitrary",)))(chip, a)


def _sum_slots(stack, name):
    n, rows, cols = stack.shape
    tr = _row_tile(rows, cols * n)

    def body(s_ref, o_ref):
        total = s_ref[0].astype(F32)
        for k in range(1, n):
            total = total + s_ref[k].astype(F32)
        o_ref[...] = total

    return pl.pallas_call(body, name=name, grid=(rows // tr,),
                          in_specs=[pl.BlockSpec((n, tr, cols), lambda i: (0, i, 0))],
                          out_specs=pl.BlockSpec((tr, cols), lambda i: (i, 0)),
                          out_shape=jax.ShapeDtypeStruct((rows, cols), F32),
                          compiler_params=_params(("parallel",)))(stack)


def _add_half(full, landed, place, name):
    n, rows, cols = full.shape
    half = rows // 2
    tr = _row_tile(half, cols)
    nb = half // tr

    def body(place_ref, a_ref, b_ref, o_ref, own_ref):
        total = (a_ref[...] + b_ref[...]).astype(_MXU_DTYPE)
        o_ref[...] = total

        @pl.when(pl.program_id(1) == place_ref[1])
        def _():
            own_ref[...] = total

    grid_spec = pltpu.PrefetchScalarGridSpec(
        num_scalar_prefetch=1, grid=(nb, n),
        in_specs=[pl.BlockSpec((None, tr, cols), lambda i, j, place_ref: (j, place_ref[0] * nb + i, 0)),
                  pl.BlockSpec((None, tr, cols), lambda i, j, place_ref: (j, i, 0))],
        out_specs=[pl.BlockSpec((None, tr, cols), lambda i, j, place_ref: (j, i, 0)),
                   pl.BlockSpec((None, tr, cols), lambda i, j, place_ref: (place_ref[1], i, 0))])
    shape = jax.ShapeDtypeStruct((n, half, cols), _MXU_DTYPE)
    return pl.pallas_call(body, name=name, grid_spec=grid_spec, out_shape=[shape, shape],
                          compiler_params=_params(("parallel", "arbitrary")))(place, full, landed)


def _adamw_update(w, grad, m, v):
    c1 = 1.0 - ADAM_B1 ** ADAM_STEP
    c2 = 1.0 - ADAM_B2 ** ADAM_STEP
    nm = ADAM_B1 * m + (1.0 - ADAM_B1) * grad
    nv = ADAM_B2 * v + (1.0 - ADAM_B2) * (grad * grad)
    return (-ADAM_LR) * ((nm / c1) / (jnp.sqrt(nv / c2) + ADAM_EPS) + ADAM_WD * w), nm, nv


def _adamw(w, g, m, v, name):
    rows, cols = w.shape
    tr = _row_tile(rows, cols, budget=1024 * 1024)

    def body(w_ref, g_ref, m_ref, v_ref, d_ref, nm_ref, nv_ref):
        d_ref[...], nm_ref[...], nv_ref[...] = _adamw_update(w_ref[...], g_ref[...], m_ref[...], v_ref[...])

    spec = pl.BlockSpec((tr, cols), lambda i: (i, 0))
    shape = jax.ShapeDtypeStruct((rows, cols), F32)
    return pl.pallas_call(body, name=name, grid=(rows // tr,), in_specs=[spec] * 4, out_specs=[spec] * 3,
                          out_shape=[shape] * 3, compiler_params=_params(("parallel",)))(w, g, m, v)


def _adamw_halves(w, g_mine, g_sibling, m, v, core, name):
    rows, cols = w.shape
    half = rows // 2
    tr = _row_tile(half, cols, budget=1024 * 1024)
    nb = half // tr

    def body(core_ref, w_ref, gm_ref, gs_ref, m_ref, v_ref, g_ref, d_ref, nm_ref, nv_ref):
        mine = pl.program_id(0) // nb == core_ref[0]
        grad = jnp.where(mine, gm_ref[...], gs_ref[...])
        g_ref[...] = grad
        d_ref[...], nm_ref[...], nv_ref[...] = _adamw_update(w_ref[...], grad, m_ref[...], v_ref[...])

    spec = pl.BlockSpec((tr, cols), lambda i, core_ref: (i, 0))
    half_spec = pl.BlockSpec((tr, cols), lambda i, core_ref: (i % nb, 0))
    grid_spec = pltpu.PrefetchScalarGridSpec(num_scalar_prefetch=1, grid=(rows // tr,),
                                             in_specs=[spec, half_spec, half_spec, spec, spec], out_specs=[spec] * 4)
    shape = jax.ShapeDtypeStruct((rows, cols), F32)
    return pl.pallas_call(body, name=name, grid_spec=grid_spec, out_shape=[shape] * 4,
                          compiler_params=_params(("parallel",)))(core, w, g_mine, g_sibling, m, v)


def _local_step(x, loss_target, w_all, pa, pb, wo, conv_w, b_merge, conv_b, rg_wx, rg_bx, rg_wa, rg_ba,
                rg_lambda, hg_lb_logits, hg_norm_g, norm_g, final_norm_g):
    batch, seq, d = x.shape
    x2d = x.reshape(batch * seq, d)
    tgt2d = loss_target.reshape(batch * seq, d)
    z, h_t = _inproj_fwd(x2d, norm_g, w_all)
    lru = (conv_w, conv_b, rg_wx, rg_bx, rg_wa, rg_ba, rg_lambda)
    ya, hl = _branch_a_fwd(z, *lru, batch, seq)
    yb, states = _branch_b_fwd(z, hg_lb_logits, hg_norm_g, batch, seq)
    dya, dyb, dx2, dz, loss, d_final_g, d_b_merge, d_pa, d_pb, d_wo = _merge_tail(
        ya, yb, z, x2d, tgt2d, b_merge, final_norm_g, pa, pb, wo)
    dz, d_lb_logits, d_hg_g = _branch_b_bwd(z, states, dyb, dz, hg_lb_logits, hg_norm_g, batch, seq)
    dz, d_conv_w, d_conv_b, d_wx, d_bx, d_wa, d_ba, d_lam = _branch_a_bwd(z, hl, dya, dz, *lru, batch, seq)
    d_w_in = _inproj_dw(h_t, dz)
    grad_x, d_norm_g = _inproj_dx(dz, w_all, x2d, dx2, norm_g)
    big = (d_w_in, d_pa, d_pb, d_wo)
    small = dict(b_merge=d_b_merge, conv_w=d_conv_w, conv_b=d_conv_b, rg_wx=d_wx, rg_bx=d_bx, rg_wa=d_wa,
                 rg_ba=d_ba, rg_lambda=d_lam, hg_lb_logits=d_lb_logits, hg_norm_g=d_hg_g, norm_g=d_norm_g,
                 final_norm_g=d_final_g)
    return loss[0, 0], grad_x.reshape(batch, seq, d), big, small


_SMALL_ORDER = ("b_merge", "conv_w", "conv_b", "rg_wx", "rg_bx", "rg_wa", "rg_ba", "rg_lambda", "hg_lb_logits",
                "hg_norm_g", "norm_g", "final_norm_g")
N_DEV = 8
PIECE_ROWS = 272


def _pack_small(tree):
    flat = jnp.concatenate([tree[k].reshape(-1) for k in _SMALL_ORDER])
    flat = jnp.pad(flat, (0, N_DEV * PIECE_ROWS * LANES - flat.shape[0]))
    return flat.reshape(N_DEV * PIECE_ROWS, LANES)


def _unpack_small(packed, like):
    flat = packed.reshape(-1)
    out, pos = {}, 0
    for k in _SMALL_ORDER:
        n = like[k].size
        out[k] = flat[pos:pos + n].reshape(like[k].shape)
        pos += n
    return out


def _mesh_position():
    x, y, c = lax.axis_index("x"), lax.axis_index("y"), lax.axis_index("c")
    other_chips = [(1 - x, y), (x, 1 - y), (1 - x, 1 - y)]
    return x, y, c, other_chips


def _other_devices(x, y, c):
    flips = [(fx, fy, fc) for fx in (0, 1) for fy in (0, 1) for fc in (0, 1) if (fx, fy, fc) != (0, 0, 0)]
    return [(jnp.where(fx, 1 - x, x), jnp.where(fy, 1 - y, y), jnp.where(fc, 1 - c, c)) for fx, fy, fc in flips]


def _remote(src, dst, send_sems, recv_sems, k, device):
    return pltpu.make_async_remote_copy(src_ref=src, dst_ref=dst, send_sem=send_sems.at[k], recv_sem=recv_sems.at[k],
                                        device_id=device, device_id_type=MESH)


def _gather_weights(slotted, conv_slotted):
    n_big = len(slotted)
    n_sem = 6 * n_big + 3

    def body(*refs):
        bufs, cw = refs[n_big + 1:2 * n_big + 1], refs[2 * n_big + 1]
        send_sems, recv_sems = refs[2 * n_big + 2:]
        x, y, c, chips = _mesh_position()
        me, sibling = 2 * x + y, (x, y, 1 - c)

        def half(buf, slot, which):
            hs = buf.shape[1] // 2
            return buf.at[slot, pl.ds(which * hs, hs), :]

        sends = []
        for a in range(n_big):
            for j, (cx, cy) in enumerate(chips):
                mine = half(bufs[a], me, c)
                sends.append(_remote(mine, mine, send_sems, recv_sems, 6 * a + j, (cx, cy, c)))
        for j, (cx, cy) in enumerate(chips):
            sends.append(_remote(cw.at[me], cw.at[me], send_sems, recv_sems, 6 * n_big + j, (cx, cy, c)))
        for cp in sends:
            cp.start()
        for j, (cx, cy) in enumerate(chips):
            theirs = 2 * cx + cy
            for a in range(n_big):
                landed = half(bufs[a], theirs, c)
                _remote(landed, landed, send_sems, recv_sems, 6 * a + j, (cx, cy, c)).wait_recv()
                passed = _remote(landed, landed, send_sems, recv_sems, 6 * a + 3 + j, sibling)
                passed.start()
                sends.append(passed)
        for j, (cx, cy) in enumerate(chips):
            theirs = 2 * cx + cy
            _remote(cw.at[theirs], cw.at[theirs], send_sems, recv_sems, 6 * n_big + j, (cx, cy, c)).wait_recv()
            for a in range(n_big):
                landed = half(bufs[a], theirs, 1 - c)
                _remote(landed, landed, send_sems, recv_sems, 6 * a + 3 + j, sibling).wait_recv()
        for cp in sends:
            cp.wait_send()

    hbm = pl.BlockSpec(memory_space=pl.ANY)
    operands = list(slotted) + [conv_slotted]
    return pl.pallas_call(
        body, name="gather_weights",
        in_specs=[hbm] * (n_big + 1), out_specs=[hbm] * (n_big + 1),
        out_shape=[jax.ShapeDtypeStruct(a.shape, a.dtype) for a in operands],
        scratch_shapes=[pltpu.SemaphoreType.DMA((n_sem,)), pltpu.SemaphoreType.DMA((n_sem,))],
        input_output_aliases={i: i for i in range(n_big + 1)},
    )(*operands)


def _exchange_halves(bigs, small):
    n_big = len(bigs)
    n_sem = n_big + N_DEV - 1

    def body(*refs):
        srcs, small_src = refs[:n_big], refs[n_big]
        outs, small_out = refs[n_big + 1:2 * n_big + 1], refs[2 * n_big + 1]
        send_sems, recv_sems, local_sem = refs[2 * n_big + 2:]
        x, y, c, _ = _mesh_position()
        me, sibling = 4 * x + 2 * y + c, (x, y, 1 - c)
        mine = pltpu.make_async_copy(small_src.at[pl.ds(me * PIECE_ROWS, PIECE_ROWS), :], small_out.at[me], local_sem)
        mine.start()
        copies = []
        for a in range(n_big):
            hs = srcs[a].shape[1] // 2
            copies.append(_remote(srcs[a].at[:, pl.ds((1 - c) * hs, hs), :], outs[a], send_sems, recv_sems, a, sibling))
        for k, (px, py, pc) in enumerate(_other_devices(x, y, c)):
            piece = small_src.at[pl.ds((4 * px + 2 * py + pc) * PIECE_ROWS, PIECE_ROWS), :]
            copies.append(_remote(piece, small_out.at[me], send_sems, recv_sems, n_big + k, (px, py, pc)))
        for cp in copies:
            cp.start()
        for cp in copies:
            cp.wait()
        mine.wait()

    hbm = pl.BlockSpec(memory_space=pl.ANY)
    out_shape = [jax.ShapeDtypeStruct((g.shape[0], g.shape[1] // 2, g.shape[2]), F32) for g in bigs]
    out_shape.append(jax.ShapeDtypeStruct((N_DEV, PIECE_ROWS, LANES), F32))
    return pl.pallas_call(
        body, name="exchange_halves",
        in_specs=[hbm] * (n_big + 1), out_specs=[hbm] * (n_big + 1), out_shape=out_shape,
        scratch_shapes=[pltpu.SemaphoreType.DMA((n_sem,)), pltpu.SemaphoreType.DMA((n_sem,)), pltpu.SemaphoreType.DMA],
    )(*bigs, small)


def _scatter_shards(bigs, by_chip, small_piece):
    n_big = len(bigs)
    n_sem = 3 * n_big + N_DEV - 1

    def body(*refs):
        srcs, small_src = refs[:n_big], refs[2 * n_big]
        outs, small_out = refs[2 * n_big + 1:3 * n_big + 1], refs[3 * n_big + 1]
        send_sems, recv_sems, local_sem = refs[3 * n_big + 2:]
        x, y, c, chips = _mesh_position()
        chip, me = 2 * x + y, 4 * x + 2 * y + c
        mine = pltpu.make_async_copy(small_src, small_out.at[me], local_sem)
        mine.start()
        copies = []
        for a in range(n_big):
            for j, (cx, cy) in enumerate(chips):
                copies.append(_remote(srcs[a].at[2 * cx + cy], outs[a].at[chip], send_sems, recv_sems, 3 * a + j, (cx, cy, c)))
        for k, peer in enumerate(_other_devices(x, y, c)):
            copies.append(_remote(small_src, small_out.at[me], send_sems, recv_sems, 3 * n_big + k, peer))
        for cp in copies:
            cp.start()
        for cp in copies:
            cp.wait()
        mine.wait()

    hbm = pl.BlockSpec(memory_space=pl.ANY)
    out_shape = [jax.ShapeDtypeStruct(g.shape, g.dtype) for g in by_chip]
    out_shape.append(jax.ShapeDtypeStruct((N_DEV, PIECE_ROWS, LANES), F32))
    return pl.pallas_call(
        body, name="scatter_shards",
        in_specs=[hbm] * (2 * n_big + 1), out_specs=[hbm] * (n_big + 1), out_shape=out_shape,
        scratch_shapes=[pltpu.SemaphoreType.DMA((n_sem,)), pltpu.SemaphoreType.DMA((n_sem,)), pltpu.SemaphoreType.DMA],
        input_output_aliases={n_big + a: a for a in range(n_big)},
    )(*bigs, *by_chip, small_piece)


def _swap_halves(halves):
    n_big = len(halves)

    def body(*refs):
        srcs, outs = refs[:n_big], refs[n_big:2 * n_big]
        send_sems, recv_sems = refs[2 * n_big:]
        x, y, c, _ = _mesh_position()
        copies = [_remote(srcs[a], outs[a], send_sems, recv_sems, a, (x, y, 1 - c)) for a in range(n_big)]
        for cp in copies:
            cp.start()
        for cp in copies:
            cp.wait()

    hbm = pl.BlockSpec(memory_space=pl.ANY)
    return pl.pallas_call(
        body, name="swap_halves",
        in_specs=[hbm] * n_big, out_specs=[hbm] * n_big,
        out_shape=[jax.ShapeDtypeStruct(h.shape, F32) for h in halves],
        scratch_shapes=[pltpu.SemaphoreType.DMA((n_big,)), pltpu.SemaphoreType.DMA((n_big,))],
    )(*halves)


def _reduce_gradients(bigs, small_tree):
    core = lax.axis_index("c").astype(jnp.int32)
    chip = (2 * lax.axis_index("x") + lax.axis_index("y")).astype(jnp.int32)
    place = jnp.stack([core, chip])
    *landed, small_landed = _exchange_halves(bigs, _pack_small(small_tree))
    sums = [_add_half(g, l, place, f"add_half_{a}") for a, (g, l) in enumerate(zip(bigs, landed))]
    small_piece = _sum_slots(small_landed, "sum_small")
    *by_chip, small_all = _scatter_shards([s[0] for s in sums], [s[1] for s in sums], small_piece)
    mine = [_sum_slots(s, f"sum_chips_{a}") for a, s in enumerate(by_chip)]
    theirs = _swap_halves(mine)
    return mine, theirs, _unpack_small(small_all, small_tree)


def kernel(x, w_in, b_merge, conv_w, conv_b, rg_wx, rg_bx, rg_wa, rg_ba, rg_lambda, hg_lb_logits, hg_norm_g, proj_a, proj_b, w_out, norm_g, final_norm_g, loss_target, m_w_in, m_b_merge, m_conv_w, m_conv_b, m_rg_wx, m_rg_bx, m_rg_wa, m_rg_ba, m_rg_lambda, m_hg_lb_logits, m_hg_norm_g, m_proj_a, m_proj_b, m_w_out, m_norm_g, m_final_norm_g, v_w_in, v_b_merge, v_conv_w, v_conv_b, v_rg_wx, v_rg_bx, v_rg_wa, v_rg_ba, v_rg_lambda, v_hg_lb_logits, v_hg_norm_g, v_proj_a, v_proj_b, v_w_out, v_norm_g, v_final_norm_g):
    d = D_MODEL
    weights = dict(w_in=w_in, b_merge=b_merge, conv_w=conv_w, conv_b=conv_b, rg_wx=rg_wx, rg_bx=rg_bx, rg_wa=rg_wa,
                   rg_ba=rg_ba, rg_lambda=rg_lambda, hg_lb_logits=hg_lb_logits, hg_norm_g=hg_norm_g, proj_a=proj_a,
                   proj_b=proj_b, w_out=w_out, norm_g=norm_g, final_norm_g=final_norm_g)
    m = dict(w_in=m_w_in, b_merge=m_b_merge, conv_w=m_conv_w, conv_b=m_conv_b, rg_wx=m_rg_wx, rg_bx=m_rg_bx,
             rg_wa=m_rg_wa, rg_ba=m_rg_ba, rg_lambda=m_rg_lambda, hg_lb_logits=m_hg_lb_logits, hg_norm_g=m_hg_norm_g,
             proj_a=m_proj_a, proj_b=m_proj_b, w_out=m_w_out, norm_g=m_norm_g, final_norm_g=m_final_norm_g)
    v = dict(w_in=v_w_in, b_merge=v_b_merge, conv_w=v_conv_w, conv_b=v_conv_b, rg_wx=v_rg_wx, rg_bx=v_rg_bx,
             rg_wa=v_rg_wa, rg_ba=v_rg_ba, rg_lambda=v_rg_lambda, hg_lb_logits=v_hg_lb_logits, hg_norm_g=v_hg_norm_g,
             proj_a=v_proj_a, proj_b=v_proj_b, w_out=v_w_out, norm_g=v_norm_g, final_norm_g=v_final_norm_g)
    big_names = ("w_in", "proj_a", "proj_b", "w_out")

    core = lax.axis_index("c").astype(jnp.int32).reshape(1)
    chip = (2 * lax.axis_index("x") + lax.axis_index("y")).astype(jnp.int32)

    slotted = [_cast_into_slot(weights[k][0], chip.reshape(1), _MXU_DTYPE, f"cast_{k}") for k in big_names]
    conv_slotted = _cast_into_slot(conv_w[0], chip.reshape(1), F32, "slot_conv_w")
    w_all, pa_all, pb_all, wo_all, cw_all = _gather_weights(slotted, conv_slotted)
    conv_w_full = jnp.transpose(cw_all, (1, 0, 2)).reshape(CONV_WIDTH, d)

    loss_part, grad_x, big_grads, small_grads = _local_step(
        x, loss_target, w_all, pa_all.reshape(d, d), pb_all.reshape(d, d), wo_all.reshape(d, d), conv_w_full,
        b_merge, conv_b, rg_wx[0], rg_bx.reshape(1, d), rg_wa[0], rg_ba.reshape(1, d), rg_lambda, hg_lb_logits,
        hg_norm_g, norm_g, final_norm_g.reshape(1, d))
    loss = lax.psum(loss_part, ("x", "y", "c"))

    big_grads = [big_grads[0]] + [g.reshape(N_SHARDS, d // N_SHARDS, d) for g in big_grads[1:]]
    mine, theirs, small_red = _reduce_gradients(big_grads, small_grads)

    grads, delta, new_m, new_v = {}, {}, {}, {}
    for k, g_mine, g_theirs in zip(big_names, mine, theirs):
        out = _adamw_halves(weights[k][0], g_mine, g_theirs, m[k][0], v[k][0], core, f"adamw_{k}")
        grads[k], delta[k], new_m[k], new_v[k] = (t.reshape(weights[k].shape) for t in out)
    cols = d // N_SHARDS
    g_conv = lax.dynamic_slice(small_red["conv_w"], (0, chip * cols), (CONV_WIDTH, cols))
    grads["conv_w"] = g_conv.reshape(conv_w.shape)
    dl, nm, nv = _adamw(conv_w[0], g_conv, m_conv_w[0], v_conv_w[0], "adamw_conv_w")
    delta["conv_w"], new_m["conv_w"], new_v["conv_w"] = (t.reshape(conv_w.shape) for t in (dl, nm, nv))
    rest = [k for k in _SMALL_ORDER if k != "conv_w"]
    like = {k: (weights[k] if k != "conv_w" else jnp.zeros((CONV_WIDTH, d), F32)) for k in _SMALL_ORDER}
    packs = [_pack_small({k: (t[k] if k != "conv_w" else like[k]) for k in _SMALL_ORDER}) for t in (weights, m, v)]
    g_pack = _pack_small({k: small_red[k].reshape(like[k].shape) for k in _SMALL_ORDER})
    outs = [_unpack_small(p, like) for p in _adamw(packs[0], g_pack, packs[1], packs[2], "adamw_small")]
    for k in rest:
        grads[k] = small_red[k].reshape(weights[k].shape)
        delta[k], new_m[k], new_v[k] = outs[0][k], outs[1][k], outs[2][k]

    order = ("w_in", "b_merge", "conv_w", "conv_b", "rg_wx", "rg_bx", "rg_wa", "rg_ba", "rg_lambda", "hg_lb_logits",
             "hg_norm_g", "proj_a", "proj_b", "w_out", "norm_g", "final_norm_g")
    return (loss, grad_x, *[grads[k] for k in order], *[delta[k] for k in order], *[new_m[k] for k in order],
            *[new_v[k] for k in order])
```

```python
import functools

import jax
import jax.numpy as jnp
from jax import lax
from jax.experimental import pallas as pl
from jax.experimental.pallas import tpu as pltpu

F32 = jnp.float32
_MXU_DTYPE = jnp.bfloat16

D_MODEL = 1024
LANES = 128
SUBLANES = 8
N_BLK = D_MODEL // LANES
N_GROUPS = 8
N_SHARDS = 4
CONV_WIDTH = 4
LRU_C = 8.0
CHUNK = 64
CHUNKS_IN_FLIGHT = 4
HG_SCALE = float(LANES) ** -0.5
EPS = 1e-6
ADAM_LR, ADAM_B1, ADAM_B2, ADAM_EPS, ADAM_WD, ADAM_STEP = 0.001, 0.9, 0.999, 1e-08, 0.01, 10
VMEM_LIMIT = 56 * 1024 * 1024
MESH = pl.DeviceIdType.MESH

_SLOT_TO_GROUP = (2, 3, 4, 5, 0, 1, 6, 7)


def _group_of_slot(s):
    return jnp.where(s < 4, s + 2, jnp.where(s < 6, s - 4, s))


def _mm(a, b):
    return lax.dot_general(a.astype(_MXU_DTYPE), b.astype(_MXU_DTYPE), (((1,), (0,)), ((), ())),
                           preferred_element_type=F32)


def _mm_nt(a, b):
    return lax.dot_general(a.astype(_MXU_DTYPE), b.astype(_MXU_DTYPE), (((1,), (1,)), ((), ())),
                           preferred_element_type=F32)


def _mm_tn(a, b):
    return lax.dot_general(a.astype(_MXU_DTYPE), b.astype(_MXU_DTYPE), (((0,), (0,)), ((), ())),
                           preferred_element_type=F32)


def _sigmoid(x):
    return 1.0 / (1.0 + jnp.exp(-x))


def _log1p_pos(y):
    series = y * (1.0 - y * (0.5 - y * (1.0 / 3.0 - y * 0.25)))
    return jnp.where(y < 0.01, series, jnp.log(1.0 + y))


def _expm1_neg(y):
    series = y * (1.0 + y * 0.5 * (1.0 + y * (1.0 / 3.0) * (1.0 + y * 0.25 * (1.0 + y * 0.2))))
    return jnp.where(y > -0.02, series, jnp.exp(y) - 1.0)


def _softplus(x):
    return jnp.maximum(x, 0.0) + _log1p_pos(jnp.exp(-jnp.abs(x)))


def _shift_down(x, n):
    rows = lax.broadcasted_iota(jnp.int32, x.shape, 0)
    return jnp.where(rows >= n, pltpu.roll(x, n, 0), 0.0)


def _shift_up(x, n):
    size = x.shape[0]
    rows = lax.broadcasted_iota(jnp.int32, x.shape, 0)
    return jnp.where(rows < size - n, pltpu.roll(x, size - n, 0), 0.0)


def _params(dims, vmem=VMEM_LIMIT):
    return pltpu.CompilerParams(dimension_semantics=dims, vmem_limit_bytes=vmem)


def _load_w_in_by_slot(w_hbm, w_res):
    for slot, g in enumerate(_SLOT_TO_GROUP):
        pltpu.sync_copy(w_hbm.at[g // 2, :, pl.ds((g % 2) * D_MODEL, D_MODEL)], w_res.at[slot])


def _inproj_fwd(x2d, norm_g, w_all):
    tokens, d = x2d.shape
    tm = min(512, tokens)

    def body(x_ref, g_ref, w_hbm, z_ref, ht_ref, h_scr, w_res):
        @pl.when((pl.program_id(0) == 0) & (pl.program_id(1) == 0))
        def _():
            _load_w_in_by_slot(w_hbm, w_res)

        @pl.when(pl.program_id(1) == 0)
        def _():
            x = x_ref[...]
            r = lax.rsqrt(jnp.mean(x * x, axis=-1, keepdims=True) + EPS)
            h = (x * r) * g_ref[...]
            h_scr[...] = h.astype(_MXU_DTYPE)
            ht_ref[...] = jnp.transpose(h).astype(_MXU_DTYPE)

        z_ref[...] = _mm(h_scr[...], w_res[pl.program_id(1)])

    return pl.pallas_call(
        body, name="inproj_fwd",
        grid=(tokens // tm, N_GROUPS),
        in_specs=[pl.BlockSpec((tm, d), lambda i, s: (i, 0)),
                  pl.BlockSpec((1, d), lambda i, s: (0, 0)),
                  pl.BlockSpec(memory_space=pl.ANY)],
        out_specs=[pl.BlockSpec((None, tm, D_MODEL), lambda i, s: (s, i, 0)),
                   pl.BlockSpec((d, tm), lambda i, s: (0, i))],
        out_shape=[jax.ShapeDtypeStruct((N_GROUPS, tokens, D_MODEL), F32),
                   jax.ShapeDtypeStruct((d, tokens), _MXU_DTYPE)],
        scratch_shapes=[pltpu.VMEM((tm, d), _MXU_DTYPE), pltpu.VMEM((N_GROUPS, d, D_MODEL), _MXU_DTYPE)],
        compiler_params=_params(("arbitrary", "arbitrary")),
    )(x2d, norm_g, w_all)


def _lru_gates(xa, cw_ref, cb_ref, wx_ref, bx_ref, wa_ref, ba_ref, lam_ref):
    xc = (cb_ref[...] + cw_ref[3:4, :] * xa + cw_ref[2:3, :] * _shift_down(xa, 1)
          + cw_ref[1:2, :] * _shift_down(xa, 2) + cw_ref[0:1, :] * _shift_down(xa, 3))
    gi = _sigmoid(_mm(xc, wx_ref[...]) + bx_ref[...])
    gr = _sigmoid(_mm(xc, wa_ref[...]) + ba_ref[...])
    sp = _softplus(-lam_ref[...])
    log_a = (-LRU_C) * gr * sp
    a = jnp.exp(log_a)
    mult = jnp.sqrt(-_expm1_neg(2.0 * log_a))
    return xc, gi, gr, sp, a, mult


def _tile_rows():
    return lax.broadcasted_iota(jnp.int32, (SUBLANES, LANES), 0)


def _scan_forward(a_scr, u_scr, h_scr, seq):
    rows = _tile_rows()

    def tile(j, carry):
        sl = pl.ds(pl.multiple_of(j * SUBLANES, SUBLANES), SUBLANES)
        a = a_scr[sl, :]
        u = u_scr[sl, :]
        for d in (1, 2, 4):
            keep = rows >= d
            a_sh = jnp.where(keep, pltpu.roll(a, d, 0), 1.0)
            u_sh = jnp.where(keep, pltpu.roll(u, d, 0), 0.0)
            u = a * u_sh + u
            a = a * a_sh
        h = u + a * carry
        h_scr[sl, :] = h
        return jnp.broadcast_to(h[SUBLANES - 1:SUBLANES, :], (SUBLANES, LANES))

    lax.fori_loop(0, seq // SUBLANES, tile, jnp.zeros((SUBLANES, LANES), F32))


def _scan_backward(c_scr, d_scr, g_scr, seq):
    rows = _tile_rows()
    n_tiles = seq // SUBLANES

    def tile(jj, carry):
        j = n_tiles - 1 - jj
        sl = pl.ds(pl.multiple_of(j * SUBLANES, SUBLANES), SUBLANES)
        c = c_scr[sl, :]
        g = d_scr[sl, :]
        for d in (1, 2, 4):
            keep = rows < SUBLANES - d
            c_sh = jnp.where(keep, pltpu.roll(c, SUBLANES - d, 0), 1.0)
            g_sh = jnp.where(keep, pltpu.roll(g, SUBLANES - d, 0), 0.0)
            g = c * g_sh + g
            c = c * c_sh
        g = g + c * carry
        g_scr[sl, :] = g
        return jnp.broadcast_to(g[0:1, :], (SUBLANES, LANES))

    lax.fori_loop(0, n_tiles, tile, jnp.zeros((SUBLANES, LANES), F32))


def _lru_param_specs(cb_axis):
    def pick(*ids):
        return ids[cb_axis]

    vec = pl.BlockSpec((1, LANES), lambda *ids: (0, pick(*ids)))
    mat = pl.BlockSpec((None, LANES, LANES), lambda *ids: (pick(*ids), 0, 0))
    return [pl.BlockSpec((CONV_WIDTH, LANES), lambda *ids: (0, pick(*ids))), vec, mat, vec, mat, vec, vec]


def _branch_a_fwd(z, conv_w, conv_b, wx, bx, wa, ba, lam, batch, seq):
    tokens = batch * seq

    def body(z_ref, cw_ref, cb_ref, wx_ref, bx_ref, wa_ref, ba_ref, lam_ref, ya_ref, hl_ref, a_scr, u_scr):
        xa = z_ref[0]
        ga = z_ref[1]
        xc, gi, _, _, a, mult = _lru_gates(xa, cw_ref, cb_ref, wx_ref, bx_ref, wa_ref, ba_ref, lam_ref)
        a_scr[...] = a
        u_scr[...] = mult * gi * xc
        _scan_forward(a_scr, u_scr, hl_ref, seq)
        ya_ref[...] = (hl_ref[...] * (ga * _sigmoid(ga))).astype(_MXU_DTYPE)

    blk = pl.BlockSpec((seq, LANES), lambda b, c: (b, c))
    return pl.pallas_call(
        body, name="branch_a_fwd",
        grid=(batch, N_BLK),
        in_specs=[pl.BlockSpec((2, seq, LANES), lambda b, c: (2, b, c))] + _lru_param_specs(1),
        out_specs=[blk, blk],
        out_shape=[jax.ShapeDtypeStruct((tokens, D_MODEL), _MXU_DTYPE), jax.ShapeDtypeStruct((tokens, D_MODEL), F32)],
        scratch_shapes=[pltpu.VMEM((seq, LANES), F32), pltpu.VMEM((seq, LANES), F32)],
        compiler_params=_params(("parallel", "parallel")),
    )(z, conv_w, conv_b, wx, bx, wa, ba, lam)


def _branch_a_bwd(z, hl, dya, dz, conv_w, conv_b, wx, bx, wa, ba, lam, batch, seq):
    def body(z_ref, hl_ref, dya_ref, dz_in_ref, cw_ref, cb_ref, wx_ref, bx_ref, wa_ref, ba_ref, lam_ref,
             dz_ref, dcw_ref, dcb_ref, dwx_ref, dbx_ref, dwa_ref, dba_ref, dlam_ref, c_scr, d_scr, g_scr):
        del dz_in_ref
        xa = z_ref[0]
        ga = z_ref[1]
        hl = hl_ref[...]
        dya = dya_ref[...]
        xc, gi, gr, sp, a, mult = _lru_gates(xa, cw_ref, cb_ref, wx_ref, bx_ref, wa_ref, ba_ref, lam_ref)
        sga = _sigmoid(ga)
        dz_ref[1] = (dya * hl * (sga * (1.0 + ga * (1.0 - sga)))).astype(_MXU_DTYPE)
        c_scr[...] = _shift_up(a, 1)
        d_scr[...] = dya * (ga * sga)
        _scan_backward(c_scr, d_scr, g_scr, seq)
        g = g_scr[...]
        da = g * _shift_down(hl, 1)
        dmult = g * gi * xc
        dgi = g * mult * xc
        dxc = g * mult * gi
        dlog_a = da * a - dmult * (a * a) / mult
        dgr = dlog_a * (-LRU_C) * sp
        dsp = jnp.sum(dlog_a * gr, axis=0, keepdims=True) * (-LRU_C)
        dlam = -dsp * _sigmoid(-lam_ref[...])
        dpi = dgi * gi * (1.0 - gi)
        dpr = dgr * gr * (1.0 - gr)
        dxc = dxc + _mm_nt(dpi, wx_ref[...]) + _mm_nt(dpr, wa_ref[...])
        dwx = _mm_tn(xc, dpi)
        dwa = _mm_tn(xc, dpr)
        dbx = jnp.sum(dpi, axis=0, keepdims=True)
        dba = jnp.sum(dpr, axis=0, keepdims=True)
        dxa = (cw_ref[3:4, :] * dxc + cw_ref[2:3, :] * _shift_up(dxc, 1) + cw_ref[1:2, :] * _shift_up(dxc, 2)
               + cw_ref[0:1, :] * _shift_up(dxc, 3))
        dz_ref[0] = dxa.astype(_MXU_DTYPE)
        dcb = jnp.sum(dxc, axis=0, keepdims=True)
        dcw = [jnp.sum(dxc * _shift_down(xa, CONV_WIDTH - 1 - k), axis=0, keepdims=True) if k < CONV_WIDTH - 1
               else jnp.sum(dxc * xa, axis=0, keepdims=True) for k in range(CONV_WIDTH)]

        @pl.when(pl.program_id(1) == 0)
        def _():
            for k in range(CONV_WIDTH):
                dcw_ref[k:k + 1, :] = dcw[k]
            dcb_ref[...] = dcb
            dwx_ref[...] = dwx
            dbx_ref[...] = dbx
            dwa_ref[...] = dwa
            dba_ref[...] = dba
            dlam_ref[...] = dlam

        @pl.when(pl.program_id(1) != 0)
        def _():
            for k in range(CONV_WIDTH):
                dcw_ref[k:k + 1, :] += dcw[k]
            dcb_ref[...] += dcb
            dwx_ref[...] += dwx
            dbx_ref[...] += dbx
            dwa_ref[...] += dwa
            dba_ref[...] += dba
            dlam_ref[...] += dlam

    tokens = batch * seq
    blk = pl.BlockSpec((seq, LANES), lambda c, b: (b, c))
    vec = pl.BlockSpec((1, LANES), lambda c, b: (0, c))
    mat = pl.BlockSpec((None, LANES, LANES), lambda c, b: (c, 0, 0))
    vec_shape = jax.ShapeDtypeStruct((1, D_MODEL), F32)
    mat_shape = jax.ShapeDtypeStruct((N_BLK, LANES, LANES), F32)
    return pl.pallas_call(
        body, name="branch_a_bwd",
        grid=(N_BLK, batch),
        in_specs=[pl.BlockSpec((2, seq, LANES), lambda c, b: (2, b, c)), blk, blk,
                  pl.BlockSpec(memory_space=pl.ANY)] + _lru_param_specs(0),
        out_specs=[pl.BlockSpec((2, seq, LANES), lambda c, b: (2, b, c)),
                   pl.BlockSpec((CONV_WIDTH, LANES), lambda c, b: (0, c)), vec, mat, vec, mat, vec, vec],
        out_shape=[jax.ShapeDtypeStruct((N_GROUPS, tokens, D_MODEL), _MXU_DTYPE),
                   jax.ShapeDtypeStruct((CONV_WIDTH, D_MODEL), F32), vec_shape, mat_shape, vec_shape, mat_shape,
                   vec_shape, vec_shape],
        scratch_shapes=[pltpu.VMEM((seq, LANES), F32)] * 3,
        input_output_aliases={3: 0},
        compiler_params=_params(("parallel", "arbitrary")),
    )(z, hl, dya, dz, conv_w, conv_b, wx, bx, wa, ba, lam)


def _chunk_masks(transposed=False):
    r = lax.broadcasted_iota(jnp.int32, (CHUNK, CHUNK), 0)
    c = lax.broadcasted_iota(jnp.int32, (CHUNK, CHUNK), 1)
    return r <= c if transposed else r >= c


def _row_blocks(seq, fn):
    block = min(256, seq)

    def trip(i, carry):
        fn(pl.ds(pl.multiple_of(i * block, block), block))
        return carry

    lax.fori_loop(0, seq // block, trip, 0)


def _hgrn_prepare(z_ref, lb_ref, f_scr, logf_scr, qh_scr, seq):
    lb = _sigmoid(lb_ref[0:1, :] - lb_ref[1:2, :])

    def block(rows):
        q = z_ref[0, rows, :]
        f = lb + (1.0 - lb) * _sigmoid(z_ref[1, rows, :])
        f_scr[rows, :] = f
        logf_scr[rows, :] = jnp.log(f)
        qh_scr[rows, :] = q * _sigmoid(q)

    _row_blocks(seq, block)
    return lb


def _cumsum_rows(x, reverse=False):
    shift = _shift_up if reverse else _shift_down
    d = 1
    while d < x.shape[0]:
        x = x + shift(x, d)
        d *= 2
    return x


def _lane_mean(x):
    return jnp.mean(x, axis=-1, keepdims=True)


def _token_contractions(lhs_scr, rhs_scr, out_ref, seq):
    rows_id = lax.broadcasted_iota(jnp.int32, (LANES, LANES), 0)

    def transposed(p):
        rows = pl.ds(pl.multiple_of(p * LANES, LANES), LANES)
        return jnp.transpose(lhs_scr[rows, :]).astype(_MXU_DTYPE), rhs_scr[rows, :]

    def contract(p, s):
        lhs_t, rhs = s
        return (_mm(lhs_t, jnp.where(rows_id < CHUNK, rhs, 0.0)), _mm(lhs_t, jnp.where(rows_id >= CHUNK, rhs, 0.0)))

    def store(p, out):
        out_ref[2 * p] = out[0]
        out_ref[2 * p + 1] = out[1]

    _independent_trips(seq // LANES, [transposed, contract], store)


def _chunk_rows(c):
    return pl.ds(pl.multiple_of(c * CHUNK, CHUNK), CHUNK)


def _chunk_terms(c, z_ref, f_scr, qh_scr, b_scr):
    rows = _chunk_rows(c)
    b = b_scr[rows, :]
    b_mid = b_scr[pl.ds(c * CHUNK + CHUNK // 2, 1), :]
    b_last = b_scr[pl.ds(c * CHUNK + CHUNK - 1, 1), :]
    qh = qh_scr[rows, :]
    k = 1.0 - f_scr[rows, :]
    v = z_ref[2, rows, :]
    e_q = jnp.exp(b - b_mid) * HG_SCALE
    e_k = jnp.exp(b_mid - b)
    e_qi = jnp.exp(b) * HG_SCALE
    e_ks = jnp.exp(b_last - b)
    decay = jnp.exp(b_last)
    return rows, qh, k, v, e_q, e_k, e_qi, e_ks, decay


def _independent_trips(n, stages, store, group=CHUNKS_IN_FLIGHT):
    stages = stages if isinstance(stages, (list, tuple)) else [stages]
    group = min(group, n)

    def trip(g, carry):
        ids = [g * group + i for i in range(group)]
        state = [stages[0](c) for c in ids]
        for stage in stages[1:]:
            state = [stage(c, s) for c, s in zip(ids, state)]
        for c, s in zip(ids, state):
            store(c, s)
        return carry

    lax.fori_loop(0, n // group, trip, 0)


def _branch_b_fwd(z, lb_logits, hg_g, batch, seq):
    tokens = batch * seq
    n_chunks = seq // CHUNK

    def body(z_ref, lb_ref, g_ref, yb_ref, st_ref, f_scr, logf_scr, qh_scr, b_scr, o_scr, qi_scr, ks_scr, dec_scr):
        _hgrn_prepare(z_ref, lb_ref, f_scr, logf_scr, qh_scr, seq)
        causal = _chunk_masks()
        gain = g_ref[...]

        def cumulate(c):
            return _cumsum_rows(logf_scr[_chunk_rows(c), :])

        def store_cumulated(c, b):
            b_scr[_chunk_rows(c), :] = b

        def scores(c):
            _, qh, k, v, e_q, e_k, e_qi, e_ks, decay = _chunk_terms(c, z_ref, f_scr, qh_scr, b_scr)
            return _mm_nt(qh * e_q, k * e_k), v, qh * e_qi, k * e_ks, decay

        def within_chunk(c, s):
            att, v, q_int, k_st, decay = s
            return _mm(jnp.where(causal, att, 0.0), v), q_int, k_st, decay

        def store_within_chunk(c, out):
            rows = _chunk_rows(c)
            o_scr[rows, :], qi_scr[rows, :], ks_scr[rows, :], dec_scr[pl.ds(c, 1), :] = out

        def carry_state(c, state_t):
            update = st_ref[c]
            st_ref[c] = state_t
            return state_t * dec_scr[pl.ds(c, 1), :] + update

        def finish(c):
            rows = _chunk_rows(c)
            o = o_scr[rows, :] + _mm_nt(qi_scr[rows, :], st_ref[c])
            r = lax.rsqrt(_lane_mean(o * o) + EPS)
            gb = z_ref[3, rows, :]
            return (((o * r) * gain) * (gb * _sigmoid(gb))).astype(_MXU_DTYPE)

        def store_finished(c, yb):
            yb_ref[_chunk_rows(c), :] = yb

        _independent_trips(n_chunks, cumulate, store_cumulated)
        _independent_trips(n_chunks, [scores, within_chunk], store_within_chunk)
        _token_contractions(z_ref.at[2], ks_scr, st_ref, seq)
        lax.fori_loop(0, n_chunks, carry_state, jnp.zeros((LANES, LANES), F32))
        _independent_trips(n_chunks, finish, store_finished, group=2 * CHUNKS_IN_FLIGHT)

    seq_buf = pltpu.VMEM((seq, LANES), F32)
    return pl.pallas_call(
        body, name="branch_b_fwd",
        grid=(batch, N_BLK),
        in_specs=[pl.BlockSpec((4, seq, LANES), lambda b, h: (0, b, h)),
                  pl.BlockSpec((2, LANES), lambda b, h: (0, h)),
                  pl.BlockSpec((1, LANES), lambda b, h: (0, 0))],
        out_specs=[pl.BlockSpec((seq, LANES), lambda b, h: (b, h)),
                   pl.BlockSpec((None, n_chunks, LANES, LANES), lambda b, h: (b * N_BLK + h, 0, 0, 0))],
        out_shape=[jax.ShapeDtypeStruct((tokens, D_MODEL), _MXU_DTYPE),
                   jax.ShapeDtypeStruct((batch * N_BLK, n_chunks, LANES, LANES), F32)],
        scratch_shapes=[seq_buf] * 7 + [pltpu.VMEM((n_chunks, LANES), F32)],
        compiler_params=_params(("parallel", "parallel")),
    )(z, lb_logits, hg_g)


def _branch_b_bwd(z, states, dyb, dz, lb_logits, hg_g, batch, seq):
    n_chunks = seq // CHUNK

    def body(z_ref, st_ref, dyb_ref, dz_in_ref, lb_ref, g_ref, dz_ref, dlog_ref, dg_ref,
             f_scr, logf_scr, qh_scr, b_scr, do_scr, qi_scr, dqh_scr, df_scr, dec_scr, dgp_scr, dlb_scr, dst_scr):
        del dz_in_ref
        first = (pl.program_id(0) == 0) & (pl.program_id(1) == 0)
        lb = _hgrn_prepare(z_ref, lb_ref, f_scr, logf_scr, qh_scr, seq)
        causal = _chunk_masks()
        anti_causal = _chunk_masks(transposed=True)
        gain = g_ref[...]

        @pl.when(first)
        def _():
            dg_ref[...] = jnp.zeros_like(dg_ref)

        @pl.when(pl.program_id(1) == 0)
        def _():
            dlb_scr[...] = jnp.zeros_like(dlb_scr)

        def cumulate(c):
            return _cumsum_rows(logf_scr[_chunk_rows(c), :])

        def store_cumulated(c, b):
            b_scr[_chunk_rows(c), :] = b

        def scores(c):
            _, qh, k, v, e_q, e_k, e_qi, e_ks, decay = _chunk_terms(c, z_ref, f_scr, qh_scr, b_scr)
            q_int = qh * e_qi
            return _mm_nt(qh * e_q, k * e_k), _mm_nt(q_int, st_ref[c]), v, q_int, decay

        def output_gradient(c, s):
            att, o_inter, v, q_int, decay = s
            rows = _chunk_rows(c)
            o = _mm(jnp.where(causal, att, 0.0), v) + o_inter
            r = lax.rsqrt(_lane_mean(o * o) + EPS)
            o_n = o * r
            gb = z_ref[3, rows, :]
            sgb = _sigmoid(gb)
            dyb_c = dyb_ref[rows, :]
            d_ong = dyb_c * (gb * sgb)
            d_gb = (dyb_c * (o_n * gain) * (sgb * (1.0 + gb * (1.0 - sgb)))).astype(_MXU_DTYPE)
            d_gain = jnp.sum(d_ong * o_n, axis=0, keepdims=True)
            d_on = d_ong * gain
            return d_gb, d_gain, r * (d_on - o_n * _lane_mean(d_on * o_n)), q_int, decay

        def store_output_gradient(c, out):
            rows = _chunk_rows(c)
            dz_ref[3, rows, :], dgp_scr[pl.ds(c, 1), :], do_scr[rows, :], qi_scr[rows, :], dec_scr[pl.ds(c, 1), :] = out

        def carry_state_gradient(cc, d_state_t):
            c = n_chunks - 1 - cc
            update = dst_scr[c]
            dst_scr[c] = d_state_t
            return d_state_t * dec_scr[pl.ds(c, 1), :] + update

        def score_gradients(c):
            rows, qh, k, v, e_q, e_k, e_qi, e_ks, decay = _chunk_terms(c, z_ref, f_scr, qh_scr, b_scr)
            state_t = st_ref[c]
            d_state_t = dst_scr[c]
            d_o = do_scr[rows, :]
            q_in, k_in, q_int, k_st = qh * e_q, k * e_k, qh * e_qi, k * e_ks
            first = (_mm_nt(k_in, q_in), _mm_nt(d_o, v), _mm_nt(v, d_o), _mm_nt(k_st, d_state_t), _mm(d_o, state_t),
                     _mm(v, d_state_t))
            d_decay = jnp.sum(state_t * d_state_t, axis=0, keepdims=True)
            return first, d_o, q_in, k_in, q_int, k_st, e_q, e_k, e_qi, e_ks, decay, d_decay

        def input_gradients(c, s):
            (att_t, d_att, d_att_t, dv_inter, dq_int, dk_st), d_o, q_in, k_in, q_int, k_st, e_q, e_k, e_qi, e_ks, decay, d_decay = s
            rows = _chunk_rows(c)
            d_v = _mm(jnp.where(anti_causal, att_t, 0.0), d_o) + dv_inter
            dq_in = _mm(jnp.where(causal, d_att, 0.0), k_in)
            dk_in = _mm(jnp.where(anti_causal, d_att_t, 0.0), q_in)
            d_k = dk_in * e_k + dk_st * e_ks
            kk = dk_st * k_st
            d_b = dq_in * q_in + dq_int * q_int - dk_in * k_in - kk
            d_b_last = jnp.sum(kk, axis=0, keepdims=True) + decay * d_decay
            d_logf = _cumsum_rows(d_b, reverse=True) + d_b_last
            return d_v.astype(_MXU_DTYPE), dq_in * e_q + dq_int * e_qi, d_logf / f_scr[rows, :] - d_k

        def store_input_gradients(c, out):
            rows = _chunk_rows(c)
            dz_ref[2, rows, :], dqh_scr[rows, :], df_scr[rows, :] = out

        def input_activations(rows):
            q = z_ref[0, rows, :]
            sq = _sigmoid(q)
            dz_ref[0, rows, :] = (dqh_scr[rows, :] * (sq * (1.0 + q * (1.0 - sq)))).astype(_MXU_DTYPE)
            sg = _sigmoid(z_ref[1, rows, :])
            d_f = df_scr[rows, :]
            dz_ref[1, rows, :] = (d_f * (1.0 - lb) * sg * (1.0 - sg)).astype(_MXU_DTYPE)
            dlb_scr[...] += jnp.sum(d_f * (1.0 - sg), axis=0, keepdims=True)

        _independent_trips(n_chunks, cumulate, store_cumulated)
        _independent_trips(n_chunks, [scores, output_gradient], store_output_gradient)
        _token_contractions(do_scr, qi_scr, dst_scr, seq)
        lax.fori_loop(0, n_chunks, carry_state_gradient, jnp.zeros((LANES, LANES), F32))
        _independent_trips(n_chunks, [score_gradients, input_gradients], store_input_gradients)
        dg_ref[...] += jnp.sum(dgp_scr[...], axis=0, keepdims=True)
        _row_blocks(seq, input_activations)
        d_l0 = dlb_scr[...] * lb * (1.0 - lb)
        dlog_ref[0:1, :] = d_l0
        dlog_ref[1:2, :] = -d_l0

    tokens = batch * seq
    seq_buf = pltpu.VMEM((seq, LANES), F32)
    chunk_rows = pltpu.VMEM((n_chunks, LANES), F32)
    return pl.pallas_call(
        body, name="branch_b_bwd",
        grid=(N_BLK, batch),
        in_specs=[pl.BlockSpec((4, seq, LANES), lambda h, b: (0, b, h)),
                  pl.BlockSpec((None, n_chunks, LANES, LANES), lambda h, b: (b * N_BLK + h, 0, 0, 0)),
                  pl.BlockSpec((seq, LANES), lambda h, b: (b, h)),
                  pl.BlockSpec(memory_space=pl.ANY),
                  pl.BlockSpec((2, LANES), lambda h, b: (0, h)),
                  pl.BlockSpec((1, LANES), lambda h, b: (0, 0))],
        out_specs=[pl.BlockSpec((4, seq, LANES), lambda h, b: (0, b, h)),
                   pl.BlockSpec((2, LANES), lambda h, b: (0, h)),
                   pl.BlockSpec((1, LANES), lambda h, b: (0, 0))],
        out_shape=[jax.ShapeDtypeStruct((N_GROUPS, tokens, D_MODEL), _MXU_DTYPE),
                   jax.ShapeDtypeStruct((2, D_MODEL), F32),
                   jax.ShapeDtypeStruct((1, LANES), F32)],
        scratch_shapes=[seq_buf] * 8 + [chunk_rows, chunk_rows, pltpu.VMEM((1, LANES), F32),
                                        pltpu.VMEM((n_chunks, LANES, LANES), F32)],
        input_output_aliases={3: 0},
        compiler_params=_params(("arbitrary", "arbitrary")),
    )(z, states, dyb, dz, lb_logits, hg_g)


def _merge_tail(ya, yb, z, x2d, tgt2d, b_merge, final_g, pa, pb, wo):
    tokens, d = x2d.shape
    tm = min(256, tokens)
    n_tiles = tokens // tm

    def body(ya_ref, yb_ref, z_ref, x_ref, t_ref, bm_ref, fg_ref, pa_hbm, pb_hbm, wo_hbm,
             dya_ref, dyb_ref, dx2_ref, dz_ref, loss_ref, dfg_ref, dbm_ref, dpa_hbm, dpb_hbm, dwo_hbm,
             pa_s, pb_s, wo_s, dpa_s, dpb_s, dwo_s):
        i = pl.program_id(0)

        @pl.when(i == 0)
        def _():
            pltpu.sync_copy(pa_hbm, pa_s)
            pltpu.sync_copy(pb_hbm, pb_s)
            pltpu.sync_copy(wo_hbm, wo_s)
            dpa_s[...] = jnp.zeros_like(dpa_s)
            dpb_s[...] = jnp.zeros_like(dpb_s)
            dwo_s[...] = jnp.zeros_like(dwo_s)
            loss_ref[...] = jnp.zeros_like(loss_ref)
            dfg_ref[...] = jnp.zeros_like(dfg_ref)
            dbm_ref[...] = jnp.zeros_like(dbm_ref)

        ya_t = ya_ref[...]
        yb_t = yb_ref[...]
        out_a = _mm(ya_t, pa_s[...])
        out_b = _mm(yb_t, pb_s[...])
        g_a = _sigmoid(z_ref[0] + bm_ref[:, :d])
        g_b = _sigmoid(z_ref[1] + bm_ref[:, d:])
        mixed = g_a * out_a + g_b * out_b
        x2 = x_ref[...] + _mm(mixed, wo_s[...])
        r = lax.rsqrt(jnp.mean(x2 * x2, axis=-1, keepdims=True) + EPS)
        xn = x2 * r
        fg = fg_ref[...]
        diff = xn * fg - t_ref[...]
        loss_ref[...] += jnp.sum(diff * diff) * (0.5 / d)
        dy = diff * (1.0 / d)
        dfg_ref[...] += jnp.sum(dy * xn, axis=0, keepdims=True)
        dxn = dy * fg
        dx2 = r * (dxn - xn * jnp.mean(dxn * xn, axis=-1, keepdims=True))
        dx2_ref[...] = dx2
        dmixed = _mm_nt(dx2, wo_s[...])
        dwo_s[...] += _mm_tn(mixed, dx2)
        dgm_a = dmixed * out_a * g_a * (1.0 - g_a)
        dgm_b = dmixed * out_b * g_b * (1.0 - g_b)
        dz_ref[0] = dgm_a.astype(_MXU_DTYPE)
        dz_ref[1] = dgm_b.astype(_MXU_DTYPE)
        dbm_ref[:, :d] += jnp.sum(dgm_a, axis=0, keepdims=True)
        dbm_ref[:, d:] += jnp.sum(dgm_b, axis=0, keepdims=True)
        dout_a = dmixed * g_a
        dout_b = dmixed * g_b
        dpa_s[...] += _mm_tn(ya_t, dout_a)
        dpb_s[...] += _mm_tn(yb_t, dout_b)
        dya_ref[...] = _mm_nt(dout_a, pa_s[...])
        dyb_ref[...] = _mm_nt(dout_b, pb_s[...])

        @pl.when(i == n_tiles - 1)
        def _():
            pltpu.sync_copy(dpa_s, dpa_hbm)
            pltpu.sync_copy(dpb_s, dpb_hbm)
            pltpu.sync_copy(dwo_s, dwo_hbm)

    tile = pl.BlockSpec((tm, d), lambda i: (i, 0))
    gm = pl.BlockSpec((2, tm, d), lambda i: (3, i, 0))
    row = lambda n: pl.BlockSpec((1, n), lambda i: (0, 0))
    hbm = pl.BlockSpec(memory_space=pl.ANY)
    act = jax.ShapeDtypeStruct((tokens, d), F32)
    mat = jax.ShapeDtypeStruct((d, d), F32)
    return pl.pallas_call(
        body, name="merge_tail",
        grid=(n_tiles,),
        in_specs=[tile, tile, gm, tile, tile, row(2 * d), row(d), hbm, hbm, hbm],
        out_specs=[tile, tile, tile, gm, row(LANES), row(d), row(2 * d), hbm, hbm, hbm],
        out_shape=[act, act, act, jax.ShapeDtypeStruct((N_GROUPS, tokens, d), _MXU_DTYPE),
                   jax.ShapeDtypeStruct((1, LANES), F32), jax.ShapeDtypeStruct((1, d), F32),
                   jax.ShapeDtypeStruct((1, 2 * d), F32), mat, mat, mat],
        scratch_shapes=[pltpu.VMEM((d, d), _MXU_DTYPE)] * 3 + [pltpu.VMEM((d, d), F32)] * 3,
        compiler_params=_params(("arbitrary",)),
    )(ya, yb, z, x2d, tgt2d, b_merge, final_g, pa, pb, wo)


def _inproj_dw(h_t, dz):
    d, tokens = h_t.shape
    tm = min(512, tokens)

    def body(h_ref, dz_ref, dw_ref):
        part = _mm(h_ref[...], dz_ref[...])

        @pl.when(pl.program_id(1) == 0)
        def _():
            dw_ref[...] = part

        @pl.when(pl.program_id(1) != 0)
        def _():
            dw_ref[...] += part

    def out_index(s, i):
        g = _group_of_slot(s)
        return (g // 2, 0, g % 2)

    return pl.pallas_call(
        body, name="inproj_dw",
        grid=(N_GROUPS, tokens // tm),
        in_specs=[pl.BlockSpec((d, tm), lambda s, i: (0, i)),
                  pl.BlockSpec((None, tm, D_MODEL), lambda s, i: (s, i, 0))],
        out_specs=pl.BlockSpec((None, d, D_MODEL), out_index),
        out_shape=jax.ShapeDtypeStruct((N_SHARDS, d, 2 * D_MODEL), F32),
        compiler_params=_params(("parallel", "arbitrary")),
    )(h_t, dz)


def _inproj_dx(dz, w_all, x2d, dx2, norm_g, scatter=None):
    tokens, d = x2d.shape
    tm = min(512, tokens)
    n_tiles = tokens // tm
    n_big = len(scatter[0]) if scatter else 0

    def body(dz_ref, w_hbm, x_ref, dx2_ref, g_ref, *rest):
        if scatter:
            srcs, small_src = rest[:n_big], rest[2 * n_big]
            dx_ref, dg_ref = rest[2 * n_big + 1:2 * n_big + 3]
            outs, small_out = rest[2 * n_big + 3:3 * n_big + 3], rest[3 * n_big + 3]
            acc, w_res, send_sems, recv_sems, local_sem = rest[3 * n_big + 4:]
            copies = _scatter_copies(srcs, small_src, outs, small_out, send_sems, recv_sems, local_sem)
        else:
            dx_ref, dg_ref, acc, w_res = rest
            copies = []
        s = pl.program_id(1)

        @pl.when((pl.program_id(0) == 0) & (s == 0))
        def _():
            for cp in copies:
                cp.start()
            _load_w_in_by_slot(w_hbm, w_res)

        part = _mm_nt(dz_ref[...], w_res[s])

        @pl.when(s == 0)
        def _():
            acc[...] = part

        @pl.when(s != 0)
        def _():
            acc[...] += part

        @pl.when((pl.program_id(0) == 0) & (s == 0))
        def _():
            dg_ref[...] = jnp.zeros_like(dg_ref)

        @pl.when(s == N_GROUPS - 1)
        def _():
            x = x_ref[...]
            r = lax.rsqrt(jnp.mean(x * x, axis=-1, keepdims=True) + EPS)
            xn = x * r
            dh = acc[...]
            dg_ref[...] += jnp.sum(dh * xn, axis=0, keepdims=True)
            dxn = dh * g_ref[...]
            dx_ref[...] = r * (dxn - xn * jnp.mean(dxn * xn, axis=-1, keepdims=True)) + dx2_ref[...]

        @pl.when((pl.program_id(0) == n_tiles - 1) & (s == N_GROUPS - 1))
        def _():
            for cp in copies:
                cp.wait()

    tile = pl.BlockSpec((tm, d), lambda i, s: (i, 0))
    hbm = pl.BlockSpec(memory_space=pl.ANY)
    in_specs = [pl.BlockSpec((None, tm, D_MODEL), lambda i, s: (s, i, 0)), hbm, tile, tile,
                pl.BlockSpec((1, d), lambda i, s: (0, 0))]
    out_specs = [tile, pl.BlockSpec((1, d), lambda i, s: (0, 0))]
    out_shape = [jax.ShapeDtypeStruct((tokens, d), F32), jax.ShapeDtypeStruct((1, d), F32)]
    scratch = [pltpu.VMEM((tm, d), F32), pltpu.VMEM((N_GROUPS, d, D_MODEL), _MXU_DTYPE)]
    operands, aliases = [dz, w_all, x2d, dx2, norm_g], {}
    if scatter:
        bigs, by_chip, small_piece = scatter
        n_sem = 3 * n_big + N_DEV - 1
        in_specs += [hbm] * (2 * n_big + 1)
        out_specs += [hbm] * (n_big + 1)
        out_shape += [jax.ShapeDtypeStruct(g.shape, g.dtype) for g in by_chip]
        out_shape.append(jax.ShapeDtypeStruct((N_DEV, PIECE_ROWS, LANES), F32))
        scratch += [pltpu.SemaphoreType.DMA((n_sem,)), pltpu.SemaphoreType.DMA((n_sem,)), pltpu.SemaphoreType.DMA]
        operands += [*bigs, *by_chip, small_piece]
        aliases = {5 + n_big + a: 2 + a for a in range(n_big)}
    return pl.pallas_call(
        body, name="inproj_dx", grid=(n_tiles, N_GROUPS), in_specs=in_specs, out_specs=out_specs, out_shape=out_shape,
        scratch_shapes=scratch, input_output_aliases=aliases,
        compiler_params=_params(("arbitrary", "arbitrary")),
    )(*operands)


def _row_tile(rows, cols, itemsize=4, budget=2 * 1024 * 1024):
    tr = rows
    while tr * cols * itemsize > budget and tr % 16 == 0:
        tr //= 2
    return tr


def _cast_into_slot(a, chip, dtype, name):
    rows, cols = a.shape
    tr = _row_tile(rows, cols)

    def body(chip_ref, a_ref, o_ref):
        del chip_ref
        o_ref[...] = a_ref[...].astype(dtype)

    grid_spec = pltpu.PrefetchScalarGridSpec(
        num_scalar_prefetch=1, grid=(rows // tr,),
        in_specs=[pl.BlockSpec((tr, cols), lambda i, chip_ref: (i, 0))],
        out_specs=pl.BlockSpec((None, tr, cols), lambda i, chip_ref: (chip_ref[0], i, 0)))
    return pl.pallas_call(body, name=name, grid_spec=grid_spec,
                          out_shape=jax.ShapeDtypeStruct((N_SHARDS, rows, cols), dtype),
                          compiler_params=_params(("arbitrary",)))(chip, a)


def _sum_slots(stack, name):
    n, rows, cols = stack.shape
    tr = _row_tile(rows, cols * n)

    def body(s_ref, o_ref):
        total = s_ref[0].astype(F32)
        for k in range(1, n):
            total = total + s_ref[k].astype(F32)
        o_ref[...] = total

    return pl.pallas_call(body, name=name, grid=(rows // tr,),
                          in_specs=[pl.BlockSpec((n, tr, cols), lambda i: (0, i, 0))],
                          out_specs=pl.BlockSpec((tr, cols), lambda i: (i, 0)),
                          out_shape=jax.ShapeDtypeStruct((rows, cols), F32),
                          compiler_params=_params(("parallel",)))(stack)


def _add_half(full, landed, place, name):
    n, rows, cols = full.shape
    half = rows // 2
    tr = _row_tile(half, cols)
    nb = half // tr

    def body(place_ref, a_ref, b_ref, o_ref, own_ref):
        total = (a_ref[...] + b_ref[...]).astype(_MXU_DTYPE)
        o_ref[...] = total

        @pl.when(pl.program_id(1) == place_ref[1])
        def _():
            own_ref[...] = total

    grid_spec = pltpu.PrefetchScalarGridSpec(
        num_scalar_prefetch=1, grid=(nb, n),
        in_specs=[pl.BlockSpec((None, tr, cols), lambda i, j, place_ref: (j, place_ref[0] * nb + i, 0)),
                  pl.BlockSpec((None, tr, cols), lambda i, j, place_ref: (j, i, 0))],
        out_specs=[pl.BlockSpec((None, tr, cols), lambda i, j, place_ref: (j, i, 0)),
                   pl.BlockSpec((None, tr, cols), lambda i, j, place_ref: (place_ref[1], i, 0))])
    shape = jax.ShapeDtypeStruct((n, half, cols), _MXU_DTYPE)
    return pl.pallas_call(body, name=name, grid_spec=grid_spec, out_shape=[shape, shape],
                          compiler_params=_params(("parallel", "arbitrary")))(place, full, landed)


def _adamw_update(w, grad, m, v):
    c1 = 1.0 - ADAM_B1 ** ADAM_STEP
    c2 = 1.0 - ADAM_B2 ** ADAM_STEP
    nm = ADAM_B1 * m + (1.0 - ADAM_B1) * grad
    nv = ADAM_B2 * v + (1.0 - ADAM_B2) * (grad * grad)
    return (-ADAM_LR) * ((nm / c1) / (jnp.sqrt(nv / c2) + ADAM_EPS) + ADAM_WD * w), nm, nv


def _adamw(w, g, m, v, name):
    rows, cols = w.shape
    tr = _row_tile(rows, cols, budget=1024 * 1024)

    def body(w_ref, g_ref, m_ref, v_ref, d_ref, nm_ref, nv_ref):
        d_ref[...], nm_ref[...], nv_ref[...] = _adamw_update(w_ref[...], g_ref[...], m_ref[...], v_ref[...])

    spec = pl.BlockSpec((tr, cols), lambda i: (i, 0))
    shape = jax.ShapeDtypeStruct((rows, cols), F32)
    return pl.pallas_call(body, name=name, grid=(rows // tr,), in_specs=[spec] * 4, out_specs=[spec] * 3,
                          out_shape=[shape] * 3, compiler_params=_params(("parallel",)))(w, g, m, v)


def _adamw_halves(w, g_mine, g_sibling, m, v, core, name):
    rows, cols = w.shape
    half = rows // 2
    tr = _row_tile(half, cols, budget=1024 * 1024)
    nb = half // tr

    def body(core_ref, w_ref, gm_ref, gs_ref, m_ref, v_ref, g_ref, d_ref, nm_ref, nv_ref):
        mine = pl.program_id(0) // nb == core_ref[0]
        grad = jnp.where(mine, gm_ref[...], gs_ref[...])
        g_ref[...] = grad
        d_ref[...], nm_ref[...], nv_ref[...] = _adamw_update(w_ref[...], grad, m_ref[...], v_ref[...])

    spec = pl.BlockSpec((tr, cols), lambda i, core_ref: (i, 0))
    half_spec = pl.BlockSpec((tr, cols), lambda i, core_ref: (i % nb, 0))
    grid_spec = pltpu.PrefetchScalarGridSpec(num_scalar_prefetch=1, grid=(rows // tr,),
                                             in_specs=[spec, half_spec, half_spec, spec, spec], out_specs=[spec] * 4)
    shape = jax.ShapeDtypeStruct((rows, cols), F32)
    return pl.pallas_call(body, name=name, grid_spec=grid_spec, out_shape=[shape] * 4,
                          compiler_params=_params(("parallel",)))(core, w, g_mine, g_sibling, m, v)


def _local_step(x, loss_target, w_all, pa, pb, wo, conv_w, b_merge, conv_b, rg_wx, rg_bx, rg_wa, rg_ba,
                rg_lambda, hg_lb_logits, hg_norm_g, norm_g, final_norm_g, start_reduction=None):
    batch, seq, d = x.shape
    x2d = x.reshape(batch * seq, d)
    tgt2d = loss_target.reshape(batch * seq, d)
    z, h_t = _inproj_fwd(x2d, norm_g, w_all)
    lru = (conv_w, conv_b, rg_wx, rg_bx, rg_wa, rg_ba, rg_lambda)
    ya, hl = _branch_a_fwd(z, *lru, batch, seq)
    yb, states = _branch_b_fwd(z, hg_lb_logits, hg_norm_g, batch, seq)
    dya, dyb, dx2, dz, loss, d_final_g, d_b_merge, d_pa, d_pb, d_wo = _merge_tail(
        ya, yb, z, x2d, tgt2d, b_merge, final_norm_g, pa, pb, wo)
    dz, d_lb_logits, d_hg_g = _branch_b_bwd(z, states, dyb, dz, hg_lb_logits, hg_norm_g, batch, seq)
    dz, d_conv_w, d_conv_b, d_wx, d_bx, d_wa, d_ba, d_lam = _branch_a_bwd(z, hl, dya, dz, *lru, batch, seq)
    d_w_in = _inproj_dw(h_t, dz)
    big = (d_w_in, d_pa, d_pb, d_wo)
    small = dict(b_merge=d_b_merge, conv_w=d_conv_w, conv_b=d_conv_b, rg_wx=d_wx, rg_bx=d_bx, rg_wa=d_wa,
                 rg_ba=d_ba, rg_lambda=d_lam, hg_lb_logits=d_lb_logits, hg_norm_g=d_hg_g,
                 norm_g=jnp.zeros((1, d), F32), final_norm_g=d_final_g)
    if start_reduction is None:
        grad_x, small["norm_g"] = _inproj_dx(dz, w_all, x2d, dx2, norm_g)
        return loss[0, 0], grad_x.reshape(batch, seq, d), big, small
    grad_x, d_norm_g, *scattered = _inproj_dx(dz, w_all, x2d, dx2, norm_g, scatter=start_reduction(big, small))
    return loss[0, 0], grad_x.reshape(batch, seq, d), d_norm_g, scattered


_SMALL_ORDER = ("b_merge", "conv_w", "conv_b", "rg_wx", "rg_bx", "rg_wa", "rg_ba", "rg_lambda", "hg_lb_logits",
                "hg_norm_g", "norm_g", "final_norm_g")
N_DEV = 8
PIECE_ROWS = 272


def _pack_small(tree):
    flat = jnp.concatenate([tree[k].reshape(-1) for k in _SMALL_ORDER])
    flat = jnp.pad(flat, (0, N_DEV * PIECE_ROWS * LANES - flat.shape[0]))
    return flat.reshape(N_DEV * PIECE_ROWS, LANES)


def _unpack_small(packed, like):
    flat = packed.reshape(-1)
    out, pos = {}, 0
    for k in _SMALL_ORDER:
        n = like[k].size
        out[k] = flat[pos:pos + n].reshape(like[k].shape)
        pos += n
    return out


def _mesh_position():
    x, y, c = lax.axis_index("x"), lax.axis_index("y"), lax.axis_index("c")
    other_chips = [(1 - x, y), (x, 1 - y), (1 - x, 1 - y)]
    return x, y, c, other_chips


def _other_devices(x, y, c):
    flips = [(fx, fy, fc) for fx in (0, 1) for fy in (0, 1) for fc in (0, 1) if (fx, fy, fc) != (0, 0, 0)]
    return [(jnp.where(fx, 1 - x, x), jnp.where(fy, 1 - y, y), jnp.where(fc, 1 - c, c)) for fx, fy, fc in flips]


def _remote(src, dst, send_sems, recv_sems, k, device):
    return pltpu.make_async_remote_copy(src_ref=src, dst_ref=dst, send_sem=send_sems.at[k], recv_sem=recv_sems.at[k],
                                        device_id=device, device_id_type=MESH)


def _gather_weights(slotted, conv_slotted):
    n_big = len(slotted)
    n_sem = 6 * n_big + 3

    def body(*refs):
        bufs, cw = refs[n_big + 1:2 * n_big + 1], refs[2 * n_big + 1]
        send_sems, recv_sems = refs[2 * n_big + 2:]
        x, y, c, chips = _mesh_position()
        me, sibling = 2 * x + y, (x, y, 1 - c)

        def half(buf, slot, which):
            hs = buf.shape[1] // 2
            return buf.at[slot, pl.ds(which * hs, hs), :]

        sends = []
        for a in range(n_big):
            for j, (cx, cy) in enumerate(chips):
                mine = half(bufs[a], me, c)
                sends.append(_remote(mine, mine, send_sems, recv_sems, 6 * a + j, (cx, cy, c)))
        for j, (cx, cy) in enumerate(chips):
            sends.append(_remote(cw.at[me], cw.at[me], send_sems, recv_sems, 6 * n_big + j, (cx, cy, c)))
        for cp in sends:
            cp.start()
        for j, (cx, cy) in enumerate(chips):
            theirs = 2 * cx + cy
            for a in range(n_big):
                landed = half(bufs[a], theirs, c)
                _remote(landed, landed, send_sems, recv_sems, 6 * a + j, (cx, cy, c)).wait_recv()
                passed = _remote(landed, landed, send_sems, recv_sems, 6 * a + 3 + j, sibling)
                passed.start()
                sends.append(passed)
        for j, (cx, cy) in enumerate(chips):
            theirs = 2 * cx + cy
            _remote(cw.at[theirs], cw.at[theirs], send_sems, recv_sems, 6 * n_big + j, (cx, cy, c)).wait_recv()
            for a in range(n_big):
                landed = half(bufs[a], theirs, 1 - c)
                _remote(landed, landed, send_sems, recv_sems, 6 * a + 3 + j, sibling).wait_recv()
        for cp in sends:
            cp.wait_send()

    hbm = pl.BlockSpec(memory_space=pl.ANY)
    operands = list(slotted) + [conv_slotted]
    return pl.pallas_call(
        body, name="gather_weights",
        in_specs=[hbm] * (n_big + 1), out_specs=[hbm] * (n_big + 1),
        out_shape=[jax.ShapeDtypeStruct(a.shape, a.dtype) for a in operands],
        scratch_shapes=[pltpu.SemaphoreType.DMA((n_sem,)), pltpu.SemaphoreType.DMA((n_sem,))],
        input_output_aliases={i: i for i in range(n_big + 1)},
    )(*operands)


def _exchange_halves(bigs, small):
    n_big = len(bigs)
    n_sem = n_big + N_DEV - 1

    def body(*refs):
        srcs, small_src = refs[:n_big], refs[n_big]
        outs, small_out = refs[n_big + 1:2 * n_big + 1], refs[2 * n_big + 1]
        send_sems, recv_sems, local_sem = refs[2 * n_big + 2:]
        x, y, c, _ = _mesh_position()
        me, sibling = 4 * x + 2 * y + c, (x, y, 1 - c)
        mine = pltpu.make_async_copy(small_src.at[pl.ds(me * PIECE_ROWS, PIECE_ROWS), :], small_out.at[me], local_sem)
        mine.start()
        copies = []
        for a in range(n_big):
            hs = srcs[a].shape[1] // 2
            copies.append(_remote(srcs[a].at[:, pl.ds((1 - c) * hs, hs), :], outs[a], send_sems, recv_sems, a, sibling))
        for k, (px, py, pc) in enumerate(_other_devices(x, y, c)):
            piece = small_src.at[pl.ds((4 * px + 2 * py + pc) * PIECE_ROWS, PIECE_ROWS), :]
            copies.append(_remote(piece, small_out.at[me], send_sems, recv_sems, n_big + k, (px, py, pc)))
        for cp in copies:
            cp.start()
        for cp in copies:
            cp.wait()
        mine.wait()

    hbm = pl.BlockSpec(memory_space=pl.ANY)
    out_shape = [jax.ShapeDtypeStruct((g.shape[0], g.shape[1] // 2, g.shape[2]), F32) for g in bigs]
    out_shape.append(jax.ShapeDtypeStruct((N_DEV, PIECE_ROWS, LANES), F32))
    return pl.pallas_call(
        body, name="exchange_halves",
        in_specs=[hbm] * (n_big + 1), out_specs=[hbm] * (n_big + 1), out_shape=out_shape,
        scratch_shapes=[pltpu.SemaphoreType.DMA((n_sem,)), pltpu.SemaphoreType.DMA((n_sem,)), pltpu.SemaphoreType.DMA],
    )(*bigs, small)


def _scatter_copies(srcs, small_src, outs, small_out, send_sems, recv_sems, local_sem):
    n_big = len(srcs)
    x, y, c, chips = _mesh_position()
    chip, me = 2 * x + y, 4 * x + 2 * y + c
    copies = [pltpu.make_async_copy(small_src, small_out.at[me], local_sem)]
    for a in range(n_big):
        for j, (cx, cy) in enumerate(chips):
            copies.append(_remote(srcs[a].at[2 * cx + cy], outs[a].at[chip], send_sems, recv_sems, 3 * a + j, (cx, cy, c)))
    for k, peer in enumerate(_other_devices(x, y, c)):
        copies.append(_remote(small_src, small_out.at[me], send_sems, recv_sems, 3 * n_big + k, peer))
    return copies


def _swap_halves(halves, vec):
    n_big = len(halves)

    def body(*refs):
        srcs, vec_src = refs[:n_big], refs[n_big]
        outs, vec_out = refs[n_big + 1:2 * n_big + 1], refs[2 * n_big + 1]
        send_sems, recv_sems, local_sem = refs[2 * n_big + 2:]
        x, y, c, _ = _mesh_position()
        me = 4 * x + 2 * y + c
        copies = [pltpu.make_async_copy(vec_src, vec_out.at[me], local_sem)]
        copies += [_remote(srcs[a], outs[a], send_sems, recv_sems, a, (x, y, 1 - c)) for a in range(n_big)]
        copies += [_remote(vec_src, vec_out.at[me], send_sems, recv_sems, n_big + k, peer)
                   for k, peer in enumerate(_other_devices(x, y, c))]
        for cp in copies:
            cp.start()
        for cp in copies:
            cp.wait()

    hbm = pl.BlockSpec(memory_space=pl.ANY)
    n_sem = n_big + N_DEV - 1
    return pl.pallas_call(
        body, name="swap_halves",
        in_specs=[hbm] * (n_big + 1), out_specs=[hbm] * (n_big + 1),
        out_shape=[jax.ShapeDtypeStruct(h.shape, F32) for h in halves] + [jax.ShapeDtypeStruct((N_DEV,) + vec.shape, F32)],
        scratch_shapes=[pltpu.SemaphoreType.DMA((n_sem,)), pltpu.SemaphoreType.DMA((n_sem,)), pltpu.SemaphoreType.DMA],
    )(*halves, vec)


def kernel(x, w_in, b_merge, conv_w, conv_b, rg_wx, rg_bx, rg_wa, rg_ba, rg_lambda, hg_lb_logits, hg_norm_g, proj_a, proj_b, w_out, norm_g, final_norm_g, loss_target, m_w_in, m_b_merge, m_conv_w, m_conv_b, m_rg_wx, m_rg_bx, m_rg_wa, m_rg_ba, m_rg_lambda, m_hg_lb_logits, m_hg_norm_g, m_proj_a, m_proj_b, m_w_out, m_norm_g, m_final_norm_g, v_w_in, v_b_merge, v_conv_w, v_conv_b, v_rg_wx, v_rg_bx, v_rg_wa, v_rg_ba, v_rg_lambda, v_hg_lb_logits, v_hg_norm_g, v_proj_a, v_proj_b, v_w_out, v_norm_g, v_final_norm_g):
    d = D_MODEL
    weights = dict(w_in=w_in, b_merge=b_merge, conv_w=conv_w, conv_b=conv_b, rg_wx=rg_wx, rg_bx=rg_bx, rg_wa=rg_wa,
                   rg_ba=rg_ba, rg_lambda=rg_lambda, hg_lb_logits=hg_lb_logits, hg_norm_g=hg_norm_g, proj_a=proj_a,
                   proj_b=proj_b, w_out=w_out, norm_g=norm_g, final_norm_g=final_norm_g)
    m = dict(w_in=m_w_in, b_merge=m_b_merge, conv_w=m_conv_w, conv_b=m_conv_b, rg_wx=m_rg_wx, rg_bx=m_rg_bx,
             rg_wa=m_rg_wa, rg_ba=m_rg_ba, rg_lambda=m_rg_lambda, hg_lb_logits=m_hg_lb_logits, hg_norm_g=m_hg_norm_g,
             proj_a=m_proj_a, proj_b=m_proj_b, w_out=m_w_out, norm_g=m_norm_g, final_norm_g=m_final_norm_g)
    v = dict(w_in=v_w_in, b_merge=v_b_merge, conv_w=v_conv_w, conv_b=v_conv_b, rg_wx=v_rg_wx, rg_bx=v_rg_bx,
             rg_wa=v_rg_wa, rg_ba=v_rg_ba, rg_lambda=v_rg_lambda, hg_lb_logits=v_hg_lb_logits, hg_norm_g=v_hg_norm_g,
             proj_a=v_proj_a, proj_b=v_proj_b, w_out=v_w_out, norm_g=v_norm_g, final_norm_g=v_final_norm_g)
    big_names = ("w_in", "proj_a", "proj_b", "w_out")

    core = lax.axis_index("c").astype(jnp.int32).reshape(1)
    chip = (2 * lax.axis_index("x") + lax.axis_index("y")).astype(jnp.int32)

    slotted = [_cast_into_slot(weights[k][0], chip.reshape(1), _MXU_DTYPE, f"cast_{k}") for k in big_names]
    conv_slotted = _cast_into_slot(conv_w[0], chip.reshape(1), F32, "slot_conv_w")
    w_all, pa_all, pb_all, wo_all, cw_all = _gather_weights(slotted, conv_slotted)
    conv_w_full = jnp.transpose(cw_all, (1, 0, 2)).reshape(CONV_WIDTH, d)

    small_shapes = {}

    def start_reduction(big_grads, small_grads):
        small_shapes.update({k: t.shape for k, t in small_grads.items()})
        bigs = [big_grads[0]] + [g.reshape(N_SHARDS, d // N_SHARDS, d) for g in big_grads[1:]]
        *landed, small_landed = _exchange_halves(bigs, _pack_small(small_grads))
        place = jnp.concatenate([core, chip.reshape(1)])
        sums = [_add_half(g, l, place, f"add_half_{a}") for a, (g, l) in enumerate(zip(bigs, landed))]
        return [s[0] for s in sums], [s[1] for s in sums], _sum_slots(small_landed, "sum_small")

    loss_part, grad_x, d_norm_g, (*by_chip, small_all) = _local_step(
        x, loss_target, w_all, pa_all.reshape(d, d), pb_all.reshape(d, d), wo_all.reshape(d, d), conv_w_full,
        b_merge, conv_b, rg_wx[0], rg_bx.reshape(1, d), rg_wa[0], rg_ba.reshape(1, d), rg_lambda, hg_lb_logits,
        hg_norm_g, norm_g, final_norm_g.reshape(1, d), start_reduction=start_reduction)
    loss = lax.psum(loss_part, ("x", "y", "c"))

    mine = [_sum_slots(s, f"sum_chips_{a}") for a, s in enumerate(by_chip)]
    *theirs, norm_g_parts = _swap_halves(mine, d_norm_g.reshape(SUBLANES, LANES))
    small_red = _unpack_small(small_all, {k: jax.ShapeDtypeStruct(s, F32) for k, s in small_shapes.items()})
    small_red["norm_g"] = _sum_slots(norm_g_parts, "sum_norm_g").reshape(1, d)

    grads, delta, new_m, new_v = {}, {}, {}, {}
    for k, g_mine, g_theirs in zip(big_names, mine, theirs):
        out = _adamw_halves(weights[k][0], g_mine, g_theirs, m[k][0], v[k][0], core, f"adamw_{k}")
        grads[k], delta[k], new_m[k], new_v[k] = (t.reshape(weights[k].shape) for t in out)
    cols = d // N_SHARDS
    g_conv = lax.dynamic_slice(small_red["conv_w"], (0, chip * cols), (CONV_WIDTH, cols))
    grads["conv_w"] = g_conv.reshape(conv_w.shape)
    dl, nm, nv = _adamw(conv_w[0], g_conv, m_conv_w[0], v_conv_w[0], "adamw_conv_w")
    delta["conv_w"], new_m["conv_w"], new_v["conv_w"] = (t.reshape(conv_w.shape) for t in (dl, nm, nv))
    rest = [k for k in _SMALL_ORDER if k != "conv_w"]
    like = {k: (weights[k] if k != "conv_w" else jnp.zeros((CONV_WIDTH, d), F32)) for k in _SMALL_ORDER}
    packs = [_pack_small({k: (t[k] if k != "conv_w" else like[k]) for k in _SMALL_ORDER}) for t in (weights, m, v)]
    g_pack = _pack_small({k: small_red[k].reshape(like[k].shape) for k in _SMALL_ORDER})
    outs = [_unpack_small(p, like) for p in _adamw(packs[0], g_pack, packs[1], packs[2], "adamw_small")]
    for k in rest:
        grads[k] = small_red[k].reshape(weights[k].shape)
        delta[k], new_m[k], new_v[k] = outs[0][k], outs[1][k], outs[2][k]

    order = ("w_in", "b_merge", "conv_w", "conv_b", "rg_wx", "rg_bx", "rg_wa", "rg_ba", "rg_lambda", "hg_lb_logits",
             "hg_norm_g", "proj_a", "proj_b", "w_out", "norm_g", "final_norm_g")
    return (loss, grad_x, *[grads[k] for k in order], *[delta[k] for k in order], *[new_m[k] for k in order],
            *[new_v[k] for k in order])
```

```python
import functools

import jax
import jax.numpy as jnp
from jax import lax
from jax.experimental import pallas as pl
from jax.experimental.pallas import tpu as pltpu

F32 = jnp.float32
_MXU_DTYPE = jnp.bfloat16

D_MODEL = 1024
LANES = 128
SUBLANES = 8
N_BLK = D_MODEL // LANES
N_GROUPS = 8
N_SHARDS = 4
CONV_WIDTH = 4
LRU_C = 8.0
CHUNK = 64
CHUNKS_IN_FLIGHT = 4
HG_SCALE = float(LANES) ** -0.5
EPS = 1e-6
ADAM_LR, ADAM_B1, ADAM_B2, ADAM_EPS, ADAM_WD, ADAM_STEP = 0.001, 0.9, 0.999, 1e-08, 0.01, 10
VMEM_LIMIT = 56 * 1024 * 1024
MESH = pl.DeviceIdType.MESH

_SLOT_TO_GROUP = (2, 3, 4, 5, 0, 1, 6, 7)


def _group_of_slot(s):
    return jnp.where(s < 4, s + 2, jnp.where(s < 6, s - 4, s))


def _mm(a, b):
    return lax.dot_general(a.astype(_MXU_DTYPE), b.astype(_MXU_DTYPE), (((1,), (0,)), ((), ())),
                           preferred_element_type=F32)


def _mm_nt(a, b):
    return lax.dot_general(a.astype(_MXU_DTYPE), b.astype(_MXU_DTYPE), (((1,), (1,)), ((), ())),
                           preferred_element_type=F32)


def _mm_tn(a, b):
    return lax.dot_general(a.astype(_MXU_DTYPE), b.astype(_MXU_DTYPE), (((0,), (0,)), ((), ())),
                           preferred_element_type=F32)


def _sigmoid(x):
    return 1.0 / (1.0 + jnp.exp(-x))


def _log1p_pos(y):
    series = y * (1.0 - y * (0.5 - y * (1.0 / 3.0 - y * 0.25)))
    return jnp.where(y < 0.01, series, jnp.log(1.0 + y))


def _expm1_neg(y):
    series = y * (1.0 + y * 0.5 * (1.0 + y * (1.0 / 3.0) * (1.0 + y * 0.25 * (1.0 + y * 0.2))))
    return jnp.where(y > -0.02, series, jnp.exp(y) - 1.0)


def _softplus(x):
    return jnp.maximum(x, 0.0) + _log1p_pos(jnp.exp(-jnp.abs(x)))


def _shift_down(x, n):
    rows = lax.broadcasted_iota(jnp.int32, x.shape, 0)
    return jnp.where(rows >= n, pltpu.roll(x, n, 0), 0.0)


def _shift_up(x, n):
    size = x.shape[0]
    rows = lax.broadcasted_iota(jnp.int32, x.shape, 0)
    return jnp.where(rows < size - n, pltpu.roll(x, size - n, 0), 0.0)


def _params(dims, vmem=VMEM_LIMIT):
    return pltpu.CompilerParams(dimension_semantics=dims, vmem_limit_bytes=vmem)


def _load_w_in_by_slot(w_hbm, w_res):
    for slot, g in enumerate(_SLOT_TO_GROUP):
        pltpu.sync_copy(w_hbm.at[g // 2, :, pl.ds((g % 2) * D_MODEL, D_MODEL)], w_res.at[slot])


def _inproj_fwd(x2d, norm_g, w_all):
    tokens, d = x2d.shape
    tm = min(512, tokens)

    def body(x_ref, g_ref, w_hbm, z_ref, ht_ref, h_scr, w_res):
        @pl.when((pl.program_id(0) == 0) & (pl.program_id(1) == 0))
        def _():
            _load_w_in_by_slot(w_hbm, w_res)

        @pl.when(pl.program_id(1) == 0)
        def _():
            x = x_ref[...]
            r = lax.rsqrt(jnp.mean(x * x, axis=-1, keepdims=True) + EPS)
            h = (x * r) * g_ref[...]
            h_scr[...] = h.astype(_MXU_DTYPE)
            ht_ref[...] = jnp.transpose(h).astype(_MXU_DTYPE)

        z_ref[...] = _mm(h_scr[...], w_res[pl.program_id(1)])

    return pl.pallas_call(
        body, name="inproj_fwd",
        grid=(tokens // tm, N_GROUPS),
        in_specs=[pl.BlockSpec((tm, d), lambda i, s: (i, 0)),
                  pl.BlockSpec((1, d), lambda i, s: (0, 0)),
                  pl.BlockSpec(memory_space=pl.ANY)],
        out_specs=[pl.BlockSpec((None, tm, D_MODEL), lambda i, s: (s, i, 0)),
                   pl.BlockSpec((d, tm), lambda i, s: (0, i))],
        out_shape=[jax.ShapeDtypeStruct((N_GROUPS, tokens, D_MODEL), F32),
                   jax.ShapeDtypeStruct((d, tokens), _MXU_DTYPE)],
        scratch_shapes=[pltpu.VMEM((tm, d), _MXU_DTYPE), pltpu.VMEM((N_GROUPS, d, D_MODEL), _MXU_DTYPE)],
        compiler_params=_params(("arbitrary", "arbitrary")),
    )(x2d, norm_g, w_all)


def _slot_of_group(g):
    return jnp.where(g < 2, g + 4, jnp.where(g < 6, g - 2, g))


def _inproj_fwd_gather(x2d, norm_g, slotted, conv_slotted, chip):
    tokens, d = x2d.shape
    tm = min(512, tokens)
    n_tiles = tokens // tm
    n_big = len(slotted)
    n_sem = 6 * n_big + 3
    last_pass = N_GROUPS - 1

    def shard_of(k, chip_id):
        x, y = chip_id // 2, chip_id % 2
        return 2 * jnp.where(k % 2 == 1, 1 - x, x) + jnp.where(k // 2 == 1, 1 - y, y)

    def body(chip_ref, x_ref, g_ref, *rest):
        bufs, cw = rest[n_big + 1:2 * n_big + 1], rest[2 * n_big + 1]
        z_ref, ht_ref = rest[2 * n_big + 2:2 * n_big + 4]
        h_all, slab, send_sems, recv_sems, slab_sem = rest[2 * n_big + 4:]
        del chip_ref
        p, i = pl.program_id(0), pl.program_id(1)
        x, y, c, chips = _mesh_position()
        me, sibling = 2 * x + y, (x, y, 1 - c)

        def half(a, slot, which):
            hs = bufs[a].shape[1] // 2
            return bufs[a].at[slot, pl.ds(which * hs, hs), :]

        def send(a, j):
            mine = half(a, me, c)
            return _remote(mine, mine, send_sems, recv_sems, 6 * a + j, (chips[j][0], chips[j][1], c))

        def arrival(a, j):
            landed = half(a, 2 * chips[j][0] + chips[j][1], c)
            return _remote(landed, landed, send_sems, recv_sems, 6 * a + j, (chips[j][0], chips[j][1], c))

        def passed_on(a, j, which):
            landed = half(a, 2 * chips[j][0] + chips[j][1], which)
            return _remote(landed, landed, send_sems, recv_sems, 6 * a + 3 + j, sibling)

        def conv_copy(j, slot):
            return _remote(cw.at[slot], cw.at[slot], send_sems, recv_sems, 6 * n_big + j, (chips[j][0], chips[j][1], c))

        def land(a, j):
            arrival(a, j).wait_recv()
            passed_on(a, j, c).start()
            passed_on(a, j, 1 - c).wait_recv()

        @pl.when((p == 0) & (i == 0))
        def _():
            for j in range(3):
                send(0, j).start()

        for j in range(3):
            @pl.when((p == 2 * (j + 1)) & (i == 0))
            def _(j=j):
                land(0, j)
                if j == 0:
                    for a in range(1, n_big):
                        for jj in range(3):
                            send(a, jj).start()
                    for jj in range(3):
                        conv_copy(jj, me).start()

        @pl.when(i == 0)
        def _():
            shard = shard_of(p // 2, me)
            for which in range(2):
                @pl.when(p % 2 == which)
                def _(which=which):
                    cp = pltpu.make_async_copy(bufs[0].at[shard, :, pl.ds(which * D_MODEL, D_MODEL)], slab, slab_sem)
                    cp.start()
                    cp.wait()

        rows = pl.ds(pl.multiple_of(i * tm, tm), tm)

        @pl.when(p == 0)
        def _():
            xt = x_ref[...]
            r = lax.rsqrt(jnp.mean(xt * xt, axis=-1, keepdims=True) + EPS)
            h = (xt * r) * g_ref[...]
            h_all[rows, :] = h.astype(_MXU_DTYPE)
            ht_ref[...] = jnp.transpose(h).astype(_MXU_DTYPE)

        z_ref[...] = _mm(h_all[rows, :], slab[...])

        @pl.when((p == last_pass) & (i == n_tiles - 1))
        def _():
            for a in range(1, n_big):
                for j in range(3):
                    land(a, j)
            for j in range(3):
                conv_copy(j, 2 * chips[j][0] + chips[j][1]).wait_recv()
            for a in range(n_big):
                for j in range(3):
                    send(a, j).wait_send()
                    passed_on(a, j, c).wait_send()
            for j in range(3):
                conv_copy(j, me).wait_send()

    def z_index(p, i, chip_ref):
        g = 2 * shard_of(p // 2, chip_ref[0]) + p % 2
        return (_slot_of_group(g), i, 0)

    def first_pass_tile(p, i, chip_ref):
        return jnp.where(p == 0, i, n_tiles - 1)

    hbm = pl.BlockSpec(memory_space=pl.ANY)
    operands = list(slotted) + [conv_slotted]
    grid_spec = pltpu.PrefetchScalarGridSpec(
        num_scalar_prefetch=1, grid=(N_GROUPS, n_tiles),
        in_specs=[pl.BlockSpec((tm, d), lambda p, i, chip_ref: (first_pass_tile(p, i, chip_ref), 0)),
                  pl.BlockSpec((1, d), lambda p, i, chip_ref: (0, 0))] + [hbm] * (n_big + 1),
        out_specs=[hbm] * (n_big + 1) + [pl.BlockSpec((None, tm, D_MODEL), z_index),
                                         pl.BlockSpec((d, tm), lambda p, i, chip_ref: (0, first_pass_tile(p, i, chip_ref)))],
        scratch_shapes=[pltpu.VMEM((tokens, d), _MXU_DTYPE), pltpu.VMEM((d, D_MODEL), _MXU_DTYPE),
                        pltpu.SemaphoreType.DMA((n_sem,)), pltpu.SemaphoreType.DMA((n_sem,)), pltpu.SemaphoreType.DMA])
    out = pl.pallas_call(
        body, name="inproj_fwd_gather", grid_spec=grid_spec,
        out_shape=[jax.ShapeDtypeStruct(a.shape, a.dtype) for a in operands]
        + [jax.ShapeDtypeStruct((N_GROUPS, tokens, D_MODEL), F32), jax.ShapeDtypeStruct((d, tokens), _MXU_DTYPE)],
        input_output_aliases={3 + a: a for a in range(n_big + 1)},
        compiler_params=_params(("arbitrary", "arbitrary")),
    )(chip, x2d, norm_g, *operands)
    return out[n_big + 1], out[n_big + 2], out[:n_big], out[n_big]


def _lru_gates(xa, cw_ref, cb_ref, wx_ref, bx_ref, wa_ref, ba_ref, lam_ref):
    xc = (cb_ref[...] + cw_ref[3:4, :] * xa + cw_ref[2:3, :] * _shift_down(xa, 1)
          + cw_ref[1:2, :] * _shift_down(xa, 2) + cw_ref[0:1, :] * _shift_down(xa, 3))
    gi = _sigmoid(_mm(xc, wx_ref[...]) + bx_ref[...])
    gr = _sigmoid(_mm(xc, wa_ref[...]) + ba_ref[...])
    sp = _softplus(-lam_ref[...])
    log_a = (-LRU_C) * gr * sp
    a = jnp.exp(log_a)
    mult = jnp.sqrt(-_expm1_neg(2.0 * log_a))
    return xc, gi, gr, sp, a, mult


def _tile_rows():
    return lax.broadcasted_iota(jnp.int32, (SUBLANES, LANES), 0)


def _scan_forward(a_scr, u_scr, h_scr, seq):
    rows = _tile_rows()

    def tile(j, carry):
        sl = pl.ds(pl.multiple_of(j * SUBLANES, SUBLANES), SUBLANES)
        a = a_scr[sl, :]
        u = u_scr[sl, :]
        for d in (1, 2, 4):
            keep = rows >= d
            a_sh = jnp.where(keep, pltpu.roll(a, d, 0), 1.0)
            u_sh = jnp.where(keep, pltpu.roll(u, d, 0), 0.0)
            u = a * u_sh + u
            a = a * a_sh
        h = u + a * carry
        h_scr[sl, :] = h
        return jnp.broadcast_to(h[SUBLANES - 1:SUBLANES, :], (SUBLANES, LANES))

    lax.fori_loop(0, seq // SUBLANES, tile, jnp.zeros((SUBLANES, LANES), F32))


def _scan_backward(c_scr, d_scr, g_scr, seq):
    rows = _tile_rows()
    n_tiles = seq // SUBLANES

    def tile(jj, carry):
        j = n_tiles - 1 - jj
        sl = pl.ds(pl.multiple_of(j * SUBLANES, SUBLANES), SUBLANES)
        c = c_scr[sl, :]
        g = d_scr[sl, :]
        for d in (1, 2, 4):
            keep = rows < SUBLANES - d
            c_sh = jnp.where(keep, pltpu.roll(c, SUBLANES - d, 0), 1.0)
            g_sh = jnp.where(keep, pltpu.roll(g, SUBLANES - d, 0), 0.0)
            g = c * g_sh + g
            c = c * c_sh
        g = g + c * carry
        g_scr[sl, :] = g
        return jnp.broadcast_to(g[0:1, :], (SUBLANES, LANES))

    lax.fori_loop(0, n_tiles, tile, jnp.zeros((SUBLANES, LANES), F32))


def _lru_param_specs(cb_axis):
    def pick(*ids):
        return ids[cb_axis]

    vec = pl.BlockSpec((1, LANES), lambda *ids: (0, pick(*ids)))
    mat = pl.BlockSpec((None, LANES, LANES), lambda *ids: (pick(*ids), 0, 0))
    return [pl.BlockSpec((CONV_WIDTH, LANES), lambda *ids: (0, pick(*ids))), vec, mat, vec, mat, vec, vec]


def _branch_a_fwd(z, conv_w, conv_b, wx, bx, wa, ba, lam, batch, seq):
    tokens = batch * seq

    def body(z_ref, cw_ref, cb_ref, wx_ref, bx_ref, wa_ref, ba_ref, lam_ref, ya_ref, hl_ref, a_scr, u_scr):
        xa = z_ref[0]
        ga = z_ref[1]
        xc, gi, _, _, a, mult = _lru_gates(xa, cw_ref, cb_ref, wx_ref, bx_ref, wa_ref, ba_ref, lam_ref)
        a_scr[...] = a
        u_scr[...] = mult * gi * xc
        _scan_forward(a_scr, u_scr, hl_ref, seq)
        ya_ref[...] = (hl_ref[...] * (ga * _sigmoid(ga))).astype(_MXU_DTYPE)

    blk = pl.BlockSpec((seq, LANES), lambda b, c: (b, c))
    return pl.pallas_call(
        body, name="branch_a_fwd",
        grid=(batch, N_BLK),
        in_specs=[pl.BlockSpec((2, seq, LANES), lambda b, c: (2, b, c))] + _lru_param_specs(1),
        out_specs=[blk, blk],
        out_shape=[jax.ShapeDtypeStruct((tokens, D_MODEL), _MXU_DTYPE), jax.ShapeDtypeStruct((tokens, D_MODEL), F32)],
        scratch_shapes=[pltpu.VMEM((seq, LANES), F32), pltpu.VMEM((seq, LANES), F32)],
        compiler_params=_params(("parallel", "parallel")),
    )(z, conv_w, conv_b, wx, bx, wa, ba, lam)


def _branch_a_bwd(z, hl, dya, dz, conv_w, conv_b, wx, bx, wa, ba, lam, batch, seq):
    def body(z_ref, hl_ref, dya_ref, dz_in_ref, cw_ref, cb_ref, wx_ref, bx_ref, wa_ref, ba_ref, lam_ref,
             dz_ref, dcw_ref, dcb_ref, dwx_ref, dbx_ref, dwa_ref, dba_ref, dlam_ref, c_scr, d_scr, g_scr):
        del dz_in_ref
        xa = z_ref[0]
        ga = z_ref[1]
        hl = hl_ref[...]
        dya = dya_ref[...]
        xc, gi, gr, sp, a, mult = _lru_gates(xa, cw_ref, cb_ref, wx_ref, bx_ref, wa_ref, ba_ref, lam_ref)
        sga = _sigmoid(ga)
        dz_ref[1] = (dya * hl * (sga * (1.0 + ga * (1.0 - sga)))).astype(_MXU_DTYPE)
        c_scr[...] = _shift_up(a, 1)
        d_scr[...] = dya * (ga * sga)
        _scan_backward(c_scr, d_scr, g_scr, seq)
        g = g_scr[...]
        da = g * _shift_down(hl, 1)
        dmult = g * gi * xc
        dgi = g * mult * xc
        dxc = g * mult * gi
        dlog_a = da * a - dmult * (a * a) / mult
        dgr = dlog_a * (-LRU_C) * sp
        dsp = jnp.sum(dlog_a * gr, axis=0, keepdims=True) * (-LRU_C)
        dlam = -dsp * _sigmoid(-lam_ref[...])
        dpi = dgi * gi * (1.0 - gi)
        dpr = dgr * gr * (1.0 - gr)
        dxc = dxc + _mm_nt(dpi, wx_ref[...]) + _mm_nt(dpr, wa_ref[...])
        dwx = _mm_tn(xc, dpi)
        dwa = _mm_tn(xc, dpr)
        dbx = jnp.sum(dpi, axis=0, keepdims=True)
        dba = jnp.sum(dpr, axis=0, keepdims=True)
        dxa = (cw_ref[3:4, :] * dxc + cw_ref[2:3, :] * _shift_up(dxc, 1) + cw_ref[1:2, :] * _shift_up(dxc, 2)
               + cw_ref[0:1, :] * _shift_up(dxc, 3))
        dz_ref[0] = dxa.astype(_MXU_DTYPE)
        dcb = jnp.sum(dxc, axis=0, keepdims=True)
        dcw = [jnp.sum(dxc * _shift_down(xa, CONV_WIDTH - 1 - k), axis=0, keepdims=True) if k < CONV_WIDTH - 1
               else jnp.sum(dxc * xa, axis=0, keepdims=True) for k in range(CONV_WIDTH)]

        @pl.when(pl.program_id(1) == 0)
        def _():
            for k in range(CONV_WIDTH):
                dcw_ref[k:k + 1, :] = dcw[k]
            dcb_ref[...] = dcb
            dwx_ref[...] = dwx
            dbx_ref[...] = dbx
            dwa_ref[...] = dwa
            dba_ref[...] = dba
            dlam_ref[...] = dlam

        @pl.when(pl.program_id(1) != 0)
        def _():
            for k in range(CONV_WIDTH):
                dcw_ref[k:k + 1, :] += dcw[k]
            dcb_ref[...] += dcb
            dwx_ref[...] += dwx
            dbx_ref[...] += dbx
            dwa_ref[...] += dwa
            dba_ref[...] += dba
            dlam_ref[...] += dlam

    tokens = batch * seq
    blk = pl.BlockSpec((seq, LANES), lambda c, b: (b, c))
    vec = pl.BlockSpec((1, LANES), lambda c, b: (0, c))
    mat = pl.BlockSpec((None, LANES, LANES), lambda c, b: (c, 0, 0))
    vec_shape = jax.ShapeDtypeStruct((1, D_MODEL), F32)
    mat_shape = jax.ShapeDtypeStruct((N_BLK, LANES, LANES), F32)
    return pl.pallas_call(
        body, name="branch_a_bwd",
        grid=(N_BLK, batch),
        in_specs=[pl.BlockSpec((2, seq, LANES), lambda c, b: (2, b, c)), blk, blk,
                  pl.BlockSpec(memory_space=pl.ANY)] + _lru_param_specs(0),
        out_specs=[pl.BlockSpec((2, seq, LANES), lambda c, b: (2, b, c)),
                   pl.BlockSpec((CONV_WIDTH, LANES), lambda c, b: (0, c)), vec, mat, vec, mat, vec, vec],
        out_shape=[jax.ShapeDtypeStruct((N_GROUPS, tokens, D_MODEL), _MXU_DTYPE),
                   jax.ShapeDtypeStruct((CONV_WIDTH, D_MODEL), F32), vec_shape, mat_shape, vec_shape, mat_shape,
                   vec_shape, vec_shape],
        scratch_shapes=[pltpu.VMEM((seq, LANES), F32)] * 3,
        input_output_aliases={3: 0},
        compiler_params=_params(("parallel", "arbitrary")),
    )(z, hl, dya, dz, conv_w, conv_b, wx, bx, wa, ba, lam)


def _chunk_masks(transposed=False):
    r = lax.broadcasted_iota(jnp.int32, (CHUNK, CHUNK), 0)
    c = lax.broadcasted_iota(jnp.int32, (CHUNK, CHUNK), 1)
    return r <= c if transposed else r >= c


def _row_blocks(seq, fn):
    block = min(256, seq)

    def trip(i, carry):
        fn(pl.ds(pl.multiple_of(i * block, block), block))
        return carry

    lax.fori_loop(0, seq // block, trip, 0)


def _hgrn_prepare(z_ref, lb_ref, f_scr, logf_scr, qh_scr, seq):
    lb = _sigmoid(lb_ref[0:1, :] - lb_ref[1:2, :])

    def block(rows):
        q = z_ref[0, rows, :]
        f = lb + (1.0 - lb) * _sigmoid(z_ref[1, rows, :])
        f_scr[rows, :] = f
        logf_scr[rows, :] = jnp.log(f)
        qh_scr[rows, :] = q * _sigmoid(q)

    _row_blocks(seq, block)
    return lb


def _cumsum_rows(x, reverse=False):
    shift = _shift_up if reverse else _shift_down
    d = 1
    while d < x.shape[0]:
        x = x + shift(x, d)
        d *= 2
    return x


def _lane_mean(x):
    return jnp.mean(x, axis=-1, keepdims=True)


def _token_contractions(lhs_scr, rhs_scr, out_ref, seq):
    rows_id = lax.broadcasted_iota(jnp.int32, (LANES, LANES), 0)

    def transposed(p):
        rows = pl.ds(pl.multiple_of(p * LANES, LANES), LANES)
        return jnp.transpose(lhs_scr[rows, :]).astype(_MXU_DTYPE), rhs_scr[rows, :]

    def contract(p, s):
        lhs_t, rhs = s
        return (_mm(lhs_t, jnp.where(rows_id < CHUNK, rhs, 0.0)), _mm(lhs_t, jnp.where(rows_id >= CHUNK, rhs, 0.0)))

    def store(p, out):
        out_ref[2 * p] = out[0]
        out_ref[2 * p + 1] = out[1]

    _independent_trips(seq // LANES, [transposed, contract], store)


def _chunk_rows(c):
    return pl.ds(pl.multiple_of(c * CHUNK, CHUNK), CHUNK)


def _chunk_terms(c, z_ref, f_scr, qh_scr, b_scr):
    rows = _chunk_rows(c)
    b = b_scr[rows, :]
    b_mid = b_scr[pl.ds(c * CHUNK + CHUNK // 2, 1), :]
    b_last = b_scr[pl.ds(c * CHUNK + CHUNK - 1, 1), :]
    qh = qh_scr[rows, :]
    k = 1.0 - f_scr[rows, :]
    v = z_ref[2, rows, :]
    e_q = jnp.exp(b - b_mid) * HG_SCALE
    e_k = jnp.exp(b_mid - b)
    e_qi = jnp.exp(b) * HG_SCALE
    e_ks = jnp.exp(b_last - b)
    decay = jnp.exp(b_last)
    return rows, qh, k, v, e_q, e_k, e_qi, e_ks, decay


def _independent_trips(n, stages, store, group=CHUNKS_IN_FLIGHT):
    stages = stages if isinstance(stages, (list, tuple)) else [stages]
    group = min(group, n)

    def trip(g, carry):
        ids = [g * group + i for i in range(group)]
        state = [stages[0](c) for c in ids]
        for stage in stages[1:]:
            state = [stage(c, s) for c, s in zip(ids, state)]
        for c, s in zip(ids, state):
            store(c, s)
        return carry

    lax.fori_loop(0, n // group, trip, 0)


def _branch_b_fwd(z, lb_logits, hg_g, batch, seq):
    tokens = batch * seq
    n_chunks = seq // CHUNK

    def body(z_ref, lb_ref, g_ref, yb_ref, st_ref, f_scr, logf_scr, qh_scr, b_scr, o_scr, qi_scr, ks_scr, dec_scr):
        _hgrn_prepare(z_ref, lb_ref, f_scr, logf_scr, qh_scr, seq)
        causal = _chunk_masks()
        gain = g_ref[...]

        def cumulate(c):
            return _cumsum_rows(logf_scr[_chunk_rows(c), :])

        def store_cumulated(c, b):
            b_scr[_chunk_rows(c), :] = b

        def scores(c):
            _, qh, k, v, e_q, e_k, e_qi, e_ks, decay = _chunk_terms(c, z_ref, f_scr, qh_scr, b_scr)
            return _mm_nt(qh * e_q, k * e_k), v, qh * e_qi, k * e_ks, decay

        def within_chunk(c, s):
            att, v, q_int, k_st, decay = s
            return _mm(jnp.where(causal, att, 0.0), v), q_int, k_st, decay

        def store_within_chunk(c, out):
            rows = _chunk_rows(c)
            o_scr[rows, :], qi_scr[rows, :], ks_scr[rows, :], dec_scr[pl.ds(c, 1), :] = out

        def carry_state(c, state_t):
            update = st_ref[c]
            st_ref[c] = state_t
            return state_t * dec_scr[pl.ds(c, 1), :] + update

        def finish(c):
            rows = _chunk_rows(c)
            o = o_scr[rows, :] + _mm_nt(qi_scr[rows, :], st_ref[c])
            r = lax.rsqrt(_lane_mean(o * o) + EPS)
            gb = z_ref[3, rows, :]
            return (((o * r) * gain) * (gb * _sigmoid(gb))).astype(_MXU_DTYPE)

        def store_finished(c, yb):
            yb_ref[_chunk_rows(c), :] = yb

        _independent_trips(n_chunks, cumulate, store_cumulated)
        _independent_trips(n_chunks, [scores, within_chunk], store_within_chunk)
        _token_contractions(z_ref.at[2], ks_scr, st_ref, seq)
        lax.fori_loop(0, n_chunks, carry_state, jnp.zeros((LANES, LANES), F32))
        _independent_trips(n_chunks, finish, store_finished, group=2 * CHUNKS_IN_FLIGHT)

    seq_buf = pltpu.VMEM((seq, LANES), F32)
    return pl.pallas_call(
        body, name="branch_b_fwd",
        grid=(batch, N_BLK),
        in_specs=[pl.BlockSpec((4, seq, LANES), lambda b, h: (0, b, h)),
                  pl.BlockSpec((2, LANES), lambda b, h: (0, h)),
                  pl.BlockSpec((1, LANES), lambda b, h: (0, 0))],
        out_specs=[pl.BlockSpec((seq, LANES), lambda b, h: (b, h)),
                   pl.BlockSpec((None, n_chunks, LANES, LANES), lambda b, h: (b * N_BLK + h, 0, 0, 0))],
        out_shape=[jax.ShapeDtypeStruct((tokens, D_MODEL), _MXU_DTYPE),
                   jax.ShapeDtypeStruct((batch * N_BLK, n_chunks, LANES, LANES), F32)],
        scratch_shapes=[seq_buf] * 7 + [pltpu.VMEM((n_chunks, LANES), F32)],
        compiler_params=_params(("parallel", "parallel")),
    )(z, lb_logits, hg_g)


def _branch_b_bwd(z, states, dyb, dz, lb_logits, hg_g, batch, seq):
    n_chunks = seq // CHUNK

    def body(z_ref, st_ref, dyb_ref, dz_in_ref, lb_ref, g_ref, dz_ref, dlog_ref, dg_ref,
             f_scr, logf_scr, qh_scr, b_scr, do_scr, qi_scr, dqh_scr, df_scr, dec_scr, dgp_scr, dlb_scr, dst_scr):
        del dz_in_ref
        first = (pl.program_id(0) == 0) & (pl.program_id(1) == 0)
        lb = _hgrn_prepare(z_ref, lb_ref, f_scr, logf_scr, qh_scr, seq)
        causal = _chunk_masks()
        anti_causal = _chunk_masks(transposed=True)
        gain = g_ref[...]

        @pl.when(first)
        def _():
            dg_ref[...] = jnp.zeros_like(dg_ref)

        @pl.when(pl.program_id(1) == 0)
        def _():
            dlb_scr[...] = jnp.zeros_like(dlb_scr)

        def cumulate(c):
            return _cumsum_rows(logf_scr[_chunk_rows(c), :])

        def store_cumulated(c, b):
            b_scr[_chunk_rows(c), :] = b

        def scores(c):
            _, qh, k, v, e_q, e_k, e_qi, e_ks, decay = _chunk_terms(c, z_ref, f_scr, qh_scr, b_scr)
            q_int = qh * e_qi
            return _mm_nt(qh * e_q, k * e_k), _mm_nt(q_int, st_ref[c]), v, q_int, decay

        def output_gradient(c, s):
            att, o_inter, v, q_int, decay = s
            rows = _chunk_rows(c)
            o = _mm(jnp.where(causal, att, 0.0), v) + o_inter
            r = lax.rsqrt(_lane_mean(o * o) + EPS)
            o_n = o * r
            gb = z_ref[3, rows, :]
            sgb = _sigmoid(gb)
            dyb_c = dyb_ref[rows, :]
            d_ong = dyb_c * (gb * sgb)
            d_gb = (dyb_c * (o_n * gain) * (sgb * (1.0 + gb * (1.0 - sgb)))).astype(_MXU_DTYPE)
            d_gain = jnp.sum(d_ong * o_n, axis=0, keepdims=True)
            d_on = d_ong * gain
            return d_gb, d_gain, r * (d_on - o_n * _lane_mean(d_on * o_n)), q_int, decay

        def store_output_gradient(c, out):
            rows = _chunk_rows(c)
            dz_ref[3, rows, :], dgp_scr[pl.ds(c, 1), :], do_scr[rows, :], qi_scr[rows, :], dec_scr[pl.ds(c, 1), :] = out

        def carry_state_gradient(cc, d_state_t):
            c = n_chunks - 1 - cc
            update = dst_scr[c]
            dst_scr[c] = d_state_t
            return d_state_t * dec_scr[pl.ds(c, 1), :] + update

        def score_gradients(c):
            rows, qh, k, v, e_q, e_k, e_qi, e_ks, decay = _chunk_terms(c, z_ref, f_scr, qh_scr, b_scr)
            state_t = st_ref[c]
            d_state_t = dst_scr[c]
            d_o = do_scr[rows, :]
            q_in, k_in, q_int, k_st = qh * e_q, k * e_k, qh * e_qi, k * e_ks
            first = (_mm_nt(k_in, q_in), _mm_nt(d_o, v), _mm_nt(v, d_o), _mm_nt(k_st, d_state_t), _mm(d_o, state_t),
                     _mm(v, d_state_t))
            d_decay = jnp.sum(state_t * d_state_t, axis=0, keepdims=True)
            return first, d_o, q_in, k_in, q_int, k_st, e_q, e_k, e_qi, e_ks, decay, d_decay

        def input_gradients(c, s):
            (att_t, d_att, d_att_t, dv_inter, dq_int, dk_st), d_o, q_in, k_in, q_int, k_st, e_q, e_k, e_qi, e_ks, decay, d_decay = s
            rows = _chunk_rows(c)
            d_v = _mm(jnp.where(anti_causal, att_t, 0.0), d_o) + dv_inter
            dq_in = _mm(jnp.where(causal, d_att, 0.0), k_in)
            dk_in = _mm(jnp.where(anti_causal, d_att_t, 0.0), q_in)
            d_k = dk_in * e_k + dk_st * e_ks
            kk = dk_st * k_st
            d_b = dq_in * q_in + dq_int * q_int - dk_in * k_in - kk
            d_b_last = jnp.sum(kk, axis=0, keepdims=True) + decay * d_decay
            d_logf = _cumsum_rows(d_b, reverse=True) + d_b_last
            return d_v.astype(_MXU_DTYPE), dq_in * e_q + dq_int * e_qi, d_logf / f_scr[rows, :] - d_k

        def store_input_gradients(c, out):
            rows = _chunk_rows(c)
            dz_ref[2, rows, :], dqh_scr[rows, :], df_scr[rows, :] = out

        def input_activations(rows):
            q = z_ref[0, rows, :]
            sq = _sigmoid(q)
            dz_ref[0, rows, :] = (dqh_scr[rows, :] * (sq * (1.0 + q * (1.0 - sq)))).astype(_MXU_DTYPE)
            sg = _sigmoid(z_ref[1, rows, :])
            d_f = df_scr[rows, :]
            dz_ref[1, rows, :] = (d_f * (1.0 - lb) * sg * (1.0 - sg)).astype(_MXU_DTYPE)
            dlb_scr[...] += jnp.sum(d_f * (1.0 - sg), axis=0, keepdims=True)

        _independent_trips(n_chunks, cumulate, store_cumulated)
        _independent_trips(n_chunks, [scores, output_gradient], store_output_gradient)
        _token_contractions(do_scr, qi_scr, dst_scr, seq)
        lax.fori_loop(0, n_chunks, carry_state_gradient, jnp.zeros((LANES, LANES), F32))
        _independent_trips(n_chunks, [score_gradients, input_gradients], store_input_gradients)
        dg_ref[...] += jnp.sum(dgp_scr[...], axis=0, keepdims=True)
        _row_blocks(seq, input_activations)
        d_l0 = dlb_scr[...] * lb * (1.0 - lb)
        dlog_ref[0:1, :] = d_l0
        dlog_ref[1:2, :] = -d_l0

    tokens = batch * seq
    seq_buf = pltpu.VMEM((seq, LANES), F32)
    chunk_rows = pltpu.VMEM((n_chunks, LANES), F32)
    return pl.pallas_call(
        body, name="branch_b_bwd",
        grid=(N_BLK, batch),
        in_specs=[pl.BlockSpec((4, seq, LANES), lambda h, b: (0, b, h)),
                  pl.BlockSpec((None, n_chunks, LANES, LANES), lambda h, b: (b * N_BLK + h, 0, 0, 0)),
                  pl.BlockSpec((seq, LANES), lambda h, b: (b, h)),
                  pl.BlockSpec(memory_space=pl.ANY),
                  pl.BlockSpec((2, LANES), lambda h, b: (0, h)),
                  pl.BlockSpec((1, LANES), lambda h, b: (0, 0))],
        out_specs=[pl.BlockSpec((4, seq, LANES), lambda h, b: (0, b, h)),
                   pl.BlockSpec((2, LANES), lambda h, b: (0, h)),
                   pl.BlockSpec((1, LANES), lambda h, b: (0, 0))],
        out_shape=[jax.ShapeDtypeStruct((N_GROUPS, tokens, D_MODEL), _MXU_DTYPE),
                   jax.ShapeDtypeStruct((2, D_MODEL), F32),
                   jax.ShapeDtypeStruct((1, LANES), F32)],
        scratch_shapes=[seq_buf] * 8 + [chunk_rows, chunk_rows, pltpu.VMEM((1, LANES), F32),
                                        pltpu.VMEM((n_chunks, LANES, LANES), F32)],
        input_output_aliases={3: 0},
        compiler_params=_params(("arbitrary", "arbitrary")),
    )(z, states, dyb, dz, lb_logits, hg_g)


def _merge_tail(ya, yb, z, x2d, tgt2d, b_merge, final_g, pa, pb, wo):
    tokens, d = x2d.shape
    tm = min(256, tokens)
    n_tiles = tokens // tm

    def body(ya_ref, yb_ref, z_ref, x_ref, t_ref, bm_ref, fg_ref, pa_hbm, pb_hbm, wo_hbm,
             dya_ref, dyb_ref, dx2_ref, dz_ref, loss_ref, dfg_ref, dbm_ref, dpa_hbm, dpb_hbm, dwo_hbm,
             pa_s, pb_s, wo_s, dpa_s, dpb_s, dwo_s):
        i = pl.program_id(0)

        @pl.when(i == 0)
        def _():
            pltpu.sync_copy(pa_hbm, pa_s)
            pltpu.sync_copy(pb_hbm, pb_s)
            pltpu.sync_copy(wo_hbm, wo_s)
            dpa_s[...] = jnp.zeros_like(dpa_s)
            dpb_s[...] = jnp.zeros_like(dpb_s)
            dwo_s[...] = jnp.zeros_like(dwo_s)
            loss_ref[...] = jnp.zeros_like(loss_ref)
            dfg_ref[...] = jnp.zeros_like(dfg_ref)
            dbm_ref[...] = jnp.zeros_like(dbm_ref)

        ya_t = ya_ref[...]
        yb_t = yb_ref[...]
        out_a = _mm(ya_t, pa_s[...])
        out_b = _mm(yb_t, pb_s[...])
        g_a = _sigmoid(z_ref[0] + bm_ref[:, :d])
        g_b = _sigmoid(z_ref[1] + bm_ref[:, d:])
        mixed = g_a * out_a + g_b * out_b
        x2 = x_ref[...] + _mm(mixed, wo_s[...])
        r = lax.rsqrt(jnp.mean(x2 * x2, axis=-1, keepdims=True) + EPS)
        xn = x2 * r
        fg = fg_ref[...]
        diff = xn * fg - t_ref[...]
        loss_ref[...] += jnp.sum(diff * diff) * (0.5 / d)
        dy = diff * (1.0 / d)
        dfg_ref[...] += jnp.sum(dy * xn, axis=0, keepdims=True)
        dxn = dy * fg
        dx2 = r * (dxn - xn * jnp.mean(dxn * xn, axis=-1, keepdims=True))
        dx2_ref[...] = dx2
        dmixed = _mm_nt(dx2, wo_s[...])
        dwo_s[...] += _mm_tn(mixed, dx2)
        dgm_a = dmixed * out_a * g_a * (1.0 - g_a)
        dgm_b = dmixed * out_b * g_b * (1.0 - g_b)
        dz_ref[0] = dgm_a.astype(_MXU_DTYPE)
        dz_ref[1] = dgm_b.astype(_MXU_DTYPE)
        dbm_ref[:, :d] += jnp.sum(dgm_a, axis=0, keepdims=True)
        dbm_ref[:, d:] += jnp.sum(dgm_b, axis=0, keepdims=True)
        dout_a = dmixed * g_a
        dout_b = dmixed * g_b
        dpa_s[...] += _mm_tn(ya_t, dout_a)
        dpb_s[...] += _mm_tn(yb_t, dout_b)
        dya_ref[...] = _mm_nt(dout_a, pa_s[...])
        dyb_ref[...] = _mm_nt(dout_b, pb_s[...])

        @pl.when(i == n_tiles - 1)
        def _():
            pltpu.sync_copy(dpa_s, dpa_hbm)
            pltpu.sync_copy(dpb_s, dpb_hbm)
            pltpu.sync_copy(dwo_s, dwo_hbm)

    tile = pl.BlockSpec((tm, d), lambda i: (i, 0))
    gm = pl.BlockSpec((2, tm, d), lambda i: (3, i, 0))
    row = lambda n: pl.BlockSpec((1, n), lambda i: (0, 0))
    hbm = pl.BlockSpec(memory_space=pl.ANY)
    act = jax.ShapeDtypeStruct((tokens, d), F32)
    mat = jax.ShapeDtypeStruct((d, d), F32)
    return pl.pallas_call(
        body, name="merge_tail",
        grid=(n_tiles,),
        in_specs=[tile, tile, gm, tile, tile, row(2 * d), row(d), hbm, hbm, hbm],
        out_specs=[tile, tile, tile, gm, row(LANES), row(d), row(2 * d), hbm, hbm, hbm],
        out_shape=[act, act, act, jax.ShapeDtypeStruct((N_GROUPS, tokens, d), _MXU_DTYPE),
                   jax.ShapeDtypeStruct((1, LANES), F32), jax.ShapeDtypeStruct((1, d), F32),
                   jax.ShapeDtypeStruct((1, 2 * d), F32), mat, mat, mat],
        scratch_shapes=[pltpu.VMEM((d, d), _MXU_DTYPE)] * 3 + [pltpu.VMEM((d, d), F32)] * 3,
        compiler_params=_params(("arbitrary",)),
    )(ya, yb, z, x2d, tgt2d, b_merge, final_g, pa, pb, wo)


def _inproj_dw(h_t, dz):
    d, tokens = h_t.shape
    tm = min(512, tokens)

    def body(h_ref, dz_ref, dw_ref):
        part = _mm(h_ref[...], dz_ref[...])

        @pl.when(pl.program_id(1) == 0)
        def _():
            dw_ref[...] = part

        @pl.when(pl.program_id(1) != 0)
        def _():
            dw_ref[...] += part

    def out_index(s, i):
        g = _group_of_slot(s)
        return (g // 2, 0, g % 2)

    return pl.pallas_call(
        body, name="inproj_dw",
        grid=(N_GROUPS, tokens // tm),
        in_specs=[pl.BlockSpec((d, tm), lambda s, i: (0, i)),
                  pl.BlockSpec((None, tm, D_MODEL), lambda s, i: (s, i, 0))],
        out_specs=pl.BlockSpec((None, d, D_MODEL), out_index),
        out_shape=jax.ShapeDtypeStruct((N_SHARDS, d, 2 * D_MODEL), F32),
        compiler_params=_params(("parallel", "arbitrary")),
    )(h_t, dz)


def _inproj_dx(dz, w_all, x2d, dx2, norm_g, scatter=None):
    tokens, d = x2d.shape
    tm = min(512, tokens)
    n_tiles = tokens // tm
    n_big = len(scatter[0]) if scatter else 0

    def body(dz_ref, w_hbm, x_ref, dx2_ref, g_ref, *rest):
        if scatter:
            srcs, small_src = rest[:n_big], rest[2 * n_big]
            dx_ref, dg_ref = rest[2 * n_big + 1:2 * n_big + 3]
            outs, small_out = rest[2 * n_big + 3:3 * n_big + 3], rest[3 * n_big + 3]
            acc, w_res, send_sems, recv_sems, local_sem = rest[3 * n_big + 4:]
            copies = _scatter_copies(srcs, small_src, outs, small_out, send_sems, recv_sems, local_sem)
        else:
            dx_ref, dg_ref, acc, w_res = rest
            copies = []
        s = pl.program_id(1)

        @pl.when((pl.program_id(0) == 0) & (s == 0))
        def _():
            for cp in copies:
                cp.start()
            _load_w_in_by_slot(w_hbm, w_res)

        part = _mm_nt(dz_ref[...], w_res[s])

        @pl.when(s == 0)
        def _():
            acc[...] = part

        @pl.when(s != 0)
        def _():
            acc[...] += part

        @pl.when((pl.program_id(0) == 0) & (s == 0))
        def _():
            dg_ref[...] = jnp.zeros_like(dg_ref)

        @pl.when(s == N_GROUPS - 1)
        def _():
            x = x_ref[...]
            r = lax.rsqrt(jnp.mean(x * x, axis=-1, keepdims=True) + EPS)
            xn = x * r
            dh = acc[...]
            dg_ref[...] += jnp.sum(dh * xn, axis=0, keepdims=True)
            dxn = dh * g_ref[...]
            dx_ref[...] = r * (dxn - xn * jnp.mean(dxn * xn, axis=-1, keepdims=True)) + dx2_ref[...]

        @pl.when((pl.program_id(0) == n_tiles - 1) & (s == N_GROUPS - 1))
        def _():
            for cp in copies:
                cp.wait()

    tile = pl.BlockSpec((tm, d), lambda i, s: (i, 0))
    hbm = pl.BlockSpec(memory_space=pl.ANY)
    in_specs = [pl.BlockSpec((None, tm, D_MODEL), lambda i, s: (s, i, 0)), hbm, tile, tile,
                pl.BlockSpec((1, d), lambda i, s: (0, 0))]
    out_specs = [tile, pl.BlockSpec((1, d), lambda i, s: (0, 0))]
    out_shape = [jax.ShapeDtypeStruct((tokens, d), F32), jax.ShapeDtypeStruct((1, d), F32)]
    scratch = [pltpu.VMEM((tm, d), F32), pltpu.VMEM((N_GROUPS, d, D_MODEL), _MXU_DTYPE)]
    operands, aliases = [dz, w_all, x2d, dx2, norm_g], {}
    if scatter:
        bigs, by_chip, small_piece = scatter
        n_sem = 3 * n_big + N_DEV - 1
        in_specs += [hbm] * (2 * n_big + 1)
        out_specs += [hbm] * (n_big + 1)
        out_shape += [jax.ShapeDtypeStruct(g.shape, g.dtype) for g in by_chip]
        out_shape.append(jax.ShapeDtypeStruct((N_DEV, PIECE_ROWS, LANES), F32))
        scratch += [pltpu.SemaphoreType.DMA((n_sem,)), pltpu.SemaphoreType.DMA((n_sem,)), pltpu.SemaphoreType.DMA]
        operands += [*bigs, *by_chip, small_piece]
        aliases = {5 + n_big + a: 2 + a for a in range(n_big)}
    return pl.pallas_call(
        body, name="inproj_dx", grid=(n_tiles, N_GROUPS), in_specs=in_specs, out_specs=out_specs, out_shape=out_shape,
        scratch_shapes=scratch, input_output_aliases=aliases,
        compiler_params=_params(("arbitrary", "arbitrary")),
    )(*operands)


def _row_tile(rows, cols, itemsize=4, budget=2 * 1024 * 1024):
    tr = rows
    while tr * cols * itemsize > budget and tr % 16 == 0:
        tr //= 2
    return tr


def _cast_into_slot(a, chip, dtype, name):
    rows, cols = a.shape
    tr = _row_tile(rows, cols)

    def body(chip_ref, a_ref, o_ref):
        del chip_ref
        o_ref[...] = a_ref[...].astype(dtype)

    grid_spec = pltpu.PrefetchScalarGridSpec(
        num_scalar_prefetch=1, grid=(rows // tr,),
        in_specs=[pl.BlockSpec((tr, cols), lambda i, chip_ref: (i, 0))],
        out_specs=pl.BlockSpec((None, tr, cols), lambda i, chip_ref: (chip_ref[0], i, 0)))
    return pl.pallas_call(body, name=name, grid_spec=grid_spec,
                          out_shape=jax.ShapeDtypeStruct((N_SHARDS, rows, cols), dtype),
                          compiler_params=_params(("arbitrary",)))(chip, a)


def _sum_slots(stack, name):
    n, rows, cols = stack.shape
    tr = _row_tile(rows, cols * n)

    def body(s_ref, o_ref):
        total = s_ref[0].astype(F32)
        for k in range(1, n):
            total = total + s_ref[k].astype(F32)
        o_ref[...] = total

    return pl.pallas_call(body, name=name, grid=(rows // tr,),
                          in_specs=[pl.BlockSpec((n, tr, cols), lambda i: (0, i, 0))],
                          out_specs=pl.BlockSpec((tr, cols), lambda i: (i, 0)),
                          out_shape=jax.ShapeDtypeStruct((rows, cols), F32),
                          compiler_params=_params(("parallel",)))(stack)


def _add_half(full, landed, place, name):
    n, rows, cols = full.shape
    half = rows // 2
    tr = _row_tile(half, cols)
    nb = half // tr

    def body(place_ref, a_ref, b_ref, o_ref, own_ref):
        total = (a_ref[...] + b_ref[...]).astype(_MXU_DTYPE)
        o_ref[...] = total

        @pl.when(pl.program_id(1) == place_ref[1])
        def _():
            own_ref[...] = total

    grid_spec = pltpu.PrefetchScalarGridSpec(
        num_scalar_prefetch=1, grid=(nb, n),
        in_specs=[pl.BlockSpec((None, tr, cols), lambda i, j, place_ref: (j, place_ref[0] * nb + i, 0)),
                  pl.BlockSpec((None, tr, cols), lambda i, j, place_ref: (j, i, 0))],
        out_specs=[pl.BlockSpec((None, tr, cols), lambda i, j, place_ref: (j, i, 0)),
                   pl.BlockSpec((None, tr, cols), lambda i, j, place_ref: (place_ref[1], i, 0))])
    shape = jax.ShapeDtypeStruct((n, half, cols), _MXU_DTYPE)
    return pl.pallas_call(body, name=name, grid_spec=grid_spec, out_shape=[shape, shape],
                          compiler_params=_params(("parallel", "arbitrary")))(place, full, landed)


def _adamw_update(w, grad, m, v):
    c1 = 1.0 - ADAM_B1 ** ADAM_STEP
    c2 = 1.0 - ADAM_B2 ** ADAM_STEP
    nm = ADAM_B1 * m + (1.0 - ADAM_B1) * grad
    nv = ADAM_B2 * v + (1.0 - ADAM_B2) * (grad * grad)
    return (-ADAM_LR) * ((nm / c1) / (jnp.sqrt(nv / c2) + ADAM_EPS) + ADAM_WD * w), nm, nv


def _adamw(w, g, m, v, name):
    rows, cols = w.shape
    tr = _row_tile(rows, cols, budget=1024 * 1024)

    def body(w_ref, g_ref, m_ref, v_ref, d_ref, nm_ref, nv_ref):
        d_ref[...], nm_ref[...], nv_ref[...] = _adamw_update(w_ref[...], g_ref[...], m_ref[...], v_ref[...])

    spec = pl.BlockSpec((tr, cols), lambda i: (i, 0))
    shape = jax.ShapeDtypeStruct((rows, cols), F32)
    return pl.pallas_call(body, name=name, grid=(rows // tr,), in_specs=[spec] * 4, out_specs=[spec] * 3,
                          out_shape=[shape] * 3, compiler_params=_params(("parallel",)))(w, g, m, v)


def _adamw_halves(w, g_mine, g_sibling, m, v, core, name):
    rows, cols = w.shape
    half = rows // 2
    tr = _row_tile(half, cols, budget=1024 * 1024)
    nb = half // tr

    def body(core_ref, w_ref, gm_ref, gs_ref, m_ref, v_ref, g_ref, d_ref, nm_ref, nv_ref):
        mine = pl.program_id(0) // nb == core_ref[0]
        grad = jnp.where(mine, gm_ref[...], gs_ref[...])
        g_ref[...] = grad
        d_ref[...], nm_ref[...], nv_ref[...] = _adamw_update(w_ref[...], grad, m_ref[...], v_ref[...])

    spec = pl.BlockSpec((tr, cols), lambda i, core_ref: (i, 0))
    half_spec = pl.BlockSpec((tr, cols), lambda i, core_ref: (i % nb, 0))
    grid_spec = pltpu.PrefetchScalarGridSpec(num_scalar_prefetch=1, grid=(rows // tr,),
                                             in_specs=[spec, half_spec, half_spec, spec, spec], out_specs=[spec] * 4)
    shape = jax.ShapeDtypeStruct((rows, cols), F32)
    return pl.pallas_call(body, name=name, grid_spec=grid_spec, out_shape=[shape] * 4,
                          compiler_params=_params(("parallel",)))(core, w, g_mine, g_sibling, m, v)


def _local_step(x, loss_target, w_all, pa, pb, wo, conv_w, b_merge, conv_b, rg_wx, rg_bx, rg_wa, rg_ba,
                rg_lambda, hg_lb_logits, hg_norm_g, norm_g, final_norm_g, gather=None, start_reduction=None):
    batch, seq, d = x.shape
    x2d = x.reshape(batch * seq, d)
    tgt2d = loss_target.reshape(batch * seq, d)
    if gather is None:
        z, h_t = _inproj_fwd(x2d, norm_g, w_all)
    else:
        z, h_t, (w_all, pa, pb, wo), cw_all = _inproj_fwd_gather(x2d, norm_g, *gather)
        pa, pb, wo = (t.reshape(d, d) for t in (pa, pb, wo))
        conv_w = jnp.transpose(cw_all, (1, 0, 2)).reshape(CONV_WIDTH, d)
    lru = (conv_w, conv_b, rg_wx, rg_bx, rg_wa, rg_ba, rg_lambda)
    ya, hl = _branch_a_fwd(z, *lru, batch, seq)
    yb, states = _branch_b_fwd(z, hg_lb_logits, hg_norm_g, batch, seq)
    dya, dyb, dx2, dz, loss, d_final_g, d_b_merge, d_pa, d_pb, d_wo = _merge_tail(
        ya, yb, z, x2d, tgt2d, b_merge, final_norm_g, pa, pb, wo)
    dz, d_lb_logits, d_hg_g = _branch_b_bwd(z, states, dyb, dz, hg_lb_logits, hg_norm_g, batch, seq)
    dz, d_conv_w, d_conv_b, d_wx, d_bx, d_wa, d_ba, d_lam = _branch_a_bwd(z, hl, dya, dz, *lru, batch, seq)
    d_w_in = _inproj_dw(h_t, dz)
    big = (d_w_in, d_pa, d_pb, d_wo)
    small = dict(b_merge=d_b_merge, conv_w=d_conv_w, conv_b=d_conv_b, rg_wx=d_wx, rg_bx=d_bx, rg_wa=d_wa,
                 rg_ba=d_ba, rg_lambda=d_lam, hg_lb_logits=d_lb_logits, hg_norm_g=d_hg_g,
                 norm_g=jnp.zeros((1, d), F32), final_norm_g=d_final_g)
    if start_reduction is None:
        grad_x, small["norm_g"] = _inproj_dx(dz, w_all, x2d, dx2, norm_g)
        return loss[0, 0], grad_x.reshape(batch, seq, d), big, small
    grad_x, d_norm_g, *scattered = _inproj_dx(dz, w_all, x2d, dx2, norm_g, scatter=start_reduction(big, small))
    return loss[0, 0], grad_x.reshape(batch, seq, d), d_norm_g, scattered


_SMALL_ORDER = ("b_merge", "conv_w", "conv_b", "rg_wx", "rg_bx", "rg_wa", "rg_ba", "rg_lambda", "hg_lb_logits",
                "hg_norm_g", "norm_g", "final_norm_g")
N_DEV = 8
PIECE_ROWS = 272


def _pack_small(tree):
    flat = jnp.concatenate([tree[k].reshape(-1) for k in _SMALL_ORDER])
    flat = jnp.pad(flat, (0, N_DEV * PIECE_ROWS * LANES - flat.shape[0]))
    return flat.reshape(N_DEV * PIECE_ROWS, LANES)


def _unpack_small(packed, like):
    flat = packed.reshape(-1)
    out, pos = {}, 0
    for k in _SMALL_ORDER:
        n = like[k].size
        out[k] = flat[pos:pos + n].reshape(like[k].shape)
        pos += n
    return out


def _mesh_position():
    x, y, c = lax.axis_index("x"), lax.axis_index("y"), lax.axis_index("c")
    other_chips = [(1 - x, y), (x, 1 - y), (1 - x, 1 - y)]
    return x, y, c, other_chips


def _other_devices(x, y, c):
    flips = [(fx, fy, fc) for fx in (0, 1) for fy in (0, 1) for fc in (0, 1) if (fx, fy, fc) != (0, 0, 0)]
    return [(jnp.where(fx, 1 - x, x), jnp.where(fy, 1 - y, y), jnp.where(fc, 1 - c, c)) for fx, fy, fc in flips]


def _remote(src, dst, send_sems, recv_sems, k, device):
    return pltpu.make_async_remote_copy(src_ref=src, dst_ref=dst, send_sem=send_sems.at[k], recv_sem=recv_sems.at[k],
                                        device_id=device, device_id_type=MESH)


def _exchange_halves(bigs, small):
    n_big = len(bigs)
    n_sem = n_big + N_DEV - 1

    def body(*refs):
        srcs, small_src = refs[:n_big], refs[n_big]
        outs, small_out = refs[n_big + 1:2 * n_big + 1], refs[2 * n_big + 1]
        send_sems, recv_sems, local_sem = refs[2 * n_big + 2:]
        x, y, c, _ = _mesh_position()
        me, sibling = 4 * x + 2 * y + c, (x, y, 1 - c)
        mine = pltpu.make_async_copy(small_src.at[pl.ds(me * PIECE_ROWS, PIECE_ROWS), :], small_out.at[me], local_sem)
        mine.start()
        copies = []
        for a in range(n_big):
            hs = srcs[a].shape[1] // 2
            copies.append(_remote(srcs[a].at[:, pl.ds((1 - c) * hs, hs), :], outs[a], send_sems, recv_sems, a, sibling))
        for k, (px, py, pc) in enumerate(_other_devices(x, y, c)):
            piece = small_src.at[pl.ds((4 * px + 2 * py + pc) * PIECE_ROWS, PIECE_ROWS), :]
            copies.append(_remote(piece, small_out.at[me], send_sems, recv_sems, n_big + k, (px, py, pc)))
        for cp in copies:
            cp.start()
        for cp in copies:
            cp.wait()
        mine.wait()

    hbm = pl.BlockSpec(memory_space=pl.ANY)
    out_shape = [jax.ShapeDtypeStruct((g.shape[0], g.shape[1] // 2, g.shape[2]), F32) for g in bigs]
    out_shape.append(jax.ShapeDtypeStruct((N_DEV, PIECE_ROWS, LANES), F32))
    return pl.pallas_call(
        body, name="exchange_halves",
        in_specs=[hbm] * (n_big + 1), out_specs=[hbm] * (n_big + 1), out_shape=out_shape,
        scratch_shapes=[pltpu.SemaphoreType.DMA((n_sem,)), pltpu.SemaphoreType.DMA((n_sem,)), pltpu.SemaphoreType.DMA],
    )(*bigs, small)


def _scatter_copies(srcs, small_src, outs, small_out, send_sems, recv_sems, local_sem):
    n_big = len(srcs)
    x, y, c, chips = _mesh_position()
    chip, me = 2 * x + y, 4 * x + 2 * y + c
    copies = [pltpu.make_async_copy(small_src, small_out.at[me], local_sem)]
    for a in range(n_big):
        for j, (cx, cy) in enumerate(chips):
            copies.append(_remote(srcs[a].at[2 * cx + cy], outs[a].at[chip], send_sems, recv_sems, 3 * a + j, (cx, cy, c)))
    for k, peer in enumerate(_other_devices(x, y, c)):
        copies.append(_remote(small_src, small_out.at[me], send_sems, recv_sems, 3 * n_big + k, peer))
    return copies


def _swap_halves(halves, vec):
    n_big = len(halves)

    def body(*refs):
        srcs, vec_src = refs[:n_big], refs[n_big]
        outs, vec_out = refs[n_big + 1:2 * n_big + 1], refs[2 * n_big + 1]
        send_sems, recv_sems, local_sem = refs[2 * n_big + 2:]
        x, y, c, _ = _mesh_position()
        me = 4 * x + 2 * y + c
        copies = [pltpu.make_async_copy(vec_src, vec_out.at[me], local_sem)]
        copies += [_remote(srcs[a], outs[a], send_sems, recv_sems, a, (x, y, 1 - c)) for a in range(n_big)]
        copies += [_remote(vec_src, vec_out.at[me], send_sems, recv_sems, n_big + k, peer)
                   for k, peer in enumerate(_other_devices(x, y, c))]
        for cp in copies:
            cp.start()
        for cp in copies:
            cp.wait()

    hbm = pl.BlockSpec(memory_space=pl.ANY)
    n_sem = n_big + N_DEV - 1
    return pl.pallas_call(
        body, name="swap_halves",
        in_specs=[hbm] * (n_big + 1), out_specs=[hbm] * (n_big + 1),
        out_shape=[jax.ShapeDtypeStruct(h.shape, F32) for h in halves] + [jax.ShapeDtypeStruct((N_DEV,) + vec.shape, F32)],
        scratch_shapes=[pltpu.SemaphoreType.DMA((n_sem,)), pltpu.SemaphoreType.DMA((n_sem,)), pltpu.SemaphoreType.DMA],
    )(*halves, vec)


def kernel(x, w_in, b_merge, conv_w, conv_b, rg_wx, rg_bx, rg_wa, rg_ba, rg_lambda, hg_lb_logits, hg_norm_g, proj_a, proj_b, w_out, norm_g, final_norm_g, loss_target, m_w_in, m_b_merge, m_conv_w, m_conv_b, m_rg_wx, m_rg_bx, m_rg_wa, m_rg_ba, m_rg_lambda, m_hg_lb_logits, m_hg_norm_g, m_proj_a, m_proj_b, m_w_out, m_norm_g, m_final_norm_g, v_w_in, v_b_merge, v_conv_w, v_conv_b, v_rg_wx, v_rg_bx, v_rg_wa, v_rg_ba, v_rg_lambda, v_hg_lb_logits, v_hg_norm_g, v_proj_a, v_proj_b, v_w_out, v_norm_g, v_final_norm_g):
    d = D_MODEL
    weights = dict(w_in=w_in, b_merge=b_merge, conv_w=conv_w, conv_b=conv_b, rg_wx=rg_wx, rg_bx=rg_bx, rg_wa=rg_wa,
                   rg_ba=rg_ba, rg_lambda=rg_lambda, hg_lb_logits=hg_lb_logits, hg_norm_g=hg_norm_g, proj_a=proj_a,
                   proj_b=proj_b, w_out=w_out, norm_g=norm_g, final_norm_g=final_norm_g)
    m = dict(w_in=m_w_in, b_merge=m_b_merge, conv_w=m_conv_w, conv_b=m_conv_b, rg_wx=m_rg_wx, rg_bx=m_rg_bx,
             rg_wa=m_rg_wa, rg_ba=m_rg_ba, rg_lambda=m_rg_lambda, hg_lb_logits=m_hg_lb_logits, hg_norm_g=m_hg_norm_g,
             proj_a=m_proj_a, proj_b=m_proj_b, w_out=m_w_out, norm_g=m_norm_g, final_norm_g=m_final_norm_g)
    v = dict(w_in=v_w_in, b_merge=v_b_merge, conv_w=v_conv_w, conv_b=v_conv_b, rg_wx=v_rg_wx, rg_bx=v_rg_bx,
             rg_wa=v_rg_wa, rg_ba=v_rg_ba, rg_lambda=v_rg_lambda, hg_lb_logits=v_hg_lb_logits, hg_norm_g=v_hg_norm_g,
             proj_a=v_proj_a, proj_b=v_proj_b, w_out=v_w_out, norm_g=v_norm_g, final_norm_g=v_final_norm_g)
    big_names = ("w_in", "proj_a", "proj_b", "w_out")

    core = lax.axis_index("c").astype(jnp.int32).reshape(1)
    chip = (2 * lax.axis_index("x") + lax.axis_index("y")).astype(jnp.int32)

    slotted = [_cast_into_slot(weights[k][0], chip.reshape(1), _MXU_DTYPE, f"cast_{k}") for k in big_names]
    conv_slotted = _cast_into_slot(conv_w[0], chip.reshape(1), F32, "slot_conv_w")

    small_shapes = {}

    def start_reduction(big_grads, small_grads):
        small_shapes.update({k: t.shape for k, t in small_grads.items()})
        bigs = [big_grads[0]] + [g.reshape(N_SHARDS, d // N_SHARDS, d) for g in big_grads[1:]]
        *landed, small_landed = _exchange_halves(bigs, _pack_small(small_grads))
        place = jnp.concatenate([core, chip.reshape(1)])
        sums = [_add_half(g, l, place, f"add_half_{a}") for a, (g, l) in enumerate(zip(bigs, landed))]
        return [s[0] for s in sums], [s[1] for s in sums], _sum_slots(small_landed, "sum_small")

    loss_part, grad_x, d_norm_g, (*by_chip, small_all) = _local_step(
        x, loss_target, None, None, None, None, None,
        b_merge, conv_b, rg_wx[0], rg_bx.reshape(1, d), rg_wa[0], rg_ba.reshape(1, d), rg_lambda, hg_lb_logits,
        hg_norm_g, norm_g, final_norm_g.reshape(1, d), gather=(slotted, conv_slotted, chip.reshape(1)),
        start_reduction=start_reduction)
    loss = lax.psum(loss_part, ("x", "y", "c"))

    mine = [_sum_slots(s, f"sum_chips_{a}") for a, s in enumerate(by_chip)]
    *theirs, norm_g_parts = _swap_halves(mine, d_norm_g.reshape(SUBLANES, LANES))
    small_red = _unpack_small(small_all, {k: jax.ShapeDtypeStruct(s, F32) for k, s in small_shapes.items()})
    small_red["norm_g"] = _sum_slots(norm_g_parts, "sum_norm_g").reshape(1, d)

    grads, delta, new_m, new_v = {}, {}, {}, {}
    for k, g_mine, g_theirs in zip(big_names, mine, theirs):
        out = _adamw_halves(weights[k][0], g_mine, g_theirs, m[k][0], v[k][0], core, f"adamw_{k}")
        grads[k], delta[k], new_m[k], new_v[k] = (t.reshape(weights[k].shape) for t in out)
    cols = d // N_SHARDS
    g_conv = lax.dynamic_slice(small_red["conv_w"], (0, chip * cols), (CONV_WIDTH, cols))
    grads["conv_w"] = g_conv.reshape(conv_w.shape)
    dl, nm, nv = _adamw(conv_w[0], g_conv, m_conv_w[0], v_conv_w[0], "adamw_conv_w")
    delta["conv_w"], new_m["conv_w"], new_v["conv_w"] = (t.reshape(conv_w.shape) for t in (dl, nm, nv))
    rest = [k for k in _SMALL_ORDER if k != "conv_w"]
    like = {k: (weights[k] if k != "conv_w" else jnp.zeros((CONV_WIDTH, d), F32)) for k in _SMALL_ORDER}
    packs = [_pack_small({k: (t[k] if k != "conv_w" else like[k]) for k in _SMALL_ORDER}) for t in (weights, m, v)]
    g_pack = _pack_small({k: small_red[k].reshape(like[k].shape) for k in _SMALL_ORDER})
    outs = [_unpack_small(p, like) for p in _adamw(packs[0], g_pack, packs[1], packs[2], "adamw_small")]
    for k in rest:
        grads[k] = small_red[k].reshape(weights[k].shape)
        delta[k], new_m[k], new_v[k] = outs[0][k], outs[1][k], outs[2][k]

    order = ("w_in", "b_merge", "conv_w", "conv_b", "rg_wx", "rg_bx", "rg_wa", "rg_ba", "rg_lambda", "hg_lb_logits",
             "hg_norm_g", "proj_a", "proj_b", "w_out", "norm_g", "final_norm_g")
    return (loss, grad_x, *[grads[k] for k in order], *[delta[k] for k in order], *[new_m[k] for k in order],
            *[new_v[k] for k in order])
```

```python
import functools

import jax
import jax.numpy as jnp
from jax import lax
from jax.experimental import pallas as pl
from jax.experimental.pallas import tpu as pltpu

F32 = jnp.float32
_MXU_DTYPE = jnp.bfloat16

D_MODEL = 1024
LANES = 128
SUBLANES = 8
N_BLK = D_MODEL // LANES
N_GROUPS = 8
N_SHARDS = 4
CONV_WIDTH = 4
LRU_C = 8.0
CHUNK = 64
CHUNKS_IN_FLIGHT = 4
HG_SCALE = float(LANES) ** -0.5
EPS = 1e-6
ADAM_LR, ADAM_B1, ADAM_B2, ADAM_EPS, ADAM_WD, ADAM_STEP = 0.001, 0.9, 0.999, 1e-08, 0.01, 10
VMEM_LIMIT = 56 * 1024 * 1024
MESH = pl.DeviceIdType.MESH

_SLOT_TO_GROUP = (2, 3, 4, 5, 0, 1, 6, 7)


def _group_of_slot(s):
    return jnp.where(s < 4, s + 2, jnp.where(s < 6, s - 4, s))


def _mm(a, b):
    return lax.dot_general(a.astype(_MXU_DTYPE), b.astype(_MXU_DTYPE), (((1,), (0,)), ((), ())),
                           preferred_element_type=F32)


def _mm_nt(a, b):
    return lax.dot_general(a.astype(_MXU_DTYPE), b.astype(_MXU_DTYPE), (((1,), (1,)), ((), ())),
                           preferred_element_type=F32)


def _mm_tn(a, b):
    return lax.dot_general(a.astype(_MXU_DTYPE), b.astype(_MXU_DTYPE), (((0,), (0,)), ((), ())),
                           preferred_element_type=F32)


def _sigmoid(x):
    return 0.5 * jnp.tanh(0.5 * x) + 0.5


def _log1p_pos(y):
    series = y * (1.0 - y * (0.5 - y * (1.0 / 3.0 - y * 0.25)))
    return jnp.where(y < 0.01, series, jnp.log(1.0 + y))


def _expm1_neg(y):
    series = y * (1.0 + y * 0.5 * (1.0 + y * (1.0 / 3.0) * (1.0 + y * 0.25 * (1.0 + y * 0.2))))
    return jnp.where(y > -0.02, series, jnp.exp(y) - 1.0)


def _softplus(x):
    return jnp.maximum(x, 0.0) + _log1p_pos(jnp.exp(-jnp.abs(x)))


def _shift_down(x, n):
    rolled = pltpu.roll(x, n, 0)
    edge = SUBLANES if (n < SUBLANES and x.shape[0] > SUBLANES) else x.shape[0]
    rows = lax.broadcasted_iota(jnp.int32, (edge, x.shape[1]), 0)
    head = jnp.where(rows >= n, rolled[:edge], 0.0)
    return head if edge == x.shape[0] else jnp.concatenate([head, rolled[edge:]], axis=0)


def _shift_up(x, n):
    size = x.shape[0]
    rolled = pltpu.roll(x, size - n, 0)
    edge = SUBLANES if (n < SUBLANES and size > SUBLANES) else size
    rows = lax.broadcasted_iota(jnp.int32, (edge, x.shape[1]), 0)
    tail = jnp.where(rows < edge - n, rolled[size - edge:], 0.0)
    return tail if edge == size else jnp.concatenate([rolled[:size - edge], tail], axis=0)


def _params(dims, vmem=VMEM_LIMIT):
    return pltpu.CompilerParams(dimension_semantics=dims, vmem_limit_bytes=vmem)


def _load_w_in_by_slot(w_hbm, w_res):
    for slot, g in enumerate(_SLOT_TO_GROUP):
        pltpu.sync_copy(w_hbm.at[g // 2, :, pl.ds((g % 2) * D_MODEL, D_MODEL)], w_res.at[slot])


def _inproj_fwd(x2d, norm_g, w_all):
    tokens, d = x2d.shape
    tm = min(512, tokens)

    def body(x_ref, g_ref, w_hbm, z_ref, ht_ref, h_scr, w_res):
        @pl.when((pl.program_id(0) == 0) & (pl.program_id(1) == 0))
        def _():
            _load_w_in_by_slot(w_hbm, w_res)

        @pl.when(pl.program_id(1) == 0)
        def _():
            x = x_ref[...]
            r = lax.rsqrt(jnp.mean(x * x, axis=-1, keepdims=True) + EPS)
            h = (x * r) * g_ref[...]
            h_scr[...] = h.astype(_MXU_DTYPE)
            ht_ref[...] = jnp.transpose(h).astype(_MXU_DTYPE)

        z_ref[...] = _mm(h_scr[...], w_res[pl.program_id(1)])

    return pl.pallas_call(
        body, name="inproj_fwd",
        grid=(tokens // tm, N_GROUPS),
        in_specs=[pl.BlockSpec((tm, d), lambda i, s: (i, 0)),
                  pl.BlockSpec((1, d), lambda i, s: (0, 0)),
                  pl.BlockSpec(memory_space=pl.ANY)],
        out_specs=[pl.BlockSpec((None, tm, D_MODEL), lambda i, s: (s, i, 0)),
                   pl.BlockSpec((d, tm), lambda i, s: (0, i))],
        out_shape=[jax.ShapeDtypeStruct((N_GROUPS, tokens, D_MODEL), F32),
                   jax.ShapeDtypeStruct((d, tokens), _MXU_DTYPE)],
        scratch_shapes=[pltpu.VMEM((tm, d), _MXU_DTYPE), pltpu.VMEM((N_GROUPS, d, D_MODEL), _MXU_DTYPE)],
        compiler_params=_params(("arbitrary", "arbitrary")),
    )(x2d, norm_g, w_all)


def _slot_of_group(g):
    return jnp.where(g < 2, g + 4, jnp.where(g < 6, g - 2, g))


def _inproj_fwd_gather(x2d, norm_g, slotted, conv_slotted, chip):
    tokens, d = x2d.shape
    tm = min(512, tokens)
    n_tiles = tokens // tm
    n_big = len(slotted)
    n_sem = 6 * n_big + 3
    last_pass = N_GROUPS - 1

    def shard_of(k, chip_id):
        x, y = chip_id // 2, chip_id % 2
        return 2 * jnp.where(k % 2 == 1, 1 - x, x) + jnp.where(k // 2 == 1, 1 - y, y)

    def body(chip_ref, x_ref, g_ref, *rest):
        bufs, cw = rest[n_big + 1:2 * n_big + 1], rest[2 * n_big + 1]
        z_ref, ht_ref = rest[2 * n_big + 2:2 * n_big + 4]
        h_all, slab, send_sems, recv_sems, slab_sem = rest[2 * n_big + 4:]
        del chip_ref
        p, i = pl.program_id(0), pl.program_id(1)
        x, y, c, chips = _mesh_position()
        me, sibling = 2 * x + y, (x, y, 1 - c)

        def half(a, slot, which):
            hs = bufs[a].shape[1] // 2
            return bufs[a].at[slot, pl.ds(which * hs, hs), :]

        def send(a, j):
            mine = half(a, me, c)
            return _remote(mine, mine, send_sems, recv_sems, 6 * a + j, (chips[j][0], chips[j][1], c))

        def arrival(a, j):
            landed = half(a, 2 * chips[j][0] + chips[j][1], c)
            return _remote(landed, landed, send_sems, recv_sems, 6 * a + j, (chips[j][0], chips[j][1], c))

        def passed_on(a, j, which):
            landed = half(a, 2 * chips[j][0] + chips[j][1], which)
            return _remote(landed, landed, send_sems, recv_sems, 6 * a + 3 + j, sibling)

        def conv_copy(j, slot):
            return _remote(cw.at[slot], cw.at[slot], send_sems, recv_sems, 6 * n_big + j, (chips[j][0], chips[j][1], c))

        def land(a, j):
            arrival(a, j).wait_recv()
            passed_on(a, j, c).start()
            passed_on(a, j, 1 - c).wait_recv()

        @pl.when((p == 0) & (i == 0))
        def _():
            for j in range(3):
                send(0, j).start()

        for j in range(3):
            @pl.when((p == 2 * (j + 1)) & (i == 0))
            def _(j=j):
                land(0, j)
                if j == 0:
                    for a in range(1, n_big):
                        for jj in range(3):
                            send(a, jj).start()
                    for jj in range(3):
                        conv_copy(jj, me).start()

        @pl.when(i == 0)
        def _():
            shard = shard_of(p // 2, me)
            for which in range(2):
                @pl.when(p % 2 == which)
                def _(which=which):
                    cp = pltpu.make_async_copy(bufs[0].at[shard, :, pl.ds(which * D_MODEL, D_MODEL)], slab, slab_sem)
                    cp.start()
                    cp.wait()

        rows = pl.ds(pl.multiple_of(i * tm, tm), tm)

        @pl.when(p == 0)
        def _():
            xt = x_ref[...]
            r = lax.rsqrt(jnp.mean(xt * xt, axis=-1, keepdims=True) + EPS)
            h = (xt * r) * g_ref[...]
            h_all[rows, :] = h.astype(_MXU_DTYPE)
            ht_ref[...] = jnp.transpose(h).astype(_MXU_DTYPE)

        z_ref[...] = _mm(h_all[rows, :], slab[...])

        @pl.when((p == last_pass) & (i == n_tiles - 1))
        def _():
            for a in range(1, n_big):
                for j in range(3):
                    land(a, j)
            for j in range(3):
                conv_copy(j, 2 * chips[j][0] + chips[j][1]).wait_recv()
            for a in range(n_big):
                for j in range(3):
                    send(a, j).wait_send()
                    passed_on(a, j, c).wait_send()
            for j in range(3):
                conv_copy(j, me).wait_send()

    def z_index(p, i, chip_ref):
        g = 2 * shard_of(p // 2, chip_ref[0]) + p % 2
        return (_slot_of_group(g), i, 0)

    def first_pass_tile(p, i, chip_ref):
        return jnp.where(p == 0, i, n_tiles - 1)

    hbm = pl.BlockSpec(memory_space=pl.ANY)
    operands = list(slotted) + [conv_slotted]
    grid_spec = pltpu.PrefetchScalarGridSpec(
        num_scalar_prefetch=1, grid=(N_GROUPS, n_tiles),
        in_specs=[pl.BlockSpec((tm, d), lambda p, i, chip_ref: (first_pass_tile(p, i, chip_ref), 0)),
                  pl.BlockSpec((1, d), lambda p, i, chip_ref: (0, 0))] + [hbm] * (n_big + 1),
        out_specs=[hbm] * (n_big + 1) + [pl.BlockSpec((None, tm, D_MODEL), z_index),
                                         pl.BlockSpec((d, tm), lambda p, i, chip_ref: (0, first_pass_tile(p, i, chip_ref)))],
        scratch_shapes=[pltpu.VMEM((tokens, d), _MXU_DTYPE), pltpu.VMEM((d, D_MODEL), _MXU_DTYPE),
                        pltpu.SemaphoreType.DMA((n_sem,)), pltpu.SemaphoreType.DMA((n_sem,)), pltpu.SemaphoreType.DMA])
    out = pl.pallas_call(
        body, name="inproj_fwd_gather", grid_spec=grid_spec,
        out_shape=[jax.ShapeDtypeStruct(a.shape, a.dtype) for a in operands]
        + [jax.ShapeDtypeStruct((N_GROUPS, tokens, D_MODEL), F32), jax.ShapeDtypeStruct((d, tokens), _MXU_DTYPE)],
        input_output_aliases={3 + a: a for a in range(n_big + 1)},
        compiler_params=_params(("arbitrary", "arbitrary")),
    )(chip, x2d, norm_g, *operands)
    return out[n_big + 1], out[n_big + 2], out[:n_big], out[n_big]


def _lru_gates(xa, cw_ref, cb_ref, wx_ref, bx_ref, wa_ref, ba_ref, lam_ref):
    xc = (cb_ref[...] + cw_ref[3:4, :] * xa + cw_ref[2:3, :] * _shift_down(xa, 1)
          + cw_ref[1:2, :] * _shift_down(xa, 2) + cw_ref[0:1, :] * _shift_down(xa, 3))
    gi = _sigmoid(_mm(xc, wx_ref[...]) + bx_ref[...])
    gr = _sigmoid(_mm(xc, wa_ref[...]) + ba_ref[...])
    sp = _softplus(-lam_ref[...])
    log_a = (-LRU_C) * gr * sp
    a = jnp.exp(log_a)
    mult = jnp.sqrt(-_expm1_neg(2.0 * log_a))
    return xc, gi, gr, sp, a, mult


def _tile_rows():
    return lax.broadcasted_iota(jnp.int32, (SUBLANES, LANES), 0)


def _scan_forward(a_scr, u_scr, h_scr, seq):
    rows = _tile_rows()

    def tile(j, carry):
        sl = pl.ds(pl.multiple_of(j * SUBLANES, SUBLANES), SUBLANES)
        a = a_scr[sl, :]
        u = u_scr[sl, :]
        for d in (1, 2, 4):
            keep = rows >= d
            a_sh = jnp.where(keep, pltpu.roll(a, d, 0), 1.0)
            u_sh = jnp.where(keep, pltpu.roll(u, d, 0), 0.0)
            u = a * u_sh + u
            a = a * a_sh
        h = u + a * carry
        h_scr[sl, :] = h
        return jnp.broadcast_to(h[SUBLANES - 1:SUBLANES, :], (SUBLANES, LANES))

    lax.fori_loop(0, seq // SUBLANES, tile, jnp.zeros((SUBLANES, LANES), F32))


def _scan_backward(c_scr, d_scr, g_scr, seq):
    rows = _tile_rows()
    n_tiles = seq // SUBLANES

    def tile(jj, carry):
        j = n_tiles - 1 - jj
        sl = pl.ds(pl.multiple_of(j * SUBLANES, SUBLANES), SUBLANES)
        c = c_scr[sl, :]
        g = d_scr[sl, :]
        for d in (1, 2, 4):
            keep = rows < SUBLANES - d
            c_sh = jnp.where(keep, pltpu.roll(c, SUBLANES - d, 0), 1.0)
            g_sh = jnp.where(keep, pltpu.roll(g, SUBLANES - d, 0), 0.0)
            g = c * g_sh + g
            c = c * c_sh
        g = g + c * carry
        g_scr[sl, :] = g
        return jnp.broadcast_to(g[0:1, :], (SUBLANES, LANES))

    lax.fori_loop(0, n_tiles, tile, jnp.zeros((SUBLANES, LANES), F32))


def _lru_param_specs(cb_axis):
    def pick(*ids):
        return ids[cb_axis]

    vec = pl.BlockSpec((1, LANES), lambda *ids: (0, pick(*ids)))
    mat = pl.BlockSpec((None, LANES, LANES), lambda *ids: (pick(*ids), 0, 0))
    return [pl.BlockSpec((CONV_WIDTH, LANES), lambda *ids: (0, pick(*ids))), vec, mat, vec, mat, vec, vec]


def _branch_a_fwd(z, conv_w, conv_b, wx, bx, wa, ba, lam, batch, seq):
    tokens = batch * seq

    def body(z_ref, cw_ref, cb_ref, wx_ref, bx_ref, wa_ref, ba_ref, lam_ref, ya_ref, hl_ref, a_scr, u_scr):
        xa = z_ref[0]
        ga = z_ref[1]
        xc, gi, _, _, a, mult = _lru_gates(xa, cw_ref, cb_ref, wx_ref, bx_ref, wa_ref, ba_ref, lam_ref)
        a_scr[...] = a
        u_scr[...] = mult * gi * xc
        _scan_forward(a_scr, u_scr, hl_ref, seq)
        ya_ref[...] = (hl_ref[...] * (ga * _sigmoid(ga))).astype(_MXU_DTYPE)

    blk = pl.BlockSpec((seq, LANES), lambda b, c: (b, c))
    return pl.pallas_call(
        body, name="branch_a_fwd",
        grid=(batch, N_BLK),
        in_specs=[pl.BlockSpec((2, seq, LANES), lambda b, c: (2, b, c))] + _lru_param_specs(1),
        out_specs=[blk, blk],
        out_shape=[jax.ShapeDtypeStruct((tokens, D_MODEL), _MXU_DTYPE), jax.ShapeDtypeStruct((tokens, D_MODEL), F32)],
        scratch_shapes=[pltpu.VMEM((seq, LANES), F32), pltpu.VMEM((seq, LANES), F32)],
        compiler_params=_params(("parallel", "parallel")),
    )(z, conv_w, conv_b, wx, bx, wa, ba, lam)


def _branch_a_bwd(z, hl, dya, dz, conv_w, conv_b, wx, bx, wa, ba, lam, batch, seq):
    def body(z_ref, hl_ref, dya_ref, dz_in_ref, cw_ref, cb_ref, wx_ref, bx_ref, wa_ref, ba_ref, lam_ref,
             dz_ref, dcw_ref, dcb_ref, dwx_ref, dbx_ref, dwa_ref, dba_ref, dlam_ref, c_scr, d_scr, g_scr):
        del dz_in_ref
        xa = z_ref[0]
        ga = z_ref[1]
        hl = hl_ref[...]
        dya = dya_ref[...]
        xc, gi, gr, sp, a, mult = _lru_gates(xa, cw_ref, cb_ref, wx_ref, bx_ref, wa_ref, ba_ref, lam_ref)
        sga = _sigmoid(ga)
        dz_ref[1] = (dya * hl * (sga * (1.0 + ga * (1.0 - sga)))).astype(_MXU_DTYPE)
        c_scr[...] = _shift_up(a, 1)
        d_scr[...] = dya * (ga * sga)
        _scan_backward(c_scr, d_scr, g_scr, seq)
        g = g_scr[...]
        da = g * _shift_down(hl, 1)
        dmult = g * gi * xc
        dgi = g * mult * xc
        dxc = g * mult * gi
        dlog_a = da * a - dmult * (a * a) / mult
        dgr = dlog_a * (-LRU_C) * sp
        dsp = jnp.sum(dlog_a * gr, axis=0, keepdims=True) * (-LRU_C)
        dlam = -dsp * _sigmoid(-lam_ref[...])
        dpi = dgi * gi * (1.0 - gi)
        dpr = dgr * gr * (1.0 - gr)
        dxc = dxc + _mm_nt(dpi, wx_ref[...]) + _mm_nt(dpr, wa_ref[...])
        dwx = _mm_tn(xc, dpi)
        dwa = _mm_tn(xc, dpr)
        dbx = jnp.sum(dpi, axis=0, keepdims=True)
        dba = jnp.sum(dpr, axis=0, keepdims=True)
        ahead = [dxc if k == CONV_WIDTH - 1 else _shift_up(dxc, CONV_WIDTH - 1 - k) for k in range(CONV_WIDTH)]
        dxa = sum(cw_ref[k:k + 1, :] * ahead[k] for k in range(CONV_WIDTH))
        dz_ref[0] = dxa.astype(_MXU_DTYPE)
        dcb = jnp.sum(dxc, axis=0, keepdims=True)
        dcw = [jnp.sum(ahead[k] * xa, axis=0, keepdims=True) for k in range(CONV_WIDTH)]

        @pl.when(pl.program_id(1) == 0)
        def _():
            for k in range(CONV_WIDTH):
                dcw_ref[k:k + 1, :] = dcw[k]
            dcb_ref[...] = dcb
            dwx_ref[...] = dwx
            dbx_ref[...] = dbx
            dwa_ref[...] = dwa
            dba_ref[...] = dba
            dlam_ref[...] = dlam

        @pl.when(pl.program_id(1) != 0)
        def _():
            for k in range(CONV_WIDTH):
                dcw_ref[k:k + 1, :] += dcw[k]
            dcb_ref[...] += dcb
            dwx_ref[...] += dwx
            dbx_ref[...] += dbx
            dwa_ref[...] += dwa
            dba_ref[...] += dba
            dlam_ref[...] += dlam

    tokens = batch * seq
    blk = pl.BlockSpec((seq, LANES), lambda c, b: (b, c))
    vec = pl.BlockSpec((1, LANES), lambda c, b: (0, c))
    mat = pl.BlockSpec((None, LANES, LANES), lambda c, b: (c, 0, 0))
    vec_shape = jax.ShapeDtypeStruct((1, D_MODEL), F32)
    mat_shape = jax.ShapeDtypeStruct((N_BLK, LANES, LANES), F32)
    return pl.pallas_call(
        body, name="branch_a_bwd",
        grid=(N_BLK, batch),
        in_specs=[pl.BlockSpec((2, seq, LANES), lambda c, b: (2, b, c)), blk, blk,
                  pl.BlockSpec(memory_space=pl.ANY)] + _lru_param_specs(0),
        out_specs=[pl.BlockSpec((2, seq, LANES), lambda c, b: (2, b, c)),
                   pl.BlockSpec((CONV_WIDTH, LANES), lambda c, b: (0, c)), vec, mat, vec, mat, vec, vec],
        out_shape=[jax.ShapeDtypeStruct((N_GROUPS, tokens, D_MODEL), _MXU_DTYPE),
                   jax.ShapeDtypeStruct((CONV_WIDTH, D_MODEL), F32), vec_shape, mat_shape, vec_shape, mat_shape,
                   vec_shape, vec_shape],
        scratch_shapes=[pltpu.VMEM((seq, LANES), F32)] * 3,
        input_output_aliases={3: 0},
        compiler_params=_params(("parallel", "arbitrary")),
    )(z, hl, dya, dz, conv_w, conv_b, wx, bx, wa, ba, lam)


def _chunk_masks(transposed=False):
    r = lax.broadcasted_iota(jnp.int32, (CHUNK, CHUNK), 0)
    c = lax.broadcasted_iota(jnp.int32, (CHUNK, CHUNK), 1)
    return r <= c if transposed else r >= c


def _row_blocks(seq, fn):
    block = min(256, seq)

    def trip(i, carry):
        fn(pl.ds(pl.multiple_of(i * block, block), block))
        return carry

    lax.fori_loop(0, seq // block, trip, 0)


def _hgrn_prepare(z_ref, lb_ref, f_scr, logf_scr, qh_scr, seq):
    lb = _sigmoid(lb_ref[0:1, :] - lb_ref[1:2, :])

    def block(rows):
        q = z_ref[0, rows, :]
        f = lb + (1.0 - lb) * _sigmoid(z_ref[1, rows, :])
        f_scr[rows, :] = f
        logf_scr[rows, :] = jnp.log(f)
        qh_scr[rows, :] = q * _sigmoid(q)

    _row_blocks(seq, block)
    return lb


def _cumsum_rows(x, reverse=False):
    shift = _shift_up if reverse else _shift_down
    d = 1
    while d < x.shape[0]:
        x = x + shift(x, d)
        d *= 2
    return x


def _lane_mean(x):
    return jnp.mean(x, axis=-1, keepdims=True)


def _token_contractions(lhs_scr, rhs_scr, out_ref, seq):
    rows_id = lax.broadcasted_iota(jnp.int32, (LANES, LANES), 0)

    def transposed(p):
        rows = pl.ds(pl.multiple_of(p * LANES, LANES), LANES)
        return jnp.transpose(lhs_scr[rows, :]).astype(_MXU_DTYPE), rhs_scr[rows, :]

    def contract(p, s):
        lhs_t, rhs = s
        return (_mm(lhs_t, jnp.where(rows_id < CHUNK, rhs, 0.0)), _mm(lhs_t, jnp.where(rows_id >= CHUNK, rhs, 0.0)))

    def store(p, out):
        out_ref[2 * p] = out[0]
        out_ref[2 * p + 1] = out[1]

    _independent_trips(seq // LANES, [transposed, contract], store)


def _chunk_rows(c):
    return pl.ds(pl.multiple_of(c * CHUNK, CHUNK), CHUNK)


def _chunk_terms(c, z_ref, f_scr, qh_scr, b_scr):
    rows = _chunk_rows(c)
    b = b_scr[rows, :]
    b_mid = b_scr[pl.ds(c * CHUNK + CHUNK // 2, 1), :]
    b_last = b_scr[pl.ds(c * CHUNK + CHUNK - 1, 1), :]
    qh = qh_scr[rows, :]
    k = 1.0 - f_scr[rows, :]
    v = z_ref[2, rows, :]
    e_q = jnp.exp(b - b_mid) * HG_SCALE
    e_k = jnp.exp(b_mid - b)
    e_qi = jnp.exp(b) * HG_SCALE
    e_ks = jnp.exp(b_last - b)
    decay = jnp.exp(b_last)
    return rows, qh, k, v, e_q, e_k, e_qi, e_ks, decay


def _independent_trips(n, stages, store, group=CHUNKS_IN_FLIGHT):
    stages = stages if isinstance(stages, (list, tuple)) else [stages]
    group = min(group, n)

    def trip(g, carry):
        ids = [g * group + i for i in range(group)]
        state = [stages[0](c) for c in ids]
        for stage in stages[1:]:
            state = [stage(c, s) for c, s in zip(ids, state)]
        for c, s in zip(ids, state):
            store(c, s)
        return carry

    lax.fori_loop(0, n // group, trip, 0)


def _branch_b_fwd(z, lb_logits, hg_g, batch, seq):
    tokens = batch * seq
    n_chunks = seq // CHUNK

    def body(z_ref, lb_ref, g_ref, yb_ref, st_ref, f_scr, logf_scr, qh_scr, b_scr, o_scr, qi_scr, ks_scr, dec_scr):
        _hgrn_prepare(z_ref, lb_ref, f_scr, logf_scr, qh_scr, seq)
        causal = _chunk_masks()
        gain = g_ref[...]

        def cumulate(c):
            return _cumsum_rows(logf_scr[_chunk_rows(c), :])

        def store_cumulated(c, b):
            b_scr[_chunk_rows(c), :] = b

        def scores(c):
            _, qh, k, v, e_q, e_k, e_qi, e_ks, decay = _chunk_terms(c, z_ref, f_scr, qh_scr, b_scr)
            return _mm_nt(qh * e_q, k * e_k), v, qh * e_qi, k * e_ks, decay

        def within_chunk(c, s):
            att, v, q_int, k_st, decay = s
            return _mm(jnp.where(causal, att, 0.0), v), q_int, k_st, decay

        def store_within_chunk(c, out):
            rows = _chunk_rows(c)
            o_scr[rows, :], qi_scr[rows, :], ks_scr[rows, :], dec_scr[pl.ds(c, 1), :] = out

        def carry_state(c, state_t):
            update = st_ref[c]
            st_ref[c] = state_t
            return state_t * dec_scr[pl.ds(c, 1), :] + update

        def finish(c):
            rows = _chunk_rows(c)
            o = o_scr[rows, :] + _mm_nt(qi_scr[rows, :], st_ref[c])
            r = lax.rsqrt(_lane_mean(o * o) + EPS)
            gb = z_ref[3, rows, :]
            return (((o * r) * gain) * (gb * _sigmoid(gb))).astype(_MXU_DTYPE)

        def store_finished(c, yb):
            yb_ref[_chunk_rows(c), :] = yb

        _independent_trips(n_chunks, cumulate, store_cumulated)
        _independent_trips(n_chunks, [scores, within_chunk], store_within_chunk)
        _token_contractions(z_ref.at[2], ks_scr, st_ref, seq)
        lax.fori_loop(0, n_chunks, carry_state, jnp.zeros((LANES, LANES), F32))
        _independent_trips(n_chunks, finish, store_finished, group=2 * CHUNKS_IN_FLIGHT)

    seq_buf = pltpu.VMEM((seq, LANES), F32)
    return pl.pallas_call(
        body, name="branch_b_fwd",
        grid=(batch, N_BLK),
        in_specs=[pl.BlockSpec((4, seq, LANES), lambda b, h: (0, b, h)),
                  pl.BlockSpec((2, LANES), lambda b, h: (0, h)),
                  pl.BlockSpec((1, LANES), lambda b, h: (0, 0))],
        out_specs=[pl.BlockSpec((seq, LANES), lambda b, h: (b, h)),
                   pl.BlockSpec((None, n_chunks, LANES, LANES), lambda b, h: (b * N_BLK + h, 0, 0, 0))],
        out_shape=[jax.ShapeDtypeStruct((tokens, D_MODEL), _MXU_DTYPE),
                   jax.ShapeDtypeStruct((batch * N_BLK, n_chunks, LANES, LANES), F32)],
        scratch_shapes=[seq_buf] * 7 + [pltpu.VMEM((n_chunks, LANES), F32)],
        compiler_params=_params(("parallel", "parallel")),
    )(z, lb_logits, hg_g)


def _branch_b_bwd(z, states, dyb, dz, lb_logits, hg_g, batch, seq):
    n_chunks = seq // CHUNK

    def body(z_ref, st_ref, dyb_ref, dz_in_ref, lb_ref, g_ref, dz_ref, dlog_ref, dg_ref,
             f_scr, logf_scr, qh_scr, b_scr, do_scr, qi_scr, dqh_scr, df_scr, dec_scr, dgp_scr, dlb_scr, dst_scr):
        del dz_in_ref
        first = (pl.program_id(0) == 0) & (pl.program_id(1) == 0)
        lb = _hgrn_prepare(z_ref, lb_ref, f_scr, logf_scr, qh_scr, seq)
        causal = _chunk_masks()
        anti_causal = _chunk_masks(transposed=True)
        gain = g_ref[...]

        @pl.when(first)
        def _():
            dg_ref[...] = jnp.zeros_like(dg_ref)

        @pl.when(pl.program_id(1) == 0)
        def _():
            dlb_scr[...] = jnp.zeros_like(dlb_scr)

        def cumulate(c):
            return _cumsum_rows(logf_scr[_chunk_rows(c), :])

        def store_cumulated(c, b):
            b_scr[_chunk_rows(c), :] = b

        def scores(c):
            _, qh, k, v, e_q, e_k, e_qi, e_ks, decay = _chunk_terms(c, z_ref, f_scr, qh_scr, b_scr)
            q_int = qh * e_qi
            return _mm_nt(qh * e_q, k * e_k), _mm_nt(q_int, st_ref[c]), v, q_int, decay

        def output_gradient(c, s):
            att, o_inter, v, q_int, decay = s
            rows = _chunk_rows(c)
            o = _mm(jnp.where(causal, att, 0.0), v) + o_inter
            r = lax.rsqrt(_lane_mean(o * o) + EPS)
            o_n = o * r
            gb = z_ref[3, rows, :]
            sgb = _sigmoid(gb)
            dyb_c = dyb_ref[rows, :]
            d_ong = dyb_c * (gb * sgb)
            d_gb = (dyb_c * (o_n * gain) * (sgb * (1.0 + gb * (1.0 - sgb)))).astype(_MXU_DTYPE)
            d_gain = jnp.sum(d_ong * o_n, axis=0, keepdims=True)
            d_on = d_ong * gain
            return d_gb, d_gain, r * (d_on - o_n * _lane_mean(d_on * o_n)), q_int, decay

        def store_output_gradient(c, out):
            rows = _chunk_rows(c)
            dz_ref[3, rows, :], dgp_scr[pl.ds(c, 1), :], do_scr[rows, :], qi_scr[rows, :], dec_scr[pl.ds(c, 1), :] = out

        def carry_state_gradient(cc, d_state_t):
            c = n_chunks - 1 - cc
            update = dst_scr[c]
            dst_scr[c] = d_state_t
            return d_state_t * dec_scr[pl.ds(c, 1), :] + update

        def score_gradients(c):
            rows, qh, k, v, e_q, e_k, e_qi, e_ks, decay = _chunk_terms(c, z_ref, f_scr, qh_scr, b_scr)
            state_t = st_ref[c]
            d_state_t = dst_scr[c]
            d_o = do_scr[rows, :]
            q_in, k_in, q_int, k_st = qh * e_q, k * e_k, qh * e_qi, k * e_ks
            first = (_mm_nt(k_in, q_in), _mm_nt(d_o, v), _mm_nt(v, d_o), _mm_nt(k_st, d_state_t), _mm(d_o, state_t),
                     _mm(v, d_state_t))
            d_decay = jnp.sum(state_t * d_state_t, axis=0, keepdims=True)
            return first, d_o, q_in, k_in, q_int, k_st, e_q, e_k, e_qi, e_ks, decay, d_decay

        def input_gradients(c, s):
            (att_t, d_att, d_att_t, dv_inter, dq_int, dk_st), d_o, q_in, k_in, q_int, k_st, e_q, e_k, e_qi, e_ks, decay, d_decay = s
            rows = _chunk_rows(c)
            d_v = _mm(jnp.where(anti_causal, att_t, 0.0), d_o) + dv_inter
            dq_in = _mm(jnp.where(causal, d_att, 0.0), k_in)
            dk_in = _mm(jnp.where(anti_causal, d_att_t, 0.0), q_in)
            d_k = dk_in * e_k + dk_st * e_ks
            kk = dk_st * k_st
            d_b = dq_in * q_in + dq_int * q_int - dk_in * k_in - kk
            d_b_last = jnp.sum(kk, axis=0, keepdims=True) + decay * d_decay
            d_logf = _cumsum_rows(d_b, reverse=True) + d_b_last
            return d_v.astype(_MXU_DTYPE), dq_in * e_q + dq_int * e_qi, d_logf / f_scr[rows, :] - d_k

        def store_input_gradients(c, out):
            rows = _chunk_rows(c)
            dz_ref[2, rows, :], dqh_scr[rows, :], df_scr[rows, :] = out

        def input_activations(rows):
            q = z_ref[0, rows, :]
            sq = _sigmoid(q)
            dz_ref[0, rows, :] = (dqh_scr[rows, :] * (sq * (1.0 + q * (1.0 - sq)))).astype(_MXU_DTYPE)
            sg = _sigmoid(z_ref[1, rows, :])
            d_f = df_scr[rows, :]
            dz_ref[1, rows, :] = (d_f * (1.0 - lb) * sg * (1.0 - sg)).astype(_MXU_DTYPE)
            dlb_scr[...] += jnp.sum(d_f * (1.0 - sg), axis=0, keepdims=True)

        _independent_trips(n_chunks, cumulate, store_cumulated)
        _independent_trips(n_chunks, [scores, output_gradient], store_output_gradient)
        _token_contractions(do_scr, qi_scr, dst_scr, seq)
        lax.fori_loop(0, n_chunks, carry_state_gradient, jnp.zeros((LANES, LANES), F32))
        _independent_trips(n_chunks, [score_gradients, input_gradients], store_input_gradients)
        dg_ref[...] += jnp.sum(dgp_scr[...], axis=0, keepdims=True)
        _row_blocks(seq, input_activations)
        d_l0 = dlb_scr[...] * lb * (1.0 - lb)
        dlog_ref[0:1, :] = d_l0
        dlog_ref[1:2, :] = -d_l0

    tokens = batch * seq
    seq_buf = pltpu.VMEM((seq, LANES), F32)
    chunk_rows = pltpu.VMEM((n_chunks, LANES), F32)
    return pl.pallas_call(
        body, name="branch_b_bwd",
        grid=(N_BLK, batch),
        in_specs=[pl.BlockSpec((4, seq, LANES), lambda h, b: (0, b, h)),
                  pl.BlockSpec((None, n_chunks, LANES, LANES), lambda h, b: (b * N_BLK + h, 0, 0, 0)),
                  pl.BlockSpec((seq, LANES), lambda h, b: (b, h)),
                  pl.BlockSpec(memory_space=pl.ANY),
                  pl.BlockSpec((2, LANES), lambda h, b: (0, h)),
                  pl.BlockSpec((1, LANES), lambda h, b: (0, 0))],
        out_specs=[pl.BlockSpec((4, seq, LANES), lambda h, b: (0, b, h)),
                   pl.BlockSpec((2, LANES), lambda h, b: (0, h)),
                   pl.BlockSpec((1, LANES), lambda h, b: (0, 0))],
        out_shape=[jax.ShapeDtypeStruct((N_GROUPS, tokens, D_MODEL), _MXU_DTYPE),
                   jax.ShapeDtypeStruct((2, D_MODEL), F32),
                   jax.ShapeDtypeStruct((1, LANES), F32)],
        scratch_shapes=[seq_buf] * 8 + [chunk_rows, chunk_rows, pltpu.VMEM((1, LANES), F32),
                                        pltpu.VMEM((n_chunks, LANES, LANES), F32)],
        input_output_aliases={3: 0},
        compiler_params=_params(("arbitrary", "arbitrary")),
    )(z, states, dyb, dz, lb_logits, hg_g)


def _merge_tail(ya, yb, z, x2d, tgt2d, b_merge, final_g, pa, pb, wo):
    tokens, d = x2d.shape
    tm = min(256, tokens)
    n_tiles = tokens // tm

    def body(ya_ref, yb_ref, z_ref, x_ref, t_ref, bm_ref, fg_ref, pa_hbm, pb_hbm, wo_hbm,
             dya_ref, dyb_ref, dx2_ref, dz_ref, loss_ref, dfg_ref, dbm_ref, dpa_hbm, dpb_hbm, dwo_hbm,
             pa_s, pb_s, wo_s, dpa_s, dpb_s, dwo_s):
        i = pl.program_id(0)

        @pl.when(i == 0)
        def _():
            pltpu.sync_copy(pa_hbm, pa_s)
            pltpu.sync_copy(pb_hbm, pb_s)
            pltpu.sync_copy(wo_hbm, wo_s)
            dpa_s[...] = jnp.zeros_like(dpa_s)
            dpb_s[...] = jnp.zeros_like(dpb_s)
            dwo_s[...] = jnp.zeros_like(dwo_s)
            loss_ref[...] = jnp.zeros_like(loss_ref)
            dfg_ref[...] = jnp.zeros_like(dfg_ref)
            dbm_ref[...] = jnp.zeros_like(dbm_ref)

        ya_t = ya_ref[...]
        yb_t = yb_ref[...]
        out_a = _mm(ya_t, pa_s[...])
        out_b = _mm(yb_t, pb_s[...])
        g_a = _sigmoid(z_ref[0] + bm_ref[:, :d])
        g_b = _sigmoid(z_ref[1] + bm_ref[:, d:])
        mixed = g_a * out_a + g_b * out_b
        x2 = x_ref[...] + _mm(mixed, wo_s[...])
        r = lax.rsqrt(jnp.mean(x2 * x2, axis=-1, keepdims=True) + EPS)
        xn = x2 * r
        fg = fg_ref[...]
        diff = xn * fg - t_ref[...]
        loss_ref[...] += jnp.sum(diff * diff) * (0.5 / d)
        dy = diff * (1.0 / d)
        dfg_ref[...] += jnp.sum(dy * xn, axis=0, keepdims=True)
        dxn = dy * fg
        dx2 = r * (dxn - xn * jnp.mean(dxn * xn, axis=-1, keepdims=True))
        dx2_ref[...] = dx2
        dmixed = _mm_nt(dx2, wo_s[...])
        dwo_s[...] += _mm_tn(mixed, dx2)
        dgm_a = dmixed * out_a * g_a * (1.0 - g_a)
        dgm_b = dmixed * out_b * g_b * (1.0 - g_b)
        dz_ref[0] = dgm_a.astype(_MXU_DTYPE)
        dz_ref[1] = dgm_b.astype(_MXU_DTYPE)
        dbm_ref[:, :d] += jnp.sum(dgm_a, axis=0, keepdims=True)
        dbm_ref[:, d:] += jnp.sum(dgm_b, axis=0, keepdims=True)
        dout_a = dmixed * g_a
        dout_b = dmixed * g_b
        dpa_s[...] += _mm_tn(ya_t, dout_a)
        dpb_s[...] += _mm_tn(yb_t, dout_b)
        dya_ref[...] = _mm_nt(dout_a, pa_s[...])
        dyb_ref[...] = _mm_nt(dout_b, pb_s[...])

        @pl.when(i == n_tiles - 1)
        def _():
            pltpu.sync_copy(dpa_s, dpa_hbm)
            pltpu.sync_copy(dpb_s, dpb_hbm)
            pltpu.sync_copy(dwo_s, dwo_hbm)

    tile = pl.BlockSpec((tm, d), lambda i: (i, 0))
    gm = pl.BlockSpec((2, tm, d), lambda i: (3, i, 0))
    row = lambda n: pl.BlockSpec((1, n), lambda i: (0, 0))
    hbm = pl.BlockSpec(memory_space=pl.ANY)
    act = jax.ShapeDtypeStruct((tokens, d), F32)
    mat = jax.ShapeDtypeStruct((d, d), F32)
    return pl.pallas_call(
        body, name="merge_tail",
        grid=(n_tiles,),
        in_specs=[tile, tile, gm, tile, tile, row(2 * d), row(d), hbm, hbm, hbm],
        out_specs=[tile, tile, tile, gm, row(LANES), row(d), row(2 * d), hbm, hbm, hbm],
        out_shape=[act, act, act, jax.ShapeDtypeStruct((N_GROUPS, tokens, d), _MXU_DTYPE),
                   jax.ShapeDtypeStruct((1, LANES), F32), jax.ShapeDtypeStruct((1, d), F32),
                   jax.ShapeDtypeStruct((1, 2 * d), F32), mat, mat, mat],
        scratch_shapes=[pltpu.VMEM((d, d), _MXU_DTYPE)] * 3 + [pltpu.VMEM((d, d), F32)] * 3,
        compiler_params=_params(("arbitrary",)),
    )(ya, yb, z, x2d, tgt2d, b_merge, final_g, pa, pb, wo)


def _inproj_dw(h_t, dz):
    d, tokens = h_t.shape
    tm = min(2048, tokens)

    def body(h_ref, dz_ref, dw_ref):
        part = _mm(h_ref[...], dz_ref[...])

        @pl.when(pl.program_id(1) == 0)
        def _():
            dw_ref[...] = part

        @pl.when(pl.program_id(1) != 0)
        def _():
            dw_ref[...] += part

    def out_index(s, i):
        g = _group_of_slot(s)
        return (g // 2, 0, g % 2)

    return pl.pallas_call(
        body, name="inproj_dw",
        grid=(N_GROUPS, tokens // tm),
        in_specs=[pl.BlockSpec((d, tm), lambda s, i: (0, i)),
                  pl.BlockSpec((None, tm, D_MODEL), lambda s, i: (s, i, 0))],
        out_specs=pl.BlockSpec((None, d, D_MODEL), out_index),
        out_shape=jax.ShapeDtypeStruct((N_SHARDS, d, 2 * D_MODEL), F32),
        compiler_params=_params(("parallel", "arbitrary")),
    )(h_t, dz)


def _inproj_dx(dz, w_all, x2d, dx2, norm_g, scatter=None):
    tokens, d = x2d.shape
    tm = min(512, tokens)
    n_tiles = tokens // tm
    n_big = len(scatter[0]) if scatter else 0

    def body(dz_ref, w_hbm, x_ref, dx2_ref, g_ref, *rest):
        if scatter:
            srcs, small_src = rest[:n_big], rest[2 * n_big]
            dx_ref, dg_ref = rest[2 * n_big + 1:2 * n_big + 3]
            outs, small_out = rest[2 * n_big + 3:3 * n_big + 3], rest[3 * n_big + 3]
            acc, w_res, send_sems, recv_sems, local_sem = rest[3 * n_big + 4:]
            copies = _scatter_copies(srcs, small_src, outs, small_out, send_sems, recv_sems, local_sem)
        else:
            dx_ref, dg_ref, acc, w_res = rest
            copies = []
        s = pl.program_id(1)

        @pl.when((pl.program_id(0) == 0) & (s == 0))
        def _():
            for cp in copies:
                cp.start()
            _load_w_in_by_slot(w_hbm, w_res)

        part = _mm_nt(dz_ref[...], w_res[s])

        @pl.when(s == 0)
        def _():
            acc[...] = part

        @pl.when(s != 0)
        def _():
            acc[...] += part

        @pl.when((pl.program_id(0) == 0) & (s == 0))
        def _():
            dg_ref[...] = jnp.zeros_like(dg_ref)

        @pl.when(s == N_GROUPS - 1)
        def _():
            x = x_ref[...]
            r = lax.rsqrt(jnp.mean(x * x, axis=-1, keepdims=True) + EPS)
            xn = x * r
            dh = acc[...]
            dg_ref[...] += jnp.sum(dh * xn, axis=0, keepdims=True)
            dxn = dh * g_ref[...]
            dx_ref[...] = r * (dxn - xn * jnp.mean(dxn * xn, axis=-1, keepdims=True)) + dx2_ref[...]

        @pl.when((pl.program_id(0) == n_tiles - 1) & (s == N_GROUPS - 1))
        def _():
            for cp in copies:
                cp.wait()

    tile = pl.BlockSpec((tm, d), lambda i, s: (i, 0))
    hbm = pl.BlockSpec(memory_space=pl.ANY)
    in_specs = [pl.BlockSpec((None, tm, D_MODEL), lambda i, s: (s, i, 0)), hbm, tile, tile,
                pl.BlockSpec((1, d), lambda i, s: (0, 0))]
    out_specs = [tile, pl.BlockSpec((1, d), lambda i, s: (0, 0))]
    out_shape = [jax.ShapeDtypeStruct((tokens, d), F32), jax.ShapeDtypeStruct((1, d), F32)]
    scratch = [pltpu.VMEM((tm, d), F32), pltpu.VMEM((N_GROUPS, d, D_MODEL), _MXU_DTYPE)]
    operands, aliases = [dz, w_all, x2d, dx2, norm_g], {}
    if scatter:
        bigs, by_chip, small_piece = scatter
        n_sem = 3 * n_big + N_DEV - 1
        in_specs += [hbm] * (2 * n_big + 1)
        out_specs += [hbm] * (n_big + 1)
        out_shape += [jax.ShapeDtypeStruct(g.shape, g.dtype) for g in by_chip]
        out_shape.append(jax.ShapeDtypeStruct((N_DEV, PIECE_ROWS, LANES), F32))
        scratch += [pltpu.SemaphoreType.DMA((n_sem,)), pltpu.SemaphoreType.DMA((n_sem,)), pltpu.SemaphoreType.DMA]
        operands += [*bigs, *by_chip, small_piece]
        aliases = {5 + n_big + a: 2 + a for a in range(n_big)}
    return pl.pallas_call(
        body, name="inproj_dx", grid=(n_tiles, N_GROUPS), in_specs=in_specs, out_specs=out_specs, out_shape=out_shape,
        scratch_shapes=scratch, input_output_aliases=aliases,
        compiler_params=_params(("arbitrary", "arbitrary")),
    )(*operands)


def _row_tile(rows, cols, itemsize=4, budget=2 * 1024 * 1024):
    tr = rows
    while tr * cols * itemsize > budget and tr % 16 == 0:
        tr //= 2
    return tr


def _cast_into_slot(a, chip, dtype, name):
    rows, cols = a.shape
    tr = _row_tile(rows, cols)

    def body(chip_ref, a_ref, o_ref):
        del chip_ref
        o_ref[...] = a_ref[...].astype(dtype)

    grid_spec = pltpu.PrefetchScalarGridSpec(
        num_scalar_prefetch=1, grid=(rows // tr,),
        in_specs=[pl.BlockSpec((tr, cols), lambda i, chip_ref: (i, 0))],
        out_specs=pl.BlockSpec((None, tr, cols), lambda i, chip_ref: (chip_ref[0], i, 0)))
    return pl.pallas_call(body, name=name, grid_spec=grid_spec,
                          out_shape=jax.ShapeDtypeStruct((N_SHARDS, rows, cols), dtype),
                          compiler_params=_params(("arbitrary",)))(chip, a)


def _sum_slots(stack, name):
    n, rows, cols = stack.shape
    tr = _row_tile(rows, cols * n)

    def body(s_ref, o_ref):
        total = s_ref[0].astype(F32)
        for k in range(1, n):
            total = total + s_ref[k].astype(F32)
        o_ref[...] = total

    return pl.pallas_call(body, name=name, grid=(rows // tr,),
                          in_specs=[pl.BlockSpec((n, tr, cols), lambda i: (0, i, 0))],
                          out_specs=pl.BlockSpec((tr, cols), lambda i: (i, 0)),
                          out_shape=jax.ShapeDtypeStruct((rows, cols), F32),
                          compiler_params=_params(("parallel",)))(stack)


def _add_half(full, landed, place, name):
    n, rows, cols = full.shape
    half = rows // 2
    tr = _row_tile(half, cols)
    nb = half // tr

    def body(place_ref, a_ref, b_ref, o_ref, own_ref):
        total = (a_ref[...] + b_ref[...]).astype(_MXU_DTYPE)
        o_ref[...] = total

        @pl.when(pl.program_id(1) == place_ref[1])
        def _():
            own_ref[...] = total

    grid_spec = pltpu.PrefetchScalarGridSpec(
        num_scalar_prefetch=1, grid=(nb, n),
        in_specs=[pl.BlockSpec((None, tr, cols), lambda i, j, place_ref: (j, place_ref[0] * nb + i, 0)),
                  pl.BlockSpec((None, tr, cols), lambda i, j, place_ref: (j, i, 0))],
        out_specs=[pl.BlockSpec((None, tr, cols), lambda i, j, place_ref: (j, i, 0)),
                   pl.BlockSpec((None, tr, cols), lambda i, j, place_ref: (place_ref[1], i, 0))])
    shape = jax.ShapeDtypeStruct((n, half, cols), _MXU_DTYPE)
    return pl.pallas_call(body, name=name, grid_spec=grid_spec, out_shape=[shape, shape],
                          compiler_params=_params(("parallel", "arbitrary")))(place, full, landed)


def _adamw_update(w, grad, m, v):
    c1 = 1.0 - ADAM_B1 ** ADAM_STEP
    c2 = 1.0 - ADAM_B2 ** ADAM_STEP
    nm = ADAM_B1 * m + (1.0 - ADAM_B1) * grad
    nv = ADAM_B2 * v + (1.0 - ADAM_B2) * (grad * grad)
    return (-ADAM_LR) * ((nm / c1) / (jnp.sqrt(nv / c2) + ADAM_EPS) + ADAM_WD * w), nm, nv


def _adamw(w, g, m, v, name):
    rows, cols = w.shape
    tr = _row_tile(rows, cols, budget=1024 * 1024)

    def body(w_ref, g_ref, m_ref, v_ref, d_ref, nm_ref, nv_ref):
        d_ref[...], nm_ref[...], nv_ref[...] = _adamw_update(w_ref[...], g_ref[...], m_ref[...], v_ref[...])

    spec = pl.BlockSpec((tr, cols), lambda i: (i, 0))
    shape = jax.ShapeDtypeStruct((rows, cols), F32)
    return pl.pallas_call(body, name=name, grid=(rows // tr,), in_specs=[spec] * 4, out_specs=[spec] * 3,
                          out_shape=[shape] * 3, compiler_params=_params(("parallel",)))(w, g, m, v)


def _adamw_halves(w, g_mine, g_sibling, m, v, core, name):
    rows, cols = w.shape
    half = rows // 2
    tr = _row_tile(half, cols, budget=1024 * 1024)
    nb = half // tr

    def body(core_ref, w_ref, gm_ref, gs_ref, m_ref, v_ref, g_ref, d_ref, nm_ref, nv_ref):
        mine = pl.program_id(0) // nb == core_ref[0]
        grad = jnp.where(mine, gm_ref[...], gs_ref[...])
        g_ref[...] = grad
        d_ref[...], nm_ref[...], nv_ref[...] = _adamw_update(w_ref[...], grad, m_ref[...], v_ref[...])

    spec = pl.BlockSpec((tr, cols), lambda i, core_ref: (i, 0))
    half_spec = pl.BlockSpec((tr, cols), lambda i, core_ref: (i % nb, 0))
    grid_spec = pltpu.PrefetchScalarGridSpec(num_scalar_prefetch=1, grid=(rows // tr,),
                                             in_specs=[spec, half_spec, half_spec, spec, spec], out_specs=[spec] * 4)
    shape = jax.ShapeDtypeStruct((rows, cols), F32)
    return pl.pallas_call(body, name=name, grid_spec=grid_spec, out_shape=[shape] * 4,
                          compiler_params=_params(("parallel",)))(core, w, g_mine, g_sibling, m, v)


def _local_step(x, loss_target, w_all, pa, pb, wo, conv_w, b_merge, conv_b, rg_wx, rg_bx, rg_wa, rg_ba,
                rg_lambda, hg_lb_logits, hg_norm_g, norm_g, final_norm_g, gather=None, start_reduction=None):
    batch, seq, d = x.shape
    x2d = x.reshape(batch * seq, d)
    tgt2d = loss_target.reshape(batch * seq, d)
    if gather is None:
        z, h_t = _inproj_fwd(x2d, norm_g, w_all)
    else:
        z, h_t, (w_all, pa, pb, wo), cw_all = _inproj_fwd_gather(x2d, norm_g, *gather)
        pa, pb, wo = (t.reshape(d, d) for t in (pa, pb, wo))
        conv_w = jnp.transpose(cw_all, (1, 0, 2)).reshape(CONV_WIDTH, d)
    lru = (conv_w, conv_b, rg_wx, rg_bx, rg_wa, rg_ba, rg_lambda)
    ya, hl = _branch_a_fwd(z, *lru, batch, seq)
    yb, states = _branch_b_fwd(z, hg_lb_logits, hg_norm_g, batch, seq)
    dya, dyb, dx2, dz, loss, d_final_g, d_b_merge, d_pa, d_pb, d_wo = _merge_tail(
        ya, yb, z, x2d, tgt2d, b_merge, final_norm_g, pa, pb, wo)
    dz, d_lb_logits, d_hg_g = _branch_b_bwd(z, states, dyb, dz, hg_lb_logits, hg_norm_g, batch, seq)
    dz, d_conv_w, d_conv_b, d_wx, d_bx, d_wa, d_ba, d_lam = _branch_a_bwd(z, hl, dya, dz, *lru, batch, seq)
    d_w_in = _inproj_dw(h_t, dz)
    big = (d_w_in, d_pa, d_pb, d_wo)
    small = dict(b_merge=d_b_merge, conv_w=d_conv_w, conv_b=d_conv_b, rg_wx=d_wx, rg_bx=d_bx, rg_wa=d_wa,
                 rg_ba=d_ba, rg_lambda=d_lam, hg_lb_logits=d_lb_logits, hg_norm_g=d_hg_g,
                 norm_g=jnp.zeros((1, d), F32), final_norm_g=d_final_g)
    if start_reduction is None:
        grad_x, small["norm_g"] = _inproj_dx(dz, w_all, x2d, dx2, norm_g)
        return loss[0, 0], grad_x.reshape(batch, seq, d), big, small
    grad_x, d_norm_g, *scattered = _inproj_dx(dz, w_all, x2d, dx2, norm_g, scatter=start_reduction(big, small))
    return loss[0, 0], grad_x.reshape(batch, seq, d), d_norm_g, scattered


_SMALL_ORDER = ("b_merge", "conv_w", "conv_b", "rg_wx", "rg_bx", "rg_wa", "rg_ba", "rg_lambda", "hg_lb_logits",
                "hg_norm_g", "norm_g", "final_norm_g")
N_DEV = 8
PIECE_ROWS = 272


def _pack_small(tree):
    flat = jnp.concatenate([tree[k].reshape(-1) for k in _SMALL_ORDER])
    flat = jnp.pad(flat, (0, N_DEV * PIECE_ROWS * LANES - flat.shape[0]))
    return flat.reshape(N_DEV * PIECE_ROWS, LANES)


def _unpack_small(packed, like):
    flat = packed.reshape(-1)
    out, pos = {}, 0
    for k in _SMALL_ORDER:
        n = like[k].size
        out[k] = flat[pos:pos + n].reshape(like[k].shape)
        pos += n
    return out


def _mesh_position():
    x, y, c = lax.axis_index("x"), lax.axis_index("y"), lax.axis_index("c")
    other_chips = [(1 - x, y), (x, 1 - y), (1 - x, 1 - y)]
    return x, y, c, other_chips


def _other_devices(x, y, c):
    flips = [(fx, fy, fc) for fx in (0, 1) for fy in (0, 1) for fc in (0, 1) if (fx, fy, fc) != (0, 0, 0)]
    return [(jnp.where(fx, 1 - x, x), jnp.where(fy, 1 - y, y), jnp.where(fc, 1 - c, c)) for fx, fy, fc in flips]


def _remote(src, dst, send_sems, recv_sems, k, device):
    return pltpu.make_async_remote_copy(src_ref=src, dst_ref=dst, send_sem=send_sems.at[k], recv_sem=recv_sems.at[k],
                                        device_id=device, device_id_type=MESH)


def _exchange_halves(bigs, small):
    n_big = len(bigs)
    n_sem = n_big + N_DEV - 1

    def body(*refs):
        srcs, small_src = refs[:n_big], refs[n_big]
        outs, small_out = refs[n_big + 1:2 * n_big + 1], refs[2 * n_big + 1]
        send_sems, recv_sems, local_sem = refs[2 * n_big + 2:]
        x, y, c, _ = _mesh_position()
        me, sibling = 4 * x + 2 * y + c, (x, y, 1 - c)
        mine = pltpu.make_async_copy(small_src.at[pl.ds(me * PIECE_ROWS, PIECE_ROWS), :], small_out.at[me], local_sem)
        mine.start()
        copies = []
        for a in range(n_big):
            hs = srcs[a].shape[1] // 2
            copies.append(_remote(srcs[a].at[:, pl.ds((1 - c) * hs, hs), :], outs[a], send_sems, recv_sems, a, sibling))
        for k, (px, py, pc) in enumerate(_other_devices(x, y, c)):
            piece = small_src.at[pl.ds((4 * px + 2 * py + pc) * PIECE_ROWS, PIECE_ROWS), :]
            copies.append(_remote(piece, small_out.at[me], send_sems, recv_sems, n_big + k, (px, py, pc)))
        for cp in copies:
            cp.start()
        for cp in copies:
            cp.wait()
        mine.wait()

    hbm = pl.BlockSpec(memory_space=pl.ANY)
    out_shape = [jax.ShapeDtypeStruct((g.shape[0], g.shape[1] // 2, g.shape[2]), F32) for g in bigs]
    out_shape.append(jax.ShapeDtypeStruct((N_DEV, PIECE_ROWS, LANES), F32))
    return pl.pallas_call(
        body, name="exchange_halves",
        in_specs=[hbm] * (n_big + 1), out_specs=[hbm] * (n_big + 1), out_shape=out_shape,
        scratch_shapes=[pltpu.SemaphoreType.DMA((n_sem,)), pltpu.SemaphoreType.DMA((n_sem,)), pltpu.SemaphoreType.DMA],
    )(*bigs, small)


def _scatter_copies(srcs, small_src, outs, small_out, send_sems, recv_sems, local_sem):
    n_big = len(srcs)
    x, y, c, chips = _mesh_position()
    chip, me = 2 * x + y, 4 * x + 2 * y + c
    copies = [pltpu.make_async_copy(small_src, small_out.at[me], local_sem)]
    for a in range(n_big):
        for j, (cx, cy) in enumerate(chips):
            copies.append(_remote(srcs[a].at[2 * cx + cy], outs[a].at[chip], send_sems, recv_sems, 3 * a + j, (cx, cy, c)))
    for k, peer in enumerate(_other_devices(x, y, c)):
        copies.append(_remote(small_src, small_out.at[me], send_sems, recv_sems, 3 * n_big + k, peer))
    return copies


def _swap_halves(halves, vec):
    n_big = len(halves)

    def body(*refs):
        srcs, vec_src = refs[:n_big], refs[n_big]
        outs, vec_out = refs[n_big + 1:2 * n_big + 1], refs[2 * n_big + 1]
        send_sems, recv_sems, local_sem = refs[2 * n_big + 2:]
        x, y, c, _ = _mesh_position()
        me = 4 * x + 2 * y + c
        copies = [pltpu.make_async_copy(vec_src, vec_out.at[me], local_sem)]
        copies += [_remote(srcs[a], outs[a], send_sems, recv_sems, a, (x, y, 1 - c)) for a in range(n_big)]
        copies += [_remote(vec_src, vec_out.at[me], send_sems, recv_sems, n_big + k, peer)
                   for k, peer in enumerate(_other_devices(x, y, c))]
        for cp in copies:
            cp.start()
        for cp in copies:
            cp.wait()

    hbm = pl.BlockSpec(memory_space=pl.ANY)
    n_sem = n_big + N_DEV - 1
    return pl.pallas_call(
        body, name="swap_halves",
        in_specs=[hbm] * (n_big + 1), out_specs=[hbm] * (n_big + 1),
        out_shape=[jax.ShapeDtypeStruct(h.shape, F32) for h in halves] + [jax.ShapeDtypeStruct((N_DEV,) + vec.shape, F32)],
        scratch_shapes=[pltpu.SemaphoreType.DMA((n_sem,)), pltpu.SemaphoreType.DMA((n_sem,)), pltpu.SemaphoreType.DMA],
    )(*halves, vec)


def kernel(x, w_in, b_merge, conv_w, conv_b, rg_wx, rg_bx, rg_wa, rg_ba, rg_lambda, hg_lb_logits, hg_norm_g, proj_a, proj_b, w_out, norm_g, final_norm_g, loss_target, m_w_in, m_b_merge, m_conv_w, m_conv_b, m_rg_wx, m_rg_bx, m_rg_wa, m_rg_ba, m_rg_lambda, m_hg_lb_logits, m_hg_norm_g, m_proj_a, m_proj_b, m_w_out, m_norm_g, m_final_norm_g, v_w_in, v_b_merge, v_conv_w, v_conv_b, v_rg_wx, v_rg_bx, v_rg_wa, v_rg_ba, v_rg_lambda, v_hg_lb_logits, v_hg_norm_g, v_proj_a, v_proj_b, v_w_out, v_norm_g, v_final_norm_g):
    d = D_MODEL
    weights = dict(w_in=w_in, b_merge=b_merge, conv_w=conv_w, conv_b=conv_b, rg_wx=rg_wx, rg_bx=rg_bx, rg_wa=rg_wa,
                   rg_ba=rg_ba, rg_lambda=rg_lambda, hg_lb_logits=hg_lb_logits, hg_norm_g=hg_norm_g, proj_a=proj_a,
                   proj_b=proj_b, w_out=w_out, norm_g=norm_g, final_norm_g=final_norm_g)
    m = dict(w_in=m_w_in, b_merge=m_b_merge, conv_w=m_conv_w, conv_b=m_conv_b, rg_wx=m_rg_wx, rg_bx=m_rg_bx,
             rg_wa=m_rg_wa, rg_ba=m_rg_ba, rg_lambda=m_rg_lambda, hg_lb_logits=m_hg_lb_logits, hg_norm_g=m_hg_norm_g,
             proj_a=m_proj_a, proj_b=m_proj_b, w_out=m_w_out, norm_g=m_norm_g, final_norm_g=m_final_norm_g)
    v = dict(w_in=v_w_in, b_merge=v_b_merge, conv_w=v_conv_w, conv_b=v_conv_b, rg_wx=v_rg_wx, rg_bx=v_rg_bx,
             rg_wa=v_rg_wa, rg_ba=v_rg_ba, rg_lambda=v_rg_lambda, hg_lb_logits=v_hg_lb_logits, hg_norm_g=v_hg_norm_g,
             proj_a=v_proj_a, proj_b=v_proj_b, w_out=v_w_out, norm_g=v_norm_g, final_norm_g=v_final_norm_g)
    big_names = ("w_in", "proj_a", "proj_b", "w_out")

    core = lax.axis_index("c").astype(jnp.int32).reshape(1)
    chip = (2 * lax.axis_index("x") + lax.axis_index("y")).astype(jnp.int32)

    slotted = [_cast_into_slot(weights[k][0], chip.reshape(1), _MXU_DTYPE, f"cast_{k}") for k in big_names]
    conv_slotted = _cast_into_slot(conv_w[0], chip.reshape(1), F32, "slot_conv_w")

    small_shapes = {}

    def start_reduction(big_grads, small_grads):
        small_shapes.update({k: t.shape for k, t in small_grads.items()})
        bigs = [big_grads[0]] + [g.reshape(N_SHARDS, d // N_SHARDS, d) for g in big_grads[1:]]
        *landed, small_landed = _exchange_halves(bigs, _pack_small(small_grads))
        place = jnp.concatenate([core, chip.reshape(1)])
        sums = [_add_half(g, l, place, f"add_half_{a}") for a, (g, l) in enumerate(zip(bigs, landed))]
        return [s[0] for s in sums], [s[1] for s in sums], _sum_slots(small_landed, "sum_small")

    loss_part, grad_x, d_norm_g, (*by_chip, small_all) = _local_step(
        x, loss_target, None, None, None, None, None,
        b_merge, conv_b, rg_wx[0], rg_bx.reshape(1, d), rg_wa[0], rg_ba.reshape(1, d), rg_lambda, hg_lb_logits,
        hg_norm_g, norm_g, final_norm_g.reshape(1, d), gather=(slotted, conv_slotted, chip.reshape(1)),
        start_reduction=start_reduction)
    mine = [_sum_slots(s, f"sum_chips_{a}") for a, s in enumerate(by_chip)]
    late = jnp.concatenate([d_norm_g.reshape(SUBLANES, LANES), jnp.full((SUBLANES, LANES), loss_part, F32)])
    *theirs, late_parts = _swap_halves(mine, late)
    late_sum = _sum_slots(late_parts, "sum_late")
    loss = late_sum[SUBLANES, 0]
    small_red = _unpack_small(small_all, {k: jax.ShapeDtypeStruct(s, F32) for k, s in small_shapes.items()})
    small_red["norm_g"] = late_sum[:SUBLANES].reshape(1, d)

    grads, delta, new_m, new_v = {}, {}, {}, {}
    for k, g_mine, g_theirs in zip(big_names, mine, theirs):
        out = _adamw_halves(weights[k][0], g_mine, g_theirs, m[k][0], v[k][0], core, f"adamw_{k}")
        grads[k], delta[k], new_m[k], new_v[k] = (t.reshape(weights[k].shape) for t in out)
    cols = d // N_SHARDS
    g_conv = lax.dynamic_slice(small_red["conv_w"], (0, chip * cols), (CONV_WIDTH, cols))
    grads["conv_w"] = g_conv.reshape(conv_w.shape)
    dl, nm, nv = _adamw(conv_w[0], g_conv, m_conv_w[0], v_conv_w[0], "adamw_conv_w")
    delta["conv_w"], new_m["conv_w"], new_v["conv_w"] = (t.reshape(conv_w.shape) for t in (dl, nm, nv))
    rest = [k for k in _SMALL_ORDER if k != "conv_w"]
    like = {k: (weights[k] if k != "conv_w" else jnp.zeros((CONV_WIDTH, d), F32)) for k in _SMALL_ORDER}
    packs = [_pack_small({k: (t[k] if k != "conv_w" else like[k]) for k in _SMALL_ORDER}) for t in (weights, m, v)]
    g_pack = _pack_small({k: small_red[k].reshape(like[k].shape) for k in _SMALL_ORDER})
    outs = [_unpack_small(p, like) for p in _adamw(packs[0], g_pack, packs[1], packs[2], "adamw_small")]
    for k in rest:
        grads[k] = small_red[k].reshape(weights[k].shape)
        delta[k], new_m[k], new_v[k] = outs[0][k], outs[1][k], outs[2][k]

    order = ("w_in", "b_merge", "conv_w", "conv_b", "rg_wx", "rg_bx", "rg_wa", "rg_ba", "rg_lambda", "hg_lb_logits",
             "hg_norm_g", "proj_a", "proj_b", "w_out", "norm_g", "final_norm_g")
    return (loss, grad_x, *[grads[k] for k in order], *[delta[k] for k in order], *[new_m[k] for k in order],
            *[new_v[k] for k in order])
```

```python
import functools

import jax
import jax.numpy as jnp
from jax import lax
from jax.experimental import pallas as pl
from jax.experimental.pallas import tpu as pltpu

F32 = jnp.float32
_MXU_DTYPE = jnp.bfloat16

D_MODEL = 1024
LANES = 128
SUBLANES = 8
N_BLK = D_MODEL // LANES
N_GROUPS = 8
N_SHARDS = 4
CONV_WIDTH = 4
LRU_C = 8.0
CHUNK = 64
CHUNKS_IN_FLIGHT = 4
HG_SCALE = float(LANES) ** -0.5
EPS = 1e-6
ADAM_LR, ADAM_B1, ADAM_B2, ADAM_EPS, ADAM_WD, ADAM_STEP = 0.001, 0.9, 0.999, 1e-08, 0.01, 10
VMEM_LIMIT = 56 * 1024 * 1024
MESH = pl.DeviceIdType.MESH

_SLOT_TO_GROUP = (2, 3, 4, 5, 0, 1, 6, 7)


def _group_of_slot(s):
    return jnp.where(s < 4, s + 2, jnp.where(s < 6, s - 4, s))


def _mm(a, b):
    return lax.dot_general(a.astype(_MXU_DTYPE), b.astype(_MXU_DTYPE), (((1,), (0,)), ((), ())),
                           preferred_element_type=F32)


def _mm_nt(a, b):
    return lax.dot_general(a.astype(_MXU_DTYPE), b.astype(_MXU_DTYPE), (((1,), (1,)), ((), ())),
                           preferred_element_type=F32)


def _mm_tn(a, b):
    return lax.dot_general(a.astype(_MXU_DTYPE), b.astype(_MXU_DTYPE), (((0,), (0,)), ((), ())),
                           preferred_element_type=F32)


def _sigmoid(x):
    return 0.5 * jnp.tanh(0.5 * x) + 0.5


def _log1p_pos(y):
    series = y * (1.0 - y * (0.5 - y * (1.0 / 3.0 - y * 0.25)))
    return jnp.where(y < 0.01, series, jnp.log(1.0 + y))


def _softplus(x):
    return jnp.maximum(x, 0.0) + _log1p_pos(jnp.exp(-jnp.abs(x)))


def _shift_down(x, n):
    rolled = pltpu.roll(x, n, 0)
    edge = SUBLANES if (n < SUBLANES and x.shape[0] > SUBLANES) else x.shape[0]
    rows = lax.broadcasted_iota(jnp.int32, (edge, x.shape[1]), 0)
    head = jnp.where(rows >= n, rolled[:edge], 0.0)
    return head if edge == x.shape[0] else jnp.concatenate([head, rolled[edge:]], axis=0)


def _shift_up(x, n):
    size = x.shape[0]
    rolled = pltpu.roll(x, size - n, 0)
    edge = SUBLANES if (n < SUBLANES and size > SUBLANES) else size
    rows = lax.broadcasted_iota(jnp.int32, (edge, x.shape[1]), 0)
    tail = jnp.where(rows < edge - n, rolled[size - edge:], 0.0)
    return tail if edge == size else jnp.concatenate([rolled[:size - edge], tail], axis=0)


def _params(dims, vmem=VMEM_LIMIT):
    return pltpu.CompilerParams(dimension_semantics=dims, vmem_limit_bytes=vmem)


def _load_w_in_by_slot(w_hbm, w_res):
    for slot, g in enumerate(_SLOT_TO_GROUP):
        pltpu.sync_copy(w_hbm.at[g // 2, :, pl.ds((g % 2) * D_MODEL, D_MODEL)], w_res.at[slot])


def _inproj_fwd(x2d, norm_g, w_all):
    tokens, d = x2d.shape
    tm = min(512, tokens)

    def body(x_ref, g_ref, w_hbm, z_ref, ht_ref, h_scr, w_res):
        @pl.when((pl.program_id(0) == 0) & (pl.program_id(1) == 0))
        def _():
            _load_w_in_by_slot(w_hbm, w_res)

        @pl.when(pl.program_id(1) == 0)
        def _():
            x = x_ref[...]
            r = lax.rsqrt(jnp.mean(x * x, axis=-1, keepdims=True) + EPS)
            h = (x * r) * g_ref[...]
            h_scr[...] = h.astype(_MXU_DTYPE)
            ht_ref[...] = jnp.transpose(h).astype(_MXU_DTYPE)

        z_ref[...] = _mm(h_scr[...], w_res[pl.program_id(1)])

    return pl.pallas_call(
        body, name="inproj_fwd",
        grid=(tokens // tm, N_GROUPS),
        in_specs=[pl.BlockSpec((tm, d), lambda i, s: (i, 0)),
                  pl.BlockSpec((1, d), lambda i, s: (0, 0)),
                  pl.BlockSpec(memory_space=pl.ANY)],
        out_specs=[pl.BlockSpec((None, tm, D_MODEL), lambda i, s: (s, i, 0)),
                   pl.BlockSpec((d, tm), lambda i, s: (0, i))],
        out_shape=[jax.ShapeDtypeStruct((N_GROUPS, tokens, D_MODEL), F32),
                   jax.ShapeDtypeStruct((d, tokens), _MXU_DTYPE)],
        scratch_shapes=[pltpu.VMEM((tm, d), _MXU_DTYPE), pltpu.VMEM((N_GROUPS, d, D_MODEL), _MXU_DTYPE)],
        compiler_params=_params(("arbitrary", "arbitrary")),
    )(x2d, norm_g, w_all)


def _slot_of_group(g):
    return jnp.where(g < 2, g + 4, jnp.where(g < 6, g - 2, g))


def _inproj_fwd_gather(x2d, norm_g, slotted, conv_slotted, chip):
    tokens, d = x2d.shape
    tm = min(512, tokens)
    n_tiles = tokens // tm
    n_big = len(slotted)
    n_sem = 6 * n_big + 3
    last_pass = N_GROUPS - 1

    def shard_of(k, chip_id):
        x, y = chip_id // 2, chip_id % 2
        return 2 * jnp.where(k % 2 == 1, 1 - x, x) + jnp.where(k // 2 == 1, 1 - y, y)

    def body(chip_ref, x_ref, g_ref, *rest):
        bufs, cw = rest[n_big + 1:2 * n_big + 1], rest[2 * n_big + 1]
        z_ref, ht_ref = rest[2 * n_big + 2:2 * n_big + 4]
        h_all, slab, send_sems, recv_sems, slab_sem = rest[2 * n_big + 4:]
        del chip_ref
        p, i = pl.program_id(0), pl.program_id(1)
        x, y, c, chips = _mesh_position()
        me, sibling = 2 * x + y, (x, y, 1 - c)

        def half(a, slot, which):
            hs = bufs[a].shape[1] // 2
            return bufs[a].at[slot, pl.ds(which * hs, hs), :]

        def send(a, j):
            mine = half(a, me, c)
            return _remote(mine, mine, send_sems, recv_sems, 6 * a + j, (chips[j][0], chips[j][1], c))

        def arrival(a, j):
            landed = half(a, 2 * chips[j][0] + chips[j][1], c)
            return _remote(landed, landed, send_sems, recv_sems, 6 * a + j, (chips[j][0], chips[j][1], c))

        def passed_on(a, j, which):
            landed = half(a, 2 * chips[j][0] + chips[j][1], which)
            return _remote(landed, landed, send_sems, recv_sems, 6 * a + 3 + j, sibling)

        def conv_copy(j, slot):
            return _remote(cw.at[slot], cw.at[slot], send_sems, recv_sems, 6 * n_big + j, (chips[j][0], chips[j][1], c))

        def land(a, j):
            arrival(a, j).wait_recv()
            passed_on(a, j, c).start()
            passed_on(a, j, 1 - c).wait_recv()

        @pl.when((p == 0) & (i == 0))
        def _():
            for j in range(3):
                send(0, j).start()

        for j in range(3):
            @pl.when((p == 2 * (j + 1)) & (i == 0))
            def _(j=j):
                land(0, j)
                if j == 0:
                    for a in range(1, n_big):
                        for jj in range(3):
                            send(a, jj).start()
                    for jj in range(3):
                        conv_copy(jj, me).start()

        @pl.when(i == 0)
        def _():
            shard = shard_of(p // 2, me)
            for which in range(2):
                @pl.when(p % 2 == which)
                def _(which=which):
                    cp = pltpu.make_async_copy(bufs[0].at[shard, :, pl.ds(which * D_MODEL, D_MODEL)], slab, slab_sem)
                    cp.start()
                    cp.wait()

        rows = pl.ds(pl.multiple_of(i * tm, tm), tm)

        @pl.when(p == 0)
        def _():
            xt = x_ref[...]
            r = lax.rsqrt(jnp.mean(xt * xt, axis=-1, keepdims=True) + EPS)
            h = (xt * r) * g_ref[...]
            h_all[rows, :] = h.astype(_MXU_DTYPE)
            ht_ref[...] = jnp.transpose(h).astype(_MXU_DTYPE)

        z_ref[...] = _mm(h_all[rows, :], slab[...])

        @pl.when((p == last_pass) & (i == n_tiles - 1))
        def _():
            for a in range(1, n_big):
                for j in range(3):
                    land(a, j)
            for j in range(3):
                conv_copy(j, 2 * chips[j][0] + chips[j][1]).wait_recv()
            for a in range(n_big):
                for j in range(3):
                    send(a, j).wait_send()
                    passed_on(a, j, c).wait_send()
            for j in range(3):
                conv_copy(j, me).wait_send()

    def z_index(p, i, chip_ref):
        g = 2 * shard_of(p // 2, chip_ref[0]) + p % 2
        return (_slot_of_group(g), i, 0)

    def first_pass_tile(p, i, chip_ref):
        return jnp.where(p == 0, i, n_tiles - 1)

    hbm = pl.BlockSpec(memory_space=pl.ANY)
    operands = list(slotted) + [conv_slotted]
    grid_spec = pltpu.PrefetchScalarGridSpec(
        num_scalar_prefetch=1, grid=(N_GROUPS, n_tiles),
        in_specs=[pl.BlockSpec((tm, d), lambda p, i, chip_ref: (first_pass_tile(p, i, chip_ref), 0)),
                  pl.BlockSpec((1, d), lambda p, i, chip_ref: (0, 0))] + [hbm] * (n_big + 1),
        out_specs=[hbm] * (n_big + 1) + [pl.BlockSpec((None, tm, D_MODEL), z_index),
                                         pl.BlockSpec((d, tm), lambda p, i, chip_ref: (0, first_pass_tile(p, i, chip_ref)))],
        scratch_shapes=[pltpu.VMEM((tokens, d), _MXU_DTYPE), pltpu.VMEM((d, D_MODEL), _MXU_DTYPE),
                        pltpu.SemaphoreType.DMA((n_sem,)), pltpu.SemaphoreType.DMA((n_sem,)), pltpu.SemaphoreType.DMA])
    out = pl.pallas_call(
        body, name="inproj_fwd_gather", grid_spec=grid_spec,
        out_shape=[jax.ShapeDtypeStruct(a.shape, a.dtype) for a in operands]
        + [jax.ShapeDtypeStruct((N_GROUPS, tokens, D_MODEL), F32), jax.ShapeDtypeStruct((d, tokens), _MXU_DTYPE)],
        input_output_aliases={3 + a: a for a in range(n_big + 1)},
        compiler_params=_params(("arbitrary", "arbitrary")),
    )(chip, x2d, norm_g, *operands)
    return out[n_big + 1], out[n_big + 2], out[:n_big], out[n_big]


def _lane_blocks(x):
    return [x[:, k * LANES:(k + 1) * LANES] for k in range(x.shape[1] // LANES)]


def _block_diag(x, w_ref, transposed=False):
    mm = _mm_nt if transposed else _mm
    return jnp.concatenate([mm(xk, w_ref[k]) for k, xk in enumerate(_lane_blocks(x))], axis=1)


def _lru_gates(xa, cw_ref, cb_ref, wx_ref, bx_ref, wa_ref, ba_ref, lam_ref):
    xc = (cb_ref[...] + cw_ref[3:4, :] * xa + cw_ref[2:3, :] * _shift_down(xa, 1)
          + cw_ref[1:2, :] * _shift_down(xa, 2) + cw_ref[0:1, :] * _shift_down(xa, 3))
    gi = _sigmoid(_block_diag(xc, wx_ref) + bx_ref[...])
    gr = _sigmoid(_block_diag(xc, wa_ref) + ba_ref[...])
    sp = _softplus(-lam_ref[...])
    log_a = (-LRU_C) * gr * sp
    a = jnp.exp(log_a)
    y = 2.0 * log_a
    mult_sq = jnp.where(y > -1e-3, -y * (1.0 + 0.5 * y), 1.0 - a * a)
    inv_mult = lax.rsqrt(jnp.maximum(mult_sq, 1e-37))
    return xc, gi, gr, sp, a, mult_sq * inv_mult, inv_mult


def _tile_rows(width):
    return lax.broadcasted_iota(jnp.int32, (SUBLANES, width), 0)


def _scan_forward(a_scr, u_scr, h_scr, seq):
    width = a_scr.shape[1]
    rows = _tile_rows(width)

    def tile(j, carry):
        sl = pl.ds(pl.multiple_of(j * SUBLANES, SUBLANES), SUBLANES)
        a = a_scr[sl, :]
        u = u_scr[sl, :]
        for d in (1, 2, 4):
            keep = rows >= d
            a_sh = jnp.where(keep, pltpu.roll(a, d, 0), 1.0)
            u_sh = jnp.where(keep, pltpu.roll(u, d, 0), 0.0)
            u = a * u_sh + u
            a = a * a_sh
        h = u + a * carry
        h_scr[sl, :] = h
        return jnp.broadcast_to(h[SUBLANES - 1:SUBLANES, :], (SUBLANES, width))

    lax.fori_loop(0, seq // SUBLANES, tile, jnp.zeros((SUBLANES, width), F32))


def _scan_backward(c_scr, d_scr, g_scr, seq):
    width = c_scr.shape[1]
    rows = _tile_rows(width)
    n_tiles = seq // SUBLANES

    def tile(jj, carry):
        j = n_tiles - 1 - jj
        sl = pl.ds(pl.multiple_of(j * SUBLANES, SUBLANES), SUBLANES)
        c = c_scr[sl, :]
        g = d_scr[sl, :]
        for d in (1, 2, 4):
            keep = rows < SUBLANES - d
            c_sh = jnp.where(keep, pltpu.roll(c, SUBLANES - d, 0), 1.0)
            g_sh = jnp.where(keep, pltpu.roll(g, SUBLANES - d, 0), 0.0)
            g = c * g_sh + g
            c = c * c_sh
        g = g + c * carry
        g_scr[sl, :] = g
        return jnp.broadcast_to(g[0:1, :], (SUBLANES, width))

    lax.fori_loop(0, n_tiles, tile, jnp.zeros((SUBLANES, width), F32))


LRU_BLOCKS_PER_STEP = 2
LRU_LANES = LRU_BLOCKS_PER_STEP * LANES
LRU_STEPS = N_BLK // LRU_BLOCKS_PER_STEP


def _lru_param_specs(cb_axis):
    def pick(*ids):
        return ids[cb_axis]

    vec = pl.BlockSpec((1, LRU_LANES), lambda *ids: (0, pick(*ids)))
    mat = pl.BlockSpec((LRU_BLOCKS_PER_STEP, LANES, LANES), lambda *ids: (pick(*ids), 0, 0))
    return [pl.BlockSpec((CONV_WIDTH, LRU_LANES), lambda *ids: (0, pick(*ids))), vec, mat, vec, mat, vec, vec]


def _branch_a_fwd(z, conv_w, conv_b, wx, bx, wa, ba, lam, batch, seq):
    tokens = batch * seq

    def body(z_ref, cw_ref, cb_ref, wx_ref, bx_ref, wa_ref, ba_ref, lam_ref, ya_ref, hl_ref, a_scr, u_scr):
        xa = z_ref[0]
        ga = z_ref[1]
        xc, gi, _, _, a, mult, _ = _lru_gates(xa, cw_ref, cb_ref, wx_ref, bx_ref, wa_ref, ba_ref, lam_ref)
        a_scr[...] = a
        u_scr[...] = mult * gi * xc
        _scan_forward(a_scr, u_scr, hl_ref, seq)
        ya_ref[...] = (hl_ref[...] * (ga * _sigmoid(ga))).astype(_MXU_DTYPE)

    blk = pl.BlockSpec((seq, LRU_LANES), lambda b, c: (b, c))
    return pl.pallas_call(
        body, name="branch_a_fwd",
        grid=(batch, LRU_STEPS),
        in_specs=[pl.BlockSpec((2, seq, LRU_LANES), lambda b, c: (2, b, c))] + _lru_param_specs(1),
        out_specs=[blk, blk],
        out_shape=[jax.ShapeDtypeStruct((tokens, D_MODEL), _MXU_DTYPE), jax.ShapeDtypeStruct((tokens, D_MODEL), F32)],
        scratch_shapes=[pltpu.VMEM((seq, LRU_LANES), F32), pltpu.VMEM((seq, LRU_LANES), F32)],
        compiler_params=_params(("parallel", "parallel")),
    )(z, conv_w, conv_b, wx, bx, wa, ba, lam)


def _branch_a_bwd(z, hl, dya, dz, conv_w, conv_b, wx, bx, wa, ba, lam, batch, seq):
    def body(z_ref, hl_ref, dya_ref, dz_in_ref, cw_ref, cb_ref, wx_ref, bx_ref, wa_ref, ba_ref, lam_ref,
             dz_ref, dcw_ref, dcb_ref, dwx_ref, dbx_ref, dwa_ref, dba_ref, dlam_ref, c_scr, d_scr, g_scr):
        del dz_in_ref
        xa = z_ref[0]
        ga = z_ref[1]
        hl = hl_ref[...]
        dya = dya_ref[...]
        xc, gi, gr, sp, a, mult, inv_mult = _lru_gates(xa, cw_ref, cb_ref, wx_ref, bx_ref, wa_ref, ba_ref, lam_ref)
        sga = _sigmoid(ga)
        dz_ref[1] = (dya * hl * (sga * (1.0 + ga * (1.0 - sga)))).astype(_MXU_DTYPE)
        c_scr[...] = _shift_up(a, 1)
        d_scr[...] = dya * (ga * sga)
        _scan_backward(c_scr, d_scr, g_scr, seq)
        g = g_scr[...]
        da = g * _shift_down(hl, 1)
        dmult = g * gi * xc
        dgi = g * mult * xc
        dxc = g * mult * gi
        dlog_a = da * a - dmult * (a * a) * inv_mult
        dgr = dlog_a * (-LRU_C) * sp
        dsp = jnp.sum(dlog_a * gr, axis=0, keepdims=True) * (-LRU_C)
        dlam = -dsp * _sigmoid(-lam_ref[...])
        dpi = dgi * gi * (1.0 - gi)
        dpr = dgr * gr * (1.0 - gr)
        dxc = dxc + _block_diag(dpi, wx_ref, transposed=True) + _block_diag(dpr, wa_ref, transposed=True)
        dwx = jnp.stack([_mm_tn(xk, dk) for xk, dk in zip(_lane_blocks(xc), _lane_blocks(dpi))])
        dwa = jnp.stack([_mm_tn(xk, dk) for xk, dk in zip(_lane_blocks(xc), _lane_blocks(dpr))])
        dbx = jnp.sum(dpi, axis=0, keepdims=True)
        dba = jnp.sum(dpr, axis=0, keepdims=True)
        ahead = [dxc if k == CONV_WIDTH - 1 else _shift_up(dxc, CONV_WIDTH - 1 - k) for k in range(CONV_WIDTH)]
        dxa = sum(cw_ref[k:k + 1, :] * ahead[k] for k in range(CONV_WIDTH))
        dz_ref[0] = dxa.astype(_MXU_DTYPE)
        dcb = jnp.sum(dxc, axis=0, keepdims=True)
        dcw = [jnp.sum(ahead[k] * xa, axis=0, keepdims=True) for k in range(CONV_WIDTH)]

        @pl.when(pl.program_id(1) == 0)
        def _():
            for k in range(CONV_WIDTH):
                dcw_ref[k:k + 1, :] = dcw[k]
            dcb_ref[...] = dcb
            dwx_ref[...] = dwx
            dbx_ref[...] = dbx
            dwa_ref[...] = dwa
            dba_ref[...] = dba
            dlam_ref[...] = dlam

        @pl.when(pl.program_id(1) != 0)
        def _():
            for k in range(CONV_WIDTH):
                dcw_ref[k:k + 1, :] += dcw[k]
            dcb_ref[...] += dcb
            dwx_ref[...] += dwx
            dbx_ref[...] += dbx
            dwa_ref[...] += dwa
            dba_ref[...] += dba
            dlam_ref[...] += dlam

    tokens = batch * seq
    blk = pl.BlockSpec((seq, LRU_LANES), lambda c, b: (b, c))
    vec = pl.BlockSpec((1, LRU_LANES), lambda c, b: (0, c))
    mat = pl.BlockSpec((LRU_BLOCKS_PER_STEP, LANES, LANES), lambda c, b: (c, 0, 0))
    vec_shape = jax.ShapeDtypeStruct((1, D_MODEL), F32)
    mat_shape = jax.ShapeDtypeStruct((N_BLK, LANES, LANES), F32)
    return pl.pallas_call(
        body, name="branch_a_bwd",
        grid=(LRU_STEPS, batch),
        in_specs=[pl.BlockSpec((2, seq, LRU_LANES), lambda c, b: (2, b, c)), blk, blk,
                  pl.BlockSpec(memory_space=pl.ANY)] + _lru_param_specs(0),
        out_specs=[pl.BlockSpec((2, seq, LRU_LANES), lambda c, b: (2, b, c)),
                   pl.BlockSpec((CONV_WIDTH, LRU_LANES), lambda c, b: (0, c)), vec, mat, vec, mat, vec, vec],
        out_shape=[jax.ShapeDtypeStruct((N_GROUPS, tokens, D_MODEL), _MXU_DTYPE),
                   jax.ShapeDtypeStruct((CONV_WIDTH, D_MODEL), F32), vec_shape, mat_shape, vec_shape, mat_shape,
                   vec_shape, vec_shape],
        scratch_shapes=[pltpu.VMEM((seq, LRU_LANES), F32)] * 3,
        input_output_aliases={3: 0},
        compiler_params=_params(("parallel", "arbitrary")),
    )(z, hl, dya, dz, conv_w, conv_b, wx, bx, wa, ba, lam)


def _chunk_masks(transposed=False):
    r = lax.broadcasted_iota(jnp.int32, (CHUNK, CHUNK), 0)
    c = lax.broadcasted_iota(jnp.int32, (CHUNK, CHUNK), 1)
    return r <= c if transposed else r >= c


def _row_blocks(seq, fn):
    block = min(256, seq)

    def trip(i, carry):
        fn(pl.ds(pl.multiple_of(i * block, block), block))
        return carry

    lax.fori_loop(0, seq // block, trip, 0)


def _hgrn_prepare(z_ref, lb_ref, f_scr, logf_scr, qh_scr, seq):
    lb = _sigmoid(lb_ref[0:1, :] - lb_ref[1:2, :])

    def block(rows):
        q = z_ref[0, rows, :]
        f = lb + (1.0 - lb) * _sigmoid(z_ref[1, rows, :])
        f_scr[rows, :] = f
        logf_scr[rows, :] = jnp.log(f)
        qh_scr[rows, :] = q * _sigmoid(q)

    _row_blocks(seq, block)
    return lb


def _cumsum_rows(x, reverse=False):
    shift = _shift_up if reverse else _shift_down
    d = 1
    while d < x.shape[0]:
        x = x + shift(x, d)
        d *= 2
    return x


def _lane_mean(x):
    return jnp.mean(x, axis=-1, keepdims=True)


def _token_contractions(lhs_scr, rhs_scr, out_ref, seq):
    rows_id = lax.broadcasted_iota(jnp.int32, (LANES, LANES), 0)

    def transposed(p):
        rows = pl.ds(pl.multiple_of(p * LANES, LANES), LANES)
        return jnp.transpose(lhs_scr[rows, :]).astype(_MXU_DTYPE), rhs_scr[rows, :]

    def contract(p, s):
        lhs_t, rhs = s
        return (_mm(lhs_t, jnp.where(rows_id < CHUNK, rhs, 0.0)), _mm(lhs_t, jnp.where(rows_id >= CHUNK, rhs, 0.0)))

    def store(p, out):
        out_ref[2 * p] = out[0]
        out_ref[2 * p + 1] = out[1]

    _independent_trips(seq // LANES, [transposed, contract], store)


def _chunk_rows(c):
    return pl.ds(pl.multiple_of(c * CHUNK, CHUNK), CHUNK)


def _chunk_terms(c, z_ref, f_scr, qh_scr, b_scr):
    rows = _chunk_rows(c)
    b = b_scr[rows, :]
    b_mid = b_scr[pl.ds(c * CHUNK + CHUNK // 2, 1), :]
    b_last = b_scr[pl.ds(c * CHUNK + CHUNK - 1, 1), :]
    qh = qh_scr[rows, :]
    k = 1.0 - f_scr[rows, :]
    v = z_ref[2, rows, :]
    e_q = jnp.exp(b - b_mid) * HG_SCALE
    e_k = jnp.exp(b_mid - b)
    e_qi = jnp.exp(b) * HG_SCALE
    e_ks = jnp.exp(b_last - b)
    decay = jnp.exp(b_last)
    return rows, qh, k, v, e_q, e_k, e_qi, e_ks, decay


def _independent_trips(n, stages, store, group=CHUNKS_IN_FLIGHT):
    stages = stages if isinstance(stages, (list, tuple)) else [stages]
    group = min(group, n)

    def trip(g, carry):
        ids = [g * group + i for i in range(group)]
        state = [stages[0](c) for c in ids]
        for stage in stages[1:]:
            state = [stage(c, s) for c, s in zip(ids, state)]
        for c, s in zip(ids, state):
            store(c, s)
        return carry

    lax.fori_loop(0, n // group, trip, 0)


def _branch_b_fwd(z, lb_logits, hg_g, batch, seq):
    tokens = batch * seq
    n_chunks = seq // CHUNK

    def body(z_ref, lb_ref, g_ref, yb_ref, st_ref, f_scr, logf_scr, qh_scr, b_scr, o_scr, qi_scr, ks_scr, dec_scr):
        _hgrn_prepare(z_ref, lb_ref, f_scr, logf_scr, qh_scr, seq)
        causal = _chunk_masks()
        gain = g_ref[...]

        def cumulate(c):
            return _cumsum_rows(logf_scr[_chunk_rows(c), :])

        def store_cumulated(c, b):
            b_scr[_chunk_rows(c), :] = b

        def scores(c):
            _, qh, k, v, e_q, e_k, e_qi, e_ks, decay = _chunk_terms(c, z_ref, f_scr, qh_scr, b_scr)
            return _mm_nt(qh * e_q, k * e_k), v, qh * e_qi, k * e_ks, decay

        def within_chunk(c, s):
            att, v, q_int, k_st, decay = s
            return _mm(jnp.where(causal, att, 0.0), v), q_int, k_st, decay

        def store_within_chunk(c, out):
            rows = _chunk_rows(c)
            o_scr[rows, :], qi_scr[rows, :], ks_scr[rows, :], dec_scr[pl.ds(c, 1), :] = out

        def carry_state(c, state_t):
            update = st_ref[c]
            st_ref[c] = state_t
            return state_t * dec_scr[pl.ds(c, 1), :] + update

        def finish(c):
            rows = _chunk_rows(c)
            o = o_scr[rows, :] + _mm_nt(qi_scr[rows, :], st_ref[c])
            r = lax.rsqrt(_lane_mean(o * o) + EPS)
            gb = z_ref[3, rows, :]
            return (((o * r) * gain) * (gb * _sigmoid(gb))).astype(_MXU_DTYPE)

        def store_finished(c, yb):
            yb_ref[_chunk_rows(c), :] = yb

        _independent_trips(n_chunks, cumulate, store_cumulated)
        _independent_trips(n_chunks, [scores, within_chunk], store_within_chunk)
        _token_contractions(z_ref.at[2], ks_scr, st_ref, seq)
        lax.fori_loop(0, n_chunks, carry_state, jnp.zeros((LANES, LANES), F32))
        _independent_trips(n_chunks, finish, store_finished, group=2 * CHUNKS_IN_FLIGHT)

    seq_buf = pltpu.VMEM((seq, LANES), F32)
    return pl.pallas_call(
        body, name="branch_b_fwd",
        grid=(batch, N_BLK),
        in_specs=[pl.BlockSpec((4, seq, LANES), lambda b, h: (0, b, h)),
                  pl.BlockSpec((2, LANES), lambda b, h: (0, h)),
                  pl.BlockSpec((1, LANES), lambda b, h: (0, 0))],
        out_specs=[pl.BlockSpec((seq, LANES), lambda b, h: (b, h)),
                   pl.BlockSpec((None, n_chunks, LANES, LANES), lambda b, h: (b * N_BLK + h, 0, 0, 0))],
        out_shape=[jax.ShapeDtypeStruct((tokens, D_MODEL), _MXU_DTYPE),
                   jax.ShapeDtypeStruct((batch * N_BLK, n_chunks, LANES, LANES), F32)],
        scratch_shapes=[seq_buf] * 7 + [pltpu.VMEM((n_chunks, LANES), F32)],
        compiler_params=_params(("parallel", "parallel")),
    )(z, lb_logits, hg_g)


def _branch_b_bwd(z, states, dyb, dz, lb_logits, hg_g, batch, seq):
    n_chunks = seq // CHUNK

    def body(z_ref, st_ref, dyb_ref, dz_in_ref, lb_ref, g_ref, dz_ref, dlog_ref, dg_ref,
             f_scr, logf_scr, qh_scr, b_scr, do_scr, qi_scr, dqh_scr, df_scr, dec_scr, dgp_scr, dlb_scr, dst_scr):
        del dz_in_ref
        first = (pl.program_id(0) == 0) & (pl.program_id(1) == 0)
        lb = _hgrn_prepare(z_ref, lb_ref, f_scr, logf_scr, qh_scr, seq)
        causal = _chunk_masks()
        anti_causal = _chunk_masks(transposed=True)
        gain = g_ref[...]

        @pl.when(first)
        def _():
            dg_ref[...] = jnp.zeros_like(dg_ref)

        @pl.when(pl.program_id(1) == 0)
        def _():
            dlb_scr[...] = jnp.zeros_like(dlb_scr)

        def cumulate(c):
            return _cumsum_rows(logf_scr[_chunk_rows(c), :])

        def store_cumulated(c, b):
            b_scr[_chunk_rows(c), :] = b

        def scores(c):
            _, qh, k, v, e_q, e_k, e_qi, e_ks, decay = _chunk_terms(c, z_ref, f_scr, qh_scr, b_scr)
            q_int = qh * e_qi
            return _mm_nt(qh * e_q, k * e_k), _mm_nt(q_int, st_ref[c]), v, q_int, decay

        def output_gradient(c, s):
            att, o_inter, v, q_int, decay = s
            rows = _chunk_rows(c)
            o = _mm(jnp.where(causal, att, 0.0), v) + o_inter
            r = lax.rsqrt(_lane_mean(o * o) + EPS)
            o_n = o * r
            gb = z_ref[3, rows, :]
            sgb = _sigmoid(gb)
            dyb_c = dyb_ref[rows, :]
            d_ong = dyb_c * (gb * sgb)
            d_gb = (dyb_c * (o_n * gain) * (sgb * (1.0 + gb * (1.0 - sgb)))).astype(_MXU_DTYPE)
            d_gain = jnp.sum(d_ong * o_n, axis=0, keepdims=True)
            d_on = d_ong * gain
            return d_gb, d_gain, r * (d_on - o_n * _lane_mean(d_on * o_n)), q_int, decay

        def store_output_gradient(c, out):
            rows = _chunk_rows(c)
            dz_ref[3, rows, :], dgp_scr[pl.ds(c, 1), :], do_scr[rows, :], qi_scr[rows, :], dec_scr[pl.ds(c, 1), :] = out

        def carry_state_gradient(cc, d_state_t):
            c = n_chunks - 1 - cc
            update = dst_scr[c]
            dst_scr[c] = d_state_t
            return d_state_t * dec_scr[pl.ds(c, 1), :] + update

        def score_gradients(c):
            rows, qh, k, v, e_q, e_k, e_qi, e_ks, decay = _chunk_terms(c, z_ref, f_scr, qh_scr, b_scr)
            state_t = st_ref[c]
            d_state_t = dst_scr[c]
            d_o = do_scr[rows, :]
            q_in, k_in, q_int, k_st = qh * e_q, k * e_k, qh * e_qi, k * e_ks
            first = (_mm_nt(k_in, q_in), _mm_nt(d_o, v), _mm_nt(v, d_o), _mm_nt(k_st, d_state_t), _mm(d_o, state_t),
                     _mm(v, d_state_t))
            d_decay = jnp.sum(state_t * d_state_t, axis=0, keepdims=True)
            return first, d_o, q_in, k_in, q_int, k_st, e_q, e_k, e_qi, e_ks, decay, d_decay

        def input_gradients(c, s):
            (att_t, d_att, d_att_t, dv_inter, dq_int, dk_st), d_o, q_in, k_in, q_int, k_st, e_q, e_k, e_qi, e_ks, decay, d_decay = s
            rows = _chunk_rows(c)
            d_v = _mm(jnp.where(anti_causal, att_t, 0.0), d_o) + dv_inter
            dq_in = _mm(jnp.where(causal, d_att, 0.0), k_in)
            dk_in = _mm(jnp.where(anti_causal, d_att_t, 0.0), q_in)
            d_k = dk_in * e_k + dk_st * e_ks
            kk = dk_st * k_st
            d_b = dq_in * q_in + dq_int * q_int - dk_in * k_in - kk
            d_b_last = jnp.sum(kk, axis=0, keepdims=True) + decay * d_decay
            d_logf = _cumsum_rows(d_b, reverse=True) + d_b_last
            return d_v.astype(_MXU_DTYPE), dq_in * e_q + dq_int * e_qi, d_logf / f_scr[rows, :] - d_k

        def store_input_gradients(c, out):
            rows = _chunk_rows(c)
            dz_ref[2, rows, :], dqh_scr[rows, :], df_scr[rows, :] = out

        def input_activations(rows):
            q = z_ref[0, rows, :]
            sq = _sigmoid(q)
            dz_ref[0, rows, :] = (dqh_scr[rows, :] * (sq * (1.0 + q * (1.0 - sq)))).astype(_MXU_DTYPE)
            sg = _sigmoid(z_ref[1, rows, :])
            d_f = df_scr[rows, :]
            dz_ref[1, rows, :] = (d_f * (1.0 - lb) * sg * (1.0 - sg)).astype(_MXU_DTYPE)
            dlb_scr[...] += jnp.sum(d_f * (1.0 - sg), axis=0, keepdims=True)

        _independent_trips(n_chunks, cumulate, store_cumulated)
        _independent_trips(n_chunks, [scores, output_gradient], store_output_gradient)
        _token_contractions(do_scr, qi_scr, dst_scr, seq)
        lax.fori_loop(0, n_chunks, carry_state_gradient, jnp.zeros((LANES, LANES), F32))
        _independent_trips(n_chunks, [score_gradients, input_gradients], store_input_gradients)
        dg_ref[...] += jnp.sum(dgp_scr[...], axis=0, keepdims=True)
        _row_blocks(seq, input_activations)
        d_l0 = dlb_scr[...] * lb * (1.0 - lb)
        dlog_ref[0:1, :] = d_l0
        dlog_ref[1:2, :] = -d_l0

    tokens = batch * seq
    seq_buf = pltpu.VMEM((seq, LANES), F32)
    chunk_rows = pltpu.VMEM((n_chunks, LANES), F32)
    return pl.pallas_call(
        body, name="branch_b_bwd",
        grid=(N_BLK, batch),
        in_specs=[pl.BlockSpec((4, seq, LANES), lambda h, b: (0, b, h)),
                  pl.BlockSpec((None, n_chunks, LANES, LANES), lambda h, b: (b * N_BLK + h, 0, 0, 0)),
                  pl.BlockSpec((seq, LANES), lambda h, b: (b, h)),
                  pl.BlockSpec(memory_space=pl.ANY),
                  pl.BlockSpec((2, LANES), lambda h, b: (0, h)),
                  pl.BlockSpec((1, LANES), lambda h, b: (0, 0))],
        out_specs=[pl.BlockSpec((4, seq, LANES), lambda h, b: (0, b, h)),
                   pl.BlockSpec((2, LANES), lambda h, b: (0, h)),
                   pl.BlockSpec((1, LANES), lambda h, b: (0, 0))],
        out_shape=[jax.ShapeDtypeStruct((N_GROUPS, tokens, D_MODEL), _MXU_DTYPE),
                   jax.ShapeDtypeStruct((2, D_MODEL), F32),
                   jax.ShapeDtypeStruct((1, LANES), F32)],
        scratch_shapes=[seq_buf] * 8 + [chunk_rows, chunk_rows, pltpu.VMEM((1, LANES), F32),
                                        pltpu.VMEM((n_chunks, LANES, LANES), F32)],
        input_output_aliases={3: 0},
        compiler_params=_params(("arbitrary", "arbitrary")),
    )(z, states, dyb, dz, lb_logits, hg_g)


def _merge_tail(ya, yb, z, x2d, tgt2d, b_merge, final_g, pa, pb, wo):
    tokens, d = x2d.shape
    tm = min(256, tokens)
    n_tiles = tokens // tm

    def body(ya_ref, yb_ref, z_ref, x_ref, t_ref, bm_ref, fg_ref, pa_hbm, pb_hbm, wo_hbm,
             dya_ref, dyb_ref, dx2_ref, dz_ref, loss_ref, dfg_ref, dbm_ref, dpa_hbm, dpb_hbm, dwo_hbm,
             pa_s, pb_s, wo_s, dpa_s, dpb_s, dwo_s):
        i = pl.program_id(0)

        @pl.when(i == 0)
        def _():
            pltpu.sync_copy(pa_hbm, pa_s)
            pltpu.sync_copy(pb_hbm, pb_s)
            pltpu.sync_copy(wo_hbm, wo_s)
            dpa_s[...] = jnp.zeros_like(dpa_s)
            dpb_s[...] = jnp.zeros_like(dpb_s)
            dwo_s[...] = jnp.zeros_like(dwo_s)
            loss_ref[...] = jnp.zeros_like(loss_ref)
            dfg_ref[...] = jnp.zeros_like(dfg_ref)
            dbm_ref[...] = jnp.zeros_like(dbm_ref)

        ya_t = ya_ref[...]
        yb_t = yb_ref[...]
        out_a = _mm(ya_t, pa_s[...])
        out_b = _mm(yb_t, pb_s[...])
        g_a = _sigmoid(z_ref[0] + bm_ref[:, :d])
        g_b = _sigmoid(z_ref[1] + bm_ref[:, d:])
        mixed = g_a * out_a + g_b * out_b
        x2 = x_ref[...] + _mm(mixed, wo_s[...])
        r = lax.rsqrt(jnp.mean(x2 * x2, axis=-1, keepdims=True) + EPS)
        xn = x2 * r
        fg = fg_ref[...]
        diff = xn * fg - t_ref[...]
        loss_ref[...] += jnp.sum(diff * diff) * (0.5 / d)
        dy = diff * (1.0 / d)
        dfg_ref[...] += jnp.sum(dy * xn, axis=0, keepdims=True)
        dxn = dy * fg
        dx2 = r * (dxn - xn * jnp.mean(dxn * xn, axis=-1, keepdims=True))
        dx2_ref[...] = dx2
        dmixed = _mm_nt(dx2, wo_s[...])
        dwo_s[...] += _mm_tn(mixed, dx2)
        dgm_a = dmixed * out_a * g_a * (1.0 - g_a)
        dgm_b = dmixed * out_b * g_b * (1.0 - g_b)
        dz_ref[0] = dgm_a.astype(_MXU_DTYPE)
        dz_ref[1] = dgm_b.astype(_MXU_DTYPE)
        dbm_ref[:, :d] += jnp.sum(dgm_a, axis=0, keepdims=True)
        dbm_ref[:, d:] += jnp.sum(dgm_b, axis=0, keepdims=True)
        dout_a = dmixed * g_a
        dout_b = dmixed * g_b
        dpa_s[...] += _mm_tn(ya_t, dout_a)
        dpb_s[...] += _mm_tn(yb_t, dout_b)
        dya_ref[...] = _mm_nt(dout_a, pa_s[...])
        dyb_ref[...] = _mm_nt(dout_b, pb_s[...])

        @pl.when(i == n_tiles - 1)
        def _():
            pltpu.sync_copy(dpa_s, dpa_hbm)
            pltpu.sync_copy(dpb_s, dpb_hbm)
            pltpu.sync_copy(dwo_s, dwo_hbm)

    tile = pl.BlockSpec((tm, d), lambda i: (i, 0))
    gm = pl.BlockSpec((2, tm, d), lambda i: (3, i, 0))
    row = lambda n: pl.BlockSpec((1, n), lambda i: (0, 0))
    hbm = pl.BlockSpec(memory_space=pl.ANY)
    act = jax.ShapeDtypeStruct((tokens, d), F32)
    mat = jax.ShapeDtypeStruct((d, d), F32)
    return pl.pallas_call(
        body, name="merge_tail",
        grid=(n_tiles,),
        in_specs=[tile, tile, gm, tile, tile, row(2 * d), row(d), hbm, hbm, hbm],
        out_specs=[tile, tile, tile, gm, row(LANES), row(d), row(2 * d), hbm, hbm, hbm],
        out_shape=[act, act, act, jax.ShapeDtypeStruct((N_GROUPS, tokens, d), _MXU_DTYPE),
                   jax.ShapeDtypeStruct((1, LANES), F32), jax.ShapeDtypeStruct((1, d), F32),
                   jax.ShapeDtypeStruct((1, 2 * d), F32), mat, mat, mat],
        scratch_shapes=[pltpu.VMEM((d, d), _MXU_DTYPE)] * 3 + [pltpu.VMEM((d, d), F32)] * 3,
        compiler_params=_params(("arbitrary",)),
    )(ya, yb, z, x2d, tgt2d, b_merge, final_g, pa, pb, wo)


def _inproj_dw(h_t, dz):
    d, tokens = h_t.shape
    tm = min(2048, tokens)

    def body(h_ref, dz_ref, dw_ref):
        part = _mm(h_ref[...], dz_ref[...])

        @pl.when(pl.program_id(1) == 0)
        def _():
            dw_ref[...] = part

        @pl.when(pl.program_id(1) != 0)
        def _():
            dw_ref[...] += part

    def out_index(s, i):
        g = _group_of_slot(s)
        return (g // 2, 0, g % 2)

    return pl.pallas_call(
        body, name="inproj_dw",
        grid=(N_GROUPS, tokens // tm),
        in_specs=[pl.BlockSpec((d, tm), lambda s, i: (0, i)),
                  pl.BlockSpec((None, tm, D_MODEL), lambda s, i: (s, i, 0))],
        out_specs=pl.BlockSpec((None, d, D_MODEL), out_index),
        out_shape=jax.ShapeDtypeStruct((N_SHARDS, d, 2 * D_MODEL), F32),
        compiler_params=_params(("parallel", "arbitrary")),
    )(h_t, dz)


def _inproj_dx(dz, w_all, x2d, dx2, norm_g, scatter=None):
    tokens, d = x2d.shape
    tm = min(512, tokens)
    n_tiles = tokens // tm
    n_big = len(scatter[0]) if scatter else 0

    def body(dz_ref, w_hbm, x_ref, dx2_ref, g_ref, *rest):
        if scatter:
            srcs, small_src = rest[:n_big], rest[2 * n_big]
            dx_ref, dg_ref = rest[2 * n_big + 1:2 * n_big + 3]
            outs, small_out = rest[2 * n_big + 3:3 * n_big + 3], rest[3 * n_big + 3]
            acc, w_res, send_sems, recv_sems, local_sem = rest[3 * n_big + 4:]
            copies = _scatter_copies(srcs, small_src, outs, small_out, send_sems, recv_sems, local_sem)
        else:
            dx_ref, dg_ref, acc, w_res = rest
            copies = []
        s = pl.program_id(1)

        @pl.when((pl.program_id(0) == 0) & (s == 0))
        def _():
            for cp in copies:
                cp.start()
            _load_w_in_by_slot(w_hbm, w_res)

        part = _mm_nt(dz_ref[...], w_res[s])

        @pl.when(s == 0)
        def _():
            acc[...] = part

        @pl.when(s != 0)
        def _():
            acc[...] += part

        @pl.when((pl.program_id(0) == 0) & (s == 0))
        def _():
            dg_ref[...] = jnp.zeros_like(dg_ref)

        @pl.when(s == N_GROUPS - 1)
        def _():
            x = x_ref[...]
            r = lax.rsqrt(jnp.mean(x * x, axis=-1, keepdims=True) + EPS)
            xn = x * r
            dh = acc[...]
            dg_ref[...] += jnp.sum(dh * xn, axis=0, keepdims=True)
            dxn = dh * g_ref[...]
            dx_ref[...] = r * (dxn - xn * jnp.mean(dxn * xn, axis=-1, keepdims=True)) + dx2_ref[...]

        @pl.when((pl.program_id(0) == n_tiles - 1) & (s == N_GROUPS - 1))
        def _():
            for cp in copies:
                cp.wait()

    tile = pl.BlockSpec((tm, d), lambda i, s: (i, 0))
    hbm = pl.BlockSpec(memory_space=pl.ANY)
    in_specs = [pl.BlockSpec((None, tm, D_MODEL), lambda i, s: (s, i, 0)), hbm, tile, tile,
                pl.BlockSpec((1, d), lambda i, s: (0, 0))]
    out_specs = [tile, pl.BlockSpec((1, d), lambda i, s: (0, 0))]
    out_shape = [jax.ShapeDtypeStruct((tokens, d), F32), jax.ShapeDtypeStruct((1, d), F32)]
    scratch = [pltpu.VMEM((tm, d), F32), pltpu.VMEM((N_GROUPS, d, D_MODEL), _MXU_DTYPE)]
    operands, aliases = [dz, w_all, x2d, dx2, norm_g], {}
    if scatter:
        bigs, by_chip, small_piece = scatter
        n_sem = 3 * n_big + N_DEV - 1
        in_specs += [hbm] * (2 * n_big + 1)
        out_specs += [hbm] * (n_big + 1)
        out_shape += [jax.ShapeDtypeStruct(g.shape, g.dtype) for g in by_chip]
        out_shape.append(jax.ShapeDtypeStruct((N_DEV, PIECE_ROWS, LANES), F32))
        scratch += [pltpu.SemaphoreType.DMA((n_sem,)), pltpu.SemaphoreType.DMA((n_sem,)), pltpu.SemaphoreType.DMA]
        operands += [*bigs, *by_chip, small_piece]
        aliases = {5 + n_big + a: 2 + a for a in range(n_big)}
    return pl.pallas_call(
        body, name="inproj_dx", grid=(n_tiles, N_GROUPS), in_specs=in_specs, out_specs=out_specs, out_shape=out_shape,
        scratch_shapes=scratch, input_output_aliases=aliases,
        compiler_params=_params(("arbitrary", "arbitrary")),
    )(*operands)


def _row_tile(rows, cols, itemsize=4, budget=2 * 1024 * 1024):
    tr = rows
    while tr * cols * itemsize > budget and tr % 16 == 0:
        tr //= 2
    return tr


def _cast_into_slot(a, chip, dtype, name):
    rows, cols = a.shape
    tr = _row_tile(rows, cols)

    def body(chip_ref, a_ref, o_ref):
        del chip_ref
        o_ref[...] = a_ref[...].astype(dtype)

    grid_spec = pltpu.PrefetchScalarGridSpec(
        num_scalar_prefetch=1, grid=(rows // tr,),
        in_specs=[pl.BlockSpec((tr, cols), lambda i, chip_ref: (i, 0))],
        out_specs=pl.BlockSpec((None, tr, cols), lambda i, chip_ref: (chip_ref[0], i, 0)))
    return pl.pallas_call(body, name=name, grid_spec=grid_spec,
                          out_shape=jax.ShapeDtypeStruct((N_SHARDS, rows, cols), dtype),
                          compiler_params=_params(("arbitrary",)))(chip, a)


def _sum_slots(stack, name):
    n, rows, cols = stack.shape
    tr = _row_tile(rows, cols * n)

    def body(s_ref, o_ref):
        total = s_ref[0].astype(F32)
        for k in range(1, n):
            total = total + s_ref[k].astype(F32)
        o_ref[...] = total

    return pl.pallas_call(body, name=name, grid=(rows // tr,),
                          in_specs=[pl.BlockSpec((n, tr, cols), lambda i: (0, i, 0))],
                          out_specs=pl.BlockSpec((tr, cols), lambda i: (i, 0)),
                          out_shape=jax.ShapeDtypeStruct((rows, cols), F32),
                          compiler_params=_params(("parallel",)))(stack)


def _add_half(full, landed, place, name):
    n, rows, cols = full.shape
    half = rows // 2
    tr = _row_tile(half, cols)
    nb = half // tr

    def body(place_ref, a_ref, b_ref, o_ref, own_ref):
        total = (a_ref[...] + b_ref[...]).astype(_MXU_DTYPE)
        o_ref[...] = total

        @pl.when(pl.program_id(1) == place_ref[1])
        def _():
            own_ref[...] = total

    grid_spec = pltpu.PrefetchScalarGridSpec(
        num_scalar_prefetch=1, grid=(nb, n),
        in_specs=[pl.BlockSpec((None, tr, cols), lambda i, j, place_ref: (j, place_ref[0] * nb + i, 0)),
                  pl.BlockSpec((None, tr, cols), lambda i, j, place_ref: (j, i, 0))],
        out_specs=[pl.BlockSpec((None, tr, cols), lambda i, j, place_ref: (j, i, 0)),
                   pl.BlockSpec((None, tr, cols), lambda i, j, place_ref: (place_ref[1], i, 0))])
    shape = jax.ShapeDtypeStruct((n, half, cols), _MXU_DTYPE)
    return pl.pallas_call(body, name=name, grid_spec=grid_spec, out_shape=[shape, shape],
                          compiler_params=_params(("parallel", "arbitrary")))(place, full, landed)


def _adamw_update(w, grad, m, v):
    c1 = 1.0 - ADAM_B1 ** ADAM_STEP
    c2 = 1.0 - ADAM_B2 ** ADAM_STEP
    nm = ADAM_B1 * m + (1.0 - ADAM_B1) * grad
    nv = ADAM_B2 * v + (1.0 - ADAM_B2) * (grad * grad)
    return (-ADAM_LR) * ((nm / c1) / (jnp.sqrt(nv / c2) + ADAM_EPS) + ADAM_WD * w), nm, nv


def _adamw(w, g, m, v, name):
    rows, cols = w.shape
    tr = _row_tile(rows, cols, budget=1024 * 1024)

    def body(w_ref, g_ref, m_ref, v_ref, d_ref, nm_ref, nv_ref):
        d_ref[...], nm_ref[...], nv_ref[...] = _adamw_update(w_ref[...], g_ref[...], m_ref[...], v_ref[...])

    spec = pl.BlockSpec((tr, cols), lambda i: (i, 0))
    shape = jax.ShapeDtypeStruct((rows, cols), F32)
    return pl.pallas_call(body, name=name, grid=(rows // tr,), in_specs=[spec] * 4, out_specs=[spec] * 3,
                          out_shape=[shape] * 3, compiler_params=_params(("parallel",)))(w, g, m, v)


def _adamw_halves(w, g_mine, g_sibling, m, v, core, name):
    rows, cols = w.shape
    half = rows // 2
    tr = _row_tile(half, cols, budget=1024 * 1024)
    nb = half // tr

    def body(core_ref, w_ref, gm_ref, gs_ref, m_ref, v_ref, g_ref, d_ref, nm_ref, nv_ref):
        mine = pl.program_id(0) // nb == core_ref[0]
        grad = jnp.where(mine, gm_ref[...], gs_ref[...])
        g_ref[...] = grad
        d_ref[...], nm_ref[...], nv_ref[...] = _adamw_update(w_ref[...], grad, m_ref[...], v_ref[...])

    spec = pl.BlockSpec((tr, cols), lambda i, core_ref: (i, 0))
    half_spec = pl.BlockSpec((tr, cols), lambda i, core_ref: (i % nb, 0))
    grid_spec = pltpu.PrefetchScalarGridSpec(num_scalar_prefetch=1, grid=(rows // tr,),
                                             in_specs=[spec, half_spec, half_spec, spec, spec], out_specs=[spec] * 4)
    shape = jax.ShapeDtypeStruct((rows, cols), F32)
    return pl.pallas_call(body, name=name, grid_spec=grid_spec, out_shape=[shape] * 4,
                          compiler_params=_params(("parallel",)))(core, w, g_mine, g_sibling, m, v)


def _local_step(x, loss_target, w_all, pa, pb, wo, conv_w, b_merge, conv_b, rg_wx, rg_bx, rg_wa, rg_ba,
                rg_lambda, hg_lb_logits, hg_norm_g, norm_g, final_norm_g, gather=None, start_reduction=None):
    batch, seq, d = x.shape
    x2d = x.reshape(batch * seq, d)
    tgt2d = loss_target.reshape(batch * seq, d)
    if gather is None:
        z, h_t = _inproj_fwd(x2d, norm_g, w_all)
    else:
        z, h_t, (w_all, pa, pb, wo), cw_all = _inproj_fwd_gather(x2d, norm_g, *gather)
        pa, pb, wo = (t.reshape(d, d) for t in (pa, pb, wo))
        conv_w = jnp.transpose(cw_all, (1, 0, 2)).reshape(CONV_WIDTH, d)
    lru = (conv_w, conv_b, rg_wx, rg_bx, rg_wa, rg_ba, rg_lambda)
    ya, hl = _branch_a_fwd(z, *lru, batch, seq)
    yb, states = _branch_b_fwd(z, hg_lb_logits, hg_norm_g, batch, seq)
    dya, dyb, dx2, dz, loss, d_final_g, d_b_merge, d_pa, d_pb, d_wo = _merge_tail(
        ya, yb, z, x2d, tgt2d, b_merge, final_norm_g, pa, pb, wo)
    dz, d_lb_logits, d_hg_g = _branch_b_bwd(z, states, dyb, dz, hg_lb_logits, hg_norm_g, batch, seq)
    dz, d_conv_w, d_conv_b, d_wx, d_bx, d_wa, d_ba, d_lam = _branch_a_bwd(z, hl, dya, dz, *lru, batch, seq)
    d_w_in = _inproj_dw(h_t, dz)
    big = (d_w_in, d_pa, d_pb, d_wo)
    small = dict(b_merge=d_b_merge, conv_w=d_conv_w, conv_b=d_conv_b, rg_wx=d_wx, rg_bx=d_bx, rg_wa=d_wa,
                 rg_ba=d_ba, rg_lambda=d_lam, hg_lb_logits=d_lb_logits, hg_norm_g=d_hg_g,
                 norm_g=jnp.zeros((1, d), F32), final_norm_g=d_final_g)
    if start_reduction is None:
        grad_x, small["norm_g"] = _inproj_dx(dz, w_all, x2d, dx2, norm_g)
        return loss[0, 0], grad_x.reshape(batch, seq, d), big, small
    grad_x, d_norm_g, *scattered = _inproj_dx(dz, w_all, x2d, dx2, norm_g, scatter=start_reduction(big, small))
    return loss[0, 0], grad_x.reshape(batch, seq, d), d_norm_g, scattered


_SMALL_ORDER = ("b_merge", "conv_w", "conv_b", "rg_wx", "rg_bx", "rg_wa", "rg_ba", "rg_lambda", "hg_lb_logits",
                "hg_norm_g", "norm_g", "final_norm_g")
N_DEV = 8
PIECE_ROWS = 272


def _pack_small(tree):
    flat = jnp.concatenate([tree[k].reshape(-1) for k in _SMALL_ORDER])
    flat = jnp.pad(flat, (0, N_DEV * PIECE_ROWS * LANES - flat.shape[0]))
    return flat.reshape(N_DEV * PIECE_ROWS, LANES)


def _unpack_small(packed, like):
    flat = packed.reshape(-1)
    out, pos = {}, 0
    for k in _SMALL_ORDER:
        n = like[k].size
        out[k] = flat[pos:pos + n].reshape(like[k].shape)
        pos += n
    return out


def _mesh_position():
    x, y, c = lax.axis_index("x"), lax.axis_index("y"), lax.axis_index("c")
    other_chips = [(1 - x, y), (x, 1 - y), (1 - x, 1 - y)]
    return x, y, c, other_chips


def _other_devices(x, y, c):
    flips = [(fx, fy, fc) for fx in (0, 1) for fy in (0, 1) for fc in (0, 1) if (fx, fy, fc) != (0, 0, 0)]
    return [(jnp.where(fx, 1 - x, x), jnp.where(fy, 1 - y, y), jnp.where(fc, 1 - c, c)) for fx, fy, fc in flips]


def _remote(src, dst, send_sems, recv_sems, k, device):
    return pltpu.make_async_remote_copy(src_ref=src, dst_ref=dst, send_sem=send_sems.at[k], recv_sem=recv_sems.at[k],
                                        device_id=device, device_id_type=MESH)


def _exchange_halves(bigs, small):
    n_big = len(bigs)
    n_sem = n_big + N_DEV - 1

    def body(*refs):
        srcs, small_src = refs[:n_big], refs[n_big]
        outs, small_out = refs[n_big + 1:2 * n_big + 1], refs[2 * n_big + 1]
        send_sems, recv_sems, local_sem = refs[2 * n_big + 2:]
        x, y, c, _ = _mesh_position()
        me, sibling = 4 * x + 2 * y + c, (x, y, 1 - c)
        mine = pltpu.make_async_copy(small_src.at[pl.ds(me * PIECE_ROWS, PIECE_ROWS), :], small_out.at[me], local_sem)
        mine.start()
        copies = []
        for a in range(n_big):
            hs = srcs[a].shape[1] // 2
            copies.append(_remote(srcs[a].at[:, pl.ds((1 - c) * hs, hs), :], outs[a], send_sems, recv_sems, a, sibling))
        for k, (px, py, pc) in enumerate(_other_devices(x, y, c)):
            piece = small_src.at[pl.ds((4 * px + 2 * py + pc) * PIECE_ROWS, PIECE_ROWS), :]
            copies.append(_remote(piece, small_out.at[me], send_sems, recv_sems, n_big + k, (px, py, pc)))
        for cp in copies:
            cp.start()
        for cp in copies:
            cp.wait()
        mine.wait()

    hbm = pl.BlockSpec(memory_space=pl.ANY)
    out_shape = [jax.ShapeDtypeStruct((g.shape[0], g.shape[1] // 2, g.shape[2]), F32) for g in bigs]
    out_shape.append(jax.ShapeDtypeStruct((N_DEV, PIECE_ROWS, LANES), F32))
    return pl.pallas_call(
        body, name="exchange_halves",
        in_specs=[hbm] * (n_big + 1), out_specs=[hbm] * (n_big + 1), out_shape=out_shape,
        scratch_shapes=[pltpu.SemaphoreType.DMA((n_sem,)), pltpu.SemaphoreType.DMA((n_sem,)), pltpu.SemaphoreType.DMA],
    )(*bigs, small)


def _scatter_copies(srcs, small_src, outs, small_out, send_sems, recv_sems, local_sem):
    n_big = len(srcs)
    x, y, c, chips = _mesh_position()
    chip, me = 2 * x + y, 4 * x + 2 * y + c
    copies = [pltpu.make_async_copy(small_src, small_out.at[me], local_sem)]
    for a in range(n_big):
        for j, (cx, cy) in enumerate(chips):
            copies.append(_remote(srcs[a].at[2 * cx + cy], outs[a].at[chip], send_sems, recv_sems, 3 * a + j, (cx, cy, c)))
    for k, peer in enumerate(_other_devices(x, y, c)):
        copies.append(_remote(small_src, small_out.at[me], send_sems, recv_sems, 3 * n_big + k, peer))
    return copies


def _swap_halves(halves, vec):
    n_big = len(halves)

    def body(*refs):
        srcs, vec_src = refs[:n_big], refs[n_big]
        outs, vec_out = refs[n_big + 1:2 * n_big + 1], refs[2 * n_big + 1]
        send_sems, recv_sems, local_sem = refs[2 * n_big + 2:]
        x, y, c, _ = _mesh_position()
        me = 4 * x + 2 * y + c
        copies = [pltpu.make_async_copy(vec_src, vec_out.at[me], local_sem)]
        copies += [_remote(srcs[a], outs[a], send_sems, recv_sems, a, (x, y, 1 - c)) for a in range(n_big)]
        copies += [_remote(vec_src, vec_out.at[me], send_sems, recv_sems, n_big + k, peer)
                   for k, peer in enumerate(_other_devices(x, y, c))]
        for cp in copies:
            cp.start()
        for cp in copies:
            cp.wait()

    hbm = pl.BlockSpec(memory_space=pl.ANY)
    n_sem = n_big + N_DEV - 1
    return pl.pallas_call(
        body, name="swap_halves",
        in_specs=[hbm] * (n_big + 1), out_specs=[hbm] * (n_big + 1),
        out_shape=[jax.ShapeDtypeStruct(h.shape, F32) for h in halves] + [jax.ShapeDtypeStruct((N_DEV,) + vec.shape, F32)],
        scratch_shapes=[pltpu.SemaphoreType.DMA((n_sem,)), pltpu.SemaphoreType.DMA((n_sem,)), pltpu.SemaphoreType.DMA],
    )(*halves, vec)


def kernel(x, w_in, b_merge, conv_w, conv_b, rg_wx, rg_bx, rg_wa, rg_ba, rg_lambda, hg_lb_logits, hg_norm_g, proj_a, proj_b, w_out, norm_g, final_norm_g, loss_target, m_w_in, m_b_merge, m_conv_w, m_conv_b, m_rg_wx, m_rg_bx, m_rg_wa, m_rg_ba, m_rg_lambda, m_hg_lb_logits, m_hg_norm_g, m_proj_a, m_proj_b, m_w_out, m_norm_g, m_final_norm_g, v_w_in, v_b_merge, v_conv_w, v_conv_b, v_rg_wx, v_rg_bx, v_rg_wa, v_rg_ba, v_rg_lambda, v_hg_lb_logits, v_hg_norm_g, v_proj_a, v_proj_b, v_w_out, v_norm_g, v_final_norm_g):
    d = D_MODEL
    weights = dict(w_in=w_in, b_merge=b_merge, conv_w=conv_w, conv_b=conv_b, rg_wx=rg_wx, rg_bx=rg_bx, rg_wa=rg_wa,
                   rg_ba=rg_ba, rg_lambda=rg_lambda, hg_lb_logits=hg_lb_logits, hg_norm_g=hg_norm_g, proj_a=proj_a,
                   proj_b=proj_b, w_out=w_out, norm_g=norm_g, final_norm_g=final_norm_g)
    m = dict(w_in=m_w_in, b_merge=m_b_merge, conv_w=m_conv_w, conv_b=m_conv_b, rg_wx=m_rg_wx, rg_bx=m_rg_bx,
             rg_wa=m_rg_wa, rg_ba=m_rg_ba, rg_lambda=m_rg_lambda, hg_lb_logits=m_hg_lb_logits, hg_norm_g=m_hg_norm_g,
             proj_a=m_proj_a, proj_b=m_proj_b, w_out=m_w_out, norm_g=m_norm_g, final_norm_g=m_final_norm_g)
    v = dict(w_in=v_w_in, b_merge=v_b_merge, conv_w=v_conv_w, conv_b=v_conv_b, rg_wx=v_rg_wx, rg_bx=v_rg_bx,
             rg_wa=v_rg_wa, rg_ba=v_rg_ba, rg_lambda=v_rg_lambda, hg_lb_logits=v_hg_lb_logits, hg_norm_g=v_hg_norm_g,
             proj_a=v_proj_a, proj_b=v_proj_b, w_out=v_w_out, norm_g=v_norm_g, final_norm_g=v_final_norm_g)
    big_names = ("w_in", "proj_a", "proj_b", "w_out")

    core = lax.axis_index("c").astype(jnp.int32).reshape(1)
    chip = (2 * lax.axis_index("x") + lax.axis_index("y")).astype(jnp.int32)

    slotted = [_cast_into_slot(weights[k][0], chip.reshape(1), _MXU_DTYPE, f"cast_{k}") for k in big_names]
    conv_slotted = _cast_into_slot(conv_w[0], chip.reshape(1), F32, "slot_conv_w")

    small_shapes = {}

    def start_reduction(big_grads, small_grads):
        small_shapes.update({k: t.shape for k, t in small_grads.items()})
        bigs = [big_grads[0]] + [g.reshape(N_SHARDS, d // N_SHARDS, d) for g in big_grads[1:]]
        *landed, small_landed = _exchange_halves(bigs, _pack_small(small_grads))
        place = jnp.concatenate([core, chip.reshape(1)])
        sums = [_add_half(g, l, place, f"add_half_{a}") for a, (g, l) in enumerate(zip(bigs, landed))]
        return [s[0] for s in sums], [s[1] for s in sums], _sum_slots(small_landed, "sum_small")

    loss_part, grad_x, d_norm_g, (*by_chip, small_all) = _local_step(
        x, loss_target, None, None, None, None, None,
        b_merge, conv_b, rg_wx[0], rg_bx.reshape(1, d), rg_wa[0], rg_ba.reshape(1, d), rg_lambda, hg_lb_logits,
        hg_norm_g, norm_g, final_norm_g.reshape(1, d), gather=(slotted, conv_slotted, chip.reshape(1)),
        start_reduction=start_reduction)
    mine = [_sum_slots(s, f"sum_chips_{a}") for a, s in enumerate(by_chip)]
    late = jnp.concatenate([d_norm_g.reshape(SUBLANES, LANES), jnp.full((SUBLANES, LANES), loss_part, F32)])
    *theirs, late_parts = _swap_halves(mine, late)
    late_sum = _sum_slots(late_parts, "sum_late")
    loss = late_sum[SUBLANES, 0]
    small_red = _unpack_small(small_all, {k: jax.ShapeDtypeStruct(s, F32) for k, s in small_shapes.items()})
    small_red["norm_g"] = late_sum[:SUBLANES].reshape(1, d)

    grads, delta, new_m, new_v = {}, {}, {}, {}
    for k, g_mine, g_theirs in zip(big_names, mine, theirs):
        out = _adamw_halves(weights[k][0], g_mine, g_theirs, m[k][0], v[k][0], core, f"adamw_{k}")
        grads[k], delta[k], new_m[k], new_v[k] = (t.reshape(weights[k].shape) for t in out)
    cols = d // N_SHARDS
    g_conv = lax.dynamic_slice(small_red["conv_w"], (0, chip * cols), (CONV_WIDTH, cols))
    grads["conv_w"] = g_conv.reshape(conv_w.shape)
    dl, nm, nv = _adamw(conv_w[0], g_conv, m_conv_w[0], v_conv_w[0], "adamw_conv_w")
    delta["conv_w"], new_m["conv_w"], new_v["conv_w"] = (t.reshape(conv_w.shape) for t in (dl, nm, nv))
    rest = [k for k in _SMALL_ORDER if k != "conv_w"]
    like = {k: (weights[k] if k != "conv_w" else jnp.zeros((CONV_WIDTH, d), F32)) for k in _SMALL_ORDER}
    packs = [_pack_small({k: (t[k] if k != "conv_w" else like[k]) for k in _SMALL_ORDER}) for t in (weights, m, v)]
    g_pack = _pack_small({k: small_red[k].reshape(like[k].shape) for k in _SMALL_ORDER})
    outs = [_unpack_small(p, like) for p in _adamw(packs[0], g_pack, packs[1], packs[2], "adamw_small")]
    for k in rest:
        grads[k] = small_red[k].reshape(weights[k].shape)
        delta[k], new_m[k], new_v[k] = outs[0][k], outs[1][k], outs[2][k]

    order = ("w_in", "b_merge", "conv_w", "conv_b", "rg_wx", "rg_bx", "rg_wa", "rg_ba", "rg_lambda", "hg_lb_logits",
             "hg_norm_g", "proj_a", "proj_b", "w_out", "norm_g", "final_norm_g")
    return (loss, grad_x, *[grads[k] for k in order], *[delta[k] for k in order], *[new_m[k] for k in order],
            *[new_v[k] for k in order])
```

```python
import functools

import jax
import jax.numpy as jnp
from jax import lax
from jax.experimental import pallas as pl
from jax.experimental.pallas import tpu as pltpu

F32 = jnp.float32
_MXU_DTYPE = jnp.bfloat16

D_MODEL = 1024
LANES = 128
SUBLANES = 8
N_BLK = D_MODEL // LANES
N_GROUPS = 8
N_SHARDS = 4
CONV_WIDTH = 4
LRU_C = 8.0
CHUNK = 64
CHUNKS_IN_FLIGHT = 16
HG_SCALE = float(LANES) ** -0.5
EPS = 1e-6
ADAM_LR, ADAM_B1, ADAM_B2, ADAM_EPS, ADAM_WD, ADAM_STEP = 0.001, 0.9, 0.999, 1e-08, 0.01, 10
VMEM_LIMIT = 56 * 1024 * 1024
MESH = pl.DeviceIdType.MESH

_SLOT_TO_GROUP = (2, 3, 4, 5, 0, 1, 6, 7)


def _group_of_slot(s):
    return jnp.where(s < 4, s + 2, jnp.where(s < 6, s - 4, s))


def _mm(a, b):
    return lax.dot_general(a.astype(_MXU_DTYPE), b.astype(_MXU_DTYPE), (((1,), (0,)), ((), ())),
                           preferred_element_type=F32)


def _mm_nt(a, b):
    return lax.dot_general(a.astype(_MXU_DTYPE), b.astype(_MXU_DTYPE), (((1,), (1,)), ((), ())),
                           preferred_element_type=F32)


def _mm_tn(a, b):
    return lax.dot_general(a.astype(_MXU_DTYPE), b.astype(_MXU_DTYPE), (((0,), (0,)), ((), ())),
                           preferred_element_type=F32)


def _sigmoid(x):
    return 0.5 * jnp.tanh(0.5 * x) + 0.5


def _log1p_pos(y):
    series = y * (1.0 - y * (0.5 - y * (1.0 / 3.0 - y * 0.25)))
    return jnp.where(y < 0.01, series, jnp.log(1.0 + y))


def _softplus(x):
    return jnp.maximum(x, 0.0) + _log1p_pos(jnp.exp(-jnp.abs(x)))


def _shift_down(x, n):
    rolled = pltpu.roll(x, n, 0)
    edge = SUBLANES if (n < SUBLANES and x.shape[0] > SUBLANES) else x.shape[0]
    rows = lax.broadcasted_iota(jnp.int32, (edge, x.shape[1]), 0)
    head = jnp.where(rows >= n, rolled[:edge], 0.0)
    return head if edge == x.shape[0] else jnp.concatenate([head, rolled[edge:]], axis=0)


def _shift_up(x, n):
    size = x.shape[0]
    rolled = pltpu.roll(x, size - n, 0)
    edge = SUBLANES if (n < SUBLANES and size > SUBLANES) else size
    rows = lax.broadcasted_iota(jnp.int32, (edge, x.shape[1]), 0)
    tail = jnp.where(rows < edge - n, rolled[size - edge:], 0.0)
    return tail if edge == size else jnp.concatenate([rolled[:size - edge], tail], axis=0)


def _params(dims, vmem=VMEM_LIMIT):
    return pltpu.CompilerParams(dimension_semantics=dims, vmem_limit_bytes=vmem)


def _load_w_in_by_slot(w_hbm, w_res):
    for slot, g in enumerate(_SLOT_TO_GROUP):
        pltpu.sync_copy(w_hbm.at[g // 2, :, pl.ds((g % 2) * D_MODEL, D_MODEL)], w_res.at[slot])


def _inproj_fwd(x2d, norm_g, w_all):
    tokens, d = x2d.shape
    tm = min(512, tokens)

    def body(x_ref, g_ref, w_hbm, z_ref, ht_ref, h_scr, w_res):
        @pl.when((pl.program_id(0) == 0) & (pl.program_id(1) == 0))
        def _():
            _load_w_in_by_slot(w_hbm, w_res)

        @pl.when(pl.program_id(1) == 0)
        def _():
            x = x_ref[...]
            r = lax.rsqrt(jnp.mean(x * x, axis=-1, keepdims=True) + EPS)
            h = (x * r) * g_ref[...]
            h_scr[...] = h.astype(_MXU_DTYPE)
            ht_ref[...] = jnp.transpose(h).astype(_MXU_DTYPE)

        z_ref[...] = _mm(h_scr[...], w_res[pl.program_id(1)])

    return pl.pallas_call(
        body, name="inproj_fwd",
        grid=(tokens // tm, N_GROUPS),
        in_specs=[pl.BlockSpec((tm, d), lambda i, s: (i, 0)),
                  pl.BlockSpec((1, d), lambda i, s: (0, 0)),
                  pl.BlockSpec(memory_space=pl.ANY)],
        out_specs=[pl.BlockSpec((None, tm, D_MODEL), lambda i, s: (s, i, 0)),
                   pl.BlockSpec((d, tm), lambda i, s: (0, i))],
        out_shape=[jax.ShapeDtypeStruct((N_GROUPS, tokens, D_MODEL), F32),
                   jax.ShapeDtypeStruct((d, tokens), _MXU_DTYPE)],
        scratch_shapes=[pltpu.VMEM((tm, d), _MXU_DTYPE), pltpu.VMEM((N_GROUPS, d, D_MODEL), _MXU_DTYPE)],
        compiler_params=_params(("arbitrary", "arbitrary")),
    )(x2d, norm_g, w_all)


def _slot_of_group(g):
    return jnp.where(g < 2, g + 4, jnp.where(g < 6, g - 2, g))


def _inproj_fwd_gather(x2d, norm_g, slotted, conv_slotted, chip):
    tokens, d = x2d.shape
    tm = min(512, tokens)
    n_tiles = tokens // tm
    n_big = len(slotted)
    n_sem = 6 * n_big + 3
    last_pass = N_GROUPS - 1

    def shard_of(k, chip_id):
        x, y = chip_id // 2, chip_id % 2
        return 2 * jnp.where(k % 2 == 1, 1 - x, x) + jnp.where(k // 2 == 1, 1 - y, y)

    def body(chip_ref, x_ref, g_ref, *rest):
        bufs, cw = rest[n_big + 1:2 * n_big + 1], rest[2 * n_big + 1]
        z_ref, ht_ref = rest[2 * n_big + 2:2 * n_big + 4]
        h_all, slab, send_sems, recv_sems, slab_sem = rest[2 * n_big + 4:]
        del chip_ref
        p, i = pl.program_id(0), pl.program_id(1)
        x, y, c, chips = _mesh_position()
        me, sibling = 2 * x + y, (x, y, 1 - c)

        def half(a, slot, which):
            hs = bufs[a].shape[1] // 2
            return bufs[a].at[slot, pl.ds(which * hs, hs), :]

        def send(a, j):
            mine = half(a, me, c)
            return _remote(mine, mine, send_sems, recv_sems, 6 * a + j, (chips[j][0], chips[j][1], c))

        def arrival(a, j):
            landed = half(a, 2 * chips[j][0] + chips[j][1], c)
            return _remote(landed, landed, send_sems, recv_sems, 6 * a + j, (chips[j][0], chips[j][1], c))

        def passed_on(a, j, which):
            landed = half(a, 2 * chips[j][0] + chips[j][1], which)
            return _remote(landed, landed, send_sems, recv_sems, 6 * a + 3 + j, sibling)

        def conv_copy(j, slot):
            return _remote(cw.at[slot], cw.at[slot], send_sems, recv_sems, 6 * n_big + j, (chips[j][0], chips[j][1], c))

        def land(a, j):
            arrival(a, j).wait_recv()
            passed_on(a, j, c).start()
            passed_on(a, j, 1 - c).wait_recv()

        @pl.when((p == 0) & (i == 0))
        def _():
            for j in range(3):
                send(0, j).start()

        for j in range(3):
            @pl.when((p == 2 * (j + 1)) & (i == 0))
            def _(j=j):
                land(0, j)
                if j == 0:
                    for a in range(1, n_big):
                        for jj in range(3):
                            send(a, jj).start()
                    for jj in range(3):
                        conv_copy(jj, me).start()

        @pl.when(i == 0)
        def _():
            shard = shard_of(p // 2, me)
            for which in range(2):
                @pl.when(p % 2 == which)
                def _(which=which):
                    cp = pltpu.make_async_copy(bufs[0].at[shard, :, pl.ds(which * D_MODEL, D_MODEL)], slab, slab_sem)
                    cp.start()
                    cp.wait()

        rows = pl.ds(pl.multiple_of(i * tm, tm), tm)

        @pl.when(p == 0)
        def _():
            xt = x_ref[...]
            r = lax.rsqrt(jnp.mean(xt * xt, axis=-1, keepdims=True) + EPS)
            h = (xt * r) * g_ref[...]
            h_all[rows, :] = h.astype(_MXU_DTYPE)
            ht_ref[...] = jnp.transpose(h).astype(_MXU_DTYPE)

        z_ref[...] = _mm(h_all[rows, :], slab[...])

        @pl.when((p == last_pass) & (i == n_tiles - 1))
        def _():
            for a in range(1, n_big):
                for j in range(3):
                    land(a, j)
            for j in range(3):
                conv_copy(j, 2 * chips[j][0] + chips[j][1]).wait_recv()
            for a in range(n_big):
                for j in range(3):
                    send(a, j).wait_send()
                    passed_on(a, j, c).wait_send()
            for j in range(3):
                conv_copy(j, me).wait_send()

    def z_index(p, i, chip_ref):
        g = 2 * shard_of(p // 2, chip_ref[0]) + p % 2
        return (_slot_of_group(g), i, 0)

    def first_pass_tile(p, i, chip_ref):
        return jnp.where(p == 0, i, n_tiles - 1)

    hbm = pl.BlockSpec(memory_space=pl.ANY)
    operands = list(slotted) + [conv_slotted]
    grid_spec = pltpu.PrefetchScalarGridSpec(
        num_scalar_prefetch=1, grid=(N_GROUPS, n_tiles),
        in_specs=[pl.BlockSpec((tm, d), lambda p, i, chip_ref: (first_pass_tile(p, i, chip_ref), 0)),
                  pl.BlockSpec((1, d), lambda p, i, chip_ref: (0, 0))] + [hbm] * (n_big + 1),
        out_specs=[hbm] * (n_big + 1) + [pl.BlockSpec((None, tm, D_MODEL), z_index),
                                         pl.BlockSpec((d, tm), lambda p, i, chip_ref: (0, first_pass_tile(p, i, chip_ref)))],
        scratch_shapes=[pltpu.VMEM((tokens, d), _MXU_DTYPE), pltpu.VMEM((d, D_MODEL), _MXU_DTYPE),
                        pltpu.SemaphoreType.DMA((n_sem,)), pltpu.SemaphoreType.DMA((n_sem,)), pltpu.SemaphoreType.DMA])
    out = pl.pallas_call(
        body, name="inproj_fwd_gather", grid_spec=grid_spec,
        out_shape=[jax.ShapeDtypeStruct(a.shape, a.dtype) for a in operands]
        + [jax.ShapeDtypeStruct((N_GROUPS, tokens, D_MODEL), F32), jax.ShapeDtypeStruct((d, tokens), _MXU_DTYPE)],
        input_output_aliases={3 + a: a for a in range(n_big + 1)},
        compiler_params=_params(("arbitrary", "arbitrary")),
    )(chip, x2d, norm_g, *operands)
    return out[n_big + 1], out[n_big + 2], out[:n_big], out[n_big]


def _lane_blocks(x):
    return [x[:, k * LANES:(k + 1) * LANES] for k in range(x.shape[1] // LANES)]


def _block_diag(x, w_ref, transposed=False):
    mm = _mm_nt if transposed else _mm
    return jnp.concatenate([mm(xk, w_ref[k]) for k, xk in enumerate(_lane_blocks(x))], axis=1)


def _lru_gates(xa, cw_ref, cb_ref, wx_ref, bx_ref, wa_ref, ba_ref, lam_ref):
    xc = (cb_ref[...] + cw_ref[3:4, :] * xa + cw_ref[2:3, :] * _shift_down(xa, 1)
          + cw_ref[1:2, :] * _shift_down(xa, 2) + cw_ref[0:1, :] * _shift_down(xa, 3))
    gi = _sigmoid(_block_diag(xc, wx_ref) + bx_ref[...])
    gr = _sigmoid(_block_diag(xc, wa_ref) + ba_ref[...])
    sp = _softplus(-lam_ref[...])
    log_a = (-LRU_C) * gr * sp
    a = jnp.exp(log_a)
    y = 2.0 * log_a
    mult_sq = jnp.where(y > -1e-3, -y * (1.0 + 0.5 * y), 1.0 - a * a)
    inv_mult = lax.rsqrt(jnp.maximum(mult_sq, 1e-37))
    return xc, gi, gr, sp, a, mult_sq * inv_mult, inv_mult


def _tile_rows(width):
    return lax.broadcasted_iota(jnp.int32, (SUBLANES, width), 0)


def _scan_forward(a_scr, u_scr, h_scr, seq):
    width = a_scr.shape[1]
    rows = _tile_rows(width)

    def tile(j, carry):
        sl = pl.ds(pl.multiple_of(j * SUBLANES, SUBLANES), SUBLANES)
        a = a_scr[sl, :]
        u = u_scr[sl, :]
        for d in (1, 2, 4):
            keep = rows >= d
            a_sh = jnp.where(keep, pltpu.roll(a, d, 0), 1.0)
            u_sh = jnp.where(keep, pltpu.roll(u, d, 0), 0.0)
            u = a * u_sh + u
            a = a * a_sh
        h = u + a * carry
        h_scr[sl, :] = h
        return jnp.broadcast_to(h[SUBLANES - 1:SUBLANES, :], (SUBLANES, width))

    lax.fori_loop(0, seq // SUBLANES, tile, jnp.zeros((SUBLANES, width), F32))


def _scan_backward(c_scr, d_scr, g_scr, seq):
    width = c_scr.shape[1]
    rows = _tile_rows(width)
    n_tiles = seq // SUBLANES

    def tile(jj, carry):
        j = n_tiles - 1 - jj
        sl = pl.ds(pl.multiple_of(j * SUBLANES, SUBLANES), SUBLANES)
        c = c_scr[sl, :]
        g = d_scr[sl, :]
        for d in (1, 2, 4):
            keep = rows < SUBLANES - d
            c_sh = jnp.where(keep, pltpu.roll(c, SUBLANES - d, 0), 1.0)
            g_sh = jnp.where(keep, pltpu.roll(g, SUBLANES - d, 0), 0.0)
            g = c * g_sh + g
            c = c * c_sh
        g = g + c * carry
        g_scr[sl, :] = g
        return jnp.broadcast_to(g[0:1, :], (SUBLANES, width))

    lax.fori_loop(0, n_tiles, tile, jnp.zeros((SUBLANES, width), F32))


LRU_BLOCKS_PER_STEP = 2
LRU_LANES = LRU_BLOCKS_PER_STEP * LANES
LRU_STEPS = N_BLK // LRU_BLOCKS_PER_STEP


def _lru_param_specs(cb_axis):
    def pick(*ids):
        return ids[cb_axis]

    vec = pl.BlockSpec((1, LRU_LANES), lambda *ids: (0, pick(*ids)))
    mat = pl.BlockSpec((LRU_BLOCKS_PER_STEP, LANES, LANES), lambda *ids: (pick(*ids), 0, 0))
    return [pl.BlockSpec((CONV_WIDTH, LRU_LANES), lambda *ids: (0, pick(*ids))), vec, mat, vec, mat, vec, vec]


def _branch_a_fwd(z, conv_w, conv_b, wx, bx, wa, ba, lam, batch, seq):
    tokens = batch * seq

    def body(z_ref, cw_ref, cb_ref, wx_ref, bx_ref, wa_ref, ba_ref, lam_ref, ya_ref, hl_ref, a_scr, u_scr):
        xa = z_ref[0]
        ga = z_ref[1]
        xc, gi, _, _, a, mult, _ = _lru_gates(xa, cw_ref, cb_ref, wx_ref, bx_ref, wa_ref, ba_ref, lam_ref)
        a_scr[...] = a
        u_scr[...] = mult * gi * xc
        _scan_forward(a_scr, u_scr, hl_ref, seq)
        ya_ref[...] = (hl_ref[...] * (ga * _sigmoid(ga))).astype(_MXU_DTYPE)

    blk = pl.BlockSpec((seq, LRU_LANES), lambda b, c: (b, c))
    return pl.pallas_call(
        body, name="branch_a_fwd",
        grid=(batch, LRU_STEPS),
        in_specs=[pl.BlockSpec((2, seq, LRU_LANES), lambda b, c: (2, b, c))] + _lru_param_specs(1),
        out_specs=[blk, blk],
        out_shape=[jax.ShapeDtypeStruct((tokens, D_MODEL), _MXU_DTYPE), jax.ShapeDtypeStruct((tokens, D_MODEL), F32)],
        scratch_shapes=[pltpu.VMEM((seq, LRU_LANES), F32), pltpu.VMEM((seq, LRU_LANES), F32)],
        compiler_params=_params(("parallel", "parallel")),
    )(z, conv_w, conv_b, wx, bx, wa, ba, lam)


def _branch_a_bwd(z, hl, dya, dz, conv_w, conv_b, wx, bx, wa, ba, lam, batch, seq):
    def body(z_ref, hl_ref, dya_ref, dz_in_ref, cw_ref, cb_ref, wx_ref, bx_ref, wa_ref, ba_ref, lam_ref,
             dz_ref, dcw_ref, dcb_ref, dwx_ref, dbx_ref, dwa_ref, dba_ref, dlam_ref, c_scr, d_scr, g_scr):
        del dz_in_ref
        xa = z_ref[0]
        ga = z_ref[1]
        hl = hl_ref[...]
        dya = dya_ref[...]
        xc, gi, gr, sp, a, mult, inv_mult = _lru_gates(xa, cw_ref, cb_ref, wx_ref, bx_ref, wa_ref, ba_ref, lam_ref)
        sga = _sigmoid(ga)
        dz_ref[1] = (dya * hl * (sga * (1.0 + ga * (1.0 - sga)))).astype(_MXU_DTYPE)
        c_scr[...] = _shift_up(a, 1)
        d_scr[...] = dya * (ga * sga)
        _scan_backward(c_scr, d_scr, g_scr, seq)
        g = g_scr[...]
        da = g * _shift_down(hl, 1)
        dmult = g * gi * xc
        dgi = g * mult * xc
        dxc = g * mult * gi
        dlog_a = da * a - dmult * (a * a) * inv_mult
        dgr = dlog_a * (-LRU_C) * sp
        dsp = jnp.sum(dlog_a * gr, axis=0, keepdims=True) * (-LRU_C)
        dlam = -dsp * _sigmoid(-lam_ref[...])
        dpi = dgi * gi * (1.0 - gi)
        dpr = dgr * gr * (1.0 - gr)
        dxc = dxc + _block_diag(dpi, wx_ref, transposed=True) + _block_diag(dpr, wa_ref, transposed=True)
        dwx = jnp.stack([_mm_tn(xk, dk) for xk, dk in zip(_lane_blocks(xc), _lane_blocks(dpi))])
        dwa = jnp.stack([_mm_tn(xk, dk) for xk, dk in zip(_lane_blocks(xc), _lane_blocks(dpr))])
        dbx = jnp.sum(dpi, axis=0, keepdims=True)
        dba = jnp.sum(dpr, axis=0, keepdims=True)
        ahead = [dxc if k == CONV_WIDTH - 1 else _shift_up(dxc, CONV_WIDTH - 1 - k) for k in range(CONV_WIDTH)]
        dxa = sum(cw_ref[k:k + 1, :] * ahead[k] for k in range(CONV_WIDTH))
        dz_ref[0] = dxa.astype(_MXU_DTYPE)
        dcb = jnp.sum(dxc, axis=0, keepdims=True)
        dcw = [jnp.sum(ahead[k] * xa, axis=0, keepdims=True) for k in range(CONV_WIDTH)]

        @pl.when(pl.program_id(1) == 0)
        def _():
            for k in range(CONV_WIDTH):
                dcw_ref[k:k + 1, :] = dcw[k]
            dcb_ref[...] = dcb
            dwx_ref[...] = dwx
            dbx_ref[...] = dbx
            dwa_ref[...] = dwa
            dba_ref[...] = dba
            dlam_ref[...] = dlam

        @pl.when(pl.program_id(1) != 0)
        def _():
            for k in range(CONV_WIDTH):
                dcw_ref[k:k + 1, :] += dcw[k]
            dcb_ref[...] += dcb
            dwx_ref[...] += dwx
            dbx_ref[...] += dbx
            dwa_ref[...] += dwa
            dba_ref[...] += dba
            dlam_ref[...] += dlam

    tokens = batch * seq
    blk = pl.BlockSpec((seq, LRU_LANES), lambda c, b: (b, c))
    vec = pl.BlockSpec((1, LRU_LANES), lambda c, b: (0, c))
    mat = pl.BlockSpec((LRU_BLOCKS_PER_STEP, LANES, LANES), lambda c, b: (c, 0, 0))
    vec_shape = jax.ShapeDtypeStruct((1, D_MODEL), F32)
    mat_shape = jax.ShapeDtypeStruct((N_BLK, LANES, LANES), F32)
    return pl.pallas_call(
        body, name="branch_a_bwd",
        grid=(LRU_STEPS, batch),
        in_specs=[pl.BlockSpec((2, seq, LRU_LANES), lambda c, b: (2, b, c)), blk, blk,
                  pl.BlockSpec(memory_space=pl.ANY)] + _lru_param_specs(0),
        out_specs=[pl.BlockSpec((2, seq, LRU_LANES), lambda c, b: (2, b, c)),
                   pl.BlockSpec((CONV_WIDTH, LRU_LANES), lambda c, b: (0, c)), vec, mat, vec, mat, vec, vec],
        out_shape=[jax.ShapeDtypeStruct((N_GROUPS, tokens, D_MODEL), _MXU_DTYPE),
                   jax.ShapeDtypeStruct((CONV_WIDTH, D_MODEL), F32), vec_shape, mat_shape, vec_shape, mat_shape,
                   vec_shape, vec_shape],
        scratch_shapes=[pltpu.VMEM((seq, LRU_LANES), F32)] * 3,
        input_output_aliases={3: 0},
        compiler_params=_params(("parallel", "arbitrary")),
    )(z, hl, dya, dz, conv_w, conv_b, wx, bx, wa, ba, lam)


def _chunk_masks(transposed=False):
    r = lax.broadcasted_iota(jnp.int32, (CHUNK, CHUNK), 0)
    c = lax.broadcasted_iota(jnp.int32, (CHUNK, CHUNK), 1)
    return r <= c if transposed else r >= c


def _row_blocks(seq, fn):
    block = min(256, seq)

    def trip(i, carry):
        fn(pl.ds(pl.multiple_of(i * block, block), block))
        return carry

    lax.fori_loop(0, seq // block, trip, 0)


def _hgrn_prepare(z_ref, lb_ref, f_scr, logf_scr, qh_scr, seq):
    lb = _sigmoid(lb_ref[0:1, :] - lb_ref[1:2, :])

    def block(rows):
        q = z_ref[0, rows, :]
        f = lb + (1.0 - lb) * _sigmoid(z_ref[1, rows, :])
        f_scr[rows, :] = f
        logf_scr[rows, :] = jnp.log(f)
        qh_scr[rows, :] = q * _sigmoid(q)

    _row_blocks(seq, block)
    return lb


def _cumsum_rows(x, reverse=False):
    shift = _shift_up if reverse else _shift_down
    d = 1
    while d < x.shape[0]:
        x = x + shift(x, d)
        d *= 2
    return x


def _lane_mean(x):
    return jnp.mean(x, axis=-1, keepdims=True)


def _token_contractions(lhs_scr, rhs_scr, out_ref, seq):
    rows_id = lax.broadcasted_iota(jnp.int32, (LANES, LANES), 0)

    def transposed(p):
        rows = pl.ds(pl.multiple_of(p * LANES, LANES), LANES)
        return jnp.transpose(lhs_scr[rows, :]).astype(_MXU_DTYPE), rhs_scr[rows, :]

    def contract(p, s):
        lhs_t, rhs = s
        return (_mm(lhs_t, jnp.where(rows_id < CHUNK, rhs, 0.0)), _mm(lhs_t, jnp.where(rows_id >= CHUNK, rhs, 0.0)))

    def store(p, out):
        out_ref[2 * p] = out[0]
        out_ref[2 * p + 1] = out[1]

    _independent_trips(seq // LANES, [transposed, contract], store)


def _chunk_rows(c):
    return pl.ds(pl.multiple_of(c * CHUNK, CHUNK), CHUNK)


def _chunk_terms(c, z_ref, f_scr, qh_scr, b_scr):
    rows = _chunk_rows(c)
    b = b_scr[rows, :]
    b_mid = b_scr[pl.ds(c * CHUNK + CHUNK // 2, 1), :]
    b_last = b_scr[pl.ds(c * CHUNK + CHUNK - 1, 1), :]
    qh = qh_scr[rows, :]
    k = 1.0 - f_scr[rows, :]
    v = z_ref[2, rows, :]
    e_q = jnp.exp(b - b_mid) * HG_SCALE
    e_k = jnp.exp(b_mid - b)
    e_qi = jnp.exp(b) * HG_SCALE
    e_ks = jnp.exp(b_last - b)
    decay = jnp.exp(b_last)
    return rows, qh, k, v, e_q, e_k, e_qi, e_ks, decay


def _independent_trips(n, stages, store, group=CHUNKS_IN_FLIGHT):
    stages = stages if isinstance(stages, (list, tuple)) else [stages]
    group = min(group, n)

    def trip(g, carry):
        ids = [g * group + i for i in range(group)]
        state = [stages[0](c) for c in ids]
        for stage in stages[1:]:
            state = [stage(c, s) for c, s in zip(ids, state)]
        for c, s in zip(ids, state):
            store(c, s)
        return carry

    lax.fori_loop(0, n // group, trip, 0)


def _branch_b_fwd(z, lb_logits, hg_g, batch, seq):
    tokens = batch * seq
    n_chunks = seq // CHUNK

    def body(z_ref, lb_ref, g_ref, yb_ref, st_ref, f_scr, logf_scr, qh_scr, b_scr, o_scr, qi_scr, ks_scr, dec_scr):
        _hgrn_prepare(z_ref, lb_ref, f_scr, logf_scr, qh_scr, seq)
        causal = _chunk_masks()
        gain = g_ref[...]

        def cumulate(c):
            return _cumsum_rows(logf_scr[_chunk_rows(c), :])

        def store_cumulated(c, b):
            b_scr[_chunk_rows(c), :] = b

        def scores(c):
            _, qh, k, v, e_q, e_k, e_qi, e_ks, decay = _chunk_terms(c, z_ref, f_scr, qh_scr, b_scr)
            return _mm_nt(qh * e_q, k * e_k), v, qh * e_qi, k * e_ks, decay

        def within_chunk(c, s):
            att, v, q_int, k_st, decay = s
            return _mm(jnp.where(causal, att, 0.0), v), q_int, k_st, decay

        def store_within_chunk(c, out):
            rows = _chunk_rows(c)
            o_scr[rows, :], qi_scr[rows, :], ks_scr[rows, :], dec_scr[pl.ds(c, 1), :] = out

        def carry_state(c, state_t):
            update = st_ref[c]
            st_ref[c] = state_t
            return state_t * dec_scr[pl.ds(c, 1), :] + update

        def finish(c):
            rows = _chunk_rows(c)
            o = o_scr[rows, :] + _mm_nt(qi_scr[rows, :], st_ref[c])
            r = lax.rsqrt(_lane_mean(o * o) + EPS)
            gb = z_ref[3, rows, :]
            return (((o * r) * gain) * (gb * _sigmoid(gb))).astype(_MXU_DTYPE)

        def store_finished(c, yb):
            yb_ref[_chunk_rows(c), :] = yb

        _independent_trips(n_chunks, cumulate, store_cumulated)
        _independent_trips(n_chunks, [scores, within_chunk], store_within_chunk)
        _token_contractions(z_ref.at[2], ks_scr, st_ref, seq)
        lax.fori_loop(0, n_chunks, carry_state, jnp.zeros((LANES, LANES), F32))
        _independent_trips(n_chunks, finish, store_finished)

    seq_buf = pltpu.VMEM((seq, LANES), F32)
    return pl.pallas_call(
        body, name="branch_b_fwd",
        grid=(batch, N_BLK),
        in_specs=[pl.BlockSpec((4, seq, LANES), lambda b, h: (0, b, h)),
                  pl.BlockSpec((2, LANES), lambda b, h: (0, h)),
                  pl.BlockSpec((1, LANES), lambda b, h: (0, 0))],
        out_specs=[pl.BlockSpec((seq, LANES), lambda b, h: (b, h)),
                   pl.BlockSpec((None, n_chunks, LANES, LANES), lambda b, h: (b * N_BLK + h, 0, 0, 0))],
        out_shape=[jax.ShapeDtypeStruct((tokens, D_MODEL), _MXU_DTYPE),
                   jax.ShapeDtypeStruct((batch * N_BLK, n_chunks, LANES, LANES), F32)],
        scratch_shapes=[seq_buf] * 7 + [pltpu.VMEM((n_chunks, LANES), F32)],
        compiler_params=_params(("parallel", "parallel")),
    )(z, lb_logits, hg_g)


def _branch_b_bwd(z, states, dyb, dz, lb_logits, hg_g, batch, seq):
    n_chunks = seq // CHUNK

    def body(z_ref, st_ref, dyb_ref, dz_in_ref, lb_ref, g_ref, dz_ref, dlog_ref, dg_ref,
             f_scr, logf_scr, qh_scr, b_scr, do_scr, qi_scr, dqh_scr, df_scr, dec_scr, dgp_scr, dlb_scr, dst_scr):
        del dz_in_ref
        first = (pl.program_id(0) == 0) & (pl.program_id(1) == 0)
        lb = _hgrn_prepare(z_ref, lb_ref, f_scr, logf_scr, qh_scr, seq)
        causal = _chunk_masks()
        anti_causal = _chunk_masks(transposed=True)
        gain = g_ref[...]

        @pl.when(first)
        def _():
            dg_ref[...] = jnp.zeros_like(dg_ref)

        @pl.when(pl.program_id(1) == 0)
        def _():
            dlb_scr[...] = jnp.zeros_like(dlb_scr)

        def cumulate(c):
            return _cumsum_rows(logf_scr[_chunk_rows(c), :])

        def store_cumulated(c, b):
            b_scr[_chunk_rows(c), :] = b

        def scores(c):
            _, qh, k, v, e_q, e_k, e_qi, e_ks, decay = _chunk_terms(c, z_ref, f_scr, qh_scr, b_scr)
            q_int = qh * e_qi
            return _mm_nt(qh * e_q, k * e_k), _mm_nt(q_int, st_ref[c]), v, q_int, decay

        def output_gradient(c, s):
            att, o_inter, v, q_int, decay = s
            rows = _chunk_rows(c)
            o = _mm(jnp.where(causal, att, 0.0), v) + o_inter
            r = lax.rsqrt(_lane_mean(o * o) + EPS)
            o_n = o * r
            gb = z_ref[3, rows, :]
            sgb = _sigmoid(gb)
            dyb_c = dyb_ref[rows, :]
            d_ong = dyb_c * (gb * sgb)
            d_gb = (dyb_c * (o_n * gain) * (sgb * (1.0 + gb * (1.0 - sgb)))).astype(_MXU_DTYPE)
            d_gain = jnp.sum(d_ong * o_n, axis=0, keepdims=True)
            d_on = d_ong * gain
            return d_gb, d_gain, r * (d_on - o_n * _lane_mean(d_on * o_n)), q_int, decay

        def store_output_gradient(c, out):
            rows = _chunk_rows(c)
            dz_ref[3, rows, :], dgp_scr[pl.ds(c, 1), :], do_scr[rows, :], qi_scr[rows, :], dec_scr[pl.ds(c, 1), :] = out

        def carry_state_gradient(cc, d_state_t):
            c = n_chunks - 1 - cc
            update = dst_scr[c]
            dst_scr[c] = d_state_t
            return d_state_t * dec_scr[pl.ds(c, 1), :] + update

        def score_gradients(c):
            rows, qh, k, v, e_q, e_k, e_qi, e_ks, decay = _chunk_terms(c, z_ref, f_scr, qh_scr, b_scr)
            state_t = st_ref[c]
            d_state_t = dst_scr[c]
            d_o = do_scr[rows, :]
            q_in, k_in, q_int, k_st = qh * e_q, k * e_k, qh * e_qi, k * e_ks
            first = (_mm_nt(k_in, q_in), _mm_nt(d_o, v), _mm_nt(v, d_o), _mm_nt(k_st, d_state_t), _mm(d_o, state_t),
                     _mm(v, d_state_t))
            d_decay = jnp.sum(state_t * d_state_t, axis=0, keepdims=True)
            return first, d_o, q_in, k_in, q_int, k_st, e_q, e_k, e_qi, e_ks, decay, d_decay

        def input_gradients(c, s):
            (att_t, d_att, d_att_t, dv_inter, dq_int, dk_st), d_o, q_in, k_in, q_int, k_st, e_q, e_k, e_qi, e_ks, decay, d_decay = s
            rows = _chunk_rows(c)
            d_v = _mm(jnp.where(anti_causal, att_t, 0.0), d_o) + dv_inter
            dq_in = _mm(jnp.where(causal, d_att, 0.0), k_in)
            dk_in = _mm(jnp.where(anti_causal, d_att_t, 0.0), q_in)
            d_k = dk_in * e_k + dk_st * e_ks
            kk = dk_st * k_st
            d_b = dq_in * q_in + dq_int * q_int - dk_in * k_in - kk
            d_b_last = jnp.sum(kk, axis=0, keepdims=True) + decay * d_decay
            d_logf = _cumsum_rows(d_b, reverse=True) + d_b_last
            return d_v.astype(_MXU_DTYPE), dq_in * e_q + dq_int * e_qi, d_logf / f_scr[rows, :] - d_k

        def store_input_gradients(c, out):
            rows = _chunk_rows(c)
            dz_ref[2, rows, :], dqh_scr[rows, :], df_scr[rows, :] = out

        def input_activations(rows):
            q = z_ref[0, rows, :]
            sq = _sigmoid(q)
            dz_ref[0, rows, :] = (dqh_scr[rows, :] * (sq * (1.0 + q * (1.0 - sq)))).astype(_MXU_DTYPE)
            sg = _sigmoid(z_ref[1, rows, :])
            d_f = df_scr[rows, :]
            dz_ref[1, rows, :] = (d_f * (1.0 - lb) * sg * (1.0 - sg)).astype(_MXU_DTYPE)
            dlb_scr[...] += jnp.sum(d_f * (1.0 - sg), axis=0, keepdims=True)

        _independent_trips(n_chunks, cumulate, store_cumulated)
        _independent_trips(n_chunks, [scores, output_gradient], store_output_gradient)
        _token_contractions(do_scr, qi_scr, dst_scr, seq)
        lax.fori_loop(0, n_chunks, carry_state_gradient, jnp.zeros((LANES, LANES), F32))
        _independent_trips(n_chunks, [score_gradients, input_gradients], store_input_gradients)
        dg_ref[...] += jnp.sum(dgp_scr[...], axis=0, keepdims=True)
        _row_blocks(seq, input_activations)
        d_l0 = dlb_scr[...] * lb * (1.0 - lb)
        dlog_ref[0:1, :] = d_l0
        dlog_ref[1:2, :] = -d_l0

    tokens = batch * seq
    seq_buf = pltpu.VMEM((seq, LANES), F32)
    chunk_rows = pltpu.VMEM((n_chunks, LANES), F32)
    return pl.pallas_call(
        body, name="branch_b_bwd",
        grid=(N_BLK, batch),
        in_specs=[pl.BlockSpec((4, seq, LANES), lambda h, b: (0, b, h)),
                  pl.BlockSpec((None, n_chunks, LANES, LANES), lambda h, b: (b * N_BLK + h, 0, 0, 0)),
                  pl.BlockSpec((seq, LANES), lambda h, b: (b, h)),
                  pl.BlockSpec(memory_space=pl.ANY),
                  pl.BlockSpec((2, LANES), lambda h, b: (0, h)),
                  pl.BlockSpec((1, LANES), lambda h, b: (0, 0))],
        out_specs=[pl.BlockSpec((4, seq, LANES), lambda h, b: (0, b, h)),
                   pl.BlockSpec((2, LANES), lambda h, b: (0, h)),
                   pl.BlockSpec((1, LANES), lambda h, b: (0, 0))],
        out_shape=[jax.ShapeDtypeStruct((N_GROUPS, tokens, D_MODEL), _MXU_DTYPE),
                   jax.ShapeDtypeStruct((2, D_MODEL), F32),
                   jax.ShapeDtypeStruct((1, LANES), F32)],
        scratch_shapes=[seq_buf] * 8 + [chunk_rows, chunk_rows, pltpu.VMEM((1, LANES), F32),
                                        pltpu.VMEM((n_chunks, LANES, LANES), F32)],
        input_output_aliases={3: 0},
        compiler_params=_params(("arbitrary", "arbitrary")),
    )(z, states, dyb, dz, lb_logits, hg_g)


def _merge_tail(ya, yb, z, x2d, tgt2d, b_merge, final_g, pa, pb, wo):
    tokens, d = x2d.shape
    tm = min(256, tokens)
    n_tiles = tokens // tm

    def body(ya_ref, yb_ref, z_ref, x_ref, t_ref, bm_ref, fg_ref, pa_hbm, pb_hbm, wo_hbm,
             dya_ref, dyb_ref, dx2_ref, dz_ref, loss_ref, dfg_ref, dbm_ref, dpa_hbm, dpb_hbm, dwo_hbm,
             pa_s, pb_s, wo_s, dpa_s, dpb_s, dwo_s):
        i = pl.program_id(0)

        @pl.when(i == 0)
        def _():
            pltpu.sync_copy(pa_hbm, pa_s)
            pltpu.sync_copy(pb_hbm, pb_s)
            pltpu.sync_copy(wo_hbm, wo_s)
            dpa_s[...] = jnp.zeros_like(dpa_s)
            dpb_s[...] = jnp.zeros_like(dpb_s)
            dwo_s[...] = jnp.zeros_like(dwo_s)
            loss_ref[...] = jnp.zeros_like(loss_ref)
            dfg_ref[...] = jnp.zeros_like(dfg_ref)
            dbm_ref[...] = jnp.zeros_like(dbm_ref)

        ya_t = ya_ref[...]
        yb_t = yb_ref[...]
        out_a = _mm(ya_t, pa_s[...])
        out_b = _mm(yb_t, pb_s[...])
        g_a = _sigmoid(z_ref[0] + bm_ref[:, :d])
        g_b = _sigmoid(z_ref[1] + bm_ref[:, d:])
        mixed = g_a * out_a + g_b * out_b
        x2 = x_ref[...] + _mm(mixed, wo_s[...])
        r = lax.rsqrt(jnp.mean(x2 * x2, axis=-1, keepdims=True) + EPS)
        xn = x2 * r
        fg = fg_ref[...]
        diff = xn * fg - t_ref[...]
        loss_ref[...] += jnp.sum(diff * diff) * (0.5 / d)
        dy = diff * (1.0 / d)
        dfg_ref[...] += jnp.sum(dy * xn, axis=0, keepdims=True)
        dxn = dy * fg
        dx2 = r * (dxn - xn * jnp.mean(dxn * xn, axis=-1, keepdims=True))
        dx2_ref[...] = dx2
        dmixed = _mm_nt(dx2, wo_s[...])
        dwo_s[...] += _mm_tn(mixed, dx2)
        dgm_a = dmixed * out_a * g_a * (1.0 - g_a)
        dgm_b = dmixed * out_b * g_b * (1.0 - g_b)
        dz_ref[0] = dgm_a.astype(_MXU_DTYPE)
        dz_ref[1] = dgm_b.astype(_MXU_DTYPE)
        dbm_ref[:, :d] += jnp.sum(dgm_a, axis=0, keepdims=True)
        dbm_ref[:, d:] += jnp.sum(dgm_b, axis=0, keepdims=True)
        dout_a = dmixed * g_a
        dout_b = dmixed * g_b
        dpa_s[...] += _mm_tn(ya_t, dout_a)
        dpb_s[...] += _mm_tn(yb_t, dout_b)
        dya_ref[...] = _mm_nt(dout_a, pa_s[...])
        dyb_ref[...] = _mm_nt(dout_b, pb_s[...])

        @pl.when(i == n_tiles - 1)
        def _():
            pltpu.sync_copy(dpa_s, dpa_hbm)
            pltpu.sync_copy(dpb_s, dpb_hbm)
            pltpu.sync_copy(dwo_s, dwo_hbm)

    tile = pl.BlockSpec((tm, d), lambda i: (i, 0))
    gm = pl.BlockSpec((2, tm, d), lambda i: (3, i, 0))
    row = lambda n: pl.BlockSpec((1, n), lambda i: (0, 0))
    hbm = pl.BlockSpec(memory_space=pl.ANY)
    act = jax.ShapeDtypeStruct((tokens, d), F32)
    mat = jax.ShapeDtypeStruct((d, d), F32)
    return pl.pallas_call(
        body, name="merge_tail",
        grid=(n_tiles,),
        in_specs=[tile, tile, gm, tile, tile, row(2 * d), row(d), hbm, hbm, hbm],
        out_specs=[tile, tile, tile, gm, row(LANES), row(d), row(2 * d), hbm, hbm, hbm],
        out_shape=[act, act, act, jax.ShapeDtypeStruct((N_GROUPS, tokens, d), _MXU_DTYPE),
                   jax.ShapeDtypeStruct((1, LANES), F32), jax.ShapeDtypeStruct((1, d), F32),
                   jax.ShapeDtypeStruct((1, 2 * d), F32), mat, mat, mat],
        scratch_shapes=[pltpu.VMEM((d, d), _MXU_DTYPE)] * 3 + [pltpu.VMEM((d, d), F32)] * 3,
        compiler_params=_params(("arbitrary",)),
    )(ya, yb, z, x2d, tgt2d, b_merge, final_g, pa, pb, wo)


def _inproj_dw(h_t, dz):
    d, tokens = h_t.shape
    tm = min(2048, tokens)

    def body(h_ref, dz_ref, dw_ref):
        part = _mm(h_ref[...], dz_ref[...])

        @pl.when(pl.program_id(1) == 0)
        def _():
            dw_ref[...] = part

        @pl.when(pl.program_id(1) != 0)
        def _():
            dw_ref[...] += part

    def out_index(s, i):
        g = _group_of_slot(s)
        return (g // 2, 0, g % 2)

    return pl.pallas_call(
        body, name="inproj_dw",
        grid=(N_GROUPS, tokens // tm),
        in_specs=[pl.BlockSpec((d, tm), lambda s, i: (0, i)),
                  pl.BlockSpec((None, tm, D_MODEL), lambda s, i: (s, i, 0))],
        out_specs=pl.BlockSpec((None, d, D_MODEL), out_index),
        out_shape=jax.ShapeDtypeStruct((N_SHARDS, d, 2 * D_MODEL), F32),
        compiler_params=_params(("parallel", "arbitrary")),
    )(h_t, dz)


def _inproj_dx(dz, w_all, x2d, dx2, norm_g, scatter=None):
    tokens, d = x2d.shape
    tm = min(512, tokens)
    n_tiles = tokens // tm
    n_big = len(scatter[0]) if scatter else 0

    def body(dz_ref, w_hbm, x_ref, dx2_ref, g_ref, *rest):
        if scatter:
            srcs, small_src = rest[:n_big], rest[2 * n_big]
            dx_ref, dg_ref = rest[2 * n_big + 1:2 * n_big + 3]
            outs, small_out = rest[2 * n_big + 3:3 * n_big + 3], rest[3 * n_big + 3]
            acc, w_res, send_sems, recv_sems, local_sem = rest[3 * n_big + 4:]
            copies = _scatter_copies(srcs, small_src, outs, small_out, send_sems, recv_sems, local_sem)
        else:
            dx_ref, dg_ref, acc, w_res = rest
            copies = []
        s = pl.program_id(1)

        @pl.when((pl.program_id(0) == 0) & (s == 0))
        def _():
            for cp in copies:
                cp.start()
            _load_w_in_by_slot(w_hbm, w_res)

        part = _mm_nt(dz_ref[...], w_res[s])

        @pl.when(s == 0)
        def _():
            acc[...] = part

        @pl.when(s != 0)
        def _():
            acc[...] += part

        @pl.when((pl.program_id(0) == 0) & (s == 0))
        def _():
            dg_ref[...] = jnp.zeros_like(dg_ref)

        @pl.when(s == N_GROUPS - 1)
        def _():
            x = x_ref[...]
            r = lax.rsqrt(jnp.mean(x * x, axis=-1, keepdims=True) + EPS)
            xn = x * r
            dh = acc[...]
            dg_ref[...] += jnp.sum(dh * xn, axis=0, keepdims=True)
            dxn = dh * g_ref[...]
            dx_ref[...] = r * (dxn - xn * jnp.mean(dxn * xn, axis=-1, keepdims=True)) + dx2_ref[...]

        @pl.when((pl.program_id(0) == n_tiles - 1) & (s == N_GROUPS - 1))
        def _():
            for cp in copies:
                cp.wait()

    tile = pl.BlockSpec((tm, d), lambda i, s: (i, 0))
    hbm = pl.BlockSpec(memory_space=pl.ANY)
    in_specs = [pl.BlockSpec((None, tm, D_MODEL), lambda i, s: (s, i, 0)), hbm, tile, tile,
                pl.BlockSpec((1, d), lambda i, s: (0, 0))]
    out_specs = [tile, pl.BlockSpec((1, d), lambda i, s: (0, 0))]
    out_shape = [jax.ShapeDtypeStruct((tokens, d), F32), jax.ShapeDtypeStruct((1, d), F32)]
    scratch = [pltpu.VMEM((tm, d), F32), pltpu.VMEM((N_GROUPS, d, D_MODEL), _MXU_DTYPE)]
    operands, aliases = [dz, w_all, x2d, dx2, norm_g], {}
    if scatter:
        bigs, by_chip, small_piece = scatter
        n_sem = 3 * n_big + N_DEV - 1
        in_specs += [hbm] * (2 * n_big + 1)
        out_specs += [hbm] * (n_big + 1)
        out_shape += [jax.ShapeDtypeStruct(g.shape, g.dtype) for g in by_chip]
        out_shape.append(jax.ShapeDtypeStruct((N_DEV, PIECE_ROWS, LANES), F32))
        scratch += [pltpu.SemaphoreType.DMA((n_sem,)), pltpu.SemaphoreType.DMA((n_sem,)), pltpu.SemaphoreType.DMA]
        operands += [*bigs, *by_chip, small_piece]
        aliases = {5 + n_big + a: 2 + a for a in range(n_big)}
    return pl.pallas_call(
        body, name="inproj_dx", grid=(n_tiles, N_GROUPS), in_specs=in_specs, out_specs=out_specs, out_shape=out_shape,
        scratch_shapes=scratch, input_output_aliases=aliases,
        compiler_params=_params(("arbitrary", "arbitrary")),
    )(*operands)


def _row_tile(rows, cols, itemsize=4, budget=2 * 1024 * 1024):
    tr = rows
    while tr * cols * itemsize > budget and tr % 16 == 0:
        tr //= 2
    return tr


def _cast_into_slot(a, chip, dtype, name):
    rows, cols = a.shape
    tr = _row_tile(rows, cols)

    def body(chip_ref, a_ref, o_ref):
        del chip_ref
        o_ref[...] = a_ref[...].astype(dtype)

    grid_spec = pltpu.PrefetchScalarGridSpec(
        num_scalar_prefetch=1, grid=(rows // tr,),
        in_specs=[pl.BlockSpec((tr, cols), lambda i, chip_ref: (i, 0))],
        out_specs=pl.BlockSpec((None, tr, cols), lambda i, chip_ref: (chip_ref[0], i, 0)))
    return pl.pallas_call(body, name=name, grid_spec=grid_spec,
                          out_shape=jax.ShapeDtypeStruct((N_SHARDS, rows, cols), dtype),
                          compiler_params=_params(("arbitrary",)))(chip, a)


def _sum_slots(stack, name):
    n, rows, cols = stack.shape
    tr = _row_tile(rows, cols * n)

    def body(s_ref, o_ref):
        total = s_ref[0].astype(F32)
        for k in range(1, n):
            total = total + s_ref[k].astype(F32)
        o_ref[...] = total

    return pl.pallas_call(body, name=name, grid=(rows // tr,),
                          in_specs=[pl.BlockSpec((n, tr, cols), lambda i: (0, i, 0))],
                          out_specs=pl.BlockSpec((tr, cols), lambda i: (i, 0)),
                          out_shape=jax.ShapeDtypeStruct((rows, cols), F32),
                          compiler_params=_params(("parallel",)))(stack)


def _add_half(full, landed, place, name):
    n, rows, cols = full.shape
    half = rows // 2
    tr = _row_tile(half, cols)
    nb = half // tr

    def body(place_ref, a_ref, b_ref, o_ref, own_ref):
        total = (a_ref[...] + b_ref[...]).astype(_MXU_DTYPE)
        o_ref[...] = total

        @pl.when(pl.program_id(1) == place_ref[1])
        def _():
            own_ref[...] = total

    grid_spec = pltpu.PrefetchScalarGridSpec(
        num_scalar_prefetch=1, grid=(nb, n),
        in_specs=[pl.BlockSpec((None, tr, cols), lambda i, j, place_ref: (j, place_ref[0] * nb + i, 0)),
                  pl.BlockSpec((None, tr, cols), lambda i, j, place_ref: (j, i, 0))],
        out_specs=[pl.BlockSpec((None, tr, cols), lambda i, j, place_ref: (j, i, 0)),
                   pl.BlockSpec((None, tr, cols), lambda i, j, place_ref: (place_ref[1], i, 0))])
    shape = jax.ShapeDtypeStruct((n, half, cols), _MXU_DTYPE)
    return pl.pallas_call(body, name=name, grid_spec=grid_spec, out_shape=[shape, shape],
                          compiler_params=_params(("parallel", "arbitrary")))(place, full, landed)


def _adamw_update(w, grad, m, v):
    c1 = 1.0 - ADAM_B1 ** ADAM_STEP
    c2 = 1.0 - ADAM_B2 ** ADAM_STEP
    nm = ADAM_B1 * m + (1.0 - ADAM_B1) * grad
    nv = ADAM_B2 * v + (1.0 - ADAM_B2) * (grad * grad)
    return (-ADAM_LR) * ((nm / c1) / (jnp.sqrt(nv / c2) + ADAM_EPS) + ADAM_WD * w), nm, nv


def _adamw(w, g, m, v, name):
    rows, cols = w.shape
    tr = _row_tile(rows, cols, budget=1024 * 1024)

    def body(w_ref, g_ref, m_ref, v_ref, d_ref, nm_ref, nv_ref):
        d_ref[...], nm_ref[...], nv_ref[...] = _adamw_update(w_ref[...], g_ref[...], m_ref[...], v_ref[...])

    spec = pl.BlockSpec((tr, cols), lambda i: (i, 0))
    shape = jax.ShapeDtypeStruct((rows, cols), F32)
    return pl.pallas_call(body, name=name, grid=(rows // tr,), in_specs=[spec] * 4, out_specs=[spec] * 3,
                          out_shape=[shape] * 3, compiler_params=_params(("parallel",)))(w, g, m, v)


def _adamw_halves(w, g_mine, g_sibling, m, v, core, name):
    rows, cols = w.shape
    half = rows // 2
    tr = _row_tile(half, cols, budget=1024 * 1024)
    nb = half // tr

    def body(core_ref, w_ref, gm_ref, gs_ref, m_ref, v_ref, g_ref, d_ref, nm_ref, nv_ref):
        mine = pl.program_id(0) // nb == core_ref[0]
        grad = jnp.where(mine, gm_ref[...], gs_ref[...])
        g_ref[...] = grad
        d_ref[...], nm_ref[...], nv_ref[...] = _adamw_update(w_ref[...], grad, m_ref[...], v_ref[...])

    spec = pl.BlockSpec((tr, cols), lambda i, core_ref: (i, 0))
    half_spec = pl.BlockSpec((tr, cols), lambda i, core_ref: (i % nb, 0))
    grid_spec = pltpu.PrefetchScalarGridSpec(num_scalar_prefetch=1, grid=(rows // tr,),
                                             in_specs=[spec, half_spec, half_spec, spec, spec], out_specs=[spec] * 4)
    shape = jax.ShapeDtypeStruct((rows, cols), F32)
    return pl.pallas_call(body, name=name, grid_spec=grid_spec, out_shape=[shape] * 4,
                          compiler_params=_params(("parallel",)))(core, w, g_mine, g_sibling, m, v)


def _local_step(x, loss_target, w_all, pa, pb, wo, conv_w, b_merge, conv_b, rg_wx, rg_bx, rg_wa, rg_ba,
                rg_lambda, hg_lb_logits, hg_norm_g, norm_g, final_norm_g, gather=None, start_reduction=None):
    batch, seq, d = x.shape
    x2d = x.reshape(batch * seq, d)
    tgt2d = loss_target.reshape(batch * seq, d)
    if gather is None:
        z, h_t = _inproj_fwd(x2d, norm_g, w_all)
    else:
        z, h_t, (w_all, pa, pb, wo), cw_all = _inproj_fwd_gather(x2d, norm_g, *gather)
        pa, pb, wo = (t.reshape(d, d) for t in (pa, pb, wo))
        conv_w = jnp.transpose(cw_all, (1, 0, 2)).reshape(CONV_WIDTH, d)
    lru = (conv_w, conv_b, rg_wx, rg_bx, rg_wa, rg_ba, rg_lambda)
    ya, hl = _branch_a_fwd(z, *lru, batch, seq)
    yb, states = _branch_b_fwd(z, hg_lb_logits, hg_norm_g, batch, seq)
    dya, dyb, dx2, dz, loss, d_final_g, d_b_merge, d_pa, d_pb, d_wo = _merge_tail(
        ya, yb, z, x2d, tgt2d, b_merge, final_norm_g, pa, pb, wo)
    dz, d_lb_logits, d_hg_g = _branch_b_bwd(z, states, dyb, dz, hg_lb_logits, hg_norm_g, batch, seq)
    dz, d_conv_w, d_conv_b, d_wx, d_bx, d_wa, d_ba, d_lam = _branch_a_bwd(z, hl, dya, dz, *lru, batch, seq)
    d_w_in = _inproj_dw(h_t, dz)
    big = (d_w_in, d_pa, d_pb, d_wo)
    small = dict(b_merge=d_b_merge, conv_w=d_conv_w, conv_b=d_conv_b, rg_wx=d_wx, rg_bx=d_bx, rg_wa=d_wa,
                 rg_ba=d_ba, rg_lambda=d_lam, hg_lb_logits=d_lb_logits, hg_norm_g=d_hg_g,
                 norm_g=jnp.zeros((1, d), F32), final_norm_g=d_final_g)
    if start_reduction is None:
        grad_x, small["norm_g"] = _inproj_dx(dz, w_all, x2d, dx2, norm_g)
        return loss[0, 0], grad_x.reshape(batch, seq, d), big, small
    grad_x, d_norm_g, *scattered = _inproj_dx(dz, w_all, x2d, dx2, norm_g, scatter=start_reduction(big, small))
    return loss[0, 0], grad_x.reshape(batch, seq, d), d_norm_g, scattered


_SMALL_ORDER = ("b_merge", "conv_w", "conv_b", "rg_wx", "rg_bx", "rg_wa", "rg_ba", "rg_lambda", "hg_lb_logits",
                "hg_norm_g", "norm_g", "final_norm_g")
N_DEV = 8
PIECE_ROWS = 272


def _pack_small(tree):
    flat = jnp.concatenate([tree[k].reshape(-1) for k in _SMALL_ORDER])
    flat = jnp.pad(flat, (0, N_DEV * PIECE_ROWS * LANES - flat.shape[0]))
    return flat.reshape(N_DEV * PIECE_ROWS, LANES)


def _unpack_small(packed, like):
    flat = packed.reshape(-1)
    out, pos = {}, 0
    for k in _SMALL_ORDER:
        n = like[k].size
        out[k] = flat[pos:pos + n].reshape(like[k].shape)
        pos += n
    return out


def _mesh_position():
    x, y, c = lax.axis_index("x"), lax.axis_index("y"), lax.axis_index("c")
    other_chips = [(1 - x, y), (x, 1 - y), (1 - x, 1 - y)]
    return x, y, c, other_chips


def _other_devices(x, y, c):
    flips = [(fx, fy, fc) for fx in (0, 1) for fy in (0, 1) for fc in (0, 1) if (fx, fy, fc) != (0, 0, 0)]
    return [(jnp.where(fx, 1 - x, x), jnp.where(fy, 1 - y, y), jnp.where(fc, 1 - c, c)) for fx, fy, fc in flips]


def _remote(src, dst, send_sems, recv_sems, k, device):
    return pltpu.make_async_remote_copy(src_ref=src, dst_ref=dst, send_sem=send_sems.at[k], recv_sem=recv_sems.at[k],
                                        device_id=device, device_id_type=MESH)


def _exchange_halves(bigs, small):
    n_big = len(bigs)
    n_sem = n_big + N_DEV - 1

    def body(*refs):
        srcs, small_src = refs[:n_big], refs[n_big]
        outs, small_out = refs[n_big + 1:2 * n_big + 1], refs[2 * n_big + 1]
        send_sems, recv_sems, local_sem = refs[2 * n_big + 2:]
        x, y, c, _ = _mesh_position()
        me, sibling = 4 * x + 2 * y + c, (x, y, 1 - c)
        mine = pltpu.make_async_copy(small_src.at[pl.ds(me * PIECE_ROWS, PIECE_ROWS), :], small_out.at[me], local_sem)
        mine.start()
        copies = []
        for a in range(n_big):
            hs = srcs[a].shape[1] // 2
            copies.append(_remote(srcs[a].at[:, pl.ds((1 - c) * hs, hs), :], outs[a], send_sems, recv_sems, a, sibling))
        for k, (px, py, pc) in enumerate(_other_devices(x, y, c)):
            piece = small_src.at[pl.ds((4 * px + 2 * py + pc) * PIECE_ROWS, PIECE_ROWS), :]
            copies.append(_remote(piece, small_out.at[me], send_sems, recv_sems, n_big + k, (px, py, pc)))
        for cp in copies:
            cp.start()
        for cp in copies:
            cp.wait()
        mine.wait()

    hbm = pl.BlockSpec(memory_space=pl.ANY)
    out_shape = [jax.ShapeDtypeStruct((g.shape[0], g.shape[1] // 2, g.shape[2]), F32) for g in bigs]
    out_shape.append(jax.ShapeDtypeStruct((N_DEV, PIECE_ROWS, LANES), F32))
    return pl.pallas_call(
        body, name="exchange_halves",
        in_specs=[hbm] * (n_big + 1), out_specs=[hbm] * (n_big + 1), out_shape=out_shape,
        scratch_shapes=[pltpu.SemaphoreType.DMA((n_sem,)), pltpu.SemaphoreType.DMA((n_sem,)), pltpu.SemaphoreType.DMA],
    )(*bigs, small)


def _scatter_copies(srcs, small_src, outs, small_out, send_sems, recv_sems, local_sem):
    n_big = len(srcs)
    x, y, c, chips = _mesh_position()
    chip, me = 2 * x + y, 4 * x + 2 * y + c
    copies = [pltpu.make_async_copy(small_src, small_out.at[me], local_sem)]
    for a in range(n_big):
        for j, (cx, cy) in enumerate(chips):
            copies.append(_remote(srcs[a].at[2 * cx + cy], outs[a].at[chip], send_sems, recv_sems, 3 * a + j, (cx, cy, c)))
    for k, peer in enumerate(_other_devices(x, y, c)):
        copies.append(_remote(small_src, small_out.at[me], send_sems, recv_sems, 3 * n_big + k, peer))
    return copies


def _swap_halves(halves, vec):
    n_big = len(halves)

    def body(*refs):
        srcs, vec_src = refs[:n_big], refs[n_big]
        outs, vec_out = refs[n_big + 1:2 * n_big + 1], refs[2 * n_big + 1]
        send_sems, recv_sems, local_sem = refs[2 * n_big + 2:]
        x, y, c, _ = _mesh_position()
        me = 4 * x + 2 * y + c
        copies = [pltpu.make_async_copy(vec_src, vec_out.at[me], local_sem)]
        copies += [_remote(srcs[a], outs[a], send_sems, recv_sems, a, (x, y, 1 - c)) for a in range(n_big)]
        copies += [_remote(vec_src, vec_out.at[me], send_sems, recv_sems, n_big + k, peer)
                   for k, peer in enumerate(_other_devices(x, y, c))]
        for cp in copies:
            cp.start()
        for cp in copies:
            cp.wait()

    hbm = pl.BlockSpec(memory_space=pl.ANY)
    n_sem = n_big + N_DEV - 1
    return pl.pallas_call(
        body, name="swap_halves",
        in_specs=[hbm] * (n_big + 1), out_specs=[hbm] * (n_big + 1),
        out_shape=[jax.ShapeDtypeStruct(h.shape, F32) for h in halves] + [jax.ShapeDtypeStruct((N_DEV,) + vec.shape, F32)],
        scratch_shapes=[pltpu.SemaphoreType.DMA((n_sem,)), pltpu.SemaphoreType.DMA((n_sem,)), pltpu.SemaphoreType.DMA],
    )(*halves, vec)


def kernel(x, w_in, b_merge, conv_w, conv_b, rg_wx, rg_bx, rg_wa, rg_ba, rg_lambda, hg_lb_logits, hg_norm_g, proj_a, proj_b, w_out, norm_g, final_norm_g, loss_target, m_w_in, m_b_merge, m_conv_w, m_conv_b, m_rg_wx, m_rg_bx, m_rg_wa, m_rg_ba, m_rg_lambda, m_hg_lb_logits, m_hg_norm_g, m_proj_a, m_proj_b, m_w_out, m_norm_g, m_final_norm_g, v_w_in, v_b_merge, v_conv_w, v_conv_b, v_rg_wx, v_rg_bx, v_rg_wa, v_rg_ba, v_rg_lambda, v_hg_lb_logits, v_hg_norm_g, v_proj_a, v_proj_b, v_w_out, v_norm_g, v_final_norm_g):
    d = D_MODEL
    weights = dict(w_in=w_in, b_merge=b_merge, conv_w=conv_w, conv_b=conv_b, rg_wx=rg_wx, rg_bx=rg_bx, rg_wa=rg_wa,
                   rg_ba=rg_ba, rg_lambda=rg_lambda, hg_lb_logits=hg_lb_logits, hg_norm_g=hg_norm_g, proj_a=proj_a,
                   proj_b=proj_b, w_out=w_out, norm_g=norm_g, final_norm_g=final_norm_g)
    m = dict(w_in=m_w_in, b_merge=m_b_merge, conv_w=m_conv_w, conv_b=m_conv_b, rg_wx=m_rg_wx, rg_bx=m_rg_bx,
             rg_wa=m_rg_wa, rg_ba=m_rg_ba, rg_lambda=m_rg_lambda, hg_lb_logits=m_hg_lb_logits, hg_norm_g=m_hg_norm_g,
             proj_a=m_proj_a, proj_b=m_proj_b, w_out=m_w_out, norm_g=m_norm_g, final_norm_g=m_final_norm_g)
    v = dict(w_in=v_w_in, b_merge=v_b_merge, conv_w=v_conv_w, conv_b=v_conv_b, rg_wx=v_rg_wx, rg_bx=v_rg_bx,
             rg_wa=v_rg_wa, rg_ba=v_rg_ba, rg_lambda=v_rg_lambda, hg_lb_logits=v_hg_lb_logits, hg_norm_g=v_hg_norm_g,
             proj_a=v_proj_a, proj_b=v_proj_b, w_out=v_w_out, norm_g=v_norm_g, final_norm_g=v_final_norm_g)
    big_names = ("w_in", "proj_a", "proj_b", "w_out")

    core = lax.axis_index("c").astype(jnp.int32).reshape(1)
    chip = (2 * lax.axis_index("x") + lax.axis_index("y")).astype(jnp.int32)

    slotted = [_cast_into_slot(weights[k][0], chip.reshape(1), _MXU_DTYPE, f"cast_{k}") for k in big_names]
    conv_slotted = _cast_into_slot(conv_w[0], chip.reshape(1), F32, "slot_conv_w")

    small_shapes = {}

    def start_reduction(big_grads, small_grads):
        small_shapes.update({k: t.shape for k, t in small_grads.items()})
        bigs = [big_grads[0]] + [g.reshape(N_SHARDS, d // N_SHARDS, d) for g in big_grads[1:]]
        *landed, small_landed = _exchange_halves(bigs, _pack_small(small_grads))
        place = jnp.concatenate([core, chip.reshape(1)])
        sums = [_add_half(g, l, place, f"add_half_{a}") for a, (g, l) in enumerate(zip(bigs, landed))]
        return [s[0] for s in sums], [s[1] for s in sums], _sum_slots(small_landed, "sum_small")

    loss_part, grad_x, d_norm_g, (*by_chip, small_all) = _local_step(
        x, loss_target, None, None, None, None, None,
        b_merge, conv_b, rg_wx[0], rg_bx.reshape(1, d), rg_wa[0], rg_ba.reshape(1, d), rg_lambda, hg_lb_logits,
        hg_norm_g, norm_g, final_norm_g.reshape(1, d), gather=(slotted, conv_slotted, chip.reshape(1)),
        start_reduction=start_reduction)
    mine = [_sum_slots(s, f"sum_chips_{a}") for a, s in enumerate(by_chip)]
    late = jnp.concatenate([d_norm_g.reshape(SUBLANES, LANES), jnp.full((SUBLANES, LANES), loss_part, F32)])
    *theirs, late_parts = _swap_halves(mine, late)
    late_sum = _sum_slots(late_parts, "sum_late")
    loss = late_sum[SUBLANES, 0]
    small_red = _unpack_small(small_all, {k: jax.ShapeDtypeStruct(s, F32) for k, s in small_shapes.items()})
    small_red["norm_g"] = late_sum[:SUBLANES].reshape(1, d)

    grads, delta, new_m, new_v = {}, {}, {}, {}
    for k, g_mine, g_theirs in zip(big_names, mine, theirs):
        out = _adamw_halves(weights[k][0], g_mine, g_theirs, m[k][0], v[k][0], core, f"adamw_{k}")
        grads[k], delta[k], new_m[k], new_v[k] = (t.reshape(weights[k].shape) for t in out)
    cols = d // N_SHARDS
    g_conv = lax.dynamic_slice(small_red["conv_w"], (0, chip * cols), (CONV_WIDTH, cols))
    grads["conv_w"] = g_conv.reshape(conv_w.shape)
    dl, nm, nv = _adamw(conv_w[0], g_conv, m_conv_w[0], v_conv_w[0], "adamw_conv_w")
    delta["conv_w"], new_m["conv_w"], new_v["conv_w"] = (t.reshape(conv_w.shape) for t in (dl, nm, nv))
    rest = [k for k in _SMALL_ORDER if k != "conv_w"]
    like = {k: (weights[k] if k != "conv_w" else jnp.zeros((CONV_WIDTH, d), F32)) for k in _SMALL_ORDER}
    packs = [_pack_small({k: (t[k] if k != "conv_w" else like[k]) for k in _SMALL_ORDER}) for t in (weights, m, v)]
    g_pack = _pack_small({k: small_red[k].reshape(like[k].shape) for k in _SMALL_ORDER})
    outs = [_unpack_small(p, like) for p in _adamw(packs[0], g_pack, packs[1], packs[2], "adamw_small")]
    for k in rest:
        grads[k] = small_red[k].reshape(weights[k].shape)
        delta[k], new_m[k], new_v[k] = outs[0][k], outs[1][k], outs[2][k]

    order = ("w_in", "b_merge", "conv_w", "conv_b", "rg_wx", "rg_bx", "rg_wa", "rg_ba", "rg_lambda", "hg_lb_logits",
             "hg_norm_g", "proj_a", "proj_b", "w_out", "norm_g", "final_norm_g")
    return (loss, grad_x, *[grads[k] for k in order], *[delta[k] for k in order], *[new_m[k] for k in order],
            *[new_v[k] for k in order])
```

```python
import functools

import jax
import jax.numpy as jnp
from jax import lax
from jax.experimental import pallas as pl
from jax.experimental.pallas import tpu as pltpu

F32 = jnp.float32
_MXU_DTYPE = jnp.bfloat16

D_MODEL = 1024
LANES = 128
SUBLANES = 8
N_BLK = D_MODEL // LANES
N_GROUPS = 8
N_SHARDS = 4
CONV_WIDTH = 4
LRU_C = 8.0
CHUNK = 64
CHUNKS_IN_FLIGHT = 16
HG_SCALE = float(LANES) ** -0.5
EPS = 1e-6
ADAM_LR, ADAM_B1, ADAM_B2, ADAM_EPS, ADAM_WD, ADAM_STEP = 0.001, 0.9, 0.999, 1e-08, 0.01, 10
VMEM_LIMIT = 56 * 1024 * 1024
MESH = pl.DeviceIdType.MESH

_SLOT_TO_GROUP = (2, 3, 4, 5, 0, 1, 6, 7)


def _group_of_slot(s):
    return jnp.where(s < 4, s + 2, jnp.where(s < 6, s - 4, s))


def _mm(a, b):
    return lax.dot_general(a.astype(_MXU_DTYPE), b.astype(_MXU_DTYPE), (((1,), (0,)), ((), ())),
                           preferred_element_type=F32)


def _mm_nt(a, b):
    return lax.dot_general(a.astype(_MXU_DTYPE), b.astype(_MXU_DTYPE), (((1,), (1,)), ((), ())),
                           preferred_element_type=F32)


def _mm_tn(a, b):
    return lax.dot_general(a.astype(_MXU_DTYPE), b.astype(_MXU_DTYPE), (((0,), (0,)), ((), ())),
                           preferred_element_type=F32)


def _sigmoid(x):
    return 0.5 * jnp.tanh(0.5 * x) + 0.5


def _log1p_pos(y):
    series = y * (1.0 - y * (0.5 - y * (1.0 / 3.0 - y * 0.25)))
    return jnp.where(y < 0.01, series, jnp.log(1.0 + y))


def _softplus(x):
    return jnp.maximum(x, 0.0) + _log1p_pos(jnp.exp(-jnp.abs(x)))


def _shift_down(x, n):
    rolled = pltpu.roll(x, n, 0)
    edge = SUBLANES if (n < SUBLANES and x.shape[0] > SUBLANES) else x.shape[0]
    rows = lax.broadcasted_iota(jnp.int32, (edge, x.shape[1]), 0)
    head = jnp.where(rows >= n, rolled[:edge], 0.0)
    return head if edge == x.shape[0] else jnp.concatenate([head, rolled[edge:]], axis=0)


def _shift_up(x, n):
    size = x.shape[0]
    rolled = pltpu.roll(x, size - n, 0)
    edge = SUBLANES if (n < SUBLANES and size > SUBLANES) else size
    rows = lax.broadcasted_iota(jnp.int32, (edge, x.shape[1]), 0)
    tail = jnp.where(rows < edge - n, rolled[size - edge:], 0.0)
    return tail if edge == size else jnp.concatenate([rolled[:size - edge], tail], axis=0)


def _params(dims, vmem=VMEM_LIMIT):
    return pltpu.CompilerParams(dimension_semantics=dims, vmem_limit_bytes=vmem)


def _load_w_in_by_slot(w_hbm, w_res):
    for slot, g in enumerate(_SLOT_TO_GROUP):
        pltpu.sync_copy(w_hbm.at[g // 2, :, pl.ds((g % 2) * D_MODEL, D_MODEL)], w_res.at[slot])


def _inproj_fwd(x2d, norm_g, w_all):
    tokens, d = x2d.shape
    tm = min(512, tokens)

    def body(x_ref, g_ref, w_hbm, z_ref, ht_ref, h_scr, w_res):
        @pl.when((pl.program_id(0) == 0) & (pl.program_id(1) == 0))
        def _():
            _load_w_in_by_slot(w_hbm, w_res)

        @pl.when(pl.program_id(1) == 0)
        def _():
            x = x_ref[...]
            r = lax.rsqrt(jnp.mean(x * x, axis=-1, keepdims=True) + EPS)
            h = (x * r) * g_ref[...]
            h_scr[...] = h.astype(_MXU_DTYPE)
            ht_ref[...] = jnp.transpose(h).astype(_MXU_DTYPE)

        z_ref[...] = _mm(h_scr[...], w_res[pl.program_id(1)])

    return pl.pallas_call(
        body, name="inproj_fwd",
        grid=(tokens // tm, N_GROUPS),
        in_specs=[pl.BlockSpec((tm, d), lambda i, s: (i, 0)),
                  pl.BlockSpec((1, d), lambda i, s: (0, 0)),
                  pl.BlockSpec(memory_space=pl.ANY)],
        out_specs=[pl.BlockSpec((None, tm, D_MODEL), lambda i, s: (s, i, 0)),
                   pl.BlockSpec((d, tm), lambda i, s: (0, i))],
        out_shape=[jax.ShapeDtypeStruct((N_GROUPS, tokens, D_MODEL), F32),
                   jax.ShapeDtypeStruct((d, tokens), _MXU_DTYPE)],
        scratch_shapes=[pltpu.VMEM((tm, d), _MXU_DTYPE), pltpu.VMEM((N_GROUPS, d, D_MODEL), _MXU_DTYPE)],
        compiler_params=_params(("arbitrary", "arbitrary")),
    )(x2d, norm_g, w_all)


def _slot_of_group(g):
    return jnp.where(g < 2, g + 4, jnp.where(g < 6, g - 2, g))


def _inproj_fwd_gather(x2d, norm_g, slotted, conv_slotted, chip):
    tokens, d = x2d.shape
    tm = min(512, tokens)
    n_tiles = tokens // tm
    n_big = len(slotted)
    n_sem = 6 * n_big + 3
    last_pass = N_GROUPS - 1

    def shard_of(k, chip_id):
        x, y = chip_id // 2, chip_id % 2
        return 2 * jnp.where(k % 2 == 1, 1 - x, x) + jnp.where(k // 2 == 1, 1 - y, y)

    def body(chip_ref, x_ref, g_ref, *rest):
        bufs, cw = rest[n_big + 1:2 * n_big + 1], rest[2 * n_big + 1]
        z_ref, ht_ref = rest[2 * n_big + 2:2 * n_big + 4]
        h_all, slab, send_sems, recv_sems, slab_sem = rest[2 * n_big + 4:]
        del chip_ref
        p, i = pl.program_id(0), pl.program_id(1)
        x, y, c, chips = _mesh_position()
        me, sibling = 2 * x + y, (x, y, 1 - c)

        def half(a, slot, which):
            hs = bufs[a].shape[1] // 2
            return bufs[a].at[slot, pl.ds(which * hs, hs), :]

        def send(a, j):
            mine = half(a, me, c)
            return _remote(mine, mine, send_sems, recv_sems, 6 * a + j, (chips[j][0], chips[j][1], c))

        def arrival(a, j):
            landed = half(a, 2 * chips[j][0] + chips[j][1], c)
            return _remote(landed, landed, send_sems, recv_sems, 6 * a + j, (chips[j][0], chips[j][1], c))

        def passed_on(a, j, which):
            landed = half(a, 2 * chips[j][0] + chips[j][1], which)
            return _remote(landed, landed, send_sems, recv_sems, 6 * a + 3 + j, sibling)

        def conv_copy(j, slot):
            return _remote(cw.at[slot], cw.at[slot], send_sems, recv_sems, 6 * n_big + j, (chips[j][0], chips[j][1], c))

        def land(a, j):
            arrival(a, j).wait_recv()
            passed_on(a, j, c).start()
            passed_on(a, j, 1 - c).wait_recv()

        @pl.when((p == 0) & (i == 0))
        def _():
            send(0, 0).start()
            send(0, 1).start()

        for j in range(3):
            @pl.when((p == 2 * (j + 1)) & (i == 0))
            def _(j=j):
                land(0, j)
                if j == 0:
                    send(0, 2).start()
                if j == 1:
                    for a in range(1, n_big):
                        for jj in range(3):
                            send(a, jj).start()
                    for jj in range(3):
                        conv_copy(jj, me).start()

        @pl.when(i == 0)
        def _():
            shard = shard_of(p // 2, me)
            for which in range(2):
                @pl.when(p % 2 == which)
                def _(which=which):
                    cp = pltpu.make_async_copy(bufs[0].at[shard, :, pl.ds(which * D_MODEL, D_MODEL)], slab, slab_sem)
                    cp.start()
                    cp.wait()

        rows = pl.ds(pl.multiple_of(i * tm, tm), tm)

        @pl.when(p == 0)
        def _():
            xt = x_ref[...]
            r = lax.rsqrt(jnp.mean(xt * xt, axis=-1, keepdims=True) + EPS)
            h = (xt * r) * g_ref[...]
            h_all[rows, :] = h.astype(_MXU_DTYPE)
            ht_ref[...] = jnp.transpose(h).astype(_MXU_DTYPE)

        z_ref[...] = _mm(h_all[rows, :], slab[...])

        @pl.when((p == last_pass) & (i == n_tiles - 1))
        def _():
            for a in range(1, n_big):
                for j in range(3):
                    land(a, j)
            for j in range(3):
                conv_copy(j, 2 * chips[j][0] + chips[j][1]).wait_recv()
            for a in range(n_big):
                for j in range(3):
                    send(a, j).wait_send()
                    passed_on(a, j, c).wait_send()
            for j in range(3):
                conv_copy(j, me).wait_send()

    def z_index(p, i, chip_ref):
        g = 2 * shard_of(p // 2, chip_ref[0]) + p % 2
        return (_slot_of_group(g), i, 0)

    def first_pass_tile(p, i, chip_ref):
        return jnp.where(p == 0, i, n_tiles - 1)

    hbm = pl.BlockSpec(memory_space=pl.ANY)
    operands = list(slotted) + [conv_slotted]
    grid_spec = pltpu.PrefetchScalarGridSpec(
        num_scalar_prefetch=1, grid=(N_GROUPS, n_tiles),
        in_specs=[pl.BlockSpec((tm, d), lambda p, i, chip_ref: (first_pass_tile(p, i, chip_ref), 0)),
                  pl.BlockSpec((1, d), lambda p, i, chip_ref: (0, 0))] + [hbm] * (n_big + 1),
        out_specs=[hbm] * (n_big + 1) + [pl.BlockSpec((None, tm, D_MODEL), z_index),
                                         pl.BlockSpec((d, tm), lambda p, i, chip_ref: (0, first_pass_tile(p, i, chip_ref)))],
        scratch_shapes=[pltpu.VMEM((tokens, d), _MXU_DTYPE), pltpu.VMEM((d, D_MODEL), _MXU_DTYPE),
                        pltpu.SemaphoreType.DMA((n_sem,)), pltpu.SemaphoreType.DMA((n_sem,)), pltpu.SemaphoreType.DMA])
    out = pl.pallas_call(
        body, name="inproj_fwd_gather", grid_spec=grid_spec,
        out_shape=[jax.ShapeDtypeStruct(a.shape, a.dtype) for a in operands]
        + [jax.ShapeDtypeStruct((N_GROUPS, tokens, D_MODEL), F32), jax.ShapeDtypeStruct((d, tokens), _MXU_DTYPE)],
        input_output_aliases={3 + a: a for a in range(n_big + 1)},
        compiler_params=_params(("arbitrary", "arbitrary")),
    )(chip, x2d, norm_g, *operands)
    return out[n_big + 1], out[n_big + 2], out[:n_big], out[n_big]


def _lane_blocks(x):
    return [x[:, k * LANES:(k + 1) * LANES] for k in range(x.shape[1] // LANES)]


def _block_diag(x, w_ref, transposed=False):
    mm = _mm_nt if transposed else _mm
    return jnp.concatenate([mm(xk, w_ref[k]) for k, xk in enumerate(_lane_blocks(x))], axis=1)


def _lru_gates(xa, cw_ref, cb_ref, wx_ref, bx_ref, wa_ref, ba_ref, lam_ref):
    xc = (cb_ref[...] + cw_ref[3:4, :] * xa + cw_ref[2:3, :] * _shift_down(xa, 1)
          + cw_ref[1:2, :] * _shift_down(xa, 2) + cw_ref[0:1, :] * _shift_down(xa, 3))
    gi = _sigmoid(_block_diag(xc, wx_ref) + bx_ref[...])
    gr = _sigmoid(_block_diag(xc, wa_ref) + ba_ref[...])
    sp = _softplus(-lam_ref[...])
    log_a = (-LRU_C) * gr * sp
    a = jnp.exp(log_a)
    y = 2.0 * log_a
    mult_sq = jnp.where(y > -1e-3, -y * (1.0 + 0.5 * y), 1.0 - a * a)
    inv_mult = lax.rsqrt(jnp.maximum(mult_sq, 1e-37))
    return xc, gi, gr, sp, a, mult_sq * inv_mult, inv_mult


def _tile_rows(width):
    return lax.broadcasted_iota(jnp.int32, (SUBLANES, width), 0)


def _scan_forward(a_scr, u_scr, h_scr, seq):
    width = a_scr.shape[1]
    rows = _tile_rows(width)

    def tile(j, carry):
        sl = pl.ds(pl.multiple_of(j * SUBLANES, SUBLANES), SUBLANES)
        a = a_scr[sl, :]
        u = u_scr[sl, :]
        for d in (1, 2, 4):
            keep = rows >= d
            a_sh = jnp.where(keep, pltpu.roll(a, d, 0), 1.0)
            u_sh = jnp.where(keep, pltpu.roll(u, d, 0), 0.0)
            u = a * u_sh + u
            a = a * a_sh
        h = u + a * carry
        h_scr[sl, :] = h
        return jnp.broadcast_to(h[SUBLANES - 1:SUBLANES, :], (SUBLANES, width))

    lax.fori_loop(0, seq // SUBLANES, tile, jnp.zeros((SUBLANES, width), F32))


def _scan_backward(c_scr, d_scr, g_scr, seq):
    width = c_scr.shape[1]
    rows = _tile_rows(width)
    n_tiles = seq // SUBLANES

    def tile(jj, carry):
        j = n_tiles - 1 - jj
        sl = pl.ds(pl.multiple_of(j * SUBLANES, SUBLANES), SUBLANES)
        c = c_scr[sl, :]
        g = d_scr[sl, :]
        for d in (1, 2, 4):
            keep = rows < SUBLANES - d
            c_sh = jnp.where(keep, pltpu.roll(c, SUBLANES - d, 0), 1.0)
            g_sh = jnp.where(keep, pltpu.roll(g, SUBLANES - d, 0), 0.0)
            g = c * g_sh + g
            c = c * c_sh
        g = g + c * carry
        g_scr[sl, :] = g
        return jnp.broadcast_to(g[0:1, :], (SUBLANES, width))

    lax.fori_loop(0, n_tiles, tile, jnp.zeros((SUBLANES, width), F32))


LRU_BLOCKS_PER_STEP = 2
LRU_LANES = LRU_BLOCKS_PER_STEP * LANES
LRU_STEPS = N_BLK // LRU_BLOCKS_PER_STEP


def _lru_param_specs(cb_axis):
    def pick(*ids):
        return ids[cb_axis]

    vec = pl.BlockSpec((1, LRU_LANES), lambda *ids: (0, pick(*ids)))
    mat = pl.BlockSpec((LRU_BLOCKS_PER_STEP, LANES, LANES), lambda *ids: (pick(*ids), 0, 0))
    return [pl.BlockSpec((CONV_WIDTH, LRU_LANES), lambda *ids: (0, pick(*ids))), vec, mat, vec, mat, vec, vec]


def _branch_a_fwd(z, conv_w, conv_b, wx, bx, wa, ba, lam, batch, seq):
    tokens = batch * seq

    def body(z_ref, cw_ref, cb_ref, wx_ref, bx_ref, wa_ref, ba_ref, lam_ref, ya_ref, hl_ref, a_scr, u_scr):
        xa = z_ref[0]
        ga = z_ref[1]
        xc, gi, _, _, a, mult, _ = _lru_gates(xa, cw_ref, cb_ref, wx_ref, bx_ref, wa_ref, ba_ref, lam_ref)
        a_scr[...] = a
        u_scr[...] = mult * gi * xc
        _scan_forward(a_scr, u_scr, hl_ref, seq)
        ya_ref[...] = (hl_ref[...] * (ga * _sigmoid(ga))).astype(_MXU_DTYPE)

    blk = pl.BlockSpec((seq, LRU_LANES), lambda b, c: (b, c))
    return pl.pallas_call(
        body, name="branch_a_fwd",
        grid=(batch, LRU_STEPS),
        in_specs=[pl.BlockSpec((2, seq, LRU_LANES), lambda b, c: (2, b, c))] + _lru_param_specs(1),
        out_specs=[blk, blk],
        out_shape=[jax.ShapeDtypeStruct((tokens, D_MODEL), _MXU_DTYPE), jax.ShapeDtypeStruct((tokens, D_MODEL), F32)],
        scratch_shapes=[pltpu.VMEM((seq, LRU_LANES), F32), pltpu.VMEM((seq, LRU_LANES), F32)],
        compiler_params=_params(("parallel", "parallel")),
    )(z, conv_w, conv_b, wx, bx, wa, ba, lam)


def _branch_a_bwd(z, hl, dya, dz, conv_w, conv_b, wx, bx, wa, ba, lam, batch, seq):
    def body(z_ref, hl_ref, dya_ref, dz_in_ref, cw_ref, cb_ref, wx_ref, bx_ref, wa_ref, ba_ref, lam_ref,
             dz_ref, dcw_ref, dcb_ref, dwx_ref, dbx_ref, dwa_ref, dba_ref, dlam_ref, c_scr, d_scr, g_scr):
        del dz_in_ref
        xa = z_ref[0]
        ga = z_ref[1]
        hl = hl_ref[...]
        dya = dya_ref[...]
        xc, gi, gr, sp, a, mult, inv_mult = _lru_gates(xa, cw_ref, cb_ref, wx_ref, bx_ref, wa_ref, ba_ref, lam_ref)
        sga = _sigmoid(ga)
        dz_ref[1] = (dya * hl * (sga * (1.0 + ga * (1.0 - sga)))).astype(_MXU_DTYPE)
        c_scr[...] = _shift_up(a, 1)
        d_scr[...] = dya * (ga * sga)
        _scan_backward(c_scr, d_scr, g_scr, seq)
        g = g_scr[...]
        da = g * _shift_down(hl, 1)
        dmult = g * gi * xc
        dgi = g * mult * xc
        dxc = g * mult * gi
        dlog_a = da * a - dmult * (a * a) * inv_mult
        dgr = dlog_a * (-LRU_C) * sp
        dsp = jnp.sum(dlog_a * gr, axis=0, keepdims=True) * (-LRU_C)
        dlam = -dsp * _sigmoid(-lam_ref[...])
        dpi = dgi * gi * (1.0 - gi)
        dpr = dgr * gr * (1.0 - gr)
        dxc = dxc + _block_diag(dpi, wx_ref, transposed=True) + _block_diag(dpr, wa_ref, transposed=True)
        dwx = jnp.stack([_mm_tn(xk, dk) for xk, dk in zip(_lane_blocks(xc), _lane_blocks(dpi))])
        dwa = jnp.stack([_mm_tn(xk, dk) for xk, dk in zip(_lane_blocks(xc), _lane_blocks(dpr))])
        dbx = jnp.sum(dpi, axis=0, keepdims=True)
        dba = jnp.sum(dpr, axis=0, keepdims=True)
        ahead = [dxc if k == CONV_WIDTH - 1 else _shift_up(dxc, CONV_WIDTH - 1 - k) for k in range(CONV_WIDTH)]
        dxa = sum(cw_ref[k:k + 1, :] * ahead[k] for k in range(CONV_WIDTH))
        dz_ref[0] = dxa.astype(_MXU_DTYPE)
        dcb = jnp.sum(dxc, axis=0, keepdims=True)
        dcw = [jnp.sum(ahead[k] * xa, axis=0, keepdims=True) for k in range(CONV_WIDTH)]

        @pl.when(pl.program_id(1) == 0)
        def _():
            for k in range(CONV_WIDTH):
                dcw_ref[k:k + 1, :] = dcw[k]
            dcb_ref[...] = dcb
            dwx_ref[...] = dwx
            dbx_ref[...] = dbx
            dwa_ref[...] = dwa
            dba_ref[...] = dba
            dlam_ref[...] = dlam

        @pl.when(pl.program_id(1) != 0)
        def _():
            for k in range(CONV_WIDTH):
                dcw_ref[k:k + 1, :] += dcw[k]
            dcb_ref[...] += dcb
            dwx_ref[...] += dwx
            dbx_ref[...] += dbx
            dwa_ref[...] += dwa
            dba_ref[...] += dba
            dlam_ref[...] += dlam

    tokens = batch * seq
    blk = pl.BlockSpec((seq, LRU_LANES), lambda c, b: (b, c))
    vec = pl.BlockSpec((1, LRU_LANES), lambda c, b: (0, c))
    mat = pl.BlockSpec((LRU_BLOCKS_PER_STEP, LANES, LANES), lambda c, b: (c, 0, 0))
    vec_shape = jax.ShapeDtypeStruct((1, D_MODEL), F32)
    mat_shape = jax.ShapeDtypeStruct((N_BLK, LANES, LANES), F32)
    return pl.pallas_call(
        body, name="branch_a_bwd",
        grid=(LRU_STEPS, batch),
        in_specs=[pl.BlockSpec((2, seq, LRU_LANES), lambda c, b: (2, b, c)), blk, blk,
                  pl.BlockSpec(memory_space=pl.ANY)] + _lru_param_specs(0),
        out_specs=[pl.BlockSpec((2, seq, LRU_LANES), lambda c, b: (2, b, c)),
                   pl.BlockSpec((CONV_WIDTH, LRU_LANES), lambda c, b: (0, c)), vec, mat, vec, mat, vec, vec],
        out_shape=[jax.ShapeDtypeStruct((N_GROUPS, tokens, D_MODEL), _MXU_DTYPE),
                   jax.ShapeDtypeStruct((CONV_WIDTH, D_MODEL), F32), vec_shape, mat_shape, vec_shape, mat_shape,
                   vec_shape, vec_shape],
        scratch_shapes=[pltpu.VMEM((seq, LRU_LANES), F32)] * 3,
        input_output_aliases={3: 0},
        compiler_params=_params(("parallel", "arbitrary")),
    )(z, hl, dya, dz, conv_w, conv_b, wx, bx, wa, ba, lam)


def _chunk_masks(transposed=False):
    r = lax.broadcasted_iota(jnp.int32, (CHUNK, CHUNK), 0)
    c = lax.broadcasted_iota(jnp.int32, (CHUNK, CHUNK), 1)
    return r <= c if transposed else r >= c


def _row_blocks(seq, fn):
    block = min(256, seq)

    def trip(i, carry):
        fn(pl.ds(pl.multiple_of(i * block, block), block))
        return carry

    lax.fori_loop(0, seq // block, trip, 0)


def _hgrn_prepare(z_ref, lb_ref, f_scr, logf_scr, qh_scr, seq):
    lb = _sigmoid(lb_ref[0:1, :] - lb_ref[1:2, :])

    def block(rows):
        q = z_ref[0, rows, :]
        f = lb + (1.0 - lb) * _sigmoid(z_ref[1, rows, :])
        f_scr[rows, :] = f
        logf_scr[rows, :] = jnp.log(f)
        qh_scr[rows, :] = q * _sigmoid(q)

    _row_blocks(seq, block)
    return lb


def _cumsum_rows(x, reverse=False):
    shift = _shift_up if reverse else _shift_down
    d = 1
    while d < x.shape[0]:
        x = x + shift(x, d)
        d *= 2
    return x


def _lane_mean(x):
    return jnp.mean(x, axis=-1, keepdims=True)


def _token_contractions(lhs_scr, rhs_scr, out_ref, seq):
    rows_id = lax.broadcasted_iota(jnp.int32, (LANES, LANES), 0)

    def transposed(p):
        rows = pl.ds(pl.multiple_of(p * LANES, LANES), LANES)
        return jnp.transpose(lhs_scr[rows, :]).astype(_MXU_DTYPE), rhs_scr[rows, :]

    def contract(p, s):
        lhs_t, rhs = s
        return (_mm(lhs_t, jnp.where(rows_id < CHUNK, rhs, 0.0)), _mm(lhs_t, jnp.where(rows_id >= CHUNK, rhs, 0.0)))

    def store(p, out):
        out_ref[2 * p] = out[0]
        out_ref[2 * p + 1] = out[1]

    _independent_trips(seq // LANES, [transposed, contract], store)


def _chunk_rows(c):
    return pl.ds(pl.multiple_of(c * CHUNK, CHUNK), CHUNK)


def _chunk_terms(c, z_ref, f_scr, qh_scr, b_scr):
    rows = _chunk_rows(c)
    b = b_scr[rows, :]
    b_mid = b_scr[pl.ds(c * CHUNK + CHUNK // 2, 1), :]
    b_last = b_scr[pl.ds(c * CHUNK + CHUNK - 1, 1), :]
    qh = qh_scr[rows, :]
    k = 1.0 - f_scr[rows, :]
    v = z_ref[2, rows, :]
    e_q = jnp.exp(b - b_mid) * HG_SCALE
    e_k = jnp.exp(b_mid - b)
    e_qi = jnp.exp(b) * HG_SCALE
    e_ks = jnp.exp(b_last - b)
    decay = jnp.exp(b_last)
    return rows, qh, k, v, e_q, e_k, e_qi, e_ks, decay


def _independent_trips(n, stages, store, group=CHUNKS_IN_FLIGHT):
    stages = stages if isinstance(stages, (list, tuple)) else [stages]
    group = min(group, n)

    def trip(g, carry):
        ids = [g * group + i for i in range(group)]
        state = [stages[0](c) for c in ids]
        for stage in stages[1:]:
            state = [stage(c, s) for c, s in zip(ids, state)]
        for c, s in zip(ids, state):
            store(c, s)
        return carry

    lax.fori_loop(0, n // group, trip, 0)


def _branch_b_fwd(z, lb_logits, hg_g, batch, seq):
    tokens = batch * seq
    n_chunks = seq // CHUNK

    def body(z_ref, lb_ref, g_ref, yb_ref, st_ref, f_scr, logf_scr, qh_scr, b_scr, o_scr, qi_scr, ks_scr, dec_scr):
        _hgrn_prepare(z_ref, lb_ref, f_scr, logf_scr, qh_scr, seq)
        causal = _chunk_masks()
        gain = g_ref[...]

        def cumulate(c):
            return _cumsum_rows(logf_scr[_chunk_rows(c), :])

        def store_cumulated(c, b):
            b_scr[_chunk_rows(c), :] = b

        def scores(c):
            _, qh, k, v, e_q, e_k, e_qi, e_ks, decay = _chunk_terms(c, z_ref, f_scr, qh_scr, b_scr)
            return _mm_nt(qh * e_q, k * e_k), v, qh * e_qi, k * e_ks, decay

        def within_chunk(c, s):
            att, v, q_int, k_st, decay = s
            return _mm(jnp.where(causal, att, 0.0), v), q_int, k_st, decay

        def store_within_chunk(c, out):
            rows = _chunk_rows(c)
            o_scr[rows, :], qi_scr[rows, :], ks_scr[rows, :], dec_scr[pl.ds(c, 1), :] = out

        def carry_state(c, state_t):
            update = st_ref[c]
            st_ref[c] = state_t
            return state_t * dec_scr[pl.ds(c, 1), :] + update

        def finish(c):
            rows = _chunk_rows(c)
            o = o_scr[rows, :] + _mm_nt(qi_scr[rows, :], st_ref[c])
            r = lax.rsqrt(_lane_mean(o * o) + EPS)
            gb = z_ref[3, rows, :]
            return (((o * r) * gain) * (gb * _sigmoid(gb))).astype(_MXU_DTYPE)

        def store_finished(c, yb):
            yb_ref[_chunk_rows(c), :] = yb

        _independent_trips(n_chunks, cumulate, store_cumulated)
        _independent_trips(n_chunks, [scores, within_chunk], store_within_chunk)
        _token_contractions(z_ref.at[2], ks_scr, st_ref, seq)
        lax.fori_loop(0, n_chunks, carry_state, jnp.zeros((LANES, LANES), F32))
        _independent_trips(n_chunks, finish, store_finished)

    seq_buf = pltpu.VMEM((seq, LANES), F32)
    return pl.pallas_call(
        body, name="branch_b_fwd",
        grid=(batch, N_BLK),
        in_specs=[pl.BlockSpec((4, seq, LANES), lambda b, h: (0, b, h)),
                  pl.BlockSpec((2, LANES), lambda b, h: (0, h)),
                  pl.BlockSpec((1, LANES), lambda b, h: (0, 0))],
        out_specs=[pl.BlockSpec((seq, LANES), lambda b, h: (b, h)),
                   pl.BlockSpec((None, n_chunks, LANES, LANES), lambda b, h: (b * N_BLK + h, 0, 0, 0))],
        out_shape=[jax.ShapeDtypeStruct((tokens, D_MODEL), _MXU_DTYPE),
                   jax.ShapeDtypeStruct((batch * N_BLK, n_chunks, LANES, LANES), F32)],
        scratch_shapes=[seq_buf] * 7 + [pltpu.VMEM((n_chunks, LANES), F32)],
        compiler_params=_params(("parallel", "parallel")),
    )(z, lb_logits, hg_g)


def _branch_b_bwd(z, states, dyb, dz, lb_logits, hg_g, batch, seq):
    n_chunks = seq // CHUNK

    def body(z_ref, st_ref, dyb_ref, dz_in_ref, lb_ref, g_ref, dz_ref, dlog_ref, dg_ref,
             f_scr, logf_scr, qh_scr, b_scr, do_scr, qi_scr, dqh_scr, df_scr, dec_scr, dgp_scr, dlb_scr, dst_scr):
        del dz_in_ref
        first = (pl.program_id(0) == 0) & (pl.program_id(1) == 0)
        lb = _hgrn_prepare(z_ref, lb_ref, f_scr, logf_scr, qh_scr, seq)
        causal = _chunk_masks()
        anti_causal = _chunk_masks(transposed=True)
        gain = g_ref[...]

        @pl.when(first)
        def _():
            dg_ref[...] = jnp.zeros_like(dg_ref)

        @pl.when(pl.program_id(1) == 0)
        def _():
            dlb_scr[...] = jnp.zeros_like(dlb_scr)

        def cumulate(c):
            return _cumsum_rows(logf_scr[_chunk_rows(c), :])

        def store_cumulated(c, b):
            b_scr[_chunk_rows(c), :] = b

        def scores(c):
            _, qh, k, v, e_q, e_k, e_qi, e_ks, decay = _chunk_terms(c, z_ref, f_scr, qh_scr, b_scr)
            q_int = qh * e_qi
            return _mm_nt(qh * e_q, k * e_k), _mm_nt(q_int, st_ref[c]), v, q_int, decay

        def output_gradient(c, s):
            att, o_inter, v, q_int, decay = s
            rows = _chunk_rows(c)
            o = _mm(jnp.where(causal, att, 0.0), v) + o_inter
            r = lax.rsqrt(_lane_mean(o * o) + EPS)
            o_n = o * r
            gb = z_ref[3, rows, :]
            sgb = _sigmoid(gb)
            dyb_c = dyb_ref[rows, :]
            d_ong = dyb_c * (gb * sgb)
            d_gb = (dyb_c * (o_n * gain) * (sgb * (1.0 + gb * (1.0 - sgb)))).astype(_MXU_DTYPE)
            d_gain = jnp.sum(d_ong * o_n, axis=0, keepdims=True)
            d_on = d_ong * gain
            return d_gb, d_gain, r * (d_on - o_n * _lane_mean(d_on * o_n)), q_int, decay

        def store_output_gradient(c, out):
            rows = _chunk_rows(c)
            dz_ref[3, rows, :], dgp_scr[pl.ds(c, 1), :], do_scr[rows, :], qi_scr[rows, :], dec_scr[pl.ds(c, 1), :] = out

        def carry_state_gradient(cc, d_state_t):
            c = n_chunks - 1 - cc
            update = dst_scr[c]
            dst_scr[c] = d_state_t
            return d_state_t * dec_scr[pl.ds(c, 1), :] + update

        def score_gradients(c):
            rows, qh, k, v, e_q, e_k, e_qi, e_ks, decay = _chunk_terms(c, z_ref, f_scr, qh_scr, b_scr)
            state_t = st_ref[c]
            d_state_t = dst_scr[c]
            d_o = do_scr[rows, :]
            q_in, k_in, q_int, k_st = qh * e_q, k * e_k, qh * e_qi, k * e_ks
            first = (_mm_nt(k_in, q_in), _mm_nt(d_o, v), _mm_nt(v, d_o), _mm_nt(k_st, d_state_t), _mm(d_o, state_t),
                     _mm(v, d_state_t))
            d_decay = jnp.sum(state_t * d_state_t, axis=0, keepdims=True)
            return first, d_o, q_in, k_in, q_int, k_st, e_q, e_k, e_qi, e_ks, decay, d_decay

        def input_gradients(c, s):
            (att_t, d_att, d_att_t, dv_inter, dq_int, dk_st), d_o, q_in, k_in, q_int, k_st, e_q, e_k, e_qi, e_ks, decay, d_decay = s
            rows = _chunk_rows(c)
            d_v = _mm(jnp.where(anti_causal, att_t, 0.0), d_o) + dv_inter
            dq_in = _mm(jnp.where(causal, d_att, 0.0), k_in)
            dk_in = _mm(jnp.where(anti_causal, d_att_t, 0.0), q_in)
            d_k = dk_in * e_k + dk_st * e_ks
            kk = dk_st * k_st
            d_b = dq_in * q_in + dq_int * q_int - dk_in * k_in - kk
            d_b_last = jnp.sum(kk, axis=0, keepdims=True) + decay * d_decay
            d_logf = _cumsum_rows(d_b, reverse=True) + d_b_last
            return d_v.astype(_MXU_DTYPE), dq_in * e_q + dq_int * e_qi, d_logf / f_scr[rows, :] - d_k

        def store_input_gradients(c, out):
            rows = _chunk_rows(c)
            dz_ref[2, rows, :], dqh_scr[rows, :], df_scr[rows, :] = out

        def input_activations(rows):
            q = z_ref[0, rows, :]
            sq = _sigmoid(q)
            dz_ref[0, rows, :] = (dqh_scr[rows, :] * (sq * (1.0 + q * (1.0 - sq)))).astype(_MXU_DTYPE)
            sg = _sigmoid(z_ref[1, rows, :])
            d_f = df_scr[rows, :]
            dz_ref[1, rows, :] = (d_f * (1.0 - lb) * sg * (1.0 - sg)).astype(_MXU_DTYPE)
            dlb_scr[...] += jnp.sum(d_f * (1.0 - sg), axis=0, keepdims=True)

        _independent_trips(n_chunks, cumulate, store_cumulated)
        _independent_trips(n_chunks, [scores, output_gradient], store_output_gradient)
        _token_contractions(do_scr, qi_scr, dst_scr, seq)
        lax.fori_loop(0, n_chunks, carry_state_gradient, jnp.zeros((LANES, LANES), F32))
        _independent_trips(n_chunks, [score_gradients, input_gradients], store_input_gradients)
        dg_ref[...] += jnp.sum(dgp_scr[...], axis=0, keepdims=True)
        _row_blocks(seq, input_activations)
        d_l0 = dlb_scr[...] * lb * (1.0 - lb)
        dlog_ref[0:1, :] = d_l0
        dlog_ref[1:2, :] = -d_l0

    tokens = batch * seq
    seq_buf = pltpu.VMEM((seq, LANES), F32)
    chunk_rows = pltpu.VMEM((n_chunks, LANES), F32)
    return pl.pallas_call(
        body, name="branch_b_bwd",
        grid=(N_BLK, batch),
        in_specs=[pl.BlockSpec((4, seq, LANES), lambda h, b: (0, b, h)),
                  pl.BlockSpec((None, n_chunks, LANES, LANES), lambda h, b: (b * N_BLK + h, 0, 0, 0)),
                  pl.BlockSpec((seq, LANES), lambda h, b: (b, h)),
                  pl.BlockSpec(memory_space=pl.ANY),
                  pl.BlockSpec((2, LANES), lambda h, b: (0, h)),
                  pl.BlockSpec((1, LANES), lambda h, b: (0, 0))],
        out_specs=[pl.BlockSpec((4, seq, LANES), lambda h, b: (0, b, h)),
                   pl.BlockSpec((2, LANES), lambda h, b: (0, h)),
                   pl.BlockSpec((1, LANES), lambda h, b: (0, 0))],
        out_shape=[jax.ShapeDtypeStruct((N_GROUPS, tokens, D_MODEL), _MXU_DTYPE),
                   jax.ShapeDtypeStruct((2, D_MODEL), F32),
                   jax.ShapeDtypeStruct((1, LANES), F32)],
        scratch_shapes=[seq_buf] * 8 + [chunk_rows, chunk_rows, pltpu.VMEM((1, LANES), F32),
                                        pltpu.VMEM((n_chunks, LANES, LANES), F32)],
        input_output_aliases={3: 0},
        compiler_params=_params(("arbitrary", "arbitrary")),
    )(z, states, dyb, dz, lb_logits, hg_g)


def _merge_tail(ya, yb, z, x2d, tgt2d, b_merge, final_g, pa, pb, wo):
    tokens, d = x2d.shape
    tm = min(256, tokens)
    n_tiles = tokens // tm

    def body(ya_ref, yb_ref, z_ref, x_ref, t_ref, bm_ref, fg_ref, pa_hbm, pb_hbm, wo_hbm,
             dya_ref, dyb_ref, dx2_ref, dz_ref, loss_ref, dfg_ref, dbm_ref, dpa_hbm, dpb_hbm, dwo_hbm,
             pa_s, pb_s, wo_s, dpa_s, dpb_s, dwo_s):
        i = pl.program_id(0)

        @pl.when(i == 0)
        def _():
            pltpu.sync_copy(pa_hbm, pa_s)
            pltpu.sync_copy(pb_hbm, pb_s)
            pltpu.sync_copy(wo_hbm, wo_s)
            dpa_s[...] = jnp.zeros_like(dpa_s)
            dpb_s[...] = jnp.zeros_like(dpb_s)
            dwo_s[...] = jnp.zeros_like(dwo_s)
            loss_ref[...] = jnp.zeros_like(loss_ref)
            dfg_ref[...] = jnp.zeros_like(dfg_ref)
            dbm_ref[...] = jnp.zeros_like(dbm_ref)

        ya_t = ya_ref[...]
        yb_t = yb_ref[...]
        out_a = _mm(ya_t, pa_s[...])
        out_b = _mm(yb_t, pb_s[...])
        g_a = _sigmoid(z_ref[0] + bm_ref[:, :d])
        g_b = _sigmoid(z_ref[1] + bm_ref[:, d:])
        mixed = g_a * out_a + g_b * out_b
        x2 = x_ref[...] + _mm(mixed, wo_s[...])
        r = lax.rsqrt(jnp.mean(x2 * x2, axis=-1, keepdims=True) + EPS)
        xn = x2 * r
        fg = fg_ref[...]
        diff = xn * fg - t_ref[...]
        loss_ref[...] += jnp.sum(diff * diff) * (0.5 / d)
        dy = diff * (1.0 / d)
        dfg_ref[...] += jnp.sum(dy * xn, axis=0, keepdims=True)
        dxn = dy * fg
        dx2 = r * (dxn - xn * jnp.mean(dxn * xn, axis=-1, keepdims=True))
        dx2_ref[...] = dx2
        dmixed = _mm_nt(dx2, wo_s[...])
        dwo_s[...] += _mm_tn(mixed, dx2)
        dgm_a = dmixed * out_a * g_a * (1.0 - g_a)
        dgm_b = dmixed * out_b * g_b * (1.0 - g_b)
        dz_ref[0] = dgm_a.astype(_MXU_DTYPE)
        dz_ref[1] = dgm_b.astype(_MXU_DTYPE)
        dbm_ref[:, :d] += jnp.sum(dgm_a, axis=0, keepdims=True)
        dbm_ref[:, d:] += jnp.sum(dgm_b, axis=0, keepdims=True)
        dout_a = dmixed * g_a
        dout_b = dmixed * g_b
        dpa_s[...] += _mm_tn(ya_t, dout_a)
        dpb_s[...] += _mm_tn(yb_t, dout_b)
        dya_ref[...] = _mm_nt(dout_a, pa_s[...])
        dyb_ref[...] = _mm_nt(dout_b, pb_s[...])

        @pl.when(i == n_tiles - 1)
        def _():
            pltpu.sync_copy(dpa_s, dpa_hbm)
            pltpu.sync_copy(dpb_s, dpb_hbm)
            pltpu.sync_copy(dwo_s, dwo_hbm)

    tile = pl.BlockSpec((tm, d), lambda i: (i, 0))
    gm = pl.BlockSpec((2, tm, d), lambda i: (3, i, 0))
    row = lambda n: pl.BlockSpec((1, n), lambda i: (0, 0))
    hbm = pl.BlockSpec(memory_space=pl.ANY)
    act = jax.ShapeDtypeStruct((tokens, d), F32)
    mat = jax.ShapeDtypeStruct((d, d), F32)
    return pl.pallas_call(
        body, name="merge_tail",
        grid=(n_tiles,),
        in_specs=[tile, tile, gm, tile, tile, row(2 * d), row(d), hbm, hbm, hbm],
        out_specs=[tile, tile, tile, gm, row(LANES), row(d), row(2 * d), hbm, hbm, hbm],
        out_shape=[act, act, act, jax.ShapeDtypeStruct((N_GROUPS, tokens, d), _MXU_DTYPE),
                   jax.ShapeDtypeStruct((1, LANES), F32), jax.ShapeDtypeStruct((1, d), F32),
                   jax.ShapeDtypeStruct((1, 2 * d), F32), mat, mat, mat],
        scratch_shapes=[pltpu.VMEM((d, d), _MXU_DTYPE)] * 3 + [pltpu.VMEM((d, d), F32)] * 3,
        compiler_params=_params(("arbitrary",)),
    )(ya, yb, z, x2d, tgt2d, b_merge, final_g, pa, pb, wo)


def _inproj_dw(h_t, dz):
    d, tokens = h_t.shape
    tm = min(2048, tokens)

    def body(h_ref, dz_ref, dw_ref):
        part = _mm(h_ref[...], dz_ref[...])

        @pl.when(pl.program_id(1) == 0)
        def _():
            dw_ref[...] = part

        @pl.when(pl.program_id(1) != 0)
        def _():
            dw_ref[...] += part

    def out_index(s, i):
        g = _group_of_slot(s)
        return (g // 2, 0, g % 2)

    return pl.pallas_call(
        body, name="inproj_dw",
        grid=(N_GROUPS, tokens // tm),
        in_specs=[pl.BlockSpec((d, tm), lambda s, i: (0, i)),
                  pl.BlockSpec((None, tm, D_MODEL), lambda s, i: (s, i, 0))],
        out_specs=pl.BlockSpec((None, d, D_MODEL), out_index),
        out_shape=jax.ShapeDtypeStruct((N_SHARDS, d, 2 * D_MODEL), F32),
        compiler_params=_params(("parallel", "arbitrary")),
    )(h_t, dz)


def _inproj_dw_exchange(h_t, dz):
    d, tokens = h_t.shape
    tm = min(2048, tokens)
    n_i = tokens // tm
    half = d // 2

    def body(h_ref, dz_ref, dw_hbm, land_hbm, acc, local_sems, send_sems, recv_sems):
        s, i = pl.program_id(0), pl.program_id(1)
        x, y, c, _ = _mesh_position()
        part = _mm(h_ref[...], dz_ref[...])
        buf = acc.at[s % 2]

        @pl.when(i == 0)
        def _():
            buf[...] = part

        @pl.when(i != 0)
        def _():
            buf[...] += part

        def copies(k):
            g = _SLOT_TO_GROUP[k]
            cols = pl.ds((g % 2) * D_MODEL, D_MODEL)
            src = acc.at[k % 2]
            mine = pltpu.make_async_copy(src, dw_hbm.at[g // 2, :, cols], local_sems.at[k % 2])
            theirs = _remote(src.at[pl.ds((1 - c) * half, half), :], land_hbm.at[g // 2, :, cols],
                             send_sems, recv_sems, k, (x, y, 1 - c))
            return mine, theirs

        for k in range(N_GROUPS):
            @pl.when((s == k) & (i == n_i - 1))
            def _(k=k):
                if k > 0:
                    mine, theirs = copies(k - 1)
                    mine.wait()
                    theirs.wait_send()
                mine, theirs = copies(k)
                mine.start()
                theirs.start()
                if k == N_GROUPS - 1:
                    mine.wait()
                    theirs.wait_send()
                    for kk in range(N_GROUPS):
                        copies(kk)[1].wait_recv()

    hbm = pl.BlockSpec(memory_space=pl.ANY)
    return pl.pallas_call(
        body, name="inproj_dw_exchange",
        grid=(N_GROUPS, n_i),
        in_specs=[pl.BlockSpec((d, tm), lambda s, i: (0, i)),
                  pl.BlockSpec((None, tm, D_MODEL), lambda s, i: (s, i, 0))],
        out_specs=[hbm, hbm],
        out_shape=[jax.ShapeDtypeStruct((N_SHARDS, d, 2 * D_MODEL), F32),
                   jax.ShapeDtypeStruct((N_SHARDS, half, 2 * D_MODEL), F32)],
        scratch_shapes=[pltpu.VMEM((2, d, D_MODEL), F32), pltpu.SemaphoreType.DMA((2,)),
                        pltpu.SemaphoreType.DMA((N_GROUPS,)), pltpu.SemaphoreType.DMA((N_GROUPS,))],
        compiler_params=_params(("arbitrary", "arbitrary")),
    )(h_t, dz)


def _inproj_dx(dz, w_all, x2d, dx2, norm_g, scatter=None):
    tokens, d = x2d.shape
    tm = min(512, tokens)
    n_tiles = tokens // tm
    n_big = len(scatter[0]) if scatter else 0

    def body(dz_ref, w_hbm, x_ref, dx2_ref, g_ref, *rest):
        if scatter:
            srcs, small_src = rest[:n_big], rest[2 * n_big]
            dx_ref, dg_ref = rest[2 * n_big + 1:2 * n_big + 3]
            outs, small_out = rest[2 * n_big + 3:3 * n_big + 3], rest[3 * n_big + 3]
            acc, w_res, send_sems, recv_sems, local_sem = rest[3 * n_big + 4:]
            copies = _scatter_copies(srcs, small_src, outs, small_out, send_sems, recv_sems, local_sem)
        else:
            dx_ref, dg_ref, acc, w_res = rest
            copies = []
        s = pl.program_id(1)

        @pl.when((pl.program_id(0) == 0) & (s == 0))
        def _():
            for cp in copies:
                cp.start()
            _load_w_in_by_slot(w_hbm, w_res)

        part = _mm_nt(dz_ref[...], w_res[s])

        @pl.when(s == 0)
        def _():
            acc[...] = part

        @pl.when(s != 0)
        def _():
            acc[...] += part

        @pl.when((pl.program_id(0) == 0) & (s == 0))
        def _():
            dg_ref[...] = jnp.zeros_like(dg_ref)

        @pl.when(s == N_GROUPS - 1)
        def _():
            x = x_ref[...]
            r = lax.rsqrt(jnp.mean(x * x, axis=-1, keepdims=True) + EPS)
            xn = x * r
            dh = acc[...]
            dg_ref[...] += jnp.sum(dh * xn, axis=0, keepdims=True)
            dxn = dh * g_ref[...]
            dx_ref[...] = r * (dxn - xn * jnp.mean(dxn * xn, axis=-1, keepdims=True)) + dx2_ref[...]

        @pl.when((pl.program_id(0) == n_tiles - 1) & (s == N_GROUPS - 1))
        def _():
            for cp in copies:
                cp.wait()

    tile = pl.BlockSpec((tm, d), lambda i, s: (i, 0))
    hbm = pl.BlockSpec(memory_space=pl.ANY)
    in_specs = [pl.BlockSpec((None, tm, D_MODEL), lambda i, s: (s, i, 0)), hbm, tile, tile,
                pl.BlockSpec((1, d), lambda i, s: (0, 0))]
    out_specs = [tile, pl.BlockSpec((1, d), lambda i, s: (0, 0))]
    out_shape = [jax.ShapeDtypeStruct((tokens, d), F32), jax.ShapeDtypeStruct((1, d), F32)]
    scratch = [pltpu.VMEM((tm, d), F32), pltpu.VMEM((N_GROUPS, d, D_MODEL), _MXU_DTYPE)]
    operands, aliases = [dz, w_all, x2d, dx2, norm_g], {}
    if scatter:
        bigs, by_chip, small_piece = scatter
        n_sem = 3 * n_big + N_DEV - 1
        in_specs += [hbm] * (2 * n_big + 1)
        out_specs += [hbm] * (n_big + 1)
        out_shape += [jax.ShapeDtypeStruct(g.shape, g.dtype) for g in by_chip]
        out_shape.append(jax.ShapeDtypeStruct((N_DEV, PIECE_ROWS, LANES), F32))
        scratch += [pltpu.SemaphoreType.DMA((n_sem,)), pltpu.SemaphoreType.DMA((n_sem,)), pltpu.SemaphoreType.DMA]
        operands += [*bigs, *by_chip, small_piece]
        aliases = {5 + n_big + a: 2 + a for a in range(n_big)}
    return pl.pallas_call(
        body, name="inproj_dx", grid=(n_tiles, N_GROUPS), in_specs=in_specs, out_specs=out_specs, out_shape=out_shape,
        scratch_shapes=scratch, input_output_aliases=aliases,
        compiler_params=_params(("arbitrary", "arbitrary")),
    )(*operands)


def _row_tile(rows, cols, itemsize=4, budget=2 * 1024 * 1024):
    tr = rows
    while tr * cols * itemsize > budget and tr % 16 == 0:
        tr //= 2
    return tr


def _cast_into_slot(a, chip, dtype, name):
    rows, cols = a.shape
    tr = _row_tile(rows, cols)

    def body(chip_ref, a_ref, o_ref):
        del chip_ref
        o_ref[...] = a_ref[...].astype(dtype)

    grid_spec = pltpu.PrefetchScalarGridSpec(
        num_scalar_prefetch=1, grid=(rows // tr,),
        in_specs=[pl.BlockSpec((tr, cols), lambda i, chip_ref: (i, 0))],
        out_specs=pl.BlockSpec((None, tr, cols), lambda i, chip_ref: (chip_ref[0], i, 0)))
    return pl.pallas_call(body, name=name, grid_spec=grid_spec,
                          out_shape=jax.ShapeDtypeStruct((N_SHARDS, rows, cols), dtype),
                          compiler_params=_params(("arbitrary",)))(chip, a)


def _sum_slots(stack, name):
    n, rows, cols = stack.shape
    tr = _row_tile(rows, cols * n)

    def body(s_ref, o_ref):
        total = s_ref[0].astype(F32)
        for k in range(1, n):
            total = total + s_ref[k].astype(F32)
        o_ref[...] = total

    return pl.pallas_call(body, name=name, grid=(rows // tr,),
                          in_specs=[pl.BlockSpec((n, tr, cols), lambda i: (0, i, 0))],
                          out_specs=pl.BlockSpec((tr, cols), lambda i: (i, 0)),
                          out_shape=jax.ShapeDtypeStruct((rows, cols), F32),
                          compiler_params=_params(("parallel",)))(stack)


def _add_half(full, landed, place, name):
    n, rows, cols = full.shape
    half = rows // 2
    tr = _row_tile(half, cols)
    nb = half // tr

    def body(place_ref, a_ref, b_ref, o_ref, own_ref):
        total = (a_ref[...] + b_ref[...]).astype(_MXU_DTYPE)
        o_ref[...] = total

        @pl.when(pl.program_id(1) == place_ref[1])
        def _():
            own_ref[...] = total

    grid_spec = pltpu.PrefetchScalarGridSpec(
        num_scalar_prefetch=1, grid=(nb, n),
        in_specs=[pl.BlockSpec((None, tr, cols), lambda i, j, place_ref: (j, place_ref[0] * nb + i, 0)),
                  pl.BlockSpec((None, tr, cols), lambda i, j, place_ref: (j, i, 0))],
        out_specs=[pl.BlockSpec((None, tr, cols), lambda i, j, place_ref: (j, i, 0)),
                   pl.BlockSpec((None, tr, cols), lambda i, j, place_ref: (place_ref[1], i, 0))])
    shape = jax.ShapeDtypeStruct((n, half, cols), _MXU_DTYPE)
    return pl.pallas_call(body, name=name, grid_spec=grid_spec, out_shape=[shape, shape],
                          compiler_params=_params(("parallel", "arbitrary")))(place, full, landed)


def _adamw_update(w, grad, m, v):
    c1 = 1.0 - ADAM_B1 ** ADAM_STEP
    c2 = 1.0 - ADAM_B2 ** ADAM_STEP
    nm = ADAM_B1 * m + (1.0 - ADAM_B1) * grad
    nv = ADAM_B2 * v + (1.0 - ADAM_B2) * (grad * grad)
    return (-ADAM_LR) * ((nm / c1) / (jnp.sqrt(nv / c2) + ADAM_EPS) + ADAM_WD * w), nm, nv


def _adamw(w, g, m, v, name):
    rows, cols = w.shape
    tr = _row_tile(rows, cols, budget=1024 * 1024)

    def body(w_ref, g_ref, m_ref, v_ref, d_ref, nm_ref, nv_ref):
        d_ref[...], nm_ref[...], nv_ref[...] = _adamw_update(w_ref[...], g_ref[...], m_ref[...], v_ref[...])

    spec = pl.BlockSpec((tr, cols), lambda i: (i, 0))
    shape = jax.ShapeDtypeStruct((rows, cols), F32)
    return pl.pallas_call(body, name=name, grid=(rows // tr,), in_specs=[spec] * 4, out_specs=[spec] * 3,
                          out_shape=[shape] * 3, compiler_params=_params(("parallel",)))(w, g, m, v)


def _adamw_halves(w, g_mine, g_sibling, m, v, core, name):
    rows, cols = w.shape
    half = rows // 2
    tr = _row_tile(half, cols, budget=1024 * 1024)
    nb = half // tr

    def body(core_ref, w_ref, gm_ref, gs_ref, m_ref, v_ref, g_ref, d_ref, nm_ref, nv_ref):
        mine = pl.program_id(0) // nb == core_ref[0]
        grad = jnp.where(mine, gm_ref[...], gs_ref[...])
        g_ref[...] = grad
        d_ref[...], nm_ref[...], nv_ref[...] = _adamw_update(w_ref[...], grad, m_ref[...], v_ref[...])

    spec = pl.BlockSpec((tr, cols), lambda i, core_ref: (i, 0))
    half_spec = pl.BlockSpec((tr, cols), lambda i, core_ref: (i % nb, 0))
    grid_spec = pltpu.PrefetchScalarGridSpec(num_scalar_prefetch=1, grid=(rows // tr,),
                                             in_specs=[spec, half_spec, half_spec, spec, spec], out_specs=[spec] * 4)
    shape = jax.ShapeDtypeStruct((rows, cols), F32)
    return pl.pallas_call(body, name=name, grid_spec=grid_spec, out_shape=[shape] * 4,
                          compiler_params=_params(("parallel",)))(core, w, g_mine, g_sibling, m, v)


def _local_step(x, loss_target, w_all, pa, pb, wo, conv_w, b_merge, conv_b, rg_wx, rg_bx, rg_wa, rg_ba,
                rg_lambda, hg_lb_logits, hg_norm_g, norm_g, final_norm_g, gather=None, start_reduction=None):
    batch, seq, d = x.shape
    x2d = x.reshape(batch * seq, d)
    tgt2d = loss_target.reshape(batch * seq, d)
    if gather is None:
        z, h_t = _inproj_fwd(x2d, norm_g, w_all)
    else:
        z, h_t, (w_all, pa, pb, wo), cw_all = _inproj_fwd_gather(x2d, norm_g, *gather)
        pa, pb, wo = (t.reshape(d, d) for t in (pa, pb, wo))
        conv_w = jnp.transpose(cw_all, (1, 0, 2)).reshape(CONV_WIDTH, d)
    lru = (conv_w, conv_b, rg_wx, rg_bx, rg_wa, rg_ba, rg_lambda)
    ya, hl = _branch_a_fwd(z, *lru, batch, seq)
    yb, states = _branch_b_fwd(z, hg_lb_logits, hg_norm_g, batch, seq)
    dya, dyb, dx2, dz, loss, d_final_g, d_b_merge, d_pa, d_pb, d_wo = _merge_tail(
        ya, yb, z, x2d, tgt2d, b_merge, final_norm_g, pa, pb, wo)
    dz, d_lb_logits, d_hg_g = _branch_b_bwd(z, states, dyb, dz, hg_lb_logits, hg_norm_g, batch, seq)
    dz, d_conv_w, d_conv_b, d_wx, d_bx, d_wa, d_ba, d_lam = _branch_a_bwd(z, hl, dya, dz, *lru, batch, seq)
    if start_reduction is None:
        d_w_in, landed_w_in = _inproj_dw(h_t, dz), None
    else:
        d_w_in, landed_w_in = _inproj_dw_exchange(h_t, dz)
    big = (d_w_in, d_pa, d_pb, d_wo)
    small = dict(b_merge=d_b_merge, conv_w=d_conv_w, conv_b=d_conv_b, rg_wx=d_wx, rg_bx=d_bx, rg_wa=d_wa,
                 rg_ba=d_ba, rg_lambda=d_lam, hg_lb_logits=d_lb_logits, hg_norm_g=d_hg_g,
                 norm_g=jnp.zeros((1, d), F32), final_norm_g=d_final_g)
    if start_reduction is None:
        grad_x, small["norm_g"] = _inproj_dx(dz, w_all, x2d, dx2, norm_g)
        return loss[0, 0], grad_x.reshape(batch, seq, d), big, small
    scatter = start_reduction(big, small, landed_w_in)
    grad_x, d_norm_g, *scattered = _inproj_dx(dz, w_all, x2d, dx2, norm_g, scatter=scatter)
    return loss[0, 0], grad_x.reshape(batch, seq, d), d_norm_g, scattered


_SMALL_ORDER = ("b_merge", "conv_w", "conv_b", "rg_wx", "rg_bx", "rg_wa", "rg_ba", "rg_lambda", "hg_lb_logits",
                "hg_norm_g", "norm_g", "final_norm_g")
N_DEV = 8
PIECE_ROWS = 272


def _pack_small(tree):
    flat = jnp.concatenate([tree[k].reshape(-1) for k in _SMALL_ORDER])
    flat = jnp.pad(flat, (0, N_DEV * PIECE_ROWS * LANES - flat.shape[0]))
    return flat.reshape(N_DEV * PIECE_ROWS, LANES)


def _unpack_small(packed, like):
    flat = packed.reshape(-1)
    out, pos = {}, 0
    for k in _SMALL_ORDER:
        n = like[k].size
        out[k] = flat[pos:pos + n].reshape(like[k].shape)
        pos += n
    return out


def _mesh_position():
    x, y, c = lax.axis_index("x"), lax.axis_index("y"), lax.axis_index("c")
    other_chips = [(1 - x, y), (x, 1 - y), (1 - x, 1 - y)]
    return x, y, c, other_chips


def _other_devices(x, y, c):
    flips = [(fx, fy, fc) for fx in (0, 1) for fy in (0, 1) for fc in (0, 1) if (fx, fy, fc) != (0, 0, 0)]
    return [(jnp.where(fx, 1 - x, x), jnp.where(fy, 1 - y, y), jnp.where(fc, 1 - c, c)) for fx, fy, fc in flips]


def _remote(src, dst, send_sems, recv_sems, k, device):
    return pltpu.make_async_remote_copy(src_ref=src, dst_ref=dst, send_sem=send_sems.at[k], recv_sem=recv_sems.at[k],
                                        device_id=device, device_id_type=MESH)


def _exchange_halves(bigs, small):
    n_big = len(bigs)
    n_sem = n_big + N_DEV - 1

    def body(*refs):
        srcs, small_src = refs[:n_big], refs[n_big]
        outs, small_out = refs[n_big + 1:2 * n_big + 1], refs[2 * n_big + 1]
        send_sems, recv_sems, local_sem = refs[2 * n_big + 2:]
        x, y, c, _ = _mesh_position()
        me, sibling = 4 * x + 2 * y + c, (x, y, 1 - c)
        mine = pltpu.make_async_copy(small_src.at[pl.ds(me * PIECE_ROWS, PIECE_ROWS), :], small_out.at[me], local_sem)
        mine.start()
        copies = []
        for a in range(n_big):
            hs = srcs[a].shape[1] // 2
            copies.append(_remote(srcs[a].at[:, pl.ds((1 - c) * hs, hs), :], outs[a], send_sems, recv_sems, a, sibling))
        for k, (px, py, pc) in enumerate(_other_devices(x, y, c)):
            piece = small_src.at[pl.ds((4 * px + 2 * py + pc) * PIECE_ROWS, PIECE_ROWS), :]
            copies.append(_remote(piece, small_out.at[me], send_sems, recv_sems, n_big + k, (px, py, pc)))
        for cp in copies:
            cp.start()
        for cp in copies:
            cp.wait()
        mine.wait()

    hbm = pl.BlockSpec(memory_space=pl.ANY)
    out_shape = [jax.ShapeDtypeStruct((g.shape[0], g.shape[1] // 2, g.shape[2]), F32) for g in bigs]
    out_shape.append(jax.ShapeDtypeStruct((N_DEV, PIECE_ROWS, LANES), F32))
    return pl.pallas_call(
        body, name="exchange_halves",
        in_specs=[hbm] * (n_big + 1), out_specs=[hbm] * (n_big + 1), out_shape=out_shape,
        scratch_shapes=[pltpu.SemaphoreType.DMA((n_sem,)), pltpu.SemaphoreType.DMA((n_sem,)), pltpu.SemaphoreType.DMA],
    )(*bigs, small)


def _scatter_copies(srcs, small_src, outs, small_out, send_sems, recv_sems, local_sem):
    n_big = len(srcs)
    x, y, c, chips = _mesh_position()
    chip, me = 2 * x + y, 4 * x + 2 * y + c
    copies = [pltpu.make_async_copy(small_src, small_out.at[me], local_sem)]
    for a in range(n_big):
        for j, (cx, cy) in enumerate(chips):
            copies.append(_remote(srcs[a].at[2 * cx + cy], outs[a].at[chip], send_sems, recv_sems, 3 * a + j, (cx, cy, c)))
    for k, peer in enumerate(_other_devices(x, y, c)):
        copies.append(_remote(small_src, small_out.at[me], send_sems, recv_sems, 3 * n_big + k, peer))
    return copies


def _swap_halves(halves, vec):
    n_big = len(halves)

    def body(*refs):
        srcs, vec_src = refs[:n_big], refs[n_big]
        outs, vec_out = refs[n_big + 1:2 * n_big + 1], refs[2 * n_big + 1]
        send_sems, recv_sems, local_sem = refs[2 * n_big + 2:]
        x, y, c, _ = _mesh_position()
        me = 4 * x + 2 * y + c
        copies = [pltpu.make_async_copy(vec_src, vec_out.at[me], local_sem)]
        copies += [_remote(srcs[a], outs[a], send_sems, recv_sems, a, (x, y, 1 - c)) for a in range(n_big)]
        copies += [_remote(vec_src, vec_out.at[me], send_sems, recv_sems, n_big + k, peer)
                   for k, peer in enumerate(_other_devices(x, y, c))]
        for cp in copies:
            cp.start()
        for cp in copies:
            cp.wait()

    hbm = pl.BlockSpec(memory_space=pl.ANY)
    n_sem = n_big + N_DEV - 1
    return pl.pallas_call(
        body, name="swap_halves",
        in_specs=[hbm] * (n_big + 1), out_specs=[hbm] * (n_big + 1),
        out_shape=[jax.ShapeDtypeStruct(h.shape, F32) for h in halves] + [jax.ShapeDtypeStruct((N_DEV,) + vec.shape, F32)],
        scratch_shapes=[pltpu.SemaphoreType.DMA((n_sem,)), pltpu.SemaphoreType.DMA((n_sem,)), pltpu.SemaphoreType.DMA],
    )(*halves, vec)


def kernel(x, w_in, b_merge, conv_w, conv_b, rg_wx, rg_bx, rg_wa, rg_ba, rg_lambda, hg_lb_logits, hg_norm_g, proj_a, proj_b, w_out, norm_g, final_norm_g, loss_target, m_w_in, m_b_merge, m_conv_w, m_conv_b, m_rg_wx, m_rg_bx, m_rg_wa, m_rg_ba, m_rg_lambda, m_hg_lb_logits, m_hg_norm_g, m_proj_a, m_proj_b, m_w_out, m_norm_g, m_final_norm_g, v_w_in, v_b_merge, v_conv_w, v_conv_b, v_rg_wx, v_rg_bx, v_rg_wa, v_rg_ba, v_rg_lambda, v_hg_lb_logits, v_hg_norm_g, v_proj_a, v_proj_b, v_w_out, v_norm_g, v_final_norm_g):
    d = D_MODEL
    weights = dict(w_in=w_in, b_merge=b_merge, conv_w=conv_w, conv_b=conv_b, rg_wx=rg_wx, rg_bx=rg_bx, rg_wa=rg_wa,
                   rg_ba=rg_ba, rg_lambda=rg_lambda, hg_lb_logits=hg_lb_logits, hg_norm_g=hg_norm_g, proj_a=proj_a,
                   proj_b=proj_b, w_out=w_out, norm_g=norm_g, final_norm_g=final_norm_g)
    m = dict(w_in=m_w_in, b_merge=m_b_merge, conv_w=m_conv_w, conv_b=m_conv_b, rg_wx=m_rg_wx, rg_bx=m_rg_bx,
             rg_wa=m_rg_wa, rg_ba=m_rg_ba, rg_lambda=m_rg_lambda, hg_lb_logits=m_hg_lb_logits, hg_norm_g=m_hg_norm_g,
             proj_a=m_proj_a, proj_b=m_proj_b, w_out=m_w_out, norm_g=m_norm_g, final_norm_g=m_final_norm_g)
    v = dict(w_in=v_w_in, b_merge=v_b_merge, conv_w=v_conv_w, conv_b=v_conv_b, rg_wx=v_rg_wx, rg_bx=v_rg_bx,
             rg_wa=v_rg_wa, rg_ba=v_rg_ba, rg_lambda=v_rg_lambda, hg_lb_logits=v_hg_lb_logits, hg_norm_g=v_hg_norm_g,
             proj_a=v_proj_a, proj_b=v_proj_b, w_out=v_w_out, norm_g=v_norm_g, final_norm_g=v_final_norm_g)
    big_names = ("w_in", "proj_a", "proj_b", "w_out")

    core = lax.axis_index("c").astype(jnp.int32).reshape(1)
    chip = (2 * lax.axis_index("x") + lax.axis_index("y")).astype(jnp.int32)

    slotted = [_cast_into_slot(weights[k][0], chip.reshape(1), _MXU_DTYPE, f"cast_{k}") for k in big_names]
    conv_slotted = _cast_into_slot(conv_w[0], chip.reshape(1), F32, "slot_conv_w")

    small_shapes = {}

    def start_reduction(big_grads, small_grads, landed_w_in):
        small_shapes.update({k: t.shape for k, t in small_grads.items()})
        bigs = [big_grads[0]] + [g.reshape(N_SHARDS, d // N_SHARDS, d) for g in big_grads[1:]]
        *landed, small_landed = _exchange_halves(bigs[1:], _pack_small(small_grads))
        landed = [landed_w_in] + landed
        place = jnp.concatenate([core, chip.reshape(1)])
        sums = [_add_half(g, l, place, f"add_half_{a}") for a, (g, l) in enumerate(zip(bigs, landed))]
        return [s[0] for s in sums], [s[1] for s in sums], _sum_slots(small_landed, "sum_small")

    loss_part, grad_x, d_norm_g, (*by_chip, small_all) = _local_step(
        x, loss_target, None, None, None, None, None,
        b_merge, conv_b, rg_wx[0], rg_bx.reshape(1, d), rg_wa[0], rg_ba.reshape(1, d), rg_lambda, hg_lb_logits,
        hg_norm_g, norm_g, final_norm_g.reshape(1, d), gather=(slotted, conv_slotted, chip.reshape(1)),
        start_reduction=start_reduction)
    mine = [_sum_slots(s, f"sum_chips_{a}") for a, s in enumerate(by_chip)]
    late = jnp.concatenate([d_norm_g.reshape(SUBLANES, LANES), jnp.full((SUBLANES, LANES), loss_part, F32)])
    *theirs, late_parts = _swap_halves(mine, late)
    late_sum = _sum_slots(late_parts, "sum_late")
    loss = late_sum[SUBLANES, 0]
    small_red = _unpack_small(small_all, {k: jax.ShapeDtypeStruct(s, F32) for k, s in small_shapes.items()})
    small_red["norm_g"] = late_sum[:SUBLANES].reshape(1, d)

    grads, delta, new_m, new_v = {}, {}, {}, {}
    for k, g_mine, g_theirs in zip(big_names, mine, theirs):
        out = _adamw_halves(weights[k][0], g_mine, g_theirs, m[k][0], v[k][0], core, f"adamw_{k}")
        grads[k], delta[k], new_m[k], new_v[k] = (t.reshape(weights[k].shape) for t in out)
    cols = d // N_SHARDS
    g_conv = lax.dynamic_slice(small_red["conv_w"], (0, chip * cols), (CONV_WIDTH, cols))
    grads["conv_w"] = g_conv.reshape(conv_w.shape)
    dl, nm, nv = _adamw(conv_w[0], g_conv, m_conv_w[0], v_conv_w[0], "adamw_conv_w")
    delta["conv_w"], new_m["conv_w"], new_v["conv_w"] = (t.reshape(conv_w.shape) for t in (dl, nm, nv))
    rest = [k for k in _SMALL_ORDER if k != "conv_w"]
    like = {k: (weights[k] if k != "conv_w" else jnp.zeros((CONV_WIDTH, d), F32)) for k in _SMALL_ORDER}
    packs = [_pack_small({k: (t[k] if k != "conv_w" else like[k]) for k in _SMALL_ORDER}) for t in (weights, m, v)]
    g_pack = _pack_small({k: small_red[k].reshape(like[k].shape) for k in _SMALL_ORDER})
    outs = [_unpack_small(p, like) for p in _adamw(packs[0], g_pack, packs[1], packs[2], "adamw_small")]
    for k in rest:
        grads[k] = small_red[k].reshape(weights[k].shape)
        delta[k], new_m[k], new_v[k] = outs[0][k], outs[1][k], outs[2][k]

    order = ("w_in", "b_merge", "conv_w", "conv_b", "rg_wx", "rg_bx", "rg_wa", "rg_ba", "rg_lambda", "hg_lb_logits",
             "hg_norm_g", "proj_a", "proj_b", "w_out", "norm_g", "final_norm_g")
    return (loss, grad_x, *[grads[k] for k in order], *[delta[k] for k in order], *[new_m[k] for k in order],
            *[new_v[k] for k in order])
```

```python
import functools

import jax
import jax.numpy as jnp
from jax import lax
from jax.experimental import pallas as pl
from jax.experimental.pallas import tpu as pltpu

F32 = jnp.float32
_MXU_DTYPE = jnp.bfloat16

D_MODEL = 1024
LANES = 128
SUBLANES = 8
N_BLK = D_MODEL // LANES
N_GROUPS = 8
N_SHARDS = 4
CONV_WIDTH = 4
LRU_C = 8.0
CHUNK = 64
CHUNKS_IN_FLIGHT = 16
HG_SCALE = float(LANES) ** -0.5
EPS = 1e-6
ADAM_LR, ADAM_B1, ADAM_B2, ADAM_EPS, ADAM_WD, ADAM_STEP = 0.001, 0.9, 0.999, 1e-08, 0.01, 10
VMEM_LIMIT = 56 * 1024 * 1024
MESH = pl.DeviceIdType.MESH

_SLOT_TO_GROUP = (2, 3, 4, 5, 0, 1, 6, 7)


def _group_of_slot(s):
    return jnp.where(s < 4, s + 2, jnp.where(s < 6, s - 4, s))


def _mm(a, b):
    return lax.dot_general(a.astype(_MXU_DTYPE), b.astype(_MXU_DTYPE), (((1,), (0,)), ((), ())),
                           preferred_element_type=F32)


def _mm_nt(a, b):
    return lax.dot_general(a.astype(_MXU_DTYPE), b.astype(_MXU_DTYPE), (((1,), (1,)), ((), ())),
                           preferred_element_type=F32)


def _mm_tn(a, b):
    return lax.dot_general(a.astype(_MXU_DTYPE), b.astype(_MXU_DTYPE), (((0,), (0,)), ((), ())),
                           preferred_element_type=F32)


def _sigmoid(x):
    return 0.5 * jnp.tanh(0.5 * x) + 0.5


def _log1p_pos(y):
    series = y * (1.0 - y * (0.5 - y * (1.0 / 3.0 - y * 0.25)))
    return jnp.where(y < 0.01, series, jnp.log(1.0 + y))


def _softplus(x):
    return jnp.maximum(x, 0.0) + _log1p_pos(jnp.exp(-jnp.abs(x)))


def _shift_down(x, n):
    rolled = pltpu.roll(x, n, 0)
    edge = SUBLANES if (n < SUBLANES and x.shape[0] > SUBLANES) else x.shape[0]
    rows = lax.broadcasted_iota(jnp.int32, (edge, x.shape[1]), 0)
    head = jnp.where(rows >= n, rolled[:edge], 0.0)
    return head if edge == x.shape[0] else jnp.concatenate([head, rolled[edge:]], axis=0)


def _shift_up(x, n):
    size = x.shape[0]
    rolled = pltpu.roll(x, size - n, 0)
    edge = SUBLANES if (n < SUBLANES and size > SUBLANES) else size
    rows = lax.broadcasted_iota(jnp.int32, (edge, x.shape[1]), 0)
    tail = jnp.where(rows < edge - n, rolled[size - edge:], 0.0)
    return tail if edge == size else jnp.concatenate([rolled[:size - edge], tail], axis=0)


def _params(dims, vmem=VMEM_LIMIT):
    return pltpu.CompilerParams(dimension_semantics=dims, vmem_limit_bytes=vmem)


def _load_w_in_by_slot(w_hbm, w_res):
    for slot, g in enumerate(_SLOT_TO_GROUP):
        pltpu.sync_copy(w_hbm.at[g // 2, :, pl.ds((g % 2) * D_MODEL, D_MODEL)], w_res.at[slot])


def _inproj_fwd(x2d, norm_g, w_all):
    tokens, d = x2d.shape
    tm = min(512, tokens)

    def body(x_ref, g_ref, w_hbm, z_ref, ht_ref, h_scr, w_res):
        @pl.when((pl.program_id(0) == 0) & (pl.program_id(1) == 0))
        def _():
            _load_w_in_by_slot(w_hbm, w_res)

        @pl.when(pl.program_id(1) == 0)
        def _():
            x = x_ref[...]
            r = lax.rsqrt(jnp.mean(x * x, axis=-1, keepdims=True) + EPS)
            h = (x * r) * g_ref[...]
            h_scr[...] = h.astype(_MXU_DTYPE)
            ht_ref[...] = jnp.transpose(h).astype(_MXU_DTYPE)

        z_ref[...] = _mm(h_scr[...], w_res[pl.program_id(1)])

    return pl.pallas_call(
        body, name="inproj_fwd",
        grid=(tokens // tm, N_GROUPS),
        in_specs=[pl.BlockSpec((tm, d), lambda i, s: (i, 0)),
                  pl.BlockSpec((1, d), lambda i, s: (0, 0)),
                  pl.BlockSpec(memory_space=pl.ANY)],
        out_specs=[pl.BlockSpec((None, tm, D_MODEL), lambda i, s: (s, i, 0)),
                   pl.BlockSpec((d, tm), lambda i, s: (0, i))],
        out_shape=[jax.ShapeDtypeStruct((N_GROUPS, tokens, D_MODEL), F32),
                   jax.ShapeDtypeStruct((d, tokens), _MXU_DTYPE)],
        scratch_shapes=[pltpu.VMEM((tm, d), _MXU_DTYPE), pltpu.VMEM((N_GROUPS, d, D_MODEL), _MXU_DTYPE)],
        compiler_params=_params(("arbitrary", "arbitrary")),
    )(x2d, norm_g, w_all)


def _slot_of_group(g):
    return jnp.where(g < 2, g + 4, jnp.where(g < 6, g - 2, g))


def _inproj_fwd_gather(x2d, norm_g, slotted, conv_slotted, chip):
    tokens, d = x2d.shape
    tm = min(512, tokens)
    n_tiles = tokens // tm
    n_big = len(slotted)
    n_sem = 6 * n_big + 3
    last_pass = N_GROUPS - 1

    def shard_of(k, chip_id):
        x, y = chip_id // 2, chip_id % 2
        return 2 * jnp.where(k % 2 == 1, 1 - x, x) + jnp.where(k // 2 == 1, 1 - y, y)

    def body(chip_ref, x_ref, g_ref, *rest):
        bufs, cw = rest[n_big + 1:2 * n_big + 1], rest[2 * n_big + 1]
        z_ref, ht_ref = rest[2 * n_big + 2:2 * n_big + 4]
        h_all, slab, send_sems, recv_sems, slab_sem = rest[2 * n_big + 4:]
        del chip_ref
        p, i = pl.program_id(0), pl.program_id(1)
        x, y, c, chips = _mesh_position()
        me, sibling = 2 * x + y, (x, y, 1 - c)

        def half(a, slot, which):
            hs = bufs[a].shape[1] // 2
            return bufs[a].at[slot, pl.ds(which * hs, hs), :]

        def send(a, j):
            mine = half(a, me, c)
            return _remote(mine, mine, send_sems, recv_sems, 6 * a + j, (chips[j][0], chips[j][1], c))

        def arrival(a, j):
            landed = half(a, 2 * chips[j][0] + chips[j][1], c)
            return _remote(landed, landed, send_sems, recv_sems, 6 * a + j, (chips[j][0], chips[j][1], c))

        def passed_on(a, j, which):
            landed = half(a, 2 * chips[j][0] + chips[j][1], which)
            return _remote(landed, landed, send_sems, recv_sems, 6 * a + 3 + j, sibling)

        def conv_copy(j, slot):
            return _remote(cw.at[slot], cw.at[slot], send_sems, recv_sems, 6 * n_big + j, (chips[j][0], chips[j][1], c))

        def land(a, j):
            arrival(a, j).wait_recv()
            passed_on(a, j, c).start()
            passed_on(a, j, 1 - c).wait_recv()

        @pl.when((p == 0) & (i == 0))
        def _():
            send(0, 0).start()
            send(0, 1).start()

        for j in range(3):
            @pl.when((p == 2 * (j + 1)) & (i == 0))
            def _(j=j):
                land(0, j)
                if j == 0:
                    send(0, 2).start()
                if j == 1:
                    for a in range(1, n_big):
                        for jj in range(3):
                            send(a, jj).start()
                    for jj in range(3):
                        conv_copy(jj, me).start()

        @pl.when(i == 0)
        def _():
            shard = shard_of(p // 2, me)
            for which in range(2):
                @pl.when(p % 2 == which)
                def _(which=which):
                    cp = pltpu.make_async_copy(bufs[0].at[shard, :, pl.ds(which * D_MODEL, D_MODEL)], slab, slab_sem)
                    cp.start()
                    cp.wait()

        rows = pl.ds(pl.multiple_of(i * tm, tm), tm)

        @pl.when(p == 0)
        def _():
            xt = x_ref[...]
            r = lax.rsqrt(jnp.mean(xt * xt, axis=-1, keepdims=True) + EPS)
            h = (xt * r) * g_ref[...]
            h_all[rows, :] = h.astype(_MXU_DTYPE)
            ht_ref[...] = jnp.transpose(h).astype(_MXU_DTYPE)

        z_ref[...] = _mm(h_all[rows, :], slab[...])

        @pl.when((p == last_pass) & (i == n_tiles - 1))
        def _():
            for a in range(1, n_big):
                for j in range(3):
                    land(a, j)
            for j in range(3):
                conv_copy(j, 2 * chips[j][0] + chips[j][1]).wait_recv()
            for a in range(n_big):
                for j in range(3):
                    send(a, j).wait_send()
                    passed_on(a, j, c).wait_send()
            for j in range(3):
                conv_copy(j, me).wait_send()

    def z_index(p, i, chip_ref):
        g = 2 * shard_of(p // 2, chip_ref[0]) + p % 2
        return (_slot_of_group(g), i, 0)

    def first_pass_tile(p, i, chip_ref):
        return jnp.where(p == 0, i, n_tiles - 1)

    hbm = pl.BlockSpec(memory_space=pl.ANY)
    operands = list(slotted) + [conv_slotted]
    grid_spec = pltpu.PrefetchScalarGridSpec(
        num_scalar_prefetch=1, grid=(N_GROUPS, n_tiles),
        in_specs=[pl.BlockSpec((tm, d), lambda p, i, chip_ref: (first_pass_tile(p, i, chip_ref), 0)),
                  pl.BlockSpec((1, d), lambda p, i, chip_ref: (0, 0))] + [hbm] * (n_big + 1),
        out_specs=[hbm] * (n_big + 1) + [pl.BlockSpec((None, tm, D_MODEL), z_index),
                                         pl.BlockSpec((d, tm), lambda p, i, chip_ref: (0, first_pass_tile(p, i, chip_ref)))],
        scratch_shapes=[pltpu.VMEM((tokens, d), _MXU_DTYPE), pltpu.VMEM((d, D_MODEL), _MXU_DTYPE),
                        pltpu.SemaphoreType.DMA((n_sem,)), pltpu.SemaphoreType.DMA((n_sem,)), pltpu.SemaphoreType.DMA])
    out = pl.pallas_call(
        body, name="inproj_fwd_gather", grid_spec=grid_spec,
        out_shape=[jax.ShapeDtypeStruct(a.shape, a.dtype) for a in operands]
        + [jax.ShapeDtypeStruct((N_GROUPS, tokens, D_MODEL), F32), jax.ShapeDtypeStruct((d, tokens), _MXU_DTYPE)],
        input_output_aliases={3 + a: a for a in range(n_big + 1)},
        compiler_params=_params(("arbitrary", "arbitrary")),
    )(chip, x2d, norm_g, *operands)
    return out[n_big + 1], out[n_big + 2], out[:n_big], out[n_big]


def _lane_blocks(x):
    return [x[:, k * LANES:(k + 1) * LANES] for k in range(x.shape[1] // LANES)]


def _block_diag(x, w_ref, transposed=False):
    mm = _mm_nt if transposed else _mm
    return jnp.concatenate([mm(xk, w_ref[k]) for k, xk in enumerate(_lane_blocks(x))], axis=1)


def _lru_gates(xa, cw_ref, cb_ref, wx_ref, bx_ref, wa_ref, ba_ref, lam_ref):
    xc = (cb_ref[...] + cw_ref[3:4, :] * xa + cw_ref[2:3, :] * _shift_down(xa, 1)
          + cw_ref[1:2, :] * _shift_down(xa, 2) + cw_ref[0:1, :] * _shift_down(xa, 3))
    gi = _sigmoid(_block_diag(xc, wx_ref) + bx_ref[...])
    gr = _sigmoid(_block_diag(xc, wa_ref) + ba_ref[...])
    sp = _softplus(-lam_ref[...])
    log_a = (-LRU_C) * gr * sp
    a = jnp.exp(log_a)
    y = 2.0 * log_a
    mult_sq = jnp.where(y > -1e-3, -y * (1.0 + 0.5 * y), 1.0 - a * a)
    inv_mult = lax.rsqrt(jnp.maximum(mult_sq, 1e-37))
    return xc, gi, gr, sp, a, mult_sq * inv_mult, inv_mult


def _tile_rows(width):
    return lax.broadcasted_iota(jnp.int32, (SUBLANES, width), 0)


def _scan_forward(a_scr, u_scr, h_scr, seq):
    width = a_scr.shape[1]
    rows = _tile_rows(width)

    def tile(j, carry):
        sl = pl.ds(pl.multiple_of(j * SUBLANES, SUBLANES), SUBLANES)
        a = a_scr[sl, :]
        u = u_scr[sl, :]
        for d in (1, 2, 4):
            keep = rows >= d
            a_sh = jnp.where(keep, pltpu.roll(a, d, 0), 1.0)
            u_sh = jnp.where(keep, pltpu.roll(u, d, 0), 0.0)
            u = a * u_sh + u
            a = a * a_sh
        h = u + a * carry
        h_scr[sl, :] = h
        return jnp.broadcast_to(h[SUBLANES - 1:SUBLANES, :], (SUBLANES, width))

    lax.fori_loop(0, seq // SUBLANES, tile, jnp.zeros((SUBLANES, width), F32))


def _scan_backward(c_scr, d_scr, g_scr, seq):
    width = c_scr.shape[1]
    rows = _tile_rows(width)
    n_tiles = seq // SUBLANES

    def tile(jj, carry):
        j = n_tiles - 1 - jj
        sl = pl.ds(pl.multiple_of(j * SUBLANES, SUBLANES), SUBLANES)
        c = c_scr[sl, :]
        g = d_scr[sl, :]
        for d in (1, 2, 4):
            keep = rows < SUBLANES - d
            c_sh = jnp.where(keep, pltpu.roll(c, SUBLANES - d, 0), 1.0)
            g_sh = jnp.where(keep, pltpu.roll(g, SUBLANES - d, 0), 0.0)
            g = c * g_sh + g
            c = c * c_sh
        g = g + c * carry
        g_scr[sl, :] = g
        return jnp.broadcast_to(g[0:1, :], (SUBLANES, width))

    lax.fori_loop(0, n_tiles, tile, jnp.zeros((SUBLANES, width), F32))


LRU_BLOCKS_PER_STEP = 2
LRU_LANES = LRU_BLOCKS_PER_STEP * LANES
LRU_STEPS = N_BLK // LRU_BLOCKS_PER_STEP


def _lru_param_specs(cb_axis):
    def pick(*ids):
        return ids[cb_axis]

    vec = pl.BlockSpec((1, LRU_LANES), lambda *ids: (0, pick(*ids)))
    mat = pl.BlockSpec((LRU_BLOCKS_PER_STEP, LANES, LANES), lambda *ids: (pick(*ids), 0, 0))
    return [pl.BlockSpec((CONV_WIDTH, LRU_LANES), lambda *ids: (0, pick(*ids))), vec, mat, vec, mat, vec, vec]


def _branch_a_fwd(z, conv_w, conv_b, wx, bx, wa, ba, lam, batch, seq):
    tokens = batch * seq

    def body(z_ref, cw_ref, cb_ref, wx_ref, bx_ref, wa_ref, ba_ref, lam_ref, ya_ref, hl_ref, a_scr, u_scr):
        xa = z_ref[0]
        ga = z_ref[1]
        xc, gi, _, _, a, mult, _ = _lru_gates(xa, cw_ref, cb_ref, wx_ref, bx_ref, wa_ref, ba_ref, lam_ref)
        a_scr[...] = a
        u_scr[...] = mult * gi * xc
        _scan_forward(a_scr, u_scr, hl_ref, seq)
        ya_ref[...] = (hl_ref[...] * (ga * _sigmoid(ga))).astype(_MXU_DTYPE)

    blk = pl.BlockSpec((seq, LRU_LANES), lambda b, c: (b, c))
    return pl.pallas_call(
        body, name="branch_a_fwd",
        grid=(batch, LRU_STEPS),
        in_specs=[pl.BlockSpec((2, seq, LRU_LANES), lambda b, c: (2, b, c))] + _lru_param_specs(1),
        out_specs=[blk, blk],
        out_shape=[jax.ShapeDtypeStruct((tokens, D_MODEL), _MXU_DTYPE), jax.ShapeDtypeStruct((tokens, D_MODEL), F32)],
        scratch_shapes=[pltpu.VMEM((seq, LRU_LANES), F32), pltpu.VMEM((seq, LRU_LANES), F32)],
        compiler_params=_params(("parallel", "parallel")),
    )(z, conv_w, conv_b, wx, bx, wa, ba, lam)


def _branch_a_bwd(z, hl, dya, dz, conv_w, conv_b, wx, bx, wa, ba, lam, batch, seq):
    def body(z_ref, hl_ref, dya_ref, dz_in_ref, cw_ref, cb_ref, wx_ref, bx_ref, wa_ref, ba_ref, lam_ref,
             dz_ref, dcw_ref, dcb_ref, dwx_ref, dbx_ref, dwa_ref, dba_ref, dlam_ref, c_scr, d_scr, g_scr):
        del dz_in_ref
        xa = z_ref[0]
        ga = z_ref[1]
        hl = hl_ref[...]
        dya = dya_ref[...]
        xc, gi, gr, sp, a, mult, inv_mult = _lru_gates(xa, cw_ref, cb_ref, wx_ref, bx_ref, wa_ref, ba_ref, lam_ref)
        sga = _sigmoid(ga)
        dz_ref[1] = (dya * hl * (sga * (1.0 + ga * (1.0 - sga)))).astype(_MXU_DTYPE)
        c_scr[...] = _shift_up(a, 1)
        d_scr[...] = dya * (ga * sga)
        _scan_backward(c_scr, d_scr, g_scr, seq)
        g = g_scr[...]
        da = g * _shift_down(hl, 1)
        dmult = g * gi * xc
        dgi = g * mult * xc
        dxc = g * mult * gi
        dlog_a = da * a - dmult * (a * a) * inv_mult
        dgr = dlog_a * (-LRU_C) * sp
        dsp = jnp.sum(dlog_a * gr, axis=0, keepdims=True) * (-LRU_C)
        dlam = -dsp * _sigmoid(-lam_ref[...])
        dpi = dgi * gi * (1.0 - gi)
        dpr = dgr * gr * (1.0 - gr)
        dxc = dxc + _block_diag(dpi, wx_ref, transposed=True) + _block_diag(dpr, wa_ref, transposed=True)
        dwx = jnp.stack([_mm_tn(xk, dk) for xk, dk in zip(_lane_blocks(xc), _lane_blocks(dpi))])
        dwa = jnp.stack([_mm_tn(xk, dk) for xk, dk in zip(_lane_blocks(xc), _lane_blocks(dpr))])
        dbx = jnp.sum(dpi, axis=0, keepdims=True)
        dba = jnp.sum(dpr, axis=0, keepdims=True)
        ahead = [dxc if k == CONV_WIDTH - 1 else _shift_up(dxc, CONV_WIDTH - 1 - k) for k in range(CONV_WIDTH)]
        dxa = sum(cw_ref[k:k + 1, :] * ahead[k] for k in range(CONV_WIDTH))
        dz_ref[0] = dxa.astype(_MXU_DTYPE)
        dcb = jnp.sum(dxc, axis=0, keepdims=True)
        dcw = [jnp.sum(ahead[k] * xa, axis=0, keepdims=True) for k in range(CONV_WIDTH)]

        @pl.when(pl.program_id(1) == 0)
        def _():
            for k in range(CONV_WIDTH):
                dcw_ref[k:k + 1, :] = dcw[k]
            dcb_ref[...] = dcb
            dwx_ref[...] = dwx
            dbx_ref[...] = dbx
            dwa_ref[...] = dwa
            dba_ref[...] = dba
            dlam_ref[...] = dlam

        @pl.when(pl.program_id(1) != 0)
        def _():
            for k in range(CONV_WIDTH):
                dcw_ref[k:k + 1, :] += dcw[k]
            dcb_ref[...] += dcb
            dwx_ref[...] += dwx
            dbx_ref[...] += dbx
            dwa_ref[...] += dwa
            dba_ref[...] += dba
            dlam_ref[...] += dlam

    tokens = batch * seq
    blk = pl.BlockSpec((seq, LRU_LANES), lambda c, b: (b, c))
    vec = pl.BlockSpec((1, LRU_LANES), lambda c, b: (0, c))
    mat = pl.BlockSpec((LRU_BLOCKS_PER_STEP, LANES, LANES), lambda c, b: (c, 0, 0))
    vec_shape = jax.ShapeDtypeStruct((1, D_MODEL), F32)
    mat_shape = jax.ShapeDtypeStruct((N_BLK, LANES, LANES), F32)
    return pl.pallas_call(
        body, name="branch_a_bwd",
        grid=(LRU_STEPS, batch),
        in_specs=[pl.BlockSpec((2, seq, LRU_LANES), lambda c, b: (2, b, c)), blk, blk,
                  pl.BlockSpec(memory_space=pl.ANY)] + _lru_param_specs(0),
        out_specs=[pl.BlockSpec((2, seq, LRU_LANES), lambda c, b: (2, b, c)),
                   pl.BlockSpec((CONV_WIDTH, LRU_LANES), lambda c, b: (0, c)), vec, mat, vec, mat, vec, vec],
        out_shape=[jax.ShapeDtypeStruct((N_GROUPS, tokens, D_MODEL), _MXU_DTYPE),
                   jax.ShapeDtypeStruct((CONV_WIDTH, D_MODEL), F32), vec_shape, mat_shape, vec_shape, mat_shape,
                   vec_shape, vec_shape],
        scratch_shapes=[pltpu.VMEM((seq, LRU_LANES), F32)] * 3,
        input_output_aliases={3: 0},
        compiler_params=_params(("parallel", "arbitrary")),
    )(z, hl, dya, dz, conv_w, conv_b, wx, bx, wa, ba, lam)


def _chunk_masks(transposed=False):
    r = lax.broadcasted_iota(jnp.int32, (CHUNK, CHUNK), 0)
    c = lax.broadcasted_iota(jnp.int32, (CHUNK, CHUNK), 1)
    return r <= c if transposed else r >= c


def _row_blocks(seq, fn):
    block = min(256, seq)

    def trip(i, carry):
        fn(pl.ds(pl.multiple_of(i * block, block), block))
        return carry

    lax.fori_loop(0, seq // block, trip, 0)


def _hgrn_prepare(z_ref, lb_ref, f_scr, logf_scr, qh_scr, seq):
    lb = _sigmoid(lb_ref[0:1, :] - lb_ref[1:2, :])

    def block(rows):
        q = z_ref[0, rows, :]
        f = lb + (1.0 - lb) * _sigmoid(z_ref[1, rows, :])
        f_scr[rows, :] = f
        logf_scr[rows, :] = jnp.log(f)
        qh_scr[rows, :] = q * _sigmoid(q)

    _row_blocks(seq, block)
    return lb


def _cumsum_rows(x, reverse=False):
    shift = _shift_up if reverse else _shift_down
    d = 1
    while d < x.shape[0]:
        x = x + shift(x, d)
        d *= 2
    return x


def _lane_mean(x):
    return jnp.mean(x, axis=-1, keepdims=True)


def _token_contractions(lhs_scr, rhs_scr, out_ref, seq):
    rows_id = lax.broadcasted_iota(jnp.int32, (LANES, LANES), 0)

    def transposed(p):
        rows = pl.ds(pl.multiple_of(p * LANES, LANES), LANES)
        return jnp.transpose(lhs_scr[rows, :]).astype(_MXU_DTYPE), rhs_scr[rows, :]

    def contract(p, s):
        lhs_t, rhs = s
        return (_mm(lhs_t, jnp.where(rows_id < CHUNK, rhs, 0.0)), _mm(lhs_t, jnp.where(rows_id >= CHUNK, rhs, 0.0)))

    def store(p, out):
        out_ref[2 * p] = out[0]
        out_ref[2 * p + 1] = out[1]

    _independent_trips(seq // LANES, [transposed, contract], store)


def _chunk_rows(c):
    return pl.ds(pl.multiple_of(c * CHUNK, CHUNK), CHUNK)


def _chunk_terms(c, z_ref, f_scr, qh_scr, b_scr):
    rows = _chunk_rows(c)
    b = b_scr[rows, :]
    b_mid = b_scr[pl.ds(c * CHUNK + CHUNK // 2, 1), :]
    b_last = b_scr[pl.ds(c * CHUNK + CHUNK - 1, 1), :]
    qh = qh_scr[rows, :]
    k = 1.0 - f_scr[rows, :]
    v = z_ref[2, rows, :]
    e_q = jnp.exp(b - b_mid) * HG_SCALE
    e_k = jnp.exp(b_mid - b)
    e_qi = jnp.exp(b) * HG_SCALE
    e_ks = jnp.exp(b_last - b)
    decay = jnp.exp(b_last)
    return rows, qh, k, v, e_q, e_k, e_qi, e_ks, decay


def _independent_trips(n, stages, store, group=CHUNKS_IN_FLIGHT):
    stages = stages if isinstance(stages, (list, tuple)) else [stages]
    group = min(group, n)

    def trip(g, carry):
        ids = [g * group + i for i in range(group)]
        state = [stages[0](c) for c in ids]
        for stage in stages[1:]:
            state = [stage(c, s) for c, s in zip(ids, state)]
        for c, s in zip(ids, state):
            store(c, s)
        return carry

    lax.fori_loop(0, n // group, trip, 0)


def _branch_b_fwd(z, lb_logits, hg_g, batch, seq):
    tokens = batch * seq
    n_chunks = seq // CHUNK

    def body(z_ref, lb_ref, g_ref, yb_ref, st_ref, f_scr, logf_scr, qh_scr, b_scr, o_scr, qi_scr, ks_scr, dec_scr):
        _hgrn_prepare(z_ref, lb_ref, f_scr, logf_scr, qh_scr, seq)
        causal = _chunk_masks()
        gain = g_ref[...]

        def cumulate(c):
            return _cumsum_rows(logf_scr[_chunk_rows(c), :])

        def store_cumulated(c, b):
            b_scr[_chunk_rows(c), :] = b

        def scores(c):
            _, qh, k, v, e_q, e_k, e_qi, e_ks, decay = _chunk_terms(c, z_ref, f_scr, qh_scr, b_scr)
            return _mm_nt(qh * e_q, k * e_k), v, qh * e_qi, k * e_ks, decay

        def within_chunk(c, s):
            att, v, q_int, k_st, decay = s
            return _mm(jnp.where(causal, att, 0.0), v), q_int, k_st, decay

        def store_within_chunk(c, out):
            rows = _chunk_rows(c)
            o_scr[rows, :], qi_scr[rows, :], ks_scr[rows, :], dec_scr[pl.ds(c, 1), :] = out

        def carry_state(c, state_t):
            update = st_ref[c]
            st_ref[c] = state_t
            return state_t * dec_scr[pl.ds(c, 1), :] + update

        def finish(c):
            rows = _chunk_rows(c)
            o = o_scr[rows, :] + _mm_nt(qi_scr[rows, :], st_ref[c])
            r = lax.rsqrt(_lane_mean(o * o) + EPS)
            gb = z_ref[3, rows, :]
            return (((o * r) * gain) * (gb * _sigmoid(gb))).astype(_MXU_DTYPE)

        def store_finished(c, yb):
            yb_ref[_chunk_rows(c), :] = yb

        _independent_trips(n_chunks, cumulate, store_cumulated)
        _independent_trips(n_chunks, [scores, within_chunk], store_within_chunk)
        _token_contractions(z_ref.at[2], ks_scr, st_ref, seq)
        lax.fori_loop(0, n_chunks, carry_state, jnp.zeros((LANES, LANES), F32))
        _independent_trips(n_chunks, finish, store_finished)

    seq_buf = pltpu.VMEM((seq, LANES), F32)
    return pl.pallas_call(
        body, name="branch_b_fwd",
        grid=(batch, N_BLK),
        in_specs=[pl.BlockSpec((4, seq, LANES), lambda b, h: (0, b, h)),
                  pl.BlockSpec((2, LANES), lambda b, h: (0, h)),
                  pl.BlockSpec((1, LANES), lambda b, h: (0, 0))],
        out_specs=[pl.BlockSpec((seq, LANES), lambda b, h: (b, h)),
                   pl.BlockSpec((None, n_chunks, LANES, LANES), lambda b, h: (b * N_BLK + h, 0, 0, 0))],
        out_shape=[jax.ShapeDtypeStruct((tokens, D_MODEL), _MXU_DTYPE),
                   jax.ShapeDtypeStruct((batch * N_BLK, n_chunks, LANES, LANES), F32)],
        scratch_shapes=[seq_buf] * 7 + [pltpu.VMEM((n_chunks, LANES), F32)],
        compiler_params=_params(("parallel", "parallel")),
    )(z, lb_logits, hg_g)


def _branch_b_bwd(z, states, dyb, dz, lb_logits, hg_g, batch, seq):
    n_chunks = seq // CHUNK

    def body(z_ref, st_ref, dyb_ref, dz_in_ref, lb_ref, g_ref, dz_ref, dlog_ref, dg_ref,
             f_scr, logf_scr, qh_scr, b_scr, do_scr, qi_scr, dqh_scr, df_scr, dec_scr, dgp_scr, dlb_scr, dst_scr):
        del dz_in_ref
        first = (pl.program_id(0) == 0) & (pl.program_id(1) == 0)
        lb = _hgrn_prepare(z_ref, lb_ref, f_scr, logf_scr, qh_scr, seq)
        causal = _chunk_masks()
        anti_causal = _chunk_masks(transposed=True)
        gain = g_ref[...]

        @pl.when(first)
        def _():
            dg_ref[...] = jnp.zeros_like(dg_ref)

        @pl.when(pl.program_id(1) == 0)
        def _():
            dlb_scr[...] = jnp.zeros_like(dlb_scr)

        def cumulate(c):
            return _cumsum_rows(logf_scr[_chunk_rows(c), :])

        def store_cumulated(c, b):
            b_scr[_chunk_rows(c), :] = b

        def scores(c):
            _, qh, k, v, e_q, e_k, e_qi, e_ks, decay = _chunk_terms(c, z_ref, f_scr, qh_scr, b_scr)
            q_int = qh * e_qi
            return _mm_nt(qh * e_q, k * e_k), _mm_nt(q_int, st_ref[c]), v, q_int, decay

        def output_gradient(c, s):
            att, o_inter, v, q_int, decay = s
            rows = _chunk_rows(c)
            o = _mm(jnp.where(causal, att, 0.0), v) + o_inter
            r = lax.rsqrt(_lane_mean(o * o) + EPS)
            o_n = o * r
            gb = z_ref[3, rows, :]
            sgb = _sigmoid(gb)
            dyb_c = dyb_ref[rows, :]
            d_ong = dyb_c * (gb * sgb)
            d_gb = (dyb_c * (o_n * gain) * (sgb * (1.0 + gb * (1.0 - sgb)))).astype(_MXU_DTYPE)
            d_gain = jnp.sum(d_ong * o_n, axis=0, keepdims=True)
            d_on = d_ong * gain
            return d_gb, d_gain, r * (d_on - o_n * _lane_mean(d_on * o_n)), q_int, decay

        def store_output_gradient(c, out):
            rows = _chunk_rows(c)
            dz_ref[3, rows, :], dgp_scr[pl.ds(c, 1), :], do_scr[rows, :], qi_scr[rows, :], dec_scr[pl.ds(c, 1), :] = out

        def carry_state_gradient(cc, d_state_t):
            c = n_chunks - 1 - cc
            update = dst_scr[c]
            dst_scr[c] = d_state_t
            return d_state_t * dec_scr[pl.ds(c, 1), :] + update

        def score_gradients(c):
            rows, qh, k, v, e_q, e_k, e_qi, e_ks, decay = _chunk_terms(c, z_ref, f_scr, qh_scr, b_scr)
            state_t = st_ref[c]
            d_state_t = dst_scr[c]
            d_o = do_scr[rows, :]
            q_in, k_in, q_int, k_st = qh * e_q, k * e_k, qh * e_qi, k * e_ks
            first = (_mm_nt(k_in, q_in), _mm_nt(d_o, v), _mm_nt(v, d_o), _mm_nt(k_st, d_state_t), _mm(d_o, state_t),
                     _mm(v, d_state_t))
            d_decay = jnp.sum(state_t * d_state_t, axis=0, keepdims=True)
            return first, d_o, q_in, k_in, q_int, k_st, e_q, e_k, e_qi, e_ks, decay, d_decay

        def input_gradients(c, s):
            (att_t, d_att, d_att_t, dv_inter, dq_int, dk_st), d_o, q_in, k_in, q_int, k_st, e_q, e_k, e_qi, e_ks, decay, d_decay = s
            rows = _chunk_rows(c)
            d_v = _mm(jnp.where(anti_causal, att_t, 0.0), d_o) + dv_inter
            dq_in = _mm(jnp.where(causal, d_att, 0.0), k_in)
            dk_in = _mm(jnp.where(anti_causal, d_att_t, 0.0), q_in)
            d_k = dk_in * e_k + dk_st * e_ks
            kk = dk_st * k_st
            d_b = dq_in * q_in + dq_int * q_int - dk_in * k_in - kk
            d_b_last = jnp.sum(kk, axis=0, keepdims=True) + decay * d_decay
            d_logf = _cumsum_rows(d_b, reverse=True) + d_b_last
            return d_v.astype(_MXU_DTYPE), dq_in * e_q + dq_int * e_qi, d_logf / f_scr[rows, :] - d_k

        def store_input_gradients(c, out):
            rows = _chunk_rows(c)
            dz_ref[2, rows, :], dqh_scr[rows, :], df_scr[rows, :] = out

        def input_activations(rows):
            q = z_ref[0, rows, :]
            sq = _sigmoid(q)
            dz_ref[0, rows, :] = (dqh_scr[rows, :] * (sq * (1.0 + q * (1.0 - sq)))).astype(_MXU_DTYPE)
            sg = _sigmoid(z_ref[1, rows, :])
            d_f = df_scr[rows, :]
            dz_ref[1, rows, :] = (d_f * (1.0 - lb) * sg * (1.0 - sg)).astype(_MXU_DTYPE)
            dlb_scr[...] += jnp.sum(d_f * (1.0 - sg), axis=0, keepdims=True)

        _independent_trips(n_chunks, cumulate, store_cumulated)
        _independent_trips(n_chunks, [scores, output_gradient], store_output_gradient)
        _token_contractions(do_scr, qi_scr, dst_scr, seq)
        lax.fori_loop(0, n_chunks, carry_state_gradient, jnp.zeros((LANES, LANES), F32))
        _independent_trips(n_chunks, [score_gradients, input_gradients], store_input_gradients)
        dg_ref[...] += jnp.sum(dgp_scr[...], axis=0, keepdims=True)
        _row_blocks(seq, input_activations)
        d_l0 = dlb_scr[...] * lb * (1.0 - lb)
        dlog_ref[0:1, :] = d_l0
        dlog_ref[1:2, :] = -d_l0

    tokens = batch * seq
    seq_buf = pltpu.VMEM((seq, LANES), F32)
    chunk_rows = pltpu.VMEM((n_chunks, LANES), F32)
    return pl.pallas_call(
        body, name="branch_b_bwd",
        grid=(N_BLK, batch),
        in_specs=[pl.BlockSpec((4, seq, LANES), lambda h, b: (0, b, h)),
                  pl.BlockSpec((None, n_chunks, LANES, LANES), lambda h, b: (b * N_BLK + h, 0, 0, 0)),
                  pl.BlockSpec((seq, LANES), lambda h, b: (b, h)),
                  pl.BlockSpec(memory_space=pl.ANY),
                  pl.BlockSpec((2, LANES), lambda h, b: (0, h)),
                  pl.BlockSpec((1, LANES), lambda h, b: (0, 0))],
        out_specs=[pl.BlockSpec((4, seq, LANES), lambda h, b: (0, b, h)),
                   pl.BlockSpec((2, LANES), lambda h, b: (0, h)),
                   pl.BlockSpec((1, LANES), lambda h, b: (0, 0))],
        out_shape=[jax.ShapeDtypeStruct((N_GROUPS, tokens, D_MODEL), _MXU_DTYPE),
                   jax.ShapeDtypeStruct((2, D_MODEL), F32),
                   jax.ShapeDtypeStruct((1, LANES), F32)],
        scratch_shapes=[seq_buf] * 8 + [chunk_rows, chunk_rows, pltpu.VMEM((1, LANES), F32),
                                        pltpu.VMEM((n_chunks, LANES, LANES), F32)],
        input_output_aliases={3: 0},
        compiler_params=_params(("arbitrary", "arbitrary")),
    )(z, states, dyb, dz, lb_logits, hg_g)


def _merge_tail(ya, yb, z, x2d, tgt2d, b_merge, final_g, pa, pb, wo):
    tokens, d = x2d.shape
    tm = min(256, tokens)
    n_tiles = tokens // tm

    def body(ya_ref, yb_ref, z_ref, x_ref, t_ref, bm_ref, fg_ref, pa_hbm, pb_hbm, wo_hbm,
             dya_ref, dyb_ref, dx2_ref, dz_ref, loss_ref, dfg_ref, dbm_ref, dpa_hbm, dpb_hbm, dwo_hbm,
             pa_s, pb_s, wo_s, dpa_s, dpb_s, dwo_s):
        i = pl.program_id(0)

        @pl.when(i == 0)
        def _():
            pltpu.sync_copy(pa_hbm, pa_s)
            pltpu.sync_copy(pb_hbm, pb_s)
            pltpu.sync_copy(wo_hbm, wo_s)
            dpa_s[...] = jnp.zeros_like(dpa_s)
            dpb_s[...] = jnp.zeros_like(dpb_s)
            dwo_s[...] = jnp.zeros_like(dwo_s)
            loss_ref[...] = jnp.zeros_like(loss_ref)
            dfg_ref[...] = jnp.zeros_like(dfg_ref)
            dbm_ref[...] = jnp.zeros_like(dbm_ref)

        ya_t = ya_ref[...]
        yb_t = yb_ref[...]
        out_a = _mm(ya_t, pa_s[...])
        out_b = _mm(yb_t, pb_s[...])
        g_a = _sigmoid(z_ref[0] + bm_ref[:, :d])
        g_b = _sigmoid(z_ref[1] + bm_ref[:, d:])
        mixed = g_a * out_a + g_b * out_b
        x2 = x_ref[...] + _mm(mixed, wo_s[...])
        r = lax.rsqrt(jnp.mean(x2 * x2, axis=-1, keepdims=True) + EPS)
        xn = x2 * r
        fg = fg_ref[...]
        diff = xn * fg - t_ref[...]
        loss_ref[...] += jnp.sum(diff * diff) * (0.5 / d)
        dy = diff * (1.0 / d)
        dfg_ref[...] += jnp.sum(dy * xn, axis=0, keepdims=True)
        dxn = dy * fg
        dx2 = r * (dxn - xn * jnp.mean(dxn * xn, axis=-1, keepdims=True))
        dx2_ref[...] = dx2
        dmixed = _mm_nt(dx2, wo_s[...])
        dwo_s[...] += _mm_tn(mixed, dx2)
        dgm_a = dmixed * out_a * g_a * (1.0 - g_a)
        dgm_b = dmixed * out_b * g_b * (1.0 - g_b)
        dz_ref[0] = dgm_a.astype(_MXU_DTYPE)
        dz_ref[1] = dgm_b.astype(_MXU_DTYPE)
        dbm_ref[:, :d] += jnp.sum(dgm_a, axis=0, keepdims=True)
        dbm_ref[:, d:] += jnp.sum(dgm_b, axis=0, keepdims=True)
        dout_a = dmixed * g_a
        dout_b = dmixed * g_b
        dpa_s[...] += _mm_tn(ya_t, dout_a)
        dpb_s[...] += _mm_tn(yb_t, dout_b)
        dya_ref[...] = _mm_nt(dout_a, pa_s[...])
        dyb_ref[...] = _mm_nt(dout_b, pb_s[...])

        @pl.when(i == n_tiles - 1)
        def _():
            pltpu.sync_copy(dpa_s, dpa_hbm)
            pltpu.sync_copy(dpb_s, dpb_hbm)
            pltpu.sync_copy(dwo_s, dwo_hbm)

    tile = pl.BlockSpec((tm, d), lambda i: (i, 0))
    gm = pl.BlockSpec((2, tm, d), lambda i: (3, i, 0))
    row = lambda n: pl.BlockSpec((1, n), lambda i: (0, 0))
    hbm = pl.BlockSpec(memory_space=pl.ANY)
    act = jax.ShapeDtypeStruct((tokens, d), F32)
    mat = jax.ShapeDtypeStruct((d, d), F32)
    return pl.pallas_call(
        body, name="merge_tail",
        grid=(n_tiles,),
        in_specs=[tile, tile, gm, tile, tile, row(2 * d), row(d), hbm, hbm, hbm],
        out_specs=[tile, tile, tile, gm, row(LANES), row(d), row(2 * d), hbm, hbm, hbm],
        out_shape=[act, act, act, jax.ShapeDtypeStruct((N_GROUPS, tokens, d), _MXU_DTYPE),
                   jax.ShapeDtypeStruct((1, LANES), F32), jax.ShapeDtypeStruct((1, d), F32),
                   jax.ShapeDtypeStruct((1, 2 * d), F32), mat, mat, mat],
        scratch_shapes=[pltpu.VMEM((d, d), _MXU_DTYPE)] * 3 + [pltpu.VMEM((d, d), F32)] * 3,
        compiler_params=_params(("arbitrary",)),
    )(ya, yb, z, x2d, tgt2d, b_merge, final_g, pa, pb, wo)


def _inproj_dw(h_t, dz):
    d, tokens = h_t.shape
    tm = min(2048, tokens)

    def body(h_ref, dz_ref, dw_ref):
        part = _mm(h_ref[...], dz_ref[...])

        @pl.when(pl.program_id(1) == 0)
        def _():
            dw_ref[...] = part

        @pl.when(pl.program_id(1) != 0)
        def _():
            dw_ref[...] += part

    def out_index(s, i):
        g = _group_of_slot(s)
        return (g // 2, 0, g % 2)

    return pl.pallas_call(
        body, name="inproj_dw",
        grid=(N_GROUPS, tokens // tm),
        in_specs=[pl.BlockSpec((d, tm), lambda s, i: (0, i)),
                  pl.BlockSpec((None, tm, D_MODEL), lambda s, i: (s, i, 0))],
        out_specs=pl.BlockSpec((None, d, D_MODEL), out_index),
        out_shape=jax.ShapeDtypeStruct((N_SHARDS, d, 2 * D_MODEL), F32),
        compiler_params=_params(("parallel", "arbitrary")),
    )(h_t, dz)


def _inproj_dw_exchange(h_t, dz):
    d, tokens = h_t.shape
    tm = min(2048, tokens)
    n_i = tokens // tm
    half = d // 2

    def body(h_ref, dz_ref, dw_hbm, land_hbm, acc, local_sems, send_sems, recv_sems):
        s, i = pl.program_id(0), pl.program_id(1)
        x, y, c, _ = _mesh_position()
        part = _mm(h_ref[...], dz_ref[...])
        buf = acc.at[s % 2]

        @pl.when(i == 0)
        def _():
            buf[...] = part

        @pl.when(i != 0)
        def _():
            buf[...] += part

        def copies(k):
            g = _SLOT_TO_GROUP[k]
            cols = pl.ds((g % 2) * D_MODEL, D_MODEL)
            src = acc.at[k % 2]
            mine = pltpu.make_async_copy(src, dw_hbm.at[g // 2, :, cols], local_sems.at[k % 2])
            theirs = _remote(src.at[pl.ds((1 - c) * half, half), :], land_hbm.at[g // 2, :, cols],
                             send_sems, recv_sems, k, (x, y, 1 - c))
            return mine, theirs

        for k in range(N_GROUPS):
            @pl.when((s == k) & (i == n_i - 1))
            def _(k=k):
                if k > 0:
                    mine, theirs = copies(k - 1)
                    mine.wait()
                    theirs.wait_send()
                mine, theirs = copies(k)
                mine.start()
                theirs.start()
                if k == N_GROUPS - 1:
                    mine.wait()
                    theirs.wait_send()
                    for kk in range(N_GROUPS):
                        copies(kk)[1].wait_recv()

    hbm = pl.BlockSpec(memory_space=pl.ANY)
    return pl.pallas_call(
        body, name="inproj_dw_exchange",
        grid=(N_GROUPS, n_i),
        in_specs=[pl.BlockSpec((d, tm), lambda s, i: (0, i)),
                  pl.BlockSpec((None, tm, D_MODEL), lambda s, i: (s, i, 0))],
        out_specs=[hbm, hbm],
        out_shape=[jax.ShapeDtypeStruct((N_SHARDS, d, 2 * D_MODEL), F32),
                   jax.ShapeDtypeStruct((N_SHARDS, half, 2 * D_MODEL), F32)],
        scratch_shapes=[pltpu.VMEM((2, d, D_MODEL), F32), pltpu.SemaphoreType.DMA((2,)),
                        pltpu.SemaphoreType.DMA((N_GROUPS,)), pltpu.SemaphoreType.DMA((N_GROUPS,))],
        compiler_params=_params(("arbitrary", "arbitrary")),
    )(h_t, dz)


def _inproj_dx(dz, w_all, x2d, dx2, norm_g, scatter=None):
    tokens, d = x2d.shape
    tm = min(256, tokens)
    n_tiles = tokens // tm
    n_big = len(scatter[0]) if scatter else 0

    def body(dz_ref, w_hbm, x_ref, dx2_ref, g_ref, *rest):
        if scatter:
            srcs, small_src = rest[:n_big], rest[2 * n_big]
            dx_ref, dg_ref = rest[2 * n_big + 1:2 * n_big + 3]
            outs, small_out = rest[2 * n_big + 3:3 * n_big + 3], rest[3 * n_big + 3]
            w_res, send_sems, recv_sems, local_sem = rest[3 * n_big + 4:]
            copies = _scatter_copies(srcs, small_src, outs, small_out, send_sems, recv_sems, local_sem)
        else:
            dx_ref, dg_ref, w_res = rest
            copies = []
        i = pl.program_id(0)

        @pl.when(i == 0)
        def _():
            for cp in copies:
                cp.start()
            for slot, g in enumerate(_SLOT_TO_GROUP):
                pltpu.sync_copy(w_hbm.at[g // 2, :, pl.ds((g % 2) * D_MODEL, D_MODEL)],
                                w_res.at[:, pl.ds(slot * D_MODEL, D_MODEL)])
            dg_ref[...] = jnp.zeros_like(dg_ref)

        dz_all = jnp.concatenate([dz_ref[s] for s in range(N_GROUPS)], axis=1)
        dh = jnp.transpose(_mm_nt(w_res[...], dz_all))
        x = x_ref[...]
        r = lax.rsqrt(jnp.mean(x * x, axis=-1, keepdims=True) + EPS)
        xn = x * r
        dg_ref[...] += jnp.sum(dh * xn, axis=0, keepdims=True)
        dxn = dh * g_ref[...]
        dx_ref[...] = r * (dxn - xn * jnp.mean(dxn * xn, axis=-1, keepdims=True)) + dx2_ref[...]

        @pl.when(i == n_tiles - 1)
        def _():
            for cp in copies:
                cp.wait()

    tile = pl.BlockSpec((tm, d), lambda i: (i, 0))
    hbm = pl.BlockSpec(memory_space=pl.ANY)
    in_specs = [pl.BlockSpec((N_GROUPS, tm, D_MODEL), lambda i: (0, i, 0)), hbm, tile, tile,
                pl.BlockSpec((1, d), lambda i: (0, 0))]
    out_specs = [tile, pl.BlockSpec((1, d), lambda i: (0, 0))]
    out_shape = [jax.ShapeDtypeStruct((tokens, d), F32), jax.ShapeDtypeStruct((1, d), F32)]
    scratch = [pltpu.VMEM((d, N_GROUPS * D_MODEL), _MXU_DTYPE)]
    operands, aliases = [dz, w_all, x2d, dx2, norm_g], {}
    if scatter:
        bigs, by_chip, small_piece = scatter
        n_sem = 3 * n_big + N_DEV - 1
        in_specs += [hbm] * (2 * n_big + 1)
        out_specs += [hbm] * (n_big + 1)
        out_shape += [jax.ShapeDtypeStruct(g.shape, g.dtype) for g in by_chip]
        out_shape.append(jax.ShapeDtypeStruct((N_DEV, PIECE_ROWS, LANES), F32))
        scratch += [pltpu.SemaphoreType.DMA((n_sem,)), pltpu.SemaphoreType.DMA((n_sem,)), pltpu.SemaphoreType.DMA]
        operands += [*bigs, *by_chip, small_piece]
        aliases = {5 + n_big + a: 2 + a for a in range(n_big)}
    return pl.pallas_call(
        body, name="inproj_dx", grid=(n_tiles,), in_specs=in_specs, out_specs=out_specs, out_shape=out_shape,
        scratch_shapes=scratch, input_output_aliases=aliases,
        compiler_params=_params(("arbitrary",)),
    )(*operands)


def _row_tile(rows, cols, itemsize=4, budget=2 * 1024 * 1024):
    tr = rows
    while tr * cols * itemsize > budget and tr % 16 == 0:
        tr //= 2
    return tr


def _cast_into_slot(a, chip, dtype, name):
    rows, cols = a.shape
    tr = _row_tile(rows, cols)

    def body(chip_ref, a_ref, o_ref):
        del chip_ref
        o_ref[...] = a_ref[...].astype(dtype)

    grid_spec = pltpu.PrefetchScalarGridSpec(
        num_scalar_prefetch=1, grid=(rows // tr,),
        in_specs=[pl.BlockSpec((tr, cols), lambda i, chip_ref: (i, 0))],
        out_specs=pl.BlockSpec((None, tr, cols), lambda i, chip_ref: (chip_ref[0], i, 0)))
    return pl.pallas_call(body, name=name, grid_spec=grid_spec,
                          out_shape=jax.ShapeDtypeStruct((N_SHARDS, rows, cols), dtype),
                          compiler_params=_params(("arbitrary",)))(chip, a)


def _sum_slots(stack, name):
    n, rows, cols = stack.shape
    tr = _row_tile(rows, cols * n)

    def body(s_ref, o_ref):
        total = s_ref[0].astype(F32)
        for k in range(1, n):
            total = total + s_ref[k].astype(F32)
        o_ref[...] = total

    return pl.pallas_call(body, name=name, grid=(rows // tr,),
                          in_specs=[pl.BlockSpec((n, tr, cols), lambda i: (0, i, 0))],
                          out_specs=pl.BlockSpec((tr, cols), lambda i: (i, 0)),
                          out_shape=jax.ShapeDtypeStruct((rows, cols), F32),
                          compiler_params=_params(("parallel",)))(stack)


def _add_half(full, landed, place, name):
    n, rows, cols = full.shape
    half = rows // 2
    tr = _row_tile(half, cols)
    nb = half // tr

    def body(place_ref, a_ref, b_ref, o_ref, own_ref):
        total = (a_ref[...] + b_ref[...]).astype(_MXU_DTYPE)
        o_ref[...] = total

        @pl.when(pl.program_id(1) == place_ref[1])
        def _():
            own_ref[...] = total

    grid_spec = pltpu.PrefetchScalarGridSpec(
        num_scalar_prefetch=1, grid=(nb, n),
        in_specs=[pl.BlockSpec((None, tr, cols), lambda i, j, place_ref: (j, place_ref[0] * nb + i, 0)),
                  pl.BlockSpec((None, tr, cols), lambda i, j, place_ref: (j, i, 0))],
        out_specs=[pl.BlockSpec((None, tr, cols), lambda i, j, place_ref: (j, i, 0)),
                   pl.BlockSpec((None, tr, cols), lambda i, j, place_ref: (place_ref[1], i, 0))])
    shape = jax.ShapeDtypeStruct((n, half, cols), _MXU_DTYPE)
    return pl.pallas_call(body, name=name, grid_spec=grid_spec, out_shape=[shape, shape],
                          compiler_params=_params(("parallel", "arbitrary")))(place, full, landed)


def _adamw_update(w, grad, m, v):
    c1 = 1.0 - ADAM_B1 ** ADAM_STEP
    c2 = 1.0 - ADAM_B2 ** ADAM_STEP
    nm = ADAM_B1 * m + (1.0 - ADAM_B1) * grad
    nv = ADAM_B2 * v + (1.0 - ADAM_B2) * (grad * grad)
    return (-ADAM_LR) * ((nm / c1) / (jnp.sqrt(nv / c2) + ADAM_EPS) + ADAM_WD * w), nm, nv


def _adamw(w, g, m, v, name):
    rows, cols = w.shape
    tr = _row_tile(rows, cols, budget=1024 * 1024)

    def body(w_ref, g_ref, m_ref, v_ref, d_ref, nm_ref, nv_ref):
        d_ref[...], nm_ref[...], nv_ref[...] = _adamw_update(w_ref[...], g_ref[...], m_ref[...], v_ref[...])

    spec = pl.BlockSpec((tr, cols), lambda i: (i, 0))
    shape = jax.ShapeDtypeStruct((rows, cols), F32)
    return pl.pallas_call(body, name=name, grid=(rows // tr,), in_specs=[spec] * 4, out_specs=[spec] * 3,
                          out_shape=[shape] * 3, compiler_params=_params(("parallel",)))(w, g, m, v)


def _adamw_halves(w, g_mine, g_sibling, m, v, core, name):
    rows, cols = w.shape
    half = rows // 2
    tr = _row_tile(half, cols, budget=1024 * 1024)
    nb = half // tr

    def body(core_ref, w_ref, gm_ref, gs_ref, m_ref, v_ref, g_ref, d_ref, nm_ref, nv_ref):
        mine = pl.program_id(0) // nb == core_ref[0]
        grad = jnp.where(mine, gm_ref[...], gs_ref[...])
        g_ref[...] = grad
        d_ref[...], nm_ref[...], nv_ref[...] = _adamw_update(w_ref[...], grad, m_ref[...], v_ref[...])

    spec = pl.BlockSpec((tr, cols), lambda i, core_ref: (i, 0))
    half_spec = pl.BlockSpec((tr, cols), lambda i, core_ref: (i % nb, 0))
    grid_spec = pltpu.PrefetchScalarGridSpec(num_scalar_prefetch=1, grid=(rows // tr,),
                                             in_specs=[spec, half_spec, half_spec, spec, spec], out_specs=[spec] * 4)
    shape = jax.ShapeDtypeStruct((rows, cols), F32)
    return pl.pallas_call(body, name=name, grid_spec=grid_spec, out_shape=[shape] * 4,
                          compiler_params=_params(("parallel",)))(core, w, g_mine, g_sibling, m, v)


def _local_step(x, loss_target, w_all, pa, pb, wo, conv_w, b_merge, conv_b, rg_wx, rg_bx, rg_wa, rg_ba,
                rg_lambda, hg_lb_logits, hg_norm_g, norm_g, final_norm_g, gather=None, start_reduction=None):
    batch, seq, d = x.shape
    x2d = x.reshape(batch * seq, d)
    tgt2d = loss_target.reshape(batch * seq, d)
    if gather is None:
        z, h_t = _inproj_fwd(x2d, norm_g, w_all)
    else:
        z, h_t, (w_all, pa, pb, wo), cw_all = _inproj_fwd_gather(x2d, norm_g, *gather)
        pa, pb, wo = (t.reshape(d, d) for t in (pa, pb, wo))
        conv_w = jnp.transpose(cw_all, (1, 0, 2)).reshape(CONV_WIDTH, d)
    lru = (conv_w, conv_b, rg_wx, rg_bx, rg_wa, rg_ba, rg_lambda)
    ya, hl = _branch_a_fwd(z, *lru, batch, seq)
    yb, states = _branch_b_fwd(z, hg_lb_logits, hg_norm_g, batch, seq)
    dya, dyb, dx2, dz, loss, d_final_g, d_b_merge, d_pa, d_pb, d_wo = _merge_tail(
        ya, yb, z, x2d, tgt2d, b_merge, final_norm_g, pa, pb, wo)
    dz, d_lb_logits, d_hg_g = _branch_b_bwd(z, states, dyb, dz, hg_lb_logits, hg_norm_g, batch, seq)
    dz, d_conv_w, d_conv_b, d_wx, d_bx, d_wa, d_ba, d_lam = _branch_a_bwd(z, hl, dya, dz, *lru, batch, seq)
    if start_reduction is None:
        d_w_in, landed_w_in = _inproj_dw(h_t, dz), None
    else:
        d_w_in, landed_w_in = _inproj_dw_exchange(h_t, dz)
    big = (d_w_in, d_pa, d_pb, d_wo)
    small = dict(b_merge=d_b_merge, conv_w=d_conv_w, conv_b=d_conv_b, rg_wx=d_wx, rg_bx=d_bx, rg_wa=d_wa,
                 rg_ba=d_ba, rg_lambda=d_lam, hg_lb_logits=d_lb_logits, hg_norm_g=d_hg_g,
                 norm_g=jnp.zeros((1, d), F32), final_norm_g=d_final_g)
    if start_reduction is None:
        grad_x, small["norm_g"] = _inproj_dx(dz, w_all, x2d, dx2, norm_g)
        return loss[0, 0], grad_x.reshape(batch, seq, d), big, small
    scatter = start_reduction(big, small, landed_w_in)
    grad_x, d_norm_g, *scattered = _inproj_dx(dz, w_all, x2d, dx2, norm_g, scatter=scatter)
    return loss[0, 0], grad_x.reshape(batch, seq, d), d_norm_g, scattered


_SMALL_ORDER = ("b_merge", "conv_w", "conv_b", "rg_wx", "rg_bx", "rg_wa", "rg_ba", "rg_lambda", "hg_lb_logits",
                "hg_norm_g", "norm_g", "final_norm_g")
N_DEV = 8
PIECE_ROWS = 272


def _pack_small(tree):
    flat = jnp.concatenate([tree[k].reshape(-1) for k in _SMALL_ORDER])
    flat = jnp.pad(flat, (0, N_DEV * PIECE_ROWS * LANES - flat.shape[0]))
    return flat.reshape(N_DEV * PIECE_ROWS, LANES)


def _unpack_small(packed, like):
    flat = packed.reshape(-1)
    out, pos = {}, 0
    for k in _SMALL_ORDER:
        n = like[k].size
        out[k] = flat[pos:pos + n].reshape(like[k].shape)
        pos += n
    return out


def _mesh_position():
    x, y, c = lax.axis_index("x"), lax.axis_index("y"), lax.axis_index("c")
    other_chips = [(1 - x, y), (x, 1 - y), (1 - x, 1 - y)]
    return x, y, c, other_chips


def _other_devices(x, y, c):
    flips = [(fx, fy, fc) for fx in (0, 1) for fy in (0, 1) for fc in (0, 1) if (fx, fy, fc) != (0, 0, 0)]
    return [(jnp.where(fx, 1 - x, x), jnp.where(fy, 1 - y, y), jnp.where(fc, 1 - c, c)) for fx, fy, fc in flips]


def _remote(src, dst, send_sems, recv_sems, k, device):
    return pltpu.make_async_remote_copy(src_ref=src, dst_ref=dst, send_sem=send_sems.at[k], recv_sem=recv_sems.at[k],
                                        device_id=device, device_id_type=MESH)


def _exchange_halves(bigs, small):
    n_big = len(bigs)
    n_sem = n_big + N_DEV - 1

    def body(*refs):
        srcs, small_src = refs[:n_big], refs[n_big]
        outs, small_out = refs[n_big + 1:2 * n_big + 1], refs[2 * n_big + 1]
        send_sems, recv_sems, local_sem = refs[2 * n_big + 2:]
        x, y, c, _ = _mesh_position()
        me, sibling = 4 * x + 2 * y + c, (x, y, 1 - c)
        mine = pltpu.make_async_copy(small_src.at[pl.ds(me * PIECE_ROWS, PIECE_ROWS), :], small_out.at[me], local_sem)
        mine.start()
        copies = []
        for a in range(n_big):
            hs = srcs[a].shape[1] // 2
            copies.append(_remote(srcs[a].at[:, pl.ds((1 - c) * hs, hs), :], outs[a], send_sems, recv_sems, a, sibling))
        for k, (px, py, pc) in enumerate(_other_devices(x, y, c)):
            piece = small_src.at[pl.ds((4 * px + 2 * py + pc) * PIECE_ROWS, PIECE_ROWS), :]
            copies.append(_remote(piece, small_out.at[me], send_sems, recv_sems, n_big + k, (px, py, pc)))
        for cp in copies:
            cp.start()
        for cp in copies:
            cp.wait()
        mine.wait()

    hbm = pl.BlockSpec(memory_space=pl.ANY)
    out_shape = [jax.ShapeDtypeStruct((g.shape[0], g.shape[1] // 2, g.shape[2]), F32) for g in bigs]
    out_shape.append(jax.ShapeDtypeStruct((N_DEV, PIECE_ROWS, LANES), F32))
    return pl.pallas_call(
        body, name="exchange_halves",
        in_specs=[hbm] * (n_big + 1), out_specs=[hbm] * (n_big + 1), out_shape=out_shape,
        scratch_shapes=[pltpu.SemaphoreType.DMA((n_sem,)), pltpu.SemaphoreType.DMA((n_sem,)), pltpu.SemaphoreType.DMA],
    )(*bigs, small)


def _scatter_copies(srcs, small_src, outs, small_out, send_sems, recv_sems, local_sem):
    n_big = len(srcs)
    x, y, c, chips = _mesh_position()
    chip, me = 2 * x + y, 4 * x + 2 * y + c
    copies = [pltpu.make_async_copy(small_src, small_out.at[me], local_sem)]
    for a in range(n_big):
        for j, (cx, cy) in enumerate(chips):
            copies.append(_remote(srcs[a].at[2 * cx + cy], outs[a].at[chip], send_sems, recv_sems, 3 * a + j, (cx, cy, c)))
    for k, peer in enumerate(_other_devices(x, y, c)):
        copies.append(_remote(small_src, small_out.at[me], send_sems, recv_sems, 3 * n_big + k, peer))
    return copies


def _swap_halves(halves, vec):
    n_big = len(halves)

    def body(*refs):
        srcs, vec_src = refs[:n_big], refs[n_big]
        outs, vec_out = refs[n_big + 1:2 * n_big + 1], refs[2 * n_big + 1]
        send_sems, recv_sems, local_sem = refs[2 * n_big + 2:]
        x, y, c, _ = _mesh_position()
        me = 4 * x + 2 * y + c
        copies = [pltpu.make_async_copy(vec_src, vec_out.at[me], local_sem)]
        copies += [_remote(srcs[a], outs[a], send_sems, recv_sems, a, (x, y, 1 - c)) for a in range(n_big)]
        copies += [_remote(vec_src, vec_out.at[me], send_sems, recv_sems, n_big + k, peer)
                   for k, peer in enumerate(_other_devices(x, y, c))]
        for cp in copies:
            cp.start()
        for cp in copies:
            cp.wait()

    hbm = pl.BlockSpec(memory_space=pl.ANY)
    n_sem = n_big + N_DEV - 1
    return pl.pallas_call(
        body, name="swap_halves",
        in_specs=[hbm] * (n_big + 1), out_specs=[hbm] * (n_big + 1),
        out_shape=[jax.ShapeDtypeStruct(h.shape, F32) for h in halves] + [jax.ShapeDtypeStruct((N_DEV,) + vec.shape, F32)],
        scratch_shapes=[pltpu.SemaphoreType.DMA((n_sem,)), pltpu.SemaphoreType.DMA((n_sem,)), pltpu.SemaphoreType.DMA],
    )(*halves, vec)


def kernel(x, w_in, b_merge, conv_w, conv_b, rg_wx, rg_bx, rg_wa, rg_ba, rg_lambda, hg_lb_logits, hg_norm_g, proj_a, proj_b, w_out, norm_g, final_norm_g, loss_target, m_w_in, m_b_merge, m_conv_w, m_conv_b, m_rg_wx, m_rg_bx, m_rg_wa, m_rg_ba, m_rg_lambda, m_hg_lb_logits, m_hg_norm_g, m_proj_a, m_proj_b, m_w_out, m_norm_g, m_final_norm_g, v_w_in, v_b_merge, v_conv_w, v_conv_b, v_rg_wx, v_rg_bx, v_rg_wa, v_rg_ba, v_rg_lambda, v_hg_lb_logits, v_hg_norm_g, v_proj_a, v_proj_b, v_w_out, v_norm_g, v_final_norm_g):
    d = D_MODEL
    weights = dict(w_in=w_in, b_merge=b_merge, conv_w=conv_w, conv_b=conv_b, rg_wx=rg_wx, rg_bx=rg_bx, rg_wa=rg_wa,
                   rg_ba=rg_ba, rg_lambda=rg_lambda, hg_lb_logits=hg_lb_logits, hg_norm_g=hg_norm_g, proj_a=proj_a,
                   proj_b=proj_b, w_out=w_out, norm_g=norm_g, final_norm_g=final_norm_g)
    m = dict(w_in=m_w_in, b_merge=m_b_merge, conv_w=m_conv_w, conv_b=m_conv_b, rg_wx=m_rg_wx, rg_bx=m_rg_bx,
             rg_wa=m_rg_wa, rg_ba=m_rg_ba, rg_lambda=m_rg_lambda, hg_lb_logits=m_hg_lb_logits, hg_norm_g=m_hg_norm_g,
             proj_a=m_proj_a, proj_b=m_proj_b, w_out=m_w_out, norm_g=m_norm_g, final_norm_g=m_final_norm_g)
    v = dict(w_in=v_w_in, b_merge=v_b_merge, conv_w=v_conv_w, conv_b=v_conv_b, rg_wx=v_rg_wx, rg_bx=v_rg_bx,
             rg_wa=v_rg_wa, rg_ba=v_rg_ba, rg_lambda=v_rg_lambda, hg_lb_logits=v_hg_lb_logits, hg_norm_g=v_hg_norm_g,
             proj_a=v_proj_a, proj_b=v_proj_b, w_out=v_w_out, norm_g=v_norm_g, final_norm_g=v_final_norm_g)
    big_names = ("w_in", "proj_a", "proj_b", "w_out")

    core = lax.axis_index("c").astype(jnp.int32).reshape(1)
    chip = (2 * lax.axis_index("x") + lax.axis_index("y")).astype(jnp.int32)

    slotted = [_cast_into_slot(weights[k][0], chip.reshape(1), _MXU_DTYPE, f"cast_{k}") for k in big_names]
    conv_slotted = _cast_into_slot(conv_w[0], chip.reshape(1), F32, "slot_conv_w")

    small_shapes = {}

    def start_reduction(big_grads, small_grads, landed_w_in):
        small_shapes.update({k: t.shape for k, t in small_grads.items()})
        bigs = [big_grads[0]] + [g.reshape(N_SHARDS, d // N_SHARDS, d) for g in big_grads[1:]]
        *landed, small_landed = _exchange_halves(bigs[1:], _pack_small(small_grads))
        landed = [landed_w_in] + landed
        place = jnp.concatenate([core, chip.reshape(1)])
        sums = [_add_half(g, l, place, f"add_half_{a}") for a, (g, l) in enumerate(zip(bigs, landed))]
        return [s[0] for s in sums], [s[1] for s in sums], _sum_slots(small_landed, "sum_small")

    loss_part, grad_x, d_norm_g, (*by_chip, small_all) = _local_step(
        x, loss_target, None, None, None, None, None,
        b_merge, conv_b, rg_wx[0], rg_bx.reshape(1, d), rg_wa[0], rg_ba.reshape(1, d), rg_lambda, hg_lb_logits,
        hg_norm_g, norm_g, final_norm_g.reshape(1, d), gather=(slotted, conv_slotted, chip.reshape(1)),
        start_reduction=start_reduction)
    mine = [_sum_slots(s, f"sum_chips_{a}") for a, s in enumerate(by_chip)]
    late = jnp.concatenate([d_norm_g.reshape(SUBLANES, LANES), jnp.full((SUBLANES, LANES), loss_part, F32)])
    *theirs, late_parts = _swap_halves(mine, late)
    late_sum = _sum_slots(late_parts, "sum_late")
    loss = late_sum[SUBLANES, 0]
    small_red = _unpack_small(small_all, {k: jax.ShapeDtypeStruct(s, F32) for k, s in small_shapes.items()})
    small_red["norm_g"] = late_sum[:SUBLANES].reshape(1, d)

    grads, delta, new_m, new_v = {}, {}, {}, {}
    for k, g_mine, g_theirs in zip(big_names, mine, theirs):
        out = _adamw_halves(weights[k][0], g_mine, g_theirs, m[k][0], v[k][0], core, f"adamw_{k}")
        grads[k], delta[k], new_m[k], new_v[k] = (t.reshape(weights[k].shape) for t in out)
    cols = d // N_SHARDS
    g_conv = lax.dynamic_slice(small_red["conv_w"], (0, chip * cols), (CONV_WIDTH, cols))
    grads["conv_w"] = g_conv.reshape(conv_w.shape)
    dl, nm, nv = _adamw(conv_w[0], g_conv, m_conv_w[0], v_conv_w[0], "adamw_conv_w")
    delta["conv_w"], new_m["conv_w"], new_v["conv_w"] = (t.reshape(conv_w.shape) for t in (dl, nm, nv))
    rest = [k for k in _SMALL_ORDER if k != "conv_w"]
    like = {k: (weights[k] if k != "conv_w" else jnp.zeros((CONV_WIDTH, d), F32)) for k in _SMALL_ORDER}
    packs = [_pack_small({k: (t[k] if k != "conv_w" else like[k]) for k in _SMALL_ORDER}) for t in (weights, m, v)]
    g_pack = _pack_small({k: small_red[k].reshape(like[k].shape) for k in _SMALL_ORDER})
    outs = [_unpack_small(p, like) for p in _adamw(packs[0], g_pack, packs[1], packs[2], "adamw_small")]
    for k in rest:
        grads[k] = small_red[k].reshape(weights[k].shape)
        delta[k], new_m[k], new_v[k] = outs[0][k], outs[1][k], outs[2][k]

    order = ("w_in", "b_merge", "conv_w", "conv_b", "rg_wx", "rg_bx", "rg_wa", "rg_ba", "rg_lambda", "hg_lb_logits",
             "hg_norm_g", "proj_a", "proj_b", "w_out", "norm_g", "final_norm_g")
    return (loss, grad_x, *[grads[k] for k in order], *[delta[k] for k in order], *[new_m[k] for k in order],
            *[new_v[k] for k in order])
```

```python
import functools

import jax
import jax.numpy as jnp
from jax import lax
from jax.experimental import pallas as pl
from jax.experimental.pallas import tpu as pltpu

F32 = jnp.float32
_MXU_DTYPE = jnp.bfloat16

D_MODEL = 1024
LANES = 128
SUBLANES = 8
N_BLK = D_MODEL // LANES
N_GROUPS = 8
N_SHARDS = 4
CONV_WIDTH = 4
LRU_C = 8.0
CHUNK = 64
CHUNKS_IN_FLIGHT = 16
HG_SCALE = float(LANES) ** -0.5
EPS = 1e-6
ADAM_LR, ADAM_B1, ADAM_B2, ADAM_EPS, ADAM_WD, ADAM_STEP = 0.001, 0.9, 0.999, 1e-08, 0.01, 10
VMEM_LIMIT = 56 * 1024 * 1024
MESH = pl.DeviceIdType.MESH

_SLOT_TO_GROUP = (2, 3, 4, 5, 0, 1, 6, 7)


def _group_of_slot(s):
    return jnp.where(s < 4, s + 2, jnp.where(s < 6, s - 4, s))


def _mm(a, b):
    return lax.dot_general(a.astype(_MXU_DTYPE), b.astype(_MXU_DTYPE), (((1,), (0,)), ((), ())),
                           preferred_element_type=F32)


def _mm_nt(a, b):
    return lax.dot_general(a.astype(_MXU_DTYPE), b.astype(_MXU_DTYPE), (((1,), (1,)), ((), ())),
                           preferred_element_type=F32)


def _mm_tn(a, b):
    return lax.dot_general(a.astype(_MXU_DTYPE), b.astype(_MXU_DTYPE), (((0,), (0,)), ((), ())),
                           preferred_element_type=F32)


def _sigmoid(x):
    return 0.5 * jnp.tanh(0.5 * x) + 0.5


def _log1p_pos(y):
    series = y * (1.0 - y * (0.5 - y * (1.0 / 3.0 - y * 0.25)))
    return jnp.where(y < 0.01, series, jnp.log(1.0 + y))


def _softplus(x):
    return jnp.maximum(x, 0.0) + _log1p_pos(jnp.exp(-jnp.abs(x)))


def _shift_down(x, n):
    rolled = pltpu.roll(x, n, 0)
    edge = SUBLANES if (n < SUBLANES and x.shape[0] > SUBLANES) else x.shape[0]
    rows = lax.broadcasted_iota(jnp.int32, (edge, x.shape[1]), 0)
    head = jnp.where(rows >= n, rolled[:edge], 0.0)
    return head if edge == x.shape[0] else jnp.concatenate([head, rolled[edge:]], axis=0)


def _shift_up(x, n):
    size = x.shape[0]
    rolled = pltpu.roll(x, size - n, 0)
    edge = SUBLANES if (n < SUBLANES and size > SUBLANES) else size
    rows = lax.broadcasted_iota(jnp.int32, (edge, x.shape[1]), 0)
    tail = jnp.where(rows < edge - n, rolled[size - edge:], 0.0)
    return tail if edge == size else jnp.concatenate([rolled[:size - edge], tail], axis=0)


def _params(dims, vmem=VMEM_LIMIT):
    return pltpu.CompilerParams(dimension_semantics=dims, vmem_limit_bytes=vmem)


def _load_w_in_by_slot(w_hbm, w_res):
    for slot, g in enumerate(_SLOT_TO_GROUP):
        pltpu.sync_copy(w_hbm.at[g // 2, :, pl.ds((g % 2) * D_MODEL, D_MODEL)], w_res.at[slot])


def _inproj_fwd(x2d, norm_g, w_all):
    tokens, d = x2d.shape
    tm = min(512, tokens)

    def body(x_ref, g_ref, w_hbm, z_ref, ht_ref, h_scr, w_res):
        @pl.when((pl.program_id(0) == 0) & (pl.program_id(1) == 0))
        def _():
            _load_w_in_by_slot(w_hbm, w_res)

        @pl.when(pl.program_id(1) == 0)
        def _():
            x = x_ref[...]
            r = lax.rsqrt(jnp.mean(x * x, axis=-1, keepdims=True) + EPS)
            h = (x * r) * g_ref[...]
            h_scr[...] = h.astype(_MXU_DTYPE)
            ht_ref[...] = jnp.transpose(h).astype(_MXU_DTYPE)

        z_ref[...] = _mm(h_scr[...], w_res[pl.program_id(1)])

    return pl.pallas_call(
        body, name="inproj_fwd",
        grid=(tokens // tm, N_GROUPS),
        in_specs=[pl.BlockSpec((tm, d), lambda i, s: (i, 0)),
                  pl.BlockSpec((1, d), lambda i, s: (0, 0)),
                  pl.BlockSpec(memory_space=pl.ANY)],
        out_specs=[pl.BlockSpec((None, tm, D_MODEL), lambda i, s: (s, i, 0)),
                   pl.BlockSpec((d, tm), lambda i, s: (0, i))],
        out_shape=[jax.ShapeDtypeStruct((N_GROUPS, tokens, D_MODEL), F32),
                   jax.ShapeDtypeStruct((d, tokens), _MXU_DTYPE)],
        scratch_shapes=[pltpu.VMEM((tm, d), _MXU_DTYPE), pltpu.VMEM((N_GROUPS, d, D_MODEL), _MXU_DTYPE)],
        compiler_params=_params(("arbitrary", "arbitrary")),
    )(x2d, norm_g, w_all)


def _slot_of_group(g):
    return jnp.where(g < 2, g + 4, jnp.where(g < 6, g - 2, g))


def _inproj_fwd_gather(x2d, norm_g, slotted, conv_slotted, chip):
    tokens, d = x2d.shape
    tm = min(512, tokens)
    n_tiles = tokens // tm
    n_big = len(slotted)
    n_sem = 6 * n_big + 3
    last_pass = N_GROUPS - 1

    def shard_of(k, chip_id):
        x, y = chip_id // 2, chip_id % 2
        return 2 * jnp.where(k % 2 == 1, 1 - x, x) + jnp.where(k // 2 == 1, 1 - y, y)

    def body(chip_ref, x_ref, g_ref, *rest):
        bufs, cw = rest[n_big + 1:2 * n_big + 1], rest[2 * n_big + 1]
        z_ref, ht_ref = rest[2 * n_big + 2:2 * n_big + 4]
        h_all, slab, send_sems, recv_sems, slab_sem = rest[2 * n_big + 4:]
        del chip_ref
        p, i = pl.program_id(0), pl.program_id(1)
        x, y, c, chips = _mesh_position()
        me, sibling = 2 * x + y, (x, y, 1 - c)

        def half(a, slot, which):
            hs = bufs[a].shape[1] // 2
            return bufs[a].at[slot, pl.ds(which * hs, hs), :]

        def send(a, j):
            mine = half(a, me, c)
            return _remote(mine, mine, send_sems, recv_sems, 6 * a + j, (chips[j][0], chips[j][1], c))

        def arrival(a, j):
            landed = half(a, 2 * chips[j][0] + chips[j][1], c)
            return _remote(landed, landed, send_sems, recv_sems, 6 * a + j, (chips[j][0], chips[j][1], c))

        def passed_on(a, j, which):
            landed = half(a, 2 * chips[j][0] + chips[j][1], which)
            return _remote(landed, landed, send_sems, recv_sems, 6 * a + 3 + j, sibling)

        def conv_copy(j, slot):
            return _remote(cw.at[slot], cw.at[slot], send_sems, recv_sems, 6 * n_big + j, (chips[j][0], chips[j][1], c))

        def land(a, j):
            arrival(a, j).wait_recv()
            passed_on(a, j, c).start()
            passed_on(a, j, 1 - c).wait_recv()

        @pl.when((p == 0) & (i == 0))
        def _():
            send(0, 0).start()
            send(0, 1).start()

        for j in range(3):
            @pl.when((p == 2 * (j + 1)) & (i == 0))
            def _(j=j):
                land(0, j)
                if j == 0:
                    send(0, 2).start()
                if j == 1:
                    for a in range(1, n_big):
                        for jj in range(3):
                            send(a, jj).start()
                    for jj in range(3):
                        conv_copy(jj, me).start()

        @pl.when(i == 0)
        def _():
            shard = shard_of(p // 2, me)
            for which in range(2):
                @pl.when(p % 2 == which)
                def _(which=which):
                    cp = pltpu.make_async_copy(bufs[0].at[shard, :, pl.ds(which * D_MODEL, D_MODEL)], slab, slab_sem)
                    cp.start()
                    cp.wait()

        rows = pl.ds(pl.multiple_of(i * tm, tm), tm)

        @pl.when(p == 0)
        def _():
            xt = x_ref[...]
            r = lax.rsqrt(jnp.mean(xt * xt, axis=-1, keepdims=True) + EPS)
            h = (xt * r) * g_ref[...]
            h_all[rows, :] = h.astype(_MXU_DTYPE)
            ht_ref[...] = jnp.transpose(h).astype(_MXU_DTYPE)

        z_ref[...] = _mm(h_all[rows, :], slab[...])

        @pl.when((p == last_pass) & (i == n_tiles - 1))
        def _():
            for a in range(1, n_big):
                for j in range(3):
                    land(a, j)
            for j in range(3):
                conv_copy(j, 2 * chips[j][0] + chips[j][1]).wait_recv()
            for a in range(n_big):
                for j in range(3):
                    send(a, j).wait_send()
                    passed_on(a, j, c).wait_send()
            for j in range(3):
                conv_copy(j, me).wait_send()

    def z_index(p, i, chip_ref):
        g = 2 * shard_of(p // 2, chip_ref[0]) + p % 2
        return (_slot_of_group(g), i, 0)

    def first_pass_tile(p, i, chip_ref):
        return jnp.where(p == 0, i, n_tiles - 1)

    hbm = pl.BlockSpec(memory_space=pl.ANY)
    operands = list(slotted) + [conv_slotted]
    grid_spec = pltpu.PrefetchScalarGridSpec(
        num_scalar_prefetch=1, grid=(N_GROUPS, n_tiles),
        in_specs=[pl.BlockSpec((tm, d), lambda p, i, chip_ref: (first_pass_tile(p, i, chip_ref), 0)),
                  pl.BlockSpec((1, d), lambda p, i, chip_ref: (0, 0))] + [hbm] * (n_big + 1),
        out_specs=[hbm] * (n_big + 1) + [pl.BlockSpec((None, tm, D_MODEL), z_index),
                                         pl.BlockSpec((d, tm), lambda p, i, chip_ref: (0, first_pass_tile(p, i, chip_ref)))],
        scratch_shapes=[pltpu.VMEM((tokens, d), _MXU_DTYPE), pltpu.VMEM((d, D_MODEL), _MXU_DTYPE),
                        pltpu.SemaphoreType.DMA((n_sem,)), pltpu.SemaphoreType.DMA((n_sem,)), pltpu.SemaphoreType.DMA])
    out = pl.pallas_call(
        body, name="inproj_fwd_gather", grid_spec=grid_spec,
        out_shape=[jax.ShapeDtypeStruct(a.shape, a.dtype) for a in operands]
        + [jax.ShapeDtypeStruct((N_GROUPS, tokens, D_MODEL), F32), jax.ShapeDtypeStruct((d, tokens), _MXU_DTYPE)],
        input_output_aliases={3 + a: a for a in range(n_big + 1)},
        compiler_params=_params(("arbitrary", "arbitrary")),
    )(chip, x2d, norm_g, *operands)
    return out[n_big + 1], out[n_big + 2], out[:n_big], out[n_big]


def _lane_blocks(x):
    return [x[:, k * LANES:(k + 1) * LANES] for k in range(x.shape[1] // LANES)]


def _block_diag(x, w_ref, transposed=False):
    mm = _mm_nt if transposed else _mm
    return jnp.concatenate([mm(xk, w_ref[k]) for k, xk in enumerate(_lane_blocks(x))], axis=1)


def _lru_gates(xa, cw_ref, cb_ref, wx_ref, bx_ref, wa_ref, ba_ref, lam_ref):
    xc = (cb_ref[...] + cw_ref[3:4, :] * xa + cw_ref[2:3, :] * _shift_down(xa, 1)
          + cw_ref[1:2, :] * _shift_down(xa, 2) + cw_ref[0:1, :] * _shift_down(xa, 3))
    gi = _sigmoid(_block_diag(xc, wx_ref) + bx_ref[...])
    gr = _sigmoid(_block_diag(xc, wa_ref) + ba_ref[...])
    sp = _softplus(-lam_ref[...])
    log_a = (-LRU_C) * gr * sp
    a = jnp.exp(log_a)
    y = 2.0 * log_a
    mult_sq = jnp.where(y > -1e-3, -y * (1.0 + 0.5 * y), 1.0 - a * a)
    inv_mult = lax.rsqrt(jnp.maximum(mult_sq, 1e-37))
    return xc, gi, gr, sp, a, mult_sq * inv_mult, inv_mult


def _tile_rows(width):
    return lax.broadcasted_iota(jnp.int32, (SUBLANES, width), 0)


def _scan_forward(a_scr, u_scr, h_scr, seq):
    width = a_scr.shape[1]
    rows = _tile_rows(width)

    def tile(j, carry):
        sl = pl.ds(pl.multiple_of(j * SUBLANES, SUBLANES), SUBLANES)
        a = a_scr[sl, :]
        u = u_scr[sl, :]
        for d in (1, 2, 4):
            keep = rows >= d
            a_sh = jnp.where(keep, pltpu.roll(a, d, 0), 1.0)
            u_sh = jnp.where(keep, pltpu.roll(u, d, 0), 0.0)
            u = a * u_sh + u
            a = a * a_sh
        h = u + a * carry
        h_scr[sl, :] = h
        return jnp.broadcast_to(h[SUBLANES - 1:SUBLANES, :], (SUBLANES, width))

    lax.fori_loop(0, seq // SUBLANES, tile, jnp.zeros((SUBLANES, width), F32))


def _scan_backward(c_scr, d_scr, g_scr, seq):
    width = c_scr.shape[1]
    rows = _tile_rows(width)
    n_tiles = seq // SUBLANES

    def tile(jj, carry):
        j = n_tiles - 1 - jj
        sl = pl.ds(pl.multiple_of(j * SUBLANES, SUBLANES), SUBLANES)
        c = c_scr[sl, :]
        g = d_scr[sl, :]
        for d in (1, 2, 4):
            keep = rows < SUBLANES - d
            c_sh = jnp.where(keep, pltpu.roll(c, SUBLANES - d, 0), 1.0)
            g_sh = jnp.where(keep, pltpu.roll(g, SUBLANES - d, 0), 0.0)
            g = c * g_sh + g
            c = c * c_sh
        g = g + c * carry
        g_scr[sl, :] = g
        return jnp.broadcast_to(g[0:1, :], (SUBLANES, width))

    lax.fori_loop(0, n_tiles, tile, jnp.zeros((SUBLANES, width), F32))


LRU_BLOCKS_PER_STEP = 2
LRU_LANES = LRU_BLOCKS_PER_STEP * LANES
LRU_STEPS = N_BLK // LRU_BLOCKS_PER_STEP


def _lru_param_specs(cb_axis):
    def pick(*ids):
        return ids[cb_axis]

    vec = pl.BlockSpec((1, LRU_LANES), lambda *ids: (0, pick(*ids)))
    mat = pl.BlockSpec((LRU_BLOCKS_PER_STEP, LANES, LANES), lambda *ids: (pick(*ids), 0, 0))
    return [pl.BlockSpec((CONV_WIDTH, LRU_LANES), lambda *ids: (0, pick(*ids))), vec, mat, vec, mat, vec, vec]


def _branch_a_fwd(z, conv_w, conv_b, wx, bx, wa, ba, lam, batch, seq):
    tokens = batch * seq

    def body(z_ref, cw_ref, cb_ref, wx_ref, bx_ref, wa_ref, ba_ref, lam_ref, ya_ref, hl_ref, a_scr, u_scr):
        xa = z_ref[0]
        ga = z_ref[1]
        xc, gi, _, _, a, mult, _ = _lru_gates(xa, cw_ref, cb_ref, wx_ref, bx_ref, wa_ref, ba_ref, lam_ref)
        a_scr[...] = a
        u_scr[...] = mult * gi * xc
        _scan_forward(a_scr, u_scr, hl_ref, seq)
        ya_ref[...] = (hl_ref[...] * (ga * _sigmoid(ga))).astype(_MXU_DTYPE)

    blk = pl.BlockSpec((seq, LRU_LANES), lambda b, c: (b, c))
    return pl.pallas_call(
        body, name="branch_a_fwd",
        grid=(batch, LRU_STEPS),
        in_specs=[pl.BlockSpec((2, seq, LRU_LANES), lambda b, c: (2, b, c))] + _lru_param_specs(1),
        out_specs=[blk, blk],
        out_shape=[jax.ShapeDtypeStruct((tokens, D_MODEL), _MXU_DTYPE), jax.ShapeDtypeStruct((tokens, D_MODEL), F32)],
        scratch_shapes=[pltpu.VMEM((seq, LRU_LANES), F32), pltpu.VMEM((seq, LRU_LANES), F32)],
        compiler_params=_params(("parallel", "parallel")),
    )(z, conv_w, conv_b, wx, bx, wa, ba, lam)


def _branch_a_bwd(z, hl, dya, dz, conv_w, conv_b, wx, bx, wa, ba, lam, batch, seq):
    def body(z_ref, hl_ref, dya_ref, dz_in_ref, cw_ref, cb_ref, wx_ref, bx_ref, wa_ref, ba_ref, lam_ref,
             dz_ref, dcw_ref, dcb_ref, dwx_ref, dbx_ref, dwa_ref, dba_ref, dlam_ref, c_scr, d_scr, g_scr):
        del dz_in_ref
        xa = z_ref[0]
        ga = z_ref[1]
        hl = hl_ref[...]
        dya = dya_ref[...]
        xc, gi, gr, sp, a, mult, inv_mult = _lru_gates(xa, cw_ref, cb_ref, wx_ref, bx_ref, wa_ref, ba_ref, lam_ref)
        sga = _sigmoid(ga)
        dz_ref[1] = (dya * hl * (sga * (1.0 + ga * (1.0 - sga)))).astype(_MXU_DTYPE)
        c_scr[...] = _shift_up(a, 1)
        d_scr[...] = dya * (ga * sga)
        _scan_backward(c_scr, d_scr, g_scr, seq)
        g = g_scr[...]
        da = g * _shift_down(hl, 1)
        dmult = g * gi * xc
        dgi = g * mult * xc
        dxc = g * mult * gi
        dlog_a = da * a - dmult * (a * a) * inv_mult
        dgr = dlog_a * (-LRU_C) * sp
        dsp = jnp.sum(dlog_a * gr, axis=0, keepdims=True) * (-LRU_C)
        dlam = -dsp * _sigmoid(-lam_ref[...])
        dpi = dgi * gi * (1.0 - gi)
        dpr = dgr * gr * (1.0 - gr)
        dxc = dxc + _block_diag(dpi, wx_ref, transposed=True) + _block_diag(dpr, wa_ref, transposed=True)
        dwx = jnp.stack([_mm_tn(xk, dk) for xk, dk in zip(_lane_blocks(xc), _lane_blocks(dpi))])
        dwa = jnp.stack([_mm_tn(xk, dk) for xk, dk in zip(_lane_blocks(xc), _lane_blocks(dpr))])
        dbx = jnp.sum(dpi, axis=0, keepdims=True)
        dba = jnp.sum(dpr, axis=0, keepdims=True)
        ahead = [dxc if k == CONV_WIDTH - 1 else _shift_up(dxc, CONV_WIDTH - 1 - k) for k in range(CONV_WIDTH)]
        dxa = sum(cw_ref[k:k + 1, :] * ahead[k] for k in range(CONV_WIDTH))
        dz_ref[0] = dxa.astype(_MXU_DTYPE)
        dcb = jnp.sum(dxc, axis=0, keepdims=True)
        dcw = [jnp.sum(ahead[k] * xa, axis=0, keepdims=True) for k in range(CONV_WIDTH)]

        @pl.when(pl.program_id(1) == 0)
        def _():
            for k in range(CONV_WIDTH):
                dcw_ref[k:k + 1, :] = dcw[k]
            dcb_ref[...] = dcb
            dwx_ref[...] = dwx
            dbx_ref[...] = dbx
            dwa_ref[...] = dwa
            dba_ref[...] = dba
            dlam_ref[...] = dlam

        @pl.when(pl.program_id(1) != 0)
        def _():
            for k in range(CONV_WIDTH):
                dcw_ref[k:k + 1, :] += dcw[k]
            dcb_ref[...] += dcb
            dwx_ref[...] += dwx
            dbx_ref[...] += dbx
            dwa_ref[...] += dwa
            dba_ref[...] += dba
            dlam_ref[...] += dlam

    tokens = batch * seq
    blk = pl.BlockSpec((seq, LRU_LANES), lambda c, b: (b, c))
    vec = pl.BlockSpec((1, LRU_LANES), lambda c, b: (0, c))
    mat = pl.BlockSpec((LRU_BLOCKS_PER_STEP, LANES, LANES), lambda c, b: (c, 0, 0))
    vec_shape = jax.ShapeDtypeStruct((1, D_MODEL), F32)
    mat_shape = jax.ShapeDtypeStruct((N_BLK, LANES, LANES), F32)
    return pl.pallas_call(
        body, name="branch_a_bwd",
        grid=(LRU_STEPS, batch),
        in_specs=[pl.BlockSpec((2, seq, LRU_LANES), lambda c, b: (2, b, c)), blk, blk,
                  pl.BlockSpec(memory_space=pl.ANY)] + _lru_param_specs(0),
        out_specs=[pl.BlockSpec((2, seq, LRU_LANES), lambda c, b: (2, b, c)),
                   pl.BlockSpec((CONV_WIDTH, LRU_LANES), lambda c, b: (0, c)), vec, mat, vec, mat, vec, vec],
        out_shape=[jax.ShapeDtypeStruct((N_GROUPS, tokens, D_MODEL), _MXU_DTYPE),
                   jax.ShapeDtypeStruct((CONV_WIDTH, D_MODEL), F32), vec_shape, mat_shape, vec_shape, mat_shape,
                   vec_shape, vec_shape],
        scratch_shapes=[pltpu.VMEM((seq, LRU_LANES), F32)] * 3,
        input_output_aliases={3: 0},
        compiler_params=_params(("parallel", "arbitrary")),
    )(z, hl, dya, dz, conv_w, conv_b, wx, bx, wa, ba, lam)


def _chunk_masks(transposed=False):
    r = lax.broadcasted_iota(jnp.int32, (CHUNK, CHUNK), 0)
    c = lax.broadcasted_iota(jnp.int32, (CHUNK, CHUNK), 1)
    return r <= c if transposed else r >= c


def _row_blocks(seq, fn):
    block = min(256, seq)

    def trip(i, carry):
        fn(pl.ds(pl.multiple_of(i * block, block), block))
        return carry

    lax.fori_loop(0, seq // block, trip, 0)


def _hgrn_prepare(z_ref, lb_ref, f_scr, logf_scr, qh_scr, seq):
    lb = _sigmoid(lb_ref[0:1, :] - lb_ref[1:2, :])

    def block(rows):
        q = z_ref[0, rows, :]
        f = lb + (1.0 - lb) * _sigmoid(z_ref[1, rows, :])
        f_scr[rows, :] = f
        logf_scr[rows, :] = jnp.log(f)
        qh_scr[rows, :] = q * _sigmoid(q)

    _row_blocks(seq, block)
    return lb


def _cumsum_rows(x, reverse=False):
    shift = _shift_up if reverse else _shift_down
    d = 1
    while d < x.shape[0]:
        x = x + shift(x, d)
        d *= 2
    return x


def _lane_mean(x):
    return jnp.mean(x, axis=-1, keepdims=True)


def _token_contractions(lhs_scr, rhs_scr, out_ref, seq):
    rows_id = lax.broadcasted_iota(jnp.int32, (LANES, LANES), 0)

    def transposed(p):
        rows = pl.ds(pl.multiple_of(p * LANES, LANES), LANES)
        return jnp.transpose(lhs_scr[rows, :]).astype(_MXU_DTYPE), rhs_scr[rows, :]

    def contract(p, s):
        lhs_t, rhs = s
        return (_mm(lhs_t, jnp.where(rows_id < CHUNK, rhs, 0.0)), _mm(lhs_t, jnp.where(rows_id >= CHUNK, rhs, 0.0)))

    def store(p, out):
        out_ref[2 * p] = out[0]
        out_ref[2 * p + 1] = out[1]

    _independent_trips(seq // LANES, [transposed, contract], store)


def _chunk_rows(c):
    return pl.ds(pl.multiple_of(c * CHUNK, CHUNK), CHUNK)


def _chunk_terms(c, z_ref, f_scr, qh_scr, b_scr):
    rows = _chunk_rows(c)
    b = b_scr[rows, :]
    b_mid = b_scr[pl.ds(c * CHUNK + CHUNK // 2, 1), :]
    b_last = b_scr[pl.ds(c * CHUNK + CHUNK - 1, 1), :]
    qh = qh_scr[rows, :]
    k = 1.0 - f_scr[rows, :]
    v = z_ref[2, rows, :]
    e_q = jnp.exp(b - b_mid) * HG_SCALE
    e_k = jnp.exp(b_mid - b)
    e_qi = jnp.exp(b) * HG_SCALE
    e_ks = jnp.exp(b_last - b)
    decay = jnp.exp(b_last)
    return rows, qh, k, v, e_q, e_k, e_qi, e_ks, decay


def _independent_trips(n, stages, store, group=CHUNKS_IN_FLIGHT):
    stages = stages if isinstance(stages, (list, tuple)) else [stages]
    group = min(group, n)

    def trip(g, carry):
        ids = [g * group + i for i in range(group)]
        state = [stages[0](c) for c in ids]
        for stage in stages[1:]:
            state = [stage(c, s) for c, s in zip(ids, state)]
        for c, s in zip(ids, state):
            store(c, s)
        return carry

    lax.fori_loop(0, n // group, trip, 0)


def _branch_b_fwd(z, lb_logits, hg_g, batch, seq):
    tokens = batch * seq
    n_chunks = seq // CHUNK

    def body(z_ref, lb_ref, g_ref, yb_ref, st_ref, f_scr, logf_scr, qh_scr, b_scr, o_scr, qi_scr, ks_scr, dec_scr):
        _hgrn_prepare(z_ref, lb_ref, f_scr, logf_scr, qh_scr, seq)
        causal = _chunk_masks()
        gain = g_ref[...]

        def cumulate(c):
            return _cumsum_rows(logf_scr[_chunk_rows(c), :])

        def store_cumulated(c, b):
            b_scr[_chunk_rows(c), :] = b

        def scores(c):
            _, qh, k, v, e_q, e_k, e_qi, e_ks, decay = _chunk_terms(c, z_ref, f_scr, qh_scr, b_scr)
            return _mm_nt(qh * e_q, k * e_k), v, qh * e_qi, k * e_ks, decay

        def within_chunk(c, s):
            att, v, q_int, k_st, decay = s
            return _mm(jnp.where(causal, att, 0.0), v), q_int, k_st, decay

        def store_within_chunk(c, out):
            rows = _chunk_rows(c)
            o_scr[rows, :], qi_scr[rows, :], ks_scr[rows, :], dec_scr[pl.ds(c, 1), :] = out

        def carry_state(c, state_t):
            update = st_ref[c]
            st_ref[c] = state_t
            return state_t * dec_scr[pl.ds(c, 1), :] + update

        def finish(c):
            rows = _chunk_rows(c)
            o = o_scr[rows, :] + _mm_nt(qi_scr[rows, :], st_ref[c])
            r = lax.rsqrt(_lane_mean(o * o) + EPS)
            gb = z_ref[3, rows, :]
            return (((o * r) * gain) * (gb * _sigmoid(gb))).astype(_MXU_DTYPE)

        def store_finished(c, yb):
            yb_ref[_chunk_rows(c), :] = yb

        _independent_trips(n_chunks, cumulate, store_cumulated)
        _independent_trips(n_chunks, [scores, within_chunk], store_within_chunk)
        _token_contractions(z_ref.at[2], ks_scr, st_ref, seq)
        lax.fori_loop(0, n_chunks, carry_state, jnp.zeros((LANES, LANES), F32))
        _independent_trips(n_chunks, finish, store_finished)

    seq_buf = pltpu.VMEM((seq, LANES), F32)
    return pl.pallas_call(
        body, name="branch_b_fwd",
        grid=(batch, N_BLK),
        in_specs=[pl.BlockSpec((4, seq, LANES), lambda b, h: (0, b, h)),
                  pl.BlockSpec((2, LANES), lambda b, h: (0, h)),
                  pl.BlockSpec((1, LANES), lambda b, h: (0, 0))],
        out_specs=[pl.BlockSpec((seq, LANES), lambda b, h: (b, h)),
                   pl.BlockSpec((None, n_chunks, LANES, LANES), lambda b, h: (b * N_BLK + h, 0, 0, 0))],
        out_shape=[jax.ShapeDtypeStruct((tokens, D_MODEL), _MXU_DTYPE),
                   jax.ShapeDtypeStruct((batch * N_BLK, n_chunks, LANES, LANES), F32)],
        scratch_shapes=[seq_buf] * 7 + [pltpu.VMEM((n_chunks, LANES), F32)],
        compiler_params=_params(("parallel", "parallel")),
    )(z, lb_logits, hg_g)


def _branch_b_bwd(z, states, dyb, dz, lb_logits, hg_g, batch, seq):
    n_chunks = seq // CHUNK

    def body(z_ref, st_ref, dyb_ref, dz_in_ref, lb_ref, g_ref, dz_ref, dlog_ref, dg_ref,
             f_scr, logf_scr, qh_scr, b_scr, do_scr, qi_scr, dqh_scr, df_scr, dec_scr, dgp_scr, dlb_scr, dst_scr):
        del dz_in_ref
        first = (pl.program_id(0) == 0) & (pl.program_id(1) == 0)
        lb = _hgrn_prepare(z_ref, lb_ref, f_scr, logf_scr, qh_scr, seq)
        causal = _chunk_masks()
        anti_causal = _chunk_masks(transposed=True)
        gain = g_ref[...]

        @pl.when(first)
        def _():
            dg_ref[...] = jnp.zeros_like(dg_ref)

        @pl.when(pl.program_id(1) == 0)
        def _():
            dlb_scr[...] = jnp.zeros_like(dlb_scr)

        def cumulate(c):
            return _cumsum_rows(logf_scr[_chunk_rows(c), :])

        def store_cumulated(c, b):
            b_scr[_chunk_rows(c), :] = b

        def scores(c):
            _, qh, k, v, e_q, e_k, e_qi, e_ks, decay = _chunk_terms(c, z_ref, f_scr, qh_scr, b_scr)
            q_int = qh * e_qi
            return _mm_nt(qh * e_q, k * e_k), _mm_nt(q_int, st_ref[c]), v, q_int, decay

        def output_gradient(c, s):
            att, o_inter, v, q_int, decay = s
            rows = _chunk_rows(c)
            o = _mm(jnp.where(causal, att, 0.0), v) + o_inter
            r = lax.rsqrt(_lane_mean(o * o) + EPS)
            o_n = o * r
            gb = z_ref[3, rows, :]
            sgb = _sigmoid(gb)
            dyb_c = dyb_ref[rows, :]
            d_ong = dyb_c * (gb * sgb)
            d_gb = (dyb_c * (o_n * gain) * (sgb * (1.0 + gb * (1.0 - sgb)))).astype(_MXU_DTYPE)
            d_gain = jnp.sum(d_ong * o_n, axis=0, keepdims=True)
            d_on = d_ong * gain
            return d_gb, d_gain, r * (d_on - o_n * _lane_mean(d_on * o_n)), q_int, decay

        def store_output_gradient(c, out):
            rows = _chunk_rows(c)
            dz_ref[3, rows, :], dgp_scr[pl.ds(c, 1), :], do_scr[rows, :], qi_scr[rows, :], dec_scr[pl.ds(c, 1), :] = out

        def carry_state_gradient(cc, d_state_t):
            c = n_chunks - 1 - cc
            update = dst_scr[c]
            dst_scr[c] = d_state_t
            return d_state_t * dec_scr[pl.ds(c, 1), :] + update

        def score_gradients(c):
            rows, qh, k, v, e_q, e_k, e_qi, e_ks, decay = _chunk_terms(c, z_ref, f_scr, qh_scr, b_scr)
            state_t = st_ref[c]
            d_state_t = dst_scr[c]
            d_o = do_scr[rows, :]
            q_in, k_in, q_int, k_st = qh * e_q, k * e_k, qh * e_qi, k * e_ks
            first = (_mm_nt(k_in, q_in), _mm_nt(d_o, v), _mm_nt(v, d_o), _mm_nt(k_st, d_state_t), _mm(d_o, state_t),
                     _mm(v, d_state_t))
            d_decay = jnp.sum(state_t * d_state_t, axis=0, keepdims=True)
            return first, d_o, q_in, k_in, q_int, k_st, e_q, e_k, e_qi, e_ks, decay, d_decay

        def input_gradients(c, s):
            (att_t, d_att, d_att_t, dv_inter, dq_int, dk_st), d_o, q_in, k_in, q_int, k_st, e_q, e_k, e_qi, e_ks, decay, d_decay = s
            rows = _chunk_rows(c)
            d_v = _mm(jnp.where(anti_causal, att_t, 0.0), d_o) + dv_inter
            dq_in = _mm(jnp.where(causal, d_att, 0.0), k_in)
            dk_in = _mm(jnp.where(anti_causal, d_att_t, 0.0), q_in)
            d_k = dk_in * e_k + dk_st * e_ks
            kk = dk_st * k_st
            d_b = dq_in * q_in + dq_int * q_int - dk_in * k_in - kk
            d_b_last = jnp.sum(kk, axis=0, keepdims=True) + decay * d_decay
            d_logf = _cumsum_rows(d_b, reverse=True) + d_b_last
            return d_v.astype(_MXU_DTYPE), dq_in * e_q + dq_int * e_qi, d_logf / f_scr[rows, :] - d_k

        def store_input_gradients(c, out):
            rows = _chunk_rows(c)
            dz_ref[2, rows, :], dqh_scr[rows, :], df_scr[rows, :] = out

        def input_activations(rows):
            q = z_ref[0, rows, :]
            sq = _sigmoid(q)
            dz_ref[0, rows, :] = (dqh_scr[rows, :] * (sq * (1.0 + q * (1.0 - sq)))).astype(_MXU_DTYPE)
            sg = _sigmoid(z_ref[1, rows, :])
            d_f = df_scr[rows, :]
            dz_ref[1, rows, :] = (d_f * (1.0 - lb) * sg * (1.0 - sg)).astype(_MXU_DTYPE)
            dlb_scr[...] += jnp.sum(d_f * (1.0 - sg), axis=0, keepdims=True)

        _independent_trips(n_chunks, cumulate, store_cumulated)
        _independent_trips(n_chunks, [scores, output_gradient], store_output_gradient)
        _token_contractions(do_scr, qi_scr, dst_scr, seq)
        lax.fori_loop(0, n_chunks, carry_state_gradient, jnp.zeros((LANES, LANES), F32))
        _independent_trips(n_chunks, [score_gradients, input_gradients], store_input_gradients)
        dg_ref[...] += jnp.sum(dgp_scr[...], axis=0, keepdims=True)
        _row_blocks(seq, input_activations)
        d_l0 = dlb_scr[...] * lb * (1.0 - lb)
        dlog_ref[0:1, :] = d_l0
        dlog_ref[1:2, :] = -d_l0

    tokens = batch * seq
    seq_buf = pltpu.VMEM((seq, LANES), F32)
    chunk_rows = pltpu.VMEM((n_chunks, LANES), F32)
    return pl.pallas_call(
        body, name="branch_b_bwd",
        grid=(N_BLK, batch),
        in_specs=[pl.BlockSpec((4, seq, LANES), lambda h, b: (0, b, h)),
                  pl.BlockSpec((None, n_chunks, LANES, LANES), lambda h, b: (b * N_BLK + h, 0, 0, 0)),
                  pl.BlockSpec((seq, LANES), lambda h, b: (b, h)),
                  pl.BlockSpec(memory_space=pl.ANY),
                  pl.BlockSpec((2, LANES), lambda h, b: (0, h)),
                  pl.BlockSpec((1, LANES), lambda h, b: (0, 0))],
        out_specs=[pl.BlockSpec((4, seq, LANES), lambda h, b: (0, b, h)),
                   pl.BlockSpec((2, LANES), lambda h, b: (0, h)),
                   pl.BlockSpec((1, LANES), lambda h, b: (0, 0))],
        out_shape=[jax.ShapeDtypeStruct((N_GROUPS, tokens, D_MODEL), _MXU_DTYPE),
                   jax.ShapeDtypeStruct((2, D_MODEL), F32),
                   jax.ShapeDtypeStruct((1, LANES), F32)],
        scratch_shapes=[seq_buf] * 8 + [chunk_rows, chunk_rows, pltpu.VMEM((1, LANES), F32),
                                        pltpu.VMEM((n_chunks, LANES, LANES), F32)],
        input_output_aliases={3: 0},
        compiler_params=_params(("arbitrary", "arbitrary")),
    )(z, states, dyb, dz, lb_logits, hg_g)


def _merge_tail(ya, yb, z, x2d, tgt2d, b_merge, final_g, pa, pb, wo):
    tokens, d = x2d.shape
    tm = min(256, tokens)
    n_tiles = tokens // tm

    def body(ya_ref, yb_ref, z_ref, x_ref, t_ref, bm_ref, fg_ref, pa_hbm, pb_hbm, wo_hbm,
             dya_ref, dyb_ref, dx2_ref, dz_ref, loss_ref, dfg_ref, dbm_ref, dpa_hbm, dpb_hbm, dwo_hbm,
             pa_s, pb_s, wo_s, dpa_s, dpb_s, dwo_s):
        i = pl.program_id(0)

        @pl.when(i == 0)
        def _():
            pltpu.sync_copy(pa_hbm, pa_s)
            pltpu.sync_copy(pb_hbm, pb_s)
            pltpu.sync_copy(wo_hbm, wo_s)
            dpa_s[...] = jnp.zeros_like(dpa_s)
            dpb_s[...] = jnp.zeros_like(dpb_s)
            dwo_s[...] = jnp.zeros_like(dwo_s)
            loss_ref[...] = jnp.zeros_like(loss_ref)
            dfg_ref[...] = jnp.zeros_like(dfg_ref)
            dbm_ref[...] = jnp.zeros_like(dbm_ref)

        ya_t = ya_ref[...]
        yb_t = yb_ref[...]
        out_a = _mm(ya_t, pa_s[...])
        out_b = _mm(yb_t, pb_s[...])
        g_a = _sigmoid(z_ref[0] + bm_ref[:, :d])
        g_b = _sigmoid(z_ref[1] + bm_ref[:, d:])
        mixed = g_a * out_a + g_b * out_b
        x2 = x_ref[...] + _mm(mixed, wo_s[...])
        r = lax.rsqrt(jnp.mean(x2 * x2, axis=-1, keepdims=True) + EPS)
        xn = x2 * r
        fg = fg_ref[...]
        diff = xn * fg - t_ref[...]
        loss_ref[...] += jnp.sum(diff * diff) * (0.5 / d)
        dy = diff * (1.0 / d)
        dfg_ref[...] += jnp.sum(dy * xn, axis=0, keepdims=True)
        dxn = dy * fg
        dx2 = r * (dxn - xn * jnp.mean(dxn * xn, axis=-1, keepdims=True))
        dx2_ref[...] = dx2
        dmixed = _mm_nt(dx2, wo_s[...])
        dwo_s[...] += _mm_tn(mixed, dx2)
        dgm_a = dmixed * out_a * g_a * (1.0 - g_a)
        dgm_b = dmixed * out_b * g_b * (1.0 - g_b)
        dz_ref[0] = dgm_a.astype(_MXU_DTYPE)
        dz_ref[1] = dgm_b.astype(_MXU_DTYPE)
        dbm_ref[:, :d] += jnp.sum(dgm_a, axis=0, keepdims=True)
        dbm_ref[:, d:] += jnp.sum(dgm_b, axis=0, keepdims=True)
        dout_a = dmixed * g_a
        dout_b = dmixed * g_b
        dpa_s[...] += _mm_tn(ya_t, dout_a)
        dpb_s[...] += _mm_tn(yb_t, dout_b)
        dya_ref[...] = _mm_nt(dout_a, pa_s[...])
        dyb_ref[...] = _mm_nt(dout_b, pb_s[...])

        @pl.when(i == n_tiles - 1)
        def _():
            pltpu.sync_copy(dpa_s, dpa_hbm)
            pltpu.sync_copy(dpb_s, dpb_hbm)
            pltpu.sync_copy(dwo_s, dwo_hbm)

    tile = pl.BlockSpec((tm, d), lambda i: (i, 0))
    gm = pl.BlockSpec((2, tm, d), lambda i: (3, i, 0))
    row = lambda n: pl.BlockSpec((1, n), lambda i: (0, 0))
    hbm = pl.BlockSpec(memory_space=pl.ANY)
    act = jax.ShapeDtypeStruct((tokens, d), F32)
    mat = jax.ShapeDtypeStruct((d, d), F32)
    return pl.pallas_call(
        body, name="merge_tail",
        grid=(n_tiles,),
        in_specs=[tile, tile, gm, tile, tile, row(2 * d), row(d), hbm, hbm, hbm],
        out_specs=[tile, tile, tile, gm, row(LANES), row(d), row(2 * d), hbm, hbm, hbm],
        out_shape=[act, act, act, jax.ShapeDtypeStruct((N_GROUPS, tokens, d), _MXU_DTYPE),
                   jax.ShapeDtypeStruct((1, LANES), F32), jax.ShapeDtypeStruct((1, d), F32),
                   jax.ShapeDtypeStruct((1, 2 * d), F32), mat, mat, mat],
        scratch_shapes=[pltpu.VMEM((d, d), _MXU_DTYPE)] * 3 + [pltpu.VMEM((d, d), F32)] * 3,
        compiler_params=_params(("arbitrary",)),
    )(ya, yb, z, x2d, tgt2d, b_merge, final_g, pa, pb, wo)


def _inproj_dw(h_t, dz):
    d, tokens = h_t.shape
    tm = min(2048, tokens)

    def body(h_ref, dz_ref, dw_ref):
        part = _mm(h_ref[...], dz_ref[...])

        @pl.when(pl.program_id(1) == 0)
        def _():
            dw_ref[...] = part

        @pl.when(pl.program_id(1) != 0)
        def _():
            dw_ref[...] += part

    def out_index(s, i):
        g = _group_of_slot(s)
        return (g // 2, 0, g % 2)

    return pl.pallas_call(
        body, name="inproj_dw",
        grid=(N_GROUPS, tokens // tm),
        in_specs=[pl.BlockSpec((d, tm), lambda s, i: (0, i)),
                  pl.BlockSpec((None, tm, D_MODEL), lambda s, i: (s, i, 0))],
        out_specs=pl.BlockSpec((None, d, D_MODEL), out_index),
        out_shape=jax.ShapeDtypeStruct((N_SHARDS, d, 2 * D_MODEL), F32),
        compiler_params=_params(("parallel", "arbitrary")),
    )(h_t, dz)


def _inproj_dw_exchange(h_t, dz, scatter):
    d, tokens = h_t.shape
    tm = min(2048, tokens)
    n_i = tokens // tm
    half = d // 2

    def body(h_ref, dz_ref, *rest):
        n_in, n_out = scatter.n_in, scatter.n_out
        dw_hbm, land_hbm = rest[n_in:n_in + 2]
        acc, local_sems, send_sems, recv_sems = rest[n_in + 2 + n_out:n_in + 6 + n_out]
        carried = scatter.copies(rest[:n_in], rest[n_in + 2:n_in + 2 + n_out], rest[n_in + 6 + n_out:])
        s, i = pl.program_id(0), pl.program_id(1)
        x, y, c, _ = _mesh_position()

        @pl.when((s == 0) & (i == 0))
        def _():
            for cp in carried:
                cp.start()

        part = _mm(h_ref[...], dz_ref[...])
        buf = acc.at[s % 2]

        @pl.when(i == 0)
        def _():
            buf[...] = part

        @pl.when(i != 0)
        def _():
            buf[...] += part

        def copies(k):
            g = _SLOT_TO_GROUP[k]
            cols = pl.ds((g % 2) * D_MODEL, D_MODEL)
            src = acc.at[k % 2]
            mine = pltpu.make_async_copy(src, dw_hbm.at[g // 2, :, cols], local_sems.at[k % 2])
            theirs = _remote(src.at[pl.ds((1 - c) * half, half), :], land_hbm.at[g // 2, :, cols],
                             send_sems, recv_sems, k, (x, y, 1 - c))
            return mine, theirs

        for k in range(N_GROUPS):
            @pl.when((s == k) & (i == n_i - 1))
            def _(k=k):
                if k > 0:
                    mine, theirs = copies(k - 1)
                    mine.wait()
                    theirs.wait_send()
                mine, theirs = copies(k)
                mine.start()
                theirs.start()
                if k == N_GROUPS - 1:
                    mine.wait()
                    theirs.wait_send()
                    for kk in range(N_GROUPS):
                        copies(kk)[1].wait_recv()
                    for cp in carried:
                        cp.wait()

    hbm = pl.BlockSpec(memory_space=pl.ANY)
    more = scatter.plumbing(first_operand=2, first_output=2)
    return pl.pallas_call(
        body, name="inproj_dw_exchange",
        grid=(N_GROUPS, n_i),
        in_specs=[pl.BlockSpec((d, tm), lambda s, i: (0, i)),
                  pl.BlockSpec((None, tm, D_MODEL), lambda s, i: (s, i, 0))] + more[1],
        out_specs=[hbm, hbm] + more[2],
        out_shape=[jax.ShapeDtypeStruct((N_SHARDS, d, 2 * D_MODEL), F32),
                   jax.ShapeDtypeStruct((N_SHARDS, half, 2 * D_MODEL), F32)] + more[3],
        scratch_shapes=[pltpu.VMEM((2, d, D_MODEL), F32), pltpu.SemaphoreType.DMA((2,)),
                        pltpu.SemaphoreType.DMA((N_GROUPS,)), pltpu.SemaphoreType.DMA((N_GROUPS,))] + more[4],
        input_output_aliases=more[5],
        compiler_params=_params(("arbitrary", "arbitrary")),
    )(h_t, dz, *more[0])


def _inproj_dx(dz, w_all, x2d, dx2, norm_g, scatter=None):
    tokens, d = x2d.shape
    tm = min(256, tokens)
    n_tiles = tokens // tm

    def body(dz_ref, w_hbm, x_ref, dx2_ref, g_ref, *rest):
        if scatter:
            n_in, n_out = scatter.n_in, scatter.n_out
            dx_ref, dg_ref = rest[n_in:n_in + 2]
            w_res = rest[n_in + 2 + n_out]
            copies = scatter.copies(rest[:n_in], rest[n_in + 2:n_in + 2 + n_out], rest[n_in + 3 + n_out:])
        else:
            dx_ref, dg_ref, w_res = rest
            copies = []
        i = pl.program_id(0)

        @pl.when(i == 0)
        def _():
            for cp in copies:
                cp.start()
            for slot, g in enumerate(_SLOT_TO_GROUP):
                pltpu.sync_copy(w_hbm.at[g // 2, :, pl.ds((g % 2) * D_MODEL, D_MODEL)],
                                w_res.at[:, pl.ds(slot * D_MODEL, D_MODEL)])
            dg_ref[...] = jnp.zeros_like(dg_ref)

        dz_all = jnp.concatenate([dz_ref[s] for s in range(N_GROUPS)], axis=1)
        dh = jnp.transpose(_mm_nt(w_res[...], dz_all))
        x = x_ref[...]
        r = lax.rsqrt(jnp.mean(x * x, axis=-1, keepdims=True) + EPS)
        xn = x * r
        dg_ref[...] += jnp.sum(dh * xn, axis=0, keepdims=True)
        dxn = dh * g_ref[...]
        dx_ref[...] = r * (dxn - xn * jnp.mean(dxn * xn, axis=-1, keepdims=True)) + dx2_ref[...]

        @pl.when(i == n_tiles - 1)
        def _():
            for cp in copies:
                cp.wait()

    tile = pl.BlockSpec((tm, d), lambda i: (i, 0))
    hbm = pl.BlockSpec(memory_space=pl.ANY)
    in_specs = [pl.BlockSpec((N_GROUPS, tm, D_MODEL), lambda i: (0, i, 0)), hbm, tile, tile,
                pl.BlockSpec((1, d), lambda i: (0, 0))]
    out_specs = [tile, pl.BlockSpec((1, d), lambda i: (0, 0))]
    out_shape = [jax.ShapeDtypeStruct((tokens, d), F32), jax.ShapeDtypeStruct((1, d), F32)]
    scratch = [pltpu.VMEM((d, N_GROUPS * D_MODEL), _MXU_DTYPE)]
    operands, aliases = [dz, w_all, x2d, dx2, norm_g], {}
    if scatter:
        more = scatter.plumbing(first_operand=len(operands), first_output=len(out_shape))
        operands, in_specs, out_specs = operands + more[0], in_specs + more[1], out_specs + more[2]
        out_shape, scratch, aliases = out_shape + more[3], scratch + more[4], more[5]
    return pl.pallas_call(
        body, name="inproj_dx", grid=(n_tiles,), in_specs=in_specs, out_specs=out_specs, out_shape=out_shape,
        scratch_shapes=scratch, input_output_aliases=aliases,
        compiler_params=_params(("arbitrary",)),
    )(*operands)


def _row_tile(rows, cols, itemsize=4, budget=2 * 1024 * 1024):
    tr = rows
    while tr * cols * itemsize > budget and tr % 16 == 0:
        tr //= 2
    return tr


def _cast_into_slot(a, chip, dtype, name):
    rows, cols = a.shape
    tr = _row_tile(rows, cols)

    def body(chip_ref, a_ref, o_ref):
        del chip_ref
        o_ref[...] = a_ref[...].astype(dtype)

    grid_spec = pltpu.PrefetchScalarGridSpec(
        num_scalar_prefetch=1, grid=(rows // tr,),
        in_specs=[pl.BlockSpec((tr, cols), lambda i, chip_ref: (i, 0))],
        out_specs=pl.BlockSpec((None, tr, cols), lambda i, chip_ref: (chip_ref[0], i, 0)))
    return pl.pallas_call(body, name=name, grid_spec=grid_spec,
                          out_shape=jax.ShapeDtypeStruct((N_SHARDS, rows, cols), dtype),
                          compiler_params=_params(("arbitrary",)))(chip, a)


def _sum_slots(stack, name):
    n, rows, cols = stack.shape
    tr = _row_tile(rows, cols * n)

    def body(s_ref, o_ref):
        total = s_ref[0].astype(F32)
        for k in range(1, n):
            total = total + s_ref[k].astype(F32)
        o_ref[...] = total

    return pl.pallas_call(body, name=name, grid=(rows // tr,),
                          in_specs=[pl.BlockSpec((n, tr, cols), lambda i: (0, i, 0))],
                          out_specs=pl.BlockSpec((tr, cols), lambda i: (i, 0)),
                          out_shape=jax.ShapeDtypeStruct((rows, cols), F32),
                          compiler_params=_params(("parallel",)))(stack)


def _add_half(full, landed, place, name):
    n, rows, cols = full.shape
    half = rows // 2
    tr = _row_tile(half, cols)
    nb = half // tr

    def body(place_ref, a_ref, b_ref, o_ref, own_ref):
        total = (a_ref[...] + b_ref[...]).astype(_MXU_DTYPE)
        o_ref[...] = total

        @pl.when(pl.program_id(1) == place_ref[1])
        def _():
            own_ref[...] = total

    grid_spec = pltpu.PrefetchScalarGridSpec(
        num_scalar_prefetch=1, grid=(nb, n),
        in_specs=[pl.BlockSpec((None, tr, cols), lambda i, j, place_ref: (j, place_ref[0] * nb + i, 0)),
                  pl.BlockSpec((None, tr, cols), lambda i, j, place_ref: (j, i, 0))],
        out_specs=[pl.BlockSpec((None, tr, cols), lambda i, j, place_ref: (j, i, 0)),
                   pl.BlockSpec((None, tr, cols), lambda i, j, place_ref: (place_ref[1], i, 0))])
    shape = jax.ShapeDtypeStruct((n, half, cols), _MXU_DTYPE)
    return pl.pallas_call(body, name=name, grid_spec=grid_spec, out_shape=[shape, shape],
                          compiler_params=_params(("parallel", "arbitrary")))(place, full, landed)


def _adamw_update(w, grad, m, v):
    c1 = 1.0 - ADAM_B1 ** ADAM_STEP
    c2 = 1.0 - ADAM_B2 ** ADAM_STEP
    nm = ADAM_B1 * m + (1.0 - ADAM_B1) * grad
    nv = ADAM_B2 * v + (1.0 - ADAM_B2) * (grad * grad)
    return (-ADAM_LR) * ((nm / c1) / (jnp.sqrt(nv / c2) + ADAM_EPS) + ADAM_WD * w), nm, nv


def _adamw(w, g, m, v, name):
    rows, cols = w.shape
    tr = _row_tile(rows, cols, budget=1024 * 1024)

    def body(w_ref, g_ref, m_ref, v_ref, d_ref, nm_ref, nv_ref):
        d_ref[...], nm_ref[...], nv_ref[...] = _adamw_update(w_ref[...], g_ref[...], m_ref[...], v_ref[...])

    spec = pl.BlockSpec((tr, cols), lambda i: (i, 0))
    shape = jax.ShapeDtypeStruct((rows, cols), F32)
    return pl.pallas_call(body, name=name, grid=(rows // tr,), in_specs=[spec] * 4, out_specs=[spec] * 3,
                          out_shape=[shape] * 3, compiler_params=_params(("parallel",)))(w, g, m, v)


def _adamw_halves(w, g_mine, g_sibling, m, v, core, name):
    rows, cols = w.shape
    half = rows // 2
    tr = _row_tile(half, cols, budget=1024 * 1024)
    nb = half // tr

    def body(core_ref, w_ref, gm_ref, gs_ref, m_ref, v_ref, g_ref, d_ref, nm_ref, nv_ref):
        mine = pl.program_id(0) // nb == core_ref[0]
        grad = jnp.where(mine, gm_ref[...], gs_ref[...])
        g_ref[...] = grad
        d_ref[...], nm_ref[...], nv_ref[...] = _adamw_update(w_ref[...], grad, m_ref[...], v_ref[...])

    spec = pl.BlockSpec((tr, cols), lambda i, core_ref: (i, 0))
    half_spec = pl.BlockSpec((tr, cols), lambda i, core_ref: (i % nb, 0))
    grid_spec = pltpu.PrefetchScalarGridSpec(num_scalar_prefetch=1, grid=(rows // tr,),
                                             in_specs=[spec, half_spec, half_spec, spec, spec], out_specs=[spec] * 4)
    shape = jax.ShapeDtypeStruct((rows, cols), F32)
    return pl.pallas_call(body, name=name, grid_spec=grid_spec, out_shape=[shape] * 4,
                          compiler_params=_params(("parallel",)))(core, w, g_mine, g_sibling, m, v)


def _local_step(x, loss_target, w_all, pa, pb, wo, conv_w, b_merge, conv_b, rg_wx, rg_bx, rg_wa, rg_ba,
                rg_lambda, hg_lb_logits, hg_norm_g, norm_g, final_norm_g, gather=None, reduction=None):
    batch, seq, d = x.shape
    x2d = x.reshape(batch * seq, d)
    tgt2d = loss_target.reshape(batch * seq, d)
    if gather is None:
        z, h_t = _inproj_fwd(x2d, norm_g, w_all)
    else:
        z, h_t, (w_all, pa, pb, wo), cw_all = _inproj_fwd_gather(x2d, norm_g, *gather)
        pa, pb, wo = (t.reshape(d, d) for t in (pa, pb, wo))
        conv_w = jnp.transpose(cw_all, (1, 0, 2)).reshape(CONV_WIDTH, d)
    lru = (conv_w, conv_b, rg_wx, rg_bx, rg_wa, rg_ba, rg_lambda)
    ya, hl = _branch_a_fwd(z, *lru, batch, seq)
    yb, states = _branch_b_fwd(z, hg_lb_logits, hg_norm_g, batch, seq)
    dya, dyb, dx2, dz, loss, d_final_g, d_b_merge, d_pa, d_pb, d_wo = _merge_tail(
        ya, yb, z, x2d, tgt2d, b_merge, final_norm_g, pa, pb, wo)
    dz, d_lb_logits, d_hg_g = _branch_b_bwd(z, states, dyb, dz, hg_lb_logits, hg_norm_g, batch, seq)
    dz, d_conv_w, d_conv_b, d_wx, d_bx, d_wa, d_ba, d_lam = _branch_a_bwd(z, hl, dya, dz, *lru, batch, seq)
    small = dict(b_merge=d_b_merge, conv_w=d_conv_w, conv_b=d_conv_b, rg_wx=d_wx, rg_bx=d_bx, rg_wa=d_wa,
                 rg_ba=d_ba, rg_lambda=d_lam, hg_lb_logits=d_lb_logits, hg_norm_g=d_hg_g,
                 norm_g=jnp.zeros((1, d), F32), final_norm_g=d_final_g)
    if reduction is None:
        big = (_inproj_dw(h_t, dz), d_pa, d_pb, d_wo)
        grad_x, small["norm_g"] = _inproj_dx(dz, w_all, x2d, dx2, norm_g)
        return loss[0, 0], grad_x.reshape(batch, seq, d), big, small
    first, second = reduction
    d_w_in, landed_w_in, *scattered_first = _inproj_dw_exchange(h_t, dz, first((d_pa, d_pb, d_wo), small))
    grad_x, d_norm_g, *scattered_second = _inproj_dx(dz, w_all, x2d, dx2, norm_g, scatter=second(d_w_in, landed_w_in))
    return loss[0, 0], grad_x.reshape(batch, seq, d), d_norm_g, (scattered_first, scattered_second)


_SMALL_ORDER = ("b_merge", "conv_w", "conv_b", "rg_wx", "rg_bx", "rg_wa", "rg_ba", "rg_lambda", "hg_lb_logits",
                "hg_norm_g", "norm_g", "final_norm_g")
N_DEV = 8
PIECE_ROWS = 272


def _pack_small(tree):
    flat = jnp.concatenate([tree[k].reshape(-1) for k in _SMALL_ORDER])
    flat = jnp.pad(flat, (0, N_DEV * PIECE_ROWS * LANES - flat.shape[0]))
    return flat.reshape(N_DEV * PIECE_ROWS, LANES)


def _unpack_small(packed, like):
    flat = packed.reshape(-1)
    out, pos = {}, 0
    for k in _SMALL_ORDER:
        n = like[k].size
        out[k] = flat[pos:pos + n].reshape(like[k].shape)
        pos += n
    return out


def _mesh_position():
    x, y, c = lax.axis_index("x"), lax.axis_index("y"), lax.axis_index("c")
    other_chips = [(1 - x, y), (x, 1 - y), (1 - x, 1 - y)]
    return x, y, c, other_chips


def _other_devices(x, y, c):
    flips = [(fx, fy, fc) for fx in (0, 1) for fy in (0, 1) for fc in (0, 1) if (fx, fy, fc) != (0, 0, 0)]
    return [(jnp.where(fx, 1 - x, x), jnp.where(fy, 1 - y, y), jnp.where(fc, 1 - c, c)) for fx, fy, fc in flips]


def _remote(src, dst, send_sems, recv_sems, k, device):
    return pltpu.make_async_remote_copy(src_ref=src, dst_ref=dst, send_sem=send_sems.at[k], recv_sem=recv_sems.at[k],
                                        device_id=device, device_id_type=MESH)


def _exchange_halves(bigs, small):
    n_big = len(bigs)
    n_sem = n_big + N_DEV - 1

    def body(*refs):
        srcs, small_src = refs[:n_big], refs[n_big]
        outs, small_out = refs[n_big + 1:2 * n_big + 1], refs[2 * n_big + 1]
        send_sems, recv_sems, local_sem = refs[2 * n_big + 2:]
        x, y, c, _ = _mesh_position()
        me, sibling = 4 * x + 2 * y + c, (x, y, 1 - c)
        mine = pltpu.make_async_copy(small_src.at[pl.ds(me * PIECE_ROWS, PIECE_ROWS), :], small_out.at[me], local_sem)
        mine.start()
        copies = []
        for a in range(n_big):
            hs = srcs[a].shape[1] // 2
            copies.append(_remote(srcs[a].at[:, pl.ds((1 - c) * hs, hs), :], outs[a], send_sems, recv_sems, a, sibling))
        for k, (px, py, pc) in enumerate(_other_devices(x, y, c)):
            piece = small_src.at[pl.ds((4 * px + 2 * py + pc) * PIECE_ROWS, PIECE_ROWS), :]
            copies.append(_remote(piece, small_out.at[me], send_sems, recv_sems, n_big + k, (px, py, pc)))
        for cp in copies:
            cp.start()
        for cp in copies:
            cp.wait()
        mine.wait()

    hbm = pl.BlockSpec(memory_space=pl.ANY)
    out_shape = [jax.ShapeDtypeStruct((g.shape[0], g.shape[1] // 2, g.shape[2]), F32) for g in bigs]
    out_shape.append(jax.ShapeDtypeStruct((N_DEV, PIECE_ROWS, LANES), F32))
    return pl.pallas_call(
        body, name="exchange_halves",
        in_specs=[hbm] * (n_big + 1), out_specs=[hbm] * (n_big + 1), out_shape=out_shape,
        scratch_shapes=[pltpu.SemaphoreType.DMA((n_sem,)), pltpu.SemaphoreType.DMA((n_sem,)), pltpu.SemaphoreType.DMA],
    )(*bigs, small)


class _Scatter:
    def __init__(self, bigs, by_chip, small=None):
        self.bigs, self.by_chip, self.small = list(bigs), list(by_chip), small
        self.n_big = len(self.bigs)
        self.n_in = 2 * self.n_big + (small is not None)
        self.n_out = self.n_big + (small is not None)
        self.n_scratch = 2 + (small is not None)

    def plumbing(self, first_operand, first_output):
        hbm = pl.BlockSpec(memory_space=pl.ANY)
        n_sem = 3 * self.n_big + (N_DEV - 1 if self.small is not None else 0)
        operands = self.bigs + self.by_chip + ([self.small] if self.small is not None else [])
        out_shapes = [jax.ShapeDtypeStruct(g.shape, g.dtype) for g in self.by_chip]
        scratch = [pltpu.SemaphoreType.DMA((n_sem,)), pltpu.SemaphoreType.DMA((n_sem,))]
        if self.small is not None:
            out_shapes.append(jax.ShapeDtypeStruct((N_DEV, PIECE_ROWS, LANES), F32))
            scratch.append(pltpu.SemaphoreType.DMA)
        aliases = {first_operand + self.n_big + a: first_output + a for a in range(self.n_big)}
        return operands, [hbm] * self.n_in, [hbm] * self.n_out, out_shapes, scratch, aliases

    def copies(self, in_refs, out_refs, scratch_refs):
        srcs, outs = in_refs[:self.n_big], out_refs[:self.n_big]
        send_sems, recv_sems = scratch_refs[:2]
        x, y, c, chips = _mesh_position()
        chip, me = 2 * x + y, 4 * x + 2 * y + c
        copies = []
        for a in range(self.n_big):
            for j, (cx, cy) in enumerate(chips):
                copies.append(_remote(srcs[a].at[2 * cx + cy], outs[a].at[chip], send_sems, recv_sems, 3 * a + j,
                                      (cx, cy, c)))
        if self.small is not None:
            small_src, small_out = in_refs[2 * self.n_big], out_refs[self.n_big]
            copies.append(pltpu.make_async_copy(small_src, small_out.at[me], scratch_refs[2]))
            for k, peer in enumerate(_other_devices(x, y, c)):
                copies.append(_remote(small_src, small_out.at[me], send_sems, recv_sems, 3 * self.n_big + k, peer))
        return copies


def _swap_halves(halves, vec):
    n_big = len(halves)

    def body(*refs):
        srcs, vec_src = refs[:n_big], refs[n_big]
        outs, vec_out = refs[n_big + 1:2 * n_big + 1], refs[2 * n_big + 1]
        send_sems, recv_sems, local_sem = refs[2 * n_big + 2:]
        x, y, c, _ = _mesh_position()
        me = 4 * x + 2 * y + c
        copies = [pltpu.make_async_copy(vec_src, vec_out.at[me], local_sem)]
        copies += [_remote(srcs[a], outs[a], send_sems, recv_sems, a, (x, y, 1 - c)) for a in range(n_big)]
        copies += [_remote(vec_src, vec_out.at[me], send_sems, recv_sems, n_big + k, peer)
                   for k, peer in enumerate(_other_devices(x, y, c))]
        for cp in copies:
            cp.start()
        for cp in copies:
            cp.wait()

    hbm = pl.BlockSpec(memory_space=pl.ANY)
    n_sem = n_big + N_DEV - 1
    return pl.pallas_call(
        body, name="swap_halves",
        in_specs=[hbm] * (n_big + 1), out_specs=[hbm] * (n_big + 1),
        out_shape=[jax.ShapeDtypeStruct(h.shape, F32) for h in halves] + [jax.ShapeDtypeStruct((N_DEV,) + vec.shape, F32)],
        scratch_shapes=[pltpu.SemaphoreType.DMA((n_sem,)), pltpu.SemaphoreType.DMA((n_sem,)), pltpu.SemaphoreType.DMA],
    )(*halves, vec)


def kernel(x, w_in, b_merge, conv_w, conv_b, rg_wx, rg_bx, rg_wa, rg_ba, rg_lambda, hg_lb_logits, hg_norm_g, proj_a, proj_b, w_out, norm_g, final_norm_g, loss_target, m_w_in, m_b_merge, m_conv_w, m_conv_b, m_rg_wx, m_rg_bx, m_rg_wa, m_rg_ba, m_rg_lambda, m_hg_lb_logits, m_hg_norm_g, m_proj_a, m_proj_b, m_w_out, m_norm_g, m_final_norm_g, v_w_in, v_b_merge, v_conv_w, v_conv_b, v_rg_wx, v_rg_bx, v_rg_wa, v_rg_ba, v_rg_lambda, v_hg_lb_logits, v_hg_norm_g, v_proj_a, v_proj_b, v_w_out, v_norm_g, v_final_norm_g):
    d = D_MODEL
    weights = dict(w_in=w_in, b_merge=b_merge, conv_w=conv_w, conv_b=conv_b, rg_wx=rg_wx, rg_bx=rg_bx, rg_wa=rg_wa,
                   rg_ba=rg_ba, rg_lambda=rg_lambda, hg_lb_logits=hg_lb_logits, hg_norm_g=hg_norm_g, proj_a=proj_a,
                   proj_b=proj_b, w_out=w_out, norm_g=norm_g, final_norm_g=final_norm_g)
    m = dict(w_in=m_w_in, b_merge=m_b_merge, conv_w=m_conv_w, conv_b=m_conv_b, rg_wx=m_rg_wx, rg_bx=m_rg_bx,
             rg_wa=m_rg_wa, rg_ba=m_rg_ba, rg_lambda=m_rg_lambda, hg_lb_logits=m_hg_lb_logits, hg_norm_g=m_hg_norm_g,
             proj_a=m_proj_a, proj_b=m_proj_b, w_out=m_w_out, norm_g=m_norm_g, final_norm_g=m_final_norm_g)
    v = dict(w_in=v_w_in, b_merge=v_b_merge, conv_w=v_conv_w, conv_b=v_conv_b, rg_wx=v_rg_wx, rg_bx=v_rg_bx,
             rg_wa=v_rg_wa, rg_ba=v_rg_ba, rg_lambda=v_rg_lambda, hg_lb_logits=v_hg_lb_logits, hg_norm_g=v_hg_norm_g,
             proj_a=v_proj_a, proj_b=v_proj_b, w_out=v_w_out, norm_g=v_norm_g, final_norm_g=v_final_norm_g)
    big_names = ("w_in", "proj_a", "proj_b", "w_out")

    core = lax.axis_index("c").astype(jnp.int32).reshape(1)
    chip = (2 * lax.axis_index("x") + lax.axis_index("y")).astype(jnp.int32)

    slotted = [_cast_into_slot(weights[k][0], chip.reshape(1), _MXU_DTYPE, f"cast_{k}") for k in big_names]
    conv_slotted = _cast_into_slot(conv_w[0], chip.reshape(1), F32, "slot_conv_w")

    small_shapes = {}

    place = jnp.concatenate([core, chip.reshape(1)])

    def reduce_proj_and_small(proj_grads, small_grads):
        small_shapes.update({k: t.shape for k, t in small_grads.items()})
        bigs = [g.reshape(N_SHARDS, d // N_SHARDS, d) for g in proj_grads]
        *landed, small_landed = _exchange_halves(bigs, _pack_small(small_grads))
        sums = [_add_half(g, l, place, f"add_half_{1 + a}") for a, (g, l) in enumerate(zip(bigs, landed))]
        return _Scatter([s[0] for s in sums], [s[1] for s in sums], _sum_slots(small_landed, "sum_small"))

    def reduce_w_in(d_w_in, landed):
        partial, own_slot = _add_half(d_w_in, landed, place, "add_half_0")
        return _Scatter([partial], [own_slot])

    loss_part, grad_x, d_norm_g, ((*by_chip_proj, small_all), by_chip_w_in) = _local_step(
        x, loss_target, None, None, None, None, None,
        b_merge, conv_b, rg_wx[0], rg_bx.reshape(1, d), rg_wa[0], rg_ba.reshape(1, d), rg_lambda, hg_lb_logits,
        hg_norm_g, norm_g, final_norm_g.reshape(1, d), gather=(slotted, conv_slotted, chip.reshape(1)),
        reduction=(reduce_proj_and_small, reduce_w_in))
    mine = [_sum_slots(s, f"sum_chips_{a}") for a, s in enumerate(by_chip_w_in + by_chip_proj)]
    late = jnp.concatenate([d_norm_g.reshape(SUBLANES, LANES), jnp.full((SUBLANES, LANES), loss_part, F32)])
    *theirs, late_parts = _swap_halves(mine, late)
    late_sum = _sum_slots(late_parts, "sum_late")
    loss = late_sum[SUBLANES, 0]
    small_red = _unpack_small(small_all, {k: jax.ShapeDtypeStruct(s, F32) for k, s in small_shapes.items()})
    small_red["norm_g"] = late_sum[:SUBLANES].reshape(1, d)

    grads, delta, new_m, new_v = {}, {}, {}, {}
    for k, g_mine, g_theirs in zip(big_names, mine, theirs):
        out = _adamw_halves(weights[k][0], g_mine, g_theirs, m[k][0], v[k][0], core, f"adamw_{k}")
        grads[k], delta[k], new_m[k], new_v[k] = (t.reshape(weights[k].shape) for t in out)
    cols = d // N_SHARDS
    g_conv = lax.dynamic_slice(small_red["conv_w"], (0, chip * cols), (CONV_WIDTH, cols))
    grads["conv_w"] = g_conv.reshape(conv_w.shape)
    dl, nm, nv = _adamw(conv_w[0], g_conv, m_conv_w[0], v_conv_w[0], "adamw_conv_w")
    delta["conv_w"], new_m["conv_w"], new_v["conv_w"] = (t.reshape(conv_w.shape) for t in (dl, nm, nv))
    rest = [k for k in _SMALL_ORDER if k != "conv_w"]
    like = {k: (weights[k] if k != "conv_w" else jnp.zeros((CONV_WIDTH, d), F32)) for k in _SMALL_ORDER}
    packs = [_pack_small({k: (t[k] if k != "conv_w" else like[k]) for k in _SMALL_ORDER}) for t in (weights, m, v)]
    g_pack = _pack_small({k: small_red[k].reshape(like[k].shape) for k in _SMALL_ORDER})
    outs = [_unpack_small(p, like) for p in _adamw(packs[0], g_pack, packs[1], packs[2], "adamw_small")]
    for k in rest:
        grads[k] = small_red[k].reshape(weights[k].shape)
        delta[k], new_m[k], new_v[k] = outs[0][k], outs[1][k], outs[2][k]

    order = ("w_in", "b_merge", "conv_w", "conv_b", "rg_wx", "rg_bx", "rg_wa", "rg_ba", "rg_lambda", "hg_lb_logits",
             "hg_norm_g", "proj_a", "proj_b", "w_out", "norm_g", "final_norm_g")
    return (loss, grad_x, *[grads[k] for k in order], *[delta[k] for k in order], *[new_m[k] for k in order],
            *[new_v[k] for k in order])
```

```python
import functools

import jax
import jax.numpy as jnp
from jax import lax
from jax.experimental import pallas as pl
from jax.experimental.pallas import tpu as pltpu

F32 = jnp.float32
_MXU_DTYPE = jnp.bfloat16

D_MODEL = 1024
LANES = 128
SUBLANES = 8
N_BLK = D_MODEL // LANES
N_GROUPS = 8
N_SHARDS = 4
CONV_WIDTH = 4
LRU_C = 8.0
CHUNK = 64
CHUNKS_IN_FLIGHT = 16
HG_SCALE = float(LANES) ** -0.5
EPS = 1e-6
ADAM_LR, ADAM_B1, ADAM_B2, ADAM_EPS, ADAM_WD, ADAM_STEP = 0.001, 0.9, 0.999, 1e-08, 0.01, 10
VMEM_LIMIT = 56 * 1024 * 1024
MESH = pl.DeviceIdType.MESH

_SLOT_TO_GROUP = (2, 3, 4, 5, 0, 1, 6, 7)


def _group_of_slot(s):
    return jnp.where(s < 4, s + 2, jnp.where(s < 6, s - 4, s))


def _mm(a, b):
    return lax.dot_general(a.astype(_MXU_DTYPE), b.astype(_MXU_DTYPE), (((1,), (0,)), ((), ())),
                           preferred_element_type=F32)


def _mm_nt(a, b):
    return lax.dot_general(a.astype(_MXU_DTYPE), b.astype(_MXU_DTYPE), (((1,), (1,)), ((), ())),
                           preferred_element_type=F32)


def _mm_tn(a, b):
    return lax.dot_general(a.astype(_MXU_DTYPE), b.astype(_MXU_DTYPE), (((0,), (0,)), ((), ())),
                           preferred_element_type=F32)


def _sigmoid(x):
    return 0.5 * jnp.tanh(0.5 * x) + 0.5


def _log1p_pos(y):
    series = y * (1.0 - y * (0.5 - y * (1.0 / 3.0 - y * 0.25)))
    return jnp.where(y < 0.01, series, jnp.log(1.0 + y))


def _softplus(x):
    return jnp.maximum(x, 0.0) + _log1p_pos(jnp.exp(-jnp.abs(x)))


def _shift_down(x, n):
    rolled = pltpu.roll(x, n, 0)
    edge = SUBLANES if (n < SUBLANES and x.shape[0] > SUBLANES) else x.shape[0]
    rows = lax.broadcasted_iota(jnp.int32, (edge, x.shape[1]), 0)
    head = jnp.where(rows >= n, rolled[:edge], 0.0)
    return head if edge == x.shape[0] else jnp.concatenate([head, rolled[edge:]], axis=0)


def _shift_up(x, n):
    size = x.shape[0]
    rolled = pltpu.roll(x, size - n, 0)
    edge = SUBLANES if (n < SUBLANES and size > SUBLANES) else size
    rows = lax.broadcasted_iota(jnp.int32, (edge, x.shape[1]), 0)
    tail = jnp.where(rows < edge - n, rolled[size - edge:], 0.0)
    return tail if edge == size else jnp.concatenate([rolled[:size - edge], tail], axis=0)


def _params(dims, vmem=VMEM_LIMIT):
    return pltpu.CompilerParams(dimension_semantics=dims, vmem_limit_bytes=vmem)


def _load_w_in_by_slot(w_hbm, w_res):
    for slot, g in enumerate(_SLOT_TO_GROUP):
        pltpu.sync_copy(w_hbm.at[g // 2, :, pl.ds((g % 2) * D_MODEL, D_MODEL)], w_res.at[slot])


def _inproj_fwd(x2d, norm_g, w_all):
    tokens, d = x2d.shape
    tm = min(512, tokens)

    def body(x_ref, g_ref, w_hbm, z_ref, ht_ref, h_scr, w_res):
        @pl.when((pl.program_id(0) == 0) & (pl.program_id(1) == 0))
        def _():
            _load_w_in_by_slot(w_hbm, w_res)

        @pl.when(pl.program_id(1) == 0)
        def _():
            x = x_ref[...]
            r = lax.rsqrt(jnp.mean(x * x, axis=-1, keepdims=True) + EPS)
            h = (x * r) * g_ref[...]
            h_scr[...] = h.astype(_MXU_DTYPE)
            ht_ref[...] = jnp.transpose(h).astype(_MXU_DTYPE)

        z_ref[...] = _mm(h_scr[...], w_res[pl.program_id(1)])

    return pl.pallas_call(
        body, name="inproj_fwd",
        grid=(tokens // tm, N_GROUPS),
        in_specs=[pl.BlockSpec((tm, d), lambda i, s: (i, 0)),
                  pl.BlockSpec((1, d), lambda i, s: (0, 0)),
                  pl.BlockSpec(memory_space=pl.ANY)],
        out_specs=[pl.BlockSpec((None, tm, D_MODEL), lambda i, s: (s, i, 0)),
                   pl.BlockSpec((d, tm), lambda i, s: (0, i))],
        out_shape=[jax.ShapeDtypeStruct((N_GROUPS, tokens, D_MODEL), F32),
                   jax.ShapeDtypeStruct((d, tokens), _MXU_DTYPE)],
        scratch_shapes=[pltpu.VMEM((tm, d), _MXU_DTYPE), pltpu.VMEM((N_GROUPS, d, D_MODEL), _MXU_DTYPE)],
        compiler_params=_params(("arbitrary", "arbitrary")),
    )(x2d, norm_g, w_all)


def _slot_of_group(g):
    return jnp.where(g < 2, g + 4, jnp.where(g < 6, g - 2, g))


def _inproj_fwd_gather(x2d, norm_g, slotted, conv_slotted, chip):
    tokens, d = x2d.shape
    tm = min(512, tokens)
    n_tiles = tokens // tm
    n_big = len(slotted)
    n_sem = 6 * (n_big + 1) + 3
    last_pass = N_GROUPS - 1

    def shard_of(k, chip_id):
        x, y = chip_id // 2, chip_id % 2
        return 2 * jnp.where(k % 2 == 1, 1 - x, x) + jnp.where(k // 2 == 1, 1 - y, y)

    def body(chip_ref, x_ref, g_ref, *rest):
        bufs, cw = rest[n_big + 1:2 * n_big + 1], rest[2 * n_big + 1]
        z_ref, ht_ref = rest[2 * n_big + 2:2 * n_big + 4]
        h_all, slab, send_sems, recv_sems, slab_sems = rest[2 * n_big + 4:]
        del chip_ref
        p, i = pl.program_id(0), pl.program_id(1)
        x, y, c, chips = _mesh_position()
        me, sibling = 2 * x + y, (x, y, 1 - c)

        pieces = [(0, 0), (0, 1)] + [(a, None) for a in range(1, n_big)]

        def half(piece, slot, which):
            a, q = pieces[piece]
            hs = bufs[a].shape[1] // 2
            cols = slice(None) if q is None else pl.ds(q * D_MODEL, D_MODEL)
            return bufs[a].at[slot, pl.ds(which * hs, hs), cols]

        def send(piece, j):
            mine = half(piece, me, c)
            return _remote(mine, mine, send_sems, recv_sems, 6 * piece + j, (chips[j][0], chips[j][1], c))

        def arrival(piece, j):
            landed = half(piece, 2 * chips[j][0] + chips[j][1], c)
            return _remote(landed, landed, send_sems, recv_sems, 6 * piece + j, (chips[j][0], chips[j][1], c))

        def passed_on(piece, j, which):
            landed = half(piece, 2 * chips[j][0] + chips[j][1], which)
            return _remote(landed, landed, send_sems, recv_sems, 6 * piece + 3 + j, sibling)

        def conv_copy(j, slot):
            return _remote(cw.at[slot], cw.at[slot], send_sems, recv_sems, 6 * len(pieces) + j,
                           (chips[j][0], chips[j][1], c))

        def land(piece, j):
            arrival(piece, j).wait_recv()
            passed_on(piece, j, c).start()
            passed_on(piece, j, 1 - c).wait_recv()

        def slab_copy(pv):
            src = bufs[0].at[shard_of(pv // 2, me), :, pl.ds((pv % 2) * D_MODEL, D_MODEL)]
            return pltpu.make_async_copy(src, slab.at[pv % 2], slab_sems.at[pv % 2])

        @pl.when((p == 0) & (i == 0))
        def _():
            for q in range(2):
                send(q, 0).start()
                send(q, 1).start()
            slab_copy(0).start()

        for pv in range(N_GROUPS):
            @pl.when((p == pv) & (i == 0))
            def _(pv=pv):
                slab_copy(pv).wait()

        rows = pl.ds(pl.multiple_of(i * tm, tm), tm)

        @pl.when(p == 0)
        def _():
            xt = x_ref[...]
            r = lax.rsqrt(jnp.mean(xt * xt, axis=-1, keepdims=True) + EPS)
            h = (xt * r) * g_ref[...]
            h_all[rows, :] = h.astype(_MXU_DTYPE)
            ht_ref[...] = jnp.transpose(h).astype(_MXU_DTYPE)

        z_ref[...] = _mm(h_all[rows, :], slab[p % 2])

        for pv in range(1, N_GROUPS):
            @pl.when((p == pv - 1) & (i == n_tiles - 1))
            def _(pv=pv):
                j, q = pv // 2 - 1, pv % 2
                if j >= 0:
                    land(q, j)
                if (j, q) == (0, 0):
                    send(0, 2).start()
                    send(1, 2).start()
                if (j, q) == (1, 0):
                    for piece in range(2, len(pieces)):
                        for jj in range(3):
                            send(piece, jj).start()
                    for jj in range(3):
                        conv_copy(jj, me).start()
                slab_copy(pv).start()

        @pl.when((p == last_pass) & (i == n_tiles - 1))
        def _():
            for piece in range(2, len(pieces)):
                for j in range(3):
                    land(piece, j)
            for j in range(3):
                conv_copy(j, 2 * chips[j][0] + chips[j][1]).wait_recv()
            for piece in range(len(pieces)):
                for j in range(3):
                    send(piece, j).wait_send()
                    passed_on(piece, j, c).wait_send()
            for j in range(3):
                conv_copy(j, me).wait_send()

    def z_index(p, i, chip_ref):
        g = 2 * shard_of(p // 2, chip_ref[0]) + p % 2
        return (_slot_of_group(g), i, 0)

    def first_pass_tile(p, i, chip_ref):
        return jnp.where(p == 0, i, n_tiles - 1)

    hbm = pl.BlockSpec(memory_space=pl.ANY)
    operands = list(slotted) + [conv_slotted]
    grid_spec = pltpu.PrefetchScalarGridSpec(
        num_scalar_prefetch=1, grid=(N_GROUPS, n_tiles),
        in_specs=[pl.BlockSpec((tm, d), lambda p, i, chip_ref: (first_pass_tile(p, i, chip_ref), 0)),
                  pl.BlockSpec((1, d), lambda p, i, chip_ref: (0, 0))] + [hbm] * (n_big + 1),
        out_specs=[hbm] * (n_big + 1) + [pl.BlockSpec((None, tm, D_MODEL), z_index),
                                         pl.BlockSpec((d, tm), lambda p, i, chip_ref: (0, first_pass_tile(p, i, chip_ref)))],
        scratch_shapes=[pltpu.VMEM((tokens, d), _MXU_DTYPE), pltpu.VMEM((2, d, D_MODEL), _MXU_DTYPE),
                        pltpu.SemaphoreType.DMA((n_sem,)), pltpu.SemaphoreType.DMA((n_sem,)),
                        pltpu.SemaphoreType.DMA((2,))])
    out = pl.pallas_call(
        body, name="inproj_fwd_gather", grid_spec=grid_spec,
        out_shape=[jax.ShapeDtypeStruct(a.shape, a.dtype) for a in operands]
        + [jax.ShapeDtypeStruct((N_GROUPS, tokens, D_MODEL), F32), jax.ShapeDtypeStruct((d, tokens), _MXU_DTYPE)],
        input_output_aliases={3 + a: a for a in range(n_big + 1)},
        compiler_params=_params(("arbitrary", "arbitrary")),
    )(chip, x2d, norm_g, *operands)
    return out[n_big + 1], out[n_big + 2], out[:n_big], out[n_big]


def _lane_blocks(x):
    return [x[:, k * LANES:(k + 1) * LANES] for k in range(x.shape[1] // LANES)]


def _block_diag(x, w_ref, transposed=False):
    mm = _mm_nt if transposed else _mm
    return jnp.concatenate([mm(xk, w_ref[k]) for k, xk in enumerate(_lane_blocks(x))], axis=1)


def _lru_gates(xa, cw_ref, cb_ref, wx_ref, bx_ref, wa_ref, ba_ref, lam_ref):
    xc = (cb_ref[...] + cw_ref[3:4, :] * xa + cw_ref[2:3, :] * _shift_down(xa, 1)
          + cw_ref[1:2, :] * _shift_down(xa, 2) + cw_ref[0:1, :] * _shift_down(xa, 3))
    gi = _sigmoid(_block_diag(xc, wx_ref) + bx_ref[...])
    gr = _sigmoid(_block_diag(xc, wa_ref) + ba_ref[...])
    sp = _softplus(-lam_ref[...])
    log_a = (-LRU_C) * gr * sp
    a = jnp.exp(log_a)
    y = 2.0 * log_a
    mult_sq = jnp.where(y > -1e-3, -y * (1.0 + 0.5 * y), 1.0 - a * a)
    inv_mult = lax.rsqrt(jnp.maximum(mult_sq, 1e-37))
    return xc, gi, gr, sp, a, mult_sq * inv_mult, inv_mult


def _tile_rows(width):
    return lax.broadcasted_iota(jnp.int32, (SUBLANES, width), 0)


def _scan_forward(a_scr, u_scr, h_scr, seq):
    width = a_scr.shape[1]
    rows = _tile_rows(width)

    def tile(j, carry):
        sl = pl.ds(pl.multiple_of(j * SUBLANES, SUBLANES), SUBLANES)
        a = a_scr[sl, :]
        u = u_scr[sl, :]
        for d in (1, 2, 4):
            keep = rows >= d
            a_sh = jnp.where(keep, pltpu.roll(a, d, 0), 1.0)
            u_sh = jnp.where(keep, pltpu.roll(u, d, 0), 0.0)
            u = a * u_sh + u
            a = a * a_sh
        h = u + a * carry
        h_scr[sl, :] = h
        return jnp.broadcast_to(h[SUBLANES - 1:SUBLANES, :], (SUBLANES, width))

    lax.fori_loop(0, seq // SUBLANES, tile, jnp.zeros((SUBLANES, width), F32))


def _scan_backward(c_scr, d_scr, g_scr, seq):
    width = c_scr.shape[1]
    rows = _tile_rows(width)
    n_tiles = seq // SUBLANES

    def tile(jj, carry):
        j = n_tiles - 1 - jj
        sl = pl.ds(pl.multiple_of(j * SUBLANES, SUBLANES), SUBLANES)
        c = c_scr[sl, :]
        g = d_scr[sl, :]
        for d in (1, 2, 4):
            keep = rows < SUBLANES - d
            c_sh = jnp.where(keep, pltpu.roll(c, SUBLANES - d, 0), 1.0)
            g_sh = jnp.where(keep, pltpu.roll(g, SUBLANES - d, 0), 0.0)
            g = c * g_sh + g
            c = c * c_sh
        g = g + c * carry
        g_scr[sl, :] = g
        return jnp.broadcast_to(g[0:1, :], (SUBLANES, width))

    lax.fori_loop(0, n_tiles, tile, jnp.zeros((SUBLANES, width), F32))


LRU_BLOCKS_PER_STEP = 2
LRU_LANES = LRU_BLOCKS_PER_STEP * LANES
LRU_STEPS = N_BLK // LRU_BLOCKS_PER_STEP


def _lru_param_specs(cb_axis):
    def pick(*ids):
        return ids[cb_axis]

    vec = pl.BlockSpec((1, LRU_LANES), lambda *ids: (0, pick(*ids)))
    mat = pl.BlockSpec((LRU_BLOCKS_PER_STEP, LANES, LANES), lambda *ids: (pick(*ids), 0, 0))
    return [pl.BlockSpec((CONV_WIDTH, LRU_LANES), lambda *ids: (0, pick(*ids))), vec, mat, vec, mat, vec, vec]


def _branch_a_fwd(z, conv_w, conv_b, wx, bx, wa, ba, lam, batch, seq):
    tokens = batch * seq

    def body(z_ref, cw_ref, cb_ref, wx_ref, bx_ref, wa_ref, ba_ref, lam_ref, ya_ref, hl_ref, a_scr, u_scr):
        xa = z_ref[0]
        ga = z_ref[1]
        xc, gi, _, _, a, mult, _ = _lru_gates(xa, cw_ref, cb_ref, wx_ref, bx_ref, wa_ref, ba_ref, lam_ref)
        a_scr[...] = a
        u_scr[...] = mult * gi * xc
        _scan_forward(a_scr, u_scr, hl_ref, seq)
        ya_ref[...] = (hl_ref[...] * (ga * _sigmoid(ga))).astype(_MXU_DTYPE)

    blk = pl.BlockSpec((seq, LRU_LANES), lambda b, c: (b, c))
    return pl.pallas_call(
        body, name="branch_a_fwd",
        grid=(batch, LRU_STEPS),
        in_specs=[pl.BlockSpec((2, seq, LRU_LANES), lambda b, c: (2, b, c))] + _lru_param_specs(1),
        out_specs=[blk, blk],
        out_shape=[jax.ShapeDtypeStruct((tokens, D_MODEL), _MXU_DTYPE), jax.ShapeDtypeStruct((tokens, D_MODEL), F32)],
        scratch_shapes=[pltpu.VMEM((seq, LRU_LANES), F32), pltpu.VMEM((seq, LRU_LANES), F32)],
        compiler_params=_params(("parallel", "parallel")),
    )(z, conv_w, conv_b, wx, bx, wa, ba, lam)


def _branch_a_bwd(z, hl, dya, dz, conv_w, conv_b, wx, bx, wa, ba, lam, batch, seq):
    def body(z_ref, hl_ref, dya_ref, dz_in_ref, cw_ref, cb_ref, wx_ref, bx_ref, wa_ref, ba_ref, lam_ref,
             dz_ref, dcw_ref, dcb_ref, dwx_ref, dbx_ref, dwa_ref, dba_ref, dlam_ref, c_scr, d_scr, g_scr):
        del dz_in_ref
        xa = z_ref[0]
        ga = z_ref[1]
        hl = hl_ref[...]
        dya = dya_ref[...]
        xc, gi, gr, sp, a, mult, inv_mult = _lru_gates(xa, cw_ref, cb_ref, wx_ref, bx_ref, wa_ref, ba_ref, lam_ref)
        sga = _sigmoid(ga)
        dz_ref[1] = (dya * hl * (sga * (1.0 + ga * (1.0 - sga)))).astype(_MXU_DTYPE)
        c_scr[...] = _shift_up(a, 1)
        d_scr[...] = dya * (ga * sga)
        _scan_backward(c_scr, d_scr, g_scr, seq)
        g = g_scr[...]
        da = g * _shift_down(hl, 1)
        dmult = g * gi * xc
        dgi = g * mult * xc
        dxc = g * mult * gi
        dlog_a = da * a - dmult * (a * a) * inv_mult
        dgr = dlog_a * (-LRU_C) * sp
        dsp = jnp.sum(dlog_a * gr, axis=0, keepdims=True) * (-LRU_C)
        dlam = -dsp * _sigmoid(-lam_ref[...])
        dpi = dgi * gi * (1.0 - gi)
        dpr = dgr * gr * (1.0 - gr)
        dxc = dxc + _block_diag(dpi, wx_ref, transposed=True) + _block_diag(dpr, wa_ref, transposed=True)
        dwx = jnp.stack([_mm_tn(xk, dk) for xk, dk in zip(_lane_blocks(xc), _lane_blocks(dpi))])
        dwa = jnp.stack([_mm_tn(xk, dk) for xk, dk in zip(_lane_blocks(xc), _lane_blocks(dpr))])
        dbx = jnp.sum(dpi, axis=0, keepdims=True)
        dba = jnp.sum(dpr, axis=0, keepdims=True)
        ahead = [dxc if k == CONV_WIDTH - 1 else _shift_up(dxc, CONV_WIDTH - 1 - k) for k in range(CONV_WIDTH)]
        dxa = sum(cw_ref[k:k + 1, :] * ahead[k] for k in range(CONV_WIDTH))
        dz_ref[0] = dxa.astype(_MXU_DTYPE)
        dcb = jnp.sum(dxc, axis=0, keepdims=True)
        dcw = [jnp.sum(ahead[k] * xa, axis=0, keepdims=True) for k in range(CONV_WIDTH)]

        @pl.when(pl.program_id(1) == 0)
        def _():
            for k in range(CONV_WIDTH):
                dcw_ref[k:k + 1, :] = dcw[k]
            dcb_ref[...] = dcb
            dwx_ref[...] = dwx
            dbx_ref[...] = dbx
            dwa_ref[...] = dwa
            dba_ref[...] = dba
            dlam_ref[...] = dlam

        @pl.when(pl.program_id(1) != 0)
        def _():
            for k in range(CONV_WIDTH):
                dcw_ref[k:k + 1, :] += dcw[k]
            dcb_ref[...] += dcb
            dwx_ref[...] += dwx
            dbx_ref[...] += dbx
            dwa_ref[...] += dwa
            dba_ref[...] += dba
            dlam_ref[...] += dlam

    tokens = batch * seq
    blk = pl.BlockSpec((seq, LRU_LANES), lambda c, b: (b, c))
    vec = pl.BlockSpec((1, LRU_LANES), lambda c, b: (0, c))
    mat = pl.BlockSpec((LRU_BLOCKS_PER_STEP, LANES, LANES), lambda c, b: (c, 0, 0))
    vec_shape = jax.ShapeDtypeStruct((1, D_MODEL), F32)
    mat_shape = jax.ShapeDtypeStruct((N_BLK, LANES, LANES), F32)
    return pl.pallas_call(
        body, name="branch_a_bwd",
        grid=(LRU_STEPS, batch),
        in_specs=[pl.BlockSpec((2, seq, LRU_LANES), lambda c, b: (2, b, c)), blk, blk,
                  pl.BlockSpec(memory_space=pl.ANY)] + _lru_param_specs(0),
        out_specs=[pl.BlockSpec((2, seq, LRU_LANES), lambda c, b: (2, b, c)),
                   pl.BlockSpec((CONV_WIDTH, LRU_LANES), lambda c, b: (0, c)), vec, mat, vec, mat, vec, vec],
        out_shape=[jax.ShapeDtypeStruct((N_GROUPS, tokens, D_MODEL), _MXU_DTYPE),
                   jax.ShapeDtypeStruct((CONV_WIDTH, D_MODEL), F32), vec_shape, mat_shape, vec_shape, mat_shape,
                   vec_shape, vec_shape],
        scratch_shapes=[pltpu.VMEM((seq, LRU_LANES), F32)] * 3,
        input_output_aliases={3: 0},
        compiler_params=_params(("parallel", "arbitrary")),
    )(z, hl, dya, dz, conv_w, conv_b, wx, bx, wa, ba, lam)


def _chunk_masks(transposed=False):
    r = lax.broadcasted_iota(jnp.int32, (CHUNK, CHUNK), 0)
    c = lax.broadcasted_iota(jnp.int32, (CHUNK, CHUNK), 1)
    return r <= c if transposed else r >= c


def _row_blocks(seq, fn):
    block = min(256, seq)

    def trip(i, carry):
        fn(pl.ds(pl.multiple_of(i * block, block), block))
        return carry

    lax.fori_loop(0, seq // block, trip, 0)


def _hgrn_prepare(z_ref, lb_ref, f_scr, logf_scr, qh_scr, seq):
    lb = _sigmoid(lb_ref[0:1, :] - lb_ref[1:2, :])

    def block(rows):
        q = z_ref[0, rows, :]
        f = lb + (1.0 - lb) * _sigmoid(z_ref[1, rows, :])
        f_scr[rows, :] = f
        logf_scr[rows, :] = jnp.log(f)
        qh_scr[rows, :] = q * _sigmoid(q)

    _row_blocks(seq, block)
    return lb


def _cumsum_rows(x, reverse=False):
    shift = _shift_up if reverse else _shift_down
    d = 1
    while d < x.shape[0]:
        x = x + shift(x, d)
        d *= 2
    return x


def _lane_mean(x):
    return jnp.mean(x, axis=-1, keepdims=True)


def _token_contractions(lhs_scr, rhs_scr, out_ref, seq):
    rows_id = lax.broadcasted_iota(jnp.int32, (LANES, LANES), 0)

    def transposed(p):
        rows = pl.ds(pl.multiple_of(p * LANES, LANES), LANES)
        return jnp.transpose(lhs_scr[rows, :]).astype(_MXU_DTYPE), rhs_scr[rows, :]

    def contract(p, s):
        lhs_t, rhs = s
        return (_mm(lhs_t, jnp.where(rows_id < CHUNK, rhs, 0.0)), _mm(lhs_t, jnp.where(rows_id >= CHUNK, rhs, 0.0)))

    def store(p, out):
        out_ref[2 * p] = out[0]
        out_ref[2 * p + 1] = out[1]

    _independent_trips(seq // LANES, [transposed, contract], store)


def _chunk_rows(c):
    return pl.ds(pl.multiple_of(c * CHUNK, CHUNK), CHUNK)


def _chunk_terms(c, z_ref, f_scr, qh_scr, b_scr):
    rows = _chunk_rows(c)
    b = b_scr[rows, :]
    b_mid = b_scr[pl.ds(c * CHUNK + CHUNK // 2, 1), :]
    b_last = b_scr[pl.ds(c * CHUNK + CHUNK - 1, 1), :]
    qh = qh_scr[rows, :]
    k = 1.0 - f_scr[rows, :]
    v = z_ref[2, rows, :]
    e_q = jnp.exp(b - b_mid) * HG_SCALE
    e_k = jnp.exp(b_mid - b)
    e_qi = jnp.exp(b) * HG_SCALE
    e_ks = jnp.exp(b_last - b)
    decay = jnp.exp(b_last)
    return rows, qh, k, v, e_q, e_k, e_qi, e_ks, decay


def _independent_trips(n, stages, store, group=CHUNKS_IN_FLIGHT):
    stages = stages if isinstance(stages, (list, tuple)) else [stages]
    group = min(group, n)

    def trip(g, carry):
        ids = [g * group + i for i in range(group)]
        state = [stages[0](c) for c in ids]
        for stage in stages[1:]:
            state = [stage(c, s) for c, s in zip(ids, state)]
        for c, s in zip(ids, state):
            store(c, s)
        return carry

    lax.fori_loop(0, n // group, trip, 0)


def _branch_b_fwd(z, lb_logits, hg_g, batch, seq):
    tokens = batch * seq
    n_chunks = seq // CHUNK

    def body(z_ref, lb_ref, g_ref, yb_ref, st_ref, f_scr, logf_scr, qh_scr, b_scr, o_scr, qi_scr, ks_scr, dec_scr):
        _hgrn_prepare(z_ref, lb_ref, f_scr, logf_scr, qh_scr, seq)
        causal = _chunk_masks()
        gain = g_ref[...]

        def cumulate(c):
            return _cumsum_rows(logf_scr[_chunk_rows(c), :])

        def store_cumulated(c, b):
            b_scr[_chunk_rows(c), :] = b

        def scores(c):
            _, qh, k, v, e_q, e_k, e_qi, e_ks, decay = _chunk_terms(c, z_ref, f_scr, qh_scr, b_scr)
            return _mm_nt(qh * e_q, k * e_k), v, qh * e_qi, k * e_ks, decay

        def within_chunk(c, s):
            att, v, q_int, k_st, decay = s
            return _mm(jnp.where(causal, att, 0.0), v), q_int, k_st, decay

        def store_within_chunk(c, out):
            rows = _chunk_rows(c)
            o_scr[rows, :], qi_scr[rows, :], ks_scr[rows, :], dec_scr[pl.ds(c, 1), :] = out

        def carry_state(c, state_t):
            update = st_ref[c]
            st_ref[c] = state_t
            return state_t * dec_scr[pl.ds(c, 1), :] + update

        def finish(c):
            rows = _chunk_rows(c)
            o = o_scr[rows, :] + _mm_nt(qi_scr[rows, :], st_ref[c])
            r = lax.rsqrt(_lane_mean(o * o) + EPS)
            gb = z_ref[3, rows, :]
            return (((o * r) * gain) * (gb * _sigmoid(gb))).astype(_MXU_DTYPE)

        def store_finished(c, yb):
            yb_ref[_chunk_rows(c), :] = yb

        _independent_trips(n_chunks, cumulate, store_cumulated)
        _independent_trips(n_chunks, [scores, within_chunk], store_within_chunk)
        _token_contractions(z_ref.at[2], ks_scr, st_ref, seq)
        lax.fori_loop(0, n_chunks, carry_state, jnp.zeros((LANES, LANES), F32))
        _independent_trips(n_chunks, finish, store_finished)

    seq_buf = pltpu.VMEM((seq, LANES), F32)
    return pl.pallas_call(
        body, name="branch_b_fwd",
        grid=(batch, N_BLK),
        in_specs=[pl.BlockSpec((4, seq, LANES), lambda b, h: (0, b, h)),
                  pl.BlockSpec((2, LANES), lambda b, h: (0, h)),
                  pl.BlockSpec((1, LANES), lambda b, h: (0, 0))],
        out_specs=[pl.BlockSpec((seq, LANES), lambda b, h: (b, h)),
                   pl.BlockSpec((None, n_chunks, LANES, LANES), lambda b, h: (b * N_BLK + h, 0, 0, 0))],
        out_shape=[jax.ShapeDtypeStruct((tokens, D_MODEL), _MXU_DTYPE),
                   jax.ShapeDtypeStruct((batch * N_BLK, n_chunks, LANES, LANES), F32)],
        scratch_shapes=[seq_buf] * 7 + [pltpu.VMEM((n_chunks, LANES), F32)],
        compiler_params=_params(("parallel", "parallel")),
    )(z, lb_logits, hg_g)


def _branch_b_bwd(z, states, dyb, dz, lb_logits, hg_g, batch, seq):
    n_chunks = seq // CHUNK

    def body(z_ref, st_ref, dyb_ref, dz_in_ref, lb_ref, g_ref, dz_ref, dlog_ref, dg_ref,
             f_scr, logf_scr, qh_scr, b_scr, do_scr, qi_scr, dqh_scr, df_scr, dec_scr, dgp_scr, dlb_scr, dst_scr):
        del dz_in_ref
        first = (pl.program_id(0) == 0) & (pl.program_id(1) == 0)
        lb = _hgrn_prepare(z_ref, lb_ref, f_scr, logf_scr, qh_scr, seq)
        causal = _chunk_masks()
        anti_causal = _chunk_masks(transposed=True)
        gain = g_ref[...]

        @pl.when(first)
        def _():
            dg_ref[...] = jnp.zeros_like(dg_ref)

        @pl.when(pl.program_id(1) == 0)
        def _():
            dlb_scr[...] = jnp.zeros_like(dlb_scr)

        def cumulate(c):
            return _cumsum_rows(logf_scr[_chunk_rows(c), :])

        def store_cumulated(c, b):
            b_scr[_chunk_rows(c), :] = b

        def scores(c):
            _, qh, k, v, e_q, e_k, e_qi, e_ks, decay = _chunk_terms(c, z_ref, f_scr, qh_scr, b_scr)
            q_int = qh * e_qi
            return _mm_nt(qh * e_q, k * e_k), _mm_nt(q_int, st_ref[c]), v, q_int, decay

        def output_gradient(c, s):
            att, o_inter, v, q_int, decay = s
            rows = _chunk_rows(c)
            o = _mm(jnp.where(causal, att, 0.0), v) + o_inter
            r = lax.rsqrt(_lane_mean(o * o) + EPS)
            o_n = o * r
            gb = z_ref[3, rows, :]
            sgb = _sigmoid(gb)
            dyb_c = dyb_ref[rows, :]
            d_ong = dyb_c * (gb * sgb)
            d_gb = (dyb_c * (o_n * gain) * (sgb * (1.0 + gb * (1.0 - sgb)))).astype(_MXU_DTYPE)
            d_gain = jnp.sum(d_ong * o_n, axis=0, keepdims=True)
            d_on = d_ong * gain
            return d_gb, d_gain, r * (d_on - o_n * _lane_mean(d_on * o_n)), q_int, decay

        def store_output_gradient(c, out):
            rows = _chunk_rows(c)
            dz_ref[3, rows, :], dgp_scr[pl.ds(c, 1), :], do_scr[rows, :], qi_scr[rows, :], dec_scr[pl.ds(c, 1), :] = out

        def carry_state_gradient(cc, d_state_t):
            c = n_chunks - 1 - cc
            update = dst_scr[c]
            dst_scr[c] = d_state_t
            return d_state_t * dec_scr[pl.ds(c, 1), :] + update

        def score_gradients(c):
            rows, qh, k, v, e_q, e_k, e_qi, e_ks, decay = _chunk_terms(c, z_ref, f_scr, qh_scr, b_scr)
            state_t = st_ref[c]
            d_state_t = dst_scr[c]
            d_o = do_scr[rows, :]
            q_in, k_in, q_int, k_st = qh * e_q, k * e_k, qh * e_qi, k * e_ks
            first = (_mm_nt(k_in, q_in), _mm_nt(d_o, v), _mm_nt(v, d_o), _mm_nt(k_st, d_state_t), _mm(d_o, state_t),
                     _mm(v, d_state_t))
            d_decay = jnp.sum(state_t * d_state_t, axis=0, keepdims=True)
            return first, d_o, q_in, k_in, q_int, k_st, e_q, e_k, e_qi, e_ks, decay, d_decay

        def input_gradients(c, s):
            (att_t, d_att, d_att_t, dv_inter, dq_int, dk_st), d_o, q_in, k_in, q_int, k_st, e_q, e_k, e_qi, e_ks, decay, d_decay = s
            rows = _chunk_rows(c)
            d_v = _mm(jnp.where(anti_causal, att_t, 0.0), d_o) + dv_inter
            dq_in = _mm(jnp.where(causal, d_att, 0.0), k_in)
            dk_in = _mm(jnp.where(anti_causal, d_att_t, 0.0), q_in)
            d_k = dk_in * e_k + dk_st * e_ks
            kk = dk_st * k_st
            d_b = dq_in * q_in + dq_int * q_int - dk_in * k_in - kk
            d_b_last = jnp.sum(kk, axis=0, keepdims=True) + decay * d_decay
            d_logf = _cumsum_rows(d_b, reverse=True) + d_b_last
            return d_v.astype(_MXU_DTYPE), dq_in * e_q + dq_int * e_qi, d_logf / f_scr[rows, :] - d_k

        def store_input_gradients(c, out):
            rows = _chunk_rows(c)
            dz_ref[2, rows, :], dqh_scr[rows, :], df_scr[rows, :] = out

        def input_activations(rows):
            q = z_ref[0, rows, :]
            sq = _sigmoid(q)
            dz_ref[0, rows, :] = (dqh_scr[rows, :] * (sq * (1.0 + q * (1.0 - sq)))).astype(_MXU_DTYPE)
            sg = _sigmoid(z_ref[1, rows, :])
            d_f = df_scr[rows, :]
            dz_ref[1, rows, :] = (d_f * (1.0 - lb) * sg * (1.0 - sg)).astype(_MXU_DTYPE)
            dlb_scr[...] += jnp.sum(d_f * (1.0 - sg), axis=0, keepdims=True)

        _independent_trips(n_chunks, cumulate, store_cumulated)
        _independent_trips(n_chunks, [scores, output_gradient], store_output_gradient)
        _token_contractions(do_scr, qi_scr, dst_scr, seq)
        lax.fori_loop(0, n_chunks, carry_state_gradient, jnp.zeros((LANES, LANES), F32))
        _independent_trips(n_chunks, [score_gradients, input_gradients], store_input_gradients)
        dg_ref[...] += jnp.sum(dgp_scr[...], axis=0, keepdims=True)
        _row_blocks(seq, input_activations)
        d_l0 = dlb_scr[...] * lb * (1.0 - lb)
        dlog_ref[0:1, :] = d_l0
        dlog_ref[1:2, :] = -d_l0

    tokens = batch * seq
    seq_buf = pltpu.VMEM((seq, LANES), F32)
    chunk_rows = pltpu.VMEM((n_chunks, LANES), F32)
    return pl.pallas_call(
        body, name="branch_b_bwd",
        grid=(N_BLK, batch),
        in_specs=[pl.BlockSpec((4, seq, LANES), lambda h, b: (0, b, h)),
                  pl.BlockSpec((None, n_chunks, LANES, LANES), lambda h, b: (b * N_BLK + h, 0, 0, 0)),
                  pl.BlockSpec((seq, LANES), lambda h, b: (b, h)),
                  pl.BlockSpec(memory_space=pl.ANY),
                  pl.BlockSpec((2, LANES), lambda h, b: (0, h)),
                  pl.BlockSpec((1, LANES), lambda h, b: (0, 0))],
        out_specs=[pl.BlockSpec((4, seq, LANES), lambda h, b: (0, b, h)),
                   pl.BlockSpec((2, LANES), lambda h, b: (0, h)),
                   pl.BlockSpec((1, LANES), lambda h, b: (0, 0))],
        out_shape=[jax.ShapeDtypeStruct((N_GROUPS, tokens, D_MODEL), _MXU_DTYPE),
                   jax.ShapeDtypeStruct((2, D_MODEL), F32),
                   jax.ShapeDtypeStruct((1, LANES), F32)],
        scratch_shapes=[seq_buf] * 8 + [chunk_rows, chunk_rows, pltpu.VMEM((1, LANES), F32),
                                        pltpu.VMEM((n_chunks, LANES, LANES), F32)],
        input_output_aliases={3: 0},
        compiler_params=_params(("arbitrary", "arbitrary")),
    )(z, states, dyb, dz, lb_logits, hg_g)


def _merge_tail(ya, yb, z, x2d, tgt2d, b_merge, final_g, pa, pb, wo):
    tokens, d = x2d.shape
    tm = min(256, tokens)
    n_tiles = tokens // tm

    def body(ya_ref, yb_ref, z_ref, x_ref, t_ref, bm_ref, fg_ref, pa_hbm, pb_hbm, wo_hbm,
             dya_ref, dyb_ref, dx2_ref, dz_ref, loss_ref, dfg_ref, dbm_ref, dpa_hbm, dpb_hbm, dwo_hbm,
             pa_s, pb_s, wo_s, dpa_s, dpb_s, dwo_s):
        i = pl.program_id(0)

        @pl.when(i == 0)
        def _():
            pltpu.sync_copy(pa_hbm, pa_s)
            pltpu.sync_copy(pb_hbm, pb_s)
            pltpu.sync_copy(wo_hbm, wo_s)
            dpa_s[...] = jnp.zeros_like(dpa_s)
            dpb_s[...] = jnp.zeros_like(dpb_s)
            dwo_s[...] = jnp.zeros_like(dwo_s)
            loss_ref[...] = jnp.zeros_like(loss_ref)
            dfg_ref[...] = jnp.zeros_like(dfg_ref)
            dbm_ref[...] = jnp.zeros_like(dbm_ref)

        ya_t = ya_ref[...]
        yb_t = yb_ref[...]
        out_a = _mm(ya_t, pa_s[...])
        out_b = _mm(yb_t, pb_s[...])
        g_a = _sigmoid(z_ref[0] + bm_ref[:, :d])
        g_b = _sigmoid(z_ref[1] + bm_ref[:, d:])
        mixed = g_a * out_a + g_b * out_b
        x2 = x_ref[...] + _mm(mixed, wo_s[...])
        r = lax.rsqrt(jnp.mean(x2 * x2, axis=-1, keepdims=True) + EPS)
        xn = x2 * r
        fg = fg_ref[...]
        diff = xn * fg - t_ref[...]
        loss_ref[...] += jnp.sum(diff * diff) * (0.5 / d)
        dy = diff * (1.0 / d)
        dfg_ref[...] += jnp.sum(dy * xn, axis=0, keepdims=True)
        dxn = dy * fg
        dx2 = r * (dxn - xn * jnp.mean(dxn * xn, axis=-1, keepdims=True))
        dx2_ref[...] = dx2
        dmixed = _mm_nt(dx2, wo_s[...])
        dwo_s[...] += _mm_tn(mixed, dx2)
        dgm_a = dmixed * out_a * g_a * (1.0 - g_a)
        dgm_b = dmixed * out_b * g_b * (1.0 - g_b)
        dz_ref[0] = dgm_a.astype(_MXU_DTYPE)
        dz_ref[1] = dgm_b.astype(_MXU_DTYPE)
        dbm_ref[:, :d] += jnp.sum(dgm_a, axis=0, keepdims=True)
        dbm_ref[:, d:] += jnp.sum(dgm_b, axis=0, keepdims=True)
        dout_a = dmixed * g_a
        dout_b = dmixed * g_b
        dpa_s[...] += _mm_tn(ya_t, dout_a)
        dpb_s[...] += _mm_tn(yb_t, dout_b)
        dya_ref[...] = _mm_nt(dout_a, pa_s[...])
        dyb_ref[...] = _mm_nt(dout_b, pb_s[...])

        @pl.when(i == n_tiles - 1)
        def _():
            pltpu.sync_copy(dpa_s, dpa_hbm)
            pltpu.sync_copy(dpb_s, dpb_hbm)
            pltpu.sync_copy(dwo_s, dwo_hbm)

    tile = pl.BlockSpec((tm, d), lambda i: (i, 0))
    gm = pl.BlockSpec((2, tm, d), lambda i: (3, i, 0))
    row = lambda n: pl.BlockSpec((1, n), lambda i: (0, 0))
    hbm = pl.BlockSpec(memory_space=pl.ANY)
    act = jax.ShapeDtypeStruct((tokens, d), F32)
    mat = jax.ShapeDtypeStruct((d, d), F32)
    return pl.pallas_call(
        body, name="merge_tail",
        grid=(n_tiles,),
        in_specs=[tile, tile, gm, tile, tile, row(2 * d), row(d), hbm, hbm, hbm],
        out_specs=[tile, tile, tile, gm, row(LANES), row(d), row(2 * d), hbm, hbm, hbm],
        out_shape=[act, act, act, jax.ShapeDtypeStruct((N_GROUPS, tokens, d), _MXU_DTYPE),
                   jax.ShapeDtypeStruct((1, LANES), F32), jax.ShapeDtypeStruct((1, d), F32),
                   jax.ShapeDtypeStruct((1, 2 * d), F32), mat, mat, mat],
        scratch_shapes=[pltpu.VMEM((d, d), _MXU_DTYPE)] * 3 + [pltpu.VMEM((d, d), F32)] * 3,
        compiler_params=_params(("arbitrary",)),
    )(ya, yb, z, x2d, tgt2d, b_merge, final_g, pa, pb, wo)


def _inproj_dw(h_t, dz):
    d, tokens = h_t.shape
    tm = min(2048, tokens)

    def body(h_ref, dz_ref, dw_ref):
        part = _mm(h_ref[...], dz_ref[...])

        @pl.when(pl.program_id(1) == 0)
        def _():
            dw_ref[...] = part

        @pl.when(pl.program_id(1) != 0)
        def _():
            dw_ref[...] += part

    def out_index(s, i):
        g = _group_of_slot(s)
        return (g // 2, 0, g % 2)

    return pl.pallas_call(
        body, name="inproj_dw",
        grid=(N_GROUPS, tokens // tm),
        in_specs=[pl.BlockSpec((d, tm), lambda s, i: (0, i)),
                  pl.BlockSpec((None, tm, D_MODEL), lambda s, i: (s, i, 0))],
        out_specs=pl.BlockSpec((None, d, D_MODEL), out_index),
        out_shape=jax.ShapeDtypeStruct((N_SHARDS, d, 2 * D_MODEL), F32),
        compiler_params=_params(("parallel", "arbitrary")),
    )(h_t, dz)


def _inproj_dw_exchange(h_t, dz, scatter):
    d, tokens = h_t.shape
    tm = min(2048, tokens)
    n_i = tokens // tm
    half = d // 2

    def body(h_ref, dz_ref, *rest):
        n_in, n_out = scatter.n_in, scatter.n_out
        dw_hbm, land_hbm = rest[n_in:n_in + 2]
        acc, local_sems, send_sems, recv_sems = rest[n_in + 2 + n_out:n_in + 6 + n_out]
        carried = scatter.copies(rest[:n_in], rest[n_in + 2:n_in + 2 + n_out], rest[n_in + 6 + n_out:])
        s, i = pl.program_id(0), pl.program_id(1)
        x, y, c, _ = _mesh_position()

        @pl.when((s == 0) & (i == 0))
        def _():
            for cp in carried:
                cp.start()

        part = _mm(h_ref[...], dz_ref[...])
        buf = acc.at[s % 2]

        @pl.when(i == 0)
        def _():
            buf[...] = part

        @pl.when(i != 0)
        def _():
            buf[...] += part

        def copies(k):
            g = _SLOT_TO_GROUP[k]
            cols = pl.ds((g % 2) * D_MODEL, D_MODEL)
            src = acc.at[k % 2]
            mine = pltpu.make_async_copy(src, dw_hbm.at[g // 2, :, cols], local_sems.at[k % 2])
            theirs = _remote(src.at[pl.ds((1 - c) * half, half), :], land_hbm.at[g // 2, :, cols],
                             send_sems, recv_sems, k, (x, y, 1 - c))
            return mine, theirs

        for k in range(N_GROUPS):
            @pl.when((s == k) & (i == n_i - 1))
            def _(k=k):
                if k > 0:
                    mine, theirs = copies(k - 1)
                    mine.wait()
                    theirs.wait_send()
                mine, theirs = copies(k)
                mine.start()
                theirs.start()
                if k == N_GROUPS - 1:
                    mine.wait()
                    theirs.wait_send()
                    for kk in range(N_GROUPS):
                        copies(kk)[1].wait_recv()
                    for cp in carried:
                        cp.wait()

    hbm = pl.BlockSpec(memory_space=pl.ANY)
    more = scatter.plumbing(first_operand=2, first_output=2)
    return pl.pallas_call(
        body, name="inproj_dw_exchange",
        grid=(N_GROUPS, n_i),
        in_specs=[pl.BlockSpec((d, tm), lambda s, i: (0, i)),
                  pl.BlockSpec((None, tm, D_MODEL), lambda s, i: (s, i, 0))] + more[1],
        out_specs=[hbm, hbm] + more[2],
        out_shape=[jax.ShapeDtypeStruct((N_SHARDS, d, 2 * D_MODEL), F32),
                   jax.ShapeDtypeStruct((N_SHARDS, half, 2 * D_MODEL), F32)] + more[3],
        scratch_shapes=[pltpu.VMEM((2, d, D_MODEL), F32), pltpu.SemaphoreType.DMA((2,)),
                        pltpu.SemaphoreType.DMA((N_GROUPS,)), pltpu.SemaphoreType.DMA((N_GROUPS,))] + more[4],
        input_output_aliases=more[5],
        compiler_params=_params(("arbitrary", "arbitrary")),
    )(h_t, dz, *more[0])


def _inproj_dx(dz, w_all, x2d, dx2, norm_g, scatter=None):
    tokens, d = x2d.shape
    tm = min(256, tokens)
    n_tiles = tokens // tm

    def body(dz_ref, w_hbm, x_ref, dx2_ref, g_ref, *rest):
        if scatter:
            n_in, n_out = scatter.n_in, scatter.n_out
            dx_ref, dg_ref = rest[n_in:n_in + 2]
            w_res = rest[n_in + 2 + n_out]
            copies = scatter.copies(rest[:n_in], rest[n_in + 2:n_in + 2 + n_out], rest[n_in + 3 + n_out:])
        else:
            dx_ref, dg_ref, w_res = rest
            copies = []
        i = pl.program_id(0)

        @pl.when(i == 0)
        def _():
            for cp in copies:
                cp.start()
            for slot, g in enumerate(_SLOT_TO_GROUP):
                pltpu.sync_copy(w_hbm.at[g // 2, :, pl.ds((g % 2) * D_MODEL, D_MODEL)],
                                w_res.at[:, pl.ds(slot * D_MODEL, D_MODEL)])
            dg_ref[...] = jnp.zeros_like(dg_ref)

        dz_all = jnp.concatenate([dz_ref[s] for s in range(N_GROUPS)], axis=1)
        dh = jnp.transpose(_mm_nt(w_res[...], dz_all))
        x = x_ref[...]
        r = lax.rsqrt(jnp.mean(x * x, axis=-1, keepdims=True) + EPS)
        xn = x * r
        dg_ref[...] += jnp.sum(dh * xn, axis=0, keepdims=True)
        dxn = dh * g_ref[...]
        dx_ref[...] = r * (dxn - xn * jnp.mean(dxn * xn, axis=-1, keepdims=True)) + dx2_ref[...]

        @pl.when(i == n_tiles - 1)
        def _():
            for cp in copies:
                cp.wait()

    tile = pl.BlockSpec((tm, d), lambda i: (i, 0))
    hbm = pl.BlockSpec(memory_space=pl.ANY)
    in_specs = [pl.BlockSpec((N_GROUPS, tm, D_MODEL), lambda i: (0, i, 0)), hbm, tile, tile,
                pl.BlockSpec((1, d), lambda i: (0, 0))]
    out_specs = [tile, pl.BlockSpec((1, d), lambda i: (0, 0))]
    out_shape = [jax.ShapeDtypeStruct((tokens, d), F32), jax.ShapeDtypeStruct((1, d), F32)]
    scratch = [pltpu.VMEM((d, N_GROUPS * D_MODEL), _MXU_DTYPE)]
    operands, aliases = [dz, w_all, x2d, dx2, norm_g], {}
    if scatter:
        more = scatter.plumbing(first_operand=len(operands), first_output=len(out_shape))
        operands, in_specs, out_specs = operands + more[0], in_specs + more[1], out_specs + more[2]
        out_shape, scratch, aliases = out_shape + more[3], scratch + more[4], more[5]
    return pl.pallas_call(
        body, name="inproj_dx", grid=(n_tiles,), in_specs=in_specs, out_specs=out_specs, out_shape=out_shape,
        scratch_shapes=scratch, input_output_aliases=aliases,
        compiler_params=_params(("arbitrary",)),
    )(*operands)


def _row_tile(rows, cols, itemsize=4, budget=2 * 1024 * 1024):
    tr = rows
    while tr * cols * itemsize > budget and tr % 16 == 0:
        tr //= 2
    return tr


def _cast_into_slot(a, chip, dtype, name):
    rows, cols = a.shape
    tr = _row_tile(rows, cols)

    def body(chip_ref, a_ref, o_ref):
        del chip_ref
        o_ref[...] = a_ref[...].astype(dtype)

    grid_spec = pltpu.PrefetchScalarGridSpec(
        num_scalar_prefetch=1, grid=(rows // tr,),
        in_specs=[pl.BlockSpec((tr, cols), lambda i, chip_ref: (i, 0))],
        out_specs=pl.BlockSpec((None, tr, cols), lambda i, chip_ref: (chip_ref[0], i, 0)))
    return pl.pallas_call(body, name=name, grid_spec=grid_spec,
                          out_shape=jax.ShapeDtypeStruct((N_SHARDS, rows, cols), dtype),
                          compiler_params=_params(("arbitrary",)))(chip, a)


def _sum_slots(stack, name):
    n, rows, cols = stack.shape
    tr = _row_tile(rows, cols * n)

    def body(s_ref, o_ref):
        total = s_ref[0].astype(F32)
        for k in range(1, n):
            total = total + s_ref[k].astype(F32)
        o_ref[...] = total

    return pl.pallas_call(body, name=name, grid=(rows // tr,),
                          in_specs=[pl.BlockSpec((n, tr, cols), lambda i: (0, i, 0))],
                          out_specs=pl.BlockSpec((tr, cols), lambda i: (i, 0)),
                          out_shape=jax.ShapeDtypeStruct((rows, cols), F32),
                          compiler_params=_params(("parallel",)))(stack)


def _add_half(full, landed, place, name):
    n, rows, cols = full.shape
    half = rows // 2
    tr = _row_tile(half, cols)
    nb = half // tr

    def body(place_ref, a_ref, b_ref, o_ref, own_ref):
        total = (a_ref[...] + b_ref[...]).astype(_MXU_DTYPE)
        o_ref[...] = total

        @pl.when(pl.program_id(1) == place_ref[1])
        def _():
            own_ref[...] = total

    grid_spec = pltpu.PrefetchScalarGridSpec(
        num_scalar_prefetch=1, grid=(nb, n),
        in_specs=[pl.BlockSpec((None, tr, cols), lambda i, j, place_ref: (j, place_ref[0] * nb + i, 0)),
                  pl.BlockSpec((None, tr, cols), lambda i, j, place_ref: (j, i, 0))],
        out_specs=[pl.BlockSpec((None, tr, cols), lambda i, j, place_ref: (j, i, 0)),
                   pl.BlockSpec((None, tr, cols), lambda i, j, place_ref: (place_ref[1], i, 0))])
    shape = jax.ShapeDtypeStruct((n, half, cols), _MXU_DTYPE)
    return pl.pallas_call(body, name=name, grid_spec=grid_spec, out_shape=[shape, shape],
                          compiler_params=_params(("parallel", "arbitrary")))(place, full, landed)


def _adamw_update(w, grad, m, v):
    c1 = 1.0 - ADAM_B1 ** ADAM_STEP
    c2 = 1.0 - ADAM_B2 ** ADAM_STEP
    nm = ADAM_B1 * m + (1.0 - ADAM_B1) * grad
    nv = ADAM_B2 * v + (1.0 - ADAM_B2) * (grad * grad)
    return (-ADAM_LR) * ((nm / c1) / (jnp.sqrt(nv / c2) + ADAM_EPS) + ADAM_WD * w), nm, nv


def _adamw(w, g, m, v, name):
    rows, cols = w.shape
    tr = _row_tile(rows, cols, budget=1024 * 1024)

    def body(w_ref, g_ref, m_ref, v_ref, d_ref, nm_ref, nv_ref):
        d_ref[...], nm_ref[...], nv_ref[...] = _adamw_update(w_ref[...], g_ref[...], m_ref[...], v_ref[...])

    spec = pl.BlockSpec((tr, cols), lambda i: (i, 0))
    shape = jax.ShapeDtypeStruct((rows, cols), F32)
    return pl.pallas_call(body, name=name, grid=(rows // tr,), in_specs=[spec] * 4, out_specs=[spec] * 3,
                          out_shape=[shape] * 3, compiler_params=_params(("parallel",)))(w, g, m, v)


def _adamw_halves(w, g_mine, g_sibling, m, v, core, name):
    rows, cols = w.shape
    half = rows // 2
    tr = _row_tile(half, cols, budget=1024 * 1024)
    nb = half // tr

    def body(core_ref, w_ref, gm_ref, gs_ref, m_ref, v_ref, g_ref, d_ref, nm_ref, nv_ref):
        mine = pl.program_id(0) // nb == core_ref[0]
        grad = jnp.where(mine, gm_ref[...], gs_ref[...])
        g_ref[...] = grad
        d_ref[...], nm_ref[...], nv_ref[...] = _adamw_update(w_ref[...], grad, m_ref[...], v_ref[...])

    spec = pl.BlockSpec((tr, cols), lambda i, core_ref: (i, 0))
    half_spec = pl.BlockSpec((tr, cols), lambda i, core_ref: (i % nb, 0))
    grid_spec = pltpu.PrefetchScalarGridSpec(num_scalar_prefetch=1, grid=(rows // tr,),
                                             in_specs=[spec, half_spec, half_spec, spec, spec], out_specs=[spec] * 4)
    shape = jax.ShapeDtypeStruct((rows, cols), F32)
    return pl.pallas_call(body, name=name, grid_spec=grid_spec, out_shape=[shape] * 4,
                          compiler_params=_params(("parallel",)))(core, w, g_mine, g_sibling, m, v)


def _local_step(x, loss_target, w_all, pa, pb, wo, conv_w, b_merge, conv_b, rg_wx, rg_bx, rg_wa, rg_ba,
                rg_lambda, hg_lb_logits, hg_norm_g, norm_g, final_norm_g, gather=None, reduction=None):
    batch, seq, d = x.shape
    x2d = x.reshape(batch * seq, d)
    tgt2d = loss_target.reshape(batch * seq, d)
    if gather is None:
        z, h_t = _inproj_fwd(x2d, norm_g, w_all)
    else:
        z, h_t, (w_all, pa, pb, wo), cw_all = _inproj_fwd_gather(x2d, norm_g, *gather)
        pa, pb, wo = (t.reshape(d, d) for t in (pa, pb, wo))
        conv_w = jnp.transpose(cw_all, (1, 0, 2)).reshape(CONV_WIDTH, d)
    lru = (conv_w, conv_b, rg_wx, rg_bx, rg_wa, rg_ba, rg_lambda)
    ya, hl = _branch_a_fwd(z, *lru, batch, seq)
    yb, states = _branch_b_fwd(z, hg_lb_logits, hg_norm_g, batch, seq)
    dya, dyb, dx2, dz, loss, d_final_g, d_b_merge, d_pa, d_pb, d_wo = _merge_tail(
        ya, yb, z, x2d, tgt2d, b_merge, final_norm_g, pa, pb, wo)
    dz, d_lb_logits, d_hg_g = _branch_b_bwd(z, states, dyb, dz, hg_lb_logits, hg_norm_g, batch, seq)
    dz, d_conv_w, d_conv_b, d_wx, d_bx, d_wa, d_ba, d_lam = _branch_a_bwd(z, hl, dya, dz, *lru, batch, seq)
    small = dict(b_merge=d_b_merge, conv_w=d_conv_w, conv_b=d_conv_b, rg_wx=d_wx, rg_bx=d_bx, rg_wa=d_wa,
                 rg_ba=d_ba, rg_lambda=d_lam, hg_lb_logits=d_lb_logits, hg_norm_g=d_hg_g,
                 norm_g=jnp.zeros((1, d), F32), final_norm_g=d_final_g)
    if reduction is None:
        big = (_inproj_dw(h_t, dz), d_pa, d_pb, d_wo)
        grad_x, small["norm_g"] = _inproj_dx(dz, w_all, x2d, dx2, norm_g)
        return loss[0, 0], grad_x.reshape(batch, seq, d), big, small
    first, second = reduction
    d_w_in, landed_w_in, *scattered_first = _inproj_dw_exchange(h_t, dz, first((d_pa, d_pb, d_wo), small))
    grad_x, d_norm_g, *scattered_second = _inproj_dx(dz, w_all, x2d, dx2, norm_g, scatter=second(d_w_in, landed_w_in))
    return loss[0, 0], grad_x.reshape(batch, seq, d), d_norm_g, (scattered_first, scattered_second)


_SMALL_ORDER = ("b_merge", "conv_w", "conv_b", "rg_wx", "rg_bx", "rg_wa", "rg_ba", "rg_lambda", "hg_lb_logits",
                "hg_norm_g", "norm_g", "final_norm_g")
N_DEV = 8
PIECE_ROWS = 272


def _pack_small(tree):
    flat = jnp.concatenate([tree[k].reshape(-1) for k in _SMALL_ORDER])
    flat = jnp.pad(flat, (0, N_DEV * PIECE_ROWS * LANES - flat.shape[0]))
    return flat.reshape(N_DEV * PIECE_ROWS, LANES)


def _unpack_small(packed, like):
    flat = packed.reshape(-1)
    out, pos = {}, 0
    for k in _SMALL_ORDER:
        n = like[k].size
        out[k] = flat[pos:pos + n].reshape(like[k].shape)
        pos += n
    return out


def _mesh_position():
    x, y, c = lax.axis_index("x"), lax.axis_index("y"), lax.axis_index("c")
    other_chips = [(1 - x, y), (x, 1 - y), (1 - x, 1 - y)]
    return x, y, c, other_chips


def _other_devices(x, y, c):
    flips = [(fx, fy, fc) for fx in (0, 1) for fy in (0, 1) for fc in (0, 1) if (fx, fy, fc) != (0, 0, 0)]
    return [(jnp.where(fx, 1 - x, x), jnp.where(fy, 1 - y, y), jnp.where(fc, 1 - c, c)) for fx, fy, fc in flips]


def _remote(src, dst, send_sems, recv_sems, k, device):
    return pltpu.make_async_remote_copy(src_ref=src, dst_ref=dst, send_sem=send_sems.at[k], recv_sem=recv_sems.at[k],
                                        device_id=device, device_id_type=MESH)


def _exchange_halves(bigs, small):
    n_big = len(bigs)
    n_sem = n_big + N_DEV - 1

    def body(*refs):
        srcs, small_src = refs[:n_big], refs[n_big]
        outs, small_out = refs[n_big + 1:2 * n_big + 1], refs[2 * n_big + 1]
        send_sems, recv_sems, local_sem = refs[2 * n_big + 2:]
        x, y, c, _ = _mesh_position()
        me, sibling = 4 * x + 2 * y + c, (x, y, 1 - c)
        mine = pltpu.make_async_copy(small_src.at[pl.ds(me * PIECE_ROWS, PIECE_ROWS), :], small_out.at[me], local_sem)
        mine.start()
        copies = []
        for a in range(n_big):
            hs = srcs[a].shape[1] // 2
            copies.append(_remote(srcs[a].at[:, pl.ds((1 - c) * hs, hs), :], outs[a], send_sems, recv_sems, a, sibling))
        for k, (px, py, pc) in enumerate(_other_devices(x, y, c)):
            piece = small_src.at[pl.ds((4 * px + 2 * py + pc) * PIECE_ROWS, PIECE_ROWS), :]
            copies.append(_remote(piece, small_out.at[me], send_sems, recv_sems, n_big + k, (px, py, pc)))
        for cp in copies:
            cp.start()
        for cp in copies:
            cp.wait()
        mine.wait()

    hbm = pl.BlockSpec(memory_space=pl.ANY)
    out_shape = [jax.ShapeDtypeStruct((g.shape[0], g.shape[1] // 2, g.shape[2]), F32) for g in bigs]
    out_shape.append(jax.ShapeDtypeStruct((N_DEV, PIECE_ROWS, LANES), F32))
    return pl.pallas_call(
        body, name="exchange_halves",
        in_specs=[hbm] * (n_big + 1), out_specs=[hbm] * (n_big + 1), out_shape=out_shape,
        scratch_shapes=[pltpu.SemaphoreType.DMA((n_sem,)), pltpu.SemaphoreType.DMA((n_sem,)), pltpu.SemaphoreType.DMA],
    )(*bigs, small)


class _Scatter:
    def __init__(self, bigs, by_chip, small=None):
        self.bigs, self.by_chip, self.small = list(bigs), list(by_chip), small
        self.n_big = len(self.bigs)
        self.n_in = 2 * self.n_big + (small is not None)
        self.n_out = self.n_big + (small is not None)
        self.n_scratch = 2 + (small is not None)

    def plumbing(self, first_operand, first_output):
        hbm = pl.BlockSpec(memory_space=pl.ANY)
        n_sem = 3 * self.n_big + (N_DEV - 1 if self.small is not None else 0)
        operands = self.bigs + self.by_chip + ([self.small] if self.small is not None else [])
        out_shapes = [jax.ShapeDtypeStruct(g.shape, g.dtype) for g in self.by_chip]
        scratch = [pltpu.SemaphoreType.DMA((n_sem,)), pltpu.SemaphoreType.DMA((n_sem,))]
        if self.small is not None:
            out_shapes.append(jax.ShapeDtypeStruct((N_DEV, PIECE_ROWS, LANES), F32))
            scratch.append(pltpu.SemaphoreType.DMA)
        aliases = {first_operand + self.n_big + a: first_output + a for a in range(self.n_big)}
        return operands, [hbm] * self.n_in, [hbm] * self.n_out, out_shapes, scratch, aliases

    def copies(self, in_refs, out_refs, scratch_refs):
        srcs, outs = in_refs[:self.n_big], out_refs[:self.n_big]
        send_sems, recv_sems = scratch_refs[:2]
        x, y, c, chips = _mesh_position()
        chip, me = 2 * x + y, 4 * x + 2 * y + c
        copies = []
        for a in range(self.n_big):
            for j, (cx, cy) in enumerate(chips):
                copies.append(_remote(srcs[a].at[2 * cx + cy], outs[a].at[chip], send_sems, recv_sems, 3 * a + j,
                                      (cx, cy, c)))
        if self.small is not None:
            small_src, small_out = in_refs[2 * self.n_big], out_refs[self.n_big]
            copies.append(pltpu.make_async_copy(small_src, small_out.at[me], scratch_refs[2]))
            for k, peer in enumerate(_other_devices(x, y, c)):
                copies.append(_remote(small_src, small_out.at[me], send_sems, recv_sems, 3 * self.n_big + k, peer))
        return copies


def _swap_halves(halves, vec):
    n_big = len(halves)

    def body(*refs):
        srcs, vec_src = refs[:n_big], refs[n_big]
        outs, vec_out = refs[n_big + 1:2 * n_big + 1], refs[2 * n_big + 1]
        send_sems, recv_sems, local_sem = refs[2 * n_big + 2:]
        x, y, c, _ = _mesh_position()
        me = 4 * x + 2 * y + c
        copies = [pltpu.make_async_copy(vec_src, vec_out.at[me], local_sem)]
        copies += [_remote(srcs[a], outs[a], send_sems, recv_sems, a, (x, y, 1 - c)) for a in range(n_big)]
        copies += [_remote(vec_src, vec_out.at[me], send_sems, recv_sems, n_big + k, peer)
                   for k, peer in enumerate(_other_devices(x, y, c))]
        for cp in copies:
            cp.start()
        for cp in copies:
            cp.wait()

    hbm = pl.BlockSpec(memory_space=pl.ANY)
    n_sem = n_big + N_DEV - 1
    return pl.pallas_call(
        body, name="swap_halves",
        in_specs=[hbm] * (n_big + 1), out_specs=[hbm] * (n_big + 1),
        out_shape=[jax.ShapeDtypeStruct(h.shape, F32) for h in halves] + [jax.ShapeDtypeStruct((N_DEV,) + vec.shape, F32)],
        scratch_shapes=[pltpu.SemaphoreType.DMA((n_sem,)), pltpu.SemaphoreType.DMA((n_sem,)), pltpu.SemaphoreType.DMA],
    )(*halves, vec)


def kernel(x, w_in, b_merge, conv_w, conv_b, rg_wx, rg_bx, rg_wa, rg_ba, rg_lambda, hg_lb_logits, hg_norm_g, proj_a, proj_b, w_out, norm_g, final_norm_g, loss_target, m_w_in, m_b_merge, m_conv_w, m_conv_b, m_rg_wx, m_rg_bx, m_rg_wa, m_rg_ba, m_rg_lambda, m_hg_lb_logits, m_hg_norm_g, m_proj_a, m_proj_b, m_w_out, m_norm_g, m_final_norm_g, v_w_in, v_b_merge, v_conv_w, v_conv_b, v_rg_wx, v_rg_bx, v_rg_wa, v_rg_ba, v_rg_lambda, v_hg_lb_logits, v_hg_norm_g, v_proj_a, v_proj_b, v_w_out, v_norm_g, v_final_norm_g):
    d = D_MODEL
    weights = dict(w_in=w_in, b_merge=b_merge, conv_w=conv_w, conv_b=conv_b, rg_wx=rg_wx, rg_bx=rg_bx, rg_wa=rg_wa,
                   rg_ba=rg_ba, rg_lambda=rg_lambda, hg_lb_logits=hg_lb_logits, hg_norm_g=hg_norm_g, proj_a=proj_a,
                   proj_b=proj_b, w_out=w_out, norm_g=norm_g, final_norm_g=final_norm_g)
    m = dict(w_in=m_w_in, b_merge=m_b_merge, conv_w=m_conv_w, conv_b=m_conv_b, rg_wx=m_rg_wx, rg_bx=m_rg_bx,
             rg_wa=m_rg_wa, rg_ba=m_rg_ba, rg_lambda=m_rg_lambda, hg_lb_logits=m_hg_lb_logits, hg_norm_g=m_hg_norm_g,
             proj_a=m_proj_a, proj_b=m_proj_b, w_out=m_w_out, norm_g=m_norm_g, final_norm_g=m_final_norm_g)
    v = dict(w_in=v_w_in, b_merge=v_b_merge, conv_w=v_conv_w, conv_b=v_conv_b, rg_wx=v_rg_wx, rg_bx=v_rg_bx,
             rg_wa=v_rg_wa, rg_ba=v_rg_ba, rg_lambda=v_rg_lambda, hg_lb_logits=v_hg_lb_logits, hg_norm_g=v_hg_norm_g,
             proj_a=v_proj_a, proj_b=v_proj_b, w_out=v_w_out, norm_g=v_norm_g, final_norm_g=v_final_norm_g)
    big_names = ("w_in", "proj_a", "proj_b", "w_out")

    core = lax.axis_index("c").astype(jnp.int32).reshape(1)
    chip = (2 * lax.axis_index("x") + lax.axis_index("y")).astype(jnp.int32)

    slotted = [_cast_into_slot(weights[k][0], chip.reshape(1), _MXU_DTYPE, f"cast_{k}") for k in big_names]
    conv_slotted = _cast_into_slot(conv_w[0], chip.reshape(1), F32, "slot_conv_w")

    small_shapes = {}

    place = jnp.concatenate([core, chip.reshape(1)])

    def reduce_proj_and_small(proj_grads, small_grads):
        small_shapes.update({k: t.shape for k, t in small_grads.items()})
        bigs = [g.reshape(N_SHARDS, d // N_SHARDS, d) for g in proj_grads]
        *landed, small_landed = _exchange_halves(bigs, _pack_small(small_grads))
        sums = [_add_half(g, l, place, f"add_half_{1 + a}") for a, (g, l) in enumerate(zip(bigs, landed))]
        return _Scatter([s[0] for s in sums], [s[1] for s in sums], _sum_slots(small_landed, "sum_small"))

    def reduce_w_in(d_w_in, landed):
        partial, own_slot = _add_half(d_w_in, landed, place, "add_half_0")
        return _Scatter([partial], [own_slot])

    loss_part, grad_x, d_norm_g, ((*by_chip_proj, small_all), by_chip_w_in) = _local_step(
        x, loss_target, None, None, None, None, None,
        b_merge, conv_b, rg_wx[0], rg_bx.reshape(1, d), rg_wa[0], rg_ba.reshape(1, d), rg_lambda, hg_lb_logits,
        hg_norm_g, norm_g, final_norm_g.reshape(1, d), gather=(slotted, conv_slotted, chip.reshape(1)),
        reduction=(reduce_proj_and_small, reduce_w_in))
    mine = [_sum_slots(s, f"sum_chips_{a}") for a, s in enumerate(by_chip_w_in + by_chip_proj)]
    late = jnp.concatenate([d_norm_g.reshape(SUBLANES, LANES), jnp.full((SUBLANES, LANES), loss_part, F32)])
    *theirs, late_parts = _swap_halves(mine, late)
    late_sum = _sum_slots(late_parts, "sum_late")
    loss = late_sum[SUBLANES, 0]
    small_red = _unpack_small(small_all, {k: jax.ShapeDtypeStruct(s, F32) for k, s in small_shapes.items()})
    small_red["norm_g"] = late_sum[:SUBLANES].reshape(1, d)

    grads, delta, new_m, new_v = {}, {}, {}, {}
    for k, g_mine, g_theirs in zip(big_names, mine, theirs):
        out = _adamw_halves(weights[k][0], g_mine, g_theirs, m[k][0], v[k][0], core, f"adamw_{k}")
        grads[k], delta[k], new_m[k], new_v[k] = (t.reshape(weights[k].shape) for t in out)
    cols = d // N_SHARDS
    g_conv = lax.dynamic_slice(small_red["conv_w"], (0, chip * cols), (CONV_WIDTH, cols))
    grads["conv_w"] = g_conv.reshape(conv_w.shape)
    dl, nm, nv = _adamw(conv_w[0], g_conv, m_conv_w[0], v_conv_w[0], "adamw_conv_w")
    delta["conv_w"], new_m["conv_w"], new_v["conv_w"] = (t.reshape(conv_w.shape) for t in (dl, nm, nv))
    rest = [k for k in _SMALL_ORDER if k != "conv_w"]
    like = {k: (weights[k] if k != "conv_w" else jnp.zeros((CONV_WIDTH, d), F32)) for k in _SMALL_ORDER}
    packs = [_pack_small({k: (t[k] if k != "conv_w" else like[k]) for k in _SMALL_ORDER}) for t in (weights, m, v)]
    g_pack = _pack_small({k: small_red[k].reshape(like[k].shape) for k in _SMALL_ORDER})
    outs = [_unpack_small(p, like) for p in _adamw(packs[0], g_pack, packs[1], packs[2], "adamw_small")]
    for k in rest:
        grads[k] = small_red[k].reshape(weights[k].shape)
        delta[k], new_m[k], new_v[k] = outs[0][k], outs[1][k], outs[2][k]

    order = ("w_in", "b_merge", "conv_w", "conv_b", "rg_wx", "rg_bx", "rg_wa", "rg_ba", "rg_lambda", "hg_lb_logits",
             "hg_norm_g", "proj_a", "proj_b", "w_out", "norm_g", "final_norm_g")
    return (loss, grad_x, *[grads[k] for k in order], *[delta[k] for k in order], *[new_m[k] for k in order],
            *[new_v[k] for k in order])
```

```python
import functools

import jax
import jax.numpy as jnp
from jax import lax
from jax.experimental import pallas as pl
from jax.experimental.pallas import tpu as pltpu

F32 = jnp.float32
_MXU_DTYPE = jnp.bfloat16

D_MODEL = 1024
LANES = 128
SUBLANES = 8
N_BLK = D_MODEL // LANES
N_GROUPS = 8
N_SHARDS = 4
CONV_WIDTH = 4
LRU_C = 8.0
CHUNK = 64
CHUNKS_IN_FLIGHT = 16
HG_SCALE = float(LANES) ** -0.5
EPS = 1e-6
ADAM_LR, ADAM_B1, ADAM_B2, ADAM_EPS, ADAM_WD, ADAM_STEP = 0.001, 0.9, 0.999, 1e-08, 0.01, 10
VMEM_LIMIT = 56 * 1024 * 1024
VMEM_LIMIT_BIG = 60 * 1024 * 1024
MESH = pl.DeviceIdType.MESH

_SLOT_TO_GROUP = (2, 3, 4, 5, 0, 1, 6, 7)


def _group_of_slot(s):
    return jnp.where(s < 4, s + 2, jnp.where(s < 6, s - 4, s))


def _mm(a, b):
    return lax.dot_general(a.astype(_MXU_DTYPE), b.astype(_MXU_DTYPE), (((1,), (0,)), ((), ())),
                           preferred_element_type=F32)


def _mm_nt(a, b):
    return lax.dot_general(a.astype(_MXU_DTYPE), b.astype(_MXU_DTYPE), (((1,), (1,)), ((), ())),
                           preferred_element_type=F32)


def _mm_tn(a, b):
    return lax.dot_general(a.astype(_MXU_DTYPE), b.astype(_MXU_DTYPE), (((0,), (0,)), ((), ())),
                           preferred_element_type=F32)


def _sigmoid(x):
    return 0.5 * jnp.tanh(0.5 * x) + 0.5


def _log1p_pos(y):
    series = y * (1.0 - y * (0.5 - y * (1.0 / 3.0 - y * 0.25)))
    return jnp.where(y < 0.01, series, jnp.log(1.0 + y))


def _softplus(x):
    return jnp.maximum(x, 0.0) + _log1p_pos(jnp.exp(-jnp.abs(x)))


def _shift_down(x, n):
    rolled = pltpu.roll(x, n, 0)
    edge = SUBLANES if (n < SUBLANES and x.shape[0] > SUBLANES) else x.shape[0]
    rows = lax.broadcasted_iota(jnp.int32, (edge, x.shape[1]), 0)
    head = jnp.where(rows >= n, rolled[:edge], 0.0)
    return head if edge == x.shape[0] else jnp.concatenate([head, rolled[edge:]], axis=0)


def _shift_up(x, n):
    size = x.shape[0]
    rolled = pltpu.roll(x, size - n, 0)
    edge = SUBLANES if (n < SUBLANES and size > SUBLANES) else size
    rows = lax.broadcasted_iota(jnp.int32, (edge, x.shape[1]), 0)
    tail = jnp.where(rows < edge - n, rolled[size - edge:], 0.0)
    return tail if edge == size else jnp.concatenate([rolled[:size - edge], tail], axis=0)


def _params(dims, vmem=VMEM_LIMIT):
    return pltpu.CompilerParams(dimension_semantics=dims, vmem_limit_bytes=vmem)


def _load_w_in_by_slot(w_hbm, w_res):
    for slot, g in enumerate(_SLOT_TO_GROUP):
        pltpu.sync_copy(w_hbm.at[g // 2, :, pl.ds((g % 2) * D_MODEL, D_MODEL)], w_res.at[slot])


def _inproj_fwd(x2d, norm_g, w_all):
    tokens, d = x2d.shape
    tm = min(512, tokens)

    def body(x_ref, g_ref, w_hbm, z_ref, ht_ref, h_scr, w_res):
        @pl.when((pl.program_id(0) == 0) & (pl.program_id(1) == 0))
        def _():
            _load_w_in_by_slot(w_hbm, w_res)

        @pl.when(pl.program_id(1) == 0)
        def _():
            x = x_ref[...]
            r = lax.rsqrt(jnp.mean(x * x, axis=-1, keepdims=True) + EPS)
            h = (x * r) * g_ref[...]
            h_scr[...] = h.astype(_MXU_DTYPE)
            ht_ref[...] = jnp.transpose(h).astype(_MXU_DTYPE)

        z_ref[...] = _mm(h_scr[...], w_res[pl.program_id(1)])

    return pl.pallas_call(
        body, name="inproj_fwd",
        grid=(tokens // tm, N_GROUPS),
        in_specs=[pl.BlockSpec((tm, d), lambda i, s: (i, 0)),
                  pl.BlockSpec((1, d), lambda i, s: (0, 0)),
                  pl.BlockSpec(memory_space=pl.ANY)],
        out_specs=[pl.BlockSpec((None, tm, D_MODEL), lambda i, s: (s, i, 0)),
                   pl.BlockSpec((d, tm), lambda i, s: (0, i))],
        out_shape=[jax.ShapeDtypeStruct((N_GROUPS, tokens, D_MODEL), F32),
                   jax.ShapeDtypeStruct((d, tokens), _MXU_DTYPE)],
        scratch_shapes=[pltpu.VMEM((tm, d), _MXU_DTYPE), pltpu.VMEM((N_GROUPS, d, D_MODEL), _MXU_DTYPE)],
        compiler_params=_params(("arbitrary", "arbitrary")),
    )(x2d, norm_g, w_all)


def _slot_of_group(g):
    return jnp.where(g < 2, g + 4, jnp.where(g < 6, g - 2, g))


def _inproj_fwd_gather(x2d, norm_g, slotted, conv_slotted, chip):
    tokens, d = x2d.shape
    tm = min(512, tokens)
    n_tiles = tokens // tm
    n_big = len(slotted)
    n_sem = 6 * (n_big + 1) + 3
    last_pass = N_GROUPS - 1

    def shard_of(k, chip_id):
        x, y = chip_id // 2, chip_id % 2
        return 2 * jnp.where(k % 2 == 1, 1 - x, x) + jnp.where(k // 2 == 1, 1 - y, y)

    def body(chip_ref, x_ref, g_ref, *rest):
        bufs, cw = rest[n_big + 1:2 * n_big + 1], rest[2 * n_big + 1]
        z_ref, ht_ref = rest[2 * n_big + 2:2 * n_big + 4]
        h_all, slab, send_sems, recv_sems, slab_sems = rest[2 * n_big + 4:]
        del chip_ref
        p, i = pl.program_id(0), pl.program_id(1)
        x, y, c, chips = _mesh_position()
        me, sibling = 2 * x + y, (x, y, 1 - c)

        pieces = [(0, 0), (0, 1)] + [(a, None) for a in range(1, n_big)]

        def half(piece, slot, which):
            a, q = pieces[piece]
            hs = bufs[a].shape[1] // 2
            cols = slice(None) if q is None else pl.ds(q * D_MODEL, D_MODEL)
            return bufs[a].at[slot, pl.ds(which * hs, hs), cols]

        def send(piece, j):
            mine = half(piece, me, c)
            return _remote(mine, mine, send_sems, recv_sems, 6 * piece + j, (chips[j][0], chips[j][1], c))

        def arrival(piece, j):
            landed = half(piece, 2 * chips[j][0] + chips[j][1], c)
            return _remote(landed, landed, send_sems, recv_sems, 6 * piece + j, (chips[j][0], chips[j][1], c))

        def passed_on(piece, j, which):
            landed = half(piece, 2 * chips[j][0] + chips[j][1], which)
            return _remote(landed, landed, send_sems, recv_sems, 6 * piece + 3 + j, sibling)

        def conv_copy(j, slot):
            return _remote(cw.at[slot], cw.at[slot], send_sems, recv_sems, 6 * len(pieces) + j,
                           (chips[j][0], chips[j][1], c))

        def land(piece, j):
            arrival(piece, j).wait_recv()
            passed_on(piece, j, c).start()
            passed_on(piece, j, 1 - c).wait_recv()

        def slab_copy(pv):
            src = bufs[0].at[shard_of(pv // 2, me), :, pl.ds((pv % 2) * D_MODEL, D_MODEL)]
            return pltpu.make_async_copy(src, slab.at[pv % 2], slab_sems.at[pv % 2])

        @pl.when((p == 0) & (i == 0))
        def _():
            for q in range(2):
                send(q, 0).start()
                send(q, 1).start()
            slab_copy(0).start()

        for pv in range(N_GROUPS):
            @pl.when((p == pv) & (i == 0))
            def _(pv=pv):
                slab_copy(pv).wait()

        rows = pl.ds(pl.multiple_of(i * tm, tm), tm)

        @pl.when(p == 0)
        def _():
            xt = x_ref[...]
            r = lax.rsqrt(jnp.mean(xt * xt, axis=-1, keepdims=True) + EPS)
            h = (xt * r) * g_ref[...]
            h_all[rows, :] = h.astype(_MXU_DTYPE)
            ht_ref[...] = jnp.transpose(h).astype(_MXU_DTYPE)

        z_ref[...] = _mm(h_all[rows, :], slab[p % 2])

        for pv in range(1, N_GROUPS):
            @pl.when((p == pv - 1) & (i == n_tiles - 1))
            def _(pv=pv):
                j, q = pv // 2 - 1, pv % 2
                if j >= 0:
                    land(q, j)
                if (j, q) == (0, 0):
                    send(0, 2).start()
                    send(1, 2).start()
                if (j, q) == (1, 0):
                    for piece in range(2, len(pieces)):
                        for jj in range(3):
                            send(piece, jj).start()
                    for jj in range(3):
                        conv_copy(jj, me).start()
                slab_copy(pv).start()

        @pl.when((p == last_pass) & (i == n_tiles - 1))
        def _():
            for piece in range(2, len(pieces)):
                for j in range(3):
                    land(piece, j)
            for j in range(3):
                conv_copy(j, 2 * chips[j][0] + chips[j][1]).wait_recv()
            for piece in range(len(pieces)):
                for j in range(3):
                    send(piece, j).wait_send()
                    passed_on(piece, j, c).wait_send()
            for j in range(3):
                conv_copy(j, me).wait_send()

    def z_index(p, i, chip_ref):
        g = 2 * shard_of(p // 2, chip_ref[0]) + p % 2
        return (_slot_of_group(g), i, 0)

    def first_pass_tile(p, i, chip_ref):
        return jnp.where(p == 0, i, n_tiles - 1)

    hbm = pl.BlockSpec(memory_space=pl.ANY)
    operands = list(slotted) + [conv_slotted]
    grid_spec = pltpu.PrefetchScalarGridSpec(
        num_scalar_prefetch=1, grid=(N_GROUPS, n_tiles),
        in_specs=[pl.BlockSpec((tm, d), lambda p, i, chip_ref: (first_pass_tile(p, i, chip_ref), 0)),
                  pl.BlockSpec((1, d), lambda p, i, chip_ref: (0, 0))] + [hbm] * (n_big + 1),
        out_specs=[hbm] * (n_big + 1) + [pl.BlockSpec((None, tm, D_MODEL), z_index),
                                         pl.BlockSpec((d, tm), lambda p, i, chip_ref: (0, first_pass_tile(p, i, chip_ref)))],
        scratch_shapes=[pltpu.VMEM((tokens, d), _MXU_DTYPE), pltpu.VMEM((2, d, D_MODEL), _MXU_DTYPE),
                        pltpu.SemaphoreType.DMA((n_sem,)), pltpu.SemaphoreType.DMA((n_sem,)),
                        pltpu.SemaphoreType.DMA((2,))])
    out = pl.pallas_call(
        body, name="inproj_fwd_gather", grid_spec=grid_spec,
        out_shape=[jax.ShapeDtypeStruct(a.shape, a.dtype) for a in operands]
        + [jax.ShapeDtypeStruct((N_GROUPS, tokens, D_MODEL), F32), jax.ShapeDtypeStruct((d, tokens), _MXU_DTYPE)],
        input_output_aliases={3 + a: a for a in range(n_big + 1)},
        compiler_params=_params(("arbitrary", "arbitrary")),
    )(chip, x2d, norm_g, *operands)
    return out[n_big + 1], out[n_big + 2], out[:n_big], out[n_big]


def _lane_blocks(x):
    return [x[:, k * LANES:(k + 1) * LANES] for k in range(x.shape[1] // LANES)]


def _block_diag(x, w_ref, transposed=False):
    mm = _mm_nt if transposed else _mm
    return jnp.concatenate([mm(xk, w_ref[k]) for k, xk in enumerate(_lane_blocks(x))], axis=1)


def _lru_gates(xa, cw_ref, cb_ref, wx_ref, bx_ref, wa_ref, ba_ref, lam_ref):
    xc = (cb_ref[...] + cw_ref[3:4, :] * xa + cw_ref[2:3, :] * _shift_down(xa, 1)
          + cw_ref[1:2, :] * _shift_down(xa, 2) + cw_ref[0:1, :] * _shift_down(xa, 3))
    gi = _sigmoid(_block_diag(xc, wx_ref) + bx_ref[...])
    gr = _sigmoid(_block_diag(xc, wa_ref) + ba_ref[...])
    sp = _softplus(-lam_ref[...])
    a, mult, inv_mult = _lru_decay(gr, sp)
    return xc, gi, gr, sp, a, mult, inv_mult


def _lru_decay(gr, sp):
    log_a = (-LRU_C) * gr * sp
    a = jnp.exp(log_a)
    y = 2.0 * log_a
    mult_sq = jnp.where(y > -1e-3, -y * (1.0 + 0.5 * y), 1.0 - a * a)
    inv_mult = lax.rsqrt(jnp.maximum(mult_sq, 1e-37))
    return a, mult_sq * inv_mult, inv_mult


def _tile_rows(width):
    return lax.broadcasted_iota(jnp.int32, (SUBLANES, width), 0)


def _scan_forward(a_scr, u_scr, h_scr, seq):
    width = a_scr.shape[1]
    rows = _tile_rows(width)

    def tile(j, carry):
        sl = pl.ds(pl.multiple_of(j * SUBLANES, SUBLANES), SUBLANES)
        a = a_scr[sl, :]
        u = u_scr[sl, :]
        for d in (1, 2, 4):
            keep = rows >= d
            a_sh = jnp.where(keep, pltpu.roll(a, d, 0), 1.0)
            u_sh = jnp.where(keep, pltpu.roll(u, d, 0), 0.0)
            u = a * u_sh + u
            a = a * a_sh
        h = u + a * carry
        h_scr[sl, :] = h
        return jnp.broadcast_to(h[SUBLANES - 1:SUBLANES, :], (SUBLANES, width))

    lax.fori_loop(0, seq // SUBLANES, tile, jnp.zeros((SUBLANES, width), F32))


def _scan_backward(c_scr, d_scr, g_scr, seq):
    width = c_scr.shape[1]
    rows = _tile_rows(width)
    n_tiles = seq // SUBLANES

    def tile(jj, carry):
        j = n_tiles - 1 - jj
        sl = pl.ds(pl.multiple_of(j * SUBLANES, SUBLANES), SUBLANES)
        c = c_scr[sl, :]
        g = d_scr[sl, :]
        for d in (1, 2, 4):
            keep = rows < SUBLANES - d
            c_sh = jnp.where(keep, pltpu.roll(c, SUBLANES - d, 0), 1.0)
            g_sh = jnp.where(keep, pltpu.roll(g, SUBLANES - d, 0), 0.0)
            g = c * g_sh + g
            c = c * c_sh
        g = g + c * carry
        g_scr[sl, :] = g
        return jnp.broadcast_to(g[0:1, :], (SUBLANES, width))

    lax.fori_loop(0, n_tiles, tile, jnp.zeros((SUBLANES, width), F32))


LRU_BLOCKS_PER_STEP = 2
LRU_LANES = LRU_BLOCKS_PER_STEP * LANES
LRU_STEPS = N_BLK // LRU_BLOCKS_PER_STEP


def _lru_param_specs(cb_axis):
    def pick(*ids):
        return ids[cb_axis]

    vec = pl.BlockSpec((1, LRU_LANES), lambda *ids: (0, pick(*ids)))
    mat = pl.BlockSpec((LRU_BLOCKS_PER_STEP, LANES, LANES), lambda *ids: (pick(*ids), 0, 0))
    return [pl.BlockSpec((CONV_WIDTH, LRU_LANES), lambda *ids: (0, pick(*ids))), vec, mat, vec, mat, vec, vec]


def _branch_a_fwd(z, conv_w, conv_b, wx, bx, wa, ba, lam, batch, seq):
    tokens = batch * seq

    def body(z_ref, cw_ref, cb_ref, wx_ref, bx_ref, wa_ref, ba_ref, lam_ref, ya_ref, hl_ref, kept_ref, a_scr, u_scr):
        xa = z_ref[0]
        ga = z_ref[1]
        xc, gi, gr, _, a, mult, _ = _lru_gates(xa, cw_ref, cb_ref, wx_ref, bx_ref, wa_ref, ba_ref, lam_ref)
        kept_ref[0], kept_ref[1], kept_ref[2] = xc, gi, gr
        a_scr[...] = a
        u_scr[...] = mult * gi * xc
        _scan_forward(a_scr, u_scr, hl_ref, seq)
        ya_ref[...] = (hl_ref[...] * (ga * _sigmoid(ga))).astype(_MXU_DTYPE)

    blk = pl.BlockSpec((seq, LRU_LANES), lambda b, c: (b, c))
    return pl.pallas_call(
        body, name="branch_a_fwd",
        grid=(batch, LRU_STEPS),
        in_specs=[pl.BlockSpec((2, seq, LRU_LANES), lambda b, c: (2, b, c))] + _lru_param_specs(1),
        out_specs=[blk, blk, pl.BlockSpec((3, seq, LRU_LANES), lambda b, c: (0, b, c))],
        out_shape=[jax.ShapeDtypeStruct((tokens, D_MODEL), _MXU_DTYPE), jax.ShapeDtypeStruct((tokens, D_MODEL), F32),
                   jax.ShapeDtypeStruct((3, tokens, D_MODEL), F32)],
        scratch_shapes=[pltpu.VMEM((seq, LRU_LANES), F32), pltpu.VMEM((seq, LRU_LANES), F32)],
        compiler_params=_params(("parallel", "parallel")),
    )(z, conv_w, conv_b, wx, bx, wa, ba, lam)


def _branch_a_bwd(z, hl, kept, dya, dz, conv_w, conv_b, wx, bx, wa, ba, lam, batch, seq):
    def body(z_ref, hl_ref, kept_ref, dya_ref, dz_in_ref, cw_ref, cb_ref, wx_ref, bx_ref, wa_ref, ba_ref, lam_ref,
             dz_ref, dcw_ref, dcb_ref, dwx_ref, dbx_ref, dwa_ref, dba_ref, dlam_ref, c_scr, d_scr):
        del dz_in_ref, cb_ref, bx_ref, ba_ref
        g_scr = d_scr
        xa = z_ref[0]
        ga = z_ref[1]
        hl = hl_ref[...]
        dya = dya_ref[...]
        xc, gi, gr = kept_ref[0], kept_ref[1], kept_ref[2]
        sp = _softplus(-lam_ref[...])
        a, mult, inv_mult = _lru_decay(gr, sp)
        sga = _sigmoid(ga)
        dz_ref[1] = (dya * hl * (sga * (1.0 + ga * (1.0 - sga)))).astype(_MXU_DTYPE)
        c_scr[...] = _shift_up(a, 1)
        d_scr[...] = dya * (ga * sga)
        _scan_backward(c_scr, d_scr, g_scr, seq)
        g = g_scr[...]
        da = g * _shift_down(hl, 1)
        dmult = g * gi * xc
        dgi = g * mult * xc
        dxc = g * mult * gi
        dlog_a = da * a - dmult * (a * a) * inv_mult
        dgr = dlog_a * (-LRU_C) * sp
        dsp = jnp.sum(dlog_a * gr, axis=0, keepdims=True) * (-LRU_C)
        dlam = -dsp * _sigmoid(-lam_ref[...])
        dpi = dgi * gi * (1.0 - gi)
        dpr = dgr * gr * (1.0 - gr)
        dxc = dxc + _block_diag(dpi, wx_ref, transposed=True) + _block_diag(dpr, wa_ref, transposed=True)
        dwx = jnp.stack([_mm_tn(xk, dk) for xk, dk in zip(_lane_blocks(xc), _lane_blocks(dpi))])
        dwa = jnp.stack([_mm_tn(xk, dk) for xk, dk in zip(_lane_blocks(xc), _lane_blocks(dpr))])
        dbx = jnp.sum(dpi, axis=0, keepdims=True)
        dba = jnp.sum(dpr, axis=0, keepdims=True)
        ahead = [dxc if k == CONV_WIDTH - 1 else _shift_up(dxc, CONV_WIDTH - 1 - k) for k in range(CONV_WIDTH)]
        dxa = sum(cw_ref[k:k + 1, :] * ahead[k] for k in range(CONV_WIDTH))
        dz_ref[0] = dxa.astype(_MXU_DTYPE)
        dcb = jnp.sum(dxc, axis=0, keepdims=True)
        dcw = [jnp.sum(ahead[k] * xa, axis=0, keepdims=True) for k in range(CONV_WIDTH)]

        @pl.when(pl.program_id(1) == 0)
        def _():
            for k in range(CONV_WIDTH):
                dcw_ref[k:k + 1, :] = dcw[k]
            dcb_ref[...] = dcb
            dwx_ref[...] = dwx
            dbx_ref[...] = dbx
            dwa_ref[...] = dwa
            dba_ref[...] = dba
            dlam_ref[...] = dlam

        @pl.when(pl.program_id(1) != 0)
        def _():
            for k in range(CONV_WIDTH):
                dcw_ref[k:k + 1, :] += dcw[k]
            dcb_ref[...] += dcb
            dwx_ref[...] += dwx
            dbx_ref[...] += dbx
            dwa_ref[...] += dwa
            dba_ref[...] += dba
            dlam_ref[...] += dlam

    tokens = batch * seq
    blk = pl.BlockSpec((seq, LRU_LANES), lambda c, b: (b, c))
    vec = pl.BlockSpec((1, LRU_LANES), lambda c, b: (0, c))
    mat = pl.BlockSpec((LRU_BLOCKS_PER_STEP, LANES, LANES), lambda c, b: (c, 0, 0))
    vec_shape = jax.ShapeDtypeStruct((1, D_MODEL), F32)
    mat_shape = jax.ShapeDtypeStruct((N_BLK, LANES, LANES), F32)
    return pl.pallas_call(
        body, name="branch_a_bwd",
        grid=(LRU_STEPS, batch),
        in_specs=[pl.BlockSpec((2, seq, LRU_LANES), lambda c, b: (2, b, c)), blk,
                  pl.BlockSpec((3, seq, LRU_LANES), lambda c, b: (0, b, c)), blk,
                  pl.BlockSpec(memory_space=pl.ANY)] + _lru_param_specs(0),
        out_specs=[pl.BlockSpec((2, seq, LRU_LANES), lambda c, b: (2, b, c)),
                   pl.BlockSpec((CONV_WIDTH, LRU_LANES), lambda c, b: (0, c)), vec, mat, vec, mat, vec, vec],
        out_shape=[jax.ShapeDtypeStruct((N_GROUPS, tokens, D_MODEL), _MXU_DTYPE),
                   jax.ShapeDtypeStruct((CONV_WIDTH, D_MODEL), F32), vec_shape, mat_shape, vec_shape, mat_shape,
                   vec_shape, vec_shape],
        scratch_shapes=[pltpu.VMEM((seq, LRU_LANES), F32)] * 2,
        input_output_aliases={4: 0},
        compiler_params=_params(("parallel", "arbitrary"), vmem=VMEM_LIMIT_BIG),
    )(z, hl, kept, dya, dz, conv_w, conv_b, wx, bx, wa, ba, lam)


def _chunk_masks(transposed=False):
    r = lax.broadcasted_iota(jnp.int32, (CHUNK, CHUNK), 0)
    c = lax.broadcasted_iota(jnp.int32, (CHUNK, CHUNK), 1)
    return r <= c if transposed else r >= c


def _row_blocks(seq, fn):
    block = min(256, seq)

    def trip(i, carry):
        fn(pl.ds(pl.multiple_of(i * block, block), block))
        return carry

    lax.fori_loop(0, seq // block, trip, 0)


def _hgrn_prepare(z_ref, lb_ref, f_scr, logf_scr, qh_scr, seq):
    lb = _sigmoid(lb_ref[0:1, :] - lb_ref[1:2, :])

    def block(rows):
        q = z_ref[0, rows, :]
        f = lb + (1.0 - lb) * _sigmoid(z_ref[1, rows, :])
        f_scr[rows, :] = f
        logf_scr[rows, :] = jnp.log(f)
        qh_scr[rows, :] = q * _sigmoid(q)

    _row_blocks(seq, block)
    return lb


def _cumsum_rows(x, reverse=False):
    shift = _shift_up if reverse else _shift_down
    d = 1
    while d < x.shape[0]:
        x = x + shift(x, d)
        d *= 2
    return x


def _lane_mean(x):
    return jnp.mean(x, axis=-1, keepdims=True)


def _token_contractions(lhs_scr, rhs_scr, out_ref, seq):
    rows_id = lax.broadcasted_iota(jnp.int32, (LANES, LANES), 0)

    def transposed(p):
        rows = pl.ds(pl.multiple_of(p * LANES, LANES), LANES)
        return jnp.transpose(lhs_scr[rows, :]).astype(_MXU_DTYPE), rhs_scr[rows, :]

    def contract(p, s):
        lhs_t, rhs = s
        return (_mm(lhs_t, jnp.where(rows_id < CHUNK, rhs, 0.0)), _mm(lhs_t, jnp.where(rows_id >= CHUNK, rhs, 0.0)))

    def store(p, out):
        out_ref[2 * p] = out[0]
        out_ref[2 * p + 1] = out[1]

    _independent_trips(seq // LANES, [transposed, contract], store)


def _chunk_rows(c):
    return pl.ds(pl.multiple_of(c * CHUNK, CHUNK), CHUNK)


def _chunk_terms(c, z_ref, f_scr, qh_scr, b_scr):
    rows = _chunk_rows(c)
    b = b_scr[rows, :]
    b_mid = b_scr[pl.ds(c * CHUNK + CHUNK // 2, 1), :]
    b_last = b_scr[pl.ds(c * CHUNK + CHUNK - 1, 1), :]
    qh = qh_scr[rows, :]
    k = 1.0 - f_scr[rows, :]
    v = z_ref[2, rows, :]
    e_q = jnp.exp(b - b_mid) * HG_SCALE
    e_k = jnp.exp(b_mid - b)
    e_qi = jnp.exp(b) * HG_SCALE
    e_ks = jnp.exp(b_last - b)
    decay = jnp.exp(b_last)
    return rows, qh, k, v, e_q, e_k, e_qi, e_ks, decay


def _independent_trips(n, stages, store, group=CHUNKS_IN_FLIGHT):
    stages = stages if isinstance(stages, (list, tuple)) else [stages]
    group = min(group, n)

    def trip(g, carry):
        ids = [g * group + i for i in range(group)]
        state = [stages[0](c) for c in ids]
        for stage in stages[1:]:
            state = [stage(c, s) for c, s in zip(ids, state)]
        for c, s in zip(ids, state):
            store(c, s)
        return carry

    lax.fori_loop(0, n // group, trip, 0)


def _branch_b_fwd(z, lb_logits, hg_g, batch, seq):
    tokens = batch * seq
    n_chunks = seq // CHUNK

    def body(z_ref, lb_ref, g_ref, yb_ref, st_ref, f_scr, logf_scr, qh_scr, b_scr, o_scr, qi_scr, ks_scr, dec_scr):
        _hgrn_prepare(z_ref, lb_ref, f_scr, logf_scr, qh_scr, seq)
        causal = _chunk_masks()
        gain = g_ref[...]

        def cumulate(c):
            return _cumsum_rows(logf_scr[_chunk_rows(c), :])

        def store_cumulated(c, b):
            b_scr[_chunk_rows(c), :] = b

        def scores(c):
            _, qh, k, v, e_q, e_k, e_qi, e_ks, decay = _chunk_terms(c, z_ref, f_scr, qh_scr, b_scr)
            return _mm_nt(qh * e_q, k * e_k), v, qh * e_qi, k * e_ks, decay

        def within_chunk(c, s):
            att, v, q_int, k_st, decay = s
            return _mm(jnp.where(causal, att, 0.0), v), q_int, k_st, decay

        def store_within_chunk(c, out):
            rows = _chunk_rows(c)
            o_scr[rows, :], qi_scr[rows, :], ks_scr[rows, :], dec_scr[pl.ds(c, 1), :] = out

        def carry_state(c, state_t):
            update = st_ref[c]
            st_ref[c] = state_t
            return state_t * dec_scr[pl.ds(c, 1), :] + update

        def finish(c):
            rows = _chunk_rows(c)
            o = o_scr[rows, :] + _mm_nt(qi_scr[rows, :], st_ref[c])
            r = lax.rsqrt(_lane_mean(o * o) + EPS)
            gb = z_ref[3, rows, :]
            return (((o * r) * gain) * (gb * _sigmoid(gb))).astype(_MXU_DTYPE)

        def store_finished(c, yb):
            yb_ref[_chunk_rows(c), :] = yb

        _independent_trips(n_chunks, cumulate, store_cumulated)
        _independent_trips(n_chunks, [scores, within_chunk], store_within_chunk)
        _token_contractions(z_ref.at[2], ks_scr, st_ref, seq)
        lax.fori_loop(0, n_chunks, carry_state, jnp.zeros((LANES, LANES), F32))
        _independent_trips(n_chunks, finish, store_finished)

    seq_buf = pltpu.VMEM((seq, LANES), F32)
    return pl.pallas_call(
        body, name="branch_b_fwd",
        grid=(batch, N_BLK),
        in_specs=[pl.BlockSpec((4, seq, LANES), lambda b, h: (0, b, h)),
                  pl.BlockSpec((2, LANES), lambda b, h: (0, h)),
                  pl.BlockSpec((1, LANES), lambda b, h: (0, 0))],
        out_specs=[pl.BlockSpec((seq, LANES), lambda b, h: (b, h)),
                   pl.BlockSpec((None, n_chunks, LANES, LANES), lambda b, h: (b * N_BLK + h, 0, 0, 0))],
        out_shape=[jax.ShapeDtypeStruct((tokens, D_MODEL), _MXU_DTYPE),
                   jax.ShapeDtypeStruct((batch * N_BLK, n_chunks, LANES, LANES), F32)],
        scratch_shapes=[seq_buf] * 7 + [pltpu.VMEM((n_chunks, LANES), F32)],
        compiler_params=_params(("parallel", "parallel")),
    )(z, lb_logits, hg_g)


def _branch_b_bwd(z, states, dyb, dz, lb_logits, hg_g, batch, seq):
    n_chunks = seq // CHUNK

    def body(z_ref, st_ref, dyb_ref, dz_in_ref, lb_ref, g_ref, dz_ref, dlog_ref, dg_ref,
             f_scr, logf_scr, qh_scr, b_scr, do_scr, qi_scr, dqh_scr, df_scr, dec_scr, dgp_scr, dlb_scr, dst_scr):
        del dz_in_ref
        first = (pl.program_id(0) == 0) & (pl.program_id(1) == 0)
        lb = _hgrn_prepare(z_ref, lb_ref, f_scr, logf_scr, qh_scr, seq)
        causal = _chunk_masks()
        anti_causal = _chunk_masks(transposed=True)
        gain = g_ref[...]

        @pl.when(first)
        def _():
            dg_ref[...] = jnp.zeros_like(dg_ref)

        @pl.when(pl.program_id(1) == 0)
        def _():
            dlb_scr[...] = jnp.zeros_like(dlb_scr)

        def cumulate(c):
            return _cumsum_rows(logf_scr[_chunk_rows(c), :])

        def store_cumulated(c, b):
            b_scr[_chunk_rows(c), :] = b

        def scores(c):
            _, qh, k, v, e_q, e_k, e_qi, e_ks, decay = _chunk_terms(c, z_ref, f_scr, qh_scr, b_scr)
            q_int = qh * e_qi
            return _mm_nt(qh * e_q, k * e_k), _mm_nt(q_int, st_ref[c]), v, q_int, decay

        def output_gradient(c, s):
            att, o_inter, v, q_int, decay = s
            rows = _chunk_rows(c)
            o = _mm(jnp.where(causal, att, 0.0), v) + o_inter
            r = lax.rsqrt(_lane_mean(o * o) + EPS)
            o_n = o * r
            gb = z_ref[3, rows, :]
            sgb = _sigmoid(gb)
            dyb_c = dyb_ref[rows, :]
            d_ong = dyb_c * (gb * sgb)
            d_gb = (dyb_c * (o_n * gain) * (sgb * (1.0 + gb * (1.0 - sgb)))).astype(_MXU_DTYPE)
            d_gain = jnp.sum(d_ong * o_n, axis=0, keepdims=True)
            d_on = d_ong * gain
            return d_gb, d_gain, r * (d_on - o_n * _lane_mean(d_on * o_n)), q_int, decay

        def store_output_gradient(c, out):
            rows = _chunk_rows(c)
            dz_ref[3, rows, :], dgp_scr[pl.ds(c, 1), :], do_scr[rows, :], qi_scr[rows, :], dec_scr[pl.ds(c, 1), :] = out

        def carry_state_gradient(cc, d_state_t):
            c = n_chunks - 1 - cc
            update = dst_scr[c]
            dst_scr[c] = d_state_t
            return d_state_t * dec_scr[pl.ds(c, 1), :] + update

        def score_gradients(c):
            rows, qh, k, v, e_q, e_k, e_qi, e_ks, decay = _chunk_terms(c, z_ref, f_scr, qh_scr, b_scr)
            state_t = st_ref[c]
            d_state_t = dst_scr[c]
            d_o = do_scr[rows, :]
            q_in, k_in, q_int, k_st = qh * e_q, k * e_k, qh * e_qi, k * e_ks
            first = (_mm_nt(k_in, q_in), _mm_nt(d_o, v), _mm_nt(v, d_o), _mm_nt(k_st, d_state_t), _mm(d_o, state_t),
                     _mm(v, d_state_t))
            d_decay = jnp.sum(state_t * d_state_t, axis=0, keepdims=True)
            return first, d_o, q_in, k_in, q_int, k_st, e_q, e_k, e_qi, e_ks, decay, d_decay

        def input_gradients(c, s):
            (att_t, d_att, d_att_t, dv_inter, dq_int, dk_st), d_o, q_in, k_in, q_int, k_st, e_q, e_k, e_qi, e_ks, decay, d_decay = s
            rows = _chunk_rows(c)
            d_v = _mm(jnp.where(anti_causal, att_t, 0.0), d_o) + dv_inter
            dq_in = _mm(jnp.where(causal, d_att, 0.0), k_in)
            dk_in = _mm(jnp.where(anti_causal, d_att_t, 0.0), q_in)
            d_k = dk_in * e_k + dk_st * e_ks
            kk = dk_st * k_st
            d_b = dq_in * q_in + dq_int * q_int - dk_in * k_in - kk
            d_b_last = jnp.sum(kk, axis=0, keepdims=True) + decay * d_decay
            d_logf = _cumsum_rows(d_b, reverse=True) + d_b_last
            return d_v.astype(_MXU_DTYPE), dq_in * e_q + dq_int * e_qi, d_logf / f_scr[rows, :] - d_k

        def store_input_gradients(c, out):
            rows = _chunk_rows(c)
            dz_ref[2, rows, :], dqh_scr[rows, :], df_scr[rows, :] = out

        def input_activations(rows):
            q = z_ref[0, rows, :]
            sq = _sigmoid(q)
            dz_ref[0, rows, :] = (dqh_scr[rows, :] * (sq * (1.0 + q * (1.0 - sq)))).astype(_MXU_DTYPE)
            sg = _sigmoid(z_ref[1, rows, :])
            d_f = df_scr[rows, :]
            dz_ref[1, rows, :] = (d_f * (1.0 - lb) * sg * (1.0 - sg)).astype(_MXU_DTYPE)
            dlb_scr[...] += jnp.sum(d_f * (1.0 - sg), axis=0, keepdims=True)

        _independent_trips(n_chunks, cumulate, store_cumulated)
        _independent_trips(n_chunks, [scores, output_gradient], store_output_gradient)
        _token_contractions(do_scr, qi_scr, dst_scr, seq)
        lax.fori_loop(0, n_chunks, carry_state_gradient, jnp.zeros((LANES, LANES), F32))
        _independent_trips(n_chunks, [score_gradients, input_gradients], store_input_gradients)
        dg_ref[...] += jnp.sum(dgp_scr[...], axis=0, keepdims=True)
        _row_blocks(seq, input_activations)
        d_l0 = dlb_scr[...] * lb * (1.0 - lb)
        dlog_ref[0:1, :] = d_l0
        dlog_ref[1:2, :] = -d_l0

    tokens = batch * seq
    seq_buf = pltpu.VMEM((seq, LANES), F32)
    chunk_rows = pltpu.VMEM((n_chunks, LANES), F32)
    return pl.pallas_call(
        body, name="branch_b_bwd",
        grid=(N_BLK, batch),
        in_specs=[pl.BlockSpec((4, seq, LANES), lambda h, b: (0, b, h)),
                  pl.BlockSpec((None, n_chunks, LANES, LANES), lambda h, b: (b * N_BLK + h, 0, 0, 0)),
                  pl.BlockSpec((seq, LANES), lambda h, b: (b, h)),
                  pl.BlockSpec(memory_space=pl.ANY),
                  pl.BlockSpec((2, LANES), lambda h, b: (0, h)),
                  pl.BlockSpec((1, LANES), lambda h, b: (0, 0))],
        out_specs=[pl.BlockSpec((4, seq, LANES), lambda h, b: (0, b, h)),
                   pl.BlockSpec((2, LANES), lambda h, b: (0, h)),
                   pl.BlockSpec((1, LANES), lambda h, b: (0, 0))],
        out_shape=[jax.ShapeDtypeStruct((N_GROUPS, tokens, D_MODEL), _MXU_DTYPE),
                   jax.ShapeDtypeStruct((2, D_MODEL), F32),
                   jax.ShapeDtypeStruct((1, LANES), F32)],
        scratch_shapes=[seq_buf] * 8 + [chunk_rows, chunk_rows, pltpu.VMEM((1, LANES), F32),
                                        pltpu.VMEM((n_chunks, LANES, LANES), F32)],
        input_output_aliases={3: 0},
        compiler_params=_params(("arbitrary", "arbitrary")),
    )(z, states, dyb, dz, lb_logits, hg_g)


def _merge_tail(ya, yb, z, x2d, tgt2d, b_merge, final_g, pa, pb, wo):
    tokens, d = x2d.shape
    tm = min(256, tokens)
    n_tiles = tokens // tm

    def body(ya_ref, yb_ref, z_ref, x_ref, t_ref, bm_ref, fg_ref, pa_hbm, pb_hbm, wo_hbm,
             dya_ref, dyb_ref, dx2_ref, dz_ref, loss_ref, dfg_ref, dbm_ref, dpa_hbm, dpb_hbm, dwo_hbm,
             pa_s, pb_s, wo_s, dpa_s, dpb_s, dwo_s):
        i = pl.program_id(0)

        @pl.when(i == 0)
        def _():
            pltpu.sync_copy(pa_hbm, pa_s)
            pltpu.sync_copy(pb_hbm, pb_s)
            pltpu.sync_copy(wo_hbm, wo_s)
            dpa_s[...] = jnp.zeros_like(dpa_s)
            dpb_s[...] = jnp.zeros_like(dpb_s)
            dwo_s[...] = jnp.zeros_like(dwo_s)
            loss_ref[...] = jnp.zeros_like(loss_ref)
            dfg_ref[...] = jnp.zeros_like(dfg_ref)
            dbm_ref[...] = jnp.zeros_like(dbm_ref)

        ya_t = ya_ref[...]
        yb_t = yb_ref[...]
        out_a = _mm(ya_t, pa_s[...])
        out_b = _mm(yb_t, pb_s[...])
        g_a = _sigmoid(z_ref[0] + bm_ref[:, :d])
        g_b = _sigmoid(z_ref[1] + bm_ref[:, d:])
        mixed = g_a * out_a + g_b * out_b
        x2 = x_ref[...] + _mm(mixed, wo_s[...])
        r = lax.rsqrt(jnp.mean(x2 * x2, axis=-1, keepdims=True) + EPS)
        xn = x2 * r
        fg = fg_ref[...]
        diff = xn * fg - t_ref[...]
        loss_ref[...] += jnp.sum(diff * diff) * (0.5 / d)
        dy = diff * (1.0 / d)
        dfg_ref[...] += jnp.sum(dy * xn, axis=0, keepdims=True)
        dxn = dy * fg
        dx2 = r * (dxn - xn * jnp.mean(dxn * xn, axis=-1, keepdims=True))
        dx2_ref[...] = dx2
        dmixed = _mm_nt(dx2, wo_s[...])
        dwo_s[...] += _mm_tn(mixed, dx2)
        dgm_a = dmixed * out_a * g_a * (1.0 - g_a)
        dgm_b = dmixed * out_b * g_b * (1.0 - g_b)
        dz_ref[0] = dgm_a.astype(_MXU_DTYPE)
        dz_ref[1] = dgm_b.astype(_MXU_DTYPE)
        dbm_ref[:, :d] += jnp.sum(dgm_a, axis=0, keepdims=True)
        dbm_ref[:, d:] += jnp.sum(dgm_b, axis=0, keepdims=True)
        dout_a = dmixed * g_a
        dout_b = dmixed * g_b
        dpa_s[...] += _mm_tn(ya_t, dout_a)
        dpb_s[...] += _mm_tn(yb_t, dout_b)
        dya_ref[...] = _mm_nt(dout_a, pa_s[...])
        dyb_ref[...] = _mm_nt(dout_b, pb_s[...])

        @pl.when(i == n_tiles - 1)
        def _():
            pltpu.sync_copy(dpa_s, dpa_hbm)
            pltpu.sync_copy(dpb_s, dpb_hbm)
            pltpu.sync_copy(dwo_s, dwo_hbm)

    tile = pl.BlockSpec((tm, d), lambda i: (i, 0))
    gm = pl.BlockSpec((2, tm, d), lambda i: (3, i, 0))
    row = lambda n: pl.BlockSpec((1, n), lambda i: (0, 0))
    hbm = pl.BlockSpec(memory_space=pl.ANY)
    act = jax.ShapeDtypeStruct((tokens, d), F32)
    mat = jax.ShapeDtypeStruct((d, d), F32)
    return pl.pallas_call(
        body, name="merge_tail",
        grid=(n_tiles,),
        in_specs=[tile, tile, gm, tile, tile, row(2 * d), row(d), hbm, hbm, hbm],
        out_specs=[tile, tile, tile, gm, row(LANES), row(d), row(2 * d), hbm, hbm, hbm],
        out_shape=[act, act, act, jax.ShapeDtypeStruct((N_GROUPS, tokens, d), _MXU_DTYPE),
                   jax.ShapeDtypeStruct((1, LANES), F32), jax.ShapeDtypeStruct((1, d), F32),
                   jax.ShapeDtypeStruct((1, 2 * d), F32), mat, mat, mat],
        scratch_shapes=[pltpu.VMEM((d, d), _MXU_DTYPE)] * 3 + [pltpu.VMEM((d, d), F32)] * 3,
        compiler_params=_params(("arbitrary",)),
    )(ya, yb, z, x2d, tgt2d, b_merge, final_g, pa, pb, wo)


def _inproj_dw(h_t, dz):
    d, tokens = h_t.shape
    tm = min(2048, tokens)

    def body(h_ref, dz_ref, dw_ref):
        part = _mm(h_ref[...], dz_ref[...])

        @pl.when(pl.program_id(1) == 0)
        def _():
            dw_ref[...] = part

        @pl.when(pl.program_id(1) != 0)
        def _():
            dw_ref[...] += part

    def out_index(s, i):
        g = _group_of_slot(s)
        return (g // 2, 0, g % 2)

    return pl.pallas_call(
        body, name="inproj_dw",
        grid=(N_GROUPS, tokens // tm),
        in_specs=[pl.BlockSpec((d, tm), lambda s, i: (0, i)),
                  pl.BlockSpec((None, tm, D_MODEL), lambda s, i: (s, i, 0))],
        out_specs=pl.BlockSpec((None, d, D_MODEL), out_index),
        out_shape=jax.ShapeDtypeStruct((N_SHARDS, d, 2 * D_MODEL), F32),
        compiler_params=_params(("parallel", "arbitrary")),
    )(h_t, dz)


def _inproj_dw_exchange(h_t, dz, scatter):
    d, tokens = h_t.shape
    tm = min(2048, tokens)
    n_i = tokens // tm
    half = d // 2

    def body(h_ref, dz_ref, *rest):
        n_in, n_out = scatter.n_in, scatter.n_out
        dw_hbm, land_hbm = rest[n_in:n_in + 2]
        acc, local_sems, send_sems, recv_sems = rest[n_in + 2 + n_out:n_in + 6 + n_out]
        carried = scatter.copies(rest[:n_in], rest[n_in + 2:n_in + 2 + n_out], rest[n_in + 6 + n_out:])
        s, i = pl.program_id(0), pl.program_id(1)
        x, y, c, _ = _mesh_position()

        @pl.when((s == 0) & (i == 0))
        def _():
            for cp in carried:
                cp.start()

        part = _mm(h_ref[...], dz_ref[...])
        buf = acc.at[s % 2]

        @pl.when(i == 0)
        def _():
            buf[...] = part

        @pl.when(i != 0)
        def _():
            buf[...] += part

        def copies(k):
            g = _SLOT_TO_GROUP[k]
            cols = pl.ds((g % 2) * D_MODEL, D_MODEL)
            src = acc.at[k % 2]
            mine = pltpu.make_async_copy(src, dw_hbm.at[g // 2, :, cols], local_sems.at[k % 2])
            theirs = _remote(src.at[pl.ds((1 - c) * half, half), :], land_hbm.at[g // 2, :, cols],
                             send_sems, recv_sems, k, (x, y, 1 - c))
            return mine, theirs

        for k in range(N_GROUPS):
            @pl.when((s == k) & (i == n_i - 1))
            def _(k=k):
                if k > 0:
                    mine, theirs = copies(k - 1)
                    mine.wait()
                    theirs.wait_send()
                mine, theirs = copies(k)
                mine.start()
                theirs.start()
                if k == N_GROUPS - 1:
                    mine.wait()
                    theirs.wait_send()
                    for kk in range(N_GROUPS):
                        copies(kk)[1].wait_recv()
                    for cp in carried:
                        cp.wait()

    hbm = pl.BlockSpec(memory_space=pl.ANY)
    more = scatter.plumbing(first_operand=2, first_output=2)
    return pl.pallas_call(
        body, name="inproj_dw_exchange",
        grid=(N_GROUPS, n_i),
        in_specs=[pl.BlockSpec((d, tm), lambda s, i: (0, i)),
                  pl.BlockSpec((None, tm, D_MODEL), lambda s, i: (s, i, 0))] + more[1],
        out_specs=[hbm, hbm] + more[2],
        out_shape=[jax.ShapeDtypeStruct((N_SHARDS, d, 2 * D_MODEL), F32),
                   jax.ShapeDtypeStruct((N_SHARDS, half, 2 * D_MODEL), F32)] + more[3],
        scratch_shapes=[pltpu.VMEM((2, d, D_MODEL), F32), pltpu.SemaphoreType.DMA((2,)),
                        pltpu.SemaphoreType.DMA((N_GROUPS,)), pltpu.SemaphoreType.DMA((N_GROUPS,))] + more[4],
        input_output_aliases=more[5],
        compiler_params=_params(("arbitrary", "arbitrary")),
    )(h_t, dz, *more[0])


def _inproj_dx(dz, w_all, x2d, dx2, norm_g, scatter=None):
    tokens, d = x2d.shape
    tm = min(256, tokens)
    n_tiles = tokens // tm

    def body(dz_ref, w_hbm, x_ref, dx2_ref, g_ref, *rest):
        if scatter:
            n_in, n_out = scatter.n_in, scatter.n_out
            dx_ref, dg_ref = rest[n_in:n_in + 2]
            w_res = rest[n_in + 2 + n_out]
            copies = scatter.copies(rest[:n_in], rest[n_in + 2:n_in + 2 + n_out], rest[n_in + 3 + n_out:])
        else:
            dx_ref, dg_ref, w_res = rest
            copies = []
        i = pl.program_id(0)

        @pl.when(i == 0)
        def _():
            for cp in copies:
                cp.start()
            for slot, g in enumerate(_SLOT_TO_GROUP):
                pltpu.sync_copy(w_hbm.at[g // 2, :, pl.ds((g % 2) * D_MODEL, D_MODEL)],
                                w_res.at[:, pl.ds(slot * D_MODEL, D_MODEL)])
            dg_ref[...] = jnp.zeros_like(dg_ref)

        dz_all = jnp.concatenate([dz_ref[s] for s in range(N_GROUPS)], axis=1)
        dh = jnp.transpose(_mm_nt(w_res[...], dz_all))
        x = x_ref[...]
        r = lax.rsqrt(jnp.mean(x * x, axis=-1, keepdims=True) + EPS)
        xn = x * r
        dg_ref[...] += jnp.sum(dh * xn, axis=0, keepdims=True)
        dxn = dh * g_ref[...]
        dx_ref[...] = r * (dxn - xn * jnp.mean(dxn * xn, axis=-1, keepdims=True)) + dx2_ref[...]

        @pl.when(i == n_tiles - 1)
        def _():
            for cp in copies:
                cp.wait()

    tile = pl.BlockSpec((tm, d), lambda i: (i, 0))
    hbm = pl.BlockSpec(memory_space=pl.ANY)
    in_specs = [pl.BlockSpec((N_GROUPS, tm, D_MODEL), lambda i: (0, i, 0)), hbm, tile, tile,
                pl.BlockSpec((1, d), lambda i: (0, 0))]
    out_specs = [tile, pl.BlockSpec((1, d), lambda i: (0, 0))]
    out_shape = [jax.ShapeDtypeStruct((tokens, d), F32), jax.ShapeDtypeStruct((1, d), F32)]
    scratch = [pltpu.VMEM((d, N_GROUPS * D_MODEL), _MXU_DTYPE)]
    operands, aliases = [dz, w_all, x2d, dx2, norm_g], {}
    if scatter:
        more = scatter.plumbing(first_operand=len(operands), first_output=len(out_shape))
        operands, in_specs, out_specs = operands + more[0], in_specs + more[1], out_specs + more[2]
        out_shape, scratch, aliases = out_shape + more[3], scratch + more[4], more[5]
    return pl.pallas_call(
        body, name="inproj_dx", grid=(n_tiles,), in_specs=in_specs, out_specs=out_specs, out_shape=out_shape,
        scratch_shapes=scratch, input_output_aliases=aliases,
        compiler_params=_params(("arbitrary",)),
    )(*operands)


def _row_tile(rows, cols, itemsize=4, budget=2 * 1024 * 1024):
    tr = rows
    while tr * cols * itemsize > budget and tr % 16 == 0:
        tr //= 2
    return tr


def _cast_into_slot(a, chip, dtype, name):
    rows, cols = a.shape
    tr = _row_tile(rows, cols)

    def body(chip_ref, a_ref, o_ref):
        del chip_ref
        o_ref[...] = a_ref[...].astype(dtype)

    grid_spec = pltpu.PrefetchScalarGridSpec(
        num_scalar_prefetch=1, grid=(rows // tr,),
        in_specs=[pl.BlockSpec((tr, cols), lambda i, chip_ref: (i, 0))],
        out_specs=pl.BlockSpec((None, tr, cols), lambda i, chip_ref: (chip_ref[0], i, 0)))
    return pl.pallas_call(body, name=name, grid_spec=grid_spec,
                          out_shape=jax.ShapeDtypeStruct((N_SHARDS, rows, cols), dtype),
                          compiler_params=_params(("arbitrary",)))(chip, a)


def _sum_slots(stack, name):
    n, rows, cols = stack.shape
    tr = _row_tile(rows, cols * n)

    def body(s_ref, o_ref):
        total = s_ref[0].astype(F32)
        for k in range(1, n):
            total = total + s_ref[k].astype(F32)
        o_ref[...] = total

    return pl.pallas_call(body, name=name, grid=(rows // tr,),
                          in_specs=[pl.BlockSpec((n, tr, cols), lambda i: (0, i, 0))],
                          out_specs=pl.BlockSpec((tr, cols), lambda i: (i, 0)),
                          out_shape=jax.ShapeDtypeStruct((rows, cols), F32),
                          compiler_params=_params(("parallel",)))(stack)


def _add_half(full, landed, place, name):
    n, rows, cols = full.shape
    half = rows // 2
    tr = _row_tile(half, cols)
    nb = half // tr

    def body(place_ref, a_ref, b_ref, o_ref, own_ref):
        total = (a_ref[...] + b_ref[...]).astype(_MXU_DTYPE)
        o_ref[...] = total

        @pl.when(pl.program_id(1) == place_ref[1])
        def _():
            own_ref[...] = total

    grid_spec = pltpu.PrefetchScalarGridSpec(
        num_scalar_prefetch=1, grid=(nb, n),
        in_specs=[pl.BlockSpec((None, tr, cols), lambda i, j, place_ref: (j, place_ref[0] * nb + i, 0)),
                  pl.BlockSpec((None, tr, cols), lambda i, j, place_ref: (j, i, 0))],
        out_specs=[pl.BlockSpec((None, tr, cols), lambda i, j, place_ref: (j, i, 0)),
                   pl.BlockSpec((None, tr, cols), lambda i, j, place_ref: (place_ref[1], i, 0))])
    shape = jax.ShapeDtypeStruct((n, half, cols), _MXU_DTYPE)
    return pl.pallas_call(body, name=name, grid_spec=grid_spec, out_shape=[shape, shape],
                          compiler_params=_params(("parallel", "arbitrary")))(place, full, landed)


def _adamw_update(w, grad, m, v):
    c1 = 1.0 - ADAM_B1 ** ADAM_STEP
    c2 = 1.0 - ADAM_B2 ** ADAM_STEP
    nm = ADAM_B1 * m + (1.0 - ADAM_B1) * grad
    nv = ADAM_B2 * v + (1.0 - ADAM_B2) * (grad * grad)
    return (-ADAM_LR) * ((nm / c1) / (jnp.sqrt(nv / c2) + ADAM_EPS) + ADAM_WD * w), nm, nv


def _adamw(w, g, m, v, name):
    rows, cols = w.shape
    tr = _row_tile(rows, cols, budget=1024 * 1024)

    def body(w_ref, g_ref, m_ref, v_ref, d_ref, nm_ref, nv_ref):
        d_ref[...], nm_ref[...], nv_ref[...] = _adamw_update(w_ref[...], g_ref[...], m_ref[...], v_ref[...])

    spec = pl.BlockSpec((tr, cols), lambda i: (i, 0))
    shape = jax.ShapeDtypeStruct((rows, cols), F32)
    return pl.pallas_call(body, name=name, grid=(rows // tr,), in_specs=[spec] * 4, out_specs=[spec] * 3,
                          out_shape=[shape] * 3, compiler_params=_params(("parallel",)))(w, g, m, v)


def _adamw_halves(w, g_mine, g_sibling, m, v, core, name):
    rows, cols = w.shape
    half = rows // 2
    tr = _row_tile(half, cols, budget=1024 * 1024)
    nb = half // tr

    def body(core_ref, w_ref, gm_ref, gs_ref, m_ref, v_ref, g_ref, d_ref, nm_ref, nv_ref):
        mine = pl.program_id(0) // nb == core_ref[0]
        grad = jnp.where(mine, gm_ref[...], gs_ref[...])
        g_ref[...] = grad
        d_ref[...], nm_ref[...], nv_ref[...] = _adamw_update(w_ref[...], grad, m_ref[...], v_ref[...])

    spec = pl.BlockSpec((tr, cols), lambda i, core_ref: (i, 0))
    half_spec = pl.BlockSpec((tr, cols), lambda i, core_ref: (i % nb, 0))
    grid_spec = pltpu.PrefetchScalarGridSpec(num_scalar_prefetch=1, grid=(rows // tr,),
                                             in_specs=[spec, half_spec, half_spec, spec, spec], out_specs=[spec] * 4)
    shape = jax.ShapeDtypeStruct((rows, cols), F32)
    return pl.pallas_call(body, name=name, grid_spec=grid_spec, out_shape=[shape] * 4,
                          compiler_params=_params(("parallel",)))(core, w, g_mine, g_sibling, m, v)


def _local_step(x, loss_target, w_all, pa, pb, wo, conv_w, b_merge, conv_b, rg_wx, rg_bx, rg_wa, rg_ba,
                rg_lambda, hg_lb_logits, hg_norm_g, norm_g, final_norm_g, gather=None, reduction=None):
    batch, seq, d = x.shape
    x2d = x.reshape(batch * seq, d)
    tgt2d = loss_target.reshape(batch * seq, d)
    if gather is None:
        z, h_t = _inproj_fwd(x2d, norm_g, w_all)
    else:
        z, h_t, (w_all, pa, pb, wo), cw_all = _inproj_fwd_gather(x2d, norm_g, *gather)
        pa, pb, wo = (t.reshape(d, d) for t in (pa, pb, wo))
        conv_w = jnp.transpose(cw_all, (1, 0, 2)).reshape(CONV_WIDTH, d)
    lru = (conv_w, conv_b, rg_wx, rg_bx, rg_wa, rg_ba, rg_lambda)
    ya, hl, kept = _branch_a_fwd(z, *lru, batch, seq)
    yb, states = _branch_b_fwd(z, hg_lb_logits, hg_norm_g, batch, seq)
    dya, dyb, dx2, dz, loss, d_final_g, d_b_merge, d_pa, d_pb, d_wo = _merge_tail(
        ya, yb, z, x2d, tgt2d, b_merge, final_norm_g, pa, pb, wo)
    dz, d_lb_logits, d_hg_g = _branch_b_bwd(z, states, dyb, dz, hg_lb_logits, hg_norm_g, batch, seq)
    dz, d_conv_w, d_conv_b, d_wx, d_bx, d_wa, d_ba, d_lam = _branch_a_bwd(z, hl, kept, dya, dz, *lru, batch, seq)
    small = dict(b_merge=d_b_merge, conv_w=d_conv_w, conv_b=d_conv_b, rg_wx=d_wx, rg_bx=d_bx, rg_wa=d_wa,
                 rg_ba=d_ba, rg_lambda=d_lam, hg_lb_logits=d_lb_logits, hg_norm_g=d_hg_g,
                 norm_g=jnp.zeros((1, d), F32), final_norm_g=d_final_g)
    if reduction is None:
        big = (_inproj_dw(h_t, dz), d_pa, d_pb, d_wo)
        grad_x, small["norm_g"] = _inproj_dx(dz, w_all, x2d, dx2, norm_g)
        return loss[0, 0], grad_x.reshape(batch, seq, d), big, small
    first, second = reduction
    d_w_in, landed_w_in, *scattered_first = _inproj_dw_exchange(h_t, dz, first((d_pa, d_pb, d_wo), small))
    grad_x, d_norm_g, *scattered_second = _inproj_dx(dz, w_all, x2d, dx2, norm_g, scatter=second(d_w_in, landed_w_in))
    return loss[0, 0], grad_x.reshape(batch, seq, d), d_norm_g, (scattered_first, scattered_second)


_SMALL_ORDER = ("b_merge", "conv_w", "conv_b", "rg_wx", "rg_bx", "rg_wa", "rg_ba", "rg_lambda", "hg_lb_logits",
                "hg_norm_g", "norm_g", "final_norm_g")
N_DEV = 8
PIECE_ROWS = 272


def _pack_small(tree):
    flat = jnp.concatenate([tree[k].reshape(-1) for k in _SMALL_ORDER])
    flat = jnp.pad(flat, (0, N_DEV * PIECE_ROWS * LANES - flat.shape[0]))
    return flat.reshape(N_DEV * PIECE_ROWS, LANES)


def _unpack_small(packed, like):
    flat = packed.reshape(-1)
    out, pos = {}, 0
    for k in _SMALL_ORDER:
        n = like[k].size
        out[k] = flat[pos:pos + n].reshape(like[k].shape)
        pos += n
    return out


def _mesh_position():
    x, y, c = lax.axis_index("x"), lax.axis_index("y"), lax.axis_index("c")
    other_chips = [(1 - x, y), (x, 1 - y), (1 - x, 1 - y)]
    return x, y, c, other_chips


def _other_devices(x, y, c):
    flips = [(fx, fy, fc) for fx in (0, 1) for fy in (0, 1) for fc in (0, 1) if (fx, fy, fc) != (0, 0, 0)]
    return [(jnp.where(fx, 1 - x, x), jnp.where(fy, 1 - y, y), jnp.where(fc, 1 - c, c)) for fx, fy, fc in flips]


def _remote(src, dst, send_sems, recv_sems, k, device):
    return pltpu.make_async_remote_copy(src_ref=src, dst_ref=dst, send_sem=send_sems.at[k], recv_sem=recv_sems.at[k],
                                        device_id=device, device_id_type=MESH)


def _exchange_halves(bigs, small):
    n_big = len(bigs)
    n_sem = n_big + N_DEV - 1

    def body(*refs):
        srcs, small_src = refs[:n_big], refs[n_big]
        outs, small_out = refs[n_big + 1:2 * n_big + 1], refs[2 * n_big + 1]
        send_sems, recv_sems, local_sem = refs[2 * n_big + 2:]
        x, y, c, _ = _mesh_position()
        me, sibling = 4 * x + 2 * y + c, (x, y, 1 - c)
        mine = pltpu.make_async_copy(small_src.at[pl.ds(me * PIECE_ROWS, PIECE_ROWS), :], small_out.at[me], local_sem)
        mine.start()
        copies = []
        for a in range(n_big):
            hs = srcs[a].shape[1] // 2
            copies.append(_remote(srcs[a].at[:, pl.ds((1 - c) * hs, hs), :], outs[a], send_sems, recv_sems, a, sibling))
        for k, (px, py, pc) in enumerate(_other_devices(x, y, c)):
            piece = small_src.at[pl.ds((4 * px + 2 * py + pc) * PIECE_ROWS, PIECE_ROWS), :]
            copies.append(_remote(piece, small_out.at[me], send_sems, recv_sems, n_big + k, (px, py, pc)))
        for cp in copies:
            cp.start()
        for cp in copies:
            cp.wait()
        mine.wait()

    hbm = pl.BlockSpec(memory_space=pl.ANY)
    out_shape = [jax.ShapeDtypeStruct((g.shape[0], g.shape[1] // 2, g.shape[2]), F32) for g in bigs]
    out_shape.append(jax.ShapeDtypeStruct((N_DEV, PIECE_ROWS, LANES), F32))
    return pl.pallas_call(
        body, name="exchange_halves",
        in_specs=[hbm] * (n_big + 1), out_specs=[hbm] * (n_big + 1), out_shape=out_shape,
        scratch_shapes=[pltpu.SemaphoreType.DMA((n_sem,)), pltpu.SemaphoreType.DMA((n_sem,)), pltpu.SemaphoreType.DMA],
    )(*bigs, small)


class _Scatter:
    def __init__(self, bigs, by_chip, small=None):
        self.bigs, self.by_chip, self.small = list(bigs), list(by_chip), small
        self.n_big = len(self.bigs)
        self.n_in = 2 * self.n_big + (small is not None)
        self.n_out = self.n_big + (small is not None)
        self.n_scratch = 2 + (small is not None)

    def plumbing(self, first_operand, first_output):
        hbm = pl.BlockSpec(memory_space=pl.ANY)
        n_sem = 3 * self.n_big + (N_DEV - 1 if self.small is not None else 0)
        operands = self.bigs + self.by_chip + ([self.small] if self.small is not None else [])
        out_shapes = [jax.ShapeDtypeStruct(g.shape, g.dtype) for g in self.by_chip]
        scratch = [pltpu.SemaphoreType.DMA((n_sem,)), pltpu.SemaphoreType.DMA((n_sem,))]
        if self.small is not None:
            out_shapes.append(jax.ShapeDtypeStruct((N_DEV, PIECE_ROWS, LANES), F32))
            scratch.append(pltpu.SemaphoreType.DMA)
        aliases = {first_operand + self.n_big + a: first_output + a for a in range(self.n_big)}
        return operands, [hbm] * self.n_in, [hbm] * self.n_out, out_shapes, scratch, aliases

    def copies(self, in_refs, out_refs, scratch_refs):
        srcs, outs = in_refs[:self.n_big], out_refs[:self.n_big]
        send_sems, recv_sems = scratch_refs[:2]
        x, y, c, chips = _mesh_position()
        chip, me = 2 * x + y, 4 * x + 2 * y + c
        copies = []
        for a in range(self.n_big):
            for j, (cx, cy) in enumerate(chips):
                copies.append(_remote(srcs[a].at[2 * cx + cy], outs[a].at[chip], send_sems, recv_sems, 3 * a + j,
                                      (cx, cy, c)))
        if self.small is not None:
            small_src, small_out = in_refs[2 * self.n_big], out_refs[self.n_big]
            copies.append(pltpu.make_async_copy(small_src, small_out.at[me], scratch_refs[2]))
            for k, peer in enumerate(_other_devices(x, y, c)):
                copies.append(_remote(small_src, small_out.at[me], send_sems, recv_sems, 3 * self.n_big + k, peer))
        return copies


def _swap_halves(halves, vec):
    n_big = len(halves)

    def body(*refs):
        srcs, vec_src = refs[:n_big], refs[n_big]
        outs, vec_out = refs[n_big + 1:2 * n_big + 1], refs[2 * n_big + 1]
        send_sems, recv_sems, local_sem = refs[2 * n_big + 2:]
        x, y, c, _ = _mesh_position()
        me = 4 * x + 2 * y + c
        copies = [pltpu.make_async_copy(vec_src, vec_out.at[me], local_sem)]
        copies += [_remote(srcs[a], outs[a], send_sems, recv_sems, a, (x, y, 1 - c)) for a in range(n_big)]
        copies += [_remote(vec_src, vec_out.at[me], send_sems, recv_sems, n_big + k, peer)
                   for k, peer in enumerate(_other_devices(x, y, c))]
        for cp in copies:
            cp.start()
        for cp in copies:
            cp.wait()

    hbm = pl.BlockSpec(memory_space=pl.ANY)
    n_sem = n_big + N_DEV - 1
    return pl.pallas_call(
        body, name="swap_halves",
        in_specs=[hbm] * (n_big + 1), out_specs=[hbm] * (n_big + 1),
        out_shape=[jax.ShapeDtypeStruct(h.shape, F32) for h in halves] + [jax.ShapeDtypeStruct((N_DEV,) + vec.shape, F32)],
        scratch_shapes=[pltpu.SemaphoreType.DMA((n_sem,)), pltpu.SemaphoreType.DMA((n_sem,)), pltpu.SemaphoreType.DMA],
    )(*halves, vec)


def kernel(x, w_in, b_merge, conv_w, conv_b, rg_wx, rg_bx, rg_wa, rg_ba, rg_lambda, hg_lb_logits, hg_norm_g, proj_a, proj_b, w_out, norm_g, final_norm_g, loss_target, m_w_in, m_b_merge, m_conv_w, m_conv_b, m_rg_wx, m_rg_bx, m_rg_wa, m_rg_ba, m_rg_lambda, m_hg_lb_logits, m_hg_norm_g, m_proj_a, m_proj_b, m_w_out, m_norm_g, m_final_norm_g, v_w_in, v_b_merge, v_conv_w, v_conv_b, v_rg_wx, v_rg_bx, v_rg_wa, v_rg_ba, v_rg_lambda, v_hg_lb_logits, v_hg_norm_g, v_proj_a, v_proj_b, v_w_out, v_norm_g, v_final_norm_g):
    d = D_MODEL
    weights = dict(w_in=w_in, b_merge=b_merge, conv_w=conv_w, conv_b=conv_b, rg_wx=rg_wx, rg_bx=rg_bx, rg_wa=rg_wa,
                   rg_ba=rg_ba, rg_lambda=rg_lambda, hg_lb_logits=hg_lb_logits, hg_norm_g=hg_norm_g, proj_a=proj_a,
                   proj_b=proj_b, w_out=w_out, norm_g=norm_g, final_norm_g=final_norm_g)
    m = dict(w_in=m_w_in, b_merge=m_b_merge, conv_w=m_conv_w, conv_b=m_conv_b, rg_wx=m_rg_wx, rg_bx=m_rg_bx,
             rg_wa=m_rg_wa, rg_ba=m_rg_ba, rg_lambda=m_rg_lambda, hg_lb_logits=m_hg_lb_logits, hg_norm_g=m_hg_norm_g,
             proj_a=m_proj_a, proj_b=m_proj_b, w_out=m_w_out, norm_g=m_norm_g, final_norm_g=m_final_norm_g)
    v = dict(w_in=v_w_in, b_merge=v_b_merge, conv_w=v_conv_w, conv_b=v_conv_b, rg_wx=v_rg_wx, rg_bx=v_rg_bx,
             rg_wa=v_rg_wa, rg_ba=v_rg_ba, rg_lambda=v_rg_lambda, hg_lb_logits=v_hg_lb_logits, hg_norm_g=v_hg_norm_g,
             proj_a=v_proj_a, proj_b=v_proj_b, w_out=v_w_out, norm_g=v_norm_g, final_norm_g=v_final_norm_g)
    big_names = ("w_in", "proj_a", "proj_b", "w_out")

    core = lax.axis_index("c").astype(jnp.int32).reshape(1)
    chip = (2 * lax.axis_index("x") + lax.axis_index("y")).astype(jnp.int32)

    slotted = [_cast_into_slot(weights[k][0], chip.reshape(1), _MXU_DTYPE, f"cast_{k}") for k in big_names]
    conv_slotted = _cast_into_slot(conv_w[0], chip.reshape(1), F32, "slot_conv_w")

    small_shapes = {}

    place = jnp.concatenate([core, chip.reshape(1)])

    def reduce_proj_and_small(proj_grads, small_grads):
        small_shapes.update({k: t.shape for k, t in small_grads.items()})
        bigs = [g.reshape(N_SHARDS, d // N_SHARDS, d) for g in proj_grads]
        *landed, small_landed = _exchange_halves(bigs, _pack_small(small_grads))
        sums = [_add_half(g, l, place, f"add_half_{1 + a}") for a, (g, l) in enumerate(zip(bigs, landed))]
        return _Scatter([s[0] for s in sums], [s[1] for s in sums], _sum_slots(small_landed, "sum_small"))

    def reduce_w_in(d_w_in, landed):
        partial, own_slot = _add_half(d_w_in, landed, place, "add_half_0")
        return _Scatter([partial], [own_slot])

    loss_part, grad_x, d_norm_g, ((*by_chip_proj, small_all), by_chip_w_in) = _local_step(
        x, loss_target, None, None, None, None, None,
        b_merge, conv_b, rg_wx[0], rg_bx.reshape(1, d), rg_wa[0], rg_ba.reshape(1, d), rg_lambda, hg_lb_logits,
        hg_norm_g, norm_g, final_norm_g.reshape(1, d), gather=(slotted, conv_slotted, chip.reshape(1)),
        reduction=(reduce_proj_and_small, reduce_w_in))
    mine = [_sum_slots(s, f"sum_chips_{a}") for a, s in enumerate(by_chip_w_in + by_chip_proj)]
    late = jnp.concatenate([d_norm_g.reshape(SUBLANES, LANES), jnp.full((SUBLANES, LANES), loss_part, F32)])
    *theirs, late_parts = _swap_halves(mine, late)
    late_sum = _sum_slots(late_parts, "sum_late")
    loss = late_sum[SUBLANES, 0]
    small_red = _unpack_small(small_all, {k: jax.ShapeDtypeStruct(s, F32) for k, s in small_shapes.items()})
    small_red["norm_g"] = late_sum[:SUBLANES].reshape(1, d)

    grads, delta, new_m, new_v = {}, {}, {}, {}
    for k, g_mine, g_theirs in zip(big_names, mine, theirs):
        out = _adamw_halves(weights[k][0], g_mine, g_theirs, m[k][0], v[k][0], core, f"adamw_{k}")
        grads[k], delta[k], new_m[k], new_v[k] = (t.reshape(weights[k].shape) for t in out)
    cols = d // N_SHARDS
    g_conv = lax.dynamic_slice(small_red["conv_w"], (0, chip * cols), (CONV_WIDTH, cols))
    grads["conv_w"] = g_conv.reshape(conv_w.shape)
    dl, nm, nv = _adamw(conv_w[0], g_conv, m_conv_w[0], v_conv_w[0], "adamw_conv_w")
    delta["conv_w"], new_m["conv_w"], new_v["conv_w"] = (t.reshape(conv_w.shape) for t in (dl, nm, nv))
    rest = [k for k in _SMALL_ORDER if k != "conv_w"]
    like = {k: (weights[k] if k != "conv_w" else jnp.zeros((CONV_WIDTH, d), F32)) for k in _SMALL_ORDER}
    packs = [_pack_small({k: (t[k] if k != "conv_w" else like[k]) for k in _SMALL_ORDER}) for t in (weights, m, v)]
    g_pack = _pack_small({k: small_red[k].reshape(like[k].shape) for k in _SMALL_ORDER})
    outs = [_unpack_small(p, like) for p in _adamw(packs[0], g_pack, packs[1], packs[2], "adamw_small")]
    for k in rest:
        grads[k] = small_red[k].reshape(weights[k].shape)
        delta[k], new_m[k], new_v[k] = outs[0][k], outs[1][k], outs[2][k]

    order = ("w_in", "b_merge", "conv_w", "conv_b", "rg_wx", "rg_bx", "rg_wa", "rg_ba", "rg_lambda", "hg_lb_logits",
             "hg_norm_g", "proj_a", "proj_b", "w_out", "norm_g", "final_norm_g")
    return (loss, grad_x, *[grads[k] for k in order], *[delta[k] for k in order], *[new_m[k] for k in order],
            *[new_v[k] for k in order])
```

```python
import functools

import jax
import jax.numpy as jnp
from jax import lax
from jax.experimental import pallas as pl
from jax.experimental.pallas import tpu as pltpu

F32 = jnp.float32
_MXU_DTYPE = jnp.bfloat16

D_MODEL = 1024
LANES = 128
SUBLANES = 8
N_BLK = D_MODEL // LANES
N_GROUPS = 8
N_SHARDS = 4
CONV_WIDTH = 4
LRU_C = 8.0
CHUNK = 64
CHUNKS_IN_FLIGHT = 16
HG_SCALE = float(LANES) ** -0.5
EPS = 1e-6
ADAM_LR, ADAM_B1, ADAM_B2, ADAM_EPS, ADAM_WD, ADAM_STEP = 0.001, 0.9, 0.999, 1e-08, 0.01, 10
VMEM_LIMIT = 56 * 1024 * 1024
VMEM_LIMIT_BIG = 60 * 1024 * 1024
MESH = pl.DeviceIdType.MESH

_SLOT_TO_GROUP = (2, 3, 4, 5, 0, 1, 6, 7)


def _group_of_slot(s):
    return jnp.where(s < 4, s + 2, jnp.where(s < 6, s - 4, s))


def _mm(a, b):
    return lax.dot_general(a.astype(_MXU_DTYPE), b.astype(_MXU_DTYPE), (((1,), (0,)), ((), ())),
                           preferred_element_type=F32)


def _mm_nt(a, b):
    return lax.dot_general(a.astype(_MXU_DTYPE), b.astype(_MXU_DTYPE), (((1,), (1,)), ((), ())),
                           preferred_element_type=F32)


def _mm_tn(a, b):
    return lax.dot_general(a.astype(_MXU_DTYPE), b.astype(_MXU_DTYPE), (((0,), (0,)), ((), ())),
                           preferred_element_type=F32)


def _sigmoid(x):
    return 0.5 * jnp.tanh(0.5 * x) + 0.5


def _log1p_pos(y):
    series = y * (1.0 - y * (0.5 - y * (1.0 / 3.0 - y * 0.25)))
    return jnp.where(y < 0.01, series, jnp.log(1.0 + y))


def _softplus(x):
    return jnp.maximum(x, 0.0) + _log1p_pos(jnp.exp(-jnp.abs(x)))


def _shift_down(x, n):
    rolled = pltpu.roll(x, n, 0)
    edge = SUBLANES if (n < SUBLANES and x.shape[0] > SUBLANES) else x.shape[0]
    rows = lax.broadcasted_iota(jnp.int32, (edge, x.shape[1]), 0)
    head = jnp.where(rows >= n, rolled[:edge], 0.0)
    return head if edge == x.shape[0] else jnp.concatenate([head, rolled[edge:]], axis=0)


def _shift_up(x, n):
    size = x.shape[0]
    rolled = pltpu.roll(x, size - n, 0)
    edge = SUBLANES if (n < SUBLANES and size > SUBLANES) else size
    rows = lax.broadcasted_iota(jnp.int32, (edge, x.shape[1]), 0)
    tail = jnp.where(rows < edge - n, rolled[size - edge:], 0.0)
    return tail if edge == size else jnp.concatenate([rolled[:size - edge], tail], axis=0)


def _params(dims, vmem=VMEM_LIMIT):
    return pltpu.CompilerParams(dimension_semantics=dims, vmem_limit_bytes=vmem)


def _load_w_in_by_slot(w_hbm, w_res):
    for slot, g in enumerate(_SLOT_TO_GROUP):
        pltpu.sync_copy(w_hbm.at[g // 2, :, pl.ds((g % 2) * D_MODEL, D_MODEL)], w_res.at[slot])


def _inproj_fwd(x2d, norm_g, w_all):
    tokens, d = x2d.shape
    tm = min(512, tokens)

    def body(x_ref, g_ref, w_hbm, z_ref, ht_ref, h_scr, w_res):
        @pl.when((pl.program_id(0) == 0) & (pl.program_id(1) == 0))
        def _():
            _load_w_in_by_slot(w_hbm, w_res)

        @pl.when(pl.program_id(1) == 0)
        def _():
            x = x_ref[...]
            r = lax.rsqrt(jnp.mean(x * x, axis=-1, keepdims=True) + EPS)
            h = (x * r) * g_ref[...]
            h_scr[...] = h.astype(_MXU_DTYPE)
            ht_ref[...] = jnp.transpose(h).astype(_MXU_DTYPE)

        z_ref[...] = _mm(h_scr[...], w_res[pl.program_id(1)])

    return pl.pallas_call(
        body, name="inproj_fwd",
        grid=(tokens // tm, N_GROUPS),
        in_specs=[pl.BlockSpec((tm, d), lambda i, s: (i, 0)),
                  pl.BlockSpec((1, d), lambda i, s: (0, 0)),
                  pl.BlockSpec(memory_space=pl.ANY)],
        out_specs=[pl.BlockSpec((None, tm, D_MODEL), lambda i, s: (s, i, 0)),
                   pl.BlockSpec((d, tm), lambda i, s: (0, i))],
        out_shape=[jax.ShapeDtypeStruct((N_GROUPS, tokens, D_MODEL), F32),
                   jax.ShapeDtypeStruct((d, tokens), _MXU_DTYPE)],
        scratch_shapes=[pltpu.VMEM((tm, d), _MXU_DTYPE), pltpu.VMEM((N_GROUPS, d, D_MODEL), _MXU_DTYPE)],
        compiler_params=_params(("arbitrary", "arbitrary")),
    )(x2d, norm_g, w_all)


def _slot_of_group(g):
    return jnp.where(g < 2, g + 4, jnp.where(g < 6, g - 2, g))


def _inproj_fwd_gather(x2d, norm_g, slotted, conv_slotted, chip):
    tokens, d = x2d.shape
    tm = min(512, tokens)
    n_tiles = tokens // tm
    n_big = len(slotted)
    n_sem = 6 * (n_big + 1) + 3
    last_pass = N_GROUPS - 1

    def shard_of(k, chip_id):
        x, y = chip_id // 2, chip_id % 2
        return 2 * jnp.where(k % 2 == 1, 1 - x, x) + jnp.where(k // 2 == 1, 1 - y, y)

    def body(chip_ref, x_ref, g_ref, *rest):
        bufs, cw = rest[n_big + 1:2 * n_big + 1], rest[2 * n_big + 1]
        z_ref, ht_ref = rest[2 * n_big + 2:2 * n_big + 4]
        h_all, slab, send_sems, recv_sems, slab_sems = rest[2 * n_big + 4:]
        del chip_ref
        p, i = pl.program_id(0), pl.program_id(1)
        x, y, c, chips = _mesh_position()
        me, sibling = 2 * x + y, (x, y, 1 - c)

        pieces = [(0, 0), (0, 1)] + [(a, None) for a in range(1, n_big)]

        def half(piece, slot, which):
            a, q = pieces[piece]
            hs = bufs[a].shape[1] // 2
            cols = slice(None) if q is None else pl.ds(q * D_MODEL, D_MODEL)
            return bufs[a].at[slot, pl.ds(which * hs, hs), cols]

        def send(piece, j):
            mine = half(piece, me, c)
            return _remote(mine, mine, send_sems, recv_sems, 6 * piece + j, (chips[j][0], chips[j][1], c))

        def arrival(piece, j):
            landed = half(piece, 2 * chips[j][0] + chips[j][1], c)
            return _remote(landed, landed, send_sems, recv_sems, 6 * piece + j, (chips[j][0], chips[j][1], c))

        def passed_on(piece, j, which):
            landed = half(piece, 2 * chips[j][0] + chips[j][1], which)
            return _remote(landed, landed, send_sems, recv_sems, 6 * piece + 3 + j, sibling)

        def conv_copy(j, slot):
            return _remote(cw.at[slot], cw.at[slot], send_sems, recv_sems, 6 * len(pieces) + j,
                           (chips[j][0], chips[j][1], c))

        def land(piece, j):
            arrival(piece, j).wait_recv()
            passed_on(piece, j, c).start()
            passed_on(piece, j, 1 - c).wait_recv()

        def slab_copy(pv):
            src = bufs[0].at[shard_of(pv // 2, me), :, pl.ds((pv % 2) * D_MODEL, D_MODEL)]
            return pltpu.make_async_copy(src, slab.at[pv % 2], slab_sems.at[pv % 2])

        @pl.when((p == 0) & (i == 0))
        def _():
            for q in range(2):
                send(q, 0).start()
                send(q, 1).start()
            slab_copy(0).start()

        for pv in range(N_GROUPS):
            @pl.when((p == pv) & (i == 0))
            def _(pv=pv):
                slab_copy(pv).wait()

        rows = pl.ds(pl.multiple_of(i * tm, tm), tm)

        @pl.when(p == 0)
        def _():
            xt = x_ref[...]
            r = lax.rsqrt(jnp.mean(xt * xt, axis=-1, keepdims=True) + EPS)
            h = (xt * r) * g_ref[...]
            h_all[rows, :] = h.astype(_MXU_DTYPE)
            ht_ref[...] = jnp.transpose(h).astype(_MXU_DTYPE)

        z_ref[...] = _mm(h_all[rows, :], slab[p % 2])

        for pv in range(1, N_GROUPS):
            @pl.when((p == pv - 1) & (i == n_tiles - 1))
            def _(pv=pv):
                j, q = pv // 2 - 1, pv % 2
                if j >= 0:
                    land(q, j)
                if (j, q) == (0, 0):
                    send(0, 2).start()
                    send(1, 2).start()
                if (j, q) == (1, 0):
                    for piece in range(2, len(pieces)):
                        for jj in range(3):
                            send(piece, jj).start()
                    for jj in range(3):
                        conv_copy(jj, me).start()
                slab_copy(pv).start()

        @pl.when((p == last_pass) & (i == n_tiles - 1))
        def _():
            for piece in range(2, len(pieces)):
                for j in range(3):
                    land(piece, j)
            for j in range(3):
                conv_copy(j, 2 * chips[j][0] + chips[j][1]).wait_recv()
            for piece in range(len(pieces)):
                for j in range(3):
                    send(piece, j).wait_send()
                    passed_on(piece, j, c).wait_send()
            for j in range(3):
                conv_copy(j, me).wait_send()

    def z_index(p, i, chip_ref):
        g = 2 * shard_of(p // 2, chip_ref[0]) + p % 2
        return (_slot_of_group(g), i, 0)

    def first_pass_tile(p, i, chip_ref):
        return jnp.where(p == 0, i, n_tiles - 1)

    hbm = pl.BlockSpec(memory_space=pl.ANY)
    operands = list(slotted) + [conv_slotted]
    grid_spec = pltpu.PrefetchScalarGridSpec(
        num_scalar_prefetch=1, grid=(N_GROUPS, n_tiles),
        in_specs=[pl.BlockSpec((tm, d), lambda p, i, chip_ref: (first_pass_tile(p, i, chip_ref), 0)),
                  pl.BlockSpec((1, d), lambda p, i, chip_ref: (0, 0))] + [hbm] * (n_big + 1),
        out_specs=[hbm] * (n_big + 1) + [pl.BlockSpec((None, tm, D_MODEL), z_index),
                                         pl.BlockSpec((d, tm), lambda p, i, chip_ref: (0, first_pass_tile(p, i, chip_ref)))],
        scratch_shapes=[pltpu.VMEM((tokens, d), _MXU_DTYPE), pltpu.VMEM((2, d, D_MODEL), _MXU_DTYPE),
                        pltpu.SemaphoreType.DMA((n_sem,)), pltpu.SemaphoreType.DMA((n_sem,)),
                        pltpu.SemaphoreType.DMA((2,))])
    out = pl.pallas_call(
        body, name="inproj_fwd_gather", grid_spec=grid_spec,
        out_shape=[jax.ShapeDtypeStruct(a.shape, a.dtype) for a in operands]
        + [jax.ShapeDtypeStruct((N_GROUPS, tokens, D_MODEL), F32), jax.ShapeDtypeStruct((d, tokens), _MXU_DTYPE)],
        input_output_aliases={3 + a: a for a in range(n_big + 1)},
        compiler_params=_params(("arbitrary", "arbitrary")),
    )(chip, x2d, norm_g, *operands)
    return out[n_big + 1], out[n_big + 2], out[:n_big], out[n_big]


def _lane_blocks(x):
    return [x[:, k * LANES:(k + 1) * LANES] for k in range(x.shape[1] // LANES)]


def _block_diag(x, w_ref, transposed=False):
    mm = _mm_nt if transposed else _mm
    return jnp.concatenate([mm(xk, w_ref[k]) for k, xk in enumerate(_lane_blocks(x))], axis=1)


def _lru_gates(xa, cw_ref, cb_ref, wx_ref, bx_ref, wa_ref, ba_ref, lam_ref):
    xc = (cb_ref[...] + cw_ref[3:4, :] * xa + cw_ref[2:3, :] * _shift_down(xa, 1)
          + cw_ref[1:2, :] * _shift_down(xa, 2) + cw_ref[0:1, :] * _shift_down(xa, 3))
    gi = _sigmoid(_block_diag(xc, wx_ref) + bx_ref[...])
    gr = _sigmoid(_block_diag(xc, wa_ref) + ba_ref[...])
    sp = _softplus(-lam_ref[...])
    a, mult, inv_mult = _lru_decay(gr, sp)
    return xc, gi, gr, sp, a, mult, inv_mult


def _lru_decay(gr, sp):
    log_a = (-LRU_C) * gr * sp
    a = jnp.exp(log_a)
    y = 2.0 * log_a
    mult_sq = jnp.where(y > -1e-3, -y * (1.0 + 0.5 * y), 1.0 - a * a)
    inv_mult = lax.rsqrt(jnp.maximum(mult_sq, 1e-37))
    return a, mult_sq * inv_mult, inv_mult


def _tile_rows(width):
    return lax.broadcasted_iota(jnp.int32, (SUBLANES, width), 0)


def _scan_forward(a_scr, u_scr, h_scr, seq):
    width = a_scr.shape[1]
    rows = _tile_rows(width)

    def tile(j, carry):
        sl = pl.ds(pl.multiple_of(j * SUBLANES, SUBLANES), SUBLANES)
        a = a_scr[sl, :]
        u = u_scr[sl, :]
        for d in (1, 2, 4):
            keep = rows >= d
            a_sh = jnp.where(keep, pltpu.roll(a, d, 0), 1.0)
            u_sh = jnp.where(keep, pltpu.roll(u, d, 0), 0.0)
            u = a * u_sh + u
            a = a * a_sh
        h = u + a * carry
        h_scr[sl, :] = h
        return jnp.broadcast_to(h[SUBLANES - 1:SUBLANES, :], (SUBLANES, width))

    lax.fori_loop(0, seq // SUBLANES, tile, jnp.zeros((SUBLANES, width), F32))


def _scan_backward(c_scr, d_scr, g_scr, seq):
    width = c_scr.shape[1]
    rows = _tile_rows(width)
    n_tiles = seq // SUBLANES

    def tile(jj, carry):
        j = n_tiles - 1 - jj
        sl = pl.ds(pl.multiple_of(j * SUBLANES, SUBLANES), SUBLANES)
        c = c_scr[sl, :]
        g = d_scr[sl, :]
        for d in (1, 2, 4):
            keep = rows < SUBLANES - d
            c_sh = jnp.where(keep, pltpu.roll(c, SUBLANES - d, 0), 1.0)
            g_sh = jnp.where(keep, pltpu.roll(g, SUBLANES - d, 0), 0.0)
            g = c * g_sh + g
            c = c * c_sh
        g = g + c * carry
        g_scr[sl, :] = g
        return jnp.broadcast_to(g[0:1, :], (SUBLANES, width))

    lax.fori_loop(0, n_tiles, tile, jnp.zeros((SUBLANES, width), F32))


LRU_BLOCKS_PER_STEP = 2
LRU_LANES = LRU_BLOCKS_PER_STEP * LANES
LRU_STEPS = N_BLK // LRU_BLOCKS_PER_STEP


def _lru_param_specs(cb_axis):
    def pick(*ids):
        return ids[cb_axis]

    vec = pl.BlockSpec((1, LRU_LANES), lambda *ids: (0, pick(*ids)))
    mat = pl.BlockSpec((LRU_BLOCKS_PER_STEP, LANES, LANES), lambda *ids: (pick(*ids), 0, 0))
    return [pl.BlockSpec((CONV_WIDTH, LRU_LANES), lambda *ids: (0, pick(*ids))), vec, mat, vec, mat, vec, vec]


def _branch_a_fwd(z, conv_w, conv_b, wx, bx, wa, ba, lam, batch, seq):
    tokens = batch * seq

    def body(z_ref, cw_ref, cb_ref, wx_ref, bx_ref, wa_ref, ba_ref, lam_ref, ya_ref, hl_ref, kept_ref, a_scr, u_scr):
        xa = z_ref[0]
        ga = z_ref[1]
        xc, gi, gr, _, a, mult, _ = _lru_gates(xa, cw_ref, cb_ref, wx_ref, bx_ref, wa_ref, ba_ref, lam_ref)
        kept_ref[0], kept_ref[1], kept_ref[2] = xc, gi, gr
        a_scr[...] = a
        u_scr[...] = mult * gi * xc
        _scan_forward(a_scr, u_scr, hl_ref, seq)
        ya_ref[...] = (hl_ref[...] * (ga * _sigmoid(ga))).astype(_MXU_DTYPE)

    blk = pl.BlockSpec((seq, LRU_LANES), lambda b, c: (b, c))
    return pl.pallas_call(
        body, name="branch_a_fwd",
        grid=(batch, LRU_STEPS),
        in_specs=[pl.BlockSpec((2, seq, LRU_LANES), lambda b, c: (2, b, c))] + _lru_param_specs(1),
        out_specs=[blk, blk, pl.BlockSpec((3, seq, LRU_LANES), lambda b, c: (0, b, c))],
        out_shape=[jax.ShapeDtypeStruct((tokens, D_MODEL), _MXU_DTYPE), jax.ShapeDtypeStruct((tokens, D_MODEL), F32),
                   jax.ShapeDtypeStruct((3, tokens, D_MODEL), F32)],
        scratch_shapes=[pltpu.VMEM((seq, LRU_LANES), F32), pltpu.VMEM((seq, LRU_LANES), F32)],
        compiler_params=_params(("parallel", "parallel")),
    )(z, conv_w, conv_b, wx, bx, wa, ba, lam)


def _branch_a_bwd(z, hl, kept, dya, dz, conv_w, conv_b, wx, bx, wa, ba, lam, batch, seq):
    def body(z_ref, hl_ref, kept_ref, dya_ref, dz_in_ref, cw_ref, cb_ref, wx_ref, bx_ref, wa_ref, ba_ref, lam_ref,
             dz_ref, dcw_ref, dcb_ref, dwx_ref, dbx_ref, dwa_ref, dba_ref, dlam_ref, c_scr, d_scr):
        del dz_in_ref, cb_ref, bx_ref, ba_ref
        g_scr = d_scr
        xa = z_ref[0]
        ga = z_ref[1]
        hl = hl_ref[...]
        dya = dya_ref[...]
        xc, gi, gr = kept_ref[0], kept_ref[1], kept_ref[2]
        sp = _softplus(-lam_ref[...])
        a, mult, inv_mult = _lru_decay(gr, sp)
        sga = _sigmoid(ga)
        dz_ref[1] = (dya * hl * (sga * (1.0 + ga * (1.0 - sga)))).astype(_MXU_DTYPE)
        c_scr[...] = _shift_up(a, 1)
        d_scr[...] = dya * (ga * sga)
        _scan_backward(c_scr, d_scr, g_scr, seq)
        g = g_scr[...]
        da = g * _shift_down(hl, 1)
        dmult = g * gi * xc
        dgi = g * mult * xc
        dxc = g * mult * gi
        dlog_a = da * a - dmult * (a * a) * inv_mult
        dgr = dlog_a * (-LRU_C) * sp
        dsp = jnp.sum(dlog_a * gr, axis=0, keepdims=True) * (-LRU_C)
        dlam = -dsp * _sigmoid(-lam_ref[...])
        dpi = dgi * gi * (1.0 - gi)
        dpr = dgr * gr * (1.0 - gr)
        dxc = dxc + _block_diag(dpi, wx_ref, transposed=True) + _block_diag(dpr, wa_ref, transposed=True)
        dwx = jnp.stack([_mm_tn(xk, dk) for xk, dk in zip(_lane_blocks(xc), _lane_blocks(dpi))])
        dwa = jnp.stack([_mm_tn(xk, dk) for xk, dk in zip(_lane_blocks(xc), _lane_blocks(dpr))])
        dbx = jnp.sum(dpi, axis=0, keepdims=True)
        dba = jnp.sum(dpr, axis=0, keepdims=True)
        ahead = [dxc if k == CONV_WIDTH - 1 else _shift_up(dxc, CONV_WIDTH - 1 - k) for k in range(CONV_WIDTH)]
        dxa = sum(cw_ref[k:k + 1, :] * ahead[k] for k in range(CONV_WIDTH))
        dz_ref[0] = dxa.astype(_MXU_DTYPE)
        dcb = jnp.sum(dxc, axis=0, keepdims=True)
        dcw = [jnp.sum(ahead[k] * xa, axis=0, keepdims=True) for k in range(CONV_WIDTH)]

        @pl.when(pl.program_id(1) == 0)
        def _():
            for k in range(CONV_WIDTH):
                dcw_ref[k:k + 1, :] = dcw[k]
            dcb_ref[...] = dcb
            dwx_ref[...] = dwx
            dbx_ref[...] = dbx
            dwa_ref[...] = dwa
            dba_ref[...] = dba
            dlam_ref[...] = dlam

        @pl.when(pl.program_id(1) != 0)
        def _():
            for k in range(CONV_WIDTH):
                dcw_ref[k:k + 1, :] += dcw[k]
            dcb_ref[...] += dcb
            dwx_ref[...] += dwx
            dbx_ref[...] += dbx
            dwa_ref[...] += dwa
            dba_ref[...] += dba
            dlam_ref[...] += dlam

    tokens = batch * seq
    blk = pl.BlockSpec((seq, LRU_LANES), lambda c, b: (b, c))
    vec = pl.BlockSpec((1, LRU_LANES), lambda c, b: (0, c))
    mat = pl.BlockSpec((LRU_BLOCKS_PER_STEP, LANES, LANES), lambda c, b: (c, 0, 0))
    vec_shape = jax.ShapeDtypeStruct((1, D_MODEL), F32)
    mat_shape = jax.ShapeDtypeStruct((N_BLK, LANES, LANES), F32)
    return pl.pallas_call(
        body, name="branch_a_bwd",
        grid=(LRU_STEPS, batch),
        in_specs=[pl.BlockSpec((2, seq, LRU_LANES), lambda c, b: (2, b, c)), blk,
                  pl.BlockSpec((3, seq, LRU_LANES), lambda c, b: (0, b, c)), blk,
                  pl.BlockSpec(memory_space=pl.ANY)] + _lru_param_specs(0),
        out_specs=[pl.BlockSpec((2, seq, LRU_LANES), lambda c, b: (2, b, c)),
                   pl.BlockSpec((CONV_WIDTH, LRU_LANES), lambda c, b: (0, c)), vec, mat, vec, mat, vec, vec],
        out_shape=[jax.ShapeDtypeStruct((N_GROUPS, tokens, D_MODEL), _MXU_DTYPE),
                   jax.ShapeDtypeStruct((CONV_WIDTH, D_MODEL), F32), vec_shape, mat_shape, vec_shape, mat_shape,
                   vec_shape, vec_shape],
        scratch_shapes=[pltpu.VMEM((seq, LRU_LANES), F32)] * 2,
        input_output_aliases={4: 0},
        compiler_params=_params(("parallel", "arbitrary"), vmem=VMEM_LIMIT_BIG),
    )(z, hl, kept, dya, dz, conv_w, conv_b, wx, bx, wa, ba, lam)


def _chunk_masks(transposed=False):
    r = lax.broadcasted_iota(jnp.int32, (CHUNK, CHUNK), 0)
    c = lax.broadcasted_iota(jnp.int32, (CHUNK, CHUNK), 1)
    return r <= c if transposed else r >= c


def _row_blocks(seq, fn):
    block = min(256, seq)

    def trip(i, carry):
        fn(pl.ds(pl.multiple_of(i * block, block), block))
        return carry

    lax.fori_loop(0, seq // block, trip, 0)


def _hgrn_prepare(z_ref, lb_ref, f_scr, logf_scr, qh_scr, seq):
    lb = _sigmoid(lb_ref[0:1, :] - lb_ref[1:2, :])

    def block(rows):
        q = z_ref[0, rows, :]
        f = lb + (1.0 - lb) * _sigmoid(z_ref[1, rows, :])
        f_scr[rows, :] = f
        logf_scr[rows, :] = jnp.log(f)
        qh_scr[rows, :] = q * _sigmoid(q)

    _row_blocks(seq, block)
    return lb


def _cumsum_rows(x, reverse=False):
    shift = _shift_up if reverse else _shift_down
    d = 1
    while d < x.shape[0]:
        x = x + shift(x, d)
        d *= 2
    return x


def _lane_mean(x):
    return jnp.mean(x, axis=-1, keepdims=True)


def _token_contractions(lhs_scr, rhs_scr, out_ref, seq):
    rows_id = lax.broadcasted_iota(jnp.int32, (LANES, LANES), 0)

    def transposed(p):
        rows = pl.ds(pl.multiple_of(p * LANES, LANES), LANES)
        return jnp.transpose(lhs_scr[rows, :]).astype(_MXU_DTYPE), rhs_scr[rows, :]

    def contract(p, s):
        lhs_t, rhs = s
        return (_mm(lhs_t, jnp.where(rows_id < CHUNK, rhs, 0.0)), _mm(lhs_t, jnp.where(rows_id >= CHUNK, rhs, 0.0)))

    def store(p, out):
        out_ref[2 * p] = out[0]
        out_ref[2 * p + 1] = out[1]

    _independent_trips(seq // LANES, [transposed, contract], store)


def _chunk_rows(c):
    return pl.ds(pl.multiple_of(c * CHUNK, CHUNK), CHUNK)


def _chunk_terms(c, z_ref, f_scr, qh_scr, b_scr):
    rows = _chunk_rows(c)
    b = b_scr[rows, :]
    b_mid = b_scr[pl.ds(c * CHUNK + CHUNK // 2, 1), :]
    b_last = b_scr[pl.ds(c * CHUNK + CHUNK - 1, 1), :]
    qh = qh_scr[rows, :]
    k = 1.0 - f_scr[rows, :]
    v = z_ref[2, rows, :]
    e_q = jnp.exp(b - b_mid) * HG_SCALE
    e_k = jnp.exp(b_mid - b)
    e_qi = jnp.exp(b) * HG_SCALE
    e_ks = jnp.exp(b_last - b)
    decay = jnp.exp(b_last)
    return rows, qh, k, v, e_q, e_k, e_qi, e_ks, decay


def _independent_trips(n, stages, store, group=CHUNKS_IN_FLIGHT):
    stages = stages if isinstance(stages, (list, tuple)) else [stages]
    group = min(group, n)

    def trip(g, carry):
        ids = [g * group + i for i in range(group)]
        state = [stages[0](c) for c in ids]
        for stage in stages[1:]:
            state = [stage(c, s) for c, s in zip(ids, state)]
        for c, s in zip(ids, state):
            store(c, s)
        return carry

    lax.fori_loop(0, n // group, trip, 0)


def _branch_b_fwd(z, lb_logits, hg_g, batch, seq):
    tokens = batch * seq
    n_chunks = seq // CHUNK

    def body(z_ref, lb_ref, g_ref, yb_ref, st_ref, kept_ref, logf_scr, o_scr, qi_scr, ks_scr, dec_scr):
        f_scr, qh_scr, b_scr, o_kept = (kept_ref.at[k] for k in range(4))
        _hgrn_prepare(z_ref, lb_ref, f_scr, logf_scr, qh_scr, seq)
        causal = _chunk_masks()
        gain = g_ref[...]

        def cumulate(c):
            return _cumsum_rows(logf_scr[_chunk_rows(c), :])

        def store_cumulated(c, b):
            b_scr[_chunk_rows(c), :] = b

        def scores(c):
            _, qh, k, v, e_q, e_k, e_qi, e_ks, decay = _chunk_terms(c, z_ref, f_scr, qh_scr, b_scr)
            return _mm_nt(qh * e_q, k * e_k), v, qh * e_qi, k * e_ks, decay

        def within_chunk(c, s):
            att, v, q_int, k_st, decay = s
            return _mm(jnp.where(causal, att, 0.0), v), q_int, k_st, decay

        def store_within_chunk(c, out):
            rows = _chunk_rows(c)
            o_scr[rows, :], qi_scr[rows, :], ks_scr[rows, :], dec_scr[pl.ds(c, 1), :] = out

        def carry_state(c, state_t):
            update = st_ref[c]
            st_ref[c] = state_t
            return state_t * dec_scr[pl.ds(c, 1), :] + update

        def finish(c):
            rows = _chunk_rows(c)
            o = o_scr[rows, :] + _mm_nt(qi_scr[rows, :], st_ref[c])
            r = lax.rsqrt(_lane_mean(o * o) + EPS)
            gb = z_ref[3, rows, :]
            return (((o * r) * gain) * (gb * _sigmoid(gb))).astype(_MXU_DTYPE), o

        def store_finished(c, out):
            yb_ref[_chunk_rows(c), :], o_kept[_chunk_rows(c), :] = out

        _independent_trips(n_chunks, cumulate, store_cumulated)
        _independent_trips(n_chunks, [scores, within_chunk], store_within_chunk)
        _token_contractions(z_ref.at[2], ks_scr, st_ref, seq)
        lax.fori_loop(0, n_chunks, carry_state, jnp.zeros((LANES, LANES), F32))
        _independent_trips(n_chunks, finish, store_finished)

    seq_buf = pltpu.VMEM((seq, LANES), F32)
    return pl.pallas_call(
        body, name="branch_b_fwd",
        grid=(batch, N_BLK),
        in_specs=[pl.BlockSpec((4, seq, LANES), lambda b, h: (0, b, h)),
                  pl.BlockSpec((2, LANES), lambda b, h: (0, h)),
                  pl.BlockSpec((1, LANES), lambda b, h: (0, 0))],
        out_specs=[pl.BlockSpec((seq, LANES), lambda b, h: (b, h)),
                   pl.BlockSpec((None, n_chunks, LANES, LANES), lambda b, h: (b * N_BLK + h, 0, 0, 0)),
                   pl.BlockSpec((4, seq, LANES), lambda b, h: (0, b, h))],
        out_shape=[jax.ShapeDtypeStruct((tokens, D_MODEL), _MXU_DTYPE),
                   jax.ShapeDtypeStruct((batch * N_BLK, n_chunks, LANES, LANES), F32),
                   jax.ShapeDtypeStruct((4, tokens, D_MODEL), F32)],
        scratch_shapes=[seq_buf] * 4 + [pltpu.VMEM((n_chunks, LANES), F32)],
        compiler_params=_params(("parallel", "parallel")),
    )(z, lb_logits, hg_g)


def _branch_b_bwd(z, states, kept, dyb, dz, lb_logits, hg_g, batch, seq):
    n_chunks = seq // CHUNK

    def body(z_ref, st_ref, kept_ref, dyb_ref, dz_in_ref, lb_ref, g_ref, dz_ref, dlog_ref, dg_ref,
             do_scr, qi_scr, dqh_scr, df_scr, dec_scr, dgp_scr, dlb_scr, dst_scr):
        del dz_in_ref
        f_scr, qh_scr, b_scr, o_kept = (kept_ref.at[k] for k in range(4))
        first = (pl.program_id(0) == 0) & (pl.program_id(1) == 0)
        lb = _sigmoid(lb_ref[0:1, :] - lb_ref[1:2, :])
        causal = _chunk_masks()
        anti_causal = _chunk_masks(transposed=True)
        gain = g_ref[...]

        @pl.when(first)
        def _():
            dg_ref[...] = jnp.zeros_like(dg_ref)

        @pl.when(pl.program_id(1) == 0)
        def _():
            dlb_scr[...] = jnp.zeros_like(dlb_scr)

        def output_gradient(c):
            rows = _chunk_rows(c)
            b = b_scr[rows, :]
            q_int = qh_scr[rows, :] * (jnp.exp(b) * HG_SCALE)
            decay = jnp.exp(b_scr[pl.ds(c * CHUNK + CHUNK - 1, 1), :])
            o = o_kept[rows, :]
            r = lax.rsqrt(_lane_mean(o * o) + EPS)
            o_n = o * r
            gb = z_ref[3, rows, :]
            sgb = _sigmoid(gb)
            dyb_c = dyb_ref[rows, :]
            d_ong = dyb_c * (gb * sgb)
            d_gb = (dyb_c * (o_n * gain) * (sgb * (1.0 + gb * (1.0 - sgb)))).astype(_MXU_DTYPE)
            d_gain = jnp.sum(d_ong * o_n, axis=0, keepdims=True)
            d_on = d_ong * gain
            return d_gb, d_gain, r * (d_on - o_n * _lane_mean(d_on * o_n)), q_int, decay

        def store_output_gradient(c, out):
            rows = _chunk_rows(c)
            dz_ref[3, rows, :], dgp_scr[pl.ds(c, 1), :], do_scr[rows, :], qi_scr[rows, :], dec_scr[pl.ds(c, 1), :] = out

        def carry_state_gradient(cc, d_state_t):
            c = n_chunks - 1 - cc
            update = dst_scr[c]
            dst_scr[c] = d_state_t
            return d_state_t * dec_scr[pl.ds(c, 1), :] + update

        def score_gradients(c):
            rows, qh, k, v, e_q, e_k, e_qi, e_ks, decay = _chunk_terms(c, z_ref, f_scr, qh_scr, b_scr)
            state_t = st_ref[c]
            d_state_t = dst_scr[c]
            d_o = do_scr[rows, :]
            q_in, k_in, q_int, k_st = qh * e_q, k * e_k, qh * e_qi, k * e_ks
            first = (_mm_nt(k_in, q_in), _mm_nt(d_o, v), _mm_nt(v, d_o), _mm_nt(k_st, d_state_t), _mm(d_o, state_t),
                     _mm(v, d_state_t))
            d_decay = jnp.sum(state_t * d_state_t, axis=0, keepdims=True)
            return first, d_o, q_in, k_in, q_int, k_st, e_q, e_k, e_qi, e_ks, decay, d_decay

        def input_gradients(c, s):
            (att_t, d_att, d_att_t, dv_inter, dq_int, dk_st), d_o, q_in, k_in, q_int, k_st, e_q, e_k, e_qi, e_ks, decay, d_decay = s
            rows = _chunk_rows(c)
            d_v = _mm(jnp.where(anti_causal, att_t, 0.0), d_o) + dv_inter
            dq_in = _mm(jnp.where(causal, d_att, 0.0), k_in)
            dk_in = _mm(jnp.where(anti_causal, d_att_t, 0.0), q_in)
            d_k = dk_in * e_k + dk_st * e_ks
            kk = dk_st * k_st
            d_b = dq_in * q_in + dq_int * q_int - dk_in * k_in - kk
            d_b_last = jnp.sum(kk, axis=0, keepdims=True) + decay * d_decay
            d_logf = _cumsum_rows(d_b, reverse=True) + d_b_last
            return d_v.astype(_MXU_DTYPE), dq_in * e_q + dq_int * e_qi, d_logf / f_scr[rows, :] - d_k

        def store_input_gradients(c, out):
            rows = _chunk_rows(c)
            dz_ref[2, rows, :], dqh_scr[rows, :], df_scr[rows, :] = out

        def input_activations(rows):
            q = z_ref[0, rows, :]
            sq = _sigmoid(q)
            dz_ref[0, rows, :] = (dqh_scr[rows, :] * (sq * (1.0 + q * (1.0 - sq)))).astype(_MXU_DTYPE)
            sg = _sigmoid(z_ref[1, rows, :])
            d_f = df_scr[rows, :]
            dz_ref[1, rows, :] = (d_f * (1.0 - lb) * sg * (1.0 - sg)).astype(_MXU_DTYPE)
            dlb_scr[...] += jnp.sum(d_f * (1.0 - sg), axis=0, keepdims=True)

        _independent_trips(n_chunks, output_gradient, store_output_gradient)
        _token_contractions(do_scr, qi_scr, dst_scr, seq)
        lax.fori_loop(0, n_chunks, carry_state_gradient, jnp.zeros((LANES, LANES), F32))
        _independent_trips(n_chunks, [score_gradients, input_gradients], store_input_gradients)
        dg_ref[...] += jnp.sum(dgp_scr[...], axis=0, keepdims=True)
        _row_blocks(seq, input_activations)
        d_l0 = dlb_scr[...] * lb * (1.0 - lb)
        dlog_ref[0:1, :] = d_l0
        dlog_ref[1:2, :] = -d_l0

    tokens = batch * seq
    seq_buf = pltpu.VMEM((seq, LANES), F32)
    chunk_rows = pltpu.VMEM((n_chunks, LANES), F32)
    return pl.pallas_call(
        body, name="branch_b_bwd",
        grid=(N_BLK, batch),
        in_specs=[pl.BlockSpec((4, seq, LANES), lambda h, b: (0, b, h)),
                  pl.BlockSpec((None, n_chunks, LANES, LANES), lambda h, b: (b * N_BLK + h, 0, 0, 0)),
                  pl.BlockSpec((4, seq, LANES), lambda h, b: (0, b, h)),
                  pl.BlockSpec((seq, LANES), lambda h, b: (b, h)),
                  pl.BlockSpec(memory_space=pl.ANY),
                  pl.BlockSpec((2, LANES), lambda h, b: (0, h)),
                  pl.BlockSpec((1, LANES), lambda h, b: (0, 0))],
        out_specs=[pl.BlockSpec((4, seq, LANES), lambda h, b: (0, b, h)),
                   pl.BlockSpec((2, LANES), lambda h, b: (0, h)),
                   pl.BlockSpec((1, LANES), lambda h, b: (0, 0))],
        out_shape=[jax.ShapeDtypeStruct((N_GROUPS, tokens, D_MODEL), _MXU_DTYPE),
                   jax.ShapeDtypeStruct((2, D_MODEL), F32),
                   jax.ShapeDtypeStruct((1, LANES), F32)],
        scratch_shapes=[seq_buf] * 4 + [chunk_rows, chunk_rows, pltpu.VMEM((1, LANES), F32),
                                        pltpu.VMEM((n_chunks, LANES, LANES), F32)],
        input_output_aliases={4: 0},
        compiler_params=_params(("arbitrary", "arbitrary")),
    )(z, states, kept, dyb, dz, lb_logits, hg_g)


def _merge_tail(ya, yb, z, x2d, tgt2d, b_merge, final_g, pa, pb, wo):
    tokens, d = x2d.shape
    tm = min(256, tokens)
    n_tiles = tokens // tm

    def body(ya_ref, yb_ref, z_ref, x_ref, t_ref, bm_ref, fg_ref, pa_hbm, pb_hbm, wo_hbm,
             dya_ref, dyb_ref, dx2_ref, dz_ref, loss_ref, dfg_ref, dbm_ref, dpa_hbm, dpb_hbm, dwo_hbm,
             pa_s, pb_s, wo_s, dpa_s, dpb_s, dwo_s):
        i = pl.program_id(0)

        @pl.when(i == 0)
        def _():
            pltpu.sync_copy(pa_hbm, pa_s)
            pltpu.sync_copy(pb_hbm, pb_s)
            pltpu.sync_copy(wo_hbm, wo_s)
            dpa_s[...] = jnp.zeros_like(dpa_s)
            dpb_s[...] = jnp.zeros_like(dpb_s)
            dwo_s[...] = jnp.zeros_like(dwo_s)
            loss_ref[...] = jnp.zeros_like(loss_ref)
            dfg_ref[...] = jnp.zeros_like(dfg_ref)
            dbm_ref[...] = jnp.zeros_like(dbm_ref)

        ya_t = ya_ref[...]
        yb_t = yb_ref[...]
        out_a = _mm(ya_t, pa_s[...])
        out_b = _mm(yb_t, pb_s[...])
        g_a = _sigmoid(z_ref[0] + bm_ref[:, :d])
        g_b = _sigmoid(z_ref[1] + bm_ref[:, d:])
        mixed = g_a * out_a + g_b * out_b
        x2 = x_ref[...] + _mm(mixed, wo_s[...])
        r = lax.rsqrt(jnp.mean(x2 * x2, axis=-1, keepdims=True) + EPS)
        xn = x2 * r
        fg = fg_ref[...]
        diff = xn * fg - t_ref[...]
        loss_ref[...] += jnp.sum(diff * diff) * (0.5 / d)
        dy = diff * (1.0 / d)
        dfg_ref[...] += jnp.sum(dy * xn, axis=0, keepdims=True)
        dxn = dy * fg
        dx2 = r * (dxn - xn * jnp.mean(dxn * xn, axis=-1, keepdims=True))
        dx2_ref[...] = dx2
        dmixed = _mm_nt(dx2, wo_s[...])
        dwo_s[...] += _mm_tn(mixed, dx2)
        dgm_a = dmixed * out_a * g_a * (1.0 - g_a)
        dgm_b = dmixed * out_b * g_b * (1.0 - g_b)
        dz_ref[0] = dgm_a.astype(_MXU_DTYPE)
        dz_ref[1] = dgm_b.astype(_MXU_DTYPE)
        dbm_ref[:, :d] += jnp.sum(dgm_a, axis=0, keepdims=True)
        dbm_ref[:, d:] += jnp.sum(dgm_b, axis=0, keepdims=True)
        dout_a = dmixed * g_a
        dout_b = dmixed * g_b
        dpa_s[...] += _mm_tn(ya_t, dout_a)
        dpb_s[...] += _mm_tn(yb_t, dout_b)
        dya_ref[...] = _mm_nt(dout_a, pa_s[...])
        dyb_ref[...] = _mm_nt(dout_b, pb_s[...])

        @pl.when(i == n_tiles - 1)
        def _():
            pltpu.sync_copy(dpa_s, dpa_hbm)
            pltpu.sync_copy(dpb_s, dpb_hbm)
            pltpu.sync_copy(dwo_s, dwo_hbm)

    tile = pl.BlockSpec((tm, d), lambda i: (i, 0))
    gm = pl.BlockSpec((2, tm, d), lambda i: (3, i, 0))
    row = lambda n: pl.BlockSpec((1, n), lambda i: (0, 0))
    hbm = pl.BlockSpec(memory_space=pl.ANY)
    act = jax.ShapeDtypeStruct((tokens, d), F32)
    mat = jax.ShapeDtypeStruct((d, d), F32)
    return pl.pallas_call(
        body, name="merge_tail",
        grid=(n_tiles,),
        in_specs=[tile, tile, gm, tile, tile, row(2 * d), row(d), hbm, hbm, hbm],
        out_specs=[tile, tile, tile, gm, row(LANES), row(d), row(2 * d), hbm, hbm, hbm],
        out_shape=[act, act, act, jax.ShapeDtypeStruct((N_GROUPS, tokens, d), _MXU_DTYPE),
                   jax.ShapeDtypeStruct((1, LANES), F32), jax.ShapeDtypeStruct((1, d), F32),
                   jax.ShapeDtypeStruct((1, 2 * d), F32), mat, mat, mat],
        scratch_shapes=[pltpu.VMEM((d, d), _MXU_DTYPE)] * 3 + [pltpu.VMEM((d, d), F32)] * 3,
        compiler_params=_params(("arbitrary",)),
    )(ya, yb, z, x2d, tgt2d, b_merge, final_g, pa, pb, wo)


def _inproj_dw(h_t, dz):
    d, tokens = h_t.shape
    tm = min(2048, tokens)

    def body(h_ref, dz_ref, dw_ref):
        part = _mm(h_ref[...], dz_ref[...])

        @pl.when(pl.program_id(1) == 0)
        def _():
            dw_ref[...] = part

        @pl.when(pl.program_id(1) != 0)
        def _():
            dw_ref[...] += part

    def out_index(s, i):
        g = _group_of_slot(s)
        return (g // 2, 0, g % 2)

    return pl.pallas_call(
        body, name="inproj_dw",
        grid=(N_GROUPS, tokens // tm),
        in_specs=[pl.BlockSpec((d, tm), lambda s, i: (0, i)),
                  pl.BlockSpec((None, tm, D_MODEL), lambda s, i: (s, i, 0))],
        out_specs=pl.BlockSpec((None, d, D_MODEL), out_index),
        out_shape=jax.ShapeDtypeStruct((N_SHARDS, d, 2 * D_MODEL), F32),
        compiler_params=_params(("parallel", "arbitrary")),
    )(h_t, dz)


def _inproj_dw_exchange(h_t, dz, scatter):
    d, tokens = h_t.shape
    tm = min(2048, tokens)
    n_i = tokens // tm
    half = d // 2

    def body(h_ref, dz_ref, *rest):
        n_in, n_out = scatter.n_in, scatter.n_out
        dw_hbm, land_hbm = rest[n_in:n_in + 2]
        acc, local_sems, send_sems, recv_sems = rest[n_in + 2 + n_out:n_in + 6 + n_out]
        carried = scatter.copies(rest[:n_in], rest[n_in + 2:n_in + 2 + n_out], rest[n_in + 6 + n_out:])
        s, i = pl.program_id(0), pl.program_id(1)
        x, y, c, _ = _mesh_position()

        @pl.when((s == 0) & (i == 0))
        def _():
            for cp in carried:
                cp.start()

        part = _mm(h_ref[...], dz_ref[...])
        buf = acc.at[s % 2]

        @pl.when(i == 0)
        def _():
            buf[...] = part

        @pl.when(i != 0)
        def _():
            buf[...] += part

        def copies(k):
            g = _SLOT_TO_GROUP[k]
            cols = pl.ds((g % 2) * D_MODEL, D_MODEL)
            src = acc.at[k % 2]
            mine = pltpu.make_async_copy(src, dw_hbm.at[g // 2, :, cols], local_sems.at[k % 2])
            theirs = _remote(src.at[pl.ds((1 - c) * half, half), :], land_hbm.at[g // 2, :, cols],
                             send_sems, recv_sems, k, (x, y, 1 - c))
            return mine, theirs

        for k in range(N_GROUPS):
            @pl.when((s == k) & (i == n_i - 1))
            def _(k=k):
                if k > 0:
                    mine, theirs = copies(k - 1)
                    mine.wait()
                    theirs.wait_send()
                mine, theirs = copies(k)
                mine.start()
                theirs.start()
                if k == N_GROUPS - 1:
                    mine.wait()
                    theirs.wait_send()
                    for kk in range(N_GROUPS):
                        copies(kk)[1].wait_recv()
                    for cp in carried:
                        cp.wait()

    hbm = pl.BlockSpec(memory_space=pl.ANY)
    more = scatter.plumbing(first_operand=2, first_output=2)
    return pl.pallas_call(
        body, name="inproj_dw_exchange",
        grid=(N_GROUPS, n_i),
        in_specs=[pl.BlockSpec((d, tm), lambda s, i: (0, i)),
                  pl.BlockSpec((None, tm, D_MODEL), lambda s, i: (s, i, 0))] + more[1],
        out_specs=[hbm, hbm] + more[2],
        out_shape=[jax.ShapeDtypeStruct((N_SHARDS, d, 2 * D_MODEL), F32),
                   jax.ShapeDtypeStruct((N_SHARDS, half, 2 * D_MODEL), F32)] + more[3],
        scratch_shapes=[pltpu.VMEM((2, d, D_MODEL), F32), pltpu.SemaphoreType.DMA((2,)),
                        pltpu.SemaphoreType.DMA((N_GROUPS,)), pltpu.SemaphoreType.DMA((N_GROUPS,))] + more[4],
        input_output_aliases=more[5],
        compiler_params=_params(("arbitrary", "arbitrary")),
    )(h_t, dz, *more[0])


def _inproj_dx(dz, w_all, x2d, dx2, norm_g, scatter=None):
    tokens, d = x2d.shape
    tm = min(256, tokens)
    n_tiles = tokens // tm

    def body(dz_ref, w_hbm, x_ref, dx2_ref, g_ref, *rest):
        if scatter:
            n_in, n_out = scatter.n_in, scatter.n_out
            dx_ref, dg_ref = rest[n_in:n_in + 2]
            w_res = rest[n_in + 2 + n_out]
            copies = scatter.copies(rest[:n_in], rest[n_in + 2:n_in + 2 + n_out], rest[n_in + 3 + n_out:])
        else:
            dx_ref, dg_ref, w_res = rest
            copies = []
        i = pl.program_id(0)

        @pl.when(i == 0)
        def _():
            for cp in copies:
                cp.start()
            for slot, g in enumerate(_SLOT_TO_GROUP):
                pltpu.sync_copy(w_hbm.at[g // 2, :, pl.ds((g % 2) * D_MODEL, D_MODEL)],
                                w_res.at[:, pl.ds(slot * D_MODEL, D_MODEL)])
            dg_ref[...] = jnp.zeros_like(dg_ref)

        dz_all = jnp.concatenate([dz_ref[s] for s in range(N_GROUPS)], axis=1)
        dh = jnp.transpose(_mm_nt(w_res[...], dz_all))
        x = x_ref[...]
        r = lax.rsqrt(jnp.mean(x * x, axis=-1, keepdims=True) + EPS)
        xn = x * r
        dg_ref[...] += jnp.sum(dh * xn, axis=0, keepdims=True)
        dxn = dh * g_ref[...]
        dx_ref[...] = r * (dxn - xn * jnp.mean(dxn * xn, axis=-1, keepdims=True)) + dx2_ref[...]

        @pl.when(i == n_tiles - 1)
        def _():
            for cp in copies:
                cp.wait()

    tile = pl.BlockSpec((tm, d), lambda i: (i, 0))
    hbm = pl.BlockSpec(memory_space=pl.ANY)
    in_specs = [pl.BlockSpec((N_GROUPS, tm, D_MODEL), lambda i: (0, i, 0)), hbm, tile, tile,
                pl.BlockSpec((1, d), lambda i: (0, 0))]
    out_specs = [tile, pl.BlockSpec((1, d), lambda i: (0, 0))]
    out_shape = [jax.ShapeDtypeStruct((tokens, d), F32), jax.ShapeDtypeStruct((1, d), F32)]
    scratch = [pltpu.VMEM((d, N_GROUPS * D_MODEL), _MXU_DTYPE)]
    operands, aliases = [dz, w_all, x2d, dx2, norm_g], {}
    if scatter:
        more = scatter.plumbing(first_operand=len(operands), first_output=len(out_shape))
        operands, in_specs, out_specs = operands + more[0], in_specs + more[1], out_specs + more[2]
        out_shape, scratch, aliases = out_shape + more[3], scratch + more[4], more[5]
    return pl.pallas_call(
        body, name="inproj_dx", grid=(n_tiles,), in_specs=in_specs, out_specs=out_specs, out_shape=out_shape,
        scratch_shapes=scratch, input_output_aliases=aliases,
        compiler_params=_params(("arbitrary",)),
    )(*operands)


def _row_tile(rows, cols, itemsize=4, budget=2 * 1024 * 1024):
    tr = rows
    while tr * cols * itemsize > budget and tr % 16 == 0:
        tr //= 2
    return tr


def _cast_into_slot(a, chip, dtype, name):
    rows, cols = a.shape
    tr = _row_tile(rows, cols)

    def body(chip_ref, a_ref, o_ref):
        del chip_ref
        o_ref[...] = a_ref[...].astype(dtype)

    grid_spec = pltpu.PrefetchScalarGridSpec(
        num_scalar_prefetch=1, grid=(rows // tr,),
        in_specs=[pl.BlockSpec((tr, cols), lambda i, chip_ref: (i, 0))],
        out_specs=pl.BlockSpec((None, tr, cols), lambda i, chip_ref: (chip_ref[0], i, 0)))
    return pl.pallas_call(body, name=name, grid_spec=grid_spec,
                          out_shape=jax.ShapeDtypeStruct((N_SHARDS, rows, cols), dtype),
                          compiler_params=_params(("arbitrary",)))(chip, a)


def _sum_slots(stack, name):
    n, rows, cols = stack.shape
    tr = _row_tile(rows, cols * n)

    def body(s_ref, o_ref):
        total = s_ref[0].astype(F32)
        for k in range(1, n):
            total = total + s_ref[k].astype(F32)
        o_ref[...] = total

    return pl.pallas_call(body, name=name, grid=(rows // tr,),
                          in_specs=[pl.BlockSpec((n, tr, cols), lambda i: (0, i, 0))],
                          out_specs=pl.BlockSpec((tr, cols), lambda i: (i, 0)),
                          out_shape=jax.ShapeDtypeStruct((rows, cols), F32),
                          compiler_params=_params(("parallel",)))(stack)


def _add_half(full, landed, place, name):
    n, rows, cols = full.shape
    half = rows // 2
    tr = _row_tile(half, cols)
    nb = half // tr

    def body(place_ref, a_ref, b_ref, o_ref, own_ref):
        total = (a_ref[...] + b_ref[...]).astype(_MXU_DTYPE)
        o_ref[...] = total

        @pl.when(pl.program_id(1) == place_ref[1])
        def _():
            own_ref[...] = total

    grid_spec = pltpu.PrefetchScalarGridSpec(
        num_scalar_prefetch=1, grid=(nb, n),
        in_specs=[pl.BlockSpec((None, tr, cols), lambda i, j, place_ref: (j, place_ref[0] * nb + i, 0)),
                  pl.BlockSpec((None, tr, cols), lambda i, j, place_ref: (j, i, 0))],
        out_specs=[pl.BlockSpec((None, tr, cols), lambda i, j, place_ref: (j, i, 0)),
                   pl.BlockSpec((None, tr, cols), lambda i, j, place_ref: (place_ref[1], i, 0))])
    shape = jax.ShapeDtypeStruct((n, half, cols), _MXU_DTYPE)
    return pl.pallas_call(body, name=name, grid_spec=grid_spec, out_shape=[shape, shape],
                          compiler_params=_params(("parallel", "arbitrary")))(place, full, landed)


def _adamw_update(w, grad, m, v):
    c1 = 1.0 - ADAM_B1 ** ADAM_STEP
    c2 = 1.0 - ADAM_B2 ** ADAM_STEP
    nm = ADAM_B1 * m + (1.0 - ADAM_B1) * grad
    nv = ADAM_B2 * v + (1.0 - ADAM_B2) * (grad * grad)
    return (-ADAM_LR) * ((nm / c1) / (jnp.sqrt(nv / c2) + ADAM_EPS) + ADAM_WD * w), nm, nv


def _adamw(w, g, m, v, name):
    rows, cols = w.shape
    tr = _row_tile(rows, cols, budget=1024 * 1024)

    def body(w_ref, g_ref, m_ref, v_ref, d_ref, nm_ref, nv_ref):
        d_ref[...], nm_ref[...], nv_ref[...] = _adamw_update(w_ref[...], g_ref[...], m_ref[...], v_ref[...])

    spec = pl.BlockSpec((tr, cols), lambda i: (i, 0))
    shape = jax.ShapeDtypeStruct((rows, cols), F32)
    return pl.pallas_call(body, name=name, grid=(rows // tr,), in_specs=[spec] * 4, out_specs=[spec] * 3,
                          out_shape=[shape] * 3, compiler_params=_params(("parallel",)))(w, g, m, v)


def _adamw_halves(w, g_mine, g_sibling, m, v, core, name):
    rows, cols = w.shape
    half = rows // 2
    tr = _row_tile(half, cols, budget=1024 * 1024)
    nb = half // tr

    def body(core_ref, w_ref, gm_ref, gs_ref, m_ref, v_ref, g_ref, d_ref, nm_ref, nv_ref):
        mine = pl.program_id(0) // nb == core_ref[0]
        grad = jnp.where(mine, gm_ref[...], gs_ref[...])
        g_ref[...] = grad
        d_ref[...], nm_ref[...], nv_ref[...] = _adamw_update(w_ref[...], grad, m_ref[...], v_ref[...])

    spec = pl.BlockSpec((tr, cols), lambda i, core_ref: (i, 0))
    half_spec = pl.BlockSpec((tr, cols), lambda i, core_ref: (i % nb, 0))
    grid_spec = pltpu.PrefetchScalarGridSpec(num_scalar_prefetch=1, grid=(rows // tr,),
                                             in_specs=[spec, half_spec, half_spec, spec, spec], out_specs=[spec] * 4)
    shape = jax.ShapeDtypeStruct((rows, cols), F32)
    return pl.pallas_call(body, name=name, grid_spec=grid_spec, out_shape=[shape] * 4,
                          compiler_params=_params(("parallel",)))(core, w, g_mine, g_sibling, m, v)


def _local_step(x, loss_target, w_all, pa, pb, wo, conv_w, b_merge, conv_b, rg_wx, rg_bx, rg_wa, rg_ba,
                rg_lambda, hg_lb_logits, hg_norm_g, norm_g, final_norm_g, gather=None, reduction=None):
    batch, seq, d = x.shape
    x2d = x.reshape(batch * seq, d)
    tgt2d = loss_target.reshape(batch * seq, d)
    if gather is None:
        z, h_t = _inproj_fwd(x2d, norm_g, w_all)
    else:
        z, h_t, (w_all, pa, pb, wo), cw_all = _inproj_fwd_gather(x2d, norm_g, *gather)
        pa, pb, wo = (t.reshape(d, d) for t in (pa, pb, wo))
        conv_w = jnp.transpose(cw_all, (1, 0, 2)).reshape(CONV_WIDTH, d)
    lru = (conv_w, conv_b, rg_wx, rg_bx, rg_wa, rg_ba, rg_lambda)
    ya, hl, kept = _branch_a_fwd(z, *lru, batch, seq)
    yb, states, kept_b = _branch_b_fwd(z, hg_lb_logits, hg_norm_g, batch, seq)
    dya, dyb, dx2, dz, loss, d_final_g, d_b_merge, d_pa, d_pb, d_wo = _merge_tail(
        ya, yb, z, x2d, tgt2d, b_merge, final_norm_g, pa, pb, wo)
    dz, d_lb_logits, d_hg_g = _branch_b_bwd(z, states, kept_b, dyb, dz, hg_lb_logits, hg_norm_g, batch, seq)
    dz, d_conv_w, d_conv_b, d_wx, d_bx, d_wa, d_ba, d_lam = _branch_a_bwd(z, hl, kept, dya, dz, *lru, batch, seq)
    small = dict(b_merge=d_b_merge, conv_w=d_conv_w, conv_b=d_conv_b, rg_wx=d_wx, rg_bx=d_bx, rg_wa=d_wa,
                 rg_ba=d_ba, rg_lambda=d_lam, hg_lb_logits=d_lb_logits, hg_norm_g=d_hg_g,
                 norm_g=jnp.zeros((1, d), F32), final_norm_g=d_final_g)
    if reduction is None:
        big = (_inproj_dw(h_t, dz), d_pa, d_pb, d_wo)
        grad_x, small["norm_g"] = _inproj_dx(dz, w_all, x2d, dx2, norm_g)
        return loss[0, 0], grad_x.reshape(batch, seq, d), big, small
    first, second = reduction
    d_w_in, landed_w_in, *scattered_first = _inproj_dw_exchange(h_t, dz, first((d_pa, d_pb, d_wo), small))
    grad_x, d_norm_g, *scattered_second = _inproj_dx(dz, w_all, x2d, dx2, norm_g, scatter=second(d_w_in, landed_w_in))
    return loss[0, 0], grad_x.reshape(batch, seq, d), d_norm_g, (scattered_first, scattered_second)


_SMALL_ORDER = ("b_merge", "conv_w", "conv_b", "rg_wx", "rg_bx", "rg_wa", "rg_ba", "rg_lambda", "hg_lb_logits",
                "hg_norm_g", "norm_g", "final_norm_g")
N_DEV = 8
PIECE_ROWS = 272


def _pack_small(tree):
    flat = jnp.concatenate([tree[k].reshape(-1) for k in _SMALL_ORDER])
    flat = jnp.pad(flat, (0, N_DEV * PIECE_ROWS * LANES - flat.shape[0]))
    return flat.reshape(N_DEV * PIECE_ROWS, LANES)


def _unpack_small(packed, like):
    flat = packed.reshape(-1)
    out, pos = {}, 0
    for k in _SMALL_ORDER:
        n = like[k].size
        out[k] = flat[pos:pos + n].reshape(like[k].shape)
        pos += n
    return out


def _mesh_position():
    x, y, c = lax.axis_index("x"), lax.axis_index("y"), lax.axis_index("c")
    other_chips = [(1 - x, y), (x, 1 - y), (1 - x, 1 - y)]
    return x, y, c, other_chips


def _other_devices(x, y, c):
    flips = [(fx, fy, fc) for fx in (0, 1) for fy in (0, 1) for fc in (0, 1) if (fx, fy, fc) != (0, 0, 0)]
    return [(jnp.where(fx, 1 - x, x), jnp.where(fy, 1 - y, y), jnp.where(fc, 1 - c, c)) for fx, fy, fc in flips]


def _remote(src, dst, send_sems, recv_sems, k, device):
    return pltpu.make_async_remote_copy(src_ref=src, dst_ref=dst, send_sem=send_sems.at[k], recv_sem=recv_sems.at[k],
                                        device_id=device, device_id_type=MESH)


def _exchange_halves(bigs, small):
    n_big = len(bigs)
    n_sem = n_big + N_DEV - 1

    def body(*refs):
        srcs, small_src = refs[:n_big], refs[n_big]
        outs, small_out = refs[n_big + 1:2 * n_big + 1], refs[2 * n_big + 1]
        send_sems, recv_sems, local_sem = refs[2 * n_big + 2:]
        x, y, c, _ = _mesh_position()
        me, sibling = 4 * x + 2 * y + c, (x, y, 1 - c)
        mine = pltpu.make_async_copy(small_src.at[pl.ds(me * PIECE_ROWS, PIECE_ROWS), :], small_out.at[me], local_sem)
        mine.start()
        copies = []
        for a in range(n_big):
            hs = srcs[a].shape[1] // 2
            copies.append(_remote(srcs[a].at[:, pl.ds((1 - c) * hs, hs), :], outs[a], send_sems, recv_sems, a, sibling))
        for k, (px, py, pc) in enumerate(_other_devices(x, y, c)):
            piece = small_src.at[pl.ds((4 * px + 2 * py + pc) * PIECE_ROWS, PIECE_ROWS), :]
            copies.append(_remote(piece, small_out.at[me], send_sems, recv_sems, n_big + k, (px, py, pc)))
        for cp in copies:
            cp.start()
        for cp in copies:
            cp.wait()
        mine.wait()

    hbm = pl.BlockSpec(memory_space=pl.ANY)
    out_shape = [jax.ShapeDtypeStruct((g.shape[0], g.shape[1] // 2, g.shape[2]), F32) for g in bigs]
    out_shape.append(jax.ShapeDtypeStruct((N_DEV, PIECE_ROWS, LANES), F32))
    return pl.pallas_call(
        body, name="exchange_halves",
        in_specs=[hbm] * (n_big + 1), out_specs=[hbm] * (n_big + 1), out_shape=out_shape,
        scratch_shapes=[pltpu.SemaphoreType.DMA((n_sem,)), pltpu.SemaphoreType.DMA((n_sem,)), pltpu.SemaphoreType.DMA],
    )(*bigs, small)


class _Scatter:
    def __init__(self, bigs, by_chip, small=None):
        self.bigs, self.by_chip, self.small = list(bigs), list(by_chip), small
        self.n_big = len(self.bigs)
        self.n_in = 2 * self.n_big + (small is not None)
        self.n_out = self.n_big + (small is not None)
        self.n_scratch = 2 + (small is not None)

    def plumbing(self, first_operand, first_output):
        hbm = pl.BlockSpec(memory_space=pl.ANY)
        n_sem = 3 * self.n_big + (N_DEV - 1 if self.small is not None else 0)
        operands = self.bigs + self.by_chip + ([self.small] if self.small is not None else [])
        out_shapes = [jax.ShapeDtypeStruct(g.shape, g.dtype) for g in self.by_chip]
        scratch = [pltpu.SemaphoreType.DMA((n_sem,)), pltpu.SemaphoreType.DMA((n_sem,))]
        if self.small is not None:
            out_shapes.append(jax.ShapeDtypeStruct((N_DEV, PIECE_ROWS, LANES), F32))
            scratch.append(pltpu.SemaphoreType.DMA)
        aliases = {first_operand + self.n_big + a: first_output + a for a in range(self.n_big)}
        return operands, [hbm] * self.n_in, [hbm] * self.n_out, out_shapes, scratch, aliases

    def copies(self, in_refs, out_refs, scratch_refs):
        srcs, outs = in_refs[:self.n_big], out_refs[:self.n_big]
        send_sems, recv_sems = scratch_refs[:2]
        x, y, c, chips = _mesh_position()
        chip, me = 2 * x + y, 4 * x + 2 * y + c
        copies = []
        for a in range(self.n_big):
            for j, (cx, cy) in enumerate(chips):
                copies.append(_remote(srcs[a].at[2 * cx + cy], outs[a].at[chip], send_sems, recv_sems, 3 * a + j,
                                      (cx, cy, c)))
        if self.small is not None:
            small_src, small_out = in_refs[2 * self.n_big], out_refs[self.n_big]
            copies.append(pltpu.make_async_copy(small_src, small_out.at[me], scratch_refs[2]))
            for k, peer in enumerate(_other_devices(x, y, c)):
                copies.append(_remote(small_src, small_out.at[me], send_sems, recv_sems, 3 * self.n_big + k, peer))
        return copies


def _swap_halves(halves, vec):
    n_big = len(halves)

    def body(*refs):
        srcs, vec_src = refs[:n_big], refs[n_big]
        outs, vec_out = refs[n_big + 1:2 * n_big + 1], refs[2 * n_big + 1]
        send_sems, recv_sems, local_sem = refs[2 * n_big + 2:]
        x, y, c, _ = _mesh_position()
        me = 4 * x + 2 * y + c
        copies = [pltpu.make_async_copy(vec_src, vec_out.at[me], local_sem)]
        copies += [_remote(srcs[a], outs[a], send_sems, recv_sems, a, (x, y, 1 - c)) for a in range(n_big)]
        copies += [_remote(vec_src, vec_out.at[me], send_sems, recv_sems, n_big + k, peer)
                   for k, peer in enumerate(_other_devices(x, y, c))]
        for cp in copies:
            cp.start()
        for cp in copies:
            cp.wait()

    hbm = pl.BlockSpec(memory_space=pl.ANY)
    n_sem = n_big + N_DEV - 1
    return pl.pallas_call(
        body, name="swap_halves",
        in_specs=[hbm] * (n_big + 1), out_specs=[hbm] * (n_big + 1),
        out_shape=[jax.ShapeDtypeStruct(h.shape, F32) for h in halves] + [jax.ShapeDtypeStruct((N_DEV,) + vec.shape, F32)],
        scratch_shapes=[pltpu.SemaphoreType.DMA((n_sem,)), pltpu.SemaphoreType.DMA((n_sem,)), pltpu.SemaphoreType.DMA],
    )(*halves, vec)


def kernel(x, w_in, b_merge, conv_w, conv_b, rg_wx, rg_bx, rg_wa, rg_ba, rg_lambda, hg_lb_logits, hg_norm_g, proj_a, proj_b, w_out, norm_g, final_norm_g, loss_target, m_w_in, m_b_merge, m_conv_w, m_conv_b, m_rg_wx, m_rg_bx, m_rg_wa, m_rg_ba, m_rg_lambda, m_hg_lb_logits, m_hg_norm_g, m_proj_a, m_proj_b, m_w_out, m_norm_g, m_final_norm_g, v_w_in, v_b_merge, v_conv_w, v_conv_b, v_rg_wx, v_rg_bx, v_rg_wa, v_rg_ba, v_rg_lambda, v_hg_lb_logits, v_hg_norm_g, v_proj_a, v_proj_b, v_w_out, v_norm_g, v_final_norm_g):
    d = D_MODEL
    weights = dict(w_in=w_in, b_merge=b_merge, conv_w=conv_w, conv_b=conv_b, rg_wx=rg_wx, rg_bx=rg_bx, rg_wa=rg_wa,
                   rg_ba=rg_ba, rg_lambda=rg_lambda, hg_lb_logits=hg_lb_logits, hg_norm_g=hg_norm_g, proj_a=proj_a,
                   proj_b=proj_b, w_out=w_out, norm_g=norm_g, final_norm_g=final_norm_g)
    m = dict(w_in=m_w_in, b_merge=m_b_merge, conv_w=m_conv_w, conv_b=m_conv_b, rg_wx=m_rg_wx, rg_bx=m_rg_bx,
             rg_wa=m_rg_wa, rg_ba=m_rg_ba, rg_lambda=m_rg_lambda, hg_lb_logits=m_hg_lb_logits, hg_norm_g=m_hg_norm_g,
             proj_a=m_proj_a, proj_b=m_proj_b, w_out=m_w_out, norm_g=m_norm_g, final_norm_g=m_final_norm_g)
    v = dict(w_in=v_w_in, b_merge=v_b_merge, conv_w=v_conv_w, conv_b=v_conv_b, rg_wx=v_rg_wx, rg_bx=v_rg_bx,
             rg_wa=v_rg_wa, rg_ba=v_rg_ba, rg_lambda=v_rg_lambda, hg_lb_logits=v_hg_lb_logits, hg_norm_g=v_hg_norm_g,
             proj_a=v_proj_a, proj_b=v_proj_b, w_out=v_w_out, norm_g=v_norm_g, final_norm_g=v_final_norm_g)
    big_names = ("w_in", "proj_a", "proj_b", "w_out")

    core = lax.axis_index("c").astype(jnp.int32).reshape(1)
    chip = (2 * lax.axis_index("x") + lax.axis_index("y")).astype(jnp.int32)

    slotted = [_cast_into_slot(weights[k][0], chip.reshape(1), _MXU_DTYPE, f"cast_{k}") for k in big_names]
    conv_slotted = _cast_into_slot(conv_w[0], chip.reshape(1), F32, "slot_conv_w")

    small_shapes = {}

    place = jnp.concatenate([core, chip.reshape(1)])

    def reduce_proj_and_small(proj_grads, small_grads):
        small_shapes.update({k: t.shape for k, t in small_grads.items()})
        bigs = [g.reshape(N_SHARDS, d // N_SHARDS, d) for g in proj_grads]
        *landed, small_landed = _exchange_halves(bigs, _pack_small(small_grads))
        sums = [_add_half(g, l, place, f"add_half_{1 + a}") for a, (g, l) in enumerate(zip(bigs, landed))]
        return _Scatter([s[0] for s in sums], [s[1] for s in sums], _sum_slots(small_landed, "sum_small"))

    def reduce_w_in(d_w_in, landed):
        partial, own_slot = _add_half(d_w_in, landed, place, "add_half_0")
        return _Scatter([partial], [own_slot])

    loss_part, grad_x, d_norm_g, ((*by_chip_proj, small_all), by_chip_w_in) = _local_step(
        x, loss_target, None, None, None, None, None,
        b_merge, conv_b, rg_wx[0], rg_bx.reshape(1, d), rg_wa[0], rg_ba.reshape(1, d), rg_lambda, hg_lb_logits,
        hg_norm_g, norm_g, final_norm_g.reshape(1, d), gather=(slotted, conv_slotted, chip.reshape(1)),
        reduction=(reduce_proj_and_small, reduce_w_in))
    mine = [_sum_slots(s, f"sum_chips_{a}") for a, s in enumerate(by_chip_w_in + by_chip_proj)]
    late = jnp.concatenate([d_norm_g.reshape(SUBLANES, LANES), jnp.full((SUBLANES, LANES), loss_part, F32)])
    *theirs, late_parts = _swap_halves(mine, late)
    late_sum = _sum_slots(late_parts, "sum_late")
    loss = late_sum[SUBLANES, 0]
    small_red = _unpack_small(small_all, {k: jax.ShapeDtypeStruct(s, F32) for k, s in small_shapes.items()})
    small_red["norm_g"] = late_sum[:SUBLANES].reshape(1, d)

    grads, delta, new_m, new_v = {}, {}, {}, {}
    for k, g_mine, g_theirs in zip(big_names, mine, theirs):
        out = _adamw_halves(weights[k][0], g_mine, g_theirs, m[k][0], v[k][0], core, f"adamw_{k}")
        grads[k], delta[k], new_m[k], new_v[k] = (t.reshape(weights[k].shape) for t in out)
    cols = d // N_SHARDS
    g_conv = lax.dynamic_slice(small_red["conv_w"], (0, chip * cols), (CONV_WIDTH, cols))
    grads["conv_w"] = g_conv.reshape(conv_w.shape)
    dl, nm, nv = _adamw(conv_w[0], g_conv, m_conv_w[0], v_conv_w[0], "adamw_conv_w")
    delta["conv_w"], new_m["conv_w"], new_v["conv_w"] = (t.reshape(conv_w.shape) for t in (dl, nm, nv))
    rest = [k for k in _SMALL_ORDER if k != "conv_w"]
    like = {k: (weights[k] if k != "conv_w" else jnp.zeros((CONV_WIDTH, d), F32)) for k in _SMALL_ORDER}
    packs = [_pack_small({k: (t[k] if k != "conv_w" else like[k]) for k in _SMALL_ORDER}) for t in (weights, m, v)]
    g_pack = _pack_small({k: small_red[k].reshape(like[k].shape) for k in _SMALL_ORDER})
    outs = [_unpack_small(p, like) for p in _adamw(packs[0], g_pack, packs[1], packs[2], "adamw_small")]
    for k in rest:
        grads[k] = small_red[k].reshape(weights[k].shape)
        delta[k], new_m[k], new_v[k] = outs[0][k], outs[1][k], outs[2][k]

    order = ("w_in", "b_merge", "conv_w", "conv_b", "rg_wx", "rg_bx", "rg_wa", "rg_ba", "rg_lambda", "hg_lb_logits",
             "hg_norm_g", "proj_a", "proj_b", "w_out", "norm_g", "final_norm_g")
    return (loss, grad_x, *[grads[k] for k in order], *[delta[k] for k in order], *[new_m[k] for k in order],
            *[new_v[k] for k in order])
```

```python
import functools

import jax
import jax.numpy as jnp
from jax import lax
from jax.experimental import pallas as pl
from jax.experimental.pallas import tpu as pltpu

F32 = jnp.float32
_MXU_DTYPE = jnp.bfloat16

D_MODEL = 1024
LANES = 128
SUBLANES = 8
N_BLK = D_MODEL // LANES
N_GROUPS = 8
N_SHARDS = 4
CONV_WIDTH = 4
LRU_C = 8.0
CHUNK = 64
CHUNKS_IN_FLIGHT = 16
HG_SCALE = float(LANES) ** -0.5
EPS = 1e-6
ADAM_LR, ADAM_B1, ADAM_B2, ADAM_EPS, ADAM_WD, ADAM_STEP = 0.001, 0.9, 0.999, 1e-08, 0.01, 10
VMEM_LIMIT = 56 * 1024 * 1024
VMEM_LIMIT_BIG = 60 * 1024 * 1024
MESH = pl.DeviceIdType.MESH

_SLOT_TO_GROUP = (2, 3, 4, 5, 0, 1, 6, 7)


def _group_of_slot(s):
    return jnp.where(s < 4, s + 2, jnp.where(s < 6, s - 4, s))


def _mm(a, b):
    return lax.dot_general(a.astype(_MXU_DTYPE), b.astype(_MXU_DTYPE), (((1,), (0,)), ((), ())),
                           preferred_element_type=F32)


def _mm_nt(a, b):
    return lax.dot_general(a.astype(_MXU_DTYPE), b.astype(_MXU_DTYPE), (((1,), (1,)), ((), ())),
                           preferred_element_type=F32)


def _mm_tn(a, b):
    return lax.dot_general(a.astype(_MXU_DTYPE), b.astype(_MXU_DTYPE), (((0,), (0,)), ((), ())),
                           preferred_element_type=F32)


def _sigmoid(x):
    return 0.5 * jnp.tanh(0.5 * x) + 0.5


def _log1p_pos(y):
    series = y * (1.0 - y * (0.5 - y * (1.0 / 3.0 - y * 0.25)))
    return jnp.where(y < 0.01, series, jnp.log(1.0 + y))


def _softplus(x):
    return jnp.maximum(x, 0.0) + _log1p_pos(jnp.exp(-jnp.abs(x)))


def _shift_down(x, n):
    rolled = pltpu.roll(x, n, 0)
    edge = SUBLANES if (n < SUBLANES and x.shape[0] > SUBLANES) else x.shape[0]
    rows = lax.broadcasted_iota(jnp.int32, (edge, x.shape[1]), 0)
    head = jnp.where(rows >= n, rolled[:edge], 0.0)
    return head if edge == x.shape[0] else jnp.concatenate([head, rolled[edge:]], axis=0)


def _shift_up(x, n):
    size = x.shape[0]
    rolled = pltpu.roll(x, size - n, 0)
    edge = SUBLANES if (n < SUBLANES and size > SUBLANES) else size
    rows = lax.broadcasted_iota(jnp.int32, (edge, x.shape[1]), 0)
    tail = jnp.where(rows < edge - n, rolled[size - edge:], 0.0)
    return tail if edge == size else jnp.concatenate([rolled[:size - edge], tail], axis=0)


def _params(dims, vmem=VMEM_LIMIT):
    return pltpu.CompilerParams(dimension_semantics=dims, vmem_limit_bytes=vmem)


def _load_w_in_by_slot(w_hbm, w_res):
    for slot, g in enumerate(_SLOT_TO_GROUP):
        pltpu.sync_copy(w_hbm.at[g // 2, :, pl.ds((g % 2) * D_MODEL, D_MODEL)], w_res.at[slot])


def _inproj_fwd(x2d, norm_g, w_all):
    tokens, d = x2d.shape
    tm = min(512, tokens)

    def body(x_ref, g_ref, w_hbm, z_ref, ht_ref, h_scr, w_res):
        @pl.when((pl.program_id(0) == 0) & (pl.program_id(1) == 0))
        def _():
            _load_w_in_by_slot(w_hbm, w_res)

        @pl.when(pl.program_id(1) == 0)
        def _():
            x = x_ref[...]
            r = lax.rsqrt(jnp.mean(x * x, axis=-1, keepdims=True) + EPS)
            h = (x * r) * g_ref[...]
            h_scr[...] = h.astype(_MXU_DTYPE)
            ht_ref[...] = jnp.transpose(h).astype(_MXU_DTYPE)

        z_ref[...] = _mm(h_scr[...], w_res[pl.program_id(1)])

    return pl.pallas_call(
        body, name="inproj_fwd",
        grid=(tokens // tm, N_GROUPS),
        in_specs=[pl.BlockSpec((tm, d), lambda i, s: (i, 0)),
                  pl.BlockSpec((1, d), lambda i, s: (0, 0)),
                  pl.BlockSpec(memory_space=pl.ANY)],
        out_specs=[pl.BlockSpec((None, tm, D_MODEL), lambda i, s: (s, i, 0)),
                   pl.BlockSpec((d, tm), lambda i, s: (0, i))],
        out_shape=[jax.ShapeDtypeStruct((N_GROUPS, tokens, D_MODEL), F32),
                   jax.ShapeDtypeStruct((d, tokens), _MXU_DTYPE)],
        scratch_shapes=[pltpu.VMEM((tm, d), _MXU_DTYPE), pltpu.VMEM((N_GROUPS, d, D_MODEL), _MXU_DTYPE)],
        compiler_params=_params(("arbitrary", "arbitrary")),
    )(x2d, norm_g, w_all)


def _slot_of_group(g):
    return jnp.where(g < 2, g + 4, jnp.where(g < 6, g - 2, g))


def _inproj_fwd_gather(x2d, norm_g, slotted, conv_slotted, chip):
    tokens, d = x2d.shape
    tm = min(512, tokens)
    n_tiles = tokens // tm
    n_big = len(slotted)
    n_sem = 6 * (n_big + 1) + 3
    last_pass = N_GROUPS - 1

    def shard_of(k, chip_id):
        x, y = chip_id // 2, chip_id % 2
        return 2 * jnp.where(k % 2 == 1, 1 - x, x) + jnp.where(k // 2 == 1, 1 - y, y)

    def body(chip_ref, x_ref, g_ref, *rest):
        bufs, cw = rest[n_big + 1:2 * n_big + 1], rest[2 * n_big + 1]
        z_ref, ht_ref = rest[2 * n_big + 2:2 * n_big + 4]
        h_all, slab, send_sems, recv_sems, slab_sems = rest[2 * n_big + 4:]
        del chip_ref
        p, i = pl.program_id(0), pl.program_id(1)
        x, y, c, chips = _mesh_position()
        me, sibling = 2 * x + y, (x, y, 1 - c)

        pieces = [(0, 0), (0, 1)] + [(a, None) for a in range(1, n_big)]

        def half(piece, slot, which):
            a, q = pieces[piece]
            hs = bufs[a].shape[1] // 2
            cols = slice(None) if q is None else pl.ds(q * D_MODEL, D_MODEL)
            return bufs[a].at[slot, pl.ds(which * hs, hs), cols]

        def send(piece, j):
            mine = half(piece, me, c)
            return _remote(mine, mine, send_sems, recv_sems, 6 * piece + j, (chips[j][0], chips[j][1], c))

        def arrival(piece, j):
            landed = half(piece, 2 * chips[j][0] + chips[j][1], c)
            return _remote(landed, landed, send_sems, recv_sems, 6 * piece + j, (chips[j][0], chips[j][1], c))

        def passed_on(piece, j, which):
            landed = half(piece, 2 * chips[j][0] + chips[j][1], which)
            return _remote(landed, landed, send_sems, recv_sems, 6 * piece + 3 + j, sibling)

        def conv_copy(j, slot):
            return _remote(cw.at[slot], cw.at[slot], send_sems, recv_sems, 6 * len(pieces) + j,
                           (chips[j][0], chips[j][1], c))

        def land(piece, j):
            arrival(piece, j).wait_recv()
            passed_on(piece, j, c).start()
            passed_on(piece, j, 1 - c).wait_recv()

        def slab_copy(pv):
            src = bufs[0].at[shard_of(pv // 2, me), :, pl.ds((pv % 2) * D_MODEL, D_MODEL)]
            return pltpu.make_async_copy(src, slab.at[pv % 2], slab_sems.at[pv % 2])

        @pl.when((p == 0) & (i == 0))
        def _():
            for q in range(2):
                send(q, 0).start()
                send(q, 1).start()
            slab_copy(0).start()

        for pv in range(N_GROUPS):
            @pl.when((p == pv) & (i == 0))
            def _(pv=pv):
                slab_copy(pv).wait()

        rows = pl.ds(pl.multiple_of(i * tm, tm), tm)

        @pl.when(p == 0)
        def _():
            xt = x_ref[...]
            r = lax.rsqrt(jnp.mean(xt * xt, axis=-1, keepdims=True) + EPS)
            h = (xt * r) * g_ref[...]
            h_all[rows, :] = h.astype(_MXU_DTYPE)
            ht_ref[...] = jnp.transpose(h).astype(_MXU_DTYPE)

        z_ref[...] = _mm(h_all[rows, :], slab[p % 2])

        def relay(q):
            landed = half(q, 2 * chips[q][0] + chips[q][1], c)
            to = chips[1 - q]
            return _remote(landed, landed, send_sems, recv_sems, 6 * q + 2, (to[0], to[1], c))

        landings = {1: [(0, 0)], 2: [(1, 0), (1, 1)], 3: [(0, 1)], 5: [(0, 2)], 6: [(1, 2)]}
        for pv in range(1, N_GROUPS):
            @pl.when((p == pv - 1) & (i == n_tiles - 1))
            def _(pv=pv):
                for q, j in landings.get(pv - 1, []):
                    land(q, j)
                    if j == q:
                        relay(q).start()
                if pv - 1 == 3:
                    for piece in range(2, len(pieces)):
                        for jj in range(3):
                            send(piece, jj).start()
                    for jj in range(3):
                        conv_copy(jj, me).start()
                slab_copy(pv).start()

        @pl.when((p == last_pass) & (i == n_tiles - 1))
        def _():
            for piece in range(2, len(pieces)):
                for j in range(3):
                    land(piece, j)
            for j in range(3):
                conv_copy(j, 2 * chips[j][0] + chips[j][1]).wait_recv()
            for piece in range(len(pieces)):
                for j in range(3):
                    (relay(piece) if (piece < 2 and j == 2) else send(piece, j)).wait_send()
                    passed_on(piece, j, c).wait_send()
            for j in range(3):
                conv_copy(j, me).wait_send()

    def z_index(p, i, chip_ref):
        g = 2 * shard_of(p // 2, chip_ref[0]) + p % 2
        return (_slot_of_group(g), i, 0)

    def first_pass_tile(p, i, chip_ref):
        return jnp.where(p == 0, i, n_tiles - 1)

    hbm = pl.BlockSpec(memory_space=pl.ANY)
    operands = list(slotted) + [conv_slotted]
    grid_spec = pltpu.PrefetchScalarGridSpec(
        num_scalar_prefetch=1, grid=(N_GROUPS, n_tiles),
        in_specs=[pl.BlockSpec((tm, d), lambda p, i, chip_ref: (first_pass_tile(p, i, chip_ref), 0)),
                  pl.BlockSpec((1, d), lambda p, i, chip_ref: (0, 0))] + [hbm] * (n_big + 1),
        out_specs=[hbm] * (n_big + 1) + [pl.BlockSpec((None, tm, D_MODEL), z_index),
                                         pl.BlockSpec((d, tm), lambda p, i, chip_ref: (0, first_pass_tile(p, i, chip_ref)))],
        scratch_shapes=[pltpu.VMEM((tokens, d), _MXU_DTYPE), pltpu.VMEM((2, d, D_MODEL), _MXU_DTYPE),
                        pltpu.SemaphoreType.DMA((n_sem,)), pltpu.SemaphoreType.DMA((n_sem,)),
                        pltpu.SemaphoreType.DMA((2,))])
    out = pl.pallas_call(
        body, name="inproj_fwd_gather", grid_spec=grid_spec,
        out_shape=[jax.ShapeDtypeStruct(a.shape, a.dtype) for a in operands]
        + [jax.ShapeDtypeStruct((N_GROUPS, tokens, D_MODEL), F32), jax.ShapeDtypeStruct((d, tokens), _MXU_DTYPE)],
        input_output_aliases={3 + a: a for a in range(n_big + 1)},
        compiler_params=_params(("arbitrary", "arbitrary")),
    )(chip, x2d, norm_g, *operands)
    return out[n_big + 1], out[n_big + 2], out[:n_big], out[n_big]


def _lane_blocks(x):
    return [x[:, k * LANES:(k + 1) * LANES] for k in range(x.shape[1] // LANES)]


def _block_diag(x, w_ref, transposed=False):
    mm = _mm_nt if transposed else _mm
    return jnp.concatenate([mm(xk, w_ref[k]) for k, xk in enumerate(_lane_blocks(x))], axis=1)


def _lru_gates(xa, cw_ref, cb_ref, wx_ref, bx_ref, wa_ref, ba_ref, lam_ref):
    xc = (cb_ref[...] + cw_ref[3:4, :] * xa + cw_ref[2:3, :] * _shift_down(xa, 1)
          + cw_ref[1:2, :] * _shift_down(xa, 2) + cw_ref[0:1, :] * _shift_down(xa, 3))
    gi = _sigmoid(_block_diag(xc, wx_ref) + bx_ref[...])
    gr = _sigmoid(_block_diag(xc, wa_ref) + ba_ref[...])
    sp = _softplus(-lam_ref[...])
    a, mult, inv_mult = _lru_decay(gr, sp)
    return xc, gi, gr, sp, a, mult, inv_mult


def _lru_decay(gr, sp):
    log_a = (-LRU_C) * gr * sp
    a = jnp.exp(log_a)
    y = 2.0 * log_a
    mult_sq = jnp.where(y > -1e-3, -y * (1.0 + 0.5 * y), 1.0 - a * a)
    inv_mult = lax.rsqrt(jnp.maximum(mult_sq, 1e-37))
    return a, mult_sq * inv_mult, inv_mult


def _tile_rows(width):
    return lax.broadcasted_iota(jnp.int32, (SUBLANES, width), 0)


def _scan_forward(a_scr, u_scr, h_scr, seq):
    width = a_scr.shape[1]
    rows = _tile_rows(width)

    def tile(j, carry):
        sl = pl.ds(pl.multiple_of(j * SUBLANES, SUBLANES), SUBLANES)
        a = a_scr[sl, :]
        u = u_scr[sl, :]
        for d in (1, 2, 4):
            keep = rows >= d
            a_sh = jnp.where(keep, pltpu.roll(a, d, 0), 1.0)
            u_sh = jnp.where(keep, pltpu.roll(u, d, 0), 0.0)
            u = a * u_sh + u
            a = a * a_sh
        h = u + a * carry
        h_scr[sl, :] = h
        return jnp.broadcast_to(h[SUBLANES - 1:SUBLANES, :], (SUBLANES, width))

    lax.fori_loop(0, seq // SUBLANES, tile, jnp.zeros((SUBLANES, width), F32))


def _scan_backward(c_scr, d_scr, g_scr, seq):
    width = c_scr.shape[1]
    rows = _tile_rows(width)
    n_tiles = seq // SUBLANES

    def tile(jj, carry):
        j = n_tiles - 1 - jj
        sl = pl.ds(pl.multiple_of(j * SUBLANES, SUBLANES), SUBLANES)
        c = c_scr[sl, :]
        g = d_scr[sl, :]
        for d in (1, 2, 4):
            keep = rows < SUBLANES - d
            c_sh = jnp.where(keep, pltpu.roll(c, SUBLANES - d, 0), 1.0)
            g_sh = jnp.where(keep, pltpu.roll(g, SUBLANES - d, 0), 0.0)
            g = c * g_sh + g
            c = c * c_sh
        g = g + c * carry
        g_scr[sl, :] = g
        return jnp.broadcast_to(g[0:1, :], (SUBLANES, width))

    lax.fori_loop(0, n_tiles, tile, jnp.zeros((SUBLANES, width), F32))


LRU_BLOCKS_PER_STEP = 2
LRU_LANES = LRU_BLOCKS_PER_STEP * LANES
LRU_STEPS = N_BLK // LRU_BLOCKS_PER_STEP


def _lru_param_specs(cb_axis):
    def pick(*ids):
        return ids[cb_axis]

    vec = pl.BlockSpec((1, LRU_LANES), lambda *ids: (0, pick(*ids)))
    mat = pl.BlockSpec((LRU_BLOCKS_PER_STEP, LANES, LANES), lambda *ids: (pick(*ids), 0, 0))
    return [pl.BlockSpec((CONV_WIDTH, LRU_LANES), lambda *ids: (0, pick(*ids))), vec, mat, vec, mat, vec, vec]


def _branch_a_fwd(z, conv_w, conv_b, wx, bx, wa, ba, lam, batch, seq):
    tokens = batch * seq

    def body(z_ref, cw_ref, cb_ref, wx_ref, bx_ref, wa_ref, ba_ref, lam_ref, ya_ref, hl_ref, kept_ref, a_scr, u_scr):
        xa = z_ref[0]
        ga = z_ref[1]
        xc, gi, gr, _, a, mult, _ = _lru_gates(xa, cw_ref, cb_ref, wx_ref, bx_ref, wa_ref, ba_ref, lam_ref)
        kept_ref[0], kept_ref[1], kept_ref[2] = xc, gi, gr
        a_scr[...] = a
        u_scr[...] = mult * gi * xc
        _scan_forward(a_scr, u_scr, hl_ref, seq)
        ya_ref[...] = (hl_ref[...] * (ga * _sigmoid(ga))).astype(_MXU_DTYPE)

    blk = pl.BlockSpec((seq, LRU_LANES), lambda b, c: (b, c))
    return pl.pallas_call(
        body, name="branch_a_fwd",
        grid=(batch, LRU_STEPS),
        in_specs=[pl.BlockSpec((2, seq, LRU_LANES), lambda b, c: (2, b, c))] + _lru_param_specs(1),
        out_specs=[blk, blk, pl.BlockSpec((3, seq, LRU_LANES), lambda b, c: (0, b, c))],
        out_shape=[jax.ShapeDtypeStruct((tokens, D_MODEL), _MXU_DTYPE), jax.ShapeDtypeStruct((tokens, D_MODEL), F32),
                   jax.ShapeDtypeStruct((3, tokens, D_MODEL), F32)],
        scratch_shapes=[pltpu.VMEM((seq, LRU_LANES), F32), pltpu.VMEM((seq, LRU_LANES), F32)],
        compiler_params=_params(("parallel", "parallel")),
    )(z, conv_w, conv_b, wx, bx, wa, ba, lam)


def _branch_a_bwd(z, hl, kept, dya, dz, conv_w, conv_b, wx, bx, wa, ba, lam, batch, seq):
    def body(z_ref, hl_ref, kept_ref, dya_ref, dz_in_ref, cw_ref, cb_ref, wx_ref, bx_ref, wa_ref, ba_ref, lam_ref,
             dz_ref, dcw_ref, dcb_ref, dwx_ref, dbx_ref, dwa_ref, dba_ref, dlam_ref, c_scr, d_scr):
        del dz_in_ref, cb_ref, bx_ref, ba_ref
        g_scr = d_scr
        xa = z_ref[0]
        ga = z_ref[1]
        hl = hl_ref[...]
        dya = dya_ref[...]
        xc, gi, gr = kept_ref[0], kept_ref[1], kept_ref[2]
        sp = _softplus(-lam_ref[...])
        a, mult, inv_mult = _lru_decay(gr, sp)
        sga = _sigmoid(ga)
        dz_ref[1] = (dya * hl * (sga * (1.0 + ga * (1.0 - sga)))).astype(_MXU_DTYPE)
        c_scr[...] = _shift_up(a, 1)
        d_scr[...] = dya * (ga * sga)
        _scan_backward(c_scr, d_scr, g_scr, seq)
        g = g_scr[...]
        da = g * _shift_down(hl, 1)
        dmult = g * gi * xc
        dgi = g * mult * xc
        dxc = g * mult * gi
        dlog_a = da * a - dmult * (a * a) * inv_mult
        dgr = dlog_a * (-LRU_C) * sp
        dsp = jnp.sum(dlog_a * gr, axis=0, keepdims=True) * (-LRU_C)
        dlam = -dsp * _sigmoid(-lam_ref[...])
        dpi = dgi * gi * (1.0 - gi)
        dpr = dgr * gr * (1.0 - gr)
        dxc = dxc + _block_diag(dpi, wx_ref, transposed=True) + _block_diag(dpr, wa_ref, transposed=True)
        dwx = jnp.stack([_mm_tn(xk, dk) for xk, dk in zip(_lane_blocks(xc), _lane_blocks(dpi))])
        dwa = jnp.stack([_mm_tn(xk, dk) for xk, dk in zip(_lane_blocks(xc), _lane_blocks(dpr))])
        dbx = jnp.sum(dpi, axis=0, keepdims=True)
        dba = jnp.sum(dpr, axis=0, keepdims=True)
        ahead = [dxc if k == CONV_WIDTH - 1 else _shift_up(dxc, CONV_WIDTH - 1 - k) for k in range(CONV_WIDTH)]
        dxa = sum(cw_ref[k:k + 1, :] * ahead[k] for k in range(CONV_WIDTH))
        dz_ref[0] = dxa.astype(_MXU_DTYPE)
        dcb = jnp.sum(dxc, axis=0, keepdims=True)
        dcw = [jnp.sum(ahead[k] * xa, axis=0, keepdims=True) for k in range(CONV_WIDTH)]

        @pl.when(pl.program_id(1) == 0)
        def _():
            for k in range(CONV_WIDTH):
                dcw_ref[k:k + 1, :] = dcw[k]
            dcb_ref[...] = dcb
            dwx_ref[...] = dwx
            dbx_ref[...] = dbx
            dwa_ref[...] = dwa
            dba_ref[...] = dba
            dlam_ref[...] = dlam

        @pl.when(pl.program_id(1) != 0)
        def _():
            for k in range(CONV_WIDTH):
                dcw_ref[k:k + 1, :] += dcw[k]
            dcb_ref[...] += dcb
            dwx_ref[...] += dwx
            dbx_ref[...] += dbx
            dwa_ref[...] += dwa
            dba_ref[...] += dba
            dlam_ref[...] += dlam

    tokens = batch * seq
    blk = pl.BlockSpec((seq, LRU_LANES), lambda c, b: (b, c))
    vec = pl.BlockSpec((1, LRU_LANES), lambda c, b: (0, c))
    mat = pl.BlockSpec((LRU_BLOCKS_PER_STEP, LANES, LANES), lambda c, b: (c, 0, 0))
    vec_shape = jax.ShapeDtypeStruct((1, D_MODEL), F32)
    mat_shape = jax.ShapeDtypeStruct((N_BLK, LANES, LANES), F32)
    return pl.pallas_call(
        body, name="branch_a_bwd",
        grid=(LRU_STEPS, batch),
        in_specs=[pl.BlockSpec((2, seq, LRU_LANES), lambda c, b: (2, b, c)), blk,
                  pl.BlockSpec((3, seq, LRU_LANES), lambda c, b: (0, b, c)), blk,
                  pl.BlockSpec(memory_space=pl.ANY)] + _lru_param_specs(0),
        out_specs=[pl.BlockSpec((2, seq, LRU_LANES), lambda c, b: (2, b, c)),
                   pl.BlockSpec((CONV_WIDTH, LRU_LANES), lambda c, b: (0, c)), vec, mat, vec, mat, vec, vec],
        out_shape=[jax.ShapeDtypeStruct((N_GROUPS, tokens, D_MODEL), _MXU_DTYPE),
                   jax.ShapeDtypeStruct((CONV_WIDTH, D_MODEL), F32), vec_shape, mat_shape, vec_shape, mat_shape,
                   vec_shape, vec_shape],
        scratch_shapes=[pltpu.VMEM((seq, LRU_LANES), F32)] * 2,
        input_output_aliases={4: 0},
        compiler_params=_params(("parallel", "arbitrary"), vmem=VMEM_LIMIT_BIG),
    )(z, hl, kept, dya, dz, conv_w, conv_b, wx, bx, wa, ba, lam)


def _chunk_masks(transposed=False):
    r = lax.broadcasted_iota(jnp.int32, (CHUNK, CHUNK), 0)
    c = lax.broadcasted_iota(jnp.int32, (CHUNK, CHUNK), 1)
    return r <= c if transposed else r >= c


def _row_blocks(seq, fn):
    block = min(256, seq)

    def trip(i, carry):
        fn(pl.ds(pl.multiple_of(i * block, block), block))
        return carry

    lax.fori_loop(0, seq // block, trip, 0)


def _hgrn_prepare(z_ref, lb_ref, f_scr, logf_scr, qh_scr, seq):
    lb = _sigmoid(lb_ref[0:1, :] - lb_ref[1:2, :])

    def block(rows):
        q = z_ref[0, rows, :]
        f = lb + (1.0 - lb) * _sigmoid(z_ref[1, rows, :])
        f_scr[rows, :] = f
        logf_scr[rows, :] = jnp.log(f)
        qh_scr[rows, :] = q * _sigmoid(q)

    _row_blocks(seq, block)
    return lb


def _cumsum_rows(x, reverse=False):
    shift = _shift_up if reverse else _shift_down
    d = 1
    while d < x.shape[0]:
        x = x + shift(x, d)
        d *= 2
    return x


def _lane_mean(x):
    return jnp.mean(x, axis=-1, keepdims=True)


def _token_contractions(lhs_scr, rhs_scr, out_ref, seq):
    rows_id = lax.broadcasted_iota(jnp.int32, (LANES, LANES), 0)

    def transposed(p):
        rows = pl.ds(pl.multiple_of(p * LANES, LANES), LANES)
        return jnp.transpose(lhs_scr[rows, :]).astype(_MXU_DTYPE), rhs_scr[rows, :]

    def contract(p, s):
        lhs_t, rhs = s
        return (_mm(lhs_t, jnp.where(rows_id < CHUNK, rhs, 0.0)), _mm(lhs_t, jnp.where(rows_id >= CHUNK, rhs, 0.0)))

    def store(p, out):
        out_ref[2 * p] = out[0]
        out_ref[2 * p + 1] = out[1]

    _independent_trips(seq // LANES, [transposed, contract], store)


def _chunk_rows(c):
    return pl.ds(pl.multiple_of(c * CHUNK, CHUNK), CHUNK)


def _chunk_terms(c, z_ref, f_scr, qh_scr, b_scr):
    rows = _chunk_rows(c)
    b = b_scr[rows, :]
    b_mid = b_scr[pl.ds(c * CHUNK + CHUNK // 2, 1), :]
    b_last = b_scr[pl.ds(c * CHUNK + CHUNK - 1, 1), :]
    qh = qh_scr[rows, :]
    k = 1.0 - f_scr[rows, :]
    v = z_ref[2, rows, :]
    e_q = jnp.exp(b - b_mid) * HG_SCALE
    e_k = jnp.exp(b_mid - b)
    e_qi = jnp.exp(b) * HG_SCALE
    e_ks = jnp.exp(b_last - b)
    decay = jnp.exp(b_last)
    return rows, qh, k, v, e_q, e_k, e_qi, e_ks, decay


def _independent_trips(n, stages, store, group=CHUNKS_IN_FLIGHT):
    stages = stages if isinstance(stages, (list, tuple)) else [stages]
    group = min(group, n)

    def trip(g, carry):
        ids = [g * group + i for i in range(group)]
        state = [stages[0](c) for c in ids]
        for stage in stages[1:]:
            state = [stage(c, s) for c, s in zip(ids, state)]
        for c, s in zip(ids, state):
            store(c, s)
        return carry

    lax.fori_loop(0, n // group, trip, 0)


def _branch_b_fwd(z, lb_logits, hg_g, batch, seq):
    tokens = batch * seq
    n_chunks = seq // CHUNK

    def body(z_ref, lb_ref, g_ref, yb_ref, st_ref, kept_ref, logf_scr, o_scr, qi_scr, ks_scr, dec_scr):
        f_scr, qh_scr, b_scr, o_kept = (kept_ref.at[k] for k in range(4))
        _hgrn_prepare(z_ref, lb_ref, f_scr, logf_scr, qh_scr, seq)
        causal = _chunk_masks()
        gain = g_ref[...]

        def cumulate(c):
            return _cumsum_rows(logf_scr[_chunk_rows(c), :])

        def store_cumulated(c, b):
            b_scr[_chunk_rows(c), :] = b

        def scores(c):
            _, qh, k, v, e_q, e_k, e_qi, e_ks, decay = _chunk_terms(c, z_ref, f_scr, qh_scr, b_scr)
            return _mm_nt(qh * e_q, k * e_k), v, qh * e_qi, k * e_ks, decay

        def within_chunk(c, s):
            att, v, q_int, k_st, decay = s
            return _mm(jnp.where(causal, att, 0.0), v), q_int, k_st, decay

        def store_within_chunk(c, out):
            rows = _chunk_rows(c)
            o_scr[rows, :], qi_scr[rows, :], ks_scr[rows, :], dec_scr[pl.ds(c, 1), :] = out

        def carry_state(c, state_t):
            update = st_ref[c]
            st_ref[c] = state_t
            return state_t * dec_scr[pl.ds(c, 1), :] + update

        def finish(c):
            rows = _chunk_rows(c)
            o = o_scr[rows, :] + _mm_nt(qi_scr[rows, :], st_ref[c])
            r = lax.rsqrt(_lane_mean(o * o) + EPS)
            gb = z_ref[3, rows, :]
            return (((o * r) * gain) * (gb * _sigmoid(gb))).astype(_MXU_DTYPE), o

        def store_finished(c, out):
            yb_ref[_chunk_rows(c), :], o_kept[_chunk_rows(c), :] = out

        _independent_trips(n_chunks, cumulate, store_cumulated)
        _independent_trips(n_chunks, [scores, within_chunk], store_within_chunk)
        _token_contractions(z_ref.at[2], ks_scr, st_ref, seq)
        lax.fori_loop(0, n_chunks, carry_state, jnp.zeros((LANES, LANES), F32))
        _independent_trips(n_chunks, finish, store_finished)

    seq_buf = pltpu.VMEM((seq, LANES), F32)
    return pl.pallas_call(
        body, name="branch_b_fwd",
        grid=(batch, N_BLK),
        in_specs=[pl.BlockSpec((4, seq, LANES), lambda b, h: (0, b, h)),
                  pl.BlockSpec((2, LANES), lambda b, h: (0, h)),
                  pl.BlockSpec((1, LANES), lambda b, h: (0, 0))],
        out_specs=[pl.BlockSpec((seq, LANES), lambda b, h: (b, h)),
                   pl.BlockSpec((None, n_chunks, LANES, LANES), lambda b, h: (b * N_BLK + h, 0, 0, 0)),
                   pl.BlockSpec((4, seq, LANES), lambda b, h: (0, b, h))],
        out_shape=[jax.ShapeDtypeStruct((tokens, D_MODEL), _MXU_DTYPE),
                   jax.ShapeDtypeStruct((batch * N_BLK, n_chunks, LANES, LANES), F32),
                   jax.ShapeDtypeStruct((4, tokens, D_MODEL), F32)],
        scratch_shapes=[seq_buf] * 4 + [pltpu.VMEM((n_chunks, LANES), F32)],
        compiler_params=_params(("parallel", "parallel")),
    )(z, lb_logits, hg_g)


def _branch_b_bwd(z, states, kept, dyb, dz, lb_logits, hg_g, batch, seq):
    n_chunks = seq // CHUNK

    def body(z_ref, st_ref, kept_ref, dyb_ref, dz_in_ref, lb_ref, g_ref, dz_ref, dlog_ref, dg_ref,
             do_scr, qi_scr, dqh_scr, df_scr, dec_scr, dgp_scr, dlb_scr, dst_scr):
        del dz_in_ref
        f_scr, qh_scr, b_scr, o_kept = (kept_ref.at[k] for k in range(4))
        first = (pl.program_id(0) == 0) & (pl.program_id(1) == 0)
        lb = _sigmoid(lb_ref[0:1, :] - lb_ref[1:2, :])
        causal = _chunk_masks()
        anti_causal = _chunk_masks(transposed=True)
        gain = g_ref[...]

        @pl.when(first)
        def _():
            dg_ref[...] = jnp.zeros_like(dg_ref)

        @pl.when(pl.program_id(1) == 0)
        def _():
            dlb_scr[...] = jnp.zeros_like(dlb_scr)

        def output_gradient(c):
            rows = _chunk_rows(c)
            b = b_scr[rows, :]
            q_int = qh_scr[rows, :] * (jnp.exp(b) * HG_SCALE)
            decay = jnp.exp(b_scr[pl.ds(c * CHUNK + CHUNK - 1, 1), :])
            o = o_kept[rows, :]
            r = lax.rsqrt(_lane_mean(o * o) + EPS)
            o_n = o * r
            gb = z_ref[3, rows, :]
            sgb = _sigmoid(gb)
            dyb_c = dyb_ref[rows, :]
            d_ong = dyb_c * (gb * sgb)
            d_gb = (dyb_c * (o_n * gain) * (sgb * (1.0 + gb * (1.0 - sgb)))).astype(_MXU_DTYPE)
            d_gain = jnp.sum(d_ong * o_n, axis=0, keepdims=True)
            d_on = d_ong * gain
            return d_gb, d_gain, r * (d_on - o_n * _lane_mean(d_on * o_n)), q_int, decay

        def store_output_gradient(c, out):
            rows = _chunk_rows(c)
            dz_ref[3, rows, :], dgp_scr[pl.ds(c, 1), :], do_scr[rows, :], qi_scr[rows, :], dec_scr[pl.ds(c, 1), :] = out

        def carry_state_gradient(cc, d_state_t):
            c = n_chunks - 1 - cc
            update = dst_scr[c]
            dst_scr[c] = d_state_t
            return d_state_t * dec_scr[pl.ds(c, 1), :] + update

        def score_gradients(c):
            rows, qh, k, v, e_q, e_k, e_qi, e_ks, decay = _chunk_terms(c, z_ref, f_scr, qh_scr, b_scr)
            state_t = st_ref[c]
            d_state_t = dst_scr[c]
            d_o = do_scr[rows, :]
            q_in, k_in, q_int, k_st = qh * e_q, k * e_k, qh * e_qi, k * e_ks
            first = (_mm_nt(k_in, q_in), _mm_nt(d_o, v), _mm_nt(v, d_o), _mm_nt(k_st, d_state_t), _mm(d_o, state_t),
                     _mm(v, d_state_t))
            d_decay = jnp.sum(state_t * d_state_t, axis=0, keepdims=True)
            return first, d_o, q_in, k_in, q_int, k_st, e_q, e_k, e_qi, e_ks, decay, d_decay

        def input_gradients(c, s):
            (att_t, d_att, d_att_t, dv_inter, dq_int, dk_st), d_o, q_in, k_in, q_int, k_st, e_q, e_k, e_qi, e_ks, decay, d_decay = s
            rows = _chunk_rows(c)
            d_v = _mm(jnp.where(anti_causal, att_t, 0.0), d_o) + dv_inter
            dq_in = _mm(jnp.where(causal, d_att, 0.0), k_in)
            dk_in = _mm(jnp.where(anti_causal, d_att_t, 0.0), q_in)
            d_k = dk_in * e_k + dk_st * e_ks
            kk = dk_st * k_st
            d_b = dq_in * q_in + dq_int * q_int - dk_in * k_in - kk
            d_b_last = jnp.sum(kk, axis=0, keepdims=True) + decay * d_decay
            d_logf = _cumsum_rows(d_b, reverse=True) + d_b_last
            return d_v.astype(_MXU_DTYPE), dq_in * e_q + dq_int * e_qi, d_logf / f_scr[rows, :] - d_k

        def store_input_gradients(c, out):
            rows = _chunk_rows(c)
            dz_ref[2, rows, :], dqh_scr[rows, :], df_scr[rows, :] = out

        def input_activations(rows):
            q = z_ref[0, rows, :]
            sq = _sigmoid(q)
            dz_ref[0, rows, :] = (dqh_scr[rows, :] * (sq * (1.0 + q * (1.0 - sq)))).astype(_MXU_DTYPE)
            sg = _sigmoid(z_ref[1, rows, :])
            d_f = df_scr[rows, :]
            dz_ref[1, rows, :] = (d_f * (1.0 - lb) * sg * (1.0 - sg)).astype(_MXU_DTYPE)
            dlb_scr[...] += jnp.sum(d_f * (1.0 - sg), axis=0, keepdims=True)

        _independent_trips(n_chunks, output_gradient, store_output_gradient)
        _token_contractions(do_scr, qi_scr, dst_scr, seq)
        lax.fori_loop(0, n_chunks, carry_state_gradient, jnp.zeros((LANES, LANES), F32))
        _independent_trips(n_chunks, [score_gradients, input_gradients], store_input_gradients)
        dg_ref[...] += jnp.sum(dgp_scr[...], axis=0, keepdims=True)
        _row_blocks(seq, input_activations)
        d_l0 = dlb_scr[...] * lb * (1.0 - lb)
        dlog_ref[0:1, :] = d_l0
        dlog_ref[1:2, :] = -d_l0

    tokens = batch * seq
    seq_buf = pltpu.VMEM((seq, LANES), F32)
    chunk_rows = pltpu.VMEM((n_chunks, LANES), F32)
    return pl.pallas_call(
        body, name="branch_b_bwd",
        grid=(N_BLK, batch),
        in_specs=[pl.BlockSpec((4, seq, LANES), lambda h, b: (0, b, h)),
                  pl.BlockSpec((None, n_chunks, LANES, LANES), lambda h, b: (b * N_BLK + h, 0, 0, 0)),
                  pl.BlockSpec((4, seq, LANES), lambda h, b: (0, b, h)),
                  pl.BlockSpec((seq, LANES), lambda h, b: (b, h)),
                  pl.BlockSpec(memory_space=pl.ANY),
                  pl.BlockSpec((2, LANES), lambda h, b: (0, h)),
                  pl.BlockSpec((1, LANES), lambda h, b: (0, 0))],
        out_specs=[pl.BlockSpec((4, seq, LANES), lambda h, b: (0, b, h)),
                   pl.BlockSpec((2, LANES), lambda h, b: (0, h)),
                   pl.BlockSpec((1, LANES), lambda h, b: (0, 0))],
        out_shape=[jax.ShapeDtypeStruct((N_GROUPS, tokens, D_MODEL), _MXU_DTYPE),
                   jax.ShapeDtypeStruct((2, D_MODEL), F32),
                   jax.ShapeDtypeStruct((1, LANES), F32)],
        scratch_shapes=[seq_buf] * 4 + [chunk_rows, chunk_rows, pltpu.VMEM((1, LANES), F32),
                                        pltpu.VMEM((n_chunks, LANES, LANES), F32)],
        input_output_aliases={4: 0},
        compiler_params=_params(("arbitrary", "arbitrary")),
    )(z, states, kept, dyb, dz, lb_logits, hg_g)


def _merge_tail(ya, yb, z, x2d, tgt2d, b_merge, final_g, pa, pb, wo):
    tokens, d = x2d.shape
    tm = min(256, tokens)
    n_tiles = tokens // tm

    def body(ya_ref, yb_ref, z_ref, x_ref, t_ref, bm_ref, fg_ref, pa_hbm, pb_hbm, wo_hbm,
             dya_ref, dyb_ref, dx2_ref, dz_ref, loss_ref, dfg_ref, dbm_ref, dpa_hbm, dpb_hbm, dwo_hbm,
             pa_s, pb_s, wo_s, dpa_s, dpb_s, dwo_s):
        i = pl.program_id(0)

        @pl.when(i == 0)
        def _():
            pltpu.sync_copy(pa_hbm, pa_s)
            pltpu.sync_copy(pb_hbm, pb_s)
            pltpu.sync_copy(wo_hbm, wo_s)
            dpa_s[...] = jnp.zeros_like(dpa_s)
            dpb_s[...] = jnp.zeros_like(dpb_s)
            dwo_s[...] = jnp.zeros_like(dwo_s)
            loss_ref[...] = jnp.zeros_like(loss_ref)
            dfg_ref[...] = jnp.zeros_like(dfg_ref)
            dbm_ref[...] = jnp.zeros_like(dbm_ref)

        ya_t = ya_ref[...]
        yb_t = yb_ref[...]
        out_a = _mm(ya_t, pa_s[...])
        out_b = _mm(yb_t, pb_s[...])
        g_a = _sigmoid(z_ref[0] + bm_ref[:, :d])
        g_b = _sigmoid(z_ref[1] + bm_ref[:, d:])
        mixed = g_a * out_a + g_b * out_b
        x2 = x_ref[...] + _mm(mixed, wo_s[...])
        r = lax.rsqrt(jnp.mean(x2 * x2, axis=-1, keepdims=True) + EPS)
        xn = x2 * r
        fg = fg_ref[...]
        diff = xn * fg - t_ref[...]
        loss_ref[...] += jnp.sum(diff * diff) * (0.5 / d)
        dy = diff * (1.0 / d)
        dfg_ref[...] += jnp.sum(dy * xn, axis=0, keepdims=True)
        dxn = dy * fg
        dx2 = r * (dxn - xn * jnp.mean(dxn * xn, axis=-1, keepdims=True))
        dx2_ref[...] = dx2
        dmixed = _mm_nt(dx2, wo_s[...])
        dwo_s[...] += _mm_tn(mixed, dx2)
        dgm_a = dmixed * out_a * g_a * (1.0 - g_a)
        dgm_b = dmixed * out_b * g_b * (1.0 - g_b)
        dz_ref[0] = dgm_a.astype(_MXU_DTYPE)
        dz_ref[1] = dgm_b.astype(_MXU_DTYPE)
        dbm_ref[:, :d] += jnp.sum(dgm_a, axis=0, keepdims=True)
        dbm_ref[:, d:] += jnp.sum(dgm_b, axis=0, keepdims=True)
        dout_a = dmixed * g_a
        dout_b = dmixed * g_b
        dpa_s[...] += _mm_tn(ya_t, dout_a)
        dpb_s[...] += _mm_tn(yb_t, dout_b)
        dya_ref[...] = _mm_nt(dout_a, pa_s[...])
        dyb_ref[...] = _mm_nt(dout_b, pb_s[...])

        @pl.when(i == n_tiles - 1)
        def _():
            pltpu.sync_copy(dpa_s, dpa_hbm)
            pltpu.sync_copy(dpb_s, dpb_hbm)
            pltpu.sync_copy(dwo_s, dwo_hbm)

    tile = pl.BlockSpec((tm, d), lambda i: (i, 0))
    gm = pl.BlockSpec((2, tm, d), lambda i: (3, i, 0))
    row = lambda n: pl.BlockSpec((1, n), lambda i: (0, 0))
    hbm = pl.BlockSpec(memory_space=pl.ANY)
    act = jax.ShapeDtypeStruct((tokens, d), F32)
    mat = jax.ShapeDtypeStruct((d, d), F32)
    return pl.pallas_call(
        body, name="merge_tail",
        grid=(n_tiles,),
        in_specs=[tile, tile, gm, tile, tile, row(2 * d), row(d), hbm, hbm, hbm],
        out_specs=[tile, tile, tile, gm, row(LANES), row(d), row(2 * d), hbm, hbm, hbm],
        out_shape=[act, act, act, jax.ShapeDtypeStruct((N_GROUPS, tokens, d), _MXU_DTYPE),
                   jax.ShapeDtypeStruct((1, LANES), F32), jax.ShapeDtypeStruct((1, d), F32),
                   jax.ShapeDtypeStruct((1, 2 * d), F32), mat, mat, mat],
        scratch_shapes=[pltpu.VMEM((d, d), _MXU_DTYPE)] * 3 + [pltpu.VMEM((d, d), F32)] * 3,
        compiler_params=_params(("arbitrary",)),
    )(ya, yb, z, x2d, tgt2d, b_merge, final_g, pa, pb, wo)


def _inproj_dw(h_t, dz):
    d, tokens = h_t.shape
    tm = min(2048, tokens)

    def body(h_ref, dz_ref, dw_ref):
        part = _mm(h_ref[...], dz_ref[...])

        @pl.when(pl.program_id(1) == 0)
        def _():
            dw_ref[...] = part

        @pl.when(pl.program_id(1) != 0)
        def _():
            dw_ref[...] += part

    def out_index(s, i):
        g = _group_of_slot(s)
        return (g // 2, 0, g % 2)

    return pl.pallas_call(
        body, name="inproj_dw",
        grid=(N_GROUPS, tokens // tm),
        in_specs=[pl.BlockSpec((d, tm), lambda s, i: (0, i)),
                  pl.BlockSpec((None, tm, D_MODEL), lambda s, i: (s, i, 0))],
        out_specs=pl.BlockSpec((None, d, D_MODEL), out_index),
        out_shape=jax.ShapeDtypeStruct((N_SHARDS, d, 2 * D_MODEL), F32),
        compiler_params=_params(("parallel", "arbitrary")),
    )(h_t, dz)


def _inproj_dw_exchange(h_t, dz, scatter):
    d, tokens = h_t.shape
    tm = min(2048, tokens)
    n_i = tokens // tm
    half = d // 2

    def body(h_ref, dz_ref, *rest):
        n_in, n_out = scatter.n_in, scatter.n_out
        dw_hbm, land_hbm = rest[n_in:n_in + 2]
        acc, local_sems, send_sems, recv_sems = rest[n_in + 2 + n_out:n_in + 6 + n_out]
        carried = scatter.copies(rest[:n_in], rest[n_in + 2:n_in + 2 + n_out], rest[n_in + 6 + n_out:])
        s, i = pl.program_id(0), pl.program_id(1)
        x, y, c, _ = _mesh_position()

        @pl.when((s == 0) & (i == 0))
        def _():
            for cp in carried:
                cp.start()

        part = _mm(h_ref[...], dz_ref[...])
        buf = acc.at[s % 2]

        @pl.when(i == 0)
        def _():
            buf[...] = part

        @pl.when(i != 0)
        def _():
            buf[...] += part

        def copies(k):
            g = _SLOT_TO_GROUP[k]
            cols = pl.ds((g % 2) * D_MODEL, D_MODEL)
            src = acc.at[k % 2]
            mine = pltpu.make_async_copy(src, dw_hbm.at[g // 2, :, cols], local_sems.at[k % 2])
            theirs = _remote(src.at[pl.ds((1 - c) * half, half), :], land_hbm.at[g // 2, :, cols],
                             send_sems, recv_sems, k, (x, y, 1 - c))
            return mine, theirs

        for k in range(N_GROUPS):
            @pl.when((s == k) & (i == n_i - 1))
            def _(k=k):
                if k > 0:
                    mine, theirs = copies(k - 1)
                    mine.wait()
                    theirs.wait_send()
                mine, theirs = copies(k)
                mine.start()
                theirs.start()
                if k == N_GROUPS - 1:
                    mine.wait()
                    theirs.wait_send()
                    for kk in range(N_GROUPS):
                        copies(kk)[1].wait_recv()
                    for cp in carried:
                        cp.wait()

    hbm = pl.BlockSpec(memory_space=pl.ANY)
    more = scatter.plumbing(first_operand=2, first_output=2)
    return pl.pallas_call(
        body, name="inproj_dw_exchange",
        grid=(N_GROUPS, n_i),
        in_specs=[pl.BlockSpec((d, tm), lambda s, i: (0, i)),
                  pl.BlockSpec((None, tm, D_MODEL), lambda s, i: (s, i, 0))] + more[1],
        out_specs=[hbm, hbm] + more[2],
        out_shape=[jax.ShapeDtypeStruct((N_SHARDS, d, 2 * D_MODEL), F32),
                   jax.ShapeDtypeStruct((N_SHARDS, half, 2 * D_MODEL), F32)] + more[3],
        scratch_shapes=[pltpu.VMEM((2, d, D_MODEL), F32), pltpu.SemaphoreType.DMA((2,)),
                        pltpu.SemaphoreType.DMA((N_GROUPS,)), pltpu.SemaphoreType.DMA((N_GROUPS,))] + more[4],
        input_output_aliases=more[5],
        compiler_params=_params(("arbitrary", "arbitrary")),
    )(h_t, dz, *more[0])


def _inproj_dx(dz, w_all, x2d, dx2, norm_g, scatter=None):
    tokens, d = x2d.shape
    tm = min(256, tokens)
    n_tiles = tokens // tm

    def body(dz_ref, w_hbm, x_ref, dx2_ref, g_ref, *rest):
        if scatter:
            n_in, n_out = scatter.n_in, scatter.n_out
            dx_ref, dg_ref = rest[n_in:n_in + 2]
            w_res = rest[n_in + 2 + n_out]
            copies = scatter.copies(rest[:n_in], rest[n_in + 2:n_in + 2 + n_out], rest[n_in + 3 + n_out:])
        else:
            dx_ref, dg_ref, w_res = rest
            copies = []
        i = pl.program_id(0)

        @pl.when(i == 0)
        def _():
            for cp in copies:
                cp.start()
            for slot, g in enumerate(_SLOT_TO_GROUP):
                pltpu.sync_copy(w_hbm.at[g // 2, :, pl.ds((g % 2) * D_MODEL, D_MODEL)],
                                w_res.at[:, pl.ds(slot * D_MODEL, D_MODEL)])
            dg_ref[...] = jnp.zeros_like(dg_ref)

        dz_all = jnp.concatenate([dz_ref[s] for s in range(N_GROUPS)], axis=1)
        dh = jnp.transpose(_mm_nt(w_res[...], dz_all))
        x = x_ref[...]
        r = lax.rsqrt(jnp.mean(x * x, axis=-1, keepdims=True) + EPS)
        xn = x * r
        dg_ref[...] += jnp.sum(dh * xn, axis=0, keepdims=True)
        dxn = dh * g_ref[...]
        dx_ref[...] = r * (dxn - xn * jnp.mean(dxn * xn, axis=-1, keepdims=True)) + dx2_ref[...]

        @pl.when(i == n_tiles - 1)
        def _():
            for cp in copies:
                cp.wait()

    tile = pl.BlockSpec((tm, d), lambda i: (i, 0))
    hbm = pl.BlockSpec(memory_space=pl.ANY)
    in_specs = [pl.BlockSpec((N_GROUPS, tm, D_MODEL), lambda i: (0, i, 0)), hbm, tile, tile,
                pl.BlockSpec((1, d), lambda i: (0, 0))]
    out_specs = [tile, pl.BlockSpec((1, d), lambda i: (0, 0))]
    out_shape = [jax.ShapeDtypeStruct((tokens, d), F32), jax.ShapeDtypeStruct((1, d), F32)]
    scratch = [pltpu.VMEM((d, N_GROUPS * D_MODEL), _MXU_DTYPE)]
    operands, aliases = [dz, w_all, x2d, dx2, norm_g], {}
    if scatter:
        more = scatter.plumbing(first_operand=len(operands), first_output=len(out_shape))
        operands, in_specs, out_specs = operands + more[0], in_specs + more[1], out_specs + more[2]
        out_shape, scratch, aliases = out_shape + more[3], scratch + more[4], more[5]
    return pl.pallas_call(
        body, name="inproj_dx", grid=(n_tiles,), in_specs=in_specs, out_specs=out_specs, out_shape=out_shape,
        scratch_shapes=scratch, input_output_aliases=aliases,
        compiler_params=_params(("arbitrary",)),
    )(*operands)


def _row_tile(rows, cols, itemsize=4, budget=2 * 1024 * 1024):
    tr = rows
    while tr * cols * itemsize > budget and tr % 16 == 0:
        tr //= 2
    return tr


def _cast_into_slot(a, chip, dtype, name):
    rows, cols = a.shape
    tr = _row_tile(rows, cols)

    def body(chip_ref, a_ref, o_ref):
        del chip_ref
        o_ref[...] = a_ref[...].astype(dtype)

    grid_spec = pltpu.PrefetchScalarGridSpec(
        num_scalar_prefetch=1, grid=(rows // tr,),
        in_specs=[pl.BlockSpec((tr, cols), lambda i, chip_ref: (i, 0))],
        out_specs=pl.BlockSpec((None, tr, cols), lambda i, chip_ref: (chip_ref[0], i, 0)))
    return pl.pallas_call(body, name=name, grid_spec=grid_spec,
                          out_shape=jax.ShapeDtypeStruct((N_SHARDS, rows, cols), dtype),
                          compiler_params=_params(("arbitrary",)))(chip, a)


def _sum_slots(stack, name):
    n, rows, cols = stack.shape
    tr = _row_tile(rows, cols * n)

    def body(s_ref, o_ref):
        total = s_ref[0].astype(F32)
        for k in range(1, n):
            total = total + s_ref[k].astype(F32)
        o_ref[...] = total

    return pl.pallas_call(body, name=name, grid=(rows // tr,),
                          in_specs=[pl.BlockSpec((n, tr, cols), lambda i: (0, i, 0))],
                          out_specs=pl.BlockSpec((tr, cols), lambda i: (i, 0)),
                          out_shape=jax.ShapeDtypeStruct((rows, cols), F32),
                          compiler_params=_params(("parallel",)))(stack)


def _add_half(full, landed, place, name):
    n, rows, cols = full.shape
    half = rows // 2
    tr = _row_tile(half, cols)
    nb = half // tr

    def body(place_ref, a_ref, b_ref, o_ref, own_ref):
        total = (a_ref[...] + b_ref[...]).astype(_MXU_DTYPE)
        o_ref[...] = total

        @pl.when(pl.program_id(1) == place_ref[1])
        def _():
            own_ref[...] = total

    grid_spec = pltpu.PrefetchScalarGridSpec(
        num_scalar_prefetch=1, grid=(nb, n),
        in_specs=[pl.BlockSpec((None, tr, cols), lambda i, j, place_ref: (j, place_ref[0] * nb + i, 0)),
                  pl.BlockSpec((None, tr, cols), lambda i, j, place_ref: (j, i, 0))],
        out_specs=[pl.BlockSpec((None, tr, cols), lambda i, j, place_ref: (j, i, 0)),
                   pl.BlockSpec((None, tr, cols), lambda i, j, place_ref: (place_ref[1], i, 0))])
    shape = jax.ShapeDtypeStruct((n, half, cols), _MXU_DTYPE)
    return pl.pallas_call(body, name=name, grid_spec=grid_spec, out_shape=[shape, shape],
                          compiler_params=_params(("parallel", "arbitrary")))(place, full, landed)


def _adamw_update(w, grad, m, v):
    c1 = 1.0 - ADAM_B1 ** ADAM_STEP
    c2 = 1.0 - ADAM_B2 ** ADAM_STEP
    nm = ADAM_B1 * m + (1.0 - ADAM_B1) * grad
    nv = ADAM_B2 * v + (1.0 - ADAM_B2) * (grad * grad)
    return (-ADAM_LR) * ((nm / c1) / (jnp.sqrt(nv / c2) + ADAM_EPS) + ADAM_WD * w), nm, nv


def _adamw(w, g, m, v, name):
    rows, cols = w.shape
    tr = _row_tile(rows, cols, budget=1024 * 1024)

    def body(w_ref, g_ref, m_ref, v_ref, d_ref, nm_ref, nv_ref):
        d_ref[...], nm_ref[...], nv_ref[...] = _adamw_update(w_ref[...], g_ref[...], m_ref[...], v_ref[...])

    spec = pl.BlockSpec((tr, cols), lambda i: (i, 0))
    shape = jax.ShapeDtypeStruct((rows, cols), F32)
    return pl.pallas_call(body, name=name, grid=(rows // tr,), in_specs=[spec] * 4, out_specs=[spec] * 3,
                          out_shape=[shape] * 3, compiler_params=_params(("parallel",)))(w, g, m, v)


def _adamw_halves(w, g_mine, g_sibling, m, v, core, name):
    rows, cols = w.shape
    half = rows // 2
    tr = _row_tile(half, cols, budget=1024 * 1024)
    nb = half // tr

    def body(core_ref, w_ref, gm_ref, gs_ref, m_ref, v_ref, g_ref, d_ref, nm_ref, nv_ref):
        mine = pl.program_id(0) // nb == core_ref[0]
        grad = jnp.where(mine, gm_ref[...], gs_ref[...])
        g_ref[...] = grad
        d_ref[...], nm_ref[...], nv_ref[...] = _adamw_update(w_ref[...], grad, m_ref[...], v_ref[...])

    spec = pl.BlockSpec((tr, cols), lambda i, core_ref: (i, 0))
    half_spec = pl.BlockSpec((tr, cols), lambda i, core_ref: (i % nb, 0))
    grid_spec = pltpu.PrefetchScalarGridSpec(num_scalar_prefetch=1, grid=(rows // tr,),
                                             in_specs=[spec, half_spec, half_spec, spec, spec], out_specs=[spec] * 4)
    shape = jax.ShapeDtypeStruct((rows, cols), F32)
    return pl.pallas_call(body, name=name, grid_spec=grid_spec, out_shape=[shape] * 4,
                          compiler_params=_params(("parallel",)))(core, w, g_mine, g_sibling, m, v)


def _local_step(x, loss_target, w_all, pa, pb, wo, conv_w, b_merge, conv_b, rg_wx, rg_bx, rg_wa, rg_ba,
                rg_lambda, hg_lb_logits, hg_norm_g, norm_g, final_norm_g, gather=None, reduction=None):
    batch, seq, d = x.shape
    x2d = x.reshape(batch * seq, d)
    tgt2d = loss_target.reshape(batch * seq, d)
    if gather is None:
        z, h_t = _inproj_fwd(x2d, norm_g, w_all)
    else:
        z, h_t, (w_all, pa, pb, wo), cw_all = _inproj_fwd_gather(x2d, norm_g, *gather)
        pa, pb, wo = (t.reshape(d, d) for t in (pa, pb, wo))
        conv_w = jnp.transpose(cw_all, (1, 0, 2)).reshape(CONV_WIDTH, d)
    lru = (conv_w, conv_b, rg_wx, rg_bx, rg_wa, rg_ba, rg_lambda)
    ya, hl, kept = _branch_a_fwd(z, *lru, batch, seq)
    yb, states, kept_b = _branch_b_fwd(z, hg_lb_logits, hg_norm_g, batch, seq)
    dya, dyb, dx2, dz, loss, d_final_g, d_b_merge, d_pa, d_pb, d_wo = _merge_tail(
        ya, yb, z, x2d, tgt2d, b_merge, final_norm_g, pa, pb, wo)
    dz, d_lb_logits, d_hg_g = _branch_b_bwd(z, states, kept_b, dyb, dz, hg_lb_logits, hg_norm_g, batch, seq)
    dz, d_conv_w, d_conv_b, d_wx, d_bx, d_wa, d_ba, d_lam = _branch_a_bwd(z, hl, kept, dya, dz, *lru, batch, seq)
    small = dict(b_merge=d_b_merge, conv_w=d_conv_w, conv_b=d_conv_b, rg_wx=d_wx, rg_bx=d_bx, rg_wa=d_wa,
                 rg_ba=d_ba, rg_lambda=d_lam, hg_lb_logits=d_lb_logits, hg_norm_g=d_hg_g,
                 norm_g=jnp.zeros((1, d), F32), final_norm_g=d_final_g)
    if reduction is None:
        big = (_inproj_dw(h_t, dz), d_pa, d_pb, d_wo)
        grad_x, small["norm_g"] = _inproj_dx(dz, w_all, x2d, dx2, norm_g)
        return loss[0, 0], grad_x.reshape(batch, seq, d), big, small
    first, second = reduction
    d_w_in, landed_w_in, *scattered_first = _inproj_dw_exchange(h_t, dz, first((d_pa, d_pb, d_wo), small))
    grad_x, d_norm_g, *scattered_second = _inproj_dx(dz, w_all, x2d, dx2, norm_g, scatter=second(d_w_in, landed_w_in))
    return loss[0, 0], grad_x.reshape(batch, seq, d), d_norm_g, (scattered_first, scattered_second)


_SMALL_ORDER = ("b_merge", "conv_w", "conv_b", "rg_wx", "rg_bx", "rg_wa", "rg_ba", "rg_lambda", "hg_lb_logits",
                "hg_norm_g", "norm_g", "final_norm_g")
N_DEV = 8
PIECE_ROWS = 272


def _pack_small(tree):
    flat = jnp.concatenate([tree[k].reshape(-1) for k in _SMALL_ORDER])
    flat = jnp.pad(flat, (0, N_DEV * PIECE_ROWS * LANES - flat.shape[0]))
    return flat.reshape(N_DEV * PIECE_ROWS, LANES)


def _unpack_small(packed, like):
    flat = packed.reshape(-1)
    out, pos = {}, 0
    for k in _SMALL_ORDER:
        n = like[k].size
        out[k] = flat[pos:pos + n].reshape(like[k].shape)
        pos += n
    return out


def _mesh_position():
    x, y, c = lax.axis_index("x"), lax.axis_index("y"), lax.axis_index("c")
    other_chips = [(1 - x, y), (x, 1 - y), (1 - x, 1 - y)]
    return x, y, c, other_chips


def _other_devices(x, y, c):
    flips = [(fx, fy, fc) for fx in (0, 1) for fy in (0, 1) for fc in (0, 1) if (fx, fy, fc) != (0, 0, 0)]
    return [(jnp.where(fx, 1 - x, x), jnp.where(fy, 1 - y, y), jnp.where(fc, 1 - c, c)) for fx, fy, fc in flips]


def _remote(src, dst, send_sems, recv_sems, k, device):
    return pltpu.make_async_remote_copy(src_ref=src, dst_ref=dst, send_sem=send_sems.at[k], recv_sem=recv_sems.at[k],
                                        device_id=device, device_id_type=MESH)


def _exchange_halves(bigs, small):
    n_big = len(bigs)
    n_sem = n_big + N_DEV - 1

    def body(*refs):
        srcs, small_src = refs[:n_big], refs[n_big]
        outs, small_out = refs[n_big + 1:2 * n_big + 1], refs[2 * n_big + 1]
        send_sems, recv_sems, local_sem = refs[2 * n_big + 2:]
        x, y, c, _ = _mesh_position()
        me, sibling = 4 * x + 2 * y + c, (x, y, 1 - c)
        mine = pltpu.make_async_copy(small_src.at[pl.ds(me * PIECE_ROWS, PIECE_ROWS), :], small_out.at[me], local_sem)
        mine.start()
        copies = []
        for a in range(n_big):
            hs = srcs[a].shape[1] // 2
            copies.append(_remote(srcs[a].at[:, pl.ds((1 - c) * hs, hs), :], outs[a], send_sems, recv_sems, a, sibling))
        for k, (px, py, pc) in enumerate(_other_devices(x, y, c)):
            piece = small_src.at[pl.ds((4 * px + 2 * py + pc) * PIECE_ROWS, PIECE_ROWS), :]
            copies.append(_remote(piece, small_out.at[me], send_sems, recv_sems, n_big + k, (px, py, pc)))
        for cp in copies:
            cp.start()
        for cp in copies:
            cp.wait()
        mine.wait()

    hbm = pl.BlockSpec(memory_space=pl.ANY)
    out_shape = [jax.ShapeDtypeStruct((g.shape[0], g.shape[1] // 2, g.shape[2]), F32) for g in bigs]
    out_shape.append(jax.ShapeDtypeStruct((N_DEV, PIECE_ROWS, LANES), F32))
    return pl.pallas_call(
        body, name="exchange_halves",
        in_specs=[hbm] * (n_big + 1), out_specs=[hbm] * (n_big + 1), out_shape=out_shape,
        scratch_shapes=[pltpu.SemaphoreType.DMA((n_sem,)), pltpu.SemaphoreType.DMA((n_sem,)), pltpu.SemaphoreType.DMA],
    )(*bigs, small)


class _Scatter:
    def __init__(self, bigs, by_chip, small=None):
        self.bigs, self.by_chip, self.small = list(bigs), list(by_chip), small
        self.n_big = len(self.bigs)
        self.n_in = 2 * self.n_big + (small is not None)
        self.n_out = self.n_big + (small is not None)
        self.n_scratch = 2 + (small is not None)

    def plumbing(self, first_operand, first_output):
        hbm = pl.BlockSpec(memory_space=pl.ANY)
        n_sem = 3 * self.n_big + (N_DEV - 1 if self.small is not None else 0)
        operands = self.bigs + self.by_chip + ([self.small] if self.small is not None else [])
        out_shapes = [jax.ShapeDtypeStruct(g.shape, g.dtype) for g in self.by_chip]
        scratch = [pltpu.SemaphoreType.DMA((n_sem,)), pltpu.SemaphoreType.DMA((n_sem,))]
        if self.small is not None:
            out_shapes.append(jax.ShapeDtypeStruct((N_DEV, PIECE_ROWS, LANES), F32))
            scratch.append(pltpu.SemaphoreType.DMA)
        aliases = {first_operand + self.n_big + a: first_output + a for a in range(self.n_big)}
        return operands, [hbm] * self.n_in, [hbm] * self.n_out, out_shapes, scratch, aliases

    def copies(self, in_refs, out_refs, scratch_refs):
        srcs, outs = in_refs[:self.n_big], out_refs[:self.n_big]
        send_sems, recv_sems = scratch_refs[:2]
        x, y, c, chips = _mesh_position()
        chip, me = 2 * x + y, 4 * x + 2 * y + c
        copies = []
        for a in range(self.n_big):
            for j, (cx, cy) in enumerate(chips):
                copies.append(_remote(srcs[a].at[2 * cx + cy], outs[a].at[chip], send_sems, recv_sems, 3 * a + j,
                                      (cx, cy, c)))
        if self.small is not None:
            small_src, small_out = in_refs[2 * self.n_big], out_refs[self.n_big]
            copies.append(pltpu.make_async_copy(small_src, small_out.at[me], scratch_refs[2]))
            for k, peer in enumerate(_other_devices(x, y, c)):
                copies.append(_remote(small_src, small_out.at[me], send_sems, recv_sems, 3 * self.n_big + k, peer))
        return copies


def _swap_halves(halves, vec):
    n_big = len(halves)

    def body(*refs):
        srcs, vec_src = refs[:n_big], refs[n_big]
        outs, vec_out = refs[n_big + 1:2 * n_big + 1], refs[2 * n_big + 1]
        send_sems, recv_sems, local_sem = refs[2 * n_big + 2:]
        x, y, c, _ = _mesh_position()
        me = 4 * x + 2 * y + c
        copies = [pltpu.make_async_copy(vec_src, vec_out.at[me], local_sem)]
        copies += [_remote(srcs[a], outs[a], send_sems, recv_sems, a, (x, y, 1 - c)) for a in range(n_big)]
        copies += [_remote(vec_src, vec_out.at[me], send_sems, recv_sems, n_big + k, peer)
                   for k, peer in enumerate(_other_devices(x, y, c))]
        for cp in copies:
            cp.start()
        for cp in copies:
            cp.wait()

    hbm = pl.BlockSpec(memory_space=pl.ANY)
    n_sem = n_big + N_DEV - 1
    return pl.pallas_call(
        body, name="swap_halves",
        in_specs=[hbm] * (n_big + 1), out_specs=[hbm] * (n_big + 1),
        out_shape=[jax.ShapeDtypeStruct(h.shape, F32) for h in halves] + [jax.ShapeDtypeStruct((N_DEV,) + vec.shape, F32)],
        scratch_shapes=[pltpu.SemaphoreType.DMA((n_sem,)), pltpu.SemaphoreType.DMA((n_sem,)), pltpu.SemaphoreType.DMA],
    )(*halves, vec)


def kernel(x, w_in, b_merge, conv_w, conv_b, rg_wx, rg_bx, rg_wa, rg_ba, rg_lambda, hg_lb_logits, hg_norm_g, proj_a, proj_b, w_out, norm_g, final_norm_g, loss_target, m_w_in, m_b_merge, m_conv_w, m_conv_b, m_rg_wx, m_rg_bx, m_rg_wa, m_rg_ba, m_rg_lambda, m_hg_lb_logits, m_hg_norm_g, m_proj_a, m_proj_b, m_w_out, m_norm_g, m_final_norm_g, v_w_in, v_b_merge, v_conv_w, v_conv_b, v_rg_wx, v_rg_bx, v_rg_wa, v_rg_ba, v_rg_lambda, v_hg_lb_logits, v_hg_norm_g, v_proj_a, v_proj_b, v_w_out, v_norm_g, v_final_norm_g):
    d = D_MODEL
    weights = dict(w_in=w_in, b_merge=b_merge, conv_w=conv_w, conv_b=conv_b, rg_wx=rg_wx, rg_bx=rg_bx, rg_wa=rg_wa,
                   rg_ba=rg_ba, rg_lambda=rg_lambda, hg_lb_logits=hg_lb_logits, hg_norm_g=hg_norm_g, proj_a=proj_a,
                   proj_b=proj_b, w_out=w_out, norm_g=norm_g, final_norm_g=final_norm_g)
    m = dict(w_in=m_w_in, b_merge=m_b_merge, conv_w=m_conv_w, conv_b=m_conv_b, rg_wx=m_rg_wx, rg_bx=m_rg_bx,
             rg_wa=m_rg_wa, rg_ba=m_rg_ba, rg_lambda=m_rg_lambda, hg_lb_logits=m_hg_lb_logits, hg_norm_g=m_hg_norm_g,
             proj_a=m_proj_a, proj_b=m_proj_b, w_out=m_w_out, norm_g=m_norm_g, final_norm_g=m_final_norm_g)
    v = dict(w_in=v_w_in, b_merge=v_b_merge, conv_w=v_conv_w, conv_b=v_conv_b, rg_wx=v_rg_wx, rg_bx=v_rg_bx,
             rg_wa=v_rg_wa, rg_ba=v_rg_ba, rg_lambda=v_rg_lambda, hg_lb_logits=v_hg_lb_logits, hg_norm_g=v_hg_norm_g,
             proj_a=v_proj_a, proj_b=v_proj_b, w_out=v_w_out, norm_g=v_norm_g, final_norm_g=v_final_norm_g)
    big_names = ("w_in", "proj_a", "proj_b", "w_out")

    core = lax.axis_index("c").astype(jnp.int32).reshape(1)
    chip = (2 * lax.axis_index("x") + lax.axis_index("y")).astype(jnp.int32)

    slotted = [_cast_into_slot(weights[k][0], chip.reshape(1), _MXU_DTYPE, f"cast_{k}") for k in big_names]
    conv_slotted = _cast_into_slot(conv_w[0], chip.reshape(1), F32, "slot_conv_w")

    small_shapes = {}

    place = jnp.concatenate([core, chip.reshape(1)])

    def reduce_proj_and_small(proj_grads, small_grads):
        small_shapes.update({k: t.shape for k, t in small_grads.items()})
        bigs = [g.reshape(N_SHARDS, d // N_SHARDS, d) for g in proj_grads]
        *landed, small_landed = _exchange_halves(bigs, _pack_small(small_grads))
        sums = [_add_half(g, l, place, f"add_half_{1 + a}") for a, (g, l) in enumerate(zip(bigs, landed))]
        return _Scatter([s[0] for s in sums], [s[1] for s in sums], _sum_slots(small_landed, "sum_small"))

    def reduce_w_in(d_w_in, landed):
        partial, own_slot = _add_half(d_w_in, landed, place, "add_half_0")
        return _Scatter([partial], [own_slot])

    loss_part, grad_x, d_norm_g, ((*by_chip_proj, small_all), by_chip_w_in) = _local_step(
        x, loss_target, None, None, None, None, None,
        b_merge, conv_b, rg_wx[0], rg_bx.reshape(1, d), rg_wa[0], rg_ba.reshape(1, d), rg_lambda, hg_lb_logits,
        hg_norm_g, norm_g, final_norm_g.reshape(1, d), gather=(slotted, conv_slotted, chip.reshape(1)),
        reduction=(reduce_proj_and_small, reduce_w_in))
    mine = [_sum_slots(s, f"sum_chips_{a}") for a, s in enumerate(by_chip_w_in + by_chip_proj)]
    late = jnp.concatenate([d_norm_g.reshape(SUBLANES, LANES), jnp.full((SUBLANES, LANES), loss_part, F32)])
    *theirs, late_parts = _swap_halves(mine, late)
    late_sum = _sum_slots(late_parts, "sum_late")
    loss = late_sum[SUBLANES, 0]
    small_red = _unpack_small(small_all, {k: jax.ShapeDtypeStruct(s, F32) for k, s in small_shapes.items()})
    small_red["norm_g"] = late_sum[:SUBLANES].reshape(1, d)

    grads, delta, new_m, new_v = {}, {}, {}, {}
    for k, g_mine, g_theirs in zip(big_names, mine, theirs):
        out = _adamw_halves(weights[k][0], g_mine, g_theirs, m[k][0], v[k][0], core, f"adamw_{k}")
        grads[k], delta[k], new_m[k], new_v[k] = (t.reshape(weights[k].shape) for t in out)
    cols = d // N_SHARDS
    g_conv = lax.dynamic_slice(small_red["conv_w"], (0, chip * cols), (CONV_WIDTH, cols))
    grads["conv_w"] = g_conv.reshape(conv_w.shape)
    dl, nm, nv = _adamw(conv_w[0], g_conv, m_conv_w[0], v_conv_w[0], "adamw_conv_w")
    delta["conv_w"], new_m["conv_w"], new_v["conv_w"] = (t.reshape(conv_w.shape) for t in (dl, nm, nv))
    rest = [k for k in _SMALL_ORDER if k != "conv_w"]
    like = {k: (weights[k] if k != "conv_w" else jnp.zeros((CONV_WIDTH, d), F32)) for k in _SMALL_ORDER}
    packs = [_pack_small({k: (t[k] if k != "conv_w" else like[k]) for k in _SMALL_ORDER}) for t in (weights, m, v)]
    g_pack = _pack_small({k: small_red[k].reshape(like[k].shape) for k in _SMALL_ORDER})
    outs = [_unpack_small(p, like) for p in _adamw(packs[0], g_pack, packs[1], packs[2], "adamw_small")]
    for k in rest:
        grads[k] = small_red[k].reshape(weights[k].shape)
        delta[k], new_m[k], new_v[k] = outs[0][k], outs[1][k], outs[2][k]

    order = ("w_in", "b_merge", "conv_w", "conv_b", "rg_wx", "rg_bx", "rg_wa", "rg_ba", "rg_lambda", "hg_lb_logits",
             "hg_norm_g", "proj_a", "proj_b", "w_out", "norm_g", "final_norm_g")
    return (loss, grad_x, *[grads[k] for k in order], *[delta[k] for k in order], *[new_m[k] for k in order],
            *[new_v[k] for k in order])
```

```python
import functools

import jax
import jax.numpy as jnp
from jax import lax
from jax.experimental import pallas as pl
from jax.experimental.pallas import tpu as pltpu

F32 = jnp.float32
_MXU_DTYPE = jnp.bfloat16

D_MODEL = 1024
LANES = 128
SUBLANES = 8
N_BLK = D_MODEL // LANES
N_GROUPS = 8
N_SHARDS = 4
CONV_WIDTH = 4
LRU_C = 8.0
CHUNK = 64
CHUNKS_IN_FLIGHT = 16
HG_SCALE = float(LANES) ** -0.5
EPS = 1e-6
ADAM_LR, ADAM_B1, ADAM_B2, ADAM_EPS, ADAM_WD, ADAM_STEP = 0.001, 0.9, 0.999, 1e-08, 0.01, 10
VMEM_LIMIT = 56 * 1024 * 1024
VMEM_LIMIT_BIG = 60 * 1024 * 1024
MESH = pl.DeviceIdType.MESH

_SLOT_TO_GROUP = (2, 3, 4, 5, 0, 1, 6, 7)


def _group_of_slot(s):
    return jnp.where(s < 4, s + 2, jnp.where(s < 6, s - 4, s))


def _mm(a, b):
    return lax.dot_general(a.astype(_MXU_DTYPE), b.astype(_MXU_DTYPE), (((1,), (0,)), ((), ())),
                           preferred_element_type=F32)


def _mm_nt(a, b):
    return lax.dot_general(a.astype(_MXU_DTYPE), b.astype(_MXU_DTYPE), (((1,), (1,)), ((), ())),
                           preferred_element_type=F32)


def _mm_tn(a, b):
    return lax.dot_general(a.astype(_MXU_DTYPE), b.astype(_MXU_DTYPE), (((0,), (0,)), ((), ())),
                           preferred_element_type=F32)


def _sigmoid(x):
    return 0.5 * jnp.tanh(0.5 * x) + 0.5


def _log1p_pos(y):
    series = y * (1.0 - y * (0.5 - y * (1.0 / 3.0 - y * 0.25)))
    return jnp.where(y < 0.01, series, jnp.log(1.0 + y))


def _softplus(x):
    return jnp.maximum(x, 0.0) + _log1p_pos(jnp.exp(-jnp.abs(x)))


def _shift_down(x, n):
    rolled = pltpu.roll(x, n, 0)
    edge = SUBLANES if (n < SUBLANES and x.shape[0] > SUBLANES) else x.shape[0]
    rows = lax.broadcasted_iota(jnp.int32, (edge, x.shape[1]), 0)
    head = jnp.where(rows >= n, rolled[:edge], 0.0)
    return head if edge == x.shape[0] else jnp.concatenate([head, rolled[edge:]], axis=0)


def _shift_up(x, n):
    size = x.shape[0]
    rolled = pltpu.roll(x, size - n, 0)
    edge = SUBLANES if (n < SUBLANES and size > SUBLANES) else size
    rows = lax.broadcasted_iota(jnp.int32, (edge, x.shape[1]), 0)
    tail = jnp.where(rows < edge - n, rolled[size - edge:], 0.0)
    return tail if edge == size else jnp.concatenate([rolled[:size - edge], tail], axis=0)


def _params(dims, vmem=VMEM_LIMIT):
    return pltpu.CompilerParams(dimension_semantics=dims, vmem_limit_bytes=vmem)


def _load_w_in_by_slot(w_hbm, w_res):
    for slot, g in enumerate(_SLOT_TO_GROUP):
        pltpu.sync_copy(w_hbm.at[g // 2, :, pl.ds((g % 2) * D_MODEL, D_MODEL)], w_res.at[slot])


def _inproj_fwd(x2d, norm_g, w_all):
    tokens, d = x2d.shape
    tm = min(512, tokens)

    def body(x_ref, g_ref, w_hbm, z_ref, ht_ref, h_scr, w_res):
        @pl.when((pl.program_id(0) == 0) & (pl.program_id(1) == 0))
        def _():
            _load_w_in_by_slot(w_hbm, w_res)

        @pl.when(pl.program_id(1) == 0)
        def _():
            x = x_ref[...]
            r = lax.rsqrt(jnp.mean(x * x, axis=-1, keepdims=True) + EPS)
            h = (x * r) * g_ref[...]
            h_scr[...] = h.astype(_MXU_DTYPE)
            ht_ref[...] = jnp.transpose(h).astype(_MXU_DTYPE)

        z_ref[...] = _mm(h_scr[...], w_res[pl.program_id(1)])

    return pl.pallas_call(
        body, name="inproj_fwd",
        grid=(tokens // tm, N_GROUPS),
        in_specs=[pl.BlockSpec((tm, d), lambda i, s: (i, 0)),
                  pl.BlockSpec((1, d), lambda i, s: (0, 0)),
                  pl.BlockSpec(memory_space=pl.ANY)],
        out_specs=[pl.BlockSpec((None, tm, D_MODEL), lambda i, s: (s, i, 0)),
                   pl.BlockSpec((d, tm), lambda i, s: (0, i))],
        out_shape=[jax.ShapeDtypeStruct((N_GROUPS, tokens, D_MODEL), F32),
                   jax.ShapeDtypeStruct((d, tokens), _MXU_DTYPE)],
        scratch_shapes=[pltpu.VMEM((tm, d), _MXU_DTYPE), pltpu.VMEM((N_GROUPS, d, D_MODEL), _MXU_DTYPE)],
        compiler_params=_params(("arbitrary", "arbitrary")),
    )(x2d, norm_g, w_all)


def _slot_of_group(g):
    return jnp.where(g < 2, g + 4, jnp.where(g < 6, g - 2, g))


def _inproj_fwd_gather(x2d, norm_g, slotted, conv_slotted, chip):
    tokens, d = x2d.shape
    tm = min(512, tokens)
    n_tiles = tokens // tm
    n_big = len(slotted)
    n_sem = 6 * (n_big + 1) + 3
    last_pass = N_GROUPS - 1

    def shard_of(k, chip_id):
        x, y = chip_id // 2, chip_id % 2
        return 2 * jnp.where(k % 2 == 1, 1 - x, x) + jnp.where(k // 2 == 1, 1 - y, y)

    def body(chip_ref, x_ref, g_ref, *rest):
        bufs, cw = rest[n_big + 1:2 * n_big + 1], rest[2 * n_big + 1]
        z_ref, ht_ref = rest[2 * n_big + 2:2 * n_big + 4]
        h_all, slab, send_sems, recv_sems, slab_sems = rest[2 * n_big + 4:]
        del chip_ref
        p, i = pl.program_id(0), pl.program_id(1)
        x, y, c, chips = _mesh_position()
        me, sibling = 2 * x + y, (x, y, 1 - c)

        pieces = [(0, 0), (0, 1)] + [(a, None) for a in range(1, n_big)]

        def half(piece, slot, which):
            a, q = pieces[piece]
            hs = bufs[a].shape[1] // 2
            cols = slice(None) if q is None else pl.ds(q * D_MODEL, D_MODEL)
            return bufs[a].at[slot, pl.ds(which * hs, hs), cols]

        def send(piece, j):
            mine = half(piece, me, c)
            return _remote(mine, mine, send_sems, recv_sems, 6 * piece + j, (chips[j][0], chips[j][1], c))

        def arrival(piece, j):
            landed = half(piece, 2 * chips[j][0] + chips[j][1], c)
            return _remote(landed, landed, send_sems, recv_sems, 6 * piece + j, (chips[j][0], chips[j][1], c))

        def passed_on(piece, j, which):
            landed = half(piece, 2 * chips[j][0] + chips[j][1], which)
            return _remote(landed, landed, send_sems, recv_sems, 6 * piece + 3 + j, sibling)

        def conv_copy(j, slot):
            return _remote(cw.at[slot], cw.at[slot], send_sems, recv_sems, 6 * len(pieces) + j,
                           (chips[j][0], chips[j][1], c))

        def land(piece, j):
            arrival(piece, j).wait_recv()
            passed_on(piece, j, c).start()
            passed_on(piece, j, 1 - c).wait_recv()

        def slab_copy(pv):
            src = bufs[0].at[shard_of(pv // 2, me), :, pl.ds((pv % 2) * D_MODEL, D_MODEL)]
            return pltpu.make_async_copy(src, slab.at[pv % 2], slab_sems.at[pv % 2])

        @pl.when((p == 0) & (i == 0))
        def _():
            for q in range(2):
                send(q, 0).start()
                send(q, 1).start()
            slab_copy(0).start()

        @pl.when(i == 0)
        def _():
            for pv in range(N_GROUPS):
                @pl.when(p == pv)
                def _(pv=pv):
                    slab_copy(pv).wait()

        rows = pl.ds(pl.multiple_of(i * tm, tm), tm)

        @pl.when(p == 0)
        def _():
            xt = x_ref[...]
            r = lax.rsqrt(jnp.mean(xt * xt, axis=-1, keepdims=True) + EPS)
            h = (xt * r) * g_ref[...]
            h_all[rows, :] = h.astype(_MXU_DTYPE)
            ht_ref[...] = jnp.transpose(h).astype(_MXU_DTYPE)

        z_ref[...] = _mm(h_all[rows, :], slab[p % 2])

        def relay(q):
            landed = half(q, 2 * chips[q][0] + chips[q][1], c)
            to = chips[1 - q]
            return _remote(landed, landed, send_sems, recv_sems, 6 * q + 2, (to[0], to[1], c))

        landings = {1: [(0, 0)], 2: [(1, 0), (1, 1)], 3: [(0, 1)], 5: [(0, 2)], 6: [(1, 2)]}

        def end_of_pass(pv):
            for q, j in landings.get(pv - 1, []):
                land(q, j)
                if j == q:
                    relay(q).start()
            if pv - 1 == 3:
                for piece in range(2, len(pieces)):
                    for jj in range(3):
                        send(piece, jj).start()
                for jj in range(3):
                    conv_copy(jj, me).start()
            slab_copy(pv).start()

        @pl.when(i == n_tiles - 1)
        def _():
            for pv in range(1, N_GROUPS):
                pl.when(p == pv - 1)(functools.partial(end_of_pass, pv))

        @pl.when((p == last_pass) & (i == n_tiles - 1))
        def _():
            for piece in range(2, len(pieces)):
                for j in range(3):
                    land(piece, j)
            for j in range(3):
                conv_copy(j, 2 * chips[j][0] + chips[j][1]).wait_recv()
            for piece in range(len(pieces)):
                for j in range(3):
                    (relay(piece) if (piece < 2 and j == 2) else send(piece, j)).wait_send()
                    passed_on(piece, j, c).wait_send()
            for j in range(3):
                conv_copy(j, me).wait_send()

    def z_index(p, i, chip_ref):
        g = 2 * shard_of(p // 2, chip_ref[0]) + p % 2
        return (_slot_of_group(g), i, 0)

    def first_pass_tile(p, i, chip_ref):
        return jnp.where(p == 0, i, n_tiles - 1)

    hbm = pl.BlockSpec(memory_space=pl.ANY)
    operands = list(slotted) + [conv_slotted]
    grid_spec = pltpu.PrefetchScalarGridSpec(
        num_scalar_prefetch=1, grid=(N_GROUPS, n_tiles),
        in_specs=[pl.BlockSpec((tm, d), lambda p, i, chip_ref: (first_pass_tile(p, i, chip_ref), 0)),
                  pl.BlockSpec((1, d), lambda p, i, chip_ref: (0, 0))] + [hbm] * (n_big + 1),
        out_specs=[hbm] * (n_big + 1) + [pl.BlockSpec((None, tm, D_MODEL), z_index),
                                         pl.BlockSpec((d, tm), lambda p, i, chip_ref: (0, first_pass_tile(p, i, chip_ref)))],
        scratch_shapes=[pltpu.VMEM((tokens, d), _MXU_DTYPE), pltpu.VMEM((2, d, D_MODEL), _MXU_DTYPE),
                        pltpu.SemaphoreType.DMA((n_sem,)), pltpu.SemaphoreType.DMA((n_sem,)),
                        pltpu.SemaphoreType.DMA((2,))])
    out = pl.pallas_call(
        body, name="inproj_fwd_gather", grid_spec=grid_spec,
        out_shape=[jax.ShapeDtypeStruct(a.shape, a.dtype) for a in operands]
        + [jax.ShapeDtypeStruct((N_GROUPS, tokens, D_MODEL), F32), jax.ShapeDtypeStruct((d, tokens), _MXU_DTYPE)],
        input_output_aliases={3 + a: a for a in range(n_big + 1)},
        compiler_params=_params(("arbitrary", "arbitrary")),
    )(chip, x2d, norm_g, *operands)
    return out[n_big + 1], out[n_big + 2], out[:n_big], out[n_big]


def _lane_blocks(x):
    return [x[:, k * LANES:(k + 1) * LANES] for k in range(x.shape[1] // LANES)]


def _block_diag(x, w_ref, transposed=False):
    mm = _mm_nt if transposed else _mm
    return jnp.concatenate([mm(xk, w_ref[k]) for k, xk in enumerate(_lane_blocks(x))], axis=1)


def _lru_gates(xa, cw_ref, cb_ref, wx_ref, bx_ref, wa_ref, ba_ref, lam_ref):
    xc = (cb_ref[...] + cw_ref[3:4, :] * xa + cw_ref[2:3, :] * _shift_down(xa, 1)
          + cw_ref[1:2, :] * _shift_down(xa, 2) + cw_ref[0:1, :] * _shift_down(xa, 3))
    gi = _sigmoid(_block_diag(xc, wx_ref) + bx_ref[...])
    gr = _sigmoid(_block_diag(xc, wa_ref) + ba_ref[...])
    sp = _softplus(-lam_ref[...])
    a, mult, inv_mult = _lru_decay(gr, sp)
    return xc, gi, gr, sp, a, mult, inv_mult


def _lru_decay(gr, sp):
    log_a = (-LRU_C) * gr * sp
    a = jnp.exp(log_a)
    y = 2.0 * log_a
    mult_sq = jnp.where(y > -1e-3, -y * (1.0 + 0.5 * y), 1.0 - a * a)
    inv_mult = lax.rsqrt(jnp.maximum(mult_sq, 1e-37))
    return a, mult_sq * inv_mult, inv_mult


def _tile_rows(width):
    return lax.broadcasted_iota(jnp.int32, (SUBLANES, width), 0)


def _scan_forward(a_scr, u_scr, h_scr, seq):
    width = a_scr.shape[1]
    rows = _tile_rows(width)

    def tile(j, carry):
        sl = pl.ds(pl.multiple_of(j * SUBLANES, SUBLANES), SUBLANES)
        a = a_scr[sl, :]
        u = u_scr[sl, :]
        for d in (1, 2, 4):
            keep = rows >= d
            a_sh = jnp.where(keep, pltpu.roll(a, d, 0), 1.0)
            u_sh = jnp.where(keep, pltpu.roll(u, d, 0), 0.0)
            u = a * u_sh + u
            a = a * a_sh
        h = u + a * carry
        h_scr[sl, :] = h
        return jnp.broadcast_to(h[SUBLANES - 1:SUBLANES, :], (SUBLANES, width))

    lax.fori_loop(0, seq // SUBLANES, tile, jnp.zeros((SUBLANES, width), F32))


def _scan_backward(c_scr, d_scr, g_scr, seq):
    width = c_scr.shape[1]
    rows = _tile_rows(width)
    n_tiles = seq // SUBLANES

    def tile(jj, carry):
        j = n_tiles - 1 - jj
        sl = pl.ds(pl.multiple_of(j * SUBLANES, SUBLANES), SUBLANES)
        c = c_scr[sl, :]
        g = d_scr[sl, :]
        for d in (1, 2, 4):
            keep = rows < SUBLANES - d
            c_sh = jnp.where(keep, pltpu.roll(c, SUBLANES - d, 0), 1.0)
            g_sh = jnp.where(keep, pltpu.roll(g, SUBLANES - d, 0), 0.0)
            g = c * g_sh + g
            c = c * c_sh
        g = g + c * carry
        g_scr[sl, :] = g
        return jnp.broadcast_to(g[0:1, :], (SUBLANES, width))

    lax.fori_loop(0, n_tiles, tile, jnp.zeros((SUBLANES, width), F32))


LRU_BLOCKS_PER_STEP = 2
LRU_LANES = LRU_BLOCKS_PER_STEP * LANES
LRU_STEPS = N_BLK // LRU_BLOCKS_PER_STEP


def _lru_param_specs(cb_axis):
    def pick(*ids):
        return ids[cb_axis]

    vec = pl.BlockSpec((1, LRU_LANES), lambda *ids: (0, pick(*ids)))
    mat = pl.BlockSpec((LRU_BLOCKS_PER_STEP, LANES, LANES), lambda *ids: (pick(*ids), 0, 0))
    return [pl.BlockSpec((CONV_WIDTH, LRU_LANES), lambda *ids: (0, pick(*ids))), vec, mat, vec, mat, vec, vec]


def _branch_a_fwd(z, conv_w, conv_b, wx, bx, wa, ba, lam, batch, seq):
    tokens = batch * seq

    def body(z_ref, cw_ref, cb_ref, wx_ref, bx_ref, wa_ref, ba_ref, lam_ref, ya_ref, hl_ref, kept_ref, a_scr, u_scr):
        xa = z_ref[0]
        ga = z_ref[1]
        xc, gi, gr, _, a, mult, _ = _lru_gates(xa, cw_ref, cb_ref, wx_ref, bx_ref, wa_ref, ba_ref, lam_ref)
        kept_ref[0], kept_ref[1], kept_ref[2] = xc, gi, gr
        a_scr[...] = a
        u_scr[...] = mult * gi * xc
        _scan_forward(a_scr, u_scr, hl_ref, seq)
        ya_ref[...] = (hl_ref[...] * (ga * _sigmoid(ga))).astype(_MXU_DTYPE)

    blk = pl.BlockSpec((seq, LRU_LANES), lambda b, c: (b, c))
    return pl.pallas_call(
        body, name="branch_a_fwd",
        grid=(batch, LRU_STEPS),
        in_specs=[pl.BlockSpec((2, seq, LRU_LANES), lambda b, c: (2, b, c))] + _lru_param_specs(1),
        out_specs=[blk, blk, pl.BlockSpec((3, seq, LRU_LANES), lambda b, c: (0, b, c))],
        out_shape=[jax.ShapeDtypeStruct((tokens, D_MODEL), _MXU_DTYPE), jax.ShapeDtypeStruct((tokens, D_MODEL), F32),
                   jax.ShapeDtypeStruct((3, tokens, D_MODEL), F32)],
        scratch_shapes=[pltpu.VMEM((seq, LRU_LANES), F32), pltpu.VMEM((seq, LRU_LANES), F32)],
        compiler_params=_params(("parallel", "parallel")),
    )(z, conv_w, conv_b, wx, bx, wa, ba, lam)


def _branch_a_bwd(z, hl, kept, dya, dz, conv_w, conv_b, wx, bx, wa, ba, lam, batch, seq):
    def body(z_ref, hl_ref, kept_ref, dya_ref, dz_in_ref, cw_ref, cb_ref, wx_ref, bx_ref, wa_ref, ba_ref, lam_ref,
             dz_ref, dcw_ref, dcb_ref, dwx_ref, dbx_ref, dwa_ref, dba_ref, dlam_ref, c_scr, d_scr):
        del dz_in_ref, cb_ref, bx_ref, ba_ref
        g_scr = d_scr
        xa = z_ref[0]
        ga = z_ref[1]
        hl = hl_ref[...]
        dya = dya_ref[...]
        xc, gi, gr = kept_ref[0], kept_ref[1], kept_ref[2]
        sp = _softplus(-lam_ref[...])
        a, mult, inv_mult = _lru_decay(gr, sp)
        sga = _sigmoid(ga)
        dz_ref[1] = (dya * hl * (sga * (1.0 + ga * (1.0 - sga)))).astype(_MXU_DTYPE)
        c_scr[...] = _shift_up(a, 1)
        d_scr[...] = dya * (ga * sga)
        _scan_backward(c_scr, d_scr, g_scr, seq)
        g = g_scr[...]
        da = g * _shift_down(hl, 1)
        dmult = g * gi * xc
        dgi = g * mult * xc
        dxc = g * mult * gi
        dlog_a = da * a - dmult * (a * a) * inv_mult
        dgr = dlog_a * (-LRU_C) * sp
        dsp = jnp.sum(dlog_a * gr, axis=0, keepdims=True) * (-LRU_C)
        dlam = -dsp * _sigmoid(-lam_ref[...])
        dpi = dgi * gi * (1.0 - gi)
        dpr = dgr * gr * (1.0 - gr)
        dxc = dxc + _block_diag(dpi, wx_ref, transposed=True) + _block_diag(dpr, wa_ref, transposed=True)
        dwx = jnp.stack([_mm_tn(xk, dk) for xk, dk in zip(_lane_blocks(xc), _lane_blocks(dpi))])
        dwa = jnp.stack([_mm_tn(xk, dk) for xk, dk in zip(_lane_blocks(xc), _lane_blocks(dpr))])
        dbx = jnp.sum(dpi, axis=0, keepdims=True)
        dba = jnp.sum(dpr, axis=0, keepdims=True)
        ahead = [dxc if k == CONV_WIDTH - 1 else _shift_up(dxc, CONV_WIDTH - 1 - k) for k in range(CONV_WIDTH)]
        dxa = sum(cw_ref[k:k + 1, :] * ahead[k] for k in range(CONV_WIDTH))
        dz_ref[0] = dxa.astype(_MXU_DTYPE)
        dcb = jnp.sum(dxc, axis=0, keepdims=True)
        dcw = [jnp.sum(ahead[k] * xa, axis=0, keepdims=True) for k in range(CONV_WIDTH)]

        @pl.when(pl.program_id(1) == 0)
        def _():
            for k in range(CONV_WIDTH):
                dcw_ref[k:k + 1, :] = dcw[k]
            dcb_ref[...] = dcb
            dwx_ref[...] = dwx
            dbx_ref[...] = dbx
            dwa_ref[...] = dwa
            dba_ref[...] = dba
            dlam_ref[...] = dlam

        @pl.when(pl.program_id(1) != 0)
        def _():
            for k in range(CONV_WIDTH):
                dcw_ref[k:k + 1, :] += dcw[k]
            dcb_ref[...] += dcb
            dwx_ref[...] += dwx
            dbx_ref[...] += dbx
            dwa_ref[...] += dwa
            dba_ref[...] += dba
            dlam_ref[...] += dlam

    tokens = batch * seq
    blk = pl.BlockSpec((seq, LRU_LANES), lambda c, b: (b, c))
    vec = pl.BlockSpec((1, LRU_LANES), lambda c, b: (0, c))
    mat = pl.BlockSpec((LRU_BLOCKS_PER_STEP, LANES, LANES), lambda c, b: (c, 0, 0))
    vec_shape = jax.ShapeDtypeStruct((1, D_MODEL), F32)
    mat_shape = jax.ShapeDtypeStruct((N_BLK, LANES, LANES), F32)
    return pl.pallas_call(
        body, name="branch_a_bwd",
        grid=(LRU_STEPS, batch),
        in_specs=[pl.BlockSpec((2, seq, LRU_LANES), lambda c, b: (2, b, c)), blk,
                  pl.BlockSpec((3, seq, LRU_LANES), lambda c, b: (0, b, c)), blk,
                  pl.BlockSpec(memory_space=pl.ANY)] + _lru_param_specs(0),
        out_specs=[pl.BlockSpec((2, seq, LRU_LANES), lambda c, b: (2, b, c)),
                   pl.BlockSpec((CONV_WIDTH, LRU_LANES), lambda c, b: (0, c)), vec, mat, vec, mat, vec, vec],
        out_shape=[jax.ShapeDtypeStruct((N_GROUPS, tokens, D_MODEL), _MXU_DTYPE),
                   jax.ShapeDtypeStruct((CONV_WIDTH, D_MODEL), F32), vec_shape, mat_shape, vec_shape, mat_shape,
                   vec_shape, vec_shape],
        scratch_shapes=[pltpu.VMEM((seq, LRU_LANES), F32)] * 2,
        input_output_aliases={4: 0},
        compiler_params=_params(("parallel", "arbitrary"), vmem=VMEM_LIMIT_BIG),
    )(z, hl, kept, dya, dz, conv_w, conv_b, wx, bx, wa, ba, lam)


def _chunk_masks(transposed=False):
    r = lax.broadcasted_iota(jnp.int32, (CHUNK, CHUNK), 0)
    c = lax.broadcasted_iota(jnp.int32, (CHUNK, CHUNK), 1)
    return r <= c if transposed else r >= c


def _row_blocks(seq, fn):
    block = min(256, seq)

    def trip(i, carry):
        fn(pl.ds(pl.multiple_of(i * block, block), block))
        return carry

    lax.fori_loop(0, seq // block, trip, 0)


def _hgrn_prepare(z_ref, lb_ref, f_scr, logf_scr, qh_scr, seq):
    lb = _sigmoid(lb_ref[0:1, :] - lb_ref[1:2, :])

    def block(rows):
        q = z_ref[0, rows, :]
        f = lb + (1.0 - lb) * _sigmoid(z_ref[1, rows, :])
        f_scr[rows, :] = f
        logf_scr[rows, :] = jnp.log(f)
        qh_scr[rows, :] = q * _sigmoid(q)

    _row_blocks(seq, block)
    return lb


def _cumsum_rows(x, reverse=False):
    shift = _shift_up if reverse else _shift_down
    d = 1
    while d < x.shape[0]:
        x = x + shift(x, d)
        d *= 2
    return x


def _lane_mean(x):
    return jnp.mean(x, axis=-1, keepdims=True)


def _token_contractions(lhs_scr, rhs_scr, out_ref, seq):
    rows_id = lax.broadcasted_iota(jnp.int32, (LANES, LANES), 0)

    def transposed(p):
        rows = pl.ds(pl.multiple_of(p * LANES, LANES), LANES)
        return jnp.transpose(lhs_scr[rows, :]).astype(_MXU_DTYPE), rhs_scr[rows, :]

    def contract(p, s):
        lhs_t, rhs = s
        return (_mm(lhs_t, jnp.where(rows_id < CHUNK, rhs, 0.0)), _mm(lhs_t, jnp.where(rows_id >= CHUNK, rhs, 0.0)))

    def store(p, out):
        out_ref[2 * p] = out[0]
        out_ref[2 * p + 1] = out[1]

    _independent_trips(seq // LANES, [transposed, contract], store)


def _chunk_rows(c):
    return pl.ds(pl.multiple_of(c * CHUNK, CHUNK), CHUNK)


def _chunk_terms(c, z_ref, f_scr, qh_scr, b_scr):
    rows = _chunk_rows(c)
    b = b_scr[rows, :]
    b_mid = b_scr[pl.ds(c * CHUNK + CHUNK // 2, 1), :]
    b_last = b_scr[pl.ds(c * CHUNK + CHUNK - 1, 1), :]
    qh = qh_scr[rows, :]
    k = 1.0 - f_scr[rows, :]
    v = z_ref[2, rows, :]
    e_q = jnp.exp(b - b_mid) * HG_SCALE
    e_k = jnp.exp(b_mid - b)
    e_qi = jnp.exp(b) * HG_SCALE
    e_ks = jnp.exp(b_last - b)
    decay = jnp.exp(b_last)
    return rows, qh, k, v, e_q, e_k, e_qi, e_ks, decay


def _independent_trips(n, stages, store, group=CHUNKS_IN_FLIGHT):
    stages = stages if isinstance(stages, (list, tuple)) else [stages]
    group = min(group, n)

    def trip(g, carry):
        ids = [g * group + i for i in range(group)]
        state = [stages[0](c) for c in ids]
        for stage in stages[1:]:
            state = [stage(c, s) for c, s in zip(ids, state)]
        for c, s in zip(ids, state):
            store(c, s)
        return carry

    lax.fori_loop(0, n // group, trip, 0)


def _branch_b_fwd(z, lb_logits, hg_g, batch, seq):
    tokens = batch * seq
    n_chunks = seq // CHUNK

    def body(z_ref, lb_ref, g_ref, yb_ref, st_ref, kept_ref, logf_scr, o_scr, qi_scr, ks_scr, dec_scr):
        f_scr, qh_scr, b_scr, o_kept = (kept_ref.at[k] for k in range(4))
        _hgrn_prepare(z_ref, lb_ref, f_scr, logf_scr, qh_scr, seq)
        causal = _chunk_masks()
        gain = g_ref[...]

        def cumulate(c):
            return _cumsum_rows(logf_scr[_chunk_rows(c), :])

        def store_cumulated(c, b):
            b_scr[_chunk_rows(c), :] = b

        def scores(c):
            _, qh, k, v, e_q, e_k, e_qi, e_ks, decay = _chunk_terms(c, z_ref, f_scr, qh_scr, b_scr)
            return _mm_nt(qh * e_q, k * e_k), v, qh * e_qi, k * e_ks, decay

        def within_chunk(c, s):
            att, v, q_int, k_st, decay = s
            return _mm(jnp.where(causal, att, 0.0), v), q_int, k_st, decay

        def store_within_chunk(c, out):
            rows = _chunk_rows(c)
            o_scr[rows, :], qi_scr[rows, :], ks_scr[rows, :], dec_scr[pl.ds(c, 1), :] = out

        def carry_state(c, state_t):
            update = st_ref[c]
            st_ref[c] = state_t
            return state_t * dec_scr[pl.ds(c, 1), :] + update

        def finish(c):
            rows = _chunk_rows(c)
            o = o_scr[rows, :] + _mm_nt(qi_scr[rows, :], st_ref[c])
            r = lax.rsqrt(_lane_mean(o * o) + EPS)
            gb = z_ref[3, rows, :]
            return (((o * r) * gain) * (gb * _sigmoid(gb))).astype(_MXU_DTYPE), o

        def store_finished(c, out):
            yb_ref[_chunk_rows(c), :], o_kept[_chunk_rows(c), :] = out

        _independent_trips(n_chunks, cumulate, store_cumulated)
        _independent_trips(n_chunks, [scores, within_chunk], store_within_chunk)
        _token_contractions(z_ref.at[2], ks_scr, st_ref, seq)
        lax.fori_loop(0, n_chunks, carry_state, jnp.zeros((LANES, LANES), F32))
        _independent_trips(n_chunks, finish, store_finished)

    seq_buf = pltpu.VMEM((seq, LANES), F32)
    return pl.pallas_call(
        body, name="branch_b_fwd",
        grid=(batch, N_BLK),
        in_specs=[pl.BlockSpec((4, seq, LANES), lambda b, h: (0, b, h)),
                  pl.BlockSpec((2, LANES), lambda b, h: (0, h)),
                  pl.BlockSpec((1, LANES), lambda b, h: (0, 0))],
        out_specs=[pl.BlockSpec((seq, LANES), lambda b, h: (b, h)),
                   pl.BlockSpec((None, n_chunks, LANES, LANES), lambda b, h: (b * N_BLK + h, 0, 0, 0)),
                   pl.BlockSpec((4, seq, LANES), lambda b, h: (0, b, h))],
        out_shape=[jax.ShapeDtypeStruct((tokens, D_MODEL), _MXU_DTYPE),
                   jax.ShapeDtypeStruct((batch * N_BLK, n_chunks, LANES, LANES), F32),
                   jax.ShapeDtypeStruct((4, tokens, D_MODEL), F32)],
        scratch_shapes=[seq_buf] * 4 + [pltpu.VMEM((n_chunks, LANES), F32)],
        compiler_params=_params(("parallel", "parallel")),
    )(z, lb_logits, hg_g)


def _branch_b_bwd(z, states, kept, dyb, dz, lb_logits, hg_g, batch, seq):
    n_chunks = seq // CHUNK

    def body(z_ref, st_ref, kept_ref, dyb_ref, dz_in_ref, lb_ref, g_ref, dz_ref, dlog_ref, dg_ref,
             do_scr, qi_scr, dqh_scr, df_scr, dec_scr, dgp_scr, dlb_scr, dst_scr):
        del dz_in_ref
        f_scr, qh_scr, b_scr, o_kept = (kept_ref.at[k] for k in range(4))
        first = (pl.program_id(0) == 0) & (pl.program_id(1) == 0)
        lb = _sigmoid(lb_ref[0:1, :] - lb_ref[1:2, :])
        causal = _chunk_masks()
        anti_causal = _chunk_masks(transposed=True)
        gain = g_ref[...]

        @pl.when(first)
        def _():
            dg_ref[...] = jnp.zeros_like(dg_ref)

        @pl.when(pl.program_id(1) == 0)
        def _():
            dlb_scr[...] = jnp.zeros_like(dlb_scr)

        def output_gradient(c):
            rows = _chunk_rows(c)
            b = b_scr[rows, :]
            q_int = qh_scr[rows, :] * (jnp.exp(b) * HG_SCALE)
            decay = jnp.exp(b_scr[pl.ds(c * CHUNK + CHUNK - 1, 1), :])
            o = o_kept[rows, :]
            r = lax.rsqrt(_lane_mean(o * o) + EPS)
            o_n = o * r
            gb = z_ref[3, rows, :]
            sgb = _sigmoid(gb)
            dyb_c = dyb_ref[rows, :]
            d_ong = dyb_c * (gb * sgb)
            d_gb = (dyb_c * (o_n * gain) * (sgb * (1.0 + gb * (1.0 - sgb)))).astype(_MXU_DTYPE)
            d_gain = jnp.sum(d_ong * o_n, axis=0, keepdims=True)
            d_on = d_ong * gain
            return d_gb, d_gain, r * (d_on - o_n * _lane_mean(d_on * o_n)), q_int, decay

        def store_output_gradient(c, out):
            rows = _chunk_rows(c)
            dz_ref[3, rows, :], dgp_scr[pl.ds(c, 1), :], do_scr[rows, :], qi_scr[rows, :], dec_scr[pl.ds(c, 1), :] = out

        def carry_state_gradient(cc, d_state_t):
            c = n_chunks - 1 - cc
            update = dst_scr[c]
            dst_scr[c] = d_state_t
            return d_state_t * dec_scr[pl.ds(c, 1), :] + update

        def score_gradients(c):
            rows, qh, k, v, e_q, e_k, e_qi, e_ks, decay = _chunk_terms(c, z_ref, f_scr, qh_scr, b_scr)
            state_t = st_ref[c]
            d_state_t = dst_scr[c]
            d_o = do_scr[rows, :]
            q_in, k_in, q_int, k_st = qh * e_q, k * e_k, qh * e_qi, k * e_ks
            first = (_mm_nt(k_in, q_in), _mm_nt(d_o, v), _mm_nt(v, d_o), _mm_nt(k_st, d_state_t), _mm(d_o, state_t),
                     _mm(v, d_state_t))
            d_decay = jnp.sum(state_t * d_state_t, axis=0, keepdims=True)
            return first, d_o, q_in, k_in, q_int, k_st, e_q, e_k, e_qi, e_ks, decay, d_decay

        def input_gradients(c, s):
            (att_t, d_att, d_att_t, dv_inter, dq_int, dk_st), d_o, q_in, k_in, q_int, k_st, e_q, e_k, e_qi, e_ks, decay, d_decay = s
            rows = _chunk_rows(c)
            d_v = _mm(jnp.where(anti_causal, att_t, 0.0), d_o) + dv_inter
            dq_in = _mm(jnp.where(causal, d_att, 0.0), k_in)
            dk_in = _mm(jnp.where(anti_causal, d_att_t, 0.0), q_in)
            d_k = dk_in * e_k + dk_st * e_ks
            kk = dk_st * k_st
            d_b = dq_in * q_in + dq_int * q_int - dk_in * k_in - kk
            d_b_last = jnp.sum(kk, axis=0, keepdims=True) + decay * d_decay
            d_logf = _cumsum_rows(d_b, reverse=True) + d_b_last
            return d_v.astype(_MXU_DTYPE), dq_in * e_q + dq_int * e_qi, d_logf / f_scr[rows, :] - d_k

        def store_input_gradients(c, out):
            rows = _chunk_rows(c)
            dz_ref[2, rows, :], dqh_scr[rows, :], df_scr[rows, :] = out

        def input_activations(rows):
            q = z_ref[0, rows, :]
            sq = _sigmoid(q)
            dz_ref[0, rows, :] = (dqh_scr[rows, :] * (sq * (1.0 + q * (1.0 - sq)))).astype(_MXU_DTYPE)
            sg = _sigmoid(z_ref[1, rows, :])
            d_f = df_scr[rows, :]
            dz_ref[1, rows, :] = (d_f * (1.0 - lb) * sg * (1.0 - sg)).astype(_MXU_DTYPE)
            dlb_scr[...] += jnp.sum(d_f * (1.0 - sg), axis=0, keepdims=True)

        _independent_trips(n_chunks, output_gradient, store_output_gradient)
        _token_contractions(do_scr, qi_scr, dst_scr, seq)
        lax.fori_loop(0, n_chunks, carry_state_gradient, jnp.zeros((LANES, LANES), F32))
        _independent_trips(n_chunks, [score_gradients, input_gradients], store_input_gradients)
        dg_ref[...] += jnp.sum(dgp_scr[...], axis=0, keepdims=True)
        _row_blocks(seq, input_activations)
        d_l0 = dlb_scr[...] * lb * (1.0 - lb)
        dlog_ref[0:1, :] = d_l0
        dlog_ref[1:2, :] = -d_l0

    tokens = batch * seq
    seq_buf = pltpu.VMEM((seq, LANES), F32)
    chunk_rows = pltpu.VMEM((n_chunks, LANES), F32)
    return pl.pallas_call(
        body, name="branch_b_bwd",
        grid=(N_BLK, batch),
        in_specs=[pl.BlockSpec((4, seq, LANES), lambda h, b: (0, b, h)),
                  pl.BlockSpec((None, n_chunks, LANES, LANES), lambda h, b: (b * N_BLK + h, 0, 0, 0)),
                  pl.BlockSpec((4, seq, LANES), lambda h, b: (0, b, h)),
                  pl.BlockSpec((seq, LANES), lambda h, b: (b, h)),
                  pl.BlockSpec(memory_space=pl.ANY),
                  pl.BlockSpec((2, LANES), lambda h, b: (0, h)),
                  pl.BlockSpec((1, LANES), lambda h, b: (0, 0))],
        out_specs=[pl.BlockSpec((4, seq, LANES), lambda h, b: (0, b, h)),
                   pl.BlockSpec((2, LANES), lambda h, b: (0, h)),
                   pl.BlockSpec((1, LANES), lambda h, b: (0, 0))],
        out_shape=[jax.ShapeDtypeStruct((N_GROUPS, tokens, D_MODEL), _MXU_DTYPE),
                   jax.ShapeDtypeStruct((2, D_MODEL), F32),
                   jax.ShapeDtypeStruct((1, LANES), F32)],
        scratch_shapes=[seq_buf] * 4 + [chunk_rows, chunk_rows, pltpu.VMEM((1, LANES), F32),
                                        pltpu.VMEM((n_chunks, LANES, LANES), F32)],
        input_output_aliases={4: 0},
        compiler_params=_params(("arbitrary", "arbitrary")),
    )(z, states, kept, dyb, dz, lb_logits, hg_g)


def _merge_tail(ya, yb, z, x2d, tgt2d, b_merge, final_g, pa, pb, wo):
    tokens, d = x2d.shape
    tm = min(256, tokens)
    n_tiles = tokens // tm

    def body(ya_ref, yb_ref, z_ref, x_ref, t_ref, bm_ref, fg_ref, pa_hbm, pb_hbm, wo_hbm,
             dya_ref, dyb_ref, dx2_ref, dz_ref, loss_ref, dfg_ref, dbm_ref, dpa_hbm, dpb_hbm, dwo_hbm,
             pa_s, pb_s, wo_s, dpa_s, dpb_s, dwo_s):
        i = pl.program_id(0)

        @pl.when(i == 0)
        def _():
            pltpu.sync_copy(pa_hbm, pa_s)
            pltpu.sync_copy(pb_hbm, pb_s)
            pltpu.sync_copy(wo_hbm, wo_s)
            dpa_s[...] = jnp.zeros_like(dpa_s)
            dpb_s[...] = jnp.zeros_like(dpb_s)
            dwo_s[...] = jnp.zeros_like(dwo_s)
            loss_ref[...] = jnp.zeros_like(loss_ref)
            dfg_ref[...] = jnp.zeros_like(dfg_ref)
            dbm_ref[...] = jnp.zeros_like(dbm_ref)

        ya_t = ya_ref[...]
        yb_t = yb_ref[...]
        out_a = _mm(ya_t, pa_s[...])
        out_b = _mm(yb_t, pb_s[...])
        g_a = _sigmoid(z_ref[0] + bm_ref[:, :d])
        g_b = _sigmoid(z_ref[1] + bm_ref[:, d:])
        mixed = g_a * out_a + g_b * out_b
        x2 = x_ref[...] + _mm(mixed, wo_s[...])
        r = lax.rsqrt(jnp.mean(x2 * x2, axis=-1, keepdims=True) + EPS)
        xn = x2 * r
        fg = fg_ref[...]
        diff = xn * fg - t_ref[...]
        loss_ref[...] += jnp.sum(diff * diff) * (0.5 / d)
        dy = diff * (1.0 / d)
        dfg_ref[...] += jnp.sum(dy * xn, axis=0, keepdims=True)
        dxn = dy * fg
        dx2 = r * (dxn - xn * jnp.mean(dxn * xn, axis=-1, keepdims=True))
        dx2_ref[...] = dx2
        dmixed = _mm_nt(dx2, wo_s[...])
        dwo_s[...] += _mm_tn(mixed, dx2)
        dgm_a = dmixed * out_a * g_a * (1.0 - g_a)
        dgm_b = dmixed * out_b * g_b * (1.0 - g_b)
        dz_ref[0] = dgm_a.astype(_MXU_DTYPE)
        dz_ref[1] = dgm_b.astype(_MXU_DTYPE)
        dbm_ref[:, :d] += jnp.sum(dgm_a, axis=0, keepdims=True)
        dbm_ref[:, d:] += jnp.sum(dgm_b, axis=0, keepdims=True)
        dout_a = dmixed * g_a
        dout_b = dmixed * g_b
        dpa_s[...] += _mm_tn(ya_t, dout_a)
        dpb_s[...] += _mm_tn(yb_t, dout_b)
        dya_ref[...] = _mm_nt(dout_a, pa_s[...])
        dyb_ref[...] = _mm_nt(dout_b, pb_s[...])

        @pl.when(i == n_tiles - 1)
        def _():
            pltpu.sync_copy(dpa_s, dpa_hbm)
            pltpu.sync_copy(dpb_s, dpb_hbm)
            pltpu.sync_copy(dwo_s, dwo_hbm)

    tile = pl.BlockSpec((tm, d), lambda i: (i, 0))
    gm = pl.BlockSpec((2, tm, d), lambda i: (3, i, 0))
    row = lambda n: pl.BlockSpec((1, n), lambda i: (0, 0))
    hbm = pl.BlockSpec(memory_space=pl.ANY)
    act = jax.ShapeDtypeStruct((tokens, d), F32)
    mat = jax.ShapeDtypeStruct((d, d), F32)
    return pl.pallas_call(
        body, name="merge_tail",
        grid=(n_tiles,),
        in_specs=[tile, tile, gm, tile, tile, row(2 * d), row(d), hbm, hbm, hbm],
        out_specs=[tile, tile, tile, gm, row(LANES), row(d), row(2 * d), hbm, hbm, hbm],
        out_shape=[act, act, act, jax.ShapeDtypeStruct((N_GROUPS, tokens, d), _MXU_DTYPE),
                   jax.ShapeDtypeStruct((1, LANES), F32), jax.ShapeDtypeStruct((1, d), F32),
                   jax.ShapeDtypeStruct((1, 2 * d), F32), mat, mat, mat],
        scratch_shapes=[pltpu.VMEM((d, d), _MXU_DTYPE)] * 3 + [pltpu.VMEM((d, d), F32)] * 3,
        compiler_params=_params(("arbitrary",)),
    )(ya, yb, z, x2d, tgt2d, b_merge, final_g, pa, pb, wo)


def _inproj_dw(h_t, dz):
    d, tokens = h_t.shape
    tm = min(2048, tokens)

    def body(h_ref, dz_ref, dw_ref):
        part = _mm(h_ref[...], dz_ref[...])

        @pl.when(pl.program_id(1) == 0)
        def _():
            dw_ref[...] = part

        @pl.when(pl.program_id(1) != 0)
        def _():
            dw_ref[...] += part

    def out_index(s, i):
        g = _group_of_slot(s)
        return (g // 2, 0, g % 2)

    return pl.pallas_call(
        body, name="inproj_dw",
        grid=(N_GROUPS, tokens // tm),
        in_specs=[pl.BlockSpec((d, tm), lambda s, i: (0, i)),
                  pl.BlockSpec((None, tm, D_MODEL), lambda s, i: (s, i, 0))],
        out_specs=pl.BlockSpec((None, d, D_MODEL), out_index),
        out_shape=jax.ShapeDtypeStruct((N_SHARDS, d, 2 * D_MODEL), F32),
        compiler_params=_params(("parallel", "arbitrary")),
    )(h_t, dz)


def _inproj_dw_exchange(h_t, dz, scatter):
    d, tokens = h_t.shape
    tm = min(2048, tokens)
    n_i = tokens // tm
    half = d // 2

    def body(h_ref, dz_ref, *rest):
        n_in, n_out = scatter.n_in, scatter.n_out
        dw_hbm, land_hbm = rest[n_in:n_in + 2]
        acc, local_sems, send_sems, recv_sems = rest[n_in + 2 + n_out:n_in + 6 + n_out]
        carried = scatter.copies(rest[:n_in], rest[n_in + 2:n_in + 2 + n_out], rest[n_in + 6 + n_out:])
        s, i = pl.program_id(0), pl.program_id(1)
        x, y, c, _ = _mesh_position()

        @pl.when((s == 0) & (i == 0))
        def _():
            for cp in carried:
                cp.start()

        part = _mm(h_ref[...], dz_ref[...])
        buf = acc.at[s % 2]

        @pl.when(i == 0)
        def _():
            buf[...] = part

        @pl.when(i != 0)
        def _():
            buf[...] += part

        def copies(k):
            g = _SLOT_TO_GROUP[k]
            cols = pl.ds((g % 2) * D_MODEL, D_MODEL)
            src = acc.at[k % 2]
            mine = pltpu.make_async_copy(src, dw_hbm.at[g // 2, :, cols], local_sems.at[k % 2])
            theirs = _remote(src.at[pl.ds((1 - c) * half, half), :], land_hbm.at[g // 2, :, cols],
                             send_sems, recv_sems, k, (x, y, 1 - c))
            return mine, theirs

        for k in range(N_GROUPS):
            @pl.when((s == k) & (i == n_i - 1))
            def _(k=k):
                if k > 0:
                    mine, theirs = copies(k - 1)
                    mine.wait()
                    theirs.wait_send()
                mine, theirs = copies(k)
                mine.start()
                theirs.start()
                if k == N_GROUPS - 1:
                    mine.wait()
                    theirs.wait_send()
                    for kk in range(N_GROUPS):
                        copies(kk)[1].wait_recv()
                    for cp in carried:
                        cp.wait()

    hbm = pl.BlockSpec(memory_space=pl.ANY)
    more = scatter.plumbing(first_operand=2, first_output=2)
    return pl.pallas_call(
        body, name="inproj_dw_exchange",
        grid=(N_GROUPS, n_i),
        in_specs=[pl.BlockSpec((d, tm), lambda s, i: (0, i)),
                  pl.BlockSpec((None, tm, D_MODEL), lambda s, i: (s, i, 0))] + more[1],
        out_specs=[hbm, hbm] + more[2],
        out_shape=[jax.ShapeDtypeStruct((N_SHARDS, d, 2 * D_MODEL), F32),
                   jax.ShapeDtypeStruct((N_SHARDS, half, 2 * D_MODEL), F32)] + more[3],
        scratch_shapes=[pltpu.VMEM((2, d, D_MODEL), F32), pltpu.SemaphoreType.DMA((2,)),
                        pltpu.SemaphoreType.DMA((N_GROUPS,)), pltpu.SemaphoreType.DMA((N_GROUPS,))] + more[4],
        input_output_aliases=more[5],
        compiler_params=_params(("arbitrary", "arbitrary")),
    )(h_t, dz, *more[0])


def _inproj_dx(dz, w_all, x2d, dx2, norm_g, scatter=None):
    tokens, d = x2d.shape
    tm = min(256, tokens)
    n_tiles = tokens // tm

    def body(dz_ref, w_hbm, x_ref, dx2_ref, g_ref, *rest):
        if scatter:
            n_in, n_out = scatter.n_in, scatter.n_out
            dx_ref, dg_ref = rest[n_in:n_in + 2]
            w_res = rest[n_in + 2 + n_out]
            copies = scatter.copies(rest[:n_in], rest[n_in + 2:n_in + 2 + n_out], rest[n_in + 3 + n_out:])
        else:
            dx_ref, dg_ref, w_res = rest
            copies = []
        i = pl.program_id(0)

        @pl.when(i == 0)
        def _():
            for cp in copies:
                cp.start()
            for slot, g in enumerate(_SLOT_TO_GROUP):
                pltpu.sync_copy(w_hbm.at[g // 2, :, pl.ds((g % 2) * D_MODEL, D_MODEL)],
                                w_res.at[:, pl.ds(slot * D_MODEL, D_MODEL)])
            dg_ref[...] = jnp.zeros_like(dg_ref)

        dz_all = jnp.concatenate([dz_ref[s] for s in range(N_GROUPS)], axis=1)
        dh = jnp.transpose(_mm_nt(w_res[...], dz_all))
        x = x_ref[...]
        r = lax.rsqrt(jnp.mean(x * x, axis=-1, keepdims=True) + EPS)
        xn = x * r
        dg_ref[...] += jnp.sum(dh * xn, axis=0, keepdims=True)
        dxn = dh * g_ref[...]
        dx_ref[...] = r * (dxn - xn * jnp.mean(dxn * xn, axis=-1, keepdims=True)) + dx2_ref[...]

        @pl.when(i == n_tiles - 1)
        def _():
            for cp in copies:
                cp.wait()

    tile = pl.BlockSpec((tm, d), lambda i: (i, 0))
    hbm = pl.BlockSpec(memory_space=pl.ANY)
    in_specs = [pl.BlockSpec((N_GROUPS, tm, D_MODEL), lambda i: (0, i, 0)), hbm, tile, tile,
                pl.BlockSpec((1, d), lambda i: (0, 0))]
    out_specs = [tile, pl.BlockSpec((1, d), lambda i: (0, 0))]
    out_shape = [jax.ShapeDtypeStruct((tokens, d), F32), jax.ShapeDtypeStruct((1, d), F32)]
    scratch = [pltpu.VMEM((d, N_GROUPS * D_MODEL), _MXU_DTYPE)]
    operands, aliases = [dz, w_all, x2d, dx2, norm_g], {}
    if scatter:
        more = scatter.plumbing(first_operand=len(operands), first_output=len(out_shape))
        operands, in_specs, out_specs = operands + more[0], in_specs + more[1], out_specs + more[2]
        out_shape, scratch, aliases = out_shape + more[3], scratch + more[4], more[5]
    return pl.pallas_call(
        body, name="inproj_dx", grid=(n_tiles,), in_specs=in_specs, out_specs=out_specs, out_shape=out_shape,
        scratch_shapes=scratch, input_output_aliases=aliases,
        compiler_params=_params(("arbitrary",)),
    )(*operands)


def _row_tile(rows, cols, itemsize=4, budget=2 * 1024 * 1024):
    tr = rows
    while tr * cols * itemsize > budget and tr % 16 == 0:
        tr //= 2
    return tr


def _cast_into_slot(a, chip, dtype, name):
    rows, cols = a.shape
    tr = _row_tile(rows, cols)

    def body(chip_ref, a_ref, o_ref):
        del chip_ref
        o_ref[...] = a_ref[...].astype(dtype)

    grid_spec = pltpu.PrefetchScalarGridSpec(
        num_scalar_prefetch=1, grid=(rows // tr,),
        in_specs=[pl.BlockSpec((tr, cols), lambda i, chip_ref: (i, 0))],
        out_specs=pl.BlockSpec((None, tr, cols), lambda i, chip_ref: (chip_ref[0], i, 0)))
    return pl.pallas_call(body, name=name, grid_spec=grid_spec,
                          out_shape=jax.ShapeDtypeStruct((N_SHARDS, rows, cols), dtype),
                          compiler_params=_params(("arbitrary",)))(chip, a)


def _sum_slots(stack, name):
    n, rows, cols = stack.shape
    tr = _row_tile(rows, cols * n)

    def body(s_ref, o_ref):
        total = s_ref[0].astype(F32)
        for k in range(1, n):
            total = total + s_ref[k].astype(F32)
        o_ref[...] = total

    return pl.pallas_call(body, name=name, grid=(rows // tr,),
                          in_specs=[pl.BlockSpec((n, tr, cols), lambda i: (0, i, 0))],
                          out_specs=pl.BlockSpec((tr, cols), lambda i: (i, 0)),
                          out_shape=jax.ShapeDtypeStruct((rows, cols), F32),
                          compiler_params=_params(("parallel",)))(stack)


def _add_half(full, landed, place, name):
    n, rows, cols = full.shape
    half = rows // 2
    tr = _row_tile(half, cols)
    nb = half // tr

    def body(place_ref, a_ref, b_ref, o_ref, own_ref):
        total = (a_ref[...] + b_ref[...]).astype(_MXU_DTYPE)
        o_ref[...] = total

        @pl.when(pl.program_id(1) == place_ref[1])
        def _():
            own_ref[...] = total

    grid_spec = pltpu.PrefetchScalarGridSpec(
        num_scalar_prefetch=1, grid=(nb, n),
        in_specs=[pl.BlockSpec((None, tr, cols), lambda i, j, place_ref: (j, place_ref[0] * nb + i, 0)),
                  pl.BlockSpec((None, tr, cols), lambda i, j, place_ref: (j, i, 0))],
        out_specs=[pl.BlockSpec((None, tr, cols), lambda i, j, place_ref: (j, i, 0)),
                   pl.BlockSpec((None, tr, cols), lambda i, j, place_ref: (place_ref[1], i, 0))])
    shape = jax.ShapeDtypeStruct((n, half, cols), _MXU_DTYPE)
    return pl.pallas_call(body, name=name, grid_spec=grid_spec, out_shape=[shape, shape],
                          compiler_params=_params(("parallel", "arbitrary")))(place, full, landed)


def _adamw_update(w, grad, m, v):
    c1 = 1.0 - ADAM_B1 ** ADAM_STEP
    c2 = 1.0 - ADAM_B2 ** ADAM_STEP
    nm = ADAM_B1 * m + (1.0 - ADAM_B1) * grad
    nv = ADAM_B2 * v + (1.0 - ADAM_B2) * (grad * grad)
    return (-ADAM_LR) * ((nm / c1) / (jnp.sqrt(nv / c2) + ADAM_EPS) + ADAM_WD * w), nm, nv


def _adamw(w, g, m, v, name):
    rows, cols = w.shape
    tr = _row_tile(rows, cols, budget=1024 * 1024)

    def body(w_ref, g_ref, m_ref, v_ref, d_ref, nm_ref, nv_ref):
        d_ref[...], nm_ref[...], nv_ref[...] = _adamw_update(w_ref[...], g_ref[...], m_ref[...], v_ref[...])

    spec = pl.BlockSpec((tr, cols), lambda i: (i, 0))
    shape = jax.ShapeDtypeStruct((rows, cols), F32)
    return pl.pallas_call(body, name=name, grid=(rows // tr,), in_specs=[spec] * 4, out_specs=[spec] * 3,
                          out_shape=[shape] * 3, compiler_params=_params(("parallel",)))(w, g, m, v)


def _adamw_halves(w, g_mine, g_sibling, m, v, core, name):
    rows, cols = w.shape
    half = rows // 2
    tr = _row_tile(half, cols, budget=1024 * 1024)
    nb = half // tr

    def body(core_ref, w_ref, gm_ref, gs_ref, m_ref, v_ref, g_ref, d_ref, nm_ref, nv_ref):
        mine = pl.program_id(0) // nb == core_ref[0]
        grad = jnp.where(mine, gm_ref[...], gs_ref[...])
        g_ref[...] = grad
        d_ref[...], nm_ref[...], nv_ref[...] = _adamw_update(w_ref[...], grad, m_ref[...], v_ref[...])

    spec = pl.BlockSpec((tr, cols), lambda i, core_ref: (i, 0))
    half_spec = pl.BlockSpec((tr, cols), lambda i, core_ref: (i % nb, 0))
    grid_spec = pltpu.PrefetchScalarGridSpec(num_scalar_prefetch=1, grid=(rows // tr,),
                                             in_specs=[spec, half_spec, half_spec, spec, spec], out_specs=[spec] * 4)
    shape = jax.ShapeDtypeStruct((rows, cols), F32)
    return pl.pallas_call(body, name=name, grid_spec=grid_spec, out_shape=[shape] * 4,
                          compiler_params=_params(("parallel",)))(core, w, g_mine, g_sibling, m, v)


def _local_step(x, loss_target, w_all, pa, pb, wo, conv_w, b_merge, conv_b, rg_wx, rg_bx, rg_wa, rg_ba,
                rg_lambda, hg_lb_logits, hg_norm_g, norm_g, final_norm_g, gather=None, reduction=None):
    batch, seq, d = x.shape
    x2d = x.reshape(batch * seq, d)
    tgt2d = loss_target.reshape(batch * seq, d)
    if gather is None:
        z, h_t = _inproj_fwd(x2d, norm_g, w_all)
    else:
        z, h_t, (w_all, pa, pb, wo), cw_all = _inproj_fwd_gather(x2d, norm_g, *gather)
        pa, pb, wo = (t.reshape(d, d) for t in (pa, pb, wo))
        conv_w = jnp.transpose(cw_all, (1, 0, 2)).reshape(CONV_WIDTH, d)
    lru = (conv_w, conv_b, rg_wx, rg_bx, rg_wa, rg_ba, rg_lambda)
    ya, hl, kept = _branch_a_fwd(z, *lru, batch, seq)
    yb, states, kept_b = _branch_b_fwd(z, hg_lb_logits, hg_norm_g, batch, seq)
    dya, dyb, dx2, dz, loss, d_final_g, d_b_merge, d_pa, d_pb, d_wo = _merge_tail(
        ya, yb, z, x2d, tgt2d, b_merge, final_norm_g, pa, pb, wo)
    dz, d_lb_logits, d_hg_g = _branch_b_bwd(z, states, kept_b, dyb, dz, hg_lb_logits, hg_norm_g, batch, seq)
    dz, d_conv_w, d_conv_b, d_wx, d_bx, d_wa, d_ba, d_lam = _branch_a_bwd(z, hl, kept, dya, dz, *lru, batch, seq)
    small = dict(b_merge=d_b_merge, conv_w=d_conv_w, conv_b=d_conv_b, rg_wx=d_wx, rg_bx=d_bx, rg_wa=d_wa,
                 rg_ba=d_ba, rg_lambda=d_lam, hg_lb_logits=d_lb_logits, hg_norm_g=d_hg_g,
                 norm_g=jnp.zeros((1, d), F32), final_norm_g=d_final_g)
    if reduction is None:
        big = (_inproj_dw(h_t, dz), d_pa, d_pb, d_wo)
        grad_x, small["norm_g"] = _inproj_dx(dz, w_all, x2d, dx2, norm_g)
        return loss[0, 0], grad_x.reshape(batch, seq, d), big, small
    first, second = reduction
    d_w_in, landed_w_in, *scattered_first = _inproj_dw_exchange(h_t, dz, first((d_pa, d_pb, d_wo), small))
    grad_x, d_norm_g, *scattered_second = _inproj_dx(dz, w_all, x2d, dx2, norm_g, scatter=second(d_w_in, landed_w_in))
    return loss[0, 0], grad_x.reshape(batch, seq, d), d_norm_g, (scattered_first, scattered_second)


_SMALL_ORDER = ("b_merge", "conv_w", "conv_b", "rg_wx", "rg_bx", "rg_wa", "rg_ba", "rg_lambda", "hg_lb_logits",
                "hg_norm_g", "norm_g", "final_norm_g")
N_DEV = 8
PIECE_ROWS = 272


def _pack_small(tree):
    flat = jnp.concatenate([tree[k].reshape(-1) for k in _SMALL_ORDER])
    flat = jnp.pad(flat, (0, N_DEV * PIECE_ROWS * LANES - flat.shape[0]))
    return flat.reshape(N_DEV * PIECE_ROWS, LANES)


def _unpack_small(packed, like):
    flat = packed.reshape(-1)
    out, pos = {}, 0
    for k in _SMALL_ORDER:
        n = like[k].size
        out[k] = flat[pos:pos + n].reshape(like[k].shape)
        pos += n
    return out


def _mesh_position():
    x, y, c = lax.axis_index("x"), lax.axis_index("y"), lax.axis_index("c")
    other_chips = [(1 - x, y), (x, 1 - y), (1 - x, 1 - y)]
    return x, y, c, other_chips


def _other_devices(x, y, c):
    flips = [(fx, fy, fc) for fx in (0, 1) for fy in (0, 1) for fc in (0, 1) if (fx, fy, fc) != (0, 0, 0)]
    return [(jnp.where(fx, 1 - x, x), jnp.where(fy, 1 - y, y), jnp.where(fc, 1 - c, c)) for fx, fy, fc in flips]


def _remote(src, dst, send_sems, recv_sems, k, device):
    return pltpu.make_async_remote_copy(src_ref=src, dst_ref=dst, send_sem=send_sems.at[k], recv_sem=recv_sems.at[k],
                                        device_id=device, device_id_type=MESH)


def _exchange_halves(bigs, small):
    n_big = len(bigs)
    n_sem = n_big + N_DEV - 1

    def body(*refs):
        srcs, small_src = refs[:n_big], refs[n_big]
        outs, small_out = refs[n_big + 1:2 * n_big + 1], refs[2 * n_big + 1]
        send_sems, recv_sems, local_sem = refs[2 * n_big + 2:]
        x, y, c, _ = _mesh_position()
        me, sibling = 4 * x + 2 * y + c, (x, y, 1 - c)
        mine = pltpu.make_async_copy(small_src.at[pl.ds(me * PIECE_ROWS, PIECE_ROWS), :], small_out.at[me], local_sem)
        mine.start()
        copies = []
        for a in range(n_big):
            hs = srcs[a].shape[1] // 2
            copies.append(_remote(srcs[a].at[:, pl.ds((1 - c) * hs, hs), :], outs[a], send_sems, recv_sems, a, sibling))
        for k, (px, py, pc) in enumerate(_other_devices(x, y, c)):
            piece = small_src.at[pl.ds((4 * px + 2 * py + pc) * PIECE_ROWS, PIECE_ROWS), :]
            copies.append(_remote(piece, small_out.at[me], send_sems, recv_sems, n_big + k, (px, py, pc)))
        for cp in copies:
            cp.start()
        for cp in copies:
            cp.wait()
        mine.wait()

    hbm = pl.BlockSpec(memory_space=pl.ANY)
    out_shape = [jax.ShapeDtypeStruct((g.shape[0], g.shape[1] // 2, g.shape[2]), F32) for g in bigs]
    out_shape.append(jax.ShapeDtypeStruct((N_DEV, PIECE_ROWS, LANES), F32))
    return pl.pallas_call(
        body, name="exchange_halves",
        in_specs=[hbm] * (n_big + 1), out_specs=[hbm] * (n_big + 1), out_shape=out_shape,
        scratch_shapes=[pltpu.SemaphoreType.DMA((n_sem,)), pltpu.SemaphoreType.DMA((n_sem,)), pltpu.SemaphoreType.DMA],
    )(*bigs, small)


class _Scatter:
    def __init__(self, bigs, by_chip, small=None):
        self.bigs, self.by_chip, self.small = list(bigs), list(by_chip), small
        self.n_big = len(self.bigs)
        self.n_in = 2 * self.n_big + (small is not None)
        self.n_out = self.n_big + (small is not None)
        self.n_scratch = 2 + (small is not None)

    def plumbing(self, first_operand, first_output):
        hbm = pl.BlockSpec(memory_space=pl.ANY)
        n_sem = 3 * self.n_big + (N_DEV - 1 if self.small is not None else 0)
        operands = self.bigs + self.by_chip + ([self.small] if self.small is not None else [])
        out_shapes = [jax.ShapeDtypeStruct(g.shape, g.dtype) for g in self.by_chip]
        scratch = [pltpu.SemaphoreType.DMA((n_sem,)), pltpu.SemaphoreType.DMA((n_sem,))]
        if self.small is not None:
            out_shapes.append(jax.ShapeDtypeStruct((N_DEV, PIECE_ROWS, LANES), F32))
            scratch.append(pltpu.SemaphoreType.DMA)
        aliases = {first_operand + self.n_big + a: first_output + a for a in range(self.n_big)}
        return operands, [hbm] * self.n_in, [hbm] * self.n_out, out_shapes, scratch, aliases

    def copies(self, in_refs, out_refs, scratch_refs):
        srcs, outs = in_refs[:self.n_big], out_refs[:self.n_big]
        send_sems, recv_sems = scratch_refs[:2]
        x, y, c, chips = _mesh_position()
        chip, me = 2 * x + y, 4 * x + 2 * y + c
        copies = []
        for a in range(self.n_big):
            for j, (cx, cy) in enumerate(chips):
                copies.append(_remote(srcs[a].at[2 * cx + cy], outs[a].at[chip], send_sems, recv_sems, 3 * a + j,
                                      (cx, cy, c)))
        if self.small is not None:
            small_src, small_out = in_refs[2 * self.n_big], out_refs[self.n_big]
            copies.append(pltpu.make_async_copy(small_src, small_out.at[me], scratch_refs[2]))
            for k, peer in enumerate(_other_devices(x, y, c)):
                copies.append(_remote(small_src, small_out.at[me], send_sems, recv_sems, 3 * self.n_big + k, peer))
        return copies


def _swap_halves(halves, vec):
    n_big = len(halves)

    def body(*refs):
        srcs, vec_src = refs[:n_big], refs[n_big]
        outs, vec_out = refs[n_big + 1:2 * n_big + 1], refs[2 * n_big + 1]
        send_sems, recv_sems, local_sem = refs[2 * n_big + 2:]
        x, y, c, _ = _mesh_position()
        me = 4 * x + 2 * y + c
        copies = [pltpu.make_async_copy(vec_src, vec_out.at[me], local_sem)]
        copies += [_remote(srcs[a], outs[a], send_sems, recv_sems, a, (x, y, 1 - c)) for a in range(n_big)]
        copies += [_remote(vec_src, vec_out.at[me], send_sems, recv_sems, n_big + k, peer)
                   for k, peer in enumerate(_other_devices(x, y, c))]
        for cp in copies:
            cp.start()
        for cp in copies:
            cp.wait()

    hbm = pl.BlockSpec(memory_space=pl.ANY)
    n_sem = n_big + N_DEV - 1
    return pl.pallas_call(
        body, name="swap_halves",
        in_specs=[hbm] * (n_big + 1), out_specs=[hbm] * (n_big + 1),
        out_shape=[jax.ShapeDtypeStruct(h.shape, F32) for h in halves] + [jax.ShapeDtypeStruct((N_DEV,) + vec.shape, F32)],
        scratch_shapes=[pltpu.SemaphoreType.DMA((n_sem,)), pltpu.SemaphoreType.DMA((n_sem,)), pltpu.SemaphoreType.DMA],
    )(*halves, vec)


def kernel(x, w_in, b_merge, conv_w, conv_b, rg_wx, rg_bx, rg_wa, rg_ba, rg_lambda, hg_lb_logits, hg_norm_g, proj_a, proj_b, w_out, norm_g, final_norm_g, loss_target, m_w_in, m_b_merge, m_conv_w, m_conv_b, m_rg_wx, m_rg_bx, m_rg_wa, m_rg_ba, m_rg_lambda, m_hg_lb_logits, m_hg_norm_g, m_proj_a, m_proj_b, m_w_out, m_norm_g, m_final_norm_g, v_w_in, v_b_merge, v_conv_w, v_conv_b, v_rg_wx, v_rg_bx, v_rg_wa, v_rg_ba, v_rg_lambda, v_hg_lb_logits, v_hg_norm_g, v_proj_a, v_proj_b, v_w_out, v_norm_g, v_final_norm_g):
    d = D_MODEL
    weights = dict(w_in=w_in, b_merge=b_merge, conv_w=conv_w, conv_b=conv_b, rg_wx=rg_wx, rg_bx=rg_bx, rg_wa=rg_wa,
                   rg_ba=rg_ba, rg_lambda=rg_lambda, hg_lb_logits=hg_lb_logits, hg_norm_g=hg_norm_g, proj_a=proj_a,
                   proj_b=proj_b, w_out=w_out, norm_g=norm_g, final_norm_g=final_norm_g)
    m = dict(w_in=m_w_in, b_merge=m_b_merge, conv_w=m_conv_w, conv_b=m_conv_b, rg_wx=m_rg_wx, rg_bx=m_rg_bx,
             rg_wa=m_rg_wa, rg_ba=m_rg_ba, rg_lambda=m_rg_lambda, hg_lb_logits=m_hg_lb_logits, hg_norm_g=m_hg_norm_g,
             proj_a=m_proj_a, proj_b=m_proj_b, w_out=m_w_out, norm_g=m_norm_g, final_norm_g=m_final_norm_g)
    v = dict(w_in=v_w_in, b_merge=v_b_merge, conv_w=v_conv_w, conv_b=v_conv_b, rg_wx=v_rg_wx, rg_bx=v_rg_bx,
             rg_wa=v_rg_wa, rg_ba=v_rg_ba, rg_lambda=v_rg_lambda, hg_lb_logits=v_hg_lb_logits, hg_norm_g=v_hg_norm_g,
             proj_a=v_proj_a, proj_b=v_proj_b, w_out=v_w_out, norm_g=v_norm_g, final_norm_g=v_final_norm_g)
    big_names = ("w_in", "proj_a", "proj_b", "w_out")

    core = lax.axis_index("c").astype(jnp.int32).reshape(1)
    chip = (2 * lax.axis_index("x") + lax.axis_index("y")).astype(jnp.int32)

    slotted = [_cast_into_slot(weights[k][0], chip.reshape(1), _MXU_DTYPE, f"cast_{k}") for k in big_names]
    conv_slotted = _cast_into_slot(conv_w[0], chip.reshape(1), F32, "slot_conv_w")

    small_shapes = {}

    place = jnp.concatenate([core, chip.reshape(1)])

    def reduce_proj_and_small(proj_grads, small_grads):
        small_shapes.update({k: t.shape for k, t in small_grads.items()})
        bigs = [g.reshape(N_SHARDS, d // N_SHARDS, d) for g in proj_grads]
        *landed, small_landed = _exchange_halves(bigs, _pack_small(small_grads))
        sums = [_add_half(g, l, place, f"add_half_{1 + a}") for a, (g, l) in enumerate(zip(bigs, landed))]
        return _Scatter([s[0] for s in sums], [s[1] for s in sums], _sum_slots(small_landed, "sum_small"))

    def reduce_w_in(d_w_in, landed):
        partial, own_slot = _add_half(d_w_in, landed, place, "add_half_0")
        return _Scatter([partial], [own_slot])

    loss_part, grad_x, d_norm_g, ((*by_chip_proj, small_all), by_chip_w_in) = _local_step(
        x, loss_target, None, None, None, None, None,
        b_merge, conv_b, rg_wx[0], rg_bx.reshape(1, d), rg_wa[0], rg_ba.reshape(1, d), rg_lambda, hg_lb_logits,
        hg_norm_g, norm_g, final_norm_g.reshape(1, d), gather=(slotted, conv_slotted, chip.reshape(1)),
        reduction=(reduce_proj_and_small, reduce_w_in))
    mine = [_sum_slots(s, f"sum_chips_{a}") for a, s in enumerate(by_chip_w_in + by_chip_proj)]
    late = jnp.concatenate([d_norm_g.reshape(SUBLANES, LANES), jnp.full((SUBLANES, LANES), loss_part, F32)])
    *theirs, late_parts = _swap_halves(mine, late)
    late_sum = _sum_slots(late_parts, "sum_late")
    loss = late_sum[SUBLANES, 0]
    small_red = _unpack_small(small_all, {k: jax.ShapeDtypeStruct(s, F32) for k, s in small_shapes.items()})
    small_red["norm_g"] = late_sum[:SUBLANES].reshape(1, d)

    grads, delta, new_m, new_v = {}, {}, {}, {}
    for k, g_mine, g_theirs in zip(big_names, mine, theirs):
        out = _adamw_halves(weights[k][0], g_mine, g_theirs, m[k][0], v[k][0], core, f"adamw_{k}")
        grads[k], delta[k], new_m[k], new_v[k] = (t.reshape(weights[k].shape) for t in out)
    cols = d // N_SHARDS
    g_conv = lax.dynamic_slice(small_red["conv_w"], (0, chip * cols), (CONV_WIDTH, cols))
    grads["conv_w"] = g_conv.reshape(conv_w.shape)
    dl, nm, nv = _adamw(conv_w[0], g_conv, m_conv_w[0], v_conv_w[0], "adamw_conv_w")
    delta["conv_w"], new_m["conv_w"], new_v["conv_w"] = (t.reshape(conv_w.shape) for t in (dl, nm, nv))
    rest = [k for k in _SMALL_ORDER if k != "conv_w"]
    like = {k: (weights[k] if k != "conv_w" else jnp.zeros((CONV_WIDTH, d), F32)) for k in _SMALL_ORDER}
    packs = [_pack_small({k: (t[k] if k != "conv_w" else like[k]) for k in _SMALL_ORDER}) for t in (weights, m, v)]
    g_pack = _pack_small({k: small_red[k].reshape(like[k].shape) for k in _SMALL_ORDER})
    outs = [_unpack_small(p, like) for p in _adamw(packs[0], g_pack, packs[1], packs[2], "adamw_small")]
    for k in rest:
        grads[k] = small_red[k].reshape(weights[k].shape)
        delta[k], new_m[k], new_v[k] = outs[0][k], outs[1][k], outs[2][k]

    order = ("w_in", "b_merge", "conv_w", "conv_b", "rg_wx", "rg_bx", "rg_wa", "rg_ba", "rg_lambda", "hg_lb_logits",
             "hg_norm_g", "proj_a", "proj_b", "w_out", "norm_g", "final_norm_g")
    return (loss, grad_x, *[grads[k] for k in order], *[delta[k] for k in order], *[new_m[k] for k in order],
            *[new_v[k] for k in order])
```

```python
import functools

import jax
import jax.numpy as jnp
from jax import lax
from jax.experimental import pallas as pl
from jax.experimental.pallas import tpu as pltpu

F32 = jnp.float32
_MXU_DTYPE = jnp.bfloat16

D_MODEL = 1024
LANES = 128
SUBLANES = 8
N_BLK = D_MODEL // LANES
N_GROUPS = 8
N_SHARDS = 4
CONV_WIDTH = 4
LRU_C = 8.0
CHUNK = 64
CHUNKS_IN_FLIGHT = 16
HG_SCALE = float(LANES) ** -0.5
EPS = 1e-6
ADAM_LR, ADAM_B1, ADAM_B2, ADAM_EPS, ADAM_WD, ADAM_STEP = 0.001, 0.9, 0.999, 1e-08, 0.01, 10
MATMUL_TOKENS = 512
TAIL_TOKENS = 256
CONTRACT_TOKENS = 2048
VMEM_LIMIT = 56 * 1024 * 1024
VMEM_LIMIT_BIG = 60 * 1024 * 1024
MESH = pl.DeviceIdType.MESH

_SLOT_TO_GROUP = (2, 3, 4, 5, 0, 1, 6, 7)


def _mm(a, b):
    return lax.dot_general(a.astype(_MXU_DTYPE), b.astype(_MXU_DTYPE), (((1,), (0,)), ((), ())),
                           preferred_element_type=F32)


def _mm_nt(a, b):
    return lax.dot_general(a.astype(_MXU_DTYPE), b.astype(_MXU_DTYPE), (((1,), (1,)), ((), ())),
                           preferred_element_type=F32)


def _mm_tn(a, b):
    return lax.dot_general(a.astype(_MXU_DTYPE), b.astype(_MXU_DTYPE), (((0,), (0,)), ((), ())),
                           preferred_element_type=F32)


def _sigmoid(x):
    return 0.5 * jnp.tanh(0.5 * x) + 0.5


def _log1p_pos(y):
    series = y * (1.0 - y * (0.5 - y * (1.0 / 3.0 - y * 0.25)))
    return jnp.where(y < 0.01, series, jnp.log(1.0 + y))


def _softplus(x):
    return jnp.maximum(x, 0.0) + _log1p_pos(jnp.exp(-jnp.abs(x)))


def _shift_down(x, n):
    rolled = pltpu.roll(x, n, 0)
    edge = SUBLANES if (n < SUBLANES and x.shape[0] > SUBLANES) else x.shape[0]
    rows = lax.broadcasted_iota(jnp.int32, (edge, x.shape[1]), 0)
    head = jnp.where(rows >= n, rolled[:edge], 0.0)
    return head if edge == x.shape[0] else jnp.concatenate([head, rolled[edge:]], axis=0)


def _shift_up(x, n):
    size = x.shape[0]
    rolled = pltpu.roll(x, size - n, 0)
    edge = SUBLANES if (n < SUBLANES and size > SUBLANES) else size
    rows = lax.broadcasted_iota(jnp.int32, (edge, x.shape[1]), 0)
    tail = jnp.where(rows < edge - n, rolled[size - edge:], 0.0)
    return tail if edge == size else jnp.concatenate([rolled[:size - edge], tail], axis=0)


def _params(dims, vmem=VMEM_LIMIT):
    return pltpu.CompilerParams(dimension_semantics=dims, vmem_limit_bytes=vmem)


def _slot_of_group(g):
    return jnp.where(g < 2, g + 4, jnp.where(g < 6, g - 2, g))


def _inproj_fwd_gather(x2d, norm_g, slotted, conv_slotted, chip):
    tokens, d = x2d.shape
    tm = min(MATMUL_TOKENS, tokens)
    n_tiles = tokens // tm
    n_big = len(slotted)
    n_sem = 6 * (n_big + 1) + 3
    last_pass = N_GROUPS - 1

    def shard_of(k, chip_id):
        x, y = chip_id // 2, chip_id % 2
        return 2 * jnp.where(k % 2 == 1, 1 - x, x) + jnp.where(k // 2 == 1, 1 - y, y)

    def body(chip_ref, x_ref, g_ref, *rest):
        bufs, cw = rest[n_big + 1:2 * n_big + 1], rest[2 * n_big + 1]
        z_ref, ht_ref = rest[2 * n_big + 2:2 * n_big + 4]
        h_all, slab, send_sems, recv_sems, slab_sems = rest[2 * n_big + 4:]
        del chip_ref
        p, i = pl.program_id(0), pl.program_id(1)
        x, y, c, chips = _mesh_position()
        me, sibling = 2 * x + y, (x, y, 1 - c)

        pieces = [(0, 0), (0, 1)] + [(a, None) for a in range(1, n_big)]

        def half(piece, slot, which):
            a, q = pieces[piece]
            hs = bufs[a].shape[1] // 2
            cols = slice(None) if q is None else pl.ds(q * D_MODEL, D_MODEL)
            return bufs[a].at[slot, pl.ds(which * hs, hs), cols]

        def send(piece, j):
            mine = half(piece, me, c)
            return _remote(mine, mine, send_sems, recv_sems, 6 * piece + j, (chips[j][0], chips[j][1], c))

        def arrival(piece, j):
            landed = half(piece, 2 * chips[j][0] + chips[j][1], c)
            return _remote(landed, landed, send_sems, recv_sems, 6 * piece + j, (chips[j][0], chips[j][1], c))

        def passed_on(piece, j, which):
            landed = half(piece, 2 * chips[j][0] + chips[j][1], which)
            return _remote(landed, landed, send_sems, recv_sems, 6 * piece + 3 + j, sibling)

        def conv_copy(j, slot):
            return _remote(cw.at[slot], cw.at[slot], send_sems, recv_sems, 6 * len(pieces) + j,
                           (chips[j][0], chips[j][1], c))

        def land(piece, j):
            arrival(piece, j).wait_recv()
            passed_on(piece, j, c).start()
            passed_on(piece, j, 1 - c).wait_recv()

        def slab_copy(pv):
            src = bufs[0].at[shard_of(pv // 2, me), :, pl.ds((pv % 2) * D_MODEL, D_MODEL)]
            return pltpu.make_async_copy(src, slab.at[pv % 2], slab_sems.at[pv % 2])

        @pl.when((p == 0) & (i == 0))
        def _():
            for q in range(2):
                send(q, 0).start()
                send(q, 1).start()
            slab_copy(0).start()

        @pl.when(i == 0)
        def _():
            for pv in range(N_GROUPS):
                @pl.when(p == pv)
                def _(pv=pv):
                    slab_copy(pv).wait()

        rows = pl.ds(pl.multiple_of(i * tm, tm), tm)

        @pl.when(p == 0)
        def _():
            xt = x_ref[...]
            r = lax.rsqrt(jnp.mean(xt * xt, axis=-1, keepdims=True) + EPS)
            h = (xt * r) * g_ref[...]
            h_all[rows, :] = h.astype(_MXU_DTYPE)
            ht_ref[...] = jnp.transpose(h).astype(_MXU_DTYPE)

        z_ref[...] = _mm(h_all[rows, :], slab[p % 2])

        def relay(q):
            landed = half(q, 2 * chips[q][0] + chips[q][1], c)
            to = chips[1 - q]
            return _remote(landed, landed, send_sems, recv_sems, 6 * q + 2, (to[0], to[1], c))

        landings = {1: [(0, 0)], 2: [(1, 0), (1, 1)], 3: [(0, 1)], 5: [(0, 2)], 6: [(1, 2)]}

        def end_of_pass(pv):
            for q, j in landings.get(pv - 1, []):
                land(q, j)
                if j == q:
                    relay(q).start()
            if pv - 1 == 3:
                for piece in range(2, len(pieces)):
                    for jj in range(3):
                        send(piece, jj).start()
                for jj in range(3):
                    conv_copy(jj, me).start()
            slab_copy(pv).start()

        @pl.when(i == n_tiles - 1)
        def _():
            for pv in range(1, N_GROUPS):
                pl.when(p == pv - 1)(functools.partial(end_of_pass, pv))

        @pl.when((p == last_pass) & (i == n_tiles - 1))
        def _():
            for piece in range(2, len(pieces)):
                for j in range(3):
                    land(piece, j)
            for j in range(3):
                conv_copy(j, 2 * chips[j][0] + chips[j][1]).wait_recv()
            for piece in range(len(pieces)):
                for j in range(3):
                    (relay(piece) if (piece < 2 and j == 2) else send(piece, j)).wait_send()
                    passed_on(piece, j, c).wait_send()
            for j in range(3):
                conv_copy(j, me).wait_send()

    def z_index(p, i, chip_ref):
        g = 2 * shard_of(p // 2, chip_ref[0]) + p % 2
        return (_slot_of_group(g), i, 0)

    def first_pass_tile(p, i, chip_ref):
        return jnp.where(p == 0, i, n_tiles - 1)

    hbm = pl.BlockSpec(memory_space=pl.ANY)
    operands = list(slotted) + [conv_slotted]
    grid_spec = pltpu.PrefetchScalarGridSpec(
        num_scalar_prefetch=1, grid=(N_GROUPS, n_tiles),
        in_specs=[pl.BlockSpec((tm, d), lambda p, i, chip_ref: (first_pass_tile(p, i, chip_ref), 0)),
                  pl.BlockSpec((1, d), lambda p, i, chip_ref: (0, 0))] + [hbm] * (n_big + 1),
        out_specs=[hbm] * (n_big + 1) + [pl.BlockSpec((None, tm, D_MODEL), z_index),
                                         pl.BlockSpec((d, tm), lambda p, i, chip_ref: (0, first_pass_tile(p, i, chip_ref)))],
        scratch_shapes=[pltpu.VMEM((tokens, d), _MXU_DTYPE), pltpu.VMEM((2, d, D_MODEL), _MXU_DTYPE),
                        pltpu.SemaphoreType.DMA((n_sem,)), pltpu.SemaphoreType.DMA((n_sem,)),
                        pltpu.SemaphoreType.DMA((2,))])
    out = pl.pallas_call(
        body, name="inproj_fwd_gather", grid_spec=grid_spec,
        out_shape=[jax.ShapeDtypeStruct(a.shape, a.dtype) for a in operands]
        + [jax.ShapeDtypeStruct((N_GROUPS, tokens, D_MODEL), F32), jax.ShapeDtypeStruct((d, tokens), _MXU_DTYPE)],
        input_output_aliases={3 + a: a for a in range(n_big + 1)},
        compiler_params=_params(("arbitrary", "arbitrary")),
    )(chip, x2d, norm_g, *operands)
    return out[n_big + 1], out[n_big + 2], out[:n_big], out[n_big]


def _lane_blocks(x):
    return [x[:, k * LANES:(k + 1) * LANES] for k in range(x.shape[1] // LANES)]


def _block_diag(x, w_ref, transposed=False):
    mm = _mm_nt if transposed else _mm
    return jnp.concatenate([mm(xk, w_ref[k]) for k, xk in enumerate(_lane_blocks(x))], axis=1)


def _lru_decay(gr, sp):
    log_a = (-LRU_C) * gr * sp
    a = jnp.exp(log_a)
    y = 2.0 * log_a
    mult_sq = jnp.where(y > -1e-3, -y * (1.0 + 0.5 * y), 1.0 - a * a)
    inv_mult = lax.rsqrt(jnp.maximum(mult_sq, 1e-37))
    return a, mult_sq * inv_mult, inv_mult


def _tile_rows(width):
    return lax.broadcasted_iota(jnp.int32, (SUBLANES, width), 0)


def _scan_forward(a_scr, u_scr, h_scr, seq):
    width = a_scr.shape[1]
    rows = _tile_rows(width)

    def tile(j, carry):
        sl = pl.ds(pl.multiple_of(j * SUBLANES, SUBLANES), SUBLANES)
        a = a_scr[sl, :]
        u = u_scr[sl, :]
        for d in (1, 2, 4):
            keep = rows >= d
            a_sh = jnp.where(keep, pltpu.roll(a, d, 0), 1.0)
            u_sh = jnp.where(keep, pltpu.roll(u, d, 0), 0.0)
            u = a * u_sh + u
            a = a * a_sh
        h = u + a * carry
        h_scr[sl, :] = h
        return jnp.broadcast_to(h[SUBLANES - 1:SUBLANES, :], (SUBLANES, width))

    lax.fori_loop(0, seq // SUBLANES, tile, jnp.zeros((SUBLANES, width), F32))


def _scan_backward(c_scr, d_scr, g_scr, seq):
    width = c_scr.shape[1]
    rows = _tile_rows(width)
    n_tiles = seq // SUBLANES

    def tile(jj, carry):
        j = n_tiles - 1 - jj
        sl = pl.ds(pl.multiple_of(j * SUBLANES, SUBLANES), SUBLANES)
        c = c_scr[sl, :]
        g = d_scr[sl, :]
        for d in (1, 2, 4):
            keep = rows < SUBLANES - d
            c_sh = jnp.where(keep, pltpu.roll(c, SUBLANES - d, 0), 1.0)
            g_sh = jnp.where(keep, pltpu.roll(g, SUBLANES - d, 0), 0.0)
            g = c * g_sh + g
            c = c * c_sh
        g = g + c * carry
        g_scr[sl, :] = g
        return jnp.broadcast_to(g[0:1, :], (SUBLANES, width))

    lax.fori_loop(0, n_tiles, tile, jnp.zeros((SUBLANES, width), F32))


LRU_BLOCKS_PER_STEP = 2
LRU_LANES = LRU_BLOCKS_PER_STEP * LANES
LRU_STEPS = N_BLK // LRU_BLOCKS_PER_STEP


def _branch_a_fwd(z, conv_w, conv_b, wx, bx, wa, ba, lam, batch, seq):
    tokens = batch * seq

    def body(z_ref, cw_ref, cb_ref, wx_ref, bx_ref, wa_ref, ba_ref, lam_ref, ya_ref, hl_ref, kept_ref, a_scr, u_scr):
        xa = z_ref[0]
        ga = z_ref[1]
        xc = (cb_ref[...] + cw_ref[3:4, :] * xa + cw_ref[2:3, :] * _shift_down(xa, 1)
              + cw_ref[1:2, :] * _shift_down(xa, 2) + cw_ref[0:1, :] * _shift_down(xa, 3))
        gi = _sigmoid(_block_diag(xc, wx_ref) + bx_ref[...])
        gr = _sigmoid(_block_diag(xc, wa_ref) + ba_ref[...])
        a, mult, _ = _lru_decay(gr, _softplus(-lam_ref[...]))
        kept_ref[0], kept_ref[1], kept_ref[2] = xc, gi, gr
        a_scr[...] = a
        u_scr[...] = mult * gi * xc
        _scan_forward(a_scr, u_scr, hl_ref, seq)
        ya_ref[...] = (hl_ref[...] * (ga * _sigmoid(ga))).astype(_MXU_DTYPE)

    blk = pl.BlockSpec((seq, LRU_LANES), lambda b, c: (b, c))
    vec = pl.BlockSpec((1, LRU_LANES), lambda b, c: (0, c))
    mat = pl.BlockSpec((LRU_BLOCKS_PER_STEP, LANES, LANES), lambda b, c: (c, 0, 0))
    return pl.pallas_call(
        body, name="branch_a_fwd",
        grid=(batch, LRU_STEPS),
        in_specs=[pl.BlockSpec((2, seq, LRU_LANES), lambda b, c: (2, b, c)),
                  pl.BlockSpec((CONV_WIDTH, LRU_LANES), lambda b, c: (0, c)), vec, mat, vec, mat, vec, vec],
        out_specs=[blk, blk, pl.BlockSpec((3, seq, LRU_LANES), lambda b, c: (0, b, c))],
        out_shape=[jax.ShapeDtypeStruct((tokens, D_MODEL), _MXU_DTYPE), jax.ShapeDtypeStruct((tokens, D_MODEL), F32),
                   jax.ShapeDtypeStruct((3, tokens, D_MODEL), F32)],
        scratch_shapes=[pltpu.VMEM((seq, LRU_LANES), F32), pltpu.VMEM((seq, LRU_LANES), F32)],
        compiler_params=_params(("parallel", "parallel")),
    )(z, conv_w, conv_b, wx, bx, wa, ba, lam)


def _branch_a_bwd(z, hl, kept, dya, dz, conv_w, wx, wa, lam, batch, seq):
    def body(z_ref, hl_ref, kept_ref, dya_ref, dz_in_ref, cw_ref, wx_ref, wa_ref, lam_ref,
             dz_ref, dcw_ref, dcb_ref, dwx_ref, dbx_ref, dwa_ref, dba_ref, dlam_ref, c_scr, d_scr):
        del dz_in_ref
        g_scr = d_scr
        xa = z_ref[0]
        ga = z_ref[1]
        hl = hl_ref[...]
        dya = dya_ref[...]
        xc, gi, gr = kept_ref[0], kept_ref[1], kept_ref[2]
        sp = _softplus(-lam_ref[...])
        a, mult, inv_mult = _lru_decay(gr, sp)
        sga = _sigmoid(ga)
        dz_ref[1] = (dya * hl * (sga * (1.0 + ga * (1.0 - sga)))).astype(_MXU_DTYPE)
        c_scr[...] = _shift_up(a, 1)
        d_scr[...] = dya * (ga * sga)
        _scan_backward(c_scr, d_scr, g_scr, seq)
        g = g_scr[...]
        da = g * _shift_down(hl, 1)
        dmult = g * gi * xc
        dgi = g * mult * xc
        dxc = g * mult * gi
        dlog_a = da * a - dmult * (a * a) * inv_mult
        dgr = dlog_a * (-LRU_C) * sp
        dsp = jnp.sum(dlog_a * gr, axis=0, keepdims=True) * (-LRU_C)
        dlam = -dsp * _sigmoid(-lam_ref[...])
        dpi = dgi * gi * (1.0 - gi)
        dpr = dgr * gr * (1.0 - gr)
        dxc = dxc + _block_diag(dpi, wx_ref, transposed=True) + _block_diag(dpr, wa_ref, transposed=True)
        dwx = jnp.stack([_mm_tn(xk, dk) for xk, dk in zip(_lane_blocks(xc), _lane_blocks(dpi))])
        dwa = jnp.stack([_mm_tn(xk, dk) for xk, dk in zip(_lane_blocks(xc), _lane_blocks(dpr))])
        dbx = jnp.sum(dpi, axis=0, keepdims=True)
        dba = jnp.sum(dpr, axis=0, keepdims=True)
        ahead = [dxc if k == CONV_WIDTH - 1 else _shift_up(dxc, CONV_WIDTH - 1 - k) for k in range(CONV_WIDTH)]
        dxa = sum(cw_ref[k:k + 1, :] * ahead[k] for k in range(CONV_WIDTH))
        dz_ref[0] = dxa.astype(_MXU_DTYPE)
        dcb = jnp.sum(dxc, axis=0, keepdims=True)
        dcw = [jnp.sum(ahead[k] * xa, axis=0, keepdims=True) for k in range(CONV_WIDTH)]

        @pl.when(pl.program_id(1) == 0)
        def _():
            for k in range(CONV_WIDTH):
                dcw_ref[k:k + 1, :] = dcw[k]
            dcb_ref[...] = dcb
            dwx_ref[...] = dwx
            dbx_ref[...] = dbx
            dwa_ref[...] = dwa
            dba_ref[...] = dba
            dlam_ref[...] = dlam

        @pl.when(pl.program_id(1) != 0)
        def _():
            for k in range(CONV_WIDTH):
                dcw_ref[k:k + 1, :] += dcw[k]
            dcb_ref[...] += dcb
            dwx_ref[...] += dwx
            dbx_ref[...] += dbx
            dwa_ref[...] += dwa
            dba_ref[...] += dba
            dlam_ref[...] += dlam

    tokens = batch * seq
    blk = pl.BlockSpec((seq, LRU_LANES), lambda c, b: (b, c))
    vec = pl.BlockSpec((1, LRU_LANES), lambda c, b: (0, c))
    mat = pl.BlockSpec((LRU_BLOCKS_PER_STEP, LANES, LANES), lambda c, b: (c, 0, 0))
    vec_shape = jax.ShapeDtypeStruct((1, D_MODEL), F32)
    mat_shape = jax.ShapeDtypeStruct((N_BLK, LANES, LANES), F32)
    return pl.pallas_call(
        body, name="branch_a_bwd",
        grid=(LRU_STEPS, batch),
        in_specs=[pl.BlockSpec((2, seq, LRU_LANES), lambda c, b: (2, b, c)), blk,
                  pl.BlockSpec((3, seq, LRU_LANES), lambda c, b: (0, b, c)), blk,
                  pl.BlockSpec(memory_space=pl.ANY),
                  pl.BlockSpec((CONV_WIDTH, LRU_LANES), lambda c, b: (0, c)), mat, mat, vec],
        out_specs=[pl.BlockSpec((2, seq, LRU_LANES), lambda c, b: (2, b, c)),
                   pl.BlockSpec((CONV_WIDTH, LRU_LANES), lambda c, b: (0, c)), vec, mat, vec, mat, vec, vec],
        out_shape=[jax.ShapeDtypeStruct((N_GROUPS, tokens, D_MODEL), _MXU_DTYPE),
                   jax.ShapeDtypeStruct((CONV_WIDTH, D_MODEL), F32), vec_shape, mat_shape, vec_shape, mat_shape,
                   vec_shape, vec_shape],
        scratch_shapes=[pltpu.VMEM((seq, LRU_LANES), F32)] * 2,
        input_output_aliases={4: 0},
        compiler_params=_params(("parallel", "arbitrary"), vmem=VMEM_LIMIT_BIG),
    )(z, hl, kept, dya, dz, conv_w, wx, wa, lam)


def _chunk_masks(transposed=False):
    r = lax.broadcasted_iota(jnp.int32, (CHUNK, CHUNK), 0)
    c = lax.broadcasted_iota(jnp.int32, (CHUNK, CHUNK), 1)
    return r <= c if transposed else r >= c


def _row_blocks(seq, fn):
    block = min(256, seq)

    def trip(i, carry):
        fn(pl.ds(pl.multiple_of(i * block, block), block))
        return carry

    lax.fori_loop(0, seq // block, trip, 0)


def _hgrn_prepare(z_ref, lb_ref, f_scr, logf_scr, qh_scr, seq):
    lb = _sigmoid(lb_ref[0:1, :] - lb_ref[1:2, :])

    def block(rows):
        q = z_ref[0, rows, :]
        f = lb + (1.0 - lb) * _sigmoid(z_ref[1, rows, :])
        f_scr[rows, :] = f
        logf_scr[rows, :] = jnp.log(f)
        qh_scr[rows, :] = q * _sigmoid(q)

    _row_blocks(seq, block)
    return lb


def _cumsum_rows(x, reverse=False):
    shift = _shift_up if reverse else _shift_down
    d = 1
    while d < x.shape[0]:
        x = x + shift(x, d)
        d *= 2
    return x


def _lane_mean(x):
    return jnp.mean(x, axis=-1, keepdims=True)


def _token_contractions(lhs_scr, rhs_scr, out_ref, seq):
    rows_id = lax.broadcasted_iota(jnp.int32, (LANES, LANES), 0)

    def transposed(p):
        rows = pl.ds(pl.multiple_of(p * LANES, LANES), LANES)
        return jnp.transpose(lhs_scr[rows, :]).astype(_MXU_DTYPE), rhs_scr[rows, :]

    def contract(p, s):
        lhs_t, rhs = s
        return (_mm(lhs_t, jnp.where(rows_id < CHUNK, rhs, 0.0)), _mm(lhs_t, jnp.where(rows_id >= CHUNK, rhs, 0.0)))

    def store(p, out):
        out_ref[2 * p] = out[0]
        out_ref[2 * p + 1] = out[1]

    _independent_trips(seq // LANES, [transposed, contract], store)


def _chunk_rows(c):
    return pl.ds(pl.multiple_of(c * CHUNK, CHUNK), CHUNK)


def _chunk_terms(c, z_ref, f_scr, qh_scr, b_scr):
    rows = _chunk_rows(c)
    b = b_scr[rows, :]
    b_mid = b_scr[pl.ds(c * CHUNK + CHUNK // 2, 1), :]
    b_last = b_scr[pl.ds(c * CHUNK + CHUNK - 1, 1), :]
    qh = qh_scr[rows, :]
    k = 1.0 - f_scr[rows, :]
    v = z_ref[2, rows, :]
    e_q = jnp.exp(b - b_mid) * HG_SCALE
    e_k = jnp.exp(b_mid - b)
    e_qi = jnp.exp(b) * HG_SCALE
    e_ks = jnp.exp(b_last - b)
    decay = jnp.exp(b_last)
    return rows, qh, k, v, e_q, e_k, e_qi, e_ks, decay


def _independent_trips(n, stages, store, group=CHUNKS_IN_FLIGHT):
    stages = stages if isinstance(stages, (list, tuple)) else [stages]
    group = min(group, n)

    def trip(g, carry):
        ids = [g * group + i for i in range(group)]
        state = [stages[0](c) for c in ids]
        for stage in stages[1:]:
            state = [stage(c, s) for c, s in zip(ids, state)]
        for c, s in zip(ids, state):
            store(c, s)
        return carry

    lax.fori_loop(0, n // group, trip, 0)


def _branch_b_fwd(z, lb_logits, hg_g, batch, seq):
    tokens = batch * seq
    n_chunks = seq // CHUNK

    def body(z_ref, lb_ref, g_ref, yb_ref, st_ref, kept_ref, logf_scr, o_scr, qi_scr, ks_scr, dec_scr):
        f_scr, qh_scr, b_scr, o_kept = (kept_ref.at[k] for k in range(4))
        _hgrn_prepare(z_ref, lb_ref, f_scr, logf_scr, qh_scr, seq)
        causal = _chunk_masks()
        gain = g_ref[...]

        def cumulate(c):
            return _cumsum_rows(logf_scr[_chunk_rows(c), :])

        def store_cumulated(c, b):
            b_scr[_chunk_rows(c), :] = b

        def scores(c):
            _, qh, k, v, e_q, e_k, e_qi, e_ks, decay = _chunk_terms(c, z_ref, f_scr, qh_scr, b_scr)
            return _mm_nt(qh * e_q, k * e_k), v, qh * e_qi, k * e_ks, decay

        def within_chunk(c, s):
            att, v, q_int, k_st, decay = s
            return _mm(jnp.where(causal, att, 0.0), v), q_int, k_st, decay

        def store_within_chunk(c, out):
            rows = _chunk_rows(c)
            o_scr[rows, :], qi_scr[rows, :], ks_scr[rows, :], dec_scr[pl.ds(c, 1), :] = out

        def carry_state(c, state_t):
            update = st_ref[c]
            st_ref[c] = state_t
            return state_t * dec_scr[pl.ds(c, 1), :] + update

        def finish(c):
            rows = _chunk_rows(c)
            o = o_scr[rows, :] + _mm_nt(qi_scr[rows, :], st_ref[c])
            r = lax.rsqrt(_lane_mean(o * o) + EPS)
            gb = z_ref[3, rows, :]
            return (((o * r) * gain) * (gb * _sigmoid(gb))).astype(_MXU_DTYPE), o

        def store_finished(c, out):
            yb_ref[_chunk_rows(c), :], o_kept[_chunk_rows(c), :] = out

        _independent_trips(n_chunks, cumulate, store_cumulated)
        _independent_trips(n_chunks, [scores, within_chunk], store_within_chunk)
        _token_contractions(z_ref.at[2], ks_scr, st_ref, seq)
        lax.fori_loop(0, n_chunks, carry_state, jnp.zeros((LANES, LANES), F32))
        _independent_trips(n_chunks, finish, store_finished)

    seq_buf = pltpu.VMEM((seq, LANES), F32)
    return pl.pallas_call(
        body, name="branch_b_fwd",
        grid=(batch, N_BLK),
        in_specs=[pl.BlockSpec((4, seq, LANES), lambda b, h: (0, b, h)),
                  pl.BlockSpec((2, LANES), lambda b, h: (0, h)),
                  pl.BlockSpec((1, LANES), lambda b, h: (0, 0))],
        out_specs=[pl.BlockSpec((seq, LANES), lambda b, h: (b, h)),
                   pl.BlockSpec((None, n_chunks, LANES, LANES), lambda b, h: (b * N_BLK + h, 0, 0, 0)),
                   pl.BlockSpec((4, seq, LANES), lambda b, h: (0, b, h))],
        out_shape=[jax.ShapeDtypeStruct((tokens, D_MODEL), _MXU_DTYPE),
                   jax.ShapeDtypeStruct((batch * N_BLK, n_chunks, LANES, LANES), F32),
                   jax.ShapeDtypeStruct((4, tokens, D_MODEL), F32)],
        scratch_shapes=[seq_buf] * 4 + [pltpu.VMEM((n_chunks, LANES), F32)],
        compiler_params=_params(("parallel", "parallel")),
    )(z, lb_logits, hg_g)


def _branch_b_bwd(z, states, kept, dyb, dz, lb_logits, hg_g, batch, seq):
    n_chunks = seq // CHUNK

    def body(z_ref, st_ref, kept_ref, dyb_ref, dz_in_ref, lb_ref, g_ref, dz_ref, dlog_ref, dg_ref,
             do_scr, qi_scr, dqh_scr, df_scr, dec_scr, dgp_scr, dlb_scr, dst_scr):
        del dz_in_ref
        f_scr, qh_scr, b_scr, o_kept = (kept_ref.at[k] for k in range(4))
        first = (pl.program_id(0) == 0) & (pl.program_id(1) == 0)
        lb = _sigmoid(lb_ref[0:1, :] - lb_ref[1:2, :])
        causal = _chunk_masks()
        anti_causal = _chunk_masks(transposed=True)
        gain = g_ref[...]

        @pl.when(first)
        def _():
            dg_ref[...] = jnp.zeros_like(dg_ref)

        @pl.when(pl.program_id(1) == 0)
        def _():
            dlb_scr[...] = jnp.zeros_like(dlb_scr)

        def output_gradient(c):
            rows = _chunk_rows(c)
            b = b_scr[rows, :]
            q_int = qh_scr[rows, :] * (jnp.exp(b) * HG_SCALE)
            decay = jnp.exp(b_scr[pl.ds(c * CHUNK + CHUNK - 1, 1), :])
            o = o_kept[rows, :]
            r = lax.rsqrt(_lane_mean(o * o) + EPS)
            o_n = o * r
            gb = z_ref[3, rows, :]
            sgb = _sigmoid(gb)
            dyb_c = dyb_ref[rows, :]
            d_ong = dyb_c * (gb * sgb)
            d_gb = (dyb_c * (o_n * gain) * (sgb * (1.0 + gb * (1.0 - sgb)))).astype(_MXU_DTYPE)
            d_gain = jnp.sum(d_ong * o_n, axis=0, keepdims=True)
            d_on = d_ong * gain
            return d_gb, d_gain, r * (d_on - o_n * _lane_mean(d_on * o_n)), q_int, decay

        def store_output_gradient(c, out):
            rows = _chunk_rows(c)
            dz_ref[3, rows, :], dgp_scr[pl.ds(c, 1), :], do_scr[rows, :], qi_scr[rows, :], dec_scr[pl.ds(c, 1), :] = out

        def carry_state_gradient(cc, d_state_t):
            c = n_chunks - 1 - cc
            update = dst_scr[c]
            dst_scr[c] = d_state_t
            return d_state_t * dec_scr[pl.ds(c, 1), :] + update

        def score_gradients(c):
            rows, qh, k, v, e_q, e_k, e_qi, e_ks, decay = _chunk_terms(c, z_ref, f_scr, qh_scr, b_scr)
            state_t = st_ref[c]
            d_state_t = dst_scr[c]
            d_o = do_scr[rows, :]
            q_in, k_in, q_int, k_st = qh * e_q, k * e_k, qh * e_qi, k * e_ks
            first = (_mm_nt(k_in, q_in), _mm_nt(d_o, v), _mm_nt(v, d_o), _mm_nt(k_st, d_state_t), _mm(d_o, state_t),
                     _mm(v, d_state_t))
            d_decay = jnp.sum(state_t * d_state_t, axis=0, keepdims=True)
            return first, d_o, q_in, k_in, q_int, k_st, e_q, e_k, e_qi, e_ks, decay, d_decay

        def input_gradients(c, s):
            (att_t, d_att, d_att_t, dv_inter, dq_int, dk_st), d_o, q_in, k_in, q_int, k_st, e_q, e_k, e_qi, e_ks, decay, d_decay = s
            rows = _chunk_rows(c)
            d_v = _mm(jnp.where(anti_causal, att_t, 0.0), d_o) + dv_inter
            dq_in = _mm(jnp.where(causal, d_att, 0.0), k_in)
            dk_in = _mm(jnp.where(anti_causal, d_att_t, 0.0), q_in)
            d_k = dk_in * e_k + dk_st * e_ks
            kk = dk_st * k_st
            d_b = dq_in * q_in + dq_int * q_int - dk_in * k_in - kk
            d_b_last = jnp.sum(kk, axis=0, keepdims=True) + decay * d_decay
            d_logf = _cumsum_rows(d_b, reverse=True) + d_b_last
            return d_v.astype(_MXU_DTYPE), dq_in * e_q + dq_int * e_qi, d_logf / f_scr[rows, :] - d_k

        def store_input_gradients(c, out):
            rows = _chunk_rows(c)
            dz_ref[2, rows, :], dqh_scr[rows, :], df_scr[rows, :] = out

        def input_activations(rows):
            q = z_ref[0, rows, :]
            sq = _sigmoid(q)
            dz_ref[0, rows, :] = (dqh_scr[rows, :] * (sq * (1.0 + q * (1.0 - sq)))).astype(_MXU_DTYPE)
            sg = _sigmoid(z_ref[1, rows, :])
            d_f = df_scr[rows, :]
            dz_ref[1, rows, :] = (d_f * (1.0 - lb) * sg * (1.0 - sg)).astype(_MXU_DTYPE)
            dlb_scr[...] += jnp.sum(d_f * (1.0 - sg), axis=0, keepdims=True)

        _independent_trips(n_chunks, output_gradient, store_output_gradient)
        _token_contractions(do_scr, qi_scr, dst_scr, seq)
        lax.fori_loop(0, n_chunks, carry_state_gradient, jnp.zeros((LANES, LANES), F32))
        _independent_trips(n_chunks, [score_gradients, input_gradients], store_input_gradients)
        dg_ref[...] += jnp.sum(dgp_scr[...], axis=0, keepdims=True)
        _row_blocks(seq, input_activations)
        d_l0 = dlb_scr[...] * lb * (1.0 - lb)
        dlog_ref[0:1, :] = d_l0
        dlog_ref[1:2, :] = -d_l0

    tokens = batch * seq
    seq_buf = pltpu.VMEM((seq, LANES), F32)
    chunk_rows = pltpu.VMEM((n_chunks, LANES), F32)
    return pl.pallas_call(
        body, name="branch_b_bwd",
        grid=(N_BLK, batch),
        in_specs=[pl.BlockSpec((4, seq, LANES), lambda h, b: (0, b, h)),
                  pl.BlockSpec((None, n_chunks, LANES, LANES), lambda h, b: (b * N_BLK + h, 0, 0, 0)),
                  pl.BlockSpec((4, seq, LANES), lambda h, b: (0, b, h)),
                  pl.BlockSpec((seq, LANES), lambda h, b: (b, h)),
                  pl.BlockSpec(memory_space=pl.ANY),
                  pl.BlockSpec((2, LANES), lambda h, b: (0, h)),
                  pl.BlockSpec((1, LANES), lambda h, b: (0, 0))],
        out_specs=[pl.BlockSpec((4, seq, LANES), lambda h, b: (0, b, h)),
                   pl.BlockSpec((2, LANES), lambda h, b: (0, h)),
                   pl.BlockSpec((1, LANES), lambda h, b: (0, 0))],
        out_shape=[jax.ShapeDtypeStruct((N_GROUPS, tokens, D_MODEL), _MXU_DTYPE),
                   jax.ShapeDtypeStruct((2, D_MODEL), F32),
                   jax.ShapeDtypeStruct((1, LANES), F32)],
        scratch_shapes=[seq_buf] * 4 + [chunk_rows, chunk_rows, pltpu.VMEM((1, LANES), F32),
                                        pltpu.VMEM((n_chunks, LANES, LANES), F32)],
        input_output_aliases={4: 0},
        compiler_params=_params(("arbitrary", "arbitrary")),
    )(z, states, kept, dyb, dz, lb_logits, hg_g)


def _merge_tail(ya, yb, z, x2d, tgt2d, b_merge, final_g, pa, pb, wo):
    tokens, d = x2d.shape
    tm = min(TAIL_TOKENS, tokens)
    n_tiles = tokens // tm

    def body(ya_ref, yb_ref, z_ref, x_ref, t_ref, bm_ref, fg_ref, pa_hbm, pb_hbm, wo_hbm,
             dya_ref, dyb_ref, dx2_ref, dz_ref, loss_ref, dfg_ref, dbm_ref, dpa_hbm, dpb_hbm, dwo_hbm,
             pa_s, pb_s, wo_s, dpa_s, dpb_s, dwo_s):
        i = pl.program_id(0)

        @pl.when(i == 0)
        def _():
            pltpu.sync_copy(pa_hbm, pa_s)
            pltpu.sync_copy(pb_hbm, pb_s)
            pltpu.sync_copy(wo_hbm, wo_s)
            dpa_s[...] = jnp.zeros_like(dpa_s)
            dpb_s[...] = jnp.zeros_like(dpb_s)
            dwo_s[...] = jnp.zeros_like(dwo_s)
            loss_ref[...] = jnp.zeros_like(loss_ref)
            dfg_ref[...] = jnp.zeros_like(dfg_ref)
            dbm_ref[...] = jnp.zeros_like(dbm_ref)

        ya_t = ya_ref[...]
        yb_t = yb_ref[...]
        out_a = _mm(ya_t, pa_s[...])
        out_b = _mm(yb_t, pb_s[...])
        g_a = _sigmoid(z_ref[0] + bm_ref[:, :d])
        g_b = _sigmoid(z_ref[1] + bm_ref[:, d:])
        mixed = g_a * out_a + g_b * out_b
        x2 = x_ref[...] + _mm(mixed, wo_s[...])
        r = lax.rsqrt(jnp.mean(x2 * x2, axis=-1, keepdims=True) + EPS)
        xn = x2 * r
        fg = fg_ref[...]
        diff = xn * fg - t_ref[...]
        loss_ref[...] += jnp.sum(diff * diff) * (0.5 / d)
        dy = diff * (1.0 / d)
        dfg_ref[...] += jnp.sum(dy * xn, axis=0, keepdims=True)
        dxn = dy * fg
        dx2 = r * (dxn - xn * jnp.mean(dxn * xn, axis=-1, keepdims=True))
        dx2_ref[...] = dx2
        dmixed = _mm_nt(dx2, wo_s[...])
        dwo_s[...] += _mm_tn(mixed, dx2)
        dgm_a = dmixed * out_a * g_a * (1.0 - g_a)
        dgm_b = dmixed * out_b * g_b * (1.0 - g_b)
        dz_ref[0] = dgm_a.astype(_MXU_DTYPE)
        dz_ref[1] = dgm_b.astype(_MXU_DTYPE)
        dbm_ref[:, :d] += jnp.sum(dgm_a, axis=0, keepdims=True)
        dbm_ref[:, d:] += jnp.sum(dgm_b, axis=0, keepdims=True)
        dout_a = dmixed * g_a
        dout_b = dmixed * g_b
        dpa_s[...] += _mm_tn(ya_t, dout_a)
        dpb_s[...] += _mm_tn(yb_t, dout_b)
        dya_ref[...] = _mm_nt(dout_a, pa_s[...])
        dyb_ref[...] = _mm_nt(dout_b, pb_s[...])

        @pl.when(i == n_tiles - 1)
        def _():
            pltpu.sync_copy(dpa_s, dpa_hbm)
            pltpu.sync_copy(dpb_s, dpb_hbm)
            pltpu.sync_copy(dwo_s, dwo_hbm)

    tile = pl.BlockSpec((tm, d), lambda i: (i, 0))
    gm = pl.BlockSpec((2, tm, d), lambda i: (3, i, 0))
    row = lambda n: pl.BlockSpec((1, n), lambda i: (0, 0))
    hbm = pl.BlockSpec(memory_space=pl.ANY)
    act = jax.ShapeDtypeStruct((tokens, d), F32)
    mat = jax.ShapeDtypeStruct((d, d), F32)
    return pl.pallas_call(
        body, name="merge_tail",
        grid=(n_tiles,),
        in_specs=[tile, tile, gm, tile, tile, row(2 * d), row(d), hbm, hbm, hbm],
        out_specs=[tile, tile, tile, gm, row(LANES), row(d), row(2 * d), hbm, hbm, hbm],
        out_shape=[act, act, act, jax.ShapeDtypeStruct((N_GROUPS, tokens, d), _MXU_DTYPE),
                   jax.ShapeDtypeStruct((1, LANES), F32), jax.ShapeDtypeStruct((1, d), F32),
                   jax.ShapeDtypeStruct((1, 2 * d), F32), mat, mat, mat],
        scratch_shapes=[pltpu.VMEM((d, d), _MXU_DTYPE)] * 3 + [pltpu.VMEM((d, d), F32)] * 3,
        compiler_params=_params(("arbitrary",)),
    )(ya, yb, z, x2d, tgt2d, b_merge, final_g, pa, pb, wo)


def _inproj_dw_exchange(h_t, dz, scatter):
    d, tokens = h_t.shape
    tm = min(CONTRACT_TOKENS, tokens)
    n_i = tokens // tm
    half = d // 2

    def body(h_ref, dz_ref, *rest):
        n_in, n_out = scatter.n_in, scatter.n_out
        dw_hbm, land_hbm = rest[n_in:n_in + 2]
        acc, local_sems, send_sems, recv_sems = rest[n_in + 2 + n_out:n_in + 6 + n_out]
        carried = scatter.copies(rest[:n_in], rest[n_in + 2:n_in + 2 + n_out], rest[n_in + 6 + n_out:])
        s, i = pl.program_id(0), pl.program_id(1)
        x, y, c, _ = _mesh_position()

        @pl.when((s == 0) & (i == 0))
        def _():
            for cp in carried:
                cp.start()

        part = _mm(h_ref[...], dz_ref[...])
        buf = acc.at[s % 2]

        @pl.when(i == 0)
        def _():
            buf[...] = part

        @pl.when(i != 0)
        def _():
            buf[...] += part

        def copies(k):
            g = _SLOT_TO_GROUP[k]
            cols = pl.ds((g % 2) * D_MODEL, D_MODEL)
            src = acc.at[k % 2]
            mine = pltpu.make_async_copy(src, dw_hbm.at[g // 2, :, cols], local_sems.at[k % 2])
            theirs = _remote(src.at[pl.ds((1 - c) * half, half), :], land_hbm.at[g // 2, :, cols],
                             send_sems, recv_sems, k, (x, y, 1 - c))
            return mine, theirs

        for k in range(N_GROUPS):
            @pl.when((s == k) & (i == n_i - 1))
            def _(k=k):
                if k > 0:
                    mine, theirs = copies(k - 1)
                    mine.wait()
                    theirs.wait_send()
                mine, theirs = copies(k)
                mine.start()
                theirs.start()
                if k == N_GROUPS - 1:
                    mine.wait()
                    theirs.wait_send()
                    for kk in range(N_GROUPS):
                        copies(kk)[1].wait_recv()
                    for cp in carried:
                        cp.wait()

    hbm = pl.BlockSpec(memory_space=pl.ANY)
    more = scatter.plumbing(first_operand=2, first_output=2)
    return pl.pallas_call(
        body, name="inproj_dw_exchange",
        grid=(N_GROUPS, n_i),
        in_specs=[pl.BlockSpec((d, tm), lambda s, i: (0, i)),
                  pl.BlockSpec((None, tm, D_MODEL), lambda s, i: (s, i, 0))] + more[1],
        out_specs=[hbm, hbm] + more[2],
        out_shape=[jax.ShapeDtypeStruct((N_SHARDS, d, 2 * D_MODEL), F32),
                   jax.ShapeDtypeStruct((N_SHARDS, half, 2 * D_MODEL), F32)] + more[3],
        scratch_shapes=[pltpu.VMEM((2, d, D_MODEL), F32), pltpu.SemaphoreType.DMA((2,)),
                        pltpu.SemaphoreType.DMA((N_GROUPS,)), pltpu.SemaphoreType.DMA((N_GROUPS,))] + more[4],
        input_output_aliases=more[5],
        compiler_params=_params(("arbitrary", "arbitrary")),
    )(h_t, dz, *more[0])


def _inproj_dx(dz, w_all, x2d, dx2, norm_g, scatter):
    tokens, d = x2d.shape
    tm = min(TAIL_TOKENS, tokens)
    n_tiles = tokens // tm

    def body(dz_ref, w_hbm, x_ref, dx2_ref, g_ref, *rest):
        n_in, n_out = scatter.n_in, scatter.n_out
        dx_ref, dg_ref = rest[n_in:n_in + 2]
        w_res = rest[n_in + 2 + n_out]
        copies = scatter.copies(rest[:n_in], rest[n_in + 2:n_in + 2 + n_out], rest[n_in + 3 + n_out:])
        i = pl.program_id(0)

        @pl.when(i == 0)
        def _():
            for cp in copies:
                cp.start()
            for slot, g in enumerate(_SLOT_TO_GROUP):
                pltpu.sync_copy(w_hbm.at[g // 2, :, pl.ds((g % 2) * D_MODEL, D_MODEL)],
                                w_res.at[:, pl.ds(slot * D_MODEL, D_MODEL)])
            dg_ref[...] = jnp.zeros_like(dg_ref)

        dz_all = jnp.concatenate([dz_ref[s] for s in range(N_GROUPS)], axis=1)
        dh = jnp.transpose(_mm_nt(w_res[...], dz_all))
        x = x_ref[...]
        r = lax.rsqrt(jnp.mean(x * x, axis=-1, keepdims=True) + EPS)
        xn = x * r
        dg_ref[...] += jnp.sum(dh * xn, axis=0, keepdims=True)
        dxn = dh * g_ref[...]
        dx_ref[...] = r * (dxn - xn * jnp.mean(dxn * xn, axis=-1, keepdims=True)) + dx2_ref[...]

        @pl.when(i == n_tiles - 1)
        def _():
            for cp in copies:
                cp.wait()

    tile = pl.BlockSpec((tm, d), lambda i: (i, 0))
    hbm = pl.BlockSpec(memory_space=pl.ANY)
    more = scatter.plumbing(first_operand=5, first_output=2)
    return pl.pallas_call(
        body, name="inproj_dx", grid=(n_tiles,),
        in_specs=[pl.BlockSpec((N_GROUPS, tm, D_MODEL), lambda i: (0, i, 0)), hbm, tile, tile,
                  pl.BlockSpec((1, d), lambda i: (0, 0))] + more[1],
        out_specs=[tile, pl.BlockSpec((1, d), lambda i: (0, 0))] + more[2],
        out_shape=[jax.ShapeDtypeStruct((tokens, d), F32), jax.ShapeDtypeStruct((1, d), F32)] + more[3],
        scratch_shapes=[pltpu.VMEM((d, N_GROUPS * D_MODEL), _MXU_DTYPE)] + more[4],
        input_output_aliases=more[5],
        compiler_params=_params(("arbitrary",)),
    )(dz, w_all, x2d, dx2, norm_g, *more[0])


def _row_tile(rows, cols, itemsize=4, budget=2 * 1024 * 1024):
    tr = rows
    while tr * cols * itemsize > budget and tr % 16 == 0:
        tr //= 2
    return tr


def _cast_into_slot(a, chip, dtype, name):
    rows, cols = a.shape
    tr = _row_tile(rows, cols)

    def body(chip_ref, a_ref, o_ref):
        del chip_ref
        o_ref[...] = a_ref[...].astype(dtype)

    grid_spec = pltpu.PrefetchScalarGridSpec(
        num_scalar_prefetch=1, grid=(rows // tr,),
        in_specs=[pl.BlockSpec((tr, cols), lambda i, chip_ref: (i, 0))],
        out_specs=pl.BlockSpec((None, tr, cols), lambda i, chip_ref: (chip_ref[0], i, 0)))
    return pl.pallas_call(body, name=name, grid_spec=grid_spec,
                          out_shape=jax.ShapeDtypeStruct((N_SHARDS, rows, cols), dtype),
                          compiler_params=_params(("arbitrary",)))(chip, a)


def _sum_slots(stack, name):
    n, rows, cols = stack.shape
    tr = _row_tile(rows, cols * n)

    def body(s_ref, o_ref):
        total = s_ref[0].astype(F32)
        for k in range(1, n):
            total = total + s_ref[k].astype(F32)
        o_ref[...] = total

    return pl.pallas_call(body, name=name, grid=(rows // tr,),
                          in_specs=[pl.BlockSpec((n, tr, cols), lambda i: (0, i, 0))],
                          out_specs=pl.BlockSpec((tr, cols), lambda i: (i, 0)),
                          out_shape=jax.ShapeDtypeStruct((rows, cols), F32),
                          compiler_params=_params(("parallel",)))(stack)


def _add_half(full, landed, place, name):
    n, rows, cols = full.shape
    half = rows // 2
    tr = _row_tile(half, cols)
    nb = half // tr

    def body(place_ref, a_ref, b_ref, o_ref, own_ref):
        total = (a_ref[...] + b_ref[...]).astype(_MXU_DTYPE)
        o_ref[...] = total

        @pl.when(pl.program_id(1) == place_ref[1])
        def _():
            own_ref[...] = total

    grid_spec = pltpu.PrefetchScalarGridSpec(
        num_scalar_prefetch=1, grid=(nb, n),
        in_specs=[pl.BlockSpec((None, tr, cols), lambda i, j, place_ref: (j, place_ref[0] * nb + i, 0)),
                  pl.BlockSpec((None, tr, cols), lambda i, j, place_ref: (j, i, 0))],
        out_specs=[pl.BlockSpec((None, tr, cols), lambda i, j, place_ref: (j, i, 0)),
                   pl.BlockSpec((None, tr, cols), lambda i, j, place_ref: (place_ref[1], i, 0))])
    shape = jax.ShapeDtypeStruct((n, half, cols), _MXU_DTYPE)
    return pl.pallas_call(body, name=name, grid_spec=grid_spec, out_shape=[shape, shape],
                          compiler_params=_params(("parallel", "arbitrary")))(place, full, landed)


def _adamw_update(w, grad, m, v):
    c1 = 1.0 - ADAM_B1 ** ADAM_STEP
    c2 = 1.0 - ADAM_B2 ** ADAM_STEP
    nm = ADAM_B1 * m + (1.0 - ADAM_B1) * grad
    nv = ADAM_B2 * v + (1.0 - ADAM_B2) * (grad * grad)
    return (-ADAM_LR) * ((nm / c1) / (jnp.sqrt(nv / c2) + ADAM_EPS) + ADAM_WD * w), nm, nv


def _adamw(w, g, m, v, name):
    rows, cols = w.shape
    tr = _row_tile(rows, cols, budget=1024 * 1024)

    def body(w_ref, g_ref, m_ref, v_ref, d_ref, nm_ref, nv_ref):
        d_ref[...], nm_ref[...], nv_ref[...] = _adamw_update(w_ref[...], g_ref[...], m_ref[...], v_ref[...])

    spec = pl.BlockSpec((tr, cols), lambda i: (i, 0))
    shape = jax.ShapeDtypeStruct((rows, cols), F32)
    return pl.pallas_call(body, name=name, grid=(rows // tr,), in_specs=[spec] * 4, out_specs=[spec] * 3,
                          out_shape=[shape] * 3, compiler_params=_params(("parallel",)))(w, g, m, v)


def _adamw_halves(w, g_mine, g_sibling, m, v, core, name):
    rows, cols = w.shape
    half = rows // 2
    tr = _row_tile(half, cols, budget=1024 * 1024)
    nb = half // tr

    def body(core_ref, w_ref, gm_ref, gs_ref, m_ref, v_ref, g_ref, d_ref, nm_ref, nv_ref):
        mine = pl.program_id(0) // nb == core_ref[0]
        grad = jnp.where(mine, gm_ref[...], gs_ref[...])
        g_ref[...] = grad
        d_ref[...], nm_ref[...], nv_ref[...] = _adamw_update(w_ref[...], grad, m_ref[...], v_ref[...])

    spec = pl.BlockSpec((tr, cols), lambda i, core_ref: (i, 0))
    half_spec = pl.BlockSpec((tr, cols), lambda i, core_ref: (i % nb, 0))
    grid_spec = pltpu.PrefetchScalarGridSpec(num_scalar_prefetch=1, grid=(rows // tr,),
                                             in_specs=[spec, half_spec, half_spec, spec, spec], out_specs=[spec] * 4)
    shape = jax.ShapeDtypeStruct((rows, cols), F32)
    return pl.pallas_call(body, name=name, grid_spec=grid_spec, out_shape=[shape] * 4,
                          compiler_params=_params(("parallel",)))(core, w, g_mine, g_sibling, m, v)


def _local_step(x, loss_target, gather, reduction, b_merge, conv_b, rg_wx, rg_bx, rg_wa, rg_ba, rg_lambda,
                hg_lb_logits, hg_norm_g, norm_g, final_norm_g):
    batch, seq, d = x.shape
    x2d = x.reshape(batch * seq, d)
    tgt2d = loss_target.reshape(batch * seq, d)
    z, h_t, (w_all, pa, pb, wo), cw_all = _inproj_fwd_gather(x2d, norm_g, *gather)
    pa, pb, wo = (t.reshape(d, d) for t in (pa, pb, wo))
    conv_w = jnp.transpose(cw_all, (1, 0, 2)).reshape(CONV_WIDTH, d)
    lru = (conv_w, conv_b, rg_wx, rg_bx, rg_wa, rg_ba, rg_lambda)
    ya, hl, kept = _branch_a_fwd(z, *lru, batch, seq)
    yb, states, kept_b = _branch_b_fwd(z, hg_lb_logits, hg_norm_g, batch, seq)
    dya, dyb, dx2, dz, loss, d_final_g, d_b_merge, d_pa, d_pb, d_wo = _merge_tail(
        ya, yb, z, x2d, tgt2d, b_merge, final_norm_g, pa, pb, wo)
    dz, d_lb_logits, d_hg_g = _branch_b_bwd(z, states, kept_b, dyb, dz, hg_lb_logits, hg_norm_g, batch, seq)
    dz, d_conv_w, d_conv_b, d_wx, d_bx, d_wa, d_ba, d_lam = _branch_a_bwd(
        z, hl, kept, dya, dz, conv_w, rg_wx, rg_wa, rg_lambda, batch, seq)
    small = dict(b_merge=d_b_merge, conv_w=d_conv_w, conv_b=d_conv_b, rg_wx=d_wx, rg_bx=d_bx, rg_wa=d_wa,
                 rg_ba=d_ba, rg_lambda=d_lam, hg_lb_logits=d_lb_logits, hg_norm_g=d_hg_g,
                 norm_g=jnp.zeros((1, d), F32), final_norm_g=d_final_g)
    first, second = reduction
    d_w_in, landed_w_in, *scattered_first = _inproj_dw_exchange(h_t, dz, first((d_pa, d_pb, d_wo), small))
    grad_x, d_norm_g, *scattered_second = _inproj_dx(dz, w_all, x2d, dx2, norm_g, scatter=second(d_w_in, landed_w_in))
    return loss[0, 0], grad_x.reshape(batch, seq, d), d_norm_g, (scattered_first, scattered_second)


_SMALL_ORDER = ("b_merge", "conv_w", "conv_b", "rg_wx", "rg_bx", "rg_wa", "rg_ba", "rg_lambda", "hg_lb_logits",
                "hg_norm_g", "norm_g", "final_norm_g")
N_DEV = 8
PIECE_ROWS = 272


def _pack_small(tree):
    flat = jnp.concatenate([tree[k].reshape(-1) for k in _SMALL_ORDER])
    flat = jnp.pad(flat, (0, N_DEV * PIECE_ROWS * LANES - flat.shape[0]))
    return flat.reshape(N_DEV * PIECE_ROWS, LANES)


def _unpack_small(packed, like):
    flat = packed.reshape(-1)
    out, pos = {}, 0
    for k in _SMALL_ORDER:
        n = like[k].size
        out[k] = flat[pos:pos + n].reshape(like[k].shape)
        pos += n
    return out


def _mesh_position():
    x, y, c = lax.axis_index("x"), lax.axis_index("y"), lax.axis_index("c")
    other_chips = [(1 - x, y), (x, 1 - y), (1 - x, 1 - y)]
    return x, y, c, other_chips


def _other_devices(x, y, c):
    flips = [(fx, fy, fc) for fx in (0, 1) for fy in (0, 1) for fc in (0, 1) if (fx, fy, fc) != (0, 0, 0)]
    return [(jnp.where(fx, 1 - x, x), jnp.where(fy, 1 - y, y), jnp.where(fc, 1 - c, c)) for fx, fy, fc in flips]


def _remote(src, dst, send_sems, recv_sems, k, device):
    return pltpu.make_async_remote_copy(src_ref=src, dst_ref=dst, send_sem=send_sems.at[k], recv_sem=recv_sems.at[k],
                                        device_id=device, device_id_type=MESH)


def _exchange_halves(bigs, small):
    n_big = len(bigs)
    n_sem = n_big + N_DEV - 1

    def body(*refs):
        srcs, small_src = refs[:n_big], refs[n_big]
        outs, small_out = refs[n_big + 1:2 * n_big + 1], refs[2 * n_big + 1]
        send_sems, recv_sems, local_sem = refs[2 * n_big + 2:]
        x, y, c, _ = _mesh_position()
        me, sibling = 4 * x + 2 * y + c, (x, y, 1 - c)
        mine = pltpu.make_async_copy(small_src.at[pl.ds(me * PIECE_ROWS, PIECE_ROWS), :], small_out.at[me], local_sem)
        mine.start()
        copies = []
        for a in range(n_big):
            hs = srcs[a].shape[1] // 2
            copies.append(_remote(srcs[a].at[:, pl.ds((1 - c) * hs, hs), :], outs[a], send_sems, recv_sems, a, sibling))
        for k, (px, py, pc) in enumerate(_other_devices(x, y, c)):
            piece = small_src.at[pl.ds((4 * px + 2 * py + pc) * PIECE_ROWS, PIECE_ROWS), :]
            copies.append(_remote(piece, small_out.at[me], send_sems, recv_sems, n_big + k, (px, py, pc)))
        for cp in copies:
            cp.start()
        for cp in copies:
            cp.wait()
        mine.wait()

    hbm = pl.BlockSpec(memory_space=pl.ANY)
    out_shape = [jax.ShapeDtypeStruct((g.shape[0], g.shape[1] // 2, g.shape[2]), F32) for g in bigs]
    out_shape.append(jax.ShapeDtypeStruct((N_DEV, PIECE_ROWS, LANES), F32))
    return pl.pallas_call(
        body, name="exchange_halves",
        in_specs=[hbm] * (n_big + 1), out_specs=[hbm] * (n_big + 1), out_shape=out_shape,
        scratch_shapes=[pltpu.SemaphoreType.DMA((n_sem,)), pltpu.SemaphoreType.DMA((n_sem,)), pltpu.SemaphoreType.DMA],
    )(*bigs, small)


class _Scatter:
    def __init__(self, bigs, by_chip, small=None):
        self.bigs, self.by_chip, self.small = list(bigs), list(by_chip), small
        self.n_big = len(self.bigs)
        self.n_in = 2 * self.n_big + (small is not None)
        self.n_out = self.n_big + (small is not None)
        self.n_scratch = 2 + (small is not None)

    def plumbing(self, first_operand, first_output):
        hbm = pl.BlockSpec(memory_space=pl.ANY)
        n_sem = 3 * self.n_big + (N_DEV - 1 if self.small is not None else 0)
        operands = self.bigs + self.by_chip + ([self.small] if self.small is not None else [])
        out_shapes = [jax.ShapeDtypeStruct(g.shape, g.dtype) for g in self.by_chip]
        scratch = [pltpu.SemaphoreType.DMA((n_sem,)), pltpu.SemaphoreType.DMA((n_sem,))]
        if self.small is not None:
            out_shapes.append(jax.ShapeDtypeStruct((N_DEV, PIECE_ROWS, LANES), F32))
            scratch.append(pltpu.SemaphoreType.DMA)
        aliases = {first_operand + self.n_big + a: first_output + a for a in range(self.n_big)}
        return operands, [hbm] * self.n_in, [hbm] * self.n_out, out_shapes, scratch, aliases

    def copies(self, in_refs, out_refs, scratch_refs):
        srcs, outs = in_refs[:self.n_big], out_refs[:self.n_big]
        send_sems, recv_sems = scratch_refs[:2]
        x, y, c, chips = _mesh_position()
        chip, me = 2 * x + y, 4 * x + 2 * y + c
        copies = []
        for a in range(self.n_big):
            for j, (cx, cy) in enumerate(chips):
                copies.append(_remote(srcs[a].at[2 * cx + cy], outs[a].at[chip], send_sems, recv_sems, 3 * a + j,
                                      (cx, cy, c)))
        if self.small is not None:
            small_src, small_out = in_refs[2 * self.n_big], out_refs[self.n_big]
            copies.append(pltpu.make_async_copy(small_src, small_out.at[me], scratch_refs[2]))
            for k, peer in enumerate(_other_devices(x, y, c)):
                copies.append(_remote(small_src, small_out.at[me], send_sems, recv_sems, 3 * self.n_big + k, peer))
        return copies


def _swap_halves(halves, vec):
    n_big = len(halves)

    def body(*refs):
        srcs, vec_src = refs[:n_big], refs[n_big]
        outs, vec_out = refs[n_big + 1:2 * n_big + 1], refs[2 * n_big + 1]
        send_sems, recv_sems, local_sem = refs[2 * n_big + 2:]
        x, y, c, _ = _mesh_position()
        me = 4 * x + 2 * y + c
        copies = [pltpu.make_async_copy(vec_src, vec_out.at[me], local_sem)]
        copies += [_remote(srcs[a], outs[a], send_sems, recv_sems, a, (x, y, 1 - c)) for a in range(n_big)]
        copies += [_remote(vec_src, vec_out.at[me], send_sems, recv_sems, n_big + k, peer)
                   for k, peer in enumerate(_other_devices(x, y, c))]
        for cp in copies:
            cp.start()
        for cp in copies:
            cp.wait()

    hbm = pl.BlockSpec(memory_space=pl.ANY)
    n_sem = n_big + N_DEV - 1
    return pl.pallas_call(
        body, name="swap_halves",
        in_specs=[hbm] * (n_big + 1), out_specs=[hbm] * (n_big + 1),
        out_shape=[jax.ShapeDtypeStruct(h.shape, F32) for h in halves] + [jax.ShapeDtypeStruct((N_DEV,) + vec.shape, F32)],
        scratch_shapes=[pltpu.SemaphoreType.DMA((n_sem,)), pltpu.SemaphoreType.DMA((n_sem,)), pltpu.SemaphoreType.DMA],
    )(*halves, vec)


def kernel(x, w_in, b_merge, conv_w, conv_b, rg_wx, rg_bx, rg_wa, rg_ba, rg_lambda, hg_lb_logits, hg_norm_g, proj_a, proj_b, w_out, norm_g, final_norm_g, loss_target, m_w_in, m_b_merge, m_conv_w, m_conv_b, m_rg_wx, m_rg_bx, m_rg_wa, m_rg_ba, m_rg_lambda, m_hg_lb_logits, m_hg_norm_g, m_proj_a, m_proj_b, m_w_out, m_norm_g, m_final_norm_g, v_w_in, v_b_merge, v_conv_w, v_conv_b, v_rg_wx, v_rg_bx, v_rg_wa, v_rg_ba, v_rg_lambda, v_hg_lb_logits, v_hg_norm_g, v_proj_a, v_proj_b, v_w_out, v_norm_g, v_final_norm_g):
    d = D_MODEL
    weights = dict(w_in=w_in, b_merge=b_merge, conv_w=conv_w, conv_b=conv_b, rg_wx=rg_wx, rg_bx=rg_bx, rg_wa=rg_wa,
                   rg_ba=rg_ba, rg_lambda=rg_lambda, hg_lb_logits=hg_lb_logits, hg_norm_g=hg_norm_g, proj_a=proj_a,
                   proj_b=proj_b, w_out=w_out, norm_g=norm_g, final_norm_g=final_norm_g)
    m = dict(w_in=m_w_in, b_merge=m_b_merge, conv_w=m_conv_w, conv_b=m_conv_b, rg_wx=m_rg_wx, rg_bx=m_rg_bx,
             rg_wa=m_rg_wa, rg_ba=m_rg_ba, rg_lambda=m_rg_lambda, hg_lb_logits=m_hg_lb_logits, hg_norm_g=m_hg_norm_g,
             proj_a=m_proj_a, proj_b=m_proj_b, w_out=m_w_out, norm_g=m_norm_g, final_norm_g=m_final_norm_g)
    v = dict(w_in=v_w_in, b_merge=v_b_merge, conv_w=v_conv_w, conv_b=v_conv_b, rg_wx=v_rg_wx, rg_bx=v_rg_bx,
             rg_wa=v_rg_wa, rg_ba=v_rg_ba, rg_lambda=v_rg_lambda, hg_lb_logits=v_hg_lb_logits, hg_norm_g=v_hg_norm_g,
             proj_a=v_proj_a, proj_b=v_proj_b, w_out=v_w_out, norm_g=v_norm_g, final_norm_g=v_final_norm_g)
    big_names = ("w_in", "proj_a", "proj_b", "w_out")

    core = lax.axis_index("c").astype(jnp.int32).reshape(1)
    chip = (2 * lax.axis_index("x") + lax.axis_index("y")).astype(jnp.int32)

    slotted = [_cast_into_slot(weights[k][0], chip.reshape(1), _MXU_DTYPE, f"cast_{k}") for k in big_names]
    conv_slotted = _cast_into_slot(conv_w[0], chip.reshape(1), F32, "slot_conv_w")

    small_shapes = {}

    place = jnp.concatenate([core, chip.reshape(1)])

    def reduce_proj_and_small(proj_grads, small_grads):
        small_shapes.update({k: t.shape for k, t in small_grads.items()})
        bigs = [g.reshape(N_SHARDS, d // N_SHARDS, d) for g in proj_grads]
        *landed, small_landed = _exchange_halves(bigs, _pack_small(small_grads))
        sums = [_add_half(g, l, place, f"add_half_{1 + a}") for a, (g, l) in enumerate(zip(bigs, landed))]
        return _Scatter([s[0] for s in sums], [s[1] for s in sums], _sum_slots(small_landed, "sum_small"))

    def reduce_w_in(d_w_in, landed):
        partial, own_slot = _add_half(d_w_in, landed, place, "add_half_0")
        return _Scatter([partial], [own_slot])

    loss_part, grad_x, d_norm_g, ((*by_chip_proj, small_all), by_chip_w_in) = _local_step(
        x, loss_target, (slotted, conv_slotted, chip.reshape(1)), (reduce_proj_and_small, reduce_w_in),
        b_merge, conv_b, rg_wx[0], rg_bx.reshape(1, d), rg_wa[0], rg_ba.reshape(1, d), rg_lambda, hg_lb_logits,
        hg_norm_g, norm_g, final_norm_g.reshape(1, d))
    mine = [_sum_slots(s, f"sum_chips_{a}") for a, s in enumerate(by_chip_w_in + by_chip_proj)]
    late = jnp.concatenate([d_norm_g.reshape(SUBLANES, LANES), jnp.full((SUBLANES, LANES), loss_part, F32)])
    *theirs, late_parts = _swap_halves(mine, late)
    late_sum = _sum_slots(late_parts, "sum_late")
    loss = late_sum[SUBLANES, 0]
    small_red = _unpack_small(small_all, {k: jax.ShapeDtypeStruct(s, F32) for k, s in small_shapes.items()})
    small_red["norm_g"] = late_sum[:SUBLANES].reshape(1, d)

    grads, delta, new_m, new_v = {}, {}, {}, {}
    for k, g_mine, g_theirs in zip(big_names, mine, theirs):
        out = _adamw_halves(weights[k][0], g_mine, g_theirs, m[k][0], v[k][0], core, f"adamw_{k}")
        grads[k], delta[k], new_m[k], new_v[k] = (t.reshape(weights[k].shape) for t in out)
    cols = d // N_SHARDS
    g_conv = lax.dynamic_slice(small_red["conv_w"], (0, chip * cols), (CONV_WIDTH, cols))
    grads["conv_w"] = g_conv.reshape(conv_w.shape)
    dl, nm, nv = _adamw(conv_w[0], g_conv, m_conv_w[0], v_conv_w[0], "adamw_conv_w")
    delta["conv_w"], new_m["conv_w"], new_v["conv_w"] = (t.reshape(conv_w.shape) for t in (dl, nm, nv))
    rest = [k for k in _SMALL_ORDER if k != "conv_w"]
    like = {k: (weights[k] if k != "conv_w" else jnp.zeros((CONV_WIDTH, d), F32)) for k in _SMALL_ORDER}
    packs = [_pack_small({k: (t[k] if k != "conv_w" else like[k]) for k in _SMALL_ORDER}) for t in (weights, m, v)]
    g_pack = _pack_small({k: small_red[k].reshape(like[k].shape) for k in _SMALL_ORDER})
    outs = [_unpack_small(p, like) for p in _adamw(packs[0], g_pack, packs[1], packs[2], "adamw_small")]
    for k in rest:
        grads[k] = small_red[k].reshape(weights[k].shape)
        delta[k], new_m[k], new_v[k] = outs[0][k], outs[1][k], outs[2][k]

    order = ("w_in", "b_merge", "conv_w", "conv_b", "rg_wx", "rg_bx", "rg_wa", "rg_ba", "rg_lambda", "hg_lb_logits",
             "hg_norm_g", "proj_a", "proj_b", "w_out", "norm_g", "final_norm_g")
    return (loss, grad_x, *[grads[k] for k in order], *[delta[k] for k in order], *[new_m[k] for k in order],
            *[new_v[k] for k in order])
```

```python
import functools

import jax
import jax.numpy as jnp
from jax import lax
from jax.experimental import pallas as pl
from jax.experimental.pallas import tpu as pltpu

F32 = jnp.float32
_MXU_DTYPE = jnp.bfloat16

D_MODEL = 1024
LANES = 128
SUBLANES = 8
N_BLK = D_MODEL // LANES
N_GROUPS = 8
N_SHARDS = 4
CONV_WIDTH = 4
LRU_C = 8.0
CHUNK = 64
CHUNKS_IN_FLIGHT = 16
HG_SCALE = float(LANES) ** -0.5
EPS = 1e-6
ADAM_LR, ADAM_B1, ADAM_B2, ADAM_EPS, ADAM_WD, ADAM_STEP = 0.001, 0.9, 0.999, 1e-08, 0.01, 10
MATMUL_TOKENS = 512
TAIL_TOKENS = 256
CONTRACT_TOKENS = 2048
VMEM_LIMIT = 56 * 1024 * 1024
VMEM_LIMIT_BIG = 60 * 1024 * 1024
MESH = pl.DeviceIdType.MESH

_SLOT_TO_GROUP = (2, 3, 4, 5, 0, 1, 6, 7)


def _mm(a, b):
    return lax.dot_general(a.astype(_MXU_DTYPE), b.astype(_MXU_DTYPE), (((1,), (0,)), ((), ())),
                           preferred_element_type=F32)


def _mm_nt(a, b):
    return lax.dot_general(a.astype(_MXU_DTYPE), b.astype(_MXU_DTYPE), (((1,), (1,)), ((), ())),
                           preferred_element_type=F32)


def _mm_tn(a, b):
    return lax.dot_general(a.astype(_MXU_DTYPE), b.astype(_MXU_DTYPE), (((0,), (0,)), ((), ())),
                           preferred_element_type=F32)


def _sigmoid(x):
    return 0.5 * jnp.tanh(0.5 * x) + 0.5


def _log1p_pos(y):
    series = y * (1.0 - y * (0.5 - y * (1.0 / 3.0 - y * 0.25)))
    return jnp.where(y < 0.01, series, jnp.log(1.0 + y))


def _softplus(x):
    return jnp.maximum(x, 0.0) + _log1p_pos(jnp.exp(-jnp.abs(x)))


def _shift_down(x, n):
    rolled = pltpu.roll(x, n, 0)
    edge = SUBLANES if (n < SUBLANES and x.shape[0] > SUBLANES) else x.shape[0]
    rows = lax.broadcasted_iota(jnp.int32, (edge, x.shape[1]), 0)
    head = jnp.where(rows >= n, rolled[:edge], 0.0)
    return head if edge == x.shape[0] else jnp.concatenate([head, rolled[edge:]], axis=0)


def _shift_up(x, n):
    size = x.shape[0]
    rolled = pltpu.roll(x, size - n, 0)
    edge = SUBLANES if (n < SUBLANES and size > SUBLANES) else size
    rows = lax.broadcasted_iota(jnp.int32, (edge, x.shape[1]), 0)
    tail = jnp.where(rows < edge - n, rolled[size - edge:], 0.0)
    return tail if edge == size else jnp.concatenate([rolled[:size - edge], tail], axis=0)


def _params(dims, vmem=VMEM_LIMIT):
    return pltpu.CompilerParams(dimension_semantics=dims, vmem_limit_bytes=vmem)


def _slot_of_group(g):
    return jnp.where(g < 2, g + 4, jnp.where(g < 6, g - 2, g))


def _inproj_fwd_gather(x2d, norm_g, slotted, conv_slotted, chip):
    tokens, d = x2d.shape
    tm = min(MATMUL_TOKENS, tokens)
    n_tiles = tokens // tm
    n_big = len(slotted)
    n_sem = 6 * (n_big + 1) + 3
    last_pass = N_GROUPS - 1

    def shard_of(k, chip_id):
        x, y = chip_id // 2, chip_id % 2
        return 2 * jnp.where(k % 2 == 1, 1 - x, x) + jnp.where(k // 2 == 1, 1 - y, y)

    def body(chip_ref, x_ref, g_ref, *rest):
        bufs, cw = rest[n_big + 1:2 * n_big + 1], rest[2 * n_big + 1]
        z_ref, ht_ref = rest[2 * n_big + 2:2 * n_big + 4]
        h_all, slab, send_sems, recv_sems, slab_sems = rest[2 * n_big + 4:]
        del chip_ref
        p, i = pl.program_id(0), pl.program_id(1)
        x, y, c, chips = _mesh_position()
        me, sibling = 2 * x + y, (x, y, 1 - c)

        pieces = [(0, 0), (0, 1)] + [(a, None) for a in range(1, n_big)]

        def half(piece, slot, which):
            a, q = pieces[piece]
            hs = bufs[a].shape[1] // 2
            cols = slice(None) if q is None else pl.ds(q * D_MODEL, D_MODEL)
            return bufs[a].at[slot, pl.ds(which * hs, hs), cols]

        def send(piece, j):
            mine = half(piece, me, c)
            return _remote(mine, mine, send_sems, recv_sems, 6 * piece + j, (chips[j][0], chips[j][1], c))

        def arrival(piece, j):
            landed = half(piece, 2 * chips[j][0] + chips[j][1], c)
            return _remote(landed, landed, send_sems, recv_sems, 6 * piece + j, (chips[j][0], chips[j][1], c))

        def passed_on(piece, j, which):
            landed = half(piece, 2 * chips[j][0] + chips[j][1], which)
            return _remote(landed, landed, send_sems, recv_sems, 6 * piece + 3 + j, sibling)

        def conv_copy(j, slot):
            return _remote(cw.at[slot], cw.at[slot], send_sems, recv_sems, 6 * len(pieces) + j,
                           (chips[j][0], chips[j][1], c))

        def land(piece, j):
            arrival(piece, j).wait_recv()
            passed_on(piece, j, c).start()
            passed_on(piece, j, 1 - c).wait_recv()

        def slab_copy(pv):
            src = bufs[0].at[shard_of(pv // 2, me), :, pl.ds((pv % 2) * D_MODEL, D_MODEL)]
            return pltpu.make_async_copy(src, slab.at[pv % 2], slab_sems.at[pv % 2])

        @pl.when((p == 0) & (i == 0))
        def _():
            for q in range(2):
                send(q, 0).start()
                send(q, 1).start()
            slab_copy(0).start()

        @pl.when(i == 0)
        def _():
            for pv in range(N_GROUPS):
                @pl.when(p == pv)
                def _(pv=pv):
                    slab_copy(pv).wait()

        rows = pl.ds(pl.multiple_of(i * tm, tm), tm)

        @pl.when(p == 0)
        def _():
            xt = x_ref[...]
            r = lax.rsqrt(jnp.mean(xt * xt, axis=-1, keepdims=True) + EPS)
            h = (xt * r) * g_ref[...]
            h_all[rows, :] = h.astype(_MXU_DTYPE)
            ht_ref[...] = jnp.transpose(h).astype(_MXU_DTYPE)

        z_ref[...] = _mm(h_all[rows, :], slab[p % 2])

        def relay(q):
            landed = half(q, 2 * chips[q][0] + chips[q][1], c)
            to = chips[1 - q]
            return _remote(landed, landed, send_sems, recv_sems, 6 * q + 2, (to[0], to[1], c))

        landings = {1: [(0, 0)], 2: [(1, 0), (1, 1)], 3: [(0, 1)], 5: [(0, 2)], 6: [(1, 2)]}

        def end_of_pass(pv):
            for q, j in landings.get(pv - 1, []):
                land(q, j)
                if j == q:
                    relay(q).start()
            if pv - 1 == 1:
                for piece in range(2, len(pieces)):
                    for jj in range(3):
                        send(piece, jj).start()
                for jj in range(3):
                    conv_copy(jj, me).start()
            if pv - 1 in (5, 6):
                for piece in range(2, len(pieces)):
                    for jj in ((0, 1) if pv - 1 == 5 else (2,)):
                        land(piece, jj)
            slab_copy(pv).start()

        @pl.when(i == n_tiles - 1)
        def _():
            for pv in range(1, N_GROUPS):
                pl.when(p == pv - 1)(functools.partial(end_of_pass, pv))

        @pl.when((p == last_pass) & (i == n_tiles - 1))
        def _():
            for j in range(3):
                conv_copy(j, 2 * chips[j][0] + chips[j][1]).wait_recv()
            for piece in range(len(pieces)):
                for j in range(3):
                    (relay(piece) if (piece < 2 and j == 2) else send(piece, j)).wait_send()
                    passed_on(piece, j, c).wait_send()
            for j in range(3):
                conv_copy(j, me).wait_send()

    def z_index(p, i, chip_ref):
        g = 2 * shard_of(p // 2, chip_ref[0]) + p % 2
        return (_slot_of_group(g), i, 0)

    def first_pass_tile(p, i, chip_ref):
        return jnp.where(p == 0, i, n_tiles - 1)

    hbm = pl.BlockSpec(memory_space=pl.ANY)
    operands = list(slotted) + [conv_slotted]
    grid_spec = pltpu.PrefetchScalarGridSpec(
        num_scalar_prefetch=1, grid=(N_GROUPS, n_tiles),
        in_specs=[pl.BlockSpec((tm, d), lambda p, i, chip_ref: (first_pass_tile(p, i, chip_ref), 0)),
                  pl.BlockSpec((1, d), lambda p, i, chip_ref: (0, 0))] + [hbm] * (n_big + 1),
        out_specs=[hbm] * (n_big + 1) + [pl.BlockSpec((None, tm, D_MODEL), z_index),
                                         pl.BlockSpec((d, tm), lambda p, i, chip_ref: (0, first_pass_tile(p, i, chip_ref)))],
        scratch_shapes=[pltpu.VMEM((tokens, d), _MXU_DTYPE), pltpu.VMEM((2, d, D_MODEL), _MXU_DTYPE),
                        pltpu.SemaphoreType.DMA((n_sem,)), pltpu.SemaphoreType.DMA((n_sem,)),
                        pltpu.SemaphoreType.DMA((2,))])
    out = pl.pallas_call(
        body, name="inproj_fwd_gather", grid_spec=grid_spec,
        out_shape=[jax.ShapeDtypeStruct(a.shape, a.dtype) for a in operands]
        + [jax.ShapeDtypeStruct((N_GROUPS, tokens, D_MODEL), F32), jax.ShapeDtypeStruct((d, tokens), _MXU_DTYPE)],
        input_output_aliases={3 + a: a for a in range(n_big + 1)},
        compiler_params=_params(("arbitrary", "arbitrary")),
    )(chip, x2d, norm_g, *operands)
    return out[n_big + 1], out[n_big + 2], out[:n_big], out[n_big]


def _lane_blocks(x):
    return [x[:, k * LANES:(k + 1) * LANES] for k in range(x.shape[1] // LANES)]


def _block_diag(x, w_ref, transposed=False):
    mm = _mm_nt if transposed else _mm
    return jnp.concatenate([mm(xk, w_ref[k]) for k, xk in enumerate(_lane_blocks(x))], axis=1)


def _lru_decay(gr, sp):
    log_a = (-LRU_C) * gr * sp
    a = jnp.exp(log_a)
    y = 2.0 * log_a
    mult_sq = jnp.where(y > -1e-3, -y * (1.0 + 0.5 * y), 1.0 - a * a)
    inv_mult = lax.rsqrt(jnp.maximum(mult_sq, 1e-37))
    return a, mult_sq * inv_mult, inv_mult


def _tile_rows(width):
    return lax.broadcasted_iota(jnp.int32, (SUBLANES, width), 0)


def _scan_forward(a_scr, u_scr, h_scr, seq):
    width = a_scr.shape[1]
    rows = _tile_rows(width)

    def tile(j, carry):
        sl = pl.ds(pl.multiple_of(j * SUBLANES, SUBLANES), SUBLANES)
        a = a_scr[sl, :]
        u = u_scr[sl, :]
        for d in (1, 2, 4):
            keep = rows >= d
            a_sh = jnp.where(keep, pltpu.roll(a, d, 0), 1.0)
            u_sh = jnp.where(keep, pltpu.roll(u, d, 0), 0.0)
            u = a * u_sh + u
            a = a * a_sh
        h = u + a * carry
        h_scr[sl, :] = h
        return jnp.broadcast_to(h[SUBLANES - 1:SUBLANES, :], (SUBLANES, width))

    lax.fori_loop(0, seq // SUBLANES, tile, jnp.zeros((SUBLANES, width), F32))


def _scan_backward(c_scr, d_scr, g_scr, seq):
    width = c_scr.shape[1]
    rows = _tile_rows(width)
    n_tiles = seq // SUBLANES

    def tile(jj, carry):
        j = n_tiles - 1 - jj
        sl = pl.ds(pl.multiple_of(j * SUBLANES, SUBLANES), SUBLANES)
        c = c_scr[sl, :]
        g = d_scr[sl, :]
        for d in (1, 2, 4):
            keep = rows < SUBLANES - d
            c_sh = jnp.where(keep, pltpu.roll(c, SUBLANES - d, 0), 1.0)
            g_sh = jnp.where(keep, pltpu.roll(g, SUBLANES - d, 0), 0.0)
            g = c * g_sh + g
            c = c * c_sh
        g = g + c * carry
        g_scr[sl, :] = g
        return jnp.broadcast_to(g[0:1, :], (SUBLANES, width))

    lax.fori_loop(0, n_tiles, tile, jnp.zeros((SUBLANES, width), F32))


LRU_BLOCKS_PER_STEP = 2
LRU_LANES = LRU_BLOCKS_PER_STEP * LANES
LRU_STEPS = N_BLK // LRU_BLOCKS_PER_STEP


def _branch_a_fwd(z, conv_w, conv_b, wx, bx, wa, ba, lam, batch, seq):
    tokens = batch * seq

    def body(z_ref, cw_ref, cb_ref, wx_ref, bx_ref, wa_ref, ba_ref, lam_ref, ya_ref, hl_ref, kept_ref, a_scr, u_scr):
        xa = z_ref[0]
        ga = z_ref[1]
        xc = (cb_ref[...] + cw_ref[3:4, :] * xa + cw_ref[2:3, :] * _shift_down(xa, 1)
              + cw_ref[1:2, :] * _shift_down(xa, 2) + cw_ref[0:1, :] * _shift_down(xa, 3))
        gi = _sigmoid(_block_diag(xc, wx_ref) + bx_ref[...])
        gr = _sigmoid(_block_diag(xc, wa_ref) + ba_ref[...])
        a, mult, _ = _lru_decay(gr, _softplus(-lam_ref[...]))
        kept_ref[0], kept_ref[1], kept_ref[2] = xc, gi, gr
        a_scr[...] = a
        u_scr[...] = mult * gi * xc
        _scan_forward(a_scr, u_scr, hl_ref, seq)
        ya_ref[...] = (hl_ref[...] * (ga * _sigmoid(ga))).astype(_MXU_DTYPE)

    blk = pl.BlockSpec((seq, LRU_LANES), lambda b, c: (b, c))
    vec = pl.BlockSpec((1, LRU_LANES), lambda b, c: (0, c))
    mat = pl.BlockSpec((LRU_BLOCKS_PER_STEP, LANES, LANES), lambda b, c: (c, 0, 0))
    return pl.pallas_call(
        body, name="branch_a_fwd",
        grid=(batch, LRU_STEPS),
        in_specs=[pl.BlockSpec((2, seq, LRU_LANES), lambda b, c: (2, b, c)),
                  pl.BlockSpec((CONV_WIDTH, LRU_LANES), lambda b, c: (0, c)), vec, mat, vec, mat, vec, vec],
        out_specs=[blk, blk, pl.BlockSpec((3, seq, LRU_LANES), lambda b, c: (0, b, c))],
        out_shape=[jax.ShapeDtypeStruct((tokens, D_MODEL), _MXU_DTYPE), jax.ShapeDtypeStruct((tokens, D_MODEL), F32),
                   jax.ShapeDtypeStruct((3, tokens, D_MODEL), F32)],
        scratch_shapes=[pltpu.VMEM((seq, LRU_LANES), F32), pltpu.VMEM((seq, LRU_LANES), F32)],
        compiler_params=_params(("parallel", "parallel")),
    )(z, conv_w, conv_b, wx, bx, wa, ba, lam)


def _branch_a_bwd(z, hl, kept, dya, dz, conv_w, wx, wa, lam, batch, seq):
    def body(z_ref, hl_ref, kept_ref, dya_ref, dz_in_ref, cw_ref, wx_ref, wa_ref, lam_ref,
             dz_ref, dcw_ref, dcb_ref, dwx_ref, dbx_ref, dwa_ref, dba_ref, dlam_ref, c_scr, d_scr):
        del dz_in_ref
        g_scr = d_scr
        xa = z_ref[0]
        ga = z_ref[1]
        hl = hl_ref[...]
        dya = dya_ref[...]
        xc, gi, gr = kept_ref[0], kept_ref[1], kept_ref[2]
        sp = _softplus(-lam_ref[...])
        a, mult, inv_mult = _lru_decay(gr, sp)
        sga = _sigmoid(ga)
        dz_ref[1] = (dya * hl * (sga * (1.0 + ga * (1.0 - sga)))).astype(_MXU_DTYPE)
        c_scr[...] = _shift_up(a, 1)
        d_scr[...] = dya * (ga * sga)
        _scan_backward(c_scr, d_scr, g_scr, seq)
        g = g_scr[...]
        da = g * _shift_down(hl, 1)
        dmult = g * gi * xc
        dgi = g * mult * xc
        dxc = g * mult * gi
        dlog_a = da * a - dmult * (a * a) * inv_mult
        dgr = dlog_a * (-LRU_C) * sp
        dsp = jnp.sum(dlog_a * gr, axis=0, keepdims=True) * (-LRU_C)
        dlam = -dsp * _sigmoid(-lam_ref[...])
        dpi = dgi * gi * (1.0 - gi)
        dpr = dgr * gr * (1.0 - gr)
        dxc = dxc + _block_diag(dpi, wx_ref, transposed=True) + _block_diag(dpr, wa_ref, transposed=True)
        dwx = jnp.stack([_mm_tn(xk, dk) for xk, dk in zip(_lane_blocks(xc), _lane_blocks(dpi))])
        dwa = jnp.stack([_mm_tn(xk, dk) for xk, dk in zip(_lane_blocks(xc), _lane_blocks(dpr))])
        dbx = jnp.sum(dpi, axis=0, keepdims=True)
        dba = jnp.sum(dpr, axis=0, keepdims=True)
        ahead = [dxc if k == CONV_WIDTH - 1 else _shift_up(dxc, CONV_WIDTH - 1 - k) for k in range(CONV_WIDTH)]
        dxa = sum(cw_ref[k:k + 1, :] * ahead[k] for k in range(CONV_WIDTH))
        dz_ref[0] = dxa.astype(_MXU_DTYPE)
        dcb = jnp.sum(dxc, axis=0, keepdims=True)
        dcw = [jnp.sum(ahead[k] * xa, axis=0, keepdims=True) for k in range(CONV_WIDTH)]

        @pl.when(pl.program_id(1) == 0)
        def _():
            for k in range(CONV_WIDTH):
                dcw_ref[k:k + 1, :] = dcw[k]
            dcb_ref[...] = dcb
            dwx_ref[...] = dwx
            dbx_ref[...] = dbx
            dwa_ref[...] = dwa
            dba_ref[...] = dba
            dlam_ref[...] = dlam

        @pl.when(pl.program_id(1) != 0)
        def _():
            for k in range(CONV_WIDTH):
                dcw_ref[k:k + 1, :] += dcw[k]
            dcb_ref[...] += dcb
            dwx_ref[...] += dwx
            dbx_ref[...] += dbx
            dwa_ref[...] += dwa
            dba_ref[...] += dba
            dlam_ref[...] += dlam

    tokens = batch * seq
    blk = pl.BlockSpec((seq, LRU_LANES), lambda c, b: (b, c))
    vec = pl.BlockSpec((1, LRU_LANES), lambda c, b: (0, c))
    mat = pl.BlockSpec((LRU_BLOCKS_PER_STEP, LANES, LANES), lambda c, b: (c, 0, 0))
    vec_shape = jax.ShapeDtypeStruct((1, D_MODEL), F32)
    mat_shape = jax.ShapeDtypeStruct((N_BLK, LANES, LANES), F32)
    return pl.pallas_call(
        body, name="branch_a_bwd",
        grid=(LRU_STEPS, batch),
        in_specs=[pl.BlockSpec((2, seq, LRU_LANES), lambda c, b: (2, b, c)), blk,
                  pl.BlockSpec((3, seq, LRU_LANES), lambda c, b: (0, b, c)), blk,
                  pl.BlockSpec(memory_space=pl.ANY),
                  pl.BlockSpec((CONV_WIDTH, LRU_LANES), lambda c, b: (0, c)), mat, mat, vec],
        out_specs=[pl.BlockSpec((2, seq, LRU_LANES), lambda c, b: (2, b, c)),
                   pl.BlockSpec((CONV_WIDTH, LRU_LANES), lambda c, b: (0, c)), vec, mat, vec, mat, vec, vec],
        out_shape=[jax.ShapeDtypeStruct((N_GROUPS, tokens, D_MODEL), _MXU_DTYPE),
                   jax.ShapeDtypeStruct((CONV_WIDTH, D_MODEL), F32), vec_shape, mat_shape, vec_shape, mat_shape,
                   vec_shape, vec_shape],
        scratch_shapes=[pltpu.VMEM((seq, LRU_LANES), F32)] * 2,
        input_output_aliases={4: 0},
        compiler_params=_params(("parallel", "arbitrary"), vmem=VMEM_LIMIT_BIG),
    )(z, hl, kept, dya, dz, conv_w, wx, wa, lam)


def _chunk_masks(transposed=False):
    r = lax.broadcasted_iota(jnp.int32, (CHUNK, CHUNK), 0)
    c = lax.broadcasted_iota(jnp.int32, (CHUNK, CHUNK), 1)
    return r <= c if transposed else r >= c


def _row_blocks(seq, fn):
    block = min(256, seq)

    def trip(i, carry):
        fn(pl.ds(pl.multiple_of(i * block, block), block))
        return carry

    lax.fori_loop(0, seq // block, trip, 0)


def _hgrn_prepare(z_ref, lb_ref, f_scr, logf_scr, qh_scr, seq):
    lb = _sigmoid(lb_ref[0:1, :] - lb_ref[1:2, :])

    def block(rows):
        q = z_ref[0, rows, :]
        f = lb + (1.0 - lb) * _sigmoid(z_ref[1, rows, :])
        f_scr[rows, :] = f
        logf_scr[rows, :] = jnp.log(f)
        qh_scr[rows, :] = q * _sigmoid(q)

    _row_blocks(seq, block)
    return lb


def _cumsum_rows(x, reverse=False):
    shift = _shift_up if reverse else _shift_down
    d = 1
    while d < x.shape[0]:
        x = x + shift(x, d)
        d *= 2
    return x


def _lane_mean(x):
    return jnp.mean(x, axis=-1, keepdims=True)


def _token_contractions(lhs_scr, rhs_scr, out_ref, seq):
    rows_id = lax.broadcasted_iota(jnp.int32, (LANES, LANES), 0)

    def transposed(p):
        rows = pl.ds(pl.multiple_of(p * LANES, LANES), LANES)
        return jnp.transpose(lhs_scr[rows, :]).astype(_MXU_DTYPE), rhs_scr[rows, :]

    def contract(p, s):
        lhs_t, rhs = s
        return (_mm(lhs_t, jnp.where(rows_id < CHUNK, rhs, 0.0)), _mm(lhs_t, jnp.where(rows_id >= CHUNK, rhs, 0.0)))

    def store(p, out):
        out_ref[2 * p] = out[0]
        out_ref[2 * p + 1] = out[1]

    _independent_trips(seq // LANES, [transposed, contract], store)


def _chunk_rows(c):
    return pl.ds(pl.multiple_of(c * CHUNK, CHUNK), CHUNK)


def _chunk_terms(c, z_ref, f_scr, qh_scr, b_scr):
    rows = _chunk_rows(c)
    b = b_scr[rows, :]
    b_mid = b_scr[pl.ds(c * CHUNK + CHUNK // 2, 1), :]
    b_last = b_scr[pl.ds(c * CHUNK + CHUNK - 1, 1), :]
    qh = qh_scr[rows, :]
    k = 1.0 - f_scr[rows, :]
    v = z_ref[2, rows, :]
    e_q = jnp.exp(b - b_mid) * HG_SCALE
    e_k = jnp.exp(b_mid - b)
    e_qi = jnp.exp(b) * HG_SCALE
    e_ks = jnp.exp(b_last - b)
    decay = jnp.exp(b_last)
    return rows, qh, k, v, e_q, e_k, e_qi, e_ks, decay


def _independent_trips(n, stages, store, group=CHUNKS_IN_FLIGHT):
    stages = stages if isinstance(stages, (list, tuple)) else [stages]
    group = min(group, n)

    def trip(g, carry):
        ids = [g * group + i for i in range(group)]
        state = [stages[0](c) for c in ids]
        for stage in stages[1:]:
            state = [stage(c, s) for c, s in zip(ids, state)]
        for c, s in zip(ids, state):
            store(c, s)
        return carry

    lax.fori_loop(0, n // group, trip, 0)


def _branch_b_fwd(z, lb_logits, hg_g, batch, seq):
    tokens = batch * seq
    n_chunks = seq // CHUNK

    def body(z_ref, lb_ref, g_ref, yb_ref, st_ref, kept_ref, logf_scr, o_scr, qi_scr, ks_scr, dec_scr):
        f_scr, qh_scr, b_scr, o_kept = (kept_ref.at[k] for k in range(4))
        _hgrn_prepare(z_ref, lb_ref, f_scr, logf_scr, qh_scr, seq)
        causal = _chunk_masks()
        gain = g_ref[...]

        def cumulate(c):
            return _cumsum_rows(logf_scr[_chunk_rows(c), :])

        def store_cumulated(c, b):
            b_scr[_chunk_rows(c), :] = b

        def scores(c):
            _, qh, k, v, e_q, e_k, e_qi, e_ks, decay = _chunk_terms(c, z_ref, f_scr, qh_scr, b_scr)
            return _mm_nt(qh * e_q, k * e_k), v, qh * e_qi, k * e_ks, decay

        def within_chunk(c, s):
            att, v, q_int, k_st, decay = s
            return _mm(jnp.where(causal, att, 0.0), v), q_int, k_st, decay

        def store_within_chunk(c, out):
            rows = _chunk_rows(c)
            o_scr[rows, :], qi_scr[rows, :], ks_scr[rows, :], dec_scr[pl.ds(c, 1), :] = out

        def carry_state(c, state_t):
            update = st_ref[c]
            st_ref[c] = state_t
            return state_t * dec_scr[pl.ds(c, 1), :] + update

        def finish(c):
            rows = _chunk_rows(c)
            o = o_scr[rows, :] + _mm_nt(qi_scr[rows, :], st_ref[c])
            r = lax.rsqrt(_lane_mean(o * o) + EPS)
            gb = z_ref[3, rows, :]
            return (((o * r) * gain) * (gb * _sigmoid(gb))).astype(_MXU_DTYPE), o

        def store_finished(c, out):
            yb_ref[_chunk_rows(c), :], o_kept[_chunk_rows(c), :] = out

        _independent_trips(n_chunks, cumulate, store_cumulated)
        _independent_trips(n_chunks, [scores, within_chunk], store_within_chunk)
        _token_contractions(z_ref.at[2], ks_scr, st_ref, seq)
        lax.fori_loop(0, n_chunks, carry_state, jnp.zeros((LANES, LANES), F32))
        _independent_trips(n_chunks, finish, store_finished)

    seq_buf = pltpu.VMEM((seq, LANES), F32)
    return pl.pallas_call(
        body, name="branch_b_fwd",
        grid=(batch, N_BLK),
        in_specs=[pl.BlockSpec((4, seq, LANES), lambda b, h: (0, b, h)),
                  pl.BlockSpec((2, LANES), lambda b, h: (0, h)),
                  pl.BlockSpec((1, LANES), lambda b, h: (0, 0))],
        out_specs=[pl.BlockSpec((seq, LANES), lambda b, h: (b, h)),
                   pl.BlockSpec((None, n_chunks, LANES, LANES), lambda b, h: (b * N_BLK + h, 0, 0, 0)),
                   pl.BlockSpec((4, seq, LANES), lambda b, h: (0, b, h))],
        out_shape=[jax.ShapeDtypeStruct((tokens, D_MODEL), _MXU_DTYPE),
                   jax.ShapeDtypeStruct((batch * N_BLK, n_chunks, LANES, LANES), F32),
                   jax.ShapeDtypeStruct((4, tokens, D_MODEL), F32)],
        scratch_shapes=[seq_buf] * 4 + [pltpu.VMEM((n_chunks, LANES), F32)],
        compiler_params=_params(("parallel", "parallel")),
    )(z, lb_logits, hg_g)


def _branch_b_bwd(z, states, kept, dyb, dz, lb_logits, hg_g, batch, seq):
    n_chunks = seq // CHUNK

    def body(z_ref, st_ref, kept_ref, dyb_ref, dz_in_ref, lb_ref, g_ref, dz_ref, dlog_ref, dg_ref,
             do_scr, qi_scr, dqh_scr, df_scr, dec_scr, dgp_scr, dlb_scr, dst_scr):
        del dz_in_ref
        f_scr, qh_scr, b_scr, o_kept = (kept_ref.at[k] for k in range(4))
        first = (pl.program_id(0) == 0) & (pl.program_id(1) == 0)
        lb = _sigmoid(lb_ref[0:1, :] - lb_ref[1:2, :])
        causal = _chunk_masks()
        anti_causal = _chunk_masks(transposed=True)
        gain = g_ref[...]

        @pl.when(first)
        def _():
            dg_ref[...] = jnp.zeros_like(dg_ref)

        @pl.when(pl.program_id(1) == 0)
        def _():
            dlb_scr[...] = jnp.zeros_like(dlb_scr)

        def output_gradient(c):
            rows = _chunk_rows(c)
            b = b_scr[rows, :]
            q_int = qh_scr[rows, :] * (jnp.exp(b) * HG_SCALE)
            decay = jnp.exp(b_scr[pl.ds(c * CHUNK + CHUNK - 1, 1), :])
            o = o_kept[rows, :]
            r = lax.rsqrt(_lane_mean(o * o) + EPS)
            o_n = o * r
            gb = z_ref[3, rows, :]
            sgb = _sigmoid(gb)
            dyb_c = dyb_ref[rows, :]
            d_ong = dyb_c * (gb * sgb)
            d_gb = (dyb_c * (o_n * gain) * (sgb * (1.0 + gb * (1.0 - sgb)))).astype(_MXU_DTYPE)
            d_gain = jnp.sum(d_ong * o_n, axis=0, keepdims=True)
            d_on = d_ong * gain
            return d_gb, d_gain, r * (d_on - o_n * _lane_mean(d_on * o_n)), q_int, decay

        def store_output_gradient(c, out):
            rows = _chunk_rows(c)
            dz_ref[3, rows, :], dgp_scr[pl.ds(c, 1), :], do_scr[rows, :], qi_scr[rows, :], dec_scr[pl.ds(c, 1), :] = out

        def carry_state_gradient(cc, d_state_t):
            c = n_chunks - 1 - cc
            update = dst_scr[c]
            dst_scr[c] = d_state_t
            return d_state_t * dec_scr[pl.ds(c, 1), :] + update

        def score_gradients(c):
            rows, qh, k, v, e_q, e_k, e_qi, e_ks, decay = _chunk_terms(c, z_ref, f_scr, qh_scr, b_scr)
            state_t = st_ref[c]
            d_state_t = dst_scr[c]
            d_o = do_scr[rows, :]
            q_in, k_in, q_int, k_st = qh * e_q, k * e_k, qh * e_qi, k * e_ks
            first = (_mm_nt(k_in, q_in), _mm_nt(d_o, v), _mm_nt(v, d_o), _mm_nt(k_st, d_state_t), _mm(d_o, state_t),
                     _mm(v, d_state_t))
            d_decay = jnp.sum(state_t * d_state_t, axis=0, keepdims=True)
            return first, d_o, q_in, k_in, q_int, k_st, e_q, e_k, e_qi, e_ks, decay, d_decay

        def input_gradients(c, s):
            (att_t, d_att, d_att_t, dv_inter, dq_int, dk_st), d_o, q_in, k_in, q_int, k_st, e_q, e_k, e_qi, e_ks, decay, d_decay = s
            rows = _chunk_rows(c)
            d_v = _mm(jnp.where(anti_causal, att_t, 0.0), d_o) + dv_inter
            dq_in = _mm(jnp.where(causal, d_att, 0.0), k_in)
            dk_in = _mm(jnp.where(anti_causal, d_att_t, 0.0), q_in)
            d_k = dk_in * e_k + dk_st * e_ks
            kk = dk_st * k_st
            d_b = dq_in * q_in + dq_int * q_int - dk_in * k_in - kk
            d_b_last = jnp.sum(kk, axis=0, keepdims=True) + decay * d_decay
            d_logf = _cumsum_rows(d_b, reverse=True) + d_b_last
            return d_v.astype(_MXU_DTYPE), dq_in * e_q + dq_int * e_qi, d_logf / f_scr[rows, :] - d_k

        def store_input_gradients(c, out):
            rows = _chunk_rows(c)
            dz_ref[2, rows, :], dqh_scr[rows, :], df_scr[rows, :] = out

        def input_activations(rows):
            q = z_ref[0, rows, :]
            sq = _sigmoid(q)
            dz_ref[0, rows, :] = (dqh_scr[rows, :] * (sq * (1.0 + q * (1.0 - sq)))).astype(_MXU_DTYPE)
            sg = _sigmoid(z_ref[1, rows, :])
            d_f = df_scr[rows, :]
            dz_ref[1, rows, :] = (d_f * (1.0 - lb) * sg * (1.0 - sg)).astype(_MXU_DTYPE)
            dlb_scr[...] += jnp.sum(d_f * (1.0 - sg), axis=0, keepdims=True)

        _independent_trips(n_chunks, output_gradient, store_output_gradient)
        _token_contractions(do_scr, qi_scr, dst_scr, seq)
        lax.fori_loop(0, n_chunks, carry_state_gradient, jnp.zeros((LANES, LANES), F32))
        _independent_trips(n_chunks, [score_gradients, input_gradients], store_input_gradients)
        dg_ref[...] += jnp.sum(dgp_scr[...], axis=0, keepdims=True)
        _row_blocks(seq, input_activations)
        d_l0 = dlb_scr[...] * lb * (1.0 - lb)
        dlog_ref[0:1, :] = d_l0
        dlog_ref[1:2, :] = -d_l0

    tokens = batch * seq
    seq_buf = pltpu.VMEM((seq, LANES), F32)
    chunk_rows = pltpu.VMEM((n_chunks, LANES), F32)
    return pl.pallas_call(
        body, name="branch_b_bwd",
        grid=(N_BLK, batch),
        in_specs=[pl.BlockSpec((4, seq, LANES), lambda h, b: (0, b, h)),
                  pl.BlockSpec((None, n_chunks, LANES, LANES), lambda h, b: (b * N_BLK + h, 0, 0, 0)),
                  pl.BlockSpec((4, seq, LANES), lambda h, b: (0, b, h)),
                  pl.BlockSpec((seq, LANES), lambda h, b: (b, h)),
                  pl.BlockSpec(memory_space=pl.ANY),
                  pl.BlockSpec((2, LANES), lambda h, b: (0, h)),
                  pl.BlockSpec((1, LANES), lambda h, b: (0, 0))],
        out_specs=[pl.BlockSpec((4, seq, LANES), lambda h, b: (0, b, h)),
                   pl.BlockSpec((2, LANES), lambda h, b: (0, h)),
                   pl.BlockSpec((1, LANES), lambda h, b: (0, 0))],
        out_shape=[jax.ShapeDtypeStruct((N_GROUPS, tokens, D_MODEL), _MXU_DTYPE),
                   jax.ShapeDtypeStruct((2, D_MODEL), F32),
                   jax.ShapeDtypeStruct((1, LANES), F32)],
        scratch_shapes=[seq_buf] * 4 + [chunk_rows, chunk_rows, pltpu.VMEM((1, LANES), F32),
                                        pltpu.VMEM((n_chunks, LANES, LANES), F32)],
        input_output_aliases={4: 0},
        compiler_params=_params(("arbitrary", "arbitrary")),
    )(z, states, kept, dyb, dz, lb_logits, hg_g)


def _merge_tail(ya, yb, z, x2d, tgt2d, b_merge, final_g, pa, pb, wo):
    tokens, d = x2d.shape
    tm = min(TAIL_TOKENS, tokens)
    n_tiles = tokens // tm

    def body(ya_ref, yb_ref, z_ref, x_ref, t_ref, bm_ref, fg_ref, pa_hbm, pb_hbm, wo_hbm,
             dya_ref, dyb_ref, dx2_ref, dz_ref, loss_ref, dfg_ref, dbm_ref, dpa_hbm, dpb_hbm, dwo_hbm,
             pa_s, pb_s, wo_s, dpa_s, dpb_s, dwo_s):
        i = pl.program_id(0)

        @pl.when(i == 0)
        def _():
            pltpu.sync_copy(pa_hbm, pa_s)
            pltpu.sync_copy(pb_hbm, pb_s)
            pltpu.sync_copy(wo_hbm, wo_s)
            dpa_s[...] = jnp.zeros_like(dpa_s)
            dpb_s[...] = jnp.zeros_like(dpb_s)
            dwo_s[...] = jnp.zeros_like(dwo_s)
            loss_ref[...] = jnp.zeros_like(loss_ref)
            dfg_ref[...] = jnp.zeros_like(dfg_ref)
            dbm_ref[...] = jnp.zeros_like(dbm_ref)

        ya_t = ya_ref[...]
        yb_t = yb_ref[...]
        out_a = _mm(ya_t, pa_s[...])
        out_b = _mm(yb_t, pb_s[...])
        g_a = _sigmoid(z_ref[0] + bm_ref[:, :d])
        g_b = _sigmoid(z_ref[1] + bm_ref[:, d:])
        mixed = g_a * out_a + g_b * out_b
        x2 = x_ref[...] + _mm(mixed, wo_s[...])
        r = lax.rsqrt(jnp.mean(x2 * x2, axis=-1, keepdims=True) + EPS)
        xn = x2 * r
        fg = fg_ref[...]
        diff = xn * fg - t_ref[...]
        loss_ref[...] += jnp.sum(diff * diff) * (0.5 / d)
        dy = diff * (1.0 / d)
        dfg_ref[...] += jnp.sum(dy * xn, axis=0, keepdims=True)
        dxn = dy * fg
        dx2 = r * (dxn - xn * jnp.mean(dxn * xn, axis=-1, keepdims=True))
        dx2_ref[...] = dx2
        dmixed = _mm_nt(dx2, wo_s[...])
        dwo_s[...] += _mm_tn(mixed, dx2)
        dgm_a = dmixed * out_a * g_a * (1.0 - g_a)
        dgm_b = dmixed * out_b * g_b * (1.0 - g_b)
        dz_ref[0] = dgm_a.astype(_MXU_DTYPE)
        dz_ref[1] = dgm_b.astype(_MXU_DTYPE)
        dbm_ref[:, :d] += jnp.sum(dgm_a, axis=0, keepdims=True)
        dbm_ref[:, d:] += jnp.sum(dgm_b, axis=0, keepdims=True)
        dout_a = dmixed * g_a
        dout_b = dmixed * g_b
        dpa_s[...] += _mm_tn(ya_t, dout_a)
        dpb_s[...] += _mm_tn(yb_t, dout_b)
        dya_ref[...] = _mm_nt(dout_a, pa_s[...])
        dyb_ref[...] = _mm_nt(dout_b, pb_s[...])

        @pl.when(i == n_tiles - 1)
        def _():
            pltpu.sync_copy(dpa_s, dpa_hbm)
            pltpu.sync_copy(dpb_s, dpb_hbm)
            pltpu.sync_copy(dwo_s, dwo_hbm)

    tile = pl.BlockSpec((tm, d), lambda i: (i, 0))
    gm = pl.BlockSpec((2, tm, d), lambda i: (3, i, 0))
    row = lambda n: pl.BlockSpec((1, n), lambda i: (0, 0))
    hbm = pl.BlockSpec(memory_space=pl.ANY)
    act = jax.ShapeDtypeStruct((tokens, d), F32)
    mat = jax.ShapeDtypeStruct((d, d), F32)
    return pl.pallas_call(
        body, name="merge_tail",
        grid=(n_tiles,),
        in_specs=[tile, tile, gm, tile, tile, row(2 * d), row(d), hbm, hbm, hbm],
        out_specs=[tile, tile, tile, gm, row(LANES), row(d), row(2 * d), hbm, hbm, hbm],
        out_shape=[act, act, act, jax.ShapeDtypeStruct((N_GROUPS, tokens, d), _MXU_DTYPE),
                   jax.ShapeDtypeStruct((1, LANES), F32), jax.ShapeDtypeStruct((1, d), F32),
                   jax.ShapeDtypeStruct((1, 2 * d), F32), mat, mat, mat],
        scratch_shapes=[pltpu.VMEM((d, d), _MXU_DTYPE)] * 3 + [pltpu.VMEM((d, d), F32)] * 3,
        compiler_params=_params(("arbitrary",)),
    )(ya, yb, z, x2d, tgt2d, b_merge, final_g, pa, pb, wo)


def _inproj_dw_exchange(h_t, dz, scatter):
    d, tokens = h_t.shape
    tm = min(CONTRACT_TOKENS, tokens)
    n_i = tokens // tm
    half = d // 2

    def body(h_ref, dz_ref, *rest):
        n_in, n_out = scatter.n_in, scatter.n_out
        dw_hbm, land_hbm = rest[n_in:n_in + 2]
        acc, local_sems, send_sems, recv_sems = rest[n_in + 2 + n_out:n_in + 6 + n_out]
        carried = scatter.copies(rest[:n_in], rest[n_in + 2:n_in + 2 + n_out], rest[n_in + 6 + n_out:])
        s, i = pl.program_id(0), pl.program_id(1)
        x, y, c, _ = _mesh_position()

        @pl.when((s == 0) & (i == 0))
        def _():
            for cp in carried:
                cp.start()

        part = _mm(h_ref[...], dz_ref[...])
        buf = acc.at[s % 2]

        @pl.when(i == 0)
        def _():
            buf[...] = part

        @pl.when(i != 0)
        def _():
            buf[...] += part

        def copies(k):
            g = _SLOT_TO_GROUP[k]
            cols = pl.ds((g % 2) * D_MODEL, D_MODEL)
            src = acc.at[k % 2]
            mine = pltpu.make_async_copy(src, dw_hbm.at[g // 2, :, cols], local_sems.at[k % 2])
            theirs = _remote(src.at[pl.ds((1 - c) * half, half), :], land_hbm.at[g // 2, :, cols],
                             send_sems, recv_sems, k, (x, y, 1 - c))
            return mine, theirs

        for k in range(N_GROUPS):
            @pl.when((s == k) & (i == n_i - 1))
            def _(k=k):
                if k > 0:
                    mine, theirs = copies(k - 1)
                    mine.wait()
                    theirs.wait_send()
                mine, theirs = copies(k)
                mine.start()
                theirs.start()
                if k == N_GROUPS - 1:
                    mine.wait()
                    theirs.wait_send()
                    for kk in range(N_GROUPS):
                        copies(kk)[1].wait_recv()
                    for cp in carried:
                        cp.wait()

    hbm = pl.BlockSpec(memory_space=pl.ANY)
    more = scatter.plumbing(first_operand=2, first_output=2)
    return pl.pallas_call(
        body, name="inproj_dw_exchange",
        grid=(N_GROUPS, n_i),
        in_specs=[pl.BlockSpec((d, tm), lambda s, i: (0, i)),
                  pl.BlockSpec((None, tm, D_MODEL), lambda s, i: (s, i, 0))] + more[1],
        out_specs=[hbm, hbm] + more[2],
        out_shape=[jax.ShapeDtypeStruct((N_SHARDS, d, 2 * D_MODEL), F32),
                   jax.ShapeDtypeStruct((N_SHARDS, half, 2 * D_MODEL), F32)] + more[3],
        scratch_shapes=[pltpu.VMEM((2, d, D_MODEL), F32), pltpu.SemaphoreType.DMA((2,)),
                        pltpu.SemaphoreType.DMA((N_GROUPS,)), pltpu.SemaphoreType.DMA((N_GROUPS,))] + more[4],
        input_output_aliases=more[5],
        compiler_params=_params(("arbitrary", "arbitrary")),
    )(h_t, dz, *more[0])


def _inproj_dx(dz, w_all, x2d, dx2, norm_g, scatter):
    tokens, d = x2d.shape
    tm = min(TAIL_TOKENS, tokens)
    n_tiles = tokens // tm

    def body(dz_ref, w_hbm, x_ref, dx2_ref, g_ref, *rest):
        n_in, n_out = scatter.n_in, scatter.n_out
        dx_ref, dg_ref = rest[n_in:n_in + 2]
        w_res = rest[n_in + 2 + n_out]
        copies = scatter.copies(rest[:n_in], rest[n_in + 2:n_in + 2 + n_out], rest[n_in + 3 + n_out:])
        i = pl.program_id(0)

        @pl.when(i == 0)
        def _():
            for cp in copies:
                cp.start()
            for slot, g in enumerate(_SLOT_TO_GROUP):
                pltpu.sync_copy(w_hbm.at[g // 2, :, pl.ds((g % 2) * D_MODEL, D_MODEL)],
                                w_res.at[:, pl.ds(slot * D_MODEL, D_MODEL)])
            dg_ref[...] = jnp.zeros_like(dg_ref)

        dz_all = jnp.concatenate([dz_ref[s] for s in range(N_GROUPS)], axis=1)
        dh = jnp.transpose(_mm_nt(w_res[...], dz_all))
        x = x_ref[...]
        r = lax.rsqrt(jnp.mean(x * x, axis=-1, keepdims=True) + EPS)
        xn = x * r
        dg_ref[...] += jnp.sum(dh * xn, axis=0, keepdims=True)
        dxn = dh * g_ref[...]
        dx_ref[...] = r * (dxn - xn * jnp.mean(dxn * xn, axis=-1, keepdims=True)) + dx2_ref[...]

        @pl.when(i == n_tiles - 1)
        def _():
            for cp in copies:
                cp.wait()

    tile = pl.BlockSpec((tm, d), lambda i: (i, 0))
    hbm = pl.BlockSpec(memory_space=pl.ANY)
    more = scatter.plumbing(first_operand=5, first_output=2)
    return pl.pallas_call(
        body, name="inproj_dx", grid=(n_tiles,),
        in_specs=[pl.BlockSpec((N_GROUPS, tm, D_MODEL), lambda i: (0, i, 0)), hbm, tile, tile,
                  pl.BlockSpec((1, d), lambda i: (0, 0))] + more[1],
        out_specs=[tile, pl.BlockSpec((1, d), lambda i: (0, 0))] + more[2],
        out_shape=[jax.ShapeDtypeStruct((tokens, d), F32), jax.ShapeDtypeStruct((1, d), F32)] + more[3],
        scratch_shapes=[pltpu.VMEM((d, N_GROUPS * D_MODEL), _MXU_DTYPE)] + more[4],
        input_output_aliases=more[5],
        compiler_params=_params(("arbitrary",)),
    )(dz, w_all, x2d, dx2, norm_g, *more[0])


def _row_tile(rows, cols, itemsize=4, budget=2 * 1024 * 1024):
    tr = rows
    while tr * cols * itemsize > budget and tr % 16 == 0:
        tr //= 2
    return tr


def _cast_into_slot(a, chip, dtype, name):
    rows, cols = a.shape
    tr = _row_tile(rows, cols)

    def body(chip_ref, a_ref, o_ref):
        del chip_ref
        o_ref[...] = a_ref[...].astype(dtype)

    grid_spec = pltpu.PrefetchScalarGridSpec(
        num_scalar_prefetch=1, grid=(rows // tr,),
        in_specs=[pl.BlockSpec((tr, cols), lambda i, chip_ref: (i, 0))],
        out_specs=pl.BlockSpec((None, tr, cols), lambda i, chip_ref: (chip_ref[0], i, 0)))
    return pl.pallas_call(body, name=name, grid_spec=grid_spec,
                          out_shape=jax.ShapeDtypeStruct((N_SHARDS, rows, cols), dtype),
                          compiler_params=_params(("arbitrary",)))(chip, a)


def _sum_slots(stack, name):
    n, rows, cols = stack.shape
    tr = _row_tile(rows, cols * n)

    def body(s_ref, o_ref):
        total = s_ref[0].astype(F32)
        for k in range(1, n):
            total = total + s_ref[k].astype(F32)
        o_ref[...] = total

    return pl.pallas_call(body, name=name, grid=(rows // tr,),
                          in_specs=[pl.BlockSpec((n, tr, cols), lambda i: (0, i, 0))],
                          out_specs=pl.BlockSpec((tr, cols), lambda i: (i, 0)),
                          out_shape=jax.ShapeDtypeStruct((rows, cols), F32),
                          compiler_params=_params(("parallel",)))(stack)


def _add_half(full, landed, place, name):
    n, rows, cols = full.shape
    half = rows // 2
    tr = _row_tile(half, cols)
    nb = half // tr

    def body(place_ref, a_ref, b_ref, o_ref, own_ref):
        total = (a_ref[...] + b_ref[...]).astype(_MXU_DTYPE)
        o_ref[...] = total

        @pl.when(pl.program_id(1) == place_ref[1])
        def _():
            own_ref[...] = total

    grid_spec = pltpu.PrefetchScalarGridSpec(
        num_scalar_prefetch=1, grid=(nb, n),
        in_specs=[pl.BlockSpec((None, tr, cols), lambda i, j, place_ref: (j, place_ref[0] * nb + i, 0)),
                  pl.BlockSpec((None, tr, cols), lambda i, j, place_ref: (j, i, 0))],
        out_specs=[pl.BlockSpec((None, tr, cols), lambda i, j, place_ref: (j, i, 0)),
                   pl.BlockSpec((None, tr, cols), lambda i, j, place_ref: (place_ref[1], i, 0))])
    shape = jax.ShapeDtypeStruct((n, half, cols), _MXU_DTYPE)
    return pl.pallas_call(body, name=name, grid_spec=grid_spec, out_shape=[shape, shape],
                          compiler_params=_params(("parallel", "arbitrary")))(place, full, landed)


def _adamw_update(w, grad, m, v):
    c1 = 1.0 - ADAM_B1 ** ADAM_STEP
    c2 = 1.0 - ADAM_B2 ** ADAM_STEP
    nm = ADAM_B1 * m + (1.0 - ADAM_B1) * grad
    nv = ADAM_B2 * v + (1.0 - ADAM_B2) * (grad * grad)
    return (-ADAM_LR) * ((nm / c1) / (jnp.sqrt(nv / c2) + ADAM_EPS) + ADAM_WD * w), nm, nv


def _adamw(w, g, m, v, name):
    rows, cols = w.shape
    tr = _row_tile(rows, cols, budget=1024 * 1024)

    def body(w_ref, g_ref, m_ref, v_ref, d_ref, nm_ref, nv_ref):
        d_ref[...], nm_ref[...], nv_ref[...] = _adamw_update(w_ref[...], g_ref[...], m_ref[...], v_ref[...])

    spec = pl.BlockSpec((tr, cols), lambda i: (i, 0))
    shape = jax.ShapeDtypeStruct((rows, cols), F32)
    return pl.pallas_call(body, name=name, grid=(rows // tr,), in_specs=[spec] * 4, out_specs=[spec] * 3,
                          out_shape=[shape] * 3, compiler_params=_params(("parallel",)))(w, g, m, v)


def _adamw_halves(w, g_mine, g_sibling, m, v, core, name):
    rows, cols = w.shape
    half = rows // 2
    tr = _row_tile(half, cols, budget=1024 * 1024)
    nb = half // tr

    def body(core_ref, w_ref, gm_ref, gs_ref, m_ref, v_ref, g_ref, d_ref, nm_ref, nv_ref):
        mine = pl.program_id(0) // nb == core_ref[0]
        grad = jnp.where(mine, gm_ref[...], gs_ref[...])
        g_ref[...] = grad
        d_ref[...], nm_ref[...], nv_ref[...] = _adamw_update(w_ref[...], grad, m_ref[...], v_ref[...])

    spec = pl.BlockSpec((tr, cols), lambda i, core_ref: (i, 0))
    mine_spec = pl.BlockSpec((tr, cols), lambda i, core_ref: (jnp.where(i // nb == core_ref[0], i % nb, 0), 0))
    sibling_spec = pl.BlockSpec((tr, cols), lambda i, core_ref: (jnp.where(i // nb == core_ref[0], 0, i % nb), 0))
    grid_spec = pltpu.PrefetchScalarGridSpec(num_scalar_prefetch=1, grid=(rows // tr,),
                                             in_specs=[spec, mine_spec, sibling_spec, spec, spec], out_specs=[spec] * 4)
    shape = jax.ShapeDtypeStruct((rows, cols), F32)
    return pl.pallas_call(body, name=name, grid_spec=grid_spec, out_shape=[shape] * 4,
                          compiler_params=_params(("parallel",)))(core, w, g_mine, g_sibling, m, v)


def _local_step(x, loss_target, gather, reduction, b_merge, conv_b, rg_wx, rg_bx, rg_wa, rg_ba, rg_lambda,
                hg_lb_logits, hg_norm_g, norm_g, final_norm_g):
    batch, seq, d = x.shape
    x2d = x.reshape(batch * seq, d)
    tgt2d = loss_target.reshape(batch * seq, d)
    z, h_t, (w_all, pa, pb, wo), cw_all = _inproj_fwd_gather(x2d, norm_g, *gather)
    pa, pb, wo = (t.reshape(d, d) for t in (pa, pb, wo))
    conv_w = jnp.transpose(cw_all, (1, 0, 2)).reshape(CONV_WIDTH, d)
    lru = (conv_w, conv_b, rg_wx, rg_bx, rg_wa, rg_ba, rg_lambda)
    ya, hl, kept = _branch_a_fwd(z, *lru, batch, seq)
    yb, states, kept_b = _branch_b_fwd(z, hg_lb_logits, hg_norm_g, batch, seq)
    dya, dyb, dx2, dz, loss, d_final_g, d_b_merge, d_pa, d_pb, d_wo = _merge_tail(
        ya, yb, z, x2d, tgt2d, b_merge, final_norm_g, pa, pb, wo)
    dz, d_lb_logits, d_hg_g = _branch_b_bwd(z, states, kept_b, dyb, dz, hg_lb_logits, hg_norm_g, batch, seq)
    dz, d_conv_w, d_conv_b, d_wx, d_bx, d_wa, d_ba, d_lam = _branch_a_bwd(
        z, hl, kept, dya, dz, conv_w, rg_wx, rg_wa, rg_lambda, batch, seq)
    small = dict(b_merge=d_b_merge, conv_w=d_conv_w, conv_b=d_conv_b, rg_wx=d_wx, rg_bx=d_bx, rg_wa=d_wa,
                 rg_ba=d_ba, rg_lambda=d_lam, hg_lb_logits=d_lb_logits, hg_norm_g=d_hg_g,
                 norm_g=jnp.zeros((1, d), F32), final_norm_g=d_final_g)
    first, second = reduction
    d_w_in, landed_w_in, *scattered_first = _inproj_dw_exchange(h_t, dz, first((d_pa, d_pb, d_wo), small))
    grad_x, d_norm_g, *scattered_second = _inproj_dx(dz, w_all, x2d, dx2, norm_g, scatter=second(d_w_in, landed_w_in))
    return loss[0, 0], grad_x.reshape(batch, seq, d), d_norm_g, (scattered_first, scattered_second)


_SMALL_ORDER = ("b_merge", "conv_w", "conv_b", "rg_wx", "rg_bx", "rg_wa", "rg_ba", "rg_lambda", "hg_lb_logits",
                "hg_norm_g", "norm_g", "final_norm_g")
N_DEV = 8
PIECE_ROWS = 272


def _pack_small(tree):
    flat = jnp.concatenate([tree[k].reshape(-1) for k in _SMALL_ORDER])
    flat = jnp.pad(flat, (0, N_DEV * PIECE_ROWS * LANES - flat.shape[0]))
    return flat.reshape(N_DEV * PIECE_ROWS, LANES)


def _unpack_small(packed, like):
    flat = packed.reshape(-1)
    out, pos = {}, 0
    for k in _SMALL_ORDER:
        n = like[k].size
        out[k] = flat[pos:pos + n].reshape(like[k].shape)
        pos += n
    return out


def _mesh_position():
    x, y, c = lax.axis_index("x"), lax.axis_index("y"), lax.axis_index("c")
    other_chips = [(1 - x, y), (x, 1 - y), (1 - x, 1 - y)]
    return x, y, c, other_chips


def _other_devices(x, y, c):
    flips = [(fx, fy, fc) for fx in (0, 1) for fy in (0, 1) for fc in (0, 1) if (fx, fy, fc) != (0, 0, 0)]
    return [(jnp.where(fx, 1 - x, x), jnp.where(fy, 1 - y, y), jnp.where(fc, 1 - c, c)) for fx, fy, fc in flips]


def _remote(src, dst, send_sems, recv_sems, k, device):
    return pltpu.make_async_remote_copy(src_ref=src, dst_ref=dst, send_sem=send_sems.at[k], recv_sem=recv_sems.at[k],
                                        device_id=device, device_id_type=MESH)


def _exchange_halves(bigs, small):
    n_big = len(bigs)
    n_sem = n_big + N_DEV - 1

    def body(*refs):
        srcs, small_src = refs[:n_big], refs[n_big]
        outs, small_out = refs[n_big + 1:2 * n_big + 1], refs[2 * n_big + 1]
        send_sems, recv_sems, local_sem = refs[2 * n_big + 2:]
        x, y, c, _ = _mesh_position()
        me, sibling = 4 * x + 2 * y + c, (x, y, 1 - c)
        mine = pltpu.make_async_copy(small_src.at[pl.ds(me * PIECE_ROWS, PIECE_ROWS), :], small_out.at[me], local_sem)
        mine.start()
        copies = []
        for a in range(n_big):
            hs = srcs[a].shape[1] // 2
            copies.append(_remote(srcs[a].at[:, pl.ds((1 - c) * hs, hs), :], outs[a], send_sems, recv_sems, a, sibling))
        for k, (px, py, pc) in enumerate(_other_devices(x, y, c)):
            piece = small_src.at[pl.ds((4 * px + 2 * py + pc) * PIECE_ROWS, PIECE_ROWS), :]
            copies.append(_remote(piece, small_out.at[me], send_sems, recv_sems, n_big + k, (px, py, pc)))
        for cp in copies:
            cp.start()
        for cp in copies:
            cp.wait()
        mine.wait()

    hbm = pl.BlockSpec(memory_space=pl.ANY)
    out_shape = [jax.ShapeDtypeStruct((g.shape[0], g.shape[1] // 2, g.shape[2]), F32) for g in bigs]
    out_shape.append(jax.ShapeDtypeStruct((N_DEV, PIECE_ROWS, LANES), F32))
    return pl.pallas_call(
        body, name="exchange_halves",
        in_specs=[hbm] * (n_big + 1), out_specs=[hbm] * (n_big + 1), out_shape=out_shape,
        scratch_shapes=[pltpu.SemaphoreType.DMA((n_sem,)), pltpu.SemaphoreType.DMA((n_sem,)), pltpu.SemaphoreType.DMA],
    )(*bigs, small)


class _Scatter:
    def __init__(self, bigs, by_chip, small=None):
        self.bigs, self.by_chip, self.small = list(bigs), list(by_chip), small
        self.n_big = len(self.bigs)
        self.n_in = 2 * self.n_big + (small is not None)
        self.n_out = self.n_big + (small is not None)
        self.n_scratch = 2 + (small is not None)

    def plumbing(self, first_operand, first_output):
        hbm = pl.BlockSpec(memory_space=pl.ANY)
        n_sem = 3 * self.n_big + (N_DEV - 1 if self.small is not None else 0)
        operands = self.bigs + self.by_chip + ([self.small] if self.small is not None else [])
        out_shapes = [jax.ShapeDtypeStruct(g.shape, g.dtype) for g in self.by_chip]
        scratch = [pltpu.SemaphoreType.DMA((n_sem,)), pltpu.SemaphoreType.DMA((n_sem,))]
        if self.small is not None:
            out_shapes.append(jax.ShapeDtypeStruct((N_DEV, PIECE_ROWS, LANES), F32))
            scratch.append(pltpu.SemaphoreType.DMA)
        aliases = {first_operand + self.n_big + a: first_output + a for a in range(self.n_big)}
        return operands, [hbm] * self.n_in, [hbm] * self.n_out, out_shapes, scratch, aliases

    def copies(self, in_refs, out_refs, scratch_refs):
        srcs, outs = in_refs[:self.n_big], out_refs[:self.n_big]
        send_sems, recv_sems = scratch_refs[:2]
        x, y, c, chips = _mesh_position()
        chip, me = 2 * x + y, 4 * x + 2 * y + c
        copies = []
        for a in range(self.n_big):
            for j, (cx, cy) in enumerate(chips):
                copies.append(_remote(srcs[a].at[2 * cx + cy], outs[a].at[chip], send_sems, recv_sems, 3 * a + j,
                                      (cx, cy, c)))
        if self.small is not None:
            small_src, small_out = in_refs[2 * self.n_big], out_refs[self.n_big]
            copies.append(pltpu.make_async_copy(small_src, small_out.at[me], scratch_refs[2]))
            for k, peer in enumerate(_other_devices(x, y, c)):
                copies.append(_remote(small_src, small_out.at[me], send_sems, recv_sems, 3 * self.n_big + k, peer))
        return copies


def _swap_halves(halves, vec):
    n_big = len(halves)

    def body(*refs):
        srcs, vec_src = refs[:n_big], refs[n_big]
        outs, vec_out = refs[n_big + 1:2 * n_big + 1], refs[2 * n_big + 1]
        send_sems, recv_sems, local_sem = refs[2 * n_big + 2:]
        x, y, c, _ = _mesh_position()
        me = 4 * x + 2 * y + c
        copies = [pltpu.make_async_copy(vec_src, vec_out.at[me], local_sem)]
        copies += [_remote(srcs[a], outs[a], send_sems, recv_sems, a, (x, y, 1 - c)) for a in range(n_big)]
        copies += [_remote(vec_src, vec_out.at[me], send_sems, recv_sems, n_big + k, peer)
                   for k, peer in enumerate(_other_devices(x, y, c))]
        for cp in copies:
            cp.start()
        for cp in copies:
            cp.wait()

    hbm = pl.BlockSpec(memory_space=pl.ANY)
    n_sem = n_big + N_DEV - 1
    return pl.pallas_call(
        body, name="swap_halves",
        in_specs=[hbm] * (n_big + 1), out_specs=[hbm] * (n_big + 1),
        out_shape=[jax.ShapeDtypeStruct(h.shape, F32) for h in halves] + [jax.ShapeDtypeStruct((N_DEV,) + vec.shape, F32)],
        scratch_shapes=[pltpu.SemaphoreType.DMA((n_sem,)), pltpu.SemaphoreType.DMA((n_sem,)), pltpu.SemaphoreType.DMA],
    )(*halves, vec)


def kernel(x, w_in, b_merge, conv_w, conv_b, rg_wx, rg_bx, rg_wa, rg_ba, rg_lambda, hg_lb_logits, hg_norm_g, proj_a, proj_b, w_out, norm_g, final_norm_g, loss_target, m_w_in, m_b_merge, m_conv_w, m_conv_b, m_rg_wx, m_rg_bx, m_rg_wa, m_rg_ba, m_rg_lambda, m_hg_lb_logits, m_hg_norm_g, m_proj_a, m_proj_b, m_w_out, m_norm_g, m_final_norm_g, v_w_in, v_b_merge, v_conv_w, v_conv_b, v_rg_wx, v_rg_bx, v_rg_wa, v_rg_ba, v_rg_lambda, v_hg_lb_logits, v_hg_norm_g, v_proj_a, v_proj_b, v_w_out, v_norm_g, v_final_norm_g):
    d = D_MODEL
    weights = dict(w_in=w_in, b_merge=b_merge, conv_w=conv_w, conv_b=conv_b, rg_wx=rg_wx, rg_bx=rg_bx, rg_wa=rg_wa,
                   rg_ba=rg_ba, rg_lambda=rg_lambda, hg_lb_logits=hg_lb_logits, hg_norm_g=hg_norm_g, proj_a=proj_a,
                   proj_b=proj_b, w_out=w_out, norm_g=norm_g, final_norm_g=final_norm_g)
    m = dict(w_in=m_w_in, b_merge=m_b_merge, conv_w=m_conv_w, conv_b=m_conv_b, rg_wx=m_rg_wx, rg_bx=m_rg_bx,
             rg_wa=m_rg_wa, rg_ba=m_rg_ba, rg_lambda=m_rg_lambda, hg_lb_logits=m_hg_lb_logits, hg_norm_g=m_hg_norm_g,
             proj_a=m_proj_a, proj_b=m_proj_b, w_out=m_w_out, norm_g=m_norm_g, final_norm_g=m_final_norm_g)
    v = dict(w_in=v_w_in, b_merge=v_b_merge, conv_w=v_conv_w, conv_b=v_conv_b, rg_wx=v_rg_wx, rg_bx=v_rg_bx,
             rg_wa=v_rg_wa, rg_ba=v_rg_ba, rg_lambda=v_rg_lambda, hg_lb_logits=v_hg_lb_logits, hg_norm_g=v_hg_norm_g,
             proj_a=v_proj_a, proj_b=v_proj_b, w_out=v_w_out, norm_g=v_norm_g, final_norm_g=v_final_norm_g)
    big_names = ("w_in", "proj_a", "proj_b", "w_out")

    core = lax.axis_index("c").astype(jnp.int32).reshape(1)
    chip = (2 * lax.axis_index("x") + lax.axis_index("y")).astype(jnp.int32)

    slotted = [_cast_into_slot(weights[k][0], chip.reshape(1), _MXU_DTYPE, f"cast_{k}") for k in big_names]
    conv_slotted = _cast_into_slot(conv_w[0], chip.reshape(1), F32, "slot_conv_w")

    small_shapes = {}

    place = jnp.concatenate([core, chip.reshape(1)])

    def reduce_proj_and_small(proj_grads, small_grads):
        small_shapes.update({k: t.shape for k, t in small_grads.items()})
        bigs = [g.reshape(N_SHARDS, d // N_SHARDS, d) for g in proj_grads]
        *landed, small_landed = _exchange_halves(bigs, _pack_small(small_grads))
        sums = [_add_half(g, l, place, f"add_half_{1 + a}") for a, (g, l) in enumerate(zip(bigs, landed))]
        return _Scatter([s[0] for s in sums], [s[1] for s in sums], _sum_slots(small_landed, "sum_small"))

    def reduce_w_in(d_w_in, landed):
        partial, own_slot = _add_half(d_w_in, landed, place, "add_half_0")
        return _Scatter([partial], [own_slot])

    loss_part, grad_x, d_norm_g, ((*by_chip_proj, small_all), by_chip_w_in) = _local_step(
        x, loss_target, (slotted, conv_slotted, chip.reshape(1)), (reduce_proj_and_small, reduce_w_in),
        b_merge, conv_b, rg_wx[0], rg_bx.reshape(1, d), rg_wa[0], rg_ba.reshape(1, d), rg_lambda, hg_lb_logits,
        hg_norm_g, norm_g, final_norm_g.reshape(1, d))
    mine = [_sum_slots(s, f"sum_chips_{a}") for a, s in enumerate(by_chip_w_in + by_chip_proj)]
    late = jnp.concatenate([d_norm_g.reshape(SUBLANES, LANES), jnp.full((SUBLANES, LANES), loss_part, F32)])
    *theirs, late_parts = _swap_halves(mine, late)
    late_sum = _sum_slots(late_parts, "sum_late")
    loss = late_sum[SUBLANES, 0]
    small_red = _unpack_small(small_all, {k: jax.ShapeDtypeStruct(s, F32) for k, s in small_shapes.items()})
    small_red["norm_g"] = late_sum[:SUBLANES].reshape(1, d)

    grads, delta, new_m, new_v = {}, {}, {}, {}
    for k, g_mine, g_theirs in zip(big_names, mine, theirs):
        out = _adamw_halves(weights[k][0], g_mine, g_theirs, m[k][0], v[k][0], core, f"adamw_{k}")
        grads[k], delta[k], new_m[k], new_v[k] = (t.reshape(weights[k].shape) for t in out)
    cols = d // N_SHARDS
    g_conv = lax.dynamic_slice(small_red["conv_w"], (0, chip * cols), (CONV_WIDTH, cols))
    grads["conv_w"] = g_conv.reshape(conv_w.shape)
    dl, nm, nv = _adamw(conv_w[0], g_conv, m_conv_w[0], v_conv_w[0], "adamw_conv_w")
    delta["conv_w"], new_m["conv_w"], new_v["conv_w"] = (t.reshape(conv_w.shape) for t in (dl, nm, nv))
    rest = [k for k in _SMALL_ORDER if k != "conv_w"]
    like = {k: (weights[k] if k != "conv_w" else jnp.zeros((CONV_WIDTH, d), F32)) for k in _SMALL_ORDER}
    packs = [_pack_small({k: (t[k] if k != "conv_w" else like[k]) for k in _SMALL_ORDER}) for t in (weights, m, v)]
    g_pack = _pack_small({k: small_red[k].reshape(like[k].shape) for k in _SMALL_ORDER})
    outs = [_unpack_small(p, like) for p in _adamw(packs[0], g_pack, packs[1], packs[2], "adamw_small")]
    for k in rest:
        grads[k] = small_red[k].reshape(weights[k].shape)
        delta[k], new_m[k], new_v[k] = outs[0][k], outs[1][k], outs[2][k]

    order = ("w_in", "b_merge", "conv_w", "conv_b", "rg_wx", "rg_bx", "rg_wa", "rg_ba", "rg_lambda", "hg_lb_logits",
             "hg_norm_g", "proj_a", "proj_b", "w_out", "norm_g", "final_norm_g")
    return (loss, grad_x, *[grads[k] for k in order], *[delta[k] for k in order], *[new_m[k] for k in order],
            *[new_v[k] for k in order])
```

```python
import functools

import jax
import jax.numpy as jnp
from jax import lax
from jax.experimental import pallas as pl
from jax.experimental.pallas import tpu as pltpu

F32 = jnp.float32
_MXU_DTYPE = jnp.bfloat16

D_MODEL = 1024
LANES = 128
SUBLANES = 8
N_BLK = D_MODEL // LANES
N_GROUPS = 8
N_SHARDS = 4
CONV_WIDTH = 4
LRU_C = 8.0
CHUNK = 64
CHUNKS_IN_FLIGHT = 16
HG_SCALE = float(LANES) ** -0.5
EPS = 1e-6
ADAM_LR, ADAM_B1, ADAM_B2, ADAM_EPS, ADAM_WD, ADAM_STEP = 0.001, 0.9, 0.999, 1e-08, 0.01, 10
MATMUL_TOKENS = 512
TAIL_TOKENS = 256
CONTRACT_TOKENS = 2048
Z_STREAMS = 4
VMEM_LIMIT = 56 * 1024 * 1024
VMEM_LIMIT_BIG = 60 * 1024 * 1024
MESH = pl.DeviceIdType.MESH

_SLOT_TO_GROUP = (2, 3, 4, 5, 0, 1, 6, 7)


def _mm(a, b):
    return lax.dot_general(a.astype(_MXU_DTYPE), b.astype(_MXU_DTYPE), (((1,), (0,)), ((), ())),
                           preferred_element_type=F32)


def _mm_nt(a, b):
    return lax.dot_general(a.astype(_MXU_DTYPE), b.astype(_MXU_DTYPE), (((1,), (1,)), ((), ())),
                           preferred_element_type=F32)


def _mm_tn(a, b):
    return lax.dot_general(a.astype(_MXU_DTYPE), b.astype(_MXU_DTYPE), (((0,), (0,)), ((), ())),
                           preferred_element_type=F32)


def _sigmoid(x):
    return 0.5 * jnp.tanh(0.5 * x) + 0.5


def _log1p_pos(y):
    series = y * (1.0 - y * (0.5 - y * (1.0 / 3.0 - y * 0.25)))
    return jnp.where(y < 0.01, series, jnp.log(1.0 + y))


def _softplus(x):
    return jnp.maximum(x, 0.0) + _log1p_pos(jnp.exp(-jnp.abs(x)))


def _shift_down(x, n):
    rolled = pltpu.roll(x, n, 0)
    edge = SUBLANES if (n < SUBLANES and x.shape[0] > SUBLANES) else x.shape[0]
    rows = lax.broadcasted_iota(jnp.int32, (edge, x.shape[1]), 0)
    head = jnp.where(rows >= n, rolled[:edge], 0.0)
    return head if edge == x.shape[0] else jnp.concatenate([head, rolled[edge:]], axis=0)


def _shift_up(x, n):
    size = x.shape[0]
    rolled = pltpu.roll(x, size - n, 0)
    edge = SUBLANES if (n < SUBLANES and size > SUBLANES) else size
    rows = lax.broadcasted_iota(jnp.int32, (edge, x.shape[1]), 0)
    tail = jnp.where(rows < edge - n, rolled[size - edge:], 0.0)
    return tail if edge == size else jnp.concatenate([rolled[:size - edge], tail], axis=0)


def _params(dims, vmem=VMEM_LIMIT):
    return pltpu.CompilerParams(dimension_semantics=dims, vmem_limit_bytes=vmem)


def _slot_of_group(g):
    return jnp.where(g < 2, g + 4, jnp.where(g < 6, g - 2, g))


def _inproj_fwd_gather(x2d, norm_g, slotted, conv_slotted, chip):
    tokens, d = x2d.shape
    tm = min(MATMUL_TOKENS, tokens)
    n_tiles = tokens // tm
    n_big = len(slotted)
    n_sem = 6 * (n_big + 1) + 3
    last_pass = N_GROUPS - 1

    def shard_of(k, chip_id):
        x, y = chip_id // 2, chip_id % 2
        return 2 * jnp.where(k % 2 == 1, 1 - x, x) + jnp.where(k // 2 == 1, 1 - y, y)

    def body(chip_ref, x_ref, g_ref, *rest):
        bufs, cw = rest[n_big + 1:2 * n_big + 1], rest[2 * n_big + 1]
        z_hbm, ht_ref = rest[2 * n_big + 2:2 * n_big + 4]
        h_all, slab, z_buf, send_sems, recv_sems, slab_sems, z_sems = rest[2 * n_big + 4:]
        del chip_ref
        p, i = pl.program_id(0), pl.program_id(1)
        x, y, c, chips = _mesh_position()
        me, sibling = 2 * x + y, (x, y, 1 - c)

        pieces = [(0, 0), (0, 1)] + [(a, None) for a in range(1, n_big)]

        def half(piece, slot, which):
            a, q = pieces[piece]
            hs = bufs[a].shape[1] // 2
            cols = slice(None) if q is None else pl.ds(q * D_MODEL, D_MODEL)
            return bufs[a].at[slot, pl.ds(which * hs, hs), cols]

        def send(piece, j):
            mine = half(piece, me, c)
            return _remote(mine, mine, send_sems, recv_sems, 6 * piece + j, (chips[j][0], chips[j][1], c))

        def arrival(piece, j):
            landed = half(piece, 2 * chips[j][0] + chips[j][1], c)
            return _remote(landed, landed, send_sems, recv_sems, 6 * piece + j, (chips[j][0], chips[j][1], c))

        def passed_on(piece, j, which):
            landed = half(piece, 2 * chips[j][0] + chips[j][1], which)
            return _remote(landed, landed, send_sems, recv_sems, 6 * piece + 3 + j, sibling)

        def conv_copy(j, slot):
            return _remote(cw.at[slot], cw.at[slot], send_sems, recv_sems, 6 * len(pieces) + j,
                           (chips[j][0], chips[j][1], c))

        def land(piece, j):
            arrival(piece, j).wait_recv()
            passed_on(piece, j, c).start()
            passed_on(piece, j, 1 - c).wait_recv()

        def slab_copy(pv):
            src = bufs[0].at[shard_of(pv // 2, me), :, pl.ds((pv % 2) * D_MODEL, D_MODEL)]
            return pltpu.make_async_copy(src, slab.at[pv % 2], slab_sems.at[pv % 2])

        @pl.when((p == 0) & (i == 0))
        def _():
            for q in range(2):
                send(q, 0).start()
                send(q, 1).start()
            slab_copy(0).start()

        @pl.when(i == 0)
        def _():
            for pv in range(N_GROUPS):
                @pl.when(p == pv)
                def _(pv=pv):
                    slab_copy(pv).wait()

        rows = pl.ds(pl.multiple_of(i * tm, tm), tm)

        @pl.when(p == 0)
        def _():
            xt = x_ref[...]
            r = lax.rsqrt(jnp.mean(xt * xt, axis=-1, keepdims=True) + EPS)
            h = (xt * r) * g_ref[...]
            h_all[rows, :] = h.astype(_MXU_DTYPE)
            ht_ref[...] = jnp.transpose(h).astype(_MXU_DTYPE)

        step = p * n_tiles + i
        z_slot = _slot_of_group(2 * shard_of(p // 2, me) + p % 2)
        part = tm // Z_STREAMS

        def z_copies(buf, slot, first_row):
            return [pltpu.make_async_copy(z_buf.at[buf, pl.ds(k * part, part), :],
                                          z_hbm.at[slot, pl.ds(first_row + k * part, part), :], z_sems.at[buf, k])
                    for k in range(Z_STREAMS)]

        @pl.when(step >= 2)
        def _():
            for cp in z_copies(step % 2, 0, 0):
                cp.wait()

        z_buf[step % 2] = _mm(h_all[rows, :], slab[p % 2])
        for cp in z_copies(step % 2, z_slot, pl.multiple_of(i * tm, tm)):
            cp.start()

        def relay(q):
            landed = half(q, 2 * chips[q][0] + chips[q][1], c)
            to = chips[1 - q]
            return _remote(landed, landed, send_sems, recv_sems, 6 * q + 2, (to[0], to[1], c))

        landings = {1: [(0, 0)], 2: [(1, 0), (1, 1)], 3: [(0, 1)], 5: [(0, 2)], 6: [(1, 2)]}

        def end_of_pass(pv):
            for q, j in landings.get(pv - 1, []):
                land(q, j)
                if j == q:
                    relay(q).start()
            if pv - 1 == 1:
                for piece in range(2, len(pieces)):
                    for jj in range(3):
                        send(piece, jj).start()
                for jj in range(3):
                    conv_copy(jj, me).start()
            if pv - 1 in (5, 6):
                for piece in range(2, len(pieces)):
                    for jj in ((0, 1) if pv - 1 == 5 else (2,)):
                        land(piece, jj)
            slab_copy(pv).start()

        @pl.when(i == n_tiles - 1)
        def _():
            for pv in range(1, N_GROUPS):
                pl.when(p == pv - 1)(functools.partial(end_of_pass, pv))

        @pl.when((p == last_pass) & (i == n_tiles - 1))
        def _():
            for buf in range(min(2, N_GROUPS * n_tiles)):
                for cp in z_copies(buf, 0, 0):
                    cp.wait()
            for j in range(3):
                conv_copy(j, 2 * chips[j][0] + chips[j][1]).wait_recv()
            for piece in range(len(pieces)):
                for j in range(3):
                    (relay(piece) if (piece < 2 and j == 2) else send(piece, j)).wait_send()
                    passed_on(piece, j, c).wait_send()
            for j in range(3):
                conv_copy(j, me).wait_send()

    def first_pass_tile(p, i, chip_ref):
        return jnp.where(p == 0, i, n_tiles - 1)

    hbm = pl.BlockSpec(memory_space=pl.ANY)
    operands = list(slotted) + [conv_slotted]
    grid_spec = pltpu.PrefetchScalarGridSpec(
        num_scalar_prefetch=1, grid=(N_GROUPS, n_tiles),
        in_specs=[pl.BlockSpec((tm, d), lambda p, i, chip_ref: (first_pass_tile(p, i, chip_ref), 0)),
                  pl.BlockSpec((1, d), lambda p, i, chip_ref: (0, 0))] + [hbm] * (n_big + 1),
        out_specs=[hbm] * (n_big + 2) + [pl.BlockSpec((d, tm), lambda p, i, chip_ref: (0, first_pass_tile(p, i, chip_ref)))],
        scratch_shapes=[pltpu.VMEM((tokens, d), _MXU_DTYPE), pltpu.VMEM((2, d, D_MODEL), _MXU_DTYPE),
                        pltpu.VMEM((2, tm, D_MODEL), F32),
                        pltpu.SemaphoreType.DMA((n_sem,)), pltpu.SemaphoreType.DMA((n_sem,)),
                        pltpu.SemaphoreType.DMA((2,)), pltpu.SemaphoreType.DMA((2, Z_STREAMS))])
    out = pl.pallas_call(
        body, name="inproj_fwd_gather", grid_spec=grid_spec,
        out_shape=[jax.ShapeDtypeStruct(a.shape, a.dtype) for a in operands]
        + [jax.ShapeDtypeStruct((N_GROUPS, tokens, D_MODEL), F32), jax.ShapeDtypeStruct((d, tokens), _MXU_DTYPE)],
        input_output_aliases={3 + a: a for a in range(n_big + 1)},
        compiler_params=_params(("arbitrary", "arbitrary")),
    )(chip, x2d, norm_g, *operands)
    return out[n_big + 1], out[n_big + 2], out[:n_big], out[n_big]


def _lane_blocks(x):
    return [x[:, k * LANES:(k + 1) * LANES] for k in range(x.shape[1] // LANES)]


def _block_diag(x, w_ref, transposed=False):
    mm = _mm_nt if transposed else _mm
    return jnp.concatenate([mm(xk, w_ref[k]) for k, xk in enumerate(_lane_blocks(x))], axis=1)


def _lru_decay(gr, sp):
    log_a = (-LRU_C) * gr * sp
    a = jnp.exp(log_a)
    y = 2.0 * log_a
    mult_sq = jnp.where(y > -1e-3, -y * (1.0 + 0.5 * y), 1.0 - a * a)
    inv_mult = lax.rsqrt(jnp.maximum(mult_sq, 1e-37))
    return a, mult_sq * inv_mult, inv_mult


def _tile_rows(width):
    return lax.broadcasted_iota(jnp.int32, (SUBLANES, width), 0)


def _scan_forward(a_scr, u_scr, h_scr, seq):
    width = a_scr.shape[1]
    rows = _tile_rows(width)

    def tile(j, carry):
        sl = pl.ds(pl.multiple_of(j * SUBLANES, SUBLANES), SUBLANES)
        a = a_scr[sl, :]
        u = u_scr[sl, :]
        for d in (1, 2, 4):
            keep = rows >= d
            a_sh = jnp.where(keep, pltpu.roll(a, d, 0), 1.0)
            u_sh = jnp.where(keep, pltpu.roll(u, d, 0), 0.0)
            u = a * u_sh + u
            a = a * a_sh
        h = u + a * carry
        h_scr[sl, :] = h
        return jnp.broadcast_to(h[SUBLANES - 1:SUBLANES, :], (SUBLANES, width))

    lax.fori_loop(0, seq // SUBLANES, tile, jnp.zeros((SUBLANES, width), F32))


def _scan_backward(c_scr, d_scr, g_scr, seq):
    width = c_scr.shape[1]
    rows = _tile_rows(width)
    n_tiles = seq // SUBLANES

    def tile(jj, carry):
        j = n_tiles - 1 - jj
        sl = pl.ds(pl.multiple_of(j * SUBLANES, SUBLANES), SUBLANES)
        c = c_scr[sl, :]
        g = d_scr[sl, :]
        for d in (1, 2, 4):
            keep = rows < SUBLANES - d
            c_sh = jnp.where(keep, pltpu.roll(c, SUBLANES - d, 0), 1.0)
            g_sh = jnp.where(keep, pltpu.roll(g, SUBLANES - d, 0), 0.0)
            g = c * g_sh + g
            c = c * c_sh
        g = g + c * carry
        g_scr[sl, :] = g
        return jnp.broadcast_to(g[0:1, :], (SUBLANES, width))

    lax.fori_loop(0, n_tiles, tile, jnp.zeros((SUBLANES, width), F32))


LRU_BLOCKS_PER_STEP = 2
LRU_LANES = LRU_BLOCKS_PER_STEP * LANES
LRU_STEPS = N_BLK // LRU_BLOCKS_PER_STEP


def _branch_a_fwd(z, conv_w, conv_b, wx, bx, wa, ba, lam, batch, seq):
    tokens = batch * seq

    def body(z_ref, cw_ref, cb_ref, wx_ref, bx_ref, wa_ref, ba_ref, lam_ref, ya_ref, hl_ref, kept_ref, a_scr, u_scr):
        xa = z_ref[0]
        ga = z_ref[1]
        xc = (cb_ref[...] + cw_ref[3:4, :] * xa + cw_ref[2:3, :] * _shift_down(xa, 1)
              + cw_ref[1:2, :] * _shift_down(xa, 2) + cw_ref[0:1, :] * _shift_down(xa, 3))
        gi = _sigmoid(_block_diag(xc, wx_ref) + bx_ref[...])
        gr = _sigmoid(_block_diag(xc, wa_ref) + ba_ref[...])
        a, mult, _ = _lru_decay(gr, _softplus(-lam_ref[...]))
        kept_ref[0], kept_ref[1], kept_ref[2] = xc, gi, gr
        a_scr[...] = a
        u_scr[...] = mult * gi * xc
        _scan_forward(a_scr, u_scr, hl_ref, seq)
        ya_ref[...] = (hl_ref[...] * (ga * _sigmoid(ga))).astype(_MXU_DTYPE)

    blk = pl.BlockSpec((seq, LRU_LANES), lambda b, c: (b, c))
    vec = pl.BlockSpec((1, LRU_LANES), lambda b, c: (0, c))
    mat = pl.BlockSpec((LRU_BLOCKS_PER_STEP, LANES, LANES), lambda b, c: (c, 0, 0))
    return pl.pallas_call(
        body, name="branch_a_fwd",
        grid=(batch, LRU_STEPS),
        in_specs=[pl.BlockSpec((2, seq, LRU_LANES), lambda b, c: (2, b, c)),
                  pl.BlockSpec((CONV_WIDTH, LRU_LANES), lambda b, c: (0, c)), vec, mat, vec, mat, vec, vec],
        out_specs=[blk, blk, pl.BlockSpec((3, seq, LRU_LANES), lambda b, c: (0, b, c))],
        out_shape=[jax.ShapeDtypeStruct((tokens, D_MODEL), _MXU_DTYPE), jax.ShapeDtypeStruct((tokens, D_MODEL), F32),
                   jax.ShapeDtypeStruct((3, tokens, D_MODEL), F32)],
        scratch_shapes=[pltpu.VMEM((seq, LRU_LANES), F32), pltpu.VMEM((seq, LRU_LANES), F32)],
        compiler_params=_params(("parallel", "parallel")),
    )(z, conv_w, conv_b, wx, bx, wa, ba, lam)


def _branch_a_bwd(z, hl, kept, dya, dz, conv_w, wx, wa, lam, batch, seq):
    def body(z_ref, hl_ref, kept_ref, dya_ref, dz_in_ref, cw_ref, wx_ref, wa_ref, lam_ref,
             dz_ref, dcw_ref, dcb_ref, dwx_ref, dbx_ref, dwa_ref, dba_ref, dlam_ref, c_scr, d_scr):
        del dz_in_ref
        g_scr = d_scr
        xa = z_ref[0]
        ga = z_ref[1]
        hl = hl_ref[...]
        dya = dya_ref[...]
        xc, gi, gr = kept_ref[0], kept_ref[1], kept_ref[2]
        sp = _softplus(-lam_ref[...])
        a, mult, inv_mult = _lru_decay(gr, sp)
        sga = _sigmoid(ga)
        dz_ref[1] = (dya * hl * (sga * (1.0 + ga * (1.0 - sga)))).astype(_MXU_DTYPE)
        c_scr[...] = _shift_up(a, 1)
        d_scr[...] = dya * (ga * sga)
        _scan_backward(c_scr, d_scr, g_scr, seq)
        g = g_scr[...]
        da = g * _shift_down(hl, 1)
        dmult = g * gi * xc
        dgi = g * mult * xc
        dxc = g * mult * gi
        dlog_a = da * a - dmult * (a * a) * inv_mult
        dgr = dlog_a * (-LRU_C) * sp
        dsp = jnp.sum(dlog_a * gr, axis=0, keepdims=True) * (-LRU_C)
        dlam = -dsp * _sigmoid(-lam_ref[...])
        dpi = dgi * gi * (1.0 - gi)
        dpr = dgr * gr * (1.0 - gr)
        dxc = dxc + _block_diag(dpi, wx_ref, transposed=True) + _block_diag(dpr, wa_ref, transposed=True)
        dwx = jnp.stack([_mm_tn(xk, dk) for xk, dk in zip(_lane_blocks(xc), _lane_blocks(dpi))])
        dwa = jnp.stack([_mm_tn(xk, dk) for xk, dk in zip(_lane_blocks(xc), _lane_blocks(dpr))])
        dbx = jnp.sum(dpi, axis=0, keepdims=True)
        dba = jnp.sum(dpr, axis=0, keepdims=True)
        ahead = [dxc if k == CONV_WIDTH - 1 else _shift_up(dxc, CONV_WIDTH - 1 - k) for k in range(CONV_WIDTH)]
        dxa = sum(cw_ref[k:k + 1, :] * ahead[k] for k in range(CONV_WIDTH))
        dz_ref[0] = dxa.astype(_MXU_DTYPE)
        dcb = jnp.sum(dxc, axis=0, keepdims=True)
        dcw = [jnp.sum(ahead[k] * xa, axis=0, keepdims=True) for k in range(CONV_WIDTH)]

        @pl.when(pl.program_id(1) == 0)
        def _():
            for k in range(CONV_WIDTH):
                dcw_ref[k:k + 1, :] = dcw[k]
            dcb_ref[...] = dcb
            dwx_ref[...] = dwx
            dbx_ref[...] = dbx
            dwa_ref[...] = dwa
            dba_ref[...] = dba
            dlam_ref[...] = dlam

        @pl.when(pl.program_id(1) != 0)
        def _():
            for k in range(CONV_WIDTH):
                dcw_ref[k:k + 1, :] += dcw[k]
            dcb_ref[...] += dcb
            dwx_ref[...] += dwx
            dbx_ref[...] += dbx
            dwa_ref[...] += dwa
            dba_ref[...] += dba
            dlam_ref[...] += dlam

    tokens = batch * seq
    blk = pl.BlockSpec((seq, LRU_LANES), lambda c, b: (b, c))
    vec = pl.BlockSpec((1, LRU_LANES), lambda c, b: (0, c))
    mat = pl.BlockSpec((LRU_BLOCKS_PER_STEP, LANES, LANES), lambda c, b: (c, 0, 0))
    vec_shape = jax.ShapeDtypeStruct((1, D_MODEL), F32)
    mat_shape = jax.ShapeDtypeStruct((N_BLK, LANES, LANES), F32)
    return pl.pallas_call(
        body, name="branch_a_bwd",
        grid=(LRU_STEPS, batch),
        in_specs=[pl.BlockSpec((2, seq, LRU_LANES), lambda c, b: (2, b, c)), blk,
                  pl.BlockSpec((3, seq, LRU_LANES), lambda c, b: (0, b, c)), blk,
                  pl.BlockSpec(memory_space=pl.ANY),
                  pl.BlockSpec((CONV_WIDTH, LRU_LANES), lambda c, b: (0, c)), mat, mat, vec],
        out_specs=[pl.BlockSpec((2, seq, LRU_LANES), lambda c, b: (2, b, c)),
                   pl.BlockSpec((CONV_WIDTH, LRU_LANES), lambda c, b: (0, c)), vec, mat, vec, mat, vec, vec],
        out_shape=[jax.ShapeDtypeStruct((N_GROUPS, tokens, D_MODEL), _MXU_DTYPE),
                   jax.ShapeDtypeStruct((CONV_WIDTH, D_MODEL), F32), vec_shape, mat_shape, vec_shape, mat_shape,
                   vec_shape, vec_shape],
        scratch_shapes=[pltpu.VMEM((seq, LRU_LANES), F32)] * 2,
        input_output_aliases={4: 0},
        compiler_params=_params(("parallel", "arbitrary"), vmem=VMEM_LIMIT_BIG),
    )(z, hl, kept, dya, dz, conv_w, wx, wa, lam)


def _chunk_masks(transposed=False):
    r = lax.broadcasted_iota(jnp.int32, (CHUNK, CHUNK), 0)
    c = lax.broadcasted_iota(jnp.int32, (CHUNK, CHUNK), 1)
    return r <= c if transposed else r >= c


def _row_blocks(seq, fn):
    block = min(256, seq)

    def trip(i, carry):
        fn(pl.ds(pl.multiple_of(i * block, block), block))
        return carry

    lax.fori_loop(0, seq // block, trip, 0)


def _hgrn_prepare(z_ref, lb_ref, f_scr, logf_scr, qh_scr, seq):
    lb = _sigmoid(lb_ref[0:1, :] - lb_ref[1:2, :])

    def block(rows):
        q = z_ref[0, rows, :]
        f = lb + (1.0 - lb) * _sigmoid(z_ref[1, rows, :])
        f_scr[rows, :] = f
        logf_scr[rows, :] = jnp.log(f)
        qh_scr[rows, :] = q * _sigmoid(q)

    _row_blocks(seq, block)
    return lb


def _cumsum_rows(x, reverse=False):
    shift = _shift_up if reverse else _shift_down
    d = 1
    while d < x.shape[0]:
        x = x + shift(x, d)
        d *= 2
    return x


def _lane_mean(x):
    return jnp.mean(x, axis=-1, keepdims=True)


def _token_contractions(lhs_scr, rhs_scr, out_ref, seq):
    rows_id = lax.broadcasted_iota(jnp.int32, (LANES, LANES), 0)

    def transposed(p):
        rows = pl.ds(pl.multiple_of(p * LANES, LANES), LANES)
        return jnp.transpose(lhs_scr[rows, :]).astype(_MXU_DTYPE), rhs_scr[rows, :]

    def contract(p, s):
        lhs_t, rhs = s
        return (_mm(lhs_t, jnp.where(rows_id < CHUNK, rhs, 0.0)), _mm(lhs_t, jnp.where(rows_id >= CHUNK, rhs, 0.0)))

    def store(p, out):
        out_ref[2 * p] = out[0]
        out_ref[2 * p + 1] = out[1]

    _independent_trips(seq // LANES, [transposed, contract], store)


def _chunk_rows(c):
    return pl.ds(pl.multiple_of(c * CHUNK, CHUNK), CHUNK)


def _chunk_terms(c, z_ref, f_scr, qh_scr, b_scr):
    rows = _chunk_rows(c)
    b = b_scr[rows, :]
    b_mid = b_scr[pl.ds(c * CHUNK + CHUNK // 2, 1), :]
    b_last = b_scr[pl.ds(c * CHUNK + CHUNK - 1, 1), :]
    qh = qh_scr[rows, :]
    k = 1.0 - f_scr[rows, :]
    v = z_ref[2, rows, :]
    e_q = jnp.exp(b - b_mid) * HG_SCALE
    e_k = jnp.exp(b_mid - b)
    e_qi = jnp.exp(b) * HG_SCALE
    e_ks = jnp.exp(b_last - b)
    decay = jnp.exp(b_last)
    return rows, qh, k, v, e_q, e_k, e_qi, e_ks, decay


def _independent_trips(n, stages, store, group=CHUNKS_IN_FLIGHT):
    stages = stages if isinstance(stages, (list, tuple)) else [stages]
    group = min(group, n)

    def trip(g, carry):
        ids = [g * group + i for i in range(group)]
        state = [stages[0](c) for c in ids]
        for stage in stages[1:]:
            state = [stage(c, s) for c, s in zip(ids, state)]
        for c, s in zip(ids, state):
            store(c, s)
        return carry

    lax.fori_loop(0, n // group, trip, 0)


def _branch_b_fwd(z, lb_logits, hg_g, batch, seq):
    tokens = batch * seq
    n_chunks = seq // CHUNK

    def body(z_ref, lb_ref, g_ref, yb_ref, st_ref, kept_ref, logf_scr, o_scr, qi_scr, ks_scr, dec_scr):
        f_scr, qh_scr, b_scr, o_kept = (kept_ref.at[k] for k in range(4))
        _hgrn_prepare(z_ref, lb_ref, f_scr, logf_scr, qh_scr, seq)
        causal = _chunk_masks()
        gain = g_ref[...]

        def cumulate(c):
            return _cumsum_rows(logf_scr[_chunk_rows(c), :])

        def store_cumulated(c, b):
            b_scr[_chunk_rows(c), :] = b

        def scores(c):
            _, qh, k, v, e_q, e_k, e_qi, e_ks, decay = _chunk_terms(c, z_ref, f_scr, qh_scr, b_scr)
            return _mm_nt(qh * e_q, k * e_k), v, qh * e_qi, k * e_ks, decay

        def within_chunk(c, s):
            att, v, q_int, k_st, decay = s
            return _mm(jnp.where(causal, att, 0.0), v), q_int, k_st, decay

        def store_within_chunk(c, out):
            rows = _chunk_rows(c)
            o_scr[rows, :], qi_scr[rows, :], ks_scr[rows, :], dec_scr[pl.ds(c, 1), :] = out

        def carry_state(c, state_t):
            update = st_ref[c]
            st_ref[c] = state_t
            return state_t * dec_scr[pl.ds(c, 1), :] + update

        def finish(c):
            rows = _chunk_rows(c)
            o = o_scr[rows, :] + _mm_nt(qi_scr[rows, :], st_ref[c])
            r = lax.rsqrt(_lane_mean(o * o) + EPS)
            gb = z_ref[3, rows, :]
            return (((o * r) * gain) * (gb * _sigmoid(gb))).astype(_MXU_DTYPE), o

        def store_finished(c, out):
            yb_ref[_chunk_rows(c), :], o_kept[_chunk_rows(c), :] = out

        _independent_trips(n_chunks, cumulate, store_cumulated)
        _independent_trips(n_chunks, [scores, within_chunk], store_within_chunk)
        _token_contractions(z_ref.at[2], ks_scr, st_ref, seq)
        lax.fori_loop(0, n_chunks, carry_state, jnp.zeros((LANES, LANES), F32))
        _independent_trips(n_chunks, finish, store_finished)

    seq_buf = pltpu.VMEM((seq, LANES), F32)
    return pl.pallas_call(
        body, name="branch_b_fwd",
        grid=(batch, N_BLK),
        in_specs=[pl.BlockSpec((4, seq, LANES), lambda b, h: (0, b, h)),
                  pl.BlockSpec((2, LANES), lambda b, h: (0, h)),
                  pl.BlockSpec((1, LANES), lambda b, h: (0, 0))],
        out_specs=[pl.BlockSpec((seq, LANES), lambda b, h: (b, h)),
                   pl.BlockSpec((None, n_chunks, LANES, LANES), lambda b, h: (b * N_BLK + h, 0, 0, 0)),
                   pl.BlockSpec((4, seq, LANES), lambda b, h: (0, b, h))],
        out_shape=[jax.ShapeDtypeStruct((tokens, D_MODEL), _MXU_DTYPE),
                   jax.ShapeDtypeStruct((batch * N_BLK, n_chunks, LANES, LANES), F32),
                   jax.ShapeDtypeStruct((4, tokens, D_MODEL), F32)],
        scratch_shapes=[seq_buf] * 4 + [pltpu.VMEM((n_chunks, LANES), F32)],
        compiler_params=_params(("parallel", "parallel")),
    )(z, lb_logits, hg_g)


def _branch_b_bwd(z, states, kept, dyb, dz, lb_logits, hg_g, batch, seq):
    n_chunks = seq // CHUNK

    def body(z_ref, st_ref, kept_ref, dyb_ref, dz_in_ref, lb_ref, g_ref, dz_ref, dlog_ref, dg_ref,
             do_scr, qi_scr, dqh_scr, df_scr, dec_scr, dgp_scr, dlb_scr, dst_scr):
        del dz_in_ref
        f_scr, qh_scr, b_scr, o_kept = (kept_ref.at[k] for k in range(4))
        first = (pl.program_id(0) == 0) & (pl.program_id(1) == 0)
        lb = _sigmoid(lb_ref[0:1, :] - lb_ref[1:2, :])
        causal = _chunk_masks()
        anti_causal = _chunk_masks(transposed=True)
        gain = g_ref[...]

        @pl.when(first)
        def _():
            dg_ref[...] = jnp.zeros_like(dg_ref)

        @pl.when(pl.program_id(1) == 0)
        def _():
            dlb_scr[...] = jnp.zeros_like(dlb_scr)

        def output_gradient(c):
            rows = _chunk_rows(c)
            b = b_scr[rows, :]
            q_int = qh_scr[rows, :] * (jnp.exp(b) * HG_SCALE)
            decay = jnp.exp(b_scr[pl.ds(c * CHUNK + CHUNK - 1, 1), :])
            o = o_kept[rows, :]
            r = lax.rsqrt(_lane_mean(o * o) + EPS)
            o_n = o * r
            gb = z_ref[3, rows, :]
            sgb = _sigmoid(gb)
            dyb_c = dyb_ref[rows, :]
            d_ong = dyb_c * (gb * sgb)
            d_gb = (dyb_c * (o_n * gain) * (sgb * (1.0 + gb * (1.0 - sgb)))).astype(_MXU_DTYPE)
            d_gain = jnp.sum(d_ong * o_n, axis=0, keepdims=True)
            d_on = d_ong * gain
            return d_gb, d_gain, r * (d_on - o_n * _lane_mean(d_on * o_n)), q_int, decay

        def store_output_gradient(c, out):
            rows = _chunk_rows(c)
            dz_ref[3, rows, :], dgp_scr[pl.ds(c, 1), :], do_scr[rows, :], qi_scr[rows, :], dec_scr[pl.ds(c, 1), :] = out

        def carry_state_gradient(cc, d_state_t):
            c = n_chunks - 1 - cc
            update = dst_scr[c]
            dst_scr[c] = d_state_t
            return d_state_t * dec_scr[pl.ds(c, 1), :] + update

        def score_gradients(c):
            rows, qh, k, v, e_q, e_k, e_qi, e_ks, decay = _chunk_terms(c, z_ref, f_scr, qh_scr, b_scr)
            state_t = st_ref[c]
            d_state_t = dst_scr[c]
            d_o = do_scr[rows, :]
            q_in, k_in, q_int, k_st = qh * e_q, k * e_k, qh * e_qi, k * e_ks
            first = (_mm_nt(k_in, q_in), _mm_nt(d_o, v), _mm_nt(v, d_o), _mm_nt(k_st, d_state_t), _mm(d_o, state_t),
                     _mm(v, d_state_t))
            d_decay = jnp.sum(state_t * d_state_t, axis=0, keepdims=True)
            return first, d_o, q_in, k_in, q_int, k_st, e_q, e_k, e_qi, e_ks, decay, d_decay

        def input_gradients(c, s):
            (att_t, d_att, d_att_t, dv_inter, dq_int, dk_st), d_o, q_in, k_in, q_int, k_st, e_q, e_k, e_qi, e_ks, decay, d_decay = s
            rows = _chunk_rows(c)
            d_v = _mm(jnp.where(anti_causal, att_t, 0.0), d_o) + dv_inter
            dq_in = _mm(jnp.where(causal, d_att, 0.0), k_in)
            dk_in = _mm(jnp.where(anti_causal, d_att_t, 0.0), q_in)
            d_k = dk_in * e_k + dk_st * e_ks
            kk = dk_st * k_st
            d_b = dq_in * q_in + dq_int * q_int - dk_in * k_in - kk
            d_b_last = jnp.sum(kk, axis=0, keepdims=True) + decay * d_decay
            d_logf = _cumsum_rows(d_b, reverse=True) + d_b_last
            return d_v.astype(_MXU_DTYPE), dq_in * e_q + dq_int * e_qi, d_logf / f_scr[rows, :] - d_k

        def store_input_gradients(c, out):
            rows = _chunk_rows(c)
            dz_ref[2, rows, :], dqh_scr[rows, :], df_scr[rows, :] = out

        def input_activations(rows):
            q = z_ref[0, rows, :]
            sq = _sigmoid(q)
            dz_ref[0, rows, :] = (dqh_scr[rows, :] * (sq * (1.0 + q * (1.0 - sq)))).astype(_MXU_DTYPE)
            sg = _sigmoid(z_ref[1, rows, :])
            d_f = df_scr[rows, :]
            dz_ref[1, rows, :] = (d_f * (1.0 - lb) * sg * (1.0 - sg)).astype(_MXU_DTYPE)
            dlb_scr[...] += jnp.sum(d_f * (1.0 - sg), axis=0, keepdims=True)

        _independent_trips(n_chunks, output_gradient, store_output_gradient)
        _token_contractions(do_scr, qi_scr, dst_scr, seq)
        lax.fori_loop(0, n_chunks, carry_state_gradient, jnp.zeros((LANES, LANES), F32))
        _independent_trips(n_chunks, [score_gradients, input_gradients], store_input_gradients)
        dg_ref[...] += jnp.sum(dgp_scr[...], axis=0, keepdims=True)
        _row_blocks(seq, input_activations)
        d_l0 = dlb_scr[...] * lb * (1.0 - lb)
        dlog_ref[0:1, :] = d_l0
        dlog_ref[1:2, :] = -d_l0

    tokens = batch * seq
    seq_buf = pltpu.VMEM((seq, LANES), F32)
    chunk_rows = pltpu.VMEM((n_chunks, LANES), F32)
    return pl.pallas_call(
        body, name="branch_b_bwd",
        grid=(N_BLK, batch),
        in_specs=[pl.BlockSpec((4, seq, LANES), lambda h, b: (0, b, h)),
                  pl.BlockSpec((None, n_chunks, LANES, LANES), lambda h, b: (b * N_BLK + h, 0, 0, 0)),
                  pl.BlockSpec((4, seq, LANES), lambda h, b: (0, b, h)),
                  pl.BlockSpec((seq, LANES), lambda h, b: (b, h)),
                  pl.BlockSpec(memory_space=pl.ANY),
                  pl.BlockSpec((2, LANES), lambda h, b: (0, h)),
                  pl.BlockSpec((1, LANES), lambda h, b: (0, 0))],
        out_specs=[pl.BlockSpec((4, seq, LANES), lambda h, b: (0, b, h)),
                   pl.BlockSpec((2, LANES), lambda h, b: (0, h)),
                   pl.BlockSpec((1, LANES), lambda h, b: (0, 0))],
        out_shape=[jax.ShapeDtypeStruct((N_GROUPS, tokens, D_MODEL), _MXU_DTYPE),
                   jax.ShapeDtypeStruct((2, D_MODEL), F32),
                   jax.ShapeDtypeStruct((1, LANES), F32)],
        scratch_shapes=[seq_buf] * 4 + [chunk_rows, chunk_rows, pltpu.VMEM((1, LANES), F32),
                                        pltpu.VMEM((n_chunks, LANES, LANES), F32)],
        input_output_aliases={4: 0},
        compiler_params=_params(("arbitrary", "arbitrary")),
    )(z, states, kept, dyb, dz, lb_logits, hg_g)


def _merge_tail(ya, yb, z, x2d, tgt2d, b_merge, final_g, pa, pb, wo):
    tokens, d = x2d.shape
    tm = min(TAIL_TOKENS, tokens)
    n_tiles = tokens // tm

    def body(ya_ref, yb_ref, z_ref, x_ref, t_ref, bm_ref, fg_ref, pa_hbm, pb_hbm, wo_hbm,
             dya_ref, dyb_ref, dx2_ref, dz_ref, loss_ref, dfg_ref, dbm_ref, dpa_hbm, dpb_hbm, dwo_hbm,
             pa_s, pb_s, wo_s, dpa_s, dpb_s, dwo_s):
        i = pl.program_id(0)

        @pl.when(i == 0)
        def _():
            pltpu.sync_copy(pa_hbm, pa_s)
            pltpu.sync_copy(pb_hbm, pb_s)
            pltpu.sync_copy(wo_hbm, wo_s)
            dpa_s[...] = jnp.zeros_like(dpa_s)
            dpb_s[...] = jnp.zeros_like(dpb_s)
            dwo_s[...] = jnp.zeros_like(dwo_s)
            loss_ref[...] = jnp.zeros_like(loss_ref)
            dfg_ref[...] = jnp.zeros_like(dfg_ref)
            dbm_ref[...] = jnp.zeros_like(dbm_ref)

        ya_t = ya_ref[...]
        yb_t = yb_ref[...]
        out_a = _mm(ya_t, pa_s[...])
        out_b = _mm(yb_t, pb_s[...])
        g_a = _sigmoid(z_ref[0] + bm_ref[:, :d])
        g_b = _sigmoid(z_ref[1] + bm_ref[:, d:])
        mixed = g_a * out_a + g_b * out_b
        x2 = x_ref[...] + _mm(mixed, wo_s[...])
        r = lax.rsqrt(jnp.mean(x2 * x2, axis=-1, keepdims=True) + EPS)
        xn = x2 * r
        fg = fg_ref[...]
        diff = xn * fg - t_ref[...]
        loss_ref[...] += jnp.sum(diff * diff) * (0.5 / d)
        dy = diff * (1.0 / d)
        dfg_ref[...] += jnp.sum(dy * xn, axis=0, keepdims=True)
        dxn = dy * fg
        dx2 = r * (dxn - xn * jnp.mean(dxn * xn, axis=-1, keepdims=True))
        dx2_ref[...] = dx2
        dmixed = _mm_nt(dx2, wo_s[...])
        dwo_s[...] += _mm_tn(mixed, dx2)
        dgm_a = dmixed * out_a * g_a * (1.0 - g_a)
        dgm_b = dmixed * out_b * g_b * (1.0 - g_b)
        dz_ref[0] = dgm_a.astype(_MXU_DTYPE)
        dz_ref[1] = dgm_b.astype(_MXU_DTYPE)
        dbm_ref[:, :d] += jnp.sum(dgm_a, axis=0, keepdims=True)
        dbm_ref[:, d:] += jnp.sum(dgm_b, axis=0, keepdims=True)
        dout_a = dmixed * g_a
        dout_b = dmixed * g_b
        dpa_s[...] += _mm_tn(ya_t, dout_a)
        dpb_s[...] += _mm_tn(yb_t, dout_b)
        dya_ref[...] = _mm_nt(dout_a, pa_s[...])
        dyb_ref[...] = _mm_nt(dout_b, pb_s[...])

        @pl.when(i == n_tiles - 1)
        def _():
            pltpu.sync_copy(dpa_s, dpa_hbm)
            pltpu.sync_copy(dpb_s, dpb_hbm)
            pltpu.sync_copy(dwo_s, dwo_hbm)

    tile = pl.BlockSpec((tm, d), lambda i: (i, 0))
    gm = pl.BlockSpec((2, tm, d), lambda i: (3, i, 0))
    row = lambda n: pl.BlockSpec((1, n), lambda i: (0, 0))
    hbm = pl.BlockSpec(memory_space=pl.ANY)
    act = jax.ShapeDtypeStruct((tokens, d), F32)
    mat = jax.ShapeDtypeStruct((d, d), F32)
    return pl.pallas_call(
        body, name="merge_tail",
        grid=(n_tiles,),
        in_specs=[tile, tile, gm, tile, tile, row(2 * d), row(d), hbm, hbm, hbm],
        out_specs=[tile, tile, tile, gm, row(LANES), row(d), row(2 * d), hbm, hbm, hbm],
        out_shape=[act, act, act, jax.ShapeDtypeStruct((N_GROUPS, tokens, d), _MXU_DTYPE),
                   jax.ShapeDtypeStruct((1, LANES), F32), jax.ShapeDtypeStruct((1, d), F32),
                   jax.ShapeDtypeStruct((1, 2 * d), F32), mat, mat, mat],
        scratch_shapes=[pltpu.VMEM((d, d), _MXU_DTYPE)] * 3 + [pltpu.VMEM((d, d), F32)] * 3,
        compiler_params=_params(("arbitrary",)),
    )(ya, yb, z, x2d, tgt2d, b_merge, final_g, pa, pb, wo)


def _inproj_dw_exchange(h_t, dz, scatter):
    d, tokens = h_t.shape
    tm = min(CONTRACT_TOKENS, tokens)
    n_i = tokens // tm
    half = d // 2

    def body(h_ref, dz_ref, *rest):
        n_in, n_out = scatter.n_in, scatter.n_out
        dw_hbm, land_hbm = rest[n_in:n_in + 2]
        acc, local_sems, send_sems, recv_sems = rest[n_in + 2 + n_out:n_in + 6 + n_out]
        carried = scatter.copies(rest[:n_in], rest[n_in + 2:n_in + 2 + n_out], rest[n_in + 6 + n_out:])
        s, i = pl.program_id(0), pl.program_id(1)
        x, y, c, _ = _mesh_position()

        @pl.when((s == 0) & (i == 0))
        def _():
            for cp in carried:
                cp.start()

        part = _mm(h_ref[...], dz_ref[...])
        buf = acc.at[s % 2]

        @pl.when(i == 0)
        def _():
            buf[...] = part

        @pl.when(i != 0)
        def _():
            buf[...] += part

        def copies(k):
            g = _SLOT_TO_GROUP[k]
            cols = pl.ds((g % 2) * D_MODEL, D_MODEL)
            src = acc.at[k % 2]
            mine = pltpu.make_async_copy(src, dw_hbm.at[g // 2, :, cols], local_sems.at[k % 2])
            theirs = _remote(src.at[pl.ds((1 - c) * half, half), :], land_hbm.at[g // 2, :, cols],
                             send_sems, recv_sems, k, (x, y, 1 - c))
            return mine, theirs

        for k in range(N_GROUPS):
            @pl.when((s == k) & (i == n_i - 1))
            def _(k=k):
                if k > 0:
                    mine, theirs = copies(k - 1)
                    mine.wait()
                    theirs.wait_send()
                mine, theirs = copies(k)
                mine.start()
                theirs.start()
                if k == N_GROUPS - 1:
                    mine.wait()
                    theirs.wait_send()
                    for kk in range(N_GROUPS):
                        copies(kk)[1].wait_recv()
                    for cp in carried:
                        cp.wait()

    hbm = pl.BlockSpec(memory_space=pl.ANY)
    more = scatter.plumbing(first_operand=2, first_output=2)
    return pl.pallas_call(
        body, name="inproj_dw_exchange",
        grid=(N_GROUPS, n_i),
        in_specs=[pl.BlockSpec((d, tm), lambda s, i: (0, i)),
                  pl.BlockSpec((None, tm, D_MODEL), lambda s, i: (s, i, 0))] + more[1],
        out_specs=[hbm, hbm] + more[2],
        out_shape=[jax.ShapeDtypeStruct((N_SHARDS, d, 2 * D_MODEL), F32),
                   jax.ShapeDtypeStruct((N_SHARDS, half, 2 * D_MODEL), F32)] + more[3],
        scratch_shapes=[pltpu.VMEM((2, d, D_MODEL), F32), pltpu.SemaphoreType.DMA((2,)),
                        pltpu.SemaphoreType.DMA((N_GROUPS,)), pltpu.SemaphoreType.DMA((N_GROUPS,))] + more[4],
        input_output_aliases=more[5],
        compiler_params=_params(("arbitrary", "arbitrary")),
    )(h_t, dz, *more[0])


def _inproj_dx(dz, w_all, x2d, dx2, norm_g, scatter):
    tokens, d = x2d.shape
    tm = min(TAIL_TOKENS, tokens)
    n_tiles = tokens // tm

    def body(dz_ref, w_hbm, x_ref, dx2_ref, g_ref, *rest):
        n_in, n_out = scatter.n_in, scatter.n_out
        dx_ref, dg_ref = rest[n_in:n_in + 2]
        w_res = rest[n_in + 2 + n_out]
        copies = scatter.copies(rest[:n_in], rest[n_in + 2:n_in + 2 + n_out], rest[n_in + 3 + n_out:])
        i = pl.program_id(0)

        @pl.when(i == 0)
        def _():
            for cp in copies:
                cp.start()
            for slot, g in enumerate(_SLOT_TO_GROUP):
                pltpu.sync_copy(w_hbm.at[g // 2, :, pl.ds((g % 2) * D_MODEL, D_MODEL)],
                                w_res.at[:, pl.ds(slot * D_MODEL, D_MODEL)])
            dg_ref[...] = jnp.zeros_like(dg_ref)

        dz_all = jnp.concatenate([dz_ref[s] for s in range(N_GROUPS)], axis=1)
        dh = jnp.transpose(_mm_nt(w_res[...], dz_all))
        x = x_ref[...]
        r = lax.rsqrt(jnp.mean(x * x, axis=-1, keepdims=True) + EPS)
        xn = x * r
        dg_ref[...] += jnp.sum(dh * xn, axis=0, keepdims=True)
        dxn = dh * g_ref[...]
        dx_ref[...] = r * (dxn - xn * jnp.mean(dxn * xn, axis=-1, keepdims=True)) + dx2_ref[...]

        @pl.when(i == n_tiles - 1)
        def _():
            for cp in copies:
                cp.wait()

    tile = pl.BlockSpec((tm, d), lambda i: (i, 0))
    hbm = pl.BlockSpec(memory_space=pl.ANY)
    more = scatter.plumbing(first_operand=5, first_output=2)
    return pl.pallas_call(
        body, name="inproj_dx", grid=(n_tiles,),
        in_specs=[pl.BlockSpec((N_GROUPS, tm, D_MODEL), lambda i: (0, i, 0)), hbm, tile, tile,
                  pl.BlockSpec((1, d), lambda i: (0, 0))] + more[1],
        out_specs=[tile, pl.BlockSpec((1, d), lambda i: (0, 0))] + more[2],
        out_shape=[jax.ShapeDtypeStruct((tokens, d), F32), jax.ShapeDtypeStruct((1, d), F32)] + more[3],
        scratch_shapes=[pltpu.VMEM((d, N_GROUPS * D_MODEL), _MXU_DTYPE)] + more[4],
        input_output_aliases=more[5],
        compiler_params=_params(("arbitrary",)),
    )(dz, w_all, x2d, dx2, norm_g, *more[0])


def _row_tile(rows, cols, itemsize=4, budget=2 * 1024 * 1024):
    tr = rows
    while tr * cols * itemsize > budget and tr % 16 == 0:
        tr //= 2
    return tr


def _cast_into_slot(a, chip, dtype, name):
    rows, cols = a.shape
    tr = _row_tile(rows, cols)

    def body(chip_ref, a_ref, o_ref):
        del chip_ref
        o_ref[...] = a_ref[...].astype(dtype)

    grid_spec = pltpu.PrefetchScalarGridSpec(
        num_scalar_prefetch=1, grid=(rows // tr,),
        in_specs=[pl.BlockSpec((tr, cols), lambda i, chip_ref: (i, 0))],
        out_specs=pl.BlockSpec((None, tr, cols), lambda i, chip_ref: (chip_ref[0], i, 0)))
    return pl.pallas_call(body, name=name, grid_spec=grid_spec,
                          out_shape=jax.ShapeDtypeStruct((N_SHARDS, rows, cols), dtype),
                          compiler_params=_params(("arbitrary",)))(chip, a)


def _sum_slots(stack, name):
    n, rows, cols = stack.shape
    tr = _row_tile(rows, cols * n)

    def body(s_ref, o_ref):
        total = s_ref[0].astype(F32)
        for k in range(1, n):
            total = total + s_ref[k].astype(F32)
        o_ref[...] = total

    return pl.pallas_call(body, name=name, grid=(rows // tr,),
                          in_specs=[pl.BlockSpec((n, tr, cols), lambda i: (0, i, 0))],
                          out_specs=pl.BlockSpec((tr, cols), lambda i: (i, 0)),
                          out_shape=jax.ShapeDtypeStruct((rows, cols), F32),
                          compiler_params=_params(("parallel",)))(stack)


def _add_half(full, landed, place, name):
    n, rows, cols = full.shape
    half = rows // 2
    tr = _row_tile(half, cols)
    nb = half // tr

    def body(place_ref, a_ref, b_ref, o_ref, own_ref):
        total = (a_ref[...] + b_ref[...]).astype(_MXU_DTYPE)
        o_ref[...] = total

        @pl.when(pl.program_id(1) == place_ref[1])
        def _():
            own_ref[...] = total

    grid_spec = pltpu.PrefetchScalarGridSpec(
        num_scalar_prefetch=1, grid=(nb, n),
        in_specs=[pl.BlockSpec((None, tr, cols), lambda i, j, place_ref: (j, place_ref[0] * nb + i, 0)),
                  pl.BlockSpec((None, tr, cols), lambda i, j, place_ref: (j, i, 0))],
        out_specs=[pl.BlockSpec((None, tr, cols), lambda i, j, place_ref: (j, i, 0)),
                   pl.BlockSpec((None, tr, cols), lambda i, j, place_ref: (place_ref[1], i, 0))])
    shape = jax.ShapeDtypeStruct((n, half, cols), _MXU_DTYPE)
    return pl.pallas_call(body, name=name, grid_spec=grid_spec, out_shape=[shape, shape],
                          compiler_params=_params(("parallel", "arbitrary")))(place, full, landed)


def _adamw_update(w, grad, m, v):
    c1 = 1.0 - ADAM_B1 ** ADAM_STEP
    c2 = 1.0 - ADAM_B2 ** ADAM_STEP
    nm = ADAM_B1 * m + (1.0 - ADAM_B1) * grad
    nv = ADAM_B2 * v + (1.0 - ADAM_B2) * (grad * grad)
    return (-ADAM_LR) * ((nm / c1) / (jnp.sqrt(nv / c2) + ADAM_EPS) + ADAM_WD * w), nm, nv


def _adamw(w, g, m, v, name):
    rows, cols = w.shape
    tr = _row_tile(rows, cols, budget=1024 * 1024)

    def body(w_ref, g_ref, m_ref, v_ref, d_ref, nm_ref, nv_ref):
        d_ref[...], nm_ref[...], nv_ref[...] = _adamw_update(w_ref[...], g_ref[...], m_ref[...], v_ref[...])

    spec = pl.BlockSpec((tr, cols), lambda i: (i, 0))
    shape = jax.ShapeDtypeStruct((rows, cols), F32)
    return pl.pallas_call(body, name=name, grid=(rows // tr,), in_specs=[spec] * 4, out_specs=[spec] * 3,
                          out_shape=[shape] * 3, compiler_params=_params(("parallel",)))(w, g, m, v)


def _adamw_halves(w, g_mine, g_sibling, m, v, core, name):
    rows, cols = w.shape
    half = rows // 2
    tr = _row_tile(half, cols, budget=1024 * 1024)
    nb = half // tr

    def body(core_ref, w_ref, gm_ref, gs_ref, m_ref, v_ref, g_ref, d_ref, nm_ref, nv_ref):
        mine = pl.program_id(0) // nb == core_ref[0]
        grad = jnp.where(mine, gm_ref[...], gs_ref[...])
        g_ref[...] = grad
        d_ref[...], nm_ref[...], nv_ref[...] = _adamw_update(w_ref[...], grad, m_ref[...], v_ref[...])

    spec = pl.BlockSpec((tr, cols), lambda i, core_ref: (i, 0))
    mine_spec = pl.BlockSpec((tr, cols), lambda i, core_ref: (jnp.where(i // nb == core_ref[0], i % nb, 0), 0))
    sibling_spec = pl.BlockSpec((tr, cols), lambda i, core_ref: (jnp.where(i // nb == core_ref[0], 0, i % nb), 0))
    grid_spec = pltpu.PrefetchScalarGridSpec(num_scalar_prefetch=1, grid=(rows // tr,),
                                             in_specs=[spec, mine_spec, sibling_spec, spec, spec], out_specs=[spec] * 4)
    shape = jax.ShapeDtypeStruct((rows, cols), F32)
    return pl.pallas_call(body, name=name, grid_spec=grid_spec, out_shape=[shape] * 4,
                          compiler_params=_params(("parallel",)))(core, w, g_mine, g_sibling, m, v)


def _local_step(x, loss_target, gather, reduction, b_merge, conv_b, rg_wx, rg_bx, rg_wa, rg_ba, rg_lambda,
                hg_lb_logits, hg_norm_g, norm_g, final_norm_g):
    batch, seq, d = x.shape
    x2d = x.reshape(batch * seq, d)
    tgt2d = loss_target.reshape(batch * seq, d)
    z, h_t, (w_all, pa, pb, wo), cw_all = _inproj_fwd_gather(x2d, norm_g, *gather)
    pa, pb, wo = (t.reshape(d, d) for t in (pa, pb, wo))
    conv_w = jnp.transpose(cw_all, (1, 0, 2)).reshape(CONV_WIDTH, d)
    lru = (conv_w, conv_b, rg_wx, rg_bx, rg_wa, rg_ba, rg_lambda)
    ya, hl, kept = _branch_a_fwd(z, *lru, batch, seq)
    yb, states, kept_b = _branch_b_fwd(z, hg_lb_logits, hg_norm_g, batch, seq)
    dya, dyb, dx2, dz, loss, d_final_g, d_b_merge, d_pa, d_pb, d_wo = _merge_tail(
        ya, yb, z, x2d, tgt2d, b_merge, final_norm_g, pa, pb, wo)
    dz, d_lb_logits, d_hg_g = _branch_b_bwd(z, states, kept_b, dyb, dz, hg_lb_logits, hg_norm_g, batch, seq)
    dz, d_conv_w, d_conv_b, d_wx, d_bx, d_wa, d_ba, d_lam = _branch_a_bwd(
        z, hl, kept, dya, dz, conv_w, rg_wx, rg_wa, rg_lambda, batch, seq)
    small = dict(b_merge=d_b_merge, conv_w=d_conv_w, conv_b=d_conv_b, rg_wx=d_wx, rg_bx=d_bx, rg_wa=d_wa,
                 rg_ba=d_ba, rg_lambda=d_lam, hg_lb_logits=d_lb_logits, hg_norm_g=d_hg_g,
                 norm_g=jnp.zeros((1, d), F32), final_norm_g=d_final_g)
    first, second = reduction
    d_w_in, landed_w_in, *scattered_first = _inproj_dw_exchange(h_t, dz, first((d_pa, d_pb, d_wo), small))
    grad_x, d_norm_g, *scattered_second = _inproj_dx(dz, w_all, x2d, dx2, norm_g, scatter=second(d_w_in, landed_w_in))
    return loss[0, 0], grad_x.reshape(batch, seq, d), d_norm_g, (scattered_first, scattered_second)


_SMALL_ORDER = ("b_merge", "conv_w", "conv_b", "rg_wx", "rg_bx", "rg_wa", "rg_ba", "rg_lambda", "hg_lb_logits",
                "hg_norm_g", "norm_g", "final_norm_g")
N_DEV = 8
PIECE_ROWS = 272


def _pack_small(tree):
    flat = jnp.concatenate([tree[k].reshape(-1) for k in _SMALL_ORDER])
    flat = jnp.pad(flat, (0, N_DEV * PIECE_ROWS * LANES - flat.shape[0]))
    return flat.reshape(N_DEV * PIECE_ROWS, LANES)


def _unpack_small(packed, like):
    flat = packed.reshape(-1)
    out, pos = {}, 0
    for k in _SMALL_ORDER:
        n = like[k].size
        out[k] = flat[pos:pos + n].reshape(like[k].shape)
        pos += n
    return out


def _mesh_position():
    x, y, c = lax.axis_index("x"), lax.axis_index("y"), lax.axis_index("c")
    other_chips = [(1 - x, y), (x, 1 - y), (1 - x, 1 - y)]
    return x, y, c, other_chips


def _other_devices(x, y, c):
    flips = [(fx, fy, fc) for fx in (0, 1) for fy in (0, 1) for fc in (0, 1) if (fx, fy, fc) != (0, 0, 0)]
    return [(jnp.where(fx, 1 - x, x), jnp.where(fy, 1 - y, y), jnp.where(fc, 1 - c, c)) for fx, fy, fc in flips]


def _remote(src, dst, send_sems, recv_sems, k, device):
    return pltpu.make_async_remote_copy(src_ref=src, dst_ref=dst, send_sem=send_sems.at[k], recv_sem=recv_sems.at[k],
                                        device_id=device, device_id_type=MESH)


def _exchange_halves(bigs, small):
    n_big = len(bigs)
    n_sem = n_big + N_DEV - 1

    def body(*refs):
        srcs, small_src = refs[:n_big], refs[n_big]
        outs, small_out = refs[n_big + 1:2 * n_big + 1], refs[2 * n_big + 1]
        send_sems, recv_sems, local_sem = refs[2 * n_big + 2:]
        x, y, c, _ = _mesh_position()
        me, sibling = 4 * x + 2 * y + c, (x, y, 1 - c)
        mine = pltpu.make_async_copy(small_src.at[pl.ds(me * PIECE_ROWS, PIECE_ROWS), :], small_out.at[me], local_sem)
        mine.start()
        copies = []
        for a in range(n_big):
            hs = srcs[a].shape[1] // 2
            copies.append(_remote(srcs[a].at[:, pl.ds((1 - c) * hs, hs), :], outs[a], send_sems, recv_sems, a, sibling))
        for k, (px, py, pc) in enumerate(_other_devices(x, y, c)):
            piece = small_src.at[pl.ds((4 * px + 2 * py + pc) * PIECE_ROWS, PIECE_ROWS), :]
            copies.append(_remote(piece, small_out.at[me], send_sems, recv_sems, n_big + k, (px, py, pc)))
        for cp in copies:
            cp.start()
        for cp in copies:
            cp.wait()
        mine.wait()

    hbm = pl.BlockSpec(memory_space=pl.ANY)
    out_shape = [jax.ShapeDtypeStruct((g.shape[0], g.shape[1] // 2, g.shape[2]), F32) for g in bigs]
    out_shape.append(jax.ShapeDtypeStruct((N_DEV, PIECE_ROWS, LANES), F32))
    return pl.pallas_call(
        body, name="exchange_halves",
        in_specs=[hbm] * (n_big + 1), out_specs=[hbm] * (n_big + 1), out_shape=out_shape,
        scratch_shapes=[pltpu.SemaphoreType.DMA((n_sem,)), pltpu.SemaphoreType.DMA((n_sem,)), pltpu.SemaphoreType.DMA],
    )(*bigs, small)


class _Scatter:
    def __init__(self, bigs, by_chip, small=None):
        self.bigs, self.by_chip, self.small = list(bigs), list(by_chip), small
        self.n_big = len(self.bigs)
        self.n_in = 2 * self.n_big + (small is not None)
        self.n_out = self.n_big + (small is not None)
        self.n_scratch = 2 + (small is not None)

    def plumbing(self, first_operand, first_output):
        hbm = pl.BlockSpec(memory_space=pl.ANY)
        n_sem = 3 * self.n_big + (N_DEV - 1 if self.small is not None else 0)
        operands = self.bigs + self.by_chip + ([self.small] if self.small is not None else [])
        out_shapes = [jax.ShapeDtypeStruct(g.shape, g.dtype) for g in self.by_chip]
        scratch = [pltpu.SemaphoreType.DMA((n_sem,)), pltpu.SemaphoreType.DMA((n_sem,))]
        if self.small is not None:
            out_shapes.append(jax.ShapeDtypeStruct((N_DEV, PIECE_ROWS, LANES), F32))
            scratch.append(pltpu.SemaphoreType.DMA)
        aliases = {first_operand + self.n_big + a: first_output + a for a in range(self.n_big)}
        return operands, [hbm] * self.n_in, [hbm] * self.n_out, out_shapes, scratch, aliases

    def copies(self, in_refs, out_refs, scratch_refs):
        srcs, outs = in_refs[:self.n_big], out_refs[:self.n_big]
        send_sems, recv_sems = scratch_refs[:2]
        x, y, c, chips = _mesh_position()
        chip, me = 2 * x + y, 4 * x + 2 * y + c
        copies = []
        for a in range(self.n_big):
            for j, (cx, cy) in enumerate(chips):
                copies.append(_remote(srcs[a].at[2 * cx + cy], outs[a].at[chip], send_sems, recv_sems, 3 * a + j,
                                      (cx, cy, c)))
        if self.small is not None:
            small_src, small_out = in_refs[2 * self.n_big], out_refs[self.n_big]
            copies.append(pltpu.make_async_copy(small_src, small_out.at[me], scratch_refs[2]))
            for k, peer in enumerate(_other_devices(x, y, c)):
                copies.append(_remote(small_src, small_out.at[me], send_sems, recv_sems, 3 * self.n_big + k, peer))
        return copies


def _swap_halves(halves, vec):
    n_big = len(halves)

    def body(*refs):
        srcs, vec_src = refs[:n_big], refs[n_big]
        outs, vec_out = refs[n_big + 1:2 * n_big + 1], refs[2 * n_big + 1]
        send_sems, recv_sems, local_sem = refs[2 * n_big + 2:]
        x, y, c, _ = _mesh_position()
        me = 4 * x + 2 * y + c
        copies = [pltpu.make_async_copy(vec_src, vec_out.at[me], local_sem)]
        copies += [_remote(srcs[a], outs[a], send_sems, recv_sems, a, (x, y, 1 - c)) for a in range(n_big)]
        copies += [_remote(vec_src, vec_out.at[me], send_sems, recv_sems, n_big + k, peer)
                   for k, peer in enumerate(_other_devices(x, y, c))]
        for cp in copies:
            cp.start()
        for cp in copies:
            cp.wait()

    hbm = pl.BlockSpec(memory_space=pl.ANY)
    n_sem = n_big + N_DEV - 1
    return pl.pallas_call(
        body, name="swap_halves",
        in_specs=[hbm] * (n_big + 1), out_specs=[hbm] * (n_big + 1),
        out_shape=[jax.ShapeDtypeStruct(h.shape, F32) for h in halves] + [jax.ShapeDtypeStruct((N_DEV,) + vec.shape, F32)],
        scratch_shapes=[pltpu.SemaphoreType.DMA((n_sem,)), pltpu.SemaphoreType.DMA((n_sem,)), pltpu.SemaphoreType.DMA],
    )(*halves, vec)


def kernel(x, w_in, b_merge, conv_w, conv_b, rg_wx, rg_bx, rg_wa, rg_ba, rg_lambda, hg_lb_logits, hg_norm_g, proj_a, proj_b, w_out, norm_g, final_norm_g, loss_target, m_w_in, m_b_merge, m_conv_w, m_conv_b, m_rg_wx, m_rg_bx, m_rg_wa, m_rg_ba, m_rg_lambda, m_hg_lb_logits, m_hg_norm_g, m_proj_a, m_proj_b, m_w_out, m_norm_g, m_final_norm_g, v_w_in, v_b_merge, v_conv_w, v_conv_b, v_rg_wx, v_rg_bx, v_rg_wa, v_rg_ba, v_rg_lambda, v_hg_lb_logits, v_hg_norm_g, v_proj_a, v_proj_b, v_w_out, v_norm_g, v_final_norm_g):
    d = D_MODEL
    weights = dict(w_in=w_in, b_merge=b_merge, conv_w=conv_w, conv_b=conv_b, rg_wx=rg_wx, rg_bx=rg_bx, rg_wa=rg_wa,
                   rg_ba=rg_ba, rg_lambda=rg_lambda, hg_lb_logits=hg_lb_logits, hg_norm_g=hg_norm_g, proj_a=proj_a,
                   proj_b=proj_b, w_out=w_out, norm_g=norm_g, final_norm_g=final_norm_g)
    m = dict(w_in=m_w_in, b_merge=m_b_merge, conv_w=m_conv_w, conv_b=m_conv_b, rg_wx=m_rg_wx, rg_bx=m_rg_bx,
             rg_wa=m_rg_wa, rg_ba=m_rg_ba, rg_lambda=m_rg_lambda, hg_lb_logits=m_hg_lb_logits, hg_norm_g=m_hg_norm_g,
             proj_a=m_proj_a, proj_b=m_proj_b, w_out=m_w_out, norm_g=m_norm_g, final_norm_g=m_final_norm_g)
    v = dict(w_in=v_w_in, b_merge=v_b_merge, conv_w=v_conv_w, conv_b=v_conv_b, rg_wx=v_rg_wx, rg_bx=v_rg_bx,
             rg_wa=v_rg_wa, rg_ba=v_rg_ba, rg_lambda=v_rg_lambda, hg_lb_logits=v_hg_lb_logits, hg_norm_g=v_hg_norm_g,
             proj_a=v_proj_a, proj_b=v_proj_b, w_out=v_w_out, norm_g=v_norm_g, final_norm_g=v_final_norm_g)
    big_names = ("w_in", "proj_a", "proj_b", "w_out")

    core = lax.axis_index("c").astype(jnp.int32).reshape(1)
    chip = (2 * lax.axis_index("x") + lax.axis_index("y")).astype(jnp.int32)

    slotted = [_cast_into_slot(weights[k][0], chip.reshape(1), _MXU_DTYPE, f"cast_{k}") for k in big_names]
    conv_slotted = _cast_into_slot(conv_w[0], chip.reshape(1), F32, "slot_conv_w")

    small_shapes = {}

    place = jnp.concatenate([core, chip.reshape(1)])

    def reduce_proj_and_small(proj_grads, small_grads):
        small_shapes.update({k: t.shape for k, t in small_grads.items()})
        bigs = [g.reshape(N_SHARDS, d // N_SHARDS, d) for g in proj_grads]
        *landed, small_landed = _exchange_halves(bigs, _pack_small(small_grads))
        sums = [_add_half(g, l, place, f"add_half_{1 + a}") for a, (g, l) in enumerate(zip(bigs, landed))]
        return _Scatter([s[0] for s in sums], [s[1] for s in sums], _sum_slots(small_landed, "sum_small"))

    def reduce_w_in(d_w_in, landed):
        partial, own_slot = _add_half(d_w_in, landed, place, "add_half_0")
        return _Scatter([partial], [own_slot])

    loss_part, grad_x, d_norm_g, ((*by_chip_proj, small_all), by_chip_w_in) = _local_step(
        x, loss_target, (slotted, conv_slotted, chip.reshape(1)), (reduce_proj_and_small, reduce_w_in),
        b_merge, conv_b, rg_wx[0], rg_bx.reshape(1, d), rg_wa[0], rg_ba.reshape(1, d), rg_lambda, hg_lb_logits,
        hg_norm_g, norm_g, final_norm_g.reshape(1, d))
    mine = [_sum_slots(s, f"sum_chips_{a}") for a, s in enumerate(by_chip_w_in + by_chip_proj)]
    late = jnp.concatenate([d_norm_g.reshape(SUBLANES, LANES), jnp.full((SUBLANES, LANES), loss_part, F32)])
    *theirs, late_parts = _swap_halves(mine, late)
    late_sum = _sum_slots(late_parts, "sum_late")
    loss = late_sum[SUBLANES, 0]
    small_red = _unpack_small(small_all, {k: jax.ShapeDtypeStruct(s, F32) for k, s in small_shapes.items()})
    small_red["norm_g"] = late_sum[:SUBLANES].reshape(1, d)

    grads, delta, new_m, new_v = {}, {}, {}, {}
    for k, g_mine, g_theirs in zip(big_names, mine, theirs):
        out = _adamw_halves(weights[k][0], g_mine, g_theirs, m[k][0], v[k][0], core, f"adamw_{k}")
        grads[k], delta[k], new_m[k], new_v[k] = (t.reshape(weights[k].shape) for t in out)
    cols = d // N_SHARDS
    g_conv = lax.dynamic_slice(small_red["conv_w"], (0, chip * cols), (CONV_WIDTH, cols))
    grads["conv_w"] = g_conv.reshape(conv_w.shape)
    dl, nm, nv = _adamw(conv_w[0], g_conv, m_conv_w[0], v_conv_w[0], "adamw_conv_w")
    delta["conv_w"], new_m["conv_w"], new_v["conv_w"] = (t.reshape(conv_w.shape) for t in (dl, nm, nv))
    rest = [k for k in _SMALL_ORDER if k != "conv_w"]
    like = {k: (weights[k] if k != "conv_w" else jnp.zeros((CONV_WIDTH, d), F32)) for k in _SMALL_ORDER}
    packs = [_pack_small({k: (t[k] if k != "conv_w" else like[k]) for k in _SMALL_ORDER}) for t in (weights, m, v)]
    g_pack = _pack_small({k: small_red[k].reshape(like[k].shape) for k in _SMALL_ORDER})
    outs = [_unpack_small(p, like) for p in _adamw(packs[0], g_pack, packs[1], packs[2], "adamw_small")]
    for k in rest:
        grads[k] = small_red[k].reshape(weights[k].shape)
        delta[k], new_m[k], new_v[k] = outs[0][k], outs[1][k], outs[2][k]

    order = ("w_in", "b_merge", "conv_w", "conv_b", "rg_wx", "rg_bx", "rg_wa", "rg_ba", "rg_lambda", "hg_lb_logits",
             "hg_norm_g", "proj_a", "proj_b", "w_out", "norm_g", "final_norm_g")
    return (loss, grad_x, *[grads[k] for k in order], *[delta[k] for k in order], *[new_m[k] for k in order],
            *[new_v[k] for k in order])
```

```python
import functools

import jax
import jax.numpy as jnp
from jax import lax
from jax.experimental import pallas as pl
from jax.experimental.pallas import tpu as pltpu

F32 = jnp.float32
_MXU_DTYPE = jnp.bfloat16

D_MODEL = 1024
LANES = 128
SUBLANES = 8
N_BLK = D_MODEL // LANES
N_GROUPS = 8
N_SHARDS = 4
CONV_WIDTH = 4
LRU_C = 8.0
CHUNK = 64
CHUNKS_IN_FLIGHT = 16
HG_SCALE = float(LANES) ** -0.5
EPS = 1e-6
ADAM_LR, ADAM_B1, ADAM_B2, ADAM_EPS, ADAM_WD, ADAM_STEP = 0.001, 0.9, 0.999, 1e-08, 0.01, 10
MATMUL_TOKENS = 512
TAIL_TOKENS = 256
CONTRACT_TOKENS = 2048
VMEM_LIMIT = 56 * 1024 * 1024
VMEM_LIMIT_BIG = 60 * 1024 * 1024
MESH = pl.DeviceIdType.MESH

_SLOT_TO_GROUP = (2, 3, 4, 5, 0, 1, 6, 7)


def _mm(a, b):
    return lax.dot_general(a.astype(_MXU_DTYPE), b.astype(_MXU_DTYPE), (((1,), (0,)), ((), ())),
                           preferred_element_type=F32)


def _mm_nt(a, b):
    return lax.dot_general(a.astype(_MXU_DTYPE), b.astype(_MXU_DTYPE), (((1,), (1,)), ((), ())),
                           preferred_element_type=F32)


def _mm_tn(a, b):
    return lax.dot_general(a.astype(_MXU_DTYPE), b.astype(_MXU_DTYPE), (((0,), (0,)), ((), ())),
                           preferred_element_type=F32)


def _sigmoid(x):
    return 0.5 * jnp.tanh(0.5 * x) + 0.5


def _log1p_pos(y):
    series = y * (1.0 - y * (0.5 - y * (1.0 / 3.0 - y * 0.25)))
    return jnp.where(y < 0.01, series, jnp.log(1.0 + y))


def _softplus(x):
    return jnp.maximum(x, 0.0) + _log1p_pos(jnp.exp(-jnp.abs(x)))


def _shift_down(x, n):
    rolled = pltpu.roll(x, n, 0)
    edge = SUBLANES if (n < SUBLANES and x.shape[0] > SUBLANES) else x.shape[0]
    rows = lax.broadcasted_iota(jnp.int32, (edge, x.shape[1]), 0)
    head = jnp.where(rows >= n, rolled[:edge], 0.0)
    return head if edge == x.shape[0] else jnp.concatenate([head, rolled[edge:]], axis=0)


def _shift_up(x, n):
    size = x.shape[0]
    rolled = pltpu.roll(x, size - n, 0)
    edge = SUBLANES if (n < SUBLANES and size > SUBLANES) else size
    rows = lax.broadcasted_iota(jnp.int32, (edge, x.shape[1]), 0)
    tail = jnp.where(rows < edge - n, rolled[size - edge:], 0.0)
    return tail if edge == size else jnp.concatenate([rolled[:size - edge], tail], axis=0)


def _params(dims, vmem=VMEM_LIMIT):
    return pltpu.CompilerParams(dimension_semantics=dims, vmem_limit_bytes=vmem)


def _slot_of_group(g):
    return jnp.where(g < 2, g + 4, jnp.where(g < 6, g - 2, g))


def _inproj_fwd_gather(x2d, norm_g, slotted, conv_slotted, chip):
    tokens, d = x2d.shape
    tm = min(MATMUL_TOKENS, tokens)
    n_tiles = tokens // tm
    n_big = len(slotted)
    n_sem = 6 * (n_big + 1) + 3
    last_pass = N_GROUPS - 1

    def shard_of(k, chip_id):
        x, y = chip_id // 2, chip_id % 2
        return 2 * jnp.where(k % 2 == 1, 1 - x, x) + jnp.where(k // 2 == 1, 1 - y, y)

    def body(chip_ref, x_ref, g_ref, *rest):
        bufs, cw = rest[n_big + 1:2 * n_big + 1], rest[2 * n_big + 1]
        z_ref, ht_ref = rest[2 * n_big + 2:2 * n_big + 4]
        h_all, slab, send_sems, recv_sems, slab_sems = rest[2 * n_big + 4:]
        del chip_ref
        p, i = pl.program_id(0), pl.program_id(1)
        x, y, c, chips = _mesh_position()
        me, sibling = 2 * x + y, (x, y, 1 - c)

        pieces = [(0, 0), (0, 1)] + [(a, None) for a in range(1, n_big)]

        def half(piece, slot, which):
            a, q = pieces[piece]
            hs = bufs[a].shape[1] // 2
            cols = slice(None) if q is None else pl.ds(q * D_MODEL, D_MODEL)
            return bufs[a].at[slot, pl.ds(which * hs, hs), cols]

        def send(piece, j):
            mine = half(piece, me, c)
            return _remote(mine, mine, send_sems, recv_sems, 6 * piece + j, (chips[j][0], chips[j][1], c))

        def arrival(piece, j):
            landed = half(piece, 2 * chips[j][0] + chips[j][1], c)
            return _remote(landed, landed, send_sems, recv_sems, 6 * piece + j, (chips[j][0], chips[j][1], c))

        def passed_on(piece, j, which):
            landed = half(piece, 2 * chips[j][0] + chips[j][1], which)
            return _remote(landed, landed, send_sems, recv_sems, 6 * piece + 3 + j, sibling)

        def conv_copy(j, slot):
            return _remote(cw.at[slot], cw.at[slot], send_sems, recv_sems, 6 * len(pieces) + j,
                           (chips[j][0], chips[j][1], c))

        def land(piece, j):
            arrival(piece, j).wait_recv()
            passed_on(piece, j, c).start()
            passed_on(piece, j, 1 - c).wait_recv()

        def slab_copy(pv):
            src = bufs[0].at[shard_of(pv // 2, me), :, pl.ds((pv % 2) * D_MODEL, D_MODEL)]
            return pltpu.make_async_copy(src, slab.at[pv % 2], slab_sems.at[pv % 2])

        @pl.when((p == 0) & (i == 0))
        def _():
            for q in range(2):
                send(q, 0).start()
                send(q, 1).start()
            slab_copy(0).start()

        @pl.when(i == 0)
        def _():
            for pv in range(N_GROUPS):
                @pl.when(p == pv)
                def _(pv=pv):
                    slab_copy(pv).wait()

        rows = pl.ds(pl.multiple_of(i * tm, tm), tm)

        @pl.when(p == 0)
        def _():
            xt = x_ref[...]
            r = lax.rsqrt(jnp.mean(xt * xt, axis=-1, keepdims=True) + EPS)
            h = (xt * r) * g_ref[...]
            h_all[rows, :] = h.astype(_MXU_DTYPE)
            ht_ref[...] = jnp.transpose(h).astype(_MXU_DTYPE)

        z_ref[...] = _mm(h_all[rows, :], slab[p % 2])

        def relay(q):
            landed = half(q, 2 * chips[q][0] + chips[q][1], c)
            to = chips[1 - q]
            return _remote(landed, landed, send_sems, recv_sems, 6 * q + 2, (to[0], to[1], c))

        landings = {1: [(0, 0)], 2: [(1, 0), (1, 1)], 3: [(0, 1)], 5: [(0, 2)], 6: [(1, 2)]}

        def end_of_pass(pv):
            for q, j in landings.get(pv - 1, []):
                land(q, j)
                if j == q:
                    relay(q).start()
            if pv - 1 == 1:
                for piece in range(2, len(pieces)):
                    for jj in range(3):
                        send(piece, jj).start()
                for jj in range(3):
                    conv_copy(jj, me).start()
            if pv - 1 in (5, 6):
                for piece in range(2, len(pieces)):
                    for jj in ((0, 1) if pv - 1 == 5 else (2,)):
                        land(piece, jj)
            slab_copy(pv).start()

        @pl.when(i == n_tiles - 1)
        def _():
            for pv in range(1, N_GROUPS):
                pl.when(p == pv - 1)(functools.partial(end_of_pass, pv))

        @pl.when((p == last_pass) & (i == n_tiles - 1))
        def _():
            for j in range(3):
                conv_copy(j, 2 * chips[j][0] + chips[j][1]).wait_recv()
            for piece in range(len(pieces)):
                for j in range(3):
                    (relay(piece) if (piece < 2 and j == 2) else send(piece, j)).wait_send()
                    passed_on(piece, j, c).wait_send()
            for j in range(3):
                conv_copy(j, me).wait_send()

    def z_index(p, i, chip_ref):
        g = 2 * shard_of(p // 2, chip_ref[0]) + p % 2
        return (_slot_of_group(g), i, 0)

    def first_pass_tile(p, i, chip_ref):
        return jnp.where(p == 0, i, n_tiles - 1)

    hbm = pl.BlockSpec(memory_space=pl.ANY)
    operands = list(slotted) + [conv_slotted]
    grid_spec = pltpu.PrefetchScalarGridSpec(
        num_scalar_prefetch=1, grid=(N_GROUPS, n_tiles),
        in_specs=[pl.BlockSpec((tm, d), lambda p, i, chip_ref: (first_pass_tile(p, i, chip_ref), 0)),
                  pl.BlockSpec((1, d), lambda p, i, chip_ref: (0, 0))] + [hbm] * (n_big + 1),
        out_specs=[hbm] * (n_big + 1) + [pl.BlockSpec((None, tm, D_MODEL), z_index),
                                         pl.BlockSpec((d, tm), lambda p, i, chip_ref: (0, first_pass_tile(p, i, chip_ref)))],
        scratch_shapes=[pltpu.VMEM((tokens, d), _MXU_DTYPE), pltpu.VMEM((2, d, D_MODEL), _MXU_DTYPE),
                        pltpu.SemaphoreType.DMA((n_sem,)), pltpu.SemaphoreType.DMA((n_sem,)),
                        pltpu.SemaphoreType.DMA((2,))])
    out = pl.pallas_call(
        body, name="inproj_fwd_gather", grid_spec=grid_spec,
        out_shape=[jax.ShapeDtypeStruct(a.shape, a.dtype) for a in operands]
        + [jax.ShapeDtypeStruct((N_GROUPS, tokens, D_MODEL), F32), jax.ShapeDtypeStruct((d, tokens), _MXU_DTYPE)],
        input_output_aliases={3 + a: a for a in range(n_big + 1)},
        compiler_params=_params(("arbitrary", "arbitrary")),
    )(chip, x2d, norm_g, *operands)
    return out[n_big + 1], out[n_big + 2], out[:n_big], out[n_big]


def _lane_blocks(x):
    return [x[:, k * LANES:(k + 1) * LANES] for k in range(x.shape[1] // LANES)]


def _block_diag(x, w_ref, transposed=False):
    mm = _mm_nt if transposed else _mm
    return jnp.concatenate([mm(xk, w_ref[k]) for k, xk in enumerate(_lane_blocks(x))], axis=1)


def _lru_decay(gr, sp):
    log_a = (-LRU_C) * gr * sp
    a = jnp.exp(log_a)
    y = 2.0 * log_a
    mult_sq = jnp.where(y > -1e-3, -y * (1.0 + 0.5 * y), 1.0 - a * a)
    inv_mult = lax.rsqrt(jnp.maximum(mult_sq, 1e-37))
    return a, mult_sq * inv_mult, inv_mult


def _tile_rows(width):
    return lax.broadcasted_iota(jnp.int32, (SUBLANES, width), 0)


def _scan_forward(a_scr, u_scr, h_scr, seq):
    width = a_scr.shape[1]
    rows = _tile_rows(width)

    def tile(j, carry):
        sl = pl.ds(pl.multiple_of(j * SUBLANES, SUBLANES), SUBLANES)
        a = a_scr[sl, :]
        u = u_scr[sl, :]
        for d in (1, 2, 4):
            keep = rows >= d
            a_sh = jnp.where(keep, pltpu.roll(a, d, 0), 1.0)
            u_sh = jnp.where(keep, pltpu.roll(u, d, 0), 0.0)
            u = a * u_sh + u
            a = a * a_sh
        h = u + a * carry
        h_scr[sl, :] = h
        return jnp.broadcast_to(h[SUBLANES - 1:SUBLANES, :], (SUBLANES, width))

    lax.fori_loop(0, seq // SUBLANES, tile, jnp.zeros((SUBLANES, width), F32))


def _scan_backward(c_scr, d_scr, g_scr, seq):
    width = c_scr.shape[1]
    rows = _tile_rows(width)
    n_tiles = seq // SUBLANES

    def tile(jj, carry):
        j = n_tiles - 1 - jj
        sl = pl.ds(pl.multiple_of(j * SUBLANES, SUBLANES), SUBLANES)
        c = c_scr[sl, :]
        g = d_scr[sl, :]
        for d in (1, 2, 4):
            keep = rows < SUBLANES - d
            c_sh = jnp.where(keep, pltpu.roll(c, SUBLANES - d, 0), 1.0)
            g_sh = jnp.where(keep, pltpu.roll(g, SUBLANES - d, 0), 0.0)
            g = c * g_sh + g
            c = c * c_sh
        g = g + c * carry
        g_scr[sl, :] = g
        return jnp.broadcast_to(g[0:1, :], (SUBLANES, width))

    lax.fori_loop(0, n_tiles, tile, jnp.zeros((SUBLANES, width), F32))


LRU_BLOCKS_PER_STEP = 2
LRU_LANES = LRU_BLOCKS_PER_STEP * LANES
LRU_STEPS = N_BLK // LRU_BLOCKS_PER_STEP


def _branch_a_fwd(z, conv_w, conv_b, wx, bx, wa, ba, lam, batch, seq):
    tokens = batch * seq

    def body(z_ref, cw_ref, cb_ref, wx_ref, bx_ref, wa_ref, ba_ref, lam_ref, ya_ref, hl_ref, kept_ref, a_scr, u_scr):
        xa = z_ref[0]
        ga = z_ref[1]
        xc = (cb_ref[...] + cw_ref[3:4, :] * xa + cw_ref[2:3, :] * _shift_down(xa, 1)
              + cw_ref[1:2, :] * _shift_down(xa, 2) + cw_ref[0:1, :] * _shift_down(xa, 3))
        gi = _sigmoid(_block_diag(xc, wx_ref) + bx_ref[...])
        gr = _sigmoid(_block_diag(xc, wa_ref) + ba_ref[...])
        a, mult, _ = _lru_decay(gr, _softplus(-lam_ref[...]))
        kept_ref[0], kept_ref[1], kept_ref[2] = xc, gi, gr
        a_scr[...] = a
        u_scr[...] = mult * gi * xc
        _scan_forward(a_scr, u_scr, hl_ref, seq)
        ya_ref[...] = (hl_ref[...] * (ga * _sigmoid(ga))).astype(_MXU_DTYPE)

    blk = pl.BlockSpec((seq, LRU_LANES), lambda b, c: (b, c))
    vec = pl.BlockSpec((1, LRU_LANES), lambda b, c: (0, c))
    mat = pl.BlockSpec((LRU_BLOCKS_PER_STEP, LANES, LANES), lambda b, c: (c, 0, 0))
    return pl.pallas_call(
        body, name="branch_a_fwd",
        grid=(batch, LRU_STEPS),
        in_specs=[pl.BlockSpec((2, seq, LRU_LANES), lambda b, c: (2, b, c)),
                  pl.BlockSpec((CONV_WIDTH, LRU_LANES), lambda b, c: (0, c)), vec, mat, vec, mat, vec, vec],
        out_specs=[blk, blk, pl.BlockSpec((3, seq, LRU_LANES), lambda b, c: (0, b, c))],
        out_shape=[jax.ShapeDtypeStruct((tokens, D_MODEL), _MXU_DTYPE), jax.ShapeDtypeStruct((tokens, D_MODEL), F32),
                   jax.ShapeDtypeStruct((3, tokens, D_MODEL), F32)],
        scratch_shapes=[pltpu.VMEM((seq, LRU_LANES), F32), pltpu.VMEM((seq, LRU_LANES), F32)],
        compiler_params=_params(("parallel", "parallel")),
    )(z, conv_w, conv_b, wx, bx, wa, ba, lam)


def _branch_a_bwd(z, hl, kept, dya, dz, conv_w, wx, wa, lam, batch, seq):
    def body(z_ref, hl_ref, kept_ref, dya_ref, dz_in_ref, cw_ref, wx_ref, wa_ref, lam_ref,
             dz_ref, dcw_ref, dcb_ref, dwx_ref, dbx_ref, dwa_ref, dba_ref, dlam_ref, c_scr, d_scr):
        del dz_in_ref
        g_scr = d_scr
        xa = z_ref[0]
        ga = z_ref[1]
        hl = hl_ref[...]
        dya = dya_ref[...]
        xc, gi, gr = kept_ref[0], kept_ref[1], kept_ref[2]
        sp = _softplus(-lam_ref[...])
        a, mult, inv_mult = _lru_decay(gr, sp)
        sga = _sigmoid(ga)
        dz_ref[1] = (dya * hl * (sga * (1.0 + ga * (1.0 - sga)))).astype(_MXU_DTYPE)
        c_scr[...] = _shift_up(a, 1)
        d_scr[...] = dya * (ga * sga)
        _scan_backward(c_scr, d_scr, g_scr, seq)
        g = g_scr[...]
        da = g * _shift_down(hl, 1)
        dmult = g * gi * xc
        dgi = g * mult * xc
        dxc = g * mult * gi
        dlog_a = da * a - dmult * (a * a) * inv_mult
        dgr = dlog_a * (-LRU_C) * sp
        dsp = jnp.sum(dlog_a * gr, axis=0, keepdims=True) * (-LRU_C)
        dlam = -dsp * _sigmoid(-lam_ref[...])
        dpi = dgi * gi * (1.0 - gi)
        dpr = dgr * gr * (1.0 - gr)
        dxc = dxc + _block_diag(dpi, wx_ref, transposed=True) + _block_diag(dpr, wa_ref, transposed=True)
        dwx = jnp.stack([_mm_tn(xk, dk) for xk, dk in zip(_lane_blocks(xc), _lane_blocks(dpi))])
        dwa = jnp.stack([_mm_tn(xk, dk) for xk, dk in zip(_lane_blocks(xc), _lane_blocks(dpr))])
        dbx = jnp.sum(dpi, axis=0, keepdims=True)
        dba = jnp.sum(dpr, axis=0, keepdims=True)
        ahead = [dxc if k == CONV_WIDTH - 1 else _shift_up(dxc, CONV_WIDTH - 1 - k) for k in range(CONV_WIDTH)]
        dxa = sum(cw_ref[k:k + 1, :] * ahead[k] for k in range(CONV_WIDTH))
        dz_ref[0] = dxa.astype(_MXU_DTYPE)
        dcb = jnp.sum(dxc, axis=0, keepdims=True)
        dcw = [jnp.sum(ahead[k] * xa, axis=0, keepdims=True) for k in range(CONV_WIDTH)]

        @pl.when(pl.program_id(1) == 0)
        def _():
            for k in range(CONV_WIDTH):
                dcw_ref[k:k + 1, :] = dcw[k]
            dcb_ref[...] = dcb
            dwx_ref[...] = dwx
            dbx_ref[...] = dbx
            dwa_ref[...] = dwa
            dba_ref[...] = dba
            dlam_ref[...] = dlam

        @pl.when(pl.program_id(1) != 0)
        def _():
            for k in range(CONV_WIDTH):
                dcw_ref[k:k + 1, :] += dcw[k]
            dcb_ref[...] += dcb
            dwx_ref[...] += dwx
            dbx_ref[...] += dbx
            dwa_ref[...] += dwa
            dba_ref[...] += dba
            dlam_ref[...] += dlam

    tokens = batch * seq
    blk = pl.BlockSpec((seq, LRU_LANES), lambda c, b: (b, c))
    vec = pl.BlockSpec((1, LRU_LANES), lambda c, b: (0, c))
    mat = pl.BlockSpec((LRU_BLOCKS_PER_STEP, LANES, LANES), lambda c, b: (c, 0, 0))
    vec_shape = jax.ShapeDtypeStruct((1, D_MODEL), F32)
    mat_shape = jax.ShapeDtypeStruct((N_BLK, LANES, LANES), F32)
    return pl.pallas_call(
        body, name="branch_a_bwd",
        grid=(LRU_STEPS, batch),
        in_specs=[pl.BlockSpec((2, seq, LRU_LANES), lambda c, b: (2, b, c)), blk,
                  pl.BlockSpec((3, seq, LRU_LANES), lambda c, b: (0, b, c)), blk,
                  pl.BlockSpec(memory_space=pl.ANY),
                  pl.BlockSpec((CONV_WIDTH, LRU_LANES), lambda c, b: (0, c)), mat, mat, vec],
        out_specs=[pl.BlockSpec((2, seq, LRU_LANES), lambda c, b: (2, b, c)),
                   pl.BlockSpec((CONV_WIDTH, LRU_LANES), lambda c, b: (0, c)), vec, mat, vec, mat, vec, vec],
        out_shape=[jax.ShapeDtypeStruct((N_GROUPS, tokens, D_MODEL), _MXU_DTYPE),
                   jax.ShapeDtypeStruct((CONV_WIDTH, D_MODEL), F32), vec_shape, mat_shape, vec_shape, mat_shape,
                   vec_shape, vec_shape],
        scratch_shapes=[pltpu.VMEM((seq, LRU_LANES), F32)] * 2,
        input_output_aliases={4: 0},
        compiler_params=_params(("parallel", "arbitrary"), vmem=VMEM_LIMIT_BIG),
    )(z, hl, kept, dya, dz, conv_w, wx, wa, lam)


def _chunk_masks(transposed=False):
    r = lax.broadcasted_iota(jnp.int32, (CHUNK, CHUNK), 0)
    c = lax.broadcasted_iota(jnp.int32, (CHUNK, CHUNK), 1)
    return r <= c if transposed else r >= c


def _row_blocks(seq, fn):
    block = min(256, seq)

    def trip(i, carry):
        fn(pl.ds(pl.multiple_of(i * block, block), block))
        return carry

    lax.fori_loop(0, seq // block, trip, 0)


def _hgrn_prepare(z_ref, lb_ref, f_scr, logf_scr, qh_scr, seq):
    lb = _sigmoid(lb_ref[0:1, :] - lb_ref[1:2, :])

    def block(rows):
        q = z_ref[0, rows, :]
        f = lb + (1.0 - lb) * _sigmoid(z_ref[1, rows, :])
        f_scr[rows, :] = f
        logf_scr[rows, :] = jnp.log(f)
        qh_scr[rows, :] = q * _sigmoid(q)

    _row_blocks(seq, block)
    return lb


def _cumsum_rows(x, reverse=False):
    shift = _shift_up if reverse else _shift_down
    d = 1
    while d < x.shape[0]:
        x = x + shift(x, d)
        d *= 2
    return x


def _lane_mean(x):
    return jnp.mean(x, axis=-1, keepdims=True)


def _token_contractions(lhs_scr, rhs_scr, out_ref, seq):
    rows_id = lax.broadcasted_iota(jnp.int32, (LANES, LANES), 0)

    def transposed(p):
        rows = pl.ds(pl.multiple_of(p * LANES, LANES), LANES)
        return jnp.transpose(lhs_scr[rows, :]).astype(_MXU_DTYPE), rhs_scr[rows, :]

    def contract(p, s):
        lhs_t, rhs = s
        return (_mm(lhs_t, jnp.where(rows_id < CHUNK, rhs, 0.0)), _mm(lhs_t, jnp.where(rows_id >= CHUNK, rhs, 0.0)))

    def store(p, out):
        out_ref[2 * p] = out[0]
        out_ref[2 * p + 1] = out[1]

    _independent_trips(seq // LANES, [transposed, contract], store)


def _chunk_rows(c):
    return pl.ds(pl.multiple_of(c * CHUNK, CHUNK), CHUNK)


def _chunk_terms(c, z_ref, f_scr, qh_scr, b_scr):
    rows = _chunk_rows(c)
    b = b_scr[rows, :]
    b_mid = b_scr[pl.ds(c * CHUNK + CHUNK // 2, 1), :]
    b_last = b_scr[pl.ds(c * CHUNK + CHUNK - 1, 1), :]
    qh = qh_scr[rows, :]
    k = 1.0 - f_scr[rows, :]
    v = z_ref[2, rows, :]
    e_q = jnp.exp(b - b_mid) * HG_SCALE
    e_k = jnp.exp(b_mid - b)
    e_qi = jnp.exp(b) * HG_SCALE
    e_ks = jnp.exp(b_last - b)
    decay = jnp.exp(b_last)
    return rows, qh, k, v, e_q, e_k, e_qi, e_ks, decay


def _independent_trips(n, stages, store, group=CHUNKS_IN_FLIGHT):
    stages = stages if isinstance(stages, (list, tuple)) else [stages]
    group = min(group, n)

    def trip(g, carry):
        ids = [g * group + i for i in range(group)]
        state = [stages[0](c) for c in ids]
        for stage in stages[1:]:
            state = [stage(c, s) for c, s in zip(ids, state)]
        for c, s in zip(ids, state):
            store(c, s)
        return carry

    lax.fori_loop(0, n // group, trip, 0)


def _branch_b_fwd(z, lb_logits, hg_g, batch, seq):
    tokens = batch * seq
    n_chunks = seq // CHUNK

    def body(z_ref, lb_ref, g_ref, yb_ref, st_ref, kept_ref, logf_scr, o_scr, qi_scr, ks_scr, dec_scr):
        f_scr, qh_scr, b_scr, o_kept = (kept_ref.at[k] for k in range(4))
        _hgrn_prepare(z_ref, lb_ref, f_scr, logf_scr, qh_scr, seq)
        causal = _chunk_masks()
        gain = g_ref[...]

        def cumulate(c):
            return _cumsum_rows(logf_scr[_chunk_rows(c), :])

        def store_cumulated(c, b):
            b_scr[_chunk_rows(c), :] = b

        def scores(c):
            _, qh, k, v, e_q, e_k, e_qi, e_ks, decay = _chunk_terms(c, z_ref, f_scr, qh_scr, b_scr)
            return _mm_nt(qh * e_q, k * e_k), v, qh * e_qi, k * e_ks, decay

        def within_chunk(c, s):
            att, v, q_int, k_st, decay = s
            return _mm(jnp.where(causal, att, 0.0), v), q_int, k_st, decay

        def store_within_chunk(c, out):
            rows = _chunk_rows(c)
            o_scr[rows, :], qi_scr[rows, :], ks_scr[rows, :], dec_scr[pl.ds(c, 1), :] = out

        def carry_state(c, state_t):
            update = st_ref[c]
            st_ref[c] = state_t
            return state_t * dec_scr[pl.ds(c, 1), :] + update

        def finish(c):
            rows = _chunk_rows(c)
            o = o_scr[rows, :] + _mm_nt(qi_scr[rows, :], st_ref[c])
            r = lax.rsqrt(_lane_mean(o * o) + EPS)
            gb = z_ref[3, rows, :]
            return (((o * r) * gain) * (gb * _sigmoid(gb))).astype(_MXU_DTYPE), o

        def store_finished(c, out):
            yb_ref[_chunk_rows(c), :], o_kept[_chunk_rows(c), :] = out

        _independent_trips(n_chunks, cumulate, store_cumulated)
        _independent_trips(n_chunks, [scores, within_chunk], store_within_chunk)
        _token_contractions(z_ref.at[2], ks_scr, st_ref, seq)
        lax.fori_loop(0, n_chunks, carry_state, jnp.zeros((LANES, LANES), F32))
        _independent_trips(n_chunks, finish, store_finished)

    seq_buf = pltpu.VMEM((seq, LANES), F32)
    return pl.pallas_call(
        body, name="branch_b_fwd",
        grid=(batch, N_BLK),
        in_specs=[pl.BlockSpec((4, seq, LANES), lambda b, h: (0, b, h)),
                  pl.BlockSpec((2, LANES), lambda b, h: (0, h)),
                  pl.BlockSpec((1, LANES), lambda b, h: (0, 0))],
        out_specs=[pl.BlockSpec((seq, LANES), lambda b, h: (b, h)),
                   pl.BlockSpec((None, n_chunks, LANES, LANES), lambda b, h: (b * N_BLK + h, 0, 0, 0)),
                   pl.BlockSpec((4, seq, LANES), lambda b, h: (0, b, h))],
        out_shape=[jax.ShapeDtypeStruct((tokens, D_MODEL), _MXU_DTYPE),
                   jax.ShapeDtypeStruct((batch * N_BLK, n_chunks, LANES, LANES), F32),
                   jax.ShapeDtypeStruct((4, tokens, D_MODEL), F32)],
        scratch_shapes=[seq_buf] * 4 + [pltpu.VMEM((n_chunks, LANES), F32)],
        compiler_params=_params(("parallel", "parallel")),
    )(z, lb_logits, hg_g)


def _branch_b_bwd(z, states, kept, dyb, dz, lb_logits, hg_g, batch, seq):
    n_chunks = seq // CHUNK

    def body(z_ref, st_ref, kept_ref, dyb_ref, dz_in_ref, lb_ref, g_ref, dz_ref, dlog_ref, dg_ref,
             do_scr, qi_scr, dqh_scr, df_scr, dec_scr, dgp_scr, dlb_scr, dst_scr):
        del dz_in_ref
        f_scr, qh_scr, b_scr, o_kept = (kept_ref.at[k] for k in range(4))
        first = (pl.program_id(0) == 0) & (pl.program_id(1) == 0)
        lb = _sigmoid(lb_ref[0:1, :] - lb_ref[1:2, :])
        causal = _chunk_masks()
        anti_causal = _chunk_masks(transposed=True)
        gain = g_ref[...]

        @pl.when(first)
        def _():
            dg_ref[...] = jnp.zeros_like(dg_ref)

        @pl.when(pl.program_id(1) == 0)
        def _():
            dlb_scr[...] = jnp.zeros_like(dlb_scr)

        def output_gradient(c):
            rows = _chunk_rows(c)
            b = b_scr[rows, :]
            q_int = qh_scr[rows, :] * (jnp.exp(b) * HG_SCALE)
            decay = jnp.exp(b_scr[pl.ds(c * CHUNK + CHUNK - 1, 1), :])
            o = o_kept[rows, :]
            r = lax.rsqrt(_lane_mean(o * o) + EPS)
            o_n = o * r
            gb = z_ref[3, rows, :]
            sgb = _sigmoid(gb)
            dyb_c = dyb_ref[rows, :]
            d_ong = dyb_c * (gb * sgb)
            d_gb = (dyb_c * (o_n * gain) * (sgb * (1.0 + gb * (1.0 - sgb)))).astype(_MXU_DTYPE)
            d_gain = jnp.sum(d_ong * o_n, axis=0, keepdims=True)
            d_on = d_ong * gain
            return d_gb, d_gain, r * (d_on - o_n * _lane_mean(d_on * o_n)), q_int, decay

        def store_output_gradient(c, out):
            rows = _chunk_rows(c)
            dz_ref[3, rows, :], dgp_scr[pl.ds(c, 1), :], do_scr[rows, :], qi_scr[rows, :], dec_scr[pl.ds(c, 1), :] = out

        def carry_state_gradient(cc, d_state_t):
            c = n_chunks - 1 - cc
            update = dst_scr[c]
            dst_scr[c] = d_state_t
            return d_state_t * dec_scr[pl.ds(c, 1), :] + update

        def score_gradients(c):
            rows, qh, k, v, e_q, e_k, e_qi, e_ks, decay = _chunk_terms(c, z_ref, f_scr, qh_scr, b_scr)
            state_t = st_ref[c]
            d_state_t = dst_scr[c]
            d_o = do_scr[rows, :]
            q_in, k_in, q_int, k_st = qh * e_q, k * e_k, qh * e_qi, k * e_ks
            first = (_mm_nt(k_in, q_in), _mm_nt(d_o, v), _mm_nt(v, d_o), _mm_nt(k_st, d_state_t), _mm(d_o, state_t),
                     _mm(v, d_state_t))
            d_decay = jnp.sum(state_t * d_state_t, axis=0, keepdims=True)
            return first, d_o, q_in, k_in, q_int, k_st, e_q, e_k, e_qi, e_ks, decay, d_decay

        def input_gradients(c, s):
            (att_t, d_att, d_att_t, dv_inter, dq_int, dk_st), d_o, q_in, k_in, q_int, k_st, e_q, e_k, e_qi, e_ks, decay, d_decay = s
            rows = _chunk_rows(c)
            d_v = _mm(jnp.where(anti_causal, att_t, 0.0), d_o) + dv_inter
            dq_in = _mm(jnp.where(causal, d_att, 0.0), k_in)
            dk_in = _mm(jnp.where(anti_causal, d_att_t, 0.0), q_in)
            d_k = dk_in * e_k + dk_st * e_ks
            kk = dk_st * k_st
            d_b = dq_in * q_in + dq_int * q_int - dk_in * k_in - kk
            d_b_last = jnp.sum(kk, axis=0, keepdims=True) + decay * d_decay
            d_logf = _cumsum_rows(d_b, reverse=True) + d_b_last
            return d_v.astype(_MXU_DTYPE), dq_in * e_q + dq_int * e_qi, d_logf / f_scr[rows, :] - d_k

        def store_input_gradients(c, out):
            rows = _chunk_rows(c)
            dz_ref[2, rows, :], dqh_scr[rows, :], df_scr[rows, :] = out

        def input_activations(rows):
            q = z_ref[0, rows, :]
            sq = _sigmoid(q)
            dz_ref[0, rows, :] = (dqh_scr[rows, :] * (sq * (1.0 + q * (1.0 - sq)))).astype(_MXU_DTYPE)
            sg = _sigmoid(z_ref[1, rows, :])
            d_f = df_scr[rows, :]
            dz_ref[1, rows, :] = (d_f * (1.0 - lb) * sg * (1.0 - sg)).astype(_MXU_DTYPE)
            dlb_scr[...] += jnp.sum(d_f * (1.0 - sg), axis=0, keepdims=True)

        _independent_trips(n_chunks, output_gradient, store_output_gradient)
        _token_contractions(do_scr, qi_scr, dst_scr, seq)
        lax.fori_loop(0, n_chunks, carry_state_gradient, jnp.zeros((LANES, LANES), F32))
        _independent_trips(n_chunks, [score_gradients, input_gradients], store_input_gradients)
        dg_ref[...] += jnp.sum(dgp_scr[...], axis=0, keepdims=True)
        _row_blocks(seq, input_activations)
        d_l0 = dlb_scr[...] * lb * (1.0 - lb)
        dlog_ref[0:1, :] = d_l0
        dlog_ref[1:2, :] = -d_l0

    tokens = batch * seq
    seq_buf = pltpu.VMEM((seq, LANES), F32)
    chunk_rows = pltpu.VMEM((n_chunks, LANES), F32)
    return pl.pallas_call(
        body, name="branch_b_bwd",
        grid=(N_BLK, batch),
        in_specs=[pl.BlockSpec((4, seq, LANES), lambda h, b: (0, b, h)),
                  pl.BlockSpec((None, n_chunks, LANES, LANES), lambda h, b: (b * N_BLK + h, 0, 0, 0)),
                  pl.BlockSpec((4, seq, LANES), lambda h, b: (0, b, h)),
                  pl.BlockSpec((seq, LANES), lambda h, b: (b, h)),
                  pl.BlockSpec(memory_space=pl.ANY),
                  pl.BlockSpec((2, LANES), lambda h, b: (0, h)),
                  pl.BlockSpec((1, LANES), lambda h, b: (0, 0))],
        out_specs=[pl.BlockSpec((4, seq, LANES), lambda h, b: (0, b, h)),
                   pl.BlockSpec((2, LANES), lambda h, b: (0, h)),
                   pl.BlockSpec((1, LANES), lambda h, b: (0, 0))],
        out_shape=[jax.ShapeDtypeStruct((N_GROUPS, tokens, D_MODEL), _MXU_DTYPE),
                   jax.ShapeDtypeStruct((2, D_MODEL), F32),
                   jax.ShapeDtypeStruct((1, LANES), F32)],
        scratch_shapes=[seq_buf] * 4 + [chunk_rows, chunk_rows, pltpu.VMEM((1, LANES), F32),
                                        pltpu.VMEM((n_chunks, LANES, LANES), F32)],
        input_output_aliases={4: 0},
        compiler_params=_params(("arbitrary", "arbitrary")),
    )(z, states, kept, dyb, dz, lb_logits, hg_g)


def _merge_tail(ya, yb, z, x2d, tgt2d, b_merge, final_g, pa, pb, wo):
    tokens, d = x2d.shape
    tm = min(TAIL_TOKENS, tokens)
    n_tiles = tokens // tm

    def body(ya_ref, yb_ref, z_ref, x_ref, t_ref, bm_ref, fg_ref, pa_hbm, pb_hbm, wo_hbm,
             dya_ref, dyb_ref, dx2_ref, dz_ref, loss_ref, dfg_ref, dbm_ref, dpa_hbm, dpb_hbm, dwo_hbm,
             pa_s, pb_s, wo_s, dpa_s, dpb_s, dwo_s, sems):
        i = pl.program_id(0)

        def together(pairs):
            copies = [pltpu.make_async_copy(src, dst, sems.at[k]) for k, (src, dst) in enumerate(pairs)]
            for cp in copies:
                cp.start()
            for cp in copies:
                cp.wait()

        @pl.when(i == 0)
        def _():
            together([(pa_hbm, pa_s), (pb_hbm, pb_s), (wo_hbm, wo_s)])
            dpa_s[...] = jnp.zeros_like(dpa_s)
            dpb_s[...] = jnp.zeros_like(dpb_s)
            dwo_s[...] = jnp.zeros_like(dwo_s)
            loss_ref[...] = jnp.zeros_like(loss_ref)
            dfg_ref[...] = jnp.zeros_like(dfg_ref)
            dbm_ref[...] = jnp.zeros_like(dbm_ref)

        ya_t = ya_ref[...]
        yb_t = yb_ref[...]
        out_a = _mm(ya_t, pa_s[...])
        out_b = _mm(yb_t, pb_s[...])
        g_a = _sigmoid(z_ref[0] + bm_ref[:, :d])
        g_b = _sigmoid(z_ref[1] + bm_ref[:, d:])
        mixed = g_a * out_a + g_b * out_b
        x2 = x_ref[...] + _mm(mixed, wo_s[...])
        r = lax.rsqrt(jnp.mean(x2 * x2, axis=-1, keepdims=True) + EPS)
        xn = x2 * r
        fg = fg_ref[...]
        diff = xn * fg - t_ref[...]
        loss_ref[...] += jnp.sum(diff * diff) * (0.5 / d)
        dy = diff * (1.0 / d)
        dfg_ref[...] += jnp.sum(dy * xn, axis=0, keepdims=True)
        dxn = dy * fg
        dx2 = r * (dxn - xn * jnp.mean(dxn * xn, axis=-1, keepdims=True))
        dx2_ref[...] = dx2
        dmixed = _mm_nt(dx2, wo_s[...])
        dwo_s[...] += _mm_tn(mixed, dx2)
        dgm_a = dmixed * out_a * g_a * (1.0 - g_a)
        dgm_b = dmixed * out_b * g_b * (1.0 - g_b)
        dz_ref[0] = dgm_a.astype(_MXU_DTYPE)
        dz_ref[1] = dgm_b.astype(_MXU_DTYPE)
        dbm_ref[:, :d] += jnp.sum(dgm_a, axis=0, keepdims=True)
        dbm_ref[:, d:] += jnp.sum(dgm_b, axis=0, keepdims=True)
        dout_a = dmixed * g_a
        dout_b = dmixed * g_b
        dpa_s[...] += _mm_tn(ya_t, dout_a)
        dpb_s[...] += _mm_tn(yb_t, dout_b)
        dya_ref[...] = _mm_nt(dout_a, pa_s[...])
        dyb_ref[...] = _mm_nt(dout_b, pb_s[...])

        @pl.when(i == n_tiles - 1)
        def _():
            together([(dpa_s, dpa_hbm), (dpb_s, dpb_hbm), (dwo_s, dwo_hbm)])

    tile = pl.BlockSpec((tm, d), lambda i: (i, 0))
    gm = pl.BlockSpec((2, tm, d), lambda i: (3, i, 0))
    row = lambda n: pl.BlockSpec((1, n), lambda i: (0, 0))
    hbm = pl.BlockSpec(memory_space=pl.ANY)
    act = jax.ShapeDtypeStruct((tokens, d), F32)
    mat = jax.ShapeDtypeStruct((d, d), F32)
    return pl.pallas_call(
        body, name="merge_tail",
        grid=(n_tiles,),
        in_specs=[tile, tile, gm, tile, tile, row(2 * d), row(d), hbm, hbm, hbm],
        out_specs=[tile, tile, tile, gm, row(LANES), row(d), row(2 * d), hbm, hbm, hbm],
        out_shape=[act, act, act, jax.ShapeDtypeStruct((N_GROUPS, tokens, d), _MXU_DTYPE),
                   jax.ShapeDtypeStruct((1, LANES), F32), jax.ShapeDtypeStruct((1, d), F32),
                   jax.ShapeDtypeStruct((1, 2 * d), F32), mat, mat, mat],
        scratch_shapes=[pltpu.VMEM((d, d), _MXU_DTYPE)] * 3 + [pltpu.VMEM((d, d), F32)] * 3
        + [pltpu.SemaphoreType.DMA((3,))],
        compiler_params=_params(("arbitrary",)),
    )(ya, yb, z, x2d, tgt2d, b_merge, final_g, pa, pb, wo)


def _inproj_dw_exchange(h_t, dz, scatter):
    d, tokens = h_t.shape
    tm = min(CONTRACT_TOKENS, tokens)
    n_i = tokens // tm
    half = d // 2

    def body(h_ref, dz_ref, *rest):
        n_in, n_out = scatter.n_in, scatter.n_out
        dw_hbm, land_hbm = rest[n_in:n_in + 2]
        acc, local_sems, send_sems, recv_sems = rest[n_in + 2 + n_out:n_in + 6 + n_out]
        carried = scatter.copies(rest[:n_in], rest[n_in + 2:n_in + 2 + n_out], rest[n_in + 6 + n_out:])
        s, i = pl.program_id(0), pl.program_id(1)
        x, y, c, _ = _mesh_position()

        @pl.when((s == 0) & (i == 0))
        def _():
            for cp in carried:
                cp.start()

        part = _mm(h_ref[...], dz_ref[...])
        buf = acc.at[s % 2]

        @pl.when(i == 0)
        def _():
            buf[...] = part

        @pl.when(i != 0)
        def _():
            buf[...] += part

        def copies(k):
            g = _SLOT_TO_GROUP[k]
            cols = pl.ds((g % 2) * D_MODEL, D_MODEL)
            src = acc.at[k % 2]
            mine = pltpu.make_async_copy(src, dw_hbm.at[g // 2, :, cols], local_sems.at[k % 2])
            theirs = _remote(src.at[pl.ds((1 - c) * half, half), :], land_hbm.at[g // 2, :, cols],
                             send_sems, recv_sems, k, (x, y, 1 - c))
            return mine, theirs

        for k in range(N_GROUPS):
            @pl.when((s == k) & (i == n_i - 1))
            def _(k=k):
                if k > 0:
                    mine, theirs = copies(k - 1)
                    mine.wait()
                    theirs.wait_send()
                mine, theirs = copies(k)
                mine.start()
                theirs.start()
                if k == N_GROUPS - 1:
                    mine.wait()
                    theirs.wait_send()
                    for kk in range(N_GROUPS):
                        copies(kk)[1].wait_recv()
                    for cp in carried:
                        cp.wait()

    hbm = pl.BlockSpec(memory_space=pl.ANY)
    more = scatter.plumbing(first_operand=2, first_output=2)
    return pl.pallas_call(
        body, name="inproj_dw_exchange",
        grid=(N_GROUPS, n_i),
        in_specs=[pl.BlockSpec((d, tm), lambda s, i: (0, i)),
                  pl.BlockSpec((None, tm, D_MODEL), lambda s, i: (s, i, 0))] + more[1],
        out_specs=[hbm, hbm] + more[2],
        out_shape=[jax.ShapeDtypeStruct((N_SHARDS, d, 2 * D_MODEL), F32),
                   jax.ShapeDtypeStruct((N_SHARDS, half, 2 * D_MODEL), F32)] + more[3],
        scratch_shapes=[pltpu.VMEM((2, d, D_MODEL), F32), pltpu.SemaphoreType.DMA((2,)),
                        pltpu.SemaphoreType.DMA((N_GROUPS,)), pltpu.SemaphoreType.DMA((N_GROUPS,))] + more[4],
        input_output_aliases=more[5],
        compiler_params=_params(("arbitrary", "arbitrary")),
    )(h_t, dz, *more[0])


def _inproj_dx(dz, w_all, x2d, dx2, norm_g, scatter):
    tokens, d = x2d.shape
    tm = min(TAIL_TOKENS, tokens)
    n_tiles = tokens // tm

    def body(dz_ref, w_hbm, x_ref, dx2_ref, g_ref, *rest):
        n_in, n_out = scatter.n_in, scatter.n_out
        dx_ref, dg_ref = rest[n_in:n_in + 2]
        w_res, load_sems = rest[n_in + 2 + n_out:n_in + 4 + n_out]
        copies = scatter.copies(rest[:n_in], rest[n_in + 2:n_in + 2 + n_out], rest[n_in + 4 + n_out:])
        i = pl.program_id(0)

        @pl.when(i == 0)
        def _():
            for cp in copies:
                cp.start()
            loads = [pltpu.make_async_copy(w_hbm.at[g // 2, :, pl.ds((g % 2) * D_MODEL, D_MODEL)],
                                           w_res.at[:, pl.ds(slot * D_MODEL, D_MODEL)], load_sems.at[slot])
                     for slot, g in enumerate(_SLOT_TO_GROUP)]
            for cp in loads:
                cp.start()
            for cp in loads:
                cp.wait()
            dg_ref[...] = jnp.zeros_like(dg_ref)

        dz_all = jnp.concatenate([dz_ref[s] for s in range(N_GROUPS)], axis=1)
        dh = jnp.transpose(_mm_nt(w_res[...], dz_all))
        x = x_ref[...]
        r = lax.rsqrt(jnp.mean(x * x, axis=-1, keepdims=True) + EPS)
        xn = x * r
        dg_ref[...] += jnp.sum(dh * xn, axis=0, keepdims=True)
        dxn = dh * g_ref[...]
        dx_ref[...] = r * (dxn - xn * jnp.mean(dxn * xn, axis=-1, keepdims=True)) + dx2_ref[...]

        @pl.when(i == n_tiles - 1)
        def _():
            for cp in copies:
                cp.wait()

    tile = pl.BlockSpec((tm, d), lambda i: (i, 0))
    hbm = pl.BlockSpec(memory_space=pl.ANY)
    more = scatter.plumbing(first_operand=5, first_output=2)
    return pl.pallas_call(
        body, name="inproj_dx", grid=(n_tiles,),
        in_specs=[pl.BlockSpec((N_GROUPS, tm, D_MODEL), lambda i: (0, i, 0)), hbm, tile, tile,
                  pl.BlockSpec((1, d), lambda i: (0, 0))] + more[1],
        out_specs=[tile, pl.BlockSpec((1, d), lambda i: (0, 0))] + more[2],
        out_shape=[jax.ShapeDtypeStruct((tokens, d), F32), jax.ShapeDtypeStruct((1, d), F32)] + more[3],
        scratch_shapes=[pltpu.VMEM((d, N_GROUPS * D_MODEL), _MXU_DTYPE), pltpu.SemaphoreType.DMA((N_GROUPS,))] + more[4],
        input_output_aliases=more[5],
        compiler_params=_params(("arbitrary",)),
    )(dz, w_all, x2d, dx2, norm_g, *more[0])


def _row_tile(rows, cols, itemsize=4, budget=2 * 1024 * 1024):
    tr = rows
    while tr * cols * itemsize > budget and tr % 16 == 0:
        tr //= 2
    return tr


def _cast_into_slot(a, chip, dtype, name):
    rows, cols = a.shape
    tr = _row_tile(rows, cols)

    def body(chip_ref, a_ref, o_ref):
        del chip_ref
        o_ref[...] = a_ref[...].astype(dtype)

    grid_spec = pltpu.PrefetchScalarGridSpec(
        num_scalar_prefetch=1, grid=(rows // tr,),
        in_specs=[pl.BlockSpec((tr, cols), lambda i, chip_ref: (i, 0))],
        out_specs=pl.BlockSpec((None, tr, cols), lambda i, chip_ref: (chip_ref[0], i, 0)))
    return pl.pallas_call(body, name=name, grid_spec=grid_spec,
                          out_shape=jax.ShapeDtypeStruct((N_SHARDS, rows, cols), dtype),
                          compiler_params=_params(("arbitrary",)))(chip, a)


def _sum_slots(stack, name):
    n, rows, cols = stack.shape
    tr = _row_tile(rows, cols * n)

    def body(s_ref, o_ref):
        total = s_ref[0].astype(F32)
        for k in range(1, n):
            total = total + s_ref[k].astype(F32)
        o_ref[...] = total

    return pl.pallas_call(body, name=name, grid=(rows // tr,),
                          in_specs=[pl.BlockSpec((n, tr, cols), lambda i: (0, i, 0))],
                          out_specs=pl.BlockSpec((tr, cols), lambda i: (i, 0)),
                          out_shape=jax.ShapeDtypeStruct((rows, cols), F32),
                          compiler_params=_params(("parallel",)))(stack)


def _add_half(full, landed, place, name):
    n, rows, cols = full.shape
    half = rows // 2
    tr = _row_tile(half, cols)
    nb = half // tr

    def body(place_ref, a_ref, b_ref, o_ref, own_ref):
        total = (a_ref[...] + b_ref[...]).astype(_MXU_DTYPE)
        o_ref[...] = total

        @pl.when(pl.program_id(1) == place_ref[1])
        def _():
            own_ref[...] = total

    grid_spec = pltpu.PrefetchScalarGridSpec(
        num_scalar_prefetch=1, grid=(nb, n),
        in_specs=[pl.BlockSpec((None, tr, cols), lambda i, j, place_ref: (j, place_ref[0] * nb + i, 0)),
                  pl.BlockSpec((None, tr, cols), lambda i, j, place_ref: (j, i, 0))],
        out_specs=[pl.BlockSpec((None, tr, cols), lambda i, j, place_ref: (j, i, 0)),
                   pl.BlockSpec((None, tr, cols), lambda i, j, place_ref: (place_ref[1], i, 0))])
    shape = jax.ShapeDtypeStruct((n, half, cols), _MXU_DTYPE)
    return pl.pallas_call(body, name=name, grid_spec=grid_spec, out_shape=[shape, shape],
                          compiler_params=_params(("parallel", "arbitrary")))(place, full, landed)


def _adamw_update(w, grad, m, v):
    c1 = 1.0 - ADAM_B1 ** ADAM_STEP
    c2 = 1.0 - ADAM_B2 ** ADAM_STEP
    nm = ADAM_B1 * m + (1.0 - ADAM_B1) * grad
    nv = ADAM_B2 * v + (1.0 - ADAM_B2) * (grad * grad)
    return (-ADAM_LR) * ((nm / c1) / (jnp.sqrt(nv / c2) + ADAM_EPS) + ADAM_WD * w), nm, nv


def _adamw(w, g, m, v, name):
    rows, cols = w.shape
    tr = _row_tile(rows, cols, budget=1024 * 1024)

    def body(w_ref, g_ref, m_ref, v_ref, d_ref, nm_ref, nv_ref):
        d_ref[...], nm_ref[...], nv_ref[...] = _adamw_update(w_ref[...], g_ref[...], m_ref[...], v_ref[...])

    spec = pl.BlockSpec((tr, cols), lambda i: (i, 0))
    shape = jax.ShapeDtypeStruct((rows, cols), F32)
    return pl.pallas_call(body, name=name, grid=(rows // tr,), in_specs=[spec] * 4, out_specs=[spec] * 3,
                          out_shape=[shape] * 3, compiler_params=_params(("parallel",)))(w, g, m, v)


def _adamw_halves(w, g_mine, g_sibling, m, v, core, name):
    rows, cols = w.shape
    half = rows // 2
    tr = _row_tile(half, cols, budget=1024 * 1024)
    nb = half // tr

    def body(core_ref, w_ref, gm_ref, gs_ref, m_ref, v_ref, g_ref, d_ref, nm_ref, nv_ref):
        mine = pl.program_id(0) // nb == core_ref[0]
        grad = jnp.where(mine, gm_ref[...], gs_ref[...])
        g_ref[...] = grad
        d_ref[...], nm_ref[...], nv_ref[...] = _adamw_update(w_ref[...], grad, m_ref[...], v_ref[...])

    spec = pl.BlockSpec((tr, cols), lambda i, core_ref: (i, 0))
    mine_spec = pl.BlockSpec((tr, cols), lambda i, core_ref: (jnp.where(i // nb == core_ref[0], i % nb, 0), 0))
    sibling_spec = pl.BlockSpec((tr, cols), lambda i, core_ref: (jnp.where(i // nb == core_ref[0], 0, i % nb), 0))
    grid_spec = pltpu.PrefetchScalarGridSpec(num_scalar_prefetch=1, grid=(rows // tr,),
                                             in_specs=[spec, mine_spec, sibling_spec, spec, spec], out_specs=[spec] * 4)
    shape = jax.ShapeDtypeStruct((rows, cols), F32)
    return pl.pallas_call(body, name=name, grid_spec=grid_spec, out_shape=[shape] * 4,
                          compiler_params=_params(("parallel",)))(core, w, g_mine, g_sibling, m, v)


def _local_step(x, loss_target, gather, reduction, b_merge, conv_b, rg_wx, rg_bx, rg_wa, rg_ba, rg_lambda,
                hg_lb_logits, hg_norm_g, norm_g, final_norm_g):
    batch, seq, d = x.shape
    x2d = x.reshape(batch * seq, d)
    tgt2d = loss_target.reshape(batch * seq, d)
    z, h_t, (w_all, pa, pb, wo), cw_all = _inproj_fwd_gather(x2d, norm_g, *gather)
    pa, pb, wo = (t.reshape(d, d) for t in (pa, pb, wo))
    conv_w = jnp.transpose(cw_all, (1, 0, 2)).reshape(CONV_WIDTH, d)
    lru = (conv_w, conv_b, rg_wx, rg_bx, rg_wa, rg_ba, rg_lambda)
    ya, hl, kept = _branch_a_fwd(z, *lru, batch, seq)
    yb, states, kept_b = _branch_b_fwd(z, hg_lb_logits, hg_norm_g, batch, seq)
    dya, dyb, dx2, dz, loss, d_final_g, d_b_merge, d_pa, d_pb, d_wo = _merge_tail(
        ya, yb, z, x2d, tgt2d, b_merge, final_norm_g, pa, pb, wo)
    dz, d_lb_logits, d_hg_g = _branch_b_bwd(z, states, kept_b, dyb, dz, hg_lb_logits, hg_norm_g, batch, seq)
    dz, d_conv_w, d_conv_b, d_wx, d_bx, d_wa, d_ba, d_lam = _branch_a_bwd(
        z, hl, kept, dya, dz, conv_w, rg_wx, rg_wa, rg_lambda, batch, seq)
    small = dict(b_merge=d_b_merge, conv_w=d_conv_w, conv_b=d_conv_b, rg_wx=d_wx, rg_bx=d_bx, rg_wa=d_wa,
                 rg_ba=d_ba, rg_lambda=d_lam, hg_lb_logits=d_lb_logits, hg_norm_g=d_hg_g,
                 norm_g=jnp.zeros((1, d), F32), final_norm_g=d_final_g)
    first, second = reduction
    d_w_in, landed_w_in, *scattered_first = _inproj_dw_exchange(h_t, dz, first((d_pa, d_pb, d_wo), small))
    grad_x, d_norm_g, *scattered_second = _inproj_dx(dz, w_all, x2d, dx2, norm_g, scatter=second(d_w_in, landed_w_in))
    return loss[0, 0], grad_x.reshape(batch, seq, d), d_norm_g, (scattered_first, scattered_second)


_SMALL_ORDER = ("b_merge", "conv_w", "conv_b", "rg_wx", "rg_bx", "rg_wa", "rg_ba", "rg_lambda", "hg_lb_logits",
                "hg_norm_g", "norm_g", "final_norm_g")
N_DEV = 8
PIECE_ROWS = 272


def _pack_small(tree):
    flat = jnp.concatenate([tree[k].reshape(-1) for k in _SMALL_ORDER])
    flat = jnp.pad(flat, (0, N_DEV * PIECE_ROWS * LANES - flat.shape[0]))
    return flat.reshape(N_DEV * PIECE_ROWS, LANES)


def _unpack_small(packed, like):
    flat = packed.reshape(-1)
    out, pos = {}, 0
    for k in _SMALL_ORDER:
        n = like[k].size
        out[k] = flat[pos:pos + n].reshape(like[k].shape)
        pos += n
    return out


def _mesh_position():
    x, y, c = lax.axis_index("x"), lax.axis_index("y"), lax.axis_index("c")
    other_chips = [(1 - x, y), (x, 1 - y), (1 - x, 1 - y)]
    return x, y, c, other_chips


def _other_devices(x, y, c):
    flips = [(fx, fy, fc) for fx in (0, 1) for fy in (0, 1) for fc in (0, 1) if (fx, fy, fc) != (0, 0, 0)]
    return [(jnp.where(fx, 1 - x, x), jnp.where(fy, 1 - y, y), jnp.where(fc, 1 - c, c)) for fx, fy, fc in flips]


def _remote(src, dst, send_sems, recv_sems, k, device):
    return pltpu.make_async_remote_copy(src_ref=src, dst_ref=dst, send_sem=send_sems.at[k], recv_sem=recv_sems.at[k],
                                        device_id=device, device_id_type=MESH)


def _exchange_halves(bigs, small):
    n_big = len(bigs)
    n_sem = n_big + N_DEV - 1

    def body(*refs):
        srcs, small_src = refs[:n_big], refs[n_big]
        outs, small_out = refs[n_big + 1:2 * n_big + 1], refs[2 * n_big + 1]
        send_sems, recv_sems, local_sem = refs[2 * n_big + 2:]
        x, y, c, _ = _mesh_position()
        me, sibling = 4 * x + 2 * y + c, (x, y, 1 - c)
        mine = pltpu.make_async_copy(small_src.at[pl.ds(me * PIECE_ROWS, PIECE_ROWS), :], small_out.at[me], local_sem)
        mine.start()
        copies = []
        for a in range(n_big):
            hs = srcs[a].shape[1] // 2
            copies.append(_remote(srcs[a].at[:, pl.ds((1 - c) * hs, hs), :], outs[a], send_sems, recv_sems, a, sibling))
        for k, (px, py, pc) in enumerate(_other_devices(x, y, c)):
            piece = small_src.at[pl.ds((4 * px + 2 * py + pc) * PIECE_ROWS, PIECE_ROWS), :]
            copies.append(_remote(piece, small_out.at[me], send_sems, recv_sems, n_big + k, (px, py, pc)))
        for cp in copies:
            cp.start()
        for cp in copies:
            cp.wait()
        mine.wait()

    hbm = pl.BlockSpec(memory_space=pl.ANY)
    out_shape = [jax.ShapeDtypeStruct((g.shape[0], g.shape[1] // 2, g.shape[2]), F32) for g in bigs]
    out_shape.append(jax.ShapeDtypeStruct((N_DEV, PIECE_ROWS, LANES), F32))
    return pl.pallas_call(
        body, name="exchange_halves",
        in_specs=[hbm] * (n_big + 1), out_specs=[hbm] * (n_big + 1), out_shape=out_shape,
        scratch_shapes=[pltpu.SemaphoreType.DMA((n_sem,)), pltpu.SemaphoreType.DMA((n_sem,)), pltpu.SemaphoreType.DMA],
    )(*bigs, small)


class _Scatter:
    def __init__(self, bigs, by_chip, small=None):
        self.bigs, self.by_chip, self.small = list(bigs), list(by_chip), small
        self.n_big = len(self.bigs)
        self.n_in = 2 * self.n_big + (small is not None)
        self.n_out = self.n_big + (small is not None)
        self.n_scratch = 2 + (small is not None)

    def plumbing(self, first_operand, first_output):
        hbm = pl.BlockSpec(memory_space=pl.ANY)
        n_sem = 3 * self.n_big + (N_DEV - 1 if self.small is not None else 0)
        operands = self.bigs + self.by_chip + ([self.small] if self.small is not None else [])
        out_shapes = [jax.ShapeDtypeStruct(g.shape, g.dtype) for g in self.by_chip]
        scratch = [pltpu.SemaphoreType.DMA((n_sem,)), pltpu.SemaphoreType.DMA((n_sem,))]
        if self.small is not None:
            out_shapes.append(jax.ShapeDtypeStruct((N_DEV, PIECE_ROWS, LANES), F32))
            scratch.append(pltpu.SemaphoreType.DMA)
        aliases = {first_operand + self.n_big + a: first_output + a for a in range(self.n_big)}
        return operands, [hbm] * self.n_in, [hbm] * self.n_out, out_shapes, scratch, aliases

    def copies(self, in_refs, out_refs, scratch_refs):
        srcs, outs = in_refs[:self.n_big], out_refs[:self.n_big]
        send_sems, recv_sems = scratch_refs[:2]
        x, y, c, chips = _mesh_position()
        chip, me = 2 * x + y, 4 * x + 2 * y + c
        copies = []
        for a in range(self.n_big):
            for j, (cx, cy) in enumerate(chips):
                copies.append(_remote(srcs[a].at[2 * cx + cy], outs[a].at[chip], send_sems, recv_sems, 3 * a + j,
                                      (cx, cy, c)))
        if self.small is not None:
            small_src, small_out = in_refs[2 * self.n_big], out_refs[self.n_big]
            copies.append(pltpu.make_async_copy(small_src, small_out.at[me], scratch_refs[2]))
            for k, peer in enumerate(_other_devices(x, y, c)):
                copies.append(_remote(small_src, small_out.at[me], send_sems, recv_sems, 3 * self.n_big + k, peer))
        return copies


def _swap_halves(halves, vec):
    n_big = len(halves)

    def body(*refs):
        srcs, vec_src = refs[:n_big], refs[n_big]
        outs, vec_out = refs[n_big + 1:2 * n_big + 1], refs[2 * n_big + 1]
        send_sems, recv_sems, local_sem = refs[2 * n_big + 2:]
        x, y, c, _ = _mesh_position()
        me = 4 * x + 2 * y + c
        copies = [pltpu.make_async_copy(vec_src, vec_out.at[me], local_sem)]
        copies += [_remote(srcs[a], outs[a], send_sems, recv_sems, a, (x, y, 1 - c)) for a in range(n_big)]
        copies += [_remote(vec_src, vec_out.at[me], send_sems, recv_sems, n_big + k, peer)
                   for k, peer in enumerate(_other_devices(x, y, c))]
        for cp in copies:
            cp.start()
        for cp in copies:
            cp.wait()

    hbm = pl.BlockSpec(memory_space=pl.ANY)
    n_sem = n_big + N_DEV - 1
    return pl.pallas_call(
        body, name="swap_halves",
        in_specs=[hbm] * (n_big + 1), out_specs=[hbm] * (n_big + 1),
        out_shape=[jax.ShapeDtypeStruct(h.shape, F32) for h in halves] + [jax.ShapeDtypeStruct((N_DEV,) + vec.shape, F32)],
        scratch_shapes=[pltpu.SemaphoreType.DMA((n_sem,)), pltpu.SemaphoreType.DMA((n_sem,)), pltpu.SemaphoreType.DMA],
    )(*halves, vec)


def kernel(x, w_in, b_merge, conv_w, conv_b, rg_wx, rg_bx, rg_wa, rg_ba, rg_lambda, hg_lb_logits, hg_norm_g, proj_a, proj_b, w_out, norm_g, final_norm_g, loss_target, m_w_in, m_b_merge, m_conv_w, m_conv_b, m_rg_wx, m_rg_bx, m_rg_wa, m_rg_ba, m_rg_lambda, m_hg_lb_logits, m_hg_norm_g, m_proj_a, m_proj_b, m_w_out, m_norm_g, m_final_norm_g, v_w_in, v_b_merge, v_conv_w, v_conv_b, v_rg_wx, v_rg_bx, v_rg_wa, v_rg_ba, v_rg_lambda, v_hg_lb_logits, v_hg_norm_g, v_proj_a, v_proj_b, v_w_out, v_norm_g, v_final_norm_g):
    d = D_MODEL
    weights = dict(w_in=w_in, b_merge=b_merge, conv_w=conv_w, conv_b=conv_b, rg_wx=rg_wx, rg_bx=rg_bx, rg_wa=rg_wa,
                   rg_ba=rg_ba, rg_lambda=rg_lambda, hg_lb_logits=hg_lb_logits, hg_norm_g=hg_norm_g, proj_a=proj_a,
                   proj_b=proj_b, w_out=w_out, norm_g=norm_g, final_norm_g=final_norm_g)
    m = dict(w_in=m_w_in, b_merge=m_b_merge, conv_w=m_conv_w, conv_b=m_conv_b, rg_wx=m_rg_wx, rg_bx=m_rg_bx,
             rg_wa=m_rg_wa, rg_ba=m_rg_ba, rg_lambda=m_rg_lambda, hg_lb_logits=m_hg_lb_logits, hg_norm_g=m_hg_norm_g,
             proj_a=m_proj_a, proj_b=m_proj_b, w_out=m_w_out, norm_g=m_norm_g, final_norm_g=m_final_norm_g)
    v = dict(w_in=v_w_in, b_merge=v_b_merge, conv_w=v_conv_w, conv_b=v_conv_b, rg_wx=v_rg_wx, rg_bx=v_rg_bx,
             rg_wa=v_rg_wa, rg_ba=v_rg_ba, rg_lambda=v_rg_lambda, hg_lb_logits=v_hg_lb_logits, hg_norm_g=v_hg_norm_g,
             proj_a=v_proj_a, proj_b=v_proj_b, w_out=v_w_out, norm_g=v_norm_g, final_norm_g=v_final_norm_g)
    big_names = ("w_in", "proj_a", "proj_b", "w_out")

    core = lax.axis_index("c").astype(jnp.int32).reshape(1)
    chip = (2 * lax.axis_index("x") + lax.axis_index("y")).astype(jnp.int32)

    slotted = [_cast_into_slot(weights[k][0], chip.reshape(1), _MXU_DTYPE, f"cast_{k}") for k in big_names]
    conv_slotted = _cast_into_slot(conv_w[0], chip.reshape(1), F32, "slot_conv_w")

    small_shapes = {}

    place = jnp.concatenate([core, chip.reshape(1)])

    def reduce_proj_and_small(proj_grads, small_grads):
        small_shapes.update({k: t.shape for k, t in small_grads.items()})
        bigs = [g.reshape(N_SHARDS, d // N_SHARDS, d) for g in proj_grads]
        *landed, small_landed = _exchange_halves(bigs, _pack_small(small_grads))
        sums = [_add_half(g, l, place, f"add_half_{1 + a}") for a, (g, l) in enumerate(zip(bigs, landed))]
        return _Scatter([s[0] for s in sums], [s[1] for s in sums], _sum_slots(small_landed, "sum_small"))

    def reduce_w_in(d_w_in, landed):
        partial, own_slot = _add_half(d_w_in, landed, place, "add_half_0")
        return _Scatter([partial], [own_slot])

    loss_part, grad_x, d_norm_g, ((*by_chip_proj, small_all), by_chip_w_in) = _local_step(
        x, loss_target, (slotted, conv_slotted, chip.reshape(1)), (reduce_proj_and_small, reduce_w_in),
        b_merge, conv_b, rg_wx[0], rg_bx.reshape(1, d), rg_wa[0], rg_ba.reshape(1, d), rg_lambda, hg_lb_logits,
        hg_norm_g, norm_g, final_norm_g.reshape(1, d))
    mine = [_sum_slots(s, f"sum_chips_{a}") for a, s in enumerate(by_chip_w_in + by_chip_proj)]
    late = jnp.concatenate([d_norm_g.reshape(SUBLANES, LANES), jnp.full((SUBLANES, LANES), loss_part, F32)])
    *theirs, late_parts = _swap_halves(mine, late)
    late_sum = _sum_slots(late_parts, "sum_late")
    loss = late_sum[SUBLANES, 0]
    small_red = _unpack_small(small_all, {k: jax.ShapeDtypeStruct(s, F32) for k, s in small_shapes.items()})
    small_red["norm_g"] = late_sum[:SUBLANES].reshape(1, d)

    grads, delta, new_m, new_v = {}, {}, {}, {}
    for k, g_mine, g_theirs in zip(big_names, mine, theirs):
        out = _adamw_halves(weights[k][0], g_mine, g_theirs, m[k][0], v[k][0], core, f"adamw_{k}")
        grads[k], delta[k], new_m[k], new_v[k] = (t.reshape(weights[k].shape) for t in out)
    cols = d // N_SHARDS
    g_conv = lax.dynamic_slice(small_red["conv_w"], (0, chip * cols), (CONV_WIDTH, cols))
    grads["conv_w"] = g_conv.reshape(conv_w.shape)
    dl, nm, nv = _adamw(conv_w[0], g_conv, m_conv_w[0], v_conv_w[0], "adamw_conv_w")
    delta["conv_w"], new_m["conv_w"], new_v["conv_w"] = (t.reshape(conv_w.shape) for t in (dl, nm, nv))
    rest = [k for k in _SMALL_ORDER if k != "conv_w"]
    like = {k: (weights[k] if k != "conv_w" else jnp.zeros((CONV_WIDTH, d), F32)) for k in _SMALL_ORDER}
    packs = [_pack_small({k: (t[k] if k != "conv_w" else like[k]) for k in _SMALL_ORDER}) for t in (weights, m, v)]
    g_pack = _pack_small({k: small_red[k].reshape(like[k].shape) for k in _SMALL_ORDER})
    outs = [_unpack_small(p, like) for p in _adamw(packs[0], g_pack, packs[1], packs[2], "adamw_small")]
    for k in rest:
        grads[k] = small_red[k].reshape(weights[k].shape)
        delta[k], new_m[k], new_v[k] = outs[0][k], outs[1][k], outs[2][k]

    order = ("w_in", "b_merge", "conv_w", "conv_b", "rg_wx", "rg_bx", "rg_wa", "rg_ba", "rg_lambda", "hg_lb_logits",
             "hg_norm_g", "proj_a", "proj_b", "w_out", "norm_g", "final_norm_g")
    return (loss, grad_x, *[grads[k] for k in order], *[delta[k] for k in order], *[new_m[k] for k in order],
            *[new_v[k] for k in order])
```

```python
import functools

import jax
import jax.numpy as jnp
from jax import lax
from jax.experimental import pallas as pl
from jax.experimental.pallas import tpu as pltpu

F32 = jnp.float32
_MXU_DTYPE = jnp.bfloat16

D_MODEL = 1024
LANES = 128
SUBLANES = 8
N_BLK = D_MODEL // LANES
N_GROUPS = 8
N_SHARDS = 4
CONV_WIDTH = 4
LRU_C = 8.0
CHUNK = 64
CHUNKS_IN_FLIGHT = 16
FWD_CHUNKS_IN_FLIGHT = 32
HG_SCALE = float(LANES) ** -0.5
EPS = 1e-6
ADAM_LR, ADAM_B1, ADAM_B2, ADAM_EPS, ADAM_WD, ADAM_STEP = 0.001, 0.9, 0.999, 1e-08, 0.01, 10
MATMUL_TOKENS = 512
TAIL_TOKENS = 256
CONTRACT_TOKENS = 2048
VMEM_LIMIT = 56 * 1024 * 1024
VMEM_LIMIT_BIG = 60 * 1024 * 1024
MESH = pl.DeviceIdType.MESH

_SLOT_TO_GROUP = (2, 3, 4, 5, 0, 1, 6, 7)


def _mm(a, b):
    return lax.dot_general(a.astype(_MXU_DTYPE), b.astype(_MXU_DTYPE), (((1,), (0,)), ((), ())),
                           preferred_element_type=F32)


def _mm_nt(a, b):
    return lax.dot_general(a.astype(_MXU_DTYPE), b.astype(_MXU_DTYPE), (((1,), (1,)), ((), ())),
                           preferred_element_type=F32)


def _mm_tn(a, b):
    return lax.dot_general(a.astype(_MXU_DTYPE), b.astype(_MXU_DTYPE), (((0,), (0,)), ((), ())),
                           preferred_element_type=F32)


def _sigmoid(x):
    return 0.5 * jnp.tanh(0.5 * x) + 0.5


def _log1p_pos(y):
    series = y * (1.0 - y * (0.5 - y * (1.0 / 3.0 - y * 0.25)))
    return jnp.where(y < 0.01, series, jnp.log(1.0 + y))


def _softplus(x):
    return jnp.maximum(x, 0.0) + _log1p_pos(jnp.exp(-jnp.abs(x)))


def _shift_down(x, n):
    rolled = pltpu.roll(x, n, 0)
    edge = SUBLANES if (n < SUBLANES and x.shape[0] > SUBLANES) else x.shape[0]
    rows = lax.broadcasted_iota(jnp.int32, (edge, x.shape[1]), 0)
    head = jnp.where(rows >= n, rolled[:edge], 0.0)
    return head if edge == x.shape[0] else jnp.concatenate([head, rolled[edge:]], axis=0)


def _shift_up(x, n):
    size = x.shape[0]
    rolled = pltpu.roll(x, size - n, 0)
    edge = SUBLANES if (n < SUBLANES and size > SUBLANES) else size
    rows = lax.broadcasted_iota(jnp.int32, (edge, x.shape[1]), 0)
    tail = jnp.where(rows < edge - n, rolled[size - edge:], 0.0)
    return tail if edge == size else jnp.concatenate([rolled[:size - edge], tail], axis=0)


def _params(dims, vmem=VMEM_LIMIT):
    return pltpu.CompilerParams(dimension_semantics=dims, vmem_limit_bytes=vmem)


def _slot_of_group(g):
    return jnp.where(g < 2, g + 4, jnp.where(g < 6, g - 2, g))


def _inproj_fwd_gather(x2d, norm_g, slotted, conv_slotted, chip):
    tokens, d = x2d.shape
    tm = min(MATMUL_TOKENS, tokens)
    n_tiles = tokens // tm
    n_big = len(slotted)
    n_sem = 6 * (n_big + 1) + 3
    last_pass = N_GROUPS - 1

    def shard_of(k, chip_id):
        x, y = chip_id // 2, chip_id % 2
        return 2 * jnp.where(k % 2 == 1, 1 - x, x) + jnp.where(k // 2 == 1, 1 - y, y)

    def body(chip_ref, x_ref, g_ref, *rest):
        bufs, cw = rest[n_big + 1:2 * n_big + 1], rest[2 * n_big + 1]
        z_ref, ht_ref = rest[2 * n_big + 2:2 * n_big + 4]
        h_all, slab, send_sems, recv_sems, slab_sems = rest[2 * n_big + 4:]
        del chip_ref
        p, i = pl.program_id(0), pl.program_id(1)
        x, y, c, chips = _mesh_position()
        me, sibling = 2 * x + y, (x, y, 1 - c)

        pieces = [(0, 0), (0, 1)] + [(a, None) for a in range(1, n_big)]

        def half(piece, slot, which):
            a, q = pieces[piece]
            hs = bufs[a].shape[1] // 2
            cols = slice(None) if q is None else pl.ds(q * D_MODEL, D_MODEL)
            return bufs[a].at[slot, pl.ds(which * hs, hs), cols]

        def send(piece, j):
            mine = half(piece, me, c)
            return _remote(mine, mine, send_sems, recv_sems, 6 * piece + j, (chips[j][0], chips[j][1], c))

        def arrival(piece, j):
            landed = half(piece, 2 * chips[j][0] + chips[j][1], c)
            return _remote(landed, landed, send_sems, recv_sems, 6 * piece + j, (chips[j][0], chips[j][1], c))

        def passed_on(piece, j, which):
            landed = half(piece, 2 * chips[j][0] + chips[j][1], which)
            return _remote(landed, landed, send_sems, recv_sems, 6 * piece + 3 + j, sibling)

        def conv_copy(j, slot):
            return _remote(cw.at[slot], cw.at[slot], send_sems, recv_sems, 6 * len(pieces) + j,
                           (chips[j][0], chips[j][1], c))

        def land(piece, j):
            arrival(piece, j).wait_recv()
            passed_on(piece, j, c).start()
            passed_on(piece, j, 1 - c).wait_recv()

        def slab_copy(pv):
            src = bufs[0].at[shard_of(pv // 2, me), :, pl.ds((pv % 2) * D_MODEL, D_MODEL)]
            return pltpu.make_async_copy(src, slab.at[pv % 2], slab_sems.at[pv % 2])

        @pl.when((p == 0) & (i == 0))
        def _():
            for q in range(2):
                send(q, 0).start()
                send(q, 1).start()
            slab_copy(0).start()

        @pl.when(i == 0)
        def _():
            for pv in range(N_GROUPS):
                @pl.when(p == pv)
                def _(pv=pv):
                    slab_copy(pv).wait()

        rows = pl.ds(pl.multiple_of(i * tm, tm), tm)

        @pl.when(p == 0)
        def _():
            xt = x_ref[...]
            r = lax.rsqrt(jnp.mean(xt * xt, axis=-1, keepdims=True) + EPS)
            h = (xt * r) * g_ref[...]
            h_all[rows, :] = h.astype(_MXU_DTYPE)
            ht_ref[...] = jnp.transpose(h).astype(_MXU_DTYPE)

        z_ref[...] = _mm(h_all[rows, :], slab[p % 2])

        def relay(q):
            landed = half(q, 2 * chips[q][0] + chips[q][1], c)
            to = chips[1 - q]
            return _remote(landed, landed, send_sems, recv_sems, 6 * q + 2, (to[0], to[1], c))

        landings = {1: [(0, 0)], 2: [(1, 0), (1, 1)], 3: [(0, 1)], 5: [(0, 2)], 6: [(1, 2)]}

        def end_of_pass(pv):
            for q, j in landings.get(pv - 1, []):
                land(q, j)
                if j == q:
                    relay(q).start()
            if pv - 1 == 1:
                for piece in range(2, len(pieces)):
                    for jj in range(3):
                        send(piece, jj).start()
                for jj in range(3):
                    conv_copy(jj, me).start()
            if pv - 1 in (5, 6):
                for piece in range(2, len(pieces)):
                    for jj in ((0, 1) if pv - 1 == 5 else (2,)):
                        land(piece, jj)
            slab_copy(pv).start()

        @pl.when(i == n_tiles - 1)
        def _():
            for pv in range(1, N_GROUPS):
                pl.when(p == pv - 1)(functools.partial(end_of_pass, pv))

        @pl.when((p == last_pass) & (i == n_tiles - 1))
        def _():
            for j in range(3):
                conv_copy(j, 2 * chips[j][0] + chips[j][1]).wait_recv()
            for piece in range(len(pieces)):
                for j in range(3):
                    (relay(piece) if (piece < 2 and j == 2) else send(piece, j)).wait_send()
                    passed_on(piece, j, c).wait_send()
            for j in range(3):
                conv_copy(j, me).wait_send()

    def z_index(p, i, chip_ref):
        g = 2 * shard_of(p // 2, chip_ref[0]) + p % 2
        return (_slot_of_group(g), i, 0)

    def first_pass_tile(p, i, chip_ref):
        return jnp.where(p == 0, i, n_tiles - 1)

    hbm = pl.BlockSpec(memory_space=pl.ANY)
    operands = list(slotted) + [conv_slotted]
    grid_spec = pltpu.PrefetchScalarGridSpec(
        num_scalar_prefetch=1, grid=(N_GROUPS, n_tiles),
        in_specs=[pl.BlockSpec((tm, d), lambda p, i, chip_ref: (first_pass_tile(p, i, chip_ref), 0)),
                  pl.BlockSpec((1, d), lambda p, i, chip_ref: (0, 0))] + [hbm] * (n_big + 1),
        out_specs=[hbm] * (n_big + 1) + [pl.BlockSpec((None, tm, D_MODEL), z_index),
                                         pl.BlockSpec((d, tm), lambda p, i, chip_ref: (0, first_pass_tile(p, i, chip_ref)))],
        scratch_shapes=[pltpu.VMEM((tokens, d), _MXU_DTYPE), pltpu.VMEM((2, d, D_MODEL), _MXU_DTYPE),
                        pltpu.SemaphoreType.DMA((n_sem,)), pltpu.SemaphoreType.DMA((n_sem,)),
                        pltpu.SemaphoreType.DMA((2,))])
    out = pl.pallas_call(
        body, name="inproj_fwd_gather", grid_spec=grid_spec,
        out_shape=[jax.ShapeDtypeStruct(a.shape, a.dtype) for a in operands]
        + [jax.ShapeDtypeStruct((N_GROUPS, tokens, D_MODEL), F32), jax.ShapeDtypeStruct((d, tokens), _MXU_DTYPE)],
        input_output_aliases={3 + a: a for a in range(n_big + 1)},
        compiler_params=_params(("arbitrary", "arbitrary")),
    )(chip, x2d, norm_g, *operands)
    return out[n_big + 1], out[n_big + 2], out[:n_big], out[n_big]


def _lane_blocks(x):
    return [x[:, k * LANES:(k + 1) * LANES] for k in range(x.shape[1] // LANES)]


def _block_diag(x, w_ref, transposed=False):
    mm = _mm_nt if transposed else _mm
    return jnp.concatenate([mm(xk, w_ref[k]) for k, xk in enumerate(_lane_blocks(x))], axis=1)


def _lru_decay(gr, sp):
    log_a = (-LRU_C) * gr * sp
    a = jnp.exp(log_a)
    y = 2.0 * log_a
    mult_sq = jnp.where(y > -1e-3, -y * (1.0 + 0.5 * y), 1.0 - a * a)
    inv_mult = lax.rsqrt(jnp.maximum(mult_sq, 1e-37))
    return a, mult_sq * inv_mult, inv_mult


def _tile_rows(width):
    return lax.broadcasted_iota(jnp.int32, (SUBLANES, width), 0)


def _scan_forward(a_scr, u_scr, h_scr, seq):
    width = a_scr.shape[1]
    rows = _tile_rows(width)

    def tile(j, carry):
        sl = pl.ds(pl.multiple_of(j * SUBLANES, SUBLANES), SUBLANES)
        a = a_scr[sl, :]
        u = u_scr[sl, :]
        for d in (1, 2, 4):
            keep = rows >= d
            a_sh = jnp.where(keep, pltpu.roll(a, d, 0), 1.0)
            u_sh = jnp.where(keep, pltpu.roll(u, d, 0), 0.0)
            u = a * u_sh + u
            a = a * a_sh
        h = u + a * carry
        h_scr[sl, :] = h
        return jnp.broadcast_to(h[SUBLANES - 1:SUBLANES, :], (SUBLANES, width))

    lax.fori_loop(0, seq // SUBLANES, tile, jnp.zeros((SUBLANES, width), F32))


def _scan_backward(c_scr, d_scr, g_scr, seq):
    width = c_scr.shape[1]
    rows = _tile_rows(width)
    n_tiles = seq // SUBLANES

    def tile(jj, carry):
        j = n_tiles - 1 - jj
        sl = pl.ds(pl.multiple_of(j * SUBLANES, SUBLANES), SUBLANES)
        c = c_scr[sl, :]
        g = d_scr[sl, :]
        for d in (1, 2, 4):
            keep = rows < SUBLANES - d
            c_sh = jnp.where(keep, pltpu.roll(c, SUBLANES - d, 0), 1.0)
            g_sh = jnp.where(keep, pltpu.roll(g, SUBLANES - d, 0), 0.0)
            g = c * g_sh + g
            c = c * c_sh
        g = g + c * carry
        g_scr[sl, :] = g
        return jnp.broadcast_to(g[0:1, :], (SUBLANES, width))

    lax.fori_loop(0, n_tiles, tile, jnp.zeros((SUBLANES, width), F32))


LRU_BLOCKS_PER_STEP = 2
LRU_LANES = LRU_BLOCKS_PER_STEP * LANES
LRU_STEPS = N_BLK // LRU_BLOCKS_PER_STEP


def _branch_a_fwd(z, conv_w, conv_b, wx, bx, wa, ba, lam, batch, seq):
    tokens = batch * seq

    def body(z_ref, cw_ref, cb_ref, wx_ref, bx_ref, wa_ref, ba_ref, lam_ref, ya_ref, hl_ref, kept_ref, a_scr, u_scr):
        xa = z_ref[0]
        ga = z_ref[1]
        xc = (cb_ref[...] + cw_ref[3:4, :] * xa + cw_ref[2:3, :] * _shift_down(xa, 1)
              + cw_ref[1:2, :] * _shift_down(xa, 2) + cw_ref[0:1, :] * _shift_down(xa, 3))
        gi = _sigmoid(_block_diag(xc, wx_ref) + bx_ref[...])
        gr = _sigmoid(_block_diag(xc, wa_ref) + ba_ref[...])
        a, mult, _ = _lru_decay(gr, _softplus(-lam_ref[...]))
        kept_ref[0], kept_ref[1], kept_ref[2] = xc, gi, gr
        a_scr[...] = a
        u_scr[...] = mult * gi * xc
        _scan_forward(a_scr, u_scr, hl_ref, seq)
        ya_ref[...] = (hl_ref[...] * (ga * _sigmoid(ga))).astype(_MXU_DTYPE)

    blk = pl.BlockSpec((seq, LRU_LANES), lambda b, c: (b, c))
    vec = pl.BlockSpec((1, LRU_LANES), lambda b, c: (0, c))
    mat = pl.BlockSpec((LRU_BLOCKS_PER_STEP, LANES, LANES), lambda b, c: (c, 0, 0))
    return pl.pallas_call(
        body, name="branch_a_fwd",
        grid=(batch, LRU_STEPS),
        in_specs=[pl.BlockSpec((2, seq, LRU_LANES), lambda b, c: (2, b, c)),
                  pl.BlockSpec((CONV_WIDTH, LRU_LANES), lambda b, c: (0, c)), vec, mat, vec, mat, vec, vec],
        out_specs=[blk, blk, pl.BlockSpec((3, seq, LRU_LANES), lambda b, c: (0, b, c))],
        out_shape=[jax.ShapeDtypeStruct((tokens, D_MODEL), _MXU_DTYPE), jax.ShapeDtypeStruct((tokens, D_MODEL), F32),
                   jax.ShapeDtypeStruct((3, tokens, D_MODEL), F32)],
        scratch_shapes=[pltpu.VMEM((seq, LRU_LANES), F32), pltpu.VMEM((seq, LRU_LANES), F32)],
        compiler_params=_params(("parallel", "parallel")),
    )(z, conv_w, conv_b, wx, bx, wa, ba, lam)


def _branch_a_bwd(z, hl, kept, dya, dz, conv_w, wx, wa, lam, batch, seq):
    def body(z_ref, hl_ref, kept_ref, dya_ref, dz_in_ref, cw_ref, wx_ref, wa_ref, lam_ref,
             dz_ref, dcw_ref, dcb_ref, dwx_ref, dbx_ref, dwa_ref, dba_ref, dlam_ref, c_scr, d_scr):
        del dz_in_ref
        g_scr = d_scr
        xa = z_ref[0]
        ga = z_ref[1]
        hl = hl_ref[...]
        dya = dya_ref[...]
        xc, gi, gr = kept_ref[0], kept_ref[1], kept_ref[2]
        sp = _softplus(-lam_ref[...])
        a, mult, inv_mult = _lru_decay(gr, sp)
        sga = _sigmoid(ga)
        dz_ref[1] = (dya * hl * (sga * (1.0 + ga * (1.0 - sga)))).astype(_MXU_DTYPE)
        c_scr[...] = _shift_up(a, 1)
        d_scr[...] = dya * (ga * sga)
        _scan_backward(c_scr, d_scr, g_scr, seq)
        g = g_scr[...]
        da = g * _shift_down(hl, 1)
        dmult = g * gi * xc
        dgi = g * mult * xc
        dxc = g * mult * gi
        dlog_a = da * a - dmult * (a * a) * inv_mult
        dgr = dlog_a * (-LRU_C) * sp
        dsp = jnp.sum(dlog_a * gr, axis=0, keepdims=True) * (-LRU_C)
        dlam = -dsp * _sigmoid(-lam_ref[...])
        dpi = dgi * gi * (1.0 - gi)
        dpr = dgr * gr * (1.0 - gr)
        dxc = dxc + _block_diag(dpi, wx_ref, transposed=True) + _block_diag(dpr, wa_ref, transposed=True)
        dwx = jnp.stack([_mm_tn(xk, dk) for xk, dk in zip(_lane_blocks(xc), _lane_blocks(dpi))])
        dwa = jnp.stack([_mm_tn(xk, dk) for xk, dk in zip(_lane_blocks(xc), _lane_blocks(dpr))])
        dbx = jnp.sum(dpi, axis=0, keepdims=True)
        dba = jnp.sum(dpr, axis=0, keepdims=True)
        ahead = [dxc if k == CONV_WIDTH - 1 else _shift_up(dxc, CONV_WIDTH - 1 - k) for k in range(CONV_WIDTH)]
        dxa = sum(cw_ref[k:k + 1, :] * ahead[k] for k in range(CONV_WIDTH))
        dz_ref[0] = dxa.astype(_MXU_DTYPE)
        dcb = jnp.sum(dxc, axis=0, keepdims=True)
        dcw = [jnp.sum(ahead[k] * xa, axis=0, keepdims=True) for k in range(CONV_WIDTH)]

        @pl.when(pl.program_id(1) == 0)
        def _():
            for k in range(CONV_WIDTH):
                dcw_ref[k:k + 1, :] = dcw[k]
            dcb_ref[...] = dcb
            dwx_ref[...] = dwx
            dbx_ref[...] = dbx
            dwa_ref[...] = dwa
            dba_ref[...] = dba
            dlam_ref[...] = dlam

        @pl.when(pl.program_id(1) != 0)
        def _():
            for k in range(CONV_WIDTH):
                dcw_ref[k:k + 1, :] += dcw[k]
            dcb_ref[...] += dcb
            dwx_ref[...] += dwx
            dbx_ref[...] += dbx
            dwa_ref[...] += dwa
            dba_ref[...] += dba
            dlam_ref[...] += dlam

    tokens = batch * seq
    blk = pl.BlockSpec((seq, LRU_LANES), lambda c, b: (b, c))
    vec = pl.BlockSpec((1, LRU_LANES), lambda c, b: (0, c))
    mat = pl.BlockSpec((LRU_BLOCKS_PER_STEP, LANES, LANES), lambda c, b: (c, 0, 0))
    vec_shape = jax.ShapeDtypeStruct((1, D_MODEL), F32)
    mat_shape = jax.ShapeDtypeStruct((N_BLK, LANES, LANES), F32)
    return pl.pallas_call(
        body, name="branch_a_bwd",
        grid=(LRU_STEPS, batch),
        in_specs=[pl.BlockSpec((2, seq, LRU_LANES), lambda c, b: (2, b, c)), blk,
                  pl.BlockSpec((3, seq, LRU_LANES), lambda c, b: (0, b, c)), blk,
                  pl.BlockSpec(memory_space=pl.ANY),
                  pl.BlockSpec((CONV_WIDTH, LRU_LANES), lambda c, b: (0, c)), mat, mat, vec],
        out_specs=[pl.BlockSpec((2, seq, LRU_LANES), lambda c, b: (2, b, c)),
                   pl.BlockSpec((CONV_WIDTH, LRU_LANES), lambda c, b: (0, c)), vec, mat, vec, mat, vec, vec],
        out_shape=[jax.ShapeDtypeStruct((N_GROUPS, tokens, D_MODEL), _MXU_DTYPE),
                   jax.ShapeDtypeStruct((CONV_WIDTH, D_MODEL), F32), vec_shape, mat_shape, vec_shape, mat_shape,
                   vec_shape, vec_shape],
        scratch_shapes=[pltpu.VMEM((seq, LRU_LANES), F32)] * 2,
        input_output_aliases={4: 0},
        compiler_params=_params(("parallel", "arbitrary"), vmem=VMEM_LIMIT_BIG),
    )(z, hl, kept, dya, dz, conv_w, wx, wa, lam)


def _chunk_masks(transposed=False):
    r = lax.broadcasted_iota(jnp.int32, (CHUNK, CHUNK), 0)
    c = lax.broadcasted_iota(jnp.int32, (CHUNK, CHUNK), 1)
    return r <= c if transposed else r >= c


def _row_blocks(seq, fn):
    block = min(256, seq)

    def trip(i, carry):
        fn(pl.ds(pl.multiple_of(i * block, block), block))
        return carry

    lax.fori_loop(0, seq // block, trip, 0)


def _hgrn_prepare(z_ref, lb_ref, f_scr, logf_scr, qh_scr, seq):
    lb = _sigmoid(lb_ref[0:1, :] - lb_ref[1:2, :])

    def block(rows):
        q = z_ref[0, rows, :]
        f = lb + (1.0 - lb) * _sigmoid(z_ref[1, rows, :])
        f_scr[rows, :] = f
        logf_scr[rows, :] = jnp.log(f)
        qh_scr[rows, :] = q * _sigmoid(q)

    _row_blocks(seq, block)
    return lb


def _cumsum_rows(x, reverse=False):
    shift = _shift_up if reverse else _shift_down
    d = 1
    while d < x.shape[0]:
        x = x + shift(x, d)
        d *= 2
    return x


def _lane_mean(x):
    return jnp.mean(x, axis=-1, keepdims=True)


def _token_contractions(lhs_scr, rhs_scr, out_ref, seq):
    rows_id = lax.broadcasted_iota(jnp.int32, (LANES, LANES), 0)

    def transposed(p):
        rows = pl.ds(pl.multiple_of(p * LANES, LANES), LANES)
        return jnp.transpose(lhs_scr[rows, :]).astype(_MXU_DTYPE), rhs_scr[rows, :]

    def contract(p, s):
        lhs_t, rhs = s
        return (_mm(lhs_t, jnp.where(rows_id < CHUNK, rhs, 0.0)), _mm(lhs_t, jnp.where(rows_id >= CHUNK, rhs, 0.0)))

    def store(p, out):
        out_ref[2 * p] = out[0]
        out_ref[2 * p + 1] = out[1]

    _independent_trips(seq // LANES, [transposed, contract], store)


def _chunk_rows(c):
    return pl.ds(pl.multiple_of(c * CHUNK, CHUNK), CHUNK)


def _chunk_terms(c, z_ref, f_scr, qh_scr, b_scr):
    rows = _chunk_rows(c)
    b = b_scr[rows, :]
    b_mid = b_scr[pl.ds(c * CHUNK + CHUNK // 2, 1), :]
    b_last = b_scr[pl.ds(c * CHUNK + CHUNK - 1, 1), :]
    qh = qh_scr[rows, :]
    k = 1.0 - f_scr[rows, :]
    v = z_ref[2, rows, :]
    e_q = jnp.exp(b - b_mid) * HG_SCALE
    e_k = jnp.exp(b_mid - b)
    e_qi = jnp.exp(b) * HG_SCALE
    e_ks = jnp.exp(b_last - b)
    decay = jnp.exp(b_last)
    return rows, qh, k, v, e_q, e_k, e_qi, e_ks, decay


def _independent_trips(n, stages, store, group=CHUNKS_IN_FLIGHT):
    stages = stages if isinstance(stages, (list, tuple)) else [stages]
    group = min(group, n)

    def trip(g, carry):
        ids = [g * group + i for i in range(group)]
        state = [stages[0](c) for c in ids]
        for stage in stages[1:]:
            state = [stage(c, s) for c, s in zip(ids, state)]
        for c, s in zip(ids, state):
            store(c, s)
        return carry

    lax.fori_loop(0, n // group, trip, 0)


def _branch_b_fwd(z, lb_logits, hg_g, batch, seq):
    tokens = batch * seq
    n_chunks = seq // CHUNK

    def body(z_ref, lb_ref, g_ref, yb_ref, st_ref, kept_ref, logf_scr, o_scr, qi_scr, ks_scr, dec_scr):
        f_scr, qh_scr, b_scr, o_kept = (kept_ref.at[k] for k in range(4))
        _hgrn_prepare(z_ref, lb_ref, f_scr, logf_scr, qh_scr, seq)
        causal = _chunk_masks()
        gain = g_ref[...]

        def cumulate(c):
            return _cumsum_rows(logf_scr[_chunk_rows(c), :])

        def store_cumulated(c, b):
            b_scr[_chunk_rows(c), :] = b

        def scores(c):
            _, qh, k, v, e_q, e_k, e_qi, e_ks, decay = _chunk_terms(c, z_ref, f_scr, qh_scr, b_scr)
            return _mm_nt(qh * e_q, k * e_k), v, qh * e_qi, k * e_ks, decay

        def within_chunk(c, s):
            att, v, q_int, k_st, decay = s
            return _mm(jnp.where(causal, att, 0.0), v), q_int, k_st, decay

        def store_within_chunk(c, out):
            rows = _chunk_rows(c)
            o_scr[rows, :], qi_scr[rows, :], ks_scr[rows, :], dec_scr[pl.ds(c, 1), :] = out

        def carry_state(c, state_t):
            update = st_ref[c]
            st_ref[c] = state_t
            return state_t * dec_scr[pl.ds(c, 1), :] + update

        def finish(c):
            rows = _chunk_rows(c)
            o = o_scr[rows, :] + _mm_nt(qi_scr[rows, :], st_ref[c])
            r = lax.rsqrt(_lane_mean(o * o) + EPS)
            gb = z_ref[3, rows, :]
            return (((o * r) * gain) * (gb * _sigmoid(gb))).astype(_MXU_DTYPE), o

        def store_finished(c, out):
            yb_ref[_chunk_rows(c), :], o_kept[_chunk_rows(c), :] = out

        _independent_trips(n_chunks, cumulate, store_cumulated, FWD_CHUNKS_IN_FLIGHT)
        _independent_trips(n_chunks, [scores, within_chunk], store_within_chunk, FWD_CHUNKS_IN_FLIGHT)
        _token_contractions(z_ref.at[2], ks_scr, st_ref, seq)
        lax.fori_loop(0, n_chunks, carry_state, jnp.zeros((LANES, LANES), F32))
        _independent_trips(n_chunks, finish, store_finished, FWD_CHUNKS_IN_FLIGHT)

    seq_buf = pltpu.VMEM((seq, LANES), F32)
    return pl.pallas_call(
        body, name="branch_b_fwd",
        grid=(batch, N_BLK),
        in_specs=[pl.BlockSpec((4, seq, LANES), lambda b, h: (0, b, h)),
                  pl.BlockSpec((2, LANES), lambda b, h: (0, h)),
                  pl.BlockSpec((1, LANES), lambda b, h: (0, 0))],
        out_specs=[pl.BlockSpec((seq, LANES), lambda b, h: (b, h)),
                   pl.BlockSpec((None, n_chunks, LANES, LANES), lambda b, h: (b * N_BLK + h, 0, 0, 0)),
                   pl.BlockSpec((4, seq, LANES), lambda b, h: (0, b, h))],
        out_shape=[jax.ShapeDtypeStruct((tokens, D_MODEL), _MXU_DTYPE),
                   jax.ShapeDtypeStruct((batch * N_BLK, n_chunks, LANES, LANES), F32),
                   jax.ShapeDtypeStruct((4, tokens, D_MODEL), F32)],
        scratch_shapes=[seq_buf] * 4 + [pltpu.VMEM((n_chunks, LANES), F32)],
        compiler_params=_params(("parallel", "parallel")),
    )(z, lb_logits, hg_g)


def _branch_b_bwd(z, states, kept, dyb, dz, lb_logits, hg_g, batch, seq):
    n_chunks = seq // CHUNK

    def body(z_ref, st_ref, kept_ref, dyb_ref, dz_in_ref, lb_ref, g_ref, dz_ref, dlog_ref, dg_ref,
             do_scr, qi_scr, dqh_scr, df_scr, dec_scr, dgp_scr, dlb_scr, dst_scr):
        del dz_in_ref
        f_scr, qh_scr, b_scr, o_kept = (kept_ref.at[k] for k in range(4))
        first = (pl.program_id(0) == 0) & (pl.program_id(1) == 0)
        lb = _sigmoid(lb_ref[0:1, :] - lb_ref[1:2, :])
        causal = _chunk_masks()
        anti_causal = _chunk_masks(transposed=True)
        gain = g_ref[...]

        @pl.when(first)
        def _():
            dg_ref[...] = jnp.zeros_like(dg_ref)

        @pl.when(pl.program_id(1) == 0)
        def _():
            dlb_scr[...] = jnp.zeros_like(dlb_scr)

        def output_gradient(c):
            rows = _chunk_rows(c)
            b = b_scr[rows, :]
            q_int = qh_scr[rows, :] * (jnp.exp(b) * HG_SCALE)
            decay = jnp.exp(b_scr[pl.ds(c * CHUNK + CHUNK - 1, 1), :])
            o = o_kept[rows, :]
            r = lax.rsqrt(_lane_mean(o * o) + EPS)
            o_n = o * r
            gb = z_ref[3, rows, :]
            sgb = _sigmoid(gb)
            dyb_c = dyb_ref[rows, :]
            d_ong = dyb_c * (gb * sgb)
            d_gb = (dyb_c * (o_n * gain) * (sgb * (1.0 + gb * (1.0 - sgb)))).astype(_MXU_DTYPE)
            d_gain = jnp.sum(d_ong * o_n, axis=0, keepdims=True)
            d_on = d_ong * gain
            return d_gb, d_gain, r * (d_on - o_n * _lane_mean(d_on * o_n)), q_int, decay

        def store_output_gradient(c, out):
            rows = _chunk_rows(c)
            dz_ref[3, rows, :], dgp_scr[pl.ds(c, 1), :], do_scr[rows, :], qi_scr[rows, :], dec_scr[pl.ds(c, 1), :] = out

        def carry_state_gradient(cc, d_state_t):
            c = n_chunks - 1 - cc
            update = dst_scr[c]
            dst_scr[c] = d_state_t
            return d_state_t * dec_scr[pl.ds(c, 1), :] + update

        def score_gradients(c):
            rows, qh, k, v, e_q, e_k, e_qi, e_ks, decay = _chunk_terms(c, z_ref, f_scr, qh_scr, b_scr)
            state_t = st_ref[c]
            d_state_t = dst_scr[c]
            d_o = do_scr[rows, :]
            q_in, k_in, q_int, k_st = qh * e_q, k * e_k, qh * e_qi, k * e_ks
            first = (_mm_nt(k_in, q_in), _mm_nt(d_o, v), _mm_nt(v, d_o), _mm_nt(k_st, d_state_t), _mm(d_o, state_t),
                     _mm(v, d_state_t))
            d_decay = jnp.sum(state_t * d_state_t, axis=0, keepdims=True)
            return first, d_o, q_in, k_in, q_int, k_st, e_q, e_k, e_qi, e_ks, decay, d_decay

        def input_gradients(c, s):
            (att_t, d_att, d_att_t, dv_inter, dq_int, dk_st), d_o, q_in, k_in, q_int, k_st, e_q, e_k, e_qi, e_ks, decay, d_decay = s
            rows = _chunk_rows(c)
            d_v = _mm(jnp.where(anti_causal, att_t, 0.0), d_o) + dv_inter
            dq_in = _mm(jnp.where(causal, d_att, 0.0), k_in)
            dk_in = _mm(jnp.where(anti_causal, d_att_t, 0.0), q_in)
            d_k = dk_in * e_k + dk_st * e_ks
            kk = dk_st * k_st
            d_b = dq_in * q_in + dq_int * q_int - dk_in * k_in - kk
            d_b_last = jnp.sum(kk, axis=0, keepdims=True) + decay * d_decay
            d_logf = _cumsum_rows(d_b, reverse=True) + d_b_last
            return d_v.astype(_MXU_DTYPE), dq_in * e_q + dq_int * e_qi, d_logf / f_scr[rows, :] - d_k

        def store_input_gradients(c, out):
            rows = _chunk_rows(c)
            dz_ref[2, rows, :], dqh_scr[rows, :], df_scr[rows, :] = out

        def input_activations(rows):
            q = z_ref[0, rows, :]
            sq = _sigmoid(q)
            dz_ref[0, rows, :] = (dqh_scr[rows, :] * (sq * (1.0 + q * (1.0 - sq)))).astype(_MXU_DTYPE)
            sg = _sigmoid(z_ref[1, rows, :])
            d_f = df_scr[rows, :]
            dz_ref[1, rows, :] = (d_f * (1.0 - lb) * sg * (1.0 - sg)).astype(_MXU_DTYPE)
            dlb_scr[...] += jnp.sum(d_f * (1.0 - sg), axis=0, keepdims=True)

        _independent_trips(n_chunks, output_gradient, store_output_gradient)
        _token_contractions(do_scr, qi_scr, dst_scr, seq)
        lax.fori_loop(0, n_chunks, carry_state_gradient, jnp.zeros((LANES, LANES), F32))
        _independent_trips(n_chunks, [score_gradients, input_gradients], store_input_gradients)
        dg_ref[...] += jnp.sum(dgp_scr[...], axis=0, keepdims=True)
        _row_blocks(seq, input_activations)
        d_l0 = dlb_scr[...] * lb * (1.0 - lb)
        dlog_ref[0:1, :] = d_l0
        dlog_ref[1:2, :] = -d_l0

    tokens = batch * seq
    seq_buf = pltpu.VMEM((seq, LANES), F32)
    chunk_rows = pltpu.VMEM((n_chunks, LANES), F32)
    return pl.pallas_call(
        body, name="branch_b_bwd",
        grid=(N_BLK, batch),
        in_specs=[pl.BlockSpec((4, seq, LANES), lambda h, b: (0, b, h)),
                  pl.BlockSpec((None, n_chunks, LANES, LANES), lambda h, b: (b * N_BLK + h, 0, 0, 0)),
                  pl.BlockSpec((4, seq, LANES), lambda h, b: (0, b, h)),
                  pl.BlockSpec((seq, LANES), lambda h, b: (b, h)),
                  pl.BlockSpec(memory_space=pl.ANY),
                  pl.BlockSpec((2, LANES), lambda h, b: (0, h)),
                  pl.BlockSpec((1, LANES), lambda h, b: (0, 0))],
        out_specs=[pl.BlockSpec((4, seq, LANES), lambda h, b: (0, b, h)),
                   pl.BlockSpec((2, LANES), lambda h, b: (0, h)),
                   pl.BlockSpec((1, LANES), lambda h, b: (0, 0))],
        out_shape=[jax.ShapeDtypeStruct((N_GROUPS, tokens, D_MODEL), _MXU_DTYPE),
                   jax.ShapeDtypeStruct((2, D_MODEL), F32),
                   jax.ShapeDtypeStruct((1, LANES), F32)],
        scratch_shapes=[seq_buf] * 4 + [chunk_rows, chunk_rows, pltpu.VMEM((1, LANES), F32),
                                        pltpu.VMEM((n_chunks, LANES, LANES), F32)],
        input_output_aliases={4: 0},
        compiler_params=_params(("arbitrary", "arbitrary")),
    )(z, states, kept, dyb, dz, lb_logits, hg_g)


def _merge_tail(ya, yb, z, x2d, tgt2d, b_merge, final_g, pa, pb, wo):
    tokens, d = x2d.shape
    tm = min(TAIL_TOKENS, tokens)
    n_tiles = tokens // tm

    def body(ya_ref, yb_ref, z_ref, x_ref, t_ref, bm_ref, fg_ref, pa_hbm, pb_hbm, wo_hbm,
             dya_ref, dyb_ref, dx2_ref, dz_ref, loss_ref, dfg_ref, dbm_ref, dpa_hbm, dpb_hbm, dwo_hbm,
             pa_s, pb_s, wo_s, dpa_s, dpb_s, dwo_s, sems):
        i = pl.program_id(0)

        def together(pairs):
            copies = [pltpu.make_async_copy(src, dst, sems.at[k]) for k, (src, dst) in enumerate(pairs)]
            for cp in copies:
                cp.start()
            for cp in copies:
                cp.wait()

        @pl.when(i == 0)
        def _():
            together([(pa_hbm, pa_s), (pb_hbm, pb_s), (wo_hbm, wo_s)])
            dpa_s[...] = jnp.zeros_like(dpa_s)
            dpb_s[...] = jnp.zeros_like(dpb_s)
            dwo_s[...] = jnp.zeros_like(dwo_s)
            loss_ref[...] = jnp.zeros_like(loss_ref)
            dfg_ref[...] = jnp.zeros_like(dfg_ref)
            dbm_ref[...] = jnp.zeros_like(dbm_ref)

        ya_t = ya_ref[...]
        yb_t = yb_ref[...]
        out_a = _mm(ya_t, pa_s[...])
        out_b = _mm(yb_t, pb_s[...])
        g_a = _sigmoid(z_ref[0] + bm_ref[:, :d])
        g_b = _sigmoid(z_ref[1] + bm_ref[:, d:])
        mixed = g_a * out_a + g_b * out_b
        x2 = x_ref[...] + _mm(mixed, wo_s[...])
        r = lax.rsqrt(jnp.mean(x2 * x2, axis=-1, keepdims=True) + EPS)
        xn = x2 * r
        fg = fg_ref[...]
        diff = xn * fg - t_ref[...]
        loss_ref[...] += jnp.sum(diff * diff) * (0.5 / d)
        dy = diff * (1.0 / d)
        dfg_ref[...] += jnp.sum(dy * xn, axis=0, keepdims=True)
        dxn = dy * fg
        dx2 = r * (dxn - xn * jnp.mean(dxn * xn, axis=-1, keepdims=True))
        dx2_ref[...] = dx2
        dmixed = _mm_nt(dx2, wo_s[...])
        dwo_s[...] += _mm_tn(mixed, dx2)
        dgm_a = dmixed * out_a * g_a * (1.0 - g_a)
        dgm_b = dmixed * out_b * g_b * (1.0 - g_b)
        dz_ref[0] = dgm_a.astype(_MXU_DTYPE)
        dz_ref[1] = dgm_b.astype(_MXU_DTYPE)
        dbm_ref[:, :d] += jnp.sum(dgm_a, axis=0, keepdims=True)
        dbm_ref[:, d:] += jnp.sum(dgm_b, axis=0, keepdims=True)
        dout_a = dmixed * g_a
        dout_b = dmixed * g_b
        dpa_s[...] += _mm_tn(ya_t, dout_a)
        dpb_s[...] += _mm_tn(yb_t, dout_b)
        dya_ref[...] = _mm_nt(dout_a, pa_s[...])
        dyb_ref[...] = _mm_nt(dout_b, pb_s[...])

        @pl.when(i == n_tiles - 1)
        def _():
            together([(dpa_s, dpa_hbm), (dpb_s, dpb_hbm), (dwo_s, dwo_hbm)])

    tile = pl.BlockSpec((tm, d), lambda i: (i, 0))
    gm = pl.BlockSpec((2, tm, d), lambda i: (3, i, 0))
    row = lambda n: pl.BlockSpec((1, n), lambda i: (0, 0))
    hbm = pl.BlockSpec(memory_space=pl.ANY)
    act = jax.ShapeDtypeStruct((tokens, d), F32)
    mat = jax.ShapeDtypeStruct((d, d), F32)
    return pl.pallas_call(
        body, name="merge_tail",
        grid=(n_tiles,),
        in_specs=[tile, tile, gm, tile, tile, row(2 * d), row(d), hbm, hbm, hbm],
        out_specs=[tile, tile, tile, gm, row(LANES), row(d), row(2 * d), hbm, hbm, hbm],
        out_shape=[act, act, act, jax.ShapeDtypeStruct((N_GROUPS, tokens, d), _MXU_DTYPE),
                   jax.ShapeDtypeStruct((1, LANES), F32), jax.ShapeDtypeStruct((1, d), F32),
                   jax.ShapeDtypeStruct((1, 2 * d), F32), mat, mat, mat],
        scratch_shapes=[pltpu.VMEM((d, d), _MXU_DTYPE)] * 3 + [pltpu.VMEM((d, d), F32)] * 3
        + [pltpu.SemaphoreType.DMA((3,))],
        compiler_params=_params(("arbitrary",)),
    )(ya, yb, z, x2d, tgt2d, b_merge, final_g, pa, pb, wo)


def _inproj_dw_exchange(h_t, dz, scatter):
    d, tokens = h_t.shape
    tm = min(CONTRACT_TOKENS, tokens)
    n_i = tokens // tm
    half = d // 2

    def body(h_ref, dz_ref, *rest):
        n_in, n_out = scatter.n_in, scatter.n_out
        dw_hbm, land_hbm = rest[n_in:n_in + 2]
        acc, local_sems, send_sems, recv_sems = rest[n_in + 2 + n_out:n_in + 6 + n_out]
        carried = scatter.copies(rest[:n_in], rest[n_in + 2:n_in + 2 + n_out], rest[n_in + 6 + n_out:])
        s, i = pl.program_id(0), pl.program_id(1)
        x, y, c, _ = _mesh_position()

        @pl.when((s == 0) & (i == 0))
        def _():
            for cp in carried:
                cp.start()

        part = _mm(h_ref[...], dz_ref[...])
        buf = acc.at[s % 2]

        @pl.when(i == 0)
        def _():
            buf[...] = part

        @pl.when(i != 0)
        def _():
            buf[...] += part

        def copies(k):
            g = _SLOT_TO_GROUP[k]
            cols = pl.ds((g % 2) * D_MODEL, D_MODEL)
            src = acc.at[k % 2]
            mine = pltpu.make_async_copy(src, dw_hbm.at[g // 2, :, cols], local_sems.at[k % 2])
            theirs = _remote(src.at[pl.ds((1 - c) * half, half), :], land_hbm.at[g // 2, :, cols],
                             send_sems, recv_sems, k, (x, y, 1 - c))
            return mine, theirs

        for k in range(N_GROUPS):
            @pl.when((s == k) & (i == n_i - 1))
            def _(k=k):
                if k > 0:
                    mine, theirs = copies(k - 1)
                    mine.wait()
                    theirs.wait_send()
                mine, theirs = copies(k)
                mine.start()
                theirs.start()
                if k == N_GROUPS - 1:
                    mine.wait()
                    theirs.wait_send()
                    for kk in range(N_GROUPS):
                        copies(kk)[1].wait_recv()
                    for cp in carried:
                        cp.wait()

    hbm = pl.BlockSpec(memory_space=pl.ANY)
    more = scatter.plumbing(first_operand=2, first_output=2)
    return pl.pallas_call(
        body, name="inproj_dw_exchange",
        grid=(N_GROUPS, n_i),
        in_specs=[pl.BlockSpec((d, tm), lambda s, i: (0, i)),
                  pl.BlockSpec((None, tm, D_MODEL), lambda s, i: (s, i, 0))] + more[1],
        out_specs=[hbm, hbm] + more[2],
        out_shape=[jax.ShapeDtypeStruct((N_SHARDS, d, 2 * D_MODEL), F32),
                   jax.ShapeDtypeStruct((N_SHARDS, half, 2 * D_MODEL), F32)] + more[3],
        scratch_shapes=[pltpu.VMEM((2, d, D_MODEL), F32), pltpu.SemaphoreType.DMA((2,)),
                        pltpu.SemaphoreType.DMA((N_GROUPS,)), pltpu.SemaphoreType.DMA((N_GROUPS,))] + more[4],
        input_output_aliases=more[5],
        compiler_params=_params(("arbitrary", "arbitrary")),
    )(h_t, dz, *more[0])


def _inproj_dx(dz, w_all, x2d, dx2, norm_g, scatter):
    tokens, d = x2d.shape
    tm = min(TAIL_TOKENS, tokens)
    n_tiles = tokens // tm

    def body(dz_ref, w_hbm, x_ref, dx2_ref, g_ref, *rest):
        n_in, n_out = scatter.n_in, scatter.n_out
        dx_ref, dg_ref = rest[n_in:n_in + 2]
        w_res, load_sems = rest[n_in + 2 + n_out:n_in + 4 + n_out]
        copies = scatter.copies(rest[:n_in], rest[n_in + 2:n_in + 2 + n_out], rest[n_in + 4 + n_out:])
        i = pl.program_id(0)

        @pl.when(i == 0)
        def _():
            for cp in copies:
                cp.start()
            loads = [pltpu.make_async_copy(w_hbm.at[g // 2, :, pl.ds((g % 2) * D_MODEL, D_MODEL)],
                                           w_res.at[:, pl.ds(slot * D_MODEL, D_MODEL)], load_sems.at[slot])
                     for slot, g in enumerate(_SLOT_TO_GROUP)]
            for cp in loads:
                cp.start()
            for cp in loads:
                cp.wait()
            dg_ref[...] = jnp.zeros_like(dg_ref)

        dz_all = jnp.concatenate([dz_ref[s] for s in range(N_GROUPS)], axis=1)
        dh = jnp.transpose(_mm_nt(w_res[...], dz_all))
        x = x_ref[...]
        r = lax.rsqrt(jnp.mean(x * x, axis=-1, keepdims=True) + EPS)
        xn = x * r
        dg_ref[...] += jnp.sum(dh * xn, axis=0, keepdims=True)
        dxn = dh * g_ref[...]
        dx_ref[...] = r * (dxn - xn * jnp.mean(dxn * xn, axis=-1, keepdims=True)) + dx2_ref[...]

        @pl.when(i == n_tiles - 1)
        def _():
            for cp in copies:
                cp.wait()

    tile = pl.BlockSpec((tm, d), lambda i: (i, 0))
    hbm = pl.BlockSpec(memory_space=pl.ANY)
    more = scatter.plumbing(first_operand=5, first_output=2)
    return pl.pallas_call(
        body, name="inproj_dx", grid=(n_tiles,),
        in_specs=[pl.BlockSpec((N_GROUPS, tm, D_MODEL), lambda i: (0, i, 0)), hbm, tile, tile,
                  pl.BlockSpec((1, d), lambda i: (0, 0))] + more[1],
        out_specs=[tile, pl.BlockSpec((1, d), lambda i: (0, 0))] + more[2],
        out_shape=[jax.ShapeDtypeStruct((tokens, d), F32), jax.ShapeDtypeStruct((1, d), F32)] + more[3],
        scratch_shapes=[pltpu.VMEM((d, N_GROUPS * D_MODEL), _MXU_DTYPE), pltpu.SemaphoreType.DMA((N_GROUPS,))] + more[4],
        input_output_aliases=more[5],
        compiler_params=_params(("arbitrary",)),
    )(dz, w_all, x2d, dx2, norm_g, *more[0])


def _row_tile(rows, cols, itemsize=4, budget=2 * 1024 * 1024):
    tr = rows
    while tr * cols * itemsize > budget and tr % 16 == 0:
        tr //= 2
    return tr


def _cast_into_slot(a, chip, dtype, name):
    rows, cols = a.shape
    tr = _row_tile(rows, cols)

    def body(chip_ref, a_ref, o_ref):
        del chip_ref
        o_ref[...] = a_ref[...].astype(dtype)

    grid_spec = pltpu.PrefetchScalarGridSpec(
        num_scalar_prefetch=1, grid=(rows // tr,),
        in_specs=[pl.BlockSpec((tr, cols), lambda i, chip_ref: (i, 0))],
        out_specs=pl.BlockSpec((None, tr, cols), lambda i, chip_ref: (chip_ref[0], i, 0)))
    return pl.pallas_call(body, name=name, grid_spec=grid_spec,
                          out_shape=jax.ShapeDtypeStruct((N_SHARDS, rows, cols), dtype),
                          compiler_params=_params(("arbitrary",)))(chip, a)


def _sum_slots(stack, name):
    n, rows, cols = stack.shape
    tr = _row_tile(rows, cols * n)

    def body(s_ref, o_ref):
        total = s_ref[0].astype(F32)
        for k in range(1, n):
            total = total + s_ref[k].astype(F32)
        o_ref[...] = total

    return pl.pallas_call(body, name=name, grid=(rows // tr,),
                          in_specs=[pl.BlockSpec((n, tr, cols), lambda i: (0, i, 0))],
                          out_specs=pl.BlockSpec((tr, cols), lambda i: (i, 0)),
                          out_shape=jax.ShapeDtypeStruct((rows, cols), F32),
                          compiler_params=_params(("parallel",)))(stack)


def _add_half(full, landed, place, name):
    n, rows, cols = full.shape
    half = rows // 2
    tr = _row_tile(half, cols)
    nb = half // tr

    def body(place_ref, a_ref, b_ref, o_ref, own_ref):
        total = (a_ref[...] + b_ref[...]).astype(_MXU_DTYPE)
        o_ref[...] = total

        @pl.when(pl.program_id(1) == place_ref[1])
        def _():
            own_ref[...] = total

    grid_spec = pltpu.PrefetchScalarGridSpec(
        num_scalar_prefetch=1, grid=(nb, n),
        in_specs=[pl.BlockSpec((None, tr, cols), lambda i, j, place_ref: (j, place_ref[0] * nb + i, 0)),
                  pl.BlockSpec((None, tr, cols), lambda i, j, place_ref: (j, i, 0))],
        out_specs=[pl.BlockSpec((None, tr, cols), lambda i, j, place_ref: (j, i, 0)),
                   pl.BlockSpec((None, tr, cols), lambda i, j, place_ref: (place_ref[1], i, 0))])
    shape = jax.ShapeDtypeStruct((n, half, cols), _MXU_DTYPE)
    return pl.pallas_call(body, name=name, grid_spec=grid_spec, out_shape=[shape, shape],
                          compiler_params=_params(("parallel", "arbitrary")))(place, full, landed)


def _adamw_update(w, grad, m, v):
    c1 = 1.0 - ADAM_B1 ** ADAM_STEP
    c2 = 1.0 - ADAM_B2 ** ADAM_STEP
    nm = ADAM_B1 * m + (1.0 - ADAM_B1) * grad
    nv = ADAM_B2 * v + (1.0 - ADAM_B2) * (grad * grad)
    return (-ADAM_LR) * ((nm / c1) / (jnp.sqrt(nv / c2) + ADAM_EPS) + ADAM_WD * w), nm, nv


def _adamw(w, g, m, v, name):
    rows, cols = w.shape
    tr = _row_tile(rows, cols, budget=1024 * 1024)

    def body(w_ref, g_ref, m_ref, v_ref, d_ref, nm_ref, nv_ref):
        d_ref[...], nm_ref[...], nv_ref[...] = _adamw_update(w_ref[...], g_ref[...], m_ref[...], v_ref[...])

    spec = pl.BlockSpec((tr, cols), lambda i: (i, 0))
    shape = jax.ShapeDtypeStruct((rows, cols), F32)
    return pl.pallas_call(body, name=name, grid=(rows // tr,), in_specs=[spec] * 4, out_specs=[spec] * 3,
                          out_shape=[shape] * 3, compiler_params=_params(("parallel",)))(w, g, m, v)


def _adamw_halves(w, g_mine, g_sibling, m, v, core, name):
    rows, cols = w.shape
    half = rows // 2
    tr = _row_tile(half, cols, budget=1024 * 1024)
    nb = half // tr

    def body(core_ref, w_ref, gm_ref, gs_ref, m_ref, v_ref, g_ref, d_ref, nm_ref, nv_ref):
        mine = pl.program_id(0) // nb == core_ref[0]
        grad = jnp.where(mine, gm_ref[...], gs_ref[...])
        g_ref[...] = grad
        d_ref[...], nm_ref[...], nv_ref[...] = _adamw_update(w_ref[...], grad, m_ref[...], v_ref[...])

    spec = pl.BlockSpec((tr, cols), lambda i, core_ref: (i, 0))
    mine_spec = pl.BlockSpec((tr, cols), lambda i, core_ref: (jnp.where(i // nb == core_ref[0], i % nb, 0), 0))
    sibling_spec = pl.BlockSpec((tr, cols), lambda i, core_ref: (jnp.where(i // nb == core_ref[0], 0, i % nb), 0))
    grid_spec = pltpu.PrefetchScalarGridSpec(num_scalar_prefetch=1, grid=(rows // tr,),
                                             in_specs=[spec, mine_spec, sibling_spec, spec, spec], out_specs=[spec] * 4)
    shape = jax.ShapeDtypeStruct((rows, cols), F32)
    return pl.pallas_call(body, name=name, grid_spec=grid_spec, out_shape=[shape] * 4,
                          compiler_params=_params(("parallel",)))(core, w, g_mine, g_sibling, m, v)


def _local_step(x, loss_target, gather, reduction, b_merge, conv_b, rg_wx, rg_bx, rg_wa, rg_ba, rg_lambda,
                hg_lb_logits, hg_norm_g, norm_g, final_norm_g):
    batch, seq, d = x.shape
    x2d = x.reshape(batch * seq, d)
    tgt2d = loss_target.reshape(batch * seq, d)
    z, h_t, (w_all, pa, pb, wo), cw_all = _inproj_fwd_gather(x2d, norm_g, *gather)
    pa, pb, wo = (t.reshape(d, d) for t in (pa, pb, wo))
    conv_w = jnp.transpose(cw_all, (1, 0, 2)).reshape(CONV_WIDTH, d)
    lru = (conv_w, conv_b, rg_wx, rg_bx, rg_wa, rg_ba, rg_lambda)
    ya, hl, kept = _branch_a_fwd(z, *lru, batch, seq)
    yb, states, kept_b = _branch_b_fwd(z, hg_lb_logits, hg_norm_g, batch, seq)
    dya, dyb, dx2, dz, loss, d_final_g, d_b_merge, d_pa, d_pb, d_wo = _merge_tail(
        ya, yb, z, x2d, tgt2d, b_merge, final_norm_g, pa, pb, wo)
    dz, d_lb_logits, d_hg_g = _branch_b_bwd(z, states, kept_b, dyb, dz, hg_lb_logits, hg_norm_g, batch, seq)
    dz, d_conv_w, d_conv_b, d_wx, d_bx, d_wa, d_ba, d_lam = _branch_a_bwd(
        z, hl, kept, dya, dz, conv_w, rg_wx, rg_wa, rg_lambda, batch, seq)
    small = dict(b_merge=d_b_merge, conv_w=d_conv_w, conv_b=d_conv_b, rg_wx=d_wx, rg_bx=d_bx, rg_wa=d_wa,
                 rg_ba=d_ba, rg_lambda=d_lam, hg_lb_logits=d_lb_logits, hg_norm_g=d_hg_g,
                 norm_g=jnp.zeros((1, d), F32), final_norm_g=d_final_g)
    first, second = reduction
    d_w_in, landed_w_in, *scattered_first = _inproj_dw_exchange(h_t, dz, first((d_pa, d_pb, d_wo), small))
    grad_x, d_norm_g, *scattered_second = _inproj_dx(dz, w_all, x2d, dx2, norm_g, scatter=second(d_w_in, landed_w_in))
    return loss[0, 0], grad_x.reshape(batch, seq, d), d_norm_g, (scattered_first, scattered_second)


_SMALL_ORDER = ("b_merge", "conv_w", "conv_b", "rg_wx", "rg_bx", "rg_wa", "rg_ba", "rg_lambda", "hg_lb_logits",
                "hg_norm_g", "norm_g", "final_norm_g")
N_DEV = 8
PIECE_ROWS = 272


def _pack_small(tree):
    flat = jnp.concatenate([tree[k].reshape(-1) for k in _SMALL_ORDER])
    flat = jnp.pad(flat, (0, N_DEV * PIECE_ROWS * LANES - flat.shape[0]))
    return flat.reshape(N_DEV * PIECE_ROWS, LANES)


def _unpack_small(packed, like):
    flat = packed.reshape(-1)
    out, pos = {}, 0
    for k in _SMALL_ORDER:
        n = like[k].size
        out[k] = flat[pos:pos + n].reshape(like[k].shape)
        pos += n
    return out


def _mesh_position():
    x, y, c = lax.axis_index("x"), lax.axis_index("y"), lax.axis_index("c")
    other_chips = [(1 - x, y), (x, 1 - y), (1 - x, 1 - y)]
    return x, y, c, other_chips


def _other_devices(x, y, c):
    flips = [(fx, fy, fc) for fx in (0, 1) for fy in (0, 1) for fc in (0, 1) if (fx, fy, fc) != (0, 0, 0)]
    return [(jnp.where(fx, 1 - x, x), jnp.where(fy, 1 - y, y), jnp.where(fc, 1 - c, c)) for fx, fy, fc in flips]


def _remote(src, dst, send_sems, recv_sems, k, device):
    return pltpu.make_async_remote_copy(src_ref=src, dst_ref=dst, send_sem=send_sems.at[k], recv_sem=recv_sems.at[k],
                                        device_id=device, device_id_type=MESH)


def _exchange_halves(bigs, small):
    n_big = len(bigs)
    n_sem = n_big + N_DEV - 1

    def body(*refs):
        srcs, small_src = refs[:n_big], refs[n_big]
        outs, small_out = refs[n_big + 1:2 * n_big + 1], refs[2 * n_big + 1]
        send_sems, recv_sems, local_sem = refs[2 * n_big + 2:]
        x, y, c, _ = _mesh_position()
        me, sibling = 4 * x + 2 * y + c, (x, y, 1 - c)
        mine = pltpu.make_async_copy(small_src.at[pl.ds(me * PIECE_ROWS, PIECE_ROWS), :], small_out.at[me], local_sem)
        mine.start()
        copies = []
        for a in range(n_big):
            hs = srcs[a].shape[1] // 2
            copies.append(_remote(srcs[a].at[:, pl.ds((1 - c) * hs, hs), :], outs[a], send_sems, recv_sems, a, sibling))
        for k, (px, py, pc) in enumerate(_other_devices(x, y, c)):
            piece = small_src.at[pl.ds((4 * px + 2 * py + pc) * PIECE_ROWS, PIECE_ROWS), :]
            copies.append(_remote(piece, small_out.at[me], send_sems, recv_sems, n_big + k, (px, py, pc)))
        for cp in copies:
            cp.start()
        for cp in copies:
            cp.wait()
        mine.wait()

    hbm = pl.BlockSpec(memory_space=pl.ANY)
    out_shape = [jax.ShapeDtypeStruct((g.shape[0], g.shape[1] // 2, g.shape[2]), F32) for g in bigs]
    out_shape.append(jax.ShapeDtypeStruct((N_DEV, PIECE_ROWS, LANES), F32))
    return pl.pallas_call(
        body, name="exchange_halves",
        in_specs=[hbm] * (n_big + 1), out_specs=[hbm] * (n_big + 1), out_shape=out_shape,
        scratch_shapes=[pltpu.SemaphoreType.DMA((n_sem,)), pltpu.SemaphoreType.DMA((n_sem,)), pltpu.SemaphoreType.DMA],
    )(*bigs, small)


class _Scatter:
    def __init__(self, bigs, by_chip, small=None):
        self.bigs, self.by_chip, self.small = list(bigs), list(by_chip), small
        self.n_big = len(self.bigs)
        self.n_in = 2 * self.n_big + (small is not None)
        self.n_out = self.n_big + (small is not None)
        self.n_scratch = 2 + (small is not None)

    def plumbing(self, first_operand, first_output):
        hbm = pl.BlockSpec(memory_space=pl.ANY)
        n_sem = 3 * self.n_big + (N_DEV - 1 if self.small is not None else 0)
        operands = self.bigs + self.by_chip + ([self.small] if self.small is not None else [])
        out_shapes = [jax.ShapeDtypeStruct(g.shape, g.dtype) for g in self.by_chip]
        scratch = [pltpu.SemaphoreType.DMA((n_sem,)), pltpu.SemaphoreType.DMA((n_sem,))]
        if self.small is not None:
            out_shapes.append(jax.ShapeDtypeStruct((N_DEV, PIECE_ROWS, LANES), F32))
            scratch.append(pltpu.SemaphoreType.DMA)
        aliases = {first_operand + self.n_big + a: first_output + a for a in range(self.n_big)}
        return operands, [hbm] * self.n_in, [hbm] * self.n_out, out_shapes, scratch, aliases

    def copies(self, in_refs, out_refs, scratch_refs):
        srcs, outs = in_refs[:self.n_big], out_refs[:self.n_big]
        send_sems, recv_sems = scratch_refs[:2]
        x, y, c, chips = _mesh_position()
        chip, me = 2 * x + y, 4 * x + 2 * y + c
        copies = []
        for a in range(self.n_big):
            for j, (cx, cy) in enumerate(chips):
                copies.append(_remote(srcs[a].at[2 * cx + cy], outs[a].at[chip], send_sems, recv_sems, 3 * a + j,
                                      (cx, cy, c)))
        if self.small is not None:
            small_src, small_out = in_refs[2 * self.n_big], out_refs[self.n_big]
            copies.append(pltpu.make_async_copy(small_src, small_out.at[me], scratch_refs[2]))
            for k, peer in enumerate(_other_devices(x, y, c)):
                copies.append(_remote(small_src, small_out.at[me], send_sems, recv_sems, 3 * self.n_big + k, peer))
        return copies


def _swap_halves(halves, vec):
    n_big = len(halves)

    def body(*refs):
        srcs, vec_src = refs[:n_big], refs[n_big]
        outs, vec_out = refs[n_big + 1:2 * n_big + 1], refs[2 * n_big + 1]
        send_sems, recv_sems, local_sem = refs[2 * n_big + 2:]
        x, y, c, _ = _mesh_position()
        me = 4 * x + 2 * y + c
        copies = [pltpu.make_async_copy(vec_src, vec_out.at[me], local_sem)]
        copies += [_remote(srcs[a], outs[a], send_sems, recv_sems, a, (x, y, 1 - c)) for a in range(n_big)]
        copies += [_remote(vec_src, vec_out.at[me], send_sems, recv_sems, n_big + k, peer)
                   for k, peer in enumerate(_other_devices(x, y, c))]
        for cp in copies:
            cp.start()
        for cp in copies:
            cp.wait()

    hbm = pl.BlockSpec(memory_space=pl.ANY)
    n_sem = n_big + N_DEV - 1
    return pl.pallas_call(
        body, name="swap_halves",
        in_specs=[hbm] * (n_big + 1), out_specs=[hbm] * (n_big + 1),
        out_shape=[jax.ShapeDtypeStruct(h.shape, F32) for h in halves] + [jax.ShapeDtypeStruct((N_DEV,) + vec.shape, F32)],
        scratch_shapes=[pltpu.SemaphoreType.DMA((n_sem,)), pltpu.SemaphoreType.DMA((n_sem,)), pltpu.SemaphoreType.DMA],
    )(*halves, vec)


def kernel(x, w_in, b_merge, conv_w, conv_b, rg_wx, rg_bx, rg_wa, rg_ba, rg_lambda, hg_lb_logits, hg_norm_g, proj_a, proj_b, w_out, norm_g, final_norm_g, loss_target, m_w_in, m_b_merge, m_conv_w, m_conv_b, m_rg_wx, m_rg_bx, m_rg_wa, m_rg_ba, m_rg_lambda, m_hg_lb_logits, m_hg_norm_g, m_proj_a, m_proj_b, m_w_out, m_norm_g, m_final_norm_g, v_w_in, v_b_merge, v_conv_w, v_conv_b, v_rg_wx, v_rg_bx, v_rg_wa, v_rg_ba, v_rg_lambda, v_hg_lb_logits, v_hg_norm_g, v_proj_a, v_proj_b, v_w_out, v_norm_g, v_final_norm_g):
    d = D_MODEL
    weights = dict(w_in=w_in, b_merge=b_merge, conv_w=conv_w, conv_b=conv_b, rg_wx=rg_wx, rg_bx=rg_bx, rg_wa=rg_wa,
                   rg_ba=rg_ba, rg_lambda=rg_lambda, hg_lb_logits=hg_lb_logits, hg_norm_g=hg_norm_g, proj_a=proj_a,
                   proj_b=proj_b, w_out=w_out, norm_g=norm_g, final_norm_g=final_norm_g)
    m = dict(w_in=m_w_in, b_merge=m_b_merge, conv_w=m_conv_w, conv_b=m_conv_b, rg_wx=m_rg_wx, rg_bx=m_rg_bx,
             rg_wa=m_rg_wa, rg_ba=m_rg_ba, rg_lambda=m_rg_lambda, hg_lb_logits=m_hg_lb_logits, hg_norm_g=m_hg_norm_g,
             proj_a=m_proj_a, proj_b=m_proj_b, w_out=m_w_out, norm_g=m_norm_g, final_norm_g=m_final_norm_g)
    v = dict(w_in=v_w_in, b_merge=v_b_merge, conv_w=v_conv_w, conv_b=v_conv_b, rg_wx=v_rg_wx, rg_bx=v_rg_bx,
             rg_wa=v_rg_wa, rg_ba=v_rg_ba, rg_lambda=v_rg_lambda, hg_lb_logits=v_hg_lb_logits, hg_norm_g=v_hg_norm_g,
             proj_a=v_proj_a, proj_b=v_proj_b, w_out=v_w_out, norm_g=v_norm_g, final_norm_g=v_final_norm_g)
    big_names = ("w_in", "proj_a", "proj_b", "w_out")

    core = lax.axis_index("c").astype(jnp.int32).reshape(1)
    chip = (2 * lax.axis_index("x") + lax.axis_index("y")).astype(jnp.int32)

    slotted = [_cast_into_slot(weights[k][0], chip.reshape(1), _MXU_DTYPE, f"cast_{k}") for k in big_names]
    conv_slotted = _cast_into_slot(conv_w[0], chip.reshape(1), F32, "slot_conv_w")

    small_shapes = {}

    place = jnp.concatenate([core, chip.reshape(1)])

    def reduce_proj_and_small(proj_grads, small_grads):
        small_shapes.update({k: t.shape for k, t in small_grads.items()})
        bigs = [g.reshape(N_SHARDS, d // N_SHARDS, d) for g in proj_grads]
        *landed, small_landed = _exchange_halves(bigs, _pack_small(small_grads))
        sums = [_add_half(g, l, place, f"add_half_{1 + a}") for a, (g, l) in enumerate(zip(bigs, landed))]
        return _Scatter([s[0] for s in sums], [s[1] for s in sums], _sum_slots(small_landed, "sum_small"))

    def reduce_w_in(d_w_in, landed):
        partial, own_slot = _add_half(d_w_in, landed, place, "add_half_0")
        return _Scatter([partial], [own_slot])

    loss_part, grad_x, d_norm_g, ((*by_chip_proj, small_all), by_chip_w_in) = _local_step(
        x, loss_target, (slotted, conv_slotted, chip.reshape(1)), (reduce_proj_and_small, reduce_w_in),
        b_merge, conv_b, rg_wx[0], rg_bx.reshape(1, d), rg_wa[0], rg_ba.reshape(1, d), rg_lambda, hg_lb_logits,
        hg_norm_g, norm_g, final_norm_g.reshape(1, d))
    mine = [_sum_slots(s, f"sum_chips_{a}") for a, s in enumerate(by_chip_w_in + by_chip_proj)]
    late = jnp.concatenate([d_norm_g.reshape(SUBLANES, LANES), jnp.full((SUBLANES, LANES), loss_part, F32)])
    *theirs, late_parts = _swap_halves(mine, late)
    late_sum = _sum_slots(late_parts, "sum_late")
    loss = late_sum[SUBLANES, 0]
    small_red = _unpack_small(small_all, {k: jax.ShapeDtypeStruct(s, F32) for k, s in small_shapes.items()})
    small_red["norm_g"] = late_sum[:SUBLANES].reshape(1, d)

    grads, delta, new_m, new_v = {}, {}, {}, {}
    for k, g_mine, g_theirs in zip(big_names, mine, theirs):
        out = _adamw_halves(weights[k][0], g_mine, g_theirs, m[k][0], v[k][0], core, f"adamw_{k}")
        grads[k], delta[k], new_m[k], new_v[k] = (t.reshape(weights[k].shape) for t in out)
    cols = d // N_SHARDS
    g_conv = lax.dynamic_slice(small_red["conv_w"], (0, chip * cols), (CONV_WIDTH, cols))
    grads["conv_w"] = g_conv.reshape(conv_w.shape)
    dl, nm, nv = _adamw(conv_w[0], g_conv, m_conv_w[0], v_conv_w[0], "adamw_conv_w")
    delta["conv_w"], new_m["conv_w"], new_v["conv_w"] = (t.reshape(conv_w.shape) for t in (dl, nm, nv))
    rest = [k for k in _SMALL_ORDER if k != "conv_w"]
    like = {k: (weights[k] if k != "conv_w" else jnp.zeros((CONV_WIDTH, d), F32)) for k in _SMALL_ORDER}
    packs = [_pack_small({k: (t[k] if k != "conv_w" else like[k]) for k in _SMALL_ORDER}) for t in (weights, m, v)]
    g_pack = _pack_small({k: small_red[k].reshape(like[k].shape) for k in _SMALL_ORDER})
    outs = [_unpack_small(p, like) for p in _adamw(packs[0], g_pack, packs[1], packs[2], "adamw_small")]
    for k in rest:
        grads[k] = small_red[k].reshape(weights[k].shape)
        delta[k], new_m[k], new_v[k] = outs[0][k], outs[1][k], outs[2][k]

    order = ("w_in", "b_merge", "conv_w", "conv_b", "rg_wx", "rg_bx", "rg_wa", "rg_ba", "rg_lambda", "hg_lb_logits",
             "hg_norm_g", "proj_a", "proj_b", "w_out", "norm_g", "final_norm_g")
    return (loss, grad_x, *[grads[k] for k in order], *[delta[k] for k in order], *[new_m[k] for k in order],
            *[new_v[k] for k in order])
```

```python
import functools

import jax
import jax.numpy as jnp
from jax import lax
from jax.experimental import pallas as pl
from jax.experimental.pallas import tpu as pltpu

F32 = jnp.float32
_MXU_DTYPE = jnp.bfloat16

D_MODEL = 1024
LANES = 128
SUBLANES = 8
N_BLK = D_MODEL // LANES
N_GROUPS = 8
N_SHARDS = 4
CONV_WIDTH = 4
LRU_C = 8.0
CHUNK = 64
CHUNKS_IN_FLIGHT = 16
FWD_CHUNKS_IN_FLIGHT = 32
HG_SCALE = float(LANES) ** -0.5
EPS = 1e-6
ADAM_LR, ADAM_B1, ADAM_B2, ADAM_EPS, ADAM_WD, ADAM_STEP = 0.001, 0.9, 0.999, 1e-08, 0.01, 10
MATMUL_TOKENS = 512
TAIL_TOKENS = 256
CONTRACT_TOKENS = 2048
VMEM_LIMIT = 56 * 1024 * 1024
VMEM_LIMIT_BIG = 60 * 1024 * 1024
MESH = pl.DeviceIdType.MESH

_SLOT_TO_GROUP = (2, 3, 4, 5, 0, 1, 6, 7)


def _mm(a, b):
    return lax.dot_general(a.astype(_MXU_DTYPE), b.astype(_MXU_DTYPE), (((1,), (0,)), ((), ())),
                           preferred_element_type=F32)


def _mm_nt(a, b):
    return lax.dot_general(a.astype(_MXU_DTYPE), b.astype(_MXU_DTYPE), (((1,), (1,)), ((), ())),
                           preferred_element_type=F32)


def _mm_tn(a, b):
    return lax.dot_general(a.astype(_MXU_DTYPE), b.astype(_MXU_DTYPE), (((0,), (0,)), ((), ())),
                           preferred_element_type=F32)


def _sigmoid(x):
    return 0.5 * jnp.tanh(0.5 * x) + 0.5


def _log1p_pos(y):
    series = y * (1.0 - y * (0.5 - y * (1.0 / 3.0 - y * 0.25)))
    return jnp.where(y < 0.01, series, jnp.log(1.0 + y))


def _softplus(x):
    return jnp.maximum(x, 0.0) + _log1p_pos(jnp.exp(-jnp.abs(x)))


def _shift_down(x, n):
    rolled = pltpu.roll(x, n, 0)
    edge = SUBLANES if (n < SUBLANES and x.shape[0] > SUBLANES) else x.shape[0]
    rows = lax.broadcasted_iota(jnp.int32, (edge, x.shape[1]), 0)
    head = jnp.where(rows >= n, rolled[:edge], 0.0)
    return head if edge == x.shape[0] else jnp.concatenate([head, rolled[edge:]], axis=0)


def _shift_up(x, n):
    size = x.shape[0]
    rolled = pltpu.roll(x, size - n, 0)
    edge = SUBLANES if (n < SUBLANES and size > SUBLANES) else size
    rows = lax.broadcasted_iota(jnp.int32, (edge, x.shape[1]), 0)
    tail = jnp.where(rows < edge - n, rolled[size - edge:], 0.0)
    return tail if edge == size else jnp.concatenate([rolled[:size - edge], tail], axis=0)


def _params(dims, vmem=VMEM_LIMIT):
    return pltpu.CompilerParams(dimension_semantics=dims, vmem_limit_bytes=vmem)


def _slot_of_group(g):
    return jnp.where(g < 2, g + 4, jnp.where(g < 6, g - 2, g))


def _inproj_fwd_gather(x2d, norm_g, slotted, conv_slotted, chip):
    tokens, d = x2d.shape
    tm = min(MATMUL_TOKENS, tokens)
    n_tiles = tokens // tm
    n_big = len(slotted)
    n_sem = 6 * (n_big + 1) + 3
    last_pass = N_GROUPS - 1

    def shard_of(k, chip_id):
        x, y = chip_id // 2, chip_id % 2
        return 2 * jnp.where(k % 2 == 1, 1 - x, x) + jnp.where(k // 2 == 1, 1 - y, y)

    def body(chip_ref, x_ref, g_ref, *rest):
        bufs, cw = rest[n_big + 1:2 * n_big + 1], rest[2 * n_big + 1]
        z_ref, ht_ref = rest[2 * n_big + 2:2 * n_big + 4]
        h_all, slab, send_sems, recv_sems, slab_sems = rest[2 * n_big + 4:]
        del chip_ref
        p, i = pl.program_id(0), pl.program_id(1)
        x, y, c, chips = _mesh_position()
        me, sibling = 2 * x + y, (x, y, 1 - c)

        pieces = [(0, 0), (0, 1)] + [(a, None) for a in range(1, n_big)]

        def half(piece, slot, which):
            a, q = pieces[piece]
            hs = bufs[a].shape[1] // 2
            cols = slice(None) if q is None else pl.ds(q * D_MODEL, D_MODEL)
            return bufs[a].at[slot, pl.ds(which * hs, hs), cols]

        def send(piece, j):
            mine = half(piece, me, c)
            return _remote(mine, mine, send_sems, recv_sems, 6 * piece + j, (chips[j][0], chips[j][1], c))

        def arrival(piece, j):
            landed = half(piece, 2 * chips[j][0] + chips[j][1], c)
            return _remote(landed, landed, send_sems, recv_sems, 6 * piece + j, (chips[j][0], chips[j][1], c))

        def passed_on(piece, j, which):
            landed = half(piece, 2 * chips[j][0] + chips[j][1], which)
            return _remote(landed, landed, send_sems, recv_sems, 6 * piece + 3 + j, sibling)

        def conv_copy(j, slot):
            return _remote(cw.at[slot], cw.at[slot], send_sems, recv_sems, 6 * len(pieces) + j,
                           (chips[j][0], chips[j][1], c))

        def land(piece, j):
            arrival(piece, j).wait_recv()
            passed_on(piece, j, c).start()
            passed_on(piece, j, 1 - c).wait_recv()

        def slab_copy(pv):
            src = bufs[0].at[shard_of(pv // 2, me), :, pl.ds((pv % 2) * D_MODEL, D_MODEL)]
            return pltpu.make_async_copy(src, slab.at[pv % 2], slab_sems.at[pv % 2])

        @pl.when((p == 0) & (i == 0))
        def _():
            for q in range(2):
                send(q, 0).start()
                send(q, 1).start()
            slab_copy(0).start()

        @pl.when(i == 0)
        def _():
            for pv in range(N_GROUPS):
                @pl.when(p == pv)
                def _(pv=pv):
                    slab_copy(pv).wait()

        rows = pl.ds(pl.multiple_of(i * tm, tm), tm)

        @pl.when(p == 0)
        def _():
            xt = x_ref[...]
            r = lax.rsqrt(jnp.mean(xt * xt, axis=-1, keepdims=True) + EPS)
            h = (xt * r) * g_ref[...]
            h_all[rows, :] = h.astype(_MXU_DTYPE)
            ht_ref[...] = jnp.transpose(h).astype(_MXU_DTYPE)

        z_ref[...] = _mm(h_all[rows, :], slab[p % 2])

        def relay(q):
            landed = half(q, 2 * chips[q][0] + chips[q][1], c)
            to = chips[1 - q]
            return _remote(landed, landed, send_sems, recv_sems, 6 * q + 2, (to[0], to[1], c))

        landings = {1: [(0, 0)], 2: [(1, 0), (1, 1)], 3: [(0, 1)], 5: [(0, 2)], 6: [(1, 2)]}

        def end_of_pass(pv):
            for q, j in landings.get(pv - 1, []):
                land(q, j)
                if j == q:
                    relay(q).start()
            if pv - 1 == 1:
                for piece in range(2, len(pieces)):
                    for jj in range(3):
                        send(piece, jj).start()
                for jj in range(3):
                    conv_copy(jj, me).start()
            if pv - 1 in (5, 6):
                for piece in range(2, len(pieces)):
                    for jj in ((0, 1) if pv - 1 == 5 else (2,)):
                        land(piece, jj)
            slab_copy(pv).start()

        @pl.when(i == n_tiles - 1)
        def _():
            for pv in range(1, N_GROUPS):
                pl.when(p == pv - 1)(functools.partial(end_of_pass, pv))

        @pl.when((p == last_pass) & (i == n_tiles - 1))
        def _():
            for j in range(3):
                conv_copy(j, 2 * chips[j][0] + chips[j][1]).wait_recv()
            for piece in range(len(pieces)):
                for j in range(3):
                    (relay(piece) if (piece < 2 and j == 2) else send(piece, j)).wait_send()
                    passed_on(piece, j, c).wait_send()
            for j in range(3):
                conv_copy(j, me).wait_send()

    def z_index(p, i, chip_ref):
        g = 2 * shard_of(p // 2, chip_ref[0]) + p % 2
        return (_slot_of_group(g), i, 0)

    def first_pass_tile(p, i, chip_ref):
        return jnp.where(p == 0, i, n_tiles - 1)

    hbm = pl.BlockSpec(memory_space=pl.ANY)
    operands = list(slotted) + [conv_slotted]
    grid_spec = pltpu.PrefetchScalarGridSpec(
        num_scalar_prefetch=1, grid=(N_GROUPS, n_tiles),
        in_specs=[pl.BlockSpec((tm, d), lambda p, i, chip_ref: (first_pass_tile(p, i, chip_ref), 0)),
                  pl.BlockSpec((1, d), lambda p, i, chip_ref: (0, 0))] + [hbm] * (n_big + 1),
        out_specs=[hbm] * (n_big + 1) + [pl.BlockSpec((None, tm, D_MODEL), z_index),
                                         pl.BlockSpec((d, tm), lambda p, i, chip_ref: (0, first_pass_tile(p, i, chip_ref)))],
        scratch_shapes=[pltpu.VMEM((tokens, d), _MXU_DTYPE), pltpu.VMEM((2, d, D_MODEL), _MXU_DTYPE),
                        pltpu.SemaphoreType.DMA((n_sem,)), pltpu.SemaphoreType.DMA((n_sem,)),
                        pltpu.SemaphoreType.DMA((2,))])
    out = pl.pallas_call(
        body, name="inproj_fwd_gather", grid_spec=grid_spec,
        out_shape=[jax.ShapeDtypeStruct(a.shape, a.dtype) for a in operands]
        + [jax.ShapeDtypeStruct((N_GROUPS, tokens, D_MODEL), F32), jax.ShapeDtypeStruct((d, tokens), _MXU_DTYPE)],
        input_output_aliases={3 + a: a for a in range(n_big + 1)},
        compiler_params=_params(("arbitrary", "arbitrary")),
    )(chip, x2d, norm_g, *operands)
    return out[n_big + 1], out[n_big + 2], out[:n_big], out[n_big]


def _lane_blocks(x):
    return [x[:, k * LANES:(k + 1) * LANES] for k in range(x.shape[1] // LANES)]


def _block_diag(x, w_ref, transposed=False):
    mm = _mm_nt if transposed else _mm
    return jnp.concatenate([mm(xk, w_ref[k]) for k, xk in enumerate(_lane_blocks(x))], axis=1)


def _lru_decay(gr, sp):
    log_a = (-LRU_C) * gr * sp
    a = jnp.exp(log_a)
    y = 2.0 * log_a
    mult_sq = jnp.where(y > -1e-3, -y * (1.0 + 0.5 * y), 1.0 - a * a)
    inv_mult = lax.rsqrt(jnp.maximum(mult_sq, 1e-37))
    return a, mult_sq * inv_mult, inv_mult


def _tile_rows(width):
    return lax.broadcasted_iota(jnp.int32, (SUBLANES, width), 0)


def _scan_forward(a_scr, u_scr, h_scr, seq):
    width = a_scr.shape[1]
    rows = _tile_rows(width)

    def tile(j, carry):
        sl = pl.ds(pl.multiple_of(j * SUBLANES, SUBLANES), SUBLANES)
        a = a_scr[sl, :]
        u = u_scr[sl, :]
        for d in (1, 2, 4):
            keep = rows >= d
            a_sh = jnp.where(keep, pltpu.roll(a, d, 0), 1.0)
            u_sh = jnp.where(keep, pltpu.roll(u, d, 0), 0.0)
            u = a * u_sh + u
            a = a * a_sh
        h = u + a * carry
        h_scr[sl, :] = h
        return jnp.broadcast_to(h[SUBLANES - 1:SUBLANES, :], (SUBLANES, width))

    lax.fori_loop(0, seq // SUBLANES, tile, jnp.zeros((SUBLANES, width), F32))


def _scan_backward(c_scr, d_scr, g_scr, seq):
    width = c_scr.shape[1]
    rows = _tile_rows(width)
    n_tiles = seq // SUBLANES

    def tile(jj, carry):
        j = n_tiles - 1 - jj
        sl = pl.ds(pl.multiple_of(j * SUBLANES, SUBLANES), SUBLANES)
        c = c_scr[sl, :]
        g = d_scr[sl, :]
        for d in (1, 2, 4):
            keep = rows < SUBLANES - d
            c_sh = jnp.where(keep, pltpu.roll(c, SUBLANES - d, 0), 1.0)
            g_sh = jnp.where(keep, pltpu.roll(g, SUBLANES - d, 0), 0.0)
            g = c * g_sh + g
            c = c * c_sh
        g = g + c * carry
        g_scr[sl, :] = g
        return jnp.broadcast_to(g[0:1, :], (SUBLANES, width))

    lax.fori_loop(0, n_tiles, tile, jnp.zeros((SUBLANES, width), F32))


LRU_BLOCKS_PER_STEP = 2
LRU_LANES = LRU_BLOCKS_PER_STEP * LANES
LRU_STEPS = N_BLK // LRU_BLOCKS_PER_STEP


def _branch_a_fwd(z, conv_w, conv_b, wx, bx, wa, ba, lam, batch, seq):
    tokens = batch * seq

    def body(z_ref, cw_ref, cb_ref, wx_ref, bx_ref, wa_ref, ba_ref, lam_ref, ya_ref, hl_ref, kept_ref, a_scr, u_scr):
        xa = z_ref[0]
        ga = z_ref[1]
        xc = (cb_ref[...] + cw_ref[3:4, :] * xa + cw_ref[2:3, :] * _shift_down(xa, 1)
              + cw_ref[1:2, :] * _shift_down(xa, 2) + cw_ref[0:1, :] * _shift_down(xa, 3))
        gi = _sigmoid(_block_diag(xc, wx_ref) + bx_ref[...])
        gr = _sigmoid(_block_diag(xc, wa_ref) + ba_ref[...])
        a, mult, _ = _lru_decay(gr, _softplus(-lam_ref[...]))
        kept_ref[0], kept_ref[1], kept_ref[2] = xc, gi, gr
        a_scr[...] = a
        u_scr[...] = mult * gi * xc
        _scan_forward(a_scr, u_scr, hl_ref, seq)
        ya_ref[...] = (hl_ref[...] * (ga * _sigmoid(ga))).astype(_MXU_DTYPE)

    blk = pl.BlockSpec((seq, LRU_LANES), lambda b, c: (b, c))
    vec = pl.BlockSpec((1, LRU_LANES), lambda b, c: (0, c))
    mat = pl.BlockSpec((LRU_BLOCKS_PER_STEP, LANES, LANES), lambda b, c: (c, 0, 0))
    return pl.pallas_call(
        body, name="branch_a_fwd",
        grid=(batch, LRU_STEPS),
        in_specs=[pl.BlockSpec((2, seq, LRU_LANES), lambda b, c: (2, b, c)),
                  pl.BlockSpec((CONV_WIDTH, LRU_LANES), lambda b, c: (0, c)), vec, mat, vec, mat, vec, vec],
        out_specs=[blk, blk, pl.BlockSpec((3, seq, LRU_LANES), lambda b, c: (0, b, c))],
        out_shape=[jax.ShapeDtypeStruct((tokens, D_MODEL), _MXU_DTYPE), jax.ShapeDtypeStruct((tokens, D_MODEL), F32),
                   jax.ShapeDtypeStruct((3, tokens, D_MODEL), F32)],
        scratch_shapes=[pltpu.VMEM((seq, LRU_LANES), F32), pltpu.VMEM((seq, LRU_LANES), F32)],
        compiler_params=_params(("parallel", "parallel")),
    )(z, conv_w, conv_b, wx, bx, wa, ba, lam)


def _branch_a_bwd(z, hl, kept, dya, dz, conv_w, wx, wa, lam, batch, seq):
    def body(z_ref, hl_ref, kept_ref, dya_ref, dz_in_ref, cw_ref, wx_ref, wa_ref, lam_ref,
             dz_ref, dcw_ref, dcb_ref, dwx_ref, dbx_ref, dwa_ref, dba_ref, dlam_ref, c_scr, d_scr):
        del dz_in_ref
        g_scr = d_scr
        xa = z_ref[0]
        ga = z_ref[1]
        hl = hl_ref[...]
        dya = dya_ref[...]
        xc, gi, gr = kept_ref[0], kept_ref[1], kept_ref[2]
        sp = _softplus(-lam_ref[...])
        a, mult, inv_mult = _lru_decay(gr, sp)
        sga = _sigmoid(ga)
        dz_ref[1] = (dya * hl * (sga * (1.0 + ga * (1.0 - sga)))).astype(_MXU_DTYPE)
        c_scr[...] = _shift_up(a, 1)
        d_scr[...] = dya * (ga * sga)
        _scan_backward(c_scr, d_scr, g_scr, seq)
        g = g_scr[...]
        da = g * _shift_down(hl, 1)
        dmult = g * gi * xc
        dgi = g * mult * xc
        dxc = g * mult * gi
        dlog_a = da * a - dmult * (a * a) * inv_mult
        dgr = dlog_a * (-LRU_C) * sp
        dsp = jnp.sum(dlog_a * gr, axis=0, keepdims=True) * (-LRU_C)
        dlam = -dsp * _sigmoid(-lam_ref[...])
        dpi = dgi * gi * (1.0 - gi)
        dpr = dgr * gr * (1.0 - gr)
        dxc = dxc + _block_diag(dpi, wx_ref, transposed=True) + _block_diag(dpr, wa_ref, transposed=True)
        dwx = jnp.stack([_mm_tn(xk, dk) for xk, dk in zip(_lane_blocks(xc), _lane_blocks(dpi))])
        dwa = jnp.stack([_mm_tn(xk, dk) for xk, dk in zip(_lane_blocks(xc), _lane_blocks(dpr))])
        dbx = jnp.sum(dpi, axis=0, keepdims=True)
        dba = jnp.sum(dpr, axis=0, keepdims=True)
        ahead = [dxc if k == CONV_WIDTH - 1 else _shift_up(dxc, CONV_WIDTH - 1 - k) for k in range(CONV_WIDTH)]
        dxa = sum(cw_ref[k:k + 1, :] * ahead[k] for k in range(CONV_WIDTH))
        dz_ref[0] = dxa.astype(_MXU_DTYPE)
        dcb = jnp.sum(dxc, axis=0, keepdims=True)
        dcw = [jnp.sum(ahead[k] * xa, axis=0, keepdims=True) for k in range(CONV_WIDTH)]

        @pl.when(pl.program_id(1) == 0)
        def _():
            for k in range(CONV_WIDTH):
                dcw_ref[k:k + 1, :] = dcw[k]
            dcb_ref[...] = dcb
            dwx_ref[...] = dwx
            dbx_ref[...] = dbx
            dwa_ref[...] = dwa
            dba_ref[...] = dba
            dlam_ref[...] = dlam

        @pl.when(pl.program_id(1) != 0)
        def _():
            for k in range(CONV_WIDTH):
                dcw_ref[k:k + 1, :] += dcw[k]
            dcb_ref[...] += dcb
            dwx_ref[...] += dwx
            dbx_ref[...] += dbx
            dwa_ref[...] += dwa
            dba_ref[...] += dba
            dlam_ref[...] += dlam

    tokens = batch * seq
    blk = pl.BlockSpec((seq, LRU_LANES), lambda c, b: (b, c))
    vec = pl.BlockSpec((1, LRU_LANES), lambda c, b: (0, c))
    mat = pl.BlockSpec((LRU_BLOCKS_PER_STEP, LANES, LANES), lambda c, b: (c, 0, 0))
    vec_shape = jax.ShapeDtypeStruct((1, D_MODEL), F32)
    mat_shape = jax.ShapeDtypeStruct((N_BLK, LANES, LANES), F32)
    return pl.pallas_call(
        body, name="branch_a_bwd",
        grid=(LRU_STEPS, batch),
        in_specs=[pl.BlockSpec((2, seq, LRU_LANES), lambda c, b: (2, b, c)), blk,
                  pl.BlockSpec((3, seq, LRU_LANES), lambda c, b: (0, b, c)), blk,
                  pl.BlockSpec(memory_space=pl.ANY),
                  pl.BlockSpec((CONV_WIDTH, LRU_LANES), lambda c, b: (0, c)), mat, mat, vec],
        out_specs=[pl.BlockSpec((2, seq, LRU_LANES), lambda c, b: (2, b, c)),
                   pl.BlockSpec((CONV_WIDTH, LRU_LANES), lambda c, b: (0, c)), vec, mat, vec, mat, vec, vec],
        out_shape=[jax.ShapeDtypeStruct((N_GROUPS, tokens, D_MODEL), _MXU_DTYPE),
                   jax.ShapeDtypeStruct((CONV_WIDTH, D_MODEL), F32), vec_shape, mat_shape, vec_shape, mat_shape,
                   vec_shape, vec_shape],
        scratch_shapes=[pltpu.VMEM((seq, LRU_LANES), F32)] * 2,
        input_output_aliases={4: 0},
        compiler_params=_params(("parallel", "arbitrary"), vmem=VMEM_LIMIT_BIG),
    )(z, hl, kept, dya, dz, conv_w, wx, wa, lam)


def _chunk_masks(transposed=False):
    r = lax.broadcasted_iota(jnp.int32, (CHUNK, CHUNK), 0)
    c = lax.broadcasted_iota(jnp.int32, (CHUNK, CHUNK), 1)
    return r <= c if transposed else r >= c


def _row_blocks(seq, fn):
    block = min(256, seq)

    def trip(i, carry):
        fn(pl.ds(pl.multiple_of(i * block, block), block))
        return carry

    lax.fori_loop(0, seq // block, trip, 0)


def _hgrn_prepare(z_ref, lb_ref, f_scr, logf_scr, qh_scr, seq):
    lb = _sigmoid(lb_ref[0:1, :] - lb_ref[1:2, :])

    def block(rows):
        q = z_ref[0, rows, :]
        f = lb + (1.0 - lb) * _sigmoid(z_ref[1, rows, :])
        f_scr[rows, :] = f
        logf_scr[rows, :] = jnp.log(f)
        qh_scr[rows, :] = q * _sigmoid(q)

    _row_blocks(seq, block)
    return lb


def _cumsum_rows(x, reverse=False):
    shift = _shift_up if reverse else _shift_down
    d = 1
    while d < x.shape[0]:
        x = x + shift(x, d)
        d *= 2
    return x


def _lane_mean(x):
    return jnp.mean(x, axis=-1, keepdims=True)


def _token_contractions(lhs_scr, rhs_scr, out_ref, seq):
    rows_id = lax.broadcasted_iota(jnp.int32, (LANES, LANES), 0)

    def transposed(p):
        rows = pl.ds(pl.multiple_of(p * LANES, LANES), LANES)
        return jnp.transpose(lhs_scr[rows, :]).astype(_MXU_DTYPE), rhs_scr[rows, :]

    def contract(p, s):
        lhs_t, rhs = s
        return (_mm(lhs_t, jnp.where(rows_id < CHUNK, rhs, 0.0)), _mm(lhs_t, jnp.where(rows_id >= CHUNK, rhs, 0.0)))

    def store(p, out):
        out_ref[2 * p] = out[0]
        out_ref[2 * p + 1] = out[1]

    _independent_trips(seq // LANES, [transposed, contract], store)


def _chunk_rows(c):
    return pl.ds(pl.multiple_of(c * CHUNK, CHUNK), CHUNK)


def _chunk_terms(c, z_ref, f_scr, qh_scr, b_scr):
    rows = _chunk_rows(c)
    b = b_scr[rows, :]
    b_mid = b_scr[pl.ds(c * CHUNK + CHUNK // 2, 1), :]
    b_last = b_scr[pl.ds(c * CHUNK + CHUNK - 1, 1), :]
    qh = qh_scr[rows, :]
    k = 1.0 - f_scr[rows, :]
    v = z_ref[2, rows, :]
    e_q = jnp.exp(b - b_mid) * HG_SCALE
    e_k = jnp.exp(b_mid - b)
    e_qi = jnp.exp(b) * HG_SCALE
    e_ks = jnp.exp(b_last - b)
    decay = jnp.exp(b_last)
    return rows, qh, k, v, e_q, e_k, e_qi, e_ks, decay


def _independent_trips(n, stages, store, group=CHUNKS_IN_FLIGHT):
    stages = stages if isinstance(stages, (list, tuple)) else [stages]
    group = min(group, n)

    def trip(g, carry):
        ids = [g * group + i for i in range(group)]
        state = [stages[0](c) for c in ids]
        for stage in stages[1:]:
            state = [stage(c, s) for c, s in zip(ids, state)]
        for c, s in zip(ids, state):
            store(c, s)
        return carry

    lax.fori_loop(0, n // group, trip, 0)


def _branch_b_fwd(z, lb_logits, hg_g, batch, seq):
    tokens = batch * seq
    n_chunks = seq // CHUNK

    def body(z_ref, lb_ref, g_ref, yb_ref, st_ref, kept_ref, logf_scr, o_scr, qi_scr, ks_scr, dec_scr):
        f_scr, qh_scr, b_scr, o_kept = (kept_ref.at[k] for k in range(4))
        _hgrn_prepare(z_ref, lb_ref, f_scr, logf_scr, qh_scr, seq)
        causal = _chunk_masks()
        gain = g_ref[...]

        def cumulate(c):
            return _cumsum_rows(logf_scr[_chunk_rows(c), :])

        def store_cumulated(c, b):
            b_scr[_chunk_rows(c), :] = b

        def scores(c):
            _, qh, k, v, e_q, e_k, e_qi, e_ks, decay = _chunk_terms(c, z_ref, f_scr, qh_scr, b_scr)
            return _mm_nt(qh * e_q, k * e_k), v, qh * e_qi, k * e_ks, decay

        def within_chunk(c, s):
            att, v, q_int, k_st, decay = s
            return _mm(jnp.where(causal, att, 0.0), v), q_int, k_st, decay

        def store_within_chunk(c, out):
            rows = _chunk_rows(c)
            o_scr[rows, :], qi_scr[rows, :], ks_scr[rows, :], dec_scr[pl.ds(c, 1), :] = out

        def carry_state(c, state_t):
            update = st_ref[c]
            st_ref[c] = state_t
            return state_t * dec_scr[pl.ds(c, 1), :] + update

        def finish(c):
            rows = _chunk_rows(c)
            o = o_scr[rows, :] + _mm_nt(qi_scr[rows, :], st_ref[c])
            r = lax.rsqrt(_lane_mean(o * o) + EPS)
            gb = z_ref[3, rows, :]
            return (((o * r) * gain) * (gb * _sigmoid(gb))).astype(_MXU_DTYPE), o

        def store_finished(c, out):
            yb_ref[_chunk_rows(c), :], o_kept[_chunk_rows(c), :] = out

        _independent_trips(n_chunks, cumulate, store_cumulated, FWD_CHUNKS_IN_FLIGHT)
        _independent_trips(n_chunks, [scores, within_chunk], store_within_chunk, FWD_CHUNKS_IN_FLIGHT)
        _token_contractions(z_ref.at[2], ks_scr, st_ref, seq)
        lax.fori_loop(0, n_chunks, carry_state, jnp.zeros((LANES, LANES), F32))
        _independent_trips(n_chunks, finish, store_finished, FWD_CHUNKS_IN_FLIGHT)

    seq_buf = pltpu.VMEM((seq, LANES), F32)
    return pl.pallas_call(
        body, name="branch_b_fwd",
        grid=(batch, N_BLK),
        in_specs=[pl.BlockSpec((4, seq, LANES), lambda b, h: (0, b, h)),
                  pl.BlockSpec((2, LANES), lambda b, h: (0, h)),
                  pl.BlockSpec((1, LANES), lambda b, h: (0, 0))],
        out_specs=[pl.BlockSpec((seq, LANES), lambda b, h: (b, h)),
                   pl.BlockSpec((None, n_chunks, LANES, LANES), lambda b, h: (b * N_BLK + h, 0, 0, 0)),
                   pl.BlockSpec((4, seq, LANES), lambda b, h: (0, b, h))],
        out_shape=[jax.ShapeDtypeStruct((tokens, D_MODEL), _MXU_DTYPE),
                   jax.ShapeDtypeStruct((batch * N_BLK, n_chunks, LANES, LANES), F32),
                   jax.ShapeDtypeStruct((4, tokens, D_MODEL), F32)],
        scratch_shapes=[seq_buf] * 4 + [pltpu.VMEM((n_chunks, LANES), F32)],
        compiler_params=_params(("parallel", "parallel")),
    )(z, lb_logits, hg_g)


def _branch_b_bwd(z, states, kept, dyb, dz, lb_logits, hg_g, batch, seq, halves):
    n_chunks = seq // CHUNK
    n_half = len(halves)

    def body(z_ref, st_ref, kept_ref, dyb_ref, dz_in_ref, lb_ref, g_ref, *rest):
        del dz_in_ref
        half_srcs = rest[:n_half]
        dz_ref, dlog_ref, dg_ref = rest[n_half:n_half + 3]
        half_outs = rest[n_half + 3:2 * n_half + 3]
        (do_scr, qi_scr, dqh_scr, df_scr, dec_scr, dgp_scr, dlb_scr, dst_scr,
         send_sems, recv_sems) = rest[2 * n_half + 3:]
        f_scr, qh_scr, b_scr, o_kept = (kept_ref.at[k] for k in range(4))
        first = (pl.program_id(0) == 0) & (pl.program_id(1) == 0)
        last = (pl.program_id(0) == N_BLK - 1) & (pl.program_id(1) == batch - 1)
        mx, my, mc, _ = _mesh_position()
        carried = [_remote(src.at[:, pl.ds((1 - mc) * (src.shape[1] // 2), src.shape[1] // 2), :], out,
                           send_sems, recv_sems, a, (mx, my, 1 - mc))
                   for a, (src, out) in enumerate(zip(half_srcs, half_outs))]

        @pl.when(first)
        def _():
            for cp in carried:
                cp.start()
        lb = _sigmoid(lb_ref[0:1, :] - lb_ref[1:2, :])
        causal = _chunk_masks()
        anti_causal = _chunk_masks(transposed=True)
        gain = g_ref[...]

        @pl.when(first)
        def _():
            dg_ref[...] = jnp.zeros_like(dg_ref)

        @pl.when(pl.program_id(1) == 0)
        def _():
            dlb_scr[...] = jnp.zeros_like(dlb_scr)

        def output_gradient(c):
            rows = _chunk_rows(c)
            b = b_scr[rows, :]
            q_int = qh_scr[rows, :] * (jnp.exp(b) * HG_SCALE)
            decay = jnp.exp(b_scr[pl.ds(c * CHUNK + CHUNK - 1, 1), :])
            o = o_kept[rows, :]
            r = lax.rsqrt(_lane_mean(o * o) + EPS)
            o_n = o * r
            gb = z_ref[3, rows, :]
            sgb = _sigmoid(gb)
            dyb_c = dyb_ref[rows, :]
            d_ong = dyb_c * (gb * sgb)
            d_gb = (dyb_c * (o_n * gain) * (sgb * (1.0 + gb * (1.0 - sgb)))).astype(_MXU_DTYPE)
            d_gain = jnp.sum(d_ong * o_n, axis=0, keepdims=True)
            d_on = d_ong * gain
            return d_gb, d_gain, r * (d_on - o_n * _lane_mean(d_on * o_n)), q_int, decay

        def store_output_gradient(c, out):
            rows = _chunk_rows(c)
            dz_ref[3, rows, :], dgp_scr[pl.ds(c, 1), :], do_scr[rows, :], qi_scr[rows, :], dec_scr[pl.ds(c, 1), :] = out

        def carry_state_gradient(cc, d_state_t):
            c = n_chunks - 1 - cc
            update = dst_scr[c]
            dst_scr[c] = d_state_t
            return d_state_t * dec_scr[pl.ds(c, 1), :] + update

        def score_gradients(c):
            rows, qh, k, v, e_q, e_k, e_qi, e_ks, decay = _chunk_terms(c, z_ref, f_scr, qh_scr, b_scr)
            state_t = st_ref[c]
            d_state_t = dst_scr[c]
            d_o = do_scr[rows, :]
            q_in, k_in, q_int, k_st = qh * e_q, k * e_k, qh * e_qi, k * e_ks
            first = (_mm_nt(k_in, q_in), _mm_nt(d_o, v), _mm_nt(v, d_o), _mm_nt(k_st, d_state_t), _mm(d_o, state_t),
                     _mm(v, d_state_t))
            d_decay = jnp.sum(state_t * d_state_t, axis=0, keepdims=True)
            return first, d_o, q_in, k_in, q_int, k_st, e_q, e_k, e_qi, e_ks, decay, d_decay

        def input_gradients(c, s):
            (att_t, d_att, d_att_t, dv_inter, dq_int, dk_st), d_o, q_in, k_in, q_int, k_st, e_q, e_k, e_qi, e_ks, decay, d_decay = s
            rows = _chunk_rows(c)
            d_v = _mm(jnp.where(anti_causal, att_t, 0.0), d_o) + dv_inter
            dq_in = _mm(jnp.where(causal, d_att, 0.0), k_in)
            dk_in = _mm(jnp.where(anti_causal, d_att_t, 0.0), q_in)
            d_k = dk_in * e_k + dk_st * e_ks
            kk = dk_st * k_st
            d_b = dq_in * q_in + dq_int * q_int - dk_in * k_in - kk
            d_b_last = jnp.sum(kk, axis=0, keepdims=True) + decay * d_decay
            d_logf = _cumsum_rows(d_b, reverse=True) + d_b_last
            return d_v.astype(_MXU_DTYPE), dq_in * e_q + dq_int * e_qi, d_logf / f_scr[rows, :] - d_k

        def store_input_gradients(c, out):
            rows = _chunk_rows(c)
            dz_ref[2, rows, :], dqh_scr[rows, :], df_scr[rows, :] = out

        def input_activations(rows):
            q = z_ref[0, rows, :]
            sq = _sigmoid(q)
            dz_ref[0, rows, :] = (dqh_scr[rows, :] * (sq * (1.0 + q * (1.0 - sq)))).astype(_MXU_DTYPE)
            sg = _sigmoid(z_ref[1, rows, :])
            d_f = df_scr[rows, :]
            dz_ref[1, rows, :] = (d_f * (1.0 - lb) * sg * (1.0 - sg)).astype(_MXU_DTYPE)
            dlb_scr[...] += jnp.sum(d_f * (1.0 - sg), axis=0, keepdims=True)

        _independent_trips(n_chunks, output_gradient, store_output_gradient)
        _token_contractions(do_scr, qi_scr, dst_scr, seq)
        lax.fori_loop(0, n_chunks, carry_state_gradient, jnp.zeros((LANES, LANES), F32))
        _independent_trips(n_chunks, [score_gradients, input_gradients], store_input_gradients)
        dg_ref[...] += jnp.sum(dgp_scr[...], axis=0, keepdims=True)
        _row_blocks(seq, input_activations)
        d_l0 = dlb_scr[...] * lb * (1.0 - lb)
        dlog_ref[0:1, :] = d_l0
        dlog_ref[1:2, :] = -d_l0

        @pl.when(last)
        def _():
            for cp in carried:
                cp.wait()

    tokens = batch * seq
    seq_buf = pltpu.VMEM((seq, LANES), F32)
    chunk_rows = pltpu.VMEM((n_chunks, LANES), F32)
    hbm = pl.BlockSpec(memory_space=pl.ANY)
    return pl.pallas_call(
        body, name="branch_b_bwd",
        grid=(N_BLK, batch),
        in_specs=[pl.BlockSpec((4, seq, LANES), lambda h, b: (0, b, h)),
                  pl.BlockSpec((None, n_chunks, LANES, LANES), lambda h, b: (b * N_BLK + h, 0, 0, 0)),
                  pl.BlockSpec((4, seq, LANES), lambda h, b: (0, b, h)),
                  pl.BlockSpec((seq, LANES), lambda h, b: (b, h)),
                  hbm,
                  pl.BlockSpec((2, LANES), lambda h, b: (0, h)),
                  pl.BlockSpec((1, LANES), lambda h, b: (0, 0))] + [hbm] * n_half,
        out_specs=[pl.BlockSpec((4, seq, LANES), lambda h, b: (0, b, h)),
                   pl.BlockSpec((2, LANES), lambda h, b: (0, h)),
                   pl.BlockSpec((1, LANES), lambda h, b: (0, 0))] + [hbm] * n_half,
        out_shape=[jax.ShapeDtypeStruct((N_GROUPS, tokens, D_MODEL), _MXU_DTYPE),
                   jax.ShapeDtypeStruct((2, D_MODEL), F32),
                   jax.ShapeDtypeStruct((1, LANES), F32)]
        + [jax.ShapeDtypeStruct((g.shape[0], g.shape[1] // 2, g.shape[2]), F32) for g in halves],
        scratch_shapes=[seq_buf] * 4 + [chunk_rows, chunk_rows, pltpu.VMEM((1, LANES), F32),
                                        pltpu.VMEM((n_chunks, LANES, LANES), F32),
                                        pltpu.SemaphoreType.DMA((n_half,)), pltpu.SemaphoreType.DMA((n_half,))],
        input_output_aliases={4: 0},
        compiler_params=_params(("arbitrary", "arbitrary")),
    )(z, states, kept, dyb, dz, lb_logits, hg_g, *halves)


def _merge_tail(ya, yb, z, x2d, tgt2d, b_merge, final_g, pa, pb, wo):
    tokens, d = x2d.shape
    tm = min(TAIL_TOKENS, tokens)
    n_tiles = tokens // tm

    def body(ya_ref, yb_ref, z_ref, x_ref, t_ref, bm_ref, fg_ref, pa_hbm, pb_hbm, wo_hbm,
             dya_ref, dyb_ref, dx2_ref, dz_ref, loss_ref, dfg_ref, dbm_ref, dpa_hbm, dpb_hbm, dwo_hbm,
             pa_s, pb_s, wo_s, dpa_s, dpb_s, dwo_s, sems):
        i = pl.program_id(0)

        def together(pairs):
            copies = [pltpu.make_async_copy(src, dst, sems.at[k]) for k, (src, dst) in enumerate(pairs)]
            for cp in copies:
                cp.start()
            for cp in copies:
                cp.wait()

        @pl.when(i == 0)
        def _():
            together([(pa_hbm, pa_s), (pb_hbm, pb_s), (wo_hbm, wo_s)])
            dpa_s[...] = jnp.zeros_like(dpa_s)
            dpb_s[...] = jnp.zeros_like(dpb_s)
            dwo_s[...] = jnp.zeros_like(dwo_s)
            loss_ref[...] = jnp.zeros_like(loss_ref)
            dfg_ref[...] = jnp.zeros_like(dfg_ref)
            dbm_ref[...] = jnp.zeros_like(dbm_ref)

        ya_t = ya_ref[...]
        yb_t = yb_ref[...]
        out_a = _mm(ya_t, pa_s[...])
        out_b = _mm(yb_t, pb_s[...])
        g_a = _sigmoid(z_ref[0] + bm_ref[:, :d])
        g_b = _sigmoid(z_ref[1] + bm_ref[:, d:])
        mixed = g_a * out_a + g_b * out_b
        x2 = x_ref[...] + _mm(mixed, wo_s[...])
        r = lax.rsqrt(jnp.mean(x2 * x2, axis=-1, keepdims=True) + EPS)
        xn = x2 * r
        fg = fg_ref[...]
        diff = xn * fg - t_ref[...]
        loss_ref[...] += jnp.sum(diff * diff) * (0.5 / d)
        dy = diff * (1.0 / d)
        dfg_ref[...] += jnp.sum(dy * xn, axis=0, keepdims=True)
        dxn = dy * fg
        dx2 = r * (dxn - xn * jnp.mean(dxn * xn, axis=-1, keepdims=True))
        dx2_ref[...] = dx2
        dmixed = _mm_nt(dx2, wo_s[...])
        dwo_s[...] += _mm_tn(mixed, dx2)
        dgm_a = dmixed * out_a * g_a * (1.0 - g_a)
        dgm_b = dmixed * out_b * g_b * (1.0 - g_b)
        dz_ref[0] = dgm_a.astype(_MXU_DTYPE)
        dz_ref[1] = dgm_b.astype(_MXU_DTYPE)
        dbm_ref[:, :d] += jnp.sum(dgm_a, axis=0, keepdims=True)
        dbm_ref[:, d:] += jnp.sum(dgm_b, axis=0, keepdims=True)
        dout_a = dmixed * g_a
        dout_b = dmixed * g_b
        dpa_s[...] += _mm_tn(ya_t, dout_a)
        dpb_s[...] += _mm_tn(yb_t, dout_b)
        dya_ref[...] = _mm_nt(dout_a, pa_s[...])
        dyb_ref[...] = _mm_nt(dout_b, pb_s[...])

        @pl.when(i == n_tiles - 1)
        def _():
            together([(dpa_s, dpa_hbm), (dpb_s, dpb_hbm), (dwo_s, dwo_hbm)])

    tile = pl.BlockSpec((tm, d), lambda i: (i, 0))
    gm = pl.BlockSpec((2, tm, d), lambda i: (3, i, 0))
    row = lambda n: pl.BlockSpec((1, n), lambda i: (0, 0))
    hbm = pl.BlockSpec(memory_space=pl.ANY)
    act = jax.ShapeDtypeStruct((tokens, d), F32)
    mat = jax.ShapeDtypeStruct((d, d), F32)
    return pl.pallas_call(
        body, name="merge_tail",
        grid=(n_tiles,),
        in_specs=[tile, tile, gm, tile, tile, row(2 * d), row(d), hbm, hbm, hbm],
        out_specs=[tile, tile, tile, gm, row(LANES), row(d), row(2 * d), hbm, hbm, hbm],
        out_shape=[act, act, act, jax.ShapeDtypeStruct((N_GROUPS, tokens, d), _MXU_DTYPE),
                   jax.ShapeDtypeStruct((1, LANES), F32), jax.ShapeDtypeStruct((1, d), F32),
                   jax.ShapeDtypeStruct((1, 2 * d), F32), mat, mat, mat],
        scratch_shapes=[pltpu.VMEM((d, d), _MXU_DTYPE)] * 3 + [pltpu.VMEM((d, d), F32)] * 3
        + [pltpu.SemaphoreType.DMA((3,))],
        compiler_params=_params(("arbitrary",)),
    )(ya, yb, z, x2d, tgt2d, b_merge, final_g, pa, pb, wo)


def _inproj_dw_exchange(h_t, dz, scatter):
    d, tokens = h_t.shape
    tm = min(CONTRACT_TOKENS, tokens)
    n_i = tokens // tm
    half = d // 2

    def body(h_ref, dz_ref, *rest):
        n_in, n_out = scatter.n_in, scatter.n_out
        dw_hbm, land_hbm = rest[n_in:n_in + 2]
        acc, local_sems, send_sems, recv_sems = rest[n_in + 2 + n_out:n_in + 6 + n_out]
        carried = scatter.copies(rest[:n_in], rest[n_in + 2:n_in + 2 + n_out], rest[n_in + 6 + n_out:])
        s, i = pl.program_id(0), pl.program_id(1)
        x, y, c, _ = _mesh_position()

        @pl.when((s == 0) & (i == 0))
        def _():
            for cp in carried:
                cp.start()

        part = _mm(h_ref[...], dz_ref[...])
        buf = acc.at[s % 2]

        @pl.when(i == 0)
        def _():
            buf[...] = part

        @pl.when(i != 0)
        def _():
            buf[...] += part

        def copies(k):
            g = _SLOT_TO_GROUP[k]
            cols = pl.ds((g % 2) * D_MODEL, D_MODEL)
            src = acc.at[k % 2]
            mine = pltpu.make_async_copy(src, dw_hbm.at[g // 2, :, cols], local_sems.at[k % 2])
            theirs = _remote(src.at[pl.ds((1 - c) * half, half), :], land_hbm.at[g // 2, :, cols],
                             send_sems, recv_sems, k, (x, y, 1 - c))
            return mine, theirs

        for k in range(N_GROUPS):
            @pl.when((s == k) & (i == n_i - 1))
            def _(k=k):
                if k > 0:
                    mine, theirs = copies(k - 1)
                    mine.wait()
                    theirs.wait_send()
                mine, theirs = copies(k)
                mine.start()
                theirs.start()
                if k == N_GROUPS - 1:
                    mine.wait()
                    theirs.wait_send()
                    for kk in range(N_GROUPS):
                        copies(kk)[1].wait_recv()
                    for cp in carried:
                        cp.wait()

    hbm = pl.BlockSpec(memory_space=pl.ANY)
    more = scatter.plumbing(first_operand=2, first_output=2)
    return pl.pallas_call(
        body, name="inproj_dw_exchange",
        grid=(N_GROUPS, n_i),
        in_specs=[pl.BlockSpec((d, tm), lambda s, i: (0, i)),
                  pl.BlockSpec((None, tm, D_MODEL), lambda s, i: (s, i, 0))] + more[1],
        out_specs=[hbm, hbm] + more[2],
        out_shape=[jax.ShapeDtypeStruct((N_SHARDS, d, 2 * D_MODEL), F32),
                   jax.ShapeDtypeStruct((N_SHARDS, half, 2 * D_MODEL), F32)] + more[3],
        scratch_shapes=[pltpu.VMEM((2, d, D_MODEL), F32), pltpu.SemaphoreType.DMA((2,)),
                        pltpu.SemaphoreType.DMA((N_GROUPS,)), pltpu.SemaphoreType.DMA((N_GROUPS,))] + more[4],
        input_output_aliases=more[5],
        compiler_params=_params(("arbitrary", "arbitrary")),
    )(h_t, dz, *more[0])


def _inproj_dx(dz, w_all, x2d, dx2, norm_g, scatter):
    tokens, d = x2d.shape
    tm = min(TAIL_TOKENS, tokens)
    n_tiles = tokens // tm

    def body(dz_ref, w_hbm, x_ref, dx2_ref, g_ref, *rest):
        n_in, n_out = scatter.n_in, scatter.n_out
        dx_ref, dg_ref = rest[n_in:n_in + 2]
        w_res, load_sems = rest[n_in + 2 + n_out:n_in + 4 + n_out]
        copies = scatter.copies(rest[:n_in], rest[n_in + 2:n_in + 2 + n_out], rest[n_in + 4 + n_out:])
        i = pl.program_id(0)

        @pl.when(i == 0)
        def _():
            for cp in copies:
                cp.start()
            loads = [pltpu.make_async_copy(w_hbm.at[g // 2, :, pl.ds((g % 2) * D_MODEL, D_MODEL)],
                                           w_res.at[:, pl.ds(slot * D_MODEL, D_MODEL)], load_sems.at[slot])
                     for slot, g in enumerate(_SLOT_TO_GROUP)]
            for cp in loads:
                cp.start()
            for cp in loads:
                cp.wait()
            dg_ref[...] = jnp.zeros_like(dg_ref)

        dz_all = jnp.concatenate([dz_ref[s] for s in range(N_GROUPS)], axis=1)
        dh = jnp.transpose(_mm_nt(w_res[...], dz_all))
        x = x_ref[...]
        r = lax.rsqrt(jnp.mean(x * x, axis=-1, keepdims=True) + EPS)
        xn = x * r
        dg_ref[...] += jnp.sum(dh * xn, axis=0, keepdims=True)
        dxn = dh * g_ref[...]
        dx_ref[...] = r * (dxn - xn * jnp.mean(dxn * xn, axis=-1, keepdims=True)) + dx2_ref[...]

        @pl.when(i == n_tiles - 1)
        def _():
            for cp in copies:
                cp.wait()

    tile = pl.BlockSpec((tm, d), lambda i: (i, 0))
    hbm = pl.BlockSpec(memory_space=pl.ANY)
    more = scatter.plumbing(first_operand=5, first_output=2)
    return pl.pallas_call(
        body, name="inproj_dx", grid=(n_tiles,),
        in_specs=[pl.BlockSpec((N_GROUPS, tm, D_MODEL), lambda i: (0, i, 0)), hbm, tile, tile,
                  pl.BlockSpec((1, d), lambda i: (0, 0))] + more[1],
        out_specs=[tile, pl.BlockSpec((1, d), lambda i: (0, 0))] + more[2],
        out_shape=[jax.ShapeDtypeStruct((tokens, d), F32), jax.ShapeDtypeStruct((1, d), F32)] + more[3],
        scratch_shapes=[pltpu.VMEM((d, N_GROUPS * D_MODEL), _MXU_DTYPE), pltpu.SemaphoreType.DMA((N_GROUPS,))] + more[4],
        input_output_aliases=more[5],
        compiler_params=_params(("arbitrary",)),
    )(dz, w_all, x2d, dx2, norm_g, *more[0])


def _row_tile(rows, cols, itemsize=4, budget=2 * 1024 * 1024):
    tr = rows
    while tr * cols * itemsize > budget and tr % 16 == 0:
        tr //= 2
    return tr


def _cast_into_slot(a, chip, dtype, name):
    rows, cols = a.shape
    tr = _row_tile(rows, cols)

    def body(chip_ref, a_ref, o_ref):
        del chip_ref
        o_ref[...] = a_ref[...].astype(dtype)

    grid_spec = pltpu.PrefetchScalarGridSpec(
        num_scalar_prefetch=1, grid=(rows // tr,),
        in_specs=[pl.BlockSpec((tr, cols), lambda i, chip_ref: (i, 0))],
        out_specs=pl.BlockSpec((None, tr, cols), lambda i, chip_ref: (chip_ref[0], i, 0)))
    return pl.pallas_call(body, name=name, grid_spec=grid_spec,
                          out_shape=jax.ShapeDtypeStruct((N_SHARDS, rows, cols), dtype),
                          compiler_params=_params(("arbitrary",)))(chip, a)


def _sum_slots(stack, name):
    n, rows, cols = stack.shape
    tr = _row_tile(rows, cols * n)

    def body(s_ref, o_ref):
        total = s_ref[0].astype(F32)
        for k in range(1, n):
            total = total + s_ref[k].astype(F32)
        o_ref[...] = total

    return pl.pallas_call(body, name=name, grid=(rows // tr,),
                          in_specs=[pl.BlockSpec((n, tr, cols), lambda i: (0, i, 0))],
                          out_specs=pl.BlockSpec((tr, cols), lambda i: (i, 0)),
                          out_shape=jax.ShapeDtypeStruct((rows, cols), F32),
                          compiler_params=_params(("parallel",)))(stack)


def _add_half(full, landed, place, name):
    n, rows, cols = full.shape
    half = rows // 2
    tr = _row_tile(half, cols)
    nb = half // tr

    def body(place_ref, a_ref, b_ref, o_ref, own_ref):
        total = (a_ref[...] + b_ref[...]).astype(_MXU_DTYPE)
        o_ref[...] = total

        @pl.when(pl.program_id(1) == place_ref[1])
        def _():
            own_ref[...] = total

    grid_spec = pltpu.PrefetchScalarGridSpec(
        num_scalar_prefetch=1, grid=(nb, n),
        in_specs=[pl.BlockSpec((None, tr, cols), lambda i, j, place_ref: (j, place_ref[0] * nb + i, 0)),
                  pl.BlockSpec((None, tr, cols), lambda i, j, place_ref: (j, i, 0))],
        out_specs=[pl.BlockSpec((None, tr, cols), lambda i, j, place_ref: (j, i, 0)),
                   pl.BlockSpec((None, tr, cols), lambda i, j, place_ref: (place_ref[1], i, 0))])
    shape = jax.ShapeDtypeStruct((n, half, cols), _MXU_DTYPE)
    return pl.pallas_call(body, name=name, grid_spec=grid_spec, out_shape=[shape, shape],
                          compiler_params=_params(("parallel", "arbitrary")))(place, full, landed)


def _adamw_update(w, grad, m, v):
    c1 = 1.0 - ADAM_B1 ** ADAM_STEP
    c2 = 1.0 - ADAM_B2 ** ADAM_STEP
    nm = ADAM_B1 * m + (1.0 - ADAM_B1) * grad
    nv = ADAM_B2 * v + (1.0 - ADAM_B2) * (grad * grad)
    return (-ADAM_LR) * ((nm / c1) / (jnp.sqrt(nv / c2) + ADAM_EPS) + ADAM_WD * w), nm, nv


def _adamw(w, g, m, v, name):
    rows, cols = w.shape
    tr = _row_tile(rows, cols, budget=1024 * 1024)

    def body(w_ref, g_ref, m_ref, v_ref, d_ref, nm_ref, nv_ref):
        d_ref[...], nm_ref[...], nv_ref[...] = _adamw_update(w_ref[...], g_ref[...], m_ref[...], v_ref[...])

    spec = pl.BlockSpec((tr, cols), lambda i: (i, 0))
    shape = jax.ShapeDtypeStruct((rows, cols), F32)
    return pl.pallas_call(body, name=name, grid=(rows // tr,), in_specs=[spec] * 4, out_specs=[spec] * 3,
                          out_shape=[shape] * 3, compiler_params=_params(("parallel",)))(w, g, m, v)


def _adamw_halves(w, g_mine, g_sibling, m, v, core, name):
    rows, cols = w.shape
    half = rows // 2
    tr = _row_tile(half, cols, budget=1024 * 1024)
    nb = half // tr

    def body(core_ref, w_ref, gm_ref, gs_ref, m_ref, v_ref, g_ref, d_ref, nm_ref, nv_ref):
        mine = pl.program_id(0) // nb == core_ref[0]
        grad = jnp.where(mine, gm_ref[...], gs_ref[...])
        g_ref[...] = grad
        d_ref[...], nm_ref[...], nv_ref[...] = _adamw_update(w_ref[...], grad, m_ref[...], v_ref[...])

    spec = pl.BlockSpec((tr, cols), lambda i, core_ref: (i, 0))
    mine_spec = pl.BlockSpec((tr, cols), lambda i, core_ref: (jnp.where(i // nb == core_ref[0], i % nb, 0), 0))
    sibling_spec = pl.BlockSpec((tr, cols), lambda i, core_ref: (jnp.where(i // nb == core_ref[0], 0, i % nb), 0))
    grid_spec = pltpu.PrefetchScalarGridSpec(num_scalar_prefetch=1, grid=(rows // tr,),
                                             in_specs=[spec, mine_spec, sibling_spec, spec, spec], out_specs=[spec] * 4)
    shape = jax.ShapeDtypeStruct((rows, cols), F32)
    return pl.pallas_call(body, name=name, grid_spec=grid_spec, out_shape=[shape] * 4,
                          compiler_params=_params(("parallel",)))(core, w, g_mine, g_sibling, m, v)


def _local_step(x, loss_target, gather, reduction, b_merge, conv_b, rg_wx, rg_bx, rg_wa, rg_ba, rg_lambda,
                hg_lb_logits, hg_norm_g, norm_g, final_norm_g):
    batch, seq, d = x.shape
    x2d = x.reshape(batch * seq, d)
    tgt2d = loss_target.reshape(batch * seq, d)
    z, h_t, (w_all, pa, pb, wo), cw_all = _inproj_fwd_gather(x2d, norm_g, *gather)
    pa, pb, wo = (t.reshape(d, d) for t in (pa, pb, wo))
    conv_w = jnp.transpose(cw_all, (1, 0, 2)).reshape(CONV_WIDTH, d)
    lru = (conv_w, conv_b, rg_wx, rg_bx, rg_wa, rg_ba, rg_lambda)
    ya, hl, kept = _branch_a_fwd(z, *lru, batch, seq)
    yb, states, kept_b = _branch_b_fwd(z, hg_lb_logits, hg_norm_g, batch, seq)
    dya, dyb, dx2, dz, loss, d_final_g, d_b_merge, d_pa, d_pb, d_wo = _merge_tail(
        ya, yb, z, x2d, tgt2d, b_merge, final_norm_g, pa, pb, wo)
    proj_grads = [g.reshape(N_SHARDS, d // N_SHARDS, d) for g in (d_pa, d_pb, d_wo)]
    dz, d_lb_logits, d_hg_g, *landed_proj = _branch_b_bwd(
        z, states, kept_b, dyb, dz, hg_lb_logits, hg_norm_g, batch, seq, halves=proj_grads)
    dz, d_conv_w, d_conv_b, d_wx, d_bx, d_wa, d_ba, d_lam = _branch_a_bwd(
        z, hl, kept, dya, dz, conv_w, rg_wx, rg_wa, rg_lambda, batch, seq)
    small = dict(b_merge=d_b_merge, conv_w=d_conv_w, conv_b=d_conv_b, rg_wx=d_wx, rg_bx=d_bx, rg_wa=d_wa,
                 rg_ba=d_ba, rg_lambda=d_lam, hg_lb_logits=d_lb_logits, hg_norm_g=d_hg_g,
                 norm_g=jnp.zeros((1, d), F32), final_norm_g=d_final_g)
    first, second = reduction
    d_w_in, landed_w_in, *scattered_first = _inproj_dw_exchange(h_t, dz, first(proj_grads, landed_proj, small))
    grad_x, d_norm_g, *scattered_second = _inproj_dx(dz, w_all, x2d, dx2, norm_g, scatter=second(d_w_in, landed_w_in))
    return loss[0, 0], grad_x.reshape(batch, seq, d), d_norm_g, (scattered_first, scattered_second)


_SMALL_ORDER = ("b_merge", "conv_w", "conv_b", "rg_wx", "rg_bx", "rg_wa", "rg_ba", "rg_lambda", "hg_lb_logits",
                "hg_norm_g", "norm_g", "final_norm_g")
N_DEV = 8
PIECE_ROWS = 272


def _pack_small(tree):
    flat = jnp.concatenate([tree[k].reshape(-1) for k in _SMALL_ORDER])
    flat = jnp.pad(flat, (0, N_DEV * PIECE_ROWS * LANES - flat.shape[0]))
    return flat.reshape(N_DEV * PIECE_ROWS, LANES)


def _unpack_small(packed, like):
    flat = packed.reshape(-1)
    out, pos = {}, 0
    for k in _SMALL_ORDER:
        n = like[k].size
        out[k] = flat[pos:pos + n].reshape(like[k].shape)
        pos += n
    return out


def _mesh_position():
    x, y, c = lax.axis_index("x"), lax.axis_index("y"), lax.axis_index("c")
    other_chips = [(1 - x, y), (x, 1 - y), (1 - x, 1 - y)]
    return x, y, c, other_chips


def _other_devices(x, y, c):
    flips = [(fx, fy, fc) for fx in (0, 1) for fy in (0, 1) for fc in (0, 1) if (fx, fy, fc) != (0, 0, 0)]
    return [(jnp.where(fx, 1 - x, x), jnp.where(fy, 1 - y, y), jnp.where(fc, 1 - c, c)) for fx, fy, fc in flips]


def _remote(src, dst, send_sems, recv_sems, k, device):
    return pltpu.make_async_remote_copy(src_ref=src, dst_ref=dst, send_sem=send_sems.at[k], recv_sem=recv_sems.at[k],
                                        device_id=device, device_id_type=MESH)


def _exchange_halves(bigs, small):
    n_big = len(bigs)
    n_sem = n_big + N_DEV - 1

    def body(*refs):
        srcs, small_src = refs[:n_big], refs[n_big]
        outs, small_out = refs[n_big + 1:2 * n_big + 1], refs[2 * n_big + 1]
        send_sems, recv_sems, local_sem = refs[2 * n_big + 2:]
        x, y, c, _ = _mesh_position()
        me, sibling = 4 * x + 2 * y + c, (x, y, 1 - c)
        mine = pltpu.make_async_copy(small_src.at[pl.ds(me * PIECE_ROWS, PIECE_ROWS), :], small_out.at[me], local_sem)
        mine.start()
        copies = []
        for a in range(n_big):
            hs = srcs[a].shape[1] // 2
            copies.append(_remote(srcs[a].at[:, pl.ds((1 - c) * hs, hs), :], outs[a], send_sems, recv_sems, a, sibling))
        for k, (px, py, pc) in enumerate(_other_devices(x, y, c)):
            piece = small_src.at[pl.ds((4 * px + 2 * py + pc) * PIECE_ROWS, PIECE_ROWS), :]
            copies.append(_remote(piece, small_out.at[me], send_sems, recv_sems, n_big + k, (px, py, pc)))
        for cp in copies:
            cp.start()
        for cp in copies:
            cp.wait()
        mine.wait()

    hbm = pl.BlockSpec(memory_space=pl.ANY)
    out_shape = [jax.ShapeDtypeStruct((g.shape[0], g.shape[1] // 2, g.shape[2]), F32) for g in bigs]
    out_shape.append(jax.ShapeDtypeStruct((N_DEV, PIECE_ROWS, LANES), F32))
    return pl.pallas_call(
        body, name="exchange_halves",
        in_specs=[hbm] * (n_big + 1), out_specs=[hbm] * (n_big + 1), out_shape=out_shape,
        scratch_shapes=[pltpu.SemaphoreType.DMA((n_sem,)), pltpu.SemaphoreType.DMA((n_sem,)), pltpu.SemaphoreType.DMA],
    )(*bigs, small)


class _Scatter:
    def __init__(self, bigs, by_chip, small=None):
        self.bigs, self.by_chip, self.small = list(bigs), list(by_chip), small
        self.n_big = len(self.bigs)
        self.n_in = 2 * self.n_big + (small is not None)
        self.n_out = self.n_big + (small is not None)
        self.n_scratch = 2 + (small is not None)

    def plumbing(self, first_operand, first_output):
        hbm = pl.BlockSpec(memory_space=pl.ANY)
        n_sem = 3 * self.n_big + (N_DEV - 1 if self.small is not None else 0)
        operands = self.bigs + self.by_chip + ([self.small] if self.small is not None else [])
        out_shapes = [jax.ShapeDtypeStruct(g.shape, g.dtype) for g in self.by_chip]
        scratch = [pltpu.SemaphoreType.DMA((n_sem,)), pltpu.SemaphoreType.DMA((n_sem,))]
        if self.small is not None:
            out_shapes.append(jax.ShapeDtypeStruct((N_DEV, PIECE_ROWS, LANES), F32))
            scratch.append(pltpu.SemaphoreType.DMA)
        aliases = {first_operand + self.n_big + a: first_output + a for a in range(self.n_big)}
        return operands, [hbm] * self.n_in, [hbm] * self.n_out, out_shapes, scratch, aliases

    def copies(self, in_refs, out_refs, scratch_refs):
        srcs, outs = in_refs[:self.n_big], out_refs[:self.n_big]
        send_sems, recv_sems = scratch_refs[:2]
        x, y, c, chips = _mesh_position()
        chip, me = 2 * x + y, 4 * x + 2 * y + c
        copies = []
        for a in range(self.n_big):
            for j, (cx, cy) in enumerate(chips):
                copies.append(_remote(srcs[a].at[2 * cx + cy], outs[a].at[chip], send_sems, recv_sems, 3 * a + j,
                                      (cx, cy, c)))
        if self.small is not None:
            small_src, small_out = in_refs[2 * self.n_big], out_refs[self.n_big]
            copies.append(pltpu.make_async_copy(small_src, small_out.at[me], scratch_refs[2]))
            for k, peer in enumerate(_other_devices(x, y, c)):
                copies.append(_remote(small_src, small_out.at[me], send_sems, recv_sems, 3 * self.n_big + k, peer))
        return copies


def _swap_halves(halves, vec):
    n_big = len(halves)

    def body(*refs):
        srcs, vec_src = refs[:n_big], refs[n_big]
        outs, vec_out = refs[n_big + 1:2 * n_big + 1], refs[2 * n_big + 1]
        send_sems, recv_sems, local_sem = refs[2 * n_big + 2:]
        x, y, c, _ = _mesh_position()
        me = 4 * x + 2 * y + c
        copies = [pltpu.make_async_copy(vec_src, vec_out.at[me], local_sem)]
        copies += [_remote(srcs[a], outs[a], send_sems, recv_sems, a, (x, y, 1 - c)) for a in range(n_big)]
        copies += [_remote(vec_src, vec_out.at[me], send_sems, recv_sems, n_big + k, peer)
                   for k, peer in enumerate(_other_devices(x, y, c))]
        for cp in copies:
            cp.start()
        for cp in copies:
            cp.wait()

    hbm = pl.BlockSpec(memory_space=pl.ANY)
    n_sem = n_big + N_DEV - 1
    return pl.pallas_call(
        body, name="swap_halves",
        in_specs=[hbm] * (n_big + 1), out_specs=[hbm] * (n_big + 1),
        out_shape=[jax.ShapeDtypeStruct(h.shape, F32) for h in halves] + [jax.ShapeDtypeStruct((N_DEV,) + vec.shape, F32)],
        scratch_shapes=[pltpu.SemaphoreType.DMA((n_sem,)), pltpu.SemaphoreType.DMA((n_sem,)), pltpu.SemaphoreType.DMA],
    )(*halves, vec)


def kernel(x, w_in, b_merge, conv_w, conv_b, rg_wx, rg_bx, rg_wa, rg_ba, rg_lambda, hg_lb_logits, hg_norm_g, proj_a, proj_b, w_out, norm_g, final_norm_g, loss_target, m_w_in, m_b_merge, m_conv_w, m_conv_b, m_rg_wx, m_rg_bx, m_rg_wa, m_rg_ba, m_rg_lambda, m_hg_lb_logits, m_hg_norm_g, m_proj_a, m_proj_b, m_w_out, m_norm_g, m_final_norm_g, v_w_in, v_b_merge, v_conv_w, v_conv_b, v_rg_wx, v_rg_bx, v_rg_wa, v_rg_ba, v_rg_lambda, v_hg_lb_logits, v_hg_norm_g, v_proj_a, v_proj_b, v_w_out, v_norm_g, v_final_norm_g):
    d = D_MODEL
    weights = dict(w_in=w_in, b_merge=b_merge, conv_w=conv_w, conv_b=conv_b, rg_wx=rg_wx, rg_bx=rg_bx, rg_wa=rg_wa,
                   rg_ba=rg_ba, rg_lambda=rg_lambda, hg_lb_logits=hg_lb_logits, hg_norm_g=hg_norm_g, proj_a=proj_a,
                   proj_b=proj_b, w_out=w_out, norm_g=norm_g, final_norm_g=final_norm_g)
    m = dict(w_in=m_w_in, b_merge=m_b_merge, conv_w=m_conv_w, conv_b=m_conv_b, rg_wx=m_rg_wx, rg_bx=m_rg_bx,
             rg_wa=m_rg_wa, rg_ba=m_rg_ba, rg_lambda=m_rg_lambda, hg_lb_logits=m_hg_lb_logits, hg_norm_g=m_hg_norm_g,
             proj_a=m_proj_a, proj_b=m_proj_b, w_out=m_w_out, norm_g=m_norm_g, final_norm_g=m_final_norm_g)
    v = dict(w_in=v_w_in, b_merge=v_b_merge, conv_w=v_conv_w, conv_b=v_conv_b, rg_wx=v_rg_wx, rg_bx=v_rg_bx,
             rg_wa=v_rg_wa, rg_ba=v_rg_ba, rg_lambda=v_rg_lambda, hg_lb_logits=v_hg_lb_logits, hg_norm_g=v_hg_norm_g,
             proj_a=v_proj_a, proj_b=v_proj_b, w_out=v_w_out, norm_g=v_norm_g, final_norm_g=v_final_norm_g)
    big_names = ("w_in", "proj_a", "proj_b", "w_out")

    core = lax.axis_index("c").astype(jnp.int32).reshape(1)
    chip = (2 * lax.axis_index("x") + lax.axis_index("y")).astype(jnp.int32)

    slotted = [_cast_into_slot(weights[k][0], chip.reshape(1), _MXU_DTYPE, f"cast_{k}") for k in big_names]
    conv_slotted = _cast_into_slot(conv_w[0], chip.reshape(1), F32, "slot_conv_w")

    small_shapes = {}

    place = jnp.concatenate([core, chip.reshape(1)])

    def reduce_proj_and_small(proj_grads, landed, small_grads):
        small_shapes.update({k: t.shape for k, t in small_grads.items()})
        (small_landed,) = _exchange_halves([], _pack_small(small_grads))
        sums = [_add_half(g, l, place, f"add_half_{1 + a}") for a, (g, l) in enumerate(zip(proj_grads, landed))]
        return _Scatter([s[0] for s in sums], [s[1] for s in sums], _sum_slots(small_landed, "sum_small"))

    def reduce_w_in(d_w_in, landed):
        partial, own_slot = _add_half(d_w_in, landed, place, "add_half_0")
        return _Scatter([partial], [own_slot])

    loss_part, grad_x, d_norm_g, ((*by_chip_proj, small_all), by_chip_w_in) = _local_step(
        x, loss_target, (slotted, conv_slotted, chip.reshape(1)), (reduce_proj_and_small, reduce_w_in),
        b_merge, conv_b, rg_wx[0], rg_bx.reshape(1, d), rg_wa[0], rg_ba.reshape(1, d), rg_lambda, hg_lb_logits,
        hg_norm_g, norm_g, final_norm_g.reshape(1, d))
    mine = [_sum_slots(s, f"sum_chips_{a}") for a, s in enumerate(by_chip_w_in + by_chip_proj)]
    late = jnp.concatenate([d_norm_g.reshape(SUBLANES, LANES), jnp.full((SUBLANES, LANES), loss_part, F32)])
    *theirs, late_parts = _swap_halves(mine, late)
    late_sum = _sum_slots(late_parts, "sum_late")
    loss = late_sum[SUBLANES, 0]
    small_red = _unpack_small(small_all, {k: jax.ShapeDtypeStruct(s, F32) for k, s in small_shapes.items()})
    small_red["norm_g"] = late_sum[:SUBLANES].reshape(1, d)

    grads, delta, new_m, new_v = {}, {}, {}, {}
    for k, g_mine, g_theirs in zip(big_names, mine, theirs):
        out = _adamw_halves(weights[k][0], g_mine, g_theirs, m[k][0], v[k][0], core, f"adamw_{k}")
        grads[k], delta[k], new_m[k], new_v[k] = (t.reshape(weights[k].shape) for t in out)
    cols = d // N_SHARDS
    g_conv = lax.dynamic_slice(small_red["conv_w"], (0, chip * cols), (CONV_WIDTH, cols))
    grads["conv_w"] = g_conv.reshape(conv_w.shape)
    dl, nm, nv = _adamw(conv_w[0], g_conv, m_conv_w[0], v_conv_w[0], "adamw_conv_w")
    delta["conv_w"], new_m["conv_w"], new_v["conv_w"] = (t.reshape(conv_w.shape) for t in (dl, nm, nv))
    rest = [k for k in _SMALL_ORDER if k != "conv_w"]
    like = {k: (weights[k] if k != "conv_w" else jnp.zeros((CONV_WIDTH, d), F32)) for k in _SMALL_ORDER}
    packs = [_pack_small({k: (t[k] if k != "conv_w" else like[k]) for k in _SMALL_ORDER}) for t in (weights, m, v)]
    g_pack = _pack_small({k: small_red[k].reshape(like[k].shape) for k in _SMALL_ORDER})
    outs = [_unpack_small(p, like) for p in _adamw(packs[0], g_pack, packs[1], packs[2], "adamw_small")]
    for k in rest:
        grads[k] = small_red[k].reshape(weights[k].shape)
        delta[k], new_m[k], new_v[k] = outs[0][k], outs[1][k], outs[2][k]

    order = ("w_in", "b_merge", "conv_w", "conv_b", "rg_wx", "rg_bx", "rg_wa", "rg_ba", "rg_lambda", "hg_lb_logits",
             "hg_norm_g", "proj_a", "proj_b", "w_out", "norm_g", "final_norm_g")
    return (loss, grad_x, *[grads[k] for k in order], *[delta[k] for k in order], *[new_m[k] for k in order],
            *[new_v[k] for k in order])
```

```python
import functools

import jax
import jax.numpy as jnp
from jax import lax
from jax.experimental import pallas as pl
from jax.experimental.pallas import tpu as pltpu

F32 = jnp.float32
_MXU_DTYPE = jnp.bfloat16

D_MODEL = 1024
LANES = 128
SUBLANES = 8
N_BLK = D_MODEL // LANES
N_GROUPS = 8
N_SHARDS = 4
CONV_WIDTH = 4
LRU_C = 8.0
CHUNK = 64
CHUNKS_IN_FLIGHT = 16
FWD_CHUNKS_IN_FLIGHT = 32
SCAN_UNROLL = 4
HG_SCALE = float(LANES) ** -0.5
EPS = 1e-6
ADAM_LR, ADAM_B1, ADAM_B2, ADAM_EPS, ADAM_WD, ADAM_STEP = 0.001, 0.9, 0.999, 1e-08, 0.01, 10
MATMUL_TOKENS = 512
TAIL_TOKENS = 256
CONTRACT_TOKENS = 2048
VMEM_LIMIT = 56 * 1024 * 1024
VMEM_LIMIT_BIG = 60 * 1024 * 1024
MESH = pl.DeviceIdType.MESH

_SLOT_TO_GROUP = (2, 3, 4, 5, 0, 1, 6, 7)


def _mm(a, b):
    return lax.dot_general(a.astype(_MXU_DTYPE), b.astype(_MXU_DTYPE), (((1,), (0,)), ((), ())),
                           preferred_element_type=F32)


def _mm_nt(a, b):
    return lax.dot_general(a.astype(_MXU_DTYPE), b.astype(_MXU_DTYPE), (((1,), (1,)), ((), ())),
                           preferred_element_type=F32)


def _mm_tn(a, b):
    return lax.dot_general(a.astype(_MXU_DTYPE), b.astype(_MXU_DTYPE), (((0,), (0,)), ((), ())),
                           preferred_element_type=F32)


def _sigmoid(x):
    return 0.5 * jnp.tanh(0.5 * x) + 0.5


def _log1p_pos(y):
    series = y * (1.0 - y * (0.5 - y * (1.0 / 3.0 - y * 0.25)))
    return jnp.where(y < 0.01, series, jnp.log(1.0 + y))


def _softplus(x):
    return jnp.maximum(x, 0.0) + _log1p_pos(jnp.exp(-jnp.abs(x)))


def _shift_down(x, n):
    rolled = pltpu.roll(x, n, 0)
    edge = SUBLANES if (n < SUBLANES and x.shape[0] > SUBLANES) else x.shape[0]
    rows = lax.broadcasted_iota(jnp.int32, (edge, x.shape[1]), 0)
    head = jnp.where(rows >= n, rolled[:edge], 0.0)
    return head if edge == x.shape[0] else jnp.concatenate([head, rolled[edge:]], axis=0)


def _shift_up(x, n):
    size = x.shape[0]
    rolled = pltpu.roll(x, size - n, 0)
    edge = SUBLANES if (n < SUBLANES and size > SUBLANES) else size
    rows = lax.broadcasted_iota(jnp.int32, (edge, x.shape[1]), 0)
    tail = jnp.where(rows < edge - n, rolled[size - edge:], 0.0)
    return tail if edge == size else jnp.concatenate([rolled[:size - edge], tail], axis=0)


def _params(dims, vmem=VMEM_LIMIT):
    return pltpu.CompilerParams(dimension_semantics=dims, vmem_limit_bytes=vmem)


def _slot_of_group(g):
    return jnp.where(g < 2, g + 4, jnp.where(g < 6, g - 2, g))


def _inproj_fwd_gather(x2d, norm_g, slotted, conv_slotted, chip):
    tokens, d = x2d.shape
    tm = min(MATMUL_TOKENS, tokens)
    n_tiles = tokens // tm
    n_big = len(slotted)
    n_sem = 6 * (n_big + 1) + 3
    last_pass = N_GROUPS - 1

    def shard_of(k, chip_id):
        x, y = chip_id // 2, chip_id % 2
        return 2 * jnp.where(k % 2 == 1, 1 - x, x) + jnp.where(k // 2 == 1, 1 - y, y)

    def body(chip_ref, x_ref, g_ref, *rest):
        bufs, cw = rest[n_big + 1:2 * n_big + 1], rest[2 * n_big + 1]
        z_ref, ht_ref = rest[2 * n_big + 2:2 * n_big + 4]
        h_all, slab, send_sems, recv_sems, slab_sems = rest[2 * n_big + 4:]
        del chip_ref
        p, i = pl.program_id(0), pl.program_id(1)
        x, y, c, chips = _mesh_position()
        me, sibling = 2 * x + y, (x, y, 1 - c)

        pieces = [(0, 0), (0, 1)] + [(a, None) for a in range(1, n_big)]

        def half(piece, slot, which):
            a, q = pieces[piece]
            hs = bufs[a].shape[1] // 2
            cols = slice(None) if q is None else pl.ds(q * D_MODEL, D_MODEL)
            return bufs[a].at[slot, pl.ds(which * hs, hs), cols]

        def send(piece, j):
            mine = half(piece, me, c)
            return _remote(mine, mine, send_sems, recv_sems, 6 * piece + j, (chips[j][0], chips[j][1], c))

        def arrival(piece, j):
            landed = half(piece, 2 * chips[j][0] + chips[j][1], c)
            return _remote(landed, landed, send_sems, recv_sems, 6 * piece + j, (chips[j][0], chips[j][1], c))

        def passed_on(piece, j, which):
            landed = half(piece, 2 * chips[j][0] + chips[j][1], which)
            return _remote(landed, landed, send_sems, recv_sems, 6 * piece + 3 + j, sibling)

        def conv_copy(j, slot):
            return _remote(cw.at[slot], cw.at[slot], send_sems, recv_sems, 6 * len(pieces) + j,
                           (chips[j][0], chips[j][1], c))

        def land(piece, j):
            arrival(piece, j).wait_recv()
            passed_on(piece, j, c).start()
            passed_on(piece, j, 1 - c).wait_recv()

        def slab_copy(pv):
            src = bufs[0].at[shard_of(pv // 2, me), :, pl.ds((pv % 2) * D_MODEL, D_MODEL)]
            return pltpu.make_async_copy(src, slab.at[pv % 2], slab_sems.at[pv % 2])

        @pl.when((p == 0) & (i == 0))
        def _():
            for q in range(2):
                send(q, 0).start()
                send(q, 1).start()
            slab_copy(0).start()

        @pl.when(i == 0)
        def _():
            for pv in range(N_GROUPS):
                @pl.when(p == pv)
                def _(pv=pv):
                    slab_copy(pv).wait()

        rows = pl.ds(pl.multiple_of(i * tm, tm), tm)

        @pl.when(p == 0)
        def _():
            xt = x_ref[...]
            r = lax.rsqrt(jnp.mean(xt * xt, axis=-1, keepdims=True) + EPS)
            h = (xt * r) * g_ref[...]
            h_all[rows, :] = h.astype(_MXU_DTYPE)
            ht_ref[...] = jnp.transpose(h).astype(_MXU_DTYPE)

        z_ref[...] = _mm(h_all[rows, :], slab[p % 2])

        def relay(q):
            landed = half(q, 2 * chips[q][0] + chips[q][1], c)
            to = chips[1 - q]
            return _remote(landed, landed, send_sems, recv_sems, 6 * q + 2, (to[0], to[1], c))

        landings = {1: [(0, 0)], 2: [(1, 0), (1, 1)], 3: [(0, 1)], 5: [(0, 2)], 6: [(1, 2)]}

        def end_of_pass(pv):
            for q, j in landings.get(pv - 1, []):
                land(q, j)
                if j == q:
                    relay(q).start()
            if pv - 1 == 1:
                for piece in range(2, len(pieces)):
                    for jj in range(3):
                        send(piece, jj).start()
                for jj in range(3):
                    conv_copy(jj, me).start()
            if pv - 1 in (5, 6):
                for piece in range(2, len(pieces)):
                    for jj in ((0, 1) if pv - 1 == 5 else (2,)):
                        land(piece, jj)
            slab_copy(pv).start()

        @pl.when(i == n_tiles - 1)
        def _():
            for pv in range(1, N_GROUPS):
                pl.when(p == pv - 1)(functools.partial(end_of_pass, pv))

        @pl.when((p == last_pass) & (i == n_tiles - 1))
        def _():
            for j in range(3):
                conv_copy(j, 2 * chips[j][0] + chips[j][1]).wait_recv()
            for piece in range(len(pieces)):
                for j in range(3):
                    (relay(piece) if (piece < 2 and j == 2) else send(piece, j)).wait_send()
                    passed_on(piece, j, c).wait_send()
            for j in range(3):
                conv_copy(j, me).wait_send()

    def z_index(p, i, chip_ref):
        g = 2 * shard_of(p // 2, chip_ref[0]) + p % 2
        return (_slot_of_group(g), i, 0)

    def first_pass_tile(p, i, chip_ref):
        return jnp.where(p == 0, i, n_tiles - 1)

    hbm = pl.BlockSpec(memory_space=pl.ANY)
    operands = list(slotted) + [conv_slotted]
    grid_spec = pltpu.PrefetchScalarGridSpec(
        num_scalar_prefetch=1, grid=(N_GROUPS, n_tiles),
        in_specs=[pl.BlockSpec((tm, d), lambda p, i, chip_ref: (first_pass_tile(p, i, chip_ref), 0)),
                  pl.BlockSpec((1, d), lambda p, i, chip_ref: (0, 0))] + [hbm] * (n_big + 1),
        out_specs=[hbm] * (n_big + 1) + [pl.BlockSpec((None, tm, D_MODEL), z_index),
                                         pl.BlockSpec((d, tm), lambda p, i, chip_ref: (0, first_pass_tile(p, i, chip_ref)))],
        scratch_shapes=[pltpu.VMEM((tokens, d), _MXU_DTYPE), pltpu.VMEM((2, d, D_MODEL), _MXU_DTYPE),
                        pltpu.SemaphoreType.DMA((n_sem,)), pltpu.SemaphoreType.DMA((n_sem,)),
                        pltpu.SemaphoreType.DMA((2,))])
    out = pl.pallas_call(
        body, name="inproj_fwd_gather", grid_spec=grid_spec,
        out_shape=[jax.ShapeDtypeStruct(a.shape, a.dtype) for a in operands]
        + [jax.ShapeDtypeStruct((N_GROUPS, tokens, D_MODEL), F32), jax.ShapeDtypeStruct((d, tokens), _MXU_DTYPE)],
        input_output_aliases={3 + a: a for a in range(n_big + 1)},
        compiler_params=_params(("arbitrary", "arbitrary")),
    )(chip, x2d, norm_g, *operands)
    return out[n_big + 1], out[n_big + 2], out[:n_big], out[n_big]


def _lane_blocks(x):
    return [x[:, k * LANES:(k + 1) * LANES] for k in range(x.shape[1] // LANES)]


def _block_diag(x, w_ref, transposed=False):
    mm = _mm_nt if transposed else _mm
    return jnp.concatenate([mm(xk, w_ref[k]) for k, xk in enumerate(_lane_blocks(x))], axis=1)


def _lru_decay(gr, sp):
    log_a = (-LRU_C) * gr * sp
    a = jnp.exp(log_a)
    y = 2.0 * log_a
    mult_sq = jnp.where(y > -1e-3, -y * (1.0 + 0.5 * y), 1.0 - a * a)
    inv_mult = lax.rsqrt(jnp.maximum(mult_sq, 1e-37))
    return a, mult_sq * inv_mult, inv_mult


def _tile_rows(width):
    return lax.broadcasted_iota(jnp.int32, (SUBLANES, width), 0)


def _scan_forward(a_scr, u_scr, h_scr, seq):
    width = a_scr.shape[1]
    rows = _tile_rows(width)

    group = min(SCAN_UNROLL, seq // SUBLANES)

    def within_tile(j):
        sl = pl.ds(pl.multiple_of(j * SUBLANES, SUBLANES), SUBLANES)
        a = a_scr[sl, :]
        u = u_scr[sl, :]
        for d in (1, 2, 4):
            keep = rows >= d
            a_sh = jnp.where(keep, pltpu.roll(a, d, 0), 1.0)
            u_sh = jnp.where(keep, pltpu.roll(u, d, 0), 0.0)
            u = a * u_sh + u
            a = a * a_sh
        return sl, a, u

    def tiles(t, carry):
        parts = [within_tile(t * group + k) for k in range(group)]
        out = []
        for sl, a, u in parts:
            h = u + a * carry
            out.append((sl, h))
            carry = jnp.broadcast_to(h[SUBLANES - 1:SUBLANES, :], (SUBLANES, width))
        for sl, h in out:
            h_scr[sl, :] = h
        return carry

    lax.fori_loop(0, seq // SUBLANES // group, tiles, jnp.zeros((SUBLANES, width), F32))


def _scan_backward(c_scr, d_scr, g_scr, seq):
    width = c_scr.shape[1]
    rows = _tile_rows(width)
    n_tiles = seq // SUBLANES

    group = min(SCAN_UNROLL, n_tiles)

    def within_tile(j):
        sl = pl.ds(pl.multiple_of(j * SUBLANES, SUBLANES), SUBLANES)
        c = c_scr[sl, :]
        g = d_scr[sl, :]
        for d in (1, 2, 4):
            keep = rows < SUBLANES - d
            c_sh = jnp.where(keep, pltpu.roll(c, SUBLANES - d, 0), 1.0)
            g_sh = jnp.where(keep, pltpu.roll(g, SUBLANES - d, 0), 0.0)
            g = c * g_sh + g
            c = c * c_sh
        return sl, c, g

    def tiles(t, carry):
        parts = [within_tile(n_tiles - 1 - (t * group + k)) for k in range(group)]
        out = []
        for sl, c, g in parts:
            g = g + c * carry
            out.append((sl, g))
            carry = jnp.broadcast_to(g[0:1, :], (SUBLANES, width))
        for sl, g in out:
            g_scr[sl, :] = g
        return carry

    lax.fori_loop(0, n_tiles // group, tiles, jnp.zeros((SUBLANES, width), F32))


LRU_BLOCKS_PER_STEP = 2
LRU_LANES = LRU_BLOCKS_PER_STEP * LANES
LRU_STEPS = N_BLK // LRU_BLOCKS_PER_STEP


def _branch_a_fwd(z, conv_w, conv_b, wx, bx, wa, ba, lam, batch, seq):
    tokens = batch * seq

    def body(z_ref, cw_ref, cb_ref, wx_ref, bx_ref, wa_ref, ba_ref, lam_ref, ya_ref, hl_ref, kept_ref, a_scr, u_scr):
        xa = z_ref[0]
        ga = z_ref[1]
        xc = (cb_ref[...] + cw_ref[3:4, :] * xa + cw_ref[2:3, :] * _shift_down(xa, 1)
              + cw_ref[1:2, :] * _shift_down(xa, 2) + cw_ref[0:1, :] * _shift_down(xa, 3))
        gi = _sigmoid(_block_diag(xc, wx_ref) + bx_ref[...])
        gr = _sigmoid(_block_diag(xc, wa_ref) + ba_ref[...])
        a, mult, _ = _lru_decay(gr, _softplus(-lam_ref[...]))
        kept_ref[0], kept_ref[1], kept_ref[2] = xc, gi, gr
        a_scr[...] = a
        u_scr[...] = mult * gi * xc
        _scan_forward(a_scr, u_scr, hl_ref, seq)
        ya_ref[...] = (hl_ref[...] * (ga * _sigmoid(ga))).astype(_MXU_DTYPE)

    blk = pl.BlockSpec((seq, LRU_LANES), lambda b, c: (b, c))
    vec = pl.BlockSpec((1, LRU_LANES), lambda b, c: (0, c))
    mat = pl.BlockSpec((LRU_BLOCKS_PER_STEP, LANES, LANES), lambda b, c: (c, 0, 0))
    return pl.pallas_call(
        body, name="branch_a_fwd",
        grid=(batch, LRU_STEPS),
        in_specs=[pl.BlockSpec((2, seq, LRU_LANES), lambda b, c: (2, b, c)),
                  pl.BlockSpec((CONV_WIDTH, LRU_LANES), lambda b, c: (0, c)), vec, mat, vec, mat, vec, vec],
        out_specs=[blk, blk, pl.BlockSpec((3, seq, LRU_LANES), lambda b, c: (0, b, c))],
        out_shape=[jax.ShapeDtypeStruct((tokens, D_MODEL), _MXU_DTYPE), jax.ShapeDtypeStruct((tokens, D_MODEL), F32),
                   jax.ShapeDtypeStruct((3, tokens, D_MODEL), F32)],
        scratch_shapes=[pltpu.VMEM((seq, LRU_LANES), F32), pltpu.VMEM((seq, LRU_LANES), F32)],
        compiler_params=_params(("parallel", "parallel")),
    )(z, conv_w, conv_b, wx, bx, wa, ba, lam)


def _branch_a_bwd(z, hl, kept, dya, dz, conv_w, wx, wa, lam, batch, seq):
    def body(z_ref, hl_ref, kept_ref, dya_ref, dz_in_ref, cw_ref, wx_ref, wa_ref, lam_ref,
             dz_ref, dcw_ref, dcb_ref, dwx_ref, dbx_ref, dwa_ref, dba_ref, dlam_ref, c_scr, d_scr):
        del dz_in_ref
        g_scr = d_scr
        xa = z_ref[0]
        ga = z_ref[1]
        hl = hl_ref[...]
        dya = dya_ref[...]
        xc, gi, gr = kept_ref[0], kept_ref[1], kept_ref[2]
        sp = _softplus(-lam_ref[...])
        a, mult, inv_mult = _lru_decay(gr, sp)
        sga = _sigmoid(ga)
        dz_ref[1] = (dya * hl * (sga * (1.0 + ga * (1.0 - sga)))).astype(_MXU_DTYPE)
        c_scr[...] = _shift_up(a, 1)
        d_scr[...] = dya * (ga * sga)
        _scan_backward(c_scr, d_scr, g_scr, seq)
        g = g_scr[...]
        da = g * _shift_down(hl, 1)
        dmult = g * gi * xc
        dgi = g * mult * xc
        dxc = g * mult * gi
        dlog_a = da * a - dmult * (a * a) * inv_mult
        dgr = dlog_a * (-LRU_C) * sp
        dsp = jnp.sum(dlog_a * gr, axis=0, keepdims=True) * (-LRU_C)
        dlam = -dsp * _sigmoid(-lam_ref[...])
        dpi = dgi * gi * (1.0 - gi)
        dpr = dgr * gr * (1.0 - gr)
        dxc = dxc + _block_diag(dpi, wx_ref, transposed=True) + _block_diag(dpr, wa_ref, transposed=True)
        dwx = jnp.stack([_mm_tn(xk, dk) for xk, dk in zip(_lane_blocks(xc), _lane_blocks(dpi))])
        dwa = jnp.stack([_mm_tn(xk, dk) for xk, dk in zip(_lane_blocks(xc), _lane_blocks(dpr))])
        dbx = jnp.sum(dpi, axis=0, keepdims=True)
        dba = jnp.sum(dpr, axis=0, keepdims=True)
        ahead = [dxc if k == CONV_WIDTH - 1 else _shift_up(dxc, CONV_WIDTH - 1 - k) for k in range(CONV_WIDTH)]
        dxa = sum(cw_ref[k:k + 1, :] * ahead[k] for k in range(CONV_WIDTH))
        dz_ref[0] = dxa.astype(_MXU_DTYPE)
        dcb = jnp.sum(dxc, axis=0, keepdims=True)
        dcw = [jnp.sum(ahead[k] * xa, axis=0, keepdims=True) for k in range(CONV_WIDTH)]

        @pl.when(pl.program_id(1) == 0)
        def _():
            for k in range(CONV_WIDTH):
                dcw_ref[k:k + 1, :] = dcw[k]
            dcb_ref[...] = dcb
            dwx_ref[...] = dwx
            dbx_ref[...] = dbx
            dwa_ref[...] = dwa
            dba_ref[...] = dba
            dlam_ref[...] = dlam

        @pl.when(pl.program_id(1) != 0)
        def _():
            for k in range(CONV_WIDTH):
                dcw_ref[k:k + 1, :] += dcw[k]
            dcb_ref[...] += dcb
            dwx_ref[...] += dwx
            dbx_ref[...] += dbx
            dwa_ref[...] += dwa
            dba_ref[...] += dba
            dlam_ref[...] += dlam

    tokens = batch * seq
    blk = pl.BlockSpec((seq, LRU_LANES), lambda c, b: (b, c))
    vec = pl.BlockSpec((1, LRU_LANES), lambda c, b: (0, c))
    mat = pl.BlockSpec((LRU_BLOCKS_PER_STEP, LANES, LANES), lambda c, b: (c, 0, 0))
    vec_shape = jax.ShapeDtypeStruct((1, D_MODEL), F32)
    mat_shape = jax.ShapeDtypeStruct((N_BLK, LANES, LANES), F32)
    return pl.pallas_call(
        body, name="branch_a_bwd",
        grid=(LRU_STEPS, batch),
        in_specs=[pl.BlockSpec((2, seq, LRU_LANES), lambda c, b: (2, b, c)), blk,
                  pl.BlockSpec((3, seq, LRU_LANES), lambda c, b: (0, b, c)), blk,
                  pl.BlockSpec(memory_space=pl.ANY),
                  pl.BlockSpec((CONV_WIDTH, LRU_LANES), lambda c, b: (0, c)), mat, mat, vec],
        out_specs=[pl.BlockSpec((2, seq, LRU_LANES), lambda c, b: (2, b, c)),
                   pl.BlockSpec((CONV_WIDTH, LRU_LANES), lambda c, b: (0, c)), vec, mat, vec, mat, vec, vec],
        out_shape=[jax.ShapeDtypeStruct((N_GROUPS, tokens, D_MODEL), _MXU_DTYPE),
                   jax.ShapeDtypeStruct((CONV_WIDTH, D_MODEL), F32), vec_shape, mat_shape, vec_shape, mat_shape,
                   vec_shape, vec_shape],
        scratch_shapes=[pltpu.VMEM((seq, LRU_LANES), F32)] * 2,
        input_output_aliases={4: 0},
        compiler_params=_params(("parallel", "arbitrary"), vmem=VMEM_LIMIT_BIG),
    )(z, hl, kept, dya, dz, conv_w, wx, wa, lam)


def _chunk_masks(transposed=False):
    r = lax.broadcasted_iota(jnp.int32, (CHUNK, CHUNK), 0)
    c = lax.broadcasted_iota(jnp.int32, (CHUNK, CHUNK), 1)
    return r <= c if transposed else r >= c


def _row_blocks(seq, fn):
    block = min(256, seq)

    def trip(i, carry):
        fn(pl.ds(pl.multiple_of(i * block, block), block))
        return carry

    lax.fori_loop(0, seq // block, trip, 0)


def _hgrn_prepare(z_ref, lb_ref, f_scr, logf_scr, qh_scr, seq):
    lb = _sigmoid(lb_ref[0:1, :] - lb_ref[1:2, :])

    def block(rows):
        q = z_ref[0, rows, :]
        f = lb + (1.0 - lb) * _sigmoid(z_ref[1, rows, :])
        f_scr[rows, :] = f
        logf_scr[rows, :] = jnp.log(f)
        qh_scr[rows, :] = q * _sigmoid(q)

    _row_blocks(seq, block)
    return lb


def _cumsum_rows(x, reverse=False):
    shift = _shift_up if reverse else _shift_down
    d = 1
    while d < x.shape[0]:
        x = x + shift(x, d)
        d *= 2
    return x


def _lane_mean(x):
    return jnp.mean(x, axis=-1, keepdims=True)


def _token_contractions(lhs_scr, rhs_scr, out_ref, seq):
    rows_id = lax.broadcasted_iota(jnp.int32, (LANES, LANES), 0)

    def transposed(p):
        rows = pl.ds(pl.multiple_of(p * LANES, LANES), LANES)
        return jnp.transpose(lhs_scr[rows, :]).astype(_MXU_DTYPE), rhs_scr[rows, :]

    def contract(p, s):
        lhs_t, rhs = s
        return (_mm(lhs_t, jnp.where(rows_id < CHUNK, rhs, 0.0)), _mm(lhs_t, jnp.where(rows_id >= CHUNK, rhs, 0.0)))

    def store(p, out):
        out_ref[2 * p] = out[0]
        out_ref[2 * p + 1] = out[1]

    _independent_trips(seq // LANES, [transposed, contract], store)


def _chunk_rows(c):
    return pl.ds(pl.multiple_of(c * CHUNK, CHUNK), CHUNK)


def _chunk_terms(c, z_ref, f_scr, qh_scr, b_scr):
    rows = _chunk_rows(c)
    b = b_scr[rows, :]
    b_mid = b_scr[pl.ds(c * CHUNK + CHUNK // 2, 1), :]
    b_last = b_scr[pl.ds(c * CHUNK + CHUNK - 1, 1), :]
    qh = qh_scr[rows, :]
    k = 1.0 - f_scr[rows, :]
    v = z_ref[2, rows, :]
    e_q = jnp.exp(b - b_mid) * HG_SCALE
    e_k = jnp.exp(b_mid - b)
    e_qi = jnp.exp(b) * HG_SCALE
    e_ks = jnp.exp(b_last - b)
    decay = jnp.exp(b_last)
    return rows, qh, k, v, e_q, e_k, e_qi, e_ks, decay


def _independent_trips(n, stages, store, group=CHUNKS_IN_FLIGHT):
    stages = stages if isinstance(stages, (list, tuple)) else [stages]
    group = min(group, n)

    def trip(g, carry):
        ids = [g * group + i for i in range(group)]
        state = [stages[0](c) for c in ids]
        for stage in stages[1:]:
            state = [stage(c, s) for c, s in zip(ids, state)]
        for c, s in zip(ids, state):
            store(c, s)
        return carry

    lax.fori_loop(0, n // group, trip, 0)


def _branch_b_fwd(z, lb_logits, hg_g, batch, seq):
    tokens = batch * seq
    n_chunks = seq // CHUNK

    def body(z_ref, lb_ref, g_ref, yb_ref, st_ref, kept_ref, logf_scr, o_scr, qi_scr, ks_scr, dec_scr):
        f_scr, qh_scr, b_scr, o_kept = (kept_ref.at[k] for k in range(4))
        _hgrn_prepare(z_ref, lb_ref, f_scr, logf_scr, qh_scr, seq)
        causal = _chunk_masks()
        gain = g_ref[...]

        def cumulate(c):
            return _cumsum_rows(logf_scr[_chunk_rows(c), :])

        def store_cumulated(c, b):
            b_scr[_chunk_rows(c), :] = b

        def scores(c):
            _, qh, k, v, e_q, e_k, e_qi, e_ks, decay = _chunk_terms(c, z_ref, f_scr, qh_scr, b_scr)
            return _mm_nt(qh * e_q, k * e_k), v, qh * e_qi, k * e_ks, decay

        def within_chunk(c, s):
            att, v, q_int, k_st, decay = s
            return _mm(jnp.where(causal, att, 0.0), v), q_int, k_st, decay

        def store_within_chunk(c, out):
            rows = _chunk_rows(c)
            o_scr[rows, :], qi_scr[rows, :], ks_scr[rows, :], dec_scr[pl.ds(c, 1), :] = out

        def carry_state(c, state_t):
            update = st_ref[c]
            st_ref[c] = state_t
            return state_t * dec_scr[pl.ds(c, 1), :] + update

        def finish(c):
            rows = _chunk_rows(c)
            o = o_scr[rows, :] + _mm_nt(qi_scr[rows, :], st_ref[c])
            r = lax.rsqrt(_lane_mean(o * o) + EPS)
            gb = z_ref[3, rows, :]
            return (((o * r) * gain) * (gb * _sigmoid(gb))).astype(_MXU_DTYPE), o

        def store_finished(c, out):
            yb_ref[_chunk_rows(c), :], o_kept[_chunk_rows(c), :] = out

        _independent_trips(n_chunks, cumulate, store_cumulated, FWD_CHUNKS_IN_FLIGHT)
        _independent_trips(n_chunks, [scores, within_chunk], store_within_chunk, FWD_CHUNKS_IN_FLIGHT)
        _token_contractions(z_ref.at[2], ks_scr, st_ref, seq)
        lax.fori_loop(0, n_chunks, carry_state, jnp.zeros((LANES, LANES), F32))
        _independent_trips(n_chunks, finish, store_finished, FWD_CHUNKS_IN_FLIGHT)

    seq_buf = pltpu.VMEM((seq, LANES), F32)
    return pl.pallas_call(
        body, name="branch_b_fwd",
        grid=(batch, N_BLK),
        in_specs=[pl.BlockSpec((4, seq, LANES), lambda b, h: (0, b, h)),
                  pl.BlockSpec((2, LANES), lambda b, h: (0, h)),
                  pl.BlockSpec((1, LANES), lambda b, h: (0, 0))],
        out_specs=[pl.BlockSpec((seq, LANES), lambda b, h: (b, h)),
                   pl.BlockSpec((None, n_chunks, LANES, LANES), lambda b, h: (b * N_BLK + h, 0, 0, 0)),
                   pl.BlockSpec((4, seq, LANES), lambda b, h: (0, b, h))],
        out_shape=[jax.ShapeDtypeStruct((tokens, D_MODEL), _MXU_DTYPE),
                   jax.ShapeDtypeStruct((batch * N_BLK, n_chunks, LANES, LANES), F32),
                   jax.ShapeDtypeStruct((4, tokens, D_MODEL), F32)],
        scratch_shapes=[seq_buf] * 4 + [pltpu.VMEM((n_chunks, LANES), F32)],
        compiler_params=_params(("parallel", "parallel")),
    )(z, lb_logits, hg_g)


def _branch_b_bwd(z, states, kept, dyb, dz, lb_logits, hg_g, batch, seq):
    n_chunks = seq // CHUNK

    def body(z_ref, st_ref, kept_ref, dyb_ref, dz_in_ref, lb_ref, g_ref, dz_ref, dlog_ref, dg_ref,
             do_scr, qi_scr, dqh_scr, df_scr, dec_scr, dgp_scr, dlb_scr, dst_scr):
        del dz_in_ref
        f_scr, qh_scr, b_scr, o_kept = (kept_ref.at[k] for k in range(4))
        first = (pl.program_id(0) == 0) & (pl.program_id(1) == 0)
        lb = _sigmoid(lb_ref[0:1, :] - lb_ref[1:2, :])
        causal = _chunk_masks()
        anti_causal = _chunk_masks(transposed=True)
        gain = g_ref[...]

        @pl.when(first)
        def _():
            dg_ref[...] = jnp.zeros_like(dg_ref)

        @pl.when(pl.program_id(1) == 0)
        def _():
            dlb_scr[...] = jnp.zeros_like(dlb_scr)

        def output_gradient(c):
            rows = _chunk_rows(c)
            b = b_scr[rows, :]
            q_int = qh_scr[rows, :] * (jnp.exp(b) * HG_SCALE)
            decay = jnp.exp(b_scr[pl.ds(c * CHUNK + CHUNK - 1, 1), :])
            o = o_kept[rows, :]
            r = lax.rsqrt(_lane_mean(o * o) + EPS)
            o_n = o * r
            gb = z_ref[3, rows, :]
            sgb = _sigmoid(gb)
            dyb_c = dyb_ref[rows, :]
            d_ong = dyb_c * (gb * sgb)
            d_gb = (dyb_c * (o_n * gain) * (sgb * (1.0 + gb * (1.0 - sgb)))).astype(_MXU_DTYPE)
            d_gain = jnp.sum(d_ong * o_n, axis=0, keepdims=True)
            d_on = d_ong * gain
            return d_gb, d_gain, r * (d_on - o_n * _lane_mean(d_on * o_n)), q_int, decay

        def store_output_gradient(c, out):
            rows = _chunk_rows(c)
            dz_ref[3, rows, :], dgp_scr[pl.ds(c, 1), :], do_scr[rows, :], qi_scr[rows, :], dec_scr[pl.ds(c, 1), :] = out

        def carry_state_gradient(cc, d_state_t):
            c = n_chunks - 1 - cc
            update = dst_scr[c]
            dst_scr[c] = d_state_t
            return d_state_t * dec_scr[pl.ds(c, 1), :] + update

        def score_gradients(c):
            rows, qh, k, v, e_q, e_k, e_qi, e_ks, decay = _chunk_terms(c, z_ref, f_scr, qh_scr, b_scr)
            state_t = st_ref[c]
            d_state_t = dst_scr[c]
            d_o = do_scr[rows, :]
            q_in, k_in, q_int, k_st = qh * e_q, k * e_k, qh * e_qi, k * e_ks
            first = (_mm_nt(k_in, q_in), _mm_nt(d_o, v), _mm_nt(v, d_o), _mm_nt(k_st, d_state_t), _mm(d_o, state_t),
                     _mm(v, d_state_t))
            d_decay = jnp.sum(state_t * d_state_t, axis=0, keepdims=True)
            return first, d_o, q_in, k_in, q_int, k_st, e_q, e_k, e_qi, e_ks, decay, d_decay

        def input_gradients(c, s):
            (att_t, d_att, d_att_t, dv_inter, dq_int, dk_st), d_o, q_in, k_in, q_int, k_st, e_q, e_k, e_qi, e_ks, decay, d_decay = s
            rows = _chunk_rows(c)
            d_v = _mm(jnp.where(anti_causal, att_t, 0.0), d_o) + dv_inter
            dq_in = _mm(jnp.where(causal, d_att, 0.0), k_in)
            dk_in = _mm(jnp.where(anti_causal, d_att_t, 0.0), q_in)
            d_k = dk_in * e_k + dk_st * e_ks
            kk = dk_st * k_st
            d_b = dq_in * q_in + dq_int * q_int - dk_in * k_in - kk
            d_b_last = jnp.sum(kk, axis=0, keepdims=True) + decay * d_decay
            d_logf = _cumsum_rows(d_b, reverse=True) + d_b_last
            return d_v.astype(_MXU_DTYPE), dq_in * e_q + dq_int * e_qi, d_logf / f_scr[rows, :] - d_k

        def store_input_gradients(c, out):
            rows = _chunk_rows(c)
            dz_ref[2, rows, :], dqh_scr[rows, :], df_scr[rows, :] = out

        def input_activations(rows):
            q = z_ref[0, rows, :]
            sq = _sigmoid(q)
            dz_ref[0, rows, :] = (dqh_scr[rows, :] * (sq * (1.0 + q * (1.0 - sq)))).astype(_MXU_DTYPE)
            sg = _sigmoid(z_ref[1, rows, :])
            d_f = df_scr[rows, :]
            dz_ref[1, rows, :] = (d_f * (1.0 - lb) * sg * (1.0 - sg)).astype(_MXU_DTYPE)
            dlb_scr[...] += jnp.sum(d_f * (1.0 - sg), axis=0, keepdims=True)

        _independent_trips(n_chunks, output_gradient, store_output_gradient)
        _token_contractions(do_scr, qi_scr, dst_scr, seq)
        lax.fori_loop(0, n_chunks, carry_state_gradient, jnp.zeros((LANES, LANES), F32))
        _independent_trips(n_chunks, [score_gradients, input_gradients], store_input_gradients)
        dg_ref[...] += jnp.sum(dgp_scr[...], axis=0, keepdims=True)
        _row_blocks(seq, input_activations)
        d_l0 = dlb_scr[...] * lb * (1.0 - lb)
        dlog_ref[0:1, :] = d_l0
        dlog_ref[1:2, :] = -d_l0

    tokens = batch * seq
    seq_buf = pltpu.VMEM((seq, LANES), F32)
    chunk_rows = pltpu.VMEM((n_chunks, LANES), F32)
    return pl.pallas_call(
        body, name="branch_b_bwd",
        grid=(N_BLK, batch),
        in_specs=[pl.BlockSpec((4, seq, LANES), lambda h, b: (0, b, h)),
                  pl.BlockSpec((None, n_chunks, LANES, LANES), lambda h, b: (b * N_BLK + h, 0, 0, 0)),
                  pl.BlockSpec((4, seq, LANES), lambda h, b: (0, b, h)),
                  pl.BlockSpec((seq, LANES), lambda h, b: (b, h)),
                  pl.BlockSpec(memory_space=pl.ANY),
                  pl.BlockSpec((2, LANES), lambda h, b: (0, h)),
                  pl.BlockSpec((1, LANES), lambda h, b: (0, 0))],
        out_specs=[pl.BlockSpec((4, seq, LANES), lambda h, b: (0, b, h)),
                   pl.BlockSpec((2, LANES), lambda h, b: (0, h)),
                   pl.BlockSpec((1, LANES), lambda h, b: (0, 0))],
        out_shape=[jax.ShapeDtypeStruct((N_GROUPS, tokens, D_MODEL), _MXU_DTYPE),
                   jax.ShapeDtypeStruct((2, D_MODEL), F32),
                   jax.ShapeDtypeStruct((1, LANES), F32)],
        scratch_shapes=[seq_buf] * 4 + [chunk_rows, chunk_rows, pltpu.VMEM((1, LANES), F32),
                                        pltpu.VMEM((n_chunks, LANES, LANES), F32)],
        input_output_aliases={4: 0},
        compiler_params=_params(("arbitrary", "arbitrary")),
    )(z, states, kept, dyb, dz, lb_logits, hg_g)


def _merge_tail(ya, yb, z, x2d, tgt2d, b_merge, final_g, pa, pb, wo):
    tokens, d = x2d.shape
    tm = min(TAIL_TOKENS, tokens)
    n_tiles = tokens // tm

    def body(ya_ref, yb_ref, z_ref, x_ref, t_ref, bm_ref, fg_ref, pa_hbm, pb_hbm, wo_hbm,
             dya_ref, dyb_ref, dx2_ref, dz_ref, loss_ref, dfg_ref, dbm_ref, dpa_hbm, dpb_hbm, dwo_hbm,
             pa_s, pb_s, wo_s, dpa_s, dpb_s, dwo_s, sems):
        i = pl.program_id(0)

        def together(pairs):
            copies = [pltpu.make_async_copy(src, dst, sems.at[k]) for k, (src, dst) in enumerate(pairs)]
            for cp in copies:
                cp.start()
            for cp in copies:
                cp.wait()

        @pl.when(i == 0)
        def _():
            together([(pa_hbm, pa_s), (pb_hbm, pb_s), (wo_hbm, wo_s)])
            dpa_s[...] = jnp.zeros_like(dpa_s)
            dpb_s[...] = jnp.zeros_like(dpb_s)
            dwo_s[...] = jnp.zeros_like(dwo_s)
            loss_ref[...] = jnp.zeros_like(loss_ref)
            dfg_ref[...] = jnp.zeros_like(dfg_ref)
            dbm_ref[...] = jnp.zeros_like(dbm_ref)

        ya_t = ya_ref[...]
        yb_t = yb_ref[...]
        out_a = _mm(ya_t, pa_s[...])
        out_b = _mm(yb_t, pb_s[...])
        g_a = _sigmoid(z_ref[0] + bm_ref[:, :d])
        g_b = _sigmoid(z_ref[1] + bm_ref[:, d:])
        mixed = g_a * out_a + g_b * out_b
        x2 = x_ref[...] + _mm(mixed, wo_s[...])
        r = lax.rsqrt(jnp.mean(x2 * x2, axis=-1, keepdims=True) + EPS)
        xn = x2 * r
        fg = fg_ref[...]
        diff = xn * fg - t_ref[...]
        loss_ref[...] += jnp.sum(diff * diff) * (0.5 / d)
        dy = diff * (1.0 / d)
        dfg_ref[...] += jnp.sum(dy * xn, axis=0, keepdims=True)
        dxn = dy * fg
        dx2 = r * (dxn - xn * jnp.mean(dxn * xn, axis=-1, keepdims=True))
        dx2_ref[...] = dx2
        dmixed = _mm_nt(dx2, wo_s[...])
        dwo_s[...] += _mm_tn(mixed, dx2)
        dgm_a = dmixed * out_a * g_a * (1.0 - g_a)
        dgm_b = dmixed * out_b * g_b * (1.0 - g_b)
        dz_ref[0] = dgm_a.astype(_MXU_DTYPE)
        dz_ref[1] = dgm_b.astype(_MXU_DTYPE)
        dbm_ref[:, :d] += jnp.sum(dgm_a, axis=0, keepdims=True)
        dbm_ref[:, d:] += jnp.sum(dgm_b, axis=0, keepdims=True)
        dout_a = dmixed * g_a
        dout_b = dmixed * g_b
        dpa_s[...] += _mm_tn(ya_t, dout_a)
        dpb_s[...] += _mm_tn(yb_t, dout_b)
        dya_ref[...] = _mm_nt(dout_a, pa_s[...])
        dyb_ref[...] = _mm_nt(dout_b, pb_s[...])

        @pl.when(i == n_tiles - 1)
        def _():
            together([(dpa_s, dpa_hbm), (dpb_s, dpb_hbm), (dwo_s, dwo_hbm)])

    tile = pl.BlockSpec((tm, d), lambda i: (i, 0))
    gm = pl.BlockSpec((2, tm, d), lambda i: (3, i, 0))
    row = lambda n: pl.BlockSpec((1, n), lambda i: (0, 0))
    hbm = pl.BlockSpec(memory_space=pl.ANY)
    act = jax.ShapeDtypeStruct((tokens, d), F32)
    mat = jax.ShapeDtypeStruct((d, d), F32)
    return pl.pallas_call(
        body, name="merge_tail",
        grid=(n_tiles,),
        in_specs=[tile, tile, gm, tile, tile, row(2 * d), row(d), hbm, hbm, hbm],
        out_specs=[tile, tile, tile, gm, row(LANES), row(d), row(2 * d), hbm, hbm, hbm],
        out_shape=[act, act, act, jax.ShapeDtypeStruct((N_GROUPS, tokens, d), _MXU_DTYPE),
                   jax.ShapeDtypeStruct((1, LANES), F32), jax.ShapeDtypeStruct((1, d), F32),
                   jax.ShapeDtypeStruct((1, 2 * d), F32), mat, mat, mat],
        scratch_shapes=[pltpu.VMEM((d, d), _MXU_DTYPE)] * 3 + [pltpu.VMEM((d, d), F32)] * 3
        + [pltpu.SemaphoreType.DMA((3,))],
        compiler_params=_params(("arbitrary",)),
    )(ya, yb, z, x2d, tgt2d, b_merge, final_g, pa, pb, wo)


def _inproj_dw_exchange(h_t, dz, scatter):
    d, tokens = h_t.shape
    tm = min(CONTRACT_TOKENS, tokens)
    n_i = tokens // tm
    half = d // 2

    def body(h_ref, dz_ref, *rest):
        n_in, n_out = scatter.n_in, scatter.n_out
        dw_hbm, land_hbm = rest[n_in:n_in + 2]
        acc, local_sems, send_sems, recv_sems = rest[n_in + 2 + n_out:n_in + 6 + n_out]
        carried = scatter.copies(rest[:n_in], rest[n_in + 2:n_in + 2 + n_out], rest[n_in + 6 + n_out:])
        s, i = pl.program_id(0), pl.program_id(1)
        x, y, c, _ = _mesh_position()

        @pl.when((s == 0) & (i == 0))
        def _():
            for cp in carried:
                cp.start()

        part = _mm(h_ref[...], dz_ref[...])
        buf = acc.at[s % 2]

        @pl.when(i == 0)
        def _():
            buf[...] = part

        @pl.when(i != 0)
        def _():
            buf[...] += part

        def copies(k):
            g = _SLOT_TO_GROUP[k]
            cols = pl.ds((g % 2) * D_MODEL, D_MODEL)
            src = acc.at[k % 2]
            mine = pltpu.make_async_copy(src, dw_hbm.at[g // 2, :, cols], local_sems.at[k % 2])
            theirs = _remote(src.at[pl.ds((1 - c) * half, half), :], land_hbm.at[g // 2, :, cols],
                             send_sems, recv_sems, k, (x, y, 1 - c))
            return mine, theirs

        for k in range(N_GROUPS):
            @pl.when((s == k) & (i == n_i - 1))
            def _(k=k):
                if k > 0:
                    mine, theirs = copies(k - 1)
                    mine.wait()
                    theirs.wait_send()
                mine, theirs = copies(k)
                mine.start()
                theirs.start()
                if k == N_GROUPS - 1:
                    mine.wait()
                    theirs.wait_send()
                    for kk in range(N_GROUPS):
                        copies(kk)[1].wait_recv()
                    for cp in carried:
                        cp.wait()

    hbm = pl.BlockSpec(memory_space=pl.ANY)
    more = scatter.plumbing(first_operand=2, first_output=2)
    return pl.pallas_call(
        body, name="inproj_dw_exchange",
        grid=(N_GROUPS, n_i),
        in_specs=[pl.BlockSpec((d, tm), lambda s, i: (0, i)),
                  pl.BlockSpec((None, tm, D_MODEL), lambda s, i: (s, i, 0))] + more[1],
        out_specs=[hbm, hbm] + more[2],
        out_shape=[jax.ShapeDtypeStruct((N_SHARDS, d, 2 * D_MODEL), F32),
                   jax.ShapeDtypeStruct((N_SHARDS, half, 2 * D_MODEL), F32)] + more[3],
        scratch_shapes=[pltpu.VMEM((2, d, D_MODEL), F32), pltpu.SemaphoreType.DMA((2,)),
                        pltpu.SemaphoreType.DMA((N_GROUPS,)), pltpu.SemaphoreType.DMA((N_GROUPS,))] + more[4],
        input_output_aliases=more[5],
        compiler_params=_params(("arbitrary", "arbitrary")),
    )(h_t, dz, *more[0])


def _inproj_dx(dz, w_all, x2d, dx2, norm_g, scatter):
    tokens, d = x2d.shape
    tm = min(TAIL_TOKENS, tokens)
    n_tiles = tokens // tm

    def body(dz_ref, w_hbm, x_ref, dx2_ref, g_ref, *rest):
        n_in, n_out = scatter.n_in, scatter.n_out
        dx_ref, dg_ref = rest[n_in:n_in + 2]
        w_res, load_sems = rest[n_in + 2 + n_out:n_in + 4 + n_out]
        copies = scatter.copies(rest[:n_in], rest[n_in + 2:n_in + 2 + n_out], rest[n_in + 4 + n_out:])
        i = pl.program_id(0)

        @pl.when(i == 0)
        def _():
            for cp in copies:
                cp.start()
            loads = [pltpu.make_async_copy(w_hbm.at[g // 2, :, pl.ds((g % 2) * D_MODEL, D_MODEL)],
                                           w_res.at[:, pl.ds(slot * D_MODEL, D_MODEL)], load_sems.at[slot])
                     for slot, g in enumerate(_SLOT_TO_GROUP)]
            for cp in loads:
                cp.start()
            for cp in loads:
                cp.wait()
            dg_ref[...] = jnp.zeros_like(dg_ref)

        dz_all = jnp.concatenate([dz_ref[s] for s in range(N_GROUPS)], axis=1)
        dh = jnp.transpose(_mm_nt(w_res[...], dz_all))
        x = x_ref[...]
        r = lax.rsqrt(jnp.mean(x * x, axis=-1, keepdims=True) + EPS)
        xn = x * r
        dg_ref[...] += jnp.sum(dh * xn, axis=0, keepdims=True)
        dxn = dh * g_ref[...]
        dx_ref[...] = r * (dxn - xn * jnp.mean(dxn * xn, axis=-1, keepdims=True)) + dx2_ref[...]

        @pl.when(i == n_tiles - 1)
        def _():
            for cp in copies:
                cp.wait()

    tile = pl.BlockSpec((tm, d), lambda i: (i, 0))
    hbm = pl.BlockSpec(memory_space=pl.ANY)
    more = scatter.plumbing(first_operand=5, first_output=2)
    return pl.pallas_call(
        body, name="inproj_dx", grid=(n_tiles,),
        in_specs=[pl.BlockSpec((N_GROUPS, tm, D_MODEL), lambda i: (0, i, 0)), hbm, tile, tile,
                  pl.BlockSpec((1, d), lambda i: (0, 0))] + more[1],
        out_specs=[tile, pl.BlockSpec((1, d), lambda i: (0, 0))] + more[2],
        out_shape=[jax.ShapeDtypeStruct((tokens, d), F32), jax.ShapeDtypeStruct((1, d), F32)] + more[3],
        scratch_shapes=[pltpu.VMEM((d, N_GROUPS * D_MODEL), _MXU_DTYPE), pltpu.SemaphoreType.DMA((N_GROUPS,))] + more[4],
        input_output_aliases=more[5],
        compiler_params=_params(("arbitrary",)),
    )(dz, w_all, x2d, dx2, norm_g, *more[0])


def _row_tile(rows, cols, itemsize=4, budget=2 * 1024 * 1024):
    tr = rows
    while tr * cols * itemsize > budget and tr % 16 == 0:
        tr //= 2
    return tr


def _cast_into_slot(a, chip, dtype, name):
    rows, cols = a.shape
    tr = _row_tile(rows, cols)

    def body(chip_ref, a_ref, o_ref):
        del chip_ref
        o_ref[...] = a_ref[...].astype(dtype)

    grid_spec = pltpu.PrefetchScalarGridSpec(
        num_scalar_prefetch=1, grid=(rows // tr,),
        in_specs=[pl.BlockSpec((tr, cols), lambda i, chip_ref: (i, 0))],
        out_specs=pl.BlockSpec((None, tr, cols), lambda i, chip_ref: (chip_ref[0], i, 0)))
    return pl.pallas_call(body, name=name, grid_spec=grid_spec,
                          out_shape=jax.ShapeDtypeStruct((N_SHARDS, rows, cols), dtype),
                          compiler_params=_params(("arbitrary",)))(chip, a)


def _sum_slots(stack, name):
    n, rows, cols = stack.shape
    tr = _row_tile(rows, cols * n)

    def body(s_ref, o_ref):
        total = s_ref[0].astype(F32)
        for k in range(1, n):
            total = total + s_ref[k].astype(F32)
        o_ref[...] = total

    return pl.pallas_call(body, name=name, grid=(rows // tr,),
                          in_specs=[pl.BlockSpec((n, tr, cols), lambda i: (0, i, 0))],
                          out_specs=pl.BlockSpec((tr, cols), lambda i: (i, 0)),
                          out_shape=jax.ShapeDtypeStruct((rows, cols), F32),
                          compiler_params=_params(("parallel",)))(stack)


def _add_half(full, landed, place, name):
    n, rows, cols = full.shape
    half = rows // 2
    tr = _row_tile(half, cols)
    nb = half // tr

    def body(place_ref, a_ref, b_ref, o_ref, own_ref):
        total = (a_ref[...] + b_ref[...]).astype(_MXU_DTYPE)
        o_ref[...] = total

        @pl.when(pl.program_id(1) == place_ref[1])
        def _():
            own_ref[...] = total

    grid_spec = pltpu.PrefetchScalarGridSpec(
        num_scalar_prefetch=1, grid=(nb, n),
        in_specs=[pl.BlockSpec((None, tr, cols), lambda i, j, place_ref: (j, place_ref[0] * nb + i, 0)),
                  pl.BlockSpec((None, tr, cols), lambda i, j, place_ref: (j, i, 0))],
        out_specs=[pl.BlockSpec((None, tr, cols), lambda i, j, place_ref: (j, i, 0)),
                   pl.BlockSpec((None, tr, cols), lambda i, j, place_ref: (place_ref[1], i, 0))])
    shape = jax.ShapeDtypeStruct((n, half, cols), _MXU_DTYPE)
    return pl.pallas_call(body, name=name, grid_spec=grid_spec, out_shape=[shape, shape],
                          compiler_params=_params(("parallel", "arbitrary")))(place, full, landed)


def _adamw_update(w, grad, m, v):
    c1 = 1.0 - ADAM_B1 ** ADAM_STEP
    c2 = 1.0 - ADAM_B2 ** ADAM_STEP
    nm = ADAM_B1 * m + (1.0 - ADAM_B1) * grad
    nv = ADAM_B2 * v + (1.0 - ADAM_B2) * (grad * grad)
    return (-ADAM_LR) * ((nm / c1) / (jnp.sqrt(nv / c2) + ADAM_EPS) + ADAM_WD * w), nm, nv


def _adamw(w, g, m, v, name):
    rows, cols = w.shape
    tr = _row_tile(rows, cols, budget=1024 * 1024)

    def body(w_ref, g_ref, m_ref, v_ref, d_ref, nm_ref, nv_ref):
        d_ref[...], nm_ref[...], nv_ref[...] = _adamw_update(w_ref[...], g_ref[...], m_ref[...], v_ref[...])

    spec = pl.BlockSpec((tr, cols), lambda i: (i, 0))
    shape = jax.ShapeDtypeStruct((rows, cols), F32)
    return pl.pallas_call(body, name=name, grid=(rows // tr,), in_specs=[spec] * 4, out_specs=[spec] * 3,
                          out_shape=[shape] * 3, compiler_params=_params(("parallel",)))(w, g, m, v)


def _adamw_halves(w, g_mine, g_sibling, m, v, core, name):
    rows, cols = w.shape
    half = rows // 2
    tr = _row_tile(half, cols, budget=1024 * 1024)
    nb = half // tr

    def body(core_ref, w_ref, gm_ref, gs_ref, m_ref, v_ref, g_ref, d_ref, nm_ref, nv_ref):
        mine = pl.program_id(0) // nb == core_ref[0]
        grad = jnp.where(mine, gm_ref[...], gs_ref[...])
        g_ref[...] = grad
        d_ref[...], nm_ref[...], nv_ref[...] = _adamw_update(w_ref[...], grad, m_ref[...], v_ref[...])

    spec = pl.BlockSpec((tr, cols), lambda i, core_ref: (i, 0))
    mine_spec = pl.BlockSpec((tr, cols), lambda i, core_ref: (jnp.where(i // nb == core_ref[0], i % nb, 0), 0))
    sibling_spec = pl.BlockSpec((tr, cols), lambda i, core_ref: (jnp.where(i // nb == core_ref[0], 0, i % nb), 0))
    grid_spec = pltpu.PrefetchScalarGridSpec(num_scalar_prefetch=1, grid=(rows // tr,),
                                             in_specs=[spec, mine_spec, sibling_spec, spec, spec], out_specs=[spec] * 4)
    shape = jax.ShapeDtypeStruct((rows, cols), F32)
    return pl.pallas_call(body, name=name, grid_spec=grid_spec, out_shape=[shape] * 4,
                          compiler_params=_params(("parallel",)))(core, w, g_mine, g_sibling, m, v)


def _local_step(x, loss_target, gather, reduction, b_merge, conv_b, rg_wx, rg_bx, rg_wa, rg_ba, rg_lambda,
                hg_lb_logits, hg_norm_g, norm_g, final_norm_g):
    batch, seq, d = x.shape
    x2d = x.reshape(batch * seq, d)
    tgt2d = loss_target.reshape(batch * seq, d)
    z, h_t, (w_all, pa, pb, wo), cw_all = _inproj_fwd_gather(x2d, norm_g, *gather)
    pa, pb, wo = (t.reshape(d, d) for t in (pa, pb, wo))
    conv_w = jnp.transpose(cw_all, (1, 0, 2)).reshape(CONV_WIDTH, d)
    lru = (conv_w, conv_b, rg_wx, rg_bx, rg_wa, rg_ba, rg_lambda)
    ya, hl, kept = _branch_a_fwd(z, *lru, batch, seq)
    yb, states, kept_b = _branch_b_fwd(z, hg_lb_logits, hg_norm_g, batch, seq)
    dya, dyb, dx2, dz, loss, d_final_g, d_b_merge, d_pa, d_pb, d_wo = _merge_tail(
        ya, yb, z, x2d, tgt2d, b_merge, final_norm_g, pa, pb, wo)
    dz, d_lb_logits, d_hg_g = _branch_b_bwd(z, states, kept_b, dyb, dz, hg_lb_logits, hg_norm_g, batch, seq)
    dz, d_conv_w, d_conv_b, d_wx, d_bx, d_wa, d_ba, d_lam = _branch_a_bwd(
        z, hl, kept, dya, dz, conv_w, rg_wx, rg_wa, rg_lambda, batch, seq)
    small = dict(b_merge=d_b_merge, conv_w=d_conv_w, conv_b=d_conv_b, rg_wx=d_wx, rg_bx=d_bx, rg_wa=d_wa,
                 rg_ba=d_ba, rg_lambda=d_lam, hg_lb_logits=d_lb_logits, hg_norm_g=d_hg_g,
                 norm_g=jnp.zeros((1, d), F32), final_norm_g=d_final_g)
    first, second = reduction
    d_w_in, landed_w_in, *scattered_first = _inproj_dw_exchange(h_t, dz, first((d_pa, d_pb, d_wo), small))
    grad_x, d_norm_g, *scattered_second = _inproj_dx(dz, w_all, x2d, dx2, norm_g, scatter=second(d_w_in, landed_w_in))
    return loss[0, 0], grad_x.reshape(batch, seq, d), d_norm_g, (scattered_first, scattered_second)


_SMALL_ORDER = ("b_merge", "conv_w", "conv_b", "rg_wx", "rg_bx", "rg_wa", "rg_ba", "rg_lambda", "hg_lb_logits",
                "hg_norm_g", "norm_g", "final_norm_g")
N_DEV = 8
PIECE_ROWS = 272


def _pack_small(tree):
    flat = jnp.concatenate([tree[k].reshape(-1) for k in _SMALL_ORDER])
    flat = jnp.pad(flat, (0, N_DEV * PIECE_ROWS * LANES - flat.shape[0]))
    return flat.reshape(N_DEV * PIECE_ROWS, LANES)


def _unpack_small(packed, like):
    flat = packed.reshape(-1)
    out, pos = {}, 0
    for k in _SMALL_ORDER:
        n = like[k].size
        out[k] = flat[pos:pos + n].reshape(like[k].shape)
        pos += n
    return out


def _mesh_position():
    x, y, c = lax.axis_index("x"), lax.axis_index("y"), lax.axis_index("c")
    other_chips = [(1 - x, y), (x, 1 - y), (1 - x, 1 - y)]
    return x, y, c, other_chips


def _other_devices(x, y, c):
    flips = [(fx, fy, fc) for fx in (0, 1) for fy in (0, 1) for fc in (0, 1) if (fx, fy, fc) != (0, 0, 0)]
    return [(jnp.where(fx, 1 - x, x), jnp.where(fy, 1 - y, y), jnp.where(fc, 1 - c, c)) for fx, fy, fc in flips]


def _remote(src, dst, send_sems, recv_sems, k, device):
    return pltpu.make_async_remote_copy(src_ref=src, dst_ref=dst, send_sem=send_sems.at[k], recv_sem=recv_sems.at[k],
                                        device_id=device, device_id_type=MESH)


def _exchange_halves(bigs, small):
    n_big = len(bigs)
    n_sem = n_big + N_DEV - 1

    def body(*refs):
        srcs, small_src = refs[:n_big], refs[n_big]
        outs, small_out = refs[n_big + 1:2 * n_big + 1], refs[2 * n_big + 1]
        send_sems, recv_sems, local_sem = refs[2 * n_big + 2:]
        x, y, c, _ = _mesh_position()
        me, sibling = 4 * x + 2 * y + c, (x, y, 1 - c)
        mine = pltpu.make_async_copy(small_src.at[pl.ds(me * PIECE_ROWS, PIECE_ROWS), :], small_out.at[me], local_sem)
        mine.start()
        copies = []
        for a in range(n_big):
            hs = srcs[a].shape[1] // 2
            copies.append(_remote(srcs[a].at[:, pl.ds((1 - c) * hs, hs), :], outs[a], send_sems, recv_sems, a, sibling))
        for k, (px, py, pc) in enumerate(_other_devices(x, y, c)):
            piece = small_src.at[pl.ds((4 * px + 2 * py + pc) * PIECE_ROWS, PIECE_ROWS), :]
            copies.append(_remote(piece, small_out.at[me], send_sems, recv_sems, n_big + k, (px, py, pc)))
        for cp in copies:
            cp.start()
        for cp in copies:
            cp.wait()
        mine.wait()

    hbm = pl.BlockSpec(memory_space=pl.ANY)
    out_shape = [jax.ShapeDtypeStruct((g.shape[0], g.shape[1] // 2, g.shape[2]), F32) for g in bigs]
    out_shape.append(jax.ShapeDtypeStruct((N_DEV, PIECE_ROWS, LANES), F32))
    return pl.pallas_call(
        body, name="exchange_halves",
        in_specs=[hbm] * (n_big + 1), out_specs=[hbm] * (n_big + 1), out_shape=out_shape,
        scratch_shapes=[pltpu.SemaphoreType.DMA((n_sem,)), pltpu.SemaphoreType.DMA((n_sem,)), pltpu.SemaphoreType.DMA],
    )(*bigs, small)


class _Scatter:
    def __init__(self, bigs, by_chip, small=None):
        self.bigs, self.by_chip, self.small = list(bigs), list(by_chip), small
        self.n_big = len(self.bigs)
        self.n_in = 2 * self.n_big + (small is not None)
        self.n_out = self.n_big + (small is not None)
        self.n_scratch = 2 + (small is not None)

    def plumbing(self, first_operand, first_output):
        hbm = pl.BlockSpec(memory_space=pl.ANY)
        n_sem = 3 * self.n_big + (N_DEV - 1 if self.small is not None else 0)
        operands = self.bigs + self.by_chip + ([self.small] if self.small is not None else [])
        out_shapes = [jax.ShapeDtypeStruct(g.shape, g.dtype) for g in self.by_chip]
        scratch = [pltpu.SemaphoreType.DMA((n_sem,)), pltpu.SemaphoreType.DMA((n_sem,))]
        if self.small is not None:
            out_shapes.append(jax.ShapeDtypeStruct((N_DEV, PIECE_ROWS, LANES), F32))
            scratch.append(pltpu.SemaphoreType.DMA)
        aliases = {first_operand + self.n_big + a: first_output + a for a in range(self.n_big)}
        return operands, [hbm] * self.n_in, [hbm] * self.n_out, out_shapes, scratch, aliases

    def copies(self, in_refs, out_refs, scratch_refs):
        srcs, outs = in_refs[:self.n_big], out_refs[:self.n_big]
        send_sems, recv_sems = scratch_refs[:2]
        x, y, c, chips = _mesh_position()
        chip, me = 2 * x + y, 4 * x + 2 * y + c
        copies = []
        for a in range(self.n_big):
            for j, (cx, cy) in enumerate(chips):
                copies.append(_remote(srcs[a].at[2 * cx + cy], outs[a].at[chip], send_sems, recv_sems, 3 * a + j,
                                      (cx, cy, c)))
        if self.small is not None:
            small_src, small_out = in_refs[2 * self.n_big], out_refs[self.n_big]
            copies.append(pltpu.make_async_copy(small_src, small_out.at[me], scratch_refs[2]))
            for k, peer in enumerate(_other_devices(x, y, c)):
                copies.append(_remote(small_src, small_out.at[me], send_sems, recv_sems, 3 * self.n_big + k, peer))
        return copies


def _swap_halves(halves, vec):
    n_big = len(halves)

    def body(*refs):
        srcs, vec_src = refs[:n_big], refs[n_big]
        outs, vec_out = refs[n_big + 1:2 * n_big + 1], refs[2 * n_big + 1]
        send_sems, recv_sems, local_sem = refs[2 * n_big + 2:]
        x, y, c, _ = _mesh_position()
        me = 4 * x + 2 * y + c
        copies = [pltpu.make_async_copy(vec_src, vec_out.at[me], local_sem)]
        copies += [_remote(srcs[a], outs[a], send_sems, recv_sems, a, (x, y, 1 - c)) for a in range(n_big)]
        copies += [_remote(vec_src, vec_out.at[me], send_sems, recv_sems, n_big + k, peer)
                   for k, peer in enumerate(_other_devices(x, y, c))]
        for cp in copies:
            cp.start()
        for cp in copies:
            cp.wait()

    hbm = pl.BlockSpec(memory_space=pl.ANY)
    n_sem = n_big + N_DEV - 1
    return pl.pallas_call(
        body, name="swap_halves",
        in_specs=[hbm] * (n_big + 1), out_specs=[hbm] * (n_big + 1),
        out_shape=[jax.ShapeDtypeStruct(h.shape, F32) for h in halves] + [jax.ShapeDtypeStruct((N_DEV,) + vec.shape, F32)],
        scratch_shapes=[pltpu.SemaphoreType.DMA((n_sem,)), pltpu.SemaphoreType.DMA((n_sem,)), pltpu.SemaphoreType.DMA],
    )(*halves, vec)


def kernel(x, w_in, b_merge, conv_w, conv_b, rg_wx, rg_bx, rg_wa, rg_ba, rg_lambda, hg_lb_logits, hg_norm_g, proj_a, proj_b, w_out, norm_g, final_norm_g, loss_target, m_w_in, m_b_merge, m_conv_w, m_conv_b, m_rg_wx, m_rg_bx, m_rg_wa, m_rg_ba, m_rg_lambda, m_hg_lb_logits, m_hg_norm_g, m_proj_a, m_proj_b, m_w_out, m_norm_g, m_final_norm_g, v_w_in, v_b_merge, v_conv_w, v_conv_b, v_rg_wx, v_rg_bx, v_rg_wa, v_rg_ba, v_rg_lambda, v_hg_lb_logits, v_hg_norm_g, v_proj_a, v_proj_b, v_w_out, v_norm_g, v_final_norm_g):
    d = D_MODEL
    weights = dict(w_in=w_in, b_merge=b_merge, conv_w=conv_w, conv_b=conv_b, rg_wx=rg_wx, rg_bx=rg_bx, rg_wa=rg_wa,
                   rg_ba=rg_ba, rg_lambda=rg_lambda, hg_lb_logits=hg_lb_logits, hg_norm_g=hg_norm_g, proj_a=proj_a,
                   proj_b=proj_b, w_out=w_out, norm_g=norm_g, final_norm_g=final_norm_g)
    m = dict(w_in=m_w_in, b_merge=m_b_merge, conv_w=m_conv_w, conv_b=m_conv_b, rg_wx=m_rg_wx, rg_bx=m_rg_bx,
             rg_wa=m_rg_wa, rg_ba=m_rg_ba, rg_lambda=m_rg_lambda, hg_lb_logits=m_hg_lb_logits, hg_norm_g=m_hg_norm_g,
             proj_a=m_proj_a, proj_b=m_proj_b, w_out=m_w_out, norm_g=m_norm_g, final_norm_g=m_final_norm_g)
    v = dict(w_in=v_w_in, b_merge=v_b_merge, conv_w=v_conv_w, conv_b=v_conv_b, rg_wx=v_rg_wx, rg_bx=v_rg_bx,
             rg_wa=v_rg_wa, rg_ba=v_rg_ba, rg_lambda=v_rg_lambda, hg_lb_logits=v_hg_lb_logits, hg_norm_g=v_hg_norm_g,
             proj_a=v_proj_a, proj_b=v_proj_b, w_out=v_w_out, norm_g=v_norm_g, final_norm_g=v_final_norm_g)
    big_names = ("w_in", "proj_a", "proj_b", "w_out")

    core = lax.axis_index("c").astype(jnp.int32).reshape(1)
    chip = (2 * lax.axis_index("x") + lax.axis_index("y")).astype(jnp.int32)

    slotted = [_cast_into_slot(weights[k][0], chip.reshape(1), _MXU_DTYPE, f"cast_{k}") for k in big_names]
    conv_slotted = _cast_into_slot(conv_w[0], chip.reshape(1), F32, "slot_conv_w")

    small_shapes = {}

    place = jnp.concatenate([core, chip.reshape(1)])

    def reduce_proj_and_small(proj_grads, small_grads):
        small_shapes.update({k: t.shape for k, t in small_grads.items()})
        bigs = [g.reshape(N_SHARDS, d // N_SHARDS, d) for g in proj_grads]
        *landed, small_landed = _exchange_halves(bigs, _pack_small(small_grads))
        sums = [_add_half(g, l, place, f"add_half_{1 + a}") for a, (g, l) in enumerate(zip(bigs, landed))]
        return _Scatter([s[0] for s in sums], [s[1] for s in sums], _sum_slots(small_landed, "sum_small"))

    def reduce_w_in(d_w_in, landed):
        partial, own_slot = _add_half(d_w_in, landed, place, "add_half_0")
        return _Scatter([partial], [own_slot])

    loss_part, grad_x, d_norm_g, ((*by_chip_proj, small_all), by_chip_w_in) = _local_step(
        x, loss_target, (slotted, conv_slotted, chip.reshape(1)), (reduce_proj_and_small, reduce_w_in),
        b_merge, conv_b, rg_wx[0], rg_bx.reshape(1, d), rg_wa[0], rg_ba.reshape(1, d), rg_lambda, hg_lb_logits,
        hg_norm_g, norm_g, final_norm_g.reshape(1, d))
    mine = [_sum_slots(s, f"sum_chips_{a}") for a, s in enumerate(by_chip_w_in + by_chip_proj)]
    late = jnp.concatenate([d_norm_g.reshape(SUBLANES, LANES), jnp.full((SUBLANES, LANES), loss_part, F32)])
    *theirs, late_parts = _swap_halves(mine, late)
    late_sum = _sum_slots(late_parts, "sum_late")
    loss = late_sum[SUBLANES, 0]
    small_red = _unpack_small(small_all, {k: jax.ShapeDtypeStruct(s, F32) for k, s in small_shapes.items()})
    small_red["norm_g"] = late_sum[:SUBLANES].reshape(1, d)

    grads, delta, new_m, new_v = {}, {}, {}, {}
    for k, g_mine, g_theirs in zip(big_names, mine, theirs):
        out = _adamw_halves(weights[k][0], g_mine, g_theirs, m[k][0], v[k][0], core, f"adamw_{k}")
        grads[k], delta[k], new_m[k], new_v[k] = (t.reshape(weights[k].shape) for t in out)
    cols = d // N_SHARDS
    g_conv = lax.dynamic_slice(small_red["conv_w"], (0, chip * cols), (CONV_WIDTH, cols))
    grads["conv_w"] = g_conv.reshape(conv_w.shape)
    dl, nm, nv = _adamw(conv_w[0], g_conv, m_conv_w[0], v_conv_w[0], "adamw_conv_w")
    delta["conv_w"], new_m["conv_w"], new_v["conv_w"] = (t.reshape(conv_w.shape) for t in (dl, nm, nv))
    rest = [k for k in _SMALL_ORDER if k != "conv_w"]
    like = {k: (weights[k] if k != "conv_w" else jnp.zeros((CONV_WIDTH, d), F32)) for k in _SMALL_ORDER}
    packs = [_pack_small({k: (t[k] if k != "conv_w" else like[k]) for k in _SMALL_ORDER}) for t in (weights, m, v)]
    g_pack = _pack_small({k: small_red[k].reshape(like[k].shape) for k in _SMALL_ORDER})
    outs = [_unpack_small(p, like) for p in _adamw(packs[0], g_pack, packs[1], packs[2], "adamw_small")]
    for k in rest:
        grads[k] = small_red[k].reshape(weights[k].shape)
        delta[k], new_m[k], new_v[k] = outs[0][k], outs[1][k], outs[2][k]

    order = ("w_in", "b_merge", "conv_w", "conv_b", "rg_wx", "rg_bx", "rg_wa", "rg_ba", "rg_lambda", "hg_lb_logits",
             "hg_norm_g", "proj_a", "proj_b", "w_out", "norm_g", "final_norm_g")
    return (loss, grad_x, *[grads[k] for k in order], *[delta[k] for k in order], *[new_m[k] for k in order],
            *[new_v[k] for k in order])
```

```python
import functools

import jax
import jax.numpy as jnp
from jax import lax
from jax.experimental import pallas as pl
from jax.experimental.pallas import tpu as pltpu

F32 = jnp.float32
_MXU_DTYPE = jnp.bfloat16

D_MODEL = 1024
LANES = 128
SUBLANES = 8
N_BLK = D_MODEL // LANES
N_GROUPS = 8
N_SHARDS = 4
CONV_WIDTH = 4
LRU_C = 8.0
CHUNK = 64
CHUNKS_IN_FLIGHT = 16
FWD_CHUNKS_IN_FLIGHT = 32
SCAN_UNROLL = 4
HG_SCALE = float(LANES) ** -0.5
EPS = 1e-6
ADAM_LR, ADAM_B1, ADAM_B2, ADAM_EPS, ADAM_WD, ADAM_STEP = 0.001, 0.9, 0.999, 1e-08, 0.01, 10
MATMUL_TOKENS = 1024
TAIL_TOKENS = 256
CONTRACT_TOKENS = 2048
VMEM_LIMIT = 56 * 1024 * 1024
VMEM_LIMIT_BIG = 60 * 1024 * 1024
MESH = pl.DeviceIdType.MESH

_SLOT_TO_GROUP = (2, 3, 4, 5, 0, 1, 6, 7)


def _mm(a, b):
    return lax.dot_general(a.astype(_MXU_DTYPE), b.astype(_MXU_DTYPE), (((1,), (0,)), ((), ())),
                           preferred_element_type=F32)


def _mm_nt(a, b):
    return lax.dot_general(a.astype(_MXU_DTYPE), b.astype(_MXU_DTYPE), (((1,), (1,)), ((), ())),
                           preferred_element_type=F32)


def _mm_tn(a, b):
    return lax.dot_general(a.astype(_MXU_DTYPE), b.astype(_MXU_DTYPE), (((0,), (0,)), ((), ())),
                           preferred_element_type=F32)


def _sigmoid(x):
    return 0.5 * jnp.tanh(0.5 * x) + 0.5


def _log1p_pos(y):
    series = y * (1.0 - y * (0.5 - y * (1.0 / 3.0 - y * 0.25)))
    return jnp.where(y < 0.01, series, jnp.log(1.0 + y))


def _softplus(x):
    return jnp.maximum(x, 0.0) + _log1p_pos(jnp.exp(-jnp.abs(x)))


def _shift_down(x, n):
    rolled = pltpu.roll(x, n, 0)
    edge = SUBLANES if (n < SUBLANES and x.shape[0] > SUBLANES) else x.shape[0]
    rows = lax.broadcasted_iota(jnp.int32, (edge, x.shape[1]), 0)
    head = jnp.where(rows >= n, rolled[:edge], 0.0)
    return head if edge == x.shape[0] else jnp.concatenate([head, rolled[edge:]], axis=0)


def _shift_up(x, n):
    size = x.shape[0]
    rolled = pltpu.roll(x, size - n, 0)
    edge = SUBLANES if (n < SUBLANES and size > SUBLANES) else size
    rows = lax.broadcasted_iota(jnp.int32, (edge, x.shape[1]), 0)
    tail = jnp.where(rows < edge - n, rolled[size - edge:], 0.0)
    return tail if edge == size else jnp.concatenate([rolled[:size - edge], tail], axis=0)


def _params(dims, vmem=VMEM_LIMIT):
    return pltpu.CompilerParams(dimension_semantics=dims, vmem_limit_bytes=vmem)


def _slot_of_group(g):
    return jnp.where(g < 2, g + 4, jnp.where(g < 6, g - 2, g))


def _inproj_fwd_gather(x2d, norm_g, slotted, conv_slotted, chip):
    tokens, d = x2d.shape
    tm = min(MATMUL_TOKENS, tokens)
    n_tiles = tokens // tm
    n_big = len(slotted)
    n_sem = 6 * (n_big + 1) + 3
    last_pass = N_GROUPS - 1

    def shard_of(k, chip_id):
        x, y = chip_id // 2, chip_id % 2
        return 2 * jnp.where(k % 2 == 1, 1 - x, x) + jnp.where(k // 2 == 1, 1 - y, y)

    def body(chip_ref, x_ref, g_ref, *rest):
        bufs, cw = rest[n_big + 1:2 * n_big + 1], rest[2 * n_big + 1]
        z_ref, ht_ref = rest[2 * n_big + 2:2 * n_big + 4]
        h_all, slab, send_sems, recv_sems, slab_sems = rest[2 * n_big + 4:]
        del chip_ref
        p, i = pl.program_id(0), pl.program_id(1)
        x, y, c, chips = _mesh_position()
        me, sibling = 2 * x + y, (x, y, 1 - c)

        pieces = [(0, 0), (0, 1)] + [(a, None) for a in range(1, n_big)]

        def half(piece, slot, which):
            a, q = pieces[piece]
            hs = bufs[a].shape[1] // 2
            cols = slice(None) if q is None else pl.ds(q * D_MODEL, D_MODEL)
            return bufs[a].at[slot, pl.ds(which * hs, hs), cols]

        def send(piece, j):
            mine = half(piece, me, c)
            return _remote(mine, mine, send_sems, recv_sems, 6 * piece + j, (chips[j][0], chips[j][1], c))

        def arrival(piece, j):
            landed = half(piece, 2 * chips[j][0] + chips[j][1], c)
            return _remote(landed, landed, send_sems, recv_sems, 6 * piece + j, (chips[j][0], chips[j][1], c))

        def passed_on(piece, j, which):
            landed = half(piece, 2 * chips[j][0] + chips[j][1], which)
            return _remote(landed, landed, send_sems, recv_sems, 6 * piece + 3 + j, sibling)

        def conv_copy(j, slot):
            return _remote(cw.at[slot], cw.at[slot], send_sems, recv_sems, 6 * len(pieces) + j,
                           (chips[j][0], chips[j][1], c))

        def land(piece, j):
            arrival(piece, j).wait_recv()
            passed_on(piece, j, c).start()
            passed_on(piece, j, 1 - c).wait_recv()

        def slab_copy(pv):
            src = bufs[0].at[shard_of(pv // 2, me), :, pl.ds((pv % 2) * D_MODEL, D_MODEL)]
            return pltpu.make_async_copy(src, slab.at[pv % 2], slab_sems.at[pv % 2])

        @pl.when((p == 0) & (i == 0))
        def _():
            for q in range(2):
                send(q, 0).start()
                send(q, 1).start()
            slab_copy(0).start()

        @pl.when(i == 0)
        def _():
            for pv in range(N_GROUPS):
                @pl.when(p == pv)
                def _(pv=pv):
                    slab_copy(pv).wait()

        rows = pl.ds(pl.multiple_of(i * tm, tm), tm)

        @pl.when(p == 0)
        def _():
            xt = x_ref[...]
            r = lax.rsqrt(jnp.mean(xt * xt, axis=-1, keepdims=True) + EPS)
            h = (xt * r) * g_ref[...]
            h_all[rows, :] = h.astype(_MXU_DTYPE)
            ht_ref[...] = jnp.transpose(h).astype(_MXU_DTYPE)

        z_ref[...] = _mm(h_all[rows, :], slab[p % 2])

        def relay(q):
            landed = half(q, 2 * chips[q][0] + chips[q][1], c)
            to = chips[1 - q]
            return _remote(landed, landed, send_sems, recv_sems, 6 * q + 2, (to[0], to[1], c))

        landings = {1: [(0, 0)], 2: [(1, 0), (1, 1)], 3: [(0, 1)], 5: [(0, 2)], 6: [(1, 2)]}

        def end_of_pass(pv):
            for q, j in landings.get(pv - 1, []):
                land(q, j)
                if j == q:
                    relay(q).start()
            if pv - 1 == 1:
                for piece in range(2, len(pieces)):
                    for jj in range(3):
                        send(piece, jj).start()
                for jj in range(3):
                    conv_copy(jj, me).start()
            if pv - 1 in (5, 6):
                for piece in range(2, len(pieces)):
                    for jj in ((0, 1) if pv - 1 == 5 else (2,)):
                        land(piece, jj)
            slab_copy(pv).start()

        @pl.when(i == n_tiles - 1)
        def _():
            for pv in range(1, N_GROUPS):
                pl.when(p == pv - 1)(functools.partial(end_of_pass, pv))

        @pl.when((p == last_pass) & (i == n_tiles - 1))
        def _():
            for j in range(3):
                conv_copy(j, 2 * chips[j][0] + chips[j][1]).wait_recv()
            for piece in range(len(pieces)):
                for j in range(3):
                    (relay(piece) if (piece < 2 and j == 2) else send(piece, j)).wait_send()
                    passed_on(piece, j, c).wait_send()
            for j in range(3):
                conv_copy(j, me).wait_send()

    def z_index(p, i, chip_ref):
        g = 2 * shard_of(p // 2, chip_ref[0]) + p % 2
        return (_slot_of_group(g), i, 0)

    def first_pass_tile(p, i, chip_ref):
        return jnp.where(p == 0, i, n_tiles - 1)

    hbm = pl.BlockSpec(memory_space=pl.ANY)
    operands = list(slotted) + [conv_slotted]
    grid_spec = pltpu.PrefetchScalarGridSpec(
        num_scalar_prefetch=1, grid=(N_GROUPS, n_tiles),
        in_specs=[pl.BlockSpec((tm, d), lambda p, i, chip_ref: (first_pass_tile(p, i, chip_ref), 0)),
                  pl.BlockSpec((1, d), lambda p, i, chip_ref: (0, 0))] + [hbm] * (n_big + 1),
        out_specs=[hbm] * (n_big + 1) + [pl.BlockSpec((None, tm, D_MODEL), z_index),
                                         pl.BlockSpec((d, tm), lambda p, i, chip_ref: (0, first_pass_tile(p, i, chip_ref)))],
        scratch_shapes=[pltpu.VMEM((tokens, d), _MXU_DTYPE), pltpu.VMEM((2, d, D_MODEL), _MXU_DTYPE),
                        pltpu.SemaphoreType.DMA((n_sem,)), pltpu.SemaphoreType.DMA((n_sem,)),
                        pltpu.SemaphoreType.DMA((2,))])
    out = pl.pallas_call(
        body, name="inproj_fwd_gather", grid_spec=grid_spec,
        out_shape=[jax.ShapeDtypeStruct(a.shape, a.dtype) for a in operands]
        + [jax.ShapeDtypeStruct((N_GROUPS, tokens, D_MODEL), F32), jax.ShapeDtypeStruct((d, tokens), _MXU_DTYPE)],
        input_output_aliases={3 + a: a for a in range(n_big + 1)},
        compiler_params=_params(("arbitrary", "arbitrary")),
    )(chip, x2d, norm_g, *operands)
    return out[n_big + 1], out[n_big + 2], out[:n_big], out[n_big]


def _lane_blocks(x):
    return [x[:, k * LANES:(k + 1) * LANES] for k in range(x.shape[1] // LANES)]


def _block_diag(x, w_ref, transposed=False):
    mm = _mm_nt if transposed else _mm
    return jnp.concatenate([mm(xk, w_ref[k]) for k, xk in enumerate(_lane_blocks(x))], axis=1)


def _lru_decay(gr, sp):
    log_a = (-LRU_C) * gr * sp
    a = jnp.exp(log_a)
    y = 2.0 * log_a
    mult_sq = jnp.where(y > -1e-3, -y * (1.0 + 0.5 * y), 1.0 - a * a)
    inv_mult = lax.rsqrt(jnp.maximum(mult_sq, 1e-37))
    return a, mult_sq * inv_mult, inv_mult


def _tile_rows(width):
    return lax.broadcasted_iota(jnp.int32, (SUBLANES, width), 0)


def _scan_forward(a_scr, u_scr, h_scr, seq):
    width = a_scr.shape[1]
    rows = _tile_rows(width)

    group = min(SCAN_UNROLL, seq // SUBLANES)

    def within_tile(j):
        sl = pl.ds(pl.multiple_of(j * SUBLANES, SUBLANES), SUBLANES)
        a = a_scr[sl, :]
        u = u_scr[sl, :]
        for d in (1, 2, 4):
            keep = rows >= d
            a_sh = jnp.where(keep, pltpu.roll(a, d, 0), 1.0)
            u_sh = jnp.where(keep, pltpu.roll(u, d, 0), 0.0)
            u = a * u_sh + u
            a = a * a_sh
        return sl, a, u

    def tiles(t, carry):
        parts = [within_tile(t * group + k) for k in range(group)]
        out = []
        for sl, a, u in parts:
            h = u + a * carry
            out.append((sl, h))
            carry = jnp.broadcast_to(h[SUBLANES - 1:SUBLANES, :], (SUBLANES, width))
        for sl, h in out:
            h_scr[sl, :] = h
        return carry

    lax.fori_loop(0, seq // SUBLANES // group, tiles, jnp.zeros((SUBLANES, width), F32))


def _scan_backward(c_scr, d_scr, g_scr, seq):
    width = c_scr.shape[1]
    rows = _tile_rows(width)
    n_tiles = seq // SUBLANES

    group = min(SCAN_UNROLL, n_tiles)

    def within_tile(j):
        sl = pl.ds(pl.multiple_of(j * SUBLANES, SUBLANES), SUBLANES)
        c = c_scr[sl, :]
        g = d_scr[sl, :]
        for d in (1, 2, 4):
            keep = rows < SUBLANES - d
            c_sh = jnp.where(keep, pltpu.roll(c, SUBLANES - d, 0), 1.0)
            g_sh = jnp.where(keep, pltpu.roll(g, SUBLANES - d, 0), 0.0)
            g = c * g_sh + g
            c = c * c_sh
        return sl, c, g

    def tiles(t, carry):
        parts = [within_tile(n_tiles - 1 - (t * group + k)) for k in range(group)]
        out = []
        for sl, c, g in parts:
            g = g + c * carry
            out.append((sl, g))
            carry = jnp.broadcast_to(g[0:1, :], (SUBLANES, width))
        for sl, g in out:
            g_scr[sl, :] = g
        return carry

    lax.fori_loop(0, n_tiles // group, tiles, jnp.zeros((SUBLANES, width), F32))


LRU_BLOCKS_PER_STEP = 2
LRU_LANES = LRU_BLOCKS_PER_STEP * LANES
LRU_STEPS = N_BLK // LRU_BLOCKS_PER_STEP


def _branch_a_fwd(z, conv_w, conv_b, wx, bx, wa, ba, lam, batch, seq):
    tokens = batch * seq

    def body(z_ref, cw_ref, cb_ref, wx_ref, bx_ref, wa_ref, ba_ref, lam_ref, ya_ref, hl_ref, kept_ref, a_scr, u_scr):
        xa = z_ref[0]
        ga = z_ref[1]
        xc = (cb_ref[...] + cw_ref[3:4, :] * xa + cw_ref[2:3, :] * _shift_down(xa, 1)
              + cw_ref[1:2, :] * _shift_down(xa, 2) + cw_ref[0:1, :] * _shift_down(xa, 3))
        gi = _sigmoid(_block_diag(xc, wx_ref) + bx_ref[...])
        gr = _sigmoid(_block_diag(xc, wa_ref) + ba_ref[...])
        a, mult, _ = _lru_decay(gr, _softplus(-lam_ref[...]))
        kept_ref[0], kept_ref[1], kept_ref[2] = xc, gi, gr
        a_scr[...] = a
        u_scr[...] = mult * gi * xc
        _scan_forward(a_scr, u_scr, hl_ref, seq)
        ya_ref[...] = (hl_ref[...] * (ga * _sigmoid(ga))).astype(_MXU_DTYPE)

    blk = pl.BlockSpec((seq, LRU_LANES), lambda b, c: (b, c))
    vec = pl.BlockSpec((1, LRU_LANES), lambda b, c: (0, c))
    mat = pl.BlockSpec((LRU_BLOCKS_PER_STEP, LANES, LANES), lambda b, c: (c, 0, 0))
    return pl.pallas_call(
        body, name="branch_a_fwd",
        grid=(batch, LRU_STEPS),
        in_specs=[pl.BlockSpec((2, seq, LRU_LANES), lambda b, c: (2, b, c)),
                  pl.BlockSpec((CONV_WIDTH, LRU_LANES), lambda b, c: (0, c)), vec, mat, vec, mat, vec, vec],
        out_specs=[blk, blk, pl.BlockSpec((3, seq, LRU_LANES), lambda b, c: (0, b, c))],
        out_shape=[jax.ShapeDtypeStruct((tokens, D_MODEL), _MXU_DTYPE), jax.ShapeDtypeStruct((tokens, D_MODEL), F32),
                   jax.ShapeDtypeStruct((3, tokens, D_MODEL), F32)],
        scratch_shapes=[pltpu.VMEM((seq, LRU_LANES), F32), pltpu.VMEM((seq, LRU_LANES), F32)],
        compiler_params=_params(("parallel", "parallel")),
    )(z, conv_w, conv_b, wx, bx, wa, ba, lam)


def _branch_a_bwd(z, hl, kept, dya, dz, conv_w, wx, wa, lam, batch, seq):
    def body(z_ref, hl_ref, kept_ref, dya_ref, dz_in_ref, cw_ref, wx_ref, wa_ref, lam_ref,
             dz_ref, dcw_ref, dcb_ref, dwx_ref, dbx_ref, dwa_ref, dba_ref, dlam_ref, c_scr, d_scr):
        del dz_in_ref
        g_scr = d_scr
        xa = z_ref[0]
        ga = z_ref[1]
        hl = hl_ref[...]
        dya = dya_ref[...]
        xc, gi, gr = kept_ref[0], kept_ref[1], kept_ref[2]
        sp = _softplus(-lam_ref[...])
        a, mult, inv_mult = _lru_decay(gr, sp)
        sga = _sigmoid(ga)
        dz_ref[1] = (dya * hl * (sga * (1.0 + ga * (1.0 - sga)))).astype(_MXU_DTYPE)
        c_scr[...] = _shift_up(a, 1)
        d_scr[...] = dya * (ga * sga)
        _scan_backward(c_scr, d_scr, g_scr, seq)
        g = g_scr[...]
        da = g * _shift_down(hl, 1)
        dmult = g * gi * xc
        dgi = g * mult * xc
        dxc = g * mult * gi
        dlog_a = da * a - dmult * (a * a) * inv_mult
        dgr = dlog_a * (-LRU_C) * sp
        dsp = jnp.sum(dlog_a * gr, axis=0, keepdims=True) * (-LRU_C)
        dlam = -dsp * _sigmoid(-lam_ref[...])
        dpi = dgi * gi * (1.0 - gi)
        dpr = dgr * gr * (1.0 - gr)
        dxc = dxc + _block_diag(dpi, wx_ref, transposed=True) + _block_diag(dpr, wa_ref, transposed=True)
        dwx = jnp.stack([_mm_tn(xk, dk) for xk, dk in zip(_lane_blocks(xc), _lane_blocks(dpi))])
        dwa = jnp.stack([_mm_tn(xk, dk) for xk, dk in zip(_lane_blocks(xc), _lane_blocks(dpr))])
        dbx = jnp.sum(dpi, axis=0, keepdims=True)
        dba = jnp.sum(dpr, axis=0, keepdims=True)
        ahead = [dxc if k == CONV_WIDTH - 1 else _shift_up(dxc, CONV_WIDTH - 1 - k) for k in range(CONV_WIDTH)]
        dxa = sum(cw_ref[k:k + 1, :] * ahead[k] for k in range(CONV_WIDTH))
        dz_ref[0] = dxa.astype(_MXU_DTYPE)
        dcb = jnp.sum(dxc, axis=0, keepdims=True)
        dcw = [jnp.sum(ahead[k] * xa, axis=0, keepdims=True) for k in range(CONV_WIDTH)]

        @pl.when(pl.program_id(1) == 0)
        def _():
            for k in range(CONV_WIDTH):
                dcw_ref[k:k + 1, :] = dcw[k]
            dcb_ref[...] = dcb
            dwx_ref[...] = dwx
            dbx_ref[...] = dbx
            dwa_ref[...] = dwa
            dba_ref[...] = dba
            dlam_ref[...] = dlam

        @pl.when(pl.program_id(1) != 0)
        def _():
            for k in range(CONV_WIDTH):
                dcw_ref[k:k + 1, :] += dcw[k]
            dcb_ref[...] += dcb
            dwx_ref[...] += dwx
            dbx_ref[...] += dbx
            dwa_ref[...] += dwa
            dba_ref[...] += dba
            dlam_ref[...] += dlam

    tokens = batch * seq
    blk = pl.BlockSpec((seq, LRU_LANES), lambda c, b: (b, c))
    vec = pl.BlockSpec((1, LRU_LANES), lambda c, b: (0, c))
    mat = pl.BlockSpec((LRU_BLOCKS_PER_STEP, LANES, LANES), lambda c, b: (c, 0, 0))
    vec_shape = jax.ShapeDtypeStruct((1, D_MODEL), F32)
    mat_shape = jax.ShapeDtypeStruct((N_BLK, LANES, LANES), F32)
    return pl.pallas_call(
        body, name="branch_a_bwd",
        grid=(LRU_STEPS, batch),
        in_specs=[pl.BlockSpec((2, seq, LRU_LANES), lambda c, b: (2, b, c)), blk,
                  pl.BlockSpec((3, seq, LRU_LANES), lambda c, b: (0, b, c)), blk,
                  pl.BlockSpec(memory_space=pl.ANY),
                  pl.BlockSpec((CONV_WIDTH, LRU_LANES), lambda c, b: (0, c)), mat, mat, vec],
        out_specs=[pl.BlockSpec((2, seq, LRU_LANES), lambda c, b: (2, b, c)),
                   pl.BlockSpec((CONV_WIDTH, LRU_LANES), lambda c, b: (0, c)), vec, mat, vec, mat, vec, vec],
        out_shape=[jax.ShapeDtypeStruct((N_GROUPS, tokens, D_MODEL), _MXU_DTYPE),
                   jax.ShapeDtypeStruct((CONV_WIDTH, D_MODEL), F32), vec_shape, mat_shape, vec_shape, mat_shape,
                   vec_shape, vec_shape],
        scratch_shapes=[pltpu.VMEM((seq, LRU_LANES), F32)] * 2,
        input_output_aliases={4: 0},
        compiler_params=_params(("parallel", "arbitrary"), vmem=VMEM_LIMIT_BIG),
    )(z, hl, kept, dya, dz, conv_w, wx, wa, lam)


def _chunk_masks(transposed=False):
    r = lax.broadcasted_iota(jnp.int32, (CHUNK, CHUNK), 0)
    c = lax.broadcasted_iota(jnp.int32, (CHUNK, CHUNK), 1)
    return r <= c if transposed else r >= c


def _row_blocks(seq, fn):
    block = min(256, seq)

    def trip(i, carry):
        fn(pl.ds(pl.multiple_of(i * block, block), block))
        return carry

    lax.fori_loop(0, seq // block, trip, 0)


def _hgrn_prepare(z_ref, lb_ref, f_scr, logf_scr, qh_scr, seq):
    lb = _sigmoid(lb_ref[0:1, :] - lb_ref[1:2, :])

    def block(rows):
        q = z_ref[0, rows, :]
        f = lb + (1.0 - lb) * _sigmoid(z_ref[1, rows, :])
        f_scr[rows, :] = f
        logf_scr[rows, :] = jnp.log(f)
        qh_scr[rows, :] = q * _sigmoid(q)

    _row_blocks(seq, block)
    return lb


def _cumsum_rows(x, reverse=False):
    shift = _shift_up if reverse else _shift_down
    d = 1
    while d < x.shape[0]:
        x = x + shift(x, d)
        d *= 2
    return x


def _lane_mean(x):
    return jnp.mean(x, axis=-1, keepdims=True)


def _token_contractions(lhs_scr, rhs_scr, out_ref, seq):
    rows_id = lax.broadcasted_iota(jnp.int32, (LANES, LANES), 0)

    def transposed(p):
        rows = pl.ds(pl.multiple_of(p * LANES, LANES), LANES)
        return jnp.transpose(lhs_scr[rows, :]).astype(_MXU_DTYPE), rhs_scr[rows, :]

    def contract(p, s):
        lhs_t, rhs = s
        return (_mm(lhs_t, jnp.where(rows_id < CHUNK, rhs, 0.0)), _mm(lhs_t, jnp.where(rows_id >= CHUNK, rhs, 0.0)))

    def store(p, out):
        out_ref[2 * p] = out[0]
        out_ref[2 * p + 1] = out[1]

    _independent_trips(seq // LANES, [transposed, contract], store)


def _chunk_rows(c):
    return pl.ds(pl.multiple_of(c * CHUNK, CHUNK), CHUNK)


def _chunk_terms(c, z_ref, f_scr, qh_scr, b_scr):
    rows = _chunk_rows(c)
    b = b_scr[rows, :]
    b_mid = b_scr[pl.ds(c * CHUNK + CHUNK // 2, 1), :]
    b_last = b_scr[pl.ds(c * CHUNK + CHUNK - 1, 1), :]
    qh = qh_scr[rows, :]
    k = 1.0 - f_scr[rows, :]
    v = z_ref[2, rows, :]
    e_q = jnp.exp(b - b_mid) * HG_SCALE
    e_k = jnp.exp(b_mid - b)
    e_qi = jnp.exp(b) * HG_SCALE
    e_ks = jnp.exp(b_last - b)
    decay = jnp.exp(b_last)
    return rows, qh, k, v, e_q, e_k, e_qi, e_ks, decay


def _independent_trips(n, stages, store, group=CHUNKS_IN_FLIGHT):
    stages = stages if isinstance(stages, (list, tuple)) else [stages]
    group = min(group, n)

    def trip(g, carry):
        ids = [g * group + i for i in range(group)]
        state = [stages[0](c) for c in ids]
        for stage in stages[1:]:
            state = [stage(c, s) for c, s in zip(ids, state)]
        for c, s in zip(ids, state):
            store(c, s)
        return carry

    lax.fori_loop(0, n // group, trip, 0)


def _branch_b_fwd(z, lb_logits, hg_g, batch, seq):
    tokens = batch * seq
    n_chunks = seq // CHUNK

    def body(z_ref, lb_ref, g_ref, yb_ref, st_ref, kept_ref, logf_scr, o_scr, qi_scr, ks_scr, dec_scr):
        f_scr, qh_scr, b_scr, o_kept = (kept_ref.at[k] for k in range(4))
        _hgrn_prepare(z_ref, lb_ref, f_scr, logf_scr, qh_scr, seq)
        causal = _chunk_masks()
        gain = g_ref[...]

        def cumulate(c):
            return _cumsum_rows(logf_scr[_chunk_rows(c), :])

        def store_cumulated(c, b):
            b_scr[_chunk_rows(c), :] = b

        def scores(c):
            _, qh, k, v, e_q, e_k, e_qi, e_ks, decay = _chunk_terms(c, z_ref, f_scr, qh_scr, b_scr)
            return _mm_nt(qh * e_q, k * e_k), v, qh * e_qi, k * e_ks, decay

        def within_chunk(c, s):
            att, v, q_int, k_st, decay = s
            return _mm(jnp.where(causal, att, 0.0), v), q_int, k_st, decay

        def store_within_chunk(c, out):
            rows = _chunk_rows(c)
            o_scr[rows, :], qi_scr[rows, :], ks_scr[rows, :], dec_scr[pl.ds(c, 1), :] = out

        def carry_state(c, state_t):
            update = st_ref[c]
            st_ref[c] = state_t
            return state_t * dec_scr[pl.ds(c, 1), :] + update

        def finish(c):
            rows = _chunk_rows(c)
            o = o_scr[rows, :] + _mm_nt(qi_scr[rows, :], st_ref[c])
            r = lax.rsqrt(_lane_mean(o * o) + EPS)
            gb = z_ref[3, rows, :]
            return (((o * r) * gain) * (gb * _sigmoid(gb))).astype(_MXU_DTYPE), o

        def store_finished(c, out):
            yb_ref[_chunk_rows(c), :], o_kept[_chunk_rows(c), :] = out

        _independent_trips(n_chunks, cumulate, store_cumulated, FWD_CHUNKS_IN_FLIGHT)
        _independent_trips(n_chunks, [scores, within_chunk], store_within_chunk, FWD_CHUNKS_IN_FLIGHT)
        _token_contractions(z_ref.at[2], ks_scr, st_ref, seq)
        lax.fori_loop(0, n_chunks, carry_state, jnp.zeros((LANES, LANES), F32))
        _independent_trips(n_chunks, finish, store_finished, FWD_CHUNKS_IN_FLIGHT)

    seq_buf = pltpu.VMEM((seq, LANES), F32)
    return pl.pallas_call(
        body, name="branch_b_fwd",
        grid=(batch, N_BLK),
        in_specs=[pl.BlockSpec((4, seq, LANES), lambda b, h: (0, b, h)),
                  pl.BlockSpec((2, LANES), lambda b, h: (0, h)),
                  pl.BlockSpec((1, LANES), lambda b, h: (0, 0))],
        out_specs=[pl.BlockSpec((seq, LANES), lambda b, h: (b, h)),
                   pl.BlockSpec((None, n_chunks, LANES, LANES), lambda b, h: (b * N_BLK + h, 0, 0, 0)),
                   pl.BlockSpec((4, seq, LANES), lambda b, h: (0, b, h))],
        out_shape=[jax.ShapeDtypeStruct((tokens, D_MODEL), _MXU_DTYPE),
                   jax.ShapeDtypeStruct((batch * N_BLK, n_chunks, LANES, LANES), F32),
                   jax.ShapeDtypeStruct((4, tokens, D_MODEL), F32)],
        scratch_shapes=[seq_buf] * 4 + [pltpu.VMEM((n_chunks, LANES), F32)],
        compiler_params=_params(("parallel", "parallel")),
    )(z, lb_logits, hg_g)


def _branch_b_bwd(z, states, kept, dyb, dz, lb_logits, hg_g, batch, seq):
    n_chunks = seq // CHUNK

    def body(z_ref, st_ref, kept_ref, dyb_ref, dz_in_ref, lb_ref, g_ref, dz_ref, dlog_ref, dg_ref,
             do_scr, qi_scr, dqh_scr, df_scr, dec_scr, dgp_scr, dlb_scr, dst_scr):
        del dz_in_ref
        f_scr, qh_scr, b_scr, o_kept = (kept_ref.at[k] for k in range(4))
        first = (pl.program_id(0) == 0) & (pl.program_id(1) == 0)
        lb = _sigmoid(lb_ref[0:1, :] - lb_ref[1:2, :])
        causal = _chunk_masks()
        anti_causal = _chunk_masks(transposed=True)
        gain = g_ref[...]

        @pl.when(first)
        def _():
            dg_ref[...] = jnp.zeros_like(dg_ref)

        @pl.when(pl.program_id(1) == 0)
        def _():
            dlb_scr[...] = jnp.zeros_like(dlb_scr)

        def output_gradient(c):
            rows = _chunk_rows(c)
            b = b_scr[rows, :]
            q_int = qh_scr[rows, :] * (jnp.exp(b) * HG_SCALE)
            decay = jnp.exp(b_scr[pl.ds(c * CHUNK + CHUNK - 1, 1), :])
            o = o_kept[rows, :]
            r = lax.rsqrt(_lane_mean(o * o) + EPS)
            o_n = o * r
            gb = z_ref[3, rows, :]
            sgb = _sigmoid(gb)
            dyb_c = dyb_ref[rows, :]
            d_ong = dyb_c * (gb * sgb)
            d_gb = (dyb_c * (o_n * gain) * (sgb * (1.0 + gb * (1.0 - sgb)))).astype(_MXU_DTYPE)
            d_gain = jnp.sum(d_ong * o_n, axis=0, keepdims=True)
            d_on = d_ong * gain
            return d_gb, d_gain, r * (d_on - o_n * _lane_mean(d_on * o_n)), q_int, decay

        def store_output_gradient(c, out):
            rows = _chunk_rows(c)
            dz_ref[3, rows, :], dgp_scr[pl.ds(c, 1), :], do_scr[rows, :], qi_scr[rows, :], dec_scr[pl.ds(c, 1), :] = out

        def carry_state_gradient(cc, d_state_t):
            c = n_chunks - 1 - cc
            update = dst_scr[c]
            dst_scr[c] = d_state_t
            return d_state_t * dec_scr[pl.ds(c, 1), :] + update

        def score_gradients(c):
            rows, qh, k, v, e_q, e_k, e_qi, e_ks, decay = _chunk_terms(c, z_ref, f_scr, qh_scr, b_scr)
            state_t = st_ref[c]
            d_state_t = dst_scr[c]
            d_o = do_scr[rows, :]
            q_in, k_in, q_int, k_st = qh * e_q, k * e_k, qh * e_qi, k * e_ks
            first = (_mm_nt(k_in, q_in), _mm_nt(d_o, v), _mm_nt(v, d_o), _mm_nt(k_st, d_state_t), _mm(d_o, state_t),
                     _mm(v, d_state_t))
            d_decay = jnp.sum(state_t * d_state_t, axis=0, keepdims=True)
            return first, d_o, q_in, k_in, q_int, k_st, e_q, e_k, e_qi, e_ks, decay, d_decay

        def input_gradients(c, s):
            (att_t, d_att, d_att_t, dv_inter, dq_int, dk_st), d_o, q_in, k_in, q_int, k_st, e_q, e_k, e_qi, e_ks, decay, d_decay = s
            rows = _chunk_rows(c)
            d_v = _mm(jnp.where(anti_causal, att_t, 0.0), d_o) + dv_inter
            dq_in = _mm(jnp.where(causal, d_att, 0.0), k_in)
            dk_in = _mm(jnp.where(anti_causal, d_att_t, 0.0), q_in)
            d_k = dk_in * e_k + dk_st * e_ks
            kk = dk_st * k_st
            d_b = dq_in * q_in + dq_int * q_int - dk_in * k_in - kk
            d_b_last = jnp.sum(kk, axis=0, keepdims=True) + decay * d_decay
            d_logf = _cumsum_rows(d_b, reverse=True) + d_b_last
            return d_v.astype(_MXU_DTYPE), dq_in * e_q + dq_int * e_qi, d_logf / f_scr[rows, :] - d_k

        def store_input_gradients(c, out):
            rows = _chunk_rows(c)
            dz_ref[2, rows, :], dqh_scr[rows, :], df_scr[rows, :] = out

        def input_activations(rows):
            q = z_ref[0, rows, :]
            sq = _sigmoid(q)
            dz_ref[0, rows, :] = (dqh_scr[rows, :] * (sq * (1.0 + q * (1.0 - sq)))).astype(_MXU_DTYPE)
            sg = _sigmoid(z_ref[1, rows, :])
            d_f = df_scr[rows, :]
            dz_ref[1, rows, :] = (d_f * (1.0 - lb) * sg * (1.0 - sg)).astype(_MXU_DTYPE)
            dlb_scr[...] += jnp.sum(d_f * (1.0 - sg), axis=0, keepdims=True)

        _independent_trips(n_chunks, output_gradient, store_output_gradient)
        _token_contractions(do_scr, qi_scr, dst_scr, seq)
        lax.fori_loop(0, n_chunks, carry_state_gradient, jnp.zeros((LANES, LANES), F32))
        _independent_trips(n_chunks, [score_gradients, input_gradients], store_input_gradients)
        dg_ref[...] += jnp.sum(dgp_scr[...], axis=0, keepdims=True)
        _row_blocks(seq, input_activations)
        d_l0 = dlb_scr[...] * lb * (1.0 - lb)
        dlog_ref[0:1, :] = d_l0
        dlog_ref[1:2, :] = -d_l0

    tokens = batch * seq
    seq_buf = pltpu.VMEM((seq, LANES), F32)
    chunk_rows = pltpu.VMEM((n_chunks, LANES), F32)
    return pl.pallas_call(
        body, name="branch_b_bwd",
        grid=(N_BLK, batch),
        in_specs=[pl.BlockSpec((4, seq, LANES), lambda h, b: (0, b, h)),
                  pl.BlockSpec((None, n_chunks, LANES, LANES), lambda h, b: (b * N_BLK + h, 0, 0, 0)),
                  pl.BlockSpec((4, seq, LANES), lambda h, b: (0, b, h)),
                  pl.BlockSpec((seq, LANES), lambda h, b: (b, h)),
                  pl.BlockSpec(memory_space=pl.ANY),
                  pl.BlockSpec((2, LANES), lambda h, b: (0, h)),
                  pl.BlockSpec((1, LANES), lambda h, b: (0, 0))],
        out_specs=[pl.BlockSpec((4, seq, LANES), lambda h, b: (0, b, h)),
                   pl.BlockSpec((2, LANES), lambda h, b: (0, h)),
                   pl.BlockSpec((1, LANES), lambda h, b: (0, 0))],
        out_shape=[jax.ShapeDtypeStruct((N_GROUPS, tokens, D_MODEL), _MXU_DTYPE),
                   jax.ShapeDtypeStruct((2, D_MODEL), F32),
                   jax.ShapeDtypeStruct((1, LANES), F32)],
        scratch_shapes=[seq_buf] * 4 + [chunk_rows, chunk_rows, pltpu.VMEM((1, LANES), F32),
                                        pltpu.VMEM((n_chunks, LANES, LANES), F32)],
        input_output_aliases={4: 0},
        compiler_params=_params(("arbitrary", "arbitrary")),
    )(z, states, kept, dyb, dz, lb_logits, hg_g)


def _merge_tail(ya, yb, z, x2d, tgt2d, b_merge, final_g, pa, pb, wo):
    tokens, d = x2d.shape
    tm = min(TAIL_TOKENS, tokens)
    n_tiles = tokens // tm

    def body(ya_ref, yb_ref, z_ref, x_ref, t_ref, bm_ref, fg_ref, pa_hbm, pb_hbm, wo_hbm,
             dya_ref, dyb_ref, dx2_ref, dz_ref, loss_ref, dfg_ref, dbm_ref, dpa_hbm, dpb_hbm, dwo_hbm,
             pa_s, pb_s, wo_s, dpa_s, dpb_s, dwo_s, sems):
        i = pl.program_id(0)

        def together(pairs):
            copies = [pltpu.make_async_copy(src, dst, sems.at[k]) for k, (src, dst) in enumerate(pairs)]
            for cp in copies:
                cp.start()
            for cp in copies:
                cp.wait()

        @pl.when(i == 0)
        def _():
            together([(pa_hbm, pa_s), (pb_hbm, pb_s), (wo_hbm, wo_s)])
            dpa_s[...] = jnp.zeros_like(dpa_s)
            dpb_s[...] = jnp.zeros_like(dpb_s)
            dwo_s[...] = jnp.zeros_like(dwo_s)
            loss_ref[...] = jnp.zeros_like(loss_ref)
            dfg_ref[...] = jnp.zeros_like(dfg_ref)
            dbm_ref[...] = jnp.zeros_like(dbm_ref)

        ya_t = ya_ref[...]
        yb_t = yb_ref[...]
        out_a = _mm(ya_t, pa_s[...])
        out_b = _mm(yb_t, pb_s[...])
        g_a = _sigmoid(z_ref[0] + bm_ref[:, :d])
        g_b = _sigmoid(z_ref[1] + bm_ref[:, d:])
        mixed = g_a * out_a + g_b * out_b
        x2 = x_ref[...] + _mm(mixed, wo_s[...])
        r = lax.rsqrt(jnp.mean(x2 * x2, axis=-1, keepdims=True) + EPS)
        xn = x2 * r
        fg = fg_ref[...]
        diff = xn * fg - t_ref[...]
        loss_ref[...] += jnp.sum(diff * diff) * (0.5 / d)
        dy = diff * (1.0 / d)
        dfg_ref[...] += jnp.sum(dy * xn, axis=0, keepdims=True)
        dxn = dy * fg
        dx2 = r * (dxn - xn * jnp.mean(dxn * xn, axis=-1, keepdims=True))
        dx2_ref[...] = dx2
        dmixed = _mm_nt(dx2, wo_s[...])
        dwo_s[...] += _mm_tn(mixed, dx2)
        dgm_a = dmixed * out_a * g_a * (1.0 - g_a)
        dgm_b = dmixed * out_b * g_b * (1.0 - g_b)
        dz_ref[0] = dgm_a.astype(_MXU_DTYPE)
        dz_ref[1] = dgm_b.astype(_MXU_DTYPE)
        dbm_ref[:, :d] += jnp.sum(dgm_a, axis=0, keepdims=True)
        dbm_ref[:, d:] += jnp.sum(dgm_b, axis=0, keepdims=True)
        dout_a = dmixed * g_a
        dout_b = dmixed * g_b
        dpa_s[...] += _mm_tn(ya_t, dout_a)
        dpb_s[...] += _mm_tn(yb_t, dout_b)
        dya_ref[...] = _mm_nt(dout_a, pa_s[...])
        dyb_ref[...] = _mm_nt(dout_b, pb_s[...])

        @pl.when(i == n_tiles - 1)
        def _():
            together([(dpa_s, dpa_hbm), (dpb_s, dpb_hbm), (dwo_s, dwo_hbm)])

    tile = pl.BlockSpec((tm, d), lambda i: (i, 0))
    gm = pl.BlockSpec((2, tm, d), lambda i: (3, i, 0))
    row = lambda n: pl.BlockSpec((1, n), lambda i: (0, 0))
    hbm = pl.BlockSpec(memory_space=pl.ANY)
    act = jax.ShapeDtypeStruct((tokens, d), F32)
    mat = jax.ShapeDtypeStruct((d, d), F32)
    return pl.pallas_call(
        body, name="merge_tail",
        grid=(n_tiles,),
        in_specs=[tile, tile, gm, tile, tile, row(2 * d), row(d), hbm, hbm, hbm],
        out_specs=[tile, tile, tile, gm, row(LANES), row(d), row(2 * d), hbm, hbm, hbm],
        out_shape=[act, act, act, jax.ShapeDtypeStruct((N_GROUPS, tokens, d), _MXU_DTYPE),
                   jax.ShapeDtypeStruct((1, LANES), F32), jax.ShapeDtypeStruct((1, d), F32),
                   jax.ShapeDtypeStruct((1, 2 * d), F32), mat, mat, mat],
        scratch_shapes=[pltpu.VMEM((d, d), _MXU_DTYPE)] * 3 + [pltpu.VMEM((d, d), F32)] * 3
        + [pltpu.SemaphoreType.DMA((3,))],
        compiler_params=_params(("arbitrary",)),
    )(ya, yb, z, x2d, tgt2d, b_merge, final_g, pa, pb, wo)


def _inproj_dw_exchange(h_t, dz, scatter):
    d, tokens = h_t.shape
    tm = min(CONTRACT_TOKENS, tokens)
    n_i = tokens // tm
    half = d // 2

    def body(h_ref, dz_ref, *rest):
        n_in, n_out = scatter.n_in, scatter.n_out
        dw_hbm, land_hbm = rest[n_in:n_in + 2]
        acc, local_sems, send_sems, recv_sems = rest[n_in + 2 + n_out:n_in + 6 + n_out]
        carried = scatter.copies(rest[:n_in], rest[n_in + 2:n_in + 2 + n_out], rest[n_in + 6 + n_out:])
        s, i = pl.program_id(0), pl.program_id(1)
        x, y, c, _ = _mesh_position()

        @pl.when((s == 0) & (i == 0))
        def _():
            for cp in carried:
                cp.start()

        part = _mm(h_ref[...], dz_ref[...])
        buf = acc.at[s % 2]

        @pl.when(i == 0)
        def _():
            buf[...] = part

        @pl.when(i != 0)
        def _():
            buf[...] += part

        def copies(k):
            g = _SLOT_TO_GROUP[k]
            cols = pl.ds((g % 2) * D_MODEL, D_MODEL)
            src = acc.at[k % 2]
            mine = pltpu.make_async_copy(src, dw_hbm.at[g // 2, :, cols], local_sems.at[k % 2])
            theirs = _remote(src.at[pl.ds((1 - c) * half, half), :], land_hbm.at[g // 2, :, cols],
                             send_sems, recv_sems, k, (x, y, 1 - c))
            return mine, theirs

        for k in range(N_GROUPS):
            @pl.when((s == k) & (i == n_i - 1))
            def _(k=k):
                if k > 0:
                    mine, theirs = copies(k - 1)
                    mine.wait()
                    theirs.wait_send()
                mine, theirs = copies(k)
                mine.start()
                theirs.start()
                if k == N_GROUPS - 1:
                    mine.wait()
                    theirs.wait_send()
                    for kk in range(N_GROUPS):
                        copies(kk)[1].wait_recv()
                    for cp in carried:
                        cp.wait()

    hbm = pl.BlockSpec(memory_space=pl.ANY)
    more = scatter.plumbing(first_operand=2, first_output=2)
    return pl.pallas_call(
        body, name="inproj_dw_exchange",
        grid=(N_GROUPS, n_i),
        in_specs=[pl.BlockSpec((d, tm), lambda s, i: (0, i)),
                  pl.BlockSpec((None, tm, D_MODEL), lambda s, i: (s, i, 0))] + more[1],
        out_specs=[hbm, hbm] + more[2],
        out_shape=[jax.ShapeDtypeStruct((N_SHARDS, d, 2 * D_MODEL), F32),
                   jax.ShapeDtypeStruct((N_SHARDS, half, 2 * D_MODEL), F32)] + more[3],
        scratch_shapes=[pltpu.VMEM((2, d, D_MODEL), F32), pltpu.SemaphoreType.DMA((2,)),
                        pltpu.SemaphoreType.DMA((N_GROUPS,)), pltpu.SemaphoreType.DMA((N_GROUPS,))] + more[4],
        input_output_aliases=more[5],
        compiler_params=_params(("arbitrary", "arbitrary")),
    )(h_t, dz, *more[0])


def _inproj_dx(dz, w_all, x2d, dx2, norm_g, scatter):
    tokens, d = x2d.shape
    tm = min(TAIL_TOKENS, tokens)
    n_tiles = tokens // tm

    def body(dz_ref, w_hbm, x_ref, dx2_ref, g_ref, *rest):
        n_in, n_out = scatter.n_in, scatter.n_out
        dx_ref, dg_ref = rest[n_in:n_in + 2]
        w_res, load_sems = rest[n_in + 2 + n_out:n_in + 4 + n_out]
        copies = scatter.copies(rest[:n_in], rest[n_in + 2:n_in + 2 + n_out], rest[n_in + 4 + n_out:])
        i = pl.program_id(0)

        @pl.when(i == 0)
        def _():
            for cp in copies:
                cp.start()
            loads = [pltpu.make_async_copy(w_hbm.at[g // 2, :, pl.ds((g % 2) * D_MODEL, D_MODEL)],
                                           w_res.at[:, pl.ds(slot * D_MODEL, D_MODEL)], load_sems.at[slot])
                     for slot, g in enumerate(_SLOT_TO_GROUP)]
            for cp in loads:
                cp.start()
            for cp in loads:
                cp.wait()
            dg_ref[...] = jnp.zeros_like(dg_ref)

        dz_all = jnp.concatenate([dz_ref[s] for s in range(N_GROUPS)], axis=1)
        dh = jnp.transpose(_mm_nt(w_res[...], dz_all))
        x = x_ref[...]
        r = lax.rsqrt(jnp.mean(x * x, axis=-1, keepdims=True) + EPS)
        xn = x * r
        dg_ref[...] += jnp.sum(dh * xn, axis=0, keepdims=True)
        dxn = dh * g_ref[...]
        dx_ref[...] = r * (dxn - xn * jnp.mean(dxn * xn, axis=-1, keepdims=True)) + dx2_ref[...]

        @pl.when(i == n_tiles - 1)
        def _():
            for cp in copies:
                cp.wait()

    tile = pl.BlockSpec((tm, d), lambda i: (i, 0))
    hbm = pl.BlockSpec(memory_space=pl.ANY)
    more = scatter.plumbing(first_operand=5, first_output=2)
    return pl.pallas_call(
        body, name="inproj_dx", grid=(n_tiles,),
        in_specs=[pl.BlockSpec((N_GROUPS, tm, D_MODEL), lambda i: (0, i, 0)), hbm, tile, tile,
                  pl.BlockSpec((1, d), lambda i: (0, 0))] + more[1],
        out_specs=[tile, pl.BlockSpec((1, d), lambda i: (0, 0))] + more[2],
        out_shape=[jax.ShapeDtypeStruct((tokens, d), F32), jax.ShapeDtypeStruct((1, d), F32)] + more[3],
        scratch_shapes=[pltpu.VMEM((d, N_GROUPS * D_MODEL), _MXU_DTYPE), pltpu.SemaphoreType.DMA((N_GROUPS,))] + more[4],
        input_output_aliases=more[5],
        compiler_params=_params(("arbitrary",)),
    )(dz, w_all, x2d, dx2, norm_g, *more[0])


def _row_tile(rows, cols, itemsize=4, budget=2 * 1024 * 1024):
    tr = rows
    while tr * cols * itemsize > budget and tr % 16 == 0:
        tr //= 2
    return tr


def _cast_into_slot(a, chip, dtype, name):
    rows, cols = a.shape
    tr = _row_tile(rows, cols)

    def body(chip_ref, a_ref, o_ref):
        del chip_ref
        o_ref[...] = a_ref[...].astype(dtype)

    grid_spec = pltpu.PrefetchScalarGridSpec(
        num_scalar_prefetch=1, grid=(rows // tr,),
        in_specs=[pl.BlockSpec((tr, cols), lambda i, chip_ref: (i, 0))],
        out_specs=pl.BlockSpec((None, tr, cols), lambda i, chip_ref: (chip_ref[0], i, 0)))
    return pl.pallas_call(body, name=name, grid_spec=grid_spec,
                          out_shape=jax.ShapeDtypeStruct((N_SHARDS, rows, cols), dtype),
                          compiler_params=_params(("arbitrary",)))(chip, a)


def _sum_slots(stack, name):
    n, rows, cols = stack.shape
    tr = _row_tile(rows, cols * n)

    def body(s_ref, o_ref):
        total = s_ref[0].astype(F32)
        for k in range(1, n):
            total = total + s_ref[k].astype(F32)
        o_ref[...] = total

    return pl.pallas_call(body, name=name, grid=(rows // tr,),
                          in_specs=[pl.BlockSpec((n, tr, cols), lambda i: (0, i, 0))],
                          out_specs=pl.BlockSpec((tr, cols), lambda i: (i, 0)),
                          out_shape=jax.ShapeDtypeStruct((rows, cols), F32),
                          compiler_params=_params(("parallel",)))(stack)


def _add_half(full, landed, place, name):
    n, rows, cols = full.shape
    half = rows // 2
    tr = _row_tile(half, cols)
    nb = half // tr

    def body(place_ref, a_ref, b_ref, o_ref, own_ref):
        total = (a_ref[...] + b_ref[...]).astype(_MXU_DTYPE)
        o_ref[...] = total

        @pl.when(pl.program_id(1) == place_ref[1])
        def _():
            own_ref[...] = total

    grid_spec = pltpu.PrefetchScalarGridSpec(
        num_scalar_prefetch=1, grid=(nb, n),
        in_specs=[pl.BlockSpec((None, tr, cols), lambda i, j, place_ref: (j, place_ref[0] * nb + i, 0)),
                  pl.BlockSpec((None, tr, cols), lambda i, j, place_ref: (j, i, 0))],
        out_specs=[pl.BlockSpec((None, tr, cols), lambda i, j, place_ref: (j, i, 0)),
                   pl.BlockSpec((None, tr, cols), lambda i, j, place_ref: (place_ref[1], i, 0))])
    shape = jax.ShapeDtypeStruct((n, half, cols), _MXU_DTYPE)
    return pl.pallas_call(body, name=name, grid_spec=grid_spec, out_shape=[shape, shape],
                          compiler_params=_params(("parallel", "arbitrary")))(place, full, landed)


def _adamw_update(w, grad, m, v):
    c1 = 1.0 - ADAM_B1 ** ADAM_STEP
    c2 = 1.0 - ADAM_B2 ** ADAM_STEP
    nm = ADAM_B1 * m + (1.0 - ADAM_B1) * grad
    nv = ADAM_B2 * v + (1.0 - ADAM_B2) * (grad * grad)
    return (-ADAM_LR) * ((nm / c1) / (jnp.sqrt(nv / c2) + ADAM_EPS) + ADAM_WD * w), nm, nv


def _adamw(w, g, m, v, name):
    rows, cols = w.shape
    tr = _row_tile(rows, cols, budget=1024 * 1024)

    def body(w_ref, g_ref, m_ref, v_ref, d_ref, nm_ref, nv_ref):
        d_ref[...], nm_ref[...], nv_ref[...] = _adamw_update(w_ref[...], g_ref[...], m_ref[...], v_ref[...])

    spec = pl.BlockSpec((tr, cols), lambda i: (i, 0))
    shape = jax.ShapeDtypeStruct((rows, cols), F32)
    return pl.pallas_call(body, name=name, grid=(rows // tr,), in_specs=[spec] * 4, out_specs=[spec] * 3,
                          out_shape=[shape] * 3, compiler_params=_params(("parallel",)))(w, g, m, v)


def _adamw_halves(w, g_mine, g_sibling, m, v, core, name):
    rows, cols = w.shape
    half = rows // 2
    tr = _row_tile(half, cols, budget=1024 * 1024)
    nb = half // tr

    def body(core_ref, w_ref, gm_ref, gs_ref, m_ref, v_ref, g_ref, d_ref, nm_ref, nv_ref):
        mine = pl.program_id(0) // nb == core_ref[0]
        grad = jnp.where(mine, gm_ref[...], gs_ref[...])
        g_ref[...] = grad
        d_ref[...], nm_ref[...], nv_ref[...] = _adamw_update(w_ref[...], grad, m_ref[...], v_ref[...])

    spec = pl.BlockSpec((tr, cols), lambda i, core_ref: (i, 0))
    mine_spec = pl.BlockSpec((tr, cols), lambda i, core_ref: (jnp.where(i // nb == core_ref[0], i % nb, 0), 0))
    sibling_spec = pl.BlockSpec((tr, cols), lambda i, core_ref: (jnp.where(i // nb == core_ref[0], 0, i % nb), 0))
    grid_spec = pltpu.PrefetchScalarGridSpec(num_scalar_prefetch=1, grid=(rows // tr,),
                                             in_specs=[spec, mine_spec, sibling_spec, spec, spec], out_specs=[spec] * 4)
    shape = jax.ShapeDtypeStruct((rows, cols), F32)
    return pl.pallas_call(body, name=name, grid_spec=grid_spec, out_shape=[shape] * 4,
                          compiler_params=_params(("parallel",)))(core, w, g_mine, g_sibling, m, v)


def _local_step(x, loss_target, gather, reduction, b_merge, conv_b, rg_wx, rg_bx, rg_wa, rg_ba, rg_lambda,
                hg_lb_logits, hg_norm_g, norm_g, final_norm_g):
    batch, seq, d = x.shape
    x2d = x.reshape(batch * seq, d)
    tgt2d = loss_target.reshape(batch * seq, d)
    z, h_t, (w_all, pa, pb, wo), cw_all = _inproj_fwd_gather(x2d, norm_g, *gather)
    pa, pb, wo = (t.reshape(d, d) for t in (pa, pb, wo))
    conv_w = jnp.transpose(cw_all, (1, 0, 2)).reshape(CONV_WIDTH, d)
    lru = (conv_w, conv_b, rg_wx, rg_bx, rg_wa, rg_ba, rg_lambda)
    ya, hl, kept = _branch_a_fwd(z, *lru, batch, seq)
    yb, states, kept_b = _branch_b_fwd(z, hg_lb_logits, hg_norm_g, batch, seq)
    dya, dyb, dx2, dz, loss, d_final_g, d_b_merge, d_pa, d_pb, d_wo = _merge_tail(
        ya, yb, z, x2d, tgt2d, b_merge, final_norm_g, pa, pb, wo)
    dz, d_lb_logits, d_hg_g = _branch_b_bwd(z, states, kept_b, dyb, dz, hg_lb_logits, hg_norm_g, batch, seq)
    dz, d_conv_w, d_conv_b, d_wx, d_bx, d_wa, d_ba, d_lam = _branch_a_bwd(
        z, hl, kept, dya, dz, conv_w, rg_wx, rg_wa, rg_lambda, batch, seq)
    small = dict(b_merge=d_b_merge, conv_w=d_conv_w, conv_b=d_conv_b, rg_wx=d_wx, rg_bx=d_bx, rg_wa=d_wa,
                 rg_ba=d_ba, rg_lambda=d_lam, hg_lb_logits=d_lb_logits, hg_norm_g=d_hg_g,
                 norm_g=jnp.zeros((1, d), F32), final_norm_g=d_final_g)
    first, second = reduction
    d_w_in, landed_w_in, *scattered_first = _inproj_dw_exchange(h_t, dz, first((d_pa, d_pb, d_wo), small))
    grad_x, d_norm_g, *scattered_second = _inproj_dx(dz, w_all, x2d, dx2, norm_g, scatter=second(d_w_in, landed_w_in))
    return loss[0, 0], grad_x.reshape(batch, seq, d), d_norm_g, (scattered_first, scattered_second)


_SMALL_ORDER = ("b_merge", "conv_w", "conv_b", "rg_wx", "rg_bx", "rg_wa", "rg_ba", "rg_lambda", "hg_lb_logits",
                "hg_norm_g", "norm_g", "final_norm_g")
N_DEV = 8
PIECE_ROWS = 272


def _pack_small(tree):
    flat = jnp.concatenate([tree[k].reshape(-1) for k in _SMALL_ORDER])
    flat = jnp.pad(flat, (0, N_DEV * PIECE_ROWS * LANES - flat.shape[0]))
    return flat.reshape(N_DEV * PIECE_ROWS, LANES)


def _unpack_small(packed, like):
    flat = packed.reshape(-1)
    out, pos = {}, 0
    for k in _SMALL_ORDER:
        n = like[k].size
        out[k] = flat[pos:pos + n].reshape(like[k].shape)
        pos += n
    return out


def _mesh_position():
    x, y, c = lax.axis_index("x"), lax.axis_index("y"), lax.axis_index("c")
    other_chips = [(1 - x, y), (x, 1 - y), (1 - x, 1 - y)]
    return x, y, c, other_chips


def _other_devices(x, y, c):
    flips = [(fx, fy, fc) for fx in (0, 1) for fy in (0, 1) for fc in (0, 1) if (fx, fy, fc) != (0, 0, 0)]
    return [(jnp.where(fx, 1 - x, x), jnp.where(fy, 1 - y, y), jnp.where(fc, 1 - c, c)) for fx, fy, fc in flips]


def _remote(src, dst, send_sems, recv_sems, k, device):
    return pltpu.make_async_remote_copy(src_ref=src, dst_ref=dst, send_sem=send_sems.at[k], recv_sem=recv_sems.at[k],
                                        device_id=device, device_id_type=MESH)


def _exchange_halves(bigs, small):
    n_big = len(bigs)
    n_sem = n_big + N_DEV - 1

    def body(*refs):
        srcs, small_src = refs[:n_big], refs[n_big]
        outs, small_out = refs[n_big + 1:2 * n_big + 1], refs[2 * n_big + 1]
        send_sems, recv_sems, local_sem = refs[2 * n_big + 2:]
        x, y, c, _ = _mesh_position()
        me, sibling = 4 * x + 2 * y + c, (x, y, 1 - c)
        mine = pltpu.make_async_copy(small_src.at[pl.ds(me * PIECE_ROWS, PIECE_ROWS), :], small_out.at[me], local_sem)
        mine.start()
        copies = []
        for a in range(n_big):
            hs = srcs[a].shape[1] // 2
            copies.append(_remote(srcs[a].at[:, pl.ds((1 - c) * hs, hs), :], outs[a], send_sems, recv_sems, a, sibling))
        for k, (px, py, pc) in enumerate(_other_devices(x, y, c)):
            piece = small_src.at[pl.ds((4 * px + 2 * py + pc) * PIECE_ROWS, PIECE_ROWS), :]
            copies.append(_remote(piece, small_out.at[me], send_sems, recv_sems, n_big + k, (px, py, pc)))
        for cp in copies:
            cp.start()
        for cp in copies:
            cp.wait()
        mine.wait()

    hbm = pl.BlockSpec(memory_space=pl.ANY)
    out_shape = [jax.ShapeDtypeStruct((g.shape[0], g.shape[1] // 2, g.shape[2]), F32) for g in bigs]
    out_shape.append(jax.ShapeDtypeStruct((N_DEV, PIECE_ROWS, LANES), F32))
    return pl.pallas_call(
        body, name="exchange_halves",
        in_specs=[hbm] * (n_big + 1), out_specs=[hbm] * (n_big + 1), out_shape=out_shape,
        scratch_shapes=[pltpu.SemaphoreType.DMA((n_sem,)), pltpu.SemaphoreType.DMA((n_sem,)), pltpu.SemaphoreType.DMA],
    )(*bigs, small)


class _Scatter:
    def __init__(self, bigs, by_chip, small=None):
        self.bigs, self.by_chip, self.small = list(bigs), list(by_chip), small
        self.n_big = len(self.bigs)
        self.n_in = 2 * self.n_big + (small is not None)
        self.n_out = self.n_big + (small is not None)
        self.n_scratch = 2 + (small is not None)

    def plumbing(self, first_operand, first_output):
        hbm = pl.BlockSpec(memory_space=pl.ANY)
        n_sem = 3 * self.n_big + (N_DEV - 1 if self.small is not None else 0)
        operands = self.bigs + self.by_chip + ([self.small] if self.small is not None else [])
        out_shapes = [jax.ShapeDtypeStruct(g.shape, g.dtype) for g in self.by_chip]
        scratch = [pltpu.SemaphoreType.DMA((n_sem,)), pltpu.SemaphoreType.DMA((n_sem,))]
        if self.small is not None:
            out_shapes.append(jax.ShapeDtypeStruct((N_DEV, PIECE_ROWS, LANES), F32))
            scratch.append(pltpu.SemaphoreType.DMA)
        aliases = {first_operand + self.n_big + a: first_output + a for a in range(self.n_big)}
        return operands, [hbm] * self.n_in, [hbm] * self.n_out, out_shapes, scratch, aliases

    def copies(self, in_refs, out_refs, scratch_refs):
        srcs, outs = in_refs[:self.n_big], out_refs[:self.n_big]
        send_sems, recv_sems = scratch_refs[:2]
        x, y, c, chips = _mesh_position()
        chip, me = 2 * x + y, 4 * x + 2 * y + c
        copies = []
        for a in range(self.n_big):
            for j, (cx, cy) in enumerate(chips):
                copies.append(_remote(srcs[a].at[2 * cx + cy], outs[a].at[chip], send_sems, recv_sems, 3 * a + j,
                                      (cx, cy, c)))
        if self.small is not None:
            small_src, small_out = in_refs[2 * self.n_big], out_refs[self.n_big]
            copies.append(pltpu.make_async_copy(small_src, small_out.at[me], scratch_refs[2]))
            for k, peer in enumerate(_other_devices(x, y, c)):
                copies.append(_remote(small_src, small_out.at[me], send_sems, recv_sems, 3 * self.n_big + k, peer))
        return copies


def _swap_halves(halves, vec):
    n_big = len(halves)

    def body(*refs):
        srcs, vec_src = refs[:n_big], refs[n_big]
        outs, vec_out = refs[n_big + 1:2 * n_big + 1], refs[2 * n_big + 1]
        send_sems, recv_sems, local_sem = refs[2 * n_big + 2:]
        x, y, c, _ = _mesh_position()
        me = 4 * x + 2 * y + c
        copies = [pltpu.make_async_copy(vec_src, vec_out.at[me], local_sem)]
        copies += [_remote(srcs[a], outs[a], send_sems, recv_sems, a, (x, y, 1 - c)) for a in range(n_big)]
        copies += [_remote(vec_src, vec_out.at[me], send_sems, recv_sems, n_big + k, peer)
                   for k, peer in enumerate(_other_devices(x, y, c))]
        for cp in copies:
            cp.start()
        for cp in copies:
            cp.wait()

    hbm = pl.BlockSpec(memory_space=pl.ANY)
    n_sem = n_big + N_DEV - 1
    return pl.pallas_call(
        body, name="swap_halves",
        in_specs=[hbm] * (n_big + 1), out_specs=[hbm] * (n_big + 1),
        out_shape=[jax.ShapeDtypeStruct(h.shape, F32) for h in halves] + [jax.ShapeDtypeStruct((N_DEV,) + vec.shape, F32)],
        scratch_shapes=[pltpu.SemaphoreType.DMA((n_sem,)), pltpu.SemaphoreType.DMA((n_sem,)), pltpu.SemaphoreType.DMA],
    )(*halves, vec)


def kernel(x, w_in, b_merge, conv_w, conv_b, rg_wx, rg_bx, rg_wa, rg_ba, rg_lambda, hg_lb_logits, hg_norm_g, proj_a, proj_b, w_out, norm_g, final_norm_g, loss_target, m_w_in, m_b_merge, m_conv_w, m_conv_b, m_rg_wx, m_rg_bx, m_rg_wa, m_rg_ba, m_rg_lambda, m_hg_lb_logits, m_hg_norm_g, m_proj_a, m_proj_b, m_w_out, m_norm_g, m_final_norm_g, v_w_in, v_b_merge, v_conv_w, v_conv_b, v_rg_wx, v_rg_bx, v_rg_wa, v_rg_ba, v_rg_lambda, v_hg_lb_logits, v_hg_norm_g, v_proj_a, v_proj_b, v_w_out, v_norm_g, v_final_norm_g):
    d = D_MODEL
    weights = dict(w_in=w_in, b_merge=b_merge, conv_w=conv_w, conv_b=conv_b, rg_wx=rg_wx, rg_bx=rg_bx, rg_wa=rg_wa,
                   rg_ba=rg_ba, rg_lambda=rg_lambda, hg_lb_logits=hg_lb_logits, hg_norm_g=hg_norm_g, proj_a=proj_a,
                   proj_b=proj_b, w_out=w_out, norm_g=norm_g, final_norm_g=final_norm_g)
    m = dict(w_in=m_w_in, b_merge=m_b_merge, conv_w=m_conv_w, conv_b=m_conv_b, rg_wx=m_rg_wx, rg_bx=m_rg_bx,
             rg_wa=m_rg_wa, rg_ba=m_rg_ba, rg_lambda=m_rg_lambda, hg_lb_logits=m_hg_lb_logits, hg_norm_g=m_hg_norm_g,
             proj_a=m_proj_a, proj_b=m_proj_b, w_out=m_w_out, norm_g=m_norm_g, final_norm_g=m_final_norm_g)
    v = dict(w_in=v_w_in, b_merge=v_b_merge, conv_w=v_conv_w, conv_b=v_conv_b, rg_wx=v_rg_wx, rg_bx=v_rg_bx,
             rg_wa=v_rg_wa, rg_ba=v_rg_ba, rg_lambda=v_rg_lambda, hg_lb_logits=v_hg_lb_logits, hg_norm_g=v_hg_norm_g,
             proj_a=v_proj_a, proj_b=v_proj_b, w_out=v_w_out, norm_g=v_norm_g, final_norm_g=v_final_norm_g)
    big_names = ("w_in", "proj_a", "proj_b", "w_out")

    core = lax.axis_index("c").astype(jnp.int32).reshape(1)
    chip = (2 * lax.axis_index("x") + lax.axis_index("y")).astype(jnp.int32)

    slotted = [_cast_into_slot(weights[k][0], chip.reshape(1), _MXU_DTYPE, f"cast_{k}") for k in big_names]
    conv_slotted = _cast_into_slot(conv_w[0], chip.reshape(1), F32, "slot_conv_w")

    small_shapes = {}

    place = jnp.concatenate([core, chip.reshape(1)])

    def reduce_proj_and_small(proj_grads, small_grads):
        small_shapes.update({k: t.shape for k, t in small_grads.items()})
        bigs = [g.reshape(N_SHARDS, d // N_SHARDS, d) for g in proj_grads]
        *landed, small_landed = _exchange_halves(bigs, _pack_small(small_grads))
        sums = [_add_half(g, l, place, f"add_half_{1 + a}") for a, (g, l) in enumerate(zip(bigs, landed))]
        return _Scatter([s[0] for s in sums], [s[1] for s in sums], _sum_slots(small_landed, "sum_small"))

    def reduce_w_in(d_w_in, landed):
        partial, own_slot = _add_half(d_w_in, landed, place, "add_half_0")
        return _Scatter([partial], [own_slot])

    loss_part, grad_x, d_norm_g, ((*by_chip_proj, small_all), by_chip_w_in) = _local_step(
        x, loss_target, (slotted, conv_slotted, chip.reshape(1)), (reduce_proj_and_small, reduce_w_in),
        b_merge, conv_b, rg_wx[0], rg_bx.reshape(1, d), rg_wa[0], rg_ba.reshape(1, d), rg_lambda, hg_lb_logits,
        hg_norm_g, norm_g, final_norm_g.reshape(1, d))
    mine = [_sum_slots(s, f"sum_chips_{a}") for a, s in enumerate(by_chip_w_in + by_chip_proj)]
    late = jnp.concatenate([d_norm_g.reshape(SUBLANES, LANES), jnp.full((SUBLANES, LANES), loss_part, F32)])
    *theirs, late_parts = _swap_halves(mine, late)
    late_sum = _sum_slots(late_parts, "sum_late")
    loss = late_sum[SUBLANES, 0]
    small_red = _unpack_small(small_all, {k: jax.ShapeDtypeStruct(s, F32) for k, s in small_shapes.items()})
    small_red["norm_g"] = late_sum[:SUBLANES].reshape(1, d)

    grads, delta, new_m, new_v = {}, {}, {}, {}
    for k, g_mine, g_theirs in zip(big_names, mine, theirs):
        out = _adamw_halves(weights[k][0], g_mine, g_theirs, m[k][0], v[k][0], core, f"adamw_{k}")
        grads[k], delta[k], new_m[k], new_v[k] = (t.reshape(weights[k].shape) for t in out)
    cols = d // N_SHARDS
    g_conv = lax.dynamic_slice(small_red["conv_w"], (0, chip * cols), (CONV_WIDTH, cols))
    grads["conv_w"] = g_conv.reshape(conv_w.shape)
    dl, nm, nv = _adamw(conv_w[0], g_conv, m_conv_w[0], v_conv_w[0], "adamw_conv_w")
    delta["conv_w"], new_m["conv_w"], new_v["conv_w"] = (t.reshape(conv_w.shape) for t in (dl, nm, nv))
    rest = [k for k in _SMALL_ORDER if k != "conv_w"]
    like = {k: (weights[k] if k != "conv_w" else jnp.zeros((CONV_WIDTH, d), F32)) for k in _SMALL_ORDER}
    packs = [_pack_small({k: (t[k] if k != "conv_w" else like[k]) for k in _SMALL_ORDER}) for t in (weights, m, v)]
    g_pack = _pack_small({k: small_red[k].reshape(like[k].shape) for k in _SMALL_ORDER})
    outs = [_unpack_small(p, like) for p in _adamw(packs[0], g_pack, packs[1], packs[2], "adamw_small")]
    for k in rest:
        grads[k] = small_red[k].reshape(weights[k].shape)
        delta[k], new_m[k], new_v[k] = outs[0][k], outs[1][k], outs[2][k]

    order = ("w_in", "b_merge", "conv_w", "conv_b", "rg_wx", "rg_bx", "rg_wa", "rg_ba", "rg_lambda", "hg_lb_logits",
             "hg_norm_g", "proj_a", "proj_b", "w_out", "norm_g", "final_norm_g")
    return (loss, grad_x, *[grads[k] for k in order], *[delta[k] for k in order], *[new_m[k] for k in order],
            *[new_v[k] for k in order])
```

```python
import functools

import jax
import jax.numpy as jnp
from jax import lax
from jax.experimental import pallas as pl
from jax.experimental.pallas import tpu as pltpu

F32 = jnp.float32
_MXU_DTYPE = jnp.bfloat16

D_MODEL = 1024
LANES = 128
SUBLANES = 8
N_BLK = D_MODEL // LANES
N_GROUPS = 8
N_SHARDS = 4
CONV_WIDTH = 4
LRU_C = 8.0
CHUNK = 64
CHUNKS_IN_FLIGHT = 16
FWD_CHUNKS_IN_FLIGHT = 32
SCAN_UNROLL = 4
HG_SCALE = float(LANES) ** -0.5
EPS = 1e-6
ADAM_LR, ADAM_B1, ADAM_B2, ADAM_EPS, ADAM_WD, ADAM_STEP = 0.001, 0.9, 0.999, 1e-08, 0.01, 10
MATMUL_TOKENS = 1024
TAIL_TOKENS = 256
CONTRACT_TOKENS = 2048
VMEM_LIMIT = 56 * 1024 * 1024
VMEM_LIMIT_BIG = 60 * 1024 * 1024
MESH = pl.DeviceIdType.MESH

_SLOT_TO_GROUP = (2, 3, 4, 5, 0, 1, 6, 7)


def _mm(a, b):
    return lax.dot_general(a.astype(_MXU_DTYPE), b.astype(_MXU_DTYPE), (((1,), (0,)), ((), ())),
                           preferred_element_type=F32)


def _mm_nt(a, b):
    return lax.dot_general(a.astype(_MXU_DTYPE), b.astype(_MXU_DTYPE), (((1,), (1,)), ((), ())),
                           preferred_element_type=F32)


def _mm_tn(a, b):
    return lax.dot_general(a.astype(_MXU_DTYPE), b.astype(_MXU_DTYPE), (((0,), (0,)), ((), ())),
                           preferred_element_type=F32)


def _sigmoid(x):
    return 0.5 * jnp.tanh(0.5 * x) + 0.5


def _log1p_pos(y):
    series = y * (1.0 - y * (0.5 - y * (1.0 / 3.0 - y * 0.25)))
    return jnp.where(y < 0.01, series, jnp.log(1.0 + y))


def _softplus(x):
    return jnp.maximum(x, 0.0) + _log1p_pos(jnp.exp(-jnp.abs(x)))


def _shift_down(x, n):
    rolled = pltpu.roll(x, n, 0)
    edge = SUBLANES if (n < SUBLANES and x.shape[0] > SUBLANES) else x.shape[0]
    rows = lax.broadcasted_iota(jnp.int32, (edge, x.shape[1]), 0)
    head = jnp.where(rows >= n, rolled[:edge], 0.0)
    return head if edge == x.shape[0] else jnp.concatenate([head, rolled[edge:]], axis=0)


def _shift_up(x, n):
    size = x.shape[0]
    rolled = pltpu.roll(x, size - n, 0)
    edge = SUBLANES if (n < SUBLANES and size > SUBLANES) else size
    rows = lax.broadcasted_iota(jnp.int32, (edge, x.shape[1]), 0)
    tail = jnp.where(rows < edge - n, rolled[size - edge:], 0.0)
    return tail if edge == size else jnp.concatenate([rolled[:size - edge], tail], axis=0)


def _params(dims, vmem=VMEM_LIMIT):
    return pltpu.CompilerParams(dimension_semantics=dims, vmem_limit_bytes=vmem)


def _slot_of_group(g):
    return jnp.where(g < 2, g + 4, jnp.where(g < 6, g - 2, g))


def _inproj_fwd_gather(x2d, norm_g, slotted, conv_slotted, chip):
    tokens, d = x2d.shape
    tm = min(MATMUL_TOKENS, tokens)
    n_tiles = tokens // tm
    n_big = len(slotted)
    n_sem = 6 * (n_big + 1) + 3
    last_pass = N_GROUPS - 1

    def shard_of(k, chip_id):
        x, y = chip_id // 2, chip_id % 2
        return 2 * jnp.where(k % 2 == 1, 1 - x, x) + jnp.where(k // 2 == 1, 1 - y, y)

    def body(chip_ref, x_ref, g_ref, *rest):
        bufs, cw = rest[n_big + 1:2 * n_big + 1], rest[2 * n_big + 1]
        z_ref, ht_ref = rest[2 * n_big + 2:2 * n_big + 4]
        h_all, slab, send_sems, recv_sems, slab_sems = rest[2 * n_big + 4:]
        del chip_ref
        p, i = pl.program_id(0), pl.program_id(1)
        x, y, c, chips = _mesh_position()
        me, sibling = 2 * x + y, (x, y, 1 - c)

        pieces = [(0, 0), (0, 1)] + [(a, None) for a in range(1, n_big)]

        def half(piece, slot, which):
            a, q = pieces[piece]
            hs = bufs[a].shape[1] // 2
            cols = slice(None) if q is None else pl.ds(q * D_MODEL, D_MODEL)
            return bufs[a].at[slot, pl.ds(which * hs, hs), cols]

        def send(piece, j):
            mine = half(piece, me, c)
            return _remote(mine, mine, send_sems, recv_sems, 6 * piece + j, (chips[j][0], chips[j][1], c))

        def arrival(piece, j):
            landed = half(piece, 2 * chips[j][0] + chips[j][1], c)
            return _remote(landed, landed, send_sems, recv_sems, 6 * piece + j, (chips[j][0], chips[j][1], c))

        def passed_on(piece, j, which):
            landed = half(piece, 2 * chips[j][0] + chips[j][1], which)
            return _remote(landed, landed, send_sems, recv_sems, 6 * piece + 3 + j, sibling)

        def conv_copy(j, slot):
            return _remote(cw.at[slot], cw.at[slot], send_sems, recv_sems, 6 * len(pieces) + j,
                           (chips[j][0], chips[j][1], c))

        def land(piece, j):
            arrival(piece, j).wait_recv()
            passed_on(piece, j, c).start()
            passed_on(piece, j, 1 - c).wait_recv()

        def slab_copy(pv):
            src = bufs[0].at[shard_of(pv // 2, me), :, pl.ds((pv % 2) * D_MODEL, D_MODEL)]
            return pltpu.make_async_copy(src, slab.at[pv % 2], slab_sems.at[pv % 2])

        @pl.when((p == 0) & (i == 0))
        def _():
            for q in range(2):
                send(q, 0).start()
                send(q, 1).start()
            slab_copy(0).start(priority=SIDE_STREAM)

        @pl.when(i == 0)
        def _():
            for pv in range(N_GROUPS):
                @pl.when(p == pv)
                def _(pv=pv):
                    slab_copy(pv).wait()

        rows = pl.ds(pl.multiple_of(i * tm, tm), tm)

        @pl.when(p == 0)
        def _():
            xt = x_ref[...]
            r = lax.rsqrt(jnp.mean(xt * xt, axis=-1, keepdims=True) + EPS)
            h = (xt * r) * g_ref[...]
            h_all[rows, :] = h.astype(_MXU_DTYPE)
            ht_ref[...] = jnp.transpose(h).astype(_MXU_DTYPE)

        z_ref[...] = _mm(h_all[rows, :], slab[p % 2])

        def relay(q):
            landed = half(q, 2 * chips[q][0] + chips[q][1], c)
            to = chips[1 - q]
            return _remote(landed, landed, send_sems, recv_sems, 6 * q + 2, (to[0], to[1], c))

        landings = {1: [(0, 0)], 2: [(1, 0), (1, 1)], 3: [(0, 1)], 5: [(0, 2)], 6: [(1, 2)]}

        def end_of_pass(pv):
            for q, j in landings.get(pv - 1, []):
                land(q, j)
                if j == q:
                    relay(q).start()
            if pv - 1 == 1:
                for piece in range(2, len(pieces)):
                    for jj in range(3):
                        send(piece, jj).start()
                for jj in range(3):
                    conv_copy(jj, me).start()
            if pv - 1 in (5, 6):
                for piece in range(2, len(pieces)):
                    for jj in ((0, 1) if pv - 1 == 5 else (2,)):
                        land(piece, jj)
            slab_copy(pv).start(priority=SIDE_STREAM)

        @pl.when(i == n_tiles - 1)
        def _():
            for pv in range(1, N_GROUPS):
                pl.when(p == pv - 1)(functools.partial(end_of_pass, pv))

        @pl.when((p == last_pass) & (i == n_tiles - 1))
        def _():
            for j in range(3):
                conv_copy(j, 2 * chips[j][0] + chips[j][1]).wait_recv()
            for piece in range(len(pieces)):
                for j in range(3):
                    (relay(piece) if (piece < 2 and j == 2) else send(piece, j)).wait_send()
                    passed_on(piece, j, c).wait_send()
            for j in range(3):
                conv_copy(j, me).wait_send()

    def z_index(p, i, chip_ref):
        g = 2 * shard_of(p // 2, chip_ref[0]) + p % 2
        return (_slot_of_group(g), i, 0)

    def first_pass_tile(p, i, chip_ref):
        return jnp.where(p == 0, i, n_tiles - 1)

    hbm = pl.BlockSpec(memory_space=pl.ANY)
    operands = list(slotted) + [conv_slotted]
    grid_spec = pltpu.PrefetchScalarGridSpec(
        num_scalar_prefetch=1, grid=(N_GROUPS, n_tiles),
        in_specs=[pl.BlockSpec((tm, d), lambda p, i, chip_ref: (first_pass_tile(p, i, chip_ref), 0)),
                  pl.BlockSpec((1, d), lambda p, i, chip_ref: (0, 0))] + [hbm] * (n_big + 1),
        out_specs=[hbm] * (n_big + 1) + [pl.BlockSpec((None, tm, D_MODEL), z_index),
                                         pl.BlockSpec((d, tm), lambda p, i, chip_ref: (0, first_pass_tile(p, i, chip_ref)))],
        scratch_shapes=[pltpu.VMEM((tokens, d), _MXU_DTYPE), pltpu.VMEM((2, d, D_MODEL), _MXU_DTYPE),
                        pltpu.SemaphoreType.DMA((n_sem,)), pltpu.SemaphoreType.DMA((n_sem,)),
                        pltpu.SemaphoreType.DMA((2,))])
    out = pl.pallas_call(
        body, name="inproj_fwd_gather", grid_spec=grid_spec,
        out_shape=[jax.ShapeDtypeStruct(a.shape, a.dtype) for a in operands]
        + [jax.ShapeDtypeStruct((N_GROUPS, tokens, D_MODEL), F32), jax.ShapeDtypeStruct((d, tokens), _MXU_DTYPE)],
        input_output_aliases={3 + a: a for a in range(n_big + 1)},
        compiler_params=_params(("arbitrary", "arbitrary")),
    )(chip, x2d, norm_g, *operands)
    return out[n_big + 1], out[n_big + 2], out[:n_big], out[n_big]


def _lane_blocks(x):
    return [x[:, k * LANES:(k + 1) * LANES] for k in range(x.shape[1] // LANES)]


def _block_diag(x, w_ref, transposed=False):
    mm = _mm_nt if transposed else _mm
    return jnp.concatenate([mm(xk, w_ref[k]) for k, xk in enumerate(_lane_blocks(x))], axis=1)


def _lru_decay(gr, sp):
    log_a = (-LRU_C) * gr * sp
    a = jnp.exp(log_a)
    y = 2.0 * log_a
    mult_sq = jnp.where(y > -1e-3, -y * (1.0 + 0.5 * y), 1.0 - a * a)
    inv_mult = lax.rsqrt(jnp.maximum(mult_sq, 1e-37))
    return a, mult_sq * inv_mult, inv_mult


def _tile_rows(width):
    return lax.broadcasted_iota(jnp.int32, (SUBLANES, width), 0)


def _scan_forward(a_scr, u_scr, h_scr, seq):
    width = a_scr.shape[1]
    rows = _tile_rows(width)

    group = min(SCAN_UNROLL, seq // SUBLANES)

    def within_tile(j):
        sl = pl.ds(pl.multiple_of(j * SUBLANES, SUBLANES), SUBLANES)
        a = a_scr[sl, :]
        u = u_scr[sl, :]
        for d in (1, 2, 4):
            keep = rows >= d
            a_sh = jnp.where(keep, pltpu.roll(a, d, 0), 1.0)
            u_sh = jnp.where(keep, pltpu.roll(u, d, 0), 0.0)
            u = a * u_sh + u
            a = a * a_sh
        return sl, a, u

    def tiles(t, carry):
        parts = [within_tile(t * group + k) for k in range(group)]
        out = []
        for sl, a, u in parts:
            h = u + a * carry
            out.append((sl, h))
            carry = jnp.broadcast_to(h[SUBLANES - 1:SUBLANES, :], (SUBLANES, width))
        for sl, h in out:
            h_scr[sl, :] = h
        return carry

    lax.fori_loop(0, seq // SUBLANES // group, tiles, jnp.zeros((SUBLANES, width), F32))


def _scan_backward(c_scr, d_scr, g_scr, seq):
    width = c_scr.shape[1]
    rows = _tile_rows(width)
    n_tiles = seq // SUBLANES

    group = min(SCAN_UNROLL, n_tiles)

    def within_tile(j):
        sl = pl.ds(pl.multiple_of(j * SUBLANES, SUBLANES), SUBLANES)
        c = c_scr[sl, :]
        g = d_scr[sl, :]
        for d in (1, 2, 4):
            keep = rows < SUBLANES - d
            c_sh = jnp.where(keep, pltpu.roll(c, SUBLANES - d, 0), 1.0)
            g_sh = jnp.where(keep, pltpu.roll(g, SUBLANES - d, 0), 0.0)
            g = c * g_sh + g
            c = c * c_sh
        return sl, c, g

    def tiles(t, carry):
        parts = [within_tile(n_tiles - 1 - (t * group + k)) for k in range(group)]
        out = []
        for sl, c, g in parts:
            g = g + c * carry
            out.append((sl, g))
            carry = jnp.broadcast_to(g[0:1, :], (SUBLANES, width))
        for sl, g in out:
            g_scr[sl, :] = g
        return carry

    lax.fori_loop(0, n_tiles // group, tiles, jnp.zeros((SUBLANES, width), F32))


LRU_BLOCKS_PER_STEP = 2
LRU_LANES = LRU_BLOCKS_PER_STEP * LANES
LRU_STEPS = N_BLK // LRU_BLOCKS_PER_STEP


def _branch_a_fwd(z, conv_w, conv_b, wx, bx, wa, ba, lam, batch, seq):
    tokens = batch * seq

    def body(z_ref, cw_ref, cb_ref, wx_ref, bx_ref, wa_ref, ba_ref, lam_ref, ya_ref, hl_ref, kept_ref, a_scr, u_scr):
        xa = z_ref[0]
        ga = z_ref[1]
        xc = (cb_ref[...] + cw_ref[3:4, :] * xa + cw_ref[2:3, :] * _shift_down(xa, 1)
              + cw_ref[1:2, :] * _shift_down(xa, 2) + cw_ref[0:1, :] * _shift_down(xa, 3))
        gi = _sigmoid(_block_diag(xc, wx_ref) + bx_ref[...])
        gr = _sigmoid(_block_diag(xc, wa_ref) + ba_ref[...])
        a, mult, _ = _lru_decay(gr, _softplus(-lam_ref[...]))
        kept_ref[0], kept_ref[1], kept_ref[2] = xc, gi, gr
        a_scr[...] = a
        u_scr[...] = mult * gi * xc
        _scan_forward(a_scr, u_scr, hl_ref, seq)
        ya_ref[...] = (hl_ref[...] * (ga * _sigmoid(ga))).astype(_MXU_DTYPE)

    blk = pl.BlockSpec((seq, LRU_LANES), lambda b, c: (b, c))
    vec = pl.BlockSpec((1, LRU_LANES), lambda b, c: (0, c))
    mat = pl.BlockSpec((LRU_BLOCKS_PER_STEP, LANES, LANES), lambda b, c: (c, 0, 0))
    return pl.pallas_call(
        body, name="branch_a_fwd",
        grid=(batch, LRU_STEPS),
        in_specs=[pl.BlockSpec((2, seq, LRU_LANES), lambda b, c: (2, b, c)),
                  pl.BlockSpec((CONV_WIDTH, LRU_LANES), lambda b, c: (0, c)), vec, mat, vec, mat, vec, vec],
        out_specs=[blk, blk, pl.BlockSpec((3, seq, LRU_LANES), lambda b, c: (0, b, c))],
        out_shape=[jax.ShapeDtypeStruct((tokens, D_MODEL), _MXU_DTYPE), jax.ShapeDtypeStruct((tokens, D_MODEL), F32),
                   jax.ShapeDtypeStruct((3, tokens, D_MODEL), F32)],
        scratch_shapes=[pltpu.VMEM((seq, LRU_LANES), F32), pltpu.VMEM((seq, LRU_LANES), F32)],
        compiler_params=_params(("parallel", "parallel")),
    )(z, conv_w, conv_b, wx, bx, wa, ba, lam)


def _branch_a_bwd(z, hl, kept, dya, dz, conv_w, wx, wa, lam, batch, seq):
    def body(z_ref, hl_ref, kept_ref, dya_ref, dz_in_ref, cw_ref, wx_ref, wa_ref, lam_ref,
             dz_ref, dcw_ref, dcb_ref, dwx_ref, dbx_ref, dwa_ref, dba_ref, dlam_ref, c_scr, d_scr):
        del dz_in_ref
        g_scr = d_scr
        xa = z_ref[0]
        ga = z_ref[1]
        hl = hl_ref[...]
        dya = dya_ref[...]
        xc, gi, gr = kept_ref[0], kept_ref[1], kept_ref[2]
        sp = _softplus(-lam_ref[...])
        a, mult, inv_mult = _lru_decay(gr, sp)
        sga = _sigmoid(ga)
        dz_ref[1] = (dya * hl * (sga * (1.0 + ga * (1.0 - sga)))).astype(_MXU_DTYPE)
        c_scr[...] = _shift_up(a, 1)
        d_scr[...] = dya * (ga * sga)
        _scan_backward(c_scr, d_scr, g_scr, seq)
        g = g_scr[...]
        da = g * _shift_down(hl, 1)
        dmult = g * gi * xc
        dgi = g * mult * xc
        dxc = g * mult * gi
        dlog_a = da * a - dmult * (a * a) * inv_mult
        dgr = dlog_a * (-LRU_C) * sp
        dsp = jnp.sum(dlog_a * gr, axis=0, keepdims=True) * (-LRU_C)
        dlam = -dsp * _sigmoid(-lam_ref[...])
        dpi = dgi * gi * (1.0 - gi)
        dpr = dgr * gr * (1.0 - gr)
        dxc = dxc + _block_diag(dpi, wx_ref, transposed=True) + _block_diag(dpr, wa_ref, transposed=True)
        dwx = jnp.stack([_mm_tn(xk, dk) for xk, dk in zip(_lane_blocks(xc), _lane_blocks(dpi))])
        dwa = jnp.stack([_mm_tn(xk, dk) for xk, dk in zip(_lane_blocks(xc), _lane_blocks(dpr))])
        dbx = jnp.sum(dpi, axis=0, keepdims=True)
        dba = jnp.sum(dpr, axis=0, keepdims=True)
        ahead = [dxc if k == CONV_WIDTH - 1 else _shift_up(dxc, CONV_WIDTH - 1 - k) for k in range(CONV_WIDTH)]
        dxa = sum(cw_ref[k:k + 1, :] * ahead[k] for k in range(CONV_WIDTH))
        dz_ref[0] = dxa.astype(_MXU_DTYPE)
        dcb = jnp.sum(dxc, axis=0, keepdims=True)
        dcw = [jnp.sum(ahead[k] * xa, axis=0, keepdims=True) for k in range(CONV_WIDTH)]

        @pl.when(pl.program_id(1) == 0)
        def _():
            for k in range(CONV_WIDTH):
                dcw_ref[k:k + 1, :] = dcw[k]
            dcb_ref[...] = dcb
            dwx_ref[...] = dwx
            dbx_ref[...] = dbx
            dwa_ref[...] = dwa
            dba_ref[...] = dba
            dlam_ref[...] = dlam

        @pl.when(pl.program_id(1) != 0)
        def _():
            for k in range(CONV_WIDTH):
                dcw_ref[k:k + 1, :] += dcw[k]
            dcb_ref[...] += dcb
            dwx_ref[...] += dwx
            dbx_ref[...] += dbx
            dwa_ref[...] += dwa
            dba_ref[...] += dba
            dlam_ref[...] += dlam

    tokens = batch * seq
    blk = pl.BlockSpec((seq, LRU_LANES), lambda c, b: (b, c))
    vec = pl.BlockSpec((1, LRU_LANES), lambda c, b: (0, c))
    mat = pl.BlockSpec((LRU_BLOCKS_PER_STEP, LANES, LANES), lambda c, b: (c, 0, 0))
    vec_shape = jax.ShapeDtypeStruct((1, D_MODEL), F32)
    mat_shape = jax.ShapeDtypeStruct((N_BLK, LANES, LANES), F32)
    return pl.pallas_call(
        body, name="branch_a_bwd",
        grid=(LRU_STEPS, batch),
        in_specs=[pl.BlockSpec((2, seq, LRU_LANES), lambda c, b: (2, b, c)), blk,
                  pl.BlockSpec((3, seq, LRU_LANES), lambda c, b: (0, b, c)), blk,
                  pl.BlockSpec(memory_space=pl.ANY),
                  pl.BlockSpec((CONV_WIDTH, LRU_LANES), lambda c, b: (0, c)), mat, mat, vec],
        out_specs=[pl.BlockSpec((2, seq, LRU_LANES), lambda c, b: (2, b, c)),
                   pl.BlockSpec((CONV_WIDTH, LRU_LANES), lambda c, b: (0, c)), vec, mat, vec, mat, vec, vec],
        out_shape=[jax.ShapeDtypeStruct((N_GROUPS, tokens, D_MODEL), _MXU_DTYPE),
                   jax.ShapeDtypeStruct((CONV_WIDTH, D_MODEL), F32), vec_shape, mat_shape, vec_shape, mat_shape,
                   vec_shape, vec_shape],
        scratch_shapes=[pltpu.VMEM((seq, LRU_LANES), F32)] * 2,
        input_output_aliases={4: 0},
        compiler_params=_params(("parallel", "arbitrary"), vmem=VMEM_LIMIT_BIG),
    )(z, hl, kept, dya, dz, conv_w, wx, wa, lam)


def _chunk_masks(transposed=False):
    r = lax.broadcasted_iota(jnp.int32, (CHUNK, CHUNK), 0)
    c = lax.broadcasted_iota(jnp.int32, (CHUNK, CHUNK), 1)
    return r <= c if transposed else r >= c


def _row_blocks(seq, fn):
    block = min(256, seq)

    def trip(i, carry):
        fn(pl.ds(pl.multiple_of(i * block, block), block))
        return carry

    lax.fori_loop(0, seq // block, trip, 0)


def _hgrn_prepare(z_ref, lb_ref, f_scr, logf_scr, qh_scr, seq):
    lb = _sigmoid(lb_ref[0:1, :] - lb_ref[1:2, :])

    def block(rows):
        q = z_ref[0, rows, :]
        f = lb + (1.0 - lb) * _sigmoid(z_ref[1, rows, :])
        f_scr[rows, :] = f
        logf_scr[rows, :] = jnp.log(f)
        qh_scr[rows, :] = q * _sigmoid(q)

    _row_blocks(seq, block)
    return lb


def _cumsum_rows(x, reverse=False):
    shift = _shift_up if reverse else _shift_down
    d = 1
    while d < x.shape[0]:
        x = x + shift(x, d)
        d *= 2
    return x


def _lane_mean(x):
    return jnp.mean(x, axis=-1, keepdims=True)


def _token_contractions(lhs_scr, rhs_scr, out_ref, seq):
    rows_id = lax.broadcasted_iota(jnp.int32, (LANES, LANES), 0)

    def transposed(p):
        rows = pl.ds(pl.multiple_of(p * LANES, LANES), LANES)
        return jnp.transpose(lhs_scr[rows, :]).astype(_MXU_DTYPE), rhs_scr[rows, :]

    def contract(p, s):
        lhs_t, rhs = s
        return (_mm(lhs_t, jnp.where(rows_id < CHUNK, rhs, 0.0)), _mm(lhs_t, jnp.where(rows_id >= CHUNK, rhs, 0.0)))

    def store(p, out):
        out_ref[2 * p] = out[0]
        out_ref[2 * p + 1] = out[1]

    _independent_trips(seq // LANES, [transposed, contract], store)


def _chunk_rows(c):
    return pl.ds(pl.multiple_of(c * CHUNK, CHUNK), CHUNK)


def _chunk_terms(c, z_ref, f_scr, qh_scr, b_scr):
    rows = _chunk_rows(c)
    b = b_scr[rows, :]
    b_mid = b_scr[pl.ds(c * CHUNK + CHUNK // 2, 1), :]
    b_last = b_scr[pl.ds(c * CHUNK + CHUNK - 1, 1), :]
    qh = qh_scr[rows, :]
    k = 1.0 - f_scr[rows, :]
    v = z_ref[2, rows, :]
    e_q = jnp.exp(b - b_mid) * HG_SCALE
    e_k = jnp.exp(b_mid - b)
    e_qi = jnp.exp(b) * HG_SCALE
    e_ks = jnp.exp(b_last - b)
    decay = jnp.exp(b_last)
    return rows, qh, k, v, e_q, e_k, e_qi, e_ks, decay


def _independent_trips(n, stages, store, group=CHUNKS_IN_FLIGHT):
    stages = stages if isinstance(stages, (list, tuple)) else [stages]
    group = min(group, n)

    def trip(g, carry):
        ids = [g * group + i for i in range(group)]
        state = [stages[0](c) for c in ids]
        for stage in stages[1:]:
            state = [stage(c, s) for c, s in zip(ids, state)]
        for c, s in zip(ids, state):
            store(c, s)
        return carry

    lax.fori_loop(0, n // group, trip, 0)


def _branch_b_fwd(z, lb_logits, hg_g, batch, seq):
    tokens = batch * seq
    n_chunks = seq // CHUNK

    def body(z_ref, lb_ref, g_ref, yb_ref, st_ref, kept_ref, logf_scr, o_scr, qi_scr, ks_scr, dec_scr):
        f_scr, qh_scr, b_scr, o_kept = (kept_ref.at[k] for k in range(4))
        _hgrn_prepare(z_ref, lb_ref, f_scr, logf_scr, qh_scr, seq)
        causal = _chunk_masks()
        gain = g_ref[...]

        def cumulate(c):
            return _cumsum_rows(logf_scr[_chunk_rows(c), :])

        def store_cumulated(c, b):
            b_scr[_chunk_rows(c), :] = b

        def scores(c):
            _, qh, k, v, e_q, e_k, e_qi, e_ks, decay = _chunk_terms(c, z_ref, f_scr, qh_scr, b_scr)
            return _mm_nt(qh * e_q, k * e_k), v, qh * e_qi, k * e_ks, decay

        def within_chunk(c, s):
            att, v, q_int, k_st, decay = s
            return _mm(jnp.where(causal, att, 0.0), v), q_int, k_st, decay

        def store_within_chunk(c, out):
            rows = _chunk_rows(c)
            o_scr[rows, :], qi_scr[rows, :], ks_scr[rows, :], dec_scr[pl.ds(c, 1), :] = out

        def carry_state(c, state_t):
            update = st_ref[c]
            st_ref[c] = state_t
            return state_t * dec_scr[pl.ds(c, 1), :] + update

        def finish(c):
            rows = _chunk_rows(c)
            o = o_scr[rows, :] + _mm_nt(qi_scr[rows, :], st_ref[c])
            r = lax.rsqrt(_lane_mean(o * o) + EPS)
            gb = z_ref[3, rows, :]
            return (((o * r) * gain) * (gb * _sigmoid(gb))).astype(_MXU_DTYPE), o

        def store_finished(c, out):
            yb_ref[_chunk_rows(c), :], o_kept[_chunk_rows(c), :] = out

        _independent_trips(n_chunks, cumulate, store_cumulated, FWD_CHUNKS_IN_FLIGHT)
        _independent_trips(n_chunks, [scores, within_chunk], store_within_chunk, FWD_CHUNKS_IN_FLIGHT)
        _token_contractions(z_ref.at[2], ks_scr, st_ref, seq)
        lax.fori_loop(0, n_chunks, carry_state, jnp.zeros((LANES, LANES), F32))
        _independent_trips(n_chunks, finish, store_finished, FWD_CHUNKS_IN_FLIGHT)

    seq_buf = pltpu.VMEM((seq, LANES), F32)
    return pl.pallas_call(
        body, name="branch_b_fwd",
        grid=(batch, N_BLK),
        in_specs=[pl.BlockSpec((4, seq, LANES), lambda b, h: (0, b, h)),
                  pl.BlockSpec((2, LANES), lambda b, h: (0, h)),
                  pl.BlockSpec((1, LANES), lambda b, h: (0, 0))],
        out_specs=[pl.BlockSpec((seq, LANES), lambda b, h: (b, h)),
                   pl.BlockSpec((None, n_chunks, LANES, LANES), lambda b, h: (b * N_BLK + h, 0, 0, 0)),
                   pl.BlockSpec((4, seq, LANES), lambda b, h: (0, b, h))],
        out_shape=[jax.ShapeDtypeStruct((tokens, D_MODEL), _MXU_DTYPE),
                   jax.ShapeDtypeStruct((batch * N_BLK, n_chunks, LANES, LANES), F32),
                   jax.ShapeDtypeStruct((4, tokens, D_MODEL), F32)],
        scratch_shapes=[seq_buf] * 4 + [pltpu.VMEM((n_chunks, LANES), F32)],
        compiler_params=_params(("parallel", "parallel")),
    )(z, lb_logits, hg_g)


def _branch_b_bwd(z, states, kept, dyb, dz, lb_logits, hg_g, batch, seq):
    n_chunks = seq // CHUNK

    def body(z_ref, st_ref, kept_ref, dyb_ref, dz_in_ref, lb_ref, g_ref, dz_ref, dlog_ref, dg_ref,
             do_scr, qi_scr, dqh_scr, df_scr, dec_scr, dgp_scr, dlb_scr, dst_scr):
        del dz_in_ref
        f_scr, qh_scr, b_scr, o_kept = (kept_ref.at[k] for k in range(4))
        first = (pl.program_id(0) == 0) & (pl.program_id(1) == 0)
        lb = _sigmoid(lb_ref[0:1, :] - lb_ref[1:2, :])
        causal = _chunk_masks()
        anti_causal = _chunk_masks(transposed=True)
        gain = g_ref[...]

        @pl.when(first)
        def _():
            dg_ref[...] = jnp.zeros_like(dg_ref)

        @pl.when(pl.program_id(1) == 0)
        def _():
            dlb_scr[...] = jnp.zeros_like(dlb_scr)

        def output_gradient(c):
            rows = _chunk_rows(c)
            b = b_scr[rows, :]
            q_int = qh_scr[rows, :] * (jnp.exp(b) * HG_SCALE)
            decay = jnp.exp(b_scr[pl.ds(c * CHUNK + CHUNK - 1, 1), :])
            o = o_kept[rows, :]
            r = lax.rsqrt(_lane_mean(o * o) + EPS)
            o_n = o * r
            gb = z_ref[3, rows, :]
            sgb = _sigmoid(gb)
            dyb_c = dyb_ref[rows, :]
            d_ong = dyb_c * (gb * sgb)
            d_gb = (dyb_c * (o_n * gain) * (sgb * (1.0 + gb * (1.0 - sgb)))).astype(_MXU_DTYPE)
            d_gain = jnp.sum(d_ong * o_n, axis=0, keepdims=True)
            d_on = d_ong * gain
            return d_gb, d_gain, r * (d_on - o_n * _lane_mean(d_on * o_n)), q_int, decay

        def store_output_gradient(c, out):
            rows = _chunk_rows(c)
            dz_ref[3, rows, :], dgp_scr[pl.ds(c, 1), :], do_scr[rows, :], qi_scr[rows, :], dec_scr[pl.ds(c, 1), :] = out

        def carry_state_gradient(cc, d_state_t):
            c = n_chunks - 1 - cc
            update = dst_scr[c]
            dst_scr[c] = d_state_t
            return d_state_t * dec_scr[pl.ds(c, 1), :] + update

        def score_gradients(c):
            rows, qh, k, v, e_q, e_k, e_qi, e_ks, decay = _chunk_terms(c, z_ref, f_scr, qh_scr, b_scr)
            state_t = st_ref[c]
            d_state_t = dst_scr[c]
            d_o = do_scr[rows, :]
            q_in, k_in, q_int, k_st = qh * e_q, k * e_k, qh * e_qi, k * e_ks
            first = (_mm_nt(k_in, q_in), _mm_nt(d_o, v), _mm_nt(v, d_o), _mm_nt(k_st, d_state_t), _mm(d_o, state_t),
                     _mm(v, d_state_t))
            d_decay = jnp.sum(state_t * d_state_t, axis=0, keepdims=True)
            return first, d_o, q_in, k_in, q_int, k_st, e_q, e_k, e_qi, e_ks, decay, d_decay

        def input_gradients(c, s):
            (att_t, d_att, d_att_t, dv_inter, dq_int, dk_st), d_o, q_in, k_in, q_int, k_st, e_q, e_k, e_qi, e_ks, decay, d_decay = s
            rows = _chunk_rows(c)
            d_v = _mm(jnp.where(anti_causal, att_t, 0.0), d_o) + dv_inter
            dq_in = _mm(jnp.where(causal, d_att, 0.0), k_in)
            dk_in = _mm(jnp.where(anti_causal, d_att_t, 0.0), q_in)
            d_k = dk_in * e_k + dk_st * e_ks
            kk = dk_st * k_st
            d_b = dq_in * q_in + dq_int * q_int - dk_in * k_in - kk
            d_b_last = jnp.sum(kk, axis=0, keepdims=True) + decay * d_decay
            d_logf = _cumsum_rows(d_b, reverse=True) + d_b_last
            return d_v.astype(_MXU_DTYPE), dq_in * e_q + dq_int * e_qi, d_logf / f_scr[rows, :] - d_k

        def store_input_gradients(c, out):
            rows = _chunk_rows(c)
            dz_ref[2, rows, :], dqh_scr[rows, :], df_scr[rows, :] = out

        def input_activations(rows):
            q = z_ref[0, rows, :]
            sq = _sigmoid(q)
            dz_ref[0, rows, :] = (dqh_scr[rows, :] * (sq * (1.0 + q * (1.0 - sq)))).astype(_MXU_DTYPE)
            sg = _sigmoid(z_ref[1, rows, :])
            d_f = df_scr[rows, :]
            dz_ref[1, rows, :] = (d_f * (1.0 - lb) * sg * (1.0 - sg)).astype(_MXU_DTYPE)
            dlb_scr[...] += jnp.sum(d_f * (1.0 - sg), axis=0, keepdims=True)

        _independent_trips(n_chunks, output_gradient, store_output_gradient)
        _token_contractions(do_scr, qi_scr, dst_scr, seq)
        lax.fori_loop(0, n_chunks, carry_state_gradient, jnp.zeros((LANES, LANES), F32))
        _independent_trips(n_chunks, [score_gradients, input_gradients], store_input_gradients)
        dg_ref[...] += jnp.sum(dgp_scr[...], axis=0, keepdims=True)
        _row_blocks(seq, input_activations)
        d_l0 = dlb_scr[...] * lb * (1.0 - lb)
        dlog_ref[0:1, :] = d_l0
        dlog_ref[1:2, :] = -d_l0

    tokens = batch * seq
    seq_buf = pltpu.VMEM((seq, LANES), F32)
    chunk_rows = pltpu.VMEM((n_chunks, LANES), F32)
    return pl.pallas_call(
        body, name="branch_b_bwd",
        grid=(N_BLK, batch),
        in_specs=[pl.BlockSpec((4, seq, LANES), lambda h, b: (0, b, h)),
                  pl.BlockSpec((None, n_chunks, LANES, LANES), lambda h, b: (b * N_BLK + h, 0, 0, 0)),
                  pl.BlockSpec((4, seq, LANES), lambda h, b: (0, b, h)),
                  pl.BlockSpec((seq, LANES), lambda h, b: (b, h)),
                  pl.BlockSpec(memory_space=pl.ANY),
                  pl.BlockSpec((2, LANES), lambda h, b: (0, h)),
                  pl.BlockSpec((1, LANES), lambda h, b: (0, 0))],
        out_specs=[pl.BlockSpec((4, seq, LANES), lambda h, b: (0, b, h)),
                   pl.BlockSpec((2, LANES), lambda h, b: (0, h)),
                   pl.BlockSpec((1, LANES), lambda h, b: (0, 0))],
        out_shape=[jax.ShapeDtypeStruct((N_GROUPS, tokens, D_MODEL), _MXU_DTYPE),
                   jax.ShapeDtypeStruct((2, D_MODEL), F32),
                   jax.ShapeDtypeStruct((1, LANES), F32)],
        scratch_shapes=[seq_buf] * 4 + [chunk_rows, chunk_rows, pltpu.VMEM((1, LANES), F32),
                                        pltpu.VMEM((n_chunks, LANES, LANES), F32)],
        input_output_aliases={4: 0},
        compiler_params=_params(("arbitrary", "arbitrary")),
    )(z, states, kept, dyb, dz, lb_logits, hg_g)


def _merge_tail(ya, yb, z, x2d, tgt2d, b_merge, final_g, pa, pb, wo):
    tokens, d = x2d.shape
    tm = min(TAIL_TOKENS, tokens)
    n_tiles = tokens // tm

    def body(ya_ref, yb_ref, z_ref, x_ref, t_ref, bm_ref, fg_ref, pa_hbm, pb_hbm, wo_hbm,
             dya_ref, dyb_ref, dx2_ref, dz_ref, loss_ref, dfg_ref, dbm_ref, dpa_hbm, dpb_hbm, dwo_hbm,
             pa_s, pb_s, wo_s, dpa_s, dpb_s, dwo_s, sems):
        i = pl.program_id(0)

        def together(pairs):
            copies = [pltpu.make_async_copy(src, dst, sems.at[k]) for k, (src, dst) in enumerate(pairs)]
            for cp in copies:
                cp.start()
            for cp in copies:
                cp.wait()

        @pl.when(i == 0)
        def _():
            together([(pa_hbm, pa_s), (pb_hbm, pb_s), (wo_hbm, wo_s)])
            dpa_s[...] = jnp.zeros_like(dpa_s)
            dpb_s[...] = jnp.zeros_like(dpb_s)
            dwo_s[...] = jnp.zeros_like(dwo_s)
            loss_ref[...] = jnp.zeros_like(loss_ref)
            dfg_ref[...] = jnp.zeros_like(dfg_ref)
            dbm_ref[...] = jnp.zeros_like(dbm_ref)

        ya_t = ya_ref[...]
        yb_t = yb_ref[...]
        out_a = _mm(ya_t, pa_s[...])
        out_b = _mm(yb_t, pb_s[...])
        g_a = _sigmoid(z_ref[0] + bm_ref[:, :d])
        g_b = _sigmoid(z_ref[1] + bm_ref[:, d:])
        mixed = g_a * out_a + g_b * out_b
        x2 = x_ref[...] + _mm(mixed, wo_s[...])
        r = lax.rsqrt(jnp.mean(x2 * x2, axis=-1, keepdims=True) + EPS)
        xn = x2 * r
        fg = fg_ref[...]
        diff = xn * fg - t_ref[...]
        loss_ref[...] += jnp.sum(diff * diff) * (0.5 / d)
        dy = diff * (1.0 / d)
        dfg_ref[...] += jnp.sum(dy * xn, axis=0, keepdims=True)
        dxn = dy * fg
        dx2 = r * (dxn - xn * jnp.mean(dxn * xn, axis=-1, keepdims=True))
        dx2_ref[...] = dx2
        dmixed = _mm_nt(dx2, wo_s[...])
        dwo_s[...] += _mm_tn(mixed, dx2)
        dgm_a = dmixed * out_a * g_a * (1.0 - g_a)
        dgm_b = dmixed * out_b * g_b * (1.0 - g_b)
        dz_ref[0] = dgm_a.astype(_MXU_DTYPE)
        dz_ref[1] = dgm_b.astype(_MXU_DTYPE)
        dbm_ref[:, :d] += jnp.sum(dgm_a, axis=0, keepdims=True)
        dbm_ref[:, d:] += jnp.sum(dgm_b, axis=0, keepdims=True)
        dout_a = dmixed * g_a
        dout_b = dmixed * g_b
        dpa_s[...] += _mm_tn(ya_t, dout_a)
        dpb_s[...] += _mm_tn(yb_t, dout_b)
        dya_ref[...] = _mm_nt(dout_a, pa_s[...])
        dyb_ref[...] = _mm_nt(dout_b, pb_s[...])

        @pl.when(i == n_tiles - 1)
        def _():
            together([(dpa_s, dpa_hbm), (dpb_s, dpb_hbm), (dwo_s, dwo_hbm)])

    tile = pl.BlockSpec((tm, d), lambda i: (i, 0))
    gm = pl.BlockSpec((2, tm, d), lambda i: (3, i, 0))
    row = lambda n: pl.BlockSpec((1, n), lambda i: (0, 0))
    hbm = pl.BlockSpec(memory_space=pl.ANY)
    act = jax.ShapeDtypeStruct((tokens, d), F32)
    mat = jax.ShapeDtypeStruct((d, d), F32)
    return pl.pallas_call(
        body, name="merge_tail",
        grid=(n_tiles,),
        in_specs=[tile, tile, gm, tile, tile, row(2 * d), row(d), hbm, hbm, hbm],
        out_specs=[tile, tile, tile, gm, row(LANES), row(d), row(2 * d), hbm, hbm, hbm],
        out_shape=[act, act, act, jax.ShapeDtypeStruct((N_GROUPS, tokens, d), _MXU_DTYPE),
                   jax.ShapeDtypeStruct((1, LANES), F32), jax.ShapeDtypeStruct((1, d), F32),
                   jax.ShapeDtypeStruct((1, 2 * d), F32), mat, mat, mat],
        scratch_shapes=[pltpu.VMEM((d, d), _MXU_DTYPE)] * 3 + [pltpu.VMEM((d, d), F32)] * 3
        + [pltpu.SemaphoreType.DMA((3,))],
        compiler_params=_params(("arbitrary",)),
    )(ya, yb, z, x2d, tgt2d, b_merge, final_g, pa, pb, wo)


def _inproj_dw_exchange(h_t, dz, scatter):
    d, tokens = h_t.shape
    tm = min(CONTRACT_TOKENS, tokens)
    n_i = tokens // tm
    half = d // 2

    def body(h_ref, dz_ref, *rest):
        n_in, n_out = scatter.n_in, scatter.n_out
        dw_hbm, land_hbm = rest[n_in:n_in + 2]
        acc, local_sems, send_sems, recv_sems = rest[n_in + 2 + n_out:n_in + 6 + n_out]
        carried = scatter.copies(rest[:n_in], rest[n_in + 2:n_in + 2 + n_out], rest[n_in + 6 + n_out:])
        s, i = pl.program_id(0), pl.program_id(1)
        x, y, c, _ = _mesh_position()

        @pl.when((s == 0) & (i == 0))
        def _():
            for cp in carried:
                cp.start()

        part = _mm(h_ref[...], dz_ref[...])
        buf = acc.at[s % 2]

        @pl.when(i == 0)
        def _():
            buf[...] = part

        @pl.when(i != 0)
        def _():
            buf[...] += part

        def copies(k):
            g = _SLOT_TO_GROUP[k]
            cols = pl.ds((g % 2) * D_MODEL, D_MODEL)
            src = acc.at[k % 2]
            mine = pltpu.make_async_copy(src, dw_hbm.at[g // 2, :, cols], local_sems.at[k % 2])
            theirs = _remote(src.at[pl.ds((1 - c) * half, half), :], land_hbm.at[g // 2, :, cols],
                             send_sems, recv_sems, k, (x, y, 1 - c))
            return mine, theirs

        for k in range(N_GROUPS):
            @pl.when((s == k) & (i == n_i - 1))
            def _(k=k):
                if k > 0:
                    mine, theirs = copies(k - 1)
                    mine.wait()
                    theirs.wait_send()
                mine, theirs = copies(k)
                mine.start(priority=SIDE_STREAM)
                theirs.start()
                if k == N_GROUPS - 1:
                    mine.wait()
                    theirs.wait_send()
                    for kk in range(N_GROUPS):
                        copies(kk)[1].wait_recv()
                    for cp in carried:
                        cp.wait()

    hbm = pl.BlockSpec(memory_space=pl.ANY)
    more = scatter.plumbing(first_operand=2, first_output=2)
    return pl.pallas_call(
        body, name="inproj_dw_exchange",
        grid=(N_GROUPS, n_i),
        in_specs=[pl.BlockSpec((d, tm), lambda s, i: (0, i)),
                  pl.BlockSpec((None, tm, D_MODEL), lambda s, i: (s, i, 0))] + more[1],
        out_specs=[hbm, hbm] + more[2],
        out_shape=[jax.ShapeDtypeStruct((N_SHARDS, d, 2 * D_MODEL), F32),
                   jax.ShapeDtypeStruct((N_SHARDS, half, 2 * D_MODEL), F32)] + more[3],
        scratch_shapes=[pltpu.VMEM((2, d, D_MODEL), F32), pltpu.SemaphoreType.DMA((2,)),
                        pltpu.SemaphoreType.DMA((N_GROUPS,)), pltpu.SemaphoreType.DMA((N_GROUPS,))] + more[4],
        input_output_aliases=more[5],
        compiler_params=_params(("arbitrary", "arbitrary")),
    )(h_t, dz, *more[0])


def _inproj_dx(dz, w_all, x2d, dx2, norm_g, scatter):
    tokens, d = x2d.shape
    tm = min(TAIL_TOKENS, tokens)
    n_tiles = tokens // tm

    def body(dz_ref, w_hbm, x_ref, dx2_ref, g_ref, *rest):
        n_in, n_out = scatter.n_in, scatter.n_out
        dx_ref, dg_ref = rest[n_in:n_in + 2]
        w_res, load_sems = rest[n_in + 2 + n_out:n_in + 4 + n_out]
        copies = scatter.copies(rest[:n_in], rest[n_in + 2:n_in + 2 + n_out], rest[n_in + 4 + n_out:])
        i = pl.program_id(0)

        @pl.when(i == 0)
        def _():
            for cp in copies:
                cp.start()
            loads = [pltpu.make_async_copy(w_hbm.at[g // 2, :, pl.ds((g % 2) * D_MODEL, D_MODEL)],
                                           w_res.at[:, pl.ds(slot * D_MODEL, D_MODEL)], load_sems.at[slot])
                     for slot, g in enumerate(_SLOT_TO_GROUP)]
            for slot, cp in enumerate(loads):
                cp.start(priority=slot % 2)
            for cp in loads:
                cp.wait()
            dg_ref[...] = jnp.zeros_like(dg_ref)

        dz_all = jnp.concatenate([dz_ref[s] for s in range(N_GROUPS)], axis=1)
        dh = jnp.transpose(_mm_nt(w_res[...], dz_all))
        x = x_ref[...]
        r = lax.rsqrt(jnp.mean(x * x, axis=-1, keepdims=True) + EPS)
        xn = x * r
        dg_ref[...] += jnp.sum(dh * xn, axis=0, keepdims=True)
        dxn = dh * g_ref[...]
        dx_ref[...] = r * (dxn - xn * jnp.mean(dxn * xn, axis=-1, keepdims=True)) + dx2_ref[...]

        @pl.when(i == n_tiles - 1)
        def _():
            for cp in copies:
                cp.wait()

    tile = pl.BlockSpec((tm, d), lambda i: (i, 0))
    hbm = pl.BlockSpec(memory_space=pl.ANY)
    more = scatter.plumbing(first_operand=5, first_output=2)
    return pl.pallas_call(
        body, name="inproj_dx", grid=(n_tiles,),
        in_specs=[pl.BlockSpec((N_GROUPS, tm, D_MODEL), lambda i: (0, i, 0)), hbm, tile, tile,
                  pl.BlockSpec((1, d), lambda i: (0, 0))] + more[1],
        out_specs=[tile, pl.BlockSpec((1, d), lambda i: (0, 0))] + more[2],
        out_shape=[jax.ShapeDtypeStruct((tokens, d), F32), jax.ShapeDtypeStruct((1, d), F32)] + more[3],
        scratch_shapes=[pltpu.VMEM((d, N_GROUPS * D_MODEL), _MXU_DTYPE), pltpu.SemaphoreType.DMA((N_GROUPS,))] + more[4],
        input_output_aliases=more[5],
        compiler_params=_params(("arbitrary",)),
    )(dz, w_all, x2d, dx2, norm_g, *more[0])


def _row_tile(rows, cols, itemsize=4, budget=2 * 1024 * 1024):
    tr = rows
    while tr * cols * itemsize > budget and tr % 16 == 0:
        tr //= 2
    return tr


def _cast_into_slot(a, chip, dtype, name):
    rows, cols = a.shape
    tr = _row_tile(rows, cols)

    def body(chip_ref, a_ref, o_ref):
        del chip_ref
        o_ref[...] = a_ref[...].astype(dtype)

    grid_spec = pltpu.PrefetchScalarGridSpec(
        num_scalar_prefetch=1, grid=(rows // tr,),
        in_specs=[pl.BlockSpec((tr, cols), lambda i, chip_ref: (i, 0))],
        out_specs=pl.BlockSpec((None, tr, cols), lambda i, chip_ref: (chip_ref[0], i, 0)))
    return pl.pallas_call(body, name=name, grid_spec=grid_spec,
                          out_shape=jax.ShapeDtypeStruct((N_SHARDS, rows, cols), dtype),
                          compiler_params=_params(("arbitrary",)))(chip, a)


def _sum_slots(stack, name):
    n, rows, cols = stack.shape
    tr = _row_tile(rows, cols * n)

    def body(s_ref, o_ref):
        total = s_ref[0].astype(F32)
        for k in range(1, n):
            total = total + s_ref[k].astype(F32)
        o_ref[...] = total

    return pl.pallas_call(body, name=name, grid=(rows // tr,),
                          in_specs=[pl.BlockSpec((n, tr, cols), lambda i: (0, i, 0))],
                          out_specs=pl.BlockSpec((tr, cols), lambda i: (i, 0)),
                          out_shape=jax.ShapeDtypeStruct((rows, cols), F32),
                          compiler_params=_params(("parallel",)))(stack)


def _add_half(full, landed, place, name):
    n, rows, cols = full.shape
    half = rows // 2
    tr = _row_tile(half, cols)
    nb = half // tr

    def body(place_ref, a_ref, b_ref, o_ref, own_ref):
        total = (a_ref[...] + b_ref[...]).astype(_MXU_DTYPE)
        o_ref[...] = total

        @pl.when(pl.program_id(1) == place_ref[1])
        def _():
            own_ref[...] = total

    grid_spec = pltpu.PrefetchScalarGridSpec(
        num_scalar_prefetch=1, grid=(nb, n),
        in_specs=[pl.BlockSpec((None, tr, cols), lambda i, j, place_ref: (j, place_ref[0] * nb + i, 0)),
                  pl.BlockSpec((None, tr, cols), lambda i, j, place_ref: (j, i, 0))],
        out_specs=[pl.BlockSpec((None, tr, cols), lambda i, j, place_ref: (j, i, 0)),
                   pl.BlockSpec((None, tr, cols), lambda i, j, place_ref: (place_ref[1], i, 0))])
    shape = jax.ShapeDtypeStruct((n, half, cols), _MXU_DTYPE)
    return pl.pallas_call(body, name=name, grid_spec=grid_spec, out_shape=[shape, shape],
                          compiler_params=_params(("parallel", "arbitrary")))(place, full, landed)


def _adamw_update(w, grad, m, v):
    c1 = 1.0 - ADAM_B1 ** ADAM_STEP
    c2 = 1.0 - ADAM_B2 ** ADAM_STEP
    nm = ADAM_B1 * m + (1.0 - ADAM_B1) * grad
    nv = ADAM_B2 * v + (1.0 - ADAM_B2) * (grad * grad)
    return (-ADAM_LR) * ((nm / c1) / (jnp.sqrt(nv / c2) + ADAM_EPS) + ADAM_WD * w), nm, nv


def _adamw(w, g, m, v, name):
    rows, cols = w.shape
    tr = _row_tile(rows, cols, budget=1024 * 1024)

    def body(w_ref, g_ref, m_ref, v_ref, d_ref, nm_ref, nv_ref):
        d_ref[...], nm_ref[...], nv_ref[...] = _adamw_update(w_ref[...], g_ref[...], m_ref[...], v_ref[...])

    spec = pl.BlockSpec((tr, cols), lambda i: (i, 0))
    shape = jax.ShapeDtypeStruct((rows, cols), F32)
    return pl.pallas_call(body, name=name, grid=(rows // tr,), in_specs=[spec] * 4, out_specs=[spec] * 3,
                          out_shape=[shape] * 3, compiler_params=_params(("parallel",)))(w, g, m, v)


def _adamw_halves(w, g_mine, g_sibling, m, v, core, name):
    rows, cols = w.shape
    half = rows // 2
    tr = _row_tile(half, cols, budget=1024 * 1024)
    nb = half // tr

    def body(core_ref, w_ref, gm_ref, gs_ref, m_ref, v_ref, g_ref, d_ref, nm_ref, nv_ref):
        mine = pl.program_id(0) // nb == core_ref[0]
        grad = jnp.where(mine, gm_ref[...], gs_ref[...])
        g_ref[...] = grad
        d_ref[...], nm_ref[...], nv_ref[...] = _adamw_update(w_ref[...], grad, m_ref[...], v_ref[...])

    spec = pl.BlockSpec((tr, cols), lambda i, core_ref: (i, 0))
    mine_spec = pl.BlockSpec((tr, cols), lambda i, core_ref: (jnp.where(i // nb == core_ref[0], i % nb, 0), 0))
    sibling_spec = pl.BlockSpec((tr, cols), lambda i, core_ref: (jnp.where(i // nb == core_ref[0], 0, i % nb), 0))
    grid_spec = pltpu.PrefetchScalarGridSpec(num_scalar_prefetch=1, grid=(rows // tr,),
                                             in_specs=[spec, mine_spec, sibling_spec, spec, spec], out_specs=[spec] * 4)
    shape = jax.ShapeDtypeStruct((rows, cols), F32)
    return pl.pallas_call(body, name=name, grid_spec=grid_spec, out_shape=[shape] * 4,
                          compiler_params=_params(("parallel",)))(core, w, g_mine, g_sibling, m, v)


def _local_step(x, loss_target, gather, reduction, b_merge, conv_b, rg_wx, rg_bx, rg_wa, rg_ba, rg_lambda,
                hg_lb_logits, hg_norm_g, norm_g, final_norm_g):
    batch, seq, d = x.shape
    x2d = x.reshape(batch * seq, d)
    tgt2d = loss_target.reshape(batch * seq, d)
    z, h_t, (w_all, pa, pb, wo), cw_all = _inproj_fwd_gather(x2d, norm_g, *gather)
    pa, pb, wo = (t.reshape(d, d) for t in (pa, pb, wo))
    conv_w = jnp.transpose(cw_all, (1, 0, 2)).reshape(CONV_WIDTH, d)
    lru = (conv_w, conv_b, rg_wx, rg_bx, rg_wa, rg_ba, rg_lambda)
    ya, hl, kept = _branch_a_fwd(z, *lru, batch, seq)
    yb, states, kept_b = _branch_b_fwd(z, hg_lb_logits, hg_norm_g, batch, seq)
    dya, dyb, dx2, dz, loss, d_final_g, d_b_merge, d_pa, d_pb, d_wo = _merge_tail(
        ya, yb, z, x2d, tgt2d, b_merge, final_norm_g, pa, pb, wo)
    dz, d_lb_logits, d_hg_g = _branch_b_bwd(z, states, kept_b, dyb, dz, hg_lb_logits, hg_norm_g, batch, seq)
    dz, d_conv_w, d_conv_b, d_wx, d_bx, d_wa, d_ba, d_lam = _branch_a_bwd(
        z, hl, kept, dya, dz, conv_w, rg_wx, rg_wa, rg_lambda, batch, seq)
    small = dict(b_merge=d_b_merge, conv_w=d_conv_w, conv_b=d_conv_b, rg_wx=d_wx, rg_bx=d_bx, rg_wa=d_wa,
                 rg_ba=d_ba, rg_lambda=d_lam, hg_lb_logits=d_lb_logits, hg_norm_g=d_hg_g,
                 norm_g=jnp.zeros((1, d), F32), final_norm_g=d_final_g)
    first, second = reduction
    d_w_in, landed_w_in, *scattered_first = _inproj_dw_exchange(h_t, dz, first((d_pa, d_pb, d_wo), small))
    grad_x, d_norm_g, *scattered_second = _inproj_dx(dz, w_all, x2d, dx2, norm_g, scatter=second(d_w_in, landed_w_in))
    return loss[0, 0], grad_x.reshape(batch, seq, d), d_norm_g, (scattered_first, scattered_second)


_SMALL_ORDER = ("b_merge", "conv_w", "conv_b", "rg_wx", "rg_bx", "rg_wa", "rg_ba", "rg_lambda", "hg_lb_logits",
                "hg_norm_g", "norm_g", "final_norm_g")
N_DEV = 8
SIDE_STREAM = 1
PIECE_ROWS = 272


def _pack_small(tree):
    flat = jnp.concatenate([tree[k].reshape(-1) for k in _SMALL_ORDER])
    flat = jnp.pad(flat, (0, N_DEV * PIECE_ROWS * LANES - flat.shape[0]))
    return flat.reshape(N_DEV * PIECE_ROWS, LANES)


def _unpack_small(packed, like):
    flat = packed.reshape(-1)
    out, pos = {}, 0
    for k in _SMALL_ORDER:
        n = like[k].size
        out[k] = flat[pos:pos + n].reshape(like[k].shape)
        pos += n
    return out


def _mesh_position():
    x, y, c = lax.axis_index("x"), lax.axis_index("y"), lax.axis_index("c")
    other_chips = [(1 - x, y), (x, 1 - y), (1 - x, 1 - y)]
    return x, y, c, other_chips


def _other_devices(x, y, c):
    flips = [(fx, fy, fc) for fx in (0, 1) for fy in (0, 1) for fc in (0, 1) if (fx, fy, fc) != (0, 0, 0)]
    return [(jnp.where(fx, 1 - x, x), jnp.where(fy, 1 - y, y), jnp.where(fc, 1 - c, c)) for fx, fy, fc in flips]


def _remote(src, dst, send_sems, recv_sems, k, device):
    return pltpu.make_async_remote_copy(src_ref=src, dst_ref=dst, send_sem=send_sems.at[k], recv_sem=recv_sems.at[k],
                                        device_id=device, device_id_type=MESH)


def _exchange_halves(bigs, small):
    n_big = len(bigs)
    n_sem = n_big + N_DEV - 1

    def body(*refs):
        srcs, small_src = refs[:n_big], refs[n_big]
        outs, small_out = refs[n_big + 1:2 * n_big + 1], refs[2 * n_big + 1]
        send_sems, recv_sems, local_sem = refs[2 * n_big + 2:]
        x, y, c, _ = _mesh_position()
        me, sibling = 4 * x + 2 * y + c, (x, y, 1 - c)
        mine = pltpu.make_async_copy(small_src.at[pl.ds(me * PIECE_ROWS, PIECE_ROWS), :], small_out.at[me], local_sem)
        mine.start()
        copies = []
        for a in range(n_big):
            hs = srcs[a].shape[1] // 2
            copies.append(_remote(srcs[a].at[:, pl.ds((1 - c) * hs, hs), :], outs[a], send_sems, recv_sems, a, sibling))
        for k, (px, py, pc) in enumerate(_other_devices(x, y, c)):
            piece = small_src.at[pl.ds((4 * px + 2 * py + pc) * PIECE_ROWS, PIECE_ROWS), :]
            copies.append(_remote(piece, small_out.at[me], send_sems, recv_sems, n_big + k, (px, py, pc)))
        for cp in copies:
            cp.start()
        for cp in copies:
            cp.wait()
        mine.wait()

    hbm = pl.BlockSpec(memory_space=pl.ANY)
    out_shape = [jax.ShapeDtypeStruct((g.shape[0], g.shape[1] // 2, g.shape[2]), F32) for g in bigs]
    out_shape.append(jax.ShapeDtypeStruct((N_DEV, PIECE_ROWS, LANES), F32))
    return pl.pallas_call(
        body, name="exchange_halves",
        in_specs=[hbm] * (n_big + 1), out_specs=[hbm] * (n_big + 1), out_shape=out_shape,
        scratch_shapes=[pltpu.SemaphoreType.DMA((n_sem,)), pltpu.SemaphoreType.DMA((n_sem,)), pltpu.SemaphoreType.DMA],
    )(*bigs, small)


class _Scatter:
    def __init__(self, bigs, by_chip, small=None):
        self.bigs, self.by_chip, self.small = list(bigs), list(by_chip), small
        self.n_big = len(self.bigs)
        self.n_in = 2 * self.n_big + (small is not None)
        self.n_out = self.n_big + (small is not None)
        self.n_scratch = 2 + (small is not None)

    def plumbing(self, first_operand, first_output):
        hbm = pl.BlockSpec(memory_space=pl.ANY)
        n_sem = 3 * self.n_big + (N_DEV - 1 if self.small is not None else 0)
        operands = self.bigs + self.by_chip + ([self.small] if self.small is not None else [])
        out_shapes = [jax.ShapeDtypeStruct(g.shape, g.dtype) for g in self.by_chip]
        scratch = [pltpu.SemaphoreType.DMA((n_sem,)), pltpu.SemaphoreType.DMA((n_sem,))]
        if self.small is not None:
            out_shapes.append(jax.ShapeDtypeStruct((N_DEV, PIECE_ROWS, LANES), F32))
            scratch.append(pltpu.SemaphoreType.DMA)
        aliases = {first_operand + self.n_big + a: first_output + a for a in range(self.n_big)}
        return operands, [hbm] * self.n_in, [hbm] * self.n_out, out_shapes, scratch, aliases

    def copies(self, in_refs, out_refs, scratch_refs):
        srcs, outs = in_refs[:self.n_big], out_refs[:self.n_big]
        send_sems, recv_sems = scratch_refs[:2]
        x, y, c, chips = _mesh_position()
        chip, me = 2 * x + y, 4 * x + 2 * y + c
        copies = []
        for a in range(self.n_big):
            for j, (cx, cy) in enumerate(chips):
                copies.append(_remote(srcs[a].at[2 * cx + cy], outs[a].at[chip], send_sems, recv_sems, 3 * a + j,
                                      (cx, cy, c)))
        if self.small is not None:
            small_src, small_out = in_refs[2 * self.n_big], out_refs[self.n_big]
            copies.append(pltpu.make_async_copy(small_src, small_out.at[me], scratch_refs[2]))
            for k, peer in enumerate(_other_devices(x, y, c)):
                copies.append(_remote(small_src, small_out.at[me], send_sems, recv_sems, 3 * self.n_big + k, peer))
        return copies


def _swap_halves(halves, vec):
    n_big = len(halves)

    def body(*refs):
        srcs, vec_src = refs[:n_big], refs[n_big]
        outs, vec_out = refs[n_big + 1:2 * n_big + 1], refs[2 * n_big + 1]
        send_sems, recv_sems, local_sem = refs[2 * n_big + 2:]
        x, y, c, _ = _mesh_position()
        me = 4 * x + 2 * y + c
        copies = [pltpu.make_async_copy(vec_src, vec_out.at[me], local_sem)]
        copies += [_remote(srcs[a], outs[a], send_sems, recv_sems, a, (x, y, 1 - c)) for a in range(n_big)]
        copies += [_remote(vec_src, vec_out.at[me], send_sems, recv_sems, n_big + k, peer)
                   for k, peer in enumerate(_other_devices(x, y, c))]
        for cp in copies:
            cp.start()
        for cp in copies:
            cp.wait()

    hbm = pl.BlockSpec(memory_space=pl.ANY)
    n_sem = n_big + N_DEV - 1
    return pl.pallas_call(
        body, name="swap_halves",
        in_specs=[hbm] * (n_big + 1), out_specs=[hbm] * (n_big + 1),
        out_shape=[jax.ShapeDtypeStruct(h.shape, F32) for h in halves] + [jax.ShapeDtypeStruct((N_DEV,) + vec.shape, F32)],
        scratch_shapes=[pltpu.SemaphoreType.DMA((n_sem,)), pltpu.SemaphoreType.DMA((n_sem,)), pltpu.SemaphoreType.DMA],
    )(*halves, vec)


def kernel(x, w_in, b_merge, conv_w, conv_b, rg_wx, rg_bx, rg_wa, rg_ba, rg_lambda, hg_lb_logits, hg_norm_g, proj_a, proj_b, w_out, norm_g, final_norm_g, loss_target, m_w_in, m_b_merge, m_conv_w, m_conv_b, m_rg_wx, m_rg_bx, m_rg_wa, m_rg_ba, m_rg_lambda, m_hg_lb_logits, m_hg_norm_g, m_proj_a, m_proj_b, m_w_out, m_norm_g, m_final_norm_g, v_w_in, v_b_merge, v_conv_w, v_conv_b, v_rg_wx, v_rg_bx, v_rg_wa, v_rg_ba, v_rg_lambda, v_hg_lb_logits, v_hg_norm_g, v_proj_a, v_proj_b, v_w_out, v_norm_g, v_final_norm_g):
    d = D_MODEL
    weights = dict(w_in=w_in, b_merge=b_merge, conv_w=conv_w, conv_b=conv_b, rg_wx=rg_wx, rg_bx=rg_bx, rg_wa=rg_wa,
                   rg_ba=rg_ba, rg_lambda=rg_lambda, hg_lb_logits=hg_lb_logits, hg_norm_g=hg_norm_g, proj_a=proj_a,
                   proj_b=proj_b, w_out=w_out, norm_g=norm_g, final_norm_g=final_norm_g)
    m = dict(w_in=m_w_in, b_merge=m_b_merge, conv_w=m_conv_w, conv_b=m_conv_b, rg_wx=m_rg_wx, rg_bx=m_rg_bx,
             rg_wa=m_rg_wa, rg_ba=m_rg_ba, rg_lambda=m_rg_lambda, hg_lb_logits=m_hg_lb_logits, hg_norm_g=m_hg_norm_g,
             proj_a=m_proj_a, proj_b=m_proj_b, w_out=m_w_out, norm_g=m_norm_g, final_norm_g=m_final_norm_g)
    v = dict(w_in=v_w_in, b_merge=v_b_merge, conv_w=v_conv_w, conv_b=v_conv_b, rg_wx=v_rg_wx, rg_bx=v_rg_bx,
             rg_wa=v_rg_wa, rg_ba=v_rg_ba, rg_lambda=v_rg_lambda, hg_lb_logits=v_hg_lb_logits, hg_norm_g=v_hg_norm_g,
             proj_a=v_proj_a, proj_b=v_proj_b, w_out=v_w_out, norm_g=v_norm_g, final_norm_g=v_final_norm_g)
    big_names = ("w_in", "proj_a", "proj_b", "w_out")

    core = lax.axis_index("c").astype(jnp.int32).reshape(1)
    chip = (2 * lax.axis_index("x") + lax.axis_index("y")).astype(jnp.int32)

    slotted = [_cast_into_slot(weights[k][0], chip.reshape(1), _MXU_DTYPE, f"cast_{k}") for k in big_names]
    conv_slotted = _cast_into_slot(conv_w[0], chip.reshape(1), F32, "slot_conv_w")

    small_shapes = {}

    place = jnp.concatenate([core, chip.reshape(1)])

    def reduce_proj_and_small(proj_grads, small_grads):
        small_shapes.update({k: t.shape for k, t in small_grads.items()})
        bigs = [g.reshape(N_SHARDS, d // N_SHARDS, d) for g in proj_grads]
        *landed, small_landed = _exchange_halves(bigs, _pack_small(small_grads))
        sums = [_add_half(g, l, place, f"add_half_{1 + a}") for a, (g, l) in enumerate(zip(bigs, landed))]
        return _Scatter([s[0] for s in sums], [s[1] for s in sums], _sum_slots(small_landed, "sum_small"))

    def reduce_w_in(d_w_in, landed):
        partial, own_slot = _add_half(d_w_in, landed, place, "add_half_0")
        return _Scatter([partial], [own_slot])

    loss_part, grad_x, d_norm_g, ((*by_chip_proj, small_all), by_chip_w_in) = _local_step(
        x, loss_target, (slotted, conv_slotted, chip.reshape(1)), (reduce_proj_and_small, reduce_w_in),
        b_merge, conv_b, rg_wx[0], rg_bx.reshape(1, d), rg_wa[0], rg_ba.reshape(1, d), rg_lambda, hg_lb_logits,
        hg_norm_g, norm_g, final_norm_g.reshape(1, d))
    mine = [_sum_slots(s, f"sum_chips_{a}") for a, s in enumerate(by_chip_w_in + by_chip_proj)]
    late = jnp.concatenate([d_norm_g.reshape(SUBLANES, LANES), jnp.full((SUBLANES, LANES), loss_part, F32)])
    *theirs, late_parts = _swap_halves(mine, late)
    late_sum = _sum_slots(late_parts, "sum_late")
    loss = late_sum[SUBLANES, 0]
    small_red = _unpack_small(small_all, {k: jax.ShapeDtypeStruct(s, F32) for k, s in small_shapes.items()})
    small_red["norm_g"] = late_sum[:SUBLANES].reshape(1, d)

    grads, delta, new_m, new_v = {}, {}, {}, {}
    for k, g_mine, g_theirs in zip(big_names, mine, theirs):
        out = _adamw_halves(weights[k][0], g_mine, g_theirs, m[k][0], v[k][0], core, f"adamw_{k}")
        grads[k], delta[k], new_m[k], new_v[k] = (t.reshape(weights[k].shape) for t in out)
    cols = d // N_SHARDS
    g_conv = lax.dynamic_slice(small_red["conv_w"], (0, chip * cols), (CONV_WIDTH, cols))
    grads["conv_w"] = g_conv.reshape(conv_w.shape)
    dl, nm, nv = _adamw(conv_w[0], g_conv, m_conv_w[0], v_conv_w[0], "adamw_conv_w")
    delta["conv_w"], new_m["conv_w"], new_v["conv_w"] = (t.reshape(conv_w.shape) for t in (dl, nm, nv))
    rest = [k for k in _SMALL_ORDER if k != "conv_w"]
    like = {k: (weights[k] if k != "conv_w" else jnp.zeros((CONV_WIDTH, d), F32)) for k in _SMALL_ORDER}
    packs = [_pack_small({k: (t[k] if k != "conv_w" else like[k]) for k in _SMALL_ORDER}) for t in (weights, m, v)]
    g_pack = _pack_small({k: small_red[k].reshape(like[k].shape) for k in _SMALL_ORDER})
    outs = [_unpack_small(p, like) for p in _adamw(packs[0], g_pack, packs[1], packs[2], "adamw_small")]
    for k in rest:
        grads[k] = small_red[k].reshape(weights[k].shape)
        delta[k], new_m[k], new_v[k] = outs[0][k], outs[1][k], outs[2][k]

    order = ("w_in", "b_merge", "conv_w", "conv_b", "rg_wx", "rg_bx", "rg_wa", "rg_ba", "rg_lambda", "hg_lb_logits",
             "hg_norm_g", "proj_a", "proj_b", "w_out", "norm_g", "final_norm_g")
    return (loss, grad_x, *[grads[k] for k in order], *[delta[k] for k in order], *[new_m[k] for k in order],
            *[new_v[k] for k in order])
```

```python
import functools

import jax
import jax.numpy as jnp
from jax import lax
from jax.experimental import pallas as pl
from jax.experimental.pallas import tpu as pltpu

F32 = jnp.float32
_MXU_DTYPE = jnp.bfloat16

D_MODEL = 1024
LANES = 128
SUBLANES = 8
N_BLK = D_MODEL // LANES
N_GROUPS = 8
N_SHARDS = 4
CONV_WIDTH = 4
LRU_C = 8.0
CHUNK = 64
CHUNKS_IN_FLIGHT = 16
FWD_CHUNKS_IN_FLIGHT = 32
SCAN_UNROLL = 4
HG_SCALE = float(LANES) ** -0.5
EPS = 1e-6
ADAM_LR, ADAM_B1, ADAM_B2, ADAM_EPS, ADAM_WD, ADAM_STEP = 0.001, 0.9, 0.999, 1e-08, 0.01, 10
MATMUL_TOKENS = 1024
TAIL_TOKENS = 256
CONTRACT_TOKENS = 2048
VMEM_LIMIT = 56 * 1024 * 1024
VMEM_LIMIT_BIG = 60 * 1024 * 1024
MESH = pl.DeviceIdType.MESH

_SLOT_TO_GROUP = (2, 3, 4, 5, 0, 1, 6, 7)


def _mm(a, b):
    return lax.dot_general(a.astype(_MXU_DTYPE), b.astype(_MXU_DTYPE), (((1,), (0,)), ((), ())),
                           preferred_element_type=F32)


def _mm_nt(a, b):
    return lax.dot_general(a.astype(_MXU_DTYPE), b.astype(_MXU_DTYPE), (((1,), (1,)), ((), ())),
                           preferred_element_type=F32)


def _mm_tn(a, b):
    return lax.dot_general(a.astype(_MXU_DTYPE), b.astype(_MXU_DTYPE), (((0,), (0,)), ((), ())),
                           preferred_element_type=F32)


def _sigmoid(x):
    return 0.5 * jnp.tanh(0.5 * x) + 0.5


def _log1p_pos(y):
    series = y * (1.0 - y * (0.5 - y * (1.0 / 3.0 - y * 0.25)))
    return jnp.where(y < 0.01, series, jnp.log(1.0 + y))


def _softplus(x):
    return jnp.maximum(x, 0.0) + _log1p_pos(jnp.exp(-jnp.abs(x)))


def _shift_down(x, n):
    rolled = pltpu.roll(x, n, 0)
    edge = SUBLANES if (n < SUBLANES and x.shape[0] > SUBLANES) else x.shape[0]
    rows = lax.broadcasted_iota(jnp.int32, (edge, x.shape[1]), 0)
    head = jnp.where(rows >= n, rolled[:edge], 0.0)
    return head if edge == x.shape[0] else jnp.concatenate([head, rolled[edge:]], axis=0)


def _shift_up(x, n):
    size = x.shape[0]
    rolled = pltpu.roll(x, size - n, 0)
    edge = SUBLANES if (n < SUBLANES and size > SUBLANES) else size
    rows = lax.broadcasted_iota(jnp.int32, (edge, x.shape[1]), 0)
    tail = jnp.where(rows < edge - n, rolled[size - edge:], 0.0)
    return tail if edge == size else jnp.concatenate([rolled[:size - edge], tail], axis=0)


def _params(dims, vmem=VMEM_LIMIT):
    return pltpu.CompilerParams(dimension_semantics=dims, vmem_limit_bytes=vmem)


def _slot_of_group(g):
    return jnp.where(g < 2, g + 4, jnp.where(g < 6, g - 2, g))


def _inproj_fwd_gather(x2d, norm_g, slotted, conv_slotted, chip):
    tokens, d = x2d.shape
    tm = min(MATMUL_TOKENS, tokens)
    n_tiles = tokens // tm
    n_big = len(slotted)
    n_sem = 6 * (n_big + 1) + 3
    last_pass = N_GROUPS - 1

    def shard_of(k, chip_id):
        x, y = chip_id // 2, chip_id % 2
        return 2 * jnp.where(k % 2 == 1, 1 - x, x) + jnp.where(k // 2 == 1, 1 - y, y)

    def body(chip_ref, x_ref, g_ref, *rest):
        bufs, cw = rest[n_big + 1:2 * n_big + 1], rest[2 * n_big + 1]
        z_ref, ht_ref = rest[2 * n_big + 2:2 * n_big + 4]
        h_all, slab, send_sems, recv_sems, slab_sems = rest[2 * n_big + 4:]
        del chip_ref
        p, i = pl.program_id(0), pl.program_id(1)
        x, y, c, chips = _mesh_position()
        me, sibling = 2 * x + y, (x, y, 1 - c)

        pieces = [(0, 0), (0, 1)] + [(a, None) for a in range(1, n_big)]

        def half(piece, slot, which):
            a, q = pieces[piece]
            hs = bufs[a].shape[1] // 2
            cols = slice(None) if q is None else pl.ds(q * D_MODEL, D_MODEL)
            return bufs[a].at[slot, pl.ds(which * hs, hs), cols]

        def send(piece, j):
            mine = half(piece, me, c)
            return _remote(mine, mine, send_sems, recv_sems, 6 * piece + j, (chips[j][0], chips[j][1], c))

        def arrival(piece, j):
            landed = half(piece, 2 * chips[j][0] + chips[j][1], c)
            return _remote(landed, landed, send_sems, recv_sems, 6 * piece + j, (chips[j][0], chips[j][1], c))

        def passed_on(piece, j, which):
            landed = half(piece, 2 * chips[j][0] + chips[j][1], which)
            return _remote(landed, landed, send_sems, recv_sems, 6 * piece + 3 + j, sibling)

        def conv_copy(j, slot):
            return _remote(cw.at[slot], cw.at[slot], send_sems, recv_sems, 6 * len(pieces) + j,
                           (chips[j][0], chips[j][1], c))

        def land(piece, j):
            arrival(piece, j).wait_recv()
            passed_on(piece, j, c).start()
            passed_on(piece, j, 1 - c).wait_recv()

        def slab_copy(pv):
            src = bufs[0].at[shard_of(pv // 2, me), :, pl.ds((pv % 2) * D_MODEL, D_MODEL)]
            return pltpu.make_async_copy(src, slab.at[pv % 2], slab_sems.at[pv % 2])

        @pl.when((p == 0) & (i == 0))
        def _():
            for q in range(2):
                send(q, 0).start()
                send(q, 1).start()
            slab_copy(0).start()

        @pl.when(i == 0)
        def _():
            for pv in range(N_GROUPS):
                @pl.when(p == pv)
                def _(pv=pv):
                    slab_copy(pv).wait()

        rows = pl.ds(pl.multiple_of(i * tm, tm), tm)

        @pl.when(p == 0)
        def _():
            xt = x_ref[...]
            r = lax.rsqrt(jnp.mean(xt * xt, axis=-1, keepdims=True) + EPS)
            h = (xt * r) * g_ref[...]
            h_all[rows, :] = h.astype(_MXU_DTYPE)
            ht_ref[...] = jnp.transpose(h).astype(_MXU_DTYPE)

        def relay(q):
            landed = half(q, 2 * chips[q][0] + chips[q][1], c)
            to = chips[1 - q]
            return _remote(landed, landed, send_sems, recv_sems, 6 * q + 2, (to[0], to[1], c))

        landings = {1: [(0, 0)], 2: [(1, 0), (1, 1)], 3: [(0, 1)], 5: [(0, 2)], 6: [(1, 2)]}

        def end_of_pass(pv):
            for q, j in landings.get(pv - 1, []):
                land(q, j)
                if j == q:
                    relay(q).start()
            if pv - 1 == 1:
                for piece in range(2, len(pieces)):
                    for jj in range(3):
                        send(piece, jj).start()
                for jj in range(3):
                    conv_copy(jj, me).start()
            if pv - 1 in (5, 6):
                for piece in range(2, len(pieces)):
                    for jj in ((0, 1) if pv - 1 == 5 else (2,)):
                        land(piece, jj)
            slab_copy(pv).start()

        @pl.when(i == n_tiles - 1)
        def _():
            for pv in range(1, N_GROUPS):
                pl.when(p == pv - 1)(functools.partial(end_of_pass, pv))

        z_ref[...] = _mm(h_all[rows, :], slab[p % 2])

        @pl.when((p == last_pass) & (i == n_tiles - 1))
        def _():
            for j in range(3):
                conv_copy(j, 2 * chips[j][0] + chips[j][1]).wait_recv()
            for piece in range(len(pieces)):
                for j in range(3):
                    (relay(piece) if (piece < 2 and j == 2) else send(piece, j)).wait_send()
                    passed_on(piece, j, c).wait_send()
            for j in range(3):
                conv_copy(j, me).wait_send()

    def z_index(p, i, chip_ref):
        g = 2 * shard_of(p // 2, chip_ref[0]) + p % 2
        return (_slot_of_group(g), i, 0)

    def first_pass_tile(p, i, chip_ref):
        return jnp.where(p == 0, i, n_tiles - 1)

    hbm = pl.BlockSpec(memory_space=pl.ANY)
    operands = list(slotted) + [conv_slotted]
    grid_spec = pltpu.PrefetchScalarGridSpec(
        num_scalar_prefetch=1, grid=(N_GROUPS, n_tiles),
        in_specs=[pl.BlockSpec((tm, d), lambda p, i, chip_ref: (first_pass_tile(p, i, chip_ref), 0)),
                  pl.BlockSpec((1, d), lambda p, i, chip_ref: (0, 0))] + [hbm] * (n_big + 1),
        out_specs=[hbm] * (n_big + 1) + [pl.BlockSpec((None, tm, D_MODEL), z_index),
                                         pl.BlockSpec((d, tm), lambda p, i, chip_ref: (0, first_pass_tile(p, i, chip_ref)))],
        scratch_shapes=[pltpu.VMEM((tokens, d), _MXU_DTYPE), pltpu.VMEM((2, d, D_MODEL), _MXU_DTYPE),
                        pltpu.SemaphoreType.DMA((n_sem,)), pltpu.SemaphoreType.DMA((n_sem,)),
                        pltpu.SemaphoreType.DMA((2,))])
    out = pl.pallas_call(
        body, name="inproj_fwd_gather", grid_spec=grid_spec,
        out_shape=[jax.ShapeDtypeStruct(a.shape, a.dtype) for a in operands]
        + [jax.ShapeDtypeStruct((N_GROUPS, tokens, D_MODEL), F32), jax.ShapeDtypeStruct((d, tokens), _MXU_DTYPE)],
        input_output_aliases={3 + a: a for a in range(n_big + 1)},
        compiler_params=_params(("arbitrary", "arbitrary")),
    )(chip, x2d, norm_g, *operands)
    return out[n_big + 1], out[n_big + 2], out[:n_big], out[n_big]


def _lane_blocks(x):
    return [x[:, k * LANES:(k + 1) * LANES] for k in range(x.shape[1] // LANES)]


def _block_diag(x, w_ref, transposed=False):
    mm = _mm_nt if transposed else _mm
    return jnp.concatenate([mm(xk, w_ref[k]) for k, xk in enumerate(_lane_blocks(x))], axis=1)


def _lru_decay(gr, sp):
    log_a = (-LRU_C) * gr * sp
    a = jnp.exp(log_a)
    y = 2.0 * log_a
    mult_sq = jnp.where(y > -1e-3, -y * (1.0 + 0.5 * y), 1.0 - a * a)
    inv_mult = lax.rsqrt(jnp.maximum(mult_sq, 1e-37))
    return a, mult_sq * inv_mult, inv_mult


def _tile_rows(width):
    return lax.broadcasted_iota(jnp.int32, (SUBLANES, width), 0)


def _scan_forward(a_scr, u_scr, h_scr, seq):
    width = a_scr.shape[1]
    rows = _tile_rows(width)

    group = min(SCAN_UNROLL, seq // SUBLANES)

    def within_tile(j):
        sl = pl.ds(pl.multiple_of(j * SUBLANES, SUBLANES), SUBLANES)
        a = a_scr[sl, :]
        u = u_scr[sl, :]
        for d in (1, 2, 4):
            keep = rows >= d
            a_sh = jnp.where(keep, pltpu.roll(a, d, 0), 1.0)
            u_sh = jnp.where(keep, pltpu.roll(u, d, 0), 0.0)
            u = a * u_sh + u
            a = a * a_sh
        return sl, a, u

    def tiles(t, carry):
        parts = [within_tile(t * group + k) for k in range(group)]
        out = []
        for sl, a, u in parts:
            h = u + a * carry
            out.append((sl, h))
            carry = jnp.broadcast_to(h[SUBLANES - 1:SUBLANES, :], (SUBLANES, width))
        for sl, h in out:
            h_scr[sl, :] = h
        return carry

    lax.fori_loop(0, seq // SUBLANES // group, tiles, jnp.zeros((SUBLANES, width), F32))


def _scan_backward(c_scr, d_scr, g_scr, seq):
    width = c_scr.shape[1]
    rows = _tile_rows(width)
    n_tiles = seq // SUBLANES

    group = min(SCAN_UNROLL, n_tiles)

    def within_tile(j):
        sl = pl.ds(pl.multiple_of(j * SUBLANES, SUBLANES), SUBLANES)
        c = c_scr[sl, :]
        g = d_scr[sl, :]
        for d in (1, 2, 4):
            keep = rows < SUBLANES - d
            c_sh = jnp.where(keep, pltpu.roll(c, SUBLANES - d, 0), 1.0)
            g_sh = jnp.where(keep, pltpu.roll(g, SUBLANES - d, 0), 0.0)
            g = c * g_sh + g
            c = c * c_sh
        return sl, c, g

    def tiles(t, carry):
        parts = [within_tile(n_tiles - 1 - (t * group + k)) for k in range(group)]
        out = []
        for sl, c, g in parts:
            g = g + c * carry
            out.append((sl, g))
            carry = jnp.broadcast_to(g[0:1, :], (SUBLANES, width))
        for sl, g in out:
            g_scr[sl, :] = g
        return carry

    lax.fori_loop(0, n_tiles // group, tiles, jnp.zeros((SUBLANES, width), F32))


LRU_BLOCKS_PER_STEP = 2
LRU_LANES = LRU_BLOCKS_PER_STEP * LANES
LRU_STEPS = N_BLK // LRU_BLOCKS_PER_STEP


def _branch_a_fwd(z, conv_w, conv_b, wx, bx, wa, ba, lam, batch, seq):
    tokens = batch * seq

    def body(z_ref, cw_ref, cb_ref, wx_ref, bx_ref, wa_ref, ba_ref, lam_ref, ya_ref, hl_ref, kept_ref, a_scr, u_scr):
        xa = z_ref[0]
        ga = z_ref[1]
        xc = (cb_ref[...] + cw_ref[3:4, :] * xa + cw_ref[2:3, :] * _shift_down(xa, 1)
              + cw_ref[1:2, :] * _shift_down(xa, 2) + cw_ref[0:1, :] * _shift_down(xa, 3))
        gi = _sigmoid(_block_diag(xc, wx_ref) + bx_ref[...])
        gr = _sigmoid(_block_diag(xc, wa_ref) + ba_ref[...])
        a, mult, _ = _lru_decay(gr, _softplus(-lam_ref[...]))
        kept_ref[0], kept_ref[1], kept_ref[2] = xc, gi, gr
        a_scr[...] = a
        u_scr[...] = mult * gi * xc
        _scan_forward(a_scr, u_scr, hl_ref, seq)
        ya_ref[...] = (hl_ref[...] * (ga * _sigmoid(ga))).astype(_MXU_DTYPE)

    blk = pl.BlockSpec((seq, LRU_LANES), lambda b, c: (b, c))
    vec = pl.BlockSpec((1, LRU_LANES), lambda b, c: (0, c))
    mat = pl.BlockSpec((LRU_BLOCKS_PER_STEP, LANES, LANES), lambda b, c: (c, 0, 0))
    return pl.pallas_call(
        body, name="branch_a_fwd",
        grid=(batch, LRU_STEPS),
        in_specs=[pl.BlockSpec((2, seq, LRU_LANES), lambda b, c: (2, b, c)),
                  pl.BlockSpec((CONV_WIDTH, LRU_LANES), lambda b, c: (0, c)), vec, mat, vec, mat, vec, vec],
        out_specs=[blk, blk, pl.BlockSpec((3, seq, LRU_LANES), lambda b, c: (0, b, c))],
        out_shape=[jax.ShapeDtypeStruct((tokens, D_MODEL), _MXU_DTYPE), jax.ShapeDtypeStruct((tokens, D_MODEL), F32),
                   jax.ShapeDtypeStruct((3, tokens, D_MODEL), F32)],
        scratch_shapes=[pltpu.VMEM((seq, LRU_LANES), F32), pltpu.VMEM((seq, LRU_LANES), F32)],
        compiler_params=_params(("parallel", "parallel")),
    )(z, conv_w, conv_b, wx, bx, wa, ba, lam)


def _branch_a_bwd(z, hl, kept, dya, dz, conv_w, wx, wa, lam, batch, seq):
    def body(z_ref, hl_ref, kept_ref, dya_ref, dz_in_ref, cw_ref, wx_ref, wa_ref, lam_ref,
             dz_ref, dcw_ref, dcb_ref, dwx_ref, dbx_ref, dwa_ref, dba_ref, dlam_ref, c_scr, d_scr):
        del dz_in_ref
        g_scr = d_scr
        xa = z_ref[0]
        ga = z_ref[1]
        hl = hl_ref[...]
        dya = dya_ref[...]
        xc, gi, gr = kept_ref[0], kept_ref[1], kept_ref[2]
        sp = _softplus(-lam_ref[...])
        a, mult, inv_mult = _lru_decay(gr, sp)
        sga = _sigmoid(ga)
        dz_ref[1] = (dya * hl * (sga * (1.0 + ga * (1.0 - sga)))).astype(_MXU_DTYPE)
        c_scr[...] = _shift_up(a, 1)
        d_scr[...] = dya * (ga * sga)
        _scan_backward(c_scr, d_scr, g_scr, seq)
        g = g_scr[...]
        da = g * _shift_down(hl, 1)
        dmult = g * gi * xc
        dgi = g * mult * xc
        dxc = g * mult * gi
        dlog_a = da * a - dmult * (a * a) * inv_mult
        dgr = dlog_a * (-LRU_C) * sp
        dsp = jnp.sum(dlog_a * gr, axis=0, keepdims=True) * (-LRU_C)
        dlam = -dsp * _sigmoid(-lam_ref[...])
        dpi = dgi * gi * (1.0 - gi)
        dpr = dgr * gr * (1.0 - gr)
        dxc = dxc + _block_diag(dpi, wx_ref, transposed=True) + _block_diag(dpr, wa_ref, transposed=True)
        dwx = jnp.stack([_mm_tn(xk, dk) for xk, dk in zip(_lane_blocks(xc), _lane_blocks(dpi))])
        dwa = jnp.stack([_mm_tn(xk, dk) for xk, dk in zip(_lane_blocks(xc), _lane_blocks(dpr))])
        dbx = jnp.sum(dpi, axis=0, keepdims=True)
        dba = jnp.sum(dpr, axis=0, keepdims=True)
        ahead = [dxc if k == CONV_WIDTH - 1 else _shift_up(dxc, CONV_WIDTH - 1 - k) for k in range(CONV_WIDTH)]
        dxa = sum(cw_ref[k:k + 1, :] * ahead[k] for k in range(CONV_WIDTH))
        dz_ref[0] = dxa.astype(_MXU_DTYPE)
        dcb = jnp.sum(dxc, axis=0, keepdims=True)
        dcw = [jnp.sum(ahead[k] * xa, axis=0, keepdims=True) for k in range(CONV_WIDTH)]

        @pl.when(pl.program_id(1) == 0)
        def _():
            for k in range(CONV_WIDTH):
                dcw_ref[k:k + 1, :] = dcw[k]
            dcb_ref[...] = dcb
            dwx_ref[...] = dwx
            dbx_ref[...] = dbx
            dwa_ref[...] = dwa
            dba_ref[...] = dba
            dlam_ref[...] = dlam

        @pl.when(pl.program_id(1) != 0)
        def _():
            for k in range(CONV_WIDTH):
                dcw_ref[k:k + 1, :] += dcw[k]
            dcb_ref[...] += dcb
            dwx_ref[...] += dwx
            dbx_ref[...] += dbx
            dwa_ref[...] += dwa
            dba_ref[...] += dba
            dlam_ref[...] += dlam

    tokens = batch * seq
    blk = pl.BlockSpec((seq, LRU_LANES), lambda c, b: (b, c))
    vec = pl.BlockSpec((1, LRU_LANES), lambda c, b: (0, c))
    mat = pl.BlockSpec((LRU_BLOCKS_PER_STEP, LANES, LANES), lambda c, b: (c, 0, 0))
    vec_shape = jax.ShapeDtypeStruct((1, D_MODEL), F32)
    mat_shape = jax.ShapeDtypeStruct((N_BLK, LANES, LANES), F32)
    return pl.pallas_call(
        body, name="branch_a_bwd",
        grid=(LRU_STEPS, batch),
        in_specs=[pl.BlockSpec((2, seq, LRU_LANES), lambda c, b: (2, b, c)), blk,
                  pl.BlockSpec((3, seq, LRU_LANES), lambda c, b: (0, b, c)), blk,
                  pl.BlockSpec(memory_space=pl.ANY),
                  pl.BlockSpec((CONV_WIDTH, LRU_LANES), lambda c, b: (0, c)), mat, mat, vec],
        out_specs=[pl.BlockSpec((2, seq, LRU_LANES), lambda c, b: (2, b, c)),
                   pl.BlockSpec((CONV_WIDTH, LRU_LANES), lambda c, b: (0, c)), vec, mat, vec, mat, vec, vec],
        out_shape=[jax.ShapeDtypeStruct((N_GROUPS, tokens, D_MODEL), _MXU_DTYPE),
                   jax.ShapeDtypeStruct((CONV_WIDTH, D_MODEL), F32), vec_shape, mat_shape, vec_shape, mat_shape,
                   vec_shape, vec_shape],
        scratch_shapes=[pltpu.VMEM((seq, LRU_LANES), F32)] * 2,
        input_output_aliases={4: 0},
        compiler_params=_params(("parallel", "arbitrary"), vmem=VMEM_LIMIT_BIG),
    )(z, hl, kept, dya, dz, conv_w, wx, wa, lam)


def _chunk_masks(transposed=False):
    r = lax.broadcasted_iota(jnp.int32, (CHUNK, CHUNK), 0)
    c = lax.broadcasted_iota(jnp.int32, (CHUNK, CHUNK), 1)
    return r <= c if transposed else r >= c


def _row_blocks(seq, fn):
    block = min(256, seq)

    def trip(i, carry):
        fn(pl.ds(pl.multiple_of(i * block, block), block))
        return carry

    lax.fori_loop(0, seq // block, trip, 0)


def _hgrn_prepare(z_ref, lb_ref, f_scr, logf_scr, qh_scr, seq):
    lb = _sigmoid(lb_ref[0:1, :] - lb_ref[1:2, :])

    def block(rows):
        q = z_ref[0, rows, :]
        f = lb + (1.0 - lb) * _sigmoid(z_ref[1, rows, :])
        f_scr[rows, :] = f
        logf_scr[rows, :] = jnp.log(f)
        qh_scr[rows, :] = q * _sigmoid(q)

    _row_blocks(seq, block)
    return lb


def _cumsum_rows(x, reverse=False):
    shift = _shift_up if reverse else _shift_down
    d = 1
    while d < x.shape[0]:
        x = x + shift(x, d)
        d *= 2
    return x


def _lane_mean(x):
    return jnp.mean(x, axis=-1, keepdims=True)


def _token_contractions(lhs_scr, rhs_scr, out_ref, seq):
    rows_id = lax.broadcasted_iota(jnp.int32, (LANES, LANES), 0)

    def transposed(p):
        rows = pl.ds(pl.multiple_of(p * LANES, LANES), LANES)
        return jnp.transpose(lhs_scr[rows, :]).astype(_MXU_DTYPE), rhs_scr[rows, :]

    def contract(p, s):
        lhs_t, rhs = s
        return (_mm(lhs_t, jnp.where(rows_id < CHUNK, rhs, 0.0)), _mm(lhs_t, jnp.where(rows_id >= CHUNK, rhs, 0.0)))

    def store(p, out):
        out_ref[2 * p] = out[0]
        out_ref[2 * p + 1] = out[1]

    _independent_trips(seq // LANES, [transposed, contract], store)


def _chunk_rows(c):
    return pl.ds(pl.multiple_of(c * CHUNK, CHUNK), CHUNK)


def _chunk_terms(c, z_ref, f_scr, qh_scr, b_scr):
    rows = _chunk_rows(c)
    b = b_scr[rows, :]
    b_mid = b_scr[pl.ds(c * CHUNK + CHUNK // 2, 1), :]
    b_last = b_scr[pl.ds(c * CHUNK + CHUNK - 1, 1), :]
    qh = qh_scr[rows, :]
    k = 1.0 - f_scr[rows, :]
    v = z_ref[2, rows, :]
    e_q = jnp.exp(b - b_mid) * HG_SCALE
    e_k = jnp.exp(b_mid - b)
    e_qi = jnp.exp(b) * HG_SCALE
    e_ks = jnp.exp(b_last - b)
    decay = jnp.exp(b_last)
    return rows, qh, k, v, e_q, e_k, e_qi, e_ks, decay


def _independent_trips(n, stages, store, group=CHUNKS_IN_FLIGHT):
    stages = stages if isinstance(stages, (list, tuple)) else [stages]
    group = min(group, n)

    def trip(g, carry):
        ids = [g * group + i for i in range(group)]
        state = [stages[0](c) for c in ids]
        for stage in stages[1:]:
            state = [stage(c, s) for c, s in zip(ids, state)]
        for c, s in zip(ids, state):
            store(c, s)
        return carry

    lax.fori_loop(0, n // group, trip, 0)


def _branch_b_fwd(z, lb_logits, hg_g, batch, seq):
    tokens = batch * seq
    n_chunks = seq // CHUNK

    def body(z_ref, lb_ref, g_ref, yb_ref, st_ref, kept_ref, logf_scr, o_scr, qi_scr, ks_scr, dec_scr):
        f_scr, qh_scr, b_scr, o_kept = (kept_ref.at[k] for k in range(4))
        _hgrn_prepare(z_ref, lb_ref, f_scr, logf_scr, qh_scr, seq)
        causal = _chunk_masks()
        gain = g_ref[...]

        def cumulate(c):
            return _cumsum_rows(logf_scr[_chunk_rows(c), :])

        def store_cumulated(c, b):
            b_scr[_chunk_rows(c), :] = b

        def scores(c):
            _, qh, k, v, e_q, e_k, e_qi, e_ks, decay = _chunk_terms(c, z_ref, f_scr, qh_scr, b_scr)
            return _mm_nt(qh * e_q, k * e_k), v, qh * e_qi, k * e_ks, decay

        def within_chunk(c, s):
            att, v, q_int, k_st, decay = s
            return _mm(jnp.where(causal, att, 0.0), v), q_int, k_st, decay

        def store_within_chunk(c, out):
            rows = _chunk_rows(c)
            o_scr[rows, :], qi_scr[rows, :], ks_scr[rows, :], dec_scr[pl.ds(c, 1), :] = out

        def carry_state(c, state_t):
            update = st_ref[c]
            st_ref[c] = state_t
            return state_t * dec_scr[pl.ds(c, 1), :] + update

        def finish(c):
            rows = _chunk_rows(c)
            o = o_scr[rows, :] + _mm_nt(qi_scr[rows, :], st_ref[c])
            r = lax.rsqrt(_lane_mean(o * o) + EPS)
            gb = z_ref[3, rows, :]
            return (((o * r) * gain) * (gb * _sigmoid(gb))).astype(_MXU_DTYPE), o

        def store_finished(c, out):
            yb_ref[_chunk_rows(c), :], o_kept[_chunk_rows(c), :] = out

        _independent_trips(n_chunks, cumulate, store_cumulated, FWD_CHUNKS_IN_FLIGHT)
        _independent_trips(n_chunks, [scores, within_chunk], store_within_chunk, FWD_CHUNKS_IN_FLIGHT)
        _token_contractions(z_ref.at[2], ks_scr, st_ref, seq)
        lax.fori_loop(0, n_chunks, carry_state, jnp.zeros((LANES, LANES), F32))
        _independent_trips(n_chunks, finish, store_finished, FWD_CHUNKS_IN_FLIGHT)

    seq_buf = pltpu.VMEM((seq, LANES), F32)
    return pl.pallas_call(
        body, name="branch_b_fwd",
        grid=(batch, N_BLK),
        in_specs=[pl.BlockSpec((4, seq, LANES), lambda b, h: (0, b, h)),
                  pl.BlockSpec((2, LANES), lambda b, h: (0, h)),
                  pl.BlockSpec((1, LANES), lambda b, h: (0, 0))],
        out_specs=[pl.BlockSpec((seq, LANES), lambda b, h: (b, h)),
                   pl.BlockSpec((None, n_chunks, LANES, LANES), lambda b, h: (b * N_BLK + h, 0, 0, 0)),
                   pl.BlockSpec((4, seq, LANES), lambda b, h: (0, b, h))],
        out_shape=[jax.ShapeDtypeStruct((tokens, D_MODEL), _MXU_DTYPE),
                   jax.ShapeDtypeStruct((batch * N_BLK, n_chunks, LANES, LANES), F32),
                   jax.ShapeDtypeStruct((4, tokens, D_MODEL), F32)],
        scratch_shapes=[seq_buf] * 4 + [pltpu.VMEM((n_chunks, LANES), F32)],
        compiler_params=_params(("parallel", "parallel")),
    )(z, lb_logits, hg_g)


def _branch_b_bwd(z, states, kept, dyb, dz, lb_logits, hg_g, batch, seq):
    n_chunks = seq // CHUNK

    def body(z_ref, st_ref, kept_ref, dyb_ref, dz_in_ref, lb_ref, g_ref, dz_ref, dlog_ref, dg_ref,
             do_scr, qi_scr, dqh_scr, df_scr, dec_scr, dgp_scr, dlb_scr, dst_scr):
        del dz_in_ref
        f_scr, qh_scr, b_scr, o_kept = (kept_ref.at[k] for k in range(4))
        first = (pl.program_id(0) == 0) & (pl.program_id(1) == 0)
        lb = _sigmoid(lb_ref[0:1, :] - lb_ref[1:2, :])
        causal = _chunk_masks()
        anti_causal = _chunk_masks(transposed=True)
        gain = g_ref[...]

        @pl.when(first)
        def _():
            dg_ref[...] = jnp.zeros_like(dg_ref)

        @pl.when(pl.program_id(1) == 0)
        def _():
            dlb_scr[...] = jnp.zeros_like(dlb_scr)

        def output_gradient(c):
            rows = _chunk_rows(c)
            b = b_scr[rows, :]
            q_int = qh_scr[rows, :] * (jnp.exp(b) * HG_SCALE)
            decay = jnp.exp(b_scr[pl.ds(c * CHUNK + CHUNK - 1, 1), :])
            o = o_kept[rows, :]
            r = lax.rsqrt(_lane_mean(o * o) + EPS)
            o_n = o * r
            gb = z_ref[3, rows, :]
            sgb = _sigmoid(gb)
            dyb_c = dyb_ref[rows, :]
            d_ong = dyb_c * (gb * sgb)
            d_gb = (dyb_c * (o_n * gain) * (sgb * (1.0 + gb * (1.0 - sgb)))).astype(_MXU_DTYPE)
            d_gain = jnp.sum(d_ong * o_n, axis=0, keepdims=True)
            d_on = d_ong * gain
            return d_gb, d_gain, r * (d_on - o_n * _lane_mean(d_on * o_n)), q_int, decay

        def store_output_gradient(c, out):
            rows = _chunk_rows(c)
            dz_ref[3, rows, :], dgp_scr[pl.ds(c, 1), :], do_scr[rows, :], qi_scr[rows, :], dec_scr[pl.ds(c, 1), :] = out

        def carry_state_gradient(cc, d_state_t):
            c = n_chunks - 1 - cc
            update = dst_scr[c]
            dst_scr[c] = d_state_t
            return d_state_t * dec_scr[pl.ds(c, 1), :] + update

        def score_gradients(c):
            rows, qh, k, v, e_q, e_k, e_qi, e_ks, decay = _chunk_terms(c, z_ref, f_scr, qh_scr, b_scr)
            state_t = st_ref[c]
            d_state_t = dst_scr[c]
            d_o = do_scr[rows, :]
            q_in, k_in, q_int, k_st = qh * e_q, k * e_k, qh * e_qi, k * e_ks
            first = (_mm_nt(k_in, q_in), _mm_nt(d_o, v), _mm_nt(v, d_o), _mm_nt(k_st, d_state_t), _mm(d_o, state_t),
                     _mm(v, d_state_t))
            d_decay = jnp.sum(state_t * d_state_t, axis=0, keepdims=True)
            return first, d_o, q_in, k_in, q_int, k_st, e_q, e_k, e_qi, e_ks, decay, d_decay

        def input_gradients(c, s):
            (att_t, d_att, d_att_t, dv_inter, dq_int, dk_st), d_o, q_in, k_in, q_int, k_st, e_q, e_k, e_qi, e_ks, decay, d_decay = s
            rows = _chunk_rows(c)
            d_v = _mm(jnp.where(anti_causal, att_t, 0.0), d_o) + dv_inter
            dq_in = _mm(jnp.where(causal, d_att, 0.0), k_in)
            dk_in = _mm(jnp.where(anti_causal, d_att_t, 0.0), q_in)
            d_k = dk_in * e_k + dk_st * e_ks
            kk = dk_st * k_st
            d_b = dq_in * q_in + dq_int * q_int - dk_in * k_in - kk
            d_b_last = jnp.sum(kk, axis=0, keepdims=True) + decay * d_decay
            d_logf = _cumsum_rows(d_b, reverse=True) + d_b_last
            return d_v.astype(_MXU_DTYPE), dq_in * e_q + dq_int * e_qi, d_logf / f_scr[rows, :] - d_k

        def store_input_gradients(c, out):
            rows = _chunk_rows(c)
            dz_ref[2, rows, :], dqh_scr[rows, :], df_scr[rows, :] = out

        def input_activations(rows):
            q = z_ref[0, rows, :]
            sq = _sigmoid(q)
            dz_ref[0, rows, :] = (dqh_scr[rows, :] * (sq * (1.0 + q * (1.0 - sq)))).astype(_MXU_DTYPE)
            sg = _sigmoid(z_ref[1, rows, :])
            d_f = df_scr[rows, :]
            dz_ref[1, rows, :] = (d_f * (1.0 - lb) * sg * (1.0 - sg)).astype(_MXU_DTYPE)
            dlb_scr[...] += jnp.sum(d_f * (1.0 - sg), axis=0, keepdims=True)

        _independent_trips(n_chunks, output_gradient, store_output_gradient)
        _token_contractions(do_scr, qi_scr, dst_scr, seq)
        lax.fori_loop(0, n_chunks, carry_state_gradient, jnp.zeros((LANES, LANES), F32))
        _independent_trips(n_chunks, [score_gradients, input_gradients], store_input_gradients)
        dg_ref[...] += jnp.sum(dgp_scr[...], axis=0, keepdims=True)
        _row_blocks(seq, input_activations)
        d_l0 = dlb_scr[...] * lb * (1.0 - lb)
        dlog_ref[0:1, :] = d_l0
        dlog_ref[1:2, :] = -d_l0

    tokens = batch * seq
    seq_buf = pltpu.VMEM((seq, LANES), F32)
    chunk_rows = pltpu.VMEM((n_chunks, LANES), F32)
    return pl.pallas_call(
        body, name="branch_b_bwd",
        grid=(N_BLK, batch),
        in_specs=[pl.BlockSpec((4, seq, LANES), lambda h, b: (0, b, h)),
                  pl.BlockSpec((None, n_chunks, LANES, LANES), lambda h, b: (b * N_BLK + h, 0, 0, 0)),
                  pl.BlockSpec((4, seq, LANES), lambda h, b: (0, b, h)),
                  pl.BlockSpec((seq, LANES), lambda h, b: (b, h)),
                  pl.BlockSpec(memory_space=pl.ANY),
                  pl.BlockSpec((2, LANES), lambda h, b: (0, h)),
                  pl.BlockSpec((1, LANES), lambda h, b: (0, 0))],
        out_specs=[pl.BlockSpec((4, seq, LANES), lambda h, b: (0, b, h)),
                   pl.BlockSpec((2, LANES), lambda h, b: (0, h)),
                   pl.BlockSpec((1, LANES), lambda h, b: (0, 0))],
        out_shape=[jax.ShapeDtypeStruct((N_GROUPS, tokens, D_MODEL), _MXU_DTYPE),
                   jax.ShapeDtypeStruct((2, D_MODEL), F32),
                   jax.ShapeDtypeStruct((1, LANES), F32)],
        scratch_shapes=[seq_buf] * 4 + [chunk_rows, chunk_rows, pltpu.VMEM((1, LANES), F32),
                                        pltpu.VMEM((n_chunks, LANES, LANES), F32)],
        input_output_aliases={4: 0},
        compiler_params=_params(("arbitrary", "arbitrary")),
    )(z, states, kept, dyb, dz, lb_logits, hg_g)


def _merge_tail(ya, yb, z, x2d, tgt2d, b_merge, final_g, pa, pb, wo):
    tokens, d = x2d.shape
    tm = min(TAIL_TOKENS, tokens)
    n_tiles = tokens // tm

    def body(ya_ref, yb_ref, z_ref, x_ref, t_ref, bm_ref, fg_ref, pa_hbm, pb_hbm, wo_hbm,
             dya_ref, dyb_ref, dx2_ref, dz_ref, loss_ref, dfg_ref, dbm_ref, dpa_hbm, dpb_hbm, dwo_hbm,
             pa_s, pb_s, wo_s, dpa_s, dpb_s, dwo_s, sems):
        i = pl.program_id(0)

        def together(pairs):
            copies = [pltpu.make_async_copy(src, dst, sems.at[k]) for k, (src, dst) in enumerate(pairs)]
            for cp in copies:
                cp.start()
            for cp in copies:
                cp.wait()

        @pl.when(i == 0)
        def _():
            together([(pa_hbm, pa_s), (pb_hbm, pb_s), (wo_hbm, wo_s)])
            dpa_s[...] = jnp.zeros_like(dpa_s)
            dpb_s[...] = jnp.zeros_like(dpb_s)
            dwo_s[...] = jnp.zeros_like(dwo_s)
            loss_ref[...] = jnp.zeros_like(loss_ref)
            dfg_ref[...] = jnp.zeros_like(dfg_ref)
            dbm_ref[...] = jnp.zeros_like(dbm_ref)

        ya_t = ya_ref[...]
        yb_t = yb_ref[...]
        out_a = _mm(ya_t, pa_s[...])
        out_b = _mm(yb_t, pb_s[...])
        g_a = _sigmoid(z_ref[0] + bm_ref[:, :d])
        g_b = _sigmoid(z_ref[1] + bm_ref[:, d:])
        mixed = g_a * out_a + g_b * out_b
        x2 = x_ref[...] + _mm(mixed, wo_s[...])
        r = lax.rsqrt(jnp.mean(x2 * x2, axis=-1, keepdims=True) + EPS)
        xn = x2 * r
        fg = fg_ref[...]
        diff = xn * fg - t_ref[...]
        loss_ref[...] += jnp.sum(diff * diff) * (0.5 / d)
        dy = diff * (1.0 / d)
        dfg_ref[...] += jnp.sum(dy * xn, axis=0, keepdims=True)
        dxn = dy * fg
        dx2 = r * (dxn - xn * jnp.mean(dxn * xn, axis=-1, keepdims=True))
        dx2_ref[...] = dx2
        dmixed = _mm_nt(dx2, wo_s[...])
        dwo_s[...] += _mm_tn(mixed, dx2)
        dgm_a = dmixed * out_a * g_a * (1.0 - g_a)
        dgm_b = dmixed * out_b * g_b * (1.0 - g_b)
        dz_ref[0] = dgm_a.astype(_MXU_DTYPE)
        dz_ref[1] = dgm_b.astype(_MXU_DTYPE)
        dbm_ref[:, :d] += jnp.sum(dgm_a, axis=0, keepdims=True)
        dbm_ref[:, d:] += jnp.sum(dgm_b, axis=0, keepdims=True)
        dout_a = dmixed * g_a
        dout_b = dmixed * g_b
        dpa_s[...] += _mm_tn(ya_t, dout_a)
        dpb_s[...] += _mm_tn(yb_t, dout_b)
        dya_ref[...] = _mm_nt(dout_a, pa_s[...])
        dyb_ref[...] = _mm_nt(dout_b, pb_s[...])

        @pl.when(i == n_tiles - 1)
        def _():
            together([(dpa_s, dpa_hbm), (dpb_s, dpb_hbm), (dwo_s, dwo_hbm)])

    tile = pl.BlockSpec((tm, d), lambda i: (i, 0))
    gm = pl.BlockSpec((2, tm, d), lambda i: (3, i, 0))
    row = lambda n: pl.BlockSpec((1, n), lambda i: (0, 0))
    hbm = pl.BlockSpec(memory_space=pl.ANY)
    act = jax.ShapeDtypeStruct((tokens, d), F32)
    mat = jax.ShapeDtypeStruct((d, d), F32)
    return pl.pallas_call(
        body, name="merge_tail",
        grid=(n_tiles,),
        in_specs=[tile, tile, gm, tile, tile, row(2 * d), row(d), hbm, hbm, hbm],
        out_specs=[tile, tile, tile, gm, row(LANES), row(d), row(2 * d), hbm, hbm, hbm],
        out_shape=[act, act, act, jax.ShapeDtypeStruct((N_GROUPS, tokens, d), _MXU_DTYPE),
                   jax.ShapeDtypeStruct((1, LANES), F32), jax.ShapeDtypeStruct((1, d), F32),
                   jax.ShapeDtypeStruct((1, 2 * d), F32), mat, mat, mat],
        scratch_shapes=[pltpu.VMEM((d, d), _MXU_DTYPE)] * 3 + [pltpu.VMEM((d, d), F32)] * 3
        + [pltpu.SemaphoreType.DMA((3,))],
        compiler_params=_params(("arbitrary",)),
    )(ya, yb, z, x2d, tgt2d, b_merge, final_g, pa, pb, wo)


def _inproj_dw_exchange(h_t, dz, scatter):
    d, tokens = h_t.shape
    tm = min(CONTRACT_TOKENS, tokens)
    n_i = tokens // tm
    half = d // 2

    def body(h_ref, dz_ref, *rest):
        n_in, n_out = scatter.n_in, scatter.n_out
        dw_hbm, land_hbm = rest[n_in:n_in + 2]
        acc, local_sems, send_sems, recv_sems = rest[n_in + 2 + n_out:n_in + 6 + n_out]
        carried = scatter.copies(rest[:n_in], rest[n_in + 2:n_in + 2 + n_out], rest[n_in + 6 + n_out:])
        s, i = pl.program_id(0), pl.program_id(1)
        x, y, c, _ = _mesh_position()

        @pl.when((s == 0) & (i == 0))
        def _():
            for cp in carried:
                cp.start()

        part = _mm(h_ref[...], dz_ref[...])
        buf = acc.at[s % 2]

        @pl.when(i == 0)
        def _():
            buf[...] = part

        @pl.when(i != 0)
        def _():
            buf[...] += part

        def copies(k):
            g = _SLOT_TO_GROUP[k]
            cols = pl.ds((g % 2) * D_MODEL, D_MODEL)
            src = acc.at[k % 2]
            mine = pltpu.make_async_copy(src, dw_hbm.at[g // 2, :, cols], local_sems.at[k % 2])
            theirs = _remote(src.at[pl.ds((1 - c) * half, half), :], land_hbm.at[g // 2, :, cols],
                             send_sems, recv_sems, k, (x, y, 1 - c))
            return mine, theirs

        for k in range(N_GROUPS):
            @pl.when((s == k) & (i == n_i - 1))
            def _(k=k):
                if k > 0:
                    mine, theirs = copies(k - 1)
                    mine.wait()
                    theirs.wait_send()
                mine, theirs = copies(k)
                mine.start()
                theirs.start()
                if k == N_GROUPS - 1:
                    mine.wait()
                    theirs.wait_send()
                    for kk in range(N_GROUPS):
                        copies(kk)[1].wait_recv()
                    for cp in carried:
                        cp.wait()

    hbm = pl.BlockSpec(memory_space=pl.ANY)
    more = scatter.plumbing(first_operand=2, first_output=2)
    return pl.pallas_call(
        body, name="inproj_dw_exchange",
        grid=(N_GROUPS, n_i),
        in_specs=[pl.BlockSpec((d, tm), lambda s, i: (0, i)),
                  pl.BlockSpec((None, tm, D_MODEL), lambda s, i: (s, i, 0))] + more[1],
        out_specs=[hbm, hbm] + more[2],
        out_shape=[jax.ShapeDtypeStruct((N_SHARDS, d, 2 * D_MODEL), F32),
                   jax.ShapeDtypeStruct((N_SHARDS, half, 2 * D_MODEL), F32)] + more[3],
        scratch_shapes=[pltpu.VMEM((2, d, D_MODEL), F32), pltpu.SemaphoreType.DMA((2,)),
                        pltpu.SemaphoreType.DMA((N_GROUPS,)), pltpu.SemaphoreType.DMA((N_GROUPS,))] + more[4],
        input_output_aliases=more[5],
        compiler_params=_params(("arbitrary", "arbitrary")),
    )(h_t, dz, *more[0])


def _inproj_dx(dz, w_all, x2d, dx2, norm_g, scatter):
    tokens, d = x2d.shape
    tm = min(TAIL_TOKENS, tokens)
    n_tiles = tokens // tm

    def body(dz_ref, w_hbm, x_ref, dx2_ref, g_ref, *rest):
        n_in, n_out = scatter.n_in, scatter.n_out
        dx_ref, dg_ref = rest[n_in:n_in + 2]
        w_res, load_sems = rest[n_in + 2 + n_out:n_in + 4 + n_out]
        copies = scatter.copies(rest[:n_in], rest[n_in + 2:n_in + 2 + n_out], rest[n_in + 4 + n_out:])
        i = pl.program_id(0)

        @pl.when(i == 0)
        def _():
            for cp in copies:
                cp.start()
            loads = [pltpu.make_async_copy(w_hbm.at[g // 2, :, pl.ds((g % 2) * D_MODEL, D_MODEL)],
                                           w_res.at[:, pl.ds(slot * D_MODEL, D_MODEL)], load_sems.at[slot])
                     for slot, g in enumerate(_SLOT_TO_GROUP)]
            for cp in loads:
                cp.start()
            for cp in loads:
                cp.wait()
            dg_ref[...] = jnp.zeros_like(dg_ref)

        dz_all = jnp.concatenate([dz_ref[s] for s in range(N_GROUPS)], axis=1)
        dh = jnp.transpose(_mm_nt(w_res[...], dz_all))
        x = x_ref[...]
        r = lax.rsqrt(jnp.mean(x * x, axis=-1, keepdims=True) + EPS)
        xn = x * r
        dg_ref[...] += jnp.sum(dh * xn, axis=0, keepdims=True)
        dxn = dh * g_ref[...]
        dx_ref[...] = r * (dxn - xn * jnp.mean(dxn * xn, axis=-1, keepdims=True)) + dx2_ref[...]

        @pl.when(i == n_tiles - 1)
        def _():
            for cp in copies:
                cp.wait()

    tile = pl.BlockSpec((tm, d), lambda i: (i, 0))
    hbm = pl.BlockSpec(memory_space=pl.ANY)
    more = scatter.plumbing(first_operand=5, first_output=2)
    return pl.pallas_call(
        body, name="inproj_dx", grid=(n_tiles,),
        in_specs=[pl.BlockSpec((N_GROUPS, tm, D_MODEL), lambda i: (0, i, 0)), hbm, tile, tile,
                  pl.BlockSpec((1, d), lambda i: (0, 0))] + more[1],
        out_specs=[tile, pl.BlockSpec((1, d), lambda i: (0, 0))] + more[2],
        out_shape=[jax.ShapeDtypeStruct((tokens, d), F32), jax.ShapeDtypeStruct((1, d), F32)] + more[3],
        scratch_shapes=[pltpu.VMEM((d, N_GROUPS * D_MODEL), _MXU_DTYPE), pltpu.SemaphoreType.DMA((N_GROUPS,))] + more[4],
        input_output_aliases=more[5],
        compiler_params=_params(("arbitrary",)),
    )(dz, w_all, x2d, dx2, norm_g, *more[0])


def _row_tile(rows, cols, itemsize=4, budget=2 * 1024 * 1024):
    tr = rows
    while tr * cols * itemsize > budget and tr % 16 == 0:
        tr //= 2
    return tr


def _cast_into_slot(a, chip, dtype, name):
    rows, cols = a.shape
    tr = _row_tile(rows, cols)

    def body(chip_ref, a_ref, o_ref):
        del chip_ref
        o_ref[...] = a_ref[...].astype(dtype)

    grid_spec = pltpu.PrefetchScalarGridSpec(
        num_scalar_prefetch=1, grid=(rows // tr,),
        in_specs=[pl.BlockSpec((tr, cols), lambda i, chip_ref: (i, 0))],
        out_specs=pl.BlockSpec((None, tr, cols), lambda i, chip_ref: (chip_ref[0], i, 0)))
    return pl.pallas_call(body, name=name, grid_spec=grid_spec,
                          out_shape=jax.ShapeDtypeStruct((N_SHARDS, rows, cols), dtype),
                          compiler_params=_params(("arbitrary",)))(chip, a)


def _sum_slots(stack, name):
    n, rows, cols = stack.shape
    tr = _row_tile(rows, cols * n)

    def body(s_ref, o_ref):
        total = s_ref[0].astype(F32)
        for k in range(1, n):
            total = total + s_ref[k].astype(F32)
        o_ref[...] = total

    return pl.pallas_call(body, name=name, grid=(rows // tr,),
                          in_specs=[pl.BlockSpec((n, tr, cols), lambda i: (0, i, 0))],
                          out_specs=pl.BlockSpec((tr, cols), lambda i: (i, 0)),
                          out_shape=jax.ShapeDtypeStruct((rows, cols), F32),
                          compiler_params=_params(("parallel",)))(stack)


def _add_half(full, landed, place, name):
    n, rows, cols = full.shape
    half = rows // 2
    tr = _row_tile(half, cols)
    nb = half // tr

    def body(place_ref, a_ref, b_ref, o_ref, own_ref):
        total = (a_ref[...] + b_ref[...]).astype(_MXU_DTYPE)
        o_ref[...] = total

        @pl.when(pl.program_id(1) == place_ref[1])
        def _():
            own_ref[...] = total

    grid_spec = pltpu.PrefetchScalarGridSpec(
        num_scalar_prefetch=1, grid=(nb, n),
        in_specs=[pl.BlockSpec((None, tr, cols), lambda i, j, place_ref: (j, place_ref[0] * nb + i, 0)),
                  pl.BlockSpec((None, tr, cols), lambda i, j, place_ref: (j, i, 0))],
        out_specs=[pl.BlockSpec((None, tr, cols), lambda i, j, place_ref: (j, i, 0)),
                   pl.BlockSpec((None, tr, cols), lambda i, j, place_ref: (place_ref[1], i, 0))])
    shape = jax.ShapeDtypeStruct((n, half, cols), _MXU_DTYPE)
    return pl.pallas_call(body, name=name, grid_spec=grid_spec, out_shape=[shape, shape],
                          compiler_params=_params(("parallel", "arbitrary")))(place, full, landed)


def _adamw_update(w, grad, m, v):
    c1 = 1.0 - ADAM_B1 ** ADAM_STEP
    c2 = 1.0 - ADAM_B2 ** ADAM_STEP
    nm = ADAM_B1 * m + (1.0 - ADAM_B1) * grad
    nv = ADAM_B2 * v + (1.0 - ADAM_B2) * (grad * grad)
    return (-ADAM_LR) * ((nm / c1) / (jnp.sqrt(nv / c2) + ADAM_EPS) + ADAM_WD * w), nm, nv


def _adamw(w, g, m, v, name):
    rows, cols = w.shape
    tr = _row_tile(rows, cols, budget=1024 * 1024)

    def body(w_ref, g_ref, m_ref, v_ref, d_ref, nm_ref, nv_ref):
        d_ref[...], nm_ref[...], nv_ref[...] = _adamw_update(w_ref[...], g_ref[...], m_ref[...], v_ref[...])

    spec = pl.BlockSpec((tr, cols), lambda i: (i, 0))
    shape = jax.ShapeDtypeStruct((rows, cols), F32)
    return pl.pallas_call(body, name=name, grid=(rows // tr,), in_specs=[spec] * 4, out_specs=[spec] * 3,
                          out_shape=[shape] * 3, compiler_params=_params(("parallel",)))(w, g, m, v)


def _adamw_halves(w, g_mine, g_sibling, m, v, core, name):
    rows, cols = w.shape
    half = rows // 2
    tr = _row_tile(half, cols, budget=1024 * 1024)
    nb = half // tr

    def body(core_ref, w_ref, gm_ref, gs_ref, m_ref, v_ref, g_ref, d_ref, nm_ref, nv_ref):
        mine = pl.program_id(0) // nb == core_ref[0]
        grad = jnp.where(mine, gm_ref[...], gs_ref[...])
        g_ref[...] = grad
        d_ref[...], nm_ref[...], nv_ref[...] = _adamw_update(w_ref[...], grad, m_ref[...], v_ref[...])

    spec = pl.BlockSpec((tr, cols), lambda i, core_ref: (i, 0))
    mine_spec = pl.BlockSpec((tr, cols), lambda i, core_ref: (jnp.where(i // nb == core_ref[0], i % nb, 0), 0))
    sibling_spec = pl.BlockSpec((tr, cols), lambda i, core_ref: (jnp.where(i // nb == core_ref[0], 0, i % nb), 0))
    grid_spec = pltpu.PrefetchScalarGridSpec(num_scalar_prefetch=1, grid=(rows // tr,),
                                             in_specs=[spec, mine_spec, sibling_spec, spec, spec], out_specs=[spec] * 4)
    shape = jax.ShapeDtypeStruct((rows, cols), F32)
    return pl.pallas_call(body, name=name, grid_spec=grid_spec, out_shape=[shape] * 4,
                          compiler_params=_params(("parallel",)))(core, w, g_mine, g_sibling, m, v)


def _local_step(x, loss_target, gather, reduction, b_merge, conv_b, rg_wx, rg_bx, rg_wa, rg_ba, rg_lambda,
                hg_lb_logits, hg_norm_g, norm_g, final_norm_g):
    batch, seq, d = x.shape
    x2d = x.reshape(batch * seq, d)
    tgt2d = loss_target.reshape(batch * seq, d)
    z, h_t, (w_all, pa, pb, wo), cw_all = _inproj_fwd_gather(x2d, norm_g, *gather)
    pa, pb, wo = (t.reshape(d, d) for t in (pa, pb, wo))
    conv_w = jnp.transpose(cw_all, (1, 0, 2)).reshape(CONV_WIDTH, d)
    lru = (conv_w, conv_b, rg_wx, rg_bx, rg_wa, rg_ba, rg_lambda)
    ya, hl, kept = _branch_a_fwd(z, *lru, batch, seq)
    yb, states, kept_b = _branch_b_fwd(z, hg_lb_logits, hg_norm_g, batch, seq)
    dya, dyb, dx2, dz, loss, d_final_g, d_b_merge, d_pa, d_pb, d_wo = _merge_tail(
        ya, yb, z, x2d, tgt2d, b_merge, final_norm_g, pa, pb, wo)
    dz, d_lb_logits, d_hg_g = _branch_b_bwd(z, states, kept_b, dyb, dz, hg_lb_logits, hg_norm_g, batch, seq)
    dz, d_conv_w, d_conv_b, d_wx, d_bx, d_wa, d_ba, d_lam = _branch_a_bwd(
        z, hl, kept, dya, dz, conv_w, rg_wx, rg_wa, rg_lambda, batch, seq)
    small = dict(b_merge=d_b_merge, conv_w=d_conv_w, conv_b=d_conv_b, rg_wx=d_wx, rg_bx=d_bx, rg_wa=d_wa,
                 rg_ba=d_ba, rg_lambda=d_lam, hg_lb_logits=d_lb_logits, hg_norm_g=d_hg_g,
                 norm_g=jnp.zeros((1, d), F32), final_norm_g=d_final_g)
    first, second = reduction
    d_w_in, landed_w_in, *scattered_first = _inproj_dw_exchange(h_t, dz, first((d_pa, d_pb, d_wo), small))
    grad_x, d_norm_g, *scattered_second = _inproj_dx(dz, w_all, x2d, dx2, norm_g, scatter=second(d_w_in, landed_w_in))
    return loss[0, 0], grad_x.reshape(batch, seq, d), d_norm_g, (scattered_first, scattered_second)


_SMALL_ORDER = ("b_merge", "conv_w", "conv_b", "rg_wx", "rg_bx", "rg_wa", "rg_ba", "rg_lambda", "hg_lb_logits",
                "hg_norm_g", "norm_g", "final_norm_g")
N_DEV = 8
PIECE_ROWS = 272


def _pack_small(tree):
    flat = jnp.concatenate([tree[k].reshape(-1) for k in _SMALL_ORDER])
    flat = jnp.pad(flat, (0, N_DEV * PIECE_ROWS * LANES - flat.shape[0]))
    return flat.reshape(N_DEV * PIECE_ROWS, LANES)


def _unpack_small(packed, like):
    flat = packed.reshape(-1)
    out, pos = {}, 0
    for k in _SMALL_ORDER:
        n = like[k].size
        out[k] = flat[pos:pos + n].reshape(like[k].shape)
        pos += n
    return out


def _mesh_position():
    x, y, c = lax.axis_index("x"), lax.axis_index("y"), lax.axis_index("c")
    other_chips = [(1 - x, y), (x, 1 - y), (1 - x, 1 - y)]
    return x, y, c, other_chips


def _other_devices(x, y, c):
    flips = [(fx, fy, fc) for fx in (0, 1) for fy in (0, 1) for fc in (0, 1) if (fx, fy, fc) != (0, 0, 0)]
    return [(jnp.where(fx, 1 - x, x), jnp.where(fy, 1 - y, y), jnp.where(fc, 1 - c, c)) for fx, fy, fc in flips]


def _remote(src, dst, send_sems, recv_sems, k, device):
    return pltpu.make_async_remote_copy(src_ref=src, dst_ref=dst, send_sem=send_sems.at[k], recv_sem=recv_sems.at[k],
                                        device_id=device, device_id_type=MESH)


def _exchange_halves(bigs, small):
    n_big = len(bigs)
    n_sem = n_big + N_DEV - 1

    def body(*refs):
        srcs, small_src = refs[:n_big], refs[n_big]
        outs, small_out = refs[n_big + 1:2 * n_big + 1], refs[2 * n_big + 1]
        send_sems, recv_sems, local_sem = refs[2 * n_big + 2:]
        x, y, c, _ = _mesh_position()
        me, sibling = 4 * x + 2 * y + c, (x, y, 1 - c)
        mine = pltpu.make_async_copy(small_src.at[pl.ds(me * PIECE_ROWS, PIECE_ROWS), :], small_out.at[me], local_sem)
        mine.start()
        copies = []
        for a in range(n_big):
            hs = srcs[a].shape[1] // 2
            copies.append(_remote(srcs[a].at[:, pl.ds((1 - c) * hs, hs), :], outs[a], send_sems, recv_sems, a, sibling))
        for k, (px, py, pc) in enumerate(_other_devices(x, y, c)):
            piece = small_src.at[pl.ds((4 * px + 2 * py + pc) * PIECE_ROWS, PIECE_ROWS), :]
            copies.append(_remote(piece, small_out.at[me], send_sems, recv_sems, n_big + k, (px, py, pc)))
        for cp in copies:
            cp.start()
        for cp in copies:
            cp.wait()
        mine.wait()

    hbm = pl.BlockSpec(memory_space=pl.ANY)
    out_shape = [jax.ShapeDtypeStruct((g.shape[0], g.shape[1] // 2, g.shape[2]), F32) for g in bigs]
    out_shape.append(jax.ShapeDtypeStruct((N_DEV, PIECE_ROWS, LANES), F32))
    return pl.pallas_call(
        body, name="exchange_halves",
        in_specs=[hbm] * (n_big + 1), out_specs=[hbm] * (n_big + 1), out_shape=out_shape,
        scratch_shapes=[pltpu.SemaphoreType.DMA((n_sem,)), pltpu.SemaphoreType.DMA((n_sem,)), pltpu.SemaphoreType.DMA],
    )(*bigs, small)


class _Scatter:
    def __init__(self, bigs, by_chip, small=None):
        self.bigs, self.by_chip, self.small = list(bigs), list(by_chip), small
        self.n_big = len(self.bigs)
        self.n_in = 2 * self.n_big + (small is not None)
        self.n_out = self.n_big + (small is not None)
        self.n_scratch = 2 + (small is not None)

    def plumbing(self, first_operand, first_output):
        hbm = pl.BlockSpec(memory_space=pl.ANY)
        n_sem = 3 * self.n_big + (N_DEV - 1 if self.small is not None else 0)
        operands = self.bigs + self.by_chip + ([self.small] if self.small is not None else [])
        out_shapes = [jax.ShapeDtypeStruct(g.shape, g.dtype) for g in self.by_chip]
        scratch = [pltpu.SemaphoreType.DMA((n_sem,)), pltpu.SemaphoreType.DMA((n_sem,))]
        if self.small is not None:
            out_shapes.append(jax.ShapeDtypeStruct((N_DEV, PIECE_ROWS, LANES), F32))
            scratch.append(pltpu.SemaphoreType.DMA)
        aliases = {first_operand + self.n_big + a: first_output + a for a in range(self.n_big)}
        return operands, [hbm] * self.n_in, [hbm] * self.n_out, out_shapes, scratch, aliases

    def copies(self, in_refs, out_refs, scratch_refs):
        srcs, outs = in_refs[:self.n_big], out_refs[:self.n_big]
        send_sems, recv_sems = scratch_refs[:2]
        x, y, c, chips = _mesh_position()
        chip, me = 2 * x + y, 4 * x + 2 * y + c
        copies = []
        for a in range(self.n_big):
            for j, (cx, cy) in enumerate(chips):
                copies.append(_remote(srcs[a].at[2 * cx + cy], outs[a].at[chip], send_sems, recv_sems, 3 * a + j,
                                      (cx, cy, c)))
        if self.small is not None:
            small_src, small_out = in_refs[2 * self.n_big], out_refs[self.n_big]
            copies.append(pltpu.make_async_copy(small_src, small_out.at[me], scratch_refs[2]))
            for k, peer in enumerate(_other_devices(x, y, c)):
                copies.append(_remote(small_src, small_out.at[me], send_sems, recv_sems, 3 * self.n_big + k, peer))
        return copies


def _swap_halves(halves, vec):
    n_big = len(halves)

    def body(*refs):
        srcs, vec_src = refs[:n_big], refs[n_big]
        outs, vec_out = refs[n_big + 1:2 * n_big + 1], refs[2 * n_big + 1]
        send_sems, recv_sems, local_sem = refs[2 * n_big + 2:]
        x, y, c, _ = _mesh_position()
        me = 4 * x + 2 * y + c
        copies = [pltpu.make_async_copy(vec_src, vec_out.at[me], local_sem)]
        copies += [_remote(srcs[a], outs[a], send_sems, recv_sems, a, (x, y, 1 - c)) for a in range(n_big)]
        copies += [_remote(vec_src, vec_out.at[me], send_sems, recv_sems, n_big + k, peer)
                   for k, peer in enumerate(_other_devices(x, y, c))]
        for cp in copies:
            cp.start()
        for cp in copies:
            cp.wait()

    hbm = pl.BlockSpec(memory_space=pl.ANY)
    n_sem = n_big + N_DEV - 1
    return pl.pallas_call(
        body, name="swap_halves",
        in_specs=[hbm] * (n_big + 1), out_specs=[hbm] * (n_big + 1),
        out_shape=[jax.ShapeDtypeStruct(h.shape, F32) for h in halves] + [jax.ShapeDtypeStruct((N_DEV,) + vec.shape, F32)],
        scratch_shapes=[pltpu.SemaphoreType.DMA((n_sem,)), pltpu.SemaphoreType.DMA((n_sem,)), pltpu.SemaphoreType.DMA],
    )(*halves, vec)


def kernel(x, w_in, b_merge, conv_w, conv_b, rg_wx, rg_bx, rg_wa, rg_ba, rg_lambda, hg_lb_logits, hg_norm_g, proj_a, proj_b, w_out, norm_g, final_norm_g, loss_target, m_w_in, m_b_merge, m_conv_w, m_conv_b, m_rg_wx, m_rg_bx, m_rg_wa, m_rg_ba, m_rg_lambda, m_hg_lb_logits, m_hg_norm_g, m_proj_a, m_proj_b, m_w_out, m_norm_g, m_final_norm_g, v_w_in, v_b_merge, v_conv_w, v_conv_b, v_rg_wx, v_rg_bx, v_rg_wa, v_rg_ba, v_rg_lambda, v_hg_lb_logits, v_hg_norm_g, v_proj_a, v_proj_b, v_w_out, v_norm_g, v_final_norm_g):
    d = D_MODEL
    weights = dict(w_in=w_in, b_merge=b_merge, conv_w=conv_w, conv_b=conv_b, rg_wx=rg_wx, rg_bx=rg_bx, rg_wa=rg_wa,
                   rg_ba=rg_ba, rg_lambda=rg_lambda, hg_lb_logits=hg_lb_logits, hg_norm_g=hg_norm_g, proj_a=proj_a,
                   proj_b=proj_b, w_out=w_out, norm_g=norm_g, final_norm_g=final_norm_g)
    m = dict(w_in=m_w_in, b_merge=m_b_merge, conv_w=m_conv_w, conv_b=m_conv_b, rg_wx=m_rg_wx, rg_bx=m_rg_bx,
             rg_wa=m_rg_wa, rg_ba=m_rg_ba, rg_lambda=m_rg_lambda, hg_lb_logits=m_hg_lb_logits, hg_norm_g=m_hg_norm_g,
             proj_a=m_proj_a, proj_b=m_proj_b, w_out=m_w_out, norm_g=m_norm_g, final_norm_g=m_final_norm_g)
    v = dict(w_in=v_w_in, b_merge=v_b_merge, conv_w=v_conv_w, conv_b=v_conv_b, rg_wx=v_rg_wx, rg_bx=v_rg_bx,
             rg_wa=v_rg_wa, rg_ba=v_rg_ba, rg_lambda=v_rg_lambda, hg_lb_logits=v_hg_lb_logits, hg_norm_g=v_hg_norm_g,
             proj_a=v_proj_a, proj_b=v_proj_b, w_out=v_w_out, norm_g=v_norm_g, final_norm_g=v_final_norm_g)
    big_names = ("w_in", "proj_a", "proj_b", "w_out")

    core = lax.axis_index("c").astype(jnp.int32).reshape(1)
    chip = (2 * lax.axis_index("x") + lax.axis_index("y")).astype(jnp.int32)

    slotted = [_cast_into_slot(weights[k][0], chip.reshape(1), _MXU_DTYPE, f"cast_{k}") for k in big_names]
    conv_slotted = _cast_into_slot(conv_w[0], chip.reshape(1), F32, "slot_conv_w")

    small_shapes = {}

    place = jnp.concatenate([core, chip.reshape(1)])

    def reduce_proj_and_small(proj_grads, small_grads):
        small_shapes.update({k: t.shape for k, t in small_grads.items()})
        bigs = [g.reshape(N_SHARDS, d // N_SHARDS, d) for g in proj_grads]
        *landed, small_landed = _exchange_halves(bigs, _pack_small(small_grads))
        sums = [_add_half(g, l, place, f"add_half_{1 + a}") for a, (g, l) in enumerate(zip(bigs, landed))]
        return _Scatter([s[0] for s in sums], [s[1] for s in sums], _sum_slots(small_landed, "sum_small"))

    def reduce_w_in(d_w_in, landed):
        partial, own_slot = _add_half(d_w_in, landed, place, "add_half_0")
        return _Scatter([partial], [own_slot])

    loss_part, grad_x, d_norm_g, ((*by_chip_proj, small_all), by_chip_w_in) = _local_step(
        x, loss_target, (slotted, conv_slotted, chip.reshape(1)), (reduce_proj_and_small, reduce_w_in),
        b_merge, conv_b, rg_wx[0], rg_bx.reshape(1, d), rg_wa[0], rg_ba.reshape(1, d), rg_lambda, hg_lb_logits,
        hg_norm_g, norm_g, final_norm_g.reshape(1, d))
    mine = [_sum_slots(s, f"sum_chips_{a}") for a, s in enumerate(by_chip_w_in + by_chip_proj)]
    late = jnp.concatenate([d_norm_g.reshape(SUBLANES, LANES), jnp.full((SUBLANES, LANES), loss_part, F32)])
    *theirs, late_parts = _swap_halves(mine, late)
    late_sum = _sum_slots(late_parts, "sum_late")
    loss = late_sum[SUBLANES, 0]
    small_red = _unpack_small(small_all, {k: jax.ShapeDtypeStruct(s, F32) for k, s in small_shapes.items()})
    small_red["norm_g"] = late_sum[:SUBLANES].reshape(1, d)

    grads, delta, new_m, new_v = {}, {}, {}, {}
    for k, g_mine, g_theirs in zip(big_names, mine, theirs):
        out = _adamw_halves(weights[k][0], g_mine, g_theirs, m[k][0], v[k][0], core, f"adamw_{k}")
        grads[k], delta[k], new_m[k], new_v[k] = (t.reshape(weights[k].shape) for t in out)
    cols = d // N_SHARDS
    g_conv = lax.dynamic_slice(small_red["conv_w"], (0, chip * cols), (CONV_WIDTH, cols))
    grads["conv_w"] = g_conv.reshape(conv_w.shape)
    dl, nm, nv = _adamw(conv_w[0], g_conv, m_conv_w[0], v_conv_w[0], "adamw_conv_w")
    delta["conv_w"], new_m["conv_w"], new_v["conv_w"] = (t.reshape(conv_w.shape) for t in (dl, nm, nv))
    rest = [k for k in _SMALL_ORDER if k != "conv_w"]
    like = {k: (weights[k] if k != "conv_w" else jnp.zeros((CONV_WIDTH, d), F32)) for k in _SMALL_ORDER}
    packs = [_pack_small({k: (t[k] if k != "conv_w" else like[k]) for k in _SMALL_ORDER}) for t in (weights, m, v)]
    g_pack = _pack_small({k: small_red[k].reshape(like[k].shape) for k in _SMALL_ORDER})
    outs = [_unpack_small(p, like) for p in _adamw(packs[0], g_pack, packs[1], packs[2], "adamw_small")]
    for k in rest:
        grads[k] = small_red[k].reshape(weights[k].shape)
        delta[k], new_m[k], new_v[k] = outs[0][k], outs[1][k], outs[2][k]

    order = ("w_in", "b_merge", "conv_w", "conv_b", "rg_wx", "rg_bx", "rg_wa", "rg_ba", "rg_lambda", "hg_lb_logits",
             "hg_norm_g", "proj_a", "proj_b", "w_out", "norm_g", "final_norm_g")
    return (loss, grad_x, *[grads[k] for k in order], *[delta[k] for k in order], *[new_m[k] for k in order],
            *[new_v[k] for k in order])
```

```python
import functools

import jax
import jax.numpy as jnp
from jax import lax
from jax.experimental import pallas as pl
from jax.experimental.pallas import tpu as pltpu

F32 = jnp.float32
_MXU_DTYPE = jnp.bfloat16

D_MODEL = 1024
LANES = 128
SUBLANES = 8
N_BLK = D_MODEL // LANES
N_GROUPS = 8
N_SHARDS = 4
CONV_WIDTH = 4
LRU_C = 8.0
CHUNK = 64
CHUNKS_IN_FLIGHT = 16
FWD_CHUNKS_IN_FLIGHT = 32
SCAN_UNROLL = 4
HG_SCALE = float(LANES) ** -0.5
EPS = 1e-6
ADAM_LR, ADAM_B1, ADAM_B2, ADAM_EPS, ADAM_WD, ADAM_STEP = 0.001, 0.9, 0.999, 1e-08, 0.01, 10
MATMUL_TOKENS = 1024
TAIL_TOKENS = 256
CONTRACT_TOKENS = 2048
VMEM_LIMIT = 56 * 1024 * 1024
VMEM_LIMIT_BIG = 60 * 1024 * 1024
MESH = pl.DeviceIdType.MESH

_SLOT_TO_GROUP = (2, 3, 4, 5, 0, 1, 6, 7)


def _mm(a, b):
    return lax.dot_general(a.astype(_MXU_DTYPE), b.astype(_MXU_DTYPE), (((1,), (0,)), ((), ())),
                           preferred_element_type=F32)


def _mm_nt(a, b):
    return lax.dot_general(a.astype(_MXU_DTYPE), b.astype(_MXU_DTYPE), (((1,), (1,)), ((), ())),
                           preferred_element_type=F32)


def _mm_tn(a, b):
    return lax.dot_general(a.astype(_MXU_DTYPE), b.astype(_MXU_DTYPE), (((0,), (0,)), ((), ())),
                           preferred_element_type=F32)


def _sigmoid(x):
    return 0.5 * jnp.tanh(0.5 * x) + 0.5


def _log1p_pos(y):
    series = y * (1.0 - y * (0.5 - y * (1.0 / 3.0 - y * 0.25)))
    return jnp.where(y < 0.01, series, jnp.log(1.0 + y))


def _softplus(x):
    return jnp.maximum(x, 0.0) + _log1p_pos(jnp.exp(-jnp.abs(x)))


def _shift_down(x, n):
    rolled = pltpu.roll(x, n, 0)
    edge = SUBLANES if (n < SUBLANES and x.shape[0] > SUBLANES) else x.shape[0]
    rows = lax.broadcasted_iota(jnp.int32, (edge, x.shape[1]), 0)
    head = jnp.where(rows >= n, rolled[:edge], 0.0)
    return head if edge == x.shape[0] else jnp.concatenate([head, rolled[edge:]], axis=0)


def _shift_up(x, n):
    size = x.shape[0]
    rolled = pltpu.roll(x, size - n, 0)
    edge = SUBLANES if (n < SUBLANES and size > SUBLANES) else size
    rows = lax.broadcasted_iota(jnp.int32, (edge, x.shape[1]), 0)
    tail = jnp.where(rows < edge - n, rolled[size - edge:], 0.0)
    return tail if edge == size else jnp.concatenate([rolled[:size - edge], tail], axis=0)


def _params(dims, vmem=VMEM_LIMIT):
    return pltpu.CompilerParams(dimension_semantics=dims, vmem_limit_bytes=vmem)


def _slot_of_group(g):
    return jnp.where(g < 2, g + 4, jnp.where(g < 6, g - 2, g))


def _inproj_fwd_gather(x2d, norm_g, slotted, conv_slotted, chip):
    tokens, d = x2d.shape
    tm = min(MATMUL_TOKENS, tokens)
    n_tiles = tokens // tm
    n_big = len(slotted)
    n_sem = 6 * (n_big + 1) + 3
    last_pass = N_GROUPS - 1

    def shard_of(k, chip_id):
        x, y = chip_id // 2, chip_id % 2
        return 2 * jnp.where(k % 2 == 1, 1 - x, x) + jnp.where(k // 2 == 1, 1 - y, y)

    def body(chip_ref, x_ref, g_ref, *rest):
        bufs, cw = rest[n_big + 1:2 * n_big + 1], rest[2 * n_big + 1]
        z_ref, ht_ref = rest[2 * n_big + 2:2 * n_big + 4]
        h_all, slab, send_sems, recv_sems, slab_sems = rest[2 * n_big + 4:]
        del chip_ref
        p, i = pl.program_id(0), pl.program_id(1)
        x, y, c, chips = _mesh_position()
        me, sibling = 2 * x + y, (x, y, 1 - c)

        pieces = [(0, 0), (0, 1)] + [(a, None) for a in range(1, n_big)]

        def half(piece, slot, which):
            a, q = pieces[piece]
            hs = bufs[a].shape[1] // 2
            cols = slice(None) if q is None else pl.ds(q * D_MODEL, D_MODEL)
            return bufs[a].at[slot, pl.ds(which * hs, hs), cols]

        def send(piece, j):
            mine = half(piece, me, c)
            return _remote(mine, mine, send_sems, recv_sems, 6 * piece + j, (chips[j][0], chips[j][1], c))

        def arrival(piece, j):
            landed = half(piece, 2 * chips[j][0] + chips[j][1], c)
            return _remote(landed, landed, send_sems, recv_sems, 6 * piece + j, (chips[j][0], chips[j][1], c))

        def passed_on(piece, j, which):
            landed = half(piece, 2 * chips[j][0] + chips[j][1], which)
            return _remote(landed, landed, send_sems, recv_sems, 6 * piece + 3 + j, sibling)

        def conv_copy(j, slot):
            return _remote(cw.at[slot], cw.at[slot], send_sems, recv_sems, 6 * len(pieces) + j,
                           (chips[j][0], chips[j][1], c))

        def land(piece, j):
            arrival(piece, j).wait_recv()
            passed_on(piece, j, c).start()
            passed_on(piece, j, 1 - c).wait_recv()

        def slab_copy(pv):
            src = bufs[0].at[shard_of(pv // 2, me), :, pl.ds((pv % 2) * D_MODEL, D_MODEL)]
            return pltpu.make_async_copy(src, slab.at[pv % 2], slab_sems.at[pv % 2])

        @pl.when((p == 0) & (i == 0))
        def _():
            for q in range(2):
                send(q, 0).start()
                send(q, 1).start()
            slab_copy(0).start()

        @pl.when(i == 0)
        def _():
            for pv in range(N_GROUPS):
                @pl.when(p == pv)
                def _(pv=pv):
                    slab_copy(pv).wait()

        rows = pl.ds(pl.multiple_of(i * tm, tm), tm)

        @pl.when(p == 0)
        def _():
            xt = x_ref[...]
            r = lax.rsqrt(jnp.mean(xt * xt, axis=-1, keepdims=True) + EPS)
            h = (xt * r) * g_ref[...]
            h_all[rows, :] = h.astype(_MXU_DTYPE)
            ht_ref[...] = jnp.transpose(h).astype(_MXU_DTYPE)

        def relay(q):
            landed = half(q, 2 * chips[q][0] + chips[q][1], c)
            to = chips[1 - q]
            return _remote(landed, landed, send_sems, recv_sems, 6 * q + 2, (to[0], to[1], c))

        landings = {1: [(0, 0)], 2: [(1, 0), (1, 1)], 3: [(0, 1)], 5: [(0, 2)], 6: [(1, 2)]}

        def end_of_pass(pv):
            for q, j in landings.get(pv - 1, []):
                land(q, j)
                if j == q:
                    relay(q).start()
            if pv - 1 == 1:
                for piece in range(2, len(pieces)):
                    for jj in range(3):
                        send(piece, jj).start()
                for jj in range(3):
                    conv_copy(jj, me).start()
            if pv - 1 in (5, 6):
                for piece in range(2, len(pieces)):
                    for jj in ((0, 1) if pv - 1 == 5 else (2,)):
                        land(piece, jj)
            slab_copy(pv).start()

        @pl.when(i == n_tiles - 1)
        def _():
            for pv in range(1, N_GROUPS):
                pl.when(p == pv - 1)(functools.partial(end_of_pass, pv))

        z_ref[...] = _mm(h_all[rows, :], slab[p % 2])

        @pl.when((p == last_pass) & (i == n_tiles - 1))
        def _():
            for j in range(3):
                conv_copy(j, 2 * chips[j][0] + chips[j][1]).wait_recv()
            for piece in range(len(pieces)):
                for j in range(3):
                    (relay(piece) if (piece < 2 and j == 2) else send(piece, j)).wait_send()
                    passed_on(piece, j, c).wait_send()
            for j in range(3):
                conv_copy(j, me).wait_send()

    def z_index(p, i, chip_ref):
        g = 2 * shard_of(p // 2, chip_ref[0]) + p % 2
        return (_slot_of_group(g), i, 0)

    def first_pass_tile(p, i, chip_ref):
        return jnp.where(p == 0, i, n_tiles - 1)

    hbm = pl.BlockSpec(memory_space=pl.ANY)
    operands = list(slotted) + [conv_slotted]
    grid_spec = pltpu.PrefetchScalarGridSpec(
        num_scalar_prefetch=1, grid=(N_GROUPS, n_tiles),
        in_specs=[pl.BlockSpec((tm, d), lambda p, i, chip_ref: (first_pass_tile(p, i, chip_ref), 0)),
                  pl.BlockSpec((1, d), lambda p, i, chip_ref: (0, 0))] + [hbm] * (n_big + 1),
        out_specs=[hbm] * (n_big + 1) + [pl.BlockSpec((None, tm, D_MODEL), z_index),
                                         pl.BlockSpec((d, tm), lambda p, i, chip_ref: (0, first_pass_tile(p, i, chip_ref)))],
        scratch_shapes=[pltpu.VMEM((tokens, d), _MXU_DTYPE), pltpu.VMEM((2, d, D_MODEL), _MXU_DTYPE),
                        pltpu.SemaphoreType.DMA((n_sem,)), pltpu.SemaphoreType.DMA((n_sem,)),
                        pltpu.SemaphoreType.DMA((2,))])
    out = pl.pallas_call(
        body, name="inproj_fwd_gather", grid_spec=grid_spec,
        out_shape=[jax.ShapeDtypeStruct(a.shape, a.dtype) for a in operands]
        + [jax.ShapeDtypeStruct((N_GROUPS, tokens, D_MODEL), F32), jax.ShapeDtypeStruct((d, tokens), _MXU_DTYPE)],
        input_output_aliases={3 + a: a for a in range(n_big + 1)},
        compiler_params=_params(("arbitrary", "arbitrary")),
    )(chip, x2d, norm_g, *operands)
    return out[n_big + 1], out[n_big + 2], out[:n_big], out[n_big]


def _lane_blocks(x):
    return [x[:, k * LANES:(k + 1) * LANES] for k in range(x.shape[1] // LANES)]


def _block_diag(x, w_ref, transposed=False):
    mm = _mm_nt if transposed else _mm
    return jnp.concatenate([mm(xk, w_ref[k]) for k, xk in enumerate(_lane_blocks(x))], axis=1)


def _lru_decay(gr, sp):
    log_a = (-LRU_C) * gr * sp
    a = jnp.exp(log_a)
    y = 2.0 * log_a
    mult_sq = jnp.where(y > -1e-3, -y * (1.0 + 0.5 * y), 1.0 - a * a)
    inv_mult = lax.rsqrt(jnp.maximum(mult_sq, 1e-37))
    return a, mult_sq * inv_mult, inv_mult


def _tile_rows(width):
    return lax.broadcasted_iota(jnp.int32, (SUBLANES, width), 0)


def _scan_forward(a_scr, u_scr, h_scr, seq):
    width = a_scr.shape[1]
    rows = _tile_rows(width)

    group = min(SCAN_UNROLL, seq // SUBLANES)

    def within_tile(j):
        sl = pl.ds(pl.multiple_of(j * SUBLANES, SUBLANES), SUBLANES)
        a = a_scr[sl, :]
        u = u_scr[sl, :]
        for d in (1, 2, 4):
            keep = rows >= d
            a_sh = jnp.where(keep, pltpu.roll(a, d, 0), 1.0)
            u_sh = jnp.where(keep, pltpu.roll(u, d, 0), 0.0)
            u = a * u_sh + u
            a = a * a_sh
        return sl, a, u

    def tiles(t, carry):
        parts = [within_tile(t * group + k) for k in range(group)]
        out = []
        for sl, a, u in parts:
            h = u + a * carry
            out.append((sl, h))
            carry = jnp.broadcast_to(h[SUBLANES - 1:SUBLANES, :], (SUBLANES, width))
        for sl, h in out:
            h_scr[sl, :] = h
        return carry

    lax.fori_loop(0, seq // SUBLANES // group, tiles, jnp.zeros((SUBLANES, width), F32))


def _scan_backward(c_scr, d_scr, g_scr, seq):
    width = c_scr.shape[1]
    rows = _tile_rows(width)
    n_tiles = seq // SUBLANES

    group = min(SCAN_UNROLL, n_tiles)

    def within_tile(j):
        sl = pl.ds(pl.multiple_of(j * SUBLANES, SUBLANES), SUBLANES)
        c = c_scr[sl, :]
        g = d_scr[sl, :]
        for d in (1, 2, 4):
            keep = rows < SUBLANES - d
            c_sh = jnp.where(keep, pltpu.roll(c, SUBLANES - d, 0), 1.0)
            g_sh = jnp.where(keep, pltpu.roll(g, SUBLANES - d, 0), 0.0)
            g = c * g_sh + g
            c = c * c_sh
        return sl, c, g

    def tiles(t, carry):
        parts = [within_tile(n_tiles - 1 - (t * group + k)) for k in range(group)]
        out = []
        for sl, c, g in parts:
            g = g + c * carry
            out.append((sl, g))
            carry = jnp.broadcast_to(g[0:1, :], (SUBLANES, width))
        for sl, g in out:
            g_scr[sl, :] = g
        return carry

    lax.fori_loop(0, n_tiles // group, tiles, jnp.zeros((SUBLANES, width), F32))


LRU_BLOCKS_PER_STEP = 2
LRU_LANES = LRU_BLOCKS_PER_STEP * LANES
LRU_STEPS = N_BLK // LRU_BLOCKS_PER_STEP


def _branch_a_fwd(z, conv_w, conv_b, wx, bx, wa, ba, lam, batch, seq):
    tokens = batch * seq

    def body(z_ref, cw_ref, cb_ref, wx_ref, bx_ref, wa_ref, ba_ref, lam_ref, ya_ref, hl_ref, kept_ref, a_scr, u_scr):
        xa = z_ref[0]
        ga = z_ref[1]
        xc = (cb_ref[...] + cw_ref[3:4, :] * xa + cw_ref[2:3, :] * _shift_down(xa, 1)
              + cw_ref[1:2, :] * _shift_down(xa, 2) + cw_ref[0:1, :] * _shift_down(xa, 3))
        gi = _sigmoid(_block_diag(xc, wx_ref) + bx_ref[...])
        gr = _sigmoid(_block_diag(xc, wa_ref) + ba_ref[...])
        a, mult, _ = _lru_decay(gr, _softplus(-lam_ref[...]))
        kept_ref[0], kept_ref[1], kept_ref[2] = xc, gi, gr
        a_scr[...] = a
        u_scr[...] = mult * gi * xc
        _scan_forward(a_scr, u_scr, hl_ref, seq)
        ya_ref[...] = (hl_ref[...] * (ga * _sigmoid(ga))).astype(_MXU_DTYPE)

    blk = pl.BlockSpec((seq, LRU_LANES), lambda b, c: (b, c))
    vec = pl.BlockSpec((1, LRU_LANES), lambda b, c: (0, c))
    mat = pl.BlockSpec((LRU_BLOCKS_PER_STEP, LANES, LANES), lambda b, c: (c, 0, 0))
    return pl.pallas_call(
        body, name="branch_a_fwd",
        grid=(batch, LRU_STEPS),
        in_specs=[pl.BlockSpec((2, seq, LRU_LANES), lambda b, c: (2, b, c)),
                  pl.BlockSpec((CONV_WIDTH, LRU_LANES), lambda b, c: (0, c)), vec, mat, vec, mat, vec, vec],
        out_specs=[blk, blk, pl.BlockSpec((3, seq, LRU_LANES), lambda b, c: (0, b, c))],
        out_shape=[jax.ShapeDtypeStruct((tokens, D_MODEL), _MXU_DTYPE), jax.ShapeDtypeStruct((tokens, D_MODEL), F32),
                   jax.ShapeDtypeStruct((3, tokens, D_MODEL), F32)],
        scratch_shapes=[pltpu.VMEM((seq, LRU_LANES), F32), pltpu.VMEM((seq, LRU_LANES), F32)],
        compiler_params=_params(("parallel", "parallel")),
    )(z, conv_w, conv_b, wx, bx, wa, ba, lam)


def _branch_a_bwd(z, hl, kept, dya, dz, conv_w, wx, wa, lam, batch, seq):
    def body(z_ref, hl_ref, kept_ref, dya_ref, dz_in_ref, cw_ref, wx_ref, wa_ref, lam_ref,
             dz_ref, dcw_ref, dcb_ref, dwx_ref, dbx_ref, dwa_ref, dba_ref, dlam_ref, c_scr, d_scr):
        del dz_in_ref
        g_scr = d_scr
        xa = z_ref[0]
        ga = z_ref[1]
        hl = hl_ref[...]
        dya = dya_ref[...]
        xc, gi, gr = kept_ref[0], kept_ref[1], kept_ref[2]
        sp = _softplus(-lam_ref[...])
        a, mult, inv_mult = _lru_decay(gr, sp)
        sga = _sigmoid(ga)
        dz_ref[1] = (dya * hl * (sga * (1.0 + ga * (1.0 - sga)))).astype(_MXU_DTYPE)
        c_scr[...] = _shift_up(a, 1)
        d_scr[...] = dya * (ga * sga)
        _scan_backward(c_scr, d_scr, g_scr, seq)
        g = g_scr[...]
        da = g * _shift_down(hl, 1)
        dmult = g * gi * xc
        dgi = g * mult * xc
        dxc = g * mult * gi
        dlog_a = da * a - dmult * (a * a) * inv_mult
        dgr = dlog_a * (-LRU_C) * sp
        dsp = jnp.sum(dlog_a * gr, axis=0, keepdims=True) * (-LRU_C)
        dlam = -dsp * _sigmoid(-lam_ref[...])
        dpi = dgi * gi * (1.0 - gi)
        dpr = dgr * gr * (1.0 - gr)
        dxc = dxc + _block_diag(dpi, wx_ref, transposed=True) + _block_diag(dpr, wa_ref, transposed=True)
        dwx = jnp.stack([_mm_tn(xk, dk) for xk, dk in zip(_lane_blocks(xc), _lane_blocks(dpi))])
        dwa = jnp.stack([_mm_tn(xk, dk) for xk, dk in zip(_lane_blocks(xc), _lane_blocks(dpr))])
        dbx = jnp.sum(dpi, axis=0, keepdims=True)
        dba = jnp.sum(dpr, axis=0, keepdims=True)
        ahead = [dxc if k == CONV_WIDTH - 1 else _shift_up(dxc, CONV_WIDTH - 1 - k) for k in range(CONV_WIDTH)]
        dxa = sum(cw_ref[k:k + 1, :] * ahead[k] for k in range(CONV_WIDTH))
        dz_ref[0] = dxa.astype(_MXU_DTYPE)
        dcb = jnp.sum(dxc, axis=0, keepdims=True)
        dcw = [jnp.sum(ahead[k] * xa, axis=0, keepdims=True) for k in range(CONV_WIDTH)]

        @pl.when(pl.program_id(1) == 0)
        def _():
            for k in range(CONV_WIDTH):
                dcw_ref[k:k + 1, :] = dcw[k]
            dcb_ref[...] = dcb
            dwx_ref[...] = dwx
            dbx_ref[...] = dbx
            dwa_ref[...] = dwa
            dba_ref[...] = dba
            dlam_ref[...] = dlam

        @pl.when(pl.program_id(1) != 0)
        def _():
            for k in range(CONV_WIDTH):
                dcw_ref[k:k + 1, :] += dcw[k]
            dcb_ref[...] += dcb
            dwx_ref[...] += dwx
            dbx_ref[...] += dbx
            dwa_ref[...] += dwa
            dba_ref[...] += dba
            dlam_ref[...] += dlam

    tokens = batch * seq
    blk = pl.BlockSpec((seq, LRU_LANES), lambda c, b: (b, c))
    vec = pl.BlockSpec((1, LRU_LANES), lambda c, b: (0, c))
    mat = pl.BlockSpec((LRU_BLOCKS_PER_STEP, LANES, LANES), lambda c, b: (c, 0, 0))
    vec_shape = jax.ShapeDtypeStruct((1, D_MODEL), F32)
    mat_shape = jax.ShapeDtypeStruct((N_BLK, LANES, LANES), F32)
    return pl.pallas_call(
        body, name="branch_a_bwd",
        grid=(LRU_STEPS, batch),
        in_specs=[pl.BlockSpec((2, seq, LRU_LANES), lambda c, b: (2, b, c)), blk,
                  pl.BlockSpec((3, seq, LRU_LANES), lambda c, b: (0, b, c)), blk,
                  pl.BlockSpec(memory_space=pl.ANY),
                  pl.BlockSpec((CONV_WIDTH, LRU_LANES), lambda c, b: (0, c)), mat, mat, vec],
        out_specs=[pl.BlockSpec((2, seq, LRU_LANES), lambda c, b: (2, b, c)),
                   pl.BlockSpec((CONV_WIDTH, LRU_LANES), lambda c, b: (0, c)), vec, mat, vec, mat, vec, vec],
        out_shape=[jax.ShapeDtypeStruct((N_GROUPS, tokens, D_MODEL), _MXU_DTYPE),
                   jax.ShapeDtypeStruct((CONV_WIDTH, D_MODEL), F32), vec_shape, mat_shape, vec_shape, mat_shape,
                   vec_shape, vec_shape],
        scratch_shapes=[pltpu.VMEM((seq, LRU_LANES), F32)] * 2,
        input_output_aliases={4: 0},
        compiler_params=_params(("parallel", "arbitrary"), vmem=VMEM_LIMIT_BIG),
    )(z, hl, kept, dya, dz, conv_w, wx, wa, lam)


def _chunk_masks(transposed=False):
    r = lax.broadcasted_iota(jnp.int32, (CHUNK, CHUNK), 0)
    c = lax.broadcasted_iota(jnp.int32, (CHUNK, CHUNK), 1)
    return r <= c if transposed else r >= c


def _row_blocks(seq, fn):
    block = min(256, seq)

    def trip(i, carry):
        fn(pl.ds(pl.multiple_of(i * block, block), block))
        return carry

    lax.fori_loop(0, seq // block, trip, 0)


def _hgrn_prepare(z_ref, lb_ref, f_scr, logf_scr, qh_scr, seq):
    lb = _sigmoid(lb_ref[0:1, :] - lb_ref[1:2, :])

    def block(rows):
        q = z_ref[0, rows, :]
        f = lb + (1.0 - lb) * _sigmoid(z_ref[1, rows, :])
        f_scr[rows, :] = f
        logf_scr[rows, :] = jnp.log(f)
        qh_scr[rows, :] = q * _sigmoid(q)

    _row_blocks(seq, block)
    return lb


def _cumsum_rows(x, reverse=False):
    shift = _shift_up if reverse else _shift_down
    d = 1
    while d < x.shape[0]:
        x = x + shift(x, d)
        d *= 2
    return x


def _lane_mean(x):
    return jnp.mean(x, axis=-1, keepdims=True)


def _token_contractions(lhs_scr, rhs_scr, out_ref, seq):
    rows_id = lax.broadcasted_iota(jnp.int32, (LANES, LANES), 0)

    def transposed(p):
        rows = pl.ds(pl.multiple_of(p * LANES, LANES), LANES)
        return jnp.transpose(lhs_scr[rows, :]).astype(_MXU_DTYPE), rhs_scr[rows, :]

    def contract(p, s):
        lhs_t, rhs = s
        return (_mm(lhs_t, jnp.where(rows_id < CHUNK, rhs, 0.0)), _mm(lhs_t, jnp.where(rows_id >= CHUNK, rhs, 0.0)))

    def store(p, out):
        out_ref[2 * p] = out[0]
        out_ref[2 * p + 1] = out[1]

    _independent_trips(seq // LANES, [transposed, contract], store)


def _chunk_rows(c):
    return pl.ds(pl.multiple_of(c * CHUNK, CHUNK), CHUNK)


def _chunk_terms(c, z_ref, f_scr, qh_scr, b_scr):
    rows = _chunk_rows(c)
    b = b_scr[rows, :]
    b_mid = b_scr[pl.ds(c * CHUNK + CHUNK // 2, 1), :]
    b_last = b_scr[pl.ds(c * CHUNK + CHUNK - 1, 1), :]
    qh = qh_scr[rows, :]
    k = 1.0 - f_scr[rows, :]
    v = z_ref[2, rows, :]
    e_q = jnp.exp(b - b_mid) * HG_SCALE
    e_k = jnp.exp(b_mid - b)
    e_qi = jnp.exp(b) * HG_SCALE
    e_ks = jnp.exp(b_last - b)
    decay = jnp.exp(b_last)
    return rows, qh, k, v, e_q, e_k, e_qi, e_ks, decay


def _independent_trips(n, stages, store, group=CHUNKS_IN_FLIGHT):
    stages = stages if isinstance(stages, (list, tuple)) else [stages]
    group = min(group, n)

    def trip(g, carry):
        ids = [g * group + i for i in range(group)]
        state = [stages[0](c) for c in ids]
        for stage in stages[1:]:
            state = [stage(c, s) for c, s in zip(ids, state)]
        for c, s in zip(ids, state):
            store(c, s)
        return carry

    lax.fori_loop(0, n // group, trip, 0)


def _branch_b_fwd(z, lb_logits, hg_g, batch, seq):
    tokens = batch * seq
    n_chunks = seq // CHUNK

    def body(z_ref, lb_ref, g_ref, yb_ref, st_ref, kept_ref, logf_scr, o_scr, qi_scr, ks_scr, dec_scr):
        f_scr, qh_scr, b_scr, o_kept = (kept_ref.at[k] for k in range(4))
        _hgrn_prepare(z_ref, lb_ref, f_scr, logf_scr, qh_scr, seq)
        causal = _chunk_masks()
        gain = g_ref[...]

        def cumulate(c):
            return _cumsum_rows(logf_scr[_chunk_rows(c), :])

        def store_cumulated(c, b):
            b_scr[_chunk_rows(c), :] = b

        def scores(c):
            _, qh, k, v, e_q, e_k, e_qi, e_ks, decay = _chunk_terms(c, z_ref, f_scr, qh_scr, b_scr)
            return _mm_nt(qh * e_q, k * e_k), v, qh * e_qi, k * e_ks, decay

        def within_chunk(c, s):
            att, v, q_int, k_st, decay = s
            return _mm(jnp.where(causal, att, 0.0), v), q_int, k_st, decay

        def store_within_chunk(c, out):
            rows = _chunk_rows(c)
            o_scr[rows, :], qi_scr[rows, :], ks_scr[rows, :], dec_scr[pl.ds(c, 1), :] = out

        def carry_state(c, state_t):
            update = st_ref[c]
            st_ref[c] = state_t
            return state_t * dec_scr[pl.ds(c, 1), :] + update

        def finish(c):
            rows = _chunk_rows(c)
            o = o_scr[rows, :] + _mm_nt(qi_scr[rows, :], st_ref[c])
            r = lax.rsqrt(_lane_mean(o * o) + EPS)
            gb = z_ref[3, rows, :]
            return (((o * r) * gain) * (gb * _sigmoid(gb))).astype(_MXU_DTYPE), o

        def store_finished(c, out):
            yb_ref[_chunk_rows(c), :], o_kept[_chunk_rows(c), :] = out

        _independent_trips(n_chunks, cumulate, store_cumulated, FWD_CHUNKS_IN_FLIGHT)
        _independent_trips(n_chunks, [scores, within_chunk], store_within_chunk, FWD_CHUNKS_IN_FLIGHT)
        _token_contractions(z_ref.at[2], ks_scr, st_ref, seq)
        lax.fori_loop(0, n_chunks, carry_state, jnp.zeros((LANES, LANES), F32))
        _independent_trips(n_chunks, finish, store_finished, FWD_CHUNKS_IN_FLIGHT)

    seq_buf = pltpu.VMEM((seq, LANES), F32)
    return pl.pallas_call(
        body, name="branch_b_fwd",
        grid=(batch, N_BLK),
        in_specs=[pl.BlockSpec((4, seq, LANES), lambda b, h: (0, b, h)),
                  pl.BlockSpec((2, LANES), lambda b, h: (0, h)),
                  pl.BlockSpec((1, LANES), lambda b, h: (0, 0))],
        out_specs=[pl.BlockSpec((seq, LANES), lambda b, h: (b, h)),
                   pl.BlockSpec((None, n_chunks, LANES, LANES), lambda b, h: (b * N_BLK + h, 0, 0, 0)),
                   pl.BlockSpec((4, seq, LANES), lambda b, h: (0, b, h))],
        out_shape=[jax.ShapeDtypeStruct((tokens, D_MODEL), _MXU_DTYPE),
                   jax.ShapeDtypeStruct((batch * N_BLK, n_chunks, LANES, LANES), F32),
                   jax.ShapeDtypeStruct((4, tokens, D_MODEL), F32)],
        scratch_shapes=[seq_buf] * 4 + [pltpu.VMEM((n_chunks, LANES), F32)],
        compiler_params=_params(("parallel", "parallel")),
    )(z, lb_logits, hg_g)


def _branch_b_bwd(z, states, kept, dyb, dz, lb_logits, hg_g, batch, seq):
    n_chunks = seq // CHUNK

    def body(z_ref, st_ref, kept_ref, dyb_ref, dz_in_ref, lb_ref, g_ref, dz_ref, dlog_ref, dg_ref,
             do_scr, qi_scr, dqh_scr, df_scr, dec_scr, dgp_scr, dlb_scr, dst_scr):
        del dz_in_ref
        f_scr, qh_scr, b_scr, o_kept = (kept_ref.at[k] for k in range(4))
        first = (pl.program_id(0) == 0) & (pl.program_id(1) == 0)
        lb = _sigmoid(lb_ref[0:1, :] - lb_ref[1:2, :])
        causal = _chunk_masks()
        anti_causal = _chunk_masks(transposed=True)
        gain = g_ref[...]

        @pl.when(first)
        def _():
            dg_ref[...] = jnp.zeros_like(dg_ref)

        @pl.when(pl.program_id(1) == 0)
        def _():
            dlb_scr[...] = jnp.zeros_like(dlb_scr)

        def output_gradient(c):
            rows = _chunk_rows(c)
            b = b_scr[rows, :]
            q_int = qh_scr[rows, :] * (jnp.exp(b) * HG_SCALE)
            decay = jnp.exp(b_scr[pl.ds(c * CHUNK + CHUNK - 1, 1), :])
            o = o_kept[rows, :]
            r = lax.rsqrt(_lane_mean(o * o) + EPS)
            o_n = o * r
            gb = z_ref[3, rows, :]
            sgb = _sigmoid(gb)
            dyb_c = dyb_ref[rows, :]
            d_ong = dyb_c * (gb * sgb)
            d_gb = (dyb_c * (o_n * gain) * (sgb * (1.0 + gb * (1.0 - sgb)))).astype(_MXU_DTYPE)
            d_gain = jnp.sum(d_ong * o_n, axis=0, keepdims=True)
            d_on = d_ong * gain
            return d_gb, d_gain, r * (d_on - o_n * _lane_mean(d_on * o_n)), q_int, decay

        def store_output_gradient(c, out):
            rows = _chunk_rows(c)
            dz_ref[3, rows, :], dgp_scr[pl.ds(c, 1), :], do_scr[rows, :], qi_scr[rows, :], dec_scr[pl.ds(c, 1), :] = out

        def carry_state_gradient(cc, d_state_t):
            c = n_chunks - 1 - cc
            update = dst_scr[c]
            dst_scr[c] = d_state_t
            return d_state_t * dec_scr[pl.ds(c, 1), :] + update

        def score_gradients(c):
            rows, qh, k, v, e_q, e_k, e_qi, e_ks, decay = _chunk_terms(c, z_ref, f_scr, qh_scr, b_scr)
            state_t = st_ref[c]
            d_state_t = dst_scr[c]
            d_o = do_scr[rows, :]
            q_in, k_in, q_int, k_st = qh * e_q, k * e_k, qh * e_qi, k * e_ks
            first = (_mm_nt(k_in, q_in), _mm_nt(d_o, v), _mm_nt(v, d_o), _mm_nt(k_st, d_state_t), _mm(d_o, state_t),
                     _mm(v, d_state_t))
            d_decay = jnp.sum(state_t * d_state_t, axis=0, keepdims=True)
            return first, d_o, q_in, k_in, q_int, k_st, e_q, e_k, e_qi, e_ks, decay, d_decay

        def input_gradients(c, s):
            (att_t, d_att, d_att_t, dv_inter, dq_int, dk_st), d_o, q_in, k_in, q_int, k_st, e_q, e_k, e_qi, e_ks, decay, d_decay = s
            rows = _chunk_rows(c)
            d_v = _mm(jnp.where(anti_causal, att_t, 0.0), d_o) + dv_inter
            dq_in = _mm(jnp.where(causal, d_att, 0.0), k_in)
            dk_in = _mm(jnp.where(anti_causal, d_att_t, 0.0), q_in)
            d_k = dk_in * e_k + dk_st * e_ks
            kk = dk_st * k_st
            d_b = dq_in * q_in + dq_int * q_int - dk_in * k_in - kk
            d_b_last = jnp.sum(kk, axis=0, keepdims=True) + decay * d_decay
            d_logf = _cumsum_rows(d_b, reverse=True) + d_b_last
            return d_v.astype(_MXU_DTYPE), dq_in * e_q + dq_int * e_qi, d_logf / f_scr[rows, :] - d_k

        def store_input_gradients(c, out):
            rows = _chunk_rows(c)
            dz_ref[2, rows, :], dqh_scr[rows, :], df_scr[rows, :] = out

        def input_activations(rows):
            q = z_ref[0, rows, :]
            sq = _sigmoid(q)
            dz_ref[0, rows, :] = (dqh_scr[rows, :] * (sq * (1.0 + q * (1.0 - sq)))).astype(_MXU_DTYPE)
            sg = _sigmoid(z_ref[1, rows, :])
            d_f = df_scr[rows, :]
            dz_ref[1, rows, :] = (d_f * (1.0 - lb) * sg * (1.0 - sg)).astype(_MXU_DTYPE)
            dlb_scr[...] += jnp.sum(d_f * (1.0 - sg), axis=0, keepdims=True)

        _independent_trips(n_chunks, output_gradient, store_output_gradient)
        _token_contractions(do_scr, qi_scr, dst_scr, seq)
        lax.fori_loop(0, n_chunks, carry_state_gradient, jnp.zeros((LANES, LANES), F32))
        _independent_trips(n_chunks, [score_gradients, input_gradients], store_input_gradients)
        dg_ref[...] += jnp.sum(dgp_scr[...], axis=0, keepdims=True)
        _row_blocks(seq, input_activations)
        d_l0 = dlb_scr[...] * lb * (1.0 - lb)
        dlog_ref[0:1, :] = d_l0
        dlog_ref[1:2, :] = -d_l0

    tokens = batch * seq
    seq_buf = pltpu.VMEM((seq, LANES), F32)
    chunk_rows = pltpu.VMEM((n_chunks, LANES), F32)
    return pl.pallas_call(
        body, name="branch_b_bwd",
        grid=(N_BLK, batch),
        in_specs=[pl.BlockSpec((4, seq, LANES), lambda h, b: (0, b, h)),
                  pl.BlockSpec((None, n_chunks, LANES, LANES), lambda h, b: (b * N_BLK + h, 0, 0, 0)),
                  pl.BlockSpec((4, seq, LANES), lambda h, b: (0, b, h)),
                  pl.BlockSpec((seq, LANES), lambda h, b: (b, h)),
                  pl.BlockSpec(memory_space=pl.ANY),
                  pl.BlockSpec((2, LANES), lambda h, b: (0, h)),
                  pl.BlockSpec((1, LANES), lambda h, b: (0, 0))],
        out_specs=[pl.BlockSpec((4, seq, LANES), lambda h, b: (0, b, h)),
                   pl.BlockSpec((2, LANES), lambda h, b: (0, h)),
                   pl.BlockSpec((1, LANES), lambda h, b: (0, 0))],
        out_shape=[jax.ShapeDtypeStruct((N_GROUPS, tokens, D_MODEL), _MXU_DTYPE),
                   jax.ShapeDtypeStruct((2, D_MODEL), F32),
                   jax.ShapeDtypeStruct((1, LANES), F32)],
        scratch_shapes=[seq_buf] * 4 + [chunk_rows, chunk_rows, pltpu.VMEM((1, LANES), F32),
                                        pltpu.VMEM((n_chunks, LANES, LANES), F32)],
        input_output_aliases={4: 0},
        compiler_params=_params(("arbitrary", "arbitrary")),
    )(z, states, kept, dyb, dz, lb_logits, hg_g)


def _merge_tail(ya, yb, z, x2d, tgt2d, b_merge, final_g, pa, pb, wo):
    tokens, d = x2d.shape
    tm = min(TAIL_TOKENS, tokens)
    n_tiles = tokens // tm

    def body(ya_ref, yb_ref, z_ref, x_ref, t_ref, bm_ref, fg_ref, pa_hbm, pb_hbm, wo_hbm,
             dya_ref, dyb_ref, dx2_ref, dz_ref, loss_ref, dfg_ref, dbm_ref, dpa_hbm, dpb_hbm, dwo_hbm,
             pa_s, pb_s, wo_s, dpa_s, dpb_s, dwo_s, sems):
        i = pl.program_id(0)

        def together(pairs):
            copies = [pltpu.make_async_copy(src, dst, sems.at[k]) for k, (src, dst) in enumerate(pairs)]
            for cp in copies:
                cp.start()
            for cp in copies:
                cp.wait()

        @pl.when(i == 0)
        def _():
            together([(pa_hbm, pa_s), (pb_hbm, pb_s), (wo_hbm, wo_s)])
            dpa_s[...] = jnp.zeros_like(dpa_s)
            dpb_s[...] = jnp.zeros_like(dpb_s)
            dwo_s[...] = jnp.zeros_like(dwo_s)
            loss_ref[...] = jnp.zeros_like(loss_ref)
            dfg_ref[...] = jnp.zeros_like(dfg_ref)
            dbm_ref[...] = jnp.zeros_like(dbm_ref)

        ya_t = ya_ref[...]
        yb_t = yb_ref[...]
        out_a = _mm(ya_t, pa_s[...])
        out_b = _mm(yb_t, pb_s[...])
        g_a = _sigmoid(z_ref[0] + bm_ref[:, :d])
        g_b = _sigmoid(z_ref[1] + bm_ref[:, d:])
        mixed = g_a * out_a + g_b * out_b
        x2 = x_ref[...] + _mm(mixed, wo_s[...])
        r = lax.rsqrt(jnp.mean(x2 * x2, axis=-1, keepdims=True) + EPS)
        xn = x2 * r
        fg = fg_ref[...]
        diff = xn * fg - t_ref[...]
        loss_ref[...] += jnp.sum(diff * diff) * (0.5 / d)
        dy = diff * (1.0 / d)
        dfg_ref[...] += jnp.sum(dy * xn, axis=0, keepdims=True)
        dxn = dy * fg
        dx2 = r * (dxn - xn * jnp.mean(dxn * xn, axis=-1, keepdims=True))
        dx2_ref[...] = dx2
        dmixed = _mm_nt(dx2, wo_s[...])
        dwo_s[...] += _mm_tn(mixed, dx2)
        dgm_a = dmixed * out_a * g_a * (1.0 - g_a)
        dgm_b = dmixed * out_b * g_b * (1.0 - g_b)
        dz_ref[0] = dgm_a.astype(_MXU_DTYPE)
        dz_ref[1] = dgm_b.astype(_MXU_DTYPE)
        dbm_ref[:, :d] += jnp.sum(dgm_a, axis=0, keepdims=True)
        dbm_ref[:, d:] += jnp.sum(dgm_b, axis=0, keepdims=True)
        dout_a = dmixed * g_a
        dout_b = dmixed * g_b
        dpa_s[...] += _mm_tn(ya_t, dout_a)
        dpb_s[...] += _mm_tn(yb_t, dout_b)
        dya_ref[...] = _mm_nt(dout_a, pa_s[...])
        dyb_ref[...] = _mm_nt(dout_b, pb_s[...])

        @pl.when(i == n_tiles - 1)
        def _():
            together([(dpa_s, dpa_hbm), (dpb_s, dpb_hbm), (dwo_s, dwo_hbm)])

    tile = pl.BlockSpec((tm, d), lambda i: (i, 0))
    gm = pl.BlockSpec((2, tm, d), lambda i: (3, i, 0))
    row = lambda n: pl.BlockSpec((1, n), lambda i: (0, 0))
    hbm = pl.BlockSpec(memory_space=pl.ANY)
    act = jax.ShapeDtypeStruct((tokens, d), F32)
    mat = jax.ShapeDtypeStruct((d, d), F32)
    return pl.pallas_call(
        body, name="merge_tail",
        grid=(n_tiles,),
        in_specs=[tile, tile, gm, tile, tile, row(2 * d), row(d), hbm, hbm, hbm],
        out_specs=[tile, tile, tile, gm, row(LANES), row(d), row(2 * d), hbm, hbm, hbm],
        out_shape=[act, act, act, jax.ShapeDtypeStruct((N_GROUPS, tokens, d), _MXU_DTYPE),
                   jax.ShapeDtypeStruct((1, LANES), F32), jax.ShapeDtypeStruct((1, d), F32),
                   jax.ShapeDtypeStruct((1, 2 * d), F32), mat, mat, mat],
        scratch_shapes=[pltpu.VMEM((d, d), _MXU_DTYPE)] * 3 + [pltpu.VMEM((d, d), F32)] * 3
        + [pltpu.SemaphoreType.DMA((3,))],
        compiler_params=_params(("arbitrary",)),
    )(ya, yb, z, x2d, tgt2d, b_merge, final_g, pa, pb, wo)


def _inproj_dw_exchange(h_t, dz, scatter):
    d, tokens = h_t.shape
    tm = min(CONTRACT_TOKENS, tokens)
    n_i = tokens // tm
    half = d // 2

    def body(h_ref, dz_ref, *rest):
        n_in, n_out = scatter.n_in, scatter.n_out
        dw_hbm, land_hbm = rest[n_in:n_in + 2]
        acc, local_sems, send_sems, recv_sems = rest[n_in + 2 + n_out:n_in + 6 + n_out]
        carried = scatter.copies(rest[:n_in], rest[n_in + 2:n_in + 2 + n_out], rest[n_in + 6 + n_out:])
        s, i = pl.program_id(0), pl.program_id(1)
        x, y, c, _ = _mesh_position()

        @pl.when((s == 0) & (i == 0))
        def _():
            for cp in carried:
                cp.start()

        part = _mm(h_ref[...], dz_ref[...])
        buf = acc.at[s % 2]

        @pl.when(i == 0)
        def _():
            buf[...] = part

        @pl.when(i != 0)
        def _():
            buf[...] += part

        def copies(k):
            g = _SLOT_TO_GROUP[k]
            cols = pl.ds((g % 2) * D_MODEL, D_MODEL)
            src = acc.at[k % 2]
            mine = pltpu.make_async_copy(src, dw_hbm.at[g // 2, :, cols], local_sems.at[k % 2])
            theirs = _remote(src.at[pl.ds((1 - c) * half, half), :], land_hbm.at[g // 2, :, cols],
                             send_sems, recv_sems, k, (x, y, 1 - c))
            return mine, theirs

        for k in range(N_GROUPS):
            @pl.when((s == k) & (i == n_i - 1))
            def _(k=k):
                if k > 0:
                    mine, theirs = copies(k - 1)
                    mine.wait()
                    theirs.wait_send()
                mine, theirs = copies(k)
                mine.start()
                theirs.start()
                if k == N_GROUPS - 1:
                    mine.wait()
                    theirs.wait_send()
                    for kk in range(N_GROUPS):
                        copies(kk)[1].wait_recv()
                    for cp in carried:
                        cp.wait()

    hbm = pl.BlockSpec(memory_space=pl.ANY)
    more = scatter.plumbing(first_operand=2, first_output=2)
    return pl.pallas_call(
        body, name="inproj_dw_exchange",
        grid=(N_GROUPS, n_i),
        in_specs=[pl.BlockSpec((d, tm), lambda s, i: (0, i)),
                  pl.BlockSpec((None, tm, D_MODEL), lambda s, i: (s, i, 0))] + more[1],
        out_specs=[hbm, hbm] + more[2],
        out_shape=[jax.ShapeDtypeStruct((N_SHARDS, d, 2 * D_MODEL), F32),
                   jax.ShapeDtypeStruct((N_SHARDS, half, 2 * D_MODEL), F32)] + more[3],
        scratch_shapes=[pltpu.VMEM((2, d, D_MODEL), F32), pltpu.SemaphoreType.DMA((2,)),
                        pltpu.SemaphoreType.DMA((N_GROUPS,)), pltpu.SemaphoreType.DMA((N_GROUPS,))] + more[4],
        input_output_aliases=more[5],
        compiler_params=_params(("arbitrary", "arbitrary")),
    )(h_t, dz, *more[0])


def _inproj_dx(dz, w_all, x2d, dx2, norm_g, scatter):
    tokens, d = x2d.shape
    tm = min(TAIL_TOKENS, tokens)
    n_tiles = tokens // tm

    def body(dz_ref, w_hbm, x_ref, dx2_ref, g_ref, *rest):
        n_in, n_out = scatter.n_in, scatter.n_out
        dx_ref, dg_ref = rest[n_in:n_in + 2]
        w_res, load_sems, dh_t = rest[n_in + 2 + n_out:n_in + 5 + n_out]
        copies = scatter.copies(rest[:n_in], rest[n_in + 2:n_in + 2 + n_out], rest[n_in + 5 + n_out:])
        i = pl.program_id(0)

        @pl.when(i == 0)
        def _():
            for cp in copies:
                cp.start()
            loads = [pltpu.make_async_copy(w_hbm.at[g // 2, :, pl.ds((g % 2) * D_MODEL, D_MODEL)],
                                           w_res.at[:, pl.ds(slot * D_MODEL, D_MODEL)], load_sems.at[slot])
                     for slot, g in enumerate(_SLOT_TO_GROUP)]
            for cp in loads:
                cp.start()
            dg_ref[...] = jnp.zeros_like(dg_ref)
            for slot, cp in enumerate(loads):
                cp.wait()
                part = _mm_nt(w_res[:, pl.ds(slot * D_MODEL, D_MODEL)], dz_ref[slot])
                if slot == 0:
                    dh_t[...] = part
                else:
                    dh_t[...] += part

        @pl.when(i > 0)
        def _():
            dz_all = jnp.concatenate([dz_ref[s] for s in range(N_GROUPS)], axis=1)
            dh_t[...] = _mm_nt(w_res[...], dz_all)

        dh = jnp.transpose(dh_t[...])
        x = x_ref[...]
        r = lax.rsqrt(jnp.mean(x * x, axis=-1, keepdims=True) + EPS)
        xn = x * r
        dg_ref[...] += jnp.sum(dh * xn, axis=0, keepdims=True)
        dxn = dh * g_ref[...]
        dx_ref[...] = r * (dxn - xn * jnp.mean(dxn * xn, axis=-1, keepdims=True)) + dx2_ref[...]

        @pl.when(i == n_tiles - 1)
        def _():
            for cp in copies:
                cp.wait()

    tile = pl.BlockSpec((tm, d), lambda i: (i, 0))
    hbm = pl.BlockSpec(memory_space=pl.ANY)
    more = scatter.plumbing(first_operand=5, first_output=2)
    return pl.pallas_call(
        body, name="inproj_dx", grid=(n_tiles,),
        in_specs=[pl.BlockSpec((N_GROUPS, tm, D_MODEL), lambda i: (0, i, 0)), hbm, tile, tile,
                  pl.BlockSpec((1, d), lambda i: (0, 0))] + more[1],
        out_specs=[tile, pl.BlockSpec((1, d), lambda i: (0, 0))] + more[2],
        out_shape=[jax.ShapeDtypeStruct((tokens, d), F32), jax.ShapeDtypeStruct((1, d), F32)] + more[3],
        scratch_shapes=[pltpu.VMEM((d, N_GROUPS * D_MODEL), _MXU_DTYPE), pltpu.SemaphoreType.DMA((N_GROUPS,)),
                        pltpu.VMEM((d, tm), F32)] + more[4],
        input_output_aliases=more[5],
        compiler_params=_params(("arbitrary",)),
    )(dz, w_all, x2d, dx2, norm_g, *more[0])


def _row_tile(rows, cols, itemsize=4, budget=2 * 1024 * 1024):
    tr = rows
    while tr * cols * itemsize > budget and tr % 16 == 0:
        tr //= 2
    return tr


def _cast_into_slot(a, chip, dtype, name):
    rows, cols = a.shape
    tr = _row_tile(rows, cols)

    def body(chip_ref, a_ref, o_ref):
        del chip_ref
        o_ref[...] = a_ref[...].astype(dtype)

    grid_spec = pltpu.PrefetchScalarGridSpec(
        num_scalar_prefetch=1, grid=(rows // tr,),
        in_specs=[pl.BlockSpec((tr, cols), lambda i, chip_ref: (i, 0))],
        out_specs=pl.BlockSpec((None, tr, cols), lambda i, chip_ref: (chip_ref[0], i, 0)))
    return pl.pallas_call(body, name=name, grid_spec=grid_spec,
                          out_shape=jax.ShapeDtypeStruct((N_SHARDS, rows, cols), dtype),
                          compiler_params=_params(("arbitrary",)))(chip, a)


def _sum_slots(stack, name):
    n, rows, cols = stack.shape
    tr = _row_tile(rows, cols * n)

    def body(s_ref, o_ref):
        total = s_ref[0].astype(F32)
        for k in range(1, n):
            total = total + s_ref[k].astype(F32)
        o_ref[...] = total

    return pl.pallas_call(body, name=name, grid=(rows // tr,),
                          in_specs=[pl.BlockSpec((n, tr, cols), lambda i: (0, i, 0))],
                          out_specs=pl.BlockSpec((tr, cols), lambda i: (i, 0)),
                          out_shape=jax.ShapeDtypeStruct((rows, cols), F32),
                          compiler_params=_params(("parallel",)))(stack)


def _add_half(full, landed, place, name):
    n, rows, cols = full.shape
    half = rows // 2
    tr = _row_tile(half, cols)
    nb = half // tr

    def body(place_ref, a_ref, b_ref, o_ref, own_ref):
        total = (a_ref[...] + b_ref[...]).astype(_MXU_DTYPE)
        o_ref[...] = total

        @pl.when(pl.program_id(1) == place_ref[1])
        def _():
            own_ref[...] = total

    grid_spec = pltpu.PrefetchScalarGridSpec(
        num_scalar_prefetch=1, grid=(nb, n),
        in_specs=[pl.BlockSpec((None, tr, cols), lambda i, j, place_ref: (j, place_ref[0] * nb + i, 0)),
                  pl.BlockSpec((None, tr, cols), lambda i, j, place_ref: (j, i, 0))],
        out_specs=[pl.BlockSpec((None, tr, cols), lambda i, j, place_ref: (j, i, 0)),
                   pl.BlockSpec((None, tr, cols), lambda i, j, place_ref: (place_ref[1], i, 0))])
    shape = jax.ShapeDtypeStruct((n, half, cols), _MXU_DTYPE)
    return pl.pallas_call(body, name=name, grid_spec=grid_spec, out_shape=[shape, shape],
                          compiler_params=_params(("parallel", "arbitrary")))(place, full, landed)


def _adamw_update(w, grad, m, v):
    c1 = 1.0 - ADAM_B1 ** ADAM_STEP
    c2 = 1.0 - ADAM_B2 ** ADAM_STEP
    nm = ADAM_B1 * m + (1.0 - ADAM_B1) * grad
    nv = ADAM_B2 * v + (1.0 - ADAM_B2) * (grad * grad)
    return (-ADAM_LR) * ((nm / c1) / (jnp.sqrt(nv / c2) + ADAM_EPS) + ADAM_WD * w), nm, nv


def _adamw(w, g, m, v, name):
    rows, cols = w.shape
    tr = _row_tile(rows, cols, budget=1024 * 1024)

    def body(w_ref, g_ref, m_ref, v_ref, d_ref, nm_ref, nv_ref):
        d_ref[...], nm_ref[...], nv_ref[...] = _adamw_update(w_ref[...], g_ref[...], m_ref[...], v_ref[...])

    spec = pl.BlockSpec((tr, cols), lambda i: (i, 0))
    shape = jax.ShapeDtypeStruct((rows, cols), F32)
    return pl.pallas_call(body, name=name, grid=(rows // tr,), in_specs=[spec] * 4, out_specs=[spec] * 3,
                          out_shape=[shape] * 3, compiler_params=_params(("parallel",)))(w, g, m, v)


def _adamw_halves(w, g_mine, g_sibling, m, v, core, name):
    rows, cols = w.shape
    half = rows // 2
    tr = _row_tile(half, cols, budget=1024 * 1024)
    nb = half // tr

    def body(core_ref, w_ref, gm_ref, gs_ref, m_ref, v_ref, g_ref, d_ref, nm_ref, nv_ref):
        mine = pl.program_id(0) // nb == core_ref[0]
        grad = jnp.where(mine, gm_ref[...], gs_ref[...])
        g_ref[...] = grad
        d_ref[...], nm_ref[...], nv_ref[...] = _adamw_update(w_ref[...], grad, m_ref[...], v_ref[...])

    spec = pl.BlockSpec((tr, cols), lambda i, core_ref: (i, 0))
    mine_spec = pl.BlockSpec((tr, cols), lambda i, core_ref: (jnp.where(i // nb == core_ref[0], i % nb, 0), 0))
    sibling_spec = pl.BlockSpec((tr, cols), lambda i, core_ref: (jnp.where(i // nb == core_ref[0], 0, i % nb), 0))
    grid_spec = pltpu.PrefetchScalarGridSpec(num_scalar_prefetch=1, grid=(rows // tr,),
                                             in_specs=[spec, mine_spec, sibling_spec, spec, spec], out_specs=[spec] * 4)
    shape = jax.ShapeDtypeStruct((rows, cols), F32)
    return pl.pallas_call(body, name=name, grid_spec=grid_spec, out_shape=[shape] * 4,
                          compiler_params=_params(("parallel",)))(core, w, g_mine, g_sibling, m, v)


def _local_step(x, loss_target, gather, reduction, b_merge, conv_b, rg_wx, rg_bx, rg_wa, rg_ba, rg_lambda,
                hg_lb_logits, hg_norm_g, norm_g, final_norm_g):
    batch, seq, d = x.shape
    x2d = x.reshape(batch * seq, d)
    tgt2d = loss_target.reshape(batch * seq, d)
    z, h_t, (w_all, pa, pb, wo), cw_all = _inproj_fwd_gather(x2d, norm_g, *gather)
    pa, pb, wo = (t.reshape(d, d) for t in (pa, pb, wo))
    conv_w = jnp.transpose(cw_all, (1, 0, 2)).reshape(CONV_WIDTH, d)
    lru = (conv_w, conv_b, rg_wx, rg_bx, rg_wa, rg_ba, rg_lambda)
    ya, hl, kept = _branch_a_fwd(z, *lru, batch, seq)
    yb, states, kept_b = _branch_b_fwd(z, hg_lb_logits, hg_norm_g, batch, seq)
    dya, dyb, dx2, dz, loss, d_final_g, d_b_merge, d_pa, d_pb, d_wo = _merge_tail(
        ya, yb, z, x2d, tgt2d, b_merge, final_norm_g, pa, pb, wo)
    dz, d_lb_logits, d_hg_g = _branch_b_bwd(z, states, kept_b, dyb, dz, hg_lb_logits, hg_norm_g, batch, seq)
    dz, d_conv_w, d_conv_b, d_wx, d_bx, d_wa, d_ba, d_lam = _branch_a_bwd(
        z, hl, kept, dya, dz, conv_w, rg_wx, rg_wa, rg_lambda, batch, seq)
    small = dict(b_merge=d_b_merge, conv_w=d_conv_w, conv_b=d_conv_b, rg_wx=d_wx, rg_bx=d_bx, rg_wa=d_wa,
                 rg_ba=d_ba, rg_lambda=d_lam, hg_lb_logits=d_lb_logits, hg_norm_g=d_hg_g,
                 norm_g=jnp.zeros((1, d), F32), final_norm_g=d_final_g)
    first, second = reduction
    d_w_in, landed_w_in, *scattered_first = _inproj_dw_exchange(h_t, dz, first((d_pa, d_pb, d_wo), small))
    grad_x, d_norm_g, *scattered_second = _inproj_dx(dz, w_all, x2d, dx2, norm_g, scatter=second(d_w_in, landed_w_in))
    return loss[0, 0], grad_x.reshape(batch, seq, d), d_norm_g, (scattered_first, scattered_second)


_SMALL_ORDER = ("b_merge", "conv_w", "conv_b", "rg_wx", "rg_bx", "rg_wa", "rg_ba", "rg_lambda", "hg_lb_logits",
                "hg_norm_g", "norm_g", "final_norm_g")
N_DEV = 8
PIECE_ROWS = 272


def _pack_small(tree):
    flat = jnp.concatenate([tree[k].reshape(-1) for k in _SMALL_ORDER])
    flat = jnp.pad(flat, (0, N_DEV * PIECE_ROWS * LANES - flat.shape[0]))
    return flat.reshape(N_DEV * PIECE_ROWS, LANES)


def _unpack_small(packed, like):
    flat = packed.reshape(-1)
    out, pos = {}, 0
    for k in _SMALL_ORDER:
        n = like[k].size
        out[k] = flat[pos:pos + n].reshape(like[k].shape)
        pos += n
    return out


def _mesh_position():
    x, y, c = lax.axis_index("x"), lax.axis_index("y"), lax.axis_index("c")
    other_chips = [(1 - x, y), (x, 1 - y), (1 - x, 1 - y)]
    return x, y, c, other_chips


def _other_devices(x, y, c):
    flips = [(fx, fy, fc) for fx in (0, 1) for fy in (0, 1) for fc in (0, 1) if (fx, fy, fc) != (0, 0, 0)]
    return [(jnp.where(fx, 1 - x, x), jnp.where(fy, 1 - y, y), jnp.where(fc, 1 - c, c)) for fx, fy, fc in flips]


def _remote(src, dst, send_sems, recv_sems, k, device):
    return pltpu.make_async_remote_copy(src_ref=src, dst_ref=dst, send_sem=send_sems.at[k], recv_sem=recv_sems.at[k],
                                        device_id=device, device_id_type=MESH)


def _exchange_halves(bigs, small):
    n_big = len(bigs)
    n_sem = n_big + N_DEV - 1

    def body(*refs):
        srcs, small_src = refs[:n_big], refs[n_big]
        outs, small_out = refs[n_big + 1:2 * n_big + 1], refs[2 * n_big + 1]
        send_sems, recv_sems, local_sem = refs[2 * n_big + 2:]
        x, y, c, _ = _mesh_position()
        me, sibling = 4 * x + 2 * y + c, (x, y, 1 - c)
        mine = pltpu.make_async_copy(small_src.at[pl.ds(me * PIECE_ROWS, PIECE_ROWS), :], small_out.at[me], local_sem)
        mine.start()
        copies = []
        for a in range(n_big):
            hs = srcs[a].shape[1] // 2
            copies.append(_remote(srcs[a].at[:, pl.ds((1 - c) * hs, hs), :], outs[a], send_sems, recv_sems, a, sibling))
        for k, (px, py, pc) in enumerate(_other_devices(x, y, c)):
            piece = small_src.at[pl.ds((4 * px + 2 * py + pc) * PIECE_ROWS, PIECE_ROWS), :]
            copies.append(_remote(piece, small_out.at[me], send_sems, recv_sems, n_big + k, (px, py, pc)))
        for cp in copies:
            cp.start()
        for cp in copies:
            cp.wait()
        mine.wait()

    hbm = pl.BlockSpec(memory_space=pl.ANY)
    out_shape = [jax.ShapeDtypeStruct((g.shape[0], g.shape[1] // 2, g.shape[2]), F32) for g in bigs]
    out_shape.append(jax.ShapeDtypeStruct((N_DEV, PIECE_ROWS, LANES), F32))
    return pl.pallas_call(
        body, name="exchange_halves",
        in_specs=[hbm] * (n_big + 1), out_specs=[hbm] * (n_big + 1), out_shape=out_shape,
        scratch_shapes=[pltpu.SemaphoreType.DMA((n_sem,)), pltpu.SemaphoreType.DMA((n_sem,)), pltpu.SemaphoreType.DMA],
    )(*bigs, small)


class _Scatter:
    def __init__(self, bigs, by_chip, small=None):
        self.bigs, self.by_chip, self.small = list(bigs), list(by_chip), small
        self.n_big = len(self.bigs)
        self.n_in = 2 * self.n_big + (small is not None)
        self.n_out = self.n_big + (small is not None)
        self.n_scratch = 2 + (small is not None)

    def plumbing(self, first_operand, first_output):
        hbm = pl.BlockSpec(memory_space=pl.ANY)
        n_sem = 3 * self.n_big + (N_DEV - 1 if self.small is not None else 0)
        operands = self.bigs + self.by_chip + ([self.small] if self.small is not None else [])
        out_shapes = [jax.ShapeDtypeStruct(g.shape, g.dtype) for g in self.by_chip]
        scratch = [pltpu.SemaphoreType.DMA((n_sem,)), pltpu.SemaphoreType.DMA((n_sem,))]
        if self.small is not None:
            out_shapes.append(jax.ShapeDtypeStruct((N_DEV, PIECE_ROWS, LANES), F32))
            scratch.append(pltpu.SemaphoreType.DMA)
        aliases = {first_operand + self.n_big + a: first_output + a for a in range(self.n_big)}
        return operands, [hbm] * self.n_in, [hbm] * self.n_out, out_shapes, scratch, aliases

    def copies(self, in_refs, out_refs, scratch_refs):
        srcs, outs = in_refs[:self.n_big], out_refs[:self.n_big]
        send_sems, recv_sems = scratch_refs[:2]
        x, y, c, chips = _mesh_position()
        chip, me = 2 * x + y, 4 * x + 2 * y + c
        copies = []
        for a in range(self.n_big):
            for j, (cx, cy) in enumerate(chips):
                copies.append(_remote(srcs[a].at[2 * cx + cy], outs[a].at[chip], send_sems, recv_sems, 3 * a + j,
                                      (cx, cy, c)))
        if self.small is not None:
            small_src, small_out = in_refs[2 * self.n_big], out_refs[self.n_big]
            copies.append(pltpu.make_async_copy(small_src, small_out.at[me], scratch_refs[2]))
            for k, peer in enumerate(_other_devices(x, y, c)):
                copies.append(_remote(small_src, small_out.at[me], send_sems, recv_sems, 3 * self.n_big + k, peer))
        return copies


def _swap_halves(halves, vec):
    n_big = len(halves)

    def body(*refs):
        srcs, vec_src = refs[:n_big], refs[n_big]
        outs, vec_out = refs[n_big + 1:2 * n_big + 1], refs[2 * n_big + 1]
        send_sems, recv_sems, local_sem = refs[2 * n_big + 2:]
        x, y, c, _ = _mesh_position()
        me = 4 * x + 2 * y + c
        copies = [pltpu.make_async_copy(vec_src, vec_out.at[me], local_sem)]
        copies += [_remote(srcs[a], outs[a], send_sems, recv_sems, a, (x, y, 1 - c)) for a in range(n_big)]
        copies += [_remote(vec_src, vec_out.at[me], send_sems, recv_sems, n_big + k, peer)
                   for k, peer in enumerate(_other_devices(x, y, c))]
        for cp in copies:
            cp.start()
        for cp in copies:
            cp.wait()

    hbm = pl.BlockSpec(memory_space=pl.ANY)
    n_sem = n_big + N_DEV - 1
    return pl.pallas_call(
        body, name="swap_halves",
        in_specs=[hbm] * (n_big + 1), out_specs=[hbm] * (n_big + 1),
        out_shape=[jax.ShapeDtypeStruct(h.shape, F32) for h in halves] + [jax.ShapeDtypeStruct((N_DEV,) + vec.shape, F32)],
        scratch_shapes=[pltpu.SemaphoreType.DMA((n_sem,)), pltpu.SemaphoreType.DMA((n_sem,)), pltpu.SemaphoreType.DMA],
    )(*halves, vec)


def kernel(x, w_in, b_merge, conv_w, conv_b, rg_wx, rg_bx, rg_wa, rg_ba, rg_lambda, hg_lb_logits, hg_norm_g, proj_a, proj_b, w_out, norm_g, final_norm_g, loss_target, m_w_in, m_b_merge, m_conv_w, m_conv_b, m_rg_wx, m_rg_bx, m_rg_wa, m_rg_ba, m_rg_lambda, m_hg_lb_logits, m_hg_norm_g, m_proj_a, m_proj_b, m_w_out, m_norm_g, m_final_norm_g, v_w_in, v_b_merge, v_conv_w, v_conv_b, v_rg_wx, v_rg_bx, v_rg_wa, v_rg_ba, v_rg_lambda, v_hg_lb_logits, v_hg_norm_g, v_proj_a, v_proj_b, v_w_out, v_norm_g, v_final_norm_g):
    d = D_MODEL
    weights = dict(w_in=w_in, b_merge=b_merge, conv_w=conv_w, conv_b=conv_b, rg_wx=rg_wx, rg_bx=rg_bx, rg_wa=rg_wa,
                   rg_ba=rg_ba, rg_lambda=rg_lambda, hg_lb_logits=hg_lb_logits, hg_norm_g=hg_norm_g, proj_a=proj_a,
                   proj_b=proj_b, w_out=w_out, norm_g=norm_g, final_norm_g=final_norm_g)
    m = dict(w_in=m_w_in, b_merge=m_b_merge, conv_w=m_conv_w, conv_b=m_conv_b, rg_wx=m_rg_wx, rg_bx=m_rg_bx,
             rg_wa=m_rg_wa, rg_ba=m_rg_ba, rg_lambda=m_rg_lambda, hg_lb_logits=m_hg_lb_logits, hg_norm_g=m_hg_norm_g,
             proj_a=m_proj_a, proj_b=m_proj_b, w_out=m_w_out, norm_g=m_norm_g, final_norm_g=m_final_norm_g)
    v = dict(w_in=v_w_in, b_merge=v_b_merge, conv_w=v_conv_w, conv_b=v_conv_b, rg_wx=v_rg_wx, rg_bx=v_rg_bx,
             rg_wa=v_rg_wa, rg_ba=v_rg_ba, rg_lambda=v_rg_lambda, hg_lb_logits=v_hg_lb_logits, hg_norm_g=v_hg_norm_g,
             proj_a=v_proj_a, proj_b=v_proj_b, w_out=v_w_out, norm_g=v_norm_g, final_norm_g=v_final_norm_g)
    big_names = ("w_in", "proj_a", "proj_b", "w_out")

    core = lax.axis_index("c").astype(jnp.int32).reshape(1)
    chip = (2 * lax.axis_index("x") + lax.axis_index("y")).astype(jnp.int32)

    slotted = [_cast_into_slot(weights[k][0], chip.reshape(1), _MXU_DTYPE, f"cast_{k}") for k in big_names]
    conv_slotted = _cast_into_slot(conv_w[0], chip.reshape(1), F32, "slot_conv_w")

    small_shapes = {}

    place = jnp.concatenate([core, chip.reshape(1)])

    def reduce_proj_and_small(proj_grads, small_grads):
        small_shapes.update({k: t.shape for k, t in small_grads.items()})
        bigs = [g.reshape(N_SHARDS, d // N_SHARDS, d) for g in proj_grads]
        *landed, small_landed = _exchange_halves(bigs, _pack_small(small_grads))
        sums = [_add_half(g, l, place, f"add_half_{1 + a}") for a, (g, l) in enumerate(zip(bigs, landed))]
        return _Scatter([s[0] for s in sums], [s[1] for s in sums], _sum_slots(small_landed, "sum_small"))

    def reduce_w_in(d_w_in, landed):
        partial, own_slot = _add_half(d_w_in, landed, place, "add_half_0")
        return _Scatter([partial], [own_slot])

    loss_part, grad_x, d_norm_g, ((*by_chip_proj, small_all), by_chip_w_in) = _local_step(
        x, loss_target, (slotted, conv_slotted, chip.reshape(1)), (reduce_proj_and_small, reduce_w_in),
        b_merge, conv_b, rg_wx[0], rg_bx.reshape(1, d), rg_wa[0], rg_ba.reshape(1, d), rg_lambda, hg_lb_logits,
        hg_norm_g, norm_g, final_norm_g.reshape(1, d))
    mine = [_sum_slots(s, f"sum_chips_{a}") for a, s in enumerate(by_chip_w_in + by_chip_proj)]
    late = jnp.concatenate([d_norm_g.reshape(SUBLANES, LANES), jnp.full((SUBLANES, LANES), loss_part, F32)])
    *theirs, late_parts = _swap_halves(mine, late)
    late_sum = _sum_slots(late_parts, "sum_late")
    loss = late_sum[SUBLANES, 0]
    small_red = _unpack_small(small_all, {k: jax.ShapeDtypeStruct(s, F32) for k, s in small_shapes.items()})
    small_red["norm_g"] = late_sum[:SUBLANES].reshape(1, d)

    grads, delta, new_m, new_v = {}, {}, {}, {}
    for k, g_mine, g_theirs in zip(big_names, mine, theirs):
        out = _adamw_halves(weights[k][0], g_mine, g_theirs, m[k][0], v[k][0], core, f"adamw_{k}")
        grads[k], delta[k], new_m[k], new_v[k] = (t.reshape(weights[k].shape) for t in out)
    cols = d // N_SHARDS
    g_conv = lax.dynamic_slice(small_red["conv_w"], (0, chip * cols), (CONV_WIDTH, cols))
    grads["conv_w"] = g_conv.reshape(conv_w.shape)
    dl, nm, nv = _adamw(conv_w[0], g_conv, m_conv_w[0], v_conv_w[0], "adamw_conv_w")
    delta["conv_w"], new_m["conv_w"], new_v["conv_w"] = (t.reshape(conv_w.shape) for t in (dl, nm, nv))
    rest = [k for k in _SMALL_ORDER if k != "conv_w"]
    like = {k: (weights[k] if k != "conv_w" else jnp.zeros((CONV_WIDTH, d), F32)) for k in _SMALL_ORDER}
    packs = [_pack_small({k: (t[k] if k != "conv_w" else like[k]) for k in _SMALL_ORDER}) for t in (weights, m, v)]
    g_pack = _pack_small({k: small_red[k].reshape(like[k].shape) for k in _SMALL_ORDER})
    outs = [_unpack_small(p, like) for p in _adamw(packs[0], g_pack, packs[1], packs[2], "adamw_small")]
    for k in rest:
        grads[k] = small_red[k].reshape(weights[k].shape)
        delta[k], new_m[k], new_v[k] = outs[0][k], outs[1][k], outs[2][k]

    order = ("w_in", "b_merge", "conv_w", "conv_b", "rg_wx", "rg_bx", "rg_wa", "rg_ba", "rg_lambda", "hg_lb_logits",
             "hg_norm_g", "proj_a", "proj_b", "w_out", "norm_g", "final_norm_g")
    return (loss, grad_x, *[grads[k] for k in order], *[delta[k] for k in order], *[new_m[k] for k in order],
            *[new_v[k] for k in order])
```

```python
import functools

import jax
import jax.numpy as jnp
from jax import lax
from jax.experimental import pallas as pl
from jax.experimental.pallas import tpu as pltpu

F32 = jnp.float32
_MXU_DTYPE = jnp.bfloat16

D_MODEL = 1024
LANES = 128
SUBLANES = 8
N_BLK = D_MODEL // LANES
N_GROUPS = 8
N_SHARDS = 4
CONV_WIDTH = 4
LRU_C = 8.0
CHUNK = 64
CHUNKS_IN_FLIGHT = 16
FWD_CHUNKS_IN_FLIGHT = 32
SCAN_UNROLL = 4
HG_SCALE = float(LANES) ** -0.5
EPS = 1e-6
ADAM_LR, ADAM_B1, ADAM_B2, ADAM_EPS, ADAM_WD, ADAM_STEP = 0.001, 0.9, 0.999, 1e-08, 0.01, 10
MATMUL_TOKENS = 1024
TAIL_TOKENS = 256
CONTRACT_TOKENS = 2048
VMEM_LIMIT = 56 * 1024 * 1024
VMEM_LIMIT_BIG = 60 * 1024 * 1024
MESH = pl.DeviceIdType.MESH

_SLOT_TO_GROUP = (2, 3, 4, 5, 0, 1, 6, 7)


def _mm(a, b):
    return lax.dot_general(a.astype(_MXU_DTYPE), b.astype(_MXU_DTYPE), (((1,), (0,)), ((), ())),
                           preferred_element_type=F32)


def _mm_nt(a, b):
    return lax.dot_general(a.astype(_MXU_DTYPE), b.astype(_MXU_DTYPE), (((1,), (1,)), ((), ())),
                           preferred_element_type=F32)


def _mm_tn(a, b):
    return lax.dot_general(a.astype(_MXU_DTYPE), b.astype(_MXU_DTYPE), (((0,), (0,)), ((), ())),
                           preferred_element_type=F32)


def _sigmoid(x):
    return 0.5 * jnp.tanh(0.5 * x) + 0.5


def _log1p_pos(y):
    series = y * (1.0 - y * (0.5 - y * (1.0 / 3.0 - y * 0.25)))
    return jnp.where(y < 0.01, series, jnp.log(1.0 + y))


def _softplus(x):
    return jnp.maximum(x, 0.0) + _log1p_pos(jnp.exp(-jnp.abs(x)))


def _shift_down(x, n):
    rolled = pltpu.roll(x, n, 0)
    edge = SUBLANES if (n < SUBLANES and x.shape[0] > SUBLANES) else x.shape[0]
    rows = lax.broadcasted_iota(jnp.int32, (edge, x.shape[1]), 0)
    head = jnp.where(rows >= n, rolled[:edge], 0.0)
    return head if edge == x.shape[0] else jnp.concatenate([head, rolled[edge:]], axis=0)


def _shift_up(x, n):
    size = x.shape[0]
    rolled = pltpu.roll(x, size - n, 0)
    edge = SUBLANES if (n < SUBLANES and size > SUBLANES) else size
    rows = lax.broadcasted_iota(jnp.int32, (edge, x.shape[1]), 0)
    tail = jnp.where(rows < edge - n, rolled[size - edge:], 0.0)
    return tail if edge == size else jnp.concatenate([rolled[:size - edge], tail], axis=0)


def _params(dims, vmem=VMEM_LIMIT):
    return pltpu.CompilerParams(dimension_semantics=dims, vmem_limit_bytes=vmem)


def _slot_of_group(g):
    return jnp.where(g < 2, g + 4, jnp.where(g < 6, g - 2, g))


def _inproj_fwd_gather(x2d, norm_g, slotted, conv_slotted, chip):
    tokens, d = x2d.shape
    tm = min(MATMUL_TOKENS, tokens)
    n_tiles = tokens // tm
    n_big = len(slotted)
    n_sem = 6 * (n_big + 1) + 3
    last_pass = N_GROUPS - 1

    def shard_of(k, chip_id):
        x, y = chip_id // 2, chip_id % 2
        return 2 * jnp.where(k % 2 == 1, 1 - x, x) + jnp.where(k // 2 == 1, 1 - y, y)

    def body(chip_ref, x_ref, g_ref, *rest):
        bufs, cw = rest[n_big + 1:2 * n_big + 1], rest[2 * n_big + 1]
        z_ref, ht_ref = rest[2 * n_big + 2:2 * n_big + 4]
        h_all, slab, send_sems, recv_sems, slab_sems = rest[2 * n_big + 4:]
        del chip_ref
        p, i = pl.program_id(0), pl.program_id(1)
        x, y, c, chips = _mesh_position()
        me, sibling = 2 * x + y, (x, y, 1 - c)

        pieces = [(0, 0), (0, 1)] + [(a, None) for a in range(1, n_big)]

        def half(piece, slot, which):
            a, q = pieces[piece]
            hs = bufs[a].shape[1] // 2
            cols = slice(None) if q is None else pl.ds(q * D_MODEL, D_MODEL)
            return bufs[a].at[slot, pl.ds(which * hs, hs), cols]

        def send(piece, j):
            mine = half(piece, me, c)
            return _remote(mine, mine, send_sems, recv_sems, 6 * piece + j, (chips[j][0], chips[j][1], c))

        def arrival(piece, j):
            landed = half(piece, 2 * chips[j][0] + chips[j][1], c)
            return _remote(landed, landed, send_sems, recv_sems, 6 * piece + j, (chips[j][0], chips[j][1], c))

        def passed_on(piece, j, which):
            landed = half(piece, 2 * chips[j][0] + chips[j][1], which)
            return _remote(landed, landed, send_sems, recv_sems, 6 * piece + 3 + j, sibling)

        def conv_copy(j, slot):
            return _remote(cw.at[slot], cw.at[slot], send_sems, recv_sems, 6 * len(pieces) + j,
                           (chips[j][0], chips[j][1], c))

        def land(piece, j):
            arrival(piece, j).wait_recv()
            passed_on(piece, j, c).start()
            passed_on(piece, j, 1 - c).wait_recv()

        def slab_copy(pv):
            src = bufs[0].at[shard_of(pv // 2, me), :, pl.ds((pv % 2) * D_MODEL, D_MODEL)]
            return pltpu.make_async_copy(src, slab.at[pv % 2], slab_sems.at[pv % 2])

        @pl.when((p == 0) & (i == 0))
        def _():
            for q in range(2):
                send(q, 0).start()
                send(q, 1).start()
            slab_copy(0).start()

        rows = pl.ds(pl.multiple_of(i * tm, tm), tm)

        @pl.when(p == 0)
        def _():
            xt = x_ref[...]
            r = lax.rsqrt(jnp.mean(xt * xt, axis=-1, keepdims=True) + EPS)
            h = (xt * r) * g_ref[...]
            h_all[rows, :] = h.astype(_MXU_DTYPE)
            ht_ref[...] = jnp.transpose(h).astype(_MXU_DTYPE)

        @pl.when(i == 0)
        def _():
            for pv in range(N_GROUPS):
                @pl.when(p == pv)
                def _(pv=pv):
                    slab_copy(pv).wait()

        def relay(q):
            landed = half(q, 2 * chips[q][0] + chips[q][1], c)
            to = chips[1 - q]
            return _remote(landed, landed, send_sems, recv_sems, 6 * q + 2, (to[0], to[1], c))

        landings = {1: [(0, 0)], 2: [(1, 0), (1, 1)], 3: [(0, 1)], 5: [(0, 2)], 6: [(1, 2)]}

        def end_of_pass(pv):
            for q, j in landings.get(pv - 1, []):
                land(q, j)
                if j == q:
                    relay(q).start()
            if pv - 1 == 1:
                for piece in range(2, len(pieces)):
                    for jj in range(3):
                        send(piece, jj).start()
                for jj in range(3):
                    conv_copy(jj, me).start()
            if pv - 1 in (5, 6):
                for piece in range(2, len(pieces)):
                    for jj in ((0, 1) if pv - 1 == 5 else (2,)):
                        land(piece, jj)
            slab_copy(pv).start()

        @pl.when(i == n_tiles - 1)
        def _():
            for pv in range(1, N_GROUPS):
                pl.when(p == pv - 1)(functools.partial(end_of_pass, pv))

        z_ref[...] = _mm(h_all[rows, :], slab[p % 2])

        @pl.when((p == last_pass) & (i == n_tiles - 1))
        def _():
            for j in range(3):
                conv_copy(j, 2 * chips[j][0] + chips[j][1]).wait_recv()
            for piece in range(len(pieces)):
                for j in range(3):
                    (relay(piece) if (piece < 2 and j == 2) else send(piece, j)).wait_send()
                    passed_on(piece, j, c).wait_send()
            for j in range(3):
                conv_copy(j, me).wait_send()

    def z_index(p, i, chip_ref):
        g = 2 * shard_of(p // 2, chip_ref[0]) + p % 2
        return (_slot_of_group(g), i, 0)

    def first_pass_tile(p, i, chip_ref):
        return jnp.where(p == 0, i, n_tiles - 1)

    hbm = pl.BlockSpec(memory_space=pl.ANY)
    operands = list(slotted) + [conv_slotted]
    grid_spec = pltpu.PrefetchScalarGridSpec(
        num_scalar_prefetch=1, grid=(N_GROUPS, n_tiles),
        in_specs=[pl.BlockSpec((tm, d), lambda p, i, chip_ref: (first_pass_tile(p, i, chip_ref), 0)),
                  pl.BlockSpec((1, d), lambda p, i, chip_ref: (0, 0))] + [hbm] * (n_big + 1),
        out_specs=[hbm] * (n_big + 1) + [pl.BlockSpec((None, tm, D_MODEL), z_index),
                                         pl.BlockSpec((d, tm), lambda p, i, chip_ref: (0, first_pass_tile(p, i, chip_ref)))],
        scratch_shapes=[pltpu.VMEM((tokens, d), _MXU_DTYPE), pltpu.VMEM((2, d, D_MODEL), _MXU_DTYPE),
                        pltpu.SemaphoreType.DMA((n_sem,)), pltpu.SemaphoreType.DMA((n_sem,)),
                        pltpu.SemaphoreType.DMA((2,))])
    out = pl.pallas_call(
        body, name="inproj_fwd_gather", grid_spec=grid_spec,
        out_shape=[jax.ShapeDtypeStruct(a.shape, a.dtype) for a in operands]
        + [jax.ShapeDtypeStruct((N_GROUPS, tokens, D_MODEL), F32), jax.ShapeDtypeStruct((d, tokens), _MXU_DTYPE)],
        input_output_aliases={3 + a: a for a in range(n_big + 1)},
        compiler_params=_params(("arbitrary", "arbitrary")),
    )(chip, x2d, norm_g, *operands)
    return out[n_big + 1], out[n_big + 2], out[:n_big], out[n_big]


def _lane_blocks(x):
    return [x[:, k * LANES:(k + 1) * LANES] for k in range(x.shape[1] // LANES)]


def _block_diag(x, w_ref, transposed=False):
    mm = _mm_nt if transposed else _mm
    return jnp.concatenate([mm(xk, w_ref[k]) for k, xk in enumerate(_lane_blocks(x))], axis=1)


def _lru_decay(gr, sp):
    log_a = (-LRU_C) * gr * sp
    a = jnp.exp(log_a)
    y = 2.0 * log_a
    mult_sq = jnp.where(y > -1e-3, -y * (1.0 + 0.5 * y), 1.0 - a * a)
    inv_mult = lax.rsqrt(jnp.maximum(mult_sq, 1e-37))
    return a, mult_sq * inv_mult, inv_mult


def _tile_rows(width):
    return lax.broadcasted_iota(jnp.int32, (SUBLANES, width), 0)


def _scan_forward(a_scr, u_scr, h_scr, seq):
    width = a_scr.shape[1]
    rows = _tile_rows(width)

    group = min(SCAN_UNROLL, seq // SUBLANES)

    def within_tile(j):
        sl = pl.ds(pl.multiple_of(j * SUBLANES, SUBLANES), SUBLANES)
        a = a_scr[sl, :]
        u = u_scr[sl, :]
        for d in (1, 2, 4):
            keep = rows >= d
            a_sh = jnp.where(keep, pltpu.roll(a, d, 0), 1.0)
            u_sh = jnp.where(keep, pltpu.roll(u, d, 0), 0.0)
            u = a * u_sh + u
            a = a * a_sh
        return sl, a, u

    def tiles(t, carry):
        parts = [within_tile(t * group + k) for k in range(group)]
        out = []
        for sl, a, u in parts:
            h = u + a * carry
            out.append((sl, h))
            carry = jnp.broadcast_to(h[SUBLANES - 1:SUBLANES, :], (SUBLANES, width))
        for sl, h in out:
            h_scr[sl, :] = h
        return carry

    lax.fori_loop(0, seq // SUBLANES // group, tiles, jnp.zeros((SUBLANES, width), F32))


def _scan_backward(c_scr, d_scr, g_scr, seq):
    width = c_scr.shape[1]
    rows = _tile_rows(width)
    n_tiles = seq // SUBLANES

    group = min(SCAN_UNROLL, n_tiles)

    def within_tile(j):
        sl = pl.ds(pl.multiple_of(j * SUBLANES, SUBLANES), SUBLANES)
        c = c_scr[sl, :]
        g = d_scr[sl, :]
        for d in (1, 2, 4):
            keep = rows < SUBLANES - d
            c_sh = jnp.where(keep, pltpu.roll(c, SUBLANES - d, 0), 1.0)
            g_sh = jnp.where(keep, pltpu.roll(g, SUBLANES - d, 0), 0.0)
            g = c * g_sh + g
            c = c * c_sh
        return sl, c, g

    def tiles(t, carry):
        parts = [within_tile(n_tiles - 1 - (t * group + k)) for k in range(group)]
        out = []
        for sl, c, g in parts:
            g = g + c * carry
            out.append((sl, g))
            carry = jnp.broadcast_to(g[0:1, :], (SUBLANES, width))
        for sl, g in out:
            g_scr[sl, :] = g
        return carry

    lax.fori_loop(0, n_tiles // group, tiles, jnp.zeros((SUBLANES, width), F32))


LRU_BLOCKS_PER_STEP = 2
LRU_LANES = LRU_BLOCKS_PER_STEP * LANES
LRU_STEPS = N_BLK // LRU_BLOCKS_PER_STEP


def _branch_a_fwd(z, conv_w, conv_b, wx, bx, wa, ba, lam, batch, seq):
    tokens = batch * seq

    def body(z_ref, cw_ref, cb_ref, wx_ref, bx_ref, wa_ref, ba_ref, lam_ref, ya_ref, hl_ref, kept_ref, a_scr, u_scr):
        xa = z_ref[0]
        ga = z_ref[1]
        xc = (cb_ref[...] + cw_ref[3:4, :] * xa + cw_ref[2:3, :] * _shift_down(xa, 1)
              + cw_ref[1:2, :] * _shift_down(xa, 2) + cw_ref[0:1, :] * _shift_down(xa, 3))
        gi = _sigmoid(_block_diag(xc, wx_ref) + bx_ref[...])
        gr = _sigmoid(_block_diag(xc, wa_ref) + ba_ref[...])
        a, mult, _ = _lru_decay(gr, _softplus(-lam_ref[...]))
        kept_ref[0], kept_ref[1], kept_ref[2] = xc, gi, gr
        a_scr[...] = a
        u_scr[...] = mult * gi * xc
        _scan_forward(a_scr, u_scr, hl_ref, seq)
        ya_ref[...] = (hl_ref[...] * (ga * _sigmoid(ga))).astype(_MXU_DTYPE)

    blk = pl.BlockSpec((seq, LRU_LANES), lambda b, c: (b, c))
    vec = pl.BlockSpec((1, LRU_LANES), lambda b, c: (0, c))
    mat = pl.BlockSpec((LRU_BLOCKS_PER_STEP, LANES, LANES), lambda b, c: (c, 0, 0))
    return pl.pallas_call(
        body, name="branch_a_fwd",
        grid=(batch, LRU_STEPS),
        in_specs=[pl.BlockSpec((2, seq, LRU_LANES), lambda b, c: (2, b, c)),
                  pl.BlockSpec((CONV_WIDTH, LRU_LANES), lambda b, c: (0, c)), vec, mat, vec, mat, vec, vec],
        out_specs=[blk, blk, pl.BlockSpec((3, seq, LRU_LANES), lambda b, c: (0, b, c))],
        out_shape=[jax.ShapeDtypeStruct((tokens, D_MODEL), _MXU_DTYPE), jax.ShapeDtypeStruct((tokens, D_MODEL), F32),
                   jax.ShapeDtypeStruct((3, tokens, D_MODEL), F32)],
        scratch_shapes=[pltpu.VMEM((seq, LRU_LANES), F32), pltpu.VMEM((seq, LRU_LANES), F32)],
        compiler_params=_params(("parallel", "parallel")),
    )(z, conv_w, conv_b, wx, bx, wa, ba, lam)


def _branch_a_bwd(z, hl, kept, dya, dz, conv_w, wx, wa, lam, batch, seq):
    def body(z_ref, hl_ref, kept_ref, dya_ref, dz_in_ref, cw_ref, wx_ref, wa_ref, lam_ref,
             dz_ref, dcw_ref, dcb_ref, dwx_ref, dbx_ref, dwa_ref, dba_ref, dlam_ref, c_scr, d_scr):
        del dz_in_ref
        g_scr = d_scr
        xa = z_ref[0]
        ga = z_ref[1]
        hl = hl_ref[...]
        dya = dya_ref[...]
        xc, gi, gr = kept_ref[0], kept_ref[1], kept_ref[2]
        sp = _softplus(-lam_ref[...])
        a, mult, inv_mult = _lru_decay(gr, sp)
        sga = _sigmoid(ga)
        dz_ref[1] = (dya * hl * (sga * (1.0 + ga * (1.0 - sga)))).astype(_MXU_DTYPE)
        c_scr[...] = _shift_up(a, 1)
        d_scr[...] = dya * (ga * sga)
        _scan_backward(c_scr, d_scr, g_scr, seq)
        g = g_scr[...]
        da = g * _shift_down(hl, 1)
        dmult = g * gi * xc
        dgi = g * mult * xc
        dxc = g * mult * gi
        dlog_a = da * a - dmult * (a * a) * inv_mult
        dgr = dlog_a * (-LRU_C) * sp
        dsp = jnp.sum(dlog_a * gr, axis=0, keepdims=True) * (-LRU_C)
        dlam = -dsp * _sigmoid(-lam_ref[...])
        dpi = dgi * gi * (1.0 - gi)
        dpr = dgr * gr * (1.0 - gr)
        dxc = dxc + _block_diag(dpi, wx_ref, transposed=True) + _block_diag(dpr, wa_ref, transposed=True)
        dwx = jnp.stack([_mm_tn(xk, dk) for xk, dk in zip(_lane_blocks(xc), _lane_blocks(dpi))])
        dwa = jnp.stack([_mm_tn(xk, dk) for xk, dk in zip(_lane_blocks(xc), _lane_blocks(dpr))])
        dbx = jnp.sum(dpi, axis=0, keepdims=True)
        dba = jnp.sum(dpr, axis=0, keepdims=True)
        ahead = [dxc if k == CONV_WIDTH - 1 else _shift_up(dxc, CONV_WIDTH - 1 - k) for k in range(CONV_WIDTH)]
        dxa = sum(cw_ref[k:k + 1, :] * ahead[k] for k in range(CONV_WIDTH))
        dz_ref[0] = dxa.astype(_MXU_DTYPE)
        dcb = jnp.sum(dxc, axis=0, keepdims=True)
        dcw = [jnp.sum(ahead[k] * xa, axis=0, keepdims=True) for k in range(CONV_WIDTH)]

        @pl.when(pl.program_id(1) == 0)
        def _():
            for k in range(CONV_WIDTH):
                dcw_ref[k:k + 1, :] = dcw[k]
            dcb_ref[...] = dcb
            dwx_ref[...] = dwx
            dbx_ref[...] = dbx
            dwa_ref[...] = dwa
            dba_ref[...] = dba
            dlam_ref[...] = dlam

        @pl.when(pl.program_id(1) != 0)
        def _():
            for k in range(CONV_WIDTH):
                dcw_ref[k:k + 1, :] += dcw[k]
            dcb_ref[...] += dcb
            dwx_ref[...] += dwx
            dbx_ref[...] += dbx
            dwa_ref[...] += dwa
            dba_ref[...] += dba
            dlam_ref[...] += dlam

    tokens = batch * seq
    blk = pl.BlockSpec((seq, LRU_LANES), lambda c, b: (b, c))
    vec = pl.BlockSpec((1, LRU_LANES), lambda c, b: (0, c))
    mat = pl.BlockSpec((LRU_BLOCKS_PER_STEP, LANES, LANES), lambda c, b: (c, 0, 0))
    vec_shape = jax.ShapeDtypeStruct((1, D_MODEL), F32)
    mat_shape = jax.ShapeDtypeStruct((N_BLK, LANES, LANES), F32)
    return pl.pallas_call(
        body, name="branch_a_bwd",
        grid=(LRU_STEPS, batch),
        in_specs=[pl.BlockSpec((2, seq, LRU_LANES), lambda c, b: (2, b, c)), blk,
                  pl.BlockSpec((3, seq, LRU_LANES), lambda c, b: (0, b, c)), blk,
                  pl.BlockSpec(memory_space=pl.ANY),
                  pl.BlockSpec((CONV_WIDTH, LRU_LANES), lambda c, b: (0, c)), mat, mat, vec],
        out_specs=[pl.BlockSpec((2, seq, LRU_LANES), lambda c, b: (2, b, c)),
                   pl.BlockSpec((CONV_WIDTH, LRU_LANES), lambda c, b: (0, c)), vec, mat, vec, mat, vec, vec],
        out_shape=[jax.ShapeDtypeStruct((N_GROUPS, tokens, D_MODEL), _MXU_DTYPE),
                   jax.ShapeDtypeStruct((CONV_WIDTH, D_MODEL), F32), vec_shape, mat_shape, vec_shape, mat_shape,
                   vec_shape, vec_shape],
        scratch_shapes=[pltpu.VMEM((seq, LRU_LANES), F32)] * 2,
        input_output_aliases={4: 0},
        compiler_params=_params(("parallel", "arbitrary"), vmem=VMEM_LIMIT_BIG),
    )(z, hl, kept, dya, dz, conv_w, wx, wa, lam)


def _chunk_masks(transposed=False):
    r = lax.broadcasted_iota(jnp.int32, (CHUNK, CHUNK), 0)
    c = lax.broadcasted_iota(jnp.int32, (CHUNK, CHUNK), 1)
    return r <= c if transposed else r >= c


def _row_blocks(seq, fn):
    block = min(256, seq)

    def trip(i, carry):
        fn(pl.ds(pl.multiple_of(i * block, block), block))
        return carry

    lax.fori_loop(0, seq // block, trip, 0)


def _hgrn_prepare(z_ref, lb_ref, f_scr, logf_scr, qh_scr, seq):
    lb = _sigmoid(lb_ref[0:1, :] - lb_ref[1:2, :])

    def block(rows):
        q = z_ref[0, rows, :]
        f = lb + (1.0 - lb) * _sigmoid(z_ref[1, rows, :])
        f_scr[rows, :] = f
        logf_scr[rows, :] = jnp.log(f)
        qh_scr[rows, :] = q * _sigmoid(q)

    _row_blocks(seq, block)
    return lb


def _cumsum_rows(x, reverse=False):
    shift = _shift_up if reverse else _shift_down
    d = 1
    while d < x.shape[0]:
        x = x + shift(x, d)
        d *= 2
    return x


def _lane_mean(x):
    return jnp.mean(x, axis=-1, keepdims=True)


def _token_contractions(lhs_scr, rhs_scr, out_ref, seq):
    rows_id = lax.broadcasted_iota(jnp.int32, (LANES, LANES), 0)

    def transposed(p):
        rows = pl.ds(pl.multiple_of(p * LANES, LANES), LANES)
        return jnp.transpose(lhs_scr[rows, :]).astype(_MXU_DTYPE), rhs_scr[rows, :]

    def contract(p, s):
        lhs_t, rhs = s
        return (_mm(lhs_t, jnp.where(rows_id < CHUNK, rhs, 0.0)), _mm(lhs_t, jnp.where(rows_id >= CHUNK, rhs, 0.0)))

    def store(p, out):
        out_ref[2 * p] = out[0]
        out_ref[2 * p + 1] = out[1]

    _independent_trips(seq // LANES, [transposed, contract], store)


def _chunk_rows(c):
    return pl.ds(pl.multiple_of(c * CHUNK, CHUNK), CHUNK)


def _chunk_terms(c, z_ref, f_scr, qh_scr, b_scr):
    rows = _chunk_rows(c)
    b = b_scr[rows, :]
    b_mid = b_scr[pl.ds(c * CHUNK + CHUNK // 2, 1), :]
    b_last = b_scr[pl.ds(c * CHUNK + CHUNK - 1, 1), :]
    qh = qh_scr[rows, :]
    k = 1.0 - f_scr[rows, :]
    v = z_ref[2, rows, :]
    e_q = jnp.exp(b - b_mid) * HG_SCALE
    e_k = jnp.exp(b_mid - b)
    e_qi = jnp.exp(b) * HG_SCALE
    e_ks = jnp.exp(b_last - b)
    decay = jnp.exp(b_last)
    return rows, qh, k, v, e_q, e_k, e_qi, e_ks, decay


def _independent_trips(n, stages, store, group=CHUNKS_IN_FLIGHT):
    stages = stages if isinstance(stages, (list, tuple)) else [stages]
    group = min(group, n)

    def trip(g, carry):
        ids = [g * group + i for i in range(group)]
        state = [stages[0](c) for c in ids]
        for stage in stages[1:]:
            state = [stage(c, s) for c, s in zip(ids, state)]
        for c, s in zip(ids, state):
            store(c, s)
        return carry

    lax.fori_loop(0, n // group, trip, 0)


def _branch_b_fwd(z, lb_logits, hg_g, batch, seq):
    tokens = batch * seq
    n_chunks = seq // CHUNK

    def body(z_ref, lb_ref, g_ref, yb_ref, st_ref, kept_ref, logf_scr, o_scr, qi_scr, ks_scr, dec_scr):
        f_scr, qh_scr, b_scr, o_kept = (kept_ref.at[k] for k in range(4))
        _hgrn_prepare(z_ref, lb_ref, f_scr, logf_scr, qh_scr, seq)
        causal = _chunk_masks()
        gain = g_ref[...]

        def cumulate(c):
            return _cumsum_rows(logf_scr[_chunk_rows(c), :])

        def store_cumulated(c, b):
            b_scr[_chunk_rows(c), :] = b

        def scores(c):
            _, qh, k, v, e_q, e_k, e_qi, e_ks, decay = _chunk_terms(c, z_ref, f_scr, qh_scr, b_scr)
            return _mm_nt(qh * e_q, k * e_k), v, qh * e_qi, k * e_ks, decay

        def within_chunk(c, s):
            att, v, q_int, k_st, decay = s
            return _mm(jnp.where(causal, att, 0.0), v), q_int, k_st, decay

        def store_within_chunk(c, out):
            rows = _chunk_rows(c)
            o_scr[rows, :], qi_scr[rows, :], ks_scr[rows, :], dec_scr[pl.ds(c, 1), :] = out

        def carry_state(c, state_t):
            update = st_ref[c]
            st_ref[c] = state_t
            return state_t * dec_scr[pl.ds(c, 1), :] + update

        def finish(c):
            rows = _chunk_rows(c)
            o = o_scr[rows, :] + _mm_nt(qi_scr[rows, :], st_ref[c])
            r = lax.rsqrt(_lane_mean(o * o) + EPS)
            gb = z_ref[3, rows, :]
            return (((o * r) * gain) * (gb * _sigmoid(gb))).astype(_MXU_DTYPE), o

        def store_finished(c, out):
            yb_ref[_chunk_rows(c), :], o_kept[_chunk_rows(c), :] = out

        _independent_trips(n_chunks, cumulate, store_cumulated, FWD_CHUNKS_IN_FLIGHT)
        _independent_trips(n_chunks, [scores, within_chunk], store_within_chunk, FWD_CHUNKS_IN_FLIGHT)
        _token_contractions(z_ref.at[2], ks_scr, st_ref, seq)
        lax.fori_loop(0, n_chunks, carry_state, jnp.zeros((LANES, LANES), F32))
        _independent_trips(n_chunks, finish, store_finished, FWD_CHUNKS_IN_FLIGHT)

    seq_buf = pltpu.VMEM((seq, LANES), F32)
    return pl.pallas_call(
        body, name="branch_b_fwd",
        grid=(batch, N_BLK),
        in_specs=[pl.BlockSpec((4, seq, LANES), lambda b, h: (0, b, h)),
                  pl.BlockSpec((2, LANES), lambda b, h: (0, h)),
                  pl.BlockSpec((1, LANES), lambda b, h: (0, 0))],
        out_specs=[pl.BlockSpec((seq, LANES), lambda b, h: (b, h)),
                   pl.BlockSpec((None, n_chunks, LANES, LANES), lambda b, h: (b * N_BLK + h, 0, 0, 0)),
                   pl.BlockSpec((4, seq, LANES), lambda b, h: (0, b, h))],
        out_shape=[jax.ShapeDtypeStruct((tokens, D_MODEL), _MXU_DTYPE),
                   jax.ShapeDtypeStruct((batch * N_BLK, n_chunks, LANES, LANES), F32),
                   jax.ShapeDtypeStruct((4, tokens, D_MODEL), F32)],
        scratch_shapes=[seq_buf] * 4 + [pltpu.VMEM((n_chunks, LANES), F32)],
        compiler_params=_params(("parallel", "parallel")),
    )(z, lb_logits, hg_g)


def _branch_b_bwd(z, states, kept, dyb, dz, lb_logits, hg_g, batch, seq):
    n_chunks = seq // CHUNK

    def body(z_ref, st_ref, kept_ref, dyb_ref, dz_in_ref, lb_ref, g_ref, dz_ref, dlog_ref, dg_ref,
             do_scr, qi_scr, dqh_scr, df_scr, dec_scr, dgp_scr, dlb_scr, dst_scr):
        del dz_in_ref
        f_scr, qh_scr, b_scr, o_kept = (kept_ref.at[k] for k in range(4))
        first = (pl.program_id(0) == 0) & (pl.program_id(1) == 0)
        lb = _sigmoid(lb_ref[0:1, :] - lb_ref[1:2, :])
        causal = _chunk_masks()
        anti_causal = _chunk_masks(transposed=True)
        gain = g_ref[...]

        @pl.when(first)
        def _():
            dg_ref[...] = jnp.zeros_like(dg_ref)

        @pl.when(pl.program_id(1) == 0)
        def _():
            dlb_scr[...] = jnp.zeros_like(dlb_scr)

        def output_gradient(c):
            rows = _chunk_rows(c)
            b = b_scr[rows, :]
            q_int = qh_scr[rows, :] * (jnp.exp(b) * HG_SCALE)
            decay = jnp.exp(b_scr[pl.ds(c * CHUNK + CHUNK - 1, 1), :])
            o = o_kept[rows, :]
            r = lax.rsqrt(_lane_mean(o * o) + EPS)
            o_n = o * r
            gb = z_ref[3, rows, :]
            sgb = _sigmoid(gb)
            dyb_c = dyb_ref[rows, :]
            d_ong = dyb_c * (gb * sgb)
            d_gb = (dyb_c * (o_n * gain) * (sgb * (1.0 + gb * (1.0 - sgb)))).astype(_MXU_DTYPE)
            d_gain = jnp.sum(d_ong * o_n, axis=0, keepdims=True)
            d_on = d_ong * gain
            return d_gb, d_gain, r * (d_on - o_n * _lane_mean(d_on * o_n)), q_int, decay

        def store_output_gradient(c, out):
            rows = _chunk_rows(c)
            dz_ref[3, rows, :], dgp_scr[pl.ds(c, 1), :], do_scr[rows, :], qi_scr[rows, :], dec_scr[pl.ds(c, 1), :] = out

        def carry_state_gradient(cc, d_state_t):
            c = n_chunks - 1 - cc
            update = dst_scr[c]
            dst_scr[c] = d_state_t
            return d_state_t * dec_scr[pl.ds(c, 1), :] + update

        def score_gradients(c):
            rows, qh, k, v, e_q, e_k, e_qi, e_ks, decay = _chunk_terms(c, z_ref, f_scr, qh_scr, b_scr)
            state_t = st_ref[c]
            d_state_t = dst_scr[c]
            d_o = do_scr[rows, :]
            q_in, k_in, q_int, k_st = qh * e_q, k * e_k, qh * e_qi, k * e_ks
            first = (_mm_nt(k_in, q_in), _mm_nt(d_o, v), _mm_nt(v, d_o), _mm_nt(k_st, d_state_t), _mm(d_o, state_t),
                     _mm(v, d_state_t))
            d_decay = jnp.sum(state_t * d_state_t, axis=0, keepdims=True)
            return first, d_o, q_in, k_in, q_int, k_st, e_q, e_k, e_qi, e_ks, decay, d_decay

        def input_gradients(c, s):
            (att_t, d_att, d_att_t, dv_inter, dq_int, dk_st), d_o, q_in, k_in, q_int, k_st, e_q, e_k, e_qi, e_ks, decay, d_decay = s
            rows = _chunk_rows(c)
            d_v = _mm(jnp.where(anti_causal, att_t, 0.0), d_o) + dv_inter
            dq_in = _mm(jnp.where(causal, d_att, 0.0), k_in)
            dk_in = _mm(jnp.where(anti_causal, d_att_t, 0.0), q_in)
            d_k = dk_in * e_k + dk_st * e_ks
            kk = dk_st * k_st
            d_b = dq_in * q_in + dq_int * q_int - dk_in * k_in - kk
            d_b_last = jnp.sum(kk, axis=0, keepdims=True) + decay * d_decay
            d_logf = _cumsum_rows(d_b, reverse=True) + d_b_last
            return d_v.astype(_MXU_DTYPE), dq_in * e_q + dq_int * e_qi, d_logf / f_scr[rows, :] - d_k

        def store_input_gradients(c, out):
            rows = _chunk_rows(c)
            dz_ref[2, rows, :], dqh_scr[rows, :], df_scr[rows, :] = out

        def input_activations(rows):
            q = z_ref[0, rows, :]
            sq = _sigmoid(q)
            dz_ref[0, rows, :] = (dqh_scr[rows, :] * (sq * (1.0 + q * (1.0 - sq)))).astype(_MXU_DTYPE)
            sg = _sigmoid(z_ref[1, rows, :])
            d_f = df_scr[rows, :]
            dz_ref[1, rows, :] = (d_f * (1.0 - lb) * sg * (1.0 - sg)).astype(_MXU_DTYPE)
            dlb_scr[...] += jnp.sum(d_f * (1.0 - sg), axis=0, keepdims=True)

        _independent_trips(n_chunks, output_gradient, store_output_gradient)
        _token_contractions(do_scr, qi_scr, dst_scr, seq)
        lax.fori_loop(0, n_chunks, carry_state_gradient, jnp.zeros((LANES, LANES), F32))
        _independent_trips(n_chunks, [score_gradients, input_gradients], store_input_gradients)
        dg_ref[...] += jnp.sum(dgp_scr[...], axis=0, keepdims=True)
        _row_blocks(seq, input_activations)
        d_l0 = dlb_scr[...] * lb * (1.0 - lb)
        dlog_ref[0:1, :] = d_l0
        dlog_ref[1:2, :] = -d_l0

    tokens = batch * seq
    seq_buf = pltpu.VMEM((seq, LANES), F32)
    chunk_rows = pltpu.VMEM((n_chunks, LANES), F32)
    return pl.pallas_call(
        body, name="branch_b_bwd",
        grid=(N_BLK, batch),
        in_specs=[pl.BlockSpec((4, seq, LANES), lambda h, b: (0, b, h)),
                  pl.BlockSpec((None, n_chunks, LANES, LANES), lambda h, b: (b * N_BLK + h, 0, 0, 0)),
                  pl.BlockSpec((4, seq, LANES), lambda h, b: (0, b, h)),
                  pl.BlockSpec((seq, LANES), lambda h, b: (b, h)),
                  pl.BlockSpec(memory_space=pl.ANY),
                  pl.BlockSpec((2, LANES), lambda h, b: (0, h)),
                  pl.BlockSpec((1, LANES), lambda h, b: (0, 0))],
        out_specs=[pl.BlockSpec((4, seq, LANES), lambda h, b: (0, b, h)),
                   pl.BlockSpec((2, LANES), lambda h, b: (0, h)),
                   pl.BlockSpec((1, LANES), lambda h, b: (0, 0))],
        out_shape=[jax.ShapeDtypeStruct((N_GROUPS, tokens, D_MODEL), _MXU_DTYPE),
                   jax.ShapeDtypeStruct((2, D_MODEL), F32),
                   jax.ShapeDtypeStruct((1, LANES), F32)],
        scratch_shapes=[seq_buf] * 4 + [chunk_rows, chunk_rows, pltpu.VMEM((1, LANES), F32),
                                        pltpu.VMEM((n_chunks, LANES, LANES), F32)],
        input_output_aliases={4: 0},
        compiler_params=_params(("arbitrary", "arbitrary")),
    )(z, states, kept, dyb, dz, lb_logits, hg_g)


def _merge_tail(ya, yb, z, x2d, tgt2d, b_merge, final_g, pa, pb, wo):
    tokens, d = x2d.shape
    tm = min(TAIL_TOKENS, tokens)
    n_tiles = tokens // tm

    def body(ya_ref, yb_ref, z_ref, x_ref, t_ref, bm_ref, fg_ref, pa_hbm, pb_hbm, wo_hbm,
             dya_ref, dyb_ref, dx2_ref, dz_ref, loss_ref, dfg_ref, dbm_ref, dpa_hbm, dpb_hbm, dwo_hbm,
             pa_s, pb_s, wo_s, dpa_s, dpb_s, dwo_s, sems):
        i = pl.program_id(0)

        def together(pairs):
            copies = [pltpu.make_async_copy(src, dst, sems.at[k]) for k, (src, dst) in enumerate(pairs)]
            for cp in copies:
                cp.start()
            for cp in copies:
                cp.wait()

        @pl.when(i == 0)
        def _():
            together([(pa_hbm, pa_s), (pb_hbm, pb_s), (wo_hbm, wo_s)])
            dpa_s[...] = jnp.zeros_like(dpa_s)
            dpb_s[...] = jnp.zeros_like(dpb_s)
            dwo_s[...] = jnp.zeros_like(dwo_s)
            loss_ref[...] = jnp.zeros_like(loss_ref)
            dfg_ref[...] = jnp.zeros_like(dfg_ref)
            dbm_ref[...] = jnp.zeros_like(dbm_ref)

        ya_t = ya_ref[...]
        yb_t = yb_ref[...]
        out_a = _mm(ya_t, pa_s[...])
        out_b = _mm(yb_t, pb_s[...])
        g_a = _sigmoid(z_ref[0] + bm_ref[:, :d])
        g_b = _sigmoid(z_ref[1] + bm_ref[:, d:])
        mixed = g_a * out_a + g_b * out_b
        x2 = x_ref[...] + _mm(mixed, wo_s[...])
        r = lax.rsqrt(jnp.mean(x2 * x2, axis=-1, keepdims=True) + EPS)
        xn = x2 * r
        fg = fg_ref[...]
        diff = xn * fg - t_ref[...]
        loss_ref[...] += jnp.sum(diff * diff) * (0.5 / d)
        dy = diff * (1.0 / d)
        dfg_ref[...] += jnp.sum(dy * xn, axis=0, keepdims=True)
        dxn = dy * fg
        dx2 = r * (dxn - xn * jnp.mean(dxn * xn, axis=-1, keepdims=True))
        dx2_ref[...] = dx2
        dmixed = _mm_nt(dx2, wo_s[...])
        dwo_s[...] += _mm_tn(mixed, dx2)
        dgm_a = dmixed * out_a * g_a * (1.0 - g_a)
        dgm_b = dmixed * out_b * g_b * (1.0 - g_b)
        dz_ref[0] = dgm_a.astype(_MXU_DTYPE)
        dz_ref[1] = dgm_b.astype(_MXU_DTYPE)
        dbm_ref[:, :d] += jnp.sum(dgm_a, axis=0, keepdims=True)
        dbm_ref[:, d:] += jnp.sum(dgm_b, axis=0, keepdims=True)
        dout_a = dmixed * g_a
        dout_b = dmixed * g_b
        dpa_s[...] += _mm_tn(ya_t, dout_a)
        dpb_s[...] += _mm_tn(yb_t, dout_b)
        dya_ref[...] = _mm_nt(dout_a, pa_s[...])
        dyb_ref[...] = _mm_nt(dout_b, pb_s[...])

        @pl.when(i == n_tiles - 1)
        def _():
            together([(dpa_s, dpa_hbm), (dpb_s, dpb_hbm), (dwo_s, dwo_hbm)])

    tile = pl.BlockSpec((tm, d), lambda i: (i, 0))
    gm = pl.BlockSpec((2, tm, d), lambda i: (3, i, 0))
    row = lambda n: pl.BlockSpec((1, n), lambda i: (0, 0))
    hbm = pl.BlockSpec(memory_space=pl.ANY)
    act = jax.ShapeDtypeStruct((tokens, d), F32)
    mat = jax.ShapeDtypeStruct((d, d), F32)
    return pl.pallas_call(
        body, name="merge_tail",
        grid=(n_tiles,),
        in_specs=[tile, tile, gm, tile, tile, row(2 * d), row(d), hbm, hbm, hbm],
        out_specs=[tile, tile, tile, gm, row(LANES), row(d), row(2 * d), hbm, hbm, hbm],
        out_shape=[act, act, act, jax.ShapeDtypeStruct((N_GROUPS, tokens, d), _MXU_DTYPE),
                   jax.ShapeDtypeStruct((1, LANES), F32), jax.ShapeDtypeStruct((1, d), F32),
                   jax.ShapeDtypeStruct((1, 2 * d), F32), mat, mat, mat],
        scratch_shapes=[pltpu.VMEM((d, d), _MXU_DTYPE)] * 3 + [pltpu.VMEM((d, d), F32)] * 3
        + [pltpu.SemaphoreType.DMA((3,))],
        compiler_params=_params(("arbitrary",)),
    )(ya, yb, z, x2d, tgt2d, b_merge, final_g, pa, pb, wo)


def _inproj_dw_exchange(h_t, dz, scatter):
    d, tokens = h_t.shape
    tm = min(CONTRACT_TOKENS, tokens)
    n_i = tokens // tm
    half = d // 2

    def body(h_ref, dz_ref, *rest):
        n_in, n_out = scatter.n_in, scatter.n_out
        dw_hbm, land_hbm = rest[n_in:n_in + 2]
        acc, local_sems, send_sems, recv_sems = rest[n_in + 2 + n_out:n_in + 6 + n_out]
        carried = scatter.copies(rest[:n_in], rest[n_in + 2:n_in + 2 + n_out], rest[n_in + 6 + n_out:])
        s, i = pl.program_id(0), pl.program_id(1)
        x, y, c, _ = _mesh_position()

        @pl.when((s == 0) & (i == 0))
        def _():
            for cp in carried:
                cp.start()

        part = _mm(h_ref[...], dz_ref[...])
        buf = acc.at[s % 2]

        @pl.when(i == 0)
        def _():
            buf[...] = part

        @pl.when(i != 0)
        def _():
            buf[...] += part

        def copies(k):
            g = _SLOT_TO_GROUP[k]
            cols = pl.ds((g % 2) * D_MODEL, D_MODEL)
            src = acc.at[k % 2]
            mine = pltpu.make_async_copy(src, dw_hbm.at[g // 2, :, cols], local_sems.at[k % 2])
            theirs = _remote(src.at[pl.ds((1 - c) * half, half), :], land_hbm.at[g // 2, :, cols],
                             send_sems, recv_sems, k, (x, y, 1 - c))
            return mine, theirs

        for k in range(N_GROUPS):
            @pl.when((s == k) & (i == n_i - 1))
            def _(k=k):
                if k > 0:
                    mine, theirs = copies(k - 1)
                    mine.wait()
                    theirs.wait_send()
                mine, theirs = copies(k)
                mine.start()
                theirs.start()
                if k == N_GROUPS - 1:
                    mine.wait()
                    theirs.wait_send()
                    for kk in range(N_GROUPS):
                        copies(kk)[1].wait_recv()
                    for cp in carried:
                        cp.wait()

    hbm = pl.BlockSpec(memory_space=pl.ANY)
    more = scatter.plumbing(first_operand=2, first_output=2)
    return pl.pallas_call(
        body, name="inproj_dw_exchange",
        grid=(N_GROUPS, n_i),
        in_specs=[pl.BlockSpec((d, tm), lambda s, i: (0, i)),
                  pl.BlockSpec((None, tm, D_MODEL), lambda s, i: (s, i, 0))] + more[1],
        out_specs=[hbm, hbm] + more[2],
        out_shape=[jax.ShapeDtypeStruct((N_SHARDS, d, 2 * D_MODEL), F32),
                   jax.ShapeDtypeStruct((N_SHARDS, half, 2 * D_MODEL), F32)] + more[3],
        scratch_shapes=[pltpu.VMEM((2, d, D_MODEL), F32), pltpu.SemaphoreType.DMA((2,)),
                        pltpu.SemaphoreType.DMA((N_GROUPS,)), pltpu.SemaphoreType.DMA((N_GROUPS,))] + more[4],
        input_output_aliases=more[5],
        compiler_params=_params(("arbitrary", "arbitrary")),
    )(h_t, dz, *more[0])


def _inproj_dx(dz, w_all, x2d, dx2, norm_g, scatter):
    tokens, d = x2d.shape
    tm = min(TAIL_TOKENS, tokens)
    n_tiles = tokens // tm

    def body(dz_ref, w_hbm, x_ref, dx2_ref, g_ref, *rest):
        n_in, n_out = scatter.n_in, scatter.n_out
        dx_ref, dg_ref = rest[n_in:n_in + 2]
        w_res, load_sems = rest[n_in + 2 + n_out:n_in + 4 + n_out]
        copies = scatter.copies(rest[:n_in], rest[n_in + 2:n_in + 2 + n_out], rest[n_in + 4 + n_out:])
        i = pl.program_id(0)

        @pl.when(i == 0)
        def _():
            for cp in copies:
                cp.start()
            loads = [pltpu.make_async_copy(w_hbm.at[g // 2, :, pl.ds((g % 2) * D_MODEL, D_MODEL)],
                                           w_res.at[:, pl.ds(slot * D_MODEL, D_MODEL)], load_sems.at[slot])
                     for slot, g in enumerate(_SLOT_TO_GROUP)]
            for cp in loads:
                cp.start()
            for cp in loads:
                cp.wait()
            dg_ref[...] = jnp.zeros_like(dg_ref)

        dz_all = jnp.concatenate([dz_ref[s] for s in range(N_GROUPS)], axis=1)
        dh = jnp.transpose(_mm_nt(w_res[...], dz_all))
        x = x_ref[...]
        r = lax.rsqrt(jnp.mean(x * x, axis=-1, keepdims=True) + EPS)
        xn = x * r
        dg_ref[...] += jnp.sum(dh * xn, axis=0, keepdims=True)
        dxn = dh * g_ref[...]
        dx_ref[...] = r * (dxn - xn * jnp.mean(dxn * xn, axis=-1, keepdims=True)) + dx2_ref[...]

        @pl.when(i == n_tiles - 1)
        def _():
            for cp in copies:
                cp.wait()

    tile = pl.BlockSpec((tm, d), lambda i: (i, 0))
    hbm = pl.BlockSpec(memory_space=pl.ANY)
    more = scatter.plumbing(first_operand=5, first_output=2)
    return pl.pallas_call(
        body, name="inproj_dx", grid=(n_tiles,),
        in_specs=[pl.BlockSpec((N_GROUPS, tm, D_MODEL), lambda i: (0, i, 0)), hbm, tile, tile,
                  pl.BlockSpec((1, d), lambda i: (0, 0))] + more[1],
        out_specs=[tile, pl.BlockSpec((1, d), lambda i: (0, 0))] + more[2],
        out_shape=[jax.ShapeDtypeStruct((tokens, d), F32), jax.ShapeDtypeStruct((1, d), F32)] + more[3],
        scratch_shapes=[pltpu.VMEM((d, N_GROUPS * D_MODEL), _MXU_DTYPE), pltpu.SemaphoreType.DMA((N_GROUPS,))] + more[4],
        input_output_aliases=more[5],
        compiler_params=_params(("arbitrary",)),
    )(dz, w_all, x2d, dx2, norm_g, *more[0])


def _row_tile(rows, cols, itemsize=4, budget=2 * 1024 * 1024):
    tr = rows
    while tr * cols * itemsize > budget and tr % 16 == 0:
        tr //= 2
    return tr


def _cast_into_slot(a, chip, dtype, name):
    rows, cols = a.shape
    tr = _row_tile(rows, cols)

    def body(chip_ref, a_ref, o_ref):
        del chip_ref
        o_ref[...] = a_ref[...].astype(dtype)

    grid_spec = pltpu.PrefetchScalarGridSpec(
        num_scalar_prefetch=1, grid=(rows // tr,),
        in_specs=[pl.BlockSpec((tr, cols), lambda i, chip_ref: (i, 0))],
        out_specs=pl.BlockSpec((None, tr, cols), lambda i, chip_ref: (chip_ref[0], i, 0)))
    return pl.pallas_call(body, name=name, grid_spec=grid_spec,
                          out_shape=jax.ShapeDtypeStruct((N_SHARDS, rows, cols), dtype),
                          compiler_params=_params(("arbitrary",)))(chip, a)


def _sum_slots(stack, name):
    n, rows, cols = stack.shape
    tr = _row_tile(rows, cols * n)

    def body(s_ref, o_ref):
        total = s_ref[0].astype(F32)
        for k in range(1, n):
            total = total + s_ref[k].astype(F32)
        o_ref[...] = total

    return pl.pallas_call(body, name=name, grid=(rows // tr,),
                          in_specs=[pl.BlockSpec((n, tr, cols), lambda i: (0, i, 0))],
                          out_specs=pl.BlockSpec((tr, cols), lambda i: (i, 0)),
                          out_shape=jax.ShapeDtypeStruct((rows, cols), F32),
                          compiler_params=_params(("parallel",)))(stack)


def _add_half(full, landed, place, name):
    n, rows, cols = full.shape
    half = rows // 2
    tr = _row_tile(half, cols)
    nb = half // tr

    def body(place_ref, a_ref, b_ref, o_ref, own_ref):
        total = (a_ref[...] + b_ref[...]).astype(_MXU_DTYPE)
        o_ref[...] = total

        @pl.when(pl.program_id(1) == place_ref[1])
        def _():
            own_ref[...] = total

    grid_spec = pltpu.PrefetchScalarGridSpec(
        num_scalar_prefetch=1, grid=(nb, n),
        in_specs=[pl.BlockSpec((None, tr, cols), lambda i, j, place_ref: (j, place_ref[0] * nb + i, 0)),
                  pl.BlockSpec((None, tr, cols), lambda i, j, place_ref: (j, i, 0))],
        out_specs=[pl.BlockSpec((None, tr, cols), lambda i, j, place_ref: (j, i, 0)),
                   pl.BlockSpec((None, tr, cols), lambda i, j, place_ref: (place_ref[1], i, 0))])
    shape = jax.ShapeDtypeStruct((n, half, cols), _MXU_DTYPE)
    return pl.pallas_call(body, name=name, grid_spec=grid_spec, out_shape=[shape, shape],
                          compiler_params=_params(("parallel", "arbitrary")))(place, full, landed)


def _adamw_update(w, grad, m, v):
    c1 = 1.0 - ADAM_B1 ** ADAM_STEP
    c2 = 1.0 - ADAM_B2 ** ADAM_STEP
    nm = ADAM_B1 * m + (1.0 - ADAM_B1) * grad
    nv = ADAM_B2 * v + (1.0 - ADAM_B2) * (grad * grad)
    return (-ADAM_LR) * ((nm / c1) / (jnp.sqrt(nv / c2) + ADAM_EPS) + ADAM_WD * w), nm, nv


def _adamw(w, g, m, v, name):
    rows, cols = w.shape
    tr = _row_tile(rows, cols, budget=1024 * 1024)

    def body(w_ref, g_ref, m_ref, v_ref, d_ref, nm_ref, nv_ref):
        d_ref[...], nm_ref[...], nv_ref[...] = _adamw_update(w_ref[...], g_ref[...], m_ref[...], v_ref[...])

    spec = pl.BlockSpec((tr, cols), lambda i: (i, 0))
    shape = jax.ShapeDtypeStruct((rows, cols), F32)
    return pl.pallas_call(body, name=name, grid=(rows // tr,), in_specs=[spec] * 4, out_specs=[spec] * 3,
                          out_shape=[shape] * 3, compiler_params=_params(("parallel",)))(w, g, m, v)


def _adamw_halves(w, g_mine, g_sibling, m, v, core, name):
    rows, cols = w.shape
    half = rows // 2
    tr = _row_tile(half, cols, budget=1024 * 1024)
    nb = half // tr

    def body(core_ref, w_ref, gm_ref, gs_ref, m_ref, v_ref, g_ref, d_ref, nm_ref, nv_ref):
        mine = pl.program_id(0) // nb == core_ref[0]
        grad = jnp.where(mine, gm_ref[...], gs_ref[...])
        g_ref[...] = grad
        d_ref[...], nm_ref[...], nv_ref[...] = _adamw_update(w_ref[...], grad, m_ref[...], v_ref[...])

    spec = pl.BlockSpec((tr, cols), lambda i, core_ref: (i, 0))
    mine_spec = pl.BlockSpec((tr, cols), lambda i, core_ref: (jnp.where(i // nb == core_ref[0], i % nb, 0), 0))
    sibling_spec = pl.BlockSpec((tr, cols), lambda i, core_ref: (jnp.where(i // nb == core_ref[0], 0, i % nb), 0))
    grid_spec = pltpu.PrefetchScalarGridSpec(num_scalar_prefetch=1, grid=(rows // tr,),
                                             in_specs=[spec, mine_spec, sibling_spec, spec, spec], out_specs=[spec] * 4)
    shape = jax.ShapeDtypeStruct((rows, cols), F32)
    return pl.pallas_call(body, name=name, grid_spec=grid_spec, out_shape=[shape] * 4,
                          compiler_params=_params(("parallel",)))(core, w, g_mine, g_sibling, m, v)


def _local_step(x, loss_target, gather, reduction, b_merge, conv_b, rg_wx, rg_bx, rg_wa, rg_ba, rg_lambda,
                hg_lb_logits, hg_norm_g, norm_g, final_norm_g):
    batch, seq, d = x.shape
    x2d = x.reshape(batch * seq, d)
    tgt2d = loss_target.reshape(batch * seq, d)
    z, h_t, (w_all, pa, pb, wo), cw_all = _inproj_fwd_gather(x2d, norm_g, *gather)
    pa, pb, wo = (t.reshape(d, d) for t in (pa, pb, wo))
    conv_w = jnp.transpose(cw_all, (1, 0, 2)).reshape(CONV_WIDTH, d)
    lru = (conv_w, conv_b, rg_wx, rg_bx, rg_wa, rg_ba, rg_lambda)
    ya, hl, kept = _branch_a_fwd(z, *lru, batch, seq)
    yb, states, kept_b = _branch_b_fwd(z, hg_lb_logits, hg_norm_g, batch, seq)
    dya, dyb, dx2, dz, loss, d_final_g, d_b_merge, d_pa, d_pb, d_wo = _merge_tail(
        ya, yb, z, x2d, tgt2d, b_merge, final_norm_g, pa, pb, wo)
    dz, d_lb_logits, d_hg_g = _branch_b_bwd(z, states, kept_b, dyb, dz, hg_lb_logits, hg_norm_g, batch, seq)
    dz, d_conv_w, d_conv_b, d_wx, d_bx, d_wa, d_ba, d_lam = _branch_a_bwd(
        z, hl, kept, dya, dz, conv_w, rg_wx, rg_wa, rg_lambda, batch, seq)
    small = dict(b_merge=d_b_merge, conv_w=d_conv_w, conv_b=d_conv_b, rg_wx=d_wx, rg_bx=d_bx, rg_wa=d_wa,
                 rg_ba=d_ba, rg_lambda=d_lam, hg_lb_logits=d_lb_logits, hg_norm_g=d_hg_g,
                 norm_g=jnp.zeros((1, d), F32), final_norm_g=d_final_g)
    first, second = reduction
    d_w_in, landed_w_in, *scattered_first = _inproj_dw_exchange(h_t, dz, first((d_pa, d_pb, d_wo), small))
    grad_x, d_norm_g, *scattered_second = _inproj_dx(dz, w_all, x2d, dx2, norm_g, scatter=second(d_w_in, landed_w_in))
    return loss[0, 0], grad_x.reshape(batch, seq, d), d_norm_g, (scattered_first, scattered_second)


_SMALL_ORDER = ("b_merge", "conv_w", "conv_b", "rg_wx", "rg_bx", "rg_wa", "rg_ba", "rg_lambda", "hg_lb_logits",
                "hg_norm_g", "norm_g", "final_norm_g")
N_DEV = 8
PIECE_ROWS = 272


def _pack_small(tree):
    flat = jnp.concatenate([tree[k].reshape(-1) for k in _SMALL_ORDER])
    flat = jnp.pad(flat, (0, N_DEV * PIECE_ROWS * LANES - flat.shape[0]))
    return flat.reshape(N_DEV * PIECE_ROWS, LANES)


def _unpack_small(packed, like):
    flat = packed.reshape(-1)
    out, pos = {}, 0
    for k in _SMALL_ORDER:
        n = like[k].size
        out[k] = flat[pos:pos + n].reshape(like[k].shape)
        pos += n
    return out


def _mesh_position():
    x, y, c = lax.axis_index("x"), lax.axis_index("y"), lax.axis_index("c")
    other_chips = [(1 - x, y), (x, 1 - y), (1 - x, 1 - y)]
    return x, y, c, other_chips


def _other_devices(x, y, c):
    flips = [(fx, fy, fc) for fx in (0, 1) for fy in (0, 1) for fc in (0, 1) if (fx, fy, fc) != (0, 0, 0)]
    return [(jnp.where(fx, 1 - x, x), jnp.where(fy, 1 - y, y), jnp.where(fc, 1 - c, c)) for fx, fy, fc in flips]


def _remote(src, dst, send_sems, recv_sems, k, device):
    return pltpu.make_async_remote_copy(src_ref=src, dst_ref=dst, send_sem=send_sems.at[k], recv_sem=recv_sems.at[k],
                                        device_id=device, device_id_type=MESH)


def _exchange_halves(bigs, small):
    n_big = len(bigs)
    n_sem = n_big + N_DEV - 1

    def body(*refs):
        srcs, small_src = refs[:n_big], refs[n_big]
        outs, small_out = refs[n_big + 1:2 * n_big + 1], refs[2 * n_big + 1]
        send_sems, recv_sems, local_sem = refs[2 * n_big + 2:]
        x, y, c, _ = _mesh_position()
        me, sibling = 4 * x + 2 * y + c, (x, y, 1 - c)
        mine = pltpu.make_async_copy(small_src.at[pl.ds(me * PIECE_ROWS, PIECE_ROWS), :], small_out.at[me], local_sem)
        mine.start()
        copies = []
        for a in range(n_big):
            hs = srcs[a].shape[1] // 2
            copies.append(_remote(srcs[a].at[:, pl.ds((1 - c) * hs, hs), :], outs[a], send_sems, recv_sems, a, sibling))
        for k, (px, py, pc) in enumerate(_other_devices(x, y, c)):
            piece = small_src.at[pl.ds((4 * px + 2 * py + pc) * PIECE_ROWS, PIECE_ROWS), :]
            copies.append(_remote(piece, small_out.at[me], send_sems, recv_sems, n_big + k, (px, py, pc)))
        for cp in copies:
            cp.start()
        for cp in copies:
            cp.wait()
        mine.wait()

    hbm = pl.BlockSpec(memory_space=pl.ANY)
    out_shape = [jax.ShapeDtypeStruct((g.shape[0], g.shape[1] // 2, g.shape[2]), F32) for g in bigs]
    out_shape.append(jax.ShapeDtypeStruct((N_DEV, PIECE_ROWS, LANES), F32))
    return pl.pallas_call(
        body, name="exchange_halves",
        in_specs=[hbm] * (n_big + 1), out_specs=[hbm] * (n_big + 1), out_shape=out_shape,
        scratch_shapes=[pltpu.SemaphoreType.DMA((n_sem,)), pltpu.SemaphoreType.DMA((n_sem,)), pltpu.SemaphoreType.DMA],
    )(*bigs, small)


class _Scatter:
    def __init__(self, bigs, by_chip, small=None):
        self.bigs, self.by_chip, self.small = list(bigs), list(by_chip), small
        self.n_big = len(self.bigs)
        self.n_in = 2 * self.n_big + (small is not None)
        self.n_out = self.n_big + (small is not None)
        self.n_scratch = 2 + (small is not None)

    def plumbing(self, first_operand, first_output):
        hbm = pl.BlockSpec(memory_space=pl.ANY)
        n_sem = 3 * self.n_big + (N_DEV - 1 if self.small is not None else 0)
        operands = self.bigs + self.by_chip + ([self.small] if self.small is not None else [])
        out_shapes = [jax.ShapeDtypeStruct(g.shape, g.dtype) for g in self.by_chip]
        scratch = [pltpu.SemaphoreType.DMA((n_sem,)), pltpu.SemaphoreType.DMA((n_sem,))]
        if self.small is not None:
            out_shapes.append(jax.ShapeDtypeStruct((N_DEV, PIECE_ROWS, LANES), F32))
            scratch.append(pltpu.SemaphoreType.DMA)
        aliases = {first_operand + self.n_big + a: first_output + a for a in range(self.n_big)}
        return operands, [hbm] * self.n_in, [hbm] * self.n_out, out_shapes, scratch, aliases

    def copies(self, in_refs, out_refs, scratch_refs):
        srcs, outs = in_refs[:self.n_big], out_refs[:self.n_big]
        send_sems, recv_sems = scratch_refs[:2]
        x, y, c, chips = _mesh_position()
        chip, me = 2 * x + y, 4 * x + 2 * y + c
        copies = []
        for a in range(self.n_big):
            for j, (cx, cy) in enumerate(chips):
                copies.append(_remote(srcs[a].at[2 * cx + cy], outs[a].at[chip], send_sems, recv_sems, 3 * a + j,
                                      (cx, cy, c)))
        if self.small is not None:
            small_src, small_out = in_refs[2 * self.n_big], out_refs[self.n_big]
            copies.append(pltpu.make_async_copy(small_src, small_out.at[me], scratch_refs[2]))
            for k, peer in enumerate(_other_devices(x, y, c)):
                copies.append(_remote(small_src, small_out.at[me], send_sems, recv_sems, 3 * self.n_big + k, peer))
        return copies


def _swap_halves(halves, vec):
    n_big = len(halves)

    def body(*refs):
        srcs, vec_src = refs[:n_big], refs[n_big]
        outs, vec_out = refs[n_big + 1:2 * n_big + 1], refs[2 * n_big + 1]
        send_sems, recv_sems, local_sem = refs[2 * n_big + 2:]
        x, y, c, _ = _mesh_position()
        me = 4 * x + 2 * y + c
        copies = [pltpu.make_async_copy(vec_src, vec_out.at[me], local_sem)]
        copies += [_remote(srcs[a], outs[a], send_sems, recv_sems, a, (x, y, 1 - c)) for a in range(n_big)]
        copies += [_remote(vec_src, vec_out.at[me], send_sems, recv_sems, n_big + k, peer)
                   for k, peer in enumerate(_other_devices(x, y, c))]
        for cp in copies:
            cp.start()
        for cp in copies:
            cp.wait()

    hbm = pl.BlockSpec(memory_space=pl.ANY)
    n_sem = n_big + N_DEV - 1
    return pl.pallas_call(
        body, name="swap_halves",
        in_specs=[hbm] * (n_big + 1), out_specs=[hbm] * (n_big + 1),
        out_shape=[jax.ShapeDtypeStruct(h.shape, F32) for h in halves] + [jax.ShapeDtypeStruct((N_DEV,) + vec.shape, F32)],
        scratch_shapes=[pltpu.SemaphoreType.DMA((n_sem,)), pltpu.SemaphoreType.DMA((n_sem,)), pltpu.SemaphoreType.DMA],
    )(*halves, vec)


def kernel(x, w_in, b_merge, conv_w, conv_b, rg_wx, rg_bx, rg_wa, rg_ba, rg_lambda, hg_lb_logits, hg_norm_g, proj_a, proj_b, w_out, norm_g, final_norm_g, loss_target, m_w_in, m_b_merge, m_conv_w, m_conv_b, m_rg_wx, m_rg_bx, m_rg_wa, m_rg_ba, m_rg_lambda, m_hg_lb_logits, m_hg_norm_g, m_proj_a, m_proj_b, m_w_out, m_norm_g, m_final_norm_g, v_w_in, v_b_merge, v_conv_w, v_conv_b, v_rg_wx, v_rg_bx, v_rg_wa, v_rg_ba, v_rg_lambda, v_hg_lb_logits, v_hg_norm_g, v_proj_a, v_proj_b, v_w_out, v_norm_g, v_final_norm_g):
    d = D_MODEL
    weights = dict(w_in=w_in, b_merge=b_merge, conv_w=conv_w, conv_b=conv_b, rg_wx=rg_wx, rg_bx=rg_bx, rg_wa=rg_wa,
                   rg_ba=rg_ba, rg_lambda=rg_lambda, hg_lb_logits=hg_lb_logits, hg_norm_g=hg_norm_g, proj_a=proj_a,
                   proj_b=proj_b, w_out=w_out, norm_g=norm_g, final_norm_g=final_norm_g)
    m = dict(w_in=m_w_in, b_merge=m_b_merge, conv_w=m_conv_w, conv_b=m_conv_b, rg_wx=m_rg_wx, rg_bx=m_rg_bx,
             rg_wa=m_rg_wa, rg_ba=m_rg_ba, rg_lambda=m_rg_lambda, hg_lb_logits=m_hg_lb_logits, hg_norm_g=m_hg_norm_g,
             proj_a=m_proj_a, proj_b=m_proj_b, w_out=m_w_out, norm_g=m_norm_g, final_norm_g=m_final_norm_g)
    v = dict(w_in=v_w_in, b_merge=v_b_merge, conv_w=v_conv_w, conv_b=v_conv_b, rg_wx=v_rg_wx, rg_bx=v_rg_bx,
             rg_wa=v_rg_wa, rg_ba=v_rg_ba, rg_lambda=v_rg_lambda, hg_lb_logits=v_hg_lb_logits, hg_norm_g=v_hg_norm_g,
             proj_a=v_proj_a, proj_b=v_proj_b, w_out=v_w_out, norm_g=v_norm_g, final_norm_g=v_final_norm_g)
    big_names = ("w_in", "proj_a", "proj_b", "w_out")

    core = lax.axis_index("c").astype(jnp.int32).reshape(1)
    chip = (2 * lax.axis_index("x") + lax.axis_index("y")).astype(jnp.int32)

    slotted = [_cast_into_slot(weights[k][0], chip.reshape(1), _MXU_DTYPE, f"cast_{k}") for k in big_names]
    conv_slotted = _cast_into_slot(conv_w[0], chip.reshape(1), F32, "slot_conv_w")

    small_shapes = {}

    place = jnp.concatenate([core, chip.reshape(1)])

    def reduce_proj_and_small(proj_grads, small_grads):
        small_shapes.update({k: t.shape for k, t in small_grads.items()})
        bigs = [g.reshape(N_SHARDS, d // N_SHARDS, d) for g in proj_grads]
        *landed, small_landed = _exchange_halves(bigs, _pack_small(small_grads))
        sums = [_add_half(g, l, place, f"add_half_{1 + a}") for a, (g, l) in enumerate(zip(bigs, landed))]
        return _Scatter([s[0] for s in sums], [s[1] for s in sums], _sum_slots(small_landed, "sum_small"))

    def reduce_w_in(d_w_in, landed):
        partial, own_slot = _add_half(d_w_in, landed, place, "add_half_0")
        return _Scatter([partial], [own_slot])

    loss_part, grad_x, d_norm_g, ((*by_chip_proj, small_all), by_chip_w_in) = _local_step(
        x, loss_target, (slotted, conv_slotted, chip.reshape(1)), (reduce_proj_and_small, reduce_w_in),
        b_merge, conv_b, rg_wx[0], rg_bx.reshape(1, d), rg_wa[0], rg_ba.reshape(1, d), rg_lambda, hg_lb_logits,
        hg_norm_g, norm_g, final_norm_g.reshape(1, d))
    mine = [_sum_slots(s, f"sum_chips_{a}") for a, s in enumerate(by_chip_w_in + by_chip_proj)]
    late = jnp.concatenate([d_norm_g.reshape(SUBLANES, LANES), jnp.full((SUBLANES, LANES), loss_part, F32)])
    *theirs, late_parts = _swap_halves(mine, late)
    late_sum = _sum_slots(late_parts, "sum_late")
    loss = late_sum[SUBLANES, 0]
    small_red = _unpack_small(small_all, {k: jax.ShapeDtypeStruct(s, F32) for k, s in small_shapes.items()})
    small_red["norm_g"] = late_sum[:SUBLANES].reshape(1, d)

    grads, delta, new_m, new_v = {}, {}, {}, {}
    for k, g_mine, g_theirs in zip(big_names, mine, theirs):
        out = _adamw_halves(weights[k][0], g_mine, g_theirs, m[k][0], v[k][0], core, f"adamw_{k}")
        grads[k], delta[k], new_m[k], new_v[k] = (t.reshape(weights[k].shape) for t in out)
    cols = d // N_SHARDS
    g_conv = lax.dynamic_slice(small_red["conv_w"], (0, chip * cols), (CONV_WIDTH, cols))
    grads["conv_w"] = g_conv.reshape(conv_w.shape)
    dl, nm, nv = _adamw(conv_w[0], g_conv, m_conv_w[0], v_conv_w[0], "adamw_conv_w")
    delta["conv_w"], new_m["conv_w"], new_v["conv_w"] = (t.reshape(conv_w.shape) for t in (dl, nm, nv))
    rest = [k for k in _SMALL_ORDER if k != "conv_w"]
    like = {k: (weights[k] if k != "conv_w" else jnp.zeros((CONV_WIDTH, d), F32)) for k in _SMALL_ORDER}
    packs = [_pack_small({k: (t[k] if k != "conv_w" else like[k]) for k in _SMALL_ORDER}) for t in (weights, m, v)]
    g_pack = _pack_small({k: small_red[k].reshape(like[k].shape) for k in _SMALL_ORDER})
    outs = [_unpack_small(p, like) for p in _adamw(packs[0], g_pack, packs[1], packs[2], "adamw_small")]
    for k in rest:
        grads[k] = small_red[k].reshape(weights[k].shape)
        delta[k], new_m[k], new_v[k] = outs[0][k], outs[1][k], outs[2][k]

    order = ("w_in", "b_merge", "conv_w", "conv_b", "rg_wx", "rg_bx", "rg_wa", "rg_ba", "rg_lambda", "hg_lb_logits",
             "hg_norm_g", "proj_a", "proj_b", "w_out", "norm_g", "final_norm_g")
    return (loss, grad_x, *[grads[k] for k in order], *[delta[k] for k in order], *[new_m[k] for k in order],
            *[new_v[k] for k in order])
```
